```python
import math
import jax, jax.numpy as jnp
from jax import lax
import numpy as np

D_MODEL = 1024
BATCH = 8
SEQ = 2048
DEPTH = 2

N_MIXERS = 2
BRANCH_WIDTH = 2 * D_MODEL
XQ_WIDTH = BRANCH_WIDTH // 4
PRIMARY_WIDTH = BRANCH_WIDTH - XQ_WIDTH
MEM_LEN = 256
X_HEADS = 4
X_HEAD_DIM = XQ_WIDTH // X_HEADS
S5_GROUP_CH = 16
S5_GROUPS = PRIMARY_WIDTH // S5_GROUP_CH
S5_STATE = 64
S5_STEP_MIN = 1e-3
S5_STEP_MAX = 1e-1
MLA_NOPE = 128
MLA_ROPE = 64
MLA_V = 128
MLA_HEADS = PRIMARY_WIDTH // MLA_V
MLA_Q_LORA = D_MODEL // 2
MLA_KV_LORA = D_MODEL // 4
ROPE_THETA = 10000.0
Q_BLOCK = 128
EPS = 1e-6
N_S5 = (DEPTH + 1) // 2
N_MLA = DEPTH // 2
S5_IN_WIDTH = PRIMARY_WIDTH + XQ_WIDTH + BRANCH_WIDTH
MLA_IN_WIDTH = MLA_Q_LORA + MLA_KV_LORA + MLA_ROPE + XQ_WIDTH + BRANCH_WIDTH

kernel_name = "hybrid_s5_mla_memory_block"


def rms_norm(x, g):
    xf = x.astype(jnp.float32)
    y = xf * lax.rsqrt(jnp.mean(xf * xf, axis=-1, keepdims=True) + EPS)
    return (y * g.astype(jnp.float32)).astype(x.dtype)


def rotary_tables(positions):
    half = MLA_ROPE // 2
    inv_freq = ROPE_THETA ** (-jnp.arange(half, dtype=jnp.float32) / half)
    ang = positions.astype(jnp.float32)[:, :, None, None] * inv_freq
    return jnp.cos(ang), jnp.sin(ang)


def rotary(x, cos, sin):
    x1, x2 = jnp.split(x.astype(jnp.float32), 2, axis=-1)
    return jnp.concatenate([x1 * cos - x2 * sin, x1 * sin + x2 * cos], axis=-1).astype(x.dtype)


def _ssm_combine(left, right):
    a_l, b_l = left
    a_r, b_r = right
    return a_l * a_r, a_r * b_l + b_r


def s5_mix(u, lam_re, lam_im, log_step, b_re, b_im, c_re, c_im, d):
    bsz, seq, _ = u.shape
    f32 = jnp.float32
    uf = u.astype(f32).reshape(bsz, seq, S5_GROUPS, S5_GROUP_CH)
    lam = lax.complex(lam_re.astype(f32), lam_im.astype(f32))
    step = jnp.exp(log_step.astype(f32))[:, None]
    a_bar = jnp.exp(lam * step)
    b_mat = lax.complex(b_re.astype(f32), b_im.astype(f32))
    c_mat = lax.complex(c_re.astype(f32), c_im.astype(f32))
    b_bar = ((a_bar - 1.0) / lam)[..., None] * b_mat
    bu = jnp.einsum('blgc,gpc->blgp', uf.astype(jnp.complex64), b_bar)
    a_seq = jnp.broadcast_to(a_bar, (1, seq) + a_bar.shape)
    _, state = lax.associative_scan(_ssm_combine, (a_seq, bu), axis=1)
    y = jnp.einsum('blgp,gcp->blgc', state, c_mat).real + d.astype(f32).reshape(S5_GROUPS, S5_GROUP_CH) * uf
    return y.reshape(bsz, seq, PRIMARY_WIDTH).astype(u.dtype)


def causal_block_attention(q, k, v, scale):
    bsz, seq, heads, dk = q.shape
    dv = v.shape[-1]
    n_blocks = seq // Q_BLOCK
    q_blocks = q.reshape(bsz, n_blocks, Q_BLOCK, heads, dk).transpose(1, 0, 2, 3, 4)
    k_pos = jnp.arange(seq)

    def one_block(args):
        q_blk, blk = args
        s = jnp.einsum('bqhd,bkhd->bhqk', q_blk, k).astype(jnp.float32) * scale
        q_pos = blk * Q_BLOCK + jnp.arange(Q_BLOCK)
        s = jnp.where(k_pos[None, :] <= q_pos[:, None], s, jnp.finfo(jnp.float32).min)
        p = jax.nn.softmax(s, axis=-1).astype(v.dtype)
        return jnp.einsum('bhqk,bkhd->bqhd', p, v)

    out = lax.map(one_block, (q_blocks, jnp.arange(n_blocks)))
    return out.transpose(1, 0, 2, 3, 4).reshape(bsz, seq, heads, dv)


def memory_attention(xq, mem, mem_norm, w_mem_kv, xq_norm, xk_norm):
    bsz, seq, _ = xq.shape
    kv = rms_norm(mem, mem_norm) @ w_mem_kv
    k, v = jnp.split(kv, 2, axis=-1)
    k = rms_norm(k.reshape(bsz, -1, X_HEADS, X_HEAD_DIM), xk_norm)
    v = v.reshape(bsz, -1, X_HEADS, X_HEAD_DIM)
    q = rms_norm(xq.reshape(bsz, seq, X_HEADS, X_HEAD_DIM), xq_norm)
    s = jnp.einsum('blhd,bmhd->bhlm', q, k).astype(jnp.float32) * (X_HEAD_DIM ** -0.5)
    p = jax.nn.softmax(s, axis=-1).astype(v.dtype)
    return jnp.einsum('bhlm,bmhd->blhd', p, v).reshape(bsz, seq, XQ_WIDTH)


def merge_branches(x, mixer_out, xq, gate, mem, w_out, mem_norm, w_mem_kv, xq_norm, xk_norm):
    mem_out = memory_attention(xq, mem, mem_norm, w_mem_kv, xq_norm, xk_norm)
    o = jnp.concatenate([mixer_out, mem_out], axis=-1) * jax.nn.silu(gate)
    return x + o @ w_out


def s5_layer(x, mem, ln, w_in, lam_re, lam_im, log_step, b_re, b_im, c_re, c_im, d, w_glu,
             w_out, mem_norm, w_mem_kv, xq_norm, xk_norm):
    proj = rms_norm(x, ln) @ w_in
    u, xq, gate = jnp.split(proj, [PRIMARY_WIDTH, PRIMARY_WIDTH + XQ_WIDTH], axis=-1)
    y = s5_mix(u, lam_re, lam_im, log_step, b_re, b_im, c_re, c_im, d)
    y_a, y_b = jnp.split(jax.nn.gelu(y) @ w_glu, 2, axis=-1)
    y = y_a * jax.nn.sigmoid(y_b)
    return merge_branches(x, y, xq, gate, mem, w_out, mem_norm, w_mem_kv, xq_norm, xk_norm)


def mla_layer(x, mem, cos, sin, ln, w_in, q_lora_norm, kv_lora_norm, w_uq, w_ukv,
              q_nope_norm, k_nope_norm, q_rope_norm, k_rope_norm,
              w_out, mem_norm, w_mem_kv, xq_norm, xk_norm):
    bsz, seq, _ = x.shape
    proj = rms_norm(x, ln) @ w_in
    o1 = MLA_Q_LORA
    o2 = o1 + MLA_KV_LORA
    o3 = o2 + MLA_ROPE
    o4 = o3 + XQ_WIDTH
    c_q, c_kv, k_rope, xq, gate = jnp.split(proj, [o1, o2, o3, o4], axis=-1)
    q = (rms_norm(c_q, q_lora_norm) @ w_uq).reshape(bsz, seq, MLA_HEADS, MLA_NOPE + MLA_ROPE)
    kv = (rms_norm(c_kv, kv_lora_norm) @ w_ukv).reshape(bsz, seq, MLA_HEADS, MLA_NOPE + MLA_V)
    q_nope, q_rope = q[..., :MLA_NOPE], q[..., MLA_NOPE:]
    k_nope, v = kv[..., :MLA_NOPE], kv[..., MLA_NOPE:]
    q_rope = rotary(rms_norm(q_rope, q_rope_norm), cos, sin)
    k_rope = rotary(rms_norm(k_rope.reshape(bsz, seq, 1, MLA_ROPE), k_rope_norm), cos, sin)
    q_full = jnp.concatenate([rms_norm(q_nope, q_nope_norm), q_rope], axis=-1)
    k_full = jnp.concatenate([rms_norm(k_nope, k_nope_norm),
                              jnp.broadcast_to(k_rope, (bsz, seq, MLA_HEADS, MLA_ROPE))], axis=-1)
    attn = causal_block_attention(q_full, k_full, v, (MLA_NOPE + MLA_ROPE) ** -0.5)
    attn = attn.reshape(bsz, seq, PRIMARY_WIDTH)
    return merge_branches(x, attn, xq, gate, mem, w_out, mem_norm, w_mem_kv, xq_norm, xk_norm)


def _fwd_setup_inputs(seed: int = 0) -> dict:
    key = jax.random.key(seed)
    k = jax.random.split(key, 32)
    f32 = jnp.float32

    def w(kk, shape, fan_in):
        return jax.random.normal(kk, shape, f32) * (fan_in ** -0.5)

    def gain(kk, shape):
        return 1.0 + 0.02 * jax.random.normal(kk, shape, f32)

    x = jax.random.normal(k[0], (BATCH, SEQ, D_MODEL), f32)
    mem = jax.random.normal(k[1], (BATCH, MEM_LEN, D_MODEL), f32)
    offsets = jax.random.randint(k[2], (BATCH, 1), 0, 4096, dtype=jnp.int32)
    positions = offsets + jnp.arange(SEQ, dtype=jnp.int32)[None, :]

    lam_im_base = math.pi * jnp.arange(S5_STATE, dtype=f32)
    return {
        "x": x,
        "mem": mem,
        "positions": positions,
        "ln_gain": gain(k[3], (DEPTH, D_MODEL)),
        "w_out": w(k[4], (DEPTH, BRANCH_WIDTH, D_MODEL), BRANCH_WIDTH),
        "mem_norm": gain(k[5], (DEPTH, D_MODEL)),
        "w_mem_kv": w(k[6], (DEPTH, D_MODEL, 2 * XQ_WIDTH), D_MODEL),
        "xq_norm": gain(k[7], (DEPTH, X_HEAD_DIM)),
        "xk_norm": gain(k[8], (DEPTH, X_HEAD_DIM)),
        "s5_w_in": w(k[9], (N_S5, D_MODEL, S5_IN_WIDTH), D_MODEL),
        "s5_lambda_re": -0.5 + 0.01 * jax.random.normal(k[10], (N_S5, S5_GROUPS, S5_STATE), f32),
        "s5_lambda_im": lam_im_base + 0.01 * jax.random.normal(k[11], (N_S5, S5_GROUPS, S5_STATE), f32),
        "s5_log_step": jax.random.uniform(k[12], (N_S5, S5_GROUPS), f32,
                                          math.log(S5_STEP_MIN), math.log(S5_STEP_MAX)),
        "s5_b_re": w(k[13], (N_S5, S5_GROUPS, S5_STATE, S5_GROUP_CH), 2 * S5_GROUP_CH),
        "s5_b_im": w(k[14], (N_S5, S5_GROUPS, S5_STATE, S5_GROUP_CH), 2 * S5_GROUP_CH),
        "s5_c_re": w(k[15], (N_S5, S5_GROUPS, S5_GROUP_CH, S5_STATE), S5_STATE),
        "s5_c_im": w(k[16], (N_S5, S5_GROUPS, S5_GROUP_CH, S5_STATE), S5_STATE),
        "s5_d": jax.random.normal(k[17], (N_S5, PRIMARY_WIDTH), f32),
        "s5_w_glu": w(k[18], (N_S5, PRIMARY_WIDTH, 2 * PRIMARY_WIDTH), PRIMARY_WIDTH),
        "mla_w_in": w(k[19], (N_MLA, D_MODEL, MLA_IN_WIDTH), D_MODEL),
        "mla_q_lora_norm": gain(k[20], (N_MLA, MLA_Q_LORA)),
        "mla_kv_lora_norm": gain(k[21], (N_MLA, MLA_KV_LORA)),
        "mla_w_uq": w(k[22], (N_MLA, MLA_Q_LORA, MLA_HEADS * (MLA_NOPE + MLA_ROPE)), MLA_Q_LORA),
        "mla_w_ukv": w(k[23], (N_MLA, MLA_KV_LORA, MLA_HEADS * (MLA_NOPE + MLA_V)), MLA_KV_LORA),
        "mla_q_nope_norm": gain(k[24], (N_MLA, MLA_NOPE)),
        "mla_k_nope_norm": gain(k[25], (N_MLA, MLA_NOPE)),
        "mla_q_rope_norm": gain(k[26], (N_MLA, MLA_ROPE)),
        "mla_k_rope_norm": gain(k[27], (N_MLA, MLA_ROPE)),
    }


def _fwd_reference(x, mem, positions, ln_gain, w_out, mem_norm, w_mem_kv, xq_norm, xk_norm,
              s5_w_in, s5_lambda_re, s5_lambda_im, s5_log_step, s5_b_re, s5_b_im, s5_c_re, s5_c_im,
              s5_d, s5_w_glu, mla_w_in, mla_q_lora_norm, mla_kv_lora_norm, mla_w_uq, mla_w_ukv,
              mla_q_nope_norm, mla_k_nope_norm, mla_q_rope_norm, mla_k_rope_norm):
    cos, sin = rotary_tables(positions)
    for i in range(DEPTH):
        j = i // N_MIXERS
        if i % N_MIXERS == 0:
            x = s5_layer(x, mem, ln_gain[i], s5_w_in[j], s5_lambda_re[j], s5_lambda_im[j], s5_log_step[j],
                         s5_b_re[j], s5_b_im[j], s5_c_re[j], s5_c_im[j], s5_d[j], s5_w_glu[j],
                         w_out[i], mem_norm[i], w_mem_kv[i], xq_norm[i], xk_norm[i])
        else:
            x = mla_layer(x, mem, cos, sin, ln_gain[i], mla_w_in[j], mla_q_lora_norm[j], mla_kv_lora_norm[j],
                          mla_w_uq[j], mla_w_ukv[j], mla_q_nope_norm[j], mla_k_nope_norm[j],
                          mla_q_rope_norm[j], mla_k_rope_norm[j],
                          w_out[i], mem_norm[i], w_mem_kv[i], xq_norm[i], xk_norm[i])
    return x


import jax as _jax
import jax.numpy as _jnp

TWIN_FORMAT = 'train_step'
FWD_PARAMS = ['x', 'mem', 'positions', 'ln_gain', 'w_out', 'mem_norm', 'w_mem_kv', 'xq_norm', 'xk_norm', 's5_w_in', 's5_lambda_re', 's5_lambda_im', 's5_log_step', 's5_b_re', 's5_b_im', 's5_c_re', 's5_c_im', 's5_d', 's5_w_glu', 'mla_w_in', 'mla_q_lora_norm', 'mla_kv_lora_norm', 'mla_w_uq', 'mla_w_ukv', 'mla_q_nope_norm', 'mla_k_nope_norm', 'mla_q_rope_norm', 'mla_k_rope_norm']
TWIN_WEIGHTS = ['ln_gain', 'w_out', 'mem_norm', 'w_mem_kv', 'xq_norm', 'xk_norm', 's5_w_in', 's5_lambda_re', 's5_lambda_im', 's5_log_step', 's5_b_re', 's5_b_im', 's5_c_re', 's5_c_im', 's5_d', 's5_w_glu', 'mla_w_in', 'mla_q_lora_norm', 'mla_kv_lora_norm', 'mla_w_uq', 'mla_w_ukv', 'mla_q_nope_norm', 'mla_k_nope_norm', 'mla_q_rope_norm', 'mla_k_rope_norm']
TWIN_DIFF_INPUT = 'x'
TWIN_INPUTS = ['x', 'mem', 'positions', 'ln_gain', 'w_out', 'mem_norm', 'w_mem_kv', 'xq_norm', 'xk_norm', 's5_w_in', 's5_lambda_re', 's5_lambda_im', 's5_log_step', 's5_b_re', 's5_b_im', 's5_c_re', 's5_c_im', 's5_d', 's5_w_glu', 'mla_w_in', 'mla_q_lora_norm', 'mla_kv_lora_norm', 'mla_w_uq', 'mla_w_ukv', 'mla_q_nope_norm', 'mla_k_nope_norm', 'mla_q_rope_norm', 'mla_k_rope_norm', 'loss_target', 'm_ln_gain', 'm_w_out', 'm_mem_norm', 'm_w_mem_kv', 'm_xq_norm', 'm_xk_norm', 'm_s5_w_in', 'm_s5_lambda_re', 'm_s5_lambda_im', 'm_s5_log_step', 'm_s5_b_re', 'm_s5_b_im', 'm_s5_c_re', 'm_s5_c_im', 'm_s5_d', 'm_s5_w_glu', 'm_mla_w_in', 'm_mla_q_lora_norm', 'm_mla_kv_lora_norm', 'm_mla_w_uq', 'm_mla_w_ukv', 'm_mla_q_nope_norm', 'm_mla_k_nope_norm', 'm_mla_q_rope_norm', 'm_mla_k_rope_norm', 'v_ln_gain', 'v_w_out', 'v_mem_norm', 'v_w_mem_kv', 'v_xq_norm', 'v_xk_norm', 'v_s5_w_in', 'v_s5_lambda_re', 'v_s5_lambda_im', 'v_s5_log_step', 'v_s5_b_re', 'v_s5_b_im', 'v_s5_c_re', 'v_s5_c_im', 'v_s5_d', 'v_s5_w_glu', 'v_mla_w_in', 'v_mla_q_lora_norm', 'v_mla_kv_lora_norm', 'v_mla_w_uq', 'v_mla_w_ukv', 'v_mla_q_nope_norm', 'v_mla_k_nope_norm', 'v_mla_q_rope_norm', 'v_mla_k_rope_norm']
TWIN_OUTPUTS = ['loss', 'grad_x', 'grad_ln_gain', 'grad_w_out', 'grad_mem_norm', 'grad_w_mem_kv', 'grad_xq_norm', 'grad_xk_norm', 'grad_s5_w_in', 'grad_s5_lambda_re', 'grad_s5_lambda_im', 'grad_s5_log_step', 'grad_s5_b_re', 'grad_s5_b_im', 'grad_s5_c_re', 'grad_s5_c_im', 'grad_s5_d', 'grad_s5_w_glu', 'grad_mla_w_in', 'grad_mla_q_lora_norm', 'grad_mla_kv_lora_norm', 'grad_mla_w_uq', 'grad_mla_w_ukv', 'grad_mla_q_nope_norm', 'grad_mla_k_nope_norm', 'grad_mla_q_rope_norm', 'grad_mla_k_rope_norm', 'delta_ln_gain', 'delta_w_out', 'delta_mem_norm', 'delta_w_mem_kv', 'delta_xq_norm', 'delta_xk_norm', 'delta_s5_w_in', 'delta_s5_lambda_re', 'delta_s5_lambda_im', 'delta_s5_log_step', 'delta_s5_b_re', 'delta_s5_b_im', 'delta_s5_c_re', 'delta_s5_c_im', 'delta_s5_d', 'delta_s5_w_glu', 'delta_mla_w_in', 'delta_mla_q_lora_norm', 'delta_mla_kv_lora_norm', 'delta_mla_w_uq', 'delta_mla_w_ukv', 'delta_mla_q_nope_norm', 'delta_mla_k_nope_norm', 'delta_mla_q_rope_norm', 'delta_mla_k_rope_norm', 'new_m_ln_gain', 'new_m_w_out', 'new_m_mem_norm', 'new_m_w_mem_kv', 'new_m_xq_norm', 'new_m_xk_norm', 'new_m_s5_w_in', 'new_m_s5_lambda_re', 'new_m_s5_lambda_im', 'new_m_s5_log_step', 'new_m_s5_b_re', 'new_m_s5_b_im', 'new_m_s5_c_re', 'new_m_s5_c_im', 'new_m_s5_d', 'new_m_s5_w_glu', 'new_m_mla_w_in', 'new_m_mla_q_lora_norm', 'new_m_mla_kv_lora_norm', 'new_m_mla_w_uq', 'new_m_mla_w_ukv', 'new_m_mla_q_nope_norm', 'new_m_mla_k_nope_norm', 'new_m_mla_q_rope_norm', 'new_m_mla_k_rope_norm', 'new_v_ln_gain', 'new_v_w_out', 'new_v_mem_norm', 'new_v_w_mem_kv', 'new_v_xq_norm', 'new_v_xk_norm', 'new_v_s5_w_in', 'new_v_s5_lambda_re', 'new_v_s5_lambda_im', 'new_v_s5_log_step', 'new_v_s5_b_re', 'new_v_s5_b_im', 'new_v_s5_c_re', 'new_v_s5_c_im', 'new_v_s5_d', 'new_v_s5_w_glu', 'new_v_mla_w_in', 'new_v_mla_q_lora_norm', 'new_v_mla_kv_lora_norm', 'new_v_mla_w_uq', 'new_v_mla_w_ukv', 'new_v_mla_q_nope_norm', 'new_v_mla_k_nope_norm', 'new_v_mla_q_rope_norm', 'new_v_mla_k_rope_norm']
TWIN_LEAF_KINDS = {'loss': 'loss', 'grad_x': 'grad_x', 'grad_ln_gain': 'grad_w', 'grad_w_out': 'grad_w', 'grad_mem_norm': 'grad_w', 'grad_w_mem_kv': 'grad_w', 'grad_xq_norm': 'grad_w', 'grad_xk_norm': 'grad_w', 'grad_s5_w_in': 'grad_w', 'grad_s5_lambda_re': 'grad_w', 'grad_s5_lambda_im': 'grad_w', 'grad_s5_log_step': 'grad_w', 'grad_s5_b_re': 'grad_w', 'grad_s5_b_im': 'grad_w', 'grad_s5_c_re': 'grad_w', 'grad_s5_c_im': 'grad_w', 'grad_s5_d': 'grad_w', 'grad_s5_w_glu': 'grad_w', 'grad_mla_w_in': 'grad_w', 'grad_mla_q_lora_norm': 'grad_w', 'grad_mla_kv_lora_norm': 'grad_w', 'grad_mla_w_uq': 'grad_w', 'grad_mla_w_ukv': 'grad_w', 'grad_mla_q_nope_norm': 'grad_w', 'grad_mla_k_nope_norm': 'grad_w', 'grad_mla_q_rope_norm': 'grad_w', 'grad_mla_k_rope_norm': 'grad_w', 'delta_ln_gain': 'delta_w', 'delta_w_out': 'delta_w', 'delta_mem_norm': 'delta_w', 'delta_w_mem_kv': 'delta_w', 'delta_xq_norm': 'delta_w', 'delta_xk_norm': 'delta_w', 'delta_s5_w_in': 'delta_w', 'delta_s5_lambda_re': 'delta_w', 'delta_s5_lambda_im': 'delta_w', 'delta_s5_log_step': 'delta_w', 'delta_s5_b_re': 'delta_w', 'delta_s5_b_im': 'delta_w', 'delta_s5_c_re': 'delta_w', 'delta_s5_c_im': 'delta_w', 'delta_s5_d': 'delta_w', 'delta_s5_w_glu': 'delta_w', 'delta_mla_w_in': 'delta_w', 'delta_mla_q_lora_norm': 'delta_w', 'delta_mla_kv_lora_norm': 'delta_w', 'delta_mla_w_uq': 'delta_w', 'delta_mla_w_ukv': 'delta_w', 'delta_mla_q_nope_norm': 'delta_w', 'delta_mla_k_nope_norm': 'delta_w', 'delta_mla_q_rope_norm': 'delta_w', 'delta_mla_k_rope_norm': 'delta_w', 'new_m_ln_gain': 'new_m', 'new_m_w_out': 'new_m', 'new_m_mem_norm': 'new_m', 'new_m_w_mem_kv': 'new_m', 'new_m_xq_norm': 'new_m', 'new_m_xk_norm': 'new_m', 'new_m_s5_w_in': 'new_m', 'new_m_s5_lambda_re': 'new_m', 'new_m_s5_lambda_im': 'new_m', 'new_m_s5_log_step': 'new_m', 'new_m_s5_b_re': 'new_m', 'new_m_s5_b_im': 'new_m', 'new_m_s5_c_re': 'new_m', 'new_m_s5_c_im': 'new_m', 'new_m_s5_d': 'new_m', 'new_m_s5_w_glu': 'new_m', 'new_m_mla_w_in': 'new_m', 'new_m_mla_q_lora_norm': 'new_m', 'new_m_mla_kv_lora_norm': 'new_m', 'new_m_mla_w_uq': 'new_m', 'new_m_mla_w_ukv': 'new_m', 'new_m_mla_q_nope_norm': 'new_m', 'new_m_mla_k_nope_norm': 'new_m', 'new_m_mla_q_rope_norm': 'new_m', 'new_m_mla_k_rope_norm': 'new_m', 'new_v_ln_gain': 'new_v', 'new_v_w_out': 'new_v', 'new_v_mem_norm': 'new_v', 'new_v_w_mem_kv': 'new_v', 'new_v_xq_norm': 'new_v', 'new_v_xk_norm': 'new_v', 'new_v_s5_w_in': 'new_v', 'new_v_s5_lambda_re': 'new_v', 'new_v_s5_lambda_im': 'new_v', 'new_v_s5_log_step': 'new_v', 'new_v_s5_b_re': 'new_v', 'new_v_s5_b_im': 'new_v', 'new_v_s5_c_re': 'new_v', 'new_v_s5_c_im': 'new_v', 'new_v_s5_d': 'new_v', 'new_v_s5_w_glu': 'new_v', 'new_v_mla_w_in': 'new_v', 'new_v_mla_q_lora_norm': 'new_v', 'new_v_mla_kv_lora_norm': 'new_v', 'new_v_mla_w_uq': 'new_v', 'new_v_mla_w_ukv': 'new_v', 'new_v_mla_q_nope_norm': 'new_v', 'new_v_mla_k_nope_norm': 'new_v', 'new_v_mla_q_rope_norm': 'new_v', 'new_v_mla_k_rope_norm': 'new_v'}


def _forward(args):
    return _fwd_reference(*[args[k] for k in FWD_PARAMS])


def _output_shape():
    out = _jax.eval_shape(lambda: _forward(_fwd_setup_inputs(0)))
    return out.shape, out.dtype

N_MICROBATCH = 1
ADAM_LR = 0.001
ADAM_B1 = 0.9
ADAM_B2 = 0.999
ADAM_EPS = 1e-08
ADAM_WD = 0.01
ADAM_STEP = 10
PER_EXAMPLE_BATCH_AXIS = {'x': 0, 'mem': 0, 'positions': 0, 'loss_target': 0}
SHARED_INPUTS = []
_WEIGHT_DTYPES = {'ln_gain': _jnp.float32, 'w_out': _jnp.float32, 'mem_norm': _jnp.float32, 'w_mem_kv': _jnp.float32, 'xq_norm': _jnp.float32, 'xk_norm': _jnp.float32, 's5_w_in': _jnp.float32, 's5_lambda_re': _jnp.float32, 's5_lambda_im': _jnp.float32, 's5_log_step': _jnp.float32, 's5_b_re': _jnp.float32, 's5_b_im': _jnp.float32, 's5_c_re': _jnp.float32, 's5_c_im': _jnp.float32, 's5_d': _jnp.float32, 's5_w_glu': _jnp.float32, 'mla_w_in': _jnp.float32, 'mla_q_lora_norm': _jnp.float32, 'mla_kv_lora_norm': _jnp.float32, 'mla_w_uq': _jnp.float32, 'mla_w_ukv': _jnp.float32, 'mla_q_nope_norm': _jnp.float32, 'mla_k_nope_norm': _jnp.float32, 'mla_q_rope_norm': _jnp.float32, 'mla_k_rope_norm': _jnp.float32}
MOMENT_SCALE = {'ln_gain': 8.312214e-01, 'w_out': 3.211596e-02, 'mem_norm': 1.511262e-02, 'w_mem_kv': 9.003662e-03, 'xq_norm': 1.053947e-01, 'xk_norm': 1.055171e-01, 's5_w_in': 3.157497e-02, 's5_lambda_re': 2.379912e-03, 's5_lambda_im': 2.181912e-03, 's5_log_step': 1.151114e+00, 's5_b_re': 1.540825e-03, 's5_b_im': 1.566182e-03, 's5_c_re': 2.180771e-03, 's5_c_im': 2.186878e-03, 's5_d': 4.042989e-01, 's5_w_glu': 6.329505e-02, 'mla_w_in': 2.424237e-02, 'mla_q_lora_norm': 2.741091e-02, 'mla_kv_lora_norm': 2.247514e-01, 'mla_w_uq': 1.257017e-02, 'mla_w_ukv': 1.640949e-02, 'mla_q_nope_norm': 1.659029e-01, 'mla_k_nope_norm': 1.655114e-01, 'mla_q_rope_norm': 1.564985e-01, 'mla_k_rope_norm': 1.575024e-01}


def _to_microbatches(a, axis):
    t = _jnp.moveaxis(a, axis, 0)
    t = t.reshape((N_MICROBATCH, t.shape[0] // N_MICROBATCH) + t.shape[1:])
    return _jnp.moveaxis(t, 1, axis + 1)


def setup_inputs(seed: int = 0) -> dict:
    inp = _fwd_setup_inputs(seed)
    key = _jax.random.fold_in(_jax.random.key(seed), 7919)
    shape, _ = _output_shape()
    out = dict(inp)
    out["loss_target"] = _jax.random.normal(_jax.random.fold_in(key, 0), shape, _jnp.float32)
    for i, name in enumerate(TWIN_WEIGHTS):
        w = inp[name].astype(_jnp.float32)
        if MOMENT_SCALE is None:
            s = _jnp.sqrt(_jnp.mean(_jnp.square(w)) + 1e-30)
        else:
            s = MOMENT_SCALE[name]
        km, kv = _jax.random.split(_jax.random.fold_in(key, i + 1))
        out[name] = w
        out["m_" + name] = s * _jax.random.normal(km, w.shape, _jnp.float32)
        out["v_" + name] = (s * s) * _jax.random.uniform(kv, w.shape, _jnp.float32, 0.5, 1.5)
    if N_MICROBATCH > 1:
        for name, axis in PER_EXAMPLE_BATCH_AXIS.items():
            out[name] = _to_microbatches(out[name], axis)
    return {'x': out['x'], 'mem': out['mem'], 'positions': out['positions'], 'ln_gain': out['ln_gain'], 'w_out': out['w_out'], 'mem_norm': out['mem_norm'], 'w_mem_kv': out['w_mem_kv'], 'xq_norm': out['xq_norm'], 'xk_norm': out['xk_norm'], 's5_w_in': out['s5_w_in'], 's5_lambda_re': out['s5_lambda_re'], 's5_lambda_im': out['s5_lambda_im'], 's5_log_step': out['s5_log_step'], 's5_b_re': out['s5_b_re'], 's5_b_im': out['s5_b_im'], 's5_c_re': out['s5_c_re'], 's5_c_im': out['s5_c_im'], 's5_d': out['s5_d'], 's5_w_glu': out['s5_w_glu'], 'mla_w_in': out['mla_w_in'], 'mla_q_lora_norm': out['mla_q_lora_norm'], 'mla_kv_lora_norm': out['mla_kv_lora_norm'], 'mla_w_uq': out['mla_w_uq'], 'mla_w_ukv': out['mla_w_ukv'], 'mla_q_nope_norm': out['mla_q_nope_norm'], 'mla_k_nope_norm': out['mla_k_nope_norm'], 'mla_q_rope_norm': out['mla_q_rope_norm'], 'mla_k_rope_norm': out['mla_k_rope_norm'], 'loss_target': out['loss_target'], 'm_ln_gain': out['m_ln_gain'], 'm_w_out': out['m_w_out'], 'm_mem_norm': out['m_mem_norm'], 'm_w_mem_kv': out['m_w_mem_kv'], 'm_xq_norm': out['m_xq_norm'], 'm_xk_norm': out['m_xk_norm'], 'm_s5_w_in': out['m_s5_w_in'], 'm_s5_lambda_re': out['m_s5_lambda_re'], 'm_s5_lambda_im': out['m_s5_lambda_im'], 'm_s5_log_step': out['m_s5_log_step'], 'm_s5_b_re': out['m_s5_b_re'], 'm_s5_b_im': out['m_s5_b_im'], 'm_s5_c_re': out['m_s5_c_re'], 'm_s5_c_im': out['m_s5_c_im'], 'm_s5_d': out['m_s5_d'], 'm_s5_w_glu': out['m_s5_w_glu'], 'm_mla_w_in': out['m_mla_w_in'], 'm_mla_q_lora_norm': out['m_mla_q_lora_norm'], 'm_mla_kv_lora_norm': out['m_mla_kv_lora_norm'], 'm_mla_w_uq': out['m_mla_w_uq'], 'm_mla_w_ukv': out['m_mla_w_ukv'], 'm_mla_q_nope_norm': out['m_mla_q_nope_norm'], 'm_mla_k_nope_norm': out['m_mla_k_nope_norm'], 'm_mla_q_rope_norm': out['m_mla_q_rope_norm'], 'm_mla_k_rope_norm': out['m_mla_k_rope_norm'], 'v_ln_gain': out['v_ln_gain'], 'v_w_out': out['v_w_out'], 'v_mem_norm': out['v_mem_norm'], 'v_w_mem_kv': out['v_w_mem_kv'], 'v_xq_norm': out['v_xq_norm'], 'v_xk_norm': out['v_xk_norm'], 'v_s5_w_in': out['v_s5_w_in'], 'v_s5_lambda_re': out['v_s5_lambda_re'], 'v_s5_lambda_im': out['v_s5_lambda_im'], 'v_s5_log_step': out['v_s5_log_step'], 'v_s5_b_re': out['v_s5_b_re'], 'v_s5_b_im': out['v_s5_b_im'], 'v_s5_c_re': out['v_s5_c_re'], 'v_s5_c_im': out['v_s5_c_im'], 'v_s5_d': out['v_s5_d'], 'v_s5_w_glu': out['v_s5_w_glu'], 'v_mla_w_in': out['v_mla_w_in'], 'v_mla_q_lora_norm': out['v_mla_q_lora_norm'], 'v_mla_kv_lora_norm': out['v_mla_kv_lora_norm'], 'v_mla_w_uq': out['v_mla_w_uq'], 'v_mla_w_ukv': out['v_mla_w_ukv'], 'v_mla_q_nope_norm': out['v_mla_q_nope_norm'], 'v_mla_k_nope_norm': out['v_mla_k_nope_norm'], 'v_mla_q_rope_norm': out['v_mla_q_rope_norm'], 'v_mla_k_rope_norm': out['v_mla_k_rope_norm']}


def _loss(weights, diff, rest, loss_target):
    with _jax.named_scope("forward"):
        args = {**rest, TWIN_DIFF_INPUT: diff, **{k: w.astype(_WEIGHT_DTYPES[k]) for k, w in weights.items()}}
        y = _forward(args)
    with _jax.named_scope("loss_head"):
        err = _jnp.square(y.astype(_jnp.float32) - loss_target)
        return 0.5 * _jnp.sum(_jnp.mean(err, axis=-1)) if err.ndim else 0.5 * err


def _adamw(w, g, m, v):
    m = ADAM_B1 * m + (1.0 - ADAM_B1) * g
    v = ADAM_B2 * v + (1.0 - ADAM_B2) * _jnp.square(g)
    m_hat = m / (1.0 - ADAM_B1 ** ADAM_STEP)
    v_hat = v / (1.0 - ADAM_B2 ** ADAM_STEP)
    delta = -ADAM_LR * (m_hat / (_jnp.sqrt(v_hat) + ADAM_EPS) + ADAM_WD * w)
    return delta, m, v


def reference(x, mem, positions, ln_gain, w_out, mem_norm, w_mem_kv, xq_norm, xk_norm, s5_w_in, s5_lambda_re, s5_lambda_im, s5_log_step, s5_b_re, s5_b_im, s5_c_re, s5_c_im, s5_d, s5_w_glu, mla_w_in, mla_q_lora_norm, mla_kv_lora_norm, mla_w_uq, mla_w_ukv, mla_q_nope_norm, mla_k_nope_norm, mla_q_rope_norm, mla_k_rope_norm, loss_target, m_ln_gain, m_w_out, m_mem_norm, m_w_mem_kv, m_xq_norm, m_xk_norm, m_s5_w_in, m_s5_lambda_re, m_s5_lambda_im, m_s5_log_step, m_s5_b_re, m_s5_b_im, m_s5_c_re, m_s5_c_im, m_s5_d, m_s5_w_glu, m_mla_w_in, m_mla_q_lora_norm, m_mla_kv_lora_norm, m_mla_w_uq, m_mla_w_ukv, m_mla_q_nope_norm, m_mla_k_nope_norm, m_mla_q_rope_norm, m_mla_k_rope_norm, v_ln_gain, v_w_out, v_mem_norm, v_w_mem_kv, v_xq_norm, v_xk_norm, v_s5_w_in, v_s5_lambda_re, v_s5_lambda_im, v_s5_log_step, v_s5_b_re, v_s5_b_im, v_s5_c_re, v_s5_c_im, v_s5_d, v_s5_w_glu, v_mla_w_in, v_mla_q_lora_norm, v_mla_kv_lora_norm, v_mla_w_uq, v_mla_w_ukv, v_mla_q_nope_norm, v_mla_k_nope_norm, v_mla_q_rope_norm, v_mla_k_rope_norm):
    given = dict(x=x, mem=mem, positions=positions, ln_gain=ln_gain, w_out=w_out, mem_norm=mem_norm, w_mem_kv=w_mem_kv, xq_norm=xq_norm, xk_norm=xk_norm, s5_w_in=s5_w_in, s5_lambda_re=s5_lambda_re, s5_lambda_im=s5_lambda_im, s5_log_step=s5_log_step, s5_b_re=s5_b_re, s5_b_im=s5_b_im, s5_c_re=s5_c_re, s5_c_im=s5_c_im, s5_d=s5_d, s5_w_glu=s5_w_glu, mla_w_in=mla_w_in, mla_q_lora_norm=mla_q_lora_norm, mla_kv_lora_norm=mla_kv_lora_norm, mla_w_uq=mla_w_uq, mla_w_ukv=mla_w_ukv, mla_q_nope_norm=mla_q_nope_norm, mla_k_nope_norm=mla_k_nope_norm, mla_q_rope_norm=mla_q_rope_norm, mla_k_rope_norm=mla_k_rope_norm, loss_target=loss_target, m_ln_gain=m_ln_gain, m_w_out=m_w_out, m_mem_norm=m_mem_norm, m_w_mem_kv=m_w_mem_kv, m_xq_norm=m_xq_norm, m_xk_norm=m_xk_norm, m_s5_w_in=m_s5_w_in, m_s5_lambda_re=m_s5_lambda_re, m_s5_lambda_im=m_s5_lambda_im, m_s5_log_step=m_s5_log_step, m_s5_b_re=m_s5_b_re, m_s5_b_im=m_s5_b_im, m_s5_c_re=m_s5_c_re, m_s5_c_im=m_s5_c_im, m_s5_d=m_s5_d, m_s5_w_glu=m_s5_w_glu, m_mla_w_in=m_mla_w_in, m_mla_q_lora_norm=m_mla_q_lora_norm, m_mla_kv_lora_norm=m_mla_kv_lora_norm, m_mla_w_uq=m_mla_w_uq, m_mla_w_ukv=m_mla_w_ukv, m_mla_q_nope_norm=m_mla_q_nope_norm, m_mla_k_nope_norm=m_mla_k_nope_norm, m_mla_q_rope_norm=m_mla_q_rope_norm, m_mla_k_rope_norm=m_mla_k_rope_norm, v_ln_gain=v_ln_gain, v_w_out=v_w_out, v_mem_norm=v_mem_norm, v_w_mem_kv=v_w_mem_kv, v_xq_norm=v_xq_norm, v_xk_norm=v_xk_norm, v_s5_w_in=v_s5_w_in, v_s5_lambda_re=v_s5_lambda_re, v_s5_lambda_im=v_s5_lambda_im, v_s5_log_step=v_s5_log_step, v_s5_b_re=v_s5_b_re, v_s5_b_im=v_s5_b_im, v_s5_c_re=v_s5_c_re, v_s5_c_im=v_s5_c_im, v_s5_d=v_s5_d, v_s5_w_glu=v_s5_w_glu, v_mla_w_in=v_mla_w_in, v_mla_q_lora_norm=v_mla_q_lora_norm, v_mla_kv_lora_norm=v_mla_kv_lora_norm, v_mla_w_uq=v_mla_w_uq, v_mla_w_ukv=v_mla_w_ukv, v_mla_q_nope_norm=v_mla_q_nope_norm, v_mla_k_nope_norm=v_mla_k_nope_norm, v_mla_q_rope_norm=v_mla_q_rope_norm, v_mla_k_rope_norm=v_mla_k_rope_norm)
    weights = {n: given[n] for n in TWIN_WEIGHTS}
    shared = {n: given[n] for n in SHARED_INPUTS}
    per_example = {n: given[n] for n in ['x', 'mem', 'positions']}
    grad_fn = _jax.value_and_grad(_loss, argnums=(0, 1))

    def one_microbatch(ex, loss_target):
        ex = dict(ex)
        diff = ex.pop(TWIN_DIFF_INPUT)
        return grad_fn(weights, diff, {**shared, **ex}, loss_target)

    if N_MICROBATCH == 1:
        loss, (grad_w, grad_x) = one_microbatch(per_example, given["loss_target"])
    else:
        def body(carry, xs):
            loss_sum, grad_sum = carry
            l_k, (gw_k, gx_k) = one_microbatch(xs[0], xs[1])
            with _jax.named_scope("update"):
                return (loss_sum + l_k, _jax.tree.map(_jnp.add, grad_sum, gw_k)), gx_k

        init = (_jnp.zeros((), _jnp.float32), _jax.tree.map(_jnp.zeros_like, weights))
        (loss, grad_w), grad_x = _jax.lax.scan(body, init, (per_example, given["loss_target"]))
    with _jax.named_scope("update"):
        delta_w, new_m, new_v = {}, {}, {}
        for n in TWIN_WEIGHTS:
            delta_w[n], new_m[n], new_v[n] = _adamw(weights[n], grad_w[n], given["m_" + n], given["v_" + n])
    return (loss, grad_x, *[grad_w[n] for n in TWIN_WEIGHTS], *[delta_w[n] for n in TWIN_WEIGHTS],
            *[new_m[n] for n in TWIN_WEIGHTS], *[new_v[n] for n in TWIN_WEIGHTS])
```

```python
import functools
import math

import jax
import jax.numpy as jnp
from jax import lax
from jax.experimental import pallas as pl
from jax.experimental.pallas import tpu as pltpu

F32, BF16 = jnp.float32, jnp.bfloat16
SDS = jax.ShapeDtypeStruct

D = 1024
L = 2048
ML = 256
BW = 2 * D
XQW = BW // 4
PW = BW - XQW
XH, XHD = 4, 128
SG, SC, SP = 96, 16, 64
SN = SG * SP
NOPE, ROPE, VD = 128, 64, 128
MH = 12
QL, KVL = 512, 256
EPS = 1e-6
ROPE_THETA = 10000.0
MLA_IN = QL + KVL + ROPE + XQW + BW
MLA_IN_P = 3456
ADAM_LR, ADAM_B1, ADAM_B2, ADAM_EPS, ADAM_WD, ADAM_STEP = 0.001, 0.9, 0.999, 1e-08, 0.01, 10

VMEM_LIMIT = 48 * 2**20
SEG = 8
SEG_LEN = L // SEG
MESH_AXES = ("x", "y", "c")


def _cparams():
    return pltpu.CompilerParams(vmem_limit_bytes=VMEM_LIMIT)


def _dg(a, b, ca, cb):
    return lax.dot_general(a.astype(BF16), b.astype(BF16), (((ca,), (cb,)), ((), ())), preferred_element_type=F32)


@jax.custom_vjp
def mm_nn(a, b):
    return _dg(a, b, 1, 0)


mm_nn.defvjp(lambda a, b: (_dg(a, b, 1, 0), (a, b)), lambda res, g: (_dg(g, res[1], 1, 1), _dg(res[0], g, 0, 0)))


@jax.custom_vjp
def mm_nt(a, b):
    return _dg(a, b, 1, 1)


mm_nt.defvjp(lambda a, b: (_dg(a, b, 1, 1), (a, b)), lambda res, g: (_dg(g, res[1], 1, 0), _dg(g, res[0], 0, 0)))


@functools.partial(jax.custom_vjp, nondiff_argnums=(1,))
def lane_roll(x, shift):
    return pltpu.roll(x, shift, 1)


lane_roll.defvjp(lambda x, shift: (pltpu.roll(x, shift, 1), None),
                 lambda shift, _, g: (pltpu.roll(g, (128 - shift) % 128, 1),))


def rms(x, g):
    return x * lax.rsqrt(jnp.mean(x * x, axis=-1, keepdims=True) + EPS) * g


def softmax_rows(s):
    m = lax.stop_gradient(jnp.max(s, axis=-1, keepdims=True))
    e = jnp.exp(s - m)
    return e / jnp.sum(e, axis=-1, keepdims=True)


def silu(x):
    return x * jax.nn.sigmoid(x)


def stage(name, fn, grid, ins, outs):
    n_in = len(ins)

    def kern(*refs):
        res = fn(*[r[...] for r in refs[:n_in]])
        for r, v in zip(refs[n_in:], res):
            r[...] = v.astype(r.dtype)

    return pl.pallas_call(kern, grid=grid, in_specs=[s for _, s in ins], out_specs=[s for _, s in outs],
                          out_shape=[sd for sd, _ in outs], name=name, compiler_params=_cparams())(*[a for a, _ in ins])


def stage_bwd(name, fn, grid, ins, cts, diffs):
    n_in, n_ct = len(ins), len(cts)
    didx = [i for i, d in enumerate(diffs) if d is not None]

    def kern(*refs):
        vals = [r[...] for r in refs[:n_in]]

        def f(*dv):
            full = list(vals)
            for i, v in zip(didx, dv):
                full[i] = v
            return fn(*full)

        _, vjp = jax.vjp(f, *[vals[i].astype(F32) for i in didx])
        gs = vjp(tuple(c[...].astype(F32) for c in refs[n_in:n_in + n_ct]))
        for o_ref, i, g in zip(refs[n_in + n_ct:], didx, gs):
            if diffs[i][0] == "row":
                o_ref[...] = g.astype(o_ref.dtype)
            else:
                first = functools.reduce(jnp.logical_and, [pl.program_id(ax) == 0 for ax in diffs[i][1]])

                @pl.when(first)
                def _():
                    o_ref[...] = g

                @pl.when(jnp.logical_not(first))
                def _():
                    o_ref[...] += g

    out_shape, out_specs = [], []
    for i in didx:
        if diffs[i][0] == "row":
            out_shape.append(diffs[i][1])
            out_specs.append(diffs[i][2])
        else:
            out_shape.append(SDS(ins[i][0].shape, F32))
            out_specs.append(ins[i][1])
    return pl.pallas_call(kern, grid=grid, in_specs=[s for _, s in ins] + [s for _, s in cts], out_specs=out_specs,
                          out_shape=out_shape, name=name, compiler_params=_cparams())(*[a for a, _ in ins], *[a for a, _ in cts])


def rspec(tl, w, cb=0):
    return pl.BlockSpec((tl, w), lambda i: (i, cb))


def cspec(shape):
    return pl.BlockSpec(shape, lambda i: (0,) * len(shape))


def row_fwd(name, fn, rows, tl, row_ins, consts, outs):
    ins = [(a, rspec(tl, w, cb)) for a, w, cb in row_ins] + [(a, cspec(a.shape)) for a in consts]
    return stage(name, fn, (rows // tl,), ins, [(SDS((rows, w), dt), rspec(tl, w)) for w, dt in outs])


def row_bwd(name, fn, rows, tl, row_ins, consts, cts, row_diff, const_diff, row_dtype=F32):
    ins = [(a, rspec(tl, w, cb)) for a, w, cb in row_ins] + [(a, cspec(a.shape)) for a in consts]
    diffs = [("row", SDS((rows, w), row_dtype), rspec(tl, w)) if d else None for (a, w, cb), d in zip(row_ins, row_diff)]
    diffs += [("acc", (0,)) if d else None for d in const_diff]
    return stage_bwd(name, fn, (rows // tl,), ins, [(a, rspec(tl, w, cb)) for a, w, cb in cts], diffs)


def _pick(dim, prefs):
    for p in prefs:
        if dim % p == 0:
            return p
    return dim


def matmul(name, a, b, mode, out_dtype=F32, add=None):
    if mode == "tn":
        k_dim, m = a.shape
    else:
        m, k_dim = a.shape
    n = b.shape[0] if mode == "nt" else b.shape[1]
    tm = _pick(m, (512, 256, 128))
    tn = _pick(n, (1024, 768, 512, 384, 256, 128))
    tk = _pick(k_dim, (512, 384, 256, 128))
    nk = k_dim // tk
    a_spec = pl.BlockSpec((tk, tm), lambda i, j, k: (k, i)) if mode == "tn" else pl.BlockSpec((tm, tk), lambda i, j, k: (i, k))
    b_spec = pl.BlockSpec((tn, tk), lambda i, j, k: (j, k)) if mode == "nt" else pl.BlockSpec((tk, tn), lambda i, j, k: (k, j))
    o_spec = pl.BlockSpec((tm, tn), lambda i, j, k: (i, j))
    ca, cb = {"nn": (1, 0), "nt": (1, 1), "tn": (0, 0)}[mode]

    def kern(*refs):
        a_ref, b_ref = refs[0], refs[1]
        o_ref, acc = refs[-2], refs[-1]
        k = pl.program_id(2)

        @pl.when(k == 0)
        def _():
            acc[...] = jnp.zeros_like(acc)

        acc[...] += _dg(a_ref[...], b_ref[...], ca, cb)

        @pl.when(k == nk - 1)
        def _():
            r = acc[...]
            if add is not None:
                r = r + refs[2][...]
            o_ref[...] = r.astype(o_ref.dtype)

    ins, specs = [a, b], [a_spec, b_spec]
    if add is not None:
        ins.append(add)
        specs.append(o_spec)
    return pl.pallas_call(kern, grid=(m // tm, n // tn, nk), in_specs=specs, out_specs=o_spec,
                          out_shape=SDS((m, n), out_dtype), scratch_shapes=[pltpu.VMEM((tm, tn), F32)],
                          name=name, compiler_params=_cparams())(*ins)


SCAN_LANES = 256


def _cmul(ar, ai, br, bi):
    return ar * br - ai * bi, ar * bi + ai * br


def _sub_shift(x, down):
    row = lax.broadcasted_iota(jnp.int32, x.shape, 0)
    if down:
        return jnp.where(row == 0, 0.0, pltpu.roll(x, 1, 0))
    return jnp.where(row == SEG - 1, 0.0, pltpu.roll(x, SEG - 1, 0))


def _pow_seg_len(ar, ai):
    for _ in range(int(math.log2(SEG_LEN))):
        ar, ai = _cmul(ar, ai, ar, ai)
    return ar, ai


def s5_scan_fwd(bu_re, bu_im, a_re, a_im):
    lanes = SCAN_LANES

    def kern(bur, bui, ar_ref, ai_ref, sr, si):
        ar = jnp.broadcast_to(ar_ref[...], (SEG, lanes))
        ai = jnp.broadcast_to(ai_ref[...], (SEG, lanes))
        zero = jnp.zeros((SEG, lanes), F32)

        def local(i, carry):
            rows = pl.ds(pl.multiple_of(i * SEG, SEG), SEG)
            mr, mi = _cmul(ar, ai, carry[0], carry[1])
            nr, ni = mr + bur[rows, :], mi + bui[rows, :]
            sr[rows, :] = nr
            si[rows, :] = ni
            return nr, ni

        fr, fi = lax.fori_loop(0, SEG_LEN, local, (zero, zero))
        pr, pi = _pow_seg_len(ar, ai)
        ir, ii = zero, zero
        for _ in range(SEG - 1):
            mr, mi = _cmul(pr, pi, ir, ii)
            ir, ii = _sub_shift(mr + fr, True), _sub_shift(mi + fi, True)

        def carry_in(i, pw):
            rows = pl.ds(pl.multiple_of(i * SEG, SEG), SEG)
            cr, ci = _cmul(pw[0], pw[1], ir, ii)
            sr[rows, :] += cr
            si[rows, :] += ci
            return _cmul(pw[0], pw[1], ar, ai)

        lax.fori_loop(0, SEG_LEN, carry_in, (ar, ai))

    blk = pl.BlockSpec((L, lanes), lambda j: (0, j))
    vec = pl.BlockSpec((1, lanes), lambda j: (0, j))
    return pl.pallas_call(kern, grid=(SN // lanes,), in_specs=[blk, blk, vec, vec], out_specs=[blk, blk],
                          out_shape=[SDS((L, SN), F32)] * 2, name="s5_scan_fwd", compiler_params=_cparams())(bu_re, bu_im, a_re, a_im)


def s5_scan_bwd(g_re, g_im, s_re, s_im, a_re, a_im):
    lanes = SCAN_LANES

    def kern(gr, gi, sr, si, ar_ref, ai_ref, lr, li, dar, dai):
        ar = jnp.broadcast_to(ar_ref[...], (SEG, lanes))
        ai = -jnp.broadcast_to(ai_ref[...], (SEG, lanes))
        zero = jnp.zeros((SEG, lanes), F32)

        def local(k, carry):
            i = SEG_LEN - 1 - k
            rows = pl.ds(pl.multiple_of(i * SEG, SEG), SEG)
            mr, mi = _cmul(ar, ai, carry[0], carry[1])
            nr, ni = mr + gr[rows, :], mi + gi[rows, :]
            lr[rows, :] = nr
            li[rows, :] = ni
            return nr, ni

        fr, fi = lax.fori_loop(0, SEG_LEN, local, (zero, zero))
        pr, pi = _pow_seg_len(ar, ai)
        ir, ii = zero, zero
        for _ in range(SEG - 1):
            mr, mi = _cmul(pr, pi, ir, ii)
            ir, ii = _sub_shift(mr + fr, False), _sub_shift(mi + fi, False)

        def fix(rows, pw):
            cr, ci = _cmul(pw[0], pw[1], ir, ii)
            tr, ti = lr[rows, :] + cr, li[rows, :] + ci
            lr[rows, :] = tr
            li[rows, :] = ti
            return tr, ti

        def grad_a(tr, ti, spr, spi, acc):
            return acc[0] + tr * spr + ti * spi, acc[1] + ti * spr - tr * spi

        def carry_in(k, c):
            i = SEG_LEN - 1 - k
            rows = pl.ds(pl.multiple_of(i * SEG, SEG), SEG)
            prev = pl.ds(pl.multiple_of((i - 1) * SEG, SEG), SEG)
            tr, ti = fix(rows, (c[0], c[1]))
            acc = grad_a(tr, ti, sr[prev, :], si[prev, :], (c[2], c[3]))
            nr, ni = _cmul(c[0], c[1], ar, ai)
            return nr, ni, acc[0], acc[1]

        pwr, pwi, accr, acci = lax.fori_loop(0, SEG_LEN - 1, carry_in, (ar, ai, zero, zero))
        tr, ti = fix(pl.ds(0, SEG), (pwr, pwi))
        last = pl.ds((SEG_LEN - 1) * SEG, SEG)
        accr, acci = grad_a(tr, ti, _sub_shift(sr[last, :], True), _sub_shift(si[last, :], True), (accr, acci))
        dar[...] = jnp.sum(accr, axis=0, keepdims=True)
        dai[...] = jnp.sum(acci, axis=0, keepdims=True)

    blk = pl.BlockSpec((L, lanes), lambda j: (0, j))
    vec = pl.BlockSpec((1, lanes), lambda j: (0, j))
    return pl.pallas_call(kern, grid=(SN // lanes,), in_specs=[blk, blk, blk, blk, vec, vec], out_specs=[blk, blk, vec, vec],
                          out_shape=[SDS((L, SN), F32)] * 2 + [SDS((1, SN), F32)] * 2, name="s5_scan_bwd",
                          compiler_params=_cparams())(g_re, g_im, s_re, s_im, a_re, a_im)


def fn_rms(x, g):
    return (rms(x, g),)


def fn_s5_disc(lre, lim, ls):
    step = jnp.exp(ls)
    e = jnp.exp(lre * step)
    a_re, a_im = e * jnp.cos(lim * step), e * jnp.sin(lim * step)
    den = lre * lre + lim * lim
    nr, ni = a_re - 1.0, a_im
    return a_re, a_im, (nr * lre + ni * lim) / den, (ni * lre - nr * lim) / den


def _group_mask(rows, cols, row_div, col_div):
    r = lax.broadcasted_iota(jnp.int32, (rows, cols), 0) // row_div % 8
    c = lax.broadcasted_iota(jnp.int32, (rows, cols), 1) // col_div
    return r == c


def fn_s5_bmat(b_re, b_im, coef_re, coef_im):
    rows = b_re.shape[0]
    mask = _group_mask(rows, 8 * SC, SP, SC)
    bb_re = coef_re * b_re - coef_im * b_im
    bb_im = coef_re * b_im + coef_im * b_re
    return jnp.where(mask, jnp.tile(bb_re, (1, 8)), 0.0), jnp.where(mask, jnp.tile(bb_im, (1, 8)), 0.0)


def fn_s5_cmat(c_re, c_im):
    rows = c_re.shape[0]
    mask = _group_mask(rows, 8 * SP, SC, SP)
    return jnp.where(mask, jnp.tile(c_re, (1, 8)), 0.0), jnp.where(mask, jnp.tile(c_im, (1, 8)), 0.0)


def fn_s5_bu(u, wb_re, wb_im):
    return mm_nt(u, wb_re), mm_nt(u, wb_im)


def fn_s5_out(sr, si, u, d, wc_re, wc_im):
    y = mm_nt(sr, wc_re) - mm_nt(si, wc_im) + d * u
    return (jax.nn.gelu(y),)


def fn_merge_glu(z, mo, gate):
    yg = z[:, :PW] * jax.nn.sigmoid(z[:, PW:])
    return (jnp.concatenate([yg, mo], axis=1) * silu(gate),)


def fn_merge(prim, mo, gate):
    return (jnp.concatenate([prim, mo], axis=1) * silu(gate),)


def fn_mem_k(kv, g):
    return (jnp.concatenate([rms(kv[:, h * XHD:(h + 1) * XHD], g) for h in range(XH)], axis=1),)


def fn_mem_attn(xq, kn, v, g):
    outs = []
    for h in range(XH):
        sl = slice(h * XHD, (h + 1) * XHD)
        p = softmax_rows(mm_nt(rms(xq[:, sl], g), kn[:, sl]) * (XHD ** -0.5))
        outs.append(mm_nn(p, v[:, sl]))
    return (jnp.concatenate(outs, axis=1),)


def _half_rms(x, g):
    lo = lax.broadcasted_iota(jnp.int32, x.shape, 1) < ROPE
    x2 = x * x
    s_lo = jnp.sum(jnp.where(lo, x2, 0.0), axis=1, keepdims=True)
    s_hi = jnp.sum(jnp.where(lo, 0.0, x2), axis=1, keepdims=True)
    return x * lax.rsqrt(jnp.where(lo, s_lo, s_hi) / ROPE + EPS) * g


def _rope(x, cos2, sin_signed):
    first = lax.broadcasted_iota(jnp.int32, x.shape, 1) % ROPE < ROPE // 2
    return x * cos2 + jnp.where(first, lane_roll(x, 128 - ROPE // 2), lane_roll(x, ROPE // 2)) * sin_signed


def fn_mla_prep(q, kv, kr, cos2, sin_signed, qnn, knn, qrn, krn):
    lo = lax.broadcasted_iota(jnp.int32, kr.shape, 1) < ROPE
    kr_pad = jnp.where(lo, _rope(_half_rms(kr, krn), cos2, sin_signed), 0.0)
    qf, kf, vs = [], [], []
    for m in range(MH // 2):
        pair = _rope(_half_rms(q[:, MH * NOPE + 128 * m:MH * NOPE + 128 * (m + 1)], qrn), cos2, sin_signed)
        for h, rope_h in ((2 * m, pair), (2 * m + 1, lane_roll(pair, ROPE))):
            qf.append(jnp.concatenate([rms(q[:, NOPE * h:NOPE * (h + 1)], qnn), jnp.where(lo, rope_h, 0.0)], axis=1))
    for h in range(MH):
        kf.append(jnp.concatenate([rms(kv[:, 256 * h:256 * h + NOPE], knn), kr_pad], axis=1))
        vs.append(kv[:, 256 * h + NOPE:256 * (h + 1)])
    return jnp.stack(qf), jnp.stack(kf), jnp.stack(vs)


ATT_TQ = 256


def fn_causal_attn(q, kf, v):
    s = mm_nt(q, kf) * ((NOPE + ROPE) ** -0.5)
    q_pos = pl.program_id(1) * ATT_TQ + lax.broadcasted_iota(jnp.int32, s.shape, 0)
    k_pos = lax.broadcasted_iota(jnp.int32, s.shape, 1)
    p = softmax_rows(jnp.where(k_pos <= q_pos, s, jnp.finfo(F32).min))
    return (mm_nn(p, v),)


def loss_and_grad(y, target, tl=256):
    def kern(y_ref, t_ref, dy_ref, loss_ref):
        d = y_ref[...] - t_ref[...]
        dy_ref[...] = d / D

        @pl.when(pl.program_id(0) == 0)
        def _():
            loss_ref[...] = jnp.zeros_like(loss_ref)

        loss_ref[...] += 0.5 * jnp.sum(jnp.sum(d * d, axis=1, keepdims=True), axis=0, keepdims=True) / D

    return pl.pallas_call(kern, grid=(L // tl,), in_specs=[rspec(tl, D), rspec(tl, D)], out_specs=[rspec(tl, D), cspec((1, 1))],
                          out_shape=[SDS((L, D), F32), SDS((1, 1), F32)], name="loss", compiler_params=_cparams())(y, target)


def adamw(name, w, g, m, v):
    rows, cols = w.shape
    tr = _pick(rows, (256, 128, 64, 32, 16, 8))

    def kern(w_ref, g_ref, m_ref, v_ref, d_ref, nm_ref, nv_ref):
        gg = g_ref[...]
        nm = ADAM_B1 * m_ref[...] + (1.0 - ADAM_B1) * gg
        nv = ADAM_B2 * v_ref[...] + (1.0 - ADAM_B2) * jnp.square(gg)
        m_hat = nm / (1.0 - ADAM_B1 ** ADAM_STEP)
        v_hat = nv / (1.0 - ADAM_B2 ** ADAM_STEP)
        d_ref[...] = -ADAM_LR * (m_hat / (jnp.sqrt(v_hat) + ADAM_EPS) + ADAM_WD * w_ref[...])
        nm_ref[...] = nm
        nv_ref[...] = nv

    spec = rspec(tr, cols)
    return pl.pallas_call(kern, grid=(rows // tr,), in_specs=[spec] * 4, out_specs=[spec] * 3,
                          out_shape=[SDS((rows, cols), F32)] * 3, name=name, compiler_params=_cparams())(w, g, m, v)


def add_slabs(name, parts, out_dtype, tr=128):
    rows, cols = parts[0].shape
    tr = _pick(rows, (tr, 64, 32, 16, 8))
    n = len(parts)

    def kern(*refs):
        acc = refs[0][...].astype(F32)
        for r in refs[1:n]:
            acc = acc + r[...].astype(F32)
        refs[n][...] = acc.astype(refs[n].dtype)

    spec = rspec(tr, cols)
    return pl.pallas_call(kern, grid=(rows // tr,), in_specs=[spec] * n, out_specs=spec,
                          out_shape=SDS((rows, cols), out_dtype), name=name, compiler_params=_cparams())(*parts)


ANY = pl.BlockSpec(memory_space=pl.ANY)
MESH_ID = pl.DeviceIdType.MESH


def _place():
    x, y, c = lax.axis_index("x"), lax.axis_index("y"), lax.axis_index("c")
    return x, y, c, [(1 - x, y), (x, 1 - y), (1 - x, 1 - y)]


def all_gather_chips(name, shard):
    rows, cols = shard.shape
    half = rows // 2

    def body(x_ref, out_ref, send_sems, recv_sems, local_sem):
        x, y, c, chips = _place()
        sibling = (x, y, 1 - c)
        mine = 2 * x + y

        def slab(chip, cc):
            return out_ref.at[chip, pl.ds(cc * half, half), :]

        def copy(k, chip, cc, to, src=None):
            return pltpu.make_async_remote_copy(src_ref=slab(chip, cc) if src is None else src, dst_ref=slab(chip, cc),
                                                send_sem=send_sems.at[k], recv_sem=recv_sems.at[k], device_id=to, device_id_type=MESH_ID)

        own = pltpu.make_async_copy(x_ref, out_ref.at[mine], local_sem)
        own.start()
        first = [copy(j, mine, c, (cx, cy, c), src=x_ref.at[pl.ds(c * half, half), :]) for j, (cx, cy) in enumerate(chips)]
        for cp in first:
            cp.start()
        passed = [copy(3 + j, 2 * cx + cy, c, sibling) for j, (cx, cy) in enumerate(chips)]
        for j, (cx, cy) in enumerate(chips):
            copy(j, 2 * cx + cy, c, sibling).wait_recv()
            passed[j].start()
        for j, (cx, cy) in enumerate(chips):
            copy(3 + j, 2 * cx + cy, 1 - c, sibling).wait_recv()
        for cp in first + passed:
            cp.wait_send()
        own.wait()

    return pl.pallas_call(body, in_specs=[ANY], out_specs=ANY, out_shape=SDS((4, rows, cols), shard.dtype),
                          scratch_shapes=[pltpu.SemaphoreType.DMA((6,)), pltpu.SemaphoreType.DMA((6,)), pltpu.SemaphoreType.DMA],
                          name=name)(shard)


def pair_exchange(name, g):
    _, rows, cols = g.shape
    half = rows // 2

    def body(g_ref, own_ref, got_ref, send_sem, recv_sem, local_sem):
        x, y, c, _ = _place()
        own = pltpu.make_async_copy(g_ref.at[:, pl.ds(c * half, half), :], own_ref, local_sem)
        own.start()
        swap = pltpu.make_async_remote_copy(src_ref=g_ref.at[:, pl.ds((1 - c) * half, half), :], dst_ref=got_ref,
                                            send_sem=send_sem, recv_sem=recv_sem, device_id=(x, y, 1 - c), device_id_type=MESH_ID)
        swap.start()
        swap.wait()
        own.wait()

    return pl.pallas_call(body, in_specs=[ANY], out_specs=[ANY, ANY], out_shape=[SDS((4, half, cols), g.dtype)] * 2,
                          scratch_shapes=[pltpu.SemaphoreType.DMA, pltpu.SemaphoreType.DMA, pltpu.SemaphoreType.DMA], name=name)(g)


def chip_scatter(name, p):
    def body(p_ref, q_ref, send_sems, recv_sems, local_sem):
        x, y, c, chips = _place()
        mine = 2 * x + y
        own = pltpu.make_async_copy(p_ref.at[mine], q_ref.at[mine], local_sem)
        own.start()
        sends = [pltpu.make_async_remote_copy(src_ref=p_ref.at[2 * cx + cy], dst_ref=q_ref.at[mine], send_sem=send_sems.at[j],
                                              recv_sem=recv_sems.at[j], device_id=(cx, cy, c), device_id_type=MESH_ID)
                 for j, (cx, cy) in enumerate(chips)]
        for cp in sends:
            cp.start()
        for j, (cx, cy) in enumerate(chips):
            pltpu.make_async_remote_copy(src_ref=p_ref.at[mine], dst_ref=q_ref.at[2 * cx + cy], send_sem=send_sems.at[j],
                                         recv_sem=recv_sems.at[j], device_id=(cx, cy, c), device_id_type=MESH_ID).wait_recv()
        for cp in sends:
            cp.wait_send()
        own.wait()

    return pl.pallas_call(body, in_specs=[ANY], out_specs=ANY, out_shape=SDS(p.shape, p.dtype),
                          scratch_shapes=[pltpu.SemaphoreType.DMA((3,)), pltpu.SemaphoreType.DMA((3,)), pltpu.SemaphoreType.DMA], name=name)(p)


def pair_join(name, h):
    half, cols = h.shape

    def body(h_ref, out_ref, send_sem, recv_sem, local_sem):
        x, y, c, _ = _place()
        own = pltpu.make_async_copy(h_ref, out_ref.at[pl.ds(c * half, half), :], local_sem)
        own.start()
        give = pltpu.make_async_remote_copy(src_ref=h_ref, dst_ref=out_ref.at[pl.ds(c * half, half), :], send_sem=send_sem,
                                            recv_sem=recv_sem, device_id=(x, y, 1 - c), device_id_type=MESH_ID)
        give.start()
        give.wait_send()
        pltpu.make_async_remote_copy(src_ref=h_ref, dst_ref=out_ref.at[pl.ds((1 - c) * half, half), :], send_sem=send_sem,
                                     recv_sem=recv_sem, device_id=(x, y, 1 - c), device_id_type=MESH_ID).wait_recv()
        own.wait()

    return pl.pallas_call(body, in_specs=[ANY], out_specs=ANY, out_shape=SDS((2 * half, cols), h.dtype),
                          scratch_shapes=[pltpu.SemaphoreType.DMA, pltpu.SemaphoreType.DMA, pltpu.SemaphoreType.DMA], name=name)(h)


def reduce_scatter_chips(g):
    own, got = pair_exchange("rs_pair_exchange", g)
    n, half, cols = own.shape
    pair = add_slabs("rs_pair_add", [own.reshape(n * half, cols), got.reshape(n * half, cols)], g.dtype).reshape(n, half, cols)
    q = chip_scatter("rs_chip_scatter", pair)
    return pair_join("rs_pair_join", add_slabs("rs_chip_add", [q[s] for s in range(4)], F32))


BIG = [("w_out", (2, 512, 1024)), ("w_mem_kv", (2, 256, 1024)), ("s5_w_in", (1, 1024, 1024)), ("s5_w_glu", (1, 1536, 768)),
       ("mla_w_in", (1, 1024, 848)), ("mla_w_uq", (1, 512, 576)), ("mla_w_ukv", (1, 256, 768))]
SHARDED_SMALL = [("mla_q_lora_norm", (1, 128)), ("mla_kv_lora_norm", (1, 64))]
SMALL = [("ln_gain", (2, 1024)), ("mem_norm", (2, 1024)), ("xq_norm", (2, 128)), ("xk_norm", (2, 128)),
         ("s5_lambda_re", (1, 96, 64)), ("s5_lambda_im", (1, 96, 64)), ("s5_log_step", (1, 96)),
         ("s5_b_re", (1, 96, 64, 16)), ("s5_b_im", (1, 96, 64, 16)), ("s5_c_re", (1, 96, 16, 64)), ("s5_c_im", (1, 96, 16, 64)),
         ("s5_d", (1, 1536)), ("mla_q_nope_norm", (1, 128)), ("mla_k_nope_norm", (1, 128)), ("mla_q_rope_norm", (1, 64)),
         ("mla_k_rope_norm", (1, 64))]
WEIGHT_ORDER = ["ln_gain", "w_out", "mem_norm", "w_mem_kv", "xq_norm", "xk_norm", "s5_w_in", "s5_lambda_re", "s5_lambda_im",
                "s5_log_step", "s5_b_re", "s5_b_im", "s5_c_re", "s5_c_im", "s5_d", "s5_w_glu", "mla_w_in", "mla_q_lora_norm",
                "mla_kv_lora_norm", "mla_w_uq", "mla_w_ukv", "mla_q_nope_norm", "mla_k_nope_norm", "mla_q_rope_norm", "mla_k_rope_norm"]
LANES = 1024
BIG_ROWS = sum(math.prod(s) for _, s in BIG) // LANES
W_ROWS = 5056
SMALL_FULL = SMALL + [(n, (1, 4 * s[1])) for n, s in SHARDED_SMALL]
N_SMALL = sum(math.prod(s) for _, s in SMALL_FULL)
SMALL_ROWS = 112
G_ROWS = BIG_ROWS + SMALL_ROWS


def shards_to_full(name, arr4):
    if name in ("w_out", "w_mem_kv"):
        return arr4.transpose(1, 0, 2, 3).reshape(2, 4 * arr4.shape[2], arr4.shape[3])
    if arr4.ndim == 3:
        return arr4.reshape(-1)
    return arr4[:, 0].transpose(1, 0, 2).reshape(arr4.shape[2], 4 * arr4.shape[3])


def full_to_shards(name, full):
    if name in ("w_out", "w_mem_kv"):
        return full.reshape(2, 4, full.shape[1] // 4, full.shape[2]).transpose(1, 0, 2, 3)
    return full.reshape(full.shape[0], 4, full.shape[1] // 4).transpose(1, 0, 2)


def pack_weight_shard(w):
    parts = [w[n].astype(BF16).reshape(-1) for n, _ in BIG]
    parts += [lax.bitcast_convert_type(w[n].reshape(-1), BF16).reshape(-1) for n, _ in SHARDED_SMALL]
    flat = jnp.concatenate(parts)
    return jnp.pad(flat, (0, W_ROWS * LANES - flat.shape[0])).reshape(W_ROWS, LANES)


def unpack_weights(wall):
    flat = wall.reshape(4, -1)
    out, off = {}, 0
    for n, s in BIG:
        size = math.prod(s)
        out[n] = shards_to_full(n, flat[:, off:off + size].reshape((4,) + s))
        off += size
    for n, s in SHARDED_SMALL:
        size = 2 * math.prod(s)
        out[n] = lax.bitcast_convert_type(flat[:, off:off + size].reshape(4, s[1], 2), F32).reshape(-1)
        off += size
    return out


def mla_in_permute(w):
    o1, o2, o3, o4 = QL, QL + KVL, QL + KVL + ROPE, QL + KVL + ROPE + XQW
    return jnp.concatenate([w[:, o4:], w[:, :o1], w[:, o3:o4], w[:, o1:o2], w[:, o2:o3],
                            jnp.zeros((w.shape[0], MLA_IN_P - MLA_IN), w.dtype)], axis=1)


def mla_in_unpermute(d):
    return jnp.concatenate([d[:, 2048:2560], d[:, 3072:3328], d[:, 3328:3392], d[:, 2560:3072], d[:, :2048]], axis=1)


def uq_permute(w):
    w3 = w.reshape(w.shape[0], MH, NOPE + ROPE)
    return jnp.concatenate([w3[:, :, :NOPE].reshape(w.shape[0], MH * NOPE), w3[:, :, NOPE:].reshape(w.shape[0], MH * ROPE)], axis=1)


def uq_unpermute(d):
    dn = d[:, :MH * NOPE].reshape(d.shape[0], MH, NOPE)
    dr = d[:, MH * NOPE:].reshape(d.shape[0], MH, ROPE)
    return jnp.concatenate([dn, dr], axis=2).reshape(d.shape[0], MH * (NOPE + ROPE))


def time_permute(a):
    return a.reshape(SEG, SEG_LEN, a.shape[-1]).transpose(1, 0, 2).reshape(L, a.shape[-1])


def time_unpermute(a):
    return a.reshape(SEG_LEN, SEG, a.shape[-1]).transpose(1, 0, 2).reshape(L, a.shape[-1])


def mem_branch_fwd(tag, mem, mem_norm, w_mem_kv, xk_norm):
    mn = row_fwd(tag + "_mem_rms", fn_rms, ML, ML, [(mem, D, 0)], [mem_norm], [(D, BF16)])[0]
    kv = matmul(tag + "_mem_kv", mn, w_mem_kv, "nn")
    kn = row_fwd(tag + "_mem_knorm", fn_mem_k, ML, ML, [(kv, XQW, 0)], [xk_norm], [(XQW, F32)])[0]
    return mn, kv, kn


def mem_branch_bwd(tag, mem, mem_norm, w_mem_kv, xk_norm, mn, kv, dkn, dv):
    dk, dxk = row_bwd(tag + "_mem_knorm_bwd", fn_mem_k, ML, ML, [(kv, XQW, 0)], [xk_norm], [(dkn, XQW, 0)], [True], [True])
    dkv = jnp.concatenate([dk, dv], axis=1)
    dmn = matmul(tag + "_mem_kv_dx", dkv, w_mem_kv, "nt")
    dw = matmul(tag + "_mem_kv_dw", mn, dkv, "tn")
    dmem_norm = row_bwd(tag + "_mem_rms_bwd", fn_rms, ML, ML, [(mem, D, 0)], [mem_norm], [(dmn, D, 0)], [False], [True])[0]
    return dw, dmem_norm, dxk


def mem_attn_fwd(tag, proj, cb, kn, kv, xq_norm):
    return row_fwd(tag + "_mem_attn", fn_mem_attn, L, 256, [(proj, XQW, cb)], [kn, kv[:, XQW:], xq_norm], [(XQW, F32)])[0]


def mem_attn_bwd(tag, proj, cb, kn, kv, xq_norm, dmo):
    return row_bwd(tag + "_mem_attn_bwd", fn_mem_attn, L, 256, [(proj, XQW, cb)], [kn, kv[:, XQW:], xq_norm], [(dmo, XQW, 0)],
                   [True], [True, True, True])


def device_step(x, mem, positions, target, w, small):
    g = {}
    ln, mem_norm, xq_norm, xk_norm = small["ln_gain"], small["mem_norm"], small["xq_norm"], small["xk_norm"]

    lre, lim = small["s5_lambda_re"][0], small["s5_lambda_im"][0]
    ls = small["s5_log_step"].reshape(SG, 1)
    one = pl.BlockSpec((SG, SP), lambda i: (0, 0))
    col = pl.BlockSpec((SG, 1), lambda i: (0, 0))
    disc_ins = [(lre, one), (lim, one), (ls, col)]
    a_re, a_im, coef_re, coef_im = stage("s5_disc", fn_s5_disc, (1,), disc_ins, [(SDS((SG, SP), F32), one)] * 4)
    b_re, b_im = small["s5_b_re"].reshape(SN, SC), small["s5_b_im"].reshape(SN, SC)
    c_re, c_im = small["s5_c_re"].reshape(PW, SP), small["s5_c_im"].reshape(PW, SP)
    bmat_rows = [(b_re, SC, 0), (b_im, SC, 0), (coef_re.reshape(SN, 1), 1, 0), (coef_im.reshape(SN, 1), 1, 0)]
    wb_re, wb_im = row_fwd("s5_bmat", fn_s5_bmat, SN, 512, bmat_rows, [], [(128, F32)] * 2)
    cmat_rows = [(c_re, SP, 0), (c_im, SP, 0)]
    wc_re, wc_im = row_fwd("s5_cmat", fn_s5_cmat, PW, 128, cmat_rows, [], [(512, F32)] * 2)
    a_re_v, a_im_v = a_re.reshape(1, SN), a_im.reshape(1, SN)
    s5_d = small["s5_d"]

    xp = time_permute(x)
    h0 = row_fwd("l0_rms", fn_rms, L, 256, [(xp, D, 0)], [ln[0:1]], [(D, BF16)])[0]
    proj0 = matmul("l0_in", h0, w["s5_w_in"], "nn")
    ts = 512
    u_spec = pl.BlockSpec((ts, 128), lambda j, i: (i, j))
    wb_spec = pl.BlockSpec((512, 128), lambda j, i: (j, 0))
    s_spec = pl.BlockSpec((ts, 512), lambda j, i: (i, j))
    bu_ins = [(proj0, u_spec), (wb_re, wb_spec), (wb_im, wb_spec)]
    bu_re, bu_im = stage("s5_bu", fn_s5_bu, (12, L // ts), bu_ins, [(SDS((L, SN), F32), s_spec)] * 2)
    s_re, s_im = s5_scan_fwd(bu_re, bu_im, a_re_v, a_im_v)
    d_spec = pl.BlockSpec((1, 128), lambda j, i: (0, j))
    wc_spec = pl.BlockSpec((128, 512), lambda j, i: (j, 0))
    out_ins = [(s_re, s_spec), (s_im, s_spec), (proj0, u_spec), (s5_d, d_spec), (wc_re, wc_spec), (wc_im, wc_spec)]
    g0 = stage("s5_out", fn_s5_out, (12, L // ts), out_ins, [(SDS((L, PW), BF16), u_spec)])[0]
    z0 = matmul("l0_glu", g0, w["s5_w_glu"], "nn")
    mn0, kv0, kn0 = mem_branch_fwd("l0", mem, mem_norm[0:1], w["w_mem_kv"][0], xk_norm[0:1])
    mo0 = mem_attn_fwd("l0", proj0, 3, kn0, kv0, xq_norm[0:1])
    o0 = row_fwd("l0_merge", fn_merge_glu, L, 256, [(z0, 2 * PW, 0), (mo0, XQW, 0), (proj0, BW, 1)], [], [(BW, BF16)])[0]
    x1p = matmul("l0_out", o0, w["w_out"][0], "nn", add=xp)
    x1 = time_unpermute(x1p)

    h1 = row_fwd("l1_rms", fn_rms, L, 256, [(x1, D, 0)], [ln[1:2]], [(D, BF16)])[0]
    proj1 = matmul("l1_in", h1, w["mla_w_in"], "nn")
    qln, kvln = small["mla_q_lora_norm"].reshape(1, QL), small["mla_kv_lora_norm"].reshape(1, KVL)
    cqn = row_fwd("l1_q_lora_rms", fn_rms, L, 256, [(proj1, QL, 4)], [qln], [(QL, BF16)])[0]
    ckvn = row_fwd("l1_kv_lora_rms", fn_rms, L, 256, [(proj1, KVL, 12)], [kvln], [(KVL, BF16)])[0]
    q = matmul("l1_uq", cqn, w["mla_w_uq"], "nn")
    kv = matmul("l1_ukv", ckvn, w["mla_w_ukv"], "nn")
    inv_freq = ROPE_THETA ** (-jnp.arange(ROPE // 2, dtype=F32) / (ROPE // 2))
    ang = positions.astype(F32)[:, None] * inv_freq
    cos2 = jnp.tile(jnp.cos(ang), (1, 4))
    sin_signed = jnp.tile(jnp.concatenate([-jnp.sin(ang), jnp.sin(ang)], axis=1), (1, 2))
    qnn, knn = small["mla_q_nope_norm"], small["mla_k_nope_norm"]
    qrn, krn = jnp.tile(small["mla_q_rope_norm"], (1, 2)), jnp.tile(small["mla_k_rope_norm"], (1, 2))
    tp = 256
    prep_ins = [(q, rspec(tp, MH * (NOPE + ROPE))), (kv, rspec(tp, MH * 256)), (proj1, rspec(tp, 128, 26)),
                (cos2, rspec(tp, 128)), (sin_signed, rspec(tp, 128))] + [(a, cspec((1, 128))) for a in (qnn, knn, qrn, krn)]
    hq_spec = pl.BlockSpec((MH, tp, 256), lambda i: (0, i, 0))
    hv_spec = pl.BlockSpec((MH, tp, 128), lambda i: (0, i, 0))
    qf, kf, vh = stage("l1_mla_prep", fn_mla_prep, (L // tp,), prep_ins,
                       [(SDS((MH, L, 256), BF16), hq_spec), (SDS((MH, L, 256), BF16), hq_spec), (SDS((MH, L, 128), BF16), hv_spec)])
    aq_spec = pl.BlockSpec((None, ATT_TQ, 256), lambda h, i: (h, i, 0))
    ak_spec = pl.BlockSpec((None, L, 256), lambda h, i: (h, 0, 0))
    av_spec = pl.BlockSpec((None, L, 128), lambda h, i: (h, 0, 0))
    ao_spec = pl.BlockSpec((ATT_TQ, 128), lambda h, i: (i, h))
    att_ins = [(qf, aq_spec), (kf, ak_spec), (vh, av_spec)]
    attn = stage("l1_attn", fn_causal_attn, (MH, L // ATT_TQ), att_ins, [(SDS((L, PW), F32), ao_spec)])[0]
    mn1, kv1, kn1 = mem_branch_fwd("l1", mem, mem_norm[1:2], w["w_mem_kv"][1], xk_norm[1:2])
    mo1 = mem_attn_fwd("l1", proj1, 5, kn1, kv1, xq_norm[1:2])
    o1 = row_fwd("l1_merge", fn_merge, L, 256, [(attn, PW, 0), (mo1, XQW, 0), (proj1, BW, 0)], [], [(BW, BF16)])[0]
    x2 = matmul("l1_out", o1, w["w_out"][1], "nn", add=x1)
    dx2, loss = loss_and_grad(x2, target)

    do1 = matmul("l1_out_dx", dx2, w["w_out"][1], "nt")
    dw_out1 = matmul("l1_out_dw", o1, dx2, "tn")
    dattn, dmo1, dgate1 = row_bwd("l1_merge_bwd", fn_merge, L, 256, [(attn, PW, 0), (mo1, XQW, 0), (proj1, BW, 0)], [],
                                  [(do1, BW, 0)], [True, True, True], [])
    dxq1, dkn1, dv1, dxqn1 = mem_attn_bwd("l1", proj1, 5, kn1, kv1, xq_norm[1:2], dmo1)
    dw_mkv1, dmem_norm1, dxk1 = mem_branch_bwd("l1", mem, mem_norm[1:2], w["w_mem_kv"][1], xk_norm[1:2], mn1, kv1, dkn1, dv1)
    att_diffs = [("row", SDS((MH, L, 256), F32), aq_spec), ("acc", (1,)), ("acc", (1,))]
    dqf, dkf, dvh = stage_bwd("l1_attn_bwd", fn_causal_attn, (MH, L // ATT_TQ), att_ins, [(dattn, ao_spec)], att_diffs)
    prep_diffs = [("row", SDS((L, MH * (NOPE + ROPE)), F32), rspec(tp, MH * (NOPE + ROPE))), ("row", SDS((L, MH * 256), F32), rspec(tp, MH * 256)),
                  ("row", SDS((L, 128), F32), rspec(tp, 128)), None, None] + [("acc", (0,))] * 4
    dq, dkv, dkr, dqnn, dknn, dqrn, dkrn = stage_bwd("l1_mla_prep_bwd", fn_mla_prep, (L // tp,), prep_ins,
                                                     [(dqf, hq_spec), (dkf, hq_spec), (dvh, hv_spec)], prep_diffs)
    dcqn = matmul("l1_uq_dx", dq, w["mla_w_uq"], "nt")
    dw_uq = matmul("l1_uq_dw", cqn, dq, "tn")
    dckvn = matmul("l1_ukv_dx", dkv, w["mla_w_ukv"], "nt")
    dw_ukv = matmul("l1_ukv_dw", ckvn, dkv, "tn")
    dcq, dqln = row_bwd("l1_q_lora_rms_bwd", fn_rms, L, 256, [(proj1, QL, 4)], [qln], [(dcqn, QL, 0)], [True], [True])
    dckv, dkvln = row_bwd("l1_kv_lora_rms_bwd", fn_rms, L, 256, [(proj1, KVL, 12)], [kvln], [(dckvn, KVL, 0)], [True], [True])
    dproj1 = jnp.concatenate([dgate1, dcq, dxq1, dckv, dkr], axis=1)
    dh1 = matmul("l1_in_dx", dproj1, w["mla_w_in"], "nt")
    dw_in1 = matmul("l1_in_dw", h1, dproj1, "tn")
    dx1_n, dln1 = row_bwd("l1_rms_bwd", fn_rms, L, 256, [(x1, D, 0)], [ln[1:2]], [(dh1, D, 0)], [True], [True])
    dx1p = time_permute(dx2 + dx1_n)

    do0 = matmul("l0_out_dx", dx1p, w["w_out"][0], "nt")
    dw_out0 = matmul("l0_out_dw", o0, dx1p, "tn")
    dz0, dmo0, dgate0 = row_bwd("l0_merge_bwd", fn_merge_glu, L, 256, [(z0, 2 * PW, 0), (mo0, XQW, 0), (proj0, BW, 1)], [],
                                [(do0, BW, 0)], [True, True, True], [])
    dxq0, dkn0, dv0, dxqn0 = mem_attn_bwd("l0", proj0, 3, kn0, kv0, xq_norm[0:1], dmo0)
    dw_mkv0, dmem_norm0, dxk0 = mem_branch_bwd("l0", mem, mem_norm[0:1], w["w_mem_kv"][0], xk_norm[0:1], mn0, kv0, dkn0, dv0)
    dg0 = matmul("l0_glu_dx", dz0, w["s5_w_glu"], "nt")
    dw_glu = matmul("l0_glu_dw", g0, dz0, "tn")
    out_diffs = [("row", SDS((L, SN), F32), s_spec), ("row", SDS((L, SN), F32), s_spec), ("row", SDS((L, PW), F32), u_spec),
                 ("acc", (1,)), ("acc", (1,)), ("acc", (1,))]
    gs_re, gs_im, du_a, dd, dwc_re, dwc_im = stage_bwd("s5_out_bwd", fn_s5_out, (12, L // ts), out_ins, [(dg0, u_spec)], out_diffs)
    dbu_re, dbu_im, da_re, da_im = s5_scan_bwd(gs_re, gs_im, s_re, s_im, a_re_v, a_im_v)
    bu_diffs = [("row", SDS((L, PW), F32), u_spec), ("acc", (1,)), ("acc", (1,))]
    du_b, dwb_re, dwb_im = stage_bwd("s5_bu_bwd", fn_s5_bu, (12, L // ts), bu_ins, [(dbu_re, s_spec), (dbu_im, s_spec)], bu_diffs)
    dproj0 = jnp.concatenate([du_a + du_b, dxq0, dgate0], axis=1)
    dh0 = matmul("l0_in_dx", dproj0, w["s5_w_in"], "nt")
    dw_in0 = matmul("l0_in_dw", h0, dproj0, "tn")
    dxp_n, dln0 = row_bwd("l0_rms_bwd", fn_rms, L, 256, [(xp, D, 0)], [ln[0:1]], [(dh0, D, 0)], [True], [True])
    grad_x = time_unpermute(dx1p + dxp_n)

    db_re, db_im, dcoef_re, dcoef_im = row_bwd("s5_bmat_bwd", fn_s5_bmat, SN, 512, bmat_rows, [], [(dwb_re, 128, 0), (dwb_im, 128, 0)],
                                               [True] * 4, [])
    dc_re, dc_im = row_bwd("s5_cmat_bwd", fn_s5_cmat, PW, 128, cmat_rows, [], [(dwc_re, 512, 0), (dwc_im, 512, 0)], [True] * 2, [])
    disc_cts = [(da_re.reshape(SG, SP), one), (da_im.reshape(SG, SP), one), (dcoef_re.reshape(SG, SP), one), (dcoef_im.reshape(SG, SP), one)]
    dlre, dlim, dls = stage_bwd("s5_disc_bwd", fn_s5_disc, (1,), disc_ins, disc_cts, [("acc", (0,))] * 3)

    g["ln_gain"] = jnp.concatenate([dln0, dln1], axis=0)
    g["w_out"] = jnp.stack([dw_out0, dw_out1])
    g["mem_norm"] = jnp.concatenate([dmem_norm0, dmem_norm1], axis=0)
    g["w_mem_kv"] = jnp.stack([dw_mkv0, dw_mkv1])
    g["xq_norm"] = jnp.concatenate([dxqn0, dxqn1], axis=0)
    g["xk_norm"] = jnp.concatenate([dxk0, dxk1], axis=0)
    g["s5_w_in"] = dw_in0
    g["s5_lambda_re"], g["s5_lambda_im"], g["s5_log_step"] = dlre, dlim, dls
    g["s5_b_re"], g["s5_b_im"], g["s5_c_re"], g["s5_c_im"] = db_re, db_im, dc_re, dc_im
    g["s5_d"] = dd
    g["s5_w_glu"] = dw_glu
    g["mla_w_in"] = mla_in_unpermute(dw_in1)
    g["mla_q_lora_norm"], g["mla_kv_lora_norm"] = dqln, dkvln
    g["mla_w_uq"] = uq_unpermute(dw_uq)
    g["mla_w_ukv"] = dw_ukv
    g["mla_q_nope_norm"], g["mla_k_nope_norm"] = dqnn, dknn
    g["mla_q_rope_norm"] = dqrn[:, :ROPE] + dqrn[:, ROPE:]
    g["mla_k_rope_norm"] = dkrn[:, :ROPE] + dkrn[:, ROPE:]
    return loss, grad_x, g


def kernel(x, mem, positions, ln_gain, w_out, mem_norm, w_mem_kv, xq_norm, xk_norm, s5_w_in, s5_lambda_re, s5_lambda_im, s5_log_step, s5_b_re, s5_b_im, s5_c_re, s5_c_im, s5_d, s5_w_glu, mla_w_in, mla_q_lora_norm, mla_kv_lora_norm, mla_w_uq, mla_w_ukv, mla_q_nope_norm, mla_k_nope_norm, mla_q_rope_norm, mla_k_rope_norm, loss_target, m_ln_gain, m_w_out, m_mem_norm, m_w_mem_kv, m_xq_norm, m_xk_norm, m_s5_w_in, m_s5_lambda_re, m_s5_lambda_im, m_s5_log_step, m_s5_b_re, m_s5_b_im, m_s5_c_re, m_s5_c_im, m_s5_d, m_s5_w_glu, m_mla_w_in, m_mla_q_lora_norm, m_mla_kv_lora_norm, m_mla_w_uq, m_mla_w_ukv, m_mla_q_nope_norm, m_mla_k_nope_norm, m_mla_q_rope_norm, m_mla_k_rope_norm, v_ln_gain, v_w_out, v_mem_norm, v_w_mem_kv, v_xq_norm, v_xk_norm, v_s5_w_in, v_s5_lambda_re, v_s5_lambda_im, v_s5_log_step, v_s5_b_re, v_s5_b_im, v_s5_c_re, v_s5_c_im, v_s5_d, v_s5_w_glu, v_mla_w_in, v_mla_q_lora_norm, v_mla_kv_lora_norm, v_mla_w_uq, v_mla_w_ukv, v_mla_q_nope_norm, v_mla_k_nope_norm, v_mla_q_rope_norm, v_mla_k_rope_norm):
    args = dict(locals())
    wts = {n: args[n] for n in WEIGHT_ORDER}
    mom = {n: args["m_" + n] for n in WEIGHT_ORDER}
    var = {n: args["v_" + n] for n in WEIGHT_ORDER}

    full = unpack_weights(all_gather_chips("gather_weights", pack_weight_shard(wts)))
    full["mla_w_in"] = mla_in_permute(full["mla_w_in"])
    full["mla_w_uq"] = uq_permute(full["mla_w_uq"])
    small = {n: wts[n] for n, _ in SMALL}
    small["mla_q_lora_norm"], small["mla_kv_lora_norm"] = full["mla_q_lora_norm"], full["mla_kv_lora_norm"]

    loss, grad_x, g = device_step(x[0], mem[0], positions[0], loss_target[0], full, small)
    loss = lax.psum(loss[0, 0], MESH_AXES)

    big_parts = jnp.concatenate([full_to_shards(n, g[n]).reshape(4, -1) for n, _ in BIG], axis=1)
    small_flat = jnp.concatenate([g[n].reshape(-1) for n, _ in SMALL_FULL])
    small_parts = jnp.pad(small_flat, (0, 4 * SMALL_ROWS * LANES - N_SMALL)).reshape(4, SMALL_ROWS * LANES)
    gbuf = jnp.concatenate([big_parts, small_parts], axis=1).astype(BF16).reshape(4, G_ROWS, LANES)
    red = reduce_scatter_chips(gbuf)
    small_all = all_gather_chips("gather_small_grads", red[BIG_ROWS:]).reshape(-1)[:N_SMALL]

    grads, off = {}, 0
    flat_big = red[:BIG_ROWS].reshape(-1)
    for n, s in BIG:
        grads[n] = flat_big[off:off + math.prod(s)].reshape(s)
        off += math.prod(s)
    off = 0
    shard = 2 * lax.axis_index("x") + lax.axis_index("y")
    for n, s in SMALL_FULL:
        grads[n] = small_all[off:off + math.prod(s)].reshape(s)
        off += math.prod(s)
    for n, s in SHARDED_SMALL:
        grads[n] = lax.dynamic_slice(grads[n], (0, shard * s[1]), s)

    delta, new_m, new_v = {}, {}, {}
    for n, s in BIG:
        two_d = (s[0] * s[1], s[2])
        res = adamw("adamw_" + n, wts[n].reshape(two_d), grads[n].reshape(two_d), mom[n].reshape(two_d), var[n].reshape(two_d))
        delta[n], new_m[n], new_v[n] = (r.reshape(s) for r in res)
    small_names = [n for n, _ in SMALL] + [n for n, _ in SHARDED_SMALL]
    n_own = sum(wts[n].size for n in small_names)
    rows_own = -(-n_own // (256 * 128)) * 256

    def pack_small(d):
        flat = jnp.concatenate([d[n].reshape(-1) for n in small_names])
        return jnp.pad(flat, (0, rows_own * 128 - n_own), constant_values=1.0).reshape(rows_own, 128)

    res = adamw("adamw_small", pack_small(wts), pack_small(grads), pack_small(mom), pack_small(var))
    off = 0
    for n in small_names:
        size = wts[n].size
        delta[n], new_m[n], new_v[n] = (r.reshape(-1)[off:off + size].reshape(wts[n].shape) for r in res)
        off += size

    return (loss, grad_x[None], *[grads[n] for n in WEIGHT_ORDER], *[delta[n] for n in WEIGHT_ORDER],
            *[new_m[n] for n in WEIGHT_ORDER], *[new_v[n] for n in WEIGHT_ORDER])
```

```python
import functools
import math

import jax
import jax.numpy as jnp
from jax import lax
from jax.experimental import pallas as pl
from jax.experimental.pallas import tpu as pltpu

F32, BF16 = jnp.float32, jnp.bfloat16
SDS = jax.ShapeDtypeStruct

D = 1024
L = 2048
ML = 256
BW = 2 * D
XQW = BW // 4
PW = BW - XQW
XH, XHD = 4, 128
SG, SC, SP = 96, 16, 64
SN = SG * SP
NOPE, ROPE, VD = 128, 64, 128
MH = 12
QL, KVL = 512, 256
EPS = 1e-6
ROPE_THETA = 10000.0
MLA_IN = QL + KVL + ROPE + XQW + BW
MLA_IN_P = 3456
ADAM_LR, ADAM_B1, ADAM_B2, ADAM_EPS, ADAM_WD, ADAM_STEP = 0.001, 0.9, 0.999, 1e-08, 0.01, 10

VMEM_LIMIT = 48 * 2**20
SEG = 8
SEG_LEN = L // SEG
MESH_AXES = ("x", "y", "c")


def _cparams():
    return pltpu.CompilerParams(vmem_limit_bytes=VMEM_LIMIT)


def _dg(a, b, ca, cb):
    return lax.dot_general(a.astype(BF16), b.astype(BF16), (((ca,), (cb,)), ((), ())), preferred_element_type=F32)


@jax.custom_vjp
def mm_nn(a, b):
    return _dg(a, b, 1, 0)


mm_nn.defvjp(lambda a, b: (_dg(a, b, 1, 0), (a, b)), lambda res, g: (_dg(g, res[1], 1, 1), _dg(res[0], g, 0, 0)))


@jax.custom_vjp
def mm_nt(a, b):
    return _dg(a, b, 1, 1)


mm_nt.defvjp(lambda a, b: (_dg(a, b, 1, 1), (a, b)), lambda res, g: (_dg(g, res[1], 1, 0), _dg(g, res[0], 0, 0)))


@functools.partial(jax.custom_vjp, nondiff_argnums=(1,))
def lane_roll(x, shift):
    return pltpu.roll(x, shift, 1)


lane_roll.defvjp(lambda x, shift: (pltpu.roll(x, shift, 1), None),
                 lambda shift, _, g: (pltpu.roll(g, (128 - shift) % 128, 1),))


def rms(x, g):
    return x * lax.rsqrt(jnp.mean(x * x, axis=-1, keepdims=True) + EPS) * g


def softmax_rows(s):
    m = lax.stop_gradient(jnp.max(s, axis=-1, keepdims=True))
    e = jnp.exp(s - m)
    return e / jnp.sum(e, axis=-1, keepdims=True)


def silu(x):
    return x * jax.nn.sigmoid(x)


def stage(name, fn, grid, ins, outs):
    n_in = len(ins)

    def kern(*refs):
        res = fn(*[r[...] for r in refs[:n_in]])
        for r, v in zip(refs[n_in:], res):
            r[...] = v.astype(r.dtype)

    return pl.pallas_call(kern, grid=grid, in_specs=[s for _, s in ins], out_specs=[s for _, s in outs],
                          out_shape=[sd for sd, _ in outs], name=name, compiler_params=_cparams())(*[a for a, _ in ins])


def stage_bwd(name, fn, grid, ins, cts, diffs):
    n_in, n_ct = len(ins), len(cts)
    didx = [i for i, d in enumerate(diffs) if d is not None]

    def kern(*refs):
        vals = [r[...] for r in refs[:n_in]]

        def f(*dv):
            full = list(vals)
            for i, v in zip(didx, dv):
                full[i] = v
            return fn(*full)

        _, vjp = jax.vjp(f, *[vals[i].astype(F32) for i in didx])
        gs = vjp(tuple(c[...].astype(F32) for c in refs[n_in:n_in + n_ct]))
        for o_ref, i, g in zip(refs[n_in + n_ct:], didx, gs):
            if diffs[i][0] == "row":
                o_ref[...] = g.astype(o_ref.dtype)
            else:
                first = functools.reduce(jnp.logical_and, [pl.program_id(ax) == 0 for ax in diffs[i][1]])

                @pl.when(first)
                def _():
                    o_ref[...] = g

                @pl.when(jnp.logical_not(first))
                def _():
                    o_ref[...] += g

    out_shape, out_specs = [], []
    for i in didx:
        if diffs[i][0] == "row":
            out_shape.append(diffs[i][1])
            out_specs.append(diffs[i][2])
        else:
            out_shape.append(SDS(ins[i][0].shape, F32))
            out_specs.append(ins[i][1])
    return pl.pallas_call(kern, grid=grid, in_specs=[s for _, s in ins] + [s for _, s in cts], out_specs=out_specs,
                          out_shape=out_shape, name=name, compiler_params=_cparams())(*[a for a, _ in ins], *[a for a, _ in cts])


def rspec(tl, w, cb=0):
    return pl.BlockSpec((tl, w), lambda i: (i, cb))


def cspec(shape):
    return pl.BlockSpec(shape, lambda i: (0,) * len(shape))


def row_fwd(name, fn, rows, tl, row_ins, consts, outs):
    ins = [(a, rspec(tl, w, cb)) for a, w, cb in row_ins] + [(a, cspec(a.shape)) for a in consts]
    return stage(name, fn, (rows // tl,), ins, [(SDS((rows, w), dt), rspec(tl, w)) for w, dt in outs])


def row_bwd(name, fn, rows, tl, row_ins, consts, cts, row_diff, const_diff, row_dtype=F32):
    ins = [(a, rspec(tl, w, cb)) for a, w, cb in row_ins] + [(a, cspec(a.shape)) for a in consts]
    diffs = [("row", SDS((rows, w), row_dtype), rspec(tl, w)) if d else None for (a, w, cb), d in zip(row_ins, row_diff)]
    diffs += [("acc", (0,)) if d else None for d in const_diff]
    return stage_bwd(name, fn, (rows // tl,), ins, [(a, rspec(tl, w, cb)) for a, w, cb in cts], diffs)


def _pick(dim, prefs):
    for p in prefs:
        if dim % p == 0:
            return p
    return dim


def matmul(name, a, b, mode, out_dtype=F32, add=None):
    if mode == "tn":
        k_dim, m = a.shape
    else:
        m, k_dim = a.shape
    n = b.shape[0] if mode == "nt" else b.shape[1]
    tm = _pick(m, (512, 256, 128))
    tn = _pick(n, (1024, 768, 512, 384, 256, 128))
    tk = _pick(k_dim, (512, 384, 256, 128))
    nk = k_dim // tk
    a_spec = pl.BlockSpec((tk, tm), lambda i, j, k: (k, i)) if mode == "tn" else pl.BlockSpec((tm, tk), lambda i, j, k: (i, k))
    b_spec = pl.BlockSpec((tn, tk), lambda i, j, k: (j, k)) if mode == "nt" else pl.BlockSpec((tk, tn), lambda i, j, k: (k, j))
    o_spec = pl.BlockSpec((tm, tn), lambda i, j, k: (i, j))
    ca, cb = {"nn": (1, 0), "nt": (1, 1), "tn": (0, 0)}[mode]

    def kern(*refs):
        a_ref, b_ref = refs[0], refs[1]
        o_ref, acc = refs[-2], refs[-1]
        k = pl.program_id(2)

        @pl.when(k == 0)
        def _():
            acc[...] = jnp.zeros_like(acc)

        acc[...] += _dg(a_ref[...], b_ref[...], ca, cb)

        @pl.when(k == nk - 1)
        def _():
            r = acc[...]
            if add is not None:
                r = r + refs[2][...]
            o_ref[...] = r.astype(o_ref.dtype)

    ins, specs = [a, b], [a_spec, b_spec]
    if add is not None:
        ins.append(add)
        specs.append(o_spec)
    return pl.pallas_call(kern, grid=(m // tm, n // tn, nk), in_specs=specs, out_specs=o_spec,
                          out_shape=SDS((m, n), out_dtype), scratch_shapes=[pltpu.VMEM((tm, tn), F32)],
                          name=name, compiler_params=_cparams())(*ins)


SCAN_LANES = 256


def _cmul(ar, ai, br, bi):
    return ar * br - ai * bi, ar * bi + ai * br


def _sub_shift(x, down):
    row = lax.broadcasted_iota(jnp.int32, x.shape, 0)
    if down:
        return jnp.where(row == 0, 0.0, pltpu.roll(x, 1, 0))
    return jnp.where(row == SEG - 1, 0.0, pltpu.roll(x, SEG - 1, 0))


def _pow_seg_len(ar, ai):
    for _ in range(int(math.log2(SEG_LEN))):
        ar, ai = _cmul(ar, ai, ar, ai)
    return ar, ai


def s5_scan_fwd(bu_re, bu_im, a_re, a_im):
    lanes = SCAN_LANES

    def kern(bur, bui, ar_ref, ai_ref, sr, si):
        ar = jnp.broadcast_to(ar_ref[...], (SEG, lanes))
        ai = jnp.broadcast_to(ai_ref[...], (SEG, lanes))
        zero = jnp.zeros((SEG, lanes), F32)

        def local(i, carry):
            rows = pl.ds(pl.multiple_of(i * SEG, SEG), SEG)
            mr, mi = _cmul(ar, ai, carry[0], carry[1])
            nr, ni = mr + bur[rows, :], mi + bui[rows, :]
            sr[rows, :] = nr
            si[rows, :] = ni
            return nr, ni

        fr, fi = lax.fori_loop(0, SEG_LEN, local, (zero, zero))
        pr, pi = _pow_seg_len(ar, ai)
        ir, ii = zero, zero
        for _ in range(SEG - 1):
            mr, mi = _cmul(pr, pi, ir, ii)
            ir, ii = _sub_shift(mr + fr, True), _sub_shift(mi + fi, True)

        def carry_in(i, pw):
            rows = pl.ds(pl.multiple_of(i * SEG, SEG), SEG)
            cr, ci = _cmul(pw[0], pw[1], ir, ii)
            sr[rows, :] += cr
            si[rows, :] += ci
            return _cmul(pw[0], pw[1], ar, ai)

        lax.fori_loop(0, SEG_LEN, carry_in, (ar, ai))

    blk = pl.BlockSpec((L, lanes), lambda j: (0, j))
    vec = pl.BlockSpec((1, lanes), lambda j: (0, j))
    return pl.pallas_call(kern, grid=(SN // lanes,), in_specs=[blk, blk, vec, vec], out_specs=[blk, blk],
                          out_shape=[SDS((L, SN), F32)] * 2, name="s5_scan_fwd", compiler_params=_cparams())(bu_re, bu_im, a_re, a_im)


def s5_scan_bwd(g_re, g_im, s_re, s_im, a_re, a_im):
    lanes = SCAN_LANES

    def kern(gr, gi, sr, si, ar_ref, ai_ref, lr, li, dar, dai):
        ar = jnp.broadcast_to(ar_ref[...], (SEG, lanes))
        ai = -jnp.broadcast_to(ai_ref[...], (SEG, lanes))
        zero = jnp.zeros((SEG, lanes), F32)

        def local(k, carry):
            i = SEG_LEN - 1 - k
            rows = pl.ds(pl.multiple_of(i * SEG, SEG), SEG)
            mr, mi = _cmul(ar, ai, carry[0], carry[1])
            nr, ni = mr + gr[rows, :], mi + gi[rows, :]
            lr[rows, :] = nr
            li[rows, :] = ni
            return nr, ni

        fr, fi = lax.fori_loop(0, SEG_LEN, local, (zero, zero))
        pr, pi = _pow_seg_len(ar, ai)
        ir, ii = zero, zero
        for _ in range(SEG - 1):
            mr, mi = _cmul(pr, pi, ir, ii)
            ir, ii = _sub_shift(mr + fr, False), _sub_shift(mi + fi, False)

        def fix(rows, pw):
            cr, ci = _cmul(pw[0], pw[1], ir, ii)
            tr, ti = lr[rows, :] + cr, li[rows, :] + ci
            lr[rows, :] = tr
            li[rows, :] = ti
            return tr, ti

        def grad_a(tr, ti, spr, spi, acc):
            return acc[0] + tr * spr + ti * spi, acc[1] + ti * spr - tr * spi

        def carry_in(k, c):
            i = SEG_LEN - 1 - k
            rows = pl.ds(pl.multiple_of(i * SEG, SEG), SEG)
            prev = pl.ds(pl.multiple_of((i - 1) * SEG, SEG), SEG)
            tr, ti = fix(rows, (c[0], c[1]))
            acc = grad_a(tr, ti, sr[prev, :], si[prev, :], (c[2], c[3]))
            nr, ni = _cmul(c[0], c[1], ar, ai)
            return nr, ni, acc[0], acc[1]

        pwr, pwi, accr, acci = lax.fori_loop(0, SEG_LEN - 1, carry_in, (ar, ai, zero, zero))
        tr, ti = fix(pl.ds(0, SEG), (pwr, pwi))
        last = pl.ds((SEG_LEN - 1) * SEG, SEG)
        accr, acci = grad_a(tr, ti, _sub_shift(sr[last, :], True), _sub_shift(si[last, :], True), (accr, acci))
        dar[...] = jnp.sum(accr, axis=0, keepdims=True)
        dai[...] = jnp.sum(acci, axis=0, keepdims=True)

    blk = pl.BlockSpec((L, lanes), lambda j: (0, j))
    vec = pl.BlockSpec((1, lanes), lambda j: (0, j))
    return pl.pallas_call(kern, grid=(SN // lanes,), in_specs=[blk, blk, blk, blk, vec, vec], out_specs=[blk, blk, vec, vec],
                          out_shape=[SDS((L, SN), F32)] * 2 + [SDS((1, SN), F32)] * 2, name="s5_scan_bwd",
                          compiler_params=_cparams())(g_re, g_im, s_re, s_im, a_re, a_im)


def fn_rms(x, g):
    return (rms(x, g),)


def fn_s5_disc(lre, lim, ls):
    step = jnp.exp(ls)
    e = jnp.exp(lre * step)
    a_re, a_im = e * jnp.cos(lim * step), e * jnp.sin(lim * step)
    den = lre * lre + lim * lim
    nr, ni = a_re - 1.0, a_im
    return a_re, a_im, (nr * lre + ni * lim) / den, (ni * lre - nr * lim) / den


def _group_mask(rows, cols, row_div, col_div):
    r = lax.broadcasted_iota(jnp.int32, (rows, cols), 0) // row_div % 8
    c = lax.broadcasted_iota(jnp.int32, (rows, cols), 1) // col_div
    return r == c


def fn_s5_bmat(b_re, b_im, coef_re, coef_im):
    rows = b_re.shape[0]
    mask = _group_mask(rows, 8 * SC, SP, SC)
    bb_re = coef_re * b_re - coef_im * b_im
    bb_im = coef_re * b_im + coef_im * b_re
    return jnp.where(mask, jnp.tile(bb_re, (1, 8)), 0.0), jnp.where(mask, jnp.tile(bb_im, (1, 8)), 0.0)


def fn_s5_cmat(c_re, c_im):
    rows = c_re.shape[0]
    mask = _group_mask(rows, 8 * SP, SC, SP)
    return jnp.where(mask, jnp.tile(c_re, (1, 8)), 0.0), jnp.where(mask, jnp.tile(c_im, (1, 8)), 0.0)


def fn_s5_bu(u, wb_re, wb_im):
    return mm_nt(u, wb_re), mm_nt(u, wb_im)


def fn_s5_out(sr, si, u, d, wc_re, wc_im):
    y = mm_nt(sr, wc_re) - mm_nt(si, wc_im) + d * u
    return (jax.nn.gelu(y),)


def fn_merge_glu(z, mo, gate):
    yg = z[:, :PW] * jax.nn.sigmoid(z[:, PW:])
    return (jnp.concatenate([yg, mo], axis=1) * silu(gate),)


def fn_merge(prim, mo, gate):
    return (jnp.concatenate([prim, mo], axis=1) * silu(gate),)


def fn_mem_k(kv, g):
    return (jnp.concatenate([rms(kv[:, h * XHD:(h + 1) * XHD], g) for h in range(XH)], axis=1),)


def fn_mem_attn(xq, kn, v, g):
    outs = []
    for h in range(XH):
        sl = slice(h * XHD, (h + 1) * XHD)
        p = softmax_rows(mm_nt(rms(xq[:, sl], g), kn[:, sl]) * (XHD ** -0.5))
        outs.append(mm_nn(p, v[:, sl]))
    return (jnp.concatenate(outs, axis=1),)


def _half_rms(x, g):
    lo = lax.broadcasted_iota(jnp.int32, x.shape, 1) < ROPE
    x2 = x * x
    s_lo = jnp.sum(jnp.where(lo, x2, 0.0), axis=1, keepdims=True)
    s_hi = jnp.sum(jnp.where(lo, 0.0, x2), axis=1, keepdims=True)
    return x * lax.rsqrt(jnp.where(lo, s_lo, s_hi) / ROPE + EPS) * g


def _rope(x, cos2, sin_signed):
    first = lax.broadcasted_iota(jnp.int32, x.shape, 1) % ROPE < ROPE // 2
    return x * cos2 + jnp.where(first, lane_roll(x, 128 - ROPE // 2), lane_roll(x, ROPE // 2)) * sin_signed


def fn_mla_prep(q, kv, kr, cos2, sin_signed, qnn, knn, qrn, krn):
    lo = lax.broadcasted_iota(jnp.int32, kr.shape, 1) < ROPE
    kr_pad = jnp.where(lo, _rope(_half_rms(kr, krn), cos2, sin_signed), 0.0)
    qf, kf, vs = [], [], []
    for m in range(MH // 2):
        pair = _rope(_half_rms(q[:, MH * NOPE + 128 * m:MH * NOPE + 128 * (m + 1)], qrn), cos2, sin_signed)
        for h, rope_h in ((2 * m, pair), (2 * m + 1, lane_roll(pair, ROPE))):
            qf.append(jnp.concatenate([rms(q[:, NOPE * h:NOPE * (h + 1)], qnn), jnp.where(lo, rope_h, 0.0)], axis=1))
    for h in range(MH):
        kf.append(jnp.concatenate([rms(kv[:, 256 * h:256 * h + NOPE], knn), kr_pad], axis=1))
        vs.append(kv[:, 256 * h + NOPE:256 * (h + 1)])
    return jnp.stack(qf), jnp.stack(kf), jnp.stack(vs)


ATT_TQ = 256


def fn_causal_attn(q, kf, v):
    s = mm_nt(q, kf) * ((NOPE + ROPE) ** -0.5)
    q_pos = pl.program_id(1) * ATT_TQ + lax.broadcasted_iota(jnp.int32, s.shape, 0)
    k_pos = lax.broadcasted_iota(jnp.int32, s.shape, 1)
    p = softmax_rows(jnp.where(k_pos <= q_pos, s, jnp.finfo(F32).min))
    return (mm_nn(p, v),)


def loss_and_grad(y, target, tl=256):
    def kern(y_ref, t_ref, dy_ref, loss_ref):
        d = y_ref[...] - t_ref[...]
        dy_ref[...] = d / D

        @pl.when(pl.program_id(0) == 0)
        def _():
            loss_ref[...] = jnp.zeros_like(loss_ref)

        loss_ref[...] += 0.5 * jnp.sum(jnp.sum(d * d, axis=1, keepdims=True), axis=0, keepdims=True) / D

    return pl.pallas_call(kern, grid=(L // tl,), in_specs=[rspec(tl, D), rspec(tl, D)], out_specs=[rspec(tl, D), cspec((1, 1))],
                          out_shape=[SDS((L, D), F32), SDS((1, 1), F32)], name="loss", compiler_params=_cparams())(y, target)


def adamw(name, w, g, m, v):
    rows, cols = w.shape
    tr = _pick(rows, (256, 128, 64, 32, 16, 8))

    def kern(w_ref, g_ref, m_ref, v_ref, d_ref, nm_ref, nv_ref):
        gg = g_ref[...]
        nm = ADAM_B1 * m_ref[...] + (1.0 - ADAM_B1) * gg
        nv = ADAM_B2 * v_ref[...] + (1.0 - ADAM_B2) * jnp.square(gg)
        m_hat = nm / (1.0 - ADAM_B1 ** ADAM_STEP)
        v_hat = nv / (1.0 - ADAM_B2 ** ADAM_STEP)
        d_ref[...] = -ADAM_LR * (m_hat / (jnp.sqrt(v_hat) + ADAM_EPS) + ADAM_WD * w_ref[...])
        nm_ref[...] = nm
        nv_ref[...] = nv

    spec = rspec(tr, cols)
    return pl.pallas_call(kern, grid=(rows // tr,), in_specs=[spec] * 4, out_specs=[spec] * 3,
                          out_shape=[SDS((rows, cols), F32)] * 3, name=name, compiler_params=_cparams())(w, g, m, v)


def _row_tile(rows, cap=512, unit=16):
    return max(t for t in range(unit, cap + 1, unit) if rows % t == 0)


def add_slots(name, terms, out_dtype):
    _, rows, cols = terms[0][0].shape
    tr = _row_tile(rows)
    n = len(terms)

    def kern(*refs):
        acc = refs[0][...].astype(F32)
        for r in refs[1:n]:
            acc = acc + r[...].astype(F32)
        refs[n][...] = acc.astype(refs[n].dtype)

    specs = [pl.BlockSpec((None, tr, cols), lambda i, s=s: (s, i, 0)) for _, s in terms]
    return pl.pallas_call(kern, grid=(rows // tr,), in_specs=specs, out_specs=rspec(tr, cols),
                          out_shape=SDS((rows, cols), out_dtype), name=name, compiler_params=_cparams())(*[a for a, _ in terms])


ANY = pl.BlockSpec(memory_space=pl.ANY)
MESH_ID = pl.DeviceIdType.MESH
ICI_STREAMS = 4
D2D_STREAMS = 8


def _place():
    x, y, c = lax.axis_index("x"), lax.axis_index("y"), lax.axis_index("c")
    return x, y, c, [(1 - x, y), (x, 1 - y), (1 - x, 1 - y)]


def _row_chunks(rows, n, dtype):
    unit = 32 // jnp.dtype(dtype).itemsize
    base, extra = divmod(rows // unit, n)
    out, start = [], 0
    for k in range(n):
        size = (base + (k < extra)) * unit
        if size:
            out.append((start, size))
            start += size
    assert start == rows, (rows, unit)
    return out


def _dma_sems(n):
    return [pltpu.SemaphoreType.DMA((n,)), pltpu.SemaphoreType.DMA((n,))]


def all_gather_chips(name, shard):
    rows, cols = shard.shape
    half = rows // 2
    chunks = _row_chunks(half, ICI_STREAMS, shard.dtype)
    n = len(chunks)

    def body(x_ref, out_ref, send_sems, recv_sems, local_sem):
        x, y, c, chips = _place()
        sibling = (x, y, 1 - c)
        mine = 2 * x + y

        def copy(sem, chip, cc, k, to, from_input=False):
            rows_k = pl.ds(cc * half + chunks[k][0], chunks[k][1])
            dst = out_ref.at[chip, rows_k, :]
            return pltpu.make_async_remote_copy(src_ref=x_ref.at[rows_k, :] if from_input else dst, dst_ref=dst,
                                                send_sem=send_sems.at[sem], recv_sem=recv_sems.at[sem], device_id=to, device_id_type=MESH_ID)

        own = pltpu.make_async_copy(x_ref, out_ref.at[mine], local_sem)
        own.start()
        order = [(k, j, 2 * cx + cy, (cx, cy, c)) for k in range(n) for j, (cx, cy) in enumerate(chips)]
        first = [copy(j * n + k, mine, c, k, to, from_input=True) for k, j, _, to in order]
        for cp in first:
            cp.start()
        passed = []
        for k, j, chip, _ in order:
            copy(j * n + k, chip, c, k, sibling).wait_recv()
            passed.append(copy((3 + j) * n + k, chip, c, k, sibling))
            passed[-1].start()
        for k, j, chip, _ in order:
            copy((3 + j) * n + k, chip, 1 - c, k, sibling).wait_recv()
        for cp in first + passed:
            cp.wait_send()
        own.wait()

    return pl.pallas_call(body, in_specs=[ANY], out_specs=ANY, out_shape=SDS((4, rows, cols), shard.dtype),
                          scratch_shapes=_dma_sems(6 * n) + [pltpu.SemaphoreType.DMA], name=name)(shard)


def pair_exchange(name, g):
    slots, rows, cols = g.shape
    half = rows // 2
    pieces = [(s, st, sz) for s in range(slots) for st, sz in _row_chunks(half, D2D_STREAMS // slots, g.dtype)]

    def body(g_ref, own_ref, got_ref, send_sems, recv_sems, local_sems):
        x, y, c, _ = _place()
        keep = [pltpu.make_async_copy(g_ref.at[s, pl.ds(c * half, half), :], own_ref.at[s], local_sems.at[s]) for s in range(slots)]
        for cp in keep:
            cp.start()
        swaps = [pltpu.make_async_remote_copy(src_ref=g_ref.at[s, pl.ds((1 - c) * half + st, sz), :], dst_ref=got_ref.at[s, pl.ds(st, sz), :],
                                              send_sem=send_sems.at[k], recv_sem=recv_sems.at[k], device_id=(x, y, 1 - c), device_id_type=MESH_ID)
                 for k, (s, st, sz) in enumerate(pieces)]
        for cp in swaps:
            cp.start()
        for cp in swaps:
            cp.wait()
        for cp in keep:
            cp.wait()

    return pl.pallas_call(body, in_specs=[ANY], out_specs=[ANY, ANY], out_shape=[SDS((slots, half, cols), g.dtype)] * 2,
                          scratch_shapes=_dma_sems(len(pieces)) + [pltpu.SemaphoreType.DMA((slots,))], name=name)(g)


def chip_scatter(name, p):
    chunks = _row_chunks(p.shape[1], ICI_STREAMS, p.dtype)
    n = len(chunks)

    def body(p_ref, q_ref, send_sems, recv_sems, local_sem):
        x, y, c, chips = _place()
        mine = 2 * x + y
        own = pltpu.make_async_copy(p_ref.at[mine], q_ref.at[mine], local_sem)
        own.start()

        def copy(j, k, src_slot, dst_slot, to):
            rows_k = pl.ds(chunks[k][0], chunks[k][1])
            return pltpu.make_async_remote_copy(src_ref=p_ref.at[src_slot, rows_k, :], dst_ref=q_ref.at[dst_slot, rows_k, :],
                                                send_sem=send_sems.at[j * n + k], recv_sem=recv_sems.at[j * n + k], device_id=to,
                                                device_id_type=MESH_ID)

        order = [(k, j, 2 * cx + cy, (cx, cy, c)) for k in range(n) for j, (cx, cy) in enumerate(chips)]
        sends = [copy(j, k, chip, mine, to) for k, j, chip, to in order]
        for cp in sends:
            cp.start()
        for k, j, chip, to in order:
            copy(j, k, mine, chip, to).wait_recv()
        for cp in sends:
            cp.wait_send()
        own.wait()

    return pl.pallas_call(body, in_specs=[ANY], out_specs=ANY, out_shape=SDS(p.shape, p.dtype),
                          scratch_shapes=_dma_sems(3 * n) + [pltpu.SemaphoreType.DMA], name=name)(p)


def pair_join(name, h):
    half, cols = h.shape
    chunks = _row_chunks(half, D2D_STREAMS, h.dtype)

    def body(h_ref, out_ref, send_sems, recv_sems, local_sem):
        x, y, c, _ = _place()
        own = pltpu.make_async_copy(h_ref, out_ref.at[pl.ds(c * half, half), :], local_sem)
        own.start()

        def copy(k, cc):
            st, sz = chunks[k]
            return pltpu.make_async_remote_copy(src_ref=h_ref.at[pl.ds(st, sz), :], dst_ref=out_ref.at[pl.ds(cc * half + st, sz), :],
                                                send_sem=send_sems.at[k], recv_sem=recv_sems.at[k], device_id=(x, y, 1 - c),
                                                device_id_type=MESH_ID)

        gives = [copy(k, c) for k in range(len(chunks))]
        for cp in gives:
            cp.start()
        for k in range(len(chunks)):
            copy(k, 1 - c).wait_recv()
        for cp in gives:
            cp.wait_send()
        own.wait()

    return pl.pallas_call(body, in_specs=[ANY], out_specs=ANY, out_shape=SDS((2 * half, cols), h.dtype),
                          scratch_shapes=_dma_sems(len(chunks)) + [pltpu.SemaphoreType.DMA], name=name)(h)


def reduce_scatter_chips(g):
    own, got = pair_exchange("rs_pair_exchange", g)
    n, half, cols = own.shape
    pair = add_slots("rs_pair_add", [(own.reshape(1, n * half, cols), 0), (got.reshape(1, n * half, cols), 0)], g.dtype)
    q = chip_scatter("rs_chip_scatter", pair.reshape(n, half, cols))
    return pair_join("rs_pair_join", add_slots("rs_chip_add", [(q, s) for s in range(n)], F32))


BIG = [("w_out", (2, 512, 1024)), ("w_mem_kv", (2, 256, 1024)), ("s5_w_in", (1, 1024, 1024)), ("s5_w_glu", (1, 1536, 768)),
       ("mla_w_in", (1, 1024, 848)), ("mla_w_uq", (1, 512, 576)), ("mla_w_ukv", (1, 256, 768))]
SHARDED_SMALL = [("mla_q_lora_norm", (1, 128)), ("mla_kv_lora_norm", (1, 64))]
SMALL = [("ln_gain", (2, 1024)), ("mem_norm", (2, 1024)), ("xq_norm", (2, 128)), ("xk_norm", (2, 128)),
         ("s5_lambda_re", (1, 96, 64)), ("s5_lambda_im", (1, 96, 64)), ("s5_log_step", (1, 96)),
         ("s5_b_re", (1, 96, 64, 16)), ("s5_b_im", (1, 96, 64, 16)), ("s5_c_re", (1, 96, 16, 64)), ("s5_c_im", (1, 96, 16, 64)),
         ("s5_d", (1, 1536)), ("mla_q_nope_norm", (1, 128)), ("mla_k_nope_norm", (1, 128)), ("mla_q_rope_norm", (1, 64)),
         ("mla_k_rope_norm", (1, 64))]
WEIGHT_ORDER = ["ln_gain", "w_out", "mem_norm", "w_mem_kv", "xq_norm", "xk_norm", "s5_w_in", "s5_lambda_re", "s5_lambda_im",
                "s5_log_step", "s5_b_re", "s5_b_im", "s5_c_re", "s5_c_im", "s5_d", "s5_w_glu", "mla_w_in", "mla_q_lora_norm",
                "mla_kv_lora_norm", "mla_w_uq", "mla_w_ukv", "mla_q_nope_norm", "mla_k_nope_norm", "mla_q_rope_norm", "mla_k_rope_norm"]
LANES = 1024
BIG_ROWS = sum(math.prod(s) for _, s in BIG) // LANES
W_ROWS = 5056
SMALL_FULL = SMALL + [(n, (1, 4 * s[1])) for n, s in SHARDED_SMALL]
N_SMALL = sum(math.prod(s) for _, s in SMALL_FULL)
SMALL_ROWS = 112
G_ROWS = BIG_ROWS + SMALL_ROWS


def shards_to_full(name, arr4):
    if name in ("w_out", "w_mem_kv"):
        return arr4.transpose(1, 0, 2, 3).reshape(2, 4 * arr4.shape[2], arr4.shape[3])
    if arr4.ndim == 3:
        return arr4.reshape(-1)
    return arr4[:, 0].transpose(1, 0, 2).reshape(arr4.shape[2], 4 * arr4.shape[3])


def full_to_shards(name, full):
    if name in ("w_out", "w_mem_kv"):
        return full.reshape(2, 4, full.shape[1] // 4, full.shape[2]).transpose(1, 0, 2, 3)
    return full.reshape(full.shape[0], 4, full.shape[1] // 4).transpose(1, 0, 2)


def pack_weight_shard(w):
    parts = [w[n].astype(BF16).reshape(-1) for n, _ in BIG]
    parts += [lax.bitcast_convert_type(w[n].reshape(-1), BF16).reshape(-1) for n, _ in SHARDED_SMALL]
    flat = jnp.concatenate(parts)
    return jnp.pad(flat, (0, W_ROWS * LANES - flat.shape[0])).reshape(W_ROWS, LANES)


def unpack_weights(wall):
    flat = wall.reshape(4, -1)
    out, off = {}, 0
    for n, s in BIG:
        size = math.prod(s)
        out[n] = shards_to_full(n, flat[:, off:off + size].reshape((4,) + s))
        off += size
    for n, s in SHARDED_SMALL:
        size = 2 * math.prod(s)
        out[n] = lax.bitcast_convert_type(flat[:, off:off + size].reshape(4, s[1], 2), F32).reshape(-1)
        off += size
    return out


def mla_in_permute(w):
    o1, o2, o3, o4 = QL, QL + KVL, QL + KVL + ROPE, QL + KVL + ROPE + XQW
    return jnp.concatenate([w[:, o4:], w[:, :o1], w[:, o3:o4], w[:, o1:o2], w[:, o2:o3],
                            jnp.zeros((w.shape[0], MLA_IN_P - MLA_IN), w.dtype)], axis=1)


def mla_in_unpermute(d):
    return jnp.concatenate([d[:, 2048:2560], d[:, 3072:3328], d[:, 3328:3392], d[:, 2560:3072], d[:, :2048]], axis=1)


def uq_permute(w):
    w3 = w.reshape(w.shape[0], MH, NOPE + ROPE)
    return jnp.concatenate([w3[:, :, :NOPE].reshape(w.shape[0], MH * NOPE), w3[:, :, NOPE:].reshape(w.shape[0], MH * ROPE)], axis=1)


def uq_unpermute(d):
    dn = d[:, :MH * NOPE].reshape(d.shape[0], MH, NOPE)
    dr = d[:, MH * NOPE:].reshape(d.shape[0], MH, ROPE)
    return jnp.concatenate([dn, dr], axis=2).reshape(d.shape[0], MH * (NOPE + ROPE))


def time_permute(a):
    return a.reshape(SEG, SEG_LEN, a.shape[-1]).transpose(1, 0, 2).reshape(L, a.shape[-1])


def time_unpermute(a):
    return a.reshape(SEG_LEN, SEG, a.shape[-1]).transpose(1, 0, 2).reshape(L, a.shape[-1])


def mem_branch_fwd(tag, mem, mem_norm, w_mem_kv, xk_norm):
    mn = row_fwd(tag + "_mem_rms", fn_rms, ML, ML, [(mem, D, 0)], [mem_norm], [(D, BF16)])[0]
    kv = matmul(tag + "_mem_kv", mn, w_mem_kv, "nn")
    kn = row_fwd(tag + "_mem_knorm", fn_mem_k, ML, ML, [(kv, XQW, 0)], [xk_norm], [(XQW, F32)])[0]
    return mn, kv, kn


def mem_branch_bwd(tag, mem, mem_norm, w_mem_kv, xk_norm, mn, kv, dkn, dv):
    dk, dxk = row_bwd(tag + "_mem_knorm_bwd", fn_mem_k, ML, ML, [(kv, XQW, 0)], [xk_norm], [(dkn, XQW, 0)], [True], [True])
    dkv = jnp.concatenate([dk, dv], axis=1)
    dmn = matmul(tag + "_mem_kv_dx", dkv, w_mem_kv, "nt")
    dw = matmul(tag + "_mem_kv_dw", mn, dkv, "tn")
    dmem_norm = row_bwd(tag + "_mem_rms_bwd", fn_rms, ML, ML, [(mem, D, 0)], [mem_norm], [(dmn, D, 0)], [False], [True])[0]
    return dw, dmem_norm, dxk


def mem_attn_fwd(tag, proj, cb, kn, kv, xq_norm):
    return row_fwd(tag + "_mem_attn", fn_mem_attn, L, 256, [(proj, XQW, cb)], [kn, kv[:, XQW:], xq_norm], [(XQW, F32)])[0]


def mem_attn_bwd(tag, proj, cb, kn, kv, xq_norm, dmo):
    return row_bwd(tag + "_mem_attn_bwd", fn_mem_attn, L, 256, [(proj, XQW, cb)], [kn, kv[:, XQW:], xq_norm], [(dmo, XQW, 0)],
                   [True], [True, True, True])


def device_step(x, mem, positions, target, w, small):
    g = {}
    ln, mem_norm, xq_norm, xk_norm = small["ln_gain"], small["mem_norm"], small["xq_norm"], small["xk_norm"]

    lre, lim = small["s5_lambda_re"][0], small["s5_lambda_im"][0]
    ls = small["s5_log_step"].reshape(SG, 1)
    one = pl.BlockSpec((SG, SP), lambda i: (0, 0))
    col = pl.BlockSpec((SG, 1), lambda i: (0, 0))
    disc_ins = [(lre, one), (lim, one), (ls, col)]
    a_re, a_im, coef_re, coef_im = stage("s5_disc", fn_s5_disc, (1,), disc_ins, [(SDS((SG, SP), F32), one)] * 4)
    b_re, b_im = small["s5_b_re"].reshape(SN, SC), small["s5_b_im"].reshape(SN, SC)
    c_re, c_im = small["s5_c_re"].reshape(PW, SP), small["s5_c_im"].reshape(PW, SP)
    bmat_rows = [(b_re, SC, 0), (b_im, SC, 0), (coef_re.reshape(SN, 1), 1, 0), (coef_im.reshape(SN, 1), 1, 0)]
    wb_re, wb_im = row_fwd("s5_bmat", fn_s5_bmat, SN, 512, bmat_rows, [], [(128, F32)] * 2)
    cmat_rows = [(c_re, SP, 0), (c_im, SP, 0)]
    wc_re, wc_im = row_fwd("s5_cmat", fn_s5_cmat, PW, 128, cmat_rows, [], [(512, F32)] * 2)
    a_re_v, a_im_v = a_re.reshape(1, SN), a_im.reshape(1, SN)
    s5_d = small["s5_d"]

    xp = time_permute(x)
    h0 = row_fwd("l0_rms", fn_rms, L, 256, [(xp, D, 0)], [ln[0:1]], [(D, BF16)])[0]
    proj0 = matmul("l0_in", h0, w["s5_w_in"], "nn")
    ts = 512
    u_spec = pl.BlockSpec((ts, 128), lambda j, i: (i, j))
    wb_spec = pl.BlockSpec((512, 128), lambda j, i: (j, 0))
    s_spec = pl.BlockSpec((ts, 512), lambda j, i: (i, j))
    bu_ins = [(proj0, u_spec), (wb_re, wb_spec), (wb_im, wb_spec)]
    bu_re, bu_im = stage("s5_bu", fn_s5_bu, (12, L // ts), bu_ins, [(SDS((L, SN), F32), s_spec)] * 2)
    s_re, s_im = s5_scan_fwd(bu_re, bu_im, a_re_v, a_im_v)
    d_spec = pl.BlockSpec((1, 128), lambda j, i: (0, j))
    wc_spec = pl.BlockSpec((128, 512), lambda j, i: (j, 0))
    out_ins = [(s_re, s_spec), (s_im, s_spec), (proj0, u_spec), (s5_d, d_spec), (wc_re, wc_spec), (wc_im, wc_spec)]
    g0 = stage("s5_out", fn_s5_out, (12, L // ts), out_ins, [(SDS((L, PW), BF16), u_spec)])[0]
    z0 = matmul("l0_glu", g0, w["s5_w_glu"], "nn")
    mn0, kv0, kn0 = mem_branch_fwd("l0", mem, mem_norm[0:1], w["w_mem_kv"][0], xk_norm[0:1])
    mo0 = mem_attn_fwd("l0", proj0, 3, kn0, kv0, xq_norm[0:1])
    o0 = row_fwd("l0_merge", fn_merge_glu, L, 256, [(z0, 2 * PW, 0), (mo0, XQW, 0), (proj0, BW, 1)], [], [(BW, BF16)])[0]
    x1p = matmul("l0_out", o0, w["w_out"][0], "nn", add=xp)
    x1 = time_unpermute(x1p)

    h1 = row_fwd("l1_rms", fn_rms, L, 256, [(x1, D, 0)], [ln[1:2]], [(D, BF16)])[0]
    proj1 = matmul("l1_in", h1, w["mla_w_in"], "nn")
    qln, kvln = small["mla_q_lora_norm"].reshape(1, QL), small["mla_kv_lora_norm"].reshape(1, KVL)
    cqn = row_fwd("l1_q_lora_rms", fn_rms, L, 256, [(proj1, QL, 4)], [qln], [(QL, BF16)])[0]
    ckvn = row_fwd("l1_kv_lora_rms", fn_rms, L, 256, [(proj1, KVL, 12)], [kvln], [(KVL, BF16)])[0]
    q = matmul("l1_uq", cqn, w["mla_w_uq"], "nn")
    kv = matmul("l1_ukv", ckvn, w["mla_w_ukv"], "nn")
    inv_freq = ROPE_THETA ** (-jnp.arange(ROPE // 2, dtype=F32) / (ROPE // 2))
    ang = positions.astype(F32)[:, None] * inv_freq
    cos2 = jnp.tile(jnp.cos(ang), (1, 4))
    sin_signed = jnp.tile(jnp.concatenate([-jnp.sin(ang), jnp.sin(ang)], axis=1), (1, 2))
    qnn, knn = small["mla_q_nope_norm"], small["mla_k_nope_norm"]
    qrn, krn = jnp.tile(small["mla_q_rope_norm"], (1, 2)), jnp.tile(small["mla_k_rope_norm"], (1, 2))
    tp = 256
    prep_ins = [(q, rspec(tp, MH * (NOPE + ROPE))), (kv, rspec(tp, MH * 256)), (proj1, rspec(tp, 128, 26)),
                (cos2, rspec(tp, 128)), (sin_signed, rspec(tp, 128))] + [(a, cspec((1, 128))) for a in (qnn, knn, qrn, krn)]
    hq_spec = pl.BlockSpec((MH, tp, 256), lambda i: (0, i, 0))
    hv_spec = pl.BlockSpec((MH, tp, 128), lambda i: (0, i, 0))
    qf, kf, vh = stage("l1_mla_prep", fn_mla_prep, (L // tp,), prep_ins,
                       [(SDS((MH, L, 256), BF16), hq_spec), (SDS((MH, L, 256), BF16), hq_spec), (SDS((MH, L, 128), BF16), hv_spec)])
    aq_spec = pl.BlockSpec((None, ATT_TQ, 256), lambda h, i: (h, i, 0))
    ak_spec = pl.BlockSpec((None, L, 256), lambda h, i: (h, 0, 0))
    av_spec = pl.BlockSpec((None, L, 128), lambda h, i: (h, 0, 0))
    ao_spec = pl.BlockSpec((ATT_TQ, 128), lambda h, i: (i, h))
    att_ins = [(qf, aq_spec), (kf, ak_spec), (vh, av_spec)]
    attn = stage("l1_attn", fn_causal_attn, (MH, L // ATT_TQ), att_ins, [(SDS((L, PW), F32), ao_spec)])[0]
    mn1, kv1, kn1 = mem_branch_fwd("l1", mem, mem_norm[1:2], w["w_mem_kv"][1], xk_norm[1:2])
    mo1 = mem_attn_fwd("l1", proj1, 5, kn1, kv1, xq_norm[1:2])
    o1 = row_fwd("l1_merge", fn_merge, L, 256, [(attn, PW, 0), (mo1, XQW, 0), (proj1, BW, 0)], [], [(BW, BF16)])[0]
    x2 = matmul("l1_out", o1, w["w_out"][1], "nn", add=x1)
    dx2, loss = loss_and_grad(x2, target)

    do1 = matmul("l1_out_dx", dx2, w["w_out"][1], "nt")
    dw_out1 = matmul("l1_out_dw", o1, dx2, "tn")
    dattn, dmo1, dgate1 = row_bwd("l1_merge_bwd", fn_merge, L, 256, [(attn, PW, 0), (mo1, XQW, 0), (proj1, BW, 0)], [],
                                  [(do1, BW, 0)], [True, True, True], [])
    dxq1, dkn1, dv1, dxqn1 = mem_attn_bwd("l1", proj1, 5, kn1, kv1, xq_norm[1:2], dmo1)
    dw_mkv1, dmem_norm1, dxk1 = mem_branch_bwd("l1", mem, mem_norm[1:2], w["w_mem_kv"][1], xk_norm[1:2], mn1, kv1, dkn1, dv1)
    att_diffs = [("row", SDS((MH, L, 256), F32), aq_spec), ("acc", (1,)), ("acc", (1,))]
    dqf, dkf, dvh = stage_bwd("l1_attn_bwd", fn_causal_attn, (MH, L // ATT_TQ), att_ins, [(dattn, ao_spec)], att_diffs)
    prep_diffs = [("row", SDS((L, MH * (NOPE + ROPE)), F32), rspec(tp, MH * (NOPE + ROPE))), ("row", SDS((L, MH * 256), F32), rspec(tp, MH * 256)),
                  ("row", SDS((L, 128), F32), rspec(tp, 128)), None, None] + [("acc", (0,))] * 4
    dq, dkv, dkr, dqnn, dknn, dqrn, dkrn = stage_bwd("l1_mla_prep_bwd", fn_mla_prep, (L // tp,), prep_ins,
                                                     [(dqf, hq_spec), (dkf, hq_spec), (dvh, hv_spec)], prep_diffs)
    dcqn = matmul("l1_uq_dx", dq, w["mla_w_uq"], "nt")
    dw_uq = matmul("l1_uq_dw", cqn, dq, "tn")
    dckvn = matmul("l1_ukv_dx", dkv, w["mla_w_ukv"], "nt")
    dw_ukv = matmul("l1_ukv_dw", ckvn, dkv, "tn")
    dcq, dqln = row_bwd("l1_q_lora_rms_bwd", fn_rms, L, 256, [(proj1, QL, 4)], [qln], [(dcqn, QL, 0)], [True], [True])
    dckv, dkvln = row_bwd("l1_kv_lora_rms_bwd", fn_rms, L, 256, [(proj1, KVL, 12)], [kvln], [(dckvn, KVL, 0)], [True], [True])
    dproj1 = jnp.concatenate([dgate1, dcq, dxq1, dckv, dkr], axis=1)
    dh1 = matmul("l1_in_dx", dproj1, w["mla_w_in"], "nt")
    dw_in1 = matmul("l1_in_dw", h1, dproj1, "tn")
    dx1_n, dln1 = row_bwd("l1_rms_bwd", fn_rms, L, 256, [(x1, D, 0)], [ln[1:2]], [(dh1, D, 0)], [True], [True])
    dx1p = time_permute(dx2 + dx1_n)

    do0 = matmul("l0_out_dx", dx1p, w["w_out"][0], "nt")
    dw_out0 = matmul("l0_out_dw", o0, dx1p, "tn")
    dz0, dmo0, dgate0 = row_bwd("l0_merge_bwd", fn_merge_glu, L, 256, [(z0, 2 * PW, 0), (mo0, XQW, 0), (proj0, BW, 1)], [],
                                [(do0, BW, 0)], [True, True, True], [])
    dxq0, dkn0, dv0, dxqn0 = mem_attn_bwd("l0", proj0, 3, kn0, kv0, xq_norm[0:1], dmo0)
    dw_mkv0, dmem_norm0, dxk0 = mem_branch_bwd("l0", mem, mem_norm[0:1], w["w_mem_kv"][0], xk_norm[0:1], mn0, kv0, dkn0, dv0)
    dg0 = matmul("l0_glu_dx", dz0, w["s5_w_glu"], "nt")
    dw_glu = matmul("l0_glu_dw", g0, dz0, "tn")
    out_diffs = [("row", SDS((L, SN), F32), s_spec), ("row", SDS((L, SN), F32), s_spec), ("row", SDS((L, PW), F32), u_spec),
                 ("acc", (1,)), ("acc", (1,)), ("acc", (1,))]
    gs_re, gs_im, du_a, dd, dwc_re, dwc_im = stage_bwd("s5_out_bwd", fn_s5_out, (12, L // ts), out_ins, [(dg0, u_spec)], out_diffs)
    dbu_re, dbu_im, da_re, da_im = s5_scan_bwd(gs_re, gs_im, s_re, s_im, a_re_v, a_im_v)
    bu_diffs = [("row", SDS((L, PW), F32), u_spec), ("acc", (1,)), ("acc", (1,))]
    du_b, dwb_re, dwb_im = stage_bwd("s5_bu_bwd", fn_s5_bu, (12, L // ts), bu_ins, [(dbu_re, s_spec), (dbu_im, s_spec)], bu_diffs)
    dproj0 = jnp.concatenate([du_a + du_b, dxq0, dgate0], axis=1)
    dh0 = matmul("l0_in_dx", dproj0, w["s5_w_in"], "nt")
    dw_in0 = matmul("l0_in_dw", h0, dproj0, "tn")
    dxp_n, dln0 = row_bwd("l0_rms_bwd", fn_rms, L, 256, [(xp, D, 0)], [ln[0:1]], [(dh0, D, 0)], [True], [True])
    grad_x = time_unpermute(dx1p + dxp_n)

    db_re, db_im, dcoef_re, dcoef_im = row_bwd("s5_bmat_bwd", fn_s5_bmat, SN, 512, bmat_rows, [], [(dwb_re, 128, 0), (dwb_im, 128, 0)],
                                               [True] * 4, [])
    dc_re, dc_im = row_bwd("s5_cmat_bwd", fn_s5_cmat, PW, 128, cmat_rows, [], [(dwc_re, 512, 0), (dwc_im, 512, 0)], [True] * 2, [])
    disc_cts = [(da_re.reshape(SG, SP), one), (da_im.reshape(SG, SP), one), (dcoef_re.reshape(SG, SP), one), (dcoef_im.reshape(SG, SP), one)]
    dlre, dlim, dls = stage_bwd("s5_disc_bwd", fn_s5_disc, (1,), disc_ins, disc_cts, [("acc", (0,))] * 3)

    g["ln_gain"] = jnp.concatenate([dln0, dln1], axis=0)
    g["w_out"] = jnp.stack([dw_out0, dw_out1])
    g["mem_norm"] = jnp.concatenate([dmem_norm0, dmem_norm1], axis=0)
    g["w_mem_kv"] = jnp.stack([dw_mkv0, dw_mkv1])
    g["xq_norm"] = jnp.concatenate([dxqn0, dxqn1], axis=0)
    g["xk_norm"] = jnp.concatenate([dxk0, dxk1], axis=0)
    g["s5_w_in"] = dw_in0
    g["s5_lambda_re"], g["s5_lambda_im"], g["s5_log_step"] = dlre, dlim, dls
    g["s5_b_re"], g["s5_b_im"], g["s5_c_re"], g["s5_c_im"] = db_re, db_im, dc_re, dc_im
    g["s5_d"] = dd
    g["s5_w_glu"] = dw_glu
    g["mla_w_in"] = mla_in_unpermute(dw_in1)
    g["mla_q_lora_norm"], g["mla_kv_lora_norm"] = dqln, dkvln
    g["mla_w_uq"] = uq_unpermute(dw_uq)
    g["mla_w_ukv"] = dw_ukv
    g["mla_q_nope_norm"], g["mla_k_nope_norm"] = dqnn, dknn
    g["mla_q_rope_norm"] = dqrn[:, :ROPE] + dqrn[:, ROPE:]
    g["mla_k_rope_norm"] = dkrn[:, :ROPE] + dkrn[:, ROPE:]
    return loss, grad_x, g


def kernel(x, mem, positions, ln_gain, w_out, mem_norm, w_mem_kv, xq_norm, xk_norm, s5_w_in, s5_lambda_re, s5_lambda_im, s5_log_step, s5_b_re, s5_b_im, s5_c_re, s5_c_im, s5_d, s5_w_glu, mla_w_in, mla_q_lora_norm, mla_kv_lora_norm, mla_w_uq, mla_w_ukv, mla_q_nope_norm, mla_k_nope_norm, mla_q_rope_norm, mla_k_rope_norm, loss_target, m_ln_gain, m_w_out, m_mem_norm, m_w_mem_kv, m_xq_norm, m_xk_norm, m_s5_w_in, m_s5_lambda_re, m_s5_lambda_im, m_s5_log_step, m_s5_b_re, m_s5_b_im, m_s5_c_re, m_s5_c_im, m_s5_d, m_s5_w_glu, m_mla_w_in, m_mla_q_lora_norm, m_mla_kv_lora_norm, m_mla_w_uq, m_mla_w_ukv, m_mla_q_nope_norm, m_mla_k_nope_norm, m_mla_q_rope_norm, m_mla_k_rope_norm, v_ln_gain, v_w_out, v_mem_norm, v_w_mem_kv, v_xq_norm, v_xk_norm, v_s5_w_in, v_s5_lambda_re, v_s5_lambda_im, v_s5_log_step, v_s5_b_re, v_s5_b_im, v_s5_c_re, v_s5_c_im, v_s5_d, v_s5_w_glu, v_mla_w_in, v_mla_q_lora_norm, v_mla_kv_lora_norm, v_mla_w_uq, v_mla_w_ukv, v_mla_q_nope_norm, v_mla_k_nope_norm, v_mla_q_rope_norm, v_mla_k_rope_norm):
    args = dict(locals())
    wts = {n: args[n] for n in WEIGHT_ORDER}
    mom = {n: args["m_" + n] for n in WEIGHT_ORDER}
    var = {n: args["v_" + n] for n in WEIGHT_ORDER}

    full = unpack_weights(all_gather_chips("gather_weights", pack_weight_shard(wts)))
    full["mla_w_in"] = mla_in_permute(full["mla_w_in"])
    full["mla_w_uq"] = uq_permute(full["mla_w_uq"])
    small = {n: wts[n] for n, _ in SMALL}
    small["mla_q_lora_norm"], small["mla_kv_lora_norm"] = full["mla_q_lora_norm"], full["mla_kv_lora_norm"]

    loss, grad_x, g = device_step(x[0], mem[0], positions[0], loss_target[0], full, small)
    loss = lax.psum(loss[0, 0], MESH_AXES)

    big_parts = jnp.concatenate([full_to_shards(n, g[n]).reshape(4, -1) for n, _ in BIG], axis=1)
    small_flat = jnp.concatenate([g[n].reshape(-1) for n, _ in SMALL_FULL])
    small_parts = jnp.pad(small_flat, (0, 4 * SMALL_ROWS * LANES - N_SMALL)).reshape(4, SMALL_ROWS * LANES)
    gbuf = jnp.concatenate([big_parts, small_parts], axis=1).astype(BF16).reshape(4, G_ROWS, LANES)
    red = reduce_scatter_chips(gbuf)
    small_all = all_gather_chips("gather_small_grads", red[BIG_ROWS:]).reshape(-1)[:N_SMALL]

    grads, off = {}, 0
    flat_big = red[:BIG_ROWS].reshape(-1)
    for n, s in BIG:
        grads[n] = flat_big[off:off + math.prod(s)].reshape(s)
        off += math.prod(s)
    off = 0
    shard = 2 * lax.axis_index("x") + lax.axis_index("y")
    for n, s in SMALL_FULL:
        grads[n] = small_all[off:off + math.prod(s)].reshape(s)
        off += math.prod(s)
    for n, s in SHARDED_SMALL:
        grads[n] = lax.dynamic_slice(grads[n], (0, shard * s[1]), s)

    delta, new_m, new_v = {}, {}, {}
    for n, s in BIG:
        two_d = (s[0] * s[1], s[2])
        res = adamw("adamw_" + n, wts[n].reshape(two_d), grads[n].reshape(two_d), mom[n].reshape(two_d), var[n].reshape(two_d))
        delta[n], new_m[n], new_v[n] = (r.reshape(s) for r in res)
    small_names = [n for n, _ in SMALL] + [n for n, _ in SHARDED_SMALL]
    n_own = sum(wts[n].size for n in small_names)
    rows_own = -(-n_own // (256 * 128)) * 256

    def pack_small(d):
        flat = jnp.concatenate([d[n].reshape(-1) for n in small_names])
        return jnp.pad(flat, (0, rows_own * 128 - n_own), constant_values=1.0).reshape(rows_own, 128)

    res = adamw("adamw_small", pack_small(wts), pack_small(grads), pack_small(mom), pack_small(var))
    off = 0
    for n in small_names:
        size = wts[n].size
        delta[n], new_m[n], new_v[n] = (r.reshape(-1)[off:off + size].reshape(wts[n].shape) for r in res)
        off += size

    return (loss, grad_x[None], *[grads[n] for n in WEIGHT_ORDER], *[delta[n] for n in WEIGHT_ORDER],
            *[new_m[n] for n in WEIGHT_ORDER], *[new_v[n] for n in WEIGHT_ORDER])
```

```python
import functools
import math

import jax
import jax.numpy as jnp
from jax import lax
from jax.experimental import pallas as pl
from jax.experimental.pallas import tpu as pltpu

F32, BF16 = jnp.float32, jnp.bfloat16
SDS = jax.ShapeDtypeStruct

D = 1024
L = 2048
ML = 256
BW = 2 * D
XQW = BW // 4
PW = BW - XQW
XH, XHD = 4, 128
SG, SC, SP = 96, 16, 64
SN = SG * SP
NOPE, ROPE, VD = 128, 64, 128
MH = 12
QL, KVL = 512, 256
EPS = 1e-6
ROPE_THETA = 10000.0
MLA_IN = QL + KVL + ROPE + XQW + BW
MLA_IN_P = 3456
ADAM_LR, ADAM_B1, ADAM_B2, ADAM_EPS, ADAM_WD, ADAM_STEP = 0.001, 0.9, 0.999, 1e-08, 0.01, 10

VMEM_LIMIT = 48 * 2**20
SEG = 8
SEG_LEN = L // SEG
MESH_AXES = ("x", "y", "c")


def _cparams():
    return pltpu.CompilerParams(vmem_limit_bytes=VMEM_LIMIT)


def _dg(a, b, ca, cb):
    return lax.dot_general(a.astype(BF16), b.astype(BF16), (((ca,), (cb,)), ((), ())), preferred_element_type=F32)


@jax.custom_vjp
def mm_nn(a, b):
    return _dg(a, b, 1, 0)


mm_nn.defvjp(lambda a, b: (_dg(a, b, 1, 0), (a, b)), lambda res, g: (_dg(g, res[1], 1, 1), _dg(res[0], g, 0, 0)))


@jax.custom_vjp
def mm_nt(a, b):
    return _dg(a, b, 1, 1)


mm_nt.defvjp(lambda a, b: (_dg(a, b, 1, 1), (a, b)), lambda res, g: (_dg(g, res[1], 1, 0), _dg(g, res[0], 0, 0)))


@functools.partial(jax.custom_vjp, nondiff_argnums=(1,))
def lane_roll(x, shift):
    return pltpu.roll(x, shift, 1)


lane_roll.defvjp(lambda x, shift: (pltpu.roll(x, shift, 1), None),
                 lambda shift, _, g: (pltpu.roll(g, (128 - shift) % 128, 1),))


def rms(x, g):
    return x * lax.rsqrt(jnp.mean(x * x, axis=-1, keepdims=True) + EPS) * g


def softmax_rows(s):
    m = lax.stop_gradient(jnp.max(s, axis=-1, keepdims=True))
    e = jnp.exp(s - m)
    return e / jnp.sum(e, axis=-1, keepdims=True)


def silu(x):
    return x * jax.nn.sigmoid(x)


def stage(name, fn, grid, ins, outs):
    n_in = len(ins)

    def kern(*refs):
        res = fn(*[r[...] for r in refs[:n_in]])
        for r, v in zip(refs[n_in:], res):
            r[...] = v.astype(r.dtype)

    return pl.pallas_call(kern, grid=grid, in_specs=[s for _, s in ins], out_specs=[s for _, s in outs],
                          out_shape=[sd for sd, _ in outs], name=name, compiler_params=_cparams())(*[a for a, _ in ins])


def stage_bwd(name, fn, grid, ins, cts, diffs):
    n_in, n_ct = len(ins), len(cts)
    didx = [i for i, d in enumerate(diffs) if d is not None]

    def kern(*refs):
        vals = [r[...] for r in refs[:n_in]]

        def f(*dv):
            full = list(vals)
            for i, v in zip(didx, dv):
                full[i] = v
            return fn(*full)

        _, vjp = jax.vjp(f, *[vals[i].astype(F32) for i in didx])
        gs = vjp(tuple(c[...].astype(F32) for c in refs[n_in:n_in + n_ct]))
        for o_ref, i, g in zip(refs[n_in + n_ct:], didx, gs):
            if diffs[i][0] == "row":
                o_ref[...] = g.astype(o_ref.dtype)
            else:
                first = functools.reduce(jnp.logical_and, [pl.program_id(ax) == 0 for ax in diffs[i][1]])

                @pl.when(first)
                def _():
                    o_ref[...] = g

                @pl.when(jnp.logical_not(first))
                def _():
                    o_ref[...] += g

    out_shape, out_specs = [], []
    for i in didx:
        if diffs[i][0] == "row":
            out_shape.append(diffs[i][1])
            out_specs.append(diffs[i][2])
        else:
            out_shape.append(SDS(ins[i][0].shape, F32))
            out_specs.append(ins[i][1])
    return pl.pallas_call(kern, grid=grid, in_specs=[s for _, s in ins] + [s for _, s in cts], out_specs=out_specs,
                          out_shape=out_shape, name=name, compiler_params=_cparams())(*[a for a, _ in ins], *[a for a, _ in cts])


def rspec(tl, w, cb=0):
    return pl.BlockSpec((tl, w), lambda i: (i, cb))


def cspec(shape):
    return pl.BlockSpec(shape, lambda i: (0,) * len(shape))


def row_fwd(name, fn, rows, tl, row_ins, consts, outs):
    ins = [(a, rspec(tl, w, cb)) for a, w, cb in row_ins] + [(a, cspec(a.shape)) for a in consts]
    return stage(name, fn, (rows // tl,), ins, [(SDS((rows, w), dt), rspec(tl, w)) for w, dt in outs])


def row_bwd(name, fn, rows, tl, row_ins, consts, cts, row_diff, const_diff, row_dtype=F32):
    ins = [(a, rspec(tl, w, cb)) for a, w, cb in row_ins] + [(a, cspec(a.shape)) for a in consts]
    diffs = [("row", SDS((rows, w), row_dtype), rspec(tl, w)) if d else None for (a, w, cb), d in zip(row_ins, row_diff)]
    diffs += [("acc", (0,)) if d else None for d in const_diff]
    return stage_bwd(name, fn, (rows // tl,), ins, [(a, rspec(tl, w, cb)) for a, w, cb in cts], diffs)


def _pick(dim, prefs):
    for p in prefs:
        if dim % p == 0:
            return p
    return dim


def matmul(name, a, b, mode, out_dtype=F32, add=None):
    if mode == "tn":
        k_dim, m = a.shape
    else:
        m, k_dim = a.shape
    n = b.shape[0] if mode == "nt" else b.shape[1]
    tm = _pick(m, (512, 256, 128))
    tn = _pick(n, (1024, 768, 512, 384, 256, 128))
    tk = _pick(k_dim, (512, 384, 256, 128))
    nk = k_dim // tk
    a_spec = pl.BlockSpec((tk, tm), lambda i, j, k: (k, i)) if mode == "tn" else pl.BlockSpec((tm, tk), lambda i, j, k: (i, k))
    b_spec = pl.BlockSpec((tn, tk), lambda i, j, k: (j, k)) if mode == "nt" else pl.BlockSpec((tk, tn), lambda i, j, k: (k, j))
    o_spec = pl.BlockSpec((tm, tn), lambda i, j, k: (i, j))
    ca, cb = {"nn": (1, 0), "nt": (1, 1), "tn": (0, 0)}[mode]

    def kern(*refs):
        a_ref, b_ref = refs[0], refs[1]
        o_ref, acc = refs[-2], refs[-1]
        k = pl.program_id(2)

        @pl.when(k == 0)
        def _():
            acc[...] = jnp.zeros_like(acc)

        acc[...] += _dg(a_ref[...], b_ref[...], ca, cb)

        @pl.when(k == nk - 1)
        def _():
            r = acc[...]
            if add is not None:
                r = r + refs[2][...]
            o_ref[...] = r.astype(o_ref.dtype)

    ins, specs = [a, b], [a_spec, b_spec]
    if add is not None:
        ins.append(add)
        specs.append(o_spec)
    return pl.pallas_call(kern, grid=(m // tm, n // tn, nk), in_specs=specs, out_specs=o_spec,
                          out_shape=SDS((m, n), out_dtype), scratch_shapes=[pltpu.VMEM((tm, tn), F32)],
                          name=name, compiler_params=_cparams())(*ins)


SCAN_LANES = 256


def _cmul(ar, ai, br, bi):
    return ar * br - ai * bi, ar * bi + ai * br


def _sub_shift(x, down):
    row = lax.broadcasted_iota(jnp.int32, x.shape, 0)
    if down:
        return jnp.where(row == 0, 0.0, pltpu.roll(x, 1, 0))
    return jnp.where(row == SEG - 1, 0.0, pltpu.roll(x, SEG - 1, 0))


def _pow_seg_len(ar, ai):
    for _ in range(int(math.log2(SEG_LEN))):
        ar, ai = _cmul(ar, ai, ar, ai)
    return ar, ai


def s5_scan_fwd(bu_re, bu_im, a_re, a_im):
    lanes = SCAN_LANES

    def kern(bur, bui, ar_ref, ai_ref, sr, si):
        ar = jnp.broadcast_to(ar_ref[...], (SEG, lanes))
        ai = jnp.broadcast_to(ai_ref[...], (SEG, lanes))
        zero = jnp.zeros((SEG, lanes), F32)

        def local(i, carry):
            rows = pl.ds(pl.multiple_of(i * SEG, SEG), SEG)
            mr, mi = _cmul(ar, ai, carry[0], carry[1])
            nr, ni = mr + bur[rows, :], mi + bui[rows, :]
            sr[rows, :] = nr
            si[rows, :] = ni
            return nr, ni

        fr, fi = lax.fori_loop(0, SEG_LEN, local, (zero, zero))
        pr, pi = _pow_seg_len(ar, ai)
        ir, ii = zero, zero
        for _ in range(SEG - 1):
            mr, mi = _cmul(pr, pi, ir, ii)
            ir, ii = _sub_shift(mr + fr, True), _sub_shift(mi + fi, True)

        def carry_in(i, pw):
            rows = pl.ds(pl.multiple_of(i * SEG, SEG), SEG)
            cr, ci = _cmul(pw[0], pw[1], ir, ii)
            sr[rows, :] += cr
            si[rows, :] += ci
            return _cmul(pw[0], pw[1], ar, ai)

        lax.fori_loop(0, SEG_LEN, carry_in, (ar, ai))

    blk = pl.BlockSpec((L, lanes), lambda j: (0, j))
    vec = pl.BlockSpec((1, lanes), lambda j: (0, j))
    return pl.pallas_call(kern, grid=(SN // lanes,), in_specs=[blk, blk, vec, vec], out_specs=[blk, blk],
                          out_shape=[SDS((L, SN), F32)] * 2, name="s5_scan_fwd", compiler_params=_cparams())(bu_re, bu_im, a_re, a_im)


def s5_scan_bwd(g_re, g_im, s_re, s_im, a_re, a_im):
    lanes = SCAN_LANES

    def kern(gr, gi, sr, si, ar_ref, ai_ref, lr, li, dar, dai):
        ar = jnp.broadcast_to(ar_ref[...], (SEG, lanes))
        ai = -jnp.broadcast_to(ai_ref[...], (SEG, lanes))
        zero = jnp.zeros((SEG, lanes), F32)

        def local(k, carry):
            i = SEG_LEN - 1 - k
            rows = pl.ds(pl.multiple_of(i * SEG, SEG), SEG)
            mr, mi = _cmul(ar, ai, carry[0], carry[1])
            nr, ni = mr + gr[rows, :], mi + gi[rows, :]
            lr[rows, :] = nr
            li[rows, :] = ni
            return nr, ni

        fr, fi = lax.fori_loop(0, SEG_LEN, local, (zero, zero))
        pr, pi = _pow_seg_len(ar, ai)
        ir, ii = zero, zero
        for _ in range(SEG - 1):
            mr, mi = _cmul(pr, pi, ir, ii)
            ir, ii = _sub_shift(mr + fr, False), _sub_shift(mi + fi, False)

        def fix(rows, pw):
            cr, ci = _cmul(pw[0], pw[1], ir, ii)
            tr, ti = lr[rows, :] + cr, li[rows, :] + ci
            lr[rows, :] = tr
            li[rows, :] = ti
            return tr, ti

        def grad_a(tr, ti, spr, spi, acc):
            return acc[0] + tr * spr + ti * spi, acc[1] + ti * spr - tr * spi

        def carry_in(k, c):
            i = SEG_LEN - 1 - k
            rows = pl.ds(pl.multiple_of(i * SEG, SEG), SEG)
            prev = pl.ds(pl.multiple_of((i - 1) * SEG, SEG), SEG)
            tr, ti = fix(rows, (c[0], c[1]))
            acc = grad_a(tr, ti, sr[prev, :], si[prev, :], (c[2], c[3]))
            nr, ni = _cmul(c[0], c[1], ar, ai)
            return nr, ni, acc[0], acc[1]

        pwr, pwi, accr, acci = lax.fori_loop(0, SEG_LEN - 1, carry_in, (ar, ai, zero, zero))
        tr, ti = fix(pl.ds(0, SEG), (pwr, pwi))
        last = pl.ds((SEG_LEN - 1) * SEG, SEG)
        accr, acci = grad_a(tr, ti, _sub_shift(sr[last, :], True), _sub_shift(si[last, :], True), (accr, acci))
        dar[...] = jnp.sum(accr, axis=0, keepdims=True)
        dai[...] = jnp.sum(acci, axis=0, keepdims=True)

    blk = pl.BlockSpec((L, lanes), lambda j: (0, j))
    vec = pl.BlockSpec((1, lanes), lambda j: (0, j))
    return pl.pallas_call(kern, grid=(SN // lanes,), in_specs=[blk, blk, blk, blk, vec, vec], out_specs=[blk, blk, vec, vec],
                          out_shape=[SDS((L, SN), F32)] * 2 + [SDS((1, SN), F32)] * 2, name="s5_scan_bwd",
                          compiler_params=_cparams())(g_re, g_im, s_re, s_im, a_re, a_im)


def fn_rms(x, g):
    return (rms(x, g),)


def fn_s5_disc(lre, lim, ls):
    step = jnp.exp(ls)
    e = jnp.exp(lre * step)
    a_re, a_im = e * jnp.cos(lim * step), e * jnp.sin(lim * step)
    den = lre * lre + lim * lim
    nr, ni = a_re - 1.0, a_im
    return a_re, a_im, (nr * lre + ni * lim) / den, (ni * lre - nr * lim) / den


def _group_mask(rows, cols, row_div, col_div):
    r = lax.broadcasted_iota(jnp.int32, (rows, cols), 0) // row_div % 8
    c = lax.broadcasted_iota(jnp.int32, (rows, cols), 1) // col_div
    return r == c


def fn_s5_bmat(b_re, b_im, coef_re, coef_im):
    rows = b_re.shape[0]
    mask = _group_mask(rows, 8 * SC, SP, SC)
    bb_re = coef_re * b_re - coef_im * b_im
    bb_im = coef_re * b_im + coef_im * b_re
    return jnp.where(mask, jnp.tile(bb_re, (1, 8)), 0.0), jnp.where(mask, jnp.tile(bb_im, (1, 8)), 0.0)


def fn_s5_cmat(c_re, c_im):
    rows = c_re.shape[0]
    mask = _group_mask(rows, 8 * SP, SC, SP)
    return jnp.where(mask, jnp.tile(c_re, (1, 8)), 0.0), jnp.where(mask, jnp.tile(c_im, (1, 8)), 0.0)


def fn_s5_bu(u, wb_re, wb_im):
    return mm_nt(u, wb_re), mm_nt(u, wb_im)


def fn_s5_out(sr, si, u, d, wc_re, wc_im):
    y = mm_nt(sr, wc_re) - mm_nt(si, wc_im) + d * u
    return (jax.nn.gelu(y),)


def fn_merge_glu(z, mo, gate):
    yg = z[:, :PW] * jax.nn.sigmoid(z[:, PW:])
    return (jnp.concatenate([yg, mo], axis=1) * silu(gate),)


def fn_merge(prim, mo, gate):
    return (jnp.concatenate([prim, mo], axis=1) * silu(gate),)


def fn_mem_k(kv, g):
    return (jnp.concatenate([rms(kv[:, h * XHD:(h + 1) * XHD], g) for h in range(XH)], axis=1),)


def fn_mem_attn(xq, kn, v, g):
    outs = []
    for h in range(XH):
        sl = slice(h * XHD, (h + 1) * XHD)
        p = softmax_rows(mm_nt(rms(xq[:, sl], g), kn[:, sl]) * (XHD ** -0.5))
        outs.append(mm_nn(p, v[:, sl]))
    return (jnp.concatenate(outs, axis=1),)


def _half_rms(x, g):
    lo = lax.broadcasted_iota(jnp.int32, x.shape, 1) < ROPE
    x2 = x * x
    s_lo = jnp.sum(jnp.where(lo, x2, 0.0), axis=1, keepdims=True)
    s_hi = jnp.sum(jnp.where(lo, 0.0, x2), axis=1, keepdims=True)
    return x * lax.rsqrt(jnp.where(lo, s_lo, s_hi) / ROPE + EPS) * g


def _rope(x, cos2, sin_signed):
    first = lax.broadcasted_iota(jnp.int32, x.shape, 1) % ROPE < ROPE // 2
    return x * cos2 + jnp.where(first, lane_roll(x, 128 - ROPE // 2), lane_roll(x, ROPE // 2)) * sin_signed


def fn_mla_prep(q, kv, kr, cos2, sin_signed, qnn, knn, qrn, krn):
    lo = lax.broadcasted_iota(jnp.int32, kr.shape, 1) < ROPE
    kr_pad = jnp.where(lo, _rope(_half_rms(kr, krn), cos2, sin_signed), 0.0)
    qf, kf, vs = [], [], []
    for m in range(MH // 2):
        pair = _rope(_half_rms(q[:, MH * NOPE + 128 * m:MH * NOPE + 128 * (m + 1)], qrn), cos2, sin_signed)
        for h, rope_h in ((2 * m, pair), (2 * m + 1, lane_roll(pair, ROPE))):
            qf.append(jnp.concatenate([rms(q[:, NOPE * h:NOPE * (h + 1)], qnn), jnp.where(lo, rope_h, 0.0)], axis=1))
    for h in range(MH):
        kf.append(jnp.concatenate([rms(kv[:, 256 * h:256 * h + NOPE], knn), kr_pad], axis=1))
        vs.append(kv[:, 256 * h + NOPE:256 * (h + 1)])
    return jnp.stack(qf), jnp.stack(kf), jnp.stack(vs)


ATT_TQ = 256


def fn_causal_attn(q, kf, v):
    s = mm_nt(q, kf) * ((NOPE + ROPE) ** -0.5)
    q_pos = pl.program_id(1) * ATT_TQ + lax.broadcasted_iota(jnp.int32, s.shape, 0)
    k_pos = lax.broadcasted_iota(jnp.int32, s.shape, 1)
    p = softmax_rows(jnp.where(k_pos <= q_pos, s, jnp.finfo(F32).min))
    return (mm_nn(p, v),)


def loss_and_grad(y, target, tl=256):
    def kern(y_ref, t_ref, dy_ref, loss_ref):
        d = y_ref[...] - t_ref[...]
        dy_ref[...] = d / D

        @pl.when(pl.program_id(0) == 0)
        def _():
            loss_ref[...] = jnp.zeros_like(loss_ref)

        loss_ref[...] += 0.5 * jnp.sum(jnp.sum(d * d, axis=1, keepdims=True), axis=0, keepdims=True) / D

    return pl.pallas_call(kern, grid=(L // tl,), in_specs=[rspec(tl, D), rspec(tl, D)], out_specs=[rspec(tl, D), cspec((1, 1))],
                          out_shape=[SDS((L, D), F32), SDS((1, 1), F32)], name="loss", compiler_params=_cparams())(y, target)


def adamw(name, w, g, m, v):
    rows, cols = w.shape
    tr = _pick(rows, (256, 128, 64, 32, 16, 8))

    def kern(w_ref, g_ref, m_ref, v_ref, d_ref, nm_ref, nv_ref):
        gg = g_ref[...]
        nm = ADAM_B1 * m_ref[...] + (1.0 - ADAM_B1) * gg
        nv = ADAM_B2 * v_ref[...] + (1.0 - ADAM_B2) * jnp.square(gg)
        m_hat = nm / (1.0 - ADAM_B1 ** ADAM_STEP)
        v_hat = nv / (1.0 - ADAM_B2 ** ADAM_STEP)
        d_ref[...] = -ADAM_LR * (m_hat / (jnp.sqrt(v_hat) + ADAM_EPS) + ADAM_WD * w_ref[...])
        nm_ref[...] = nm
        nv_ref[...] = nv

    spec = rspec(tr, cols)
    return pl.pallas_call(kern, grid=(rows // tr,), in_specs=[spec] * 4, out_specs=[spec] * 3,
                          out_shape=[SDS((rows, cols), F32)] * 3, name=name, compiler_params=_cparams())(w, g, m, v)


def _row_tile(rows, cap=512, unit=16):
    return max(t for t in range(unit, cap + 1, unit) if rows % t == 0)


def add_slots(name, terms, out_dtype):
    _, rows, cols = terms[0][0].shape
    tr = _row_tile(rows)
    n = len(terms)

    def kern(*refs):
        acc = refs[0][...].astype(F32)
        for r in refs[1:n]:
            acc = acc + r[...].astype(F32)
        refs[n][...] = acc.astype(refs[n].dtype)

    specs = [pl.BlockSpec((None, tr, cols), lambda i, s=s: (s, i, 0)) for _, s in terms]
    return pl.pallas_call(kern, grid=(rows // tr,), in_specs=specs, out_specs=rspec(tr, cols),
                          out_shape=SDS((rows, cols), out_dtype), name=name, compiler_params=_cparams())(*[a for a, _ in terms])


ANY = pl.BlockSpec(memory_space=pl.ANY)
MESH_ID = pl.DeviceIdType.MESH
ICI_STREAMS = 4
D2D_STREAMS = 8


def _place():
    x, y, c = lax.axis_index("x"), lax.axis_index("y"), lax.axis_index("c")
    return x, y, c, [(1 - x, y), (x, 1 - y), (1 - x, 1 - y)]


def _row_chunks(rows, n, dtype):
    unit = 32 // jnp.dtype(dtype).itemsize
    base, extra = divmod(rows // unit, n)
    out, start = [], 0
    for k in range(n):
        size = (base + (k < extra)) * unit
        if size:
            out.append((start, size))
            start += size
    assert start == rows, (rows, unit)
    return out


def _dma_sems(n):
    return [pltpu.SemaphoreType.DMA((n,)), pltpu.SemaphoreType.DMA((n,))]


def all_gather_chips(name, shard):
    rows, cols = shard.shape
    half = rows // 2
    chunks = _row_chunks(half, ICI_STREAMS, shard.dtype)
    n = len(chunks)

    def body(x_ref, out_ref, send_sems, recv_sems):
        x, y, c, chips = _place()
        sibling = (x, y, 1 - c)
        mine = 2 * x + y

        def copy(sem, chip, cc, k, to, from_input=False):
            rows_k = pl.ds(cc * half + chunks[k][0], chunks[k][1])
            dst = out_ref.at[chip, rows_k, :]
            return pltpu.make_async_remote_copy(src_ref=x_ref.at[rows_k, :] if from_input else dst, dst_ref=dst,
                                                send_sem=send_sems.at[sem], recv_sem=recv_sems.at[sem], device_id=to, device_id_type=MESH_ID)

        order = [(k, j, 2 * cx + cy, (cx, cy, c)) for k in range(n) for j, (cx, cy) in enumerate(chips)]
        first = [copy(j * n + k, mine, c, k, to, from_input=True) for k, j, _, to in order]
        for cp in first:
            cp.start()
        passed = []
        for k, j, chip, _ in order:
            copy(j * n + k, chip, c, k, sibling).wait_recv()
            passed.append(copy((3 + j) * n + k, chip, c, k, sibling))
            passed[-1].start()
        for k, j, chip, _ in order:
            copy((3 + j) * n + k, chip, 1 - c, k, sibling).wait_recv()
        for cp in first + passed:
            cp.wait_send()

    return pl.pallas_call(body, in_specs=[ANY], out_specs=ANY, out_shape=SDS((4, rows, cols), shard.dtype),
                          scratch_shapes=_dma_sems(6 * n), name=name)(shard)


def pair_exchange(name, g):
    slots, rows, cols = g.shape
    half = rows // 2
    pieces = [(s, st, sz) for s in range(slots) for st, sz in _row_chunks(half, D2D_STREAMS // slots, g.dtype)]

    def body(g_ref, got_ref, send_sems, recv_sems):
        x, y, c, _ = _place()
        swaps = [pltpu.make_async_remote_copy(src_ref=g_ref.at[s, pl.ds((1 - c) * half + st, sz), :], dst_ref=got_ref.at[s, pl.ds(st, sz), :],
                                              send_sem=send_sems.at[k], recv_sem=recv_sems.at[k], device_id=(x, y, 1 - c), device_id_type=MESH_ID)
                 for k, (s, st, sz) in enumerate(pieces)]
        for cp in swaps:
            cp.start()
        for cp in swaps:
            cp.wait()

    return pl.pallas_call(body, in_specs=[ANY], out_specs=ANY, out_shape=SDS((slots, half, cols), g.dtype),
                          scratch_shapes=_dma_sems(len(pieces)), name=name)(g)


def chip_scatter(name, p):
    chunks = _row_chunks(p.shape[1], ICI_STREAMS, p.dtype)
    n = len(chunks)

    def body(p_ref, q_ref, send_sems, recv_sems):
        x, y, c, chips = _place()
        mine = 2 * x + y

        def copy(j, k, src_slot, dst_slot, to):
            rows_k = pl.ds(chunks[k][0], chunks[k][1])
            return pltpu.make_async_remote_copy(src_ref=p_ref.at[src_slot, rows_k, :], dst_ref=q_ref.at[dst_slot, rows_k, :],
                                                send_sem=send_sems.at[j * n + k], recv_sem=recv_sems.at[j * n + k], device_id=to,
                                                device_id_type=MESH_ID)

        order = [(k, j, 2 * cx + cy, (cx, cy, c)) for k in range(n) for j, (cx, cy) in enumerate(chips)]
        sends = [copy(j, k, chip, mine, to) for k, j, chip, to in order]
        for cp in sends:
            cp.start()
        for k, j, chip, to in order:
            copy(j, k, mine, chip, to).wait_recv()
        for cp in sends:
            cp.wait_send()

    return pl.pallas_call(body, in_specs=[ANY], out_specs=ANY, out_shape=SDS(p.shape, p.dtype),
                          scratch_shapes=_dma_sems(3 * n), name=name)(p)


def pair_join(name, buf):
    rows, cols = buf.shape
    half = rows // 2
    chunks = _row_chunks(half, D2D_STREAMS, buf.dtype)

    def body(_, out_ref, send_sems, recv_sems):
        x, y, c, _ = _place()

        def copy(k, cc):
            rows_k = out_ref.at[pl.ds(cc * half + chunks[k][0], chunks[k][1]), :]
            return pltpu.make_async_remote_copy(src_ref=rows_k, dst_ref=rows_k, send_sem=send_sems.at[k], recv_sem=recv_sems.at[k],
                                                device_id=(x, y, 1 - c), device_id_type=MESH_ID)

        gives = [copy(k, c) for k in range(len(chunks))]
        for cp in gives:
            cp.start()
        for k in range(len(chunks)):
            copy(k, 1 - c).wait_recv()
        for cp in gives:
            cp.wait_send()

    return pl.pallas_call(body, in_specs=[ANY], out_specs=ANY, out_shape=SDS(buf.shape, buf.dtype), input_output_aliases={0: 0},
                          scratch_shapes=_dma_sems(len(chunks)), name=name)(buf)


def pair_add(name, g, got, place):
    slots, rows, cols = g.shape
    half = rows // 2
    tr = _row_tile(half)
    nb = half // tr

    def kern(_, g_ref, t_ref, o_ref):
        o_ref[...] = (g_ref[...].astype(F32) + t_ref[...].astype(F32)).astype(o_ref.dtype)

    blk = pl.BlockSpec((None, tr, cols), lambda s, i, p: (s, i, 0))
    grid_spec = pltpu.PrefetchScalarGridSpec(
        num_scalar_prefetch=1, grid=(slots, nb),
        in_specs=[pl.BlockSpec((None, tr, cols), lambda s, i, p: (s, p[1] * nb + i, 0)), blk], out_specs=blk)
    return pl.pallas_call(kern, grid_spec=grid_spec, out_shape=SDS((slots, half, cols), g.dtype), name=name,
                          compiler_params=_cparams())(place, g, got)


def chip_add(name, p, q, place):
    slots, half, cols = p.shape
    tr = _row_tile(half)
    nb = half // tr

    def kern(_, p_ref, q1, q2, q3, o_ref):
        o_ref[...] = p_ref[...].astype(F32) + q1[...].astype(F32) + q2[...].astype(F32) + q3[...].astype(F32)

    def slot(k):
        return pl.BlockSpec((None, tr, cols), lambda i, pr: ((pr[0] + k) % slots, i, 0))

    grid_spec = pltpu.PrefetchScalarGridSpec(
        num_scalar_prefetch=1, grid=(nb,), in_specs=[slot(0), slot(1), slot(2), slot(3)],
        out_specs=pl.BlockSpec((tr, cols), lambda i, pr: (pr[1] * nb + i, 0)))
    return pl.pallas_call(kern, grid_spec=grid_spec, out_shape=SDS((2 * half, cols), F32), name=name,
                          compiler_params=_cparams())(place, p, q, q, q)


def reduce_scatter_chips(g, place):
    pair = pair_add("rs_pair_add", g, pair_exchange("rs_pair_exchange", g), place)
    return pair_join("rs_pair_join", chip_add("rs_chip_add", pair, chip_scatter("rs_chip_scatter", pair), place))


BIG = [("w_out", (2, 512, 1024)), ("w_mem_kv", (2, 256, 1024)), ("s5_w_in", (1, 1024, 1024)), ("s5_w_glu", (1, 1536, 768)),
       ("mla_w_in", (1, 1024, 848)), ("mla_w_uq", (1, 512, 576)), ("mla_w_ukv", (1, 256, 768))]
SHARDED_SMALL = [("mla_q_lora_norm", (1, 128)), ("mla_kv_lora_norm", (1, 64))]
SMALL = [("ln_gain", (2, 1024)), ("mem_norm", (2, 1024)), ("xq_norm", (2, 128)), ("xk_norm", (2, 128)),
         ("s5_lambda_re", (1, 96, 64)), ("s5_lambda_im", (1, 96, 64)), ("s5_log_step", (1, 96)),
         ("s5_b_re", (1, 96, 64, 16)), ("s5_b_im", (1, 96, 64, 16)), ("s5_c_re", (1, 96, 16, 64)), ("s5_c_im", (1, 96, 16, 64)),
         ("s5_d", (1, 1536)), ("mla_q_nope_norm", (1, 128)), ("mla_k_nope_norm", (1, 128)), ("mla_q_rope_norm", (1, 64)),
         ("mla_k_rope_norm", (1, 64))]
WEIGHT_ORDER = ["ln_gain", "w_out", "mem_norm", "w_mem_kv", "xq_norm", "xk_norm", "s5_w_in", "s5_lambda_re", "s5_lambda_im",
                "s5_log_step", "s5_b_re", "s5_b_im", "s5_c_re", "s5_c_im", "s5_d", "s5_w_glu", "mla_w_in", "mla_q_lora_norm",
                "mla_kv_lora_norm", "mla_w_uq", "mla_w_ukv", "mla_q_nope_norm", "mla_k_nope_norm", "mla_q_rope_norm", "mla_k_rope_norm"]
LANES = 1024
BIG_ROWS = sum(math.prod(s) for _, s in BIG) // LANES
W_ROWS = 5056
SMALL_FULL = SMALL + [(n, (1, 4 * s[1])) for n, s in SHARDED_SMALL]
N_SMALL = sum(math.prod(s) for _, s in SMALL_FULL)
SMALL_ROWS = 112
G_ROWS = BIG_ROWS + SMALL_ROWS


def shards_to_full(name, arr4):
    if name in ("w_out", "w_mem_kv"):
        return arr4.transpose(1, 0, 2, 3).reshape(2, 4 * arr4.shape[2], arr4.shape[3])
    if arr4.ndim == 3:
        return arr4.reshape(-1)
    return arr4[:, 0].transpose(1, 0, 2).reshape(arr4.shape[2], 4 * arr4.shape[3])


def full_to_shards(name, full):
    if name in ("w_out", "w_mem_kv"):
        return full.reshape(2, 4, full.shape[1] // 4, full.shape[2]).transpose(1, 0, 2, 3)
    return full.reshape(full.shape[0], 4, full.shape[1] // 4).transpose(1, 0, 2)


def pack_weight_shard(w):
    parts = [w[n].astype(BF16).reshape(-1) for n, _ in BIG]
    parts += [lax.bitcast_convert_type(w[n].reshape(-1), BF16).reshape(-1) for n, _ in SHARDED_SMALL]
    flat = jnp.concatenate(parts)
    return jnp.pad(flat, (0, W_ROWS * LANES - flat.shape[0])).reshape(W_ROWS, LANES)


def unpack_weights(wall):
    flat = wall.reshape(4, -1)
    out, off = {}, 0
    for n, s in BIG:
        size = math.prod(s)
        out[n] = shards_to_full(n, flat[:, off:off + size].reshape((4,) + s))
        off += size
    for n, s in SHARDED_SMALL:
        size = 2 * math.prod(s)
        out[n] = lax.bitcast_convert_type(flat[:, off:off + size].reshape(4, s[1], 2), F32).reshape(-1)
        off += size
    return out


def mla_in_permute(w):
    o1, o2, o3, o4 = QL, QL + KVL, QL + KVL + ROPE, QL + KVL + ROPE + XQW
    return jnp.concatenate([w[:, o4:], w[:, :o1], w[:, o3:o4], w[:, o1:o2], w[:, o2:o3],
                            jnp.zeros((w.shape[0], MLA_IN_P - MLA_IN), w.dtype)], axis=1)


def mla_in_unpermute(d):
    return jnp.concatenate([d[:, 2048:2560], d[:, 3072:3328], d[:, 3328:3392], d[:, 2560:3072], d[:, :2048]], axis=1)


def uq_permute(w):
    w3 = w.reshape(w.shape[0], MH, NOPE + ROPE)
    return jnp.concatenate([w3[:, :, :NOPE].reshape(w.shape[0], MH * NOPE), w3[:, :, NOPE:].reshape(w.shape[0], MH * ROPE)], axis=1)


def uq_unpermute(d):
    dn = d[:, :MH * NOPE].reshape(d.shape[0], MH, NOPE)
    dr = d[:, MH * NOPE:].reshape(d.shape[0], MH, ROPE)
    return jnp.concatenate([dn, dr], axis=2).reshape(d.shape[0], MH * (NOPE + ROPE))


def time_permute(a):
    return a.reshape(SEG, SEG_LEN, a.shape[-1]).transpose(1, 0, 2).reshape(L, a.shape[-1])


def time_unpermute(a):
    return a.reshape(SEG_LEN, SEG, a.shape[-1]).transpose(1, 0, 2).reshape(L, a.shape[-1])


def mem_branch_fwd(tag, mem, mem_norm, w_mem_kv, xk_norm):
    mn = row_fwd(tag + "_mem_rms", fn_rms, ML, ML, [(mem, D, 0)], [mem_norm], [(D, BF16)])[0]
    kv = matmul(tag + "_mem_kv", mn, w_mem_kv, "nn")
    kn = row_fwd(tag + "_mem_knorm", fn_mem_k, ML, ML, [(kv, XQW, 0)], [xk_norm], [(XQW, F32)])[0]
    return mn, kv, kn


def mem_branch_bwd(tag, mem, mem_norm, w_mem_kv, xk_norm, mn, kv, dkn, dv):
    dk, dxk = row_bwd(tag + "_mem_knorm_bwd", fn_mem_k, ML, ML, [(kv, XQW, 0)], [xk_norm], [(dkn, XQW, 0)], [True], [True])
    dkv = jnp.concatenate([dk, dv], axis=1)
    dmn = matmul(tag + "_mem_kv_dx", dkv, w_mem_kv, "nt")
    dw = matmul(tag + "_mem_kv_dw", mn, dkv, "tn")
    dmem_norm = row_bwd(tag + "_mem_rms_bwd", fn_rms, ML, ML, [(mem, D, 0)], [mem_norm], [(dmn, D, 0)], [False], [True])[0]
    return dw, dmem_norm, dxk


def mem_attn_fwd(tag, proj, cb, kn, kv, xq_norm):
    return row_fwd(tag + "_mem_attn", fn_mem_attn, L, 256, [(proj, XQW, cb)], [kn, kv[:, XQW:], xq_norm], [(XQW, F32)])[0]


def mem_attn_bwd(tag, proj, cb, kn, kv, xq_norm, dmo):
    return row_bwd(tag + "_mem_attn_bwd", fn_mem_attn, L, 256, [(proj, XQW, cb)], [kn, kv[:, XQW:], xq_norm], [(dmo, XQW, 0)],
                   [True], [True, True, True])


def device_step(x, mem, positions, target, w, small):
    g = {}
    ln, mem_norm, xq_norm, xk_norm = small["ln_gain"], small["mem_norm"], small["xq_norm"], small["xk_norm"]

    lre, lim = small["s5_lambda_re"][0], small["s5_lambda_im"][0]
    ls = small["s5_log_step"].reshape(SG, 1)
    one = pl.BlockSpec((SG, SP), lambda i: (0, 0))
    col = pl.BlockSpec((SG, 1), lambda i: (0, 0))
    disc_ins = [(lre, one), (lim, one), (ls, col)]
    a_re, a_im, coef_re, coef_im = stage("s5_disc", fn_s5_disc, (1,), disc_ins, [(SDS((SG, SP), F32), one)] * 4)
    b_re, b_im = small["s5_b_re"].reshape(SN, SC), small["s5_b_im"].reshape(SN, SC)
    c_re, c_im = small["s5_c_re"].reshape(PW, SP), small["s5_c_im"].reshape(PW, SP)
    bmat_rows = [(b_re, SC, 0), (b_im, SC, 0), (coef_re.reshape(SN, 1), 1, 0), (coef_im.reshape(SN, 1), 1, 0)]
    wb_re, wb_im = row_fwd("s5_bmat", fn_s5_bmat, SN, 512, bmat_rows, [], [(128, F32)] * 2)
    cmat_rows = [(c_re, SP, 0), (c_im, SP, 0)]
    wc_re, wc_im = row_fwd("s5_cmat", fn_s5_cmat, PW, 128, cmat_rows, [], [(512, F32)] * 2)
    a_re_v, a_im_v = a_re.reshape(1, SN), a_im.reshape(1, SN)
    s5_d = small["s5_d"]

    xp = time_permute(x)
    h0 = row_fwd("l0_rms", fn_rms, L, 256, [(xp, D, 0)], [ln[0:1]], [(D, BF16)])[0]
    proj0 = matmul("l0_in", h0, w["s5_w_in"], "nn")
    ts = 512
    u_spec = pl.BlockSpec((ts, 128), lambda j, i: (i, j))
    wb_spec = pl.BlockSpec((512, 128), lambda j, i: (j, 0))
    s_spec = pl.BlockSpec((ts, 512), lambda j, i: (i, j))
    bu_ins = [(proj0, u_spec), (wb_re, wb_spec), (wb_im, wb_spec)]
    bu_re, bu_im = stage("s5_bu", fn_s5_bu, (12, L // ts), bu_ins, [(SDS((L, SN), F32), s_spec)] * 2)
    s_re, s_im = s5_scan_fwd(bu_re, bu_im, a_re_v, a_im_v)
    d_spec = pl.BlockSpec((1, 128), lambda j, i: (0, j))
    wc_spec = pl.BlockSpec((128, 512), lambda j, i: (j, 0))
    out_ins = [(s_re, s_spec), (s_im, s_spec), (proj0, u_spec), (s5_d, d_spec), (wc_re, wc_spec), (wc_im, wc_spec)]
    g0 = stage("s5_out", fn_s5_out, (12, L // ts), out_ins, [(SDS((L, PW), BF16), u_spec)])[0]
    z0 = matmul("l0_glu", g0, w["s5_w_glu"], "nn")
    mn0, kv0, kn0 = mem_branch_fwd("l0", mem, mem_norm[0:1], w["w_mem_kv"][0], xk_norm[0:1])
    mo0 = mem_attn_fwd("l0", proj0, 3, kn0, kv0, xq_norm[0:1])
    o0 = row_fwd("l0_merge", fn_merge_glu, L, 256, [(z0, 2 * PW, 0), (mo0, XQW, 0), (proj0, BW, 1)], [], [(BW, BF16)])[0]
    x1p = matmul("l0_out", o0, w["w_out"][0], "nn", add=xp)
    x1 = time_unpermute(x1p)

    h1 = row_fwd("l1_rms", fn_rms, L, 256, [(x1, D, 0)], [ln[1:2]], [(D, BF16)])[0]
    proj1 = matmul("l1_in", h1, w["mla_w_in"], "nn")
    qln, kvln = small["mla_q_lora_norm"].reshape(1, QL), small["mla_kv_lora_norm"].reshape(1, KVL)
    cqn = row_fwd("l1_q_lora_rms", fn_rms, L, 256, [(proj1, QL, 4)], [qln], [(QL, BF16)])[0]
    ckvn = row_fwd("l1_kv_lora_rms", fn_rms, L, 256, [(proj1, KVL, 12)], [kvln], [(KVL, BF16)])[0]
    q = matmul("l1_uq", cqn, w["mla_w_uq"], "nn")
    kv = matmul("l1_ukv", ckvn, w["mla_w_ukv"], "nn")
    inv_freq = ROPE_THETA ** (-jnp.arange(ROPE // 2, dtype=F32) / (ROPE // 2))
    ang = positions.astype(F32)[:, None] * inv_freq
    cos2 = jnp.tile(jnp.cos(ang), (1, 4))
    sin_signed = jnp.tile(jnp.concatenate([-jnp.sin(ang), jnp.sin(ang)], axis=1), (1, 2))
    qnn, knn = small["mla_q_nope_norm"], small["mla_k_nope_norm"]
    qrn, krn = jnp.tile(small["mla_q_rope_norm"], (1, 2)), jnp.tile(small["mla_k_rope_norm"], (1, 2))
    tp = 256
    prep_ins = [(q, rspec(tp, MH * (NOPE + ROPE))), (kv, rspec(tp, MH * 256)), (proj1, rspec(tp, 128, 26)),
                (cos2, rspec(tp, 128)), (sin_signed, rspec(tp, 128))] + [(a, cspec((1, 128))) for a in (qnn, knn, qrn, krn)]
    hq_spec = pl.BlockSpec((MH, tp, 256), lambda i: (0, i, 0))
    hv_spec = pl.BlockSpec((MH, tp, 128), lambda i: (0, i, 0))
    qf, kf, vh = stage("l1_mla_prep", fn_mla_prep, (L // tp,), prep_ins,
                       [(SDS((MH, L, 256), BF16), hq_spec), (SDS((MH, L, 256), BF16), hq_spec), (SDS((MH, L, 128), BF16), hv_spec)])
    aq_spec = pl.BlockSpec((None, ATT_TQ, 256), lambda h, i: (h, i, 0))
    ak_spec = pl.BlockSpec((None, L, 256), lambda h, i: (h, 0, 0))
    av_spec = pl.BlockSpec((None, L, 128), lambda h, i: (h, 0, 0))
    ao_spec = pl.BlockSpec((ATT_TQ, 128), lambda h, i: (i, h))
    att_ins = [(qf, aq_spec), (kf, ak_spec), (vh, av_spec)]
    attn = stage("l1_attn", fn_causal_attn, (MH, L // ATT_TQ), att_ins, [(SDS((L, PW), F32), ao_spec)])[0]
    mn1, kv1, kn1 = mem_branch_fwd("l1", mem, mem_norm[1:2], w["w_mem_kv"][1], xk_norm[1:2])
    mo1 = mem_attn_fwd("l1", proj1, 5, kn1, kv1, xq_norm[1:2])
    o1 = row_fwd("l1_merge", fn_merge, L, 256, [(attn, PW, 0), (mo1, XQW, 0), (proj1, BW, 0)], [], [(BW, BF16)])[0]
    x2 = matmul("l1_out", o1, w["w_out"][1], "nn", add=x1)
    dx2, loss = loss_and_grad(x2, target)

    do1 = matmul("l1_out_dx", dx2, w["w_out"][1], "nt")
    dw_out1 = matmul("l1_out_dw", o1, dx2, "tn")
    dattn, dmo1, dgate1 = row_bwd("l1_merge_bwd", fn_merge, L, 256, [(attn, PW, 0), (mo1, XQW, 0), (proj1, BW, 0)], [],
                                  [(do1, BW, 0)], [True, True, True], [])
    dxq1, dkn1, dv1, dxqn1 = mem_attn_bwd("l1", proj1, 5, kn1, kv1, xq_norm[1:2], dmo1)
    dw_mkv1, dmem_norm1, dxk1 = mem_branch_bwd("l1", mem, mem_norm[1:2], w["w_mem_kv"][1], xk_norm[1:2], mn1, kv1, dkn1, dv1)
    att_diffs = [("row", SDS((MH, L, 256), F32), aq_spec), ("acc", (1,)), ("acc", (1,))]
    dqf, dkf, dvh = stage_bwd("l1_attn_bwd", fn_causal_attn, (MH, L // ATT_TQ), att_ins, [(dattn, ao_spec)], att_diffs)
    prep_diffs = [("row", SDS((L, MH * (NOPE + ROPE)), F32), rspec(tp, MH * (NOPE + ROPE))), ("row", SDS((L, MH * 256), F32), rspec(tp, MH * 256)),
                  ("row", SDS((L, 128), F32), rspec(tp, 128)), None, None] + [("acc", (0,))] * 4
    dq, dkv, dkr, dqnn, dknn, dqrn, dkrn = stage_bwd("l1_mla_prep_bwd", fn_mla_prep, (L // tp,), prep_ins,
                                                     [(dqf, hq_spec), (dkf, hq_spec), (dvh, hv_spec)], prep_diffs)
    dcqn = matmul("l1_uq_dx", dq, w["mla_w_uq"], "nt")
    dw_uq = matmul("l1_uq_dw", cqn, dq, "tn")
    dckvn = matmul("l1_ukv_dx", dkv, w["mla_w_ukv"], "nt")
    dw_ukv = matmul("l1_ukv_dw", ckvn, dkv, "tn")
    dcq, dqln = row_bwd("l1_q_lora_rms_bwd", fn_rms, L, 256, [(proj1, QL, 4)], [qln], [(dcqn, QL, 0)], [True], [True])
    dckv, dkvln = row_bwd("l1_kv_lora_rms_bwd", fn_rms, L, 256, [(proj1, KVL, 12)], [kvln], [(dckvn, KVL, 0)], [True], [True])
    dproj1 = jnp.concatenate([dgate1, dcq, dxq1, dckv, dkr], axis=1)
    dh1 = matmul("l1_in_dx", dproj1, w["mla_w_in"], "nt")
    dw_in1 = matmul("l1_in_dw", h1, dproj1, "tn")
    dx1_n, dln1 = row_bwd("l1_rms_bwd", fn_rms, L, 256, [(x1, D, 0)], [ln[1:2]], [(dh1, D, 0)], [True], [True])
    dx1p = time_permute(dx2 + dx1_n)

    do0 = matmul("l0_out_dx", dx1p, w["w_out"][0], "nt")
    dw_out0 = matmul("l0_out_dw", o0, dx1p, "tn")
    dz0, dmo0, dgate0 = row_bwd("l0_merge_bwd", fn_merge_glu, L, 256, [(z0, 2 * PW, 0), (mo0, XQW, 0), (proj0, BW, 1)], [],
                                [(do0, BW, 0)], [True, True, True], [])
    dxq0, dkn0, dv0, dxqn0 = mem_attn_bwd("l0", proj0, 3, kn0, kv0, xq_norm[0:1], dmo0)
    dw_mkv0, dmem_norm0, dxk0 = mem_branch_bwd("l0", mem, mem_norm[0:1], w["w_mem_kv"][0], xk_norm[0:1], mn0, kv0, dkn0, dv0)
    dg0 = matmul("l0_glu_dx", dz0, w["s5_w_glu"], "nt")
    dw_glu = matmul("l0_glu_dw", g0, dz0, "tn")
    out_diffs = [("row", SDS((L, SN), F32), s_spec), ("row", SDS((L, SN), F32), s_spec), ("row", SDS((L, PW), F32), u_spec),
                 ("acc", (1,)), ("acc", (1,)), ("acc", (1,))]
    gs_re, gs_im, du_a, dd, dwc_re, dwc_im = stage_bwd("s5_out_bwd", fn_s5_out, (12, L // ts), out_ins, [(dg0, u_spec)], out_diffs)
    dbu_re, dbu_im, da_re, da_im = s5_scan_bwd(gs_re, gs_im, s_re, s_im, a_re_v, a_im_v)
    bu_diffs = [("row", SDS((L, PW), F32), u_spec), ("acc", (1,)), ("acc", (1,))]
    du_b, dwb_re, dwb_im = stage_bwd("s5_bu_bwd", fn_s5_bu, (12, L // ts), bu_ins, [(dbu_re, s_spec), (dbu_im, s_spec)], bu_diffs)
    dproj0 = jnp.concatenate([du_a + du_b, dxq0, dgate0], axis=1)
    dh0 = matmul("l0_in_dx", dproj0, w["s5_w_in"], "nt")
    dw_in0 = matmul("l0_in_dw", h0, dproj0, "tn")
    dxp_n, dln0 = row_bwd("l0_rms_bwd", fn_rms, L, 256, [(xp, D, 0)], [ln[0:1]], [(dh0, D, 0)], [True], [True])
    grad_x = time_unpermute(dx1p + dxp_n)

    db_re, db_im, dcoef_re, dcoef_im = row_bwd("s5_bmat_bwd", fn_s5_bmat, SN, 512, bmat_rows, [], [(dwb_re, 128, 0), (dwb_im, 128, 0)],
                                               [True] * 4, [])
    dc_re, dc_im = row_bwd("s5_cmat_bwd", fn_s5_cmat, PW, 128, cmat_rows, [], [(dwc_re, 512, 0), (dwc_im, 512, 0)], [True] * 2, [])
    disc_cts = [(da_re.reshape(SG, SP), one), (da_im.reshape(SG, SP), one), (dcoef_re.reshape(SG, SP), one), (dcoef_im.reshape(SG, SP), one)]
    dlre, dlim, dls = stage_bwd("s5_disc_bwd", fn_s5_disc, (1,), disc_ins, disc_cts, [("acc", (0,))] * 3)

    g["ln_gain"] = jnp.concatenate([dln0, dln1], axis=0)
    g["w_out"] = jnp.stack([dw_out0, dw_out1])
    g["mem_norm"] = jnp.concatenate([dmem_norm0, dmem_norm1], axis=0)
    g["w_mem_kv"] = jnp.stack([dw_mkv0, dw_mkv1])
    g["xq_norm"] = jnp.concatenate([dxqn0, dxqn1], axis=0)
    g["xk_norm"] = jnp.concatenate([dxk0, dxk1], axis=0)
    g["s5_w_in"] = dw_in0
    g["s5_lambda_re"], g["s5_lambda_im"], g["s5_log_step"] = dlre, dlim, dls
    g["s5_b_re"], g["s5_b_im"], g["s5_c_re"], g["s5_c_im"] = db_re, db_im, dc_re, dc_im
    g["s5_d"] = dd
    g["s5_w_glu"] = dw_glu
    g["mla_w_in"] = mla_in_unpermute(dw_in1)
    g["mla_q_lora_norm"], g["mla_kv_lora_norm"] = dqln, dkvln
    g["mla_w_uq"] = uq_unpermute(dw_uq)
    g["mla_w_ukv"] = dw_ukv
    g["mla_q_nope_norm"], g["mla_k_nope_norm"] = dqnn, dknn
    g["mla_q_rope_norm"] = dqrn[:, :ROPE] + dqrn[:, ROPE:]
    g["mla_k_rope_norm"] = dkrn[:, :ROPE] + dkrn[:, ROPE:]
    return loss, grad_x, g


def kernel(x, mem, positions, ln_gain, w_out, mem_norm, w_mem_kv, xq_norm, xk_norm, s5_w_in, s5_lambda_re, s5_lambda_im, s5_log_step, s5_b_re, s5_b_im, s5_c_re, s5_c_im, s5_d, s5_w_glu, mla_w_in, mla_q_lora_norm, mla_kv_lora_norm, mla_w_uq, mla_w_ukv, mla_q_nope_norm, mla_k_nope_norm, mla_q_rope_norm, mla_k_rope_norm, loss_target, m_ln_gain, m_w_out, m_mem_norm, m_w_mem_kv, m_xq_norm, m_xk_norm, m_s5_w_in, m_s5_lambda_re, m_s5_lambda_im, m_s5_log_step, m_s5_b_re, m_s5_b_im, m_s5_c_re, m_s5_c_im, m_s5_d, m_s5_w_glu, m_mla_w_in, m_mla_q_lora_norm, m_mla_kv_lora_norm, m_mla_w_uq, m_mla_w_ukv, m_mla_q_nope_norm, m_mla_k_nope_norm, m_mla_q_rope_norm, m_mla_k_rope_norm, v_ln_gain, v_w_out, v_mem_norm, v_w_mem_kv, v_xq_norm, v_xk_norm, v_s5_w_in, v_s5_lambda_re, v_s5_lambda_im, v_s5_log_step, v_s5_b_re, v_s5_b_im, v_s5_c_re, v_s5_c_im, v_s5_d, v_s5_w_glu, v_mla_w_in, v_mla_q_lora_norm, v_mla_kv_lora_norm, v_mla_w_uq, v_mla_w_ukv, v_mla_q_nope_norm, v_mla_k_nope_norm, v_mla_q_rope_norm, v_mla_k_rope_norm):
    args = dict(locals())
    wts = {n: args[n] for n in WEIGHT_ORDER}
    mom = {n: args["m_" + n] for n in WEIGHT_ORDER}
    var = {n: args["v_" + n] for n in WEIGHT_ORDER}

    chip = 2 * lax.axis_index("x") + lax.axis_index("y")
    place = jnp.stack([chip, lax.axis_index("c")]).astype(jnp.int32)

    def gather(name, shard):
        return lax.dynamic_update_slice(all_gather_chips(name, shard), shard[None], (chip, 0, 0))

    full = unpack_weights(gather("gather_weights", pack_weight_shard(wts)))
    full["mla_w_in"] = mla_in_permute(full["mla_w_in"])
    full["mla_w_uq"] = uq_permute(full["mla_w_uq"])
    small = {n: wts[n] for n, _ in SMALL}
    small["mla_q_lora_norm"], small["mla_kv_lora_norm"] = full["mla_q_lora_norm"], full["mla_kv_lora_norm"]

    loss, grad_x, g = device_step(x[0], mem[0], positions[0], loss_target[0], full, small)
    loss = lax.psum(loss[0, 0], MESH_AXES)

    big_parts = jnp.concatenate([full_to_shards(n, g[n]).reshape(4, -1) for n, _ in BIG], axis=1)
    small_flat = jnp.concatenate([g[n].reshape(-1) for n, _ in SMALL_FULL])
    small_parts = jnp.pad(small_flat, (0, 4 * SMALL_ROWS * LANES - N_SMALL)).reshape(4, SMALL_ROWS * LANES)
    gbuf = jnp.concatenate([big_parts, small_parts], axis=1).astype(BF16).reshape(4, G_ROWS, LANES)
    red = reduce_scatter_chips(gbuf, place)
    small_all = gather("gather_small_grads", red[BIG_ROWS:]).reshape(-1)[:N_SMALL]

    grads, off = {}, 0
    flat_big = red[:BIG_ROWS].reshape(-1)
    for n, s in BIG:
        grads[n] = flat_big[off:off + math.prod(s)].reshape(s)
        off += math.prod(s)
    off = 0
    shard = 2 * lax.axis_index("x") + lax.axis_index("y")
    for n, s in SMALL_FULL:
        grads[n] = small_all[off:off + math.prod(s)].reshape(s)
        off += math.prod(s)
    for n, s in SHARDED_SMALL:
        grads[n] = lax.dynamic_slice(grads[n], (0, shard * s[1]), s)

    delta, new_m, new_v = {}, {}, {}
    for n, s in BIG:
        two_d = (s[0] * s[1], s[2])
        res = adamw("adamw_" + n, wts[n].reshape(two_d), grads[n].reshape(two_d), mom[n].reshape(two_d), var[n].reshape(two_d))
        delta[n], new_m[n], new_v[n] = (r.reshape(s) for r in res)
    small_names = [n for n, _ in SMALL] + [n for n, _ in SHARDED_SMALL]
    n_own = sum(wts[n].size for n in small_names)
    rows_own = -(-n_own // (256 * 128)) * 256

    def pack_small(d):
        flat = jnp.concatenate([d[n].reshape(-1) for n in small_names])
        return jnp.pad(flat, (0, rows_own * 128 - n_own), constant_values=1.0).reshape(rows_own, 128)

    res = adamw("adamw_small", pack_small(wts), pack_small(grads), pack_small(mom), pack_small(var))
    off = 0
    for n in small_names:
        size = wts[n].size
        delta[n], new_m[n], new_v[n] = (r.reshape(-1)[off:off + size].reshape(wts[n].shape) for r in res)
        off += size

    return (loss, grad_x[None], *[grads[n] for n in WEIGHT_ORDER], *[delta[n] for n in WEIGHT_ORDER],
            *[new_m[n] for n in WEIGHT_ORDER], *[new_v[n] for n in WEIGHT_ORDER])
```

```python
import functools
import math

import jax
import jax.numpy as jnp
from jax import lax
from jax.experimental import pallas as pl
from jax.experimental.pallas import tpu as pltpu

F32, BF16 = jnp.float32, jnp.bfloat16
SDS = jax.ShapeDtypeStruct

D = 1024
L = 2048
ML = 256
BW = 2 * D
XQW = BW // 4
PW = BW - XQW
XH, XHD = 4, 128
SG, SC, SP = 96, 16, 64
SN = SG * SP
NOPE, ROPE, VD = 128, 64, 128
MH = 12
QL, KVL = 512, 256
EPS = 1e-6
ROPE_THETA = 10000.0
MLA_IN = QL + KVL + ROPE + XQW + BW
MLA_IN_P = 3456
ADAM_LR, ADAM_B1, ADAM_B2, ADAM_EPS, ADAM_WD, ADAM_STEP = 0.001, 0.9, 0.999, 1e-08, 0.01, 10

VMEM_LIMIT = 48 * 2**20
SEG = 8
SEG_LEN = L // SEG
MESH_AXES = ("x", "y", "c")


def _cparams():
    return pltpu.CompilerParams(vmem_limit_bytes=VMEM_LIMIT)


def _dg(a, b, ca, cb):
    return lax.dot_general(a.astype(BF16), b.astype(BF16), (((ca,), (cb,)), ((), ())), preferred_element_type=F32)


@jax.custom_vjp
def mm_nn(a, b):
    return _dg(a, b, 1, 0)


mm_nn.defvjp(lambda a, b: (_dg(a, b, 1, 0), (a, b)), lambda res, g: (_dg(g, res[1], 1, 1), _dg(res[0], g, 0, 0)))


@jax.custom_vjp
def mm_nt(a, b):
    return _dg(a, b, 1, 1)


mm_nt.defvjp(lambda a, b: (_dg(a, b, 1, 1), (a, b)), lambda res, g: (_dg(g, res[1], 1, 0), _dg(g, res[0], 0, 0)))


@functools.partial(jax.custom_vjp, nondiff_argnums=(1,))
def lane_roll(x, shift):
    return pltpu.roll(x, shift, 1)


lane_roll.defvjp(lambda x, shift: (pltpu.roll(x, shift, 1), None),
                 lambda shift, _, g: (pltpu.roll(g, (128 - shift) % 128, 1),))


def rms(x, g):
    return x * lax.rsqrt(jnp.mean(x * x, axis=-1, keepdims=True) + EPS) * g


def softmax_rows(s):
    m = lax.stop_gradient(jnp.max(s, axis=-1, keepdims=True))
    e = jnp.exp(s - m)
    return e / jnp.sum(e, axis=-1, keepdims=True)


def silu(x):
    return x * jax.nn.sigmoid(x)


def stage(name, fn, grid, ins, outs):
    n_in = len(ins)

    def kern(*refs):
        res = fn(*[r[...] for r in refs[:n_in]])
        for r, v in zip(refs[n_in:], res):
            r[...] = v.astype(r.dtype)

    return pl.pallas_call(kern, grid=grid, in_specs=[s for _, s in ins], out_specs=[s for _, s in outs],
                          out_shape=[sd for sd, _ in outs], name=name, compiler_params=_cparams())(*[a for a, _ in ins])


def stage_bwd(name, fn, grid, ins, cts, diffs):
    n_in, n_ct = len(ins), len(cts)
    didx = [i for i, d in enumerate(diffs) if d is not None]
    opts = {i: (diffs[i][3] if len(diffs[i]) > 3 else {}) for i in didx if diffs[i][0] == "row"}
    adds = [(i, opts[i]["add"]) for i in opts if "add" in opts[i]]
    intos = [(i, opts[i]["into"]) for i in opts if "into" in opts[i]]
    n_add, n_into = len(adds), len(intos)
    add_pos = {i: n_in + n_ct + k for k, (i, _) in enumerate(adds)}
    n_extra = n_in + n_ct + n_add + n_into

    def kern(*refs):
        vals = [r[...] for r in refs[:n_in]]

        def f(*dv):
            full = list(vals)
            for i, v in zip(didx, dv):
                full[i] = v
            return fn(*full)

        _, vjp = jax.vjp(f, *[vals[i].astype(F32) for i in didx])
        gs = vjp(tuple(c[...].astype(F32) for c in refs[n_in:n_in + n_ct]))
        for o_ref, i, g in zip(refs[n_extra:], didx, gs):
            if diffs[i][0] == "row":
                if i in add_pos:
                    g = g + refs[add_pos[i]][...].astype(F32)
                o_ref[...] = g.astype(o_ref.dtype)
            else:
                first = functools.reduce(jnp.logical_and, [pl.program_id(ax) == 0 for ax in diffs[i][1]])

                @pl.when(first)
                def _():
                    o_ref[...] = g

                @pl.when(jnp.logical_not(first))
                def _():
                    o_ref[...] += g

    out_shape, out_specs = [], []
    for i in didx:
        if diffs[i][0] == "row":
            out_shape.append(diffs[i][1])
            out_specs.append(diffs[i][2])
        else:
            out_shape.append(SDS(ins[i][0].shape, F32))
            out_specs.append(ins[i][1])
    aliases = {n_in + n_ct + n_add + k: didx.index(i) for k, (i, _) in enumerate(intos)}
    in_specs = [s for _, s in ins] + [s for _, s in cts] + [s for _, (_, s) in adds] + [ANY] * n_into
    operands = [a for a, _ in ins] + [a for a, _ in cts] + [a for _, (a, _) in adds] + [a for _, a in intos]
    return pl.pallas_call(kern, grid=grid, in_specs=in_specs, out_specs=out_specs, out_shape=out_shape, input_output_aliases=aliases,
                          name=name, compiler_params=_cparams())(*operands)


def rspec(tl, w, cb=0):
    return pl.BlockSpec((tl, w), lambda i: (i, cb))


def cspec(shape):
    return pl.BlockSpec(shape, lambda i: (0,) * len(shape))


def row_fwd(name, fn, rows, tl, row_ins, consts, outs):
    ins = [(a, rspec(tl, w, cb)) for a, w, cb in row_ins] + [(a, cspec(a.shape)) for a in consts]
    return stage(name, fn, (rows // tl,), ins, [(SDS((rows, w), dt), rspec(tl, w)) for w, dt in outs])


def row_bwd(name, fn, rows, tl, row_ins, consts, cts, row_diff, const_diff):
    ins = [(a, rspec(tl, w, cb)) for a, w, cb in row_ins] + [(a, cspec(a.shape)) for a in consts]
    diffs = []
    for (a, w, cb), d in zip(row_ins, row_diff):
        if not d:
            diffs.append(None)
            continue
        d = d if isinstance(d, dict) else {}
        opts = {}
        if "add" in d:
            opts["add"] = (d["add"][0], rspec(tl, d["add"][1], d["add"][2]))
        if d.get("into") is not None:
            opts["into"] = d["into"]
        diffs.append(("row", SDS((rows, d.get("cols", w)), F32), rspec(tl, w, d.get("cb", 0)), opts))
    diffs += [("acc", (0,)) if d else None for d in const_diff]
    return stage_bwd(name, fn, (rows // tl,), ins, [(a, rspec(tl, w, cb)) for a, w, cb in cts], diffs)


def _pick(dim, prefs):
    for p in prefs:
        if dim % p == 0:
            return p
    return dim


class Sharded:
    def __init__(self, arr, kind, roff, rows):
        self.arr, self.kind, self.roff, self.rows, self.n = arr, kind, roff, rows, arr.shape[2]
        self.shape = (rows, 4 * self.n) if kind == "col" else (4 * rows, self.n)

    def fits(self, t0, t1):
        return self.roff % t0 == 0 and self.rows % t0 == 0 and self.n % t1 == 0

    def spec(self, t0, t1, bidx):
        assert self.fits(t0, t1), (self.kind, self.roff, self.rows, self.n, t0, t1)
        r0 = self.roff // t0
        if self.kind == "col":
            per = self.n // t1
            return pl.BlockSpec((None, t0, t1), lambda *g: (bidx(*g)[1] // per, r0 + bidx(*g)[0], bidx(*g)[1] % per))
        per = self.rows // t0
        return pl.BlockSpec((None, t0, t1), lambda *g: (bidx(*g)[0] // per, r0 + bidx(*g)[0] % per, bidx(*g)[1]))


def matmul(name, a, b, mode, out_dtype=F32, add=None, out=None, into=None):
    if mode == "tn":
        k_dim, m = a.shape
    else:
        m, k_dim = a.shape
    n = b.shape[0] if mode == "nt" else b.shape[1]
    b_fit = b.fits if isinstance(b, Sharded) else (lambda t0, t1: True)
    o_fit = out.fits if out is not None else (lambda t0, t1: True)
    tiles = [(tm, tn, tk) for tm in (512, 256, 128) for tn in (1024, 768, 512, 384, 256, 128) for tk in (512, 384, 256, 128)
             if m % tm == 0 and n % tn == 0 and k_dim % tk == 0 and (b_fit(tn, tk) if mode == "nt" else b_fit(tk, tn)) and o_fit(tm, tn)]
    tm, tn, tk = max(tiles, key=lambda t: (t[0] * t[1], t[2]))
    nk = k_dim // tk
    a_spec = pl.BlockSpec((tk, tm), lambda i, j, k: (k, i)) if mode == "tn" else pl.BlockSpec((tm, tk), lambda i, j, k: (i, k))
    if isinstance(b, Sharded):
        b_spec = b.spec(tn, tk, lambda i, j, k: (j, k)) if mode == "nt" else b.spec(tk, tn, lambda i, j, k: (k, j))
        b = b.arr
    else:
        b_spec = pl.BlockSpec((tn, tk), lambda i, j, k: (j, k)) if mode == "nt" else pl.BlockSpec((tk, tn), lambda i, j, k: (k, j))
    o_spec = pl.BlockSpec((tm, tn), lambda i, j, k: (i, j))
    out_spec, out_shape = (o_spec, SDS((m, n), out_dtype)) if out is None else (out.spec(tm, tn, lambda i, j, k: (i, j)), out.arr)
    ca, cb = {"nn": (1, 0), "nt": (1, 1), "tn": (0, 0)}[mode]
    n_in = 2 + (add is not None)

    def kern(*refs):
        a_ref, b_ref = refs[0], refs[1]
        o_ref, acc = refs[-2], refs[-1]
        k = pl.program_id(2)

        @pl.when(k == 0)
        def _():
            acc[...] = jnp.zeros_like(acc)

        acc[...] += _dg(a_ref[...], b_ref[...], ca, cb)

        @pl.when(k == nk - 1)
        def _():
            r = acc[...]
            if add is not None:
                r = r + refs[2][...]
            o_ref[...] = r.astype(o_ref.dtype)

    ins, specs = [a, b], [a_spec, b_spec]
    if add is not None:
        ins.append(add)
        specs.append(o_spec)
    if into is not None:
        ins.append(into)
        specs.append(ANY)
    return pl.pallas_call(kern, grid=(m // tm, n // tn, nk), in_specs=specs, out_specs=out_spec, out_shape=out_shape,
                          scratch_shapes=[pltpu.VMEM((tm, tn), F32)], input_output_aliases={} if into is None else {n_in: 0},
                          name=name, compiler_params=_cparams())(*ins)


SCAN_LANES = 256


def _cmul(ar, ai, br, bi):
    return ar * br - ai * bi, ar * bi + ai * br


def _sub_shift(x, down):
    row = lax.broadcasted_iota(jnp.int32, x.shape, 0)
    if down:
        return jnp.where(row == 0, 0.0, pltpu.roll(x, 1, 0))
    return jnp.where(row == SEG - 1, 0.0, pltpu.roll(x, SEG - 1, 0))


def _pow_seg_len(ar, ai):
    for _ in range(int(math.log2(SEG_LEN))):
        ar, ai = _cmul(ar, ai, ar, ai)
    return ar, ai


def s5_scan_fwd(bu_re, bu_im, a_re, a_im):
    lanes = SCAN_LANES

    def kern(bur, bui, ar_ref, ai_ref, sr, si):
        ar = jnp.broadcast_to(ar_ref[...], (SEG, lanes))
        ai = jnp.broadcast_to(ai_ref[...], (SEG, lanes))
        zero = jnp.zeros((SEG, lanes), F32)

        def local(i, carry):
            rows = pl.ds(pl.multiple_of(i * SEG, SEG), SEG)
            mr, mi = _cmul(ar, ai, carry[0], carry[1])
            nr, ni = mr + bur[rows, :], mi + bui[rows, :]
            sr[rows, :] = nr
            si[rows, :] = ni
            return nr, ni

        fr, fi = lax.fori_loop(0, SEG_LEN, local, (zero, zero))
        pr, pi = _pow_seg_len(ar, ai)
        ir, ii = zero, zero
        for _ in range(SEG - 1):
            mr, mi = _cmul(pr, pi, ir, ii)
            ir, ii = _sub_shift(mr + fr, True), _sub_shift(mi + fi, True)

        def carry_in(i, pw):
            rows = pl.ds(pl.multiple_of(i * SEG, SEG), SEG)
            cr, ci = _cmul(pw[0], pw[1], ir, ii)
            sr[rows, :] += cr
            si[rows, :] += ci
            return _cmul(pw[0], pw[1], ar, ai)

        lax.fori_loop(0, SEG_LEN, carry_in, (ar, ai))

    blk = pl.BlockSpec((L, lanes), lambda j: (0, j))
    vec = pl.BlockSpec((1, lanes), lambda j: (0, j))
    return pl.pallas_call(kern, grid=(SN // lanes,), in_specs=[blk, blk, vec, vec], out_specs=[blk, blk],
                          out_shape=[SDS((L, SN), F32)] * 2, name="s5_scan_fwd", compiler_params=_cparams())(bu_re, bu_im, a_re, a_im)


def s5_scan_bwd(g_re, g_im, s_re, s_im, a_re, a_im):
    lanes = SCAN_LANES

    def kern(gr, gi, sr, si, ar_ref, ai_ref, lr, li, dar, dai):
        ar = jnp.broadcast_to(ar_ref[...], (SEG, lanes))
        ai = -jnp.broadcast_to(ai_ref[...], (SEG, lanes))
        zero = jnp.zeros((SEG, lanes), F32)

        def local(k, carry):
            i = SEG_LEN - 1 - k
            rows = pl.ds(pl.multiple_of(i * SEG, SEG), SEG)
            mr, mi = _cmul(ar, ai, carry[0], carry[1])
            nr, ni = mr + gr[rows, :], mi + gi[rows, :]
            lr[rows, :] = nr
            li[rows, :] = ni
            return nr, ni

        fr, fi = lax.fori_loop(0, SEG_LEN, local, (zero, zero))
        pr, pi = _pow_seg_len(ar, ai)
        ir, ii = zero, zero
        for _ in range(SEG - 1):
            mr, mi = _cmul(pr, pi, ir, ii)
            ir, ii = _sub_shift(mr + fr, False), _sub_shift(mi + fi, False)

        def fix(rows, pw):
            cr, ci = _cmul(pw[0], pw[1], ir, ii)
            tr, ti = lr[rows, :] + cr, li[rows, :] + ci
            lr[rows, :] = tr
            li[rows, :] = ti
            return tr, ti

        def grad_a(tr, ti, spr, spi, acc):
            return acc[0] + tr * spr + ti * spi, acc[1] + ti * spr - tr * spi

        def carry_in(k, c):
            i = SEG_LEN - 1 - k
            rows = pl.ds(pl.multiple_of(i * SEG, SEG), SEG)
            prev = pl.ds(pl.multiple_of((i - 1) * SEG, SEG), SEG)
            tr, ti = fix(rows, (c[0], c[1]))
            acc = grad_a(tr, ti, sr[prev, :], si[prev, :], (c[2], c[3]))
            nr, ni = _cmul(c[0], c[1], ar, ai)
            return nr, ni, acc[0], acc[1]

        pwr, pwi, accr, acci = lax.fori_loop(0, SEG_LEN - 1, carry_in, (ar, ai, zero, zero))
        tr, ti = fix(pl.ds(0, SEG), (pwr, pwi))
        last = pl.ds((SEG_LEN - 1) * SEG, SEG)
        accr, acci = grad_a(tr, ti, _sub_shift(sr[last, :], True), _sub_shift(si[last, :], True), (accr, acci))
        dar[...] = jnp.sum(accr, axis=0, keepdims=True)
        dai[...] = jnp.sum(acci, axis=0, keepdims=True)

    blk = pl.BlockSpec((L, lanes), lambda j: (0, j))
    vec = pl.BlockSpec((1, lanes), lambda j: (0, j))
    return pl.pallas_call(kern, grid=(SN // lanes,), in_specs=[blk, blk, blk, blk, vec, vec], out_specs=[blk, blk, vec, vec],
                          out_shape=[SDS((L, SN), F32)] * 2 + [SDS((1, SN), F32)] * 2, name="s5_scan_bwd",
                          compiler_params=_cparams())(g_re, g_im, s_re, s_im, a_re, a_im)


def fn_rms(x, g):
    return (rms(x, g),)


def fn_s5_disc(lre, lim, ls):
    step = jnp.exp(ls)
    e = jnp.exp(lre * step)
    a_re, a_im = e * jnp.cos(lim * step), e * jnp.sin(lim * step)
    den = lre * lre + lim * lim
    nr, ni = a_re - 1.0, a_im
    return a_re, a_im, (nr * lre + ni * lim) / den, (ni * lre - nr * lim) / den


def _group_mask(rows, cols, row_div, col_div):
    r = lax.broadcasted_iota(jnp.int32, (rows, cols), 0) // row_div % 8
    c = lax.broadcasted_iota(jnp.int32, (rows, cols), 1) // col_div
    return r == c


def fn_s5_bmat(b_re, b_im, coef_re, coef_im):
    rows = b_re.shape[0]
    mask = _group_mask(rows, 8 * SC, SP, SC)
    bb_re = coef_re * b_re - coef_im * b_im
    bb_im = coef_re * b_im + coef_im * b_re
    return jnp.where(mask, jnp.tile(bb_re, (1, 8)), 0.0), jnp.where(mask, jnp.tile(bb_im, (1, 8)), 0.0)


def fn_s5_cmat(c_re, c_im):
    rows = c_re.shape[0]
    mask = _group_mask(rows, 8 * SP, SC, SP)
    return jnp.where(mask, jnp.tile(c_re, (1, 8)), 0.0), jnp.where(mask, jnp.tile(c_im, (1, 8)), 0.0)


def fn_s5_bu(u, wb_re, wb_im):
    return mm_nt(u, wb_re), mm_nt(u, wb_im)


def fn_s5_out(sr, si, u, d, wc_re, wc_im):
    y = mm_nt(sr, wc_re) - mm_nt(si, wc_im) + d * u
    return (jax.nn.gelu(y),)


def fn_merge_glu(z, mo, gate):
    yg = z[:, :PW] * jax.nn.sigmoid(z[:, PW:])
    return (jnp.concatenate([yg, mo], axis=1) * silu(gate),)


def fn_merge(prim, mo, gate):
    return (jnp.concatenate([prim, mo], axis=1) * silu(gate),)


def fn_mem_k(kv, g):
    return (jnp.concatenate([rms(kv[:, h * XHD:(h + 1) * XHD], g) for h in range(XH)], axis=1),)


def fn_mem_attn(xq, kn, v, g):
    outs = []
    for h in range(XH):
        sl = slice(h * XHD, (h + 1) * XHD)
        p = softmax_rows(mm_nt(rms(xq[:, sl], g), kn[:, sl]) * (XHD ** -0.5))
        outs.append(mm_nn(p, v[:, sl]))
    return (jnp.concatenate(outs, axis=1),)


def _half_rms(x, g):
    lo = lax.broadcasted_iota(jnp.int32, x.shape, 1) < ROPE
    x2 = x * x
    s_lo = jnp.sum(jnp.where(lo, x2, 0.0), axis=1, keepdims=True)
    s_hi = jnp.sum(jnp.where(lo, 0.0, x2), axis=1, keepdims=True)
    return x * lax.rsqrt(jnp.where(lo, s_lo, s_hi) / ROPE + EPS) * g


def _rope(x, cos2, sin_signed):
    first = lax.broadcasted_iota(jnp.int32, x.shape, 1) % ROPE < ROPE // 2
    return x * cos2 + jnp.where(first, lane_roll(x, 128 - ROPE // 2), lane_roll(x, ROPE // 2)) * sin_signed


def fn_mla_prep(q, kv, kr, cos2, sin_signed, qnn, knn, qrn, krn):
    lo = lax.broadcasted_iota(jnp.int32, kr.shape, 1) < ROPE
    kr_pad = jnp.where(lo, _rope(_half_rms(kr, krn), cos2, sin_signed), 0.0)
    qf, kf, vs = [], [], []
    for m in range(MH // 2):
        pair = _rope(_half_rms(q[:, MH * NOPE + 128 * m:MH * NOPE + 128 * (m + 1)], qrn), cos2, sin_signed)
        for h, rope_h in ((2 * m, pair), (2 * m + 1, lane_roll(pair, ROPE))):
            qf.append(jnp.concatenate([rms(q[:, NOPE * h:NOPE * (h + 1)], qnn), jnp.where(lo, rope_h, 0.0)], axis=1))
    for h in range(MH):
        kf.append(jnp.concatenate([rms(kv[:, 256 * h:256 * h + NOPE], knn), kr_pad], axis=1))
        vs.append(kv[:, 256 * h + NOPE:256 * (h + 1)])
    return jnp.stack(qf), jnp.stack(kf), jnp.stack(vs)


ATT_TQ = 256


def fn_causal_attn(q, kf, v):
    s = mm_nt(q, kf) * ((NOPE + ROPE) ** -0.5)
    q_pos = pl.program_id(1) * ATT_TQ + lax.broadcasted_iota(jnp.int32, s.shape, 0)
    k_pos = lax.broadcasted_iota(jnp.int32, s.shape, 1)
    p = softmax_rows(jnp.where(k_pos <= q_pos, s, jnp.finfo(F32).min))
    return (mm_nn(p, v),)


def loss_and_grad(y, target, tl=256):
    def kern(y_ref, t_ref, dy_ref, loss_ref):
        d = y_ref[...] - t_ref[...]
        dy_ref[...] = d / D

        @pl.when(pl.program_id(0) == 0)
        def _():
            loss_ref[...] = jnp.zeros_like(loss_ref)

        loss_ref[...] += 0.5 * jnp.sum(jnp.sum(d * d, axis=1, keepdims=True), axis=0, keepdims=True) / D

    return pl.pallas_call(kern, grid=(L // tl,), in_specs=[rspec(tl, D), rspec(tl, D)], out_specs=[rspec(tl, D), cspec((1, 1))],
                          out_shape=[SDS((L, D), F32), SDS((1, 1), F32)], name="loss", compiler_params=_cparams())(y, target)


def adamw(name, w, g, m, v):
    rows, cols = w.shape
    tr = _row_tile(rows, 256, 8)

    def kern(w_ref, g_ref, m_ref, v_ref, d_ref, nm_ref, nv_ref):
        gg = g_ref[...]
        nm = ADAM_B1 * m_ref[...] + (1.0 - ADAM_B1) * gg
        nv = ADAM_B2 * v_ref[...] + (1.0 - ADAM_B2) * jnp.square(gg)
        m_hat = nm / (1.0 - ADAM_B1 ** ADAM_STEP)
        v_hat = nv / (1.0 - ADAM_B2 ** ADAM_STEP)
        d_ref[...] = -ADAM_LR * (m_hat / (jnp.sqrt(v_hat) + ADAM_EPS) + ADAM_WD * w_ref[...])
        nm_ref[...] = nm
        nv_ref[...] = nv

    spec = rspec(tr, cols)
    return pl.pallas_call(kern, grid=(rows // tr,), in_specs=[spec] * 4, out_specs=[spec] * 3,
                          out_shape=[SDS((rows, cols), F32)] * 3, name=name, compiler_params=_cparams())(w, g, m, v)


def _row_tile(rows, cap=512, unit=16):
    return max(t for t in range(unit, cap + 1, unit) if rows % t == 0)


ANY = pl.BlockSpec(memory_space=pl.ANY)
MESH_ID = pl.DeviceIdType.MESH


def _place():
    x, y, c = lax.axis_index("x"), lax.axis_index("y"), lax.axis_index("c")
    return x, y, c, [(1 - x, y), (x, 1 - y), (1 - x, 1 - y)]


def _row_chunks(rows, n, dtype):
    unit = 32 // jnp.dtype(dtype).itemsize
    base, extra = divmod(rows // unit, n)
    out, start = [], 0
    for k in range(n):
        size = (base + (k < extra)) * unit
        if size:
            out.append((start, size))
            start += size
    assert start == rows, (rows, unit)
    return out


def _dma_sems(n):
    return [pltpu.SemaphoreType.DMA((n,)), pltpu.SemaphoreType.DMA((n,))]


PIECE_BYTES = 1 << 20


def _pieces(shapes_dtypes, rows_of):
    out = []
    for b, (shape, dtype) in enumerate(shapes_dtypes):
        rows = rows_of(shape)
        n = max(1, min(4, rows * shape[-1] * jnp.dtype(dtype).itemsize // PIECE_BYTES))
        out += [(b, st, sz) for st, sz in _row_chunks(rows, n, dtype)]
    return out


def all_gather_chips(name, shards):
    nb = len(shards)
    pieces = _pieces([(s.shape, s.dtype) for s in shards], lambda shape: shape[0] // 2)
    n = len(pieces)

    def body(*refs):
        x_refs, out_refs, send_sems, recv_sems = refs[:nb], refs[nb:2 * nb], refs[2 * nb], refs[2 * nb + 1]
        x, y, c, chips = _place()
        sibling = (x, y, 1 - c)
        mine = 2 * x + y

        def copy(sem, chip, cc, k, to, from_input=False):
            b, st, sz = pieces[k]
            rows_k = pl.ds(cc * (x_refs[b].shape[0] // 2) + st, sz)
            dst = out_refs[b].at[chip, rows_k, :]
            return pltpu.make_async_remote_copy(src_ref=x_refs[b].at[rows_k, :] if from_input else dst, dst_ref=dst,
                                                send_sem=send_sems.at[sem], recv_sem=recv_sems.at[sem], device_id=to, device_id_type=MESH_ID)

        order = [(k, j, 2 * cx + cy, (cx, cy, c)) for k in range(n) for j, (cx, cy) in enumerate(chips)]
        first = [copy(j * n + k, mine, c, k, to, from_input=True) for k, j, _, to in order]
        for cp in first:
            cp.start()
        passed = []
        for k, j, chip, _ in order:
            copy(j * n + k, chip, c, k, sibling).wait_recv()
            passed.append(copy((3 + j) * n + k, chip, c, k, sibling))
            passed[-1].start()
        for k, j, chip, _ in order:
            copy((3 + j) * n + k, chip, 1 - c, k, sibling).wait_recv()
        for cp in first + passed:
            cp.wait_send()

    return pl.pallas_call(body, in_specs=[ANY] * nb, out_specs=[ANY] * nb, out_shape=[SDS((4,) + s.shape, s.dtype) for s in shards],
                          scratch_shapes=_dma_sems(6 * n), name=name)(*shards)


def pair_exchange(name, gs):
    nb = len(gs)
    pieces = _pieces([(g.shape, g.dtype) for g in gs], lambda shape: shape[1] // 2)

    def body(*refs):
        g_refs, got_refs, send_sems, recv_sems = refs[:nb], refs[nb:2 * nb], refs[2 * nb], refs[2 * nb + 1]
        x, y, c, _ = _place()
        swaps = [pltpu.make_async_remote_copy(src_ref=g_refs[b].at[:, pl.ds((1 - c) * (g_refs[b].shape[1] // 2) + st, sz), :],
                                              dst_ref=got_refs[b].at[:, pl.ds(st, sz), :], send_sem=send_sems.at[k], recv_sem=recv_sems.at[k],
                                              device_id=(x, y, 1 - c), device_id_type=MESH_ID)
                 for k, (b, st, sz) in enumerate(pieces)]
        for cp in swaps:
            cp.start()
        for cp in swaps:
            cp.wait()

    return pl.pallas_call(body, in_specs=[ANY] * nb, out_specs=[ANY] * nb,
                          out_shape=[SDS((g.shape[0], g.shape[1] // 2, g.shape[2]), g.dtype) for g in gs],
                          scratch_shapes=_dma_sems(len(pieces)), name=name)(*gs)


def chip_scatter(name, ps):
    nb = len(ps)
    pieces = _pieces([(p.shape, p.dtype) for p in ps], lambda shape: shape[1])
    n = len(pieces)

    def body(*refs):
        p_refs, q_refs, send_sems, recv_sems = refs[:nb], refs[nb:2 * nb], refs[2 * nb], refs[2 * nb + 1]
        x, y, c, chips = _place()
        mine = 2 * x + y

        def copy(j, k, src_slot, dst_slot, to):
            b, st, sz = pieces[k]
            return pltpu.make_async_remote_copy(src_ref=p_refs[b].at[src_slot, pl.ds(st, sz), :], dst_ref=q_refs[b].at[dst_slot, pl.ds(st, sz), :],
                                                send_sem=send_sems.at[j * n + k], recv_sem=recv_sems.at[j * n + k], device_id=to,
                                                device_id_type=MESH_ID)

        order = [(k, j, 2 * cx + cy, (cx, cy, c)) for k in range(n) for j, (cx, cy) in enumerate(chips)]
        sends = [copy(j, k, chip, mine, to) for k, j, chip, to in order]
        for cp in sends:
            cp.start()
        for k, j, chip, to in order:
            copy(j, k, mine, chip, to).wait_recv()
        for cp in sends:
            cp.wait_send()

    return pl.pallas_call(body, in_specs=[ANY] * nb, out_specs=[ANY] * nb, out_shape=[SDS(p.shape, p.dtype) for p in ps],
                          scratch_shapes=_dma_sems(3 * n), name=name)(*ps)


def pair_join(name, bufs):
    nb = len(bufs)
    pieces = _pieces([(b.shape, b.dtype) for b in bufs], lambda shape: shape[0] // 2)

    def body(*refs):
        out_refs, send_sems, recv_sems = refs[nb:2 * nb], refs[2 * nb], refs[2 * nb + 1]
        x, y, c, _ = _place()

        def copy(k, cc):
            b, st, sz = pieces[k]
            rows_k = out_refs[b].at[pl.ds(cc * (out_refs[b].shape[0] // 2) + st, sz), :]
            return pltpu.make_async_remote_copy(src_ref=rows_k, dst_ref=rows_k, send_sem=send_sems.at[k], recv_sem=recv_sems.at[k],
                                                device_id=(x, y, 1 - c), device_id_type=MESH_ID)

        gives = [copy(k, c) for k in range(len(pieces))]
        for cp in gives:
            cp.start()
        for k in range(len(pieces)):
            copy(k, 1 - c).wait_recv()
        for cp in gives:
            cp.wait_send()

    return pl.pallas_call(body, in_specs=[ANY] * nb, out_specs=[ANY] * nb, out_shape=[SDS(b.shape, b.dtype) for b in bufs],
                          input_output_aliases={i: i for i in range(nb)}, scratch_shapes=_dma_sems(len(pieces)), name=name)(*bufs)


def pair_add(name, g, got, place):
    slots, rows, cols = g.shape
    half = rows // 2
    tr = _row_tile(half)
    nb = half // tr

    def kern(_, g_ref, t_ref, o_ref):
        o_ref[...] = (g_ref[...].astype(F32) + t_ref[...].astype(F32)).astype(o_ref.dtype)

    blk = pl.BlockSpec((None, tr, cols), lambda s, i, p: (s, i, 0))
    grid_spec = pltpu.PrefetchScalarGridSpec(
        num_scalar_prefetch=1, grid=(slots, nb),
        in_specs=[pl.BlockSpec((None, tr, cols), lambda s, i, p: (s, p[1] * nb + i, 0)), blk], out_specs=blk)
    return pl.pallas_call(kern, grid_spec=grid_spec, out_shape=SDS((slots, half, cols), g.dtype), name=name,
                          compiler_params=_cparams())(place, g, got)


def chip_add(name, p, q, place):
    slots, half, cols = p.shape
    tr = _row_tile(half)
    nb = half // tr

    def kern(_, p_ref, q1, q2, q3, o_ref):
        o_ref[...] = p_ref[...].astype(F32) + q1[...].astype(F32) + q2[...].astype(F32) + q3[...].astype(F32)

    def slot(k):
        return pl.BlockSpec((None, tr, cols), lambda i, pr: ((pr[0] + k) % slots, i, 0))

    grid_spec = pltpu.PrefetchScalarGridSpec(
        num_scalar_prefetch=1, grid=(nb,), in_specs=[slot(0), slot(1), slot(2), slot(3)],
        out_specs=pl.BlockSpec((tr, cols), lambda i, pr: (pr[1] * nb + i, 0)))
    return pl.pallas_call(kern, grid_spec=grid_spec, out_shape=SDS((2 * half, cols), F32), name=name,
                          compiler_params=_cparams())(place, p, q, q, q)


def reduce_scatter_chips(gs, place):
    gots = pair_exchange("rs_pair_exchange", gs)
    pairs = [pair_add(f"rs_pair_add_{i}", g, got, place) for i, (g, got) in enumerate(zip(gs, gots))]
    qs = chip_scatter("rs_chip_scatter", pairs)
    return pair_join("rs_pair_join", [chip_add(f"rs_chip_add_{i}", p, q, place) for i, (p, q) in enumerate(zip(pairs, qs))])


BIG = [("w_out", (2, 512, 1024)), ("w_mem_kv", (2, 256, 1024)), ("s5_w_in", (1, 1024, 1024)), ("s5_w_glu", (1, 1536, 768)),
       ("mla_w_in", (1, 1024, 848)), ("mla_w_uq", (1, 512, 576)), ("mla_w_ukv", (1, 256, 768))]
SHARDED_SMALL = [("mla_q_lora_norm", (1, 128)), ("mla_kv_lora_norm", (1, 64))]
SMALL = [("ln_gain", (2, 1024)), ("mem_norm", (2, 1024)), ("xq_norm", (2, 128)), ("xk_norm", (2, 128)),
         ("s5_lambda_re", (1, 96, 64)), ("s5_lambda_im", (1, 96, 64)), ("s5_log_step", (1, 96)),
         ("s5_b_re", (1, 96, 64, 16)), ("s5_b_im", (1, 96, 64, 16)), ("s5_c_re", (1, 96, 16, 64)), ("s5_c_im", (1, 96, 16, 64)),
         ("s5_d", (1, 1536)), ("mla_q_nope_norm", (1, 128)), ("mla_k_nope_norm", (1, 128)), ("mla_q_rope_norm", (1, 64)),
         ("mla_k_rope_norm", (1, 64))]
WEIGHT_ORDER = ["ln_gain", "w_out", "mem_norm", "w_mem_kv", "xq_norm", "xk_norm", "s5_w_in", "s5_lambda_re", "s5_lambda_im",
                "s5_log_step", "s5_b_re", "s5_b_im", "s5_c_re", "s5_c_im", "s5_d", "s5_w_glu", "mla_w_in", "mla_q_lora_norm",
                "mla_kv_lora_norm", "mla_w_uq", "mla_w_ukv", "mla_q_nope_norm", "mla_k_nope_norm", "mla_q_rope_norm", "mla_k_rope_norm"]
SMALL_FULL = SMALL + [(n, (1, 4 * s[1])) for n, s in SHARDED_SMALL]
N_SMALL = sum(math.prod(s) for _, s in SMALL_FULL)
SMALL_ROWS, SMALL_LANES = 128, 1024

WIDE_ROWS, MID_ROWS = 2560, 1792
OUT_ROFF, MKV_ROFF, IN0_ROFF = (0, 512), (1024, 1280), 1536
GLU_ROFF, UKV_ROFF = 0, 1536


def weight_views(wide, mid):
    return {"w_out": [Sharded(wide, "row", r, 512) for r in OUT_ROFF], "w_mem_kv": [Sharded(wide, "row", r, 256) for r in MKV_ROFF],
            "s5_w_in": Sharded(wide, "col", IN0_ROFF, 1024), "s5_w_glu": Sharded(mid, "col", GLU_ROFF, 1536),
            "mla_w_ukv": Sharded(mid, "col", UKV_ROFF, 256)}


def stack_shards(w, dtype):
    wide = jnp.concatenate([w["w_out"].reshape(1024, 1024), w["w_mem_kv"].reshape(512, 1024), w["s5_w_in"].reshape(1024, 1024)], axis=0)
    mid = jnp.concatenate([w["s5_w_glu"].reshape(1536, 768), w["mla_w_ukv"].reshape(256, 768)], axis=0)
    return wide.astype(dtype), mid.astype(dtype)


def cols_to_shards(full):
    return full.reshape(full.shape[0], 4, full.shape[1] // 4).transpose(1, 0, 2)


def shards_to_cols(arr):
    return arr.transpose(1, 0, 2).reshape(arr.shape[1], 4 * arr.shape[2])


def mla_in_permute(w):
    o1, o2, o3, o4 = QL, QL + KVL, QL + KVL + ROPE, QL + KVL + ROPE + XQW
    return jnp.concatenate([w[:, o4:], w[:, :o1], w[:, o3:o4], w[:, o1:o2], w[:, o2:o3],
                            jnp.zeros((w.shape[0], MLA_IN_P - MLA_IN), w.dtype)], axis=1)


def mla_in_unpermute(d):
    return jnp.concatenate([d[:, 2048:2560], d[:, 3072:3328], d[:, 3328:3392], d[:, 2560:3072], d[:, :2048]], axis=1)


def uq_permute(w):
    w3 = w.reshape(w.shape[0], MH, NOPE + ROPE)
    return jnp.concatenate([w3[:, :, :NOPE].reshape(w.shape[0], MH * NOPE), w3[:, :, NOPE:].reshape(w.shape[0], MH * ROPE)], axis=1)


def uq_unpermute(d):
    dn = d[:, :MH * NOPE].reshape(d.shape[0], MH, NOPE)
    dr = d[:, MH * NOPE:].reshape(d.shape[0], MH, ROPE)
    return jnp.concatenate([dn, dr], axis=2).reshape(d.shape[0], MH * (NOPE + ROPE))


def time_permute(a):
    return a.reshape(SEG, SEG_LEN, a.shape[-1]).transpose(1, 0, 2).reshape(L, a.shape[-1])


def time_unpermute(a):
    return a.reshape(SEG_LEN, SEG, a.shape[-1]).transpose(1, 0, 2).reshape(L, a.shape[-1])


def mem_branch_fwd(tag, mem, mem_norm, w_mem_kv, xk_norm):
    mn = row_fwd(tag + "_mem_rms", fn_rms, ML, ML, [(mem, D, 0)], [mem_norm], [(D, BF16)])[0]
    kv = matmul(tag + "_mem_kv", mn, w_mem_kv, "nn")
    kn = row_fwd(tag + "_mem_knorm", fn_mem_k, ML, ML, [(kv, XQW, 0)], [xk_norm], [(XQW, F32)])[0]
    return mn, kv, kn


def mem_branch_bwd(tag, mem, mem_norm, w_mem_kv, xk_norm, mn, kv, dkn, dv, g_view, g_wide):
    dk, dxk = row_bwd(tag + "_mem_knorm_bwd", fn_mem_k, ML, ML, [(kv, XQW, 0)], [xk_norm], [(dkn, XQW, 0)], [True], [True])
    dkv = jnp.concatenate([dk, dv], axis=1)
    dmn = matmul(tag + "_mem_kv_dx", dkv, w_mem_kv, "nt")
    g_wide = matmul(tag + "_mem_kv_dw", mn, dkv, "tn", out=g_view, into=g_wide)
    dmem_norm = row_bwd(tag + "_mem_rms_bwd", fn_rms, ML, ML, [(mem, D, 0)], [mem_norm], [(dmn, D, 0)], [False], [True])[0]
    return g_wide, dmem_norm, dxk


def mem_attn_fwd(tag, proj, cb, kn, kv, xq_norm):
    return row_fwd(tag + "_mem_attn", fn_mem_attn, L, 256, [(proj, XQW, cb)], [kn, kv[:, XQW:], xq_norm], [(XQW, F32)])[0]


def mem_attn_bwd(tag, proj, cb, kn, kv, xq_norm, dmo, dproj):
    place = {"cols": proj.shape[1], "cb": cb, "into": dproj}
    return row_bwd(tag + "_mem_attn_bwd", fn_mem_attn, L, 256, [(proj, XQW, cb)], [kn, kv[:, XQW:], xq_norm], [(dmo, XQW, 0)],
                   [place], [True, True, True])


def device_step(x, mem, positions, target, wide, mid, w_in1, w_uq, small):
    g = {}
    w = weight_views(wide, mid)
    gw = weight_views(SDS(wide.shape, BF16), SDS(mid.shape, BF16))
    ln, mem_norm, xq_norm, xk_norm = small["ln_gain"], small["mem_norm"], small["xq_norm"], small["xk_norm"]

    lre, lim = small["s5_lambda_re"][0], small["s5_lambda_im"][0]
    ls = small["s5_log_step"].reshape(SG, 1)
    one = pl.BlockSpec((SG, SP), lambda i: (0, 0))
    col = pl.BlockSpec((SG, 1), lambda i: (0, 0))
    disc_ins = [(lre, one), (lim, one), (ls, col)]
    a_re, a_im, coef_re, coef_im = stage("s5_disc", fn_s5_disc, (1,), disc_ins, [(SDS((SG, SP), F32), one)] * 4)
    b_re, b_im = small["s5_b_re"].reshape(SN, SC), small["s5_b_im"].reshape(SN, SC)
    c_re, c_im = small["s5_c_re"].reshape(PW, SP), small["s5_c_im"].reshape(PW, SP)
    bmat_rows = [(b_re, SC, 0), (b_im, SC, 0), (coef_re.reshape(SN, 1), 1, 0), (coef_im.reshape(SN, 1), 1, 0)]
    wb_re, wb_im = row_fwd("s5_bmat", fn_s5_bmat, SN, 512, bmat_rows, [], [(128, F32)] * 2)
    cmat_rows = [(c_re, SP, 0), (c_im, SP, 0)]
    wc_re, wc_im = row_fwd("s5_cmat", fn_s5_cmat, PW, 128, cmat_rows, [], [(512, F32)] * 2)
    a_re_v, a_im_v = a_re.reshape(1, SN), a_im.reshape(1, SN)
    s5_d = small["s5_d"]

    xp = time_permute(x)
    h0 = row_fwd("l0_rms", fn_rms, L, 256, [(xp, D, 0)], [ln[0:1]], [(D, BF16)])[0]
    proj0 = matmul("l0_in", h0, w["s5_w_in"], "nn")
    ts = 512
    u_spec = pl.BlockSpec((ts, 128), lambda j, i: (i, j))
    wb_spec = pl.BlockSpec((512, 128), lambda j, i: (j, 0))
    s_spec = pl.BlockSpec((ts, 512), lambda j, i: (i, j))
    bu_ins = [(proj0, u_spec), (wb_re, wb_spec), (wb_im, wb_spec)]
    bu_re, bu_im = stage("s5_bu", fn_s5_bu, (12, L // ts), bu_ins, [(SDS((L, SN), F32), s_spec)] * 2)
    s_re, s_im = s5_scan_fwd(bu_re, bu_im, a_re_v, a_im_v)
    d_spec = pl.BlockSpec((1, 128), lambda j, i: (0, j))
    wc_spec = pl.BlockSpec((128, 512), lambda j, i: (j, 0))
    out_ins = [(s_re, s_spec), (s_im, s_spec), (proj0, u_spec), (s5_d, d_spec), (wc_re, wc_spec), (wc_im, wc_spec)]
    g0 = stage("s5_out", fn_s5_out, (12, L // ts), out_ins, [(SDS((L, PW), BF16), u_spec)])[0]
    z0 = matmul("l0_glu", g0, w["s5_w_glu"], "nn")
    mn0, kv0, kn0 = mem_branch_fwd("l0", mem, mem_norm[0:1], w["w_mem_kv"][0], xk_norm[0:1])
    mo0 = mem_attn_fwd("l0", proj0, 3, kn0, kv0, xq_norm[0:1])
    o0 = row_fwd("l0_merge", fn_merge_glu, L, 256, [(z0, 2 * PW, 0), (mo0, XQW, 0), (proj0, BW, 1)], [], [(BW, BF16)])[0]
    x1p = matmul("l0_out", o0, w["w_out"][0], "nn", add=xp)
    x1 = time_unpermute(x1p)

    h1 = row_fwd("l1_rms", fn_rms, L, 256, [(x1, D, 0)], [ln[1:2]], [(D, BF16)])[0]
    proj1 = matmul("l1_in", h1, w_in1, "nn")
    qln, kvln = small["mla_q_lora_norm"].reshape(1, QL), small["mla_kv_lora_norm"].reshape(1, KVL)
    cqn = row_fwd("l1_q_lora_rms", fn_rms, L, 256, [(proj1, QL, 4)], [qln], [(QL, BF16)])[0]
    ckvn = row_fwd("l1_kv_lora_rms", fn_rms, L, 256, [(proj1, KVL, 12)], [kvln], [(KVL, BF16)])[0]
    q = matmul("l1_uq", cqn, w_uq, "nn")
    kv = matmul("l1_ukv", ckvn, w["mla_w_ukv"], "nn")
    inv_freq = ROPE_THETA ** (-jnp.arange(ROPE // 2, dtype=F32) / (ROPE // 2))
    ang = positions.astype(F32)[:, None] * inv_freq
    cos2 = jnp.tile(jnp.cos(ang), (1, 4))
    sin_signed = jnp.tile(jnp.concatenate([-jnp.sin(ang), jnp.sin(ang)], axis=1), (1, 2))
    qnn, knn = small["mla_q_nope_norm"], small["mla_k_nope_norm"]
    qrn, krn = jnp.tile(small["mla_q_rope_norm"], (1, 2)), jnp.tile(small["mla_k_rope_norm"], (1, 2))
    tp = 256
    prep_ins = [(q, rspec(tp, MH * (NOPE + ROPE))), (kv, rspec(tp, MH * 256)), (proj1, rspec(tp, 128, 26)),
                (cos2, rspec(tp, 128)), (sin_signed, rspec(tp, 128))] + [(a, cspec((1, 128))) for a in (qnn, knn, qrn, krn)]
    hq_spec = pl.BlockSpec((MH, tp, 256), lambda i: (0, i, 0))
    hv_spec = pl.BlockSpec((MH, tp, 128), lambda i: (0, i, 0))
    qf, kf, vh = stage("l1_mla_prep", fn_mla_prep, (L // tp,), prep_ins,
                       [(SDS((MH, L, 256), BF16), hq_spec), (SDS((MH, L, 256), BF16), hq_spec), (SDS((MH, L, 128), BF16), hv_spec)])
    aq_spec = pl.BlockSpec((None, ATT_TQ, 256), lambda h, i: (h, i, 0))
    ak_spec = pl.BlockSpec((None, L, 256), lambda h, i: (h, 0, 0))
    av_spec = pl.BlockSpec((None, L, 128), lambda h, i: (h, 0, 0))
    ao_spec = pl.BlockSpec((ATT_TQ, 128), lambda h, i: (i, h))
    att_ins = [(qf, aq_spec), (kf, ak_spec), (vh, av_spec)]
    attn = stage("l1_attn", fn_causal_attn, (MH, L // ATT_TQ), att_ins, [(SDS((L, PW), F32), ao_spec)])[0]
    mn1, kv1, kn1 = mem_branch_fwd("l1", mem, mem_norm[1:2], w["w_mem_kv"][1], xk_norm[1:2])
    mo1 = mem_attn_fwd("l1", proj1, 5, kn1, kv1, xq_norm[1:2])
    o1 = row_fwd("l1_merge", fn_merge, L, 256, [(attn, PW, 0), (mo1, XQW, 0), (proj1, BW, 0)], [], [(BW, BF16)])[0]
    x2 = matmul("l1_out", o1, w["w_out"][1], "nn", add=x1)
    dx2, loss = loss_and_grad(x2, target)

    do1 = matmul("l1_out_dx", dx2, w["w_out"][1], "nt")
    g_wide = matmul("l1_out_dw", o1, dx2, "tn", out=gw["w_out"][1])
    dattn, dmo1, dproj1 = row_bwd("l1_merge_bwd", fn_merge, L, 256, [(attn, PW, 0), (mo1, XQW, 0), (proj1, BW, 0)], [],
                                  [(do1, BW, 0)], [True, True, {"cols": MLA_IN_P, "cb": 0}], [])
    dproj1, dkn1, dv1, dxqn1 = mem_attn_bwd("l1", proj1, 5, kn1, kv1, xq_norm[1:2], dmo1, dproj1)
    g_wide, dmem_norm1, dxk1 = mem_branch_bwd("l1", mem, mem_norm[1:2], w["w_mem_kv"][1], xk_norm[1:2], mn1, kv1, dkn1, dv1,
                                              gw["w_mem_kv"][1], g_wide)
    att_diffs = [("row", SDS((MH, L, 256), F32), aq_spec), ("acc", (1,)), ("acc", (1,))]
    dqf, dkf, dvh = stage_bwd("l1_attn_bwd", fn_causal_attn, (MH, L // ATT_TQ), att_ins, [(dattn, ao_spec)], att_diffs)
    prep_diffs = [("row", SDS((L, MH * (NOPE + ROPE)), F32), rspec(tp, MH * (NOPE + ROPE))), ("row", SDS((L, MH * 256), F32), rspec(tp, MH * 256)),
                  ("row", SDS((L, MLA_IN_P), F32), rspec(tp, 128, 26), {"into": dproj1}), None, None] + [("acc", (0,))] * 4
    dq, dkv, dproj1, dqnn, dknn, dqrn, dkrn = stage_bwd("l1_mla_prep_bwd", fn_mla_prep, (L // tp,), prep_ins,
                                                        [(dqf, hq_spec), (dkf, hq_spec), (dvh, hv_spec)], prep_diffs)
    dcqn = matmul("l1_uq_dx", dq, w_uq, "nt")
    dw_uq = matmul("l1_uq_dw", cqn, dq, "tn")
    dckvn = matmul("l1_ukv_dx", dkv, w["mla_w_ukv"], "nt")
    g_mid = matmul("l1_ukv_dw", ckvn, dkv, "tn", out=gw["mla_w_ukv"])
    dproj1, dqln = row_bwd("l1_q_lora_rms_bwd", fn_rms, L, 256, [(proj1, QL, 4)], [qln], [(dcqn, QL, 0)],
                           [{"cols": MLA_IN_P, "cb": 4, "into": dproj1}], [True])
    dproj1, dkvln = row_bwd("l1_kv_lora_rms_bwd", fn_rms, L, 256, [(proj1, KVL, 12)], [kvln], [(dckvn, KVL, 0)],
                            [{"cols": MLA_IN_P, "cb": 12, "into": dproj1}], [True])
    dh1 = matmul("l1_in_dx", dproj1, w_in1, "nt")
    dw_in1 = matmul("l1_in_dw", h1, dproj1, "tn")
    dx1, dln1 = row_bwd("l1_rms_bwd", fn_rms, L, 256, [(x1, D, 0)], [ln[1:2]], [(dh1, D, 0)], [{"add": (dx2, D, 0)}], [True])
    dx1p = time_permute(dx1)

    do0 = matmul("l0_out_dx", dx1p, w["w_out"][0], "nt")
    g_wide = matmul("l0_out_dw", o0, dx1p, "tn", out=gw["w_out"][0], into=g_wide)
    dz0, dmo0, dproj0 = row_bwd("l0_merge_bwd", fn_merge_glu, L, 256, [(z0, 2 * PW, 0), (mo0, XQW, 0), (proj0, BW, 1)], [],
                                [(do0, BW, 0)], [True, True, {"cols": 2 * BW, "cb": 1}], [])
    dproj0, dkn0, dv0, dxqn0 = mem_attn_bwd("l0", proj0, 3, kn0, kv0, xq_norm[0:1], dmo0, dproj0)
    g_wide, dmem_norm0, dxk0 = mem_branch_bwd("l0", mem, mem_norm[0:1], w["w_mem_kv"][0], xk_norm[0:1], mn0, kv0, dkn0, dv0,
                                              gw["w_mem_kv"][0], g_wide)
    dg0 = matmul("l0_glu_dx", dz0, w["s5_w_glu"], "nt")
    g_mid = matmul("l0_glu_dw", g0, dz0, "tn", out=gw["s5_w_glu"], into=g_mid)
    out_diffs = [("row", SDS((L, SN), F32), s_spec), ("row", SDS((L, SN), F32), s_spec), ("row", SDS((L, PW), F32), u_spec),
                 ("acc", (1,)), ("acc", (1,)), ("acc", (1,))]
    gs_re, gs_im, du_a, dd, dwc_re, dwc_im = stage_bwd("s5_out_bwd", fn_s5_out, (12, L // ts), out_ins, [(dg0, u_spec)], out_diffs)
    dbu_re, dbu_im, da_re, da_im = s5_scan_bwd(gs_re, gs_im, s_re, s_im, a_re_v, a_im_v)
    bu_diffs = [("row", SDS((L, 2 * BW), F32), u_spec, {"add": (du_a, u_spec), "into": dproj0}), ("acc", (1,)), ("acc", (1,))]
    dproj0, dwb_re, dwb_im = stage_bwd("s5_bu_bwd", fn_s5_bu, (12, L // ts), bu_ins, [(dbu_re, s_spec), (dbu_im, s_spec)], bu_diffs)
    dh0 = matmul("l0_in_dx", dproj0, w["s5_w_in"], "nt")
    g_wide = matmul("l0_in_dw", h0, dproj0, "tn", out=gw["s5_w_in"], into=g_wide)
    dxp, dln0 = row_bwd("l0_rms_bwd", fn_rms, L, 256, [(xp, D, 0)], [ln[0:1]], [(dh0, D, 0)], [{"add": (dx1p, D, 0)}], [True])
    grad_x = time_unpermute(dxp)

    db_re, db_im, dcoef_re, dcoef_im = row_bwd("s5_bmat_bwd", fn_s5_bmat, SN, 512, bmat_rows, [], [(dwb_re, 128, 0), (dwb_im, 128, 0)],
                                               [True] * 4, [])
    dc_re, dc_im = row_bwd("s5_cmat_bwd", fn_s5_cmat, PW, 128, cmat_rows, [], [(dwc_re, 512, 0), (dwc_im, 512, 0)], [True] * 2, [])
    disc_cts = [(da_re.reshape(SG, SP), one), (da_im.reshape(SG, SP), one), (dcoef_re.reshape(SG, SP), one), (dcoef_im.reshape(SG, SP), one)]
    dlre, dlim, dls = stage_bwd("s5_disc_bwd", fn_s5_disc, (1,), disc_ins, disc_cts, [("acc", (0,))] * 3)

    g["ln_gain"] = jnp.concatenate([dln0, dln1], axis=0)
    g["mem_norm"] = jnp.concatenate([dmem_norm0, dmem_norm1], axis=0)
    g["xq_norm"] = jnp.concatenate([dxqn0, dxqn1], axis=0)
    g["xk_norm"] = jnp.concatenate([dxk0, dxk1], axis=0)
    g["s5_lambda_re"], g["s5_lambda_im"], g["s5_log_step"] = dlre, dlim, dls
    g["s5_b_re"], g["s5_b_im"], g["s5_c_re"], g["s5_c_im"] = db_re, db_im, dc_re, dc_im
    g["s5_d"] = dd
    g["mla_q_lora_norm"], g["mla_kv_lora_norm"] = dqln, dkvln
    g["mla_q_nope_norm"], g["mla_k_nope_norm"] = dqnn, dknn
    g["mla_q_rope_norm"] = dqrn[:, :ROPE] + dqrn[:, ROPE:]
    g["mla_k_rope_norm"] = dkrn[:, :ROPE] + dkrn[:, ROPE:]
    return loss, grad_x, g_wide, g_mid, dw_in1, dw_uq, g


def kernel(x, mem, positions, ln_gain, w_out, mem_norm, w_mem_kv, xq_norm, xk_norm, s5_w_in, s5_lambda_re, s5_lambda_im, s5_log_step, s5_b_re, s5_b_im, s5_c_re, s5_c_im, s5_d, s5_w_glu, mla_w_in, mla_q_lora_norm, mla_kv_lora_norm, mla_w_uq, mla_w_ukv, mla_q_nope_norm, mla_k_nope_norm, mla_q_rope_norm, mla_k_rope_norm, loss_target, m_ln_gain, m_w_out, m_mem_norm, m_w_mem_kv, m_xq_norm, m_xk_norm, m_s5_w_in, m_s5_lambda_re, m_s5_lambda_im, m_s5_log_step, m_s5_b_re, m_s5_b_im, m_s5_c_re, m_s5_c_im, m_s5_d, m_s5_w_glu, m_mla_w_in, m_mla_q_lora_norm, m_mla_kv_lora_norm, m_mla_w_uq, m_mla_w_ukv, m_mla_q_nope_norm, m_mla_k_nope_norm, m_mla_q_rope_norm, m_mla_k_rope_norm, v_ln_gain, v_w_out, v_mem_norm, v_w_mem_kv, v_xq_norm, v_xk_norm, v_s5_w_in, v_s5_lambda_re, v_s5_lambda_im, v_s5_log_step, v_s5_b_re, v_s5_b_im, v_s5_c_re, v_s5_c_im, v_s5_d, v_s5_w_glu, v_mla_w_in, v_mla_q_lora_norm, v_mla_kv_lora_norm, v_mla_w_uq, v_mla_w_ukv, v_mla_q_nope_norm, v_mla_k_nope_norm, v_mla_q_rope_norm, v_mla_k_rope_norm):
    args = dict(locals())
    wts = {n: args[n] for n in WEIGHT_ORDER}
    mom = {n: args["m_" + n] for n in WEIGHT_ORDER}
    var = {n: args["v_" + n] for n in WEIGHT_ORDER}

    chip = 2 * lax.axis_index("x") + lax.axis_index("y")
    place = jnp.stack([chip, lax.axis_index("c")]).astype(jnp.int32)

    def gather(name, shards):
        return [lax.dynamic_update_slice(g, s[None], (chip, 0, 0)) for g, s in zip(all_gather_chips(name, shards), shards)]

    norm_rows = jnp.concatenate([mla_q_lora_norm, jnp.pad(mla_kv_lora_norm, ((0, 0), (0, 64))), jnp.zeros((14, 128), F32)], axis=0)
    wide, mid, in1, uq, norms = gather("gather_weights", [*stack_shards(wts, BF16), mla_w_in[0].astype(BF16), mla_w_uq[0].astype(BF16), norm_rows])
    small = {n: wts[n] for n, _ in SMALL}
    small["mla_q_lora_norm"], small["mla_kv_lora_norm"] = norms[:, 0, :].reshape(1, QL), norms[:, 1, :64].reshape(1, KVL)

    loss, grad_x, g_wide, g_mid, dw_in1, dw_uq, g = device_step(
        x[0], mem[0], positions[0], loss_target[0], wide, mid, mla_in_permute(shards_to_cols(in1)), uq_permute(shards_to_cols(uq)), small)
    loss = lax.psum(loss[0, 0], MESH_AXES)

    g_in1 = cols_to_shards(mla_in_unpermute(dw_in1)).astype(BF16)
    g_uq = cols_to_shards(uq_unpermute(dw_uq)).astype(BF16)
    small_flat = jnp.concatenate([g[n].reshape(-1) for n, _ in SMALL_FULL])
    g_small = jnp.pad(small_flat, (0, 4 * SMALL_ROWS * SMALL_LANES - N_SMALL)).astype(BF16).reshape(4, SMALL_ROWS, SMALL_LANES)
    r_wide, r_mid, r_in1, r_uq, r_small = reduce_scatter_chips([g_wide, g_mid, g_in1, g_uq, g_small], place)
    small_all = gather("gather_small_grads", [r_small])[0].reshape(-1)[:N_SMALL]

    grads = {"w_out": r_wide[:1024].reshape(2, 512, 1024), "w_mem_kv": r_wide[1024:IN0_ROFF].reshape(2, 256, 1024),
             "s5_w_in": r_wide[IN0_ROFF:][None], "s5_w_glu": r_mid[:UKV_ROFF][None], "mla_w_ukv": r_mid[UKV_ROFF:][None],
             "mla_w_in": r_in1[None], "mla_w_uq": r_uq[None]}
    off = 0
    for n, s in SMALL_FULL:
        grads[n] = small_all[off:off + math.prod(s)].reshape(s)
        off += math.prod(s)
    for n, s in SHARDED_SMALL:
        grads[n] = lax.dynamic_slice(grads[n], (0, chip * s[1]), s)

    delta, new_m, new_v = {}, {}, {}
    own_layout = [(n, (s[0] * s[1], s[2])) for n, s in BIG] + [(n, (s[1] * s[2], s[3])) for n, s in SMALL if len(s) == 4]
    for n, two_d in own_layout:
        res = adamw("adamw_" + n, wts[n].reshape(two_d), grads[n].reshape(two_d), mom[n].reshape(two_d), var[n].reshape(two_d))
        delta[n], new_m[n], new_v[n] = (r.reshape(wts[n].shape) for r in res)
    small_names = [n for n, s in SMALL if len(s) < 4] + [n for n, _ in SHARDED_SMALL]
    n_own = sum(wts[n].size for n in small_names)
    rows_own = -(-n_own // (8 * 128)) * 8

    def pack_small(d):
        flat = jnp.concatenate([d[n].reshape(-1) for n in small_names])
        return jnp.pad(flat, (0, rows_own * 128 - n_own), constant_values=1.0).reshape(rows_own, 128)

    res = adamw("adamw_small", pack_small(wts), pack_small(grads), pack_small(mom), pack_small(var))
    off = 0
    for n in small_names:
        size = wts[n].size
        delta[n], new_m[n], new_v[n] = (r.reshape(-1)[off:off + size].reshape(wts[n].shape) for r in res)
        off += size

    return (loss, grad_x[None], *[grads[n] for n in WEIGHT_ORDER], *[delta[n] for n in WEIGHT_ORDER],
            *[new_m[n] for n in WEIGHT_ORDER], *[new_v[n] for n in WEIGHT_ORDER])
```

```python
import functools
import math

import jax
import jax.numpy as jnp
from jax import lax
from jax.experimental import pallas as pl
from jax.experimental.pallas import tpu as pltpu

F32, BF16 = jnp.float32, jnp.bfloat16
SDS = jax.ShapeDtypeStruct

D = 1024
L = 2048
ML = 256
BW = 2 * D
XQW = BW // 4
PW = BW - XQW
XH, XHD = 4, 128
SG, SC, SP = 96, 16, 64
SN = SG * SP
NOPE, ROPE, VD = 128, 64, 128
MH = 12
QL, KVL = 512, 256
EPS = 1e-6
ROPE_THETA = 10000.0
MLA_IN = QL + KVL + ROPE + XQW + BW
MLA_IN_P = 3456
ADAM_LR, ADAM_B1, ADAM_B2, ADAM_EPS, ADAM_WD, ADAM_STEP = 0.001, 0.9, 0.999, 1e-08, 0.01, 10

VMEM_LIMIT = 48 * 2**20
SEG = 8
SEG_LEN = L // SEG
MESH_AXES = ("x", "y", "c")


def _cparams():
    return pltpu.CompilerParams(vmem_limit_bytes=VMEM_LIMIT)


def _dg(a, b, ca, cb):
    return lax.dot_general(a.astype(BF16), b.astype(BF16), (((ca,), (cb,)), ((), ())), preferred_element_type=F32)


@jax.custom_vjp
def mm_nn(a, b):
    return _dg(a, b, 1, 0)


mm_nn.defvjp(lambda a, b: (_dg(a, b, 1, 0), (a, b)), lambda res, g: (_dg(g, res[1], 1, 1), _dg(res[0], g, 0, 0)))


@jax.custom_vjp
def mm_nt(a, b):
    return _dg(a, b, 1, 1)


mm_nt.defvjp(lambda a, b: (_dg(a, b, 1, 1), (a, b)), lambda res, g: (_dg(g, res[1], 1, 0), _dg(g, res[0], 0, 0)))


@functools.partial(jax.custom_vjp, nondiff_argnums=(1,))
def lane_roll(x, shift):
    return pltpu.roll(x, shift, 1)


lane_roll.defvjp(lambda x, shift: (pltpu.roll(x, shift, 1), None),
                 lambda shift, _, g: (pltpu.roll(g, (128 - shift) % 128, 1),))


def rms(x, g):
    return x * lax.rsqrt(jnp.mean(x * x, axis=-1, keepdims=True) + EPS) * g


def softmax_rows(s):
    m = lax.stop_gradient(jnp.max(s, axis=-1, keepdims=True))
    e = jnp.exp(s - m)
    return e / jnp.sum(e, axis=-1, keepdims=True)


def silu(x):
    return x * jax.nn.sigmoid(x)


def stage(name, fn, grid, ins, outs):
    n_in = len(ins)

    def kern(*refs):
        res = fn(*[r[...] for r in refs[:n_in]])
        for r, v in zip(refs[n_in:], res):
            r[...] = v.astype(r.dtype)

    return pl.pallas_call(kern, grid=grid, in_specs=[s for _, s in ins], out_specs=[s for _, s in outs],
                          out_shape=[sd for sd, _ in outs], name=name, compiler_params=_cparams())(*[a for a, _ in ins])


def stage_bwd(name, fn, grid, ins, cts, diffs):
    n_in, n_ct = len(ins), len(cts)
    didx = [i for i, d in enumerate(diffs) if d is not None]
    opts = {i: (diffs[i][3] if len(diffs[i]) > 3 else {}) for i in didx if diffs[i][0] == "row"}
    adds = [(i, opts[i]["add"]) for i in opts if "add" in opts[i]]
    intos = [(i, opts[i]["into"]) for i in opts if "into" in opts[i]]
    n_add, n_into = len(adds), len(intos)
    add_pos = {i: n_in + n_ct + k for k, (i, _) in enumerate(adds)}
    n_extra = n_in + n_ct + n_add + n_into

    def kern(*refs):
        vals = [r[...] for r in refs[:n_in]]

        def f(*dv):
            full = list(vals)
            for i, v in zip(didx, dv):
                full[i] = v
            return fn(*full)

        _, vjp = jax.vjp(f, *[vals[i].astype(F32) for i in didx])
        gs = vjp(tuple(c[...].astype(F32) for c in refs[n_in:n_in + n_ct]))
        for o_ref, i, g in zip(refs[n_extra:], didx, gs):
            if diffs[i][0] == "row":
                if i in add_pos:
                    g = g + refs[add_pos[i]][...].astype(F32)
                o_ref[...] = g.astype(o_ref.dtype)
            else:
                first = functools.reduce(jnp.logical_and, [pl.program_id(ax) == 0 for ax in diffs[i][1]])

                @pl.when(first)
                def _():
                    o_ref[...] = g

                @pl.when(jnp.logical_not(first))
                def _():
                    o_ref[...] += g

    out_shape, out_specs = [], []
    for i in didx:
        if diffs[i][0] == "row":
            out_shape.append(diffs[i][1])
            out_specs.append(diffs[i][2])
        else:
            out_shape.append(SDS(ins[i][0].shape, F32))
            out_specs.append(ins[i][1])
    aliases = {n_in + n_ct + n_add + k: didx.index(i) for k, (i, _) in enumerate(intos)}
    in_specs = [s for _, s in ins] + [s for _, s in cts] + [s for _, (_, s) in adds] + [ANY] * n_into
    operands = [a for a, _ in ins] + [a for a, _ in cts] + [a for _, (a, _) in adds] + [a for _, a in intos]
    return pl.pallas_call(kern, grid=grid, in_specs=in_specs, out_specs=out_specs, out_shape=out_shape, input_output_aliases=aliases,
                          name=name, compiler_params=_cparams())(*operands)


def rspec(tl, w, cb=0):
    return pl.BlockSpec((tl, w), lambda i: (i, cb))


def cspec(shape):
    return pl.BlockSpec(shape, lambda i: (0,) * len(shape))


def row_fwd(name, fn, rows, tl, row_ins, consts, outs):
    ins = [(a, rspec(tl, w, cb)) for a, w, cb in row_ins] + [(a, cspec(a.shape)) for a in consts]
    return stage(name, fn, (rows // tl,), ins, [(SDS((rows, w), dt), rspec(tl, w)) for w, dt in outs])


def row_bwd(name, fn, rows, tl, row_ins, consts, cts, row_diff, const_diff):
    ins = [(a, rspec(tl, w, cb)) for a, w, cb in row_ins] + [(a, cspec(a.shape)) for a in consts]
    diffs = []
    for (a, w, cb), d in zip(row_ins, row_diff):
        if not d:
            diffs.append(None)
            continue
        d = d if isinstance(d, dict) else {}
        opts = {}
        if "add" in d:
            opts["add"] = (d["add"][0], rspec(tl, d["add"][1], d["add"][2]))
        if d.get("into") is not None:
            opts["into"] = d["into"]
        diffs.append(("row", SDS((rows, d.get("cols", w)), F32), rspec(tl, w, d.get("cb", 0)), opts))
    diffs += [("acc", (0,)) if d else None for d in const_diff]
    return stage_bwd(name, fn, (rows // tl,), ins, [(a, rspec(tl, w, cb)) for a, w, cb in cts], diffs)


MATMUL_VMEM = 28 * 2**20


class Sharded:
    def __init__(self, arr, kind, roff, rows):
        self.arr, self.kind, self.roff, self.rows, self.n = arr, kind, roff, rows, arr.shape[2]
        self.shape = (rows, 4 * self.n) if kind == "col" else (4 * rows, self.n)

    def fits(self, t0, t1):
        return self.roff % t0 == 0 and self.rows % t0 == 0 and self.n % t1 == 0

    def spec(self, t0, t1, bidx):
        assert self.fits(t0, t1), (self.kind, self.roff, self.rows, self.n, t0, t1)
        r0 = self.roff // t0
        if self.kind == "col":
            per = self.n // t1
            return pl.BlockSpec((None, t0, t1), lambda *g: (bidx(*g)[1] // per, r0 + bidx(*g)[0], bidx(*g)[1] % per))
        per = self.rows // t0
        return pl.BlockSpec((None, t0, t1), lambda *g: (bidx(*g)[0] // per, r0 + bidx(*g)[0] % per, bidx(*g)[1]))


def matmul(name, a, b, mode, out_dtype=F32, add=None, out=None, into=None):
    if mode == "tn":
        k_dim, m = a.shape
    else:
        m, k_dim = a.shape
    n = b.shape[0] if mode == "nt" else b.shape[1]
    b_fit = b.fits if isinstance(b, Sharded) else (lambda t0, t1: True)
    o_fit = out.fits if out is not None else (lambda t0, t1: True)
    a_bytes, b_bytes = jnp.dtype(a.dtype).itemsize, jnp.dtype(b.arr.dtype if isinstance(b, Sharded) else b.dtype).itemsize
    o_bytes = jnp.dtype(out_dtype if out is None else out.arr.dtype).itemsize

    def vmem(tm, tn, tk):
        return 2 * (tm * tk * a_bytes + tk * tn * b_bytes + tm * tn * (o_bytes + (4 if add is not None else 0))) + 4 * tm * tn

    tiles = [(tm, tn, tk) for tm in (2048, 1024, 512, 256, 128) for tn in (1024, 768, 512, 384, 256, 128) for tk in (1024, 512, 384, 256, 128)
             if m % tm == 0 and n % tn == 0 and k_dim % tk == 0 and (b_fit(tn, tk) if mode == "nt" else b_fit(tk, tn)) and o_fit(tm, tn)
             and vmem(tm, tn, tk) <= MATMUL_VMEM]
    tm, tn, tk = max(tiles, key=lambda t: (t[0] * t[1] * t[2], t[0] * t[1]))
    nk = k_dim // tk
    a_spec = pl.BlockSpec((tk, tm), lambda i, j, k: (k, i)) if mode == "tn" else pl.BlockSpec((tm, tk), lambda i, j, k: (i, k))
    if isinstance(b, Sharded):
        b_spec = b.spec(tn, tk, lambda i, j, k: (j, k)) if mode == "nt" else b.spec(tk, tn, lambda i, j, k: (k, j))
        b = b.arr
    else:
        b_spec = pl.BlockSpec((tn, tk), lambda i, j, k: (j, k)) if mode == "nt" else pl.BlockSpec((tk, tn), lambda i, j, k: (k, j))
    o_spec = pl.BlockSpec((tm, tn), lambda i, j, k: (i, j))
    out_spec, out_shape = (o_spec, SDS((m, n), out_dtype)) if out is None else (out.spec(tm, tn, lambda i, j, k: (i, j)), out.arr)
    ca, cb = {"nn": (1, 0), "nt": (1, 1), "tn": (0, 0)}[mode]
    n_in = 2 + (add is not None)

    def kern(*refs):
        a_ref, b_ref = refs[0], refs[1]
        o_ref, acc = refs[-2], refs[-1]
        k = pl.program_id(2)

        @pl.when(k == 0)
        def _():
            acc[...] = jnp.zeros_like(acc)

        acc[...] += _dg(a_ref[...], b_ref[...], ca, cb)

        @pl.when(k == nk - 1)
        def _():
            r = acc[...]
            if add is not None:
                r = r + refs[2][...]
            o_ref[...] = r.astype(o_ref.dtype)

    ins, specs = [a, b], [a_spec, b_spec]
    if add is not None:
        ins.append(add)
        specs.append(o_spec)
    if into is not None:
        ins.append(into)
        specs.append(ANY)
    return pl.pallas_call(kern, grid=(m // tm, n // tn, nk), in_specs=specs, out_specs=out_spec, out_shape=out_shape,
                          scratch_shapes=[pltpu.VMEM((tm, tn), F32)], input_output_aliases={} if into is None else {n_in: 0},
                          name=name, compiler_params=_cparams())(*ins)


SCAN_LANES = 256


def _cmul(ar, ai, br, bi):
    return ar * br - ai * bi, ar * bi + ai * br


def _sub_shift(x, down):
    row = lax.broadcasted_iota(jnp.int32, x.shape, 0)
    if down:
        return jnp.where(row == 0, 0.0, pltpu.roll(x, 1, 0))
    return jnp.where(row == SEG - 1, 0.0, pltpu.roll(x, SEG - 1, 0))


def _pow_seg_len(ar, ai):
    for _ in range(int(math.log2(SEG_LEN))):
        ar, ai = _cmul(ar, ai, ar, ai)
    return ar, ai


def s5_scan_fwd(bu_re, bu_im, a_re, a_im):
    lanes = SCAN_LANES

    def kern(bur, bui, ar_ref, ai_ref, sr, si):
        ar = jnp.broadcast_to(ar_ref[...], (SEG, lanes))
        ai = jnp.broadcast_to(ai_ref[...], (SEG, lanes))
        zero = jnp.zeros((SEG, lanes), F32)

        def local(i, carry):
            rows = pl.ds(pl.multiple_of(i * SEG, SEG), SEG)
            mr, mi = _cmul(ar, ai, carry[0], carry[1])
            nr, ni = mr + bur[rows, :], mi + bui[rows, :]
            sr[rows, :] = nr
            si[rows, :] = ni
            return nr, ni

        fr, fi = lax.fori_loop(0, SEG_LEN, local, (zero, zero))
        pr, pi = _pow_seg_len(ar, ai)
        ir, ii = zero, zero
        for _ in range(SEG - 1):
            mr, mi = _cmul(pr, pi, ir, ii)
            ir, ii = _sub_shift(mr + fr, True), _sub_shift(mi + fi, True)

        def carry_in(i, pw):
            rows = pl.ds(pl.multiple_of(i * SEG, SEG), SEG)
            cr, ci = _cmul(pw[0], pw[1], ir, ii)
            sr[rows, :] += cr
            si[rows, :] += ci
            return _cmul(pw[0], pw[1], ar, ai)

        lax.fori_loop(0, SEG_LEN, carry_in, (ar, ai))

    blk = pl.BlockSpec((L, lanes), lambda j: (0, j))
    vec = pl.BlockSpec((1, lanes), lambda j: (0, j))
    return pl.pallas_call(kern, grid=(SN // lanes,), in_specs=[blk, blk, vec, vec], out_specs=[blk, blk],
                          out_shape=[SDS((L, SN), F32)] * 2, name="s5_scan_fwd", compiler_params=_cparams())(bu_re, bu_im, a_re, a_im)


def s5_scan_bwd(g_re, g_im, s_re, s_im, a_re, a_im):
    lanes = SCAN_LANES

    def kern(gr, gi, sr, si, ar_ref, ai_ref, lr, li, dar, dai):
        ar = jnp.broadcast_to(ar_ref[...], (SEG, lanes))
        ai = -jnp.broadcast_to(ai_ref[...], (SEG, lanes))
        zero = jnp.zeros((SEG, lanes), F32)

        def local(k, carry):
            i = SEG_LEN - 1 - k
            rows = pl.ds(pl.multiple_of(i * SEG, SEG), SEG)
            mr, mi = _cmul(ar, ai, carry[0], carry[1])
            nr, ni = mr + gr[rows, :], mi + gi[rows, :]
            lr[rows, :] = nr
            li[rows, :] = ni
            return nr, ni

        fr, fi = lax.fori_loop(0, SEG_LEN, local, (zero, zero))
        pr, pi = _pow_seg_len(ar, ai)
        ir, ii = zero, zero
        for _ in range(SEG - 1):
            mr, mi = _cmul(pr, pi, ir, ii)
            ir, ii = _sub_shift(mr + fr, False), _sub_shift(mi + fi, False)

        def fix(rows, pw):
            cr, ci = _cmul(pw[0], pw[1], ir, ii)
            tr, ti = lr[rows, :] + cr, li[rows, :] + ci
            lr[rows, :] = tr
            li[rows, :] = ti
            return tr, ti

        def grad_a(tr, ti, spr, spi, acc):
            return acc[0] + tr * spr + ti * spi, acc[1] + ti * spr - tr * spi

        def carry_in(k, c):
            i = SEG_LEN - 1 - k
            rows = pl.ds(pl.multiple_of(i * SEG, SEG), SEG)
            prev = pl.ds(pl.multiple_of((i - 1) * SEG, SEG), SEG)
            tr, ti = fix(rows, (c[0], c[1]))
            acc = grad_a(tr, ti, sr[prev, :], si[prev, :], (c[2], c[3]))
            nr, ni = _cmul(c[0], c[1], ar, ai)
            return nr, ni, acc[0], acc[1]

        pwr, pwi, accr, acci = lax.fori_loop(0, SEG_LEN - 1, carry_in, (ar, ai, zero, zero))
        tr, ti = fix(pl.ds(0, SEG), (pwr, pwi))
        last = pl.ds((SEG_LEN - 1) * SEG, SEG)
        accr, acci = grad_a(tr, ti, _sub_shift(sr[last, :], True), _sub_shift(si[last, :], True), (accr, acci))
        dar[...] = jnp.sum(accr, axis=0, keepdims=True)
        dai[...] = jnp.sum(acci, axis=0, keepdims=True)

    blk = pl.BlockSpec((L, lanes), lambda j: (0, j))
    vec = pl.BlockSpec((1, lanes), lambda j: (0, j))
    return pl.pallas_call(kern, grid=(SN // lanes,), in_specs=[blk, blk, blk, blk, vec, vec], out_specs=[blk, blk, vec, vec],
                          out_shape=[SDS((L, SN), F32)] * 2 + [SDS((1, SN), F32)] * 2, name="s5_scan_bwd",
                          compiler_params=_cparams())(g_re, g_im, s_re, s_im, a_re, a_im)


def fn_rms(x, g):
    return (rms(x, g),)


def fn_s5_disc(lre, lim, ls):
    step = jnp.exp(ls)
    e = jnp.exp(lre * step)
    a_re, a_im = e * jnp.cos(lim * step), e * jnp.sin(lim * step)
    den = lre * lre + lim * lim
    nr, ni = a_re - 1.0, a_im
    return a_re, a_im, (nr * lre + ni * lim) / den, (ni * lre - nr * lim) / den


def _group_mask(rows, cols, row_div, col_div):
    r = lax.broadcasted_iota(jnp.int32, (rows, cols), 0) // row_div % 8
    c = lax.broadcasted_iota(jnp.int32, (rows, cols), 1) // col_div
    return r == c


def fn_s5_bmat(b_re, b_im, coef_re, coef_im):
    rows = b_re.shape[0]
    mask = _group_mask(rows, 8 * SC, SP, SC)
    bb_re = coef_re * b_re - coef_im * b_im
    bb_im = coef_re * b_im + coef_im * b_re
    return jnp.where(mask, jnp.tile(bb_re, (1, 8)), 0.0), jnp.where(mask, jnp.tile(bb_im, (1, 8)), 0.0)


def fn_s5_cmat(c_re, c_im):
    rows = c_re.shape[0]
    mask = _group_mask(rows, 8 * SP, SC, SP)
    return jnp.where(mask, jnp.tile(c_re, (1, 8)), 0.0), jnp.where(mask, jnp.tile(c_im, (1, 8)), 0.0)


def fn_s5_bu(u, wb_re, wb_im):
    return mm_nt(u, wb_re), mm_nt(u, wb_im)


def fn_s5_out(sr, si, u, d, wc_re, wc_im):
    y = mm_nt(sr, wc_re) - mm_nt(si, wc_im) + d * u
    return (jax.nn.gelu(y),)


def fn_merge_glu(z, mo, gate):
    yg = z[:, :PW] * jax.nn.sigmoid(z[:, PW:])
    return (jnp.concatenate([yg, mo], axis=1) * silu(gate),)


def fn_merge(prim, mo, gate):
    return (jnp.concatenate([prim, mo], axis=1) * silu(gate),)


def fn_mem_k(kv, g):
    return (jnp.concatenate([rms(kv[:, h * XHD:(h + 1) * XHD], g) for h in range(XH)], axis=1),)


def fn_mem_attn(xq, kn, v, g):
    outs = []
    for h in range(XH):
        sl = slice(h * XHD, (h + 1) * XHD)
        p = softmax_rows(mm_nt(rms(xq[:, sl], g), kn[:, sl]) * (XHD ** -0.5))
        outs.append(mm_nn(p, v[:, sl]))
    return (jnp.concatenate(outs, axis=1),)


def _half_rms(x, g):
    lo = lax.broadcasted_iota(jnp.int32, x.shape, 1) < ROPE
    x2 = x * x
    s_lo = jnp.sum(jnp.where(lo, x2, 0.0), axis=1, keepdims=True)
    s_hi = jnp.sum(jnp.where(lo, 0.0, x2), axis=1, keepdims=True)
    return x * lax.rsqrt(jnp.where(lo, s_lo, s_hi) / ROPE + EPS) * g


def _rope(x, cos2, sin_signed):
    first = lax.broadcasted_iota(jnp.int32, x.shape, 1) % ROPE < ROPE // 2
    return x * cos2 + jnp.where(first, lane_roll(x, 128 - ROPE // 2), lane_roll(x, ROPE // 2)) * sin_signed


def fn_mla_prep(q, kv, kr, cos2, sin_signed, qnn, knn, qrn, krn):
    lo = lax.broadcasted_iota(jnp.int32, kr.shape, 1) < ROPE
    kr_pad = jnp.where(lo, _rope(_half_rms(kr, krn), cos2, sin_signed), 0.0)
    qf, kf, vs = [], [], []
    for m in range(MH // 2):
        pair = _rope(_half_rms(q[:, MH * NOPE + 128 * m:MH * NOPE + 128 * (m + 1)], qrn), cos2, sin_signed)
        for h, rope_h in ((2 * m, pair), (2 * m + 1, lane_roll(pair, ROPE))):
            qf.append(jnp.concatenate([rms(q[:, NOPE * h:NOPE * (h + 1)], qnn), jnp.where(lo, rope_h, 0.0)], axis=1))
    for h in range(MH):
        kf.append(jnp.concatenate([rms(kv[:, 256 * h:256 * h + NOPE], knn), kr_pad], axis=1))
        vs.append(kv[:, 256 * h + NOPE:256 * (h + 1)])
    return jnp.stack(qf), jnp.stack(kf), jnp.stack(vs)


ATT_TQ = 256


def fn_causal_attn(q, kf, v):
    s = mm_nt(q, kf) * ((NOPE + ROPE) ** -0.5)
    q_pos = pl.program_id(1) * ATT_TQ + lax.broadcasted_iota(jnp.int32, s.shape, 0)
    k_pos = lax.broadcasted_iota(jnp.int32, s.shape, 1)
    p = softmax_rows(jnp.where(k_pos <= q_pos, s, jnp.finfo(F32).min))
    return (mm_nn(p, v),)


def loss_and_grad(y, target, tl=256):
    def kern(y_ref, t_ref, dy_ref, loss_ref):
        d = y_ref[...] - t_ref[...]
        dy_ref[...] = d / D

        @pl.when(pl.program_id(0) == 0)
        def _():
            loss_ref[...] = jnp.zeros_like(loss_ref)

        loss_ref[...] += 0.5 * jnp.sum(jnp.sum(d * d, axis=1, keepdims=True), axis=0, keepdims=True) / D

    return pl.pallas_call(kern, grid=(L // tl,), in_specs=[rspec(tl, D), rspec(tl, D)], out_specs=[rspec(tl, D), cspec((1, 1))],
                          out_shape=[SDS((L, D), F32), SDS((1, 1), F32)], name="loss", compiler_params=_cparams())(y, target)


def adamw(name, w, g, m, v):
    rows, cols = w.shape
    tr = _row_tile(rows, 256, 8)

    def kern(w_ref, g_ref, m_ref, v_ref, d_ref, nm_ref, nv_ref):
        gg = g_ref[...]
        nm = ADAM_B1 * m_ref[...] + (1.0 - ADAM_B1) * gg
        nv = ADAM_B2 * v_ref[...] + (1.0 - ADAM_B2) * jnp.square(gg)
        m_hat = nm / (1.0 - ADAM_B1 ** ADAM_STEP)
        v_hat = nv / (1.0 - ADAM_B2 ** ADAM_STEP)
        d_ref[...] = -ADAM_LR * (m_hat / (jnp.sqrt(v_hat) + ADAM_EPS) + ADAM_WD * w_ref[...])
        nm_ref[...] = nm
        nv_ref[...] = nv

    spec = rspec(tr, cols)
    return pl.pallas_call(kern, grid=(rows // tr,), in_specs=[spec] * 4, out_specs=[spec] * 3,
                          out_shape=[SDS((rows, cols), F32)] * 3, name=name, compiler_params=_cparams())(w, g, m, v)


def _row_tile(rows, cap=512, unit=16):
    return max(t for t in range(unit, cap + 1, unit) if rows % t == 0)


ANY = pl.BlockSpec(memory_space=pl.ANY)
MESH_ID = pl.DeviceIdType.MESH


def _place():
    x, y, c = lax.axis_index("x"), lax.axis_index("y"), lax.axis_index("c")
    return x, y, c, [(1 - x, y), (x, 1 - y), (1 - x, 1 - y)]


def _row_chunks(rows, n, dtype):
    unit = 32 // jnp.dtype(dtype).itemsize
    base, extra = divmod(rows // unit, n)
    out, start = [], 0
    for k in range(n):
        size = (base + (k < extra)) * unit
        if size:
            out.append((start, size))
            start += size
    assert start == rows, (rows, unit)
    return out


def _dma_sems(n):
    return [pltpu.SemaphoreType.DMA((n,)), pltpu.SemaphoreType.DMA((n,))]


PIECE_BYTES = 1 << 20


def _pieces(shapes_dtypes, rows_of):
    out = []
    for b, (shape, dtype) in enumerate(shapes_dtypes):
        rows = rows_of(shape)
        n = max(1, min(4, rows * shape[-1] * jnp.dtype(dtype).itemsize // PIECE_BYTES))
        out += [(b, st, sz) for st, sz in _row_chunks(rows, n, dtype)]
    return out


def all_gather_chips(name, shards):
    nb = len(shards)
    pieces = _pieces([(s.shape, s.dtype) for s in shards], lambda shape: shape[0] // 2)
    n = len(pieces)

    def body(*refs):
        x_refs, out_refs, send_sems, recv_sems = refs[:nb], refs[nb:2 * nb], refs[2 * nb], refs[2 * nb + 1]
        x, y, c, chips = _place()
        sibling = (x, y, 1 - c)
        mine = 2 * x + y

        def copy(sem, chip, cc, k, to, from_input=False):
            b, st, sz = pieces[k]
            rows_k = pl.ds(cc * (x_refs[b].shape[0] // 2) + st, sz)
            dst = out_refs[b].at[chip, rows_k, :]
            return pltpu.make_async_remote_copy(src_ref=x_refs[b].at[rows_k, :] if from_input else dst, dst_ref=dst,
                                                send_sem=send_sems.at[sem], recv_sem=recv_sems.at[sem], device_id=to, device_id_type=MESH_ID)

        order = [(k, j, 2 * cx + cy, (cx, cy, c)) for k in range(n) for j, (cx, cy) in enumerate(chips)]
        first = [copy(j * n + k, mine, c, k, to, from_input=True) for k, j, _, to in order]
        for cp in first:
            cp.start()
        passed = []
        for k, j, chip, _ in order:
            copy(j * n + k, chip, c, k, sibling).wait_recv()
            passed.append(copy((3 + j) * n + k, chip, c, k, sibling))
            passed[-1].start()
        for k, j, chip, _ in order:
            copy((3 + j) * n + k, chip, 1 - c, k, sibling).wait_recv()
        for cp in first + passed:
            cp.wait_send()

    return pl.pallas_call(body, in_specs=[ANY] * nb, out_specs=[ANY] * nb, out_shape=[SDS((4,) + s.shape, s.dtype) for s in shards],
                          scratch_shapes=_dma_sems(6 * n), name=name)(*shards)


def pair_exchange(name, gs):
    nb = len(gs)
    pieces = _pieces([(g.shape, g.dtype) for g in gs], lambda shape: shape[1] // 2)

    def body(*refs):
        g_refs, got_refs, send_sems, recv_sems = refs[:nb], refs[nb:2 * nb], refs[2 * nb], refs[2 * nb + 1]
        x, y, c, _ = _place()
        swaps = [pltpu.make_async_remote_copy(src_ref=g_refs[b].at[:, pl.ds((1 - c) * (g_refs[b].shape[1] // 2) + st, sz), :],
                                              dst_ref=got_refs[b].at[:, pl.ds(st, sz), :], send_sem=send_sems.at[k], recv_sem=recv_sems.at[k],
                                              device_id=(x, y, 1 - c), device_id_type=MESH_ID)
                 for k, (b, st, sz) in enumerate(pieces)]
        for cp in swaps:
            cp.start()
        for cp in swaps:
            cp.wait()

    return pl.pallas_call(body, in_specs=[ANY] * nb, out_specs=[ANY] * nb,
                          out_shape=[SDS((g.shape[0], g.shape[1] // 2, g.shape[2]), g.dtype) for g in gs],
                          scratch_shapes=_dma_sems(len(pieces)), name=name)(*gs)


def chip_scatter(name, ps):
    nb = len(ps)
    pieces = _pieces([(p.shape, p.dtype) for p in ps], lambda shape: shape[1])
    n = len(pieces)

    def body(*refs):
        p_refs, q_refs, send_sems, recv_sems = refs[:nb], refs[nb:2 * nb], refs[2 * nb], refs[2 * nb + 1]
        x, y, c, chips = _place()
        mine = 2 * x + y

        def copy(j, k, src_slot, dst_slot, to):
            b, st, sz = pieces[k]
            return pltpu.make_async_remote_copy(src_ref=p_refs[b].at[src_slot, pl.ds(st, sz), :], dst_ref=q_refs[b].at[dst_slot, pl.ds(st, sz), :],
                                                send_sem=send_sems.at[j * n + k], recv_sem=recv_sems.at[j * n + k], device_id=to,
                                                device_id_type=MESH_ID)

        order = [(k, j, 2 * cx + cy, (cx, cy, c)) for k in range(n) for j, (cx, cy) in enumerate(chips)]
        sends = [copy(j, k, chip, mine, to) for k, j, chip, to in order]
        for cp in sends:
            cp.start()
        for k, j, chip, to in order:
            copy(j, k, mine, chip, to).wait_recv()
        for cp in sends:
            cp.wait_send()

    return pl.pallas_call(body, in_specs=[ANY] * nb, out_specs=[ANY] * nb, out_shape=[SDS(p.shape, p.dtype) for p in ps],
                          scratch_shapes=_dma_sems(3 * n), name=name)(*ps)


def pair_join(name, bufs):
    nb = len(bufs)
    pieces = _pieces([(b.shape, b.dtype) for b in bufs], lambda shape: shape[0] // 2)

    def body(*refs):
        out_refs, send_sems, recv_sems = refs[nb:2 * nb], refs[2 * nb], refs[2 * nb + 1]
        x, y, c, _ = _place()

        def copy(k, cc):
            b, st, sz = pieces[k]
            rows_k = out_refs[b].at[pl.ds(cc * (out_refs[b].shape[0] // 2) + st, sz), :]
            return pltpu.make_async_remote_copy(src_ref=rows_k, dst_ref=rows_k, send_sem=send_sems.at[k], recv_sem=recv_sems.at[k],
                                                device_id=(x, y, 1 - c), device_id_type=MESH_ID)

        gives = [copy(k, c) for k in range(len(pieces))]
        for cp in gives:
            cp.start()
        for k in range(len(pieces)):
            copy(k, 1 - c).wait_recv()
        for cp in gives:
            cp.wait_send()

    return pl.pallas_call(body, in_specs=[ANY] * nb, out_specs=[ANY] * nb, out_shape=[SDS(b.shape, b.dtype) for b in bufs],
                          input_output_aliases={i: i for i in range(nb)}, scratch_shapes=_dma_sems(len(pieces)), name=name)(*bufs)


def pair_add(name, g, got, place):
    slots, rows, cols = g.shape
    half = rows // 2
    tr = _row_tile(half)
    nb = half // tr

    def kern(_, g_ref, t_ref, o_ref):
        o_ref[...] = (g_ref[...].astype(F32) + t_ref[...].astype(F32)).astype(o_ref.dtype)

    blk = pl.BlockSpec((None, tr, cols), lambda s, i, p: (s, i, 0))
    grid_spec = pltpu.PrefetchScalarGridSpec(
        num_scalar_prefetch=1, grid=(slots, nb),
        in_specs=[pl.BlockSpec((None, tr, cols), lambda s, i, p: (s, p[1] * nb + i, 0)), blk], out_specs=blk)
    return pl.pallas_call(kern, grid_spec=grid_spec, out_shape=SDS((slots, half, cols), g.dtype), name=name,
                          compiler_params=_cparams())(place, g, got)


def chip_add(name, p, q, place):
    slots, half, cols = p.shape
    tr = _row_tile(half)
    nb = half // tr

    def kern(_, p_ref, q1, q2, q3, o_ref):
        o_ref[...] = p_ref[...].astype(F32) + q1[...].astype(F32) + q2[...].astype(F32) + q3[...].astype(F32)

    def slot(k):
        return pl.BlockSpec((None, tr, cols), lambda i, pr: ((pr[0] + k) % slots, i, 0))

    grid_spec = pltpu.PrefetchScalarGridSpec(
        num_scalar_prefetch=1, grid=(nb,), in_specs=[slot(0), slot(1), slot(2), slot(3)],
        out_specs=pl.BlockSpec((tr, cols), lambda i, pr: (pr[1] * nb + i, 0)))
    return pl.pallas_call(kern, grid_spec=grid_spec, out_shape=SDS((2 * half, cols), F32), name=name,
                          compiler_params=_cparams())(place, p, q, q, q)


def reduce_scatter_chips(gs, place):
    gots = pair_exchange("rs_pair_exchange", gs)
    pairs = [pair_add(f"rs_pair_add_{i}", g, got, place) for i, (g, got) in enumerate(zip(gs, gots))]
    qs = chip_scatter("rs_chip_scatter", pairs)
    return pair_join("rs_pair_join", [chip_add(f"rs_chip_add_{i}", p, q, place) for i, (p, q) in enumerate(zip(pairs, qs))])


BIG = [("w_out", (2, 512, 1024)), ("w_mem_kv", (2, 256, 1024)), ("s5_w_in", (1, 1024, 1024)), ("s5_w_glu", (1, 1536, 768)),
       ("mla_w_in", (1, 1024, 848)), ("mla_w_uq", (1, 512, 576)), ("mla_w_ukv", (1, 256, 768))]
SHARDED_SMALL = [("mla_q_lora_norm", (1, 128)), ("mla_kv_lora_norm", (1, 64))]
SMALL = [("ln_gain", (2, 1024)), ("mem_norm", (2, 1024)), ("xq_norm", (2, 128)), ("xk_norm", (2, 128)),
         ("s5_lambda_re", (1, 96, 64)), ("s5_lambda_im", (1, 96, 64)), ("s5_log_step", (1, 96)),
         ("s5_b_re", (1, 96, 64, 16)), ("s5_b_im", (1, 96, 64, 16)), ("s5_c_re", (1, 96, 16, 64)), ("s5_c_im", (1, 96, 16, 64)),
         ("s5_d", (1, 1536)), ("mla_q_nope_norm", (1, 128)), ("mla_k_nope_norm", (1, 128)), ("mla_q_rope_norm", (1, 64)),
         ("mla_k_rope_norm", (1, 64))]
WEIGHT_ORDER = ["ln_gain", "w_out", "mem_norm", "w_mem_kv", "xq_norm", "xk_norm", "s5_w_in", "s5_lambda_re", "s5_lambda_im",
                "s5_log_step", "s5_b_re", "s5_b_im", "s5_c_re", "s5_c_im", "s5_d", "s5_w_glu", "mla_w_in", "mla_q_lora_norm",
                "mla_kv_lora_norm", "mla_w_uq", "mla_w_ukv", "mla_q_nope_norm", "mla_k_nope_norm", "mla_q_rope_norm", "mla_k_rope_norm"]
SMALL_FULL = SMALL + [(n, (1, 4 * s[1])) for n, s in SHARDED_SMALL]
N_SMALL = sum(math.prod(s) for _, s in SMALL_FULL)
SMALL_ROWS, SMALL_LANES = 128, 1024

WIDE_ROWS, MID_ROWS = 2560, 1792
OUT_ROFF, MKV_ROFF, IN0_ROFF = (0, 512), (1024, 1280), 1536
GLU_ROFF, UKV_ROFF = 0, 1536


def weight_views(wide, mid):
    return {"w_out": [Sharded(wide, "row", r, 512) for r in OUT_ROFF], "w_mem_kv": [Sharded(wide, "row", r, 256) for r in MKV_ROFF],
            "s5_w_in": Sharded(wide, "col", IN0_ROFF, 1024), "s5_w_glu": Sharded(mid, "col", GLU_ROFF, 1536),
            "mla_w_ukv": Sharded(mid, "col", UKV_ROFF, 256)}


def stack_shards(w, dtype):
    wide = jnp.concatenate([w["w_out"].reshape(1024, 1024), w["w_mem_kv"].reshape(512, 1024), w["s5_w_in"].reshape(1024, 1024)], axis=0)
    mid = jnp.concatenate([w["s5_w_glu"].reshape(1536, 768), w["mla_w_ukv"].reshape(256, 768)], axis=0)
    return wide.astype(dtype), mid.astype(dtype)


def cols_to_shards(full):
    return full.reshape(full.shape[0], 4, full.shape[1] // 4).transpose(1, 0, 2)


def shards_to_cols(arr):
    return arr.transpose(1, 0, 2).reshape(arr.shape[1], 4 * arr.shape[2])


def mla_in_permute(w):
    o1, o2, o3, o4 = QL, QL + KVL, QL + KVL + ROPE, QL + KVL + ROPE + XQW
    return jnp.concatenate([w[:, o4:], w[:, :o1], w[:, o3:o4], w[:, o1:o2], w[:, o2:o3],
                            jnp.zeros((w.shape[0], MLA_IN_P - MLA_IN), w.dtype)], axis=1)


def mla_in_unpermute(d):
    return jnp.concatenate([d[:, 2048:2560], d[:, 3072:3328], d[:, 3328:3392], d[:, 2560:3072], d[:, :2048]], axis=1)


def uq_permute(w):
    w3 = w.reshape(w.shape[0], MH, NOPE + ROPE)
    return jnp.concatenate([w3[:, :, :NOPE].reshape(w.shape[0], MH * NOPE), w3[:, :, NOPE:].reshape(w.shape[0], MH * ROPE)], axis=1)


def uq_unpermute(d):
    dn = d[:, :MH * NOPE].reshape(d.shape[0], MH, NOPE)
    dr = d[:, MH * NOPE:].reshape(d.shape[0], MH, ROPE)
    return jnp.concatenate([dn, dr], axis=2).reshape(d.shape[0], MH * (NOPE + ROPE))


def time_permute(a):
    return a.reshape(SEG, SEG_LEN, a.shape[-1]).transpose(1, 0, 2).reshape(L, a.shape[-1])


def time_unpermute(a):
    return a.reshape(SEG_LEN, SEG, a.shape[-1]).transpose(1, 0, 2).reshape(L, a.shape[-1])


def mem_branch_fwd(tag, mem, mem_norm, w_mem_kv, xk_norm):
    mn = row_fwd(tag + "_mem_rms", fn_rms, ML, ML, [(mem, D, 0)], [mem_norm], [(D, BF16)])[0]
    kv = matmul(tag + "_mem_kv", mn, w_mem_kv, "nn")
    kn = row_fwd(tag + "_mem_knorm", fn_mem_k, ML, ML, [(kv, XQW, 0)], [xk_norm], [(XQW, F32)])[0]
    return mn, kv, kn


def mem_branch_bwd(tag, mem, mem_norm, w_mem_kv, xk_norm, mn, kv, dkn, dv, g_view, g_wide):
    dk, dxk = row_bwd(tag + "_mem_knorm_bwd", fn_mem_k, ML, ML, [(kv, XQW, 0)], [xk_norm], [(dkn, XQW, 0)], [True], [True])
    dkv = jnp.concatenate([dk, dv], axis=1)
    dmn = matmul(tag + "_mem_kv_dx", dkv, w_mem_kv, "nt")
    g_wide = matmul(tag + "_mem_kv_dw", mn, dkv, "tn", out=g_view, into=g_wide)
    dmem_norm = row_bwd(tag + "_mem_rms_bwd", fn_rms, ML, ML, [(mem, D, 0)], [mem_norm], [(dmn, D, 0)], [False], [True])[0]
    return g_wide, dmem_norm, dxk


def mem_attn_fwd(tag, proj, cb, kn, kv, xq_norm):
    return row_fwd(tag + "_mem_attn", fn_mem_attn, L, 256, [(proj, XQW, cb)], [kn, kv[:, XQW:], xq_norm], [(XQW, F32)])[0]


def mem_attn_bwd(tag, proj, cb, kn, kv, xq_norm, dmo, dproj):
    place = {"cols": proj.shape[1], "cb": cb, "into": dproj}
    return row_bwd(tag + "_mem_attn_bwd", fn_mem_attn, L, 256, [(proj, XQW, cb)], [kn, kv[:, XQW:], xq_norm], [(dmo, XQW, 0)],
                   [place], [True, True, True])


def device_step(x, mem, positions, target, wide, mid, w_in1, w_uq, small):
    g = {}
    w = weight_views(wide, mid)
    gw = weight_views(SDS(wide.shape, BF16), SDS(mid.shape, BF16))
    ln, mem_norm, xq_norm, xk_norm = small["ln_gain"], small["mem_norm"], small["xq_norm"], small["xk_norm"]

    lre, lim = small["s5_lambda_re"][0], small["s5_lambda_im"][0]
    ls = small["s5_log_step"].reshape(SG, 1)
    one = pl.BlockSpec((SG, SP), lambda i: (0, 0))
    col = pl.BlockSpec((SG, 1), lambda i: (0, 0))
    disc_ins = [(lre, one), (lim, one), (ls, col)]
    a_re, a_im, coef_re, coef_im = stage("s5_disc", fn_s5_disc, (1,), disc_ins, [(SDS((SG, SP), F32), one)] * 4)
    b_re, b_im = small["s5_b_re"].reshape(SN, SC), small["s5_b_im"].reshape(SN, SC)
    c_re, c_im = small["s5_c_re"].reshape(PW, SP), small["s5_c_im"].reshape(PW, SP)
    bmat_rows = [(b_re, SC, 0), (b_im, SC, 0), (coef_re.reshape(SN, 1), 1, 0), (coef_im.reshape(SN, 1), 1, 0)]
    wb_re, wb_im = row_fwd("s5_bmat", fn_s5_bmat, SN, 512, bmat_rows, [], [(128, F32)] * 2)
    cmat_rows = [(c_re, SP, 0), (c_im, SP, 0)]
    wc_re, wc_im = row_fwd("s5_cmat", fn_s5_cmat, PW, 128, cmat_rows, [], [(512, F32)] * 2)
    a_re_v, a_im_v = a_re.reshape(1, SN), a_im.reshape(1, SN)
    s5_d = small["s5_d"]

    xp = time_permute(x)
    h0 = row_fwd("l0_rms", fn_rms, L, 256, [(xp, D, 0)], [ln[0:1]], [(D, BF16)])[0]
    proj0 = matmul("l0_in", h0, w["s5_w_in"], "nn")
    ts = 512
    u_spec = pl.BlockSpec((ts, 128), lambda j, i: (i, j))
    wb_spec = pl.BlockSpec((512, 128), lambda j, i: (j, 0))
    s_spec = pl.BlockSpec((ts, 512), lambda j, i: (i, j))
    bu_ins = [(proj0, u_spec), (wb_re, wb_spec), (wb_im, wb_spec)]
    bu_re, bu_im = stage("s5_bu", fn_s5_bu, (12, L // ts), bu_ins, [(SDS((L, SN), F32), s_spec)] * 2)
    s_re, s_im = s5_scan_fwd(bu_re, bu_im, a_re_v, a_im_v)
    d_spec = pl.BlockSpec((1, 128), lambda j, i: (0, j))
    wc_spec = pl.BlockSpec((128, 512), lambda j, i: (j, 0))
    out_ins = [(s_re, s_spec), (s_im, s_spec), (proj0, u_spec), (s5_d, d_spec), (wc_re, wc_spec), (wc_im, wc_spec)]
    g0 = stage("s5_out", fn_s5_out, (12, L // ts), out_ins, [(SDS((L, PW), BF16), u_spec)])[0]
    z0 = matmul("l0_glu", g0, w["s5_w_glu"], "nn")
    mn0, kv0, kn0 = mem_branch_fwd("l0", mem, mem_norm[0:1], w["w_mem_kv"][0], xk_norm[0:1])
    mo0 = mem_attn_fwd("l0", proj0, 3, kn0, kv0, xq_norm[0:1])
    o0 = row_fwd("l0_merge", fn_merge_glu, L, 256, [(z0, 2 * PW, 0), (mo0, XQW, 0), (proj0, BW, 1)], [], [(BW, BF16)])[0]
    x1p = matmul("l0_out", o0, w["w_out"][0], "nn", add=xp)
    x1 = time_unpermute(x1p)

    h1 = row_fwd("l1_rms", fn_rms, L, 256, [(x1, D, 0)], [ln[1:2]], [(D, BF16)])[0]
    proj1 = matmul("l1_in", h1, w_in1, "nn")
    qln, kvln = small["mla_q_lora_norm"].reshape(1, QL), small["mla_kv_lora_norm"].reshape(1, KVL)
    cqn = row_fwd("l1_q_lora_rms", fn_rms, L, 256, [(proj1, QL, 4)], [qln], [(QL, BF16)])[0]
    ckvn = row_fwd("l1_kv_lora_rms", fn_rms, L, 256, [(proj1, KVL, 12)], [kvln], [(KVL, BF16)])[0]
    q = matmul("l1_uq", cqn, w_uq, "nn")
    kv = matmul("l1_ukv", ckvn, w["mla_w_ukv"], "nn")
    inv_freq = ROPE_THETA ** (-jnp.arange(ROPE // 2, dtype=F32) / (ROPE // 2))
    ang = positions.astype(F32)[:, None] * inv_freq
    cos2 = jnp.tile(jnp.cos(ang), (1, 4))
    sin_signed = jnp.tile(jnp.concatenate([-jnp.sin(ang), jnp.sin(ang)], axis=1), (1, 2))
    qnn, knn = small["mla_q_nope_norm"], small["mla_k_nope_norm"]
    qrn, krn = jnp.tile(small["mla_q_rope_norm"], (1, 2)), jnp.tile(small["mla_k_rope_norm"], (1, 2))
    tp = 256
    prep_ins = [(q, rspec(tp, MH * (NOPE + ROPE))), (kv, rspec(tp, MH * 256)), (proj1, rspec(tp, 128, 26)),
                (cos2, rspec(tp, 128)), (sin_signed, rspec(tp, 128))] + [(a, cspec((1, 128))) for a in (qnn, knn, qrn, krn)]
    hq_spec = pl.BlockSpec((MH, tp, 256), lambda i: (0, i, 0))
    hv_spec = pl.BlockSpec((MH, tp, 128), lambda i: (0, i, 0))
    qf, kf, vh = stage("l1_mla_prep", fn_mla_prep, (L // tp,), prep_ins,
                       [(SDS((MH, L, 256), BF16), hq_spec), (SDS((MH, L, 256), BF16), hq_spec), (SDS((MH, L, 128), BF16), hv_spec)])
    aq_spec = pl.BlockSpec((None, ATT_TQ, 256), lambda h, i: (h, i, 0))
    ak_spec = pl.BlockSpec((None, L, 256), lambda h, i: (h, 0, 0))
    av_spec = pl.BlockSpec((None, L, 128), lambda h, i: (h, 0, 0))
    ao_spec = pl.BlockSpec((ATT_TQ, 128), lambda h, i: (i, h))
    att_ins = [(qf, aq_spec), (kf, ak_spec), (vh, av_spec)]
    attn = stage("l1_attn", fn_causal_attn, (MH, L // ATT_TQ), att_ins, [(SDS((L, PW), F32), ao_spec)])[0]
    mn1, kv1, kn1 = mem_branch_fwd("l1", mem, mem_norm[1:2], w["w_mem_kv"][1], xk_norm[1:2])
    mo1 = mem_attn_fwd("l1", proj1, 5, kn1, kv1, xq_norm[1:2])
    o1 = row_fwd("l1_merge", fn_merge, L, 256, [(attn, PW, 0), (mo1, XQW, 0), (proj1, BW, 0)], [], [(BW, BF16)])[0]
    x2 = matmul("l1_out", o1, w["w_out"][1], "nn", add=x1)
    dx2, loss = loss_and_grad(x2, target)

    do1 = matmul("l1_out_dx", dx2, w["w_out"][1], "nt")
    g_wide = matmul("l1_out_dw", o1, dx2, "tn", out=gw["w_out"][1])
    dattn, dmo1, dproj1 = row_bwd("l1_merge_bwd", fn_merge, L, 256, [(attn, PW, 0), (mo1, XQW, 0), (proj1, BW, 0)], [],
                                  [(do1, BW, 0)], [True, True, {"cols": MLA_IN_P, "cb": 0}], [])
    dproj1, dkn1, dv1, dxqn1 = mem_attn_bwd("l1", proj1, 5, kn1, kv1, xq_norm[1:2], dmo1, dproj1)
    g_wide, dmem_norm1, dxk1 = mem_branch_bwd("l1", mem, mem_norm[1:2], w["w_mem_kv"][1], xk_norm[1:2], mn1, kv1, dkn1, dv1,
                                              gw["w_mem_kv"][1], g_wide)
    att_diffs = [("row", SDS((MH, L, 256), F32), aq_spec), ("acc", (1,)), ("acc", (1,))]
    dqf, dkf, dvh = stage_bwd("l1_attn_bwd", fn_causal_attn, (MH, L // ATT_TQ), att_ins, [(dattn, ao_spec)], att_diffs)
    prep_diffs = [("row", SDS((L, MH * (NOPE + ROPE)), F32), rspec(tp, MH * (NOPE + ROPE))), ("row", SDS((L, MH * 256), F32), rspec(tp, MH * 256)),
                  ("row", SDS((L, MLA_IN_P), F32), rspec(tp, 128, 26), {"into": dproj1}), None, None] + [("acc", (0,))] * 4
    dq, dkv, dproj1, dqnn, dknn, dqrn, dkrn = stage_bwd("l1_mla_prep_bwd", fn_mla_prep, (L // tp,), prep_ins,
                                                        [(dqf, hq_spec), (dkf, hq_spec), (dvh, hv_spec)], prep_diffs)
    dcqn = matmul("l1_uq_dx", dq, w_uq, "nt")
    dw_uq = matmul("l1_uq_dw", cqn, dq, "tn")
    dckvn = matmul("l1_ukv_dx", dkv, w["mla_w_ukv"], "nt")
    g_mid = matmul("l1_ukv_dw", ckvn, dkv, "tn", out=gw["mla_w_ukv"])
    dproj1, dqln = row_bwd("l1_q_lora_rms_bwd", fn_rms, L, 256, [(proj1, QL, 4)], [qln], [(dcqn, QL, 0)],
                           [{"cols": MLA_IN_P, "cb": 4, "into": dproj1}], [True])
    dproj1, dkvln = row_bwd("l1_kv_lora_rms_bwd", fn_rms, L, 256, [(proj1, KVL, 12)], [kvln], [(dckvn, KVL, 0)],
                            [{"cols": MLA_IN_P, "cb": 12, "into": dproj1}], [True])
    dh1 = matmul("l1_in_dx", dproj1, w_in1, "nt")
    dw_in1 = matmul("l1_in_dw", h1, dproj1, "tn")
    dx1, dln1 = row_bwd("l1_rms_bwd", fn_rms, L, 256, [(x1, D, 0)], [ln[1:2]], [(dh1, D, 0)], [{"add": (dx2, D, 0)}], [True])
    dx1p = time_permute(dx1)

    do0 = matmul("l0_out_dx", dx1p, w["w_out"][0], "nt")
    g_wide = matmul("l0_out_dw", o0, dx1p, "tn", out=gw["w_out"][0], into=g_wide)
    dz0, dmo0, dproj0 = row_bwd("l0_merge_bwd", fn_merge_glu, L, 256, [(z0, 2 * PW, 0), (mo0, XQW, 0), (proj0, BW, 1)], [],
                                [(do0, BW, 0)], [True, True, {"cols": 2 * BW, "cb": 1}], [])
    dproj0, dkn0, dv0, dxqn0 = mem_attn_bwd("l0", proj0, 3, kn0, kv0, xq_norm[0:1], dmo0, dproj0)
    g_wide, dmem_norm0, dxk0 = mem_branch_bwd("l0", mem, mem_norm[0:1], w["w_mem_kv"][0], xk_norm[0:1], mn0, kv0, dkn0, dv0,
                                              gw["w_mem_kv"][0], g_wide)
    dg0 = matmul("l0_glu_dx", dz0, w["s5_w_glu"], "nt")
    g_mid = matmul("l0_glu_dw", g0, dz0, "tn", out=gw["s5_w_glu"], into=g_mid)
    out_diffs = [("row", SDS((L, SN), F32), s_spec), ("row", SDS((L, SN), F32), s_spec), ("row", SDS((L, PW), F32), u_spec),
                 ("acc", (1,)), ("acc", (1,)), ("acc", (1,))]
    gs_re, gs_im, du_a, dd, dwc_re, dwc_im = stage_bwd("s5_out_bwd", fn_s5_out, (12, L // ts), out_ins, [(dg0, u_spec)], out_diffs)
    dbu_re, dbu_im, da_re, da_im = s5_scan_bwd(gs_re, gs_im, s_re, s_im, a_re_v, a_im_v)
    bu_diffs = [("row", SDS((L, 2 * BW), F32), u_spec, {"add": (du_a, u_spec), "into": dproj0}), ("acc", (1,)), ("acc", (1,))]
    dproj0, dwb_re, dwb_im = stage_bwd("s5_bu_bwd", fn_s5_bu, (12, L // ts), bu_ins, [(dbu_re, s_spec), (dbu_im, s_spec)], bu_diffs)
    dh0 = matmul("l0_in_dx", dproj0, w["s5_w_in"], "nt")
    g_wide = matmul("l0_in_dw", h0, dproj0, "tn", out=gw["s5_w_in"], into=g_wide)
    dxp, dln0 = row_bwd("l0_rms_bwd", fn_rms, L, 256, [(xp, D, 0)], [ln[0:1]], [(dh0, D, 0)], [{"add": (dx1p, D, 0)}], [True])
    grad_x = time_unpermute(dxp)

    db_re, db_im, dcoef_re, dcoef_im = row_bwd("s5_bmat_bwd", fn_s5_bmat, SN, 512, bmat_rows, [], [(dwb_re, 128, 0), (dwb_im, 128, 0)],
                                               [True] * 4, [])
    dc_re, dc_im = row_bwd("s5_cmat_bwd", fn_s5_cmat, PW, 128, cmat_rows, [], [(dwc_re, 512, 0), (dwc_im, 512, 0)], [True] * 2, [])
    disc_cts = [(da_re.reshape(SG, SP), one), (da_im.reshape(SG, SP), one), (dcoef_re.reshape(SG, SP), one), (dcoef_im.reshape(SG, SP), one)]
    dlre, dlim, dls = stage_bwd("s5_disc_bwd", fn_s5_disc, (1,), disc_ins, disc_cts, [("acc", (0,))] * 3)

    g["ln_gain"] = jnp.concatenate([dln0, dln1], axis=0)
    g["mem_norm"] = jnp.concatenate([dmem_norm0, dmem_norm1], axis=0)
    g["xq_norm"] = jnp.concatenate([dxqn0, dxqn1], axis=0)
    g["xk_norm"] = jnp.concatenate([dxk0, dxk1], axis=0)
    g["s5_lambda_re"], g["s5_lambda_im"], g["s5_log_step"] = dlre, dlim, dls
    g["s5_b_re"], g["s5_b_im"], g["s5_c_re"], g["s5_c_im"] = db_re, db_im, dc_re, dc_im
    g["s5_d"] = dd
    g["mla_q_lora_norm"], g["mla_kv_lora_norm"] = dqln, dkvln
    g["mla_q_nope_norm"], g["mla_k_nope_norm"] = dqnn, dknn
    g["mla_q_rope_norm"] = dqrn[:, :ROPE] + dqrn[:, ROPE:]
    g["mla_k_rope_norm"] = dkrn[:, :ROPE] + dkrn[:, ROPE:]
    return loss, grad_x, g_wide, g_mid, dw_in1, dw_uq, g


def kernel(x, mem, positions, ln_gain, w_out, mem_norm, w_mem_kv, xq_norm, xk_norm, s5_w_in, s5_lambda_re, s5_lambda_im, s5_log_step, s5_b_re, s5_b_im, s5_c_re, s5_c_im, s5_d, s5_w_glu, mla_w_in, mla_q_lora_norm, mla_kv_lora_norm, mla_w_uq, mla_w_ukv, mla_q_nope_norm, mla_k_nope_norm, mla_q_rope_norm, mla_k_rope_norm, loss_target, m_ln_gain, m_w_out, m_mem_norm, m_w_mem_kv, m_xq_norm, m_xk_norm, m_s5_w_in, m_s5_lambda_re, m_s5_lambda_im, m_s5_log_step, m_s5_b_re, m_s5_b_im, m_s5_c_re, m_s5_c_im, m_s5_d, m_s5_w_glu, m_mla_w_in, m_mla_q_lora_norm, m_mla_kv_lora_norm, m_mla_w_uq, m_mla_w_ukv, m_mla_q_nope_norm, m_mla_k_nope_norm, m_mla_q_rope_norm, m_mla_k_rope_norm, v_ln_gain, v_w_out, v_mem_norm, v_w_mem_kv, v_xq_norm, v_xk_norm, v_s5_w_in, v_s5_lambda_re, v_s5_lambda_im, v_s5_log_step, v_s5_b_re, v_s5_b_im, v_s5_c_re, v_s5_c_im, v_s5_d, v_s5_w_glu, v_mla_w_in, v_mla_q_lora_norm, v_mla_kv_lora_norm, v_mla_w_uq, v_mla_w_ukv, v_mla_q_nope_norm, v_mla_k_nope_norm, v_mla_q_rope_norm, v_mla_k_rope_norm):
    args = dict(locals())
    wts = {n: args[n] for n in WEIGHT_ORDER}
    mom = {n: args["m_" + n] for n in WEIGHT_ORDER}
    var = {n: args["v_" + n] for n in WEIGHT_ORDER}

    chip = 2 * lax.axis_index("x") + lax.axis_index("y")
    place = jnp.stack([chip, lax.axis_index("c")]).astype(jnp.int32)

    def gather(name, shards):
        return [lax.dynamic_update_slice(g, s[None], (chip, 0, 0)) for g, s in zip(all_gather_chips(name, shards), shards)]

    norm_rows = jnp.concatenate([mla_q_lora_norm, jnp.pad(mla_kv_lora_norm, ((0, 0), (0, 64))), jnp.zeros((14, 128), F32)], axis=0)
    wide, mid, in1, uq, norms = gather("gather_weights", [*stack_shards(wts, BF16), mla_w_in[0].astype(BF16), mla_w_uq[0].astype(BF16), norm_rows])
    small = {n: wts[n] for n, _ in SMALL}
    small["mla_q_lora_norm"], small["mla_kv_lora_norm"] = norms[:, 0, :].reshape(1, QL), norms[:, 1, :64].reshape(1, KVL)

    loss, grad_x, g_wide, g_mid, dw_in1, dw_uq, g = device_step(
        x[0], mem[0], positions[0], loss_target[0], wide, mid, mla_in_permute(shards_to_cols(in1)), uq_permute(shards_to_cols(uq)), small)
    loss = lax.psum(loss[0, 0], MESH_AXES)

    g_in1 = cols_to_shards(mla_in_unpermute(dw_in1)).astype(BF16)
    g_uq = cols_to_shards(uq_unpermute(dw_uq)).astype(BF16)
    small_flat = jnp.concatenate([g[n].reshape(-1) for n, _ in SMALL_FULL])
    g_small = jnp.pad(small_flat, (0, 4 * SMALL_ROWS * SMALL_LANES - N_SMALL)).astype(BF16).reshape(4, SMALL_ROWS, SMALL_LANES)
    r_wide, r_mid, r_in1, r_uq, r_small = reduce_scatter_chips([g_wide, g_mid, g_in1, g_uq, g_small], place)
    small_all = gather("gather_small_grads", [r_small])[0].reshape(-1)[:N_SMALL]

    grads = {"w_out": r_wide[:1024].reshape(2, 512, 1024), "w_mem_kv": r_wide[1024:IN0_ROFF].reshape(2, 256, 1024),
             "s5_w_in": r_wide[IN0_ROFF:][None], "s5_w_glu": r_mid[:UKV_ROFF][None], "mla_w_ukv": r_mid[UKV_ROFF:][None],
             "mla_w_in": r_in1[None], "mla_w_uq": r_uq[None]}
    off = 0
    for n, s in SMALL_FULL:
        grads[n] = small_all[off:off + math.prod(s)].reshape(s)
        off += math.prod(s)
    for n, s in SHARDED_SMALL:
        grads[n] = lax.dynamic_slice(grads[n], (0, chip * s[1]), s)

    delta, new_m, new_v = {}, {}, {}
    own_layout = [(n, (s[0] * s[1], s[2])) for n, s in BIG] + [(n, (s[1] * s[2], s[3])) for n, s in SMALL if len(s) == 4]
    for n, two_d in own_layout:
        res = adamw("adamw_" + n, wts[n].reshape(two_d), grads[n].reshape(two_d), mom[n].reshape(two_d), var[n].reshape(two_d))
        delta[n], new_m[n], new_v[n] = (r.reshape(wts[n].shape) for r in res)
    small_names = [n for n, s in SMALL if len(s) < 4] + [n for n, _ in SHARDED_SMALL]
    n_own = sum(wts[n].size for n in small_names)
    rows_own = -(-n_own // (8 * 128)) * 8

    def pack_small(d):
        flat = jnp.concatenate([d[n].reshape(-1) for n in small_names])
        return jnp.pad(flat, (0, rows_own * 128 - n_own), constant_values=1.0).reshape(rows_own, 128)

    res = adamw("adamw_small", pack_small(wts), pack_small(grads), pack_small(mom), pack_small(var))
    off = 0
    for n in small_names:
        size = wts[n].size
        delta[n], new_m[n], new_v[n] = (r.reshape(-1)[off:off + size].reshape(wts[n].shape) for r in res)
        off += size

    return (loss, grad_x[None], *[grads[n] for n in WEIGHT_ORDER], *[delta[n] for n in WEIGHT_ORDER],
            *[new_m[n] for n in WEIGHT_ORDER], *[new_v[n] for n in WEIGHT_ORDER])
```

```python
import functools
import math

import jax
import jax.numpy as jnp
from jax import lax
from jax.experimental import pallas as pl
from jax.experimental.pallas import tpu as pltpu

F32, BF16 = jnp.float32, jnp.bfloat16
SDS = jax.ShapeDtypeStruct

D = 1024
L = 2048
ML = 256
BW = 2 * D
XQW = BW // 4
PW = BW - XQW
XH, XHD = 4, 128
SG, SC, SP = 96, 16, 64
SN = SG * SP
NOPE, ROPE, VD = 128, 64, 128
MH = 12
QL, KVL = 512, 256
EPS = 1e-6
ROPE_THETA = 10000.0
MLA_IN = QL + KVL + ROPE + XQW + BW
MLA_IN_P = 3456
ADAM_LR, ADAM_B1, ADAM_B2, ADAM_EPS, ADAM_WD, ADAM_STEP = 0.001, 0.9, 0.999, 1e-08, 0.01, 10

VMEM_LIMIT = 48 * 2**20
SEG = 8
SEG_LEN = L // SEG
MESH_AXES = ("x", "y", "c")


def _cparams():
    return pltpu.CompilerParams(vmem_limit_bytes=VMEM_LIMIT)


def _dg(a, b, ca, cb):
    return lax.dot_general(a.astype(BF16), b.astype(BF16), (((ca,), (cb,)), ((), ())), preferred_element_type=F32)


@jax.custom_vjp
def mm_nn(a, b):
    return _dg(a, b, 1, 0)


mm_nn.defvjp(lambda a, b: (_dg(a, b, 1, 0), (a, b)), lambda res, g: (_dg(g, res[1], 1, 1), _dg(res[0], g, 0, 0)))


@jax.custom_vjp
def mm_nt(a, b):
    return _dg(a, b, 1, 1)


mm_nt.defvjp(lambda a, b: (_dg(a, b, 1, 1), (a, b)), lambda res, g: (_dg(g, res[1], 1, 0), _dg(g, res[0], 0, 0)))


@functools.partial(jax.custom_vjp, nondiff_argnums=(1,))
def lane_roll(x, shift):
    return pltpu.roll(x, shift, 1)


lane_roll.defvjp(lambda x, shift: (pltpu.roll(x, shift, 1), None),
                 lambda shift, _, g: (pltpu.roll(g, (128 - shift) % 128, 1),))


def rms(x, g):
    return x * lax.rsqrt(jnp.mean(x * x, axis=-1, keepdims=True) + EPS) * g


def softmax_rows(s):
    m = lax.stop_gradient(jnp.max(s, axis=-1, keepdims=True))
    e = jnp.exp(s - m)
    return e / jnp.sum(e, axis=-1, keepdims=True)


def silu(x):
    return x * jax.nn.sigmoid(x)


def stage(name, fn, grid, ins, outs):
    n_in = len(ins)

    def kern(*refs):
        res = fn(*[r[...] for r in refs[:n_in]])
        for r, v in zip(refs[n_in:], res):
            r[...] = v.astype(r.dtype)

    return pl.pallas_call(kern, grid=grid, in_specs=[s for _, s in ins], out_specs=[s for _, s in outs],
                          out_shape=[sd for sd, _ in outs], name=name, compiler_params=_cparams())(*[a for a, _ in ins])


def stage_bwd(name, fn, grid, ins, cts, diffs):
    n_in, n_ct = len(ins), len(cts)
    didx = [i for i, d in enumerate(diffs) if d is not None]
    opts = {i: (diffs[i][3] if len(diffs[i]) > 3 else {}) for i in didx if diffs[i][0] == "row"}
    adds = [(i, opts[i]["add"]) for i in opts if "add" in opts[i]]
    intos = [(i, opts[i]["into"]) for i in opts if "into" in opts[i]]
    n_add, n_into = len(adds), len(intos)
    add_pos = {i: n_in + n_ct + k for k, (i, _) in enumerate(adds)}
    n_extra = n_in + n_ct + n_add + n_into

    def kern(*refs):
        vals = [r[...] for r in refs[:n_in]]

        def f(*dv):
            full = list(vals)
            for i, v in zip(didx, dv):
                full[i] = v
            return fn(*full)

        _, vjp = jax.vjp(f, *[vals[i].astype(F32) for i in didx])
        gs = vjp(tuple(c[...].astype(F32) for c in refs[n_in:n_in + n_ct]))
        for o_ref, i, g in zip(refs[n_extra:], didx, gs):
            if diffs[i][0] == "row":
                if i in add_pos:
                    g = g + refs[add_pos[i]][...].astype(F32)
                o_ref[...] = g.astype(o_ref.dtype)
            else:
                first = functools.reduce(jnp.logical_and, [pl.program_id(ax) == 0 for ax in diffs[i][1]])

                @pl.when(first)
                def _():
                    o_ref[...] = g

                @pl.when(jnp.logical_not(first))
                def _():
                    o_ref[...] += g

    out_shape, out_specs = [], []
    for i in didx:
        if diffs[i][0] == "row":
            out_shape.append(diffs[i][1])
            out_specs.append(diffs[i][2])
        else:
            out_shape.append(SDS(ins[i][0].shape, F32))
            out_specs.append(ins[i][1])
    aliases = {n_in + n_ct + n_add + k: didx.index(i) for k, (i, _) in enumerate(intos)}
    in_specs = [s for _, s in ins] + [s for _, s in cts] + [s for _, (_, s) in adds] + [ANY] * n_into
    operands = [a for a, _ in ins] + [a for a, _ in cts] + [a for _, (a, _) in adds] + [a for _, a in intos]
    return pl.pallas_call(kern, grid=grid, in_specs=in_specs, out_specs=out_specs, out_shape=out_shape, input_output_aliases=aliases,
                          name=name, compiler_params=_cparams())(*operands)


def rspec(tl, w, cb=0):
    return pl.BlockSpec((tl, w), lambda i: (i, cb))


def cspec(shape):
    return pl.BlockSpec(shape, lambda i: (0,) * len(shape))


def row_fwd(name, fn, rows, tl, row_ins, consts, outs):
    ins = [(a, rspec(tl, w, cb)) for a, w, cb in row_ins] + [(a, cspec(a.shape)) for a in consts]
    return stage(name, fn, (rows // tl,), ins, [(SDS((rows, w), dt), rspec(tl, w)) for w, dt in outs])


def row_bwd(name, fn, rows, tl, row_ins, consts, cts, row_diff, const_diff):
    ins = [(a, rspec(tl, w, cb)) for a, w, cb in row_ins] + [(a, cspec(a.shape)) for a in consts]
    diffs = []
    for (a, w, cb), d in zip(row_ins, row_diff):
        if not d:
            diffs.append(None)
            continue
        d = d if isinstance(d, dict) else {}
        opts = {}
        if "add" in d:
            opts["add"] = (d["add"][0], rspec(tl, d["add"][1], d["add"][2]))
        if d.get("into") is not None:
            opts["into"] = d["into"]
        diffs.append(("row", SDS((rows, d.get("cols", w)), F32), rspec(tl, w, d.get("cb", 0)), opts))
    diffs += [("acc", (0,)) if d else None for d in const_diff]
    return stage_bwd(name, fn, (rows // tl,), ins, [(a, rspec(tl, w, cb)) for a, w, cb in cts], diffs)


MATMUL_VMEM = 28 * 2**20
ADAMW_VMEM = 16 * 2**20


class Sharded:
    def __init__(self, arr, kind, roff, rows):
        self.arr, self.kind, self.roff, self.rows, self.n = arr, kind, roff, rows, arr.shape[2]
        self.shape = (rows, 4 * self.n) if kind == "col" else (4 * rows, self.n)

    def fits(self, t0, t1):
        return self.roff % t0 == 0 and self.rows % t0 == 0 and self.n % t1 == 0

    def spec(self, t0, t1, bidx):
        assert self.fits(t0, t1), (self.kind, self.roff, self.rows, self.n, t0, t1)
        r0 = self.roff // t0
        if self.kind == "col":
            per = self.n // t1
            return pl.BlockSpec((None, t0, t1), lambda *g: (bidx(*g)[1] // per, r0 + bidx(*g)[0], bidx(*g)[1] % per))
        per = self.rows // t0
        return pl.BlockSpec((None, t0, t1), lambda *g: (bidx(*g)[0] // per, r0 + bidx(*g)[0] % per, bidx(*g)[1]))


def matmul(name, a, b, mode, out_dtype=F32, add=None, out=None, into=None):
    if mode == "tn":
        k_dim, m = a.shape
    else:
        m, k_dim = a.shape
    n = b.shape[0] if mode == "nt" else b.shape[1]
    b_fit = b.fits if isinstance(b, Sharded) else (lambda t0, t1: True)
    o_fit = out.fits if out is not None else (lambda t0, t1: True)
    a_bytes, b_bytes = jnp.dtype(a.dtype).itemsize, jnp.dtype(b.arr.dtype if isinstance(b, Sharded) else b.dtype).itemsize
    o_bytes = jnp.dtype(out_dtype if out is None else out.arr.dtype).itemsize

    def vmem(tm, tn, tk):
        return 2 * (tm * tk * a_bytes + tk * tn * b_bytes + tm * tn * (o_bytes + (4 if add is not None else 0))) + 4 * tm * tn

    tiles = [(tm, tn, tk) for tm in (2048, 1024, 512, 256, 128) for tn in (1024, 768, 512, 384, 256, 128) for tk in (1024, 512, 384, 256, 128)
             if m % tm == 0 and n % tn == 0 and k_dim % tk == 0 and (b_fit(tn, tk) if mode == "nt" else b_fit(tk, tn)) and o_fit(tm, tn)
             and vmem(tm, tn, tk) <= MATMUL_VMEM]
    tm, tn, tk = max(tiles, key=lambda t: (t[0] * t[1] * t[2], t[0] * t[1]))
    nk = k_dim // tk
    a_spec = pl.BlockSpec((tk, tm), lambda i, j, k: (k, i)) if mode == "tn" else pl.BlockSpec((tm, tk), lambda i, j, k: (i, k))
    if isinstance(b, Sharded):
        b_spec = b.spec(tn, tk, lambda i, j, k: (j, k)) if mode == "nt" else b.spec(tk, tn, lambda i, j, k: (k, j))
        b = b.arr
    else:
        b_spec = pl.BlockSpec((tn, tk), lambda i, j, k: (j, k)) if mode == "nt" else pl.BlockSpec((tk, tn), lambda i, j, k: (k, j))
    o_spec = pl.BlockSpec((tm, tn), lambda i, j, k: (i, j))
    out_spec, out_shape = (o_spec, SDS((m, n), out_dtype)) if out is None else (out.spec(tm, tn, lambda i, j, k: (i, j)), out.arr)
    ca, cb = {"nn": (1, 0), "nt": (1, 1), "tn": (0, 0)}[mode]
    n_in = 2 + (add is not None)

    def kern(*refs):
        a_ref, b_ref = refs[0], refs[1]
        o_ref, acc = refs[-2], refs[-1]
        k = pl.program_id(2)

        @pl.when(k == 0)
        def _():
            acc[...] = jnp.zeros_like(acc)

        acc[...] += _dg(a_ref[...], b_ref[...], ca, cb)

        @pl.when(k == nk - 1)
        def _():
            r = acc[...]
            if add is not None:
                r = r + refs[2][...]
            o_ref[...] = r.astype(o_ref.dtype)

    ins, specs = [a, b], [a_spec, b_spec]
    if add is not None:
        ins.append(add)
        specs.append(o_spec)
    if into is not None:
        ins.append(into)
        specs.append(ANY)
    return pl.pallas_call(kern, grid=(m // tm, n // tn, nk), in_specs=specs, out_specs=out_spec, out_shape=out_shape,
                          scratch_shapes=[pltpu.VMEM((tm, tn), F32)], input_output_aliases={} if into is None else {n_in: 0},
                          name=name, compiler_params=_cparams())(*ins)


SCAN_LANES = 256


def _cmul(ar, ai, br, bi):
    return ar * br - ai * bi, ar * bi + ai * br


def _sub_shift(x, down):
    row = lax.broadcasted_iota(jnp.int32, x.shape, 0)
    if down:
        return jnp.where(row == 0, 0.0, pltpu.roll(x, 1, 0))
    return jnp.where(row == SEG - 1, 0.0, pltpu.roll(x, SEG - 1, 0))


def _pow_seg_len(ar, ai):
    for _ in range(int(math.log2(SEG_LEN))):
        ar, ai = _cmul(ar, ai, ar, ai)
    return ar, ai


def s5_scan_fwd(bu_re, bu_im, a_re, a_im):
    lanes = SCAN_LANES

    def kern(bur, bui, ar_ref, ai_ref, sr, si):
        ar = jnp.broadcast_to(ar_ref[...], (SEG, lanes))
        ai = jnp.broadcast_to(ai_ref[...], (SEG, lanes))
        zero = jnp.zeros((SEG, lanes), F32)

        def local(i, carry):
            rows = pl.ds(pl.multiple_of(i * SEG, SEG), SEG)
            mr, mi = _cmul(ar, ai, carry[0], carry[1])
            nr, ni = mr + bur[rows, :], mi + bui[rows, :]
            sr[rows, :] = nr
            si[rows, :] = ni
            return nr, ni

        fr, fi = lax.fori_loop(0, SEG_LEN, local, (zero, zero))
        pr, pi = _pow_seg_len(ar, ai)
        ir, ii = zero, zero
        for _ in range(SEG - 1):
            mr, mi = _cmul(pr, pi, ir, ii)
            ir, ii = _sub_shift(mr + fr, True), _sub_shift(mi + fi, True)

        def carry_in(i, pw):
            rows = pl.ds(pl.multiple_of(i * SEG, SEG), SEG)
            cr, ci = _cmul(pw[0], pw[1], ir, ii)
            sr[rows, :] += cr
            si[rows, :] += ci
            return _cmul(pw[0], pw[1], ar, ai)

        lax.fori_loop(0, SEG_LEN, carry_in, (ar, ai))

    blk = pl.BlockSpec((L, lanes), lambda j: (0, j))
    vec = pl.BlockSpec((1, lanes), lambda j: (0, j))
    return pl.pallas_call(kern, grid=(SN // lanes,), in_specs=[blk, blk, vec, vec], out_specs=[blk, blk],
                          out_shape=[SDS((L, SN), F32)] * 2, name="s5_scan_fwd", compiler_params=_cparams())(bu_re, bu_im, a_re, a_im)


def s5_scan_bwd(g_re, g_im, s_re, s_im, a_re, a_im):
    lanes = SCAN_LANES

    def kern(gr, gi, sr, si, ar_ref, ai_ref, lr, li, dar, dai):
        ar = jnp.broadcast_to(ar_ref[...], (SEG, lanes))
        ai = -jnp.broadcast_to(ai_ref[...], (SEG, lanes))
        zero = jnp.zeros((SEG, lanes), F32)

        def local(k, carry):
            i = SEG_LEN - 1 - k
            rows = pl.ds(pl.multiple_of(i * SEG, SEG), SEG)
            mr, mi = _cmul(ar, ai, carry[0], carry[1])
            nr, ni = mr + gr[rows, :], mi + gi[rows, :]
            lr[rows, :] = nr
            li[rows, :] = ni
            return nr, ni

        fr, fi = lax.fori_loop(0, SEG_LEN, local, (zero, zero))
        pr, pi = _pow_seg_len(ar, ai)
        ir, ii = zero, zero
        for _ in range(SEG - 1):
            mr, mi = _cmul(pr, pi, ir, ii)
            ir, ii = _sub_shift(mr + fr, False), _sub_shift(mi + fi, False)

        def fix(rows, pw):
            cr, ci = _cmul(pw[0], pw[1], ir, ii)
            tr, ti = lr[rows, :] + cr, li[rows, :] + ci
            lr[rows, :] = tr
            li[rows, :] = ti
            return tr, ti

        def grad_a(tr, ti, spr, spi, acc):
            return acc[0] + tr * spr + ti * spi, acc[1] + ti * spr - tr * spi

        def carry_in(k, c):
            i = SEG_LEN - 1 - k
            rows = pl.ds(pl.multiple_of(i * SEG, SEG), SEG)
            prev = pl.ds(pl.multiple_of((i - 1) * SEG, SEG), SEG)
            tr, ti = fix(rows, (c[0], c[1]))
            acc = grad_a(tr, ti, sr[prev, :], si[prev, :], (c[2], c[3]))
            nr, ni = _cmul(c[0], c[1], ar, ai)
            return nr, ni, acc[0], acc[1]

        pwr, pwi, accr, acci = lax.fori_loop(0, SEG_LEN - 1, carry_in, (ar, ai, zero, zero))
        tr, ti = fix(pl.ds(0, SEG), (pwr, pwi))
        last = pl.ds((SEG_LEN - 1) * SEG, SEG)
        accr, acci = grad_a(tr, ti, _sub_shift(sr[last, :], True), _sub_shift(si[last, :], True), (accr, acci))
        dar[...] = jnp.sum(accr, axis=0, keepdims=True)
        dai[...] = jnp.sum(acci, axis=0, keepdims=True)

    blk = pl.BlockSpec((L, lanes), lambda j: (0, j))
    vec = pl.BlockSpec((1, lanes), lambda j: (0, j))
    return pl.pallas_call(kern, grid=(SN // lanes,), in_specs=[blk, blk, blk, blk, vec, vec], out_specs=[blk, blk, vec, vec],
                          out_shape=[SDS((L, SN), F32)] * 2 + [SDS((1, SN), F32)] * 2, name="s5_scan_bwd",
                          compiler_params=_cparams())(g_re, g_im, s_re, s_im, a_re, a_im)


def fn_rms(x, g):
    return (rms(x, g),)


def fn_s5_disc(lre, lim, ls):
    step = jnp.exp(ls)
    e = jnp.exp(lre * step)
    a_re, a_im = e * jnp.cos(lim * step), e * jnp.sin(lim * step)
    den = lre * lre + lim * lim
    nr, ni = a_re - 1.0, a_im
    return a_re, a_im, (nr * lre + ni * lim) / den, (ni * lre - nr * lim) / den


def _group_mask(rows, cols, row_div, col_div):
    r = lax.broadcasted_iota(jnp.int32, (rows, cols), 0) // row_div % 8
    c = lax.broadcasted_iota(jnp.int32, (rows, cols), 1) // col_div
    return r == c


def fn_s5_bmat(b_re, b_im, coef_re, coef_im):
    rows = b_re.shape[0]
    mask = _group_mask(rows, 8 * SC, SP, SC)
    bb_re = coef_re * b_re - coef_im * b_im
    bb_im = coef_re * b_im + coef_im * b_re
    return jnp.where(mask, jnp.tile(bb_re, (1, 8)), 0.0), jnp.where(mask, jnp.tile(bb_im, (1, 8)), 0.0)


def fn_s5_cmat(c_re, c_im):
    rows = c_re.shape[0]
    mask = _group_mask(rows, 8 * SP, SC, SP)
    return jnp.where(mask, jnp.tile(c_re, (1, 8)), 0.0), jnp.where(mask, jnp.tile(c_im, (1, 8)), 0.0)


def fn_s5_bu(u, wb_re, wb_im):
    return mm_nt(u, wb_re), mm_nt(u, wb_im)


def fn_s5_out(sr, si, u, d, wc_re, wc_im):
    y = mm_nt(sr, wc_re) - mm_nt(si, wc_im) + d * u
    return (jax.nn.gelu(y),)


def fn_merge_glu(z, mo, gate):
    yg = z[:, :PW] * jax.nn.sigmoid(z[:, PW:])
    return (jnp.concatenate([yg, mo], axis=1) * silu(gate),)


def fn_merge(prim, mo, gate):
    return (jnp.concatenate([prim, mo], axis=1) * silu(gate),)


def fn_mem_k(kv, g):
    return (jnp.concatenate([rms(kv[:, h * XHD:(h + 1) * XHD], g) for h in range(XH)], axis=1),)


def fn_mem_attn(xq, kn, v, g):
    outs = []
    for h in range(XH):
        sl = slice(h * XHD, (h + 1) * XHD)
        p = softmax_rows(mm_nt(rms(xq[:, sl], g), kn[:, sl]) * (XHD ** -0.5))
        outs.append(mm_nn(p, v[:, sl]))
    return (jnp.concatenate(outs, axis=1),)


def _half_rms(x, g):
    lo = lax.broadcasted_iota(jnp.int32, x.shape, 1) < ROPE
    x2 = x * x
    s_lo = jnp.sum(jnp.where(lo, x2, 0.0), axis=1, keepdims=True)
    s_hi = jnp.sum(jnp.where(lo, 0.0, x2), axis=1, keepdims=True)
    return x * lax.rsqrt(jnp.where(lo, s_lo, s_hi) / ROPE + EPS) * g


def _rope(x, cos2, sin_signed):
    first = lax.broadcasted_iota(jnp.int32, x.shape, 1) % ROPE < ROPE // 2
    return x * cos2 + jnp.where(first, lane_roll(x, 128 - ROPE // 2), lane_roll(x, ROPE // 2)) * sin_signed


def fn_mla_prep(q, kv, kr, cos2, sin_signed, qnn, knn, qrn, krn):
    lo = lax.broadcasted_iota(jnp.int32, kr.shape, 1) < ROPE
    kr_pad = jnp.where(lo, _rope(_half_rms(kr, krn), cos2, sin_signed), 0.0)
    qf, kf, vs = [], [], []
    for m in range(MH // 2):
        pair = _rope(_half_rms(q[:, MH * NOPE + 128 * m:MH * NOPE + 128 * (m + 1)], qrn), cos2, sin_signed)
        for h, rope_h in ((2 * m, pair), (2 * m + 1, lane_roll(pair, ROPE))):
            qf.append(jnp.concatenate([rms(q[:, NOPE * h:NOPE * (h + 1)], qnn), jnp.where(lo, rope_h, 0.0)], axis=1))
    for h in range(MH):
        kf.append(jnp.concatenate([rms(kv[:, 256 * h:256 * h + NOPE], knn), kr_pad], axis=1))
        vs.append(kv[:, 256 * h + NOPE:256 * (h + 1)])
    return jnp.stack(qf), jnp.stack(kf), jnp.stack(vs)


ATT_TQ = 256


def _attn_tile(q, kf, v, q_start):
    s = mm_nt(q, kf) * ((NOPE + ROPE) ** -0.5)
    q_pos = q_start + lax.broadcasted_iota(jnp.int32, s.shape, 0)
    k_pos = lax.broadcasted_iota(jnp.int32, s.shape, 1)
    p = softmax_rows(jnp.where(k_pos <= q_pos, s, jnp.finfo(F32).min))
    return mm_nn(p, v)


def _attn_specs():
    q_spec = pl.BlockSpec((None, ATT_TQ, 256), lambda h, i: (h, i, 0))
    k_spec = pl.BlockSpec((None, L, 256), lambda h, i: (h, 0, 0))
    v_spec = pl.BlockSpec((None, L, 128), lambda h, i: (h, 0, 0))
    o_spec = pl.BlockSpec((ATT_TQ, 128), lambda h, i: (i, h))
    return q_spec, k_spec, v_spec, o_spec


def causal_attn(qf, kf, vh):
    n_tiles = L // ATT_TQ

    def kern(q_ref, k_ref, v_ref, o_ref):
        i = pl.program_id(1)
        for t in range(n_tiles):
            @pl.when(i == t)
            def _(t=t):
                keys = (t + 1) * ATT_TQ
                o_ref[...] = _attn_tile(q_ref[...], k_ref[:keys, :], v_ref[:keys, :], t * ATT_TQ)

    q_spec, k_spec, v_spec, o_spec = _attn_specs()
    return pl.pallas_call(kern, grid=(MH, n_tiles), in_specs=[q_spec, k_spec, v_spec], out_specs=o_spec,
                          out_shape=SDS((L, MH * VD), F32), name="l1_attn", compiler_params=_cparams())(qf, kf, vh)


def causal_attn_bwd(qf, kf, vh, dout):
    n_tiles = L // ATT_TQ

    def kern(q_ref, k_ref, v_ref, do_ref, dq_ref, dk_ref, dv_ref):
        i = pl.program_id(1)

        @pl.when(i == 0)
        def _():
            dk_ref[...] = jnp.zeros_like(dk_ref)
            dv_ref[...] = jnp.zeros_like(dv_ref)

        for t in range(n_tiles):
            @pl.when(i == t)
            def _(t=t):
                keys = (t + 1) * ATT_TQ
                _, vjp = jax.vjp(lambda q, k, v: _attn_tile(q, k, v, t * ATT_TQ),
                                 q_ref[...].astype(F32), k_ref[:keys, :].astype(F32), v_ref[:keys, :].astype(F32))
                dq, dk, dv = vjp(do_ref[...])
                dq_ref[...] = dq
                dk_ref[:keys, :] += dk
                dv_ref[:keys, :] += dv

    q_spec, k_spec, v_spec, o_spec = _attn_specs()
    return pl.pallas_call(kern, grid=(MH, n_tiles), in_specs=[q_spec, k_spec, v_spec, o_spec], out_specs=[q_spec, k_spec, v_spec],
                          out_shape=[SDS(qf.shape, F32), SDS(kf.shape, F32), SDS(vh.shape, F32)], name="l1_attn_bwd",
                          compiler_params=_cparams())(qf, kf, vh, dout)


def loss_and_grad(y, target, tl=256):
    def kern(y_ref, t_ref, dy_ref, loss_ref):
        d = y_ref[...] - t_ref[...]
        dy_ref[...] = d / D

        @pl.when(pl.program_id(0) == 0)
        def _():
            loss_ref[...] = jnp.zeros_like(loss_ref)

        loss_ref[...] += 0.5 * jnp.sum(jnp.sum(d * d, axis=1, keepdims=True), axis=0, keepdims=True) / D

    return pl.pallas_call(kern, grid=(L // tl,), in_specs=[rspec(tl, D), rspec(tl, D)], out_specs=[rspec(tl, D), cspec((1, 1))],
                          out_shape=[SDS((L, D), F32), SDS((1, 1), F32)], name="loss", compiler_params=_cparams())(y, target)


def adamw(name, w, g, m, v):
    rows, cols = w.shape
    block_row_bytes = 7 * 2 * 4 * max(cols, 128)
    tr = _row_tile(rows, min(2048, ADAMW_VMEM // block_row_bytes // 8 * 8), 8)

    def kern(w_ref, g_ref, m_ref, v_ref, d_ref, nm_ref, nv_ref):
        gg = g_ref[...]
        nm = ADAM_B1 * m_ref[...] + (1.0 - ADAM_B1) * gg
        nv = ADAM_B2 * v_ref[...] + (1.0 - ADAM_B2) * jnp.square(gg)
        m_hat = nm / (1.0 - ADAM_B1 ** ADAM_STEP)
        v_hat = nv / (1.0 - ADAM_B2 ** ADAM_STEP)
        d_ref[...] = -ADAM_LR * (m_hat / (jnp.sqrt(v_hat) + ADAM_EPS) + ADAM_WD * w_ref[...])
        nm_ref[...] = nm
        nv_ref[...] = nv

    spec = rspec(tr, cols)
    return pl.pallas_call(kern, grid=(rows // tr,), in_specs=[spec] * 4, out_specs=[spec] * 3,
                          out_shape=[SDS((rows, cols), F32)] * 3, name=name, compiler_params=_cparams())(w, g, m, v)


def _row_tile(rows, cap=512, unit=16):
    return max(t for t in range(unit, cap + 1, unit) if rows % t == 0)


ANY = pl.BlockSpec(memory_space=pl.ANY)
MESH_ID = pl.DeviceIdType.MESH


def _place():
    x, y, c = lax.axis_index("x"), lax.axis_index("y"), lax.axis_index("c")
    return x, y, c, [(1 - x, y), (x, 1 - y), (1 - x, 1 - y)]


def _row_chunks(rows, n, dtype):
    unit = 32 // jnp.dtype(dtype).itemsize
    base, extra = divmod(rows // unit, n)
    out, start = [], 0
    for k in range(n):
        size = (base + (k < extra)) * unit
        if size:
            out.append((start, size))
            start += size
    assert start == rows, (rows, unit)
    return out


def _dma_sems(n):
    return [pltpu.SemaphoreType.DMA((n,)), pltpu.SemaphoreType.DMA((n,))]


PIECE_BYTES = 1 << 20


def _pieces(shapes_dtypes, rows_of):
    out = []
    for b, (shape, dtype) in enumerate(shapes_dtypes):
        rows = rows_of(shape)
        n = max(1, min(4, rows * shape[-1] * jnp.dtype(dtype).itemsize // PIECE_BYTES))
        out += [(b, st, sz) for st, sz in _row_chunks(rows, n, dtype)]
    return out


def all_gather_chips(name, shards):
    nb = len(shards)
    pieces = _pieces([(s.shape, s.dtype) for s in shards], lambda shape: shape[0] // 2)
    n = len(pieces)

    def body(*refs):
        x_refs, out_refs, send_sems, recv_sems = refs[:nb], refs[nb:2 * nb], refs[2 * nb], refs[2 * nb + 1]
        x, y, c, chips = _place()
        sibling = (x, y, 1 - c)
        mine = 2 * x + y

        def copy(sem, chip, cc, k, to, from_input=False):
            b, st, sz = pieces[k]
            rows_k = pl.ds(cc * (x_refs[b].shape[0] // 2) + st, sz)
            dst = out_refs[b].at[chip, rows_k, :]
            return pltpu.make_async_remote_copy(src_ref=x_refs[b].at[rows_k, :] if from_input else dst, dst_ref=dst,
                                                send_sem=send_sems.at[sem], recv_sem=recv_sems.at[sem], device_id=to, device_id_type=MESH_ID)

        order = [(k, j, 2 * cx + cy, (cx, cy, c)) for k in range(n) for j, (cx, cy) in enumerate(chips)]
        first = [copy(j * n + k, mine, c, k, to, from_input=True) for k, j, _, to in order]
        for cp in first:
            cp.start()
        passed = []
        for k, j, chip, _ in order:
            copy(j * n + k, chip, c, k, sibling).wait_recv()
            passed.append(copy((3 + j) * n + k, chip, c, k, sibling))
            passed[-1].start()
        for k, j, chip, _ in order:
            copy((3 + j) * n + k, chip, 1 - c, k, sibling).wait_recv()
        for cp in first + passed:
            cp.wait_send()

    return pl.pallas_call(body, in_specs=[ANY] * nb, out_specs=[ANY] * nb, out_shape=[SDS((4,) + s.shape, s.dtype) for s in shards],
                          scratch_shapes=_dma_sems(6 * n), name=name)(*shards)


def pair_exchange(name, gs):
    nb = len(gs)
    pieces = _pieces([(g.shape, g.dtype) for g in gs], lambda shape: shape[1] // 2)

    def body(*refs):
        g_refs, got_refs, send_sems, recv_sems = refs[:nb], refs[nb:2 * nb], refs[2 * nb], refs[2 * nb + 1]
        x, y, c, _ = _place()
        swaps = [pltpu.make_async_remote_copy(src_ref=g_refs[b].at[:, pl.ds((1 - c) * (g_refs[b].shape[1] // 2) + st, sz), :],
                                              dst_ref=got_refs[b].at[:, pl.ds(st, sz), :], send_sem=send_sems.at[k], recv_sem=recv_sems.at[k],
                                              device_id=(x, y, 1 - c), device_id_type=MESH_ID)
                 for k, (b, st, sz) in enumerate(pieces)]
        for cp in swaps:
            cp.start()
        for cp in swaps:
            cp.wait()

    return pl.pallas_call(body, in_specs=[ANY] * nb, out_specs=[ANY] * nb,
                          out_shape=[SDS((g.shape[0], g.shape[1] // 2, g.shape[2]), g.dtype) for g in gs],
                          scratch_shapes=_dma_sems(len(pieces)), name=name)(*gs)


def chip_scatter(name, ps):
    nb = len(ps)
    pieces = _pieces([(p.shape, p.dtype) for p in ps], lambda shape: shape[1])
    n = len(pieces)

    def body(*refs):
        p_refs, q_refs, send_sems, recv_sems = refs[:nb], refs[nb:2 * nb], refs[2 * nb], refs[2 * nb + 1]
        x, y, c, chips = _place()
        mine = 2 * x + y

        def copy(j, k, src_slot, dst_slot, to):
            b, st, sz = pieces[k]
            return pltpu.make_async_remote_copy(src_ref=p_refs[b].at[src_slot, pl.ds(st, sz), :], dst_ref=q_refs[b].at[dst_slot, pl.ds(st, sz), :],
                                                send_sem=send_sems.at[j * n + k], recv_sem=recv_sems.at[j * n + k], device_id=to,
                                                device_id_type=MESH_ID)

        order = [(k, j, 2 * cx + cy, (cx, cy, c)) for k in range(n) for j, (cx, cy) in enumerate(chips)]
        sends = [copy(j, k, chip, mine, to) for k, j, chip, to in order]
        for cp in sends:
            cp.start()
        for k, j, chip, to in order:
            copy(j, k, mine, chip, to).wait_recv()
        for cp in sends:
            cp.wait_send()

    return pl.pallas_call(body, in_specs=[ANY] * nb, out_specs=[ANY] * nb, out_shape=[SDS(p.shape, p.dtype) for p in ps],
                          scratch_shapes=_dma_sems(3 * n), name=name)(*ps)


def pair_join(name, bufs):
    nb = len(bufs)
    pieces = _pieces([(b.shape, b.dtype) for b in bufs], lambda shape: shape[0] // 2)

    def body(*refs):
        out_refs, send_sems, recv_sems = refs[nb:2 * nb], refs[2 * nb], refs[2 * nb + 1]
        x, y, c, _ = _place()

        def copy(k, cc):
            b, st, sz = pieces[k]
            rows_k = out_refs[b].at[pl.ds(cc * (out_refs[b].shape[0] // 2) + st, sz), :]
            return pltpu.make_async_remote_copy(src_ref=rows_k, dst_ref=rows_k, send_sem=send_sems.at[k], recv_sem=recv_sems.at[k],
                                                device_id=(x, y, 1 - c), device_id_type=MESH_ID)

        gives = [copy(k, c) for k in range(len(pieces))]
        for cp in gives:
            cp.start()
        for k in range(len(pieces)):
            copy(k, 1 - c).wait_recv()
        for cp in gives:
            cp.wait_send()

    return pl.pallas_call(body, in_specs=[ANY] * nb, out_specs=[ANY] * nb, out_shape=[SDS(b.shape, b.dtype) for b in bufs],
                          input_output_aliases={i: i for i in range(nb)}, scratch_shapes=_dma_sems(len(pieces)), name=name)(*bufs)


def pair_add(name, g, got, place):
    slots, rows, cols = g.shape
    half = rows // 2
    tr = _row_tile(half)
    nb = half // tr

    def kern(_, g_ref, t_ref, o_ref):
        o_ref[...] = (g_ref[...].astype(F32) + t_ref[...].astype(F32)).astype(o_ref.dtype)

    blk = pl.BlockSpec((None, tr, cols), lambda s, i, p: (s, i, 0))
    grid_spec = pltpu.PrefetchScalarGridSpec(
        num_scalar_prefetch=1, grid=(slots, nb),
        in_specs=[pl.BlockSpec((None, tr, cols), lambda s, i, p: (s, p[1] * nb + i, 0)), blk], out_specs=blk)
    return pl.pallas_call(kern, grid_spec=grid_spec, out_shape=SDS((slots, half, cols), g.dtype), name=name,
                          compiler_params=_cparams())(place, g, got)


def chip_add(name, p, q, place):
    slots, half, cols = p.shape
    tr = _row_tile(half)
    nb = half // tr

    def kern(_, p_ref, q1, q2, q3, o_ref):
        o_ref[...] = p_ref[...].astype(F32) + q1[...].astype(F32) + q2[...].astype(F32) + q3[...].astype(F32)

    def slot(k):
        return pl.BlockSpec((None, tr, cols), lambda i, pr: ((pr[0] + k) % slots, i, 0))

    grid_spec = pltpu.PrefetchScalarGridSpec(
        num_scalar_prefetch=1, grid=(nb,), in_specs=[slot(0), slot(1), slot(2), slot(3)],
        out_specs=pl.BlockSpec((tr, cols), lambda i, pr: (pr[1] * nb + i, 0)))
    return pl.pallas_call(kern, grid_spec=grid_spec, out_shape=SDS((2 * half, cols), F32), name=name,
                          compiler_params=_cparams())(place, p, q, q, q)


def reduce_scatter_chips(gs, place):
    gots = pair_exchange("rs_pair_exchange", gs)
    pairs = [pair_add(f"rs_pair_add_{i}", g, got, place) for i, (g, got) in enumerate(zip(gs, gots))]
    qs = chip_scatter("rs_chip_scatter", pairs)
    return pair_join("rs_pair_join", [chip_add(f"rs_chip_add_{i}", p, q, place) for i, (p, q) in enumerate(zip(pairs, qs))])


BIG = [("w_out", (2, 512, 1024)), ("w_mem_kv", (2, 256, 1024)), ("s5_w_in", (1, 1024, 1024)), ("s5_w_glu", (1, 1536, 768)),
       ("mla_w_in", (1, 1024, 848)), ("mla_w_uq", (1, 512, 576)), ("mla_w_ukv", (1, 256, 768))]
SHARDED_SMALL = [("mla_q_lora_norm", (1, 128)), ("mla_kv_lora_norm", (1, 64))]
SMALL = [("ln_gain", (2, 1024)), ("mem_norm", (2, 1024)), ("xq_norm", (2, 128)), ("xk_norm", (2, 128)),
         ("s5_lambda_re", (1, 96, 64)), ("s5_lambda_im", (1, 96, 64)), ("s5_log_step", (1, 96)),
         ("s5_b_re", (1, 96, 64, 16)), ("s5_b_im", (1, 96, 64, 16)), ("s5_c_re", (1, 96, 16, 64)), ("s5_c_im", (1, 96, 16, 64)),
         ("s5_d", (1, 1536)), ("mla_q_nope_norm", (1, 128)), ("mla_k_nope_norm", (1, 128)), ("mla_q_rope_norm", (1, 64)),
         ("mla_k_rope_norm", (1, 64))]
WEIGHT_ORDER = ["ln_gain", "w_out", "mem_norm", "w_mem_kv", "xq_norm", "xk_norm", "s5_w_in", "s5_lambda_re", "s5_lambda_im",
                "s5_log_step", "s5_b_re", "s5_b_im", "s5_c_re", "s5_c_im", "s5_d", "s5_w_glu", "mla_w_in", "mla_q_lora_norm",
                "mla_kv_lora_norm", "mla_w_uq", "mla_w_ukv", "mla_q_nope_norm", "mla_k_nope_norm", "mla_q_rope_norm", "mla_k_rope_norm"]
SMALL_FULL = SMALL + [(n, (1, 4 * s[1])) for n, s in SHARDED_SMALL]
N_SMALL = sum(math.prod(s) for _, s in SMALL_FULL)
SMALL_ROWS, SMALL_LANES = 128, 1024

WIDE_ROWS, MID_ROWS = 2560, 1792
OUT_ROFF, MKV_ROFF, IN0_ROFF = (0, 512), (1024, 1280), 1536
GLU_ROFF, UKV_ROFF = 0, 1536


def weight_views(wide, mid):
    return {"w_out": [Sharded(wide, "row", r, 512) for r in OUT_ROFF], "w_mem_kv": [Sharded(wide, "row", r, 256) for r in MKV_ROFF],
            "s5_w_in": Sharded(wide, "col", IN0_ROFF, 1024), "s5_w_glu": Sharded(mid, "col", GLU_ROFF, 1536),
            "mla_w_ukv": Sharded(mid, "col", UKV_ROFF, 256)}


def stack_shards(w, dtype):
    wide = jnp.concatenate([w["w_out"].reshape(1024, 1024), w["w_mem_kv"].reshape(512, 1024), w["s5_w_in"].reshape(1024, 1024)], axis=0)
    mid = jnp.concatenate([w["s5_w_glu"].reshape(1536, 768), w["mla_w_ukv"].reshape(256, 768)], axis=0)
    return wide.astype(dtype), mid.astype(dtype)


def cols_to_shards(full):
    return full.reshape(full.shape[0], 4, full.shape[1] // 4).transpose(1, 0, 2)


def shards_to_cols(arr):
    return arr.transpose(1, 0, 2).reshape(arr.shape[1], 4 * arr.shape[2])


def mla_in_permute(w):
    o1, o2, o3, o4 = QL, QL + KVL, QL + KVL + ROPE, QL + KVL + ROPE + XQW
    return jnp.concatenate([w[:, o4:], w[:, :o1], w[:, o3:o4], w[:, o1:o2], w[:, o2:o3],
                            jnp.zeros((w.shape[0], MLA_IN_P - MLA_IN), w.dtype)], axis=1)


def mla_in_unpermute(d):
    return jnp.concatenate([d[:, 2048:2560], d[:, 3072:3328], d[:, 3328:3392], d[:, 2560:3072], d[:, :2048]], axis=1)


def uq_permute(w):
    w3 = w.reshape(w.shape[0], MH, NOPE + ROPE)
    return jnp.concatenate([w3[:, :, :NOPE].reshape(w.shape[0], MH * NOPE), w3[:, :, NOPE:].reshape(w.shape[0], MH * ROPE)], axis=1)


def uq_unpermute(d):
    dn = d[:, :MH * NOPE].reshape(d.shape[0], MH, NOPE)
    dr = d[:, MH * NOPE:].reshape(d.shape[0], MH, ROPE)
    return jnp.concatenate([dn, dr], axis=2).reshape(d.shape[0], MH * (NOPE + ROPE))


def time_permute(a):
    return a.reshape(SEG, SEG_LEN, a.shape[-1]).transpose(1, 0, 2).reshape(L, a.shape[-1])


def time_unpermute(a):
    return a.reshape(SEG_LEN, SEG, a.shape[-1]).transpose(1, 0, 2).reshape(L, a.shape[-1])


def mem_branch_fwd(tag, mem, mem_norm, w_mem_kv, xk_norm):
    mn = row_fwd(tag + "_mem_rms", fn_rms, ML, ML, [(mem, D, 0)], [mem_norm], [(D, BF16)])[0]
    kv = matmul(tag + "_mem_kv", mn, w_mem_kv, "nn")
    kn = row_fwd(tag + "_mem_knorm", fn_mem_k, ML, ML, [(kv, XQW, 0)], [xk_norm], [(XQW, F32)])[0]
    return mn, kv, kn


def mem_branch_bwd(tag, mem, mem_norm, w_mem_kv, xk_norm, mn, kv, dkn, dv, g_view, g_wide):
    dk, dxk = row_bwd(tag + "_mem_knorm_bwd", fn_mem_k, ML, ML, [(kv, XQW, 0)], [xk_norm], [(dkn, XQW, 0)], [True], [True])
    dkv = jnp.concatenate([dk, dv], axis=1)
    dmn = matmul(tag + "_mem_kv_dx", dkv, w_mem_kv, "nt")
    g_wide = matmul(tag + "_mem_kv_dw", mn, dkv, "tn", out=g_view, into=g_wide)
    dmem_norm = row_bwd(tag + "_mem_rms_bwd", fn_rms, ML, ML, [(mem, D, 0)], [mem_norm], [(dmn, D, 0)], [False], [True])[0]
    return g_wide, dmem_norm, dxk


def mem_attn_fwd(tag, proj, cb, kn, kv, xq_norm):
    return row_fwd(tag + "_mem_attn", fn_mem_attn, L, 256, [(proj, XQW, cb)], [kn, kv[:, XQW:], xq_norm], [(XQW, F32)])[0]


def mem_attn_bwd(tag, proj, cb, kn, kv, xq_norm, dmo, dproj):
    place = {"cols": proj.shape[1], "cb": cb, "into": dproj}
    return row_bwd(tag + "_mem_attn_bwd", fn_mem_attn, L, 256, [(proj, XQW, cb)], [kn, kv[:, XQW:], xq_norm], [(dmo, XQW, 0)],
                   [place], [True, True, True])


def device_step(x, mem, positions, target, wide, mid, w_in1, w_uq, small):
    g = {}
    w = weight_views(wide, mid)
    gw = weight_views(SDS(wide.shape, BF16), SDS(mid.shape, BF16))
    ln, mem_norm, xq_norm, xk_norm = small["ln_gain"], small["mem_norm"], small["xq_norm"], small["xk_norm"]

    lre, lim = small["s5_lambda_re"][0], small["s5_lambda_im"][0]
    ls = small["s5_log_step"].reshape(SG, 1)
    one = pl.BlockSpec((SG, SP), lambda i: (0, 0))
    col = pl.BlockSpec((SG, 1), lambda i: (0, 0))
    disc_ins = [(lre, one), (lim, one), (ls, col)]
    a_re, a_im, coef_re, coef_im = stage("s5_disc", fn_s5_disc, (1,), disc_ins, [(SDS((SG, SP), F32), one)] * 4)
    b_re, b_im = small["s5_b_re"].reshape(SN, SC), small["s5_b_im"].reshape(SN, SC)
    c_re, c_im = small["s5_c_re"].reshape(PW, SP), small["s5_c_im"].reshape(PW, SP)
    bmat_rows = [(b_re, SC, 0), (b_im, SC, 0), (coef_re.reshape(SN, 1), 1, 0), (coef_im.reshape(SN, 1), 1, 0)]
    wb_re, wb_im = row_fwd("s5_bmat", fn_s5_bmat, SN, 512, bmat_rows, [], [(128, F32)] * 2)
    cmat_rows = [(c_re, SP, 0), (c_im, SP, 0)]
    wc_re, wc_im = row_fwd("s5_cmat", fn_s5_cmat, PW, 128, cmat_rows, [], [(512, F32)] * 2)
    a_re_v, a_im_v = a_re.reshape(1, SN), a_im.reshape(1, SN)
    s5_d = small["s5_d"]

    xp = time_permute(x)
    h0 = row_fwd("l0_rms", fn_rms, L, 256, [(xp, D, 0)], [ln[0:1]], [(D, BF16)])[0]
    proj0 = matmul("l0_in", h0, w["s5_w_in"], "nn")
    ts = 512
    u_spec = pl.BlockSpec((ts, 128), lambda j, i: (i, j))
    wb_spec = pl.BlockSpec((512, 128), lambda j, i: (j, 0))
    s_spec = pl.BlockSpec((ts, 512), lambda j, i: (i, j))
    bu_ins = [(proj0, u_spec), (wb_re, wb_spec), (wb_im, wb_spec)]
    bu_re, bu_im = stage("s5_bu", fn_s5_bu, (12, L // ts), bu_ins, [(SDS((L, SN), F32), s_spec)] * 2)
    s_re, s_im = s5_scan_fwd(bu_re, bu_im, a_re_v, a_im_v)
    d_spec = pl.BlockSpec((1, 128), lambda j, i: (0, j))
    wc_spec = pl.BlockSpec((128, 512), lambda j, i: (j, 0))
    out_ins = [(s_re, s_spec), (s_im, s_spec), (proj0, u_spec), (s5_d, d_spec), (wc_re, wc_spec), (wc_im, wc_spec)]
    g0 = stage("s5_out", fn_s5_out, (12, L // ts), out_ins, [(SDS((L, PW), BF16), u_spec)])[0]
    z0 = matmul("l0_glu", g0, w["s5_w_glu"], "nn")
    mn0, kv0, kn0 = mem_branch_fwd("l0", mem, mem_norm[0:1], w["w_mem_kv"][0], xk_norm[0:1])
    mo0 = mem_attn_fwd("l0", proj0, 3, kn0, kv0, xq_norm[0:1])
    o0 = row_fwd("l0_merge", fn_merge_glu, L, 256, [(z0, 2 * PW, 0), (mo0, XQW, 0), (proj0, BW, 1)], [], [(BW, BF16)])[0]
    x1p = matmul("l0_out", o0, w["w_out"][0], "nn", add=xp)
    x1 = time_unpermute(x1p)

    h1 = row_fwd("l1_rms", fn_rms, L, 256, [(x1, D, 0)], [ln[1:2]], [(D, BF16)])[0]
    proj1 = matmul("l1_in", h1, w_in1, "nn")
    qln, kvln = small["mla_q_lora_norm"].reshape(1, QL), small["mla_kv_lora_norm"].reshape(1, KVL)
    cqn = row_fwd("l1_q_lora_rms", fn_rms, L, 256, [(proj1, QL, 4)], [qln], [(QL, BF16)])[0]
    ckvn = row_fwd("l1_kv_lora_rms", fn_rms, L, 256, [(proj1, KVL, 12)], [kvln], [(KVL, BF16)])[0]
    q = matmul("l1_uq", cqn, w_uq, "nn")
    kv = matmul("l1_ukv", ckvn, w["mla_w_ukv"], "nn")
    inv_freq = ROPE_THETA ** (-jnp.arange(ROPE // 2, dtype=F32) / (ROPE // 2))
    ang = positions.astype(F32)[:, None] * inv_freq
    cos2 = jnp.tile(jnp.cos(ang), (1, 4))
    sin_signed = jnp.tile(jnp.concatenate([-jnp.sin(ang), jnp.sin(ang)], axis=1), (1, 2))
    qnn, knn = small["mla_q_nope_norm"], small["mla_k_nope_norm"]
    qrn, krn = jnp.tile(small["mla_q_rope_norm"], (1, 2)), jnp.tile(small["mla_k_rope_norm"], (1, 2))
    tp = 256
    prep_ins = [(q, rspec(tp, MH * (NOPE + ROPE))), (kv, rspec(tp, MH * 256)), (proj1, rspec(tp, 128, 26)),
                (cos2, rspec(tp, 128)), (sin_signed, rspec(tp, 128))] + [(a, cspec((1, 128))) for a in (qnn, knn, qrn, krn)]
    hq_spec = pl.BlockSpec((MH, tp, 256), lambda i: (0, i, 0))
    hv_spec = pl.BlockSpec((MH, tp, 128), lambda i: (0, i, 0))
    qf, kf, vh = stage("l1_mla_prep", fn_mla_prep, (L // tp,), prep_ins,
                       [(SDS((MH, L, 256), BF16), hq_spec), (SDS((MH, L, 256), BF16), hq_spec), (SDS((MH, L, 128), BF16), hv_spec)])
    attn = causal_attn(qf, kf, vh)
    mn1, kv1, kn1 = mem_branch_fwd("l1", mem, mem_norm[1:2], w["w_mem_kv"][1], xk_norm[1:2])
    mo1 = mem_attn_fwd("l1", proj1, 5, kn1, kv1, xq_norm[1:2])
    o1 = row_fwd("l1_merge", fn_merge, L, 256, [(attn, PW, 0), (mo1, XQW, 0), (proj1, BW, 0)], [], [(BW, BF16)])[0]
    x2 = matmul("l1_out", o1, w["w_out"][1], "nn", add=x1)
    dx2, loss = loss_and_grad(x2, target)

    do1 = matmul("l1_out_dx", dx2, w["w_out"][1], "nt")
    g_wide = matmul("l1_out_dw", o1, dx2, "tn", out=gw["w_out"][1])
    dattn, dmo1, dproj1 = row_bwd("l1_merge_bwd", fn_merge, L, 256, [(attn, PW, 0), (mo1, XQW, 0), (proj1, BW, 0)], [],
                                  [(do1, BW, 0)], [True, True, {"cols": MLA_IN_P, "cb": 0}], [])
    dproj1, dkn1, dv1, dxqn1 = mem_attn_bwd("l1", proj1, 5, kn1, kv1, xq_norm[1:2], dmo1, dproj1)
    g_wide, dmem_norm1, dxk1 = mem_branch_bwd("l1", mem, mem_norm[1:2], w["w_mem_kv"][1], xk_norm[1:2], mn1, kv1, dkn1, dv1,
                                              gw["w_mem_kv"][1], g_wide)
    dqf, dkf, dvh = causal_attn_bwd(qf, kf, vh, dattn)
    prep_diffs = [("row", SDS((L, MH * (NOPE + ROPE)), F32), rspec(tp, MH * (NOPE + ROPE))), ("row", SDS((L, MH * 256), F32), rspec(tp, MH * 256)),
                  ("row", SDS((L, MLA_IN_P), F32), rspec(tp, 128, 26), {"into": dproj1}), None, None] + [("acc", (0,))] * 4
    dq, dkv, dproj1, dqnn, dknn, dqrn, dkrn = stage_bwd("l1_mla_prep_bwd", fn_mla_prep, (L // tp,), prep_ins,
                                                        [(dqf, hq_spec), (dkf, hq_spec), (dvh, hv_spec)], prep_diffs)
    dcqn = matmul("l1_uq_dx", dq, w_uq, "nt")
    dw_uq = matmul("l1_uq_dw", cqn, dq, "tn")
    dckvn = matmul("l1_ukv_dx", dkv, w["mla_w_ukv"], "nt")
    g_mid = matmul("l1_ukv_dw", ckvn, dkv, "tn", out=gw["mla_w_ukv"])
    dproj1, dqln = row_bwd("l1_q_lora_rms_bwd", fn_rms, L, 256, [(proj1, QL, 4)], [qln], [(dcqn, QL, 0)],
                           [{"cols": MLA_IN_P, "cb": 4, "into": dproj1}], [True])
    dproj1, dkvln = row_bwd("l1_kv_lora_rms_bwd", fn_rms, L, 256, [(proj1, KVL, 12)], [kvln], [(dckvn, KVL, 0)],
                            [{"cols": MLA_IN_P, "cb": 12, "into": dproj1}], [True])
    dh1 = matmul("l1_in_dx", dproj1, w_in1, "nt")
    dw_in1 = matmul("l1_in_dw", h1, dproj1, "tn")
    dx1, dln1 = row_bwd("l1_rms_bwd", fn_rms, L, 256, [(x1, D, 0)], [ln[1:2]], [(dh1, D, 0)], [{"add": (dx2, D, 0)}], [True])
    dx1p = time_permute(dx1)

    do0 = matmul("l0_out_dx", dx1p, w["w_out"][0], "nt")
    g_wide = matmul("l0_out_dw", o0, dx1p, "tn", out=gw["w_out"][0], into=g_wide)
    dz0, dmo0, dproj0 = row_bwd("l0_merge_bwd", fn_merge_glu, L, 256, [(z0, 2 * PW, 0), (mo0, XQW, 0), (proj0, BW, 1)], [],
                                [(do0, BW, 0)], [True, True, {"cols": 2 * BW, "cb": 1}], [])
    dproj0, dkn0, dv0, dxqn0 = mem_attn_bwd("l0", proj0, 3, kn0, kv0, xq_norm[0:1], dmo0, dproj0)
    g_wide, dmem_norm0, dxk0 = mem_branch_bwd("l0", mem, mem_norm[0:1], w["w_mem_kv"][0], xk_norm[0:1], mn0, kv0, dkn0, dv0,
                                              gw["w_mem_kv"][0], g_wide)
    dg0 = matmul("l0_glu_dx", dz0, w["s5_w_glu"], "nt")
    g_mid = matmul("l0_glu_dw", g0, dz0, "tn", out=gw["s5_w_glu"], into=g_mid)
    out_diffs = [("row", SDS((L, SN), F32), s_spec), ("row", SDS((L, SN), F32), s_spec), ("row", SDS((L, PW), F32), u_spec),
                 ("acc", (1,)), ("acc", (1,)), ("acc", (1,))]
    gs_re, gs_im, du_a, dd, dwc_re, dwc_im = stage_bwd("s5_out_bwd", fn_s5_out, (12, L // ts), out_ins, [(dg0, u_spec)], out_diffs)
    dbu_re, dbu_im, da_re, da_im = s5_scan_bwd(gs_re, gs_im, s_re, s_im, a_re_v, a_im_v)
    bu_diffs = [("row", SDS((L, 2 * BW), F32), u_spec, {"add": (du_a, u_spec), "into": dproj0}), ("acc", (1,)), ("acc", (1,))]
    dproj0, dwb_re, dwb_im = stage_bwd("s5_bu_bwd", fn_s5_bu, (12, L // ts), bu_ins, [(dbu_re, s_spec), (dbu_im, s_spec)], bu_diffs)
    dh0 = matmul("l0_in_dx", dproj0, w["s5_w_in"], "nt")
    g_wide = matmul("l0_in_dw", h0, dproj0, "tn", out=gw["s5_w_in"], into=g_wide)
    dxp, dln0 = row_bwd("l0_rms_bwd", fn_rms, L, 256, [(xp, D, 0)], [ln[0:1]], [(dh0, D, 0)], [{"add": (dx1p, D, 0)}], [True])
    grad_x = time_unpermute(dxp)

    db_re, db_im, dcoef_re, dcoef_im = row_bwd("s5_bmat_bwd", fn_s5_bmat, SN, 512, bmat_rows, [], [(dwb_re, 128, 0), (dwb_im, 128, 0)],
                                               [True] * 4, [])
    dc_re, dc_im = row_bwd("s5_cmat_bwd", fn_s5_cmat, PW, 128, cmat_rows, [], [(dwc_re, 512, 0), (dwc_im, 512, 0)], [True] * 2, [])
    disc_cts = [(da_re.reshape(SG, SP), one), (da_im.reshape(SG, SP), one), (dcoef_re.reshape(SG, SP), one), (dcoef_im.reshape(SG, SP), one)]
    dlre, dlim, dls = stage_bwd("s5_disc_bwd", fn_s5_disc, (1,), disc_ins, disc_cts, [("acc", (0,))] * 3)

    g["ln_gain"] = jnp.concatenate([dln0, dln1], axis=0)
    g["mem_norm"] = jnp.concatenate([dmem_norm0, dmem_norm1], axis=0)
    g["xq_norm"] = jnp.concatenate([dxqn0, dxqn1], axis=0)
    g["xk_norm"] = jnp.concatenate([dxk0, dxk1], axis=0)
    g["s5_lambda_re"], g["s5_lambda_im"], g["s5_log_step"] = dlre, dlim, dls
    g["s5_b_re"], g["s5_b_im"], g["s5_c_re"], g["s5_c_im"] = db_re, db_im, dc_re, dc_im
    g["s5_d"] = dd
    g["mla_q_lora_norm"], g["mla_kv_lora_norm"] = dqln, dkvln
    g["mla_q_nope_norm"], g["mla_k_nope_norm"] = dqnn, dknn
    g["mla_q_rope_norm"] = dqrn[:, :ROPE] + dqrn[:, ROPE:]
    g["mla_k_rope_norm"] = dkrn[:, :ROPE] + dkrn[:, ROPE:]
    return loss, grad_x, g_wide, g_mid, dw_in1, dw_uq, g


def kernel(x, mem, positions, ln_gain, w_out, mem_norm, w_mem_kv, xq_norm, xk_norm, s5_w_in, s5_lambda_re, s5_lambda_im, s5_log_step, s5_b_re, s5_b_im, s5_c_re, s5_c_im, s5_d, s5_w_glu, mla_w_in, mla_q_lora_norm, mla_kv_lora_norm, mla_w_uq, mla_w_ukv, mla_q_nope_norm, mla_k_nope_norm, mla_q_rope_norm, mla_k_rope_norm, loss_target, m_ln_gain, m_w_out, m_mem_norm, m_w_mem_kv, m_xq_norm, m_xk_norm, m_s5_w_in, m_s5_lambda_re, m_s5_lambda_im, m_s5_log_step, m_s5_b_re, m_s5_b_im, m_s5_c_re, m_s5_c_im, m_s5_d, m_s5_w_glu, m_mla_w_in, m_mla_q_lora_norm, m_mla_kv_lora_norm, m_mla_w_uq, m_mla_w_ukv, m_mla_q_nope_norm, m_mla_k_nope_norm, m_mla_q_rope_norm, m_mla_k_rope_norm, v_ln_gain, v_w_out, v_mem_norm, v_w_mem_kv, v_xq_norm, v_xk_norm, v_s5_w_in, v_s5_lambda_re, v_s5_lambda_im, v_s5_log_step, v_s5_b_re, v_s5_b_im, v_s5_c_re, v_s5_c_im, v_s5_d, v_s5_w_glu, v_mla_w_in, v_mla_q_lora_norm, v_mla_kv_lora_norm, v_mla_w_uq, v_mla_w_ukv, v_mla_q_nope_norm, v_mla_k_nope_norm, v_mla_q_rope_norm, v_mla_k_rope_norm):
    args = dict(locals())
    wts = {n: args[n] for n in WEIGHT_ORDER}
    mom = {n: args["m_" + n] for n in WEIGHT_ORDER}
    var = {n: args["v_" + n] for n in WEIGHT_ORDER}

    chip = 2 * lax.axis_index("x") + lax.axis_index("y")
    place = jnp.stack([chip, lax.axis_index("c")]).astype(jnp.int32)

    def gather(name, shards):
        return [lax.dynamic_update_slice(g, s[None], (chip, 0, 0)) for g, s in zip(all_gather_chips(name, shards), shards)]

    norm_rows = jnp.concatenate([mla_q_lora_norm, jnp.pad(mla_kv_lora_norm, ((0, 0), (0, 64))), jnp.zeros((14, 128), F32)], axis=0)
    wide, mid, in1, uq, norms = gather("gather_weights", [*stack_shards(wts, BF16), mla_w_in[0].astype(BF16), mla_w_uq[0].astype(BF16), norm_rows])
    small = {n: wts[n] for n, _ in SMALL}
    small["mla_q_lora_norm"], small["mla_kv_lora_norm"] = norms[:, 0, :].reshape(1, QL), norms[:, 1, :64].reshape(1, KVL)

    loss, grad_x, g_wide, g_mid, dw_in1, dw_uq, g = device_step(
        x[0], mem[0], positions[0], loss_target[0], wide, mid, mla_in_permute(shards_to_cols(in1)), uq_permute(shards_to_cols(uq)), small)
    loss = lax.psum(loss[0, 0], MESH_AXES)

    g_in1 = cols_to_shards(mla_in_unpermute(dw_in1)).astype(BF16)
    g_uq = cols_to_shards(uq_unpermute(dw_uq)).astype(BF16)
    small_flat = jnp.concatenate([g[n].reshape(-1) for n, _ in SMALL_FULL])
    g_small = jnp.pad(small_flat, (0, 4 * SMALL_ROWS * SMALL_LANES - N_SMALL)).astype(BF16).reshape(4, SMALL_ROWS, SMALL_LANES)
    r_wide, r_mid, r_in1, r_uq, r_small = reduce_scatter_chips([g_wide, g_mid, g_in1, g_uq, g_small], place)
    small_all = gather("gather_small_grads", [r_small])[0].reshape(-1)[:N_SMALL]

    grads = {"w_out": r_wide[:1024].reshape(2, 512, 1024), "w_mem_kv": r_wide[1024:IN0_ROFF].reshape(2, 256, 1024),
             "s5_w_in": r_wide[IN0_ROFF:][None], "s5_w_glu": r_mid[:UKV_ROFF][None], "mla_w_ukv": r_mid[UKV_ROFF:][None],
             "mla_w_in": r_in1[None], "mla_w_uq": r_uq[None]}
    off = 0
    for n, s in SMALL_FULL:
        grads[n] = small_all[off:off + math.prod(s)].reshape(s)
        off += math.prod(s)
    for n, s in SHARDED_SMALL:
        grads[n] = lax.dynamic_slice(grads[n], (0, chip * s[1]), s)

    delta, new_m, new_v = {}, {}, {}
    own_layout = [(n, (s[0] * s[1], s[2])) for n, s in BIG] + [(n, (s[1] * s[2], s[3])) for n, s in SMALL if len(s) == 4]
    for n, two_d in own_layout:
        res = adamw("adamw_" + n, wts[n].reshape(two_d), grads[n].reshape(two_d), mom[n].reshape(two_d), var[n].reshape(two_d))
        delta[n], new_m[n], new_v[n] = (r.reshape(wts[n].shape) for r in res)
    small_names = [n for n, s in SMALL if len(s) < 4] + [n for n, _ in SHARDED_SMALL]
    n_own = sum(wts[n].size for n in small_names)
    rows_own = -(-n_own // (8 * 128)) * 8

    def pack_small(d):
        flat = jnp.concatenate([d[n].reshape(-1) for n in small_names])
        return jnp.pad(flat, (0, rows_own * 128 - n_own), constant_values=1.0).reshape(rows_own, 128)

    res = adamw("adamw_small", pack_small(wts), pack_small(grads), pack_small(mom), pack_small(var))
    off = 0
    for n in small_names:
        size = wts[n].size
        delta[n], new_m[n], new_v[n] = (r.reshape(-1)[off:off + size].reshape(wts[n].shape) for r in res)
        off += size

    return (loss, grad_x[None], *[grads[n] for n in WEIGHT_ORDER], *[delta[n] for n in WEIGHT_ORDER],
            *[new_m[n] for n in WEIGHT_ORDER], *[new_v[n] for n in WEIGHT_ORDER])
```

```python
import functools
import math

import jax
import jax.numpy as jnp
from jax import lax
from jax.experimental import pallas as pl
from jax.experimental.pallas import tpu as pltpu

F32, BF16 = jnp.float32, jnp.bfloat16
SDS = jax.ShapeDtypeStruct

D = 1024
L = 2048
ML = 256
BW = 2 * D
XQW = BW // 4
PW = BW - XQW
XH, XHD = 4, 128
SG, SC, SP = 96, 16, 64
SN = SG * SP
NOPE, ROPE, VD = 128, 64, 128
MH = 12
QL, KVL = 512, 256
EPS = 1e-6
ROPE_THETA = 10000.0
MLA_IN = QL + KVL + ROPE + XQW + BW
MLA_IN_P = 3456
ADAM_LR, ADAM_B1, ADAM_B2, ADAM_EPS, ADAM_WD, ADAM_STEP = 0.001, 0.9, 0.999, 1e-08, 0.01, 10

VMEM_LIMIT = 48 * 2**20
SEG = 8
SEG_LEN = L // SEG
MESH_AXES = ("x", "y", "c")


def _cparams():
    return pltpu.CompilerParams(vmem_limit_bytes=VMEM_LIMIT)


def _dg(a, b, ca, cb):
    return lax.dot_general(a.astype(BF16), b.astype(BF16), (((ca,), (cb,)), ((), ())), preferred_element_type=F32)


@jax.custom_vjp
def mm_nn(a, b):
    return _dg(a, b, 1, 0)


mm_nn.defvjp(lambda a, b: (_dg(a, b, 1, 0), (a, b)), lambda res, g: (_dg(g, res[1], 1, 1), _dg(res[0], g, 0, 0)))


@jax.custom_vjp
def mm_nt(a, b):
    return _dg(a, b, 1, 1)


mm_nt.defvjp(lambda a, b: (_dg(a, b, 1, 1), (a, b)), lambda res, g: (_dg(g, res[1], 1, 0), _dg(g, res[0], 0, 0)))


@functools.partial(jax.custom_vjp, nondiff_argnums=(1,))
def lane_roll(x, shift):
    return pltpu.roll(x, shift, 1)


lane_roll.defvjp(lambda x, shift: (pltpu.roll(x, shift, 1), None),
                 lambda shift, _, g: (pltpu.roll(g, (128 - shift) % 128, 1),))


def rms(x, g):
    return x * lax.rsqrt(jnp.mean(x * x, axis=-1, keepdims=True) + EPS) * g


def softmax_rows(s):
    m = lax.stop_gradient(jnp.max(s, axis=-1, keepdims=True))
    e = jnp.exp(s - m)
    return e / jnp.sum(e, axis=-1, keepdims=True)


def silu(x):
    return x * jax.nn.sigmoid(x)


def stage(name, fn, grid, ins, outs):
    n_in = len(ins)

    def kern(*refs):
        res = fn(*[r[...] for r in refs[:n_in]])
        for r, v in zip(refs[n_in:], res):
            r[...] = v.astype(r.dtype)

    return pl.pallas_call(kern, grid=grid, in_specs=[s for _, s in ins], out_specs=[s for _, s in outs],
                          out_shape=[sd for sd, _ in outs], name=name, compiler_params=_cparams())(*[a for a, _ in ins])


def stage_bwd(name, fn, grid, ins, cts, diffs):
    n_in, n_ct = len(ins), len(cts)
    didx = [i for i, d in enumerate(diffs) if d is not None]
    opts = {i: (diffs[i][3] if len(diffs[i]) > 3 else {}) for i in didx if diffs[i][0] == "row"}
    adds = [(i, opts[i]["add"]) for i in opts if "add" in opts[i]]
    intos = [(i, opts[i]["into"]) for i in opts if "into" in opts[i]]
    n_add, n_into = len(adds), len(intos)
    add_pos = {i: n_in + n_ct + k for k, (i, _) in enumerate(adds)}
    n_extra = n_in + n_ct + n_add + n_into

    def kern(*refs):
        vals = [r[...] for r in refs[:n_in]]

        def f(*dv):
            full = list(vals)
            for i, v in zip(didx, dv):
                full[i] = v
            return fn(*full)

        _, vjp = jax.vjp(f, *[vals[i].astype(F32) for i in didx])
        gs = vjp(tuple(c[...].astype(F32) for c in refs[n_in:n_in + n_ct]))
        for o_ref, i, g in zip(refs[n_extra:], didx, gs):
            if diffs[i][0] == "row":
                if i in add_pos:
                    g = g + refs[add_pos[i]][...].astype(F32)
                o_ref[...] = g.astype(o_ref.dtype)
            else:
                first = functools.reduce(jnp.logical_and, [pl.program_id(ax) == 0 for ax in diffs[i][1]])

                @pl.when(first)
                def _():
                    o_ref[...] = g

                @pl.when(jnp.logical_not(first))
                def _():
                    o_ref[...] += g

    out_shape, out_specs = [], []
    for i in didx:
        if diffs[i][0] == "row":
            out_shape.append(diffs[i][1])
            out_specs.append(diffs[i][2])
        else:
            out_shape.append(SDS(ins[i][0].shape, F32))
            out_specs.append(ins[i][1])
    aliases = {n_in + n_ct + n_add + k: didx.index(i) for k, (i, _) in enumerate(intos)}
    in_specs = [s for _, s in ins] + [s for _, s in cts] + [s for _, (_, s) in adds] + [ANY] * n_into
    operands = [a for a, _ in ins] + [a for a, _ in cts] + [a for _, (a, _) in adds] + [a for _, a in intos]
    return pl.pallas_call(kern, grid=grid, in_specs=in_specs, out_specs=out_specs, out_shape=out_shape, input_output_aliases=aliases,
                          name=name, compiler_params=_cparams())(*operands)


def rspec(tl, w, cb=0):
    return pl.BlockSpec((tl, w), lambda i: (i, cb))


def cspec(shape):
    return pl.BlockSpec(shape, lambda i: (0,) * len(shape))


def row_fwd(name, fn, rows, tl, row_ins, consts, outs):
    ins = [(a, rspec(tl, w, cb)) for a, w, cb in row_ins] + [(a, cspec(a.shape)) for a in consts]
    return stage(name, fn, (rows // tl,), ins, [(SDS((rows, w), dt), rspec(tl, w)) for w, dt in outs])


def row_bwd(name, fn, rows, tl, row_ins, consts, cts, row_diff, const_diff):
    ins = [(a, rspec(tl, w, cb)) for a, w, cb in row_ins] + [(a, cspec(a.shape)) for a in consts]
    diffs = []
    for (a, w, cb), d in zip(row_ins, row_diff):
        if not d:
            diffs.append(None)
            continue
        d = d if isinstance(d, dict) else {}
        opts = {}
        if "add" in d:
            opts["add"] = (d["add"][0], rspec(tl, d["add"][1], d["add"][2]))
        if d.get("into") is not None:
            opts["into"] = d["into"]
        diffs.append(("row", SDS((rows, d.get("cols", w)), F32), rspec(tl, w, d.get("cb", 0)), opts))
    diffs += [("acc", (0,)) if d else None for d in const_diff]
    return stage_bwd(name, fn, (rows // tl,), ins, [(a, rspec(tl, w, cb)) for a, w, cb in cts], diffs)


MATMUL_VMEM = 28 * 2**20
ADAMW_VMEM = 16 * 2**20


class Sharded:
    def __init__(self, arr, kind, roff, rows):
        self.arr, self.kind, self.roff, self.rows, self.n = arr, kind, roff, rows, arr.shape[2]
        self.shape = (rows, 4 * self.n) if kind == "col" else (4 * rows, self.n)

    def fits(self, t0, t1):
        return self.roff % t0 == 0 and self.rows % t0 == 0 and self.n % t1 == 0

    def spec(self, t0, t1, bidx):
        assert self.fits(t0, t1), (self.kind, self.roff, self.rows, self.n, t0, t1)
        r0 = self.roff // t0
        if self.kind == "col":
            per = self.n // t1
            return pl.BlockSpec((None, t0, t1), lambda *g: (bidx(*g)[1] // per, r0 + bidx(*g)[0], bidx(*g)[1] % per))
        per = self.rows // t0
        return pl.BlockSpec((None, t0, t1), lambda *g: (bidx(*g)[0] // per, r0 + bidx(*g)[0] % per, bidx(*g)[1]))


def matmul(name, a, b, mode, out_dtype=F32, add=None, out=None, into=None):
    if mode == "tn":
        k_dim, m = a.shape
    else:
        m, k_dim = a.shape
    n = b.shape[0] if mode == "nt" else b.shape[1]
    b_fit = b.fits if isinstance(b, Sharded) else (lambda t0, t1: True)
    o_fit = out.fits if out is not None else (lambda t0, t1: True)
    a_bytes, b_bytes = jnp.dtype(a.dtype).itemsize, jnp.dtype(b.arr.dtype if isinstance(b, Sharded) else b.dtype).itemsize
    o_bytes = jnp.dtype(out_dtype if out is None else out.arr.dtype).itemsize

    def vmem(tm, tn, tk):
        return 2 * (tm * tk * a_bytes + tk * tn * b_bytes + tm * tn * (o_bytes + (4 if add is not None else 0))) + 4 * tm * tn

    tiles = [(tm, tn, tk) for tm in (2048, 1024, 512, 256, 128) for tn in (1024, 768, 512, 384, 256, 128) for tk in (1024, 512, 384, 256, 128)
             if m % tm == 0 and n % tn == 0 and k_dim % tk == 0 and (b_fit(tn, tk) if mode == "nt" else b_fit(tk, tn)) and o_fit(tm, tn)
             and vmem(tm, tn, tk) <= MATMUL_VMEM]
    tm, tn, tk = max(tiles, key=lambda t: (t[0] * t[1] * t[2], t[0] * t[1]))
    nk = k_dim // tk
    a_spec = pl.BlockSpec((tk, tm), lambda i, j, k: (k, i)) if mode == "tn" else pl.BlockSpec((tm, tk), lambda i, j, k: (i, k))
    if isinstance(b, Sharded):
        b_spec = b.spec(tn, tk, lambda i, j, k: (j, k)) if mode == "nt" else b.spec(tk, tn, lambda i, j, k: (k, j))
        b = b.arr
    else:
        b_spec = pl.BlockSpec((tn, tk), lambda i, j, k: (j, k)) if mode == "nt" else pl.BlockSpec((tk, tn), lambda i, j, k: (k, j))
    o_spec = pl.BlockSpec((tm, tn), lambda i, j, k: (i, j))
    out_spec, out_shape = (o_spec, SDS((m, n), out_dtype)) if out is None else (out.spec(tm, tn, lambda i, j, k: (i, j)), out.arr)
    ca, cb = {"nn": (1, 0), "nt": (1, 1), "tn": (0, 0)}[mode]
    n_in = 2 + (add is not None)

    def kern(*refs):
        a_ref, b_ref = refs[0], refs[1]
        o_ref, acc = refs[-2], refs[-1]
        k = pl.program_id(2)

        @pl.when(k == 0)
        def _():
            acc[...] = jnp.zeros_like(acc)

        acc[...] += _dg(a_ref[...], b_ref[...], ca, cb)

        @pl.when(k == nk - 1)
        def _():
            r = acc[...]
            if add is not None:
                r = r + refs[2][...]
            o_ref[...] = r.astype(o_ref.dtype)

    ins, specs = [a, b], [a_spec, b_spec]
    if add is not None:
        ins.append(add)
        specs.append(o_spec)
    if into is not None:
        ins.append(into)
        specs.append(ANY)
    return pl.pallas_call(kern, grid=(m // tm, n // tn, nk), in_specs=specs, out_specs=out_spec, out_shape=out_shape,
                          scratch_shapes=[pltpu.VMEM((tm, tn), F32)], input_output_aliases={} if into is None else {n_in: 0},
                          name=name, compiler_params=_cparams())(*ins)


def _cmul(ar, ai, br, bi):
    return ar * br - ai * bi, ar * bi + ai * br


def _sub_shift(x, down):
    row = lax.broadcasted_iota(jnp.int32, x.shape, 0)
    if down:
        return jnp.where(row == 0, 0.0, pltpu.roll(x, 1, 0))
    return jnp.where(row == SEG - 1, 0.0, pltpu.roll(x, SEG - 1, 0))


def _pow_seg_len(ar, ai):
    for _ in range(int(math.log2(SEG_LEN))):
        ar, ai = _cmul(ar, ai, ar, ai)
    return ar, ai


def _scan_in_place(sr, si, a_re, a_im):
    lanes = sr.shape[1]
    ar = jnp.broadcast_to(a_re, (SEG, lanes))
    ai = jnp.broadcast_to(a_im, (SEG, lanes))
    zero = jnp.zeros((SEG, lanes), F32)

    def local(i, carry):
        rows = pl.ds(pl.multiple_of(i * SEG, SEG), SEG)
        mr, mi = _cmul(ar, ai, carry[0], carry[1])
        nr, ni = mr + sr[rows, :], mi + si[rows, :]
        sr[rows, :] = nr
        si[rows, :] = ni
        return nr, ni

    fr, fi = lax.fori_loop(0, SEG_LEN, local, (zero, zero))
    pr, pi = _pow_seg_len(ar, ai)
    ir, ii = zero, zero
    for _ in range(SEG - 1):
        mr, mi = _cmul(pr, pi, ir, ii)
        ir, ii = _sub_shift(mr + fr, True), _sub_shift(mi + fi, True)

    def carry_in(i, pw):
        rows = pl.ds(pl.multiple_of(i * SEG, SEG), SEG)
        cr, ci = _cmul(pw[0], pw[1], ir, ii)
        sr[rows, :] += cr
        si[rows, :] += ci
        return _cmul(pw[0], pw[1], ar, ai)

    lax.fori_loop(0, SEG_LEN, carry_in, (ar, ai))


S5_LANES = 8 * SP
S5_BLOCKS = SN // S5_LANES


def _s5_specs():
    u_spec = pl.BlockSpec((L, 8 * SC), lambda j: (0, j))
    s_spec = pl.BlockSpec((L, S5_LANES), lambda j: (0, j))
    wb_spec = pl.BlockSpec((S5_LANES, 8 * SC), lambda j: (j, 0))
    wc_spec = pl.BlockSpec((8 * SC, S5_LANES), lambda j: (j, 0))
    a_spec = pl.BlockSpec((1, S5_LANES), lambda j: (0, j))
    d_spec = pl.BlockSpec((1, 8 * SC), lambda j: (0, j))
    return u_spec, s_spec, wb_spec, wc_spec, a_spec, d_spec


def s5_forward(proj, wb_re, wb_im, wc_re, wc_im, a_re, a_im, d):
    def kern(u_ref, wbr, wbi, wcr, wci, ar, ai, d_ref, sr, si, g_ref):
        u = u_ref[...]
        sr[...], si[...] = fn_s5_bu(u, wbr[...], wbi[...])
        _scan_in_place(sr, si, ar[...], ai[...])
        g_ref[...] = fn_s5_out(sr[...], si[...], u, d_ref[...], wcr[...], wci[...])[0].astype(g_ref.dtype)

    u_spec, s_spec, wb_spec, wc_spec, a_spec, d_spec = _s5_specs()
    return pl.pallas_call(kern, grid=(S5_BLOCKS,), in_specs=[u_spec, wb_spec, wb_spec, wc_spec, wc_spec, a_spec, a_spec, d_spec],
                          out_specs=[s_spec, s_spec, u_spec], out_shape=[SDS((L, SN), F32)] * 2 + [SDS((L, PW), BF16)],
                          name="s5_forward", compiler_params=_cparams())(proj, wb_re, wb_im, wc_re, wc_im, a_re, a_im, d)


def _adjoint_scan_in_place(lr, li, sr, si, a_re, a_im):
    lanes = lr.shape[1]
    ar = jnp.broadcast_to(a_re, (SEG, lanes))
    ai = -jnp.broadcast_to(a_im, (SEG, lanes))
    zero = jnp.zeros((SEG, lanes), F32)

    def local(k, carry):
        i = SEG_LEN - 1 - k
        rows = pl.ds(pl.multiple_of(i * SEG, SEG), SEG)
        mr, mi = _cmul(ar, ai, carry[0], carry[1])
        nr, ni = mr + lr[rows, :], mi + li[rows, :]
        lr[rows, :] = nr
        li[rows, :] = ni
        return nr, ni

    fr, fi = lax.fori_loop(0, SEG_LEN, local, (zero, zero))
    pr, pi = _pow_seg_len(ar, ai)
    ir, ii = zero, zero
    for _ in range(SEG - 1):
        mr, mi = _cmul(pr, pi, ir, ii)
        ir, ii = _sub_shift(mr + fr, False), _sub_shift(mi + fi, False)

    def fix(rows, pw):
        cr, ci = _cmul(pw[0], pw[1], ir, ii)
        tr, ti = lr[rows, :] + cr, li[rows, :] + ci
        lr[rows, :] = tr
        li[rows, :] = ti
        return tr, ti

    def grad_a(tr, ti, spr, spi, acc):
        return acc[0] + tr * spr + ti * spi, acc[1] + ti * spr - tr * spi

    def carry_in(k, c):
        i = SEG_LEN - 1 - k
        rows = pl.ds(pl.multiple_of(i * SEG, SEG), SEG)
        prev = pl.ds(pl.multiple_of((i - 1) * SEG, SEG), SEG)
        tr, ti = fix(rows, (c[0], c[1]))
        acc = grad_a(tr, ti, sr[prev, :], si[prev, :], (c[2], c[3]))
        nr, ni = _cmul(c[0], c[1], ar, ai)
        return nr, ni, acc[0], acc[1]

    pwr, pwi, accr, acci = lax.fori_loop(0, SEG_LEN - 1, carry_in, (ar, ai, zero, zero))
    tr, ti = fix(pl.ds(0, SEG), (pwr, pwi))
    last = pl.ds((SEG_LEN - 1) * SEG, SEG)
    accr, acci = grad_a(tr, ti, _sub_shift(sr[last, :], True), _sub_shift(si[last, :], True), (accr, acci))
    return jnp.sum(accr, axis=0, keepdims=True), jnp.sum(acci, axis=0, keepdims=True)


S5_BWD_VMEM = 58 * 2**20


def s5_backward(dg, proj, s_re, s_im, wb_re, wb_im, wc_re, wc_im, a_re, a_im, d, dproj):
    def kern(dg_ref, u_ref, sr, si, wbr, wbi, wcr, wci, ar, ai, d_ref, _, du_ref, dd_ref, dwcr, dwci, dwbr, dwbi, dar, dai, lr, li):
        u = u_ref[...]
        _, vjp_out = jax.vjp(fn_s5_out, sr[...], si[...], u, d_ref[...], wcr[...], wci[...])
        lr[...], li[...], du_out, dd_ref[...], dwcr[...], dwci[...] = vjp_out((dg_ref[...],))
        dar[...], dai[...] = _adjoint_scan_in_place(lr, li, sr, si, ar[...], ai[...])
        _, vjp_in = jax.vjp(fn_s5_bu, u, wbr[...], wbi[...])
        du_in, dwbr[...], dwbi[...] = vjp_in((lr[...], li[...]))
        du_ref[...] = du_out + du_in

    u_spec, s_spec, wb_spec, wc_spec, a_spec, d_spec = _s5_specs()
    outs = [(SDS(dproj.shape, F32), u_spec), (SDS(d.shape, F32), d_spec), (SDS(wc_re.shape, F32), wc_spec), (SDS(wc_im.shape, F32), wc_spec),
            (SDS(wb_re.shape, F32), wb_spec), (SDS(wb_im.shape, F32), wb_spec), (SDS(a_re.shape, F32), a_spec), (SDS(a_im.shape, F32), a_spec)]
    return pl.pallas_call(kern, grid=(S5_BLOCKS,),
                          in_specs=[u_spec, u_spec, s_spec, s_spec, wb_spec, wb_spec, wc_spec, wc_spec, a_spec, a_spec, d_spec, ANY],
                          out_specs=[sp for _, sp in outs], out_shape=[sd for sd, _ in outs], input_output_aliases={11: 0},
                          scratch_shapes=[pltpu.VMEM((L, S5_LANES), F32)] * 2, name="s5_backward",
                          compiler_params=pltpu.CompilerParams(vmem_limit_bytes=S5_BWD_VMEM))(
        dg, proj, s_re, s_im, wb_re, wb_im, wc_re, wc_im, a_re, a_im, d, dproj)


def fn_rms(x, g):
    return (rms(x, g),)


def fn_s5_disc(lre, lim, ls):
    step = jnp.exp(ls)
    e = jnp.exp(lre * step)
    a_re, a_im = e * jnp.cos(lim * step), e * jnp.sin(lim * step)
    den = lre * lre + lim * lim
    nr, ni = a_re - 1.0, a_im
    return a_re, a_im, (nr * lre + ni * lim) / den, (ni * lre - nr * lim) / den


def _group_mask(rows, cols, row_div, col_div):
    r = lax.broadcasted_iota(jnp.int32, (rows, cols), 0) // row_div % 8
    c = lax.broadcasted_iota(jnp.int32, (rows, cols), 1) // col_div
    return r == c


def fn_s5_bmat(b_re, b_im, coef_re, coef_im):
    rows = b_re.shape[0]
    mask = _group_mask(rows, 8 * SC, SP, SC)
    bb_re = coef_re * b_re - coef_im * b_im
    bb_im = coef_re * b_im + coef_im * b_re
    return jnp.where(mask, jnp.tile(bb_re, (1, 8)), 0.0), jnp.where(mask, jnp.tile(bb_im, (1, 8)), 0.0)


def fn_s5_cmat(c_re, c_im):
    rows = c_re.shape[0]
    mask = _group_mask(rows, 8 * SP, SC, SP)
    return jnp.where(mask, jnp.tile(c_re, (1, 8)), 0.0), jnp.where(mask, jnp.tile(c_im, (1, 8)), 0.0)


def fn_s5_bu(u, wb_re, wb_im):
    return mm_nt(u, wb_re), mm_nt(u, wb_im)


def fn_s5_out(sr, si, u, d, wc_re, wc_im):
    y = mm_nt(sr, wc_re) - mm_nt(si, wc_im) + d * u
    return (jax.nn.gelu(y),)


def fn_merge_glu(z, mo, gate):
    yg = z[:, :PW] * jax.nn.sigmoid(z[:, PW:])
    return (jnp.concatenate([yg, mo], axis=1) * silu(gate),)


def fn_merge(prim, mo, gate):
    return (jnp.concatenate([prim, mo], axis=1) * silu(gate),)


def fn_mem_k(kv, g):
    return (jnp.concatenate([rms(kv[:, h * XHD:(h + 1) * XHD], g) for h in range(XH)], axis=1),)


def fn_mem_attn(xq, kn, v, g):
    outs = []
    for h in range(XH):
        sl = slice(h * XHD, (h + 1) * XHD)
        p = softmax_rows(mm_nt(rms(xq[:, sl], g), kn[:, sl]) * (XHD ** -0.5))
        outs.append(mm_nn(p, v[:, sl]))
    return (jnp.concatenate(outs, axis=1),)


def _half_rms(x, g):
    lo = lax.broadcasted_iota(jnp.int32, x.shape, 1) < ROPE
    x2 = x * x
    s_lo = jnp.sum(jnp.where(lo, x2, 0.0), axis=1, keepdims=True)
    s_hi = jnp.sum(jnp.where(lo, 0.0, x2), axis=1, keepdims=True)
    return x * lax.rsqrt(jnp.where(lo, s_lo, s_hi) / ROPE + EPS) * g


def _rope(x, cos2, sin_signed):
    first = lax.broadcasted_iota(jnp.int32, x.shape, 1) % ROPE < ROPE // 2
    return x * cos2 + jnp.where(first, lane_roll(x, 128 - ROPE // 2), lane_roll(x, ROPE // 2)) * sin_signed


def fn_mla_prep(q, kv, kr, cos2, sin_signed, qnn, knn, qrn, krn):
    lo = lax.broadcasted_iota(jnp.int32, kr.shape, 1) < ROPE
    kr_pad = jnp.where(lo, _rope(_half_rms(kr, krn), cos2, sin_signed), 0.0)
    qf, kf, vs = [], [], []
    for m in range(MH // 2):
        pair = _rope(_half_rms(q[:, MH * NOPE + 128 * m:MH * NOPE + 128 * (m + 1)], qrn), cos2, sin_signed)
        for h, rope_h in ((2 * m, pair), (2 * m + 1, lane_roll(pair, ROPE))):
            qf.append(jnp.concatenate([rms(q[:, NOPE * h:NOPE * (h + 1)], qnn), jnp.where(lo, rope_h, 0.0)], axis=1))
    for h in range(MH):
        kf.append(jnp.concatenate([rms(kv[:, 256 * h:256 * h + NOPE], knn), kr_pad], axis=1))
        vs.append(kv[:, 256 * h + NOPE:256 * (h + 1)])
    return jnp.stack(qf), jnp.stack(kf), jnp.stack(vs)


ATT_TQ = 256


def _attn_tile(q, kf, v, q_start):
    s = mm_nt(q, kf) * ((NOPE + ROPE) ** -0.5)
    q_pos = q_start + lax.broadcasted_iota(jnp.int32, s.shape, 0)
    k_pos = lax.broadcasted_iota(jnp.int32, s.shape, 1)
    p = softmax_rows(jnp.where(k_pos <= q_pos, s, jnp.finfo(F32).min))
    return mm_nn(p, v)


def _attn_specs():
    q_spec = pl.BlockSpec((None, ATT_TQ, 256), lambda h, i: (h, i, 0))
    k_spec = pl.BlockSpec((None, L, 256), lambda h, i: (h, 0, 0))
    v_spec = pl.BlockSpec((None, L, 128), lambda h, i: (h, 0, 0))
    o_spec = pl.BlockSpec((ATT_TQ, 128), lambda h, i: (i, h))
    return q_spec, k_spec, v_spec, o_spec


def causal_attn(qf, kf, vh):
    n_tiles = L // ATT_TQ

    def kern(q_ref, k_ref, v_ref, o_ref):
        i = pl.program_id(1)
        for t in range(n_tiles):
            @pl.when(i == t)
            def _(t=t):
                keys = (t + 1) * ATT_TQ
                o_ref[...] = _attn_tile(q_ref[...], k_ref[:keys, :], v_ref[:keys, :], t * ATT_TQ)

    q_spec, k_spec, v_spec, o_spec = _attn_specs()
    return pl.pallas_call(kern, grid=(MH, n_tiles), in_specs=[q_spec, k_spec, v_spec], out_specs=o_spec,
                          out_shape=SDS((L, MH * VD), F32), name="l1_attn", compiler_params=_cparams())(qf, kf, vh)


def causal_attn_bwd(qf, kf, vh, dout):
    n_tiles = L // ATT_TQ

    def kern(q_ref, k_ref, v_ref, do_ref, dq_ref, dk_ref, dv_ref):
        i = pl.program_id(1)

        @pl.when(i == 0)
        def _():
            dk_ref[...] = jnp.zeros_like(dk_ref)
            dv_ref[...] = jnp.zeros_like(dv_ref)

        for t in range(n_tiles):
            @pl.when(i == t)
            def _(t=t):
                keys = (t + 1) * ATT_TQ
                _, vjp = jax.vjp(lambda q, k, v: _attn_tile(q, k, v, t * ATT_TQ),
                                 q_ref[...].astype(F32), k_ref[:keys, :].astype(F32), v_ref[:keys, :].astype(F32))
                dq, dk, dv = vjp(do_ref[...])
                dq_ref[...] = dq
                dk_ref[:keys, :] += dk
                dv_ref[:keys, :] += dv

    q_spec, k_spec, v_spec, o_spec = _attn_specs()
    return pl.pallas_call(kern, grid=(MH, n_tiles), in_specs=[q_spec, k_spec, v_spec, o_spec], out_specs=[q_spec, k_spec, v_spec],
                          out_shape=[SDS(qf.shape, F32), SDS(kf.shape, F32), SDS(vh.shape, F32)], name="l1_attn_bwd",
                          compiler_params=_cparams())(qf, kf, vh, dout)


def loss_and_grad(y, target, tl=256):
    def kern(y_ref, t_ref, dy_ref, loss_ref):
        d = y_ref[...] - t_ref[...]
        dy_ref[...] = d / D

        @pl.when(pl.program_id(0) == 0)
        def _():
            loss_ref[...] = jnp.zeros_like(loss_ref)

        loss_ref[...] += 0.5 * jnp.sum(jnp.sum(d * d, axis=1, keepdims=True), axis=0, keepdims=True) / D

    return pl.pallas_call(kern, grid=(L // tl,), in_specs=[rspec(tl, D), rspec(tl, D)], out_specs=[rspec(tl, D), cspec((1, 1))],
                          out_shape=[SDS((L, D), F32), SDS((1, 1), F32)], name="loss", compiler_params=_cparams())(y, target)


def adamw(name, w, g, m, v):
    rows, cols = w.shape
    block_row_bytes = 7 * 2 * 4 * max(cols, 128)
    tr = _row_tile(rows, min(2048, ADAMW_VMEM // block_row_bytes // 8 * 8), 8)

    def kern(w_ref, g_ref, m_ref, v_ref, d_ref, nm_ref, nv_ref):
        gg = g_ref[...]
        nm = ADAM_B1 * m_ref[...] + (1.0 - ADAM_B1) * gg
        nv = ADAM_B2 * v_ref[...] + (1.0 - ADAM_B2) * jnp.square(gg)
        m_hat = nm / (1.0 - ADAM_B1 ** ADAM_STEP)
        v_hat = nv / (1.0 - ADAM_B2 ** ADAM_STEP)
        d_ref[...] = -ADAM_LR * (m_hat / (jnp.sqrt(v_hat) + ADAM_EPS) + ADAM_WD * w_ref[...])
        nm_ref[...] = nm
        nv_ref[...] = nv

    spec = rspec(tr, cols)
    return pl.pallas_call(kern, grid=(rows // tr,), in_specs=[spec] * 4, out_specs=[spec] * 3,
                          out_shape=[SDS((rows, cols), F32)] * 3, name=name, compiler_params=_cparams())(w, g, m, v)


def _row_tile(rows, cap=512, unit=16):
    return max(t for t in range(unit, cap + 1, unit) if rows % t == 0)


ANY = pl.BlockSpec(memory_space=pl.ANY)
MESH_ID = pl.DeviceIdType.MESH


def _place():
    x, y, c = lax.axis_index("x"), lax.axis_index("y"), lax.axis_index("c")
    return x, y, c, [(1 - x, y), (x, 1 - y), (1 - x, 1 - y)]


def _row_chunks(rows, n, dtype):
    unit = 32 // jnp.dtype(dtype).itemsize
    base, extra = divmod(rows // unit, n)
    out, start = [], 0
    for k in range(n):
        size = (base + (k < extra)) * unit
        if size:
            out.append((start, size))
            start += size
    assert start == rows, (rows, unit)
    return out


def _dma_sems(n):
    return [pltpu.SemaphoreType.DMA((n,)), pltpu.SemaphoreType.DMA((n,))]


PIECE_BYTES = 1 << 20


def _pieces(shapes_dtypes, rows_of):
    out = []
    for b, (shape, dtype) in enumerate(shapes_dtypes):
        rows = rows_of(shape)
        n = max(1, min(4, rows * shape[-1] * jnp.dtype(dtype).itemsize // PIECE_BYTES))
        out += [(b, st, sz) for st, sz in _row_chunks(rows, n, dtype)]
    return out


def all_gather_chips(name, shards):
    nb = len(shards)
    pieces = _pieces([(s.shape, s.dtype) for s in shards], lambda shape: shape[0] // 2)
    n = len(pieces)

    def body(*refs):
        x_refs, out_refs, send_sems, recv_sems = refs[:nb], refs[nb:2 * nb], refs[2 * nb], refs[2 * nb + 1]
        x, y, c, chips = _place()
        sibling = (x, y, 1 - c)
        mine = 2 * x + y

        def copy(sem, chip, cc, k, to, from_input=False):
            b, st, sz = pieces[k]
            rows_k = pl.ds(cc * (x_refs[b].shape[0] // 2) + st, sz)
            dst = out_refs[b].at[chip, rows_k, :]
            return pltpu.make_async_remote_copy(src_ref=x_refs[b].at[rows_k, :] if from_input else dst, dst_ref=dst,
                                                send_sem=send_sems.at[sem], recv_sem=recv_sems.at[sem], device_id=to, device_id_type=MESH_ID)

        order = [(k, j, 2 * cx + cy, (cx, cy, c)) for k in range(n) for j, (cx, cy) in enumerate(chips)]
        first = [copy(j * n + k, mine, c, k, to, from_input=True) for k, j, _, to in order]
        for cp in first:
            cp.start()
        passed = []
        for k, j, chip, _ in order:
            copy(j * n + k, chip, c, k, sibling).wait_recv()
            passed.append(copy((3 + j) * n + k, chip, c, k, sibling))
            passed[-1].start()
        for k, j, chip, _ in order:
            copy((3 + j) * n + k, chip, 1 - c, k, sibling).wait_recv()
        for cp in first + passed:
            cp.wait_send()

    return pl.pallas_call(body, in_specs=[ANY] * nb, out_specs=[ANY] * nb, out_shape=[SDS((4,) + s.shape, s.dtype) for s in shards],
                          scratch_shapes=_dma_sems(6 * n), name=name)(*shards)


def pair_exchange(name, gs):
    nb = len(gs)
    pieces = _pieces([(g.shape, g.dtype) for g in gs], lambda shape: shape[1] // 2)

    def body(*refs):
        g_refs, got_refs, send_sems, recv_sems = refs[:nb], refs[nb:2 * nb], refs[2 * nb], refs[2 * nb + 1]
        x, y, c, _ = _place()
        swaps = [pltpu.make_async_remote_copy(src_ref=g_refs[b].at[:, pl.ds((1 - c) * (g_refs[b].shape[1] // 2) + st, sz), :],
                                              dst_ref=got_refs[b].at[:, pl.ds(st, sz), :], send_sem=send_sems.at[k], recv_sem=recv_sems.at[k],
                                              device_id=(x, y, 1 - c), device_id_type=MESH_ID)
                 for k, (b, st, sz) in enumerate(pieces)]
        for cp in swaps:
            cp.start()
        for cp in swaps:
            cp.wait()

    return pl.pallas_call(body, in_specs=[ANY] * nb, out_specs=[ANY] * nb,
                          out_shape=[SDS((g.shape[0], g.shape[1] // 2, g.shape[2]), g.dtype) for g in gs],
                          scratch_shapes=_dma_sems(len(pieces)), name=name)(*gs)


def chip_scatter(name, ps):
    nb = len(ps)
    pieces = _pieces([(p.shape, p.dtype) for p in ps], lambda shape: shape[1])
    n = len(pieces)

    def body(*refs):
        p_refs, q_refs, send_sems, recv_sems = refs[:nb], refs[nb:2 * nb], refs[2 * nb], refs[2 * nb + 1]
        x, y, c, chips = _place()
        mine = 2 * x + y

        def copy(j, k, src_slot, dst_slot, to):
            b, st, sz = pieces[k]
            return pltpu.make_async_remote_copy(src_ref=p_refs[b].at[src_slot, pl.ds(st, sz), :], dst_ref=q_refs[b].at[dst_slot, pl.ds(st, sz), :],
                                                send_sem=send_sems.at[j * n + k], recv_sem=recv_sems.at[j * n + k], device_id=to,
                                                device_id_type=MESH_ID)

        order = [(k, j, 2 * cx + cy, (cx, cy, c)) for k in range(n) for j, (cx, cy) in enumerate(chips)]
        sends = [copy(j, k, chip, mine, to) for k, j, chip, to in order]
        for cp in sends:
            cp.start()
        for k, j, chip, to in order:
            copy(j, k, mine, chip, to).wait_recv()
        for cp in sends:
            cp.wait_send()

    return pl.pallas_call(body, in_specs=[ANY] * nb, out_specs=[ANY] * nb, out_shape=[SDS(p.shape, p.dtype) for p in ps],
                          scratch_shapes=_dma_sems(3 * n), name=name)(*ps)


def pair_join(name, bufs):
    nb = len(bufs)
    pieces = _pieces([(b.shape, b.dtype) for b in bufs], lambda shape: shape[0] // 2)

    def body(*refs):
        out_refs, send_sems, recv_sems = refs[nb:2 * nb], refs[2 * nb], refs[2 * nb + 1]
        x, y, c, _ = _place()

        def copy(k, cc):
            b, st, sz = pieces[k]
            rows_k = out_refs[b].at[pl.ds(cc * (out_refs[b].shape[0] // 2) + st, sz), :]
            return pltpu.make_async_remote_copy(src_ref=rows_k, dst_ref=rows_k, send_sem=send_sems.at[k], recv_sem=recv_sems.at[k],
                                                device_id=(x, y, 1 - c), device_id_type=MESH_ID)

        gives = [copy(k, c) for k in range(len(pieces))]
        for cp in gives:
            cp.start()
        for k in range(len(pieces)):
            copy(k, 1 - c).wait_recv()
        for cp in gives:
            cp.wait_send()

    return pl.pallas_call(body, in_specs=[ANY] * nb, out_specs=[ANY] * nb, out_shape=[SDS(b.shape, b.dtype) for b in bufs],
                          input_output_aliases={i: i for i in range(nb)}, scratch_shapes=_dma_sems(len(pieces)), name=name)(*bufs)


def pair_add(name, g, got, place):
    slots, rows, cols = g.shape
    half = rows // 2
    tr = _row_tile(half)
    nb = half // tr

    def kern(_, g_ref, t_ref, o_ref):
        o_ref[...] = (g_ref[...].astype(F32) + t_ref[...].astype(F32)).astype(o_ref.dtype)

    blk = pl.BlockSpec((None, tr, cols), lambda s, i, p: (s, i, 0))
    grid_spec = pltpu.PrefetchScalarGridSpec(
        num_scalar_prefetch=1, grid=(slots, nb),
        in_specs=[pl.BlockSpec((None, tr, cols), lambda s, i, p: (s, p[1] * nb + i, 0)), blk], out_specs=blk)
    return pl.pallas_call(kern, grid_spec=grid_spec, out_shape=SDS((slots, half, cols), g.dtype), name=name,
                          compiler_params=_cparams())(place, g, got)


def chip_add(name, p, q, place):
    slots, half, cols = p.shape
    tr = _row_tile(half)
    nb = half // tr

    def kern(_, p_ref, q1, q2, q3, o_ref):
        o_ref[...] = p_ref[...].astype(F32) + q1[...].astype(F32) + q2[...].astype(F32) + q3[...].astype(F32)

    def slot(k):
        return pl.BlockSpec((None, tr, cols), lambda i, pr: ((pr[0] + k) % slots, i, 0))

    grid_spec = pltpu.PrefetchScalarGridSpec(
        num_scalar_prefetch=1, grid=(nb,), in_specs=[slot(0), slot(1), slot(2), slot(3)],
        out_specs=pl.BlockSpec((tr, cols), lambda i, pr: (pr[1] * nb + i, 0)))
    return pl.pallas_call(kern, grid_spec=grid_spec, out_shape=SDS((2 * half, cols), F32), name=name,
                          compiler_params=_cparams())(place, p, q, q, q)


def reduce_scatter_chips(gs, place):
    gots = pair_exchange("rs_pair_exchange", gs)
    pairs = [pair_add(f"rs_pair_add_{i}", g, got, place) for i, (g, got) in enumerate(zip(gs, gots))]
    qs = chip_scatter("rs_chip_scatter", pairs)
    return pair_join("rs_pair_join", [chip_add(f"rs_chip_add_{i}", p, q, place) for i, (p, q) in enumerate(zip(pairs, qs))])


BIG = [("w_out", (2, 512, 1024)), ("w_mem_kv", (2, 256, 1024)), ("s5_w_in", (1, 1024, 1024)), ("s5_w_glu", (1, 1536, 768)),
       ("mla_w_in", (1, 1024, 848)), ("mla_w_uq", (1, 512, 576)), ("mla_w_ukv", (1, 256, 768))]
SHARDED_SMALL = [("mla_q_lora_norm", (1, 128)), ("mla_kv_lora_norm", (1, 64))]
SMALL = [("ln_gain", (2, 1024)), ("mem_norm", (2, 1024)), ("xq_norm", (2, 128)), ("xk_norm", (2, 128)),
         ("s5_lambda_re", (1, 96, 64)), ("s5_lambda_im", (1, 96, 64)), ("s5_log_step", (1, 96)),
         ("s5_b_re", (1, 96, 64, 16)), ("s5_b_im", (1, 96, 64, 16)), ("s5_c_re", (1, 96, 16, 64)), ("s5_c_im", (1, 96, 16, 64)),
         ("s5_d", (1, 1536)), ("mla_q_nope_norm", (1, 128)), ("mla_k_nope_norm", (1, 128)), ("mla_q_rope_norm", (1, 64)),
         ("mla_k_rope_norm", (1, 64))]
WEIGHT_ORDER = ["ln_gain", "w_out", "mem_norm", "w_mem_kv", "xq_norm", "xk_norm", "s5_w_in", "s5_lambda_re", "s5_lambda_im",
                "s5_log_step", "s5_b_re", "s5_b_im", "s5_c_re", "s5_c_im", "s5_d", "s5_w_glu", "mla_w_in", "mla_q_lora_norm",
                "mla_kv_lora_norm", "mla_w_uq", "mla_w_ukv", "mla_q_nope_norm", "mla_k_nope_norm", "mla_q_rope_norm", "mla_k_rope_norm"]
SMALL_FULL = SMALL + [(n, (1, 4 * s[1])) for n, s in SHARDED_SMALL]
N_SMALL = sum(math.prod(s) for _, s in SMALL_FULL)
SMALL_ROWS, SMALL_LANES = 128, 1024

WIDE_ROWS, MID_ROWS = 2560, 1792
OUT_ROFF, MKV_ROFF, IN0_ROFF = (0, 512), (1024, 1280), 1536
GLU_ROFF, UKV_ROFF = 0, 1536


def weight_views(wide, mid):
    return {"w_out": [Sharded(wide, "row", r, 512) for r in OUT_ROFF], "w_mem_kv": [Sharded(wide, "row", r, 256) for r in MKV_ROFF],
            "s5_w_in": Sharded(wide, "col", IN0_ROFF, 1024), "s5_w_glu": Sharded(mid, "col", GLU_ROFF, 1536),
            "mla_w_ukv": Sharded(mid, "col", UKV_ROFF, 256)}


def stack_shards(w, dtype):
    wide = jnp.concatenate([w["w_out"].reshape(1024, 1024), w["w_mem_kv"].reshape(512, 1024), w["s5_w_in"].reshape(1024, 1024)], axis=0)
    mid = jnp.concatenate([w["s5_w_glu"].reshape(1536, 768), w["mla_w_ukv"].reshape(256, 768)], axis=0)
    return wide.astype(dtype), mid.astype(dtype)


def cols_to_shards(full):
    return full.reshape(full.shape[0], 4, full.shape[1] // 4).transpose(1, 0, 2)


def shards_to_cols(arr):
    return arr.transpose(1, 0, 2).reshape(arr.shape[1], 4 * arr.shape[2])


def mla_in_permute(w):
    o1, o2, o3, o4 = QL, QL + KVL, QL + KVL + ROPE, QL + KVL + ROPE + XQW
    return jnp.concatenate([w[:, o4:], w[:, :o1], w[:, o3:o4], w[:, o1:o2], w[:, o2:o3],
                            jnp.zeros((w.shape[0], MLA_IN_P - MLA_IN), w.dtype)], axis=1)


def mla_in_unpermute(d):
    return jnp.concatenate([d[:, 2048:2560], d[:, 3072:3328], d[:, 3328:3392], d[:, 2560:3072], d[:, :2048]], axis=1)


def uq_permute(w):
    w3 = w.reshape(w.shape[0], MH, NOPE + ROPE)
    return jnp.concatenate([w3[:, :, :NOPE].reshape(w.shape[0], MH * NOPE), w3[:, :, NOPE:].reshape(w.shape[0], MH * ROPE)], axis=1)


def uq_unpermute(d):
    dn = d[:, :MH * NOPE].reshape(d.shape[0], MH, NOPE)
    dr = d[:, MH * NOPE:].reshape(d.shape[0], MH, ROPE)
    return jnp.concatenate([dn, dr], axis=2).reshape(d.shape[0], MH * (NOPE + ROPE))


def time_permute(a):
    return a.reshape(SEG, SEG_LEN, a.shape[-1]).transpose(1, 0, 2).reshape(L, a.shape[-1])


def time_unpermute(a):
    return a.reshape(SEG_LEN, SEG, a.shape[-1]).transpose(1, 0, 2).reshape(L, a.shape[-1])


def mem_branch_fwd(tag, mem, mem_norm, w_mem_kv, xk_norm):
    mn = row_fwd(tag + "_mem_rms", fn_rms, ML, ML, [(mem, D, 0)], [mem_norm], [(D, BF16)])[0]
    kv = matmul(tag + "_mem_kv", mn, w_mem_kv, "nn")
    kn = row_fwd(tag + "_mem_knorm", fn_mem_k, ML, ML, [(kv, XQW, 0)], [xk_norm], [(XQW, F32)])[0]
    return mn, kv, kn


def mem_branch_bwd(tag, mem, mem_norm, w_mem_kv, xk_norm, mn, kv, dkn, dv, g_view, g_wide):
    dk, dxk = row_bwd(tag + "_mem_knorm_bwd", fn_mem_k, ML, ML, [(kv, XQW, 0)], [xk_norm], [(dkn, XQW, 0)], [True], [True])
    dkv = jnp.concatenate([dk, dv], axis=1)
    dmn = matmul(tag + "_mem_kv_dx", dkv, w_mem_kv, "nt")
    g_wide = matmul(tag + "_mem_kv_dw", mn, dkv, "tn", out=g_view, into=g_wide)
    dmem_norm = row_bwd(tag + "_mem_rms_bwd", fn_rms, ML, ML, [(mem, D, 0)], [mem_norm], [(dmn, D, 0)], [False], [True])[0]
    return g_wide, dmem_norm, dxk


def mem_attn_fwd(tag, proj, cb, kn, kv, xq_norm):
    return row_fwd(tag + "_mem_attn", fn_mem_attn, L, 256, [(proj, XQW, cb)], [kn, kv[:, XQW:], xq_norm], [(XQW, F32)])[0]


def mem_attn_bwd(tag, proj, cb, kn, kv, xq_norm, dmo, dproj):
    place = {"cols": proj.shape[1], "cb": cb, "into": dproj}
    return row_bwd(tag + "_mem_attn_bwd", fn_mem_attn, L, 256, [(proj, XQW, cb)], [kn, kv[:, XQW:], xq_norm], [(dmo, XQW, 0)],
                   [place], [True, True, True])


def device_step(x, mem, positions, target, wide, mid, w_in1, w_uq, small):
    g = {}
    w = weight_views(wide, mid)
    gw = weight_views(SDS(wide.shape, BF16), SDS(mid.shape, BF16))
    ln, mem_norm, xq_norm, xk_norm = small["ln_gain"], small["mem_norm"], small["xq_norm"], small["xk_norm"]

    lre, lim = small["s5_lambda_re"][0], small["s5_lambda_im"][0]
    ls = small["s5_log_step"].reshape(SG, 1)
    one = pl.BlockSpec((SG, SP), lambda i: (0, 0))
    col = pl.BlockSpec((SG, 1), lambda i: (0, 0))
    disc_ins = [(lre, one), (lim, one), (ls, col)]
    a_re, a_im, coef_re, coef_im = stage("s5_disc", fn_s5_disc, (1,), disc_ins, [(SDS((SG, SP), F32), one)] * 4)
    b_re, b_im = small["s5_b_re"].reshape(SN, SC), small["s5_b_im"].reshape(SN, SC)
    c_re, c_im = small["s5_c_re"].reshape(PW, SP), small["s5_c_im"].reshape(PW, SP)
    bmat_rows = [(b_re, SC, 0), (b_im, SC, 0), (coef_re.reshape(SN, 1), 1, 0), (coef_im.reshape(SN, 1), 1, 0)]
    wb_re, wb_im = row_fwd("s5_bmat", fn_s5_bmat, SN, 512, bmat_rows, [], [(128, F32)] * 2)
    cmat_rows = [(c_re, SP, 0), (c_im, SP, 0)]
    wc_re, wc_im = row_fwd("s5_cmat", fn_s5_cmat, PW, 128, cmat_rows, [], [(512, F32)] * 2)
    a_re_v, a_im_v = a_re.reshape(1, SN), a_im.reshape(1, SN)
    s5_d = small["s5_d"]

    xp = time_permute(x)
    h0 = row_fwd("l0_rms", fn_rms, L, 256, [(xp, D, 0)], [ln[0:1]], [(D, BF16)])[0]
    proj0 = matmul("l0_in", h0, w["s5_w_in"], "nn")
    s_re, s_im, g0 = s5_forward(proj0, wb_re, wb_im, wc_re, wc_im, a_re_v, a_im_v, s5_d)
    z0 = matmul("l0_glu", g0, w["s5_w_glu"], "nn")
    mn0, kv0, kn0 = mem_branch_fwd("l0", mem, mem_norm[0:1], w["w_mem_kv"][0], xk_norm[0:1])
    mo0 = mem_attn_fwd("l0", proj0, 3, kn0, kv0, xq_norm[0:1])
    o0 = row_fwd("l0_merge", fn_merge_glu, L, 256, [(z0, 2 * PW, 0), (mo0, XQW, 0), (proj0, BW, 1)], [], [(BW, BF16)])[0]
    x1p = matmul("l0_out", o0, w["w_out"][0], "nn", add=xp)
    x1 = time_unpermute(x1p)

    h1 = row_fwd("l1_rms", fn_rms, L, 256, [(x1, D, 0)], [ln[1:2]], [(D, BF16)])[0]
    proj1 = matmul("l1_in", h1, w_in1, "nn")
    qln, kvln = small["mla_q_lora_norm"].reshape(1, QL), small["mla_kv_lora_norm"].reshape(1, KVL)
    cqn = row_fwd("l1_q_lora_rms", fn_rms, L, 256, [(proj1, QL, 4)], [qln], [(QL, BF16)])[0]
    ckvn = row_fwd("l1_kv_lora_rms", fn_rms, L, 256, [(proj1, KVL, 12)], [kvln], [(KVL, BF16)])[0]
    q = matmul("l1_uq", cqn, w_uq, "nn")
    kv = matmul("l1_ukv", ckvn, w["mla_w_ukv"], "nn")
    inv_freq = ROPE_THETA ** (-jnp.arange(ROPE // 2, dtype=F32) / (ROPE // 2))
    ang = positions.astype(F32)[:, None] * inv_freq
    cos2 = jnp.tile(jnp.cos(ang), (1, 4))
    sin_signed = jnp.tile(jnp.concatenate([-jnp.sin(ang), jnp.sin(ang)], axis=1), (1, 2))
    qnn, knn = small["mla_q_nope_norm"], small["mla_k_nope_norm"]
    qrn, krn = jnp.tile(small["mla_q_rope_norm"], (1, 2)), jnp.tile(small["mla_k_rope_norm"], (1, 2))
    tp = 256
    prep_ins = [(q, rspec(tp, MH * (NOPE + ROPE))), (kv, rspec(tp, MH * 256)), (proj1, rspec(tp, 128, 26)),
                (cos2, rspec(tp, 128)), (sin_signed, rspec(tp, 128))] + [(a, cspec((1, 128))) for a in (qnn, knn, qrn, krn)]
    hq_spec = pl.BlockSpec((MH, tp, 256), lambda i: (0, i, 0))
    hv_spec = pl.BlockSpec((MH, tp, 128), lambda i: (0, i, 0))
    qf, kf, vh = stage("l1_mla_prep", fn_mla_prep, (L // tp,), prep_ins,
                       [(SDS((MH, L, 256), BF16), hq_spec), (SDS((MH, L, 256), BF16), hq_spec), (SDS((MH, L, 128), BF16), hv_spec)])
    attn = causal_attn(qf, kf, vh)
    mn1, kv1, kn1 = mem_branch_fwd("l1", mem, mem_norm[1:2], w["w_mem_kv"][1], xk_norm[1:2])
    mo1 = mem_attn_fwd("l1", proj1, 5, kn1, kv1, xq_norm[1:2])
    o1 = row_fwd("l1_merge", fn_merge, L, 256, [(attn, PW, 0), (mo1, XQW, 0), (proj1, BW, 0)], [], [(BW, BF16)])[0]
    x2 = matmul("l1_out", o1, w["w_out"][1], "nn", add=x1)
    dx2, loss = loss_and_grad(x2, target)

    do1 = matmul("l1_out_dx", dx2, w["w_out"][1], "nt")
    g_wide = matmul("l1_out_dw", o1, dx2, "tn", out=gw["w_out"][1])
    dattn, dmo1, dproj1 = row_bwd("l1_merge_bwd", fn_merge, L, 256, [(attn, PW, 0), (mo1, XQW, 0), (proj1, BW, 0)], [],
                                  [(do1, BW, 0)], [True, True, {"cols": MLA_IN_P, "cb": 0}], [])
    dproj1, dkn1, dv1, dxqn1 = mem_attn_bwd("l1", proj1, 5, kn1, kv1, xq_norm[1:2], dmo1, dproj1)
    g_wide, dmem_norm1, dxk1 = mem_branch_bwd("l1", mem, mem_norm[1:2], w["w_mem_kv"][1], xk_norm[1:2], mn1, kv1, dkn1, dv1,
                                              gw["w_mem_kv"][1], g_wide)
    dqf, dkf, dvh = causal_attn_bwd(qf, kf, vh, dattn)
    prep_diffs = [("row", SDS((L, MH * (NOPE + ROPE)), F32), rspec(tp, MH * (NOPE + ROPE))), ("row", SDS((L, MH * 256), F32), rspec(tp, MH * 256)),
                  ("row", SDS((L, MLA_IN_P), F32), rspec(tp, 128, 26), {"into": dproj1}), None, None] + [("acc", (0,))] * 4
    dq, dkv, dproj1, dqnn, dknn, dqrn, dkrn = stage_bwd("l1_mla_prep_bwd", fn_mla_prep, (L // tp,), prep_ins,
                                                        [(dqf, hq_spec), (dkf, hq_spec), (dvh, hv_spec)], prep_diffs)
    dcqn = matmul("l1_uq_dx", dq, w_uq, "nt")
    dw_uq = matmul("l1_uq_dw", cqn, dq, "tn")
    dckvn = matmul("l1_ukv_dx", dkv, w["mla_w_ukv"], "nt")
    g_mid = matmul("l1_ukv_dw", ckvn, dkv, "tn", out=gw["mla_w_ukv"])
    dproj1, dqln = row_bwd("l1_q_lora_rms_bwd", fn_rms, L, 256, [(proj1, QL, 4)], [qln], [(dcqn, QL, 0)],
                           [{"cols": MLA_IN_P, "cb": 4, "into": dproj1}], [True])
    dproj1, dkvln = row_bwd("l1_kv_lora_rms_bwd", fn_rms, L, 256, [(proj1, KVL, 12)], [kvln], [(dckvn, KVL, 0)],
                            [{"cols": MLA_IN_P, "cb": 12, "into": dproj1}], [True])
    dh1 = matmul("l1_in_dx", dproj1, w_in1, "nt")
    dw_in1 = matmul("l1_in_dw", h1, dproj1, "tn")
    dx1, dln1 = row_bwd("l1_rms_bwd", fn_rms, L, 256, [(x1, D, 0)], [ln[1:2]], [(dh1, D, 0)], [{"add": (dx2, D, 0)}], [True])
    dx1p = time_permute(dx1)

    do0 = matmul("l0_out_dx", dx1p, w["w_out"][0], "nt")
    g_wide = matmul("l0_out_dw", o0, dx1p, "tn", out=gw["w_out"][0], into=g_wide)
    dz0, dmo0, dproj0 = row_bwd("l0_merge_bwd", fn_merge_glu, L, 256, [(z0, 2 * PW, 0), (mo0, XQW, 0), (proj0, BW, 1)], [],
                                [(do0, BW, 0)], [True, True, {"cols": 2 * BW, "cb": 1}], [])
    dproj0, dkn0, dv0, dxqn0 = mem_attn_bwd("l0", proj0, 3, kn0, kv0, xq_norm[0:1], dmo0, dproj0)
    g_wide, dmem_norm0, dxk0 = mem_branch_bwd("l0", mem, mem_norm[0:1], w["w_mem_kv"][0], xk_norm[0:1], mn0, kv0, dkn0, dv0,
                                              gw["w_mem_kv"][0], g_wide)
    dg0 = matmul("l0_glu_dx", dz0, w["s5_w_glu"], "nt")
    g_mid = matmul("l0_glu_dw", g0, dz0, "tn", out=gw["s5_w_glu"], into=g_mid)
    dproj0, dd, dwc_re, dwc_im, dwb_re, dwb_im, da_re, da_im = s5_backward(dg0, proj0, s_re, s_im, wb_re, wb_im, wc_re, wc_im,
                                                                           a_re_v, a_im_v, s5_d, dproj0)
    dh0 = matmul("l0_in_dx", dproj0, w["s5_w_in"], "nt")
    g_wide = matmul("l0_in_dw", h0, dproj0, "tn", out=gw["s5_w_in"], into=g_wide)
    dxp, dln0 = row_bwd("l0_rms_bwd", fn_rms, L, 256, [(xp, D, 0)], [ln[0:1]], [(dh0, D, 0)], [{"add": (dx1p, D, 0)}], [True])
    grad_x = time_unpermute(dxp)

    db_re, db_im, dcoef_re, dcoef_im = row_bwd("s5_bmat_bwd", fn_s5_bmat, SN, 512, bmat_rows, [], [(dwb_re, 128, 0), (dwb_im, 128, 0)],
                                               [True] * 4, [])
    dc_re, dc_im = row_bwd("s5_cmat_bwd", fn_s5_cmat, PW, 128, cmat_rows, [], [(dwc_re, 512, 0), (dwc_im, 512, 0)], [True] * 2, [])
    disc_cts = [(da_re.reshape(SG, SP), one), (da_im.reshape(SG, SP), one), (dcoef_re.reshape(SG, SP), one), (dcoef_im.reshape(SG, SP), one)]
    dlre, dlim, dls = stage_bwd("s5_disc_bwd", fn_s5_disc, (1,), disc_ins, disc_cts, [("acc", (0,))] * 3)

    g["ln_gain"] = jnp.concatenate([dln0, dln1], axis=0)
    g["mem_norm"] = jnp.concatenate([dmem_norm0, dmem_norm1], axis=0)
    g["xq_norm"] = jnp.concatenate([dxqn0, dxqn1], axis=0)
    g["xk_norm"] = jnp.concatenate([dxk0, dxk1], axis=0)
    g["s5_lambda_re"], g["s5_lambda_im"], g["s5_log_step"] = dlre, dlim, dls
    g["s5_b_re"], g["s5_b_im"], g["s5_c_re"], g["s5_c_im"] = db_re, db_im, dc_re, dc_im
    g["s5_d"] = dd
    g["mla_q_lora_norm"], g["mla_kv_lora_norm"] = dqln, dkvln
    g["mla_q_nope_norm"], g["mla_k_nope_norm"] = dqnn, dknn
    g["mla_q_rope_norm"] = dqrn[:, :ROPE] + dqrn[:, ROPE:]
    g["mla_k_rope_norm"] = dkrn[:, :ROPE] + dkrn[:, ROPE:]
    return loss, grad_x, g_wide, g_mid, dw_in1, dw_uq, g


def kernel(x, mem, positions, ln_gain, w_out, mem_norm, w_mem_kv, xq_norm, xk_norm, s5_w_in, s5_lambda_re, s5_lambda_im, s5_log_step, s5_b_re, s5_b_im, s5_c_re, s5_c_im, s5_d, s5_w_glu, mla_w_in, mla_q_lora_norm, mla_kv_lora_norm, mla_w_uq, mla_w_ukv, mla_q_nope_norm, mla_k_nope_norm, mla_q_rope_norm, mla_k_rope_norm, loss_target, m_ln_gain, m_w_out, m_mem_norm, m_w_mem_kv, m_xq_norm, m_xk_norm, m_s5_w_in, m_s5_lambda_re, m_s5_lambda_im, m_s5_log_step, m_s5_b_re, m_s5_b_im, m_s5_c_re, m_s5_c_im, m_s5_d, m_s5_w_glu, m_mla_w_in, m_mla_q_lora_norm, m_mla_kv_lora_norm, m_mla_w_uq, m_mla_w_ukv, m_mla_q_nope_norm, m_mla_k_nope_norm, m_mla_q_rope_norm, m_mla_k_rope_norm, v_ln_gain, v_w_out, v_mem_norm, v_w_mem_kv, v_xq_norm, v_xk_norm, v_s5_w_in, v_s5_lambda_re, v_s5_lambda_im, v_s5_log_step, v_s5_b_re, v_s5_b_im, v_s5_c_re, v_s5_c_im, v_s5_d, v_s5_w_glu, v_mla_w_in, v_mla_q_lora_norm, v_mla_kv_lora_norm, v_mla_w_uq, v_mla_w_ukv, v_mla_q_nope_norm, v_mla_k_nope_norm, v_mla_q_rope_norm, v_mla_k_rope_norm):
    args = dict(locals())
    wts = {n: args[n] for n in WEIGHT_ORDER}
    mom = {n: args["m_" + n] for n in WEIGHT_ORDER}
    var = {n: args["v_" + n] for n in WEIGHT_ORDER}

    chip = 2 * lax.axis_index("x") + lax.axis_index("y")
    place = jnp.stack([chip, lax.axis_index("c")]).astype(jnp.int32)

    def gather(name, shards):
        return [lax.dynamic_update_slice(g, s[None], (chip, 0, 0)) for g, s in zip(all_gather_chips(name, shards), shards)]

    norm_rows = jnp.concatenate([mla_q_lora_norm, jnp.pad(mla_kv_lora_norm, ((0, 0), (0, 64))), jnp.zeros((14, 128), F32)], axis=0)
    wide, mid, in1, uq, norms = gather("gather_weights", [*stack_shards(wts, BF16), mla_w_in[0].astype(BF16), mla_w_uq[0].astype(BF16), norm_rows])
    small = {n: wts[n] for n, _ in SMALL}
    small["mla_q_lora_norm"], small["mla_kv_lora_norm"] = norms[:, 0, :].reshape(1, QL), norms[:, 1, :64].reshape(1, KVL)

    loss, grad_x, g_wide, g_mid, dw_in1, dw_uq, g = device_step(
        x[0], mem[0], positions[0], loss_target[0], wide, mid, mla_in_permute(shards_to_cols(in1)), uq_permute(shards_to_cols(uq)), small)
    loss = lax.psum(loss[0, 0], MESH_AXES)

    g_in1 = cols_to_shards(mla_in_unpermute(dw_in1)).astype(BF16)
    g_uq = cols_to_shards(uq_unpermute(dw_uq)).astype(BF16)
    small_flat = jnp.concatenate([g[n].reshape(-1) for n, _ in SMALL_FULL])
    g_small = jnp.pad(small_flat, (0, 4 * SMALL_ROWS * SMALL_LANES - N_SMALL)).astype(BF16).reshape(4, SMALL_ROWS, SMALL_LANES)
    r_wide, r_mid, r_in1, r_uq, r_small = reduce_scatter_chips([g_wide, g_mid, g_in1, g_uq, g_small], place)
    small_all = gather("gather_small_grads", [r_small])[0].reshape(-1)[:N_SMALL]

    grads = {"w_out": r_wide[:1024].reshape(2, 512, 1024), "w_mem_kv": r_wide[1024:IN0_ROFF].reshape(2, 256, 1024),
             "s5_w_in": r_wide[IN0_ROFF:][None], "s5_w_glu": r_mid[:UKV_ROFF][None], "mla_w_ukv": r_mid[UKV_ROFF:][None],
             "mla_w_in": r_in1[None], "mla_w_uq": r_uq[None]}
    off = 0
    for n, s in SMALL_FULL:
        grads[n] = small_all[off:off + math.prod(s)].reshape(s)
        off += math.prod(s)
    for n, s in SHARDED_SMALL:
        grads[n] = lax.dynamic_slice(grads[n], (0, chip * s[1]), s)

    delta, new_m, new_v = {}, {}, {}
    own_layout = [(n, (s[0] * s[1], s[2])) for n, s in BIG] + [(n, (s[1] * s[2], s[3])) for n, s in SMALL if len(s) == 4]
    for n, two_d in own_layout:
        res = adamw("adamw_" + n, wts[n].reshape(two_d), grads[n].reshape(two_d), mom[n].reshape(two_d), var[n].reshape(two_d))
        delta[n], new_m[n], new_v[n] = (r.reshape(wts[n].shape) for r in res)
    small_names = [n for n, s in SMALL if len(s) < 4] + [n for n, _ in SHARDED_SMALL]
    n_own = sum(wts[n].size for n in small_names)
    rows_own = -(-n_own // (8 * 128)) * 8

    def pack_small(d):
        flat = jnp.concatenate([d[n].reshape(-1) for n in small_names])
        return jnp.pad(flat, (0, rows_own * 128 - n_own), constant_values=1.0).reshape(rows_own, 128)

    res = adamw("adamw_small", pack_small(wts), pack_small(grads), pack_small(mom), pack_small(var))
    off = 0
    for n in small_names:
        size = wts[n].size
        delta[n], new_m[n], new_v[n] = (r.reshape(-1)[off:off + size].reshape(wts[n].shape) for r in res)
        off += size

    return (loss, grad_x[None], *[grads[n] for n in WEIGHT_ORDER], *[delta[n] for n in WEIGHT_ORDER],
            *[new_m[n] for n in WEIGHT_ORDER], *[new_v[n] for n in WEIGHT_ORDER])
```

```python
import functools
import math

import jax
import jax.numpy as jnp
from jax import lax
from jax.experimental import pallas as pl
from jax.experimental.pallas import tpu as pltpu

F32, BF16 = jnp.float32, jnp.bfloat16
SDS = jax.ShapeDtypeStruct

D = 1024
L = 2048
ML = 256
BW = 2 * D
XQW = BW // 4
PW = BW - XQW
XH, XHD = 4, 128
SG, SC, SP = 96, 16, 64
SN = SG * SP
NOPE, ROPE, VD = 128, 64, 128
MH = 12
QL, KVL = 512, 256
EPS = 1e-6
ROPE_THETA = 10000.0
MLA_IN = QL + KVL + ROPE + XQW + BW
MLA_IN_P = 3456
ADAM_LR, ADAM_B1, ADAM_B2, ADAM_EPS, ADAM_WD, ADAM_STEP = 0.001, 0.9, 0.999, 1e-08, 0.01, 10

VMEM_LIMIT = 48 * 2**20
SEG = 8
SEG_LEN = L // SEG
MESH_AXES = ("x", "y", "c")


def _cparams():
    return pltpu.CompilerParams(vmem_limit_bytes=VMEM_LIMIT)


def _dg(a, b, ca, cb):
    return lax.dot_general(a.astype(BF16), b.astype(BF16), (((ca,), (cb,)), ((), ())), preferred_element_type=F32)


@jax.custom_vjp
def mm_nn(a, b):
    return _dg(a, b, 1, 0)


mm_nn.defvjp(lambda a, b: (_dg(a, b, 1, 0), (a, b)), lambda res, g: (_dg(g, res[1], 1, 1), _dg(res[0], g, 0, 0)))


@jax.custom_vjp
def mm_nt(a, b):
    return _dg(a, b, 1, 1)


mm_nt.defvjp(lambda a, b: (_dg(a, b, 1, 1), (a, b)), lambda res, g: (_dg(g, res[1], 1, 0), _dg(g, res[0], 0, 0)))


@functools.partial(jax.custom_vjp, nondiff_argnums=(1,))
def lane_roll(x, shift):
    return pltpu.roll(x, shift, 1)


lane_roll.defvjp(lambda x, shift: (pltpu.roll(x, shift, 1), None),
                 lambda shift, _, g: (pltpu.roll(g, (128 - shift) % 128, 1),))


def rms(x, g):
    return x * lax.rsqrt(jnp.mean(x * x, axis=-1, keepdims=True) + EPS) * g


def softmax_rows(s):
    m = lax.stop_gradient(jnp.max(s, axis=-1, keepdims=True))
    e = jnp.exp(s - m)
    return e / jnp.sum(e, axis=-1, keepdims=True)


def silu(x):
    return x * jax.nn.sigmoid(x)


ANY = pl.BlockSpec(memory_space=pl.ANY)
MESH_ID = pl.DeviceIdType.MESH


def _dma_sems(n):
    return [pltpu.SemaphoreType.DMA((n,)), pltpu.SemaphoreType.DMA((n,))]


class Plan:
    def __init__(self, operands, out_shape, aliases, n_sems, copies):
        self.operands, self.out_shape, self.aliases, self.n_sems, self.copies = list(operands), list(out_shape), aliases, n_sems, copies
        self.results = None


def hosted_call(kern, *, name, grid, in_specs, out_specs, out_shape, operands, scratch_shapes=(), aliases=None, cparams=None, plans=()):
    n_in, n_out, n_scr = len(in_specs), len(out_specs), len(scratch_shapes)
    p_in, p_out = [len(p.operands) for p in plans], [len(p.out_shape) for p in plans]
    all_aliases = dict(aliases or {})
    in_off, out_off = n_in, n_out
    for p, ni, no in zip(plans, p_in, p_out):
        all_aliases.update({in_off + i: out_off + o for i, o in p.aliases.items()})
        in_off, out_off = in_off + ni, out_off + no

    def body(*refs):
        pos, pins, pouts = n_in, [], []
        for ni in p_in:
            pins.append(refs[pos:pos + ni])
            pos += ni
        main_out = refs[pos:pos + n_out]
        pos += n_out
        for no in p_out:
            pouts.append(refs[pos:pos + no])
            pos += no
        main_scr = refs[pos:pos + n_scr]
        pos += n_scr
        if plans:
            ids = [pl.program_id(ax) for ax in range(len(grid))]
            first = functools.reduce(jnp.logical_and, [i == 0 for i in ids])
            last = functools.reduce(jnp.logical_and, [i == g - 1 for i, g in zip(ids, grid)])
            copies = [p.copies(pins[k], pouts[k], refs[pos + 2 * k], refs[pos + 2 * k + 1]) for k, p in enumerate(plans)]

            @pl.when(first)
            def _():
                for sends, _ in copies:
                    for cp in sends:
                        cp.start()

        kern(*refs[:n_in], *main_out, *main_scr)
        if plans:
            @pl.when(last)
            def _():
                for sends, recvs in copies:
                    for cp in recvs:
                        cp.wait_recv()
                    for cp in sends:
                        cp.wait_send()

    res = pl.pallas_call(body, grid=grid, in_specs=list(in_specs) + [ANY] * sum(p_in), out_specs=list(out_specs) + [ANY] * sum(p_out),
                         out_shape=list(out_shape) + [s for p in plans for s in p.out_shape],
                         scratch_shapes=list(scratch_shapes) + [s for p in plans for s in _dma_sems(p.n_sems)],
                         input_output_aliases=all_aliases, name=name, compiler_params=cparams or _cparams())(
        *operands, *[a for p in plans for a in p.operands])
    pos = n_out
    for p, no in zip(plans, p_out):
        p.results = list(res[pos:pos + no])
        pos += no
    return list(res[:n_out])


def stage(name, fn, grid, ins, outs):
    n_in = len(ins)

    def kern(*refs):
        res = fn(*[r[...] for r in refs[:n_in]])
        for r, v in zip(refs[n_in:], res):
            r[...] = v.astype(r.dtype)

    return pl.pallas_call(kern, grid=grid, in_specs=[s for _, s in ins], out_specs=[s for _, s in outs],
                          out_shape=[sd for sd, _ in outs], name=name, compiler_params=_cparams())(*[a for a, _ in ins])


def stage_bwd(name, fn, grid, ins, cts, diffs):
    n_in, n_ct = len(ins), len(cts)
    didx = [i for i, d in enumerate(diffs) if d is not None]
    opts = {i: (diffs[i][3] if len(diffs[i]) > 3 else {}) for i in didx if diffs[i][0] == "row"}
    adds = [(i, opts[i]["add"]) for i in opts if "add" in opts[i]]
    intos = [(i, opts[i]["into"]) for i in opts if "into" in opts[i]]
    n_add, n_into = len(adds), len(intos)
    add_pos = {i: n_in + n_ct + k for k, (i, _) in enumerate(adds)}
    n_extra = n_in + n_ct + n_add + n_into

    def kern(*refs):
        vals = [r[...] for r in refs[:n_in]]

        def f(*dv):
            full = list(vals)
            for i, v in zip(didx, dv):
                full[i] = v
            return fn(*full)

        _, vjp = jax.vjp(f, *[vals[i].astype(F32) for i in didx])
        gs = vjp(tuple(c[...].astype(F32) for c in refs[n_in:n_in + n_ct]))
        for o_ref, i, g in zip(refs[n_extra:], didx, gs):
            if diffs[i][0] == "row":
                if i in add_pos:
                    g = g + refs[add_pos[i]][...].astype(F32)
                o_ref[...] = g.astype(o_ref.dtype)
            else:
                first = functools.reduce(jnp.logical_and, [pl.program_id(ax) == 0 for ax in diffs[i][1]])

                @pl.when(first)
                def _():
                    o_ref[...] = g

                @pl.when(jnp.logical_not(first))
                def _():
                    o_ref[...] += g

    out_shape, out_specs = [], []
    for i in didx:
        if diffs[i][0] == "row":
            out_shape.append(diffs[i][1])
            out_specs.append(diffs[i][2])
        else:
            out_shape.append(SDS(ins[i][0].shape, F32))
            out_specs.append(ins[i][1])
    aliases = {n_in + n_ct + n_add + k: didx.index(i) for k, (i, _) in enumerate(intos)}
    in_specs = [s for _, s in ins] + [s for _, s in cts] + [s for _, (_, s) in adds] + [ANY] * n_into
    operands = [a for a, _ in ins] + [a for a, _ in cts] + [a for _, (a, _) in adds] + [a for _, a in intos]
    return pl.pallas_call(kern, grid=grid, in_specs=in_specs, out_specs=out_specs, out_shape=out_shape, input_output_aliases=aliases,
                          name=name, compiler_params=_cparams())(*operands)


def rspec(tl, w, cb=0):
    return pl.BlockSpec((tl, w), lambda i: (i, cb))


def cspec(shape):
    return pl.BlockSpec(shape, lambda i: (0,) * len(shape))


def row_fwd(name, fn, rows, tl, row_ins, consts, outs):
    ins = [(a, rspec(tl, w, cb)) for a, w, cb in row_ins] + [(a, cspec(a.shape)) for a in consts]
    return stage(name, fn, (rows // tl,), ins, [(SDS((rows, w), dt), rspec(tl, w)) for w, dt in outs])


def row_bwd(name, fn, rows, tl, row_ins, consts, cts, row_diff, const_diff):
    ins = [(a, rspec(tl, w, cb)) for a, w, cb in row_ins] + [(a, cspec(a.shape)) for a in consts]
    diffs = []
    for (a, w, cb), d in zip(row_ins, row_diff):
        if not d:
            diffs.append(None)
            continue
        d = d if isinstance(d, dict) else {}
        opts = {}
        if "add" in d:
            opts["add"] = (d["add"][0], rspec(tl, d["add"][1], d["add"][2]))
        if d.get("into") is not None:
            opts["into"] = d["into"]
        diffs.append(("row", SDS((rows, d.get("cols", w)), F32), rspec(tl, w, d.get("cb", 0)), opts))
    diffs += [("acc", (0,)) if d else None for d in const_diff]
    return stage_bwd(name, fn, (rows // tl,), ins, [(a, rspec(tl, w, cb)) for a, w, cb in cts], diffs)


MATMUL_VMEM = 28 * 2**20
ADAMW_VMEM = 16 * 2**20


class Sharded:
    def __init__(self, arr, kind, roff, rows):
        self.arr, self.kind, self.roff, self.rows, self.n = arr, kind, roff, rows, arr.shape[2]
        self.shape = (rows, 4 * self.n) if kind == "col" else (4 * rows, self.n)

    def fits(self, t0, t1):
        return self.roff % t0 == 0 and self.rows % t0 == 0 and self.n % t1 == 0

    def spec(self, t0, t1, bidx):
        assert self.fits(t0, t1), (self.kind, self.roff, self.rows, self.n, t0, t1)
        r0 = self.roff // t0
        if self.kind == "col":
            per = self.n // t1
            return pl.BlockSpec((None, t0, t1), lambda *g: (bidx(*g)[1] // per, r0 + bidx(*g)[0], bidx(*g)[1] % per))
        per = self.rows // t0
        return pl.BlockSpec((None, t0, t1), lambda *g: (bidx(*g)[0] // per, r0 + bidx(*g)[0] % per, bidx(*g)[1]))


def matmul(name, a, b, mode, out_dtype=F32, add=None, out=None, into=None, plans=()):
    if mode == "tn":
        k_dim, m = a.shape
    else:
        m, k_dim = a.shape
    n = b.shape[0] if mode == "nt" else b.shape[1]
    b_fit = b.fits if isinstance(b, Sharded) else (lambda t0, t1: True)
    o_fit = out.fits if out is not None else (lambda t0, t1: True)
    a_bytes, b_bytes = jnp.dtype(a.dtype).itemsize, jnp.dtype(b.arr.dtype if isinstance(b, Sharded) else b.dtype).itemsize
    o_bytes = jnp.dtype(out_dtype if out is None else out.arr.dtype).itemsize

    def vmem(tm, tn, tk):
        return 2 * (tm * tk * a_bytes + tk * tn * b_bytes + tm * tn * (o_bytes + (4 if add is not None else 0))) + 4 * tm * tn

    tiles = [(tm, tn, tk) for tm in (2048, 1024, 512, 256, 128) for tn in (1024, 768, 512, 384, 256, 128) for tk in (1024, 512, 384, 256, 128)
             if m % tm == 0 and n % tn == 0 and k_dim % tk == 0 and (b_fit(tn, tk) if mode == "nt" else b_fit(tk, tn)) and o_fit(tm, tn)
             and vmem(tm, tn, tk) <= MATMUL_VMEM]
    tm, tn, tk = max(tiles, key=lambda t: (t[0] * t[1] * t[2], t[0] * t[1]))
    nk = k_dim // tk
    a_spec = pl.BlockSpec((tk, tm), lambda i, j, k: (k, i)) if mode == "tn" else pl.BlockSpec((tm, tk), lambda i, j, k: (i, k))
    if isinstance(b, Sharded):
        b_spec = b.spec(tn, tk, lambda i, j, k: (j, k)) if mode == "nt" else b.spec(tk, tn, lambda i, j, k: (k, j))
        b = b.arr
    else:
        b_spec = pl.BlockSpec((tn, tk), lambda i, j, k: (j, k)) if mode == "nt" else pl.BlockSpec((tk, tn), lambda i, j, k: (k, j))
    o_spec = pl.BlockSpec((tm, tn), lambda i, j, k: (i, j))
    out_spec, out_shape = (o_spec, SDS((m, n), out_dtype)) if out is None else (out.spec(tm, tn, lambda i, j, k: (i, j)), out.arr)
    ca, cb = {"nn": (1, 0), "nt": (1, 1), "tn": (0, 0)}[mode]
    n_in = 2 + (add is not None)

    def kern(*refs):
        a_ref, b_ref = refs[0], refs[1]
        o_ref, acc = refs[-2], refs[-1]
        k = pl.program_id(2)

        @pl.when(k == 0)
        def _():
            acc[...] = jnp.zeros_like(acc)

        acc[...] += _dg(a_ref[...], b_ref[...], ca, cb)

        @pl.when(k == nk - 1)
        def _():
            r = acc[...]
            if add is not None:
                r = r + refs[2][...]
            o_ref[...] = r.astype(o_ref.dtype)

    ins, specs = [a, b], [a_spec, b_spec]
    if add is not None:
        ins.append(add)
        specs.append(o_spec)
    if into is not None:
        ins.append(into)
        specs.append(ANY)
    return hosted_call(kern, name=name, grid=(m // tm, n // tn, nk), in_specs=specs, out_specs=[out_spec], out_shape=[out_shape],
                       operands=ins, scratch_shapes=[pltpu.VMEM((tm, tn), F32)], aliases={} if into is None else {n_in: 0}, plans=plans)[0]


def _cmul(ar, ai, br, bi):
    return ar * br - ai * bi, ar * bi + ai * br


def _sub_shift(x, down):
    row = lax.broadcasted_iota(jnp.int32, x.shape, 0)
    if down:
        return jnp.where(row == 0, 0.0, pltpu.roll(x, 1, 0))
    return jnp.where(row == SEG - 1, 0.0, pltpu.roll(x, SEG - 1, 0))


def _pow_seg_len(ar, ai):
    for _ in range(int(math.log2(SEG_LEN))):
        ar, ai = _cmul(ar, ai, ar, ai)
    return ar, ai


def _scan_in_place(sr, si, a_re, a_im):
    lanes = sr.shape[1]
    ar = jnp.broadcast_to(a_re, (SEG, lanes))
    ai = jnp.broadcast_to(a_im, (SEG, lanes))
    zero = jnp.zeros((SEG, lanes), F32)

    def local(i, carry):
        rows = pl.ds(pl.multiple_of(i * SEG, SEG), SEG)
        mr, mi = _cmul(ar, ai, carry[0], carry[1])
        nr, ni = mr + sr[rows, :], mi + si[rows, :]
        sr[rows, :] = nr
        si[rows, :] = ni
        return nr, ni

    fr, fi = lax.fori_loop(0, SEG_LEN, local, (zero, zero))
    pr, pi = _pow_seg_len(ar, ai)
    ir, ii = zero, zero
    for _ in range(SEG - 1):
        mr, mi = _cmul(pr, pi, ir, ii)
        ir, ii = _sub_shift(mr + fr, True), _sub_shift(mi + fi, True)

    def carry_in(i, pw):
        rows = pl.ds(pl.multiple_of(i * SEG, SEG), SEG)
        cr, ci = _cmul(pw[0], pw[1], ir, ii)
        sr[rows, :] += cr
        si[rows, :] += ci
        return _cmul(pw[0], pw[1], ar, ai)

    lax.fori_loop(0, SEG_LEN, carry_in, (ar, ai))


S5_LANES = 8 * SP
S5_BLOCKS = SN // S5_LANES


def _s5_specs():
    u_spec = pl.BlockSpec((L, 8 * SC), lambda j: (0, j))
    s_spec = pl.BlockSpec((L, S5_LANES), lambda j: (0, j))
    wb_spec = pl.BlockSpec((S5_LANES, 8 * SC), lambda j: (j, 0))
    wc_spec = pl.BlockSpec((8 * SC, S5_LANES), lambda j: (j, 0))
    a_spec = pl.BlockSpec((1, S5_LANES), lambda j: (0, j))
    d_spec = pl.BlockSpec((1, 8 * SC), lambda j: (0, j))
    return u_spec, s_spec, wb_spec, wc_spec, a_spec, d_spec


def s5_forward(proj, wb_re, wb_im, wc_re, wc_im, a_re, a_im, d, plans=()):
    def kern(u_ref, wbr, wbi, wcr, wci, ar, ai, d_ref, sr, si, g_ref):
        u = u_ref[...]
        sr[...], si[...] = fn_s5_bu(u, wbr[...], wbi[...])
        _scan_in_place(sr, si, ar[...], ai[...])
        g_ref[...] = fn_s5_out(sr[...], si[...], u, d_ref[...], wcr[...], wci[...])[0].astype(g_ref.dtype)

    u_spec, s_spec, wb_spec, wc_spec, a_spec, d_spec = _s5_specs()
    return hosted_call(kern, name="s5_forward", grid=(S5_BLOCKS,), in_specs=[u_spec, wb_spec, wb_spec, wc_spec, wc_spec, a_spec, a_spec, d_spec],
                       out_specs=[s_spec, s_spec, u_spec], out_shape=[SDS((L, SN), F32)] * 2 + [SDS((L, PW), BF16)],
                       operands=[proj, wb_re, wb_im, wc_re, wc_im, a_re, a_im, d], plans=plans)


def _adjoint_scan_in_place(lr, li, sr, si, a_re, a_im):
    lanes = lr.shape[1]
    ar = jnp.broadcast_to(a_re, (SEG, lanes))
    ai = -jnp.broadcast_to(a_im, (SEG, lanes))
    zero = jnp.zeros((SEG, lanes), F32)

    def local(k, carry):
        i = SEG_LEN - 1 - k
        rows = pl.ds(pl.multiple_of(i * SEG, SEG), SEG)
        mr, mi = _cmul(ar, ai, carry[0], carry[1])
        nr, ni = mr + lr[rows, :], mi + li[rows, :]
        lr[rows, :] = nr
        li[rows, :] = ni
        return nr, ni

    fr, fi = lax.fori_loop(0, SEG_LEN, local, (zero, zero))
    pr, pi = _pow_seg_len(ar, ai)
    ir, ii = zero, zero
    for _ in range(SEG - 1):
        mr, mi = _cmul(pr, pi, ir, ii)
        ir, ii = _sub_shift(mr + fr, False), _sub_shift(mi + fi, False)

    def fix(rows, pw):
        cr, ci = _cmul(pw[0], pw[1], ir, ii)
        tr, ti = lr[rows, :] + cr, li[rows, :] + ci
        lr[rows, :] = tr
        li[rows, :] = ti
        return tr, ti

    def grad_a(tr, ti, spr, spi, acc):
        return acc[0] + tr * spr + ti * spi, acc[1] + ti * spr - tr * spi

    def carry_in(k, c):
        i = SEG_LEN - 1 - k
        rows = pl.ds(pl.multiple_of(i * SEG, SEG), SEG)
        prev = pl.ds(pl.multiple_of((i - 1) * SEG, SEG), SEG)
        tr, ti = fix(rows, (c[0], c[1]))
        acc = grad_a(tr, ti, sr[prev, :], si[prev, :], (c[2], c[3]))
        nr, ni = _cmul(c[0], c[1], ar, ai)
        return nr, ni, acc[0], acc[1]

    pwr, pwi, accr, acci = lax.fori_loop(0, SEG_LEN - 1, carry_in, (ar, ai, zero, zero))
    tr, ti = fix(pl.ds(0, SEG), (pwr, pwi))
    last = pl.ds((SEG_LEN - 1) * SEG, SEG)
    accr, acci = grad_a(tr, ti, _sub_shift(sr[last, :], True), _sub_shift(si[last, :], True), (accr, acci))
    return jnp.sum(accr, axis=0, keepdims=True), jnp.sum(acci, axis=0, keepdims=True)


S5_BWD_VMEM = 58 * 2**20


def s5_backward(dg, proj, s_re, s_im, wb_re, wb_im, wc_re, wc_im, a_re, a_im, d, dproj, plans=()):
    def kern(dg_ref, u_ref, sr, si, wbr, wbi, wcr, wci, ar, ai, d_ref, _, du_ref, dd_ref, dwcr, dwci, dwbr, dwbi, dar, dai, lr, li):
        u = u_ref[...]
        _, vjp_out = jax.vjp(fn_s5_out, sr[...], si[...], u, d_ref[...], wcr[...], wci[...])
        lr[...], li[...], du_out, dd_ref[...], dwcr[...], dwci[...] = vjp_out((dg_ref[...],))
        dar[...], dai[...] = _adjoint_scan_in_place(lr, li, sr, si, ar[...], ai[...])
        _, vjp_in = jax.vjp(fn_s5_bu, u, wbr[...], wbi[...])
        du_in, dwbr[...], dwbi[...] = vjp_in((lr[...], li[...]))
        du_ref[...] = du_out + du_in

    u_spec, s_spec, wb_spec, wc_spec, a_spec, d_spec = _s5_specs()
    outs = [(SDS(dproj.shape, F32), u_spec), (SDS(d.shape, F32), d_spec), (SDS(wc_re.shape, F32), wc_spec), (SDS(wc_im.shape, F32), wc_spec),
            (SDS(wb_re.shape, F32), wb_spec), (SDS(wb_im.shape, F32), wb_spec), (SDS(a_re.shape, F32), a_spec), (SDS(a_im.shape, F32), a_spec)]
    return hosted_call(kern, name="s5_backward", grid=(S5_BLOCKS,),
                       in_specs=[u_spec, u_spec, s_spec, s_spec, wb_spec, wb_spec, wc_spec, wc_spec, a_spec, a_spec, d_spec, ANY],
                       out_specs=[sp for _, sp in outs], out_shape=[sd for sd, _ in outs], aliases={11: 0},
                       operands=[dg, proj, s_re, s_im, wb_re, wb_im, wc_re, wc_im, a_re, a_im, d, dproj],
                       scratch_shapes=[pltpu.VMEM((L, S5_LANES), F32)] * 2,
                       cparams=pltpu.CompilerParams(vmem_limit_bytes=S5_BWD_VMEM), plans=plans)


def fn_rms(x, g):
    return (rms(x, g),)


def fn_s5_disc(lre, lim, ls):
    step = jnp.exp(ls)
    e = jnp.exp(lre * step)
    a_re, a_im = e * jnp.cos(lim * step), e * jnp.sin(lim * step)
    den = lre * lre + lim * lim
    nr, ni = a_re - 1.0, a_im
    return a_re, a_im, (nr * lre + ni * lim) / den, (ni * lre - nr * lim) / den


def _group_mask(rows, cols, row_div, col_div):
    r = lax.broadcasted_iota(jnp.int32, (rows, cols), 0) // row_div % 8
    c = lax.broadcasted_iota(jnp.int32, (rows, cols), 1) // col_div
    return r == c


def fn_s5_bmat(b_re, b_im, coef_re, coef_im):
    rows = b_re.shape[0]
    mask = _group_mask(rows, 8 * SC, SP, SC)
    bb_re = coef_re * b_re - coef_im * b_im
    bb_im = coef_re * b_im + coef_im * b_re
    return jnp.where(mask, jnp.tile(bb_re, (1, 8)), 0.0), jnp.where(mask, jnp.tile(bb_im, (1, 8)), 0.0)


def fn_s5_cmat(c_re, c_im):
    rows = c_re.shape[0]
    mask = _group_mask(rows, 8 * SP, SC, SP)
    return jnp.where(mask, jnp.tile(c_re, (1, 8)), 0.0), jnp.where(mask, jnp.tile(c_im, (1, 8)), 0.0)


def fn_s5_bu(u, wb_re, wb_im):
    return mm_nt(u, wb_re), mm_nt(u, wb_im)


def fn_s5_out(sr, si, u, d, wc_re, wc_im):
    y = mm_nt(sr, wc_re) - mm_nt(si, wc_im) + d * u
    return (jax.nn.gelu(y),)


def fn_merge_glu(z, mo, gate):
    yg = z[:, :PW] * jax.nn.sigmoid(z[:, PW:])
    return (jnp.concatenate([yg, mo], axis=1) * silu(gate),)


def fn_merge(prim, mo, gate):
    return (jnp.concatenate([prim, mo], axis=1) * silu(gate),)


def fn_mem_k(kv, g):
    return (jnp.concatenate([rms(kv[:, h * XHD:(h + 1) * XHD], g) for h in range(XH)], axis=1),)


def fn_mem_attn(xq, kn, v, g):
    outs = []
    for h in range(XH):
        sl = slice(h * XHD, (h + 1) * XHD)
        p = softmax_rows(mm_nt(rms(xq[:, sl], g), kn[:, sl]) * (XHD ** -0.5))
        outs.append(mm_nn(p, v[:, sl]))
    return (jnp.concatenate(outs, axis=1),)


def _half_rms(x, g):
    lo = lax.broadcasted_iota(jnp.int32, x.shape, 1) < ROPE
    x2 = x * x
    s_lo = jnp.sum(jnp.where(lo, x2, 0.0), axis=1, keepdims=True)
    s_hi = jnp.sum(jnp.where(lo, 0.0, x2), axis=1, keepdims=True)
    return x * lax.rsqrt(jnp.where(lo, s_lo, s_hi) / ROPE + EPS) * g


def _rope(x, cos2, sin_signed):
    first = lax.broadcasted_iota(jnp.int32, x.shape, 1) % ROPE < ROPE // 2
    return x * cos2 + jnp.where(first, lane_roll(x, 128 - ROPE // 2), lane_roll(x, ROPE // 2)) * sin_signed


def fn_mla_prep(q, kv, kr, cos2, sin_signed, qnn, knn, qrn, krn):
    lo = lax.broadcasted_iota(jnp.int32, kr.shape, 1) < ROPE
    kr_pad = jnp.where(lo, _rope(_half_rms(kr, krn), cos2, sin_signed), 0.0)
    qf, kf, vs = [], [], []
    for m in range(MH // 2):
        pair = _rope(_half_rms(q[:, MH * NOPE + 128 * m:MH * NOPE + 128 * (m + 1)], qrn), cos2, sin_signed)
        for h, rope_h in ((2 * m, pair), (2 * m + 1, lane_roll(pair, ROPE))):
            qf.append(jnp.concatenate([rms(q[:, NOPE * h:NOPE * (h + 1)], qnn), jnp.where(lo, rope_h, 0.0)], axis=1))
    for h in range(MH):
        kf.append(jnp.concatenate([rms(kv[:, 256 * h:256 * h + NOPE], knn), kr_pad], axis=1))
        vs.append(kv[:, 256 * h + NOPE:256 * (h + 1)])
    return jnp.stack(qf), jnp.stack(kf), jnp.stack(vs)


ATT_TQ = 256


def _attn_tile(q, kf, v, q_start):
    s = mm_nt(q, kf) * ((NOPE + ROPE) ** -0.5)
    q_pos = q_start + lax.broadcasted_iota(jnp.int32, s.shape, 0)
    k_pos = lax.broadcasted_iota(jnp.int32, s.shape, 1)
    p = softmax_rows(jnp.where(k_pos <= q_pos, s, jnp.finfo(F32).min))
    return mm_nn(p, v)


def _attn_specs():
    q_spec = pl.BlockSpec((None, ATT_TQ, 256), lambda h, i: (h, i, 0))
    k_spec = pl.BlockSpec((None, L, 256), lambda h, i: (h, 0, 0))
    v_spec = pl.BlockSpec((None, L, 128), lambda h, i: (h, 0, 0))
    o_spec = pl.BlockSpec((ATT_TQ, 128), lambda h, i: (i, h))
    return q_spec, k_spec, v_spec, o_spec


def causal_attn(qf, kf, vh):
    n_tiles = L // ATT_TQ

    def kern(q_ref, k_ref, v_ref, o_ref):
        i = pl.program_id(1)
        for t in range(n_tiles):
            @pl.when(i == t)
            def _(t=t):
                keys = (t + 1) * ATT_TQ
                o_ref[...] = _attn_tile(q_ref[...], k_ref[:keys, :], v_ref[:keys, :], t * ATT_TQ)

    q_spec, k_spec, v_spec, o_spec = _attn_specs()
    return pl.pallas_call(kern, grid=(MH, n_tiles), in_specs=[q_spec, k_spec, v_spec], out_specs=o_spec,
                          out_shape=SDS((L, MH * VD), F32), name="l1_attn", compiler_params=_cparams())(qf, kf, vh)


def causal_attn_bwd(qf, kf, vh, dout):
    n_tiles = L // ATT_TQ

    def kern(q_ref, k_ref, v_ref, do_ref, dq_ref, dk_ref, dv_ref):
        i = pl.program_id(1)

        @pl.when(i == 0)
        def _():
            dk_ref[...] = jnp.zeros_like(dk_ref)
            dv_ref[...] = jnp.zeros_like(dv_ref)

        for t in range(n_tiles):
            @pl.when(i == t)
            def _(t=t):
                keys = (t + 1) * ATT_TQ
                _, vjp = jax.vjp(lambda q, k, v: _attn_tile(q, k, v, t * ATT_TQ),
                                 q_ref[...].astype(F32), k_ref[:keys, :].astype(F32), v_ref[:keys, :].astype(F32))
                dq, dk, dv = vjp(do_ref[...])
                dq_ref[...] = dq
                dk_ref[:keys, :] += dk
                dv_ref[:keys, :] += dv

    q_spec, k_spec, v_spec, o_spec = _attn_specs()
    return pl.pallas_call(kern, grid=(MH, n_tiles), in_specs=[q_spec, k_spec, v_spec, o_spec], out_specs=[q_spec, k_spec, v_spec],
                          out_shape=[SDS(qf.shape, F32), SDS(kf.shape, F32), SDS(vh.shape, F32)], name="l1_attn_bwd",
                          compiler_params=_cparams())(qf, kf, vh, dout)


def loss_and_grad(y, target, tl=256):
    def kern(y_ref, t_ref, dy_ref, loss_ref):
        d = y_ref[...] - t_ref[...]
        dy_ref[...] = d / D

        @pl.when(pl.program_id(0) == 0)
        def _():
            loss_ref[...] = jnp.zeros_like(loss_ref)

        loss_ref[...] += 0.5 * jnp.sum(jnp.sum(d * d, axis=1, keepdims=True), axis=0, keepdims=True) / D

    return pl.pallas_call(kern, grid=(L // tl,), in_specs=[rspec(tl, D), rspec(tl, D)], out_specs=[rspec(tl, D), cspec((1, 1))],
                          out_shape=[SDS((L, D), F32), SDS((1, 1), F32)], name="loss", compiler_params=_cparams())(y, target)


def adamw(name, w, g, m, v):
    rows, cols = w.shape
    block_row_bytes = 7 * 2 * 4 * max(cols, 128)
    tr = _row_tile(rows, min(2048, ADAMW_VMEM // block_row_bytes // 8 * 8), 8)

    def kern(w_ref, g_ref, m_ref, v_ref, d_ref, nm_ref, nv_ref):
        gg = g_ref[...]
        nm = ADAM_B1 * m_ref[...] + (1.0 - ADAM_B1) * gg
        nv = ADAM_B2 * v_ref[...] + (1.0 - ADAM_B2) * jnp.square(gg)
        m_hat = nm / (1.0 - ADAM_B1 ** ADAM_STEP)
        v_hat = nv / (1.0 - ADAM_B2 ** ADAM_STEP)
        d_ref[...] = -ADAM_LR * (m_hat / (jnp.sqrt(v_hat) + ADAM_EPS) + ADAM_WD * w_ref[...])
        nm_ref[...] = nm
        nv_ref[...] = nv

    spec = rspec(tr, cols)
    return pl.pallas_call(kern, grid=(rows // tr,), in_specs=[spec] * 4, out_specs=[spec] * 3,
                          out_shape=[SDS((rows, cols), F32)] * 3, name=name, compiler_params=_cparams())(w, g, m, v)


def _row_tile(rows, cap=512, unit=16):
    return max(t for t in range(unit, cap + 1, unit) if rows % t == 0)


def _place():
    x, y, c = lax.axis_index("x"), lax.axis_index("y"), lax.axis_index("c")
    return x, y, c, [(1 - x, y), (x, 1 - y), (1 - x, 1 - y)]


def _row_chunks(rows, n, dtype):
    unit = 32 // jnp.dtype(dtype).itemsize
    base, extra = divmod(rows // unit, n)
    out, start = [], 0
    for k in range(n):
        size = (base + (k < extra)) * unit
        if size:
            out.append((start, size))
            start += size
    assert start == rows, (rows, unit)
    return out


PIECE_BYTES = 1 << 20


def _pieces(shapes_dtypes, rows_of):
    out = []
    for b, (shape, dtype) in enumerate(shapes_dtypes):
        rows = rows_of(shape)
        n = max(1, min(4, rows * shape[-1] * jnp.dtype(dtype).itemsize // PIECE_BYTES))
        out += [(b, st, sz) for st, sz in _row_chunks(rows, n, dtype)]
    return out


def all_gather_chips(name, shards):
    nb = len(shards)
    pieces = _pieces([(s.shape, s.dtype) for s in shards], lambda shape: shape[0] // 2)
    n = len(pieces)

    def body(*refs):
        x_refs, out_refs, send_sems, recv_sems = refs[:nb], refs[nb:2 * nb], refs[2 * nb], refs[2 * nb + 1]
        x, y, c, chips = _place()
        sibling = (x, y, 1 - c)
        mine = 2 * x + y

        def copy(sem, chip, cc, k, to, from_input=False):
            b, st, sz = pieces[k]
            rows_k = pl.ds(cc * (x_refs[b].shape[0] // 2) + st, sz)
            dst = out_refs[b].at[chip, rows_k, :]
            return pltpu.make_async_remote_copy(src_ref=x_refs[b].at[rows_k, :] if from_input else dst, dst_ref=dst,
                                                send_sem=send_sems.at[sem], recv_sem=recv_sems.at[sem], device_id=to, device_id_type=MESH_ID)

        order = [(k, j, 2 * cx + cy, (cx, cy, c)) for k in range(n) for j, (cx, cy) in enumerate(chips)]
        first = [copy(j * n + k, mine, c, k, to, from_input=True) for k, j, _, to in order]
        for cp in first:
            cp.start()
        passed = []
        for k, j, chip, _ in order:
            copy(j * n + k, chip, c, k, sibling).wait_recv()
            passed.append(copy((3 + j) * n + k, chip, c, k, sibling))
            passed[-1].start()
        for k, j, chip, _ in order:
            copy((3 + j) * n + k, chip, 1 - c, k, sibling).wait_recv()
        for cp in first + passed:
            cp.wait_send()

    return pl.pallas_call(body, in_specs=[ANY] * nb, out_specs=[ANY] * nb, out_shape=[SDS((4,) + s.shape, s.dtype) for s in shards],
                          scratch_shapes=_dma_sems(6 * n), name=name)(*shards)


def plan_gather_ici(shards):
    pieces = _pieces([(s.shape, s.dtype) for s in shards], lambda shape: shape[0] // 2)
    n = len(pieces)

    def copies(x_refs, out_refs, send_sems, recv_sems):
        x, y, c, chips = _place()
        mine = 2 * x + y

        def copy(j, k, chip, to, from_input):
            b, st, sz = pieces[k]
            rows_k = pl.ds(c * (x_refs[b].shape[0] // 2) + st, sz)
            dst = out_refs[b].at[chip, rows_k, :]
            return pltpu.make_async_remote_copy(src_ref=x_refs[b].at[rows_k, :] if from_input else dst, dst_ref=dst, send_sem=send_sems.at[j * n + k],
                                                recv_sem=recv_sems.at[j * n + k], device_id=to, device_id_type=MESH_ID)

        order = [(k, j, 2 * cx + cy, (cx, cy, c)) for k in range(n) for j, (cx, cy) in enumerate(chips)]
        return [copy(j, k, mine, to, True) for k, j, _, to in order], [copy(j, k, chip, to, False) for k, j, chip, to in order]

    return Plan(shards, [SDS((4,) + s.shape, s.dtype) for s in shards], {}, 3 * n, copies)


def plan_gather_pass(gathered):
    pieces = _pieces([(g.shape[1:], g.dtype) for g in gathered], lambda shape: shape[0] // 2)
    n = len(pieces)

    def copies(_, out_refs, send_sems, recv_sems):
        x, y, c, chips = _place()

        def copy(j, k, chip, cc):
            b, st, sz = pieces[k]
            rows_k = out_refs[b].at[chip, pl.ds(cc * (out_refs[b].shape[1] // 2) + st, sz), :]
            return pltpu.make_async_remote_copy(src_ref=rows_k, dst_ref=rows_k, send_sem=send_sems.at[j * n + k], recv_sem=recv_sems.at[j * n + k],
                                                device_id=(x, y, 1 - c), device_id_type=MESH_ID)

        order = [(k, j, 2 * cx + cy) for k in range(n) for j, (cx, cy) in enumerate(chips)]
        return [copy(j, k, chip, c) for k, j, chip in order], [copy(j, k, chip, 1 - c) for k, j, chip in order]

    return Plan(gathered, [SDS(g.shape, g.dtype) for g in gathered], {i: i for i in range(len(gathered))}, 3 * n, copies)


def plan_pair_exchange(gs):
    pieces = _pieces([(g.shape, g.dtype) for g in gs], lambda shape: shape[1] // 2)

    def copies(g_refs, got_refs, send_sems, recv_sems):
        x, y, c, _ = _place()
        swaps = [pltpu.make_async_remote_copy(src_ref=g_refs[b].at[:, pl.ds((1 - c) * (g_refs[b].shape[1] // 2) + st, sz), :],
                                              dst_ref=got_refs[b].at[:, pl.ds(st, sz), :], send_sem=send_sems.at[k], recv_sem=recv_sems.at[k],
                                              device_id=(x, y, 1 - c), device_id_type=MESH_ID)
                 for k, (b, st, sz) in enumerate(pieces)]
        return swaps, swaps

    return Plan(gs, [SDS((g.shape[0], g.shape[1] // 2, g.shape[2]), g.dtype) for g in gs], {}, len(pieces), copies)


def plan_chip_scatter(ps):
    pieces = _pieces([(p.shape, p.dtype) for p in ps], lambda shape: shape[1])
    n = len(pieces)

    def copies(p_refs, q_refs, send_sems, recv_sems):
        x, y, c, chips = _place()
        mine = 2 * x + y

        def copy(j, k, src_slot, dst_slot, to):
            b, st, sz = pieces[k]
            return pltpu.make_async_remote_copy(src_ref=p_refs[b].at[src_slot, pl.ds(st, sz), :], dst_ref=q_refs[b].at[dst_slot, pl.ds(st, sz), :],
                                                send_sem=send_sems.at[j * n + k], recv_sem=recv_sems.at[j * n + k], device_id=to,
                                                device_id_type=MESH_ID)

        order = [(k, j, 2 * cx + cy, (cx, cy, c)) for k in range(n) for j, (cx, cy) in enumerate(chips)]
        return [copy(j, k, chip, mine, to) for k, j, chip, to in order], [copy(j, k, mine, chip, to) for k, j, chip, to in order]

    return Plan(ps, [SDS(p.shape, p.dtype) for p in ps], {}, 3 * n, copies)


def plan_pair_join(bufs):
    pieces = _pieces([(b.shape, b.dtype) for b in bufs], lambda shape: shape[0] // 2)

    def copies(_, out_refs, send_sems, recv_sems):
        x, y, c, _ = _place()

        def copy(k, cc):
            b, st, sz = pieces[k]
            rows_k = out_refs[b].at[pl.ds(cc * (out_refs[b].shape[0] // 2) + st, sz), :]
            return pltpu.make_async_remote_copy(src_ref=rows_k, dst_ref=rows_k, send_sem=send_sems.at[k], recv_sem=recv_sems.at[k],
                                                device_id=(x, y, 1 - c), device_id_type=MESH_ID)

        return [copy(k, c) for k in range(len(pieces))], [copy(k, 1 - c) for k in range(len(pieces))]

    return Plan(bufs, [SDS(b.shape, b.dtype) for b in bufs], {i: i for i in range(len(bufs))}, len(pieces), copies)


def run_plan(name, plan):
    hosted_call(lambda: None, name=name, grid=(1,), in_specs=[], out_specs=[], out_shape=[], operands=[], plans=[plan])
    return plan.results


def pair_add(name, g, got, place):
    slots, rows, cols = g.shape
    half = rows // 2
    tr = _row_tile(half)
    nb = half // tr

    def kern(_, g_ref, t_ref, o_ref):
        o_ref[...] = (g_ref[...].astype(F32) + t_ref[...].astype(F32)).astype(o_ref.dtype)

    blk = pl.BlockSpec((None, tr, cols), lambda s, i, p: (s, i, 0))
    grid_spec = pltpu.PrefetchScalarGridSpec(
        num_scalar_prefetch=1, grid=(slots, nb),
        in_specs=[pl.BlockSpec((None, tr, cols), lambda s, i, p: (s, p[1] * nb + i, 0)), blk], out_specs=blk)
    return pl.pallas_call(kern, grid_spec=grid_spec, out_shape=SDS((slots, half, cols), g.dtype), name=name,
                          compiler_params=_cparams())(place, g, got)


def chip_add(name, p, q, place):
    slots, half, cols = p.shape
    tr = _row_tile(half)
    nb = half // tr

    def kern(_, p_ref, q1, q2, q3, o_ref):
        o_ref[...] = p_ref[...].astype(F32) + q1[...].astype(F32) + q2[...].astype(F32) + q3[...].astype(F32)

    def slot(k):
        return pl.BlockSpec((None, tr, cols), lambda i, pr: ((pr[0] + k) % slots, i, 0))

    grid_spec = pltpu.PrefetchScalarGridSpec(
        num_scalar_prefetch=1, grid=(nb,), in_specs=[slot(0), slot(1), slot(2), slot(3)],
        out_specs=pl.BlockSpec((tr, cols), lambda i, pr: (pr[1] * nb + i, 0)))
    return pl.pallas_call(kern, grid_spec=grid_spec, out_shape=SDS((2 * half, cols), F32), name=name,
                          compiler_params=_cparams())(place, p, q, q, q)


def pair_adds(tag, gs, gots, place):
    return [pair_add(f"{tag}_pair_add_{i}", g, got, place) for i, (g, got) in enumerate(zip(gs, gots))]


def chip_adds(tag, pairs, qs, place):
    return [chip_add(f"{tag}_chip_add_{i}", p, q, place) for i, (p, q) in enumerate(zip(pairs, qs))]


def reduce_scatter_chips(tag, gs, place):
    pairs = pair_adds(tag, gs, run_plan(tag + "_pair_exchange", plan_pair_exchange(gs)), place)
    return run_plan(tag + "_pair_join", plan_pair_join(chip_adds(tag, pairs, run_plan(tag + "_chip_scatter", plan_chip_scatter(pairs)), place)))


BIG = [("w_out", (2, 512, 1024)), ("w_mem_kv", (2, 256, 1024)), ("s5_w_in", (1, 1024, 1024)), ("s5_w_glu", (1, 1536, 768)),
       ("mla_w_in", (1, 1024, 848)), ("mla_w_uq", (1, 512, 576)), ("mla_w_ukv", (1, 256, 768))]
SHARDED_SMALL = [("mla_q_lora_norm", (1, 128)), ("mla_kv_lora_norm", (1, 64))]
SMALL = [("ln_gain", (2, 1024)), ("mem_norm", (2, 1024)), ("xq_norm", (2, 128)), ("xk_norm", (2, 128)),
         ("s5_lambda_re", (1, 96, 64)), ("s5_lambda_im", (1, 96, 64)), ("s5_log_step", (1, 96)),
         ("s5_b_re", (1, 96, 64, 16)), ("s5_b_im", (1, 96, 64, 16)), ("s5_c_re", (1, 96, 16, 64)), ("s5_c_im", (1, 96, 16, 64)),
         ("s5_d", (1, 1536)), ("mla_q_nope_norm", (1, 128)), ("mla_k_nope_norm", (1, 128)), ("mla_q_rope_norm", (1, 64)),
         ("mla_k_rope_norm", (1, 64))]
WEIGHT_ORDER = ["ln_gain", "w_out", "mem_norm", "w_mem_kv", "xq_norm", "xk_norm", "s5_w_in", "s5_lambda_re", "s5_lambda_im",
                "s5_log_step", "s5_b_re", "s5_b_im", "s5_c_re", "s5_c_im", "s5_d", "s5_w_glu", "mla_w_in", "mla_q_lora_norm",
                "mla_kv_lora_norm", "mla_w_uq", "mla_w_ukv", "mla_q_nope_norm", "mla_k_nope_norm", "mla_q_rope_norm", "mla_k_rope_norm"]
SMALL_FULL = SMALL + [(n, (1, 4 * s[1])) for n, s in SHARDED_SMALL]
N_SMALL = sum(math.prod(s) for _, s in SMALL_FULL)
SMALL_ROWS, SMALL_LANES = 128, 1024

WIDE0_IN, WIDE0_OUT, WIDE0_MKV = 0, 1024, 1536
PAIR_OUT, PAIR_MKV, PAIR_ROWS = 0, 512, 768


def stack_shards(w, dtype):
    wide0 = jnp.concatenate([w["s5_w_in"][0], w["w_out"][0], w["w_mem_kv"][0]], axis=0)
    wide1 = jnp.concatenate([w["w_out"][1], w["w_mem_kv"][1]], axis=0)
    return ([wide0.astype(dtype), w["s5_w_glu"][0].astype(dtype)],
            [wide1.astype(dtype), w["mla_w_ukv"][0].astype(dtype), w["mla_w_in"][0].astype(dtype), w["mla_w_uq"][0].astype(dtype)])


def weight_views0(wide0, glu):
    return {"s5_w_in": Sharded(wide0, "col", WIDE0_IN, 1024), "w_out": Sharded(wide0, "row", WIDE0_OUT, 512),
            "w_mem_kv": Sharded(wide0, "row", WIDE0_MKV, 256), "s5_w_glu": Sharded(glu, "col", 0, 1536)}


def weight_views1(wide1, ukv):
    return {"w_out": Sharded(wide1, "row", PAIR_OUT, 512), "w_mem_kv": Sharded(wide1, "row", PAIR_MKV, 256), "mla_w_ukv": Sharded(ukv, "col", 0, 256)}


def grad_views():
    pair = SDS((4, PAIR_ROWS, 1024), BF16)
    return {"w_out": Sharded(pair, "row", PAIR_OUT, 512), "w_mem_kv": Sharded(pair, "row", PAIR_MKV, 256),
            "s5_w_in": Sharded(SDS((4, 1024, 1024), BF16), "col", 0, 1024), "s5_w_glu": Sharded(SDS((4, 1536, 768), BF16), "col", 0, 1536),
            "mla_w_ukv": Sharded(SDS((4, 256, 768), BF16), "col", 0, 256)}


def cols_to_shards(full):
    return full.reshape(full.shape[0], 4, full.shape[1] // 4).transpose(1, 0, 2)


def shards_to_cols(arr):
    return arr.transpose(1, 0, 2).reshape(arr.shape[1], 4 * arr.shape[2])


def mla_in_permute(w):
    o1, o2, o3, o4 = QL, QL + KVL, QL + KVL + ROPE, QL + KVL + ROPE + XQW
    return jnp.concatenate([w[:, o4:], w[:, :o1], w[:, o3:o4], w[:, o1:o2], w[:, o2:o3],
                            jnp.zeros((w.shape[0], MLA_IN_P - MLA_IN), w.dtype)], axis=1)


def mla_in_unpermute(d):
    return jnp.concatenate([d[:, 2048:2560], d[:, 3072:3328], d[:, 3328:3392], d[:, 2560:3072], d[:, :2048]], axis=1)


def uq_permute(w):
    w3 = w.reshape(w.shape[0], MH, NOPE + ROPE)
    return jnp.concatenate([w3[:, :, :NOPE].reshape(w.shape[0], MH * NOPE), w3[:, :, NOPE:].reshape(w.shape[0], MH * ROPE)], axis=1)


def uq_unpermute(d):
    dn = d[:, :MH * NOPE].reshape(d.shape[0], MH, NOPE)
    dr = d[:, MH * NOPE:].reshape(d.shape[0], MH, ROPE)
    return jnp.concatenate([dn, dr], axis=2).reshape(d.shape[0], MH * (NOPE + ROPE))


def time_permute(a):
    return a.reshape(SEG, SEG_LEN, a.shape[-1]).transpose(1, 0, 2).reshape(L, a.shape[-1])


def time_unpermute(a):
    return a.reshape(SEG_LEN, SEG, a.shape[-1]).transpose(1, 0, 2).reshape(L, a.shape[-1])


def mem_branch_fwd(tag, mem, mem_norm, w_mem_kv, xk_norm):
    mn = row_fwd(tag + "_mem_rms", fn_rms, ML, ML, [(mem, D, 0)], [mem_norm], [(D, BF16)])[0]
    kv = matmul(tag + "_mem_kv", mn, w_mem_kv, "nn")
    kn = row_fwd(tag + "_mem_knorm", fn_mem_k, ML, ML, [(kv, XQW, 0)], [xk_norm], [(XQW, F32)])[0]
    return mn, kv, kn


def mem_branch_bwd(tag, mem, mem_norm, w_mem_kv, xk_norm, mn, kv, dkn, dv, g_view, g_wide):
    dk, dxk = row_bwd(tag + "_mem_knorm_bwd", fn_mem_k, ML, ML, [(kv, XQW, 0)], [xk_norm], [(dkn, XQW, 0)], [True], [True])
    dkv = jnp.concatenate([dk, dv], axis=1)
    dmn = matmul(tag + "_mem_kv_dx", dkv, w_mem_kv, "nt")
    g_wide = matmul(tag + "_mem_kv_dw", mn, dkv, "tn", out=g_view, into=g_wide)
    dmem_norm = row_bwd(tag + "_mem_rms_bwd", fn_rms, ML, ML, [(mem, D, 0)], [mem_norm], [(dmn, D, 0)], [False], [True])[0]
    return g_wide, dmem_norm, dxk


def mem_attn_fwd(tag, proj, cb, kn, kv, xq_norm):
    return row_fwd(tag + "_mem_attn", fn_mem_attn, L, 256, [(proj, XQW, cb)], [kn, kv[:, XQW:], xq_norm], [(XQW, F32)])[0]


def mem_attn_bwd(tag, proj, cb, kn, kv, xq_norm, dmo, dproj):
    place = {"cols": proj.shape[1], "cb": cb, "into": dproj}
    return row_bwd(tag + "_mem_attn_bwd", fn_mem_attn, L, 256, [(proj, XQW, cb)], [kn, kv[:, XQW:], xq_norm], [(dmo, XQW, 0)],
                   [place], [True, True, True])


def device_step(x, mem, positions, target, small, env, hooks=None):
    hooks = hooks or {}

    def plans_for(name):
        return hooks[("plans", name)](env) if ("plans", name) in hooks else ()

    def after(name):
        if ("after", name) in hooks:
            hooks[("after", name)](env)

    g = {}
    w0 = weight_views0(env["wide0"], env["glu"])
    gw = grad_views()
    ln, mem_norm, xq_norm, xk_norm = small["ln_gain"], small["mem_norm"], small["xq_norm"], small["xk_norm"]

    lre, lim = small["s5_lambda_re"][0], small["s5_lambda_im"][0]
    ls = small["s5_log_step"].reshape(SG, 1)
    one = pl.BlockSpec((SG, SP), lambda i: (0, 0))
    col = pl.BlockSpec((SG, 1), lambda i: (0, 0))
    disc_ins = [(lre, one), (lim, one), (ls, col)]
    a_re, a_im, coef_re, coef_im = stage("s5_disc", fn_s5_disc, (1,), disc_ins, [(SDS((SG, SP), F32), one)] * 4)
    b_re, b_im = small["s5_b_re"].reshape(SN, SC), small["s5_b_im"].reshape(SN, SC)
    c_re, c_im = small["s5_c_re"].reshape(PW, SP), small["s5_c_im"].reshape(PW, SP)
    bmat_rows = [(b_re, SC, 0), (b_im, SC, 0), (coef_re.reshape(SN, 1), 1, 0), (coef_im.reshape(SN, 1), 1, 0)]
    wb_re, wb_im = row_fwd("s5_bmat", fn_s5_bmat, SN, 512, bmat_rows, [], [(128, F32)] * 2)
    cmat_rows = [(c_re, SP, 0), (c_im, SP, 0)]
    wc_re, wc_im = row_fwd("s5_cmat", fn_s5_cmat, PW, 128, cmat_rows, [], [(512, F32)] * 2)
    a_re_v, a_im_v = a_re.reshape(1, SN), a_im.reshape(1, SN)
    s5_d = small["s5_d"]

    xp = time_permute(x)
    h0 = row_fwd("l0_rms", fn_rms, L, 256, [(xp, D, 0)], [ln[0:1]], [(D, BF16)])[0]
    proj0 = matmul("l0_in", h0, w0["s5_w_in"], "nn")
    s_re, s_im, g0 = s5_forward(proj0, wb_re, wb_im, wc_re, wc_im, a_re_v, a_im_v, s5_d, plans=plans_for("s5_forward"))
    z0 = matmul("l0_glu", g0, w0["s5_w_glu"], "nn", plans=plans_for("l0_glu"))
    after("l0_glu")
    mn0, kv0, kn0 = mem_branch_fwd("l0", mem, mem_norm[0:1], w0["w_mem_kv"], xk_norm[0:1])
    mo0 = mem_attn_fwd("l0", proj0, 3, kn0, kv0, xq_norm[0:1])
    o0 = row_fwd("l0_merge", fn_merge_glu, L, 256, [(z0, 2 * PW, 0), (mo0, XQW, 0), (proj0, BW, 1)], [], [(BW, BF16)])[0]
    x1p = matmul("l0_out", o0, w0["w_out"], "nn", add=xp)
    x1 = time_unpermute(x1p)

    w1 = weight_views1(env["wide1"], env["ukv"])
    w_in1, w_uq = env["w_in1"], env["w_uq"]
    h1 = row_fwd("l1_rms", fn_rms, L, 256, [(x1, D, 0)], [ln[1:2]], [(D, BF16)])[0]
    proj1 = matmul("l1_in", h1, w_in1, "nn")
    qln, kvln = env["q_lora_norm"].reshape(1, QL), env["kv_lora_norm"].reshape(1, KVL)
    cqn = row_fwd("l1_q_lora_rms", fn_rms, L, 256, [(proj1, QL, 4)], [qln], [(QL, BF16)])[0]
    ckvn = row_fwd("l1_kv_lora_rms", fn_rms, L, 256, [(proj1, KVL, 12)], [kvln], [(KVL, BF16)])[0]
    q = matmul("l1_uq", cqn, w_uq, "nn")
    kv = matmul("l1_ukv", ckvn, w1["mla_w_ukv"], "nn")
    inv_freq = ROPE_THETA ** (-jnp.arange(ROPE // 2, dtype=F32) / (ROPE // 2))
    ang = positions.astype(F32)[:, None] * inv_freq
    cos2 = jnp.tile(jnp.cos(ang), (1, 4))
    sin_signed = jnp.tile(jnp.concatenate([-jnp.sin(ang), jnp.sin(ang)], axis=1), (1, 2))
    qnn, knn = small["mla_q_nope_norm"], small["mla_k_nope_norm"]
    qrn, krn = jnp.tile(small["mla_q_rope_norm"], (1, 2)), jnp.tile(small["mla_k_rope_norm"], (1, 2))
    tp = 256
    prep_ins = [(q, rspec(tp, MH * (NOPE + ROPE))), (kv, rspec(tp, MH * 256)), (proj1, rspec(tp, 128, 26)),
                (cos2, rspec(tp, 128)), (sin_signed, rspec(tp, 128))] + [(a, cspec((1, 128))) for a in (qnn, knn, qrn, krn)]
    hq_spec = pl.BlockSpec((MH, tp, 256), lambda i: (0, i, 0))
    hv_spec = pl.BlockSpec((MH, tp, 128), lambda i: (0, i, 0))
    qf, kf, vh = stage("l1_mla_prep", fn_mla_prep, (L // tp,), prep_ins,
                       [(SDS((MH, L, 256), BF16), hq_spec), (SDS((MH, L, 256), BF16), hq_spec), (SDS((MH, L, 128), BF16), hv_spec)])
    attn = causal_attn(qf, kf, vh)
    mn1, kv1, kn1 = mem_branch_fwd("l1", mem, mem_norm[1:2], w1["w_mem_kv"], xk_norm[1:2])
    mo1 = mem_attn_fwd("l1", proj1, 5, kn1, kv1, xq_norm[1:2])
    o1 = row_fwd("l1_merge", fn_merge, L, 256, [(attn, PW, 0), (mo1, XQW, 0), (proj1, BW, 0)], [], [(BW, BF16)])[0]
    x2 = matmul("l1_out", o1, w1["w_out"], "nn", add=x1)
    dx2, loss = loss_and_grad(x2, target)

    do1 = matmul("l1_out_dx", dx2, w1["w_out"], "nt")
    g_pair1 = matmul("l1_out_dw", o1, dx2, "tn", out=gw["w_out"])
    dattn, dmo1, dproj1 = row_bwd("l1_merge_bwd", fn_merge, L, 256, [(attn, PW, 0), (mo1, XQW, 0), (proj1, BW, 0)], [],
                                  [(do1, BW, 0)], [True, True, {"cols": MLA_IN_P, "cb": 0}], [])
    dproj1, dkn1, dv1, dxqn1 = mem_attn_bwd("l1", proj1, 5, kn1, kv1, xq_norm[1:2], dmo1, dproj1)
    env["g_pair1"], dmem_norm1, dxk1 = mem_branch_bwd("l1", mem, mem_norm[1:2], w1["w_mem_kv"], xk_norm[1:2], mn1, kv1, dkn1, dv1,
                                                      gw["w_mem_kv"], g_pair1)
    dqf, dkf, dvh = causal_attn_bwd(qf, kf, vh, dattn)
    prep_diffs = [("row", SDS((L, MH * (NOPE + ROPE)), F32), rspec(tp, MH * (NOPE + ROPE))), ("row", SDS((L, MH * 256), F32), rspec(tp, MH * 256)),
                  ("row", SDS((L, MLA_IN_P), F32), rspec(tp, 128, 26), {"into": dproj1}), None, None] + [("acc", (0,))] * 4
    dq, dkv, dproj1, dqnn, dknn, dqrn, dkrn = stage_bwd("l1_mla_prep_bwd", fn_mla_prep, (L // tp,), prep_ins,
                                                        [(dqf, hq_spec), (dkf, hq_spec), (dvh, hv_spec)], prep_diffs)
    dcqn = matmul("l1_uq_dx", dq, w_uq, "nt")
    env["dw_uq"] = matmul("l1_uq_dw", cqn, dq, "tn")
    dckvn = matmul("l1_ukv_dx", dkv, w1["mla_w_ukv"], "nt")
    env["g_ukv"] = matmul("l1_ukv_dw", ckvn, dkv, "tn", out=gw["mla_w_ukv"])
    dproj1, dqln = row_bwd("l1_q_lora_rms_bwd", fn_rms, L, 256, [(proj1, QL, 4)], [qln], [(dcqn, QL, 0)],
                           [{"cols": MLA_IN_P, "cb": 4, "into": dproj1}], [True])
    dproj1, dkvln = row_bwd("l1_kv_lora_rms_bwd", fn_rms, L, 256, [(proj1, KVL, 12)], [kvln], [(dckvn, KVL, 0)],
                            [{"cols": MLA_IN_P, "cb": 12, "into": dproj1}], [True])
    dh1 = matmul("l1_in_dx", dproj1, w_in1, "nt")
    env["dw_in1"] = matmul("l1_in_dw", h1, dproj1, "tn")
    dx1, dln1 = row_bwd("l1_rms_bwd", fn_rms, L, 256, [(x1, D, 0)], [ln[1:2]], [(dh1, D, 0)], [{"add": (dx2, D, 0)}], [True])
    dx1p = time_permute(dx1)

    do0 = matmul("l0_out_dx", dx1p, w0["w_out"], "nt", plans=plans_for("l0_out_dx"))
    g_pair0 = matmul("l0_out_dw", o0, dx1p, "tn", out=gw["w_out"])
    dz0, dmo0, dproj0 = row_bwd("l0_merge_bwd", fn_merge_glu, L, 256, [(z0, 2 * PW, 0), (mo0, XQW, 0), (proj0, BW, 1)], [],
                                [(do0, BW, 0)], [True, True, {"cols": 2 * BW, "cb": 1}], [])
    dproj0, dkn0, dv0, dxqn0 = mem_attn_bwd("l0", proj0, 3, kn0, kv0, xq_norm[0:1], dmo0, dproj0)
    env["g_pair0"], dmem_norm0, dxk0 = mem_branch_bwd("l0", mem, mem_norm[0:1], w0["w_mem_kv"], xk_norm[0:1], mn0, kv0, dkn0, dv0,
                                                      gw["w_mem_kv"], g_pair0)
    env["g_glu"] = matmul("l0_glu_dw", g0, dz0, "tn", out=gw["s5_w_glu"])
    dg0 = matmul("l0_glu_dx", dz0, w0["s5_w_glu"], "nt", plans=plans_for("l0_glu_dx"))
    dproj0, dd, dwc_re, dwc_im, dwb_re, dwb_im, da_re, da_im = s5_backward(dg0, proj0, s_re, s_im, wb_re, wb_im, wc_re, wc_im,
                                                                           a_re_v, a_im_v, s5_d, dproj0, plans=plans_for("s5_backward"))
    dh0 = matmul("l0_in_dx", dproj0, w0["s5_w_in"], "nt", plans=plans_for("l0_in_dx"))
    env["g_in0"] = matmul("l0_in_dw", h0, dproj0, "tn", out=gw["s5_w_in"])
    dxp, dln0 = row_bwd("l0_rms_bwd", fn_rms, L, 256, [(xp, D, 0)], [ln[0:1]], [(dh0, D, 0)], [{"add": (dx1p, D, 0)}], [True])
    grad_x = time_unpermute(dxp)

    db_re, db_im, dcoef_re, dcoef_im = row_bwd("s5_bmat_bwd", fn_s5_bmat, SN, 512, bmat_rows, [], [(dwb_re, 128, 0), (dwb_im, 128, 0)],
                                               [True] * 4, [])
    dc_re, dc_im = row_bwd("s5_cmat_bwd", fn_s5_cmat, PW, 128, cmat_rows, [], [(dwc_re, 512, 0), (dwc_im, 512, 0)], [True] * 2, [])
    disc_cts = [(da_re.reshape(SG, SP), one), (da_im.reshape(SG, SP), one), (dcoef_re.reshape(SG, SP), one), (dcoef_im.reshape(SG, SP), one)]
    dlre, dlim, dls = stage_bwd("s5_disc_bwd", fn_s5_disc, (1,), disc_ins, disc_cts, [("acc", (0,))] * 3)

    g["ln_gain"] = jnp.concatenate([dln0, dln1], axis=0)
    g["mem_norm"] = jnp.concatenate([dmem_norm0, dmem_norm1], axis=0)
    g["xq_norm"] = jnp.concatenate([dxqn0, dxqn1], axis=0)
    g["xk_norm"] = jnp.concatenate([dxk0, dxk1], axis=0)
    g["s5_lambda_re"], g["s5_lambda_im"], g["s5_log_step"] = dlre, dlim, dls
    g["s5_b_re"], g["s5_b_im"], g["s5_c_re"], g["s5_c_im"] = db_re, db_im, dc_re, dc_im
    g["s5_d"] = dd
    g["mla_q_lora_norm"], g["mla_kv_lora_norm"] = dqln, dkvln
    g["mla_q_nope_norm"], g["mla_k_nope_norm"] = dqnn, dknn
    g["mla_q_rope_norm"] = dqrn[:, :ROPE] + dqrn[:, ROPE:]
    g["mla_k_rope_norm"] = dkrn[:, :ROPE] + dkrn[:, ROPE:]
    return loss, grad_x, g


def kernel(x, mem, positions, ln_gain, w_out, mem_norm, w_mem_kv, xq_norm, xk_norm, s5_w_in, s5_lambda_re, s5_lambda_im, s5_log_step, s5_b_re, s5_b_im, s5_c_re, s5_c_im, s5_d, s5_w_glu, mla_w_in, mla_q_lora_norm, mla_kv_lora_norm, mla_w_uq, mla_w_ukv, mla_q_nope_norm, mla_k_nope_norm, mla_q_rope_norm, mla_k_rope_norm, loss_target, m_ln_gain, m_w_out, m_mem_norm, m_w_mem_kv, m_xq_norm, m_xk_norm, m_s5_w_in, m_s5_lambda_re, m_s5_lambda_im, m_s5_log_step, m_s5_b_re, m_s5_b_im, m_s5_c_re, m_s5_c_im, m_s5_d, m_s5_w_glu, m_mla_w_in, m_mla_q_lora_norm, m_mla_kv_lora_norm, m_mla_w_uq, m_mla_w_ukv, m_mla_q_nope_norm, m_mla_k_nope_norm, m_mla_q_rope_norm, m_mla_k_rope_norm, v_ln_gain, v_w_out, v_mem_norm, v_w_mem_kv, v_xq_norm, v_xk_norm, v_s5_w_in, v_s5_lambda_re, v_s5_lambda_im, v_s5_log_step, v_s5_b_re, v_s5_b_im, v_s5_c_re, v_s5_c_im, v_s5_d, v_s5_w_glu, v_mla_w_in, v_mla_q_lora_norm, v_mla_kv_lora_norm, v_mla_w_uq, v_mla_w_ukv, v_mla_q_nope_norm, v_mla_k_nope_norm, v_mla_q_rope_norm, v_mla_k_rope_norm):
    args = dict(locals())
    wts = {n: args[n] for n in WEIGHT_ORDER}
    mom = {n: args["m_" + n] for n in WEIGHT_ORDER}
    var = {n: args["v_" + n] for n in WEIGHT_ORDER}

    chip = 2 * lax.axis_index("x") + lax.axis_index("y")
    place = jnp.stack([chip, lax.axis_index("c")]).astype(jnp.int32)

    def own_slot(gathered, shards):
        return [lax.dynamic_update_slice(g, s[None], (chip, 0, 0)) for g, s in zip(gathered, shards)]

    shards0, shards1 = stack_shards(wts, BF16)
    shards1.append(jnp.concatenate([mla_q_lora_norm, jnp.pad(mla_kv_lora_norm, ((0, 0), (0, 64))), jnp.zeros((14, 128), F32)], axis=0))
    env = dict(zip(("wide0", "glu"), own_slot(all_gather_chips("gather_weights0", shards0), shards0)))
    over_ici = plan_gather_ici(shards1)
    hooks = {("plans", "s5_forward"): lambda env: [over_ici]}
    passed_on = []

    def pass_on(env):
        passed_on.append(plan_gather_pass(over_ici.results))
        return passed_on

    def layer1_weights(env):
        wide1, ukv, in1, uq, norms = own_slot(passed_on[0].results, shards1)
        env.update(wide1=wide1, ukv=ukv, w_in1=mla_in_permute(shards_to_cols(in1)), w_uq=uq_permute(shards_to_cols(uq)),
                   q_lora_norm=norms[:, 0, :], kv_lora_norm=norms[:, 1, :64])

    hooks["plans", "l0_glu"], hooks["after", "l0_glu"] = pass_on, layer1_weights

    rs = {}

    def start_group(k, names):
        def hook(env):
            rs[k, "g"] = [fix(env[n]) for n, fix in names]
            rs[k, "swap"] = plan_pair_exchange(rs[k, "g"])
            return [rs[k, "swap"]]
        return hook

    def as_shards(unpermute):
        return lambda d: cols_to_shards(unpermute(d)).astype(BF16)

    same = lambda a: a
    hooks["plans", "l0_out_dx"] = start_group(1, [("g_pair1", same), ("g_ukv", same), ("dw_in1", as_shards(mla_in_unpermute)), ("dw_uq", as_shards(uq_unpermute))])
    hooks["plans", "l0_glu_dx"] = start_group(0, [("g_pair0", same), ("g_glu", same)])

    def scatter(env):
        for k in (1, 0):
            rs[k, "pairs"] = pair_adds(f"rs{k}", rs[k, "g"], rs[k, "swap"].results, place)
            rs[k, "scatter"] = plan_chip_scatter(rs[k, "pairs"])
        return [rs[1, "scatter"], rs[0, "scatter"]]

    def join(env):
        bufs = [b for k in (1, 0) for b in chip_adds(f"rs{k}", rs[k, "pairs"], rs[k, "scatter"].results, place)]
        rs["join"] = plan_pair_join(bufs)
        return [rs["join"]]

    hooks["plans", "s5_backward"], hooks["plans", "l0_in_dx"] = scatter, join

    small = {n: wts[n] for n, _ in SMALL}
    loss, grad_x, g = device_step(x[0], mem[0], positions[0], loss_target[0], small, env, hooks)
    loss = lax.psum(loss[0, 0], MESH_AXES)
    r_pair1, r_ukv, r_in1, r_uq, r_pair0, r_glu = rs["join"].results

    small_flat = jnp.concatenate([g[n].reshape(-1) for n, _ in SMALL_FULL])
    g_small = jnp.pad(small_flat, (0, 4 * SMALL_ROWS * SMALL_LANES - N_SMALL)).astype(BF16).reshape(4, SMALL_ROWS, SMALL_LANES)
    r_in0, r_small = reduce_scatter_chips("rs2", [env["g_in0"], g_small], place)
    small_all = own_slot(all_gather_chips("gather_small_grads", [r_small]), [r_small])[0].reshape(-1)[:N_SMALL]

    grads = {"w_out": jnp.stack([r_pair0[:PAIR_MKV], r_pair1[:PAIR_MKV]]), "w_mem_kv": jnp.stack([r_pair0[PAIR_MKV:], r_pair1[PAIR_MKV:]]),
             "s5_w_in": r_in0[None], "s5_w_glu": r_glu[None], "mla_w_ukv": r_ukv[None], "mla_w_in": r_in1[None], "mla_w_uq": r_uq[None]}
    off = 0
    for n, s in SMALL_FULL:
        grads[n] = small_all[off:off + math.prod(s)].reshape(s)
        off += math.prod(s)
    for n, s in SHARDED_SMALL:
        grads[n] = lax.dynamic_slice(grads[n], (0, chip * s[1]), s)

    delta, new_m, new_v = {}, {}, {}
    own_layout = [(n, (s[0] * s[1], s[2])) for n, s in BIG] + [(n, (s[1] * s[2], s[3])) for n, s in SMALL if len(s) == 4]
    for n, two_d in own_layout:
        res = adamw("adamw_" + n, wts[n].reshape(two_d), grads[n].reshape(two_d), mom[n].reshape(two_d), var[n].reshape(two_d))
        delta[n], new_m[n], new_v[n] = (r.reshape(wts[n].shape) for r in res)
    small_names = [n for n, s in SMALL if len(s) < 4] + [n for n, _ in SHARDED_SMALL]
    n_own = sum(wts[n].size for n in small_names)
    rows_own = -(-n_own // (8 * 128)) * 8

    def pack_small(d):
        flat = jnp.concatenate([d[n].reshape(-1) for n in small_names])
        return jnp.pad(flat, (0, rows_own * 128 - n_own), constant_values=1.0).reshape(rows_own, 128)

    res = adamw("adamw_small", pack_small(wts), pack_small(grads), pack_small(mom), pack_small(var))
    off = 0
    for n in small_names:
        size = wts[n].size
        delta[n], new_m[n], new_v[n] = (r.reshape(-1)[off:off + size].reshape(wts[n].shape) for r in res)
        off += size

    return (loss, grad_x[None], *[grads[n] for n in WEIGHT_ORDER], *[delta[n] for n in WEIGHT_ORDER],
            *[new_m[n] for n in WEIGHT_ORDER], *[new_v[n] for n in WEIGHT_ORDER])
```

```python
import functools
import math

import jax
import jax.numpy as jnp
from jax import lax
from jax.experimental import pallas as pl
from jax.experimental.pallas import tpu as pltpu

F32, BF16 = jnp.float32, jnp.bfloat16
SDS = jax.ShapeDtypeStruct

D = 1024
L = 2048
ML = 256
BW = 2 * D
XQW = BW // 4
PW = BW - XQW
XH, XHD = 4, 128
SG, SC, SP = 96, 16, 64
SN = SG * SP
NOPE, ROPE, VD = 128, 64, 128
MH = 12
QL, KVL = 512, 256
EPS = 1e-6
ROPE_THETA = 10000.0
MLA_IN = QL + KVL + ROPE + XQW + BW
MLA_IN_P = 3456
ADAM_LR, ADAM_B1, ADAM_B2, ADAM_EPS, ADAM_WD, ADAM_STEP = 0.001, 0.9, 0.999, 1e-08, 0.01, 10

VMEM_LIMIT = 48 * 2**20
SEG = 8
SEG_LEN = L // SEG
MESH_AXES = ("x", "y", "c")


def _cparams():
    return pltpu.CompilerParams(vmem_limit_bytes=VMEM_LIMIT)


def _dg(a, b, ca, cb):
    return lax.dot_general(a.astype(BF16), b.astype(BF16), (((ca,), (cb,)), ((), ())), preferred_element_type=F32)


@jax.custom_vjp
def mm_nn(a, b):
    return _dg(a, b, 1, 0)


mm_nn.defvjp(lambda a, b: (_dg(a, b, 1, 0), (a, b)), lambda res, g: (_dg(g, res[1], 1, 1), _dg(res[0], g, 0, 0)))


@jax.custom_vjp
def mm_nt(a, b):
    return _dg(a, b, 1, 1)


mm_nt.defvjp(lambda a, b: (_dg(a, b, 1, 1), (a, b)), lambda res, g: (_dg(g, res[1], 1, 0), _dg(g, res[0], 0, 0)))


@functools.partial(jax.custom_vjp, nondiff_argnums=(1,))
def lane_roll(x, shift):
    return pltpu.roll(x, shift, 1)


lane_roll.defvjp(lambda x, shift: (pltpu.roll(x, shift, 1), None),
                 lambda shift, _, g: (pltpu.roll(g, (128 - shift) % 128, 1),))


def rms(x, g):
    return x * lax.rsqrt(jnp.mean(x * x, axis=-1, keepdims=True) + EPS) * g


def softmax_rows(s):
    m = lax.stop_gradient(jnp.max(s, axis=-1, keepdims=True))
    e = jnp.exp(s - m)
    return e / jnp.sum(e, axis=-1, keepdims=True)


def silu(x):
    return x * jax.nn.sigmoid(x)


ANY = pl.BlockSpec(memory_space=pl.ANY)
MESH_ID = pl.DeviceIdType.MESH


def _dma_sems(n):
    return [pltpu.SemaphoreType.DMA((n,)), pltpu.SemaphoreType.DMA((n,))]


class Plan:
    def __init__(self, operands, out_shape, aliases, n_sems, copies):
        self.operands, self.out_shape, self.aliases, self.n_sems, self.copies = list(operands), list(out_shape), aliases, n_sems, copies
        self.results = None


def hosted_call(kern, *, name, grid, in_specs, out_specs, out_shape, operands, scratch_shapes=(), aliases=None, cparams=None, plans=()):
    n_in, n_out, n_scr = len(in_specs), len(out_specs), len(scratch_shapes)
    p_in, p_out = [len(p.operands) for p in plans], [len(p.out_shape) for p in plans]
    all_aliases = dict(aliases or {})
    in_off, out_off = n_in, n_out
    for p, ni, no in zip(plans, p_in, p_out):
        all_aliases.update({in_off + i: out_off + o for i, o in p.aliases.items()})
        in_off, out_off = in_off + ni, out_off + no

    def body(*refs):
        pos, pins, pouts = n_in, [], []
        for ni in p_in:
            pins.append(refs[pos:pos + ni])
            pos += ni
        main_out = refs[pos:pos + n_out]
        pos += n_out
        for no in p_out:
            pouts.append(refs[pos:pos + no])
            pos += no
        main_scr = refs[pos:pos + n_scr]
        pos += n_scr
        if plans:
            ids = [pl.program_id(ax) for ax in range(len(grid))]
            first = functools.reduce(jnp.logical_and, [i == 0 for i in ids])
            last = functools.reduce(jnp.logical_and, [i == g - 1 for i, g in zip(ids, grid)])
            copies = [p.copies(pins[k], pouts[k], refs[pos + 2 * k], refs[pos + 2 * k + 1]) for k, p in enumerate(plans)]

            @pl.when(first)
            def _():
                for sends, _ in copies:
                    for cp in sends:
                        cp.start()

        kern(*refs[:n_in], *main_out, *main_scr)
        if plans:
            @pl.when(last)
            def _():
                for sends, recvs in copies:
                    for cp in recvs:
                        cp.wait_recv()
                    for cp in sends:
                        cp.wait_send()

    res = pl.pallas_call(body, grid=grid, in_specs=list(in_specs) + [ANY] * sum(p_in), out_specs=list(out_specs) + [ANY] * sum(p_out),
                         out_shape=list(out_shape) + [s for p in plans for s in p.out_shape],
                         scratch_shapes=list(scratch_shapes) + [s for p in plans for s in _dma_sems(p.n_sems)],
                         input_output_aliases=all_aliases, name=name, compiler_params=cparams or _cparams())(
        *operands, *[a for p in plans for a in p.operands])
    pos = n_out
    for p, no in zip(plans, p_out):
        p.results = list(res[pos:pos + no])
        pos += no
    return list(res[:n_out])


def stage(name, fn, grid, ins, outs):
    n_in = len(ins)

    def kern(*refs):
        res = fn(*[r[...] for r in refs[:n_in]])
        for r, v in zip(refs[n_in:], res):
            r[...] = v.astype(r.dtype)

    return pl.pallas_call(kern, grid=grid, in_specs=[s for _, s in ins], out_specs=[s for _, s in outs],
                          out_shape=[sd for sd, _ in outs], name=name, compiler_params=_cparams())(*[a for a, _ in ins])


def stage_bwd(name, fn, grid, ins, cts, diffs, plans=()):
    n_in, n_ct = len(ins), len(cts)
    didx = [i for i, d in enumerate(diffs) if d is not None]
    opts = {i: (diffs[i][3] if len(diffs[i]) > 3 else {}) for i in didx if diffs[i][0] == "row"}
    adds = [(i, opts[i]["add"]) for i in opts if "add" in opts[i]]
    intos = [(i, opts[i]["into"]) for i in opts if "into" in opts[i]]
    n_add, n_into = len(adds), len(intos)
    add_pos = {i: n_in + n_ct + k for k, (i, _) in enumerate(adds)}
    n_extra = n_in + n_ct + n_add + n_into

    def kern(*refs):
        vals = [r[...] for r in refs[:n_in]]

        def f(*dv):
            full = list(vals)
            for i, v in zip(didx, dv):
                full[i] = v
            return fn(*full)

        _, vjp = jax.vjp(f, *[vals[i].astype(F32) for i in didx])
        gs = vjp(tuple(c[...].astype(F32) for c in refs[n_in:n_in + n_ct]))
        for o_ref, i, g in zip(refs[n_extra:], didx, gs):
            if diffs[i][0] == "row":
                if i in add_pos:
                    g = g + refs[add_pos[i]][...].astype(F32)
                o_ref[...] = g.astype(o_ref.dtype)
            else:
                first = functools.reduce(jnp.logical_and, [pl.program_id(ax) == 0 for ax in diffs[i][1]])

                @pl.when(first)
                def _():
                    o_ref[...] = g

                @pl.when(jnp.logical_not(first))
                def _():
                    o_ref[...] += g

    out_shape, out_specs = [], []
    for i in didx:
        if diffs[i][0] == "row":
            out_shape.append(diffs[i][1])
            out_specs.append(diffs[i][2])
        else:
            out_shape.append(SDS(ins[i][0].shape, F32))
            out_specs.append(ins[i][1])
    aliases = {n_in + n_ct + n_add + k: didx.index(i) for k, (i, _) in enumerate(intos)}
    in_specs = [s for _, s in ins] + [s for _, s in cts] + [s for _, (_, s) in adds] + [ANY] * n_into
    operands = [a for a, _ in ins] + [a for a, _ in cts] + [a for _, (a, _) in adds] + [a for _, a in intos]
    return hosted_call(kern, name=name, grid=grid, in_specs=in_specs, out_specs=out_specs, out_shape=out_shape, operands=operands,
                       aliases=aliases, plans=plans)


def rspec(tl, w, cb=0):
    return pl.BlockSpec((tl, w), lambda i: (i, cb))


def cspec(shape):
    return pl.BlockSpec(shape, lambda i: (0,) * len(shape))


def row_fwd(name, fn, rows, tl, row_ins, consts, outs):
    ins = [(a, rspec(tl, w, cb)) for a, w, cb in row_ins] + [(a, cspec(a.shape)) for a in consts]
    return stage(name, fn, (rows // tl,), ins, [(SDS((rows, w), dt), rspec(tl, w)) for w, dt in outs])


def row_bwd(name, fn, rows, tl, row_ins, consts, cts, row_diff, const_diff, plans=()):
    ins = [(a, rspec(tl, w, cb)) for a, w, cb in row_ins] + [(a, cspec(a.shape)) for a in consts]
    diffs = []
    for (a, w, cb), d in zip(row_ins, row_diff):
        if not d:
            diffs.append(None)
            continue
        d = d if isinstance(d, dict) else {}
        opts = {}
        if "add" in d:
            opts["add"] = (d["add"][0], rspec(tl, d["add"][1], d["add"][2]))
        if d.get("into") is not None:
            opts["into"] = d["into"]
        diffs.append(("row", SDS((rows, d.get("cols", w)), d.get("dtype", F32)), rspec(tl, w, d.get("cb", 0)), opts))
    diffs += [("acc", (0,)) if d else None for d in const_diff]
    return stage_bwd(name, fn, (rows // tl,), ins, [(a, rspec(tl, w, cb)) for a, w, cb in cts], diffs, plans=plans)


MATMUL_VMEM = 28 * 2**20
ADAMW_VMEM = 16 * 2**20


class Sharded:
    def __init__(self, arr, kind, roff, rows):
        self.arr, self.kind, self.roff, self.rows, self.n = arr, kind, roff, rows, arr.shape[2]
        self.shape = (rows, 4 * self.n) if kind == "col" else (4 * rows, self.n)

    def fits(self, t0, t1):
        return self.roff % t0 == 0 and self.rows % t0 == 0 and self.n % t1 == 0

    def spec(self, t0, t1, bidx):
        assert self.fits(t0, t1), (self.kind, self.roff, self.rows, self.n, t0, t1)
        r0 = self.roff // t0
        if self.kind == "col":
            per = self.n // t1
            return pl.BlockSpec((None, t0, t1), lambda *g: (bidx(*g)[1] // per, r0 + bidx(*g)[0], bidx(*g)[1] % per))
        per = self.rows // t0
        return pl.BlockSpec((None, t0, t1), lambda *g: (bidx(*g)[0] // per, r0 + bidx(*g)[0] % per, bidx(*g)[1]))


def matmul(name, a, b, mode, out_dtype=F32, add=None, out=None, into=None, plans=()):
    if mode == "tn":
        k_dim, m = a.shape
    else:
        m, k_dim = a.shape
    n = b.shape[0] if mode == "nt" else b.shape[1]
    b_fit = b.fits if isinstance(b, Sharded) else (lambda t0, t1: True)
    o_fit = out.fits if out is not None else (lambda t0, t1: True)
    a_bytes, b_bytes = jnp.dtype(a.dtype).itemsize, jnp.dtype(b.arr.dtype if isinstance(b, Sharded) else b.dtype).itemsize
    o_bytes = jnp.dtype(out_dtype if out is None else out.arr.dtype).itemsize

    def vmem(tm, tn, tk):
        return 2 * (tm * tk * a_bytes + tk * tn * b_bytes + tm * tn * (o_bytes + (4 if add is not None else 0))) + 4 * tm * tn

    tiles = [(tm, tn, tk) for tm in (2048, 1024, 512, 256, 128) for tn in (1024, 768, 512, 384, 256, 128) for tk in (1024, 512, 384, 256, 128)
             if m % tm == 0 and n % tn == 0 and k_dim % tk == 0 and (b_fit(tn, tk) if mode == "nt" else b_fit(tk, tn)) and o_fit(tm, tn)
             and vmem(tm, tn, tk) <= MATMUL_VMEM]
    tm, tn, tk = max(tiles, key=lambda t: (t[0] * t[1] * t[2], t[0] * t[1]))
    nk = k_dim // tk
    a_spec = pl.BlockSpec((tk, tm), lambda i, j, k: (k, i)) if mode == "tn" else pl.BlockSpec((tm, tk), lambda i, j, k: (i, k))
    if isinstance(b, Sharded):
        b_spec = b.spec(tn, tk, lambda i, j, k: (j, k)) if mode == "nt" else b.spec(tk, tn, lambda i, j, k: (k, j))
        b = b.arr
    else:
        b_spec = pl.BlockSpec((tn, tk), lambda i, j, k: (j, k)) if mode == "nt" else pl.BlockSpec((tk, tn), lambda i, j, k: (k, j))
    o_spec = pl.BlockSpec((tm, tn), lambda i, j, k: (i, j))
    out_spec, out_shape = (o_spec, SDS((m, n), out_dtype)) if out is None else (out.spec(tm, tn, lambda i, j, k: (i, j)), out.arr)
    ca, cb = {"nn": (1, 0), "nt": (1, 1), "tn": (0, 0)}[mode]
    n_in = 2 + (add is not None)

    def kern(*refs):
        a_ref, b_ref = refs[0], refs[1]
        o_ref, acc = refs[-2], refs[-1]
        k = pl.program_id(2)

        @pl.when(k == 0)
        def _():
            acc[...] = jnp.zeros_like(acc)

        acc[...] += _dg(a_ref[...], b_ref[...], ca, cb)

        @pl.when(k == nk - 1)
        def _():
            r = acc[...]
            if add is not None:
                r = r + refs[2][...]
            o_ref[...] = r.astype(o_ref.dtype)

    ins, specs = [a, b], [a_spec, b_spec]
    if add is not None:
        ins.append(add)
        specs.append(o_spec)
    if into is not None:
        ins.append(into)
        specs.append(ANY)
    return hosted_call(kern, name=name, grid=(m // tm, n // tn, nk), in_specs=specs, out_specs=[out_spec], out_shape=[out_shape],
                       operands=ins, scratch_shapes=[pltpu.VMEM((tm, tn), F32)], aliases={} if into is None else {n_in: 0}, plans=plans)[0]


def _cmul(ar, ai, br, bi):
    return ar * br - ai * bi, ar * bi + ai * br


def _sub_shift(x, down):
    row = lax.broadcasted_iota(jnp.int32, x.shape, 0)
    if down:
        return jnp.where(row == 0, 0.0, pltpu.roll(x, 1, 0))
    return jnp.where(row == SEG - 1, 0.0, pltpu.roll(x, SEG - 1, 0))


def _pow_seg_len(ar, ai):
    for _ in range(int(math.log2(SEG_LEN))):
        ar, ai = _cmul(ar, ai, ar, ai)
    return ar, ai


def _scan_in_place(sr, si, a_re, a_im):
    lanes = sr.shape[1]
    ar = jnp.broadcast_to(a_re, (SEG, lanes))
    ai = jnp.broadcast_to(a_im, (SEG, lanes))
    zero = jnp.zeros((SEG, lanes), F32)

    def local(i, carry):
        rows = pl.ds(pl.multiple_of(i * SEG, SEG), SEG)
        mr, mi = _cmul(ar, ai, carry[0], carry[1])
        nr, ni = mr + sr[rows, :], mi + si[rows, :]
        sr[rows, :] = nr
        si[rows, :] = ni
        return nr, ni

    fr, fi = lax.fori_loop(0, SEG_LEN, local, (zero, zero))
    pr, pi = _pow_seg_len(ar, ai)
    ir, ii = zero, zero
    for _ in range(SEG - 1):
        mr, mi = _cmul(pr, pi, ir, ii)
        ir, ii = _sub_shift(mr + fr, True), _sub_shift(mi + fi, True)

    def carry_in(i, pw):
        rows = pl.ds(pl.multiple_of(i * SEG, SEG), SEG)
        cr, ci = _cmul(pw[0], pw[1], ir, ii)
        sr[rows, :] += cr
        si[rows, :] += ci
        return _cmul(pw[0], pw[1], ar, ai)

    lax.fori_loop(0, SEG_LEN, carry_in, (ar, ai))


S5_LANES = 8 * SP
S5_BLOCKS = SN // S5_LANES


def _s5_specs():
    u_spec = pl.BlockSpec((L, 8 * SC), lambda j: (0, j))
    s_spec = pl.BlockSpec((L, S5_LANES), lambda j: (0, j))
    wb_spec = pl.BlockSpec((S5_LANES, 8 * SC), lambda j: (j, 0))
    wc_spec = pl.BlockSpec((8 * SC, S5_LANES), lambda j: (j, 0))
    a_spec = pl.BlockSpec((1, S5_LANES), lambda j: (0, j))
    d_spec = pl.BlockSpec((1, 8 * SC), lambda j: (0, j))
    return u_spec, s_spec, wb_spec, wc_spec, a_spec, d_spec


def s5_forward(proj, wb_re, wb_im, wc_re, wc_im, a_re, a_im, d, plans=()):
    def kern(u_ref, wbr, wbi, wcr, wci, ar, ai, d_ref, sr, si, g_ref):
        u = u_ref[...]
        sr[...], si[...] = fn_s5_bu(u, wbr[...], wbi[...])
        _scan_in_place(sr, si, ar[...], ai[...])
        g_ref[...] = fn_s5_out(sr[...], si[...], u, d_ref[...], wcr[...], wci[...])[0].astype(g_ref.dtype)

    u_spec, s_spec, wb_spec, wc_spec, a_spec, d_spec = _s5_specs()
    return hosted_call(kern, name="s5_forward", grid=(S5_BLOCKS,), in_specs=[u_spec, wb_spec, wb_spec, wc_spec, wc_spec, a_spec, a_spec, d_spec],
                       out_specs=[s_spec, s_spec, u_spec], out_shape=[SDS((L, SN), F32)] * 2 + [SDS((L, PW), BF16)],
                       operands=[proj, wb_re, wb_im, wc_re, wc_im, a_re, a_im, d], plans=plans)


def _adjoint_scan_in_place(lr, li, sr, si, a_re, a_im):
    lanes = lr.shape[1]
    ar = jnp.broadcast_to(a_re, (SEG, lanes))
    ai = -jnp.broadcast_to(a_im, (SEG, lanes))
    zero = jnp.zeros((SEG, lanes), F32)

    def local(k, carry):
        i = SEG_LEN - 1 - k
        rows = pl.ds(pl.multiple_of(i * SEG, SEG), SEG)
        mr, mi = _cmul(ar, ai, carry[0], carry[1])
        nr, ni = mr + lr[rows, :], mi + li[rows, :]
        lr[rows, :] = nr
        li[rows, :] = ni
        return nr, ni

    fr, fi = lax.fori_loop(0, SEG_LEN, local, (zero, zero))
    pr, pi = _pow_seg_len(ar, ai)
    ir, ii = zero, zero
    for _ in range(SEG - 1):
        mr, mi = _cmul(pr, pi, ir, ii)
        ir, ii = _sub_shift(mr + fr, False), _sub_shift(mi + fi, False)

    def fix(rows, pw):
        cr, ci = _cmul(pw[0], pw[1], ir, ii)
        tr, ti = lr[rows, :] + cr, li[rows, :] + ci
        lr[rows, :] = tr
        li[rows, :] = ti
        return tr, ti

    def grad_a(tr, ti, spr, spi, acc):
        return acc[0] + tr * spr + ti * spi, acc[1] + ti * spr - tr * spi

    def carry_in(k, c):
        i = SEG_LEN - 1 - k
        rows = pl.ds(pl.multiple_of(i * SEG, SEG), SEG)
        prev = pl.ds(pl.multiple_of((i - 1) * SEG, SEG), SEG)
        tr, ti = fix(rows, (c[0], c[1]))
        acc = grad_a(tr, ti, sr[prev, :], si[prev, :], (c[2], c[3]))
        nr, ni = _cmul(c[0], c[1], ar, ai)
        return nr, ni, acc[0], acc[1]

    pwr, pwi, accr, acci = lax.fori_loop(0, SEG_LEN - 1, carry_in, (ar, ai, zero, zero))
    tr, ti = fix(pl.ds(0, SEG), (pwr, pwi))
    last = pl.ds((SEG_LEN - 1) * SEG, SEG)
    accr, acci = grad_a(tr, ti, _sub_shift(sr[last, :], True), _sub_shift(si[last, :], True), (accr, acci))
    return jnp.sum(accr, axis=0, keepdims=True), jnp.sum(acci, axis=0, keepdims=True)


S5_BWD_VMEM = 58 * 2**20


def s5_backward(dg, proj, s_re, s_im, wb_re, wb_im, wc_re, wc_im, a_re, a_im, d, dproj, plans=()):
    def kern(dg_ref, u_ref, sr, si, wbr, wbi, wcr, wci, ar, ai, d_ref, _, du_ref, dd_ref, dwcr, dwci, dwbr, dwbi, dar, dai, lr, li):
        u = u_ref[...]
        _, vjp_out = jax.vjp(fn_s5_out, sr[...], si[...], u, d_ref[...], wcr[...], wci[...])
        lr[...], li[...], du_out, dd_ref[...], dwcr[...], dwci[...] = vjp_out((dg_ref[...],))
        dar[...], dai[...] = _adjoint_scan_in_place(lr, li, sr, si, ar[...], ai[...])
        _, vjp_in = jax.vjp(fn_s5_bu, u, wbr[...], wbi[...])
        du_in, dwbr[...], dwbi[...] = vjp_in((lr[...], li[...]))
        du_ref[...] = (du_out + du_in).astype(du_ref.dtype)

    u_spec, s_spec, wb_spec, wc_spec, a_spec, d_spec = _s5_specs()
    outs = [(SDS(dproj.shape, dproj.dtype), u_spec), (SDS(d.shape, F32), d_spec), (SDS(wc_re.shape, F32), wc_spec), (SDS(wc_im.shape, F32), wc_spec),
            (SDS(wb_re.shape, F32), wb_spec), (SDS(wb_im.shape, F32), wb_spec), (SDS(a_re.shape, F32), a_spec), (SDS(a_im.shape, F32), a_spec)]
    return hosted_call(kern, name="s5_backward", grid=(S5_BLOCKS,),
                       in_specs=[u_spec, u_spec, s_spec, s_spec, wb_spec, wb_spec, wc_spec, wc_spec, a_spec, a_spec, d_spec, ANY],
                       out_specs=[sp for _, sp in outs], out_shape=[sd for sd, _ in outs], aliases={11: 0},
                       operands=[dg, proj, s_re, s_im, wb_re, wb_im, wc_re, wc_im, a_re, a_im, d, dproj],
                       scratch_shapes=[pltpu.VMEM((L, S5_LANES), F32)] * 2,
                       cparams=pltpu.CompilerParams(vmem_limit_bytes=S5_BWD_VMEM), plans=plans)


def fn_rms(x, g):
    return (rms(x, g),)


def fn_s5_disc(lre, lim, ls):
    step = jnp.exp(ls)
    e = jnp.exp(lre * step)
    a_re, a_im = e * jnp.cos(lim * step), e * jnp.sin(lim * step)
    den = lre * lre + lim * lim
    nr, ni = a_re - 1.0, a_im
    return a_re, a_im, (nr * lre + ni * lim) / den, (ni * lre - nr * lim) / den


def _group_mask(rows, cols, row_div, col_div):
    r = lax.broadcasted_iota(jnp.int32, (rows, cols), 0) // row_div % 8
    c = lax.broadcasted_iota(jnp.int32, (rows, cols), 1) // col_div
    return r == c


def fn_s5_bmat(b_re, b_im, coef_re, coef_im):
    rows = b_re.shape[0]
    mask = _group_mask(rows, 8 * SC, SP, SC)
    bb_re = coef_re * b_re - coef_im * b_im
    bb_im = coef_re * b_im + coef_im * b_re
    return jnp.where(mask, jnp.tile(bb_re, (1, 8)), 0.0), jnp.where(mask, jnp.tile(bb_im, (1, 8)), 0.0)


def fn_s5_cmat(c_re, c_im):
    rows = c_re.shape[0]
    mask = _group_mask(rows, 8 * SP, SC, SP)
    return jnp.where(mask, jnp.tile(c_re, (1, 8)), 0.0), jnp.where(mask, jnp.tile(c_im, (1, 8)), 0.0)


def fn_s5_bu(u, wb_re, wb_im):
    return mm_nt(u, wb_re), mm_nt(u, wb_im)


def fn_s5_out(sr, si, u, d, wc_re, wc_im):
    y = mm_nt(sr, wc_re) - mm_nt(si, wc_im) + d * u
    return (jax.nn.gelu(y),)


def fn_merge_glu(z, mo, gate):
    yg = z[:, :PW] * jax.nn.sigmoid(z[:, PW:])
    return (jnp.concatenate([yg, mo], axis=1) * silu(gate),)


def fn_merge(prim, mo, gate):
    return (jnp.concatenate([prim, mo], axis=1) * silu(gate),)


def fn_mem_k(kv, g):
    return (jnp.concatenate([rms(kv[:, h * XHD:(h + 1) * XHD], g) for h in range(XH)], axis=1),)


def fn_mem_attn(xq, kn, v, g):
    outs = []
    for h in range(XH):
        sl = slice(h * XHD, (h + 1) * XHD)
        p = softmax_rows(mm_nt(rms(xq[:, sl], g), kn[:, sl]) * (XHD ** -0.5))
        outs.append(mm_nn(p, v[:, sl]))
    return (jnp.concatenate(outs, axis=1),)


def _half_rms(x, g):
    lo = lax.broadcasted_iota(jnp.int32, x.shape, 1) < ROPE
    x2 = x * x
    s_lo = jnp.sum(jnp.where(lo, x2, 0.0), axis=1, keepdims=True)
    s_hi = jnp.sum(jnp.where(lo, 0.0, x2), axis=1, keepdims=True)
    return x * lax.rsqrt(jnp.where(lo, s_lo, s_hi) / ROPE + EPS) * g


def _rope(x, cos2, sin_signed):
    first = lax.broadcasted_iota(jnp.int32, x.shape, 1) % ROPE < ROPE // 2
    return x * cos2 + jnp.where(first, lane_roll(x, 128 - ROPE // 2), lane_roll(x, ROPE // 2)) * sin_signed


def fn_mla_prep(q, kv, kr, cos2, sin_signed, qnn, knn, qrn, krn):
    lo = lax.broadcasted_iota(jnp.int32, kr.shape, 1) < ROPE
    kr_pad = jnp.where(lo, _rope(_half_rms(kr, krn), cos2, sin_signed), 0.0)
    qf, kf, vs = [], [], []
    for m in range(MH // 2):
        pair = _rope(_half_rms(q[:, MH * NOPE + 128 * m:MH * NOPE + 128 * (m + 1)], qrn), cos2, sin_signed)
        for h, rope_h in ((2 * m, pair), (2 * m + 1, lane_roll(pair, ROPE))):
            qf.append(jnp.concatenate([rms(q[:, NOPE * h:NOPE * (h + 1)], qnn), jnp.where(lo, rope_h, 0.0)], axis=1))
    for h in range(MH):
        kf.append(jnp.concatenate([rms(kv[:, 256 * h:256 * h + NOPE], knn), kr_pad], axis=1))
        vs.append(kv[:, 256 * h + NOPE:256 * (h + 1)])
    return jnp.stack(qf), jnp.stack(kf), jnp.stack(vs)


ATT_TQ = 256


def _attn_tile(q, kf, v, q_start):
    s = mm_nt(q, kf) * ((NOPE + ROPE) ** -0.5)
    q_pos = q_start + lax.broadcasted_iota(jnp.int32, s.shape, 0)
    k_pos = lax.broadcasted_iota(jnp.int32, s.shape, 1)
    p = softmax_rows(jnp.where(k_pos <= q_pos, s, jnp.finfo(F32).min))
    return mm_nn(p, v)


def _attn_specs():
    q_spec = pl.BlockSpec((None, ATT_TQ, 256), lambda h, i: (h, i, 0))
    k_spec = pl.BlockSpec((None, L, 256), lambda h, i: (h, 0, 0))
    v_spec = pl.BlockSpec((None, L, 128), lambda h, i: (h, 0, 0))
    o_spec = pl.BlockSpec((ATT_TQ, 128), lambda h, i: (i, h))
    return q_spec, k_spec, v_spec, o_spec


def causal_attn(qf, kf, vh):
    n_tiles = L // ATT_TQ

    def kern(q_ref, k_ref, v_ref, o_ref):
        i = pl.program_id(1)
        for t in range(n_tiles):
            @pl.when(i == t)
            def _(t=t):
                keys = (t + 1) * ATT_TQ
                o_ref[...] = _attn_tile(q_ref[...], k_ref[:keys, :], v_ref[:keys, :], t * ATT_TQ)

    q_spec, k_spec, v_spec, o_spec = _attn_specs()
    return pl.pallas_call(kern, grid=(MH, n_tiles), in_specs=[q_spec, k_spec, v_spec], out_specs=o_spec,
                          out_shape=SDS((L, MH * VD), F32), name="l1_attn", compiler_params=_cparams())(qf, kf, vh)


def causal_attn_bwd(qf, kf, vh, dout):
    n_tiles = L // ATT_TQ

    def kern(q_ref, k_ref, v_ref, do_ref, dq_ref, dk_ref, dv_ref):
        i = pl.program_id(1)

        @pl.when(i == 0)
        def _():
            dk_ref[...] = jnp.zeros_like(dk_ref)
            dv_ref[...] = jnp.zeros_like(dv_ref)

        for t in range(n_tiles):
            @pl.when(i == t)
            def _(t=t):
                keys = (t + 1) * ATT_TQ
                _, vjp = jax.vjp(lambda q, k, v: _attn_tile(q, k, v, t * ATT_TQ),
                                 q_ref[...].astype(F32), k_ref[:keys, :].astype(F32), v_ref[:keys, :].astype(F32))
                dq, dk, dv = vjp(do_ref[...])
                dq_ref[...] = dq
                dk_ref[:keys, :] += dk
                dv_ref[:keys, :] += dv

    q_spec, k_spec, v_spec, o_spec = _attn_specs()
    return pl.pallas_call(kern, grid=(MH, n_tiles), in_specs=[q_spec, k_spec, v_spec, o_spec], out_specs=[q_spec, k_spec, v_spec],
                          out_shape=[SDS(qf.shape, F32), SDS(kf.shape, F32), SDS(vh.shape, F32)], name="l1_attn_bwd",
                          compiler_params=_cparams())(qf, kf, vh, dout)


def loss_and_grad(y, target, tl=256):
    def kern(y_ref, t_ref, dy_ref, loss_ref):
        d = y_ref[...] - t_ref[...]
        dy_ref[...] = d / D

        @pl.when(pl.program_id(0) == 0)
        def _():
            loss_ref[...] = jnp.zeros_like(loss_ref)

        loss_ref[...] += 0.5 * jnp.sum(jnp.sum(d * d, axis=1, keepdims=True), axis=0, keepdims=True) / D

    return pl.pallas_call(kern, grid=(L // tl,), in_specs=[rspec(tl, D), rspec(tl, D)], out_specs=[rspec(tl, D), cspec((1, 1))],
                          out_shape=[SDS((L, D), F32), SDS((1, 1), F32)], name="loss", compiler_params=_cparams())(y, target)


def adamw(name, w, g, m, v):
    rows, cols = w.shape
    block_row_bytes = 7 * 2 * 4 * max(cols, 128)
    tr = _row_tile(rows, min(2048, ADAMW_VMEM // block_row_bytes // 8 * 8), 8)

    def kern(w_ref, g_ref, m_ref, v_ref, d_ref, nm_ref, nv_ref):
        gg = g_ref[...]
        nm = ADAM_B1 * m_ref[...] + (1.0 - ADAM_B1) * gg
        nv = ADAM_B2 * v_ref[...] + (1.0 - ADAM_B2) * jnp.square(gg)
        m_hat = nm / (1.0 - ADAM_B1 ** ADAM_STEP)
        v_hat = nv / (1.0 - ADAM_B2 ** ADAM_STEP)
        d_ref[...] = -ADAM_LR * (m_hat / (jnp.sqrt(v_hat) + ADAM_EPS) + ADAM_WD * w_ref[...])
        nm_ref[...] = nm
        nv_ref[...] = nv

    spec = rspec(tr, cols)
    return pl.pallas_call(kern, grid=(rows // tr,), in_specs=[spec] * 4, out_specs=[spec] * 3,
                          out_shape=[SDS((rows, cols), F32)] * 3, name=name, compiler_params=_cparams())(w, g, m, v)


def _row_tile(rows, cap=512, unit=16):
    return max(t for t in range(unit, cap + 1, unit) if rows % t == 0)


def _place():
    x, y, c = lax.axis_index("x"), lax.axis_index("y"), lax.axis_index("c")
    return x, y, c, [(1 - x, y), (x, 1 - y), (1 - x, 1 - y)]


def _row_chunks(rows, n, dtype):
    unit = 32 // jnp.dtype(dtype).itemsize
    base, extra = divmod(rows // unit, n)
    out, start = [], 0
    for k in range(n):
        size = (base + (k < extra)) * unit
        if size:
            out.append((start, size))
            start += size
    assert start == rows, (rows, unit)
    return out


PIECE_BYTES = 1 << 20


def _pieces(shapes_dtypes, rows_of):
    out = []
    for b, (shape, dtype) in enumerate(shapes_dtypes):
        rows = rows_of(shape)
        n = max(1, min(4, rows * shape[-1] * jnp.dtype(dtype).itemsize // PIECE_BYTES))
        out += [(b, st, sz) for st, sz in _row_chunks(rows, n, dtype)]
    return out


def all_gather_chips(name, shards):
    nb = len(shards)
    pieces = _pieces([(s.shape, s.dtype) for s in shards], lambda shape: shape[0] // 2)
    n = len(pieces)

    def body(*refs):
        x_refs, out_refs, send_sems, recv_sems = refs[:nb], refs[nb:2 * nb], refs[2 * nb], refs[2 * nb + 1]
        x, y, c, chips = _place()
        sibling = (x, y, 1 - c)
        mine = 2 * x + y

        def copy(sem, chip, cc, k, to, from_input=False):
            b, st, sz = pieces[k]
            rows_k = pl.ds(cc * (x_refs[b].shape[0] // 2) + st, sz)
            dst = out_refs[b].at[chip, rows_k, :]
            return pltpu.make_async_remote_copy(src_ref=x_refs[b].at[rows_k, :] if from_input else dst, dst_ref=dst,
                                                send_sem=send_sems.at[sem], recv_sem=recv_sems.at[sem], device_id=to, device_id_type=MESH_ID)

        order = [(k, j, 2 * cx + cy, (cx, cy, c)) for k in range(n) for j, (cx, cy) in enumerate(chips)]
        first = [copy(j * n + k, mine, c, k, to, from_input=True) for k, j, _, to in order]
        for cp in first:
            cp.start()
        passed = []
        for k, j, chip, _ in order:
            copy(j * n + k, chip, c, k, sibling).wait_recv()
            passed.append(copy((3 + j) * n + k, chip, c, k, sibling))
            passed[-1].start()
        for k, j, chip, _ in order:
            copy((3 + j) * n + k, chip, 1 - c, k, sibling).wait_recv()
        for cp in first + passed:
            cp.wait_send()

    return pl.pallas_call(body, in_specs=[ANY] * nb, out_specs=[ANY] * nb, out_shape=[SDS((4,) + s.shape, s.dtype) for s in shards],
                          scratch_shapes=_dma_sems(6 * n), name=name)(*shards)


def plan_gather_ici(shards):
    pieces = _pieces([(s.shape, s.dtype) for s in shards], lambda shape: shape[0] // 2)
    n = len(pieces)

    def copies(x_refs, out_refs, send_sems, recv_sems):
        x, y, c, chips = _place()
        mine = 2 * x + y

        def copy(j, k, chip, to, from_input):
            b, st, sz = pieces[k]
            rows_k = pl.ds(c * (x_refs[b].shape[0] // 2) + st, sz)
            dst = out_refs[b].at[chip, rows_k, :]
            return pltpu.make_async_remote_copy(src_ref=x_refs[b].at[rows_k, :] if from_input else dst, dst_ref=dst, send_sem=send_sems.at[j * n + k],
                                                recv_sem=recv_sems.at[j * n + k], device_id=to, device_id_type=MESH_ID)

        order = [(k, j, 2 * cx + cy, (cx, cy, c)) for k in range(n) for j, (cx, cy) in enumerate(chips)]
        return [copy(j, k, mine, to, True) for k, j, _, to in order], [copy(j, k, chip, to, False) for k, j, chip, to in order]

    return Plan(shards, [SDS((4,) + s.shape, s.dtype) for s in shards], {}, 3 * n, copies)


def plan_gather_pass(gathered):
    pieces = _pieces([(g.shape[1:], g.dtype) for g in gathered], lambda shape: shape[0] // 2)
    n = len(pieces)

    def copies(_, out_refs, send_sems, recv_sems):
        x, y, c, chips = _place()

        def copy(j, k, chip, cc):
            b, st, sz = pieces[k]
            rows_k = out_refs[b].at[chip, pl.ds(cc * (out_refs[b].shape[1] // 2) + st, sz), :]
            return pltpu.make_async_remote_copy(src_ref=rows_k, dst_ref=rows_k, send_sem=send_sems.at[j * n + k], recv_sem=recv_sems.at[j * n + k],
                                                device_id=(x, y, 1 - c), device_id_type=MESH_ID)

        order = [(k, j, 2 * cx + cy) for k in range(n) for j, (cx, cy) in enumerate(chips)]
        return [copy(j, k, chip, c) for k, j, chip in order], [copy(j, k, chip, 1 - c) for k, j, chip in order]

    return Plan(gathered, [SDS(g.shape, g.dtype) for g in gathered], {i: i for i in range(len(gathered))}, 3 * n, copies)


def plan_pair_exchange(gs):
    pieces = _pieces([(g.shape, g.dtype) for g in gs], lambda shape: shape[1] // 2)

    def copies(g_refs, got_refs, send_sems, recv_sems):
        x, y, c, _ = _place()
        swaps = [pltpu.make_async_remote_copy(src_ref=g_refs[b].at[:, pl.ds((1 - c) * (g_refs[b].shape[1] // 2) + st, sz), :],
                                              dst_ref=got_refs[b].at[:, pl.ds(st, sz), :], send_sem=send_sems.at[k], recv_sem=recv_sems.at[k],
                                              device_id=(x, y, 1 - c), device_id_type=MESH_ID)
                 for k, (b, st, sz) in enumerate(pieces)]
        return swaps, swaps

    return Plan(gs, [SDS((g.shape[0], g.shape[1] // 2, g.shape[2]), g.dtype) for g in gs], {}, len(pieces), copies)


def plan_chip_scatter(ps):
    pieces = _pieces([(p.shape, p.dtype) for p in ps], lambda shape: shape[1])
    n = len(pieces)

    def copies(p_refs, q_refs, send_sems, recv_sems):
        x, y, c, chips = _place()
        mine = 2 * x + y

        def copy(j, k, src_slot, dst_slot, to):
            b, st, sz = pieces[k]
            return pltpu.make_async_remote_copy(src_ref=p_refs[b].at[src_slot, pl.ds(st, sz), :], dst_ref=q_refs[b].at[dst_slot, pl.ds(st, sz), :],
                                                send_sem=send_sems.at[j * n + k], recv_sem=recv_sems.at[j * n + k], device_id=to,
                                                device_id_type=MESH_ID)

        order = [(k, j, 2 * cx + cy, (cx, cy, c)) for k in range(n) for j, (cx, cy) in enumerate(chips)]
        return [copy(j, k, chip, mine, to) for k, j, chip, to in order], [copy(j, k, mine, chip, to) for k, j, chip, to in order]

    return Plan(ps, [SDS(p.shape, p.dtype) for p in ps], {}, 3 * n, copies)


def plan_pair_join(bufs):
    pieces = _pieces([(b.shape, b.dtype) for b in bufs], lambda shape: shape[0] // 2)

    def copies(_, out_refs, send_sems, recv_sems):
        x, y, c, _ = _place()

        def copy(k, cc):
            b, st, sz = pieces[k]
            rows_k = out_refs[b].at[pl.ds(cc * (out_refs[b].shape[0] // 2) + st, sz), :]
            return pltpu.make_async_remote_copy(src_ref=rows_k, dst_ref=rows_k, send_sem=send_sems.at[k], recv_sem=recv_sems.at[k],
                                                device_id=(x, y, 1 - c), device_id_type=MESH_ID)

        return [copy(k, c) for k in range(len(pieces))], [copy(k, 1 - c) for k in range(len(pieces))]

    return Plan(bufs, [SDS(b.shape, b.dtype) for b in bufs], {i: i for i in range(len(bufs))}, len(pieces), copies)


def run_plan(name, plan):
    hosted_call(lambda: None, name=name, grid=(1,), in_specs=[], out_specs=[], out_shape=[], operands=[], plans=[plan])
    return plan.results


def pair_add(name, g, got, place):
    slots, rows, cols = g.shape
    half = rows // 2
    tr = _row_tile(half)
    nb = half // tr

    def kern(_, g_ref, t_ref, o_ref):
        o_ref[...] = (g_ref[...].astype(F32) + t_ref[...].astype(F32)).astype(o_ref.dtype)

    blk = pl.BlockSpec((None, tr, cols), lambda s, i, p: (s, i, 0))
    grid_spec = pltpu.PrefetchScalarGridSpec(
        num_scalar_prefetch=1, grid=(slots, nb),
        in_specs=[pl.BlockSpec((None, tr, cols), lambda s, i, p: (s, p[1] * nb + i, 0)), blk], out_specs=blk)
    return pl.pallas_call(kern, grid_spec=grid_spec, out_shape=SDS((slots, half, cols), g.dtype), name=name,
                          compiler_params=_cparams())(place, g, got)


def chip_add(name, p, q, place):
    slots, half, cols = p.shape
    tr = _row_tile(half)
    nb = half // tr

    def kern(_, p_ref, q1, q2, q3, o_ref):
        o_ref[...] = p_ref[...].astype(F32) + q1[...].astype(F32) + q2[...].astype(F32) + q3[...].astype(F32)

    def slot(k):
        return pl.BlockSpec((None, tr, cols), lambda i, pr: ((pr[0] + k) % slots, i, 0))

    grid_spec = pltpu.PrefetchScalarGridSpec(
        num_scalar_prefetch=1, grid=(nb,), in_specs=[slot(0), slot(1), slot(2), slot(3)],
        out_specs=pl.BlockSpec((tr, cols), lambda i, pr: (pr[1] * nb + i, 0)))
    return pl.pallas_call(kern, grid_spec=grid_spec, out_shape=SDS((2 * half, cols), F32), name=name,
                          compiler_params=_cparams())(place, p, q, q, q)


def pair_adds(tag, gs, gots, place):
    return [pair_add(f"{tag}_pair_add_{i}", g, got, place) for i, (g, got) in enumerate(zip(gs, gots))]


def chip_adds(tag, pairs, qs, place):
    return [chip_add(f"{tag}_chip_add_{i}", p, q, place) for i, (p, q) in enumerate(zip(pairs, qs))]


def reduce_scatter_chips(tag, gs, place):
    pairs = pair_adds(tag, gs, run_plan(tag + "_pair_exchange", plan_pair_exchange(gs)), place)
    return run_plan(tag + "_pair_join", plan_pair_join(chip_adds(tag, pairs, run_plan(tag + "_chip_scatter", plan_chip_scatter(pairs)), place)))


BIG = [("w_out", (2, 512, 1024)), ("w_mem_kv", (2, 256, 1024)), ("s5_w_in", (1, 1024, 1024)), ("s5_w_glu", (1, 1536, 768)),
       ("mla_w_in", (1, 1024, 848)), ("mla_w_uq", (1, 512, 576)), ("mla_w_ukv", (1, 256, 768))]
SHARDED_SMALL = [("mla_q_lora_norm", (1, 128)), ("mla_kv_lora_norm", (1, 64))]
SMALL = [("ln_gain", (2, 1024)), ("mem_norm", (2, 1024)), ("xq_norm", (2, 128)), ("xk_norm", (2, 128)),
         ("s5_lambda_re", (1, 96, 64)), ("s5_lambda_im", (1, 96, 64)), ("s5_log_step", (1, 96)),
         ("s5_b_re", (1, 96, 64, 16)), ("s5_b_im", (1, 96, 64, 16)), ("s5_c_re", (1, 96, 16, 64)), ("s5_c_im", (1, 96, 16, 64)),
         ("s5_d", (1, 1536)), ("mla_q_nope_norm", (1, 128)), ("mla_k_nope_norm", (1, 128)), ("mla_q_rope_norm", (1, 64)),
         ("mla_k_rope_norm", (1, 64))]
WEIGHT_ORDER = ["ln_gain", "w_out", "mem_norm", "w_mem_kv", "xq_norm", "xk_norm", "s5_w_in", "s5_lambda_re", "s5_lambda_im",
                "s5_log_step", "s5_b_re", "s5_b_im", "s5_c_re", "s5_c_im", "s5_d", "s5_w_glu", "mla_w_in", "mla_q_lora_norm",
                "mla_kv_lora_norm", "mla_w_uq", "mla_w_ukv", "mla_q_nope_norm", "mla_k_nope_norm", "mla_q_rope_norm", "mla_k_rope_norm"]
SMALL_FULL = SMALL + [(n, (1, 4 * s[1])) for n, s in SHARDED_SMALL]
N_SMALL = sum(math.prod(s) for _, s in SMALL_FULL)
SMALL_ROWS, SMALL_LANES = 128, 1024

WIDE0_IN, WIDE0_OUT, WIDE0_MKV = 0, 1024, 1536
PAIR_OUT, PAIR_MKV, PAIR_ROWS = 0, 512, 768


def stack_shards(w, dtype):
    wide0 = jnp.concatenate([w["s5_w_in"][0], w["w_out"][0], w["w_mem_kv"][0]], axis=0)
    wide1 = jnp.concatenate([w["w_out"][1], w["w_mem_kv"][1]], axis=0)
    return ([wide0.astype(dtype), w["s5_w_glu"][0].astype(dtype)],
            [wide1.astype(dtype), w["mla_w_ukv"][0].astype(dtype), w["mla_w_in"][0].astype(dtype), w["mla_w_uq"][0].astype(dtype)])


def weight_views0(wide0, glu):
    return {"s5_w_in": Sharded(wide0, "col", WIDE0_IN, 1024), "w_out": Sharded(wide0, "row", WIDE0_OUT, 512),
            "w_mem_kv": Sharded(wide0, "row", WIDE0_MKV, 256), "s5_w_glu": Sharded(glu, "col", 0, 1536)}


def weight_views1(wide1, ukv):
    return {"w_out": Sharded(wide1, "row", PAIR_OUT, 512), "w_mem_kv": Sharded(wide1, "row", PAIR_MKV, 256), "mla_w_ukv": Sharded(ukv, "col", 0, 256)}


def grad_views():
    pair = SDS((4, PAIR_ROWS, 1024), BF16)
    return {"w_out": Sharded(pair, "row", PAIR_OUT, 512), "w_mem_kv": Sharded(pair, "row", PAIR_MKV, 256),
            "s5_w_in": Sharded(SDS((4, 1024, 1024), BF16), "col", 0, 1024), "s5_w_glu": Sharded(SDS((4, 1536, 768), BF16), "col", 0, 1536),
            "mla_w_ukv": Sharded(SDS((4, 256, 768), BF16), "col", 0, 256)}


def cols_to_shards(full):
    return full.reshape(full.shape[0], 4, full.shape[1] // 4).transpose(1, 0, 2)


def shards_to_cols(arr):
    return arr.transpose(1, 0, 2).reshape(arr.shape[1], 4 * arr.shape[2])


def mla_in_permute(w):
    o1, o2, o3, o4 = QL, QL + KVL, QL + KVL + ROPE, QL + KVL + ROPE + XQW
    return jnp.concatenate([w[:, o4:], w[:, :o1], w[:, o3:o4], w[:, o1:o2], w[:, o2:o3],
                            jnp.zeros((w.shape[0], MLA_IN_P - MLA_IN), w.dtype)], axis=1)


def mla_in_unpermute(d):
    return jnp.concatenate([d[:, 2048:2560], d[:, 3072:3328], d[:, 3328:3392], d[:, 2560:3072], d[:, :2048]], axis=1)


def uq_permute(w):
    w3 = w.reshape(w.shape[0], MH, NOPE + ROPE)
    return jnp.concatenate([w3[:, :, :NOPE].reshape(w.shape[0], MH * NOPE), w3[:, :, NOPE:].reshape(w.shape[0], MH * ROPE)], axis=1)


def uq_unpermute(d):
    dn = d[:, :MH * NOPE].reshape(d.shape[0], MH, NOPE)
    dr = d[:, MH * NOPE:].reshape(d.shape[0], MH, ROPE)
    return jnp.concatenate([dn, dr], axis=2).reshape(d.shape[0], MH * (NOPE + ROPE))


def time_permute(a):
    return a.reshape(SEG, SEG_LEN, a.shape[-1]).transpose(1, 0, 2).reshape(L, a.shape[-1])


def time_unpermute(a):
    return a.reshape(SEG_LEN, SEG, a.shape[-1]).transpose(1, 0, 2).reshape(L, a.shape[-1])


def mem_branch_fwd(tag, mem, mem_norm, w_mem_kv, xk_norm):
    mn = row_fwd(tag + "_mem_rms", fn_rms, ML, ML, [(mem, D, 0)], [mem_norm], [(D, BF16)])[0]
    kv = matmul(tag + "_mem_kv", mn, w_mem_kv, "nn")
    kn = row_fwd(tag + "_mem_knorm", fn_mem_k, ML, ML, [(kv, XQW, 0)], [xk_norm], [(XQW, F32)])[0]
    return mn, kv, kn


def mem_branch_bwd(tag, mem, mem_norm, w_mem_kv, xk_norm, mn, kv, dkn, dv, g_view, g_wide):
    dk, dxk = row_bwd(tag + "_mem_knorm_bwd", fn_mem_k, ML, ML, [(kv, XQW, 0)], [xk_norm], [(dkn, XQW, 0)], [True], [True])
    dkv = jnp.concatenate([dk, dv], axis=1)
    dmn = matmul(tag + "_mem_kv_dx", dkv, w_mem_kv, "nt")
    g_wide = matmul(tag + "_mem_kv_dw", mn, dkv, "tn", out=g_view, into=g_wide)
    dmem_norm = row_bwd(tag + "_mem_rms_bwd", fn_rms, ML, ML, [(mem, D, 0)], [mem_norm], [(dmn, D, 0)], [False], [True])[0]
    return g_wide, dmem_norm, dxk


def mem_attn_fwd(tag, proj, cb, kn, kv, xq_norm):
    return row_fwd(tag + "_mem_attn", fn_mem_attn, L, 256, [(proj, XQW, cb)], [kn, kv[:, XQW:], xq_norm], [(XQW, F32)])[0]


def mem_attn_bwd(tag, proj, cb, kn, kv, xq_norm, dmo, dproj):
    place = {"cols": proj.shape[1], "cb": cb, "into": dproj, "dtype": dproj.dtype}
    return row_bwd(tag + "_mem_attn_bwd", fn_mem_attn, L, 256, [(proj, XQW, cb)], [kn, kv[:, XQW:], xq_norm], [(dmo, XQW, 0)],
                   [place], [True, True, True])


def device_step(x, mem, positions, target, small, env, hooks=None):
    hooks = hooks or {}

    def plans_for(name):
        return hooks[("plans", name)](env) if ("plans", name) in hooks else ()

    def after(name):
        if ("after", name) in hooks:
            hooks[("after", name)](env)

    g = {}
    w0 = weight_views0(env["wide0"], env["glu"])
    gw = grad_views()
    ln, mem_norm, xq_norm, xk_norm = small["ln_gain"], small["mem_norm"], small["xq_norm"], small["xk_norm"]

    lre, lim = small["s5_lambda_re"][0], small["s5_lambda_im"][0]
    ls = small["s5_log_step"].reshape(SG, 1)
    one = pl.BlockSpec((SG, SP), lambda i: (0, 0))
    col = pl.BlockSpec((SG, 1), lambda i: (0, 0))
    disc_ins = [(lre, one), (lim, one), (ls, col)]
    a_re, a_im, coef_re, coef_im = stage("s5_disc", fn_s5_disc, (1,), disc_ins, [(SDS((SG, SP), F32), one)] * 4)
    b_re, b_im = small["s5_b_re"].reshape(SN, SC), small["s5_b_im"].reshape(SN, SC)
    c_re, c_im = small["s5_c_re"].reshape(PW, SP), small["s5_c_im"].reshape(PW, SP)
    bmat_rows = [(b_re, SC, 0), (b_im, SC, 0), (coef_re.reshape(SN, 1), 1, 0), (coef_im.reshape(SN, 1), 1, 0)]
    wb_re, wb_im = row_fwd("s5_bmat", fn_s5_bmat, SN, 512, bmat_rows, [], [(128, F32)] * 2)
    cmat_rows = [(c_re, SP, 0), (c_im, SP, 0)]
    wc_re, wc_im = row_fwd("s5_cmat", fn_s5_cmat, PW, 128, cmat_rows, [], [(512, F32)] * 2)
    a_re_v, a_im_v = a_re.reshape(1, SN), a_im.reshape(1, SN)
    s5_d = small["s5_d"]

    xp = time_permute(x)
    h0 = row_fwd("l0_rms", fn_rms, L, 256, [(xp, D, 0)], [ln[0:1]], [(D, BF16)])[0]
    proj0 = matmul("l0_in", h0, w0["s5_w_in"], "nn")
    s_re, s_im, g0 = s5_forward(proj0, wb_re, wb_im, wc_re, wc_im, a_re_v, a_im_v, s5_d, plans=plans_for("s5_forward"))
    z0 = matmul("l0_glu", g0, w0["s5_w_glu"], "nn", plans=plans_for("l0_glu"))
    after("l0_glu")
    mn0, kv0, kn0 = mem_branch_fwd("l0", mem, mem_norm[0:1], w0["w_mem_kv"], xk_norm[0:1])
    mo0 = mem_attn_fwd("l0", proj0, 3, kn0, kv0, xq_norm[0:1])
    o0 = row_fwd("l0_merge", fn_merge_glu, L, 256, [(z0, 2 * PW, 0), (mo0, XQW, 0), (proj0, BW, 1)], [], [(BW, BF16)])[0]
    x1p = matmul("l0_out", o0, w0["w_out"], "nn", add=xp)
    x1 = time_unpermute(x1p)

    w1 = weight_views1(env["wide1"], env["ukv"])
    w_in1, w_uq = env["w_in1"], env["w_uq"]
    h1 = row_fwd("l1_rms", fn_rms, L, 256, [(x1, D, 0)], [ln[1:2]], [(D, BF16)])[0]
    proj1 = matmul("l1_in", h1, w_in1, "nn")
    qln, kvln = env["q_lora_norm"].reshape(1, QL), env["kv_lora_norm"].reshape(1, KVL)
    cqn = row_fwd("l1_q_lora_rms", fn_rms, L, 256, [(proj1, QL, 4)], [qln], [(QL, BF16)])[0]
    ckvn = row_fwd("l1_kv_lora_rms", fn_rms, L, 256, [(proj1, KVL, 12)], [kvln], [(KVL, BF16)])[0]
    q = matmul("l1_uq", cqn, w_uq, "nn")
    kv = matmul("l1_ukv", ckvn, w1["mla_w_ukv"], "nn")
    inv_freq = ROPE_THETA ** (-jnp.arange(ROPE // 2, dtype=F32) / (ROPE // 2))
    ang = positions.astype(F32)[:, None] * inv_freq
    cos2 = jnp.tile(jnp.cos(ang), (1, 4))
    sin_signed = jnp.tile(jnp.concatenate([-jnp.sin(ang), jnp.sin(ang)], axis=1), (1, 2))
    qnn, knn = small["mla_q_nope_norm"], small["mla_k_nope_norm"]
    qrn, krn = jnp.tile(small["mla_q_rope_norm"], (1, 2)), jnp.tile(small["mla_k_rope_norm"], (1, 2))
    tp = 256
    prep_ins = [(q, rspec(tp, MH * (NOPE + ROPE))), (kv, rspec(tp, MH * 256)), (proj1, rspec(tp, 128, 26)),
                (cos2, rspec(tp, 128)), (sin_signed, rspec(tp, 128))] + [(a, cspec((1, 128))) for a in (qnn, knn, qrn, krn)]
    hq_spec = pl.BlockSpec((MH, tp, 256), lambda i: (0, i, 0))
    hv_spec = pl.BlockSpec((MH, tp, 128), lambda i: (0, i, 0))
    qf, kf, vh = stage("l1_mla_prep", fn_mla_prep, (L // tp,), prep_ins,
                       [(SDS((MH, L, 256), BF16), hq_spec), (SDS((MH, L, 256), BF16), hq_spec), (SDS((MH, L, 128), BF16), hv_spec)])
    attn = causal_attn(qf, kf, vh)
    mn1, kv1, kn1 = mem_branch_fwd("l1", mem, mem_norm[1:2], w1["w_mem_kv"], xk_norm[1:2])
    mo1 = mem_attn_fwd("l1", proj1, 5, kn1, kv1, xq_norm[1:2])
    o1 = row_fwd("l1_merge", fn_merge, L, 256, [(attn, PW, 0), (mo1, XQW, 0), (proj1, BW, 0)], [], [(BW, BF16)])[0]
    x2 = matmul("l1_out", o1, w1["w_out"], "nn", add=x1)
    dx2, loss = loss_and_grad(x2, target)

    do1 = matmul("l1_out_dx", dx2, w1["w_out"], "nt")
    g_pair1 = matmul("l1_out_dw", o1, dx2, "tn", out=gw["w_out"])
    dattn, dmo1, dproj1 = row_bwd("l1_merge_bwd", fn_merge, L, 256, [(attn, PW, 0), (mo1, XQW, 0), (proj1, BW, 0)], [],
                                  [(do1, BW, 0)], [True, True, {"cols": MLA_IN_P, "cb": 0, "dtype": BF16}], [])
    dproj1, dkn1, dv1, dxqn1 = mem_attn_bwd("l1", proj1, 5, kn1, kv1, xq_norm[1:2], dmo1, dproj1)
    env["g_pair1"], dmem_norm1, dxk1 = mem_branch_bwd("l1", mem, mem_norm[1:2], w1["w_mem_kv"], xk_norm[1:2], mn1, kv1, dkn1, dv1,
                                                      gw["w_mem_kv"], g_pair1)
    dqf, dkf, dvh = causal_attn_bwd(qf, kf, vh, dattn)
    prep_diffs = [("row", SDS((L, MH * (NOPE + ROPE)), BF16), rspec(tp, MH * (NOPE + ROPE))), ("row", SDS((L, MH * 256), BF16), rspec(tp, MH * 256)),
                  ("row", SDS((L, MLA_IN_P), BF16), rspec(tp, 128, 26), {"into": dproj1}), None, None] + [("acc", (0,))] * 4
    dq, dkv, dproj1, dqnn, dknn, dqrn, dkrn = stage_bwd("l1_mla_prep_bwd", fn_mla_prep, (L // tp,), prep_ins,
                                                        [(dqf, hq_spec), (dkf, hq_spec), (dvh, hv_spec)], prep_diffs)
    dcqn = matmul("l1_uq_dx", dq, w_uq, "nt")
    env["dw_uq"] = matmul("l1_uq_dw", cqn, dq, "tn")
    dckvn = matmul("l1_ukv_dx", dkv, w1["mla_w_ukv"], "nt")
    env["g_ukv"] = matmul("l1_ukv_dw", ckvn, dkv, "tn", out=gw["mla_w_ukv"])
    dproj1, dqln = row_bwd("l1_q_lora_rms_bwd", fn_rms, L, 256, [(proj1, QL, 4)], [qln], [(dcqn, QL, 0)],
                           [{"cols": MLA_IN_P, "cb": 4, "into": dproj1, "dtype": BF16}], [True])
    dproj1, dkvln = row_bwd("l1_kv_lora_rms_bwd", fn_rms, L, 256, [(proj1, KVL, 12)], [kvln], [(dckvn, KVL, 0)],
                            [{"cols": MLA_IN_P, "cb": 12, "into": dproj1, "dtype": BF16}], [True])
    dh1 = matmul("l1_in_dx", dproj1, w_in1, "nt")
    env["dw_in1"] = matmul("l1_in_dw", h1, dproj1, "tn")
    dx1, dln1 = row_bwd("l1_rms_bwd", fn_rms, L, 256, [(x1, D, 0)], [ln[1:2]], [(dh1, D, 0)], [{"add": (dx2, D, 0)}], [True],
                        plans=plans_for("l1_rms_bwd"))
    dx1p = time_permute(dx1)

    do0 = matmul("l0_out_dx", dx1p, w0["w_out"], "nt", plans=plans_for("l0_out_dx"))
    g_pair0 = matmul("l0_out_dw", o0, dx1p, "tn", out=gw["w_out"])
    dz0, dmo0, dproj0 = row_bwd("l0_merge_bwd", fn_merge_glu, L, 256, [(z0, 2 * PW, 0), (mo0, XQW, 0), (proj0, BW, 1)], [],
                                [(do0, BW, 0)], [{"dtype": BF16}, True, {"cols": 2 * BW, "cb": 1, "dtype": BF16}], [])
    dproj0, dkn0, dv0, dxqn0 = mem_attn_bwd("l0", proj0, 3, kn0, kv0, xq_norm[0:1], dmo0, dproj0)
    env["g_pair0"], dmem_norm0, dxk0 = mem_branch_bwd("l0", mem, mem_norm[0:1], w0["w_mem_kv"], xk_norm[0:1], mn0, kv0, dkn0, dv0,
                                                      gw["w_mem_kv"], g_pair0)
    env["g_glu"] = matmul("l0_glu_dw", g0, dz0, "tn", out=gw["s5_w_glu"], plans=plans_for("l0_glu_dw"))
    dg0 = matmul("l0_glu_dx", dz0, w0["s5_w_glu"], "nt", plans=plans_for("l0_glu_dx"))
    dproj0, dd, dwc_re, dwc_im, dwb_re, dwb_im, da_re, da_im = s5_backward(dg0, proj0, s_re, s_im, wb_re, wb_im, wc_re, wc_im,
                                                                           a_re_v, a_im_v, s5_d, dproj0, plans=plans_for("s5_backward"))
    env["g_in0"] = matmul("l0_in_dw", h0, dproj0, "tn", out=gw["s5_w_in"])
    dh0 = matmul("l0_in_dx", dproj0, w0["s5_w_in"], "nt", plans=plans_for("l0_in_dx"))
    dxp, dln0 = row_bwd("l0_rms_bwd", fn_rms, L, 256, [(xp, D, 0)], [ln[0:1]], [(dh0, D, 0)], [{"add": (dx1p, D, 0)}], [True])
    grad_x = time_unpermute(dxp)

    db_re, db_im, dcoef_re, dcoef_im = row_bwd("s5_bmat_bwd", fn_s5_bmat, SN, 512, bmat_rows, [], [(dwb_re, 128, 0), (dwb_im, 128, 0)],
                                               [True] * 4, [], plans=plans_for("s5_bmat_bwd"))
    dc_re, dc_im = row_bwd("s5_cmat_bwd", fn_s5_cmat, PW, 128, cmat_rows, [], [(dwc_re, 512, 0), (dwc_im, 512, 0)], [True] * 2, [],
                           plans=plans_for("s5_cmat_bwd"))
    disc_cts = [(da_re.reshape(SG, SP), one), (da_im.reshape(SG, SP), one), (dcoef_re.reshape(SG, SP), one), (dcoef_im.reshape(SG, SP), one)]
    dlre, dlim, dls = stage_bwd("s5_disc_bwd", fn_s5_disc, (1,), disc_ins, disc_cts, [("acc", (0,))] * 3)

    g["ln_gain"] = jnp.concatenate([dln0, dln1], axis=0)
    g["mem_norm"] = jnp.concatenate([dmem_norm0, dmem_norm1], axis=0)
    g["xq_norm"] = jnp.concatenate([dxqn0, dxqn1], axis=0)
    g["xk_norm"] = jnp.concatenate([dxk0, dxk1], axis=0)
    g["s5_lambda_re"], g["s5_lambda_im"], g["s5_log_step"] = dlre, dlim, dls
    g["s5_b_re"], g["s5_b_im"], g["s5_c_re"], g["s5_c_im"] = db_re, db_im, dc_re, dc_im
    g["s5_d"] = dd
    g["mla_q_lora_norm"], g["mla_kv_lora_norm"] = dqln, dkvln
    g["mla_q_nope_norm"], g["mla_k_nope_norm"] = dqnn, dknn
    g["mla_q_rope_norm"] = dqrn[:, :ROPE] + dqrn[:, ROPE:]
    g["mla_k_rope_norm"] = dkrn[:, :ROPE] + dkrn[:, ROPE:]
    return loss, grad_x, g


def kernel(x, mem, positions, ln_gain, w_out, mem_norm, w_mem_kv, xq_norm, xk_norm, s5_w_in, s5_lambda_re, s5_lambda_im, s5_log_step, s5_b_re, s5_b_im, s5_c_re, s5_c_im, s5_d, s5_w_glu, mla_w_in, mla_q_lora_norm, mla_kv_lora_norm, mla_w_uq, mla_w_ukv, mla_q_nope_norm, mla_k_nope_norm, mla_q_rope_norm, mla_k_rope_norm, loss_target, m_ln_gain, m_w_out, m_mem_norm, m_w_mem_kv, m_xq_norm, m_xk_norm, m_s5_w_in, m_s5_lambda_re, m_s5_lambda_im, m_s5_log_step, m_s5_b_re, m_s5_b_im, m_s5_c_re, m_s5_c_im, m_s5_d, m_s5_w_glu, m_mla_w_in, m_mla_q_lora_norm, m_mla_kv_lora_norm, m_mla_w_uq, m_mla_w_ukv, m_mla_q_nope_norm, m_mla_k_nope_norm, m_mla_q_rope_norm, m_mla_k_rope_norm, v_ln_gain, v_w_out, v_mem_norm, v_w_mem_kv, v_xq_norm, v_xk_norm, v_s5_w_in, v_s5_lambda_re, v_s5_lambda_im, v_s5_log_step, v_s5_b_re, v_s5_b_im, v_s5_c_re, v_s5_c_im, v_s5_d, v_s5_w_glu, v_mla_w_in, v_mla_q_lora_norm, v_mla_kv_lora_norm, v_mla_w_uq, v_mla_w_ukv, v_mla_q_nope_norm, v_mla_k_nope_norm, v_mla_q_rope_norm, v_mla_k_rope_norm):
    args = dict(locals())
    wts = {n: args[n] for n in WEIGHT_ORDER}
    mom = {n: args["m_" + n] for n in WEIGHT_ORDER}
    var = {n: args["v_" + n] for n in WEIGHT_ORDER}

    chip = 2 * lax.axis_index("x") + lax.axis_index("y")
    place = jnp.stack([chip, lax.axis_index("c")]).astype(jnp.int32)

    def own_slot(gathered, shards):
        return [lax.dynamic_update_slice(g, s[None], (chip, 0, 0)) for g, s in zip(gathered, shards)]

    shards0, shards1 = stack_shards(wts, BF16)
    shards1.append(jnp.concatenate([mla_q_lora_norm, jnp.pad(mla_kv_lora_norm, ((0, 0), (0, 64))), jnp.zeros((14, 128), F32)], axis=0))
    env = dict(zip(("wide0", "glu"), own_slot(all_gather_chips("gather_weights0", shards0), shards0)))
    over_ici = plan_gather_ici(shards1)
    hooks = {("plans", "s5_forward"): lambda env: [over_ici]}
    passed_on = []

    def pass_on(env):
        passed_on.append(plan_gather_pass(over_ici.results))
        return passed_on

    def layer1_weights(env):
        wide1, ukv, in1, uq, norms = own_slot(passed_on[0].results, shards1)
        env.update(wide1=wide1, ukv=ukv, w_in1=mla_in_permute(shards_to_cols(in1)), w_uq=uq_permute(shards_to_cols(uq)),
                   q_lora_norm=norms[:, 0, :], kv_lora_norm=norms[:, 1, :64])

    hooks["plans", "l0_glu"], hooks["after", "l0_glu"] = pass_on, layer1_weights

    rs = {}

    def swap(k, gs):
        rs[k, "g"], rs[k, "swap"] = gs, plan_pair_exchange(gs)
        return rs[k, "swap"]

    def scatter(k, part=slice(None)):
        if (k, "pairs") not in rs:
            rs[k, "pairs"], rs[k, "scatter"] = pair_adds(f"rs{k}", rs[k, "g"], rs[k, "swap"].results, place), []
        rs[k, "scatter"].append(plan_chip_scatter(rs[k, "pairs"][part]))
        return rs[k, "scatter"][-1]

    def join(k):
        rs[k, "join"] = plan_pair_join(chip_adds(f"rs{k}", rs[k, "pairs"], [q for p in rs[k, "scatter"] for q in p.results], place))
        return rs[k, "join"]

    hooks["plans", "l1_rms_bwd"] = lambda env: [swap(1, [env["g_pair1"], env["g_ukv"], cols_to_shards(mla_in_unpermute(env["dw_in1"])).astype(BF16),
                                                          cols_to_shards(uq_unpermute(env["dw_uq"])).astype(BF16)])]
    hooks["plans", "l0_glu_dw"] = lambda env: [scatter(1, slice(0, 2))]
    hooks["plans", "l0_glu_dx"] = lambda env: [scatter(1, slice(2, 4)), swap(0, [env["g_pair0"], env["g_glu"]])]
    hooks["plans", "s5_backward"] = lambda env: [scatter(0), join(1)]
    hooks["plans", "l0_in_dx"] = lambda env: [swap(2, [env["g_in0"]]), join(0)]
    hooks["plans", "s5_bmat_bwd"] = lambda env: [scatter(2)]
    hooks["plans", "s5_cmat_bwd"] = lambda env: [join(2)]

    small = {n: wts[n] for n, _ in SMALL}
    loss, grad_x, g = device_step(x[0], mem[0], positions[0], loss_target[0], small, env, hooks)
    loss = lax.psum(loss[0, 0], MESH_AXES)
    (r_pair1, r_ukv, r_in1, r_uq), (r_pair0, r_glu), (r_in0,) = (rs[k, "join"].results for k in (1, 0, 2))

    small_flat = jnp.concatenate([g[n].reshape(-1) for n, _ in SMALL_FULL])
    g_small = jnp.pad(small_flat, (0, 4 * SMALL_ROWS * SMALL_LANES - N_SMALL)).astype(BF16).reshape(4, SMALL_ROWS, SMALL_LANES)
    r_small = reduce_scatter_chips("rs3", [g_small], place)[0]
    small_all = own_slot(all_gather_chips("gather_small_grads", [r_small]), [r_small])[0].reshape(-1)[:N_SMALL]

    grads = {"w_out": jnp.stack([r_pair0[:PAIR_MKV], r_pair1[:PAIR_MKV]]), "w_mem_kv": jnp.stack([r_pair0[PAIR_MKV:], r_pair1[PAIR_MKV:]]),
             "s5_w_in": r_in0[None], "s5_w_glu": r_glu[None], "mla_w_ukv": r_ukv[None], "mla_w_in": r_in1[None], "mla_w_uq": r_uq[None]}
    off = 0
    for n, s in SMALL_FULL:
        grads[n] = small_all[off:off + math.prod(s)].reshape(s)
        off += math.prod(s)
    for n, s in SHARDED_SMALL:
        grads[n] = lax.dynamic_slice(grads[n], (0, chip * s[1]), s)

    delta, new_m, new_v = {}, {}, {}
    own_layout = [(n, (s[0] * s[1], s[2])) for n, s in BIG] + [(n, (s[1] * s[2], s[3])) for n, s in SMALL if len(s) == 4]
    for n, two_d in own_layout:
        res = adamw("adamw_" + n, wts[n].reshape(two_d), grads[n].reshape(two_d), mom[n].reshape(two_d), var[n].reshape(two_d))
        delta[n], new_m[n], new_v[n] = (r.reshape(wts[n].shape) for r in res)
    small_names = [n for n, s in SMALL if len(s) < 4] + [n for n, _ in SHARDED_SMALL]
    n_own = sum(wts[n].size for n in small_names)
    rows_own = -(-n_own // (8 * 128)) * 8

    def pack_small(d):
        flat = jnp.concatenate([d[n].reshape(-1) for n in small_names])
        return jnp.pad(flat, (0, rows_own * 128 - n_own), constant_values=1.0).reshape(rows_own, 128)

    res = adamw("adamw_small", pack_small(wts), pack_small(grads), pack_small(mom), pack_small(var))
    off = 0
    for n in small_names:
        size = wts[n].size
        delta[n], new_m[n], new_v[n] = (r.reshape(-1)[off:off + size].reshape(wts[n].shape) for r in res)
        off += size

    return (loss, grad_x[None], *[grads[n] for n in WEIGHT_ORDER], *[delta[n] for n in WEIGHT_ORDER],
            *[new_m[n] for n in WEIGHT_ORDER], *[new_v[n] for n in WEIGHT_ORDER])
```

```python
import functools
import math

import jax
import jax.numpy as jnp
from jax import lax
from jax.experimental import pallas as pl
from jax.experimental.pallas import tpu as pltpu

F32, BF16 = jnp.float32, jnp.bfloat16
SDS = jax.ShapeDtypeStruct

D = 1024
L = 2048
ML = 256
BW = 2 * D
XQW = BW // 4
PW = BW - XQW
XH, XHD = 4, 128
SG, SC, SP = 96, 16, 64
SN = SG * SP
NOPE, ROPE, VD = 128, 64, 128
MH = 12
QL, KVL = 512, 256
EPS = 1e-6
ROPE_THETA = 10000.0
MLA_IN = QL + KVL + ROPE + XQW + BW
MLA_IN_P = 3456
ADAM_LR, ADAM_B1, ADAM_B2, ADAM_EPS, ADAM_WD, ADAM_STEP = 0.001, 0.9, 0.999, 1e-08, 0.01, 10

VMEM_LIMIT = 48 * 2**20
SEG = 8
SEG_LEN = L // SEG
MESH_AXES = ("x", "y", "c")


def _cparams():
    return pltpu.CompilerParams(vmem_limit_bytes=VMEM_LIMIT)


def _dg(a, b, ca, cb):
    return lax.dot_general(a.astype(BF16), b.astype(BF16), (((ca,), (cb,)), ((), ())), preferred_element_type=F32)


@jax.custom_vjp
def mm_nn(a, b):
    return _dg(a, b, 1, 0)


mm_nn.defvjp(lambda a, b: (_dg(a, b, 1, 0), (a, b)), lambda res, g: (_dg(g, res[1], 1, 1), _dg(res[0], g, 0, 0)))


@jax.custom_vjp
def mm_nt(a, b):
    return _dg(a, b, 1, 1)


mm_nt.defvjp(lambda a, b: (_dg(a, b, 1, 1), (a, b)), lambda res, g: (_dg(g, res[1], 1, 0), _dg(g, res[0], 0, 0)))


@functools.partial(jax.custom_vjp, nondiff_argnums=(1,))
def lane_roll(x, shift):
    return pltpu.roll(x, shift, 1)


lane_roll.defvjp(lambda x, shift: (pltpu.roll(x, shift, 1), None),
                 lambda shift, _, g: (pltpu.roll(g, (128 - shift) % 128, 1),))


def rms(x, g):
    return x * lax.rsqrt(jnp.mean(x * x, axis=-1, keepdims=True) + EPS) * g


def softmax_rows(s):
    m = lax.stop_gradient(jnp.max(s, axis=-1, keepdims=True))
    e = jnp.exp(s - m)
    return e / jnp.sum(e, axis=-1, keepdims=True)


def silu(x):
    return x * jax.nn.sigmoid(x)


ANY = pl.BlockSpec(memory_space=pl.ANY)
MESH_ID = pl.DeviceIdType.MESH


def _dma_sems(n):
    return [pltpu.SemaphoreType.DMA((n,)), pltpu.SemaphoreType.DMA((n,))]


class Plan:
    def __init__(self, operands, out_shape, aliases, n_sems, copies):
        self.operands, self.out_shape, self.aliases, self.n_sems, self.copies = list(operands), list(out_shape), aliases, n_sems, copies
        self.results = None


def hosted_call(kern, *, name, grid, in_specs, out_specs, out_shape, operands, scratch_shapes=(), aliases=None, cparams=None, plans=()):
    n_in, n_out, n_scr = len(in_specs), len(out_specs), len(scratch_shapes)
    p_in, p_out = [len(p.operands) for p in plans], [len(p.out_shape) for p in plans]
    all_aliases = dict(aliases or {})
    in_off, out_off = n_in, n_out
    for p, ni, no in zip(plans, p_in, p_out):
        all_aliases.update({in_off + i: out_off + o for i, o in p.aliases.items()})
        in_off, out_off = in_off + ni, out_off + no

    def body(*refs):
        pos, pins, pouts = n_in, [], []
        for ni in p_in:
            pins.append(refs[pos:pos + ni])
            pos += ni
        main_out = refs[pos:pos + n_out]
        pos += n_out
        for no in p_out:
            pouts.append(refs[pos:pos + no])
            pos += no
        main_scr = refs[pos:pos + n_scr]
        pos += n_scr
        if plans:
            ids = [pl.program_id(ax) for ax in range(len(grid))]
            first = functools.reduce(jnp.logical_and, [i == 0 for i in ids])
            last = functools.reduce(jnp.logical_and, [i == g - 1 for i, g in zip(ids, grid)])
            copies = [p.copies(pins[k], pouts[k], refs[pos + 2 * k], refs[pos + 2 * k + 1]) for k, p in enumerate(plans)]

            @pl.when(first)
            def _():
                for sends, _ in copies:
                    for cp in sends:
                        cp.start()

        kern(*refs[:n_in], *main_out, *main_scr)
        if plans:
            @pl.when(last)
            def _():
                for sends, recvs in copies:
                    for cp in recvs:
                        cp.wait_recv()
                    for cp in sends:
                        cp.wait_send()

    res = pl.pallas_call(body, grid=grid, in_specs=list(in_specs) + [ANY] * sum(p_in), out_specs=list(out_specs) + [ANY] * sum(p_out),
                         out_shape=list(out_shape) + [s for p in plans for s in p.out_shape],
                         scratch_shapes=list(scratch_shapes) + [s for p in plans for s in _dma_sems(p.n_sems)],
                         input_output_aliases=all_aliases, name=name, compiler_params=cparams or _cparams())(
        *operands, *[a for p in plans for a in p.operands])
    pos = n_out
    for p, no in zip(plans, p_out):
        p.results = list(res[pos:pos + no])
        pos += no
    return list(res[:n_out])


def stage(name, fn, grid, ins, outs):
    n_in = len(ins)

    def kern(*refs):
        res = fn(*[r[...] for r in refs[:n_in]])
        for r, v in zip(refs[n_in:], res):
            r[...] = v.astype(r.dtype)

    return pl.pallas_call(kern, grid=grid, in_specs=[s for _, s in ins], out_specs=[s for _, s in outs],
                          out_shape=[sd for sd, _ in outs], name=name, compiler_params=_cparams())(*[a for a, _ in ins])


def stage_bwd(name, fn, grid, ins, cts, diffs, plans=()):
    n_in, n_ct = len(ins), len(cts)
    didx = [i for i, d in enumerate(diffs) if d is not None]
    opts = {i: (diffs[i][3] if len(diffs[i]) > 3 else {}) for i in didx if diffs[i][0] == "row"}
    adds = [(i, opts[i]["add"]) for i in opts if "add" in opts[i]]
    intos = [(i, opts[i]["into"]) for i in opts if "into" in opts[i]]
    n_add, n_into = len(adds), len(intos)
    add_pos = {i: n_in + n_ct + k for k, (i, _) in enumerate(adds)}
    n_extra = n_in + n_ct + n_add + n_into

    def kern(*refs):
        vals = [r[...] for r in refs[:n_in]]

        def f(*dv):
            full = list(vals)
            for i, v in zip(didx, dv):
                full[i] = v
            return fn(*full)

        _, vjp = jax.vjp(f, *[vals[i].astype(F32) for i in didx])
        gs = vjp(tuple(c[...].astype(F32) for c in refs[n_in:n_in + n_ct]))
        for o_ref, i, g in zip(refs[n_extra:], didx, gs):
            if diffs[i][0] == "row":
                if i in add_pos:
                    g = g + refs[add_pos[i]][...].astype(F32)
                o_ref[...] = g.astype(o_ref.dtype)
            else:
                first = functools.reduce(jnp.logical_and, [pl.program_id(ax) == 0 for ax in diffs[i][1]])

                @pl.when(first)
                def _():
                    o_ref[...] = g

                @pl.when(jnp.logical_not(first))
                def _():
                    o_ref[...] += g

    out_shape, out_specs = [], []
    for i in didx:
        if diffs[i][0] == "row":
            out_shape.append(diffs[i][1])
            out_specs.append(diffs[i][2])
        else:
            out_shape.append(SDS(ins[i][0].shape, F32))
            out_specs.append(ins[i][1])
    aliases = {n_in + n_ct + n_add + k: didx.index(i) for k, (i, _) in enumerate(intos)}
    in_specs = [s for _, s in ins] + [s for _, s in cts] + [s for _, (_, s) in adds] + [ANY] * n_into
    operands = [a for a, _ in ins] + [a for a, _ in cts] + [a for _, (a, _) in adds] + [a for _, a in intos]
    return hosted_call(kern, name=name, grid=grid, in_specs=in_specs, out_specs=out_specs, out_shape=out_shape, operands=operands,
                       aliases=aliases, plans=plans)


def rspec(tl, w, cb=0):
    return pl.BlockSpec((tl, w), lambda i: (i, cb))


def cspec(shape):
    return pl.BlockSpec(shape, lambda i: (0,) * len(shape))


def row_fwd(name, fn, rows, tl, row_ins, consts, outs):
    ins = [(a, rspec(tl, w, cb)) for a, w, cb in row_ins] + [(a, cspec(a.shape)) for a in consts]
    return stage(name, fn, (rows // tl,), ins, [(SDS((rows, w), dt), rspec(tl, w)) for w, dt in outs])


def row_bwd(name, fn, rows, tl, row_ins, consts, cts, row_diff, const_diff, plans=()):
    ins = [(a, rspec(tl, w, cb)) for a, w, cb in row_ins] + [(a, cspec(a.shape)) for a in consts]
    diffs = []
    for (a, w, cb), d in zip(row_ins, row_diff):
        if not d:
            diffs.append(None)
            continue
        d = d if isinstance(d, dict) else {}
        opts = {}
        if "add" in d:
            opts["add"] = (d["add"][0], rspec(tl, d["add"][1], d["add"][2]))
        if d.get("into") is not None:
            opts["into"] = d["into"]
        diffs.append(("row", SDS((rows, d.get("cols", w)), d.get("dtype", F32)), rspec(tl, w, d.get("cb", 0)), opts))
    diffs += [("acc", (0,)) if d else None for d in const_diff]
    return stage_bwd(name, fn, (rows // tl,), ins, [(a, rspec(tl, w, cb)) for a, w, cb in cts], diffs, plans=plans)


MATMUL_VMEM = 28 * 2**20
ADAMW_VMEM = 16 * 2**20


class Sharded:
    def __init__(self, arr, kind, roff, rows):
        self.arr, self.kind, self.roff, self.rows, self.n = arr, kind, roff, rows, arr.shape[2]
        self.shape = (rows, 4 * self.n) if kind == "col" else (4 * rows, self.n)

    def fits(self, t0, t1):
        return self.roff % t0 == 0 and self.rows % t0 == 0 and self.n % t1 == 0

    def spec(self, t0, t1, bidx):
        assert self.fits(t0, t1), (self.kind, self.roff, self.rows, self.n, t0, t1)
        r0 = self.roff // t0
        if self.kind == "col":
            per = self.n // t1
            return pl.BlockSpec((None, t0, t1), lambda *g: (bidx(*g)[1] // per, r0 + bidx(*g)[0], bidx(*g)[1] % per))
        per = self.rows // t0
        return pl.BlockSpec((None, t0, t1), lambda *g: (bidx(*g)[0] // per, r0 + bidx(*g)[0] % per, bidx(*g)[1]))


def matmul(name, a, b, mode, out_dtype=F32, add=None, out=None, into=None, plans=()):
    if mode == "tn":
        k_dim, m = a.shape
    else:
        m, k_dim = a.shape
    n = b.shape[0] if mode == "nt" else b.shape[1]
    b_fit = b.fits if isinstance(b, Sharded) else (lambda t0, t1: True)
    o_fit = out.fits if out is not None else (lambda t0, t1: True)
    a_bytes, b_bytes = jnp.dtype(a.dtype).itemsize, jnp.dtype(b.arr.dtype if isinstance(b, Sharded) else b.dtype).itemsize
    o_bytes = jnp.dtype(out_dtype if out is None else out.arr.dtype).itemsize

    def vmem(tm, tn, tk):
        return 2 * (tm * tk * a_bytes + tk * tn * b_bytes + tm * tn * (o_bytes + (4 if add is not None else 0))) + 4 * tm * tn

    tiles = [(tm, tn, tk) for tm in (2048, 1024, 512, 256, 128) for tn in (1024, 768, 512, 384, 256, 128) for tk in (1024, 512, 384, 256, 128)
             if m % tm == 0 and n % tn == 0 and k_dim % tk == 0 and (b_fit(tn, tk) if mode == "nt" else b_fit(tk, tn)) and o_fit(tm, tn)
             and vmem(tm, tn, tk) <= MATMUL_VMEM]
    tm, tn, tk = max(tiles, key=lambda t: (t[0] * t[1] * t[2], t[0] * t[1]))
    nk = k_dim // tk
    a_spec = pl.BlockSpec((tk, tm), lambda i, j, k: (k, i)) if mode == "tn" else pl.BlockSpec((tm, tk), lambda i, j, k: (i, k))
    if isinstance(b, Sharded):
        b_spec = b.spec(tn, tk, lambda i, j, k: (j, k)) if mode == "nt" else b.spec(tk, tn, lambda i, j, k: (k, j))
        b = b.arr
    else:
        b_spec = pl.BlockSpec((tn, tk), lambda i, j, k: (j, k)) if mode == "nt" else pl.BlockSpec((tk, tn), lambda i, j, k: (k, j))
    o_spec = pl.BlockSpec((tm, tn), lambda i, j, k: (i, j))
    out_spec, out_shape = (o_spec, SDS((m, n), out_dtype)) if out is None else (out.spec(tm, tn, lambda i, j, k: (i, j)), out.arr)
    ca, cb = {"nn": (1, 0), "nt": (1, 1), "tn": (0, 0)}[mode]
    n_in = 2 + (add is not None)

    def kern(*refs):
        a_ref, b_ref = refs[0], refs[1]
        o_ref, acc = refs[-2], refs[-1]
        k = pl.program_id(2)

        @pl.when(k == 0)
        def _():
            acc[...] = jnp.zeros_like(acc)

        acc[...] += _dg(a_ref[...], b_ref[...], ca, cb)

        @pl.when(k == nk - 1)
        def _():
            r = acc[...]
            if add is not None:
                r = r + refs[2][...]
            o_ref[...] = r.astype(o_ref.dtype)

    ins, specs = [a, b], [a_spec, b_spec]
    if add is not None:
        ins.append(add)
        specs.append(o_spec)
    if into is not None:
        ins.append(into)
        specs.append(ANY)
    return hosted_call(kern, name=name, grid=(m // tm, n // tn, nk), in_specs=specs, out_specs=[out_spec], out_shape=[out_shape],
                       operands=ins, scratch_shapes=[pltpu.VMEM((tm, tn), F32)], aliases={} if into is None else {n_in: 0}, plans=plans)[0]


def _cmul(ar, ai, br, bi):
    return ar * br - ai * bi, ar * bi + ai * br


def _sub_shift(x, down):
    row = lax.broadcasted_iota(jnp.int32, x.shape, 0)
    if down:
        return jnp.where(row == 0, 0.0, pltpu.roll(x, 1, 0))
    return jnp.where(row == SEG - 1, 0.0, pltpu.roll(x, SEG - 1, 0))


def _pow_seg_len(ar, ai):
    for _ in range(int(math.log2(SEG_LEN))):
        ar, ai = _cmul(ar, ai, ar, ai)
    return ar, ai


def _scan_in_place(sr, si, a_re, a_im):
    lanes = sr.shape[1]
    ar = jnp.broadcast_to(a_re, (SEG, lanes))
    ai = jnp.broadcast_to(a_im, (SEG, lanes))
    zero = jnp.zeros((SEG, lanes), F32)

    def local(i, carry):
        rows = pl.ds(pl.multiple_of(i * SEG, SEG), SEG)
        mr, mi = _cmul(ar, ai, carry[0], carry[1])
        nr, ni = mr + sr[rows, :], mi + si[rows, :]
        sr[rows, :] = nr
        si[rows, :] = ni
        return nr, ni

    fr, fi = lax.fori_loop(0, SEG_LEN, local, (zero, zero))
    pr, pi = _pow_seg_len(ar, ai)
    ir, ii = zero, zero
    for _ in range(SEG - 1):
        mr, mi = _cmul(pr, pi, ir, ii)
        ir, ii = _sub_shift(mr + fr, True), _sub_shift(mi + fi, True)

    def carry_in(i, pw):
        rows = pl.ds(pl.multiple_of(i * SEG, SEG), SEG)
        cr, ci = _cmul(pw[0], pw[1], ir, ii)
        sr[rows, :] += cr
        si[rows, :] += ci
        return _cmul(pw[0], pw[1], ar, ai)

    lax.fori_loop(0, SEG_LEN, carry_in, (ar, ai))


S5_LANES = 8 * SP
S5_BLOCKS = SN // S5_LANES


def _s5_specs():
    u_spec = pl.BlockSpec((L, 8 * SC), lambda j: (0, j))
    s_spec = pl.BlockSpec((L, S5_LANES), lambda j: (0, j))
    wb_spec = pl.BlockSpec((S5_LANES, 8 * SC), lambda j: (j, 0))
    wc_spec = pl.BlockSpec((8 * SC, S5_LANES), lambda j: (j, 0))
    a_spec = pl.BlockSpec((1, S5_LANES), lambda j: (0, j))
    d_spec = pl.BlockSpec((1, 8 * SC), lambda j: (0, j))
    return u_spec, s_spec, wb_spec, wc_spec, a_spec, d_spec


def s5_forward(proj, wb_re, wb_im, wc_re, wc_im, a_re, a_im, d, plans=()):
    def kern(u_ref, wbr, wbi, wcr, wci, ar, ai, d_ref, sr, si, g_ref):
        u = u_ref[...]
        sr[...], si[...] = fn_s5_bu(u, wbr[...], wbi[...])
        _scan_in_place(sr, si, ar[...], ai[...])
        g_ref[...] = fn_s5_out(sr[...], si[...], u, d_ref[...], wcr[...], wci[...])[0].astype(g_ref.dtype)

    u_spec, s_spec, wb_spec, wc_spec, a_spec, d_spec = _s5_specs()
    return hosted_call(kern, name="s5_forward", grid=(S5_BLOCKS,), in_specs=[u_spec, wb_spec, wb_spec, wc_spec, wc_spec, a_spec, a_spec, d_spec],
                       out_specs=[s_spec, s_spec, u_spec], out_shape=[SDS((L, SN), F32)] * 2 + [SDS((L, PW), BF16)],
                       operands=[proj, wb_re, wb_im, wc_re, wc_im, a_re, a_im, d], plans=plans)


def _adjoint_scan_in_place(lr, li, sr, si, a_re, a_im):
    lanes = lr.shape[1]
    ar = jnp.broadcast_to(a_re, (SEG, lanes))
    ai = -jnp.broadcast_to(a_im, (SEG, lanes))
    zero = jnp.zeros((SEG, lanes), F32)

    def local(k, carry):
        i = SEG_LEN - 1 - k
        rows = pl.ds(pl.multiple_of(i * SEG, SEG), SEG)
        mr, mi = _cmul(ar, ai, carry[0], carry[1])
        nr, ni = mr + lr[rows, :], mi + li[rows, :]
        lr[rows, :] = nr
        li[rows, :] = ni
        return nr, ni

    fr, fi = lax.fori_loop(0, SEG_LEN, local, (zero, zero))
    pr, pi = _pow_seg_len(ar, ai)
    ir, ii = zero, zero
    for _ in range(SEG - 1):
        mr, mi = _cmul(pr, pi, ir, ii)
        ir, ii = _sub_shift(mr + fr, False), _sub_shift(mi + fi, False)

    def fix(rows, pw):
        cr, ci = _cmul(pw[0], pw[1], ir, ii)
        tr, ti = lr[rows, :] + cr, li[rows, :] + ci
        lr[rows, :] = tr
        li[rows, :] = ti
        return tr, ti

    def grad_a(tr, ti, spr, spi, acc):
        return acc[0] + tr * spr + ti * spi, acc[1] + ti * spr - tr * spi

    def carry_in(k, c):
        i = SEG_LEN - 1 - k
        rows = pl.ds(pl.multiple_of(i * SEG, SEG), SEG)
        prev = pl.ds(pl.multiple_of((i - 1) * SEG, SEG), SEG)
        tr, ti = fix(rows, (c[0], c[1]))
        acc = grad_a(tr, ti, sr[prev, :], si[prev, :], (c[2], c[3]))
        nr, ni = _cmul(c[0], c[1], ar, ai)
        return nr, ni, acc[0], acc[1]

    pwr, pwi, accr, acci = lax.fori_loop(0, SEG_LEN - 1, carry_in, (ar, ai, zero, zero))
    tr, ti = fix(pl.ds(0, SEG), (pwr, pwi))
    last = pl.ds((SEG_LEN - 1) * SEG, SEG)
    accr, acci = grad_a(tr, ti, _sub_shift(sr[last, :], True), _sub_shift(si[last, :], True), (accr, acci))
    return jnp.sum(accr, axis=0, keepdims=True), jnp.sum(acci, axis=0, keepdims=True)


S5_BWD_VMEM = 58 * 2**20


def s5_backward(dg, proj, s_re, s_im, wb_re, wb_im, wc_re, wc_im, a_re, a_im, d, dproj, plans=()):
    def kern(dg_ref, u_ref, sr, si, wbr, wbi, wcr, wci, ar, ai, d_ref, _, du_ref, dd_ref, dwcr, dwci, dwbr, dwbi, dar, dai, lr, li):
        u = u_ref[...]
        _, vjp_out = jax.vjp(fn_s5_out, sr[...], si[...], u, d_ref[...], wcr[...], wci[...])
        lr[...], li[...], du_out, dd_ref[...], dwcr[...], dwci[...] = vjp_out((dg_ref[...],))
        dar[...], dai[...] = _adjoint_scan_in_place(lr, li, sr, si, ar[...], ai[...])
        _, vjp_in = jax.vjp(fn_s5_bu, u, wbr[...], wbi[...])
        du_in, dwbr[...], dwbi[...] = vjp_in((lr[...], li[...]))
        du_ref[...] = (du_out + du_in).astype(du_ref.dtype)

    u_spec, s_spec, wb_spec, wc_spec, a_spec, d_spec = _s5_specs()
    outs = [(SDS(dproj.shape, dproj.dtype), u_spec), (SDS(d.shape, F32), d_spec), (SDS(wc_re.shape, F32), wc_spec), (SDS(wc_im.shape, F32), wc_spec),
            (SDS(wb_re.shape, F32), wb_spec), (SDS(wb_im.shape, F32), wb_spec), (SDS(a_re.shape, F32), a_spec), (SDS(a_im.shape, F32), a_spec)]
    return hosted_call(kern, name="s5_backward", grid=(S5_BLOCKS,),
                       in_specs=[u_spec, u_spec, s_spec, s_spec, wb_spec, wb_spec, wc_spec, wc_spec, a_spec, a_spec, d_spec, ANY],
                       out_specs=[sp for _, sp in outs], out_shape=[sd for sd, _ in outs], aliases={11: 0},
                       operands=[dg, proj, s_re, s_im, wb_re, wb_im, wc_re, wc_im, a_re, a_im, d, dproj],
                       scratch_shapes=[pltpu.VMEM((L, S5_LANES), F32)] * 2,
                       cparams=pltpu.CompilerParams(vmem_limit_bytes=S5_BWD_VMEM), plans=plans)


def fn_rms(x, g):
    return (rms(x, g),)


def fn_s5_disc(lre, lim, ls):
    step = jnp.exp(ls)
    e = jnp.exp(lre * step)
    a_re, a_im = e * jnp.cos(lim * step), e * jnp.sin(lim * step)
    den = lre * lre + lim * lim
    nr, ni = a_re - 1.0, a_im
    return a_re, a_im, (nr * lre + ni * lim) / den, (ni * lre - nr * lim) / den


def _group_mask(rows, cols, row_div, col_div):
    r = lax.broadcasted_iota(jnp.int32, (rows, cols), 0) // row_div % 8
    c = lax.broadcasted_iota(jnp.int32, (rows, cols), 1) // col_div
    return r == c


def fn_s5_bmat(b_re, b_im, coef_re, coef_im):
    rows = b_re.shape[0]
    mask = _group_mask(rows, 8 * SC, SP, SC)
    bb_re = coef_re * b_re - coef_im * b_im
    bb_im = coef_re * b_im + coef_im * b_re
    return jnp.where(mask, jnp.tile(bb_re, (1, 8)), 0.0), jnp.where(mask, jnp.tile(bb_im, (1, 8)), 0.0)


def fn_s5_cmat(c_re, c_im):
    rows = c_re.shape[0]
    mask = _group_mask(rows, 8 * SP, SC, SP)
    return jnp.where(mask, jnp.tile(c_re, (1, 8)), 0.0), jnp.where(mask, jnp.tile(c_im, (1, 8)), 0.0)


def fn_s5_bu(u, wb_re, wb_im):
    return mm_nt(u, wb_re), mm_nt(u, wb_im)


def fn_s5_out(sr, si, u, d, wc_re, wc_im):
    y = mm_nt(sr, wc_re) - mm_nt(si, wc_im) + d * u
    return (jax.nn.gelu(y),)


def fn_merge_glu(z, mo, gate):
    yg = z[:, :PW] * jax.nn.sigmoid(z[:, PW:])
    return (jnp.concatenate([yg, mo], axis=1) * silu(gate),)


def fn_merge(prim, mo, gate):
    return (jnp.concatenate([prim, mo], axis=1) * silu(gate),)


def fn_mem_k(kv, g):
    return (jnp.concatenate([rms(kv[:, h * XHD:(h + 1) * XHD], g) for h in range(XH)], axis=1),)


def fn_mem_attn(xq, kn, v, g):
    outs = []
    for h in range(XH):
        sl = slice(h * XHD, (h + 1) * XHD)
        p = softmax_rows(mm_nt(rms(xq[:, sl], g), kn[:, sl]) * (XHD ** -0.5))
        outs.append(mm_nn(p, v[:, sl]))
    return (jnp.concatenate(outs, axis=1),)


def _half_rms(x, g):
    lo = lax.broadcasted_iota(jnp.int32, x.shape, 1) < ROPE
    x2 = x * x
    s_lo = jnp.sum(jnp.where(lo, x2, 0.0), axis=1, keepdims=True)
    s_hi = jnp.sum(jnp.where(lo, 0.0, x2), axis=1, keepdims=True)
    return x * lax.rsqrt(jnp.where(lo, s_lo, s_hi) / ROPE + EPS) * g


def _rope(x, cos2, sin_signed):
    first = lax.broadcasted_iota(jnp.int32, x.shape, 1) % ROPE < ROPE // 2
    return x * cos2 + jnp.where(first, lane_roll(x, 128 - ROPE // 2), lane_roll(x, ROPE // 2)) * sin_signed


def fn_mla_prep(q, kv, kr, cos2, sin_signed, qnn, knn, qrn, krn):
    lo = lax.broadcasted_iota(jnp.int32, kr.shape, 1) < ROPE
    kr_pad = jnp.where(lo, _rope(_half_rms(kr, krn), cos2, sin_signed), 0.0)
    qf, kf, vs = [], [], []
    for m in range(MH // 2):
        pair = _rope(_half_rms(q[:, MH * NOPE + 128 * m:MH * NOPE + 128 * (m + 1)], qrn), cos2, sin_signed)
        for h, rope_h in ((2 * m, pair), (2 * m + 1, lane_roll(pair, ROPE))):
            qf.append(jnp.concatenate([rms(q[:, NOPE * h:NOPE * (h + 1)], qnn), jnp.where(lo, rope_h, 0.0)], axis=1))
    for h in range(MH):
        kf.append(jnp.concatenate([rms(kv[:, 256 * h:256 * h + NOPE], knn), kr_pad], axis=1))
        vs.append(kv[:, 256 * h + NOPE:256 * (h + 1)])
    return jnp.stack(qf), jnp.stack(kf), jnp.stack(vs)


ATT_TQ = 256


def _attn_tile(q, kf, v):
    tq = q.shape[0]
    scale = (NOPE + ROPE) ** -0.5
    own = mm_nt(q, kf[-tq:]) * scale
    own = jnp.where(lax.broadcasted_iota(jnp.int32, own.shape, 1) <= lax.broadcasted_iota(jnp.int32, own.shape, 0), own, jnp.finfo(F32).min)
    s = own if kf.shape[0] == tq else jnp.concatenate([mm_nt(q, kf[:-tq]) * scale, own], axis=1)
    return mm_nn(softmax_rows(s), v)


def _attn_specs():
    q_spec = pl.BlockSpec((None, ATT_TQ, 256), lambda h, i: (h, i, 0))
    k_spec = pl.BlockSpec((None, L, 256), lambda h, i: (h, 0, 0))
    v_spec = pl.BlockSpec((None, L, 128), lambda h, i: (h, 0, 0))
    o_spec = pl.BlockSpec((ATT_TQ, 128), lambda h, i: (i, h))
    return q_spec, k_spec, v_spec, o_spec


def causal_attn(qf, kf, vh):
    n_tiles = L // ATT_TQ

    def kern(q_ref, k_ref, v_ref, o_ref):
        i = pl.program_id(1)
        for t in range(n_tiles):
            @pl.when(i == t)
            def _(t=t):
                keys = (t + 1) * ATT_TQ
                o_ref[...] = _attn_tile(q_ref[...], k_ref[:keys, :], v_ref[:keys, :])

    q_spec, k_spec, v_spec, o_spec = _attn_specs()
    return pl.pallas_call(kern, grid=(MH, n_tiles), in_specs=[q_spec, k_spec, v_spec], out_specs=o_spec,
                          out_shape=SDS((L, MH * VD), F32), name="l1_attn", compiler_params=_cparams())(qf, kf, vh)


def causal_attn_bwd(qf, kf, vh, dout):
    n_tiles = L // ATT_TQ

    def kern(q_ref, k_ref, v_ref, do_ref, dq_ref, dk_ref, dv_ref):
        i = pl.program_id(1)

        @pl.when(i == 0)
        def _():
            dk_ref[...] = jnp.zeros_like(dk_ref)
            dv_ref[...] = jnp.zeros_like(dv_ref)

        for t in range(n_tiles):
            @pl.when(i == t)
            def _(t=t):
                keys = (t + 1) * ATT_TQ
                _, vjp = jax.vjp(_attn_tile, q_ref[...].astype(F32), k_ref[:keys, :].astype(F32), v_ref[:keys, :].astype(F32))
                dq, dk, dv = vjp(do_ref[...])
                dq_ref[...] = dq
                dk_ref[:keys, :] += dk
                dv_ref[:keys, :] += dv

    q_spec, k_spec, v_spec, o_spec = _attn_specs()
    return pl.pallas_call(kern, grid=(MH, n_tiles), in_specs=[q_spec, k_spec, v_spec, o_spec], out_specs=[q_spec, k_spec, v_spec],
                          out_shape=[SDS(qf.shape, F32), SDS(kf.shape, F32), SDS(vh.shape, F32)], name="l1_attn_bwd",
                          compiler_params=_cparams())(qf, kf, vh, dout)


def loss_and_grad(y, target, tl=256):
    def kern(y_ref, t_ref, dy_ref, loss_ref):
        d = y_ref[...] - t_ref[...]
        dy_ref[...] = d / D

        @pl.when(pl.program_id(0) == 0)
        def _():
            loss_ref[...] = jnp.zeros_like(loss_ref)

        loss_ref[...] += 0.5 * jnp.sum(jnp.sum(d * d, axis=1, keepdims=True), axis=0, keepdims=True) / D

    return pl.pallas_call(kern, grid=(L // tl,), in_specs=[rspec(tl, D), rspec(tl, D)], out_specs=[rspec(tl, D), cspec((1, 1))],
                          out_shape=[SDS((L, D), F32), SDS((1, 1), F32)], name="loss", compiler_params=_cparams())(y, target)


def adamw(name, w, g, m, v):
    rows, cols = w.shape
    block_row_bytes = 7 * 2 * 4 * max(cols, 128)
    tr = _row_tile(rows, min(2048, ADAMW_VMEM // block_row_bytes // 8 * 8), 8)

    def kern(w_ref, g_ref, m_ref, v_ref, d_ref, nm_ref, nv_ref):
        gg = g_ref[...]
        nm = ADAM_B1 * m_ref[...] + (1.0 - ADAM_B1) * gg
        nv = ADAM_B2 * v_ref[...] + (1.0 - ADAM_B2) * jnp.square(gg)
        m_hat = nm / (1.0 - ADAM_B1 ** ADAM_STEP)
        v_hat = nv / (1.0 - ADAM_B2 ** ADAM_STEP)
        d_ref[...] = -ADAM_LR * (m_hat / (jnp.sqrt(v_hat) + ADAM_EPS) + ADAM_WD * w_ref[...])
        nm_ref[...] = nm
        nv_ref[...] = nv

    spec = rspec(tr, cols)
    return pl.pallas_call(kern, grid=(rows // tr,), in_specs=[spec] * 4, out_specs=[spec] * 3,
                          out_shape=[SDS((rows, cols), F32)] * 3, name=name, compiler_params=_cparams())(w, g, m, v)


def _row_tile(rows, cap=512, unit=16):
    return max(t for t in range(unit, cap + 1, unit) if rows % t == 0)


def _place():
    x, y, c = lax.axis_index("x"), lax.axis_index("y"), lax.axis_index("c")
    return x, y, c, [(1 - x, y), (x, 1 - y), (1 - x, 1 - y)]


def _row_chunks(rows, n, dtype):
    unit = 32 // jnp.dtype(dtype).itemsize
    base, extra = divmod(rows // unit, n)
    out, start = [], 0
    for k in range(n):
        size = (base + (k < extra)) * unit
        if size:
            out.append((start, size))
            start += size
    assert start == rows, (rows, unit)
    return out


PIECE_BYTES = 1 << 20


def _pieces(shapes_dtypes, rows_of):
    out = []
    for b, (shape, dtype) in enumerate(shapes_dtypes):
        rows = rows_of(shape)
        n = max(1, min(4, rows * shape[-1] * jnp.dtype(dtype).itemsize // PIECE_BYTES))
        out += [(b, st, sz) for st, sz in _row_chunks(rows, n, dtype)]
    return out


def all_gather_chips(name, shards):
    nb = len(shards)
    pieces = _pieces([(s.shape, s.dtype) for s in shards], lambda shape: shape[0] // 2)
    n = len(pieces)

    def body(*refs):
        x_refs, out_refs, send_sems, recv_sems = refs[:nb], refs[nb:2 * nb], refs[2 * nb], refs[2 * nb + 1]
        x, y, c, chips = _place()
        sibling = (x, y, 1 - c)
        mine = 2 * x + y

        def copy(sem, chip, cc, k, to, from_input=False):
            b, st, sz = pieces[k]
            rows_k = pl.ds(cc * (x_refs[b].shape[0] // 2) + st, sz)
            dst = out_refs[b].at[chip, rows_k, :]
            return pltpu.make_async_remote_copy(src_ref=x_refs[b].at[rows_k, :] if from_input else dst, dst_ref=dst,
                                                send_sem=send_sems.at[sem], recv_sem=recv_sems.at[sem], device_id=to, device_id_type=MESH_ID)

        order = [(k, j, 2 * cx + cy, (cx, cy, c)) for k in range(n) for j, (cx, cy) in enumerate(chips)]
        first = [copy(j * n + k, mine, c, k, to, from_input=True) for k, j, _, to in order]
        for cp in first:
            cp.start()
        passed = []
        for k, j, chip, _ in order:
            copy(j * n + k, chip, c, k, sibling).wait_recv()
            passed.append(copy((3 + j) * n + k, chip, c, k, sibling))
            passed[-1].start()
        for k, j, chip, _ in order:
            copy((3 + j) * n + k, chip, 1 - c, k, sibling).wait_recv()
        for cp in first + passed:
            cp.wait_send()

    return pl.pallas_call(body, in_specs=[ANY] * nb, out_specs=[ANY] * nb, out_shape=[SDS((4,) + s.shape, s.dtype) for s in shards],
                          scratch_shapes=_dma_sems(6 * n), name=name)(*shards)


def plan_gather_ici(shards):
    pieces = _pieces([(s.shape, s.dtype) for s in shards], lambda shape: shape[0] // 2)
    n = len(pieces)

    def copies(x_refs, out_refs, send_sems, recv_sems):
        x, y, c, chips = _place()
        mine = 2 * x + y

        def copy(j, k, chip, to, from_input):
            b, st, sz = pieces[k]
            rows_k = pl.ds(c * (x_refs[b].shape[0] // 2) + st, sz)
            dst = out_refs[b].at[chip, rows_k, :]
            return pltpu.make_async_remote_copy(src_ref=x_refs[b].at[rows_k, :] if from_input else dst, dst_ref=dst, send_sem=send_sems.at[j * n + k],
                                                recv_sem=recv_sems.at[j * n + k], device_id=to, device_id_type=MESH_ID)

        order = [(k, j, 2 * cx + cy, (cx, cy, c)) for k in range(n) for j, (cx, cy) in enumerate(chips)]
        return [copy(j, k, mine, to, True) for k, j, _, to in order], [copy(j, k, chip, to, False) for k, j, chip, to in order]

    return Plan(shards, [SDS((4,) + s.shape, s.dtype) for s in shards], {}, 3 * n, copies)


def plan_gather_pass(gathered):
    pieces = _pieces([(g.shape[1:], g.dtype) for g in gathered], lambda shape: shape[0] // 2)
    n = len(pieces)

    def copies(_, out_refs, send_sems, recv_sems):
        x, y, c, chips = _place()

        def copy(j, k, chip, cc):
            b, st, sz = pieces[k]
            rows_k = out_refs[b].at[chip, pl.ds(cc * (out_refs[b].shape[1] // 2) + st, sz), :]
            return pltpu.make_async_remote_copy(src_ref=rows_k, dst_ref=rows_k, send_sem=send_sems.at[j * n + k], recv_sem=recv_sems.at[j * n + k],
                                                device_id=(x, y, 1 - c), device_id_type=MESH_ID)

        order = [(k, j, 2 * cx + cy) for k in range(n) for j, (cx, cy) in enumerate(chips)]
        return [copy(j, k, chip, c) for k, j, chip in order], [copy(j, k, chip, 1 - c) for k, j, chip in order]

    return Plan(gathered, [SDS(g.shape, g.dtype) for g in gathered], {i: i for i in range(len(gathered))}, 3 * n, copies)


def plan_pair_exchange(gs):
    pieces = _pieces([(g.shape, g.dtype) for g in gs], lambda shape: shape[1] // 2)

    def copies(g_refs, got_refs, send_sems, recv_sems):
        x, y, c, _ = _place()
        swaps = [pltpu.make_async_remote_copy(src_ref=g_refs[b].at[:, pl.ds((1 - c) * (g_refs[b].shape[1] // 2) + st, sz), :],
                                              dst_ref=got_refs[b].at[:, pl.ds(st, sz), :], send_sem=send_sems.at[k], recv_sem=recv_sems.at[k],
                                              device_id=(x, y, 1 - c), device_id_type=MESH_ID)
                 for k, (b, st, sz) in enumerate(pieces)]
        return swaps, swaps

    return Plan(gs, [SDS((g.shape[0], g.shape[1] // 2, g.shape[2]), g.dtype) for g in gs], {}, len(pieces), copies)


def plan_chip_scatter(ps):
    pieces = _pieces([(p.shape, p.dtype) for p in ps], lambda shape: shape[1])
    n = len(pieces)

    def copies(p_refs, q_refs, send_sems, recv_sems):
        x, y, c, chips = _place()
        mine = 2 * x + y

        def copy(j, k, src_slot, dst_slot, to):
            b, st, sz = pieces[k]
            return pltpu.make_async_remote_copy(src_ref=p_refs[b].at[src_slot, pl.ds(st, sz), :], dst_ref=q_refs[b].at[dst_slot, pl.ds(st, sz), :],
                                                send_sem=send_sems.at[j * n + k], recv_sem=recv_sems.at[j * n + k], device_id=to,
                                                device_id_type=MESH_ID)

        order = [(k, j, 2 * cx + cy, (cx, cy, c)) for k in range(n) for j, (cx, cy) in enumerate(chips)]
        return [copy(j, k, chip, mine, to) for k, j, chip, to in order], [copy(j, k, mine, chip, to) for k, j, chip, to in order]

    return Plan(ps, [SDS(p.shape, p.dtype) for p in ps], {}, 3 * n, copies)


def plan_pair_join(bufs):
    pieces = _pieces([(b.shape, b.dtype) for b in bufs], lambda shape: shape[0] // 2)

    def copies(_, out_refs, send_sems, recv_sems):
        x, y, c, _ = _place()

        def copy(k, cc):
            b, st, sz = pieces[k]
            rows_k = out_refs[b].at[pl.ds(cc * (out_refs[b].shape[0] // 2) + st, sz), :]
            return pltpu.make_async_remote_copy(src_ref=rows_k, dst_ref=rows_k, send_sem=send_sems.at[k], recv_sem=recv_sems.at[k],
                                                device_id=(x, y, 1 - c), device_id_type=MESH_ID)

        return [copy(k, c) for k in range(len(pieces))], [copy(k, 1 - c) for k in range(len(pieces))]

    return Plan(bufs, [SDS(b.shape, b.dtype) for b in bufs], {i: i for i in range(len(bufs))}, len(pieces), copies)


def run_plan(name, plan):
    hosted_call(lambda: None, name=name, grid=(1,), in_specs=[], out_specs=[], out_shape=[], operands=[], plans=[plan])
    return plan.results


def pair_add(name, g, got, place):
    slots, rows, cols = g.shape
    half = rows // 2
    tr = _row_tile(half)
    nb = half // tr

    def kern(_, g_ref, t_ref, o_ref):
        o_ref[...] = (g_ref[...].astype(F32) + t_ref[...].astype(F32)).astype(o_ref.dtype)

    blk = pl.BlockSpec((None, tr, cols), lambda s, i, p: (s, i, 0))
    grid_spec = pltpu.PrefetchScalarGridSpec(
        num_scalar_prefetch=1, grid=(slots, nb),
        in_specs=[pl.BlockSpec((None, tr, cols), lambda s, i, p: (s, p[1] * nb + i, 0)), blk], out_specs=blk)
    return pl.pallas_call(kern, grid_spec=grid_spec, out_shape=SDS((slots, half, cols), g.dtype), name=name,
                          compiler_params=_cparams())(place, g, got)


def chip_add(name, p, q, place):
    slots, half, cols = p.shape
    tr = _row_tile(half)
    nb = half // tr

    def kern(_, p_ref, q1, q2, q3, o_ref):
        o_ref[...] = p_ref[...].astype(F32) + q1[...].astype(F32) + q2[...].astype(F32) + q3[...].astype(F32)

    def slot(k):
        return pl.BlockSpec((None, tr, cols), lambda i, pr: ((pr[0] + k) % slots, i, 0))

    grid_spec = pltpu.PrefetchScalarGridSpec(
        num_scalar_prefetch=1, grid=(nb,), in_specs=[slot(0), slot(1), slot(2), slot(3)],
        out_specs=pl.BlockSpec((tr, cols), lambda i, pr: (pr[1] * nb + i, 0)))
    return pl.pallas_call(kern, grid_spec=grid_spec, out_shape=SDS((2 * half, cols), F32), name=name,
                          compiler_params=_cparams())(place, p, q, q, q)


def pair_adds(tag, gs, gots, place):
    return [pair_add(f"{tag}_pair_add_{i}", g, got, place) for i, (g, got) in enumerate(zip(gs, gots))]


def chip_adds(tag, pairs, qs, place):
    return [chip_add(f"{tag}_chip_add_{i}", p, q, place) for i, (p, q) in enumerate(zip(pairs, qs))]


def reduce_scatter_chips(tag, gs, place):
    pairs = pair_adds(tag, gs, run_plan(tag + "_pair_exchange", plan_pair_exchange(gs)), place)
    return run_plan(tag + "_pair_join", plan_pair_join(chip_adds(tag, pairs, run_plan(tag + "_chip_scatter", plan_chip_scatter(pairs)), place)))


BIG = [("w_out", (2, 512, 1024)), ("w_mem_kv", (2, 256, 1024)), ("s5_w_in", (1, 1024, 1024)), ("s5_w_glu", (1, 1536, 768)),
       ("mla_w_in", (1, 1024, 848)), ("mla_w_uq", (1, 512, 576)), ("mla_w_ukv", (1, 256, 768))]
SHARDED_SMALL = [("mla_q_lora_norm", (1, 128)), ("mla_kv_lora_norm", (1, 64))]
SMALL = [("ln_gain", (2, 1024)), ("mem_norm", (2, 1024)), ("xq_norm", (2, 128)), ("xk_norm", (2, 128)),
         ("s5_lambda_re", (1, 96, 64)), ("s5_lambda_im", (1, 96, 64)), ("s5_log_step", (1, 96)),
         ("s5_b_re", (1, 96, 64, 16)), ("s5_b_im", (1, 96, 64, 16)), ("s5_c_re", (1, 96, 16, 64)), ("s5_c_im", (1, 96, 16, 64)),
         ("s5_d", (1, 1536)), ("mla_q_nope_norm", (1, 128)), ("mla_k_nope_norm", (1, 128)), ("mla_q_rope_norm", (1, 64)),
         ("mla_k_rope_norm", (1, 64))]
WEIGHT_ORDER = ["ln_gain", "w_out", "mem_norm", "w_mem_kv", "xq_norm", "xk_norm", "s5_w_in", "s5_lambda_re", "s5_lambda_im",
                "s5_log_step", "s5_b_re", "s5_b_im", "s5_c_re", "s5_c_im", "s5_d", "s5_w_glu", "mla_w_in", "mla_q_lora_norm",
                "mla_kv_lora_norm", "mla_w_uq", "mla_w_ukv", "mla_q_nope_norm", "mla_k_nope_norm", "mla_q_rope_norm", "mla_k_rope_norm"]
MINOR_LAST = {"mla_w_in": (0, 2, 1), "mla_w_uq": (0, 2, 1), "s5_b_re": (0, 2, 3, 1), "s5_b_im": (0, 2, 3, 1),
              "s5_c_re": (0, 2, 3, 1), "s5_c_im": (0, 2, 3, 1)}
SMALL_FULL = SMALL + [(n, (1, 4 * s[1])) for n, s in SHARDED_SMALL]
N_SMALL = sum(math.prod(s) for _, s in SMALL_FULL)
SMALL_ROWS, SMALL_LANES = 128, 1024

WIDE0_IN, WIDE0_OUT, WIDE0_MKV = 0, 1024, 1536
PAIR_OUT, PAIR_MKV, PAIR_ROWS = 0, 512, 768


def stack_shards(w, dtype):
    wide0 = jnp.concatenate([w["s5_w_in"][0], w["w_out"][0], w["w_mem_kv"][0]], axis=0)
    wide1 = jnp.concatenate([w["w_out"][1], w["w_mem_kv"][1]], axis=0)
    return ([wide0.astype(dtype), w["s5_w_glu"][0].astype(dtype)],
            [wide1.astype(dtype), w["mla_w_ukv"][0].astype(dtype), w["mla_w_in"][0].astype(dtype), w["mla_w_uq"][0].astype(dtype)])


def weight_views0(wide0, glu):
    return {"s5_w_in": Sharded(wide0, "col", WIDE0_IN, 1024), "w_out": Sharded(wide0, "row", WIDE0_OUT, 512),
            "w_mem_kv": Sharded(wide0, "row", WIDE0_MKV, 256), "s5_w_glu": Sharded(glu, "col", 0, 1536)}


def weight_views1(wide1, ukv):
    return {"w_out": Sharded(wide1, "row", PAIR_OUT, 512), "w_mem_kv": Sharded(wide1, "row", PAIR_MKV, 256), "mla_w_ukv": Sharded(ukv, "col", 0, 256)}


def grad_views():
    pair = SDS((4, PAIR_ROWS, 1024), BF16)
    return {"w_out": Sharded(pair, "row", PAIR_OUT, 512), "w_mem_kv": Sharded(pair, "row", PAIR_MKV, 256),
            "s5_w_in": Sharded(SDS((4, 1024, 1024), BF16), "col", 0, 1024), "s5_w_glu": Sharded(SDS((4, 1536, 768), BF16), "col", 0, 1536),
            "mla_w_ukv": Sharded(SDS((4, 256, 768), BF16), "col", 0, 256)}


def cols_to_shards(full):
    return full.reshape(full.shape[0], 4, full.shape[1] // 4).transpose(1, 0, 2)


def shards_to_cols(arr):
    return arr.transpose(1, 0, 2).reshape(arr.shape[1], 4 * arr.shape[2])


def mla_in_permute(w):
    o1, o2, o3, o4 = QL, QL + KVL, QL + KVL + ROPE, QL + KVL + ROPE + XQW
    return jnp.concatenate([w[:, o4:], w[:, :o1], w[:, o3:o4], w[:, o1:o2], w[:, o2:o3],
                            jnp.zeros((w.shape[0], MLA_IN_P - MLA_IN), w.dtype)], axis=1)


def mla_in_unpermute(d):
    return jnp.concatenate([d[:, 2048:2560], d[:, 3072:3328], d[:, 3328:3392], d[:, 2560:3072], d[:, :2048]], axis=1)


def uq_permute(w):
    w3 = w.reshape(w.shape[0], MH, NOPE + ROPE)
    return jnp.concatenate([w3[:, :, :NOPE].reshape(w.shape[0], MH * NOPE), w3[:, :, NOPE:].reshape(w.shape[0], MH * ROPE)], axis=1)


def uq_unpermute(d):
    dn = d[:, :MH * NOPE].reshape(d.shape[0], MH, NOPE)
    dr = d[:, MH * NOPE:].reshape(d.shape[0], MH, ROPE)
    return jnp.concatenate([dn, dr], axis=2).reshape(d.shape[0], MH * (NOPE + ROPE))


def time_permute(a):
    return a.reshape(SEG, SEG_LEN, a.shape[-1]).transpose(1, 0, 2).reshape(L, a.shape[-1])


def time_unpermute(a):
    return a.reshape(SEG_LEN, SEG, a.shape[-1]).transpose(1, 0, 2).reshape(L, a.shape[-1])


def mem_branch_fwd(tag, mem, mem_norm, w_mem_kv, xk_norm):
    mn = row_fwd(tag + "_mem_rms", fn_rms, ML, ML, [(mem, D, 0)], [mem_norm], [(D, BF16)])[0]
    kv = matmul(tag + "_mem_kv", mn, w_mem_kv, "nn")
    kn = row_fwd(tag + "_mem_knorm", fn_mem_k, ML, ML, [(kv, XQW, 0)], [xk_norm], [(XQW, F32)])[0]
    return mn, kv, kn


def mem_branch_bwd(tag, mem, mem_norm, w_mem_kv, xk_norm, mn, kv, dkn, dv, g_view, g_wide):
    dk, dxk = row_bwd(tag + "_mem_knorm_bwd", fn_mem_k, ML, ML, [(kv, XQW, 0)], [xk_norm], [(dkn, XQW, 0)], [True], [True])
    dkv = jnp.concatenate([dk, dv], axis=1)
    dmn = matmul(tag + "_mem_kv_dx", dkv, w_mem_kv, "nt")
    g_wide = matmul(tag + "_mem_kv_dw", mn, dkv, "tn", out=g_view, into=g_wide)
    dmem_norm = row_bwd(tag + "_mem_rms_bwd", fn_rms, ML, ML, [(mem, D, 0)], [mem_norm], [(dmn, D, 0)], [False], [True])[0]
    return g_wide, dmem_norm, dxk


def mem_attn_fwd(tag, proj, cb, kn, kv, xq_norm):
    return row_fwd(tag + "_mem_attn", fn_mem_attn, L, 256, [(proj, XQW, cb)], [kn, kv[:, XQW:], xq_norm], [(XQW, F32)])[0]


def mem_attn_bwd(tag, proj, cb, kn, kv, xq_norm, dmo, dproj):
    place = {"cols": proj.shape[1], "cb": cb, "into": dproj, "dtype": dproj.dtype}
    return row_bwd(tag + "_mem_attn_bwd", fn_mem_attn, L, 256, [(proj, XQW, cb)], [kn, kv[:, XQW:], xq_norm], [(dmo, XQW, 0)],
                   [place], [True, True, True])


def device_step(x, mem, positions, target, small, env, hooks=None):
    hooks = hooks or {}

    def plans_for(name):
        return hooks[("plans", name)](env) if ("plans", name) in hooks else ()

    def after(name):
        if ("after", name) in hooks:
            hooks[("after", name)](env)

    g = {}
    w0 = weight_views0(env["wide0"], env["glu"])
    gw = grad_views()
    ln, mem_norm, xq_norm, xk_norm = small["ln_gain"], small["mem_norm"], small["xq_norm"], small["xk_norm"]

    lre, lim = small["s5_lambda_re"][0], small["s5_lambda_im"][0]
    ls = small["s5_log_step"].reshape(SG, 1)
    one = pl.BlockSpec((SG, SP), lambda i: (0, 0))
    col = pl.BlockSpec((SG, 1), lambda i: (0, 0))
    disc_ins = [(lre, one), (lim, one), (ls, col)]
    a_re, a_im, coef_re, coef_im = stage("s5_disc", fn_s5_disc, (1,), disc_ins, [(SDS((SG, SP), F32), one)] * 4)
    b_re, b_im = small["s5_b_re"].reshape(SN, SC), small["s5_b_im"].reshape(SN, SC)
    c_re, c_im = small["s5_c_re"].reshape(PW, SP), small["s5_c_im"].reshape(PW, SP)
    bmat_rows = [(b_re, SC, 0), (b_im, SC, 0), (coef_re.reshape(SN, 1), 1, 0), (coef_im.reshape(SN, 1), 1, 0)]
    wb_re, wb_im = row_fwd("s5_bmat", fn_s5_bmat, SN, 512, bmat_rows, [], [(128, F32)] * 2)
    cmat_rows = [(c_re, SP, 0), (c_im, SP, 0)]
    wc_re, wc_im = row_fwd("s5_cmat", fn_s5_cmat, PW, 128, cmat_rows, [], [(512, F32)] * 2)
    a_re_v, a_im_v = a_re.reshape(1, SN), a_im.reshape(1, SN)
    s5_d = small["s5_d"]

    xp = time_permute(x)
    h0 = row_fwd("l0_rms", fn_rms, L, 256, [(xp, D, 0)], [ln[0:1]], [(D, BF16)])[0]
    proj0 = matmul("l0_in", h0, w0["s5_w_in"], "nn")
    s_re, s_im, g0 = s5_forward(proj0, wb_re, wb_im, wc_re, wc_im, a_re_v, a_im_v, s5_d, plans=plans_for("s5_forward"))
    z0 = matmul("l0_glu", g0, w0["s5_w_glu"], "nn", plans=plans_for("l0_glu"))
    after("l0_glu")
    mn0, kv0, kn0 = mem_branch_fwd("l0", mem, mem_norm[0:1], w0["w_mem_kv"], xk_norm[0:1])
    mo0 = mem_attn_fwd("l0", proj0, 3, kn0, kv0, xq_norm[0:1])
    o0 = row_fwd("l0_merge", fn_merge_glu, L, 256, [(z0, 2 * PW, 0), (mo0, XQW, 0), (proj0, BW, 1)], [], [(BW, BF16)])[0]
    x1p = matmul("l0_out", o0, w0["w_out"], "nn", add=xp)
    x1 = time_unpermute(x1p)

    w1 = weight_views1(env["wide1"], env["ukv"])
    w_in1, w_uq = env["w_in1"], env["w_uq"]
    h1 = row_fwd("l1_rms", fn_rms, L, 256, [(x1, D, 0)], [ln[1:2]], [(D, BF16)])[0]
    proj1 = matmul("l1_in", h1, w_in1, "nn")
    qln, kvln = env["q_lora_norm"].reshape(1, QL), env["kv_lora_norm"].reshape(1, KVL)
    cqn = row_fwd("l1_q_lora_rms", fn_rms, L, 256, [(proj1, QL, 4)], [qln], [(QL, BF16)])[0]
    ckvn = row_fwd("l1_kv_lora_rms", fn_rms, L, 256, [(proj1, KVL, 12)], [kvln], [(KVL, BF16)])[0]
    q = matmul("l1_uq", cqn, w_uq, "nn")
    kv = matmul("l1_ukv", ckvn, w1["mla_w_ukv"], "nn")
    inv_freq = ROPE_THETA ** (-jnp.arange(ROPE // 2, dtype=F32) / (ROPE // 2))
    ang = positions.astype(F32)[:, None] * inv_freq
    cos2 = jnp.tile(jnp.cos(ang), (1, 4))
    sin_signed = jnp.tile(jnp.concatenate([-jnp.sin(ang), jnp.sin(ang)], axis=1), (1, 2))
    qnn, knn = small["mla_q_nope_norm"], small["mla_k_nope_norm"]
    qrn, krn = jnp.tile(small["mla_q_rope_norm"], (1, 2)), jnp.tile(small["mla_k_rope_norm"], (1, 2))
    tp = 256
    prep_ins = [(q, rspec(tp, MH * (NOPE + ROPE))), (kv, rspec(tp, MH * 256)), (proj1, rspec(tp, 128, 26)),
                (cos2, rspec(tp, 128)), (sin_signed, rspec(tp, 128))] + [(a, cspec((1, 128))) for a in (qnn, knn, qrn, krn)]
    hq_spec = pl.BlockSpec((MH, tp, 256), lambda i: (0, i, 0))
    hv_spec = pl.BlockSpec((MH, tp, 128), lambda i: (0, i, 0))
    qf, kf, vh = stage("l1_mla_prep", fn_mla_prep, (L // tp,), prep_ins,
                       [(SDS((MH, L, 256), BF16), hq_spec), (SDS((MH, L, 256), BF16), hq_spec), (SDS((MH, L, 128), BF16), hv_spec)])
    attn = causal_attn(qf, kf, vh)
    mn1, kv1, kn1 = mem_branch_fwd("l1", mem, mem_norm[1:2], w1["w_mem_kv"], xk_norm[1:2])
    mo1 = mem_attn_fwd("l1", proj1, 5, kn1, kv1, xq_norm[1:2])
    o1 = row_fwd("l1_merge", fn_merge, L, 256, [(attn, PW, 0), (mo1, XQW, 0), (proj1, BW, 0)], [], [(BW, BF16)])[0]
    x2 = matmul("l1_out", o1, w1["w_out"], "nn", add=x1)
    dx2, loss = loss_and_grad(x2, target)

    do1 = matmul("l1_out_dx", dx2, w1["w_out"], "nt")
    g_pair1 = matmul("l1_out_dw", o1, dx2, "tn", out=gw["w_out"])
    dattn, dmo1, dproj1 = row_bwd("l1_merge_bwd", fn_merge, L, 256, [(attn, PW, 0), (mo1, XQW, 0), (proj1, BW, 0)], [],
                                  [(do1, BW, 0)], [True, True, {"cols": MLA_IN_P, "cb": 0, "dtype": BF16}], [])
    dproj1, dkn1, dv1, dxqn1 = mem_attn_bwd("l1", proj1, 5, kn1, kv1, xq_norm[1:2], dmo1, dproj1)
    env["g_pair1"], dmem_norm1, dxk1 = mem_branch_bwd("l1", mem, mem_norm[1:2], w1["w_mem_kv"], xk_norm[1:2], mn1, kv1, dkn1, dv1,
                                                      gw["w_mem_kv"], g_pair1)
    dqf, dkf, dvh = causal_attn_bwd(qf, kf, vh, dattn)
    prep_diffs = [("row", SDS((L, MH * (NOPE + ROPE)), BF16), rspec(tp, MH * (NOPE + ROPE))), ("row", SDS((L, MH * 256), BF16), rspec(tp, MH * 256)),
                  ("row", SDS((L, MLA_IN_P), BF16), rspec(tp, 128, 26), {"into": dproj1}), None, None] + [("acc", (0,))] * 4
    dq, dkv, dproj1, dqnn, dknn, dqrn, dkrn = stage_bwd("l1_mla_prep_bwd", fn_mla_prep, (L // tp,), prep_ins,
                                                        [(dqf, hq_spec), (dkf, hq_spec), (dvh, hv_spec)], prep_diffs)
    dcqn = matmul("l1_uq_dx", dq, w_uq, "nt")
    env["dw_uq"] = matmul("l1_uq_dw", cqn, dq, "tn")
    dckvn = matmul("l1_ukv_dx", dkv, w1["mla_w_ukv"], "nt")
    env["g_ukv"] = matmul("l1_ukv_dw", ckvn, dkv, "tn", out=gw["mla_w_ukv"])
    dproj1, dqln = row_bwd("l1_q_lora_rms_bwd", fn_rms, L, 256, [(proj1, QL, 4)], [qln], [(dcqn, QL, 0)],
                           [{"cols": MLA_IN_P, "cb": 4, "into": dproj1, "dtype": BF16}], [True])
    dproj1, dkvln = row_bwd("l1_kv_lora_rms_bwd", fn_rms, L, 256, [(proj1, KVL, 12)], [kvln], [(dckvn, KVL, 0)],
                            [{"cols": MLA_IN_P, "cb": 12, "into": dproj1, "dtype": BF16}], [True])
    dh1 = matmul("l1_in_dx", dproj1, w_in1, "nt")
    env["dw_in1"] = matmul("l1_in_dw", h1, dproj1, "tn")
    dx1, dln1 = row_bwd("l1_rms_bwd", fn_rms, L, 256, [(x1, D, 0)], [ln[1:2]], [(dh1, D, 0)], [{"add": (dx2, D, 0)}], [True],
                        plans=plans_for("l1_rms_bwd"))
    dx1p = time_permute(dx1)

    do0 = matmul("l0_out_dx", dx1p, w0["w_out"], "nt", plans=plans_for("l0_out_dx"))
    g_pair0 = matmul("l0_out_dw", o0, dx1p, "tn", out=gw["w_out"])
    dz0, dmo0, dproj0 = row_bwd("l0_merge_bwd", fn_merge_glu, L, 256, [(z0, 2 * PW, 0), (mo0, XQW, 0), (proj0, BW, 1)], [],
                                [(do0, BW, 0)], [{"dtype": BF16}, True, {"cols": 2 * BW, "cb": 1, "dtype": BF16}], [])
    dproj0, dkn0, dv0, dxqn0 = mem_attn_bwd("l0", proj0, 3, kn0, kv0, xq_norm[0:1], dmo0, dproj0)
    env["g_pair0"], dmem_norm0, dxk0 = mem_branch_bwd("l0", mem, mem_norm[0:1], w0["w_mem_kv"], xk_norm[0:1], mn0, kv0, dkn0, dv0,
                                                      gw["w_mem_kv"], g_pair0)
    env["g_glu"] = matmul("l0_glu_dw", g0, dz0, "tn", out=gw["s5_w_glu"], plans=plans_for("l0_glu_dw"))
    dg0 = matmul("l0_glu_dx", dz0, w0["s5_w_glu"], "nt", plans=plans_for("l0_glu_dx"))
    dproj0, dd, dwc_re, dwc_im, dwb_re, dwb_im, da_re, da_im = s5_backward(dg0, proj0, s_re, s_im, wb_re, wb_im, wc_re, wc_im,
                                                                           a_re_v, a_im_v, s5_d, dproj0, plans=plans_for("s5_backward"))
    env["g_in0"] = matmul("l0_in_dw", h0, dproj0, "tn", out=gw["s5_w_in"])
    dh0 = matmul("l0_in_dx", dproj0, w0["s5_w_in"], "nt", plans=plans_for("l0_in_dx"))
    dxp, dln0 = row_bwd("l0_rms_bwd", fn_rms, L, 256, [(xp, D, 0)], [ln[0:1]], [(dh0, D, 0)], [{"add": (dx1p, D, 0)}], [True])
    grad_x = time_unpermute(dxp)

    db_re, db_im, dcoef_re, dcoef_im = row_bwd("s5_bmat_bwd", fn_s5_bmat, SN, 512, bmat_rows, [], [(dwb_re, 128, 0), (dwb_im, 128, 0)],
                                               [True] * 4, [], plans=plans_for("s5_bmat_bwd"))
    dc_re, dc_im = row_bwd("s5_cmat_bwd", fn_s5_cmat, PW, 128, cmat_rows, [], [(dwc_re, 512, 0), (dwc_im, 512, 0)], [True] * 2, [],
                           plans=plans_for("s5_cmat_bwd"))
    disc_cts = [(da_re.reshape(SG, SP), one), (da_im.reshape(SG, SP), one), (dcoef_re.reshape(SG, SP), one), (dcoef_im.reshape(SG, SP), one)]
    dlre, dlim, dls = stage_bwd("s5_disc_bwd", fn_s5_disc, (1,), disc_ins, disc_cts, [("acc", (0,))] * 3)

    g["ln_gain"] = jnp.concatenate([dln0, dln1], axis=0)
    g["mem_norm"] = jnp.concatenate([dmem_norm0, dmem_norm1], axis=0)
    g["xq_norm"] = jnp.concatenate([dxqn0, dxqn1], axis=0)
    g["xk_norm"] = jnp.concatenate([dxk0, dxk1], axis=0)
    g["s5_lambda_re"], g["s5_lambda_im"], g["s5_log_step"] = dlre, dlim, dls
    g["s5_b_re"], g["s5_b_im"], g["s5_c_re"], g["s5_c_im"] = db_re, db_im, dc_re, dc_im
    g["s5_d"] = dd
    g["mla_q_lora_norm"], g["mla_kv_lora_norm"] = dqln, dkvln
    g["mla_q_nope_norm"], g["mla_k_nope_norm"] = dqnn, dknn
    g["mla_q_rope_norm"] = dqrn[:, :ROPE] + dqrn[:, ROPE:]
    g["mla_k_rope_norm"] = dkrn[:, :ROPE] + dkrn[:, ROPE:]
    return loss, grad_x, g


def kernel(x, mem, positions, ln_gain, w_out, mem_norm, w_mem_kv, xq_norm, xk_norm, s5_w_in, s5_lambda_re, s5_lambda_im, s5_log_step, s5_b_re, s5_b_im, s5_c_re, s5_c_im, s5_d, s5_w_glu, mla_w_in, mla_q_lora_norm, mla_kv_lora_norm, mla_w_uq, mla_w_ukv, mla_q_nope_norm, mla_k_nope_norm, mla_q_rope_norm, mla_k_rope_norm, loss_target, m_ln_gain, m_w_out, m_mem_norm, m_w_mem_kv, m_xq_norm, m_xk_norm, m_s5_w_in, m_s5_lambda_re, m_s5_lambda_im, m_s5_log_step, m_s5_b_re, m_s5_b_im, m_s5_c_re, m_s5_c_im, m_s5_d, m_s5_w_glu, m_mla_w_in, m_mla_q_lora_norm, m_mla_kv_lora_norm, m_mla_w_uq, m_mla_w_ukv, m_mla_q_nope_norm, m_mla_k_nope_norm, m_mla_q_rope_norm, m_mla_k_rope_norm, v_ln_gain, v_w_out, v_mem_norm, v_w_mem_kv, v_xq_norm, v_xk_norm, v_s5_w_in, v_s5_lambda_re, v_s5_lambda_im, v_s5_log_step, v_s5_b_re, v_s5_b_im, v_s5_c_re, v_s5_c_im, v_s5_d, v_s5_w_glu, v_mla_w_in, v_mla_q_lora_norm, v_mla_kv_lora_norm, v_mla_w_uq, v_mla_w_ukv, v_mla_q_nope_norm, v_mla_k_nope_norm, v_mla_q_rope_norm, v_mla_k_rope_norm):
    args = dict(locals())
    wts = {n: args[n] for n in WEIGHT_ORDER}
    mom = {n: args["m_" + n] for n in WEIGHT_ORDER}
    var = {n: args["v_" + n] for n in WEIGHT_ORDER}

    chip = 2 * lax.axis_index("x") + lax.axis_index("y")
    place = jnp.stack([chip, lax.axis_index("c")]).astype(jnp.int32)

    def own_slot(gathered, shards):
        return [lax.dynamic_update_slice(g, s[None], (chip, 0, 0)) for g, s in zip(gathered, shards)]

    shards0, shards1 = stack_shards(wts, BF16)
    shards1.append(jnp.concatenate([mla_q_lora_norm, jnp.pad(mla_kv_lora_norm, ((0, 0), (0, 64))), jnp.zeros((14, 128), F32)], axis=0))
    env = dict(zip(("wide0", "glu"), own_slot(all_gather_chips("gather_weights0", shards0), shards0)))
    over_ici = plan_gather_ici(shards1)
    hooks = {("plans", "s5_forward"): lambda env: [over_ici]}
    passed_on = []

    def pass_on(env):
        passed_on.append(plan_gather_pass(over_ici.results))
        return passed_on

    def layer1_weights(env):
        wide1, ukv, in1, uq, norms = own_slot(passed_on[0].results, shards1)
        env.update(wide1=wide1, ukv=ukv, w_in1=mla_in_permute(shards_to_cols(in1)), w_uq=uq_permute(shards_to_cols(uq)),
                   q_lora_norm=norms[:, 0, :], kv_lora_norm=norms[:, 1, :64])

    hooks["plans", "l0_glu"], hooks["after", "l0_glu"] = pass_on, layer1_weights

    rs = {}

    def swap(k, gs):
        rs[k, "g"], rs[k, "swap"] = gs, plan_pair_exchange(gs)
        return rs[k, "swap"]

    def scatter(k, part=slice(None)):
        if (k, "pairs") not in rs:
            rs[k, "pairs"], rs[k, "scatter"] = pair_adds(f"rs{k}", rs[k, "g"], rs[k, "swap"].results, place), []
        rs[k, "scatter"].append(plan_chip_scatter(rs[k, "pairs"][part]))
        return rs[k, "scatter"][-1]

    def join(k):
        rs[k, "join"] = plan_pair_join(chip_adds(f"rs{k}", rs[k, "pairs"], [q for p in rs[k, "scatter"] for q in p.results], place))
        return rs[k, "join"]

    hooks["plans", "l1_rms_bwd"] = lambda env: [swap(1, [env["g_pair1"], env["g_ukv"], cols_to_shards(mla_in_unpermute(env["dw_in1"])).astype(BF16),
                                                          cols_to_shards(uq_unpermute(env["dw_uq"])).astype(BF16)])]
    hooks["plans", "l0_glu_dw"] = lambda env: [scatter(1, slice(0, 2))]
    hooks["plans", "l0_glu_dx"] = lambda env: [scatter(1, slice(2, 4)), swap(0, [env["g_pair0"], env["g_glu"]])]
    hooks["plans", "s5_backward"] = lambda env: [scatter(0), join(1)]
    hooks["plans", "l0_in_dx"] = lambda env: [swap(2, [env["g_in0"]]), join(0)]
    hooks["plans", "s5_bmat_bwd"] = lambda env: [scatter(2)]
    hooks["plans", "s5_cmat_bwd"] = lambda env: [join(2)]

    small = {n: wts[n] for n, _ in SMALL}
    loss, grad_x, g = device_step(x[0], mem[0], positions[0], loss_target[0], small, env, hooks)
    loss = lax.psum(loss[0, 0], MESH_AXES)
    (r_pair1, r_ukv, r_in1, r_uq), (r_pair0, r_glu), (r_in0,) = (rs[k, "join"].results for k in (1, 0, 2))

    small_flat = jnp.concatenate([g[n].reshape(-1) for n, _ in SMALL_FULL])
    g_small = jnp.pad(small_flat, (0, 4 * SMALL_ROWS * SMALL_LANES - N_SMALL)).astype(BF16).reshape(4, SMALL_ROWS, SMALL_LANES)
    r_small = reduce_scatter_chips("rs3", [g_small], place)[0]
    small_all = own_slot(all_gather_chips("gather_small_grads", [r_small]), [r_small])[0].reshape(-1)[:N_SMALL]

    grads = {"w_out": jnp.stack([r_pair0[:PAIR_MKV], r_pair1[:PAIR_MKV]]), "w_mem_kv": jnp.stack([r_pair0[PAIR_MKV:], r_pair1[PAIR_MKV:]]),
             "s5_w_in": r_in0[None], "s5_w_glu": r_glu[None], "mla_w_ukv": r_ukv[None], "mla_w_in": r_in1[None], "mla_w_uq": r_uq[None]}
    off = 0
    for n, s in SMALL_FULL:
        grads[n] = small_all[off:off + math.prod(s)].reshape(s)
        off += math.prod(s)
    for n, s in SHARDED_SMALL:
        grads[n] = lax.dynamic_slice(grads[n], (0, chip * s[1]), s)

    delta, new_m, new_v = {}, {}, {}
    for n, s in BIG + [(n, s) for n, s in SMALL if len(s) == 4]:
        perm = MINOR_LAST.get(n, tuple(range(len(s))))
        turned = tuple(s[p] for p in perm)
        view = lambda a: jnp.transpose(a, perm).reshape(-1, turned[-1])
        res = adamw("adamw_" + n, view(wts[n]), view(grads[n]), view(mom[n]), view(var[n]))
        delta[n], new_m[n], new_v[n] = (jnp.transpose(r.reshape(turned), tuple(perm.index(i) for i in range(len(s)))) for r in res)
    small_names = [n for n, s in SMALL if len(s) < 4] + [n for n, _ in SHARDED_SMALL]
    n_own = sum(wts[n].size for n in small_names)
    rows_own = -(-n_own // (8 * 128)) * 8

    def pack_small(d):
        flat = jnp.concatenate([d[n].reshape(-1) for n in small_names])
        return jnp.pad(flat, (0, rows_own * 128 - n_own), constant_values=1.0).reshape(rows_own, 128)

    res = adamw("adamw_small", pack_small(wts), pack_small(grads), pack_small(mom), pack_small(var))
    off = 0
    for n in small_names:
        size = wts[n].size
        delta[n], new_m[n], new_v[n] = (r.reshape(-1)[off:off + size].reshape(wts[n].shape) for r in res)
        off += size

    return (loss, grad_x[None], *[grads[n] for n in WEIGHT_ORDER], *[delta[n] for n in WEIGHT_ORDER],
            *[new_m[n] for n in WEIGHT_ORDER], *[new_v[n] for n in WEIGHT_ORDER])
```

```python
import functools
import math

import jax
import jax.numpy as jnp
from jax import lax
from jax.experimental import pallas as pl
from jax.experimental.pallas import tpu as pltpu

F32, BF16 = jnp.float32, jnp.bfloat16
SDS = jax.ShapeDtypeStruct

D = 1024
L = 2048
ML = 256
BW = 2 * D
XQW = BW // 4
PW = BW - XQW
XH, XHD = 4, 128
SG, SC, SP = 96, 16, 64
SN = SG * SP
NOPE, ROPE, VD = 128, 64, 128
MH = 12
QL, KVL = 512, 256
EPS = 1e-6
ROPE_THETA = 10000.0
MLA_IN = QL + KVL + ROPE + XQW + BW
MLA_IN_P = 3456
ADAM_LR, ADAM_B1, ADAM_B2, ADAM_EPS, ADAM_WD, ADAM_STEP = 0.001, 0.9, 0.999, 1e-08, 0.01, 10

VMEM_LIMIT = 48 * 2**20
SEG = 8
SEG_LEN = L // SEG
MESH_AXES = ("x", "y", "c")


def _cparams():
    return pltpu.CompilerParams(vmem_limit_bytes=VMEM_LIMIT)


def _dg(a, b, ca, cb):
    return lax.dot_general(a.astype(BF16), b.astype(BF16), (((ca,), (cb,)), ((), ())), preferred_element_type=F32)


@jax.custom_vjp
def mm_nn(a, b):
    return _dg(a, b, 1, 0)


mm_nn.defvjp(lambda a, b: (_dg(a, b, 1, 0), (a, b)), lambda res, g: (_dg(g, res[1], 1, 1), _dg(res[0], g, 0, 0)))


@jax.custom_vjp
def mm_nt(a, b):
    return _dg(a, b, 1, 1)


mm_nt.defvjp(lambda a, b: (_dg(a, b, 1, 1), (a, b)), lambda res, g: (_dg(g, res[1], 1, 0), _dg(g, res[0], 0, 0)))


@functools.partial(jax.custom_vjp, nondiff_argnums=(1,))
def lane_roll(x, shift):
    return pltpu.roll(x, shift, 1)


lane_roll.defvjp(lambda x, shift: (pltpu.roll(x, shift, 1), None),
                 lambda shift, _, g: (pltpu.roll(g, (128 - shift) % 128, 1),))


def rms(x, g):
    return x * lax.rsqrt(jnp.mean(x * x, axis=-1, keepdims=True) + EPS) * g


@jax.custom_vjp
def softmax_rows(s):
    e = jnp.exp(s - jnp.max(s, axis=-1, keepdims=True))
    return e / jnp.sum(e, axis=-1, keepdims=True)


def _softmax_rows_fwd(s):
    p = softmax_rows(s)
    return p, p


def _softmax_rows_bwd(p, g):
    return (p * (g - jnp.sum(g * p, axis=-1, keepdims=True)),)


softmax_rows.defvjp(_softmax_rows_fwd, _softmax_rows_bwd)


def silu(x):
    return x * jax.nn.sigmoid(x)


ANY = pl.BlockSpec(memory_space=pl.ANY)
MESH_ID = pl.DeviceIdType.MESH


def _dma_sems(n):
    return [pltpu.SemaphoreType.DMA((n,)), pltpu.SemaphoreType.DMA((n,))]


class Plan:
    def __init__(self, operands, out_shape, aliases, n_sems, copies):
        self.operands, self.out_shape, self.aliases, self.n_sems, self.copies = list(operands), list(out_shape), aliases, n_sems, copies
        self.results = None


def hosted_call(kern, *, name, grid, in_specs, out_specs, out_shape, operands, scratch_shapes=(), aliases=None, cparams=None, plans=()):
    n_in, n_out, n_scr = len(in_specs), len(out_specs), len(scratch_shapes)
    p_in, p_out = [len(p.operands) for p in plans], [len(p.out_shape) for p in plans]
    all_aliases = dict(aliases or {})
    in_off, out_off = n_in, n_out
    for p, ni, no in zip(plans, p_in, p_out):
        all_aliases.update({in_off + i: out_off + o for i, o in p.aliases.items()})
        in_off, out_off = in_off + ni, out_off + no

    def body(*refs):
        pos, pins, pouts = n_in, [], []
        for ni in p_in:
            pins.append(refs[pos:pos + ni])
            pos += ni
        main_out = refs[pos:pos + n_out]
        pos += n_out
        for no in p_out:
            pouts.append(refs[pos:pos + no])
            pos += no
        main_scr = refs[pos:pos + n_scr]
        pos += n_scr
        if plans:
            ids = [pl.program_id(ax) for ax in range(len(grid))]
            first = functools.reduce(jnp.logical_and, [i == 0 for i in ids])
            last = functools.reduce(jnp.logical_and, [i == g - 1 for i, g in zip(ids, grid)])
            copies = [p.copies(pins[k], pouts[k], refs[pos + 2 * k], refs[pos + 2 * k + 1]) for k, p in enumerate(plans)]

            @pl.when(first)
            def _():
                for sends, _ in copies:
                    for cp in sends:
                        cp.start()

        kern(*refs[:n_in], *main_out, *main_scr)
        if plans:
            @pl.when(last)
            def _():
                for sends, recvs in copies:
                    for cp in recvs:
                        cp.wait_recv()
                    for cp in sends:
                        cp.wait_send()

    res = pl.pallas_call(body, grid=grid, in_specs=list(in_specs) + [ANY] * sum(p_in), out_specs=list(out_specs) + [ANY] * sum(p_out),
                         out_shape=list(out_shape) + [s for p in plans for s in p.out_shape],
                         scratch_shapes=list(scratch_shapes) + [s for p in plans for s in _dma_sems(p.n_sems)],
                         input_output_aliases=all_aliases, name=name, compiler_params=cparams or _cparams())(
        *operands, *[a for p in plans for a in p.operands])
    pos = n_out
    for p, no in zip(plans, p_out):
        p.results = list(res[pos:pos + no])
        pos += no
    return list(res[:n_out])


def stage(name, fn, grid, ins, outs):
    n_in = len(ins)

    def kern(*refs):
        res = fn(*[r[...] for r in refs[:n_in]])
        for r, v in zip(refs[n_in:], res):
            r[...] = v.astype(r.dtype)

    return pl.pallas_call(kern, grid=grid, in_specs=[s for _, s in ins], out_specs=[s for _, s in outs],
                          out_shape=[sd for sd, _ in outs], name=name, compiler_params=_cparams())(*[a for a, _ in ins])


def stage_bwd(name, fn, grid, ins, cts, diffs, plans=()):
    n_in, n_ct = len(ins), len(cts)
    didx = [i for i, d in enumerate(diffs) if d is not None]
    opts = {i: (diffs[i][3] if len(diffs[i]) > 3 else {}) for i in didx if diffs[i][0] == "row"}
    adds = [(i, opts[i]["add"]) for i in opts if "add" in opts[i]]
    intos = [(i, opts[i]["into"]) for i in opts if "into" in opts[i]]
    n_add, n_into = len(adds), len(intos)
    add_pos = {i: n_in + n_ct + k for k, (i, _) in enumerate(adds)}
    n_extra = n_in + n_ct + n_add + n_into

    def kern(*refs):
        vals = [r[...] for r in refs[:n_in]]

        def f(*dv):
            full = list(vals)
            for i, v in zip(didx, dv):
                full[i] = v
            return fn(*full)

        _, vjp = jax.vjp(f, *[vals[i].astype(F32) for i in didx])
        gs = vjp(tuple(c[...].astype(F32) for c in refs[n_in:n_in + n_ct]))
        for o_ref, i, g in zip(refs[n_extra:], didx, gs):
            if diffs[i][0] == "row":
                if i in add_pos:
                    g = g + refs[add_pos[i]][...].astype(F32)
                o_ref[...] = g.astype(o_ref.dtype)
            else:
                first = functools.reduce(jnp.logical_and, [pl.program_id(ax) == 0 for ax in diffs[i][1]])

                @pl.when(first)
                def _():
                    o_ref[...] = g

                @pl.when(jnp.logical_not(first))
                def _():
                    o_ref[...] += g

    out_shape, out_specs = [], []
    for i in didx:
        if diffs[i][0] == "row":
            out_shape.append(diffs[i][1])
            out_specs.append(diffs[i][2])
        else:
            out_shape.append(SDS(ins[i][0].shape, F32))
            out_specs.append(ins[i][1])
    aliases = {n_in + n_ct + n_add + k: didx.index(i) for k, (i, _) in enumerate(intos)}
    in_specs = [s for _, s in ins] + [s for _, s in cts] + [s for _, (_, s) in adds] + [ANY] * n_into
    operands = [a for a, _ in ins] + [a for a, _ in cts] + [a for _, (a, _) in adds] + [a for _, a in intos]
    return hosted_call(kern, name=name, grid=grid, in_specs=in_specs, out_specs=out_specs, out_shape=out_shape, operands=operands,
                       aliases=aliases, plans=plans)


def rspec(tl, w, cb=0):
    return pl.BlockSpec((tl, w), lambda i: (i, cb))


def cspec(shape):
    return pl.BlockSpec(shape, lambda i: (0,) * len(shape))


def row_fwd(name, fn, rows, tl, row_ins, consts, outs):
    ins = [(a, rspec(tl, w, cb)) for a, w, cb in row_ins] + [(a, cspec(a.shape)) for a in consts]
    return stage(name, fn, (rows // tl,), ins, [(SDS((rows, w), dt), rspec(tl, w)) for w, dt in outs])


def row_bwd(name, fn, rows, tl, row_ins, consts, cts, row_diff, const_diff, plans=()):
    ins = [(a, rspec(tl, w, cb)) for a, w, cb in row_ins] + [(a, cspec(a.shape)) for a in consts]
    diffs = []
    for (a, w, cb), d in zip(row_ins, row_diff):
        if not d:
            diffs.append(None)
            continue
        d = d if isinstance(d, dict) else {}
        opts = {}
        if "add" in d:
            opts["add"] = (d["add"][0], rspec(tl, d["add"][1], d["add"][2]))
        if d.get("into") is not None:
            opts["into"] = d["into"]
        diffs.append(("row", SDS((rows, d.get("cols", w)), d.get("dtype", F32)), rspec(tl, w, d.get("cb", 0)), opts))
    diffs += [("acc", (0,)) if d else None for d in const_diff]
    return stage_bwd(name, fn, (rows // tl,), ins, [(a, rspec(tl, w, cb)) for a, w, cb in cts], diffs, plans=plans)


MATMUL_VMEM = 28 * 2**20
ADAMW_VMEM = 28 * 2**20


class Sharded:
    def __init__(self, arr, kind, roff, rows):
        self.arr, self.kind, self.roff, self.rows, self.n = arr, kind, roff, rows, arr.shape[2]
        self.shape = (rows, 4 * self.n) if kind == "col" else (4 * rows, self.n)

    def fits(self, t0, t1):
        return self.roff % t0 == 0 and self.rows % t0 == 0 and self.n % t1 == 0

    def spec(self, t0, t1, bidx):
        assert self.fits(t0, t1), (self.kind, self.roff, self.rows, self.n, t0, t1)
        r0 = self.roff // t0
        if self.kind == "col":
            per = self.n // t1
            return pl.BlockSpec((None, t0, t1), lambda *g: (bidx(*g)[1] // per, r0 + bidx(*g)[0], bidx(*g)[1] % per))
        per = self.rows // t0
        return pl.BlockSpec((None, t0, t1), lambda *g: (bidx(*g)[0] // per, r0 + bidx(*g)[0] % per, bidx(*g)[1]))


def matmul(name, a, b, mode, out_dtype=F32, add=None, out=None, into=None, plans=()):
    if mode == "tn":
        k_dim, m = a.shape
    else:
        m, k_dim = a.shape
    n = b.shape[0] if mode == "nt" else b.shape[1]
    b_fit = b.fits if isinstance(b, Sharded) else (lambda t0, t1: True)
    o_fit = out.fits if out is not None else (lambda t0, t1: True)
    a_bytes, b_bytes = jnp.dtype(a.dtype).itemsize, jnp.dtype(b.arr.dtype if isinstance(b, Sharded) else b.dtype).itemsize
    o_bytes = jnp.dtype(out_dtype if out is None else out.arr.dtype).itemsize

    def vmem(tm, tn, tk):
        return 2 * (tm * tk * a_bytes + tk * tn * b_bytes + tm * tn * (o_bytes + (4 if add is not None else 0))) + 4 * tm * tn

    tiles = [(tm, tn, tk) for tm in (2048, 1024, 512, 256, 128) for tn in (1024, 768, 512, 384, 256, 128) for tk in (1024, 512, 384, 256, 128)
             if m % tm == 0 and n % tn == 0 and k_dim % tk == 0 and (b_fit(tn, tk) if mode == "nt" else b_fit(tk, tn)) and o_fit(tm, tn)
             and vmem(tm, tn, tk) <= MATMUL_VMEM]
    tm, tn, tk = max(tiles, key=lambda t: (t[0] * t[1] * t[2], t[0] * t[1]))
    nk = k_dim // tk
    a_spec = pl.BlockSpec((tk, tm), lambda i, j, k: (k, i)) if mode == "tn" else pl.BlockSpec((tm, tk), lambda i, j, k: (i, k))
    if isinstance(b, Sharded):
        b_spec = b.spec(tn, tk, lambda i, j, k: (j, k)) if mode == "nt" else b.spec(tk, tn, lambda i, j, k: (k, j))
        b = b.arr
    else:
        b_spec = pl.BlockSpec((tn, tk), lambda i, j, k: (j, k)) if mode == "nt" else pl.BlockSpec((tk, tn), lambda i, j, k: (k, j))
    o_spec = pl.BlockSpec((tm, tn), lambda i, j, k: (i, j))
    out_spec, out_shape = (o_spec, SDS((m, n), out_dtype)) if out is None else (out.spec(tm, tn, lambda i, j, k: (i, j)), out.arr)
    ca, cb = {"nn": (1, 0), "nt": (1, 1), "tn": (0, 0)}[mode]
    n_in = 2 + (add is not None)

    def kern(*refs):
        a_ref, b_ref = refs[0], refs[1]
        o_ref, acc = refs[-2], refs[-1]
        k = pl.program_id(2)

        @pl.when(k == 0)
        def _():
            acc[...] = jnp.zeros_like(acc)

        acc[...] += _dg(a_ref[...], b_ref[...], ca, cb)

        @pl.when(k == nk - 1)
        def _():
            r = acc[...]
            if add is not None:
                r = r + refs[2][...]
            o_ref[...] = r.astype(o_ref.dtype)

    ins, specs = [a, b], [a_spec, b_spec]
    if add is not None:
        ins.append(add)
        specs.append(o_spec)
    if into is not None:
        ins.append(into)
        specs.append(ANY)
    return hosted_call(kern, name=name, grid=(m // tm, n // tn, nk), in_specs=specs, out_specs=[out_spec], out_shape=[out_shape],
                       operands=ins, scratch_shapes=[pltpu.VMEM((tm, tn), F32)], aliases={} if into is None else {n_in: 0}, plans=plans)[0]


def _cmul(ar, ai, br, bi):
    return ar * br - ai * bi, ar * bi + ai * br


def _sub_shift(x, down):
    row = lax.broadcasted_iota(jnp.int32, x.shape, 0)
    if down:
        return jnp.where(row == 0, 0.0, pltpu.roll(x, 1, 0))
    return jnp.where(row == SEG - 1, 0.0, pltpu.roll(x, SEG - 1, 0))


def _pow_seg_len(ar, ai):
    for _ in range(int(math.log2(SEG_LEN))):
        ar, ai = _cmul(ar, ai, ar, ai)
    return ar, ai


def _scan_in_place(sr, si, a_re, a_im):
    lanes = sr.shape[1]
    ar = jnp.broadcast_to(a_re, (SEG, lanes))
    ai = jnp.broadcast_to(a_im, (SEG, lanes))
    zero = jnp.zeros((SEG, lanes), F32)

    def local(i, carry):
        rows = pl.ds(pl.multiple_of(i * SEG, SEG), SEG)
        mr, mi = _cmul(ar, ai, carry[0], carry[1])
        nr, ni = mr + sr[rows, :], mi + si[rows, :]
        sr[rows, :] = nr
        si[rows, :] = ni
        return nr, ni

    fr, fi = lax.fori_loop(0, SEG_LEN, local, (zero, zero))
    pr, pi = _pow_seg_len(ar, ai)
    ir, ii = zero, zero
    for _ in range(SEG - 1):
        mr, mi = _cmul(pr, pi, ir, ii)
        ir, ii = _sub_shift(mr + fr, True), _sub_shift(mi + fi, True)

    def carry_in(i, pw):
        rows = pl.ds(pl.multiple_of(i * SEG, SEG), SEG)
        cr, ci = _cmul(pw[0], pw[1], ir, ii)
        sr[rows, :] += cr
        si[rows, :] += ci
        return _cmul(pw[0], pw[1], ar, ai)

    lax.fori_loop(0, SEG_LEN, carry_in, (ar, ai))


S5_LANES = 8 * SP
S5_BLOCKS = SN // S5_LANES


def _s5_specs():
    u_spec = pl.BlockSpec((L, 8 * SC), lambda j: (0, j))
    s_spec = pl.BlockSpec((L, S5_LANES), lambda j: (0, j))
    wb_spec = pl.BlockSpec((S5_LANES, 8 * SC), lambda j: (j, 0))
    wc_spec = pl.BlockSpec((8 * SC, S5_LANES), lambda j: (j, 0))
    a_spec = pl.BlockSpec((1, S5_LANES), lambda j: (0, j))
    d_spec = pl.BlockSpec((1, 8 * SC), lambda j: (0, j))
    return u_spec, s_spec, wb_spec, wc_spec, a_spec, d_spec


def s5_forward(proj, wb_re, wb_im, wc_re, wc_im, a_re, a_im, d, plans=()):
    def kern(u_ref, wbr, wbi, wcr, wci, ar, ai, d_ref, sr, si, g_ref):
        u = u_ref[...]
        sr[...], si[...] = fn_s5_bu(u, wbr[...], wbi[...])
        _scan_in_place(sr, si, ar[...], ai[...])
        g_ref[...] = fn_s5_out(sr[...], si[...], u, d_ref[...], wcr[...], wci[...])[0].astype(g_ref.dtype)

    u_spec, s_spec, wb_spec, wc_spec, a_spec, d_spec = _s5_specs()
    return hosted_call(kern, name="s5_forward", grid=(S5_BLOCKS,), in_specs=[u_spec, wb_spec, wb_spec, wc_spec, wc_spec, a_spec, a_spec, d_spec],
                       out_specs=[s_spec, s_spec, u_spec], out_shape=[SDS((L, SN), F32)] * 2 + [SDS((L, PW), BF16)],
                       operands=[proj, wb_re, wb_im, wc_re, wc_im, a_re, a_im, d], plans=plans)


def _adjoint_scan_in_place(lr, li, sr, si, a_re, a_im):
    lanes = lr.shape[1]
    ar = jnp.broadcast_to(a_re, (SEG, lanes))
    ai = -jnp.broadcast_to(a_im, (SEG, lanes))
    zero = jnp.zeros((SEG, lanes), F32)

    def local(k, carry):
        i = SEG_LEN - 1 - k
        rows = pl.ds(pl.multiple_of(i * SEG, SEG), SEG)
        mr, mi = _cmul(ar, ai, carry[0], carry[1])
        nr, ni = mr + lr[rows, :], mi + li[rows, :]
        lr[rows, :] = nr
        li[rows, :] = ni
        return nr, ni

    fr, fi = lax.fori_loop(0, SEG_LEN, local, (zero, zero))
    pr, pi = _pow_seg_len(ar, ai)
    ir, ii = zero, zero
    for _ in range(SEG - 1):
        mr, mi = _cmul(pr, pi, ir, ii)
        ir, ii = _sub_shift(mr + fr, False), _sub_shift(mi + fi, False)

    def fix(rows, pw):
        cr, ci = _cmul(pw[0], pw[1], ir, ii)
        tr, ti = lr[rows, :] + cr, li[rows, :] + ci
        lr[rows, :] = tr
        li[rows, :] = ti
        return tr, ti

    def grad_a(tr, ti, spr, spi, acc):
        return acc[0] + tr * spr + ti * spi, acc[1] + ti * spr - tr * spi

    def carry_in(k, c):
        i = SEG_LEN - 1 - k
        rows = pl.ds(pl.multiple_of(i * SEG, SEG), SEG)
        prev = pl.ds(pl.multiple_of((i - 1) * SEG, SEG), SEG)
        tr, ti = fix(rows, (c[0], c[1]))
        acc = grad_a(tr, ti, sr[prev, :], si[prev, :], (c[2], c[3]))
        nr, ni = _cmul(c[0], c[1], ar, ai)
        return nr, ni, acc[0], acc[1]

    pwr, pwi, accr, acci = lax.fori_loop(0, SEG_LEN - 1, carry_in, (ar, ai, zero, zero))
    tr, ti = fix(pl.ds(0, SEG), (pwr, pwi))
    last = pl.ds((SEG_LEN - 1) * SEG, SEG)
    accr, acci = grad_a(tr, ti, _sub_shift(sr[last, :], True), _sub_shift(si[last, :], True), (accr, acci))
    return jnp.sum(accr, axis=0, keepdims=True), jnp.sum(acci, axis=0, keepdims=True)


S5_BWD_VMEM = 58 * 2**20


def s5_backward(dg, proj, s_re, s_im, wb_re, wb_im, wc_re, wc_im, a_re, a_im, d, dproj, plans=()):
    def kern(dg_ref, u_ref, sr, si, wbr, wbi, wcr, wci, ar, ai, d_ref, _, du_ref, dd_ref, dwcr, dwci, dwbr, dwbi, dar, dai, lr, li):
        u = u_ref[...]
        _, vjp_out = jax.vjp(fn_s5_out, sr[...], si[...], u, d_ref[...], wcr[...], wci[...])
        lr[...], li[...], du_out, dd_ref[...], dwcr[...], dwci[...] = vjp_out((dg_ref[...],))
        dar[...], dai[...] = _adjoint_scan_in_place(lr, li, sr, si, ar[...], ai[...])
        _, vjp_in = jax.vjp(fn_s5_bu, u, wbr[...], wbi[...])
        du_in, dwbr[...], dwbi[...] = vjp_in((lr[...], li[...]))
        du_ref[...] = (du_out + du_in).astype(du_ref.dtype)

    u_spec, s_spec, wb_spec, wc_spec, a_spec, d_spec = _s5_specs()
    outs = [(SDS(dproj.shape, dproj.dtype), u_spec), (SDS(d.shape, F32), d_spec), (SDS(wc_re.shape, F32), wc_spec), (SDS(wc_im.shape, F32), wc_spec),
            (SDS(wb_re.shape, F32), wb_spec), (SDS(wb_im.shape, F32), wb_spec), (SDS(a_re.shape, F32), a_spec), (SDS(a_im.shape, F32), a_spec)]
    return hosted_call(kern, name="s5_backward", grid=(S5_BLOCKS,),
                       in_specs=[u_spec, u_spec, s_spec, s_spec, wb_spec, wb_spec, wc_spec, wc_spec, a_spec, a_spec, d_spec, ANY],
                       out_specs=[sp for _, sp in outs], out_shape=[sd for sd, _ in outs], aliases={11: 0},
                       operands=[dg, proj, s_re, s_im, wb_re, wb_im, wc_re, wc_im, a_re, a_im, d, dproj],
                       scratch_shapes=[pltpu.VMEM((L, S5_LANES), F32)] * 2,
                       cparams=pltpu.CompilerParams(vmem_limit_bytes=S5_BWD_VMEM), plans=plans)


def fn_rms(x, g):
    return (rms(x, g),)


def fn_s5_disc(lre, lim, ls):
    step = jnp.exp(ls)
    e = jnp.exp(lre * step)
    a_re, a_im = e * jnp.cos(lim * step), e * jnp.sin(lim * step)
    den = lre * lre + lim * lim
    nr, ni = a_re - 1.0, a_im
    return a_re, a_im, (nr * lre + ni * lim) / den, (ni * lre - nr * lim) / den


def _group_mask(rows, cols, row_div, col_div):
    r = lax.broadcasted_iota(jnp.int32, (rows, cols), 0) // row_div % 8
    c = lax.broadcasted_iota(jnp.int32, (rows, cols), 1) // col_div
    return r == c


def _spread(x, mask):
    w = x.shape[1]
    copy = (lax.broadcasted_iota(jnp.int32, (w, 8 * w), 1) % w == lax.broadcasted_iota(jnp.int32, (w, 8 * w), 0)).astype(F32)
    return jnp.where(mask, jnp.dot(x, copy, precision=lax.Precision.HIGHEST, preferred_element_type=F32), 0.0)


def fn_s5_bmat(b_re, b_im, coef_re, coef_im):
    mask = _group_mask(b_re.shape[0], 8 * SC, SP, SC)
    return _spread(coef_re * b_re - coef_im * b_im, mask), _spread(coef_re * b_im + coef_im * b_re, mask)


def fn_s5_cmat(c_re, c_im):
    mask = _group_mask(c_re.shape[0], 8 * SP, SC, SP)
    return _spread(c_re, mask), _spread(c_im, mask)


def fn_s5_bu(u, wb_re, wb_im):
    return mm_nt(u, wb_re), mm_nt(u, wb_im)


def fn_s5_out(sr, si, u, d, wc_re, wc_im):
    y = mm_nt(sr, wc_re) - mm_nt(si, wc_im) + d * u
    return (jax.nn.gelu(y),)


def fn_merge_glu(z, mo, gate):
    yg = z[:, :PW] * jax.nn.sigmoid(z[:, PW:])
    return (jnp.concatenate([yg, mo], axis=1) * silu(gate),)


def fn_merge(prim, mo, gate):
    return (jnp.concatenate([prim, mo], axis=1) * silu(gate),)


def fn_mem_k(kv, g):
    return (jnp.concatenate([rms(kv[:, h * XHD:(h + 1) * XHD], g) for h in range(XH)], axis=1),)


def fn_mem_attn(xq, kn, v, g):
    outs = []
    for h in range(XH):
        sl = slice(h * XHD, (h + 1) * XHD)
        p = softmax_rows(mm_nt(rms(xq[:, sl], g), kn[:, sl]) * (XHD ** -0.5))
        outs.append(mm_nn(p, v[:, sl]))
    return (jnp.concatenate(outs, axis=1),)


def _half_rms(x, g):
    lo = lax.broadcasted_iota(jnp.int32, x.shape, 1) < ROPE
    x2 = x * x
    s_lo = jnp.sum(jnp.where(lo, x2, 0.0), axis=1, keepdims=True)
    s_hi = jnp.sum(jnp.where(lo, 0.0, x2), axis=1, keepdims=True)
    return x * lax.rsqrt(jnp.where(lo, s_lo, s_hi) / ROPE + EPS) * g


def _rope(x, cos2, sin_signed):
    first = lax.broadcasted_iota(jnp.int32, x.shape, 1) % ROPE < ROPE // 2
    return x * cos2 + jnp.where(first, lane_roll(x, 128 - ROPE // 2), lane_roll(x, ROPE // 2)) * sin_signed


def fn_mla_prep(q, kv, kr, cos2, sin_signed, qnn, knn, qrn, krn):
    lo = lax.broadcasted_iota(jnp.int32, kr.shape, 1) < ROPE
    kr_pad = jnp.where(lo, _rope(_half_rms(kr, krn), cos2, sin_signed), 0.0)
    qf, kf, vs = [], [], []
    for m in range(MH // 2):
        pair = _rope(_half_rms(q[:, MH * NOPE + 128 * m:MH * NOPE + 128 * (m + 1)], qrn), cos2, sin_signed)
        for h, rope_h in ((2 * m, pair), (2 * m + 1, lane_roll(pair, ROPE))):
            qf.append(jnp.concatenate([rms(q[:, NOPE * h:NOPE * (h + 1)], qnn), jnp.where(lo, rope_h, 0.0)], axis=1))
    for h in range(MH):
        kf.append(jnp.concatenate([rms(kv[:, 256 * h:256 * h + NOPE], knn), kr_pad], axis=1))
        vs.append(kv[:, 256 * h + NOPE:256 * (h + 1)])
    return jnp.stack(qf), jnp.stack(kf), jnp.stack(vs)


ATT_TQ = 256


def _attn_tile(q, kf, v):
    tq = q.shape[0]
    scale = (NOPE + ROPE) ** -0.5
    own = mm_nt(q, kf[-tq:]) * scale
    own = jnp.where(lax.broadcasted_iota(jnp.int32, own.shape, 1) <= lax.broadcasted_iota(jnp.int32, own.shape, 0), own, jnp.finfo(F32).min)
    s = own if kf.shape[0] == tq else jnp.concatenate([mm_nt(q, kf[:-tq]) * scale, own], axis=1)
    return mm_nn(softmax_rows(s), v)


def _attn_specs():
    q_spec = pl.BlockSpec((None, ATT_TQ, 256), lambda h, i: (h, i, 0))
    k_spec = pl.BlockSpec((None, L, 256), lambda h, i: (h, 0, 0))
    v_spec = pl.BlockSpec((None, L, 128), lambda h, i: (h, 0, 0))
    o_spec = pl.BlockSpec((ATT_TQ, 128), lambda h, i: (i, h))
    return q_spec, k_spec, v_spec, o_spec


def causal_attn(qf, kf, vh):
    n_tiles = L // ATT_TQ

    def kern(q_ref, k_ref, v_ref, o_ref):
        i = pl.program_id(1)
        for t in range(n_tiles):
            @pl.when(i == t)
            def _(t=t):
                keys = (t + 1) * ATT_TQ
                o_ref[...] = _attn_tile(q_ref[...], k_ref[:keys, :], v_ref[:keys, :])

    q_spec, k_spec, v_spec, o_spec = _attn_specs()
    return pl.pallas_call(kern, grid=(MH, n_tiles), in_specs=[q_spec, k_spec, v_spec], out_specs=o_spec,
                          out_shape=SDS((L, MH * VD), F32), name="l1_attn", compiler_params=_cparams())(qf, kf, vh)


def causal_attn_bwd(qf, kf, vh, dout):
    n_tiles = L // ATT_TQ

    def kern(q_ref, k_ref, v_ref, do_ref, dq_ref, dk_ref, dv_ref):
        i = pl.program_id(1)

        @pl.when(i == 0)
        def _():
            dk_ref[...] = jnp.zeros_like(dk_ref)
            dv_ref[...] = jnp.zeros_like(dv_ref)

        for t in range(n_tiles):
            @pl.when(i == t)
            def _(t=t):
                keys = (t + 1) * ATT_TQ
                _, vjp = jax.vjp(_attn_tile, q_ref[...].astype(F32), k_ref[:keys, :].astype(F32), v_ref[:keys, :].astype(F32))
                dq, dk, dv = vjp(do_ref[...])
                dq_ref[...] = dq
                dk_ref[:keys, :] += dk
                dv_ref[:keys, :] += dv

    q_spec, k_spec, v_spec, o_spec = _attn_specs()
    return pl.pallas_call(kern, grid=(MH, n_tiles), in_specs=[q_spec, k_spec, v_spec, o_spec], out_specs=[q_spec, k_spec, v_spec],
                          out_shape=[SDS(qf.shape, F32), SDS(kf.shape, F32), SDS(vh.shape, F32)], name="l1_attn_bwd",
                          compiler_params=_cparams())(qf, kf, vh, dout)


def loss_and_grad(y, target, tl=256):
    def kern(y_ref, t_ref, dy_ref, loss_ref):
        d = y_ref[...] - t_ref[...]
        dy_ref[...] = d / D

        @pl.when(pl.program_id(0) == 0)
        def _():
            loss_ref[...] = jnp.zeros_like(loss_ref)

        loss_ref[...] += 0.5 * jnp.sum(jnp.sum(d * d, axis=1, keepdims=True), axis=0, keepdims=True) / D

    return pl.pallas_call(kern, grid=(L // tl,), in_specs=[rspec(tl, D), rspec(tl, D)], out_specs=[rspec(tl, D), cspec((1, 1))],
                          out_shape=[SDS((L, D), F32), SDS((1, 1), F32)], name="loss", compiler_params=_cparams())(y, target)


def adamw(name, w, g, m, v):
    rows, cols = w.shape
    block_row_bytes = 7 * 2 * 4 * max(cols, 128)
    tr = _row_tile(rows, min(2048, ADAMW_VMEM // block_row_bytes // 8 * 8), 8)

    def kern(w_ref, g_ref, m_ref, v_ref, d_ref, nm_ref, nv_ref):
        gg = g_ref[...]
        nm = ADAM_B1 * m_ref[...] + (1.0 - ADAM_B1) * gg
        nv = ADAM_B2 * v_ref[...] + (1.0 - ADAM_B2) * jnp.square(gg)
        m_hat = nm / (1.0 - ADAM_B1 ** ADAM_STEP)
        v_hat = nv / (1.0 - ADAM_B2 ** ADAM_STEP)
        d_ref[...] = -ADAM_LR * (m_hat / (jnp.sqrt(v_hat) + ADAM_EPS) + ADAM_WD * w_ref[...])
        nm_ref[...] = nm
        nv_ref[...] = nv

    spec = rspec(tr, cols)
    return pl.pallas_call(kern, grid=(rows // tr,), in_specs=[spec] * 4, out_specs=[spec] * 3,
                          out_shape=[SDS((rows, cols), F32)] * 3, name=name, compiler_params=_cparams())(w, g, m, v)


def _row_tile(rows, cap=512, unit=16):
    return max(t for t in range(unit, cap + 1, unit) if rows % t == 0)


def _place():
    x, y, c = lax.axis_index("x"), lax.axis_index("y"), lax.axis_index("c")
    return x, y, c, [(1 - x, y), (x, 1 - y), (1 - x, 1 - y)]


def _row_chunks(rows, n, dtype):
    unit = 32 // jnp.dtype(dtype).itemsize
    base, extra = divmod(rows // unit, n)
    out, start = [], 0
    for k in range(n):
        size = (base + (k < extra)) * unit
        if size:
            out.append((start, size))
            start += size
    assert start == rows, (rows, unit)
    return out


PIECE_BYTES = 1 << 20


def _pieces(shapes_dtypes, rows_of):
    out = []
    for b, (shape, dtype) in enumerate(shapes_dtypes):
        rows = rows_of(shape)
        n = max(1, min(4, rows * shape[-1] * jnp.dtype(dtype).itemsize // PIECE_BYTES))
        out += [(b, st, sz) for st, sz in _row_chunks(rows, n, dtype)]
    return out


def all_gather_chips(name, shards):
    nb = len(shards)
    pieces = _pieces([(s.shape, s.dtype) for s in shards], lambda shape: shape[0] // 2)
    n = len(pieces)

    def body(*refs):
        x_refs, out_refs, send_sems, recv_sems = refs[:nb], refs[nb:2 * nb], refs[2 * nb], refs[2 * nb + 1]
        x, y, c, chips = _place()
        sibling = (x, y, 1 - c)
        mine = 2 * x + y

        def copy(sem, chip, cc, k, to, from_input=False):
            b, st, sz = pieces[k]
            rows_k = pl.ds(cc * (x_refs[b].shape[0] // 2) + st, sz)
            dst = out_refs[b].at[chip, rows_k, :]
            return pltpu.make_async_remote_copy(src_ref=x_refs[b].at[rows_k, :] if from_input else dst, dst_ref=dst,
                                                send_sem=send_sems.at[sem], recv_sem=recv_sems.at[sem], device_id=to, device_id_type=MESH_ID)

        order = [(k, j, 2 * cx + cy, (cx, cy, c)) for k in range(n) for j, (cx, cy) in enumerate(chips)]
        first = [copy(j * n + k, mine, c, k, to, from_input=True) for k, j, _, to in order]
        for cp in first:
            cp.start()
        passed = []
        for k, j, chip, _ in order:
            copy(j * n + k, chip, c, k, sibling).wait_recv()
            passed.append(copy((3 + j) * n + k, chip, c, k, sibling))
            passed[-1].start()
        for k, j, chip, _ in order:
            copy((3 + j) * n + k, chip, 1 - c, k, sibling).wait_recv()
        for cp in first + passed:
            cp.wait_send()

    return pl.pallas_call(body, in_specs=[ANY] * nb, out_specs=[ANY] * nb, out_shape=[SDS((4,) + s.shape, s.dtype) for s in shards],
                          scratch_shapes=_dma_sems(6 * n), name=name)(*shards)


def plan_gather_ici(shards):
    pieces = _pieces([(s.shape, s.dtype) for s in shards], lambda shape: shape[0] // 2)
    n = len(pieces)

    def copies(x_refs, out_refs, send_sems, recv_sems):
        x, y, c, chips = _place()
        mine = 2 * x + y

        def copy(j, k, chip, to, from_input):
            b, st, sz = pieces[k]
            rows_k = pl.ds(c * (x_refs[b].shape[0] // 2) + st, sz)
            dst = out_refs[b].at[chip, rows_k, :]
            return pltpu.make_async_remote_copy(src_ref=x_refs[b].at[rows_k, :] if from_input else dst, dst_ref=dst, send_sem=send_sems.at[j * n + k],
                                                recv_sem=recv_sems.at[j * n + k], device_id=to, device_id_type=MESH_ID)

        order = [(k, j, 2 * cx + cy, (cx, cy, c)) for k in range(n) for j, (cx, cy) in enumerate(chips)]
        return [copy(j, k, mine, to, True) for k, j, _, to in order], [copy(j, k, chip, to, False) for k, j, chip, to in order]

    return Plan(shards, [SDS((4,) + s.shape, s.dtype) for s in shards], {}, 3 * n, copies)


def plan_gather_pass(gathered):
    pieces = _pieces([(g.shape[1:], g.dtype) for g in gathered], lambda shape: shape[0] // 2)
    n = len(pieces)

    def copies(_, out_refs, send_sems, recv_sems):
        x, y, c, chips = _place()

        def copy(j, k, chip, cc):
            b, st, sz = pieces[k]
            rows_k = out_refs[b].at[chip, pl.ds(cc * (out_refs[b].shape[1] // 2) + st, sz), :]
            return pltpu.make_async_remote_copy(src_ref=rows_k, dst_ref=rows_k, send_sem=send_sems.at[j * n + k], recv_sem=recv_sems.at[j * n + k],
                                                device_id=(x, y, 1 - c), device_id_type=MESH_ID)

        order = [(k, j, 2 * cx + cy) for k in range(n) for j, (cx, cy) in enumerate(chips)]
        return [copy(j, k, chip, c) for k, j, chip in order], [copy(j, k, chip, 1 - c) for k, j, chip in order]

    return Plan(gathered, [SDS(g.shape, g.dtype) for g in gathered], {i: i for i in range(len(gathered))}, 3 * n, copies)


def plan_pair_exchange(gs):
    pieces = _pieces([(g.shape, g.dtype) for g in gs], lambda shape: shape[1] // 2)

    def copies(g_refs, got_refs, send_sems, recv_sems):
        x, y, c, _ = _place()
        swaps = [pltpu.make_async_remote_copy(src_ref=g_refs[b].at[:, pl.ds((1 - c) * (g_refs[b].shape[1] // 2) + st, sz), :],
                                              dst_ref=got_refs[b].at[:, pl.ds(st, sz), :], send_sem=send_sems.at[k], recv_sem=recv_sems.at[k],
                                              device_id=(x, y, 1 - c), device_id_type=MESH_ID)
                 for k, (b, st, sz) in enumerate(pieces)]
        return swaps, swaps

    return Plan(gs, [SDS((g.shape[0], g.shape[1] // 2, g.shape[2]), g.dtype) for g in gs], {}, len(pieces), copies)


def plan_chip_scatter(ps):
    pieces = _pieces([(p.shape, p.dtype) for p in ps], lambda shape: shape[1])
    n = len(pieces)

    def copies(p_refs, q_refs, send_sems, recv_sems):
        x, y, c, chips = _place()
        mine = 2 * x + y

        def copy(j, k, src_slot, dst_slot, to):
            b, st, sz = pieces[k]
            return pltpu.make_async_remote_copy(src_ref=p_refs[b].at[src_slot, pl.ds(st, sz), :], dst_ref=q_refs[b].at[dst_slot, pl.ds(st, sz), :],
                                                send_sem=send_sems.at[j * n + k], recv_sem=recv_sems.at[j * n + k], device_id=to,
                                                device_id_type=MESH_ID)

        order = [(k, j, 2 * cx + cy, (cx, cy, c)) for k in range(n) for j, (cx, cy) in enumerate(chips)]
        return [copy(j, k, chip, mine, to) for k, j, chip, to in order], [copy(j, k, mine, chip, to) for k, j, chip, to in order]

    return Plan(ps, [SDS(p.shape, p.dtype) for p in ps], {}, 3 * n, copies)


def plan_pair_join(bufs):
    pieces = _pieces([(b.shape, b.dtype) for b in bufs], lambda shape: shape[0] // 2)

    def copies(_, out_refs, send_sems, recv_sems):
        x, y, c, _ = _place()

        def copy(k, cc):
            b, st, sz = pieces[k]
            rows_k = out_refs[b].at[pl.ds(cc * (out_refs[b].shape[0] // 2) + st, sz), :]
            return pltpu.make_async_remote_copy(src_ref=rows_k, dst_ref=rows_k, send_sem=send_sems.at[k], recv_sem=recv_sems.at[k],
                                                device_id=(x, y, 1 - c), device_id_type=MESH_ID)

        return [copy(k, c) for k in range(len(pieces))], [copy(k, 1 - c) for k in range(len(pieces))]

    return Plan(bufs, [SDS(b.shape, b.dtype) for b in bufs], {i: i for i in range(len(bufs))}, len(pieces), copies)


def run_plan(name, plan):
    hosted_call(lambda: None, name=name, grid=(1,), in_specs=[], out_specs=[], out_shape=[], operands=[], plans=[plan])
    return plan.results


def pair_add(name, g, got, place):
    slots, rows, cols = g.shape
    half = rows // 2
    tr = _row_tile(half)
    nb = half // tr

    def kern(_, g_ref, t_ref, o_ref):
        o_ref[...] = (g_ref[...].astype(F32) + t_ref[...].astype(F32)).astype(o_ref.dtype)

    blk = pl.BlockSpec((None, tr, cols), lambda s, i, p: (s, i, 0))
    grid_spec = pltpu.PrefetchScalarGridSpec(
        num_scalar_prefetch=1, grid=(slots, nb),
        in_specs=[pl.BlockSpec((None, tr, cols), lambda s, i, p: (s, p[1] * nb + i, 0)), blk], out_specs=blk)
    return pl.pallas_call(kern, grid_spec=grid_spec, out_shape=SDS((slots, half, cols), g.dtype), name=name,
                          compiler_params=_cparams())(place, g, got)


def chip_add(name, p, q, place):
    slots, half, cols = p.shape
    tr = _row_tile(half)
    nb = half // tr

    def kern(_, p_ref, q1, q2, q3, o_ref):
        o_ref[...] = p_ref[...].astype(F32) + q1[...].astype(F32) + q2[...].astype(F32) + q3[...].astype(F32)

    def slot(k):
        return pl.BlockSpec((None, tr, cols), lambda i, pr: ((pr[0] + k) % slots, i, 0))

    grid_spec = pltpu.PrefetchScalarGridSpec(
        num_scalar_prefetch=1, grid=(nb,), in_specs=[slot(0), slot(1), slot(2), slot(3)],
        out_specs=pl.BlockSpec((tr, cols), lambda i, pr: (pr[1] * nb + i, 0)))
    return pl.pallas_call(kern, grid_spec=grid_spec, out_shape=SDS((2 * half, cols), F32), name=name,
                          compiler_params=_cparams())(place, p, q, q, q)


def pair_adds(tag, gs, gots, place):
    return [pair_add(f"{tag}_pair_add_{i}", g, got, place) for i, (g, got) in enumerate(zip(gs, gots))]


def chip_adds(tag, pairs, qs, place):
    return [chip_add(f"{tag}_chip_add_{i}", p, q, place) for i, (p, q) in enumerate(zip(pairs, qs))]


def reduce_scatter_chips(tag, gs, place):
    pairs = pair_adds(tag, gs, run_plan(tag + "_pair_exchange", plan_pair_exchange(gs)), place)
    return run_plan(tag + "_pair_join", plan_pair_join(chip_adds(tag, pairs, run_plan(tag + "_chip_scatter", plan_chip_scatter(pairs)), place)))


BIG = [("w_out", (2, 512, 1024)), ("w_mem_kv", (2, 256, 1024)), ("s5_w_in", (1, 1024, 1024)), ("s5_w_glu", (1, 1536, 768)),
       ("mla_w_in", (1, 1024, 848)), ("mla_w_uq", (1, 512, 576)), ("mla_w_ukv", (1, 256, 768))]
SHARDED_SMALL = [("mla_q_lora_norm", (1, 128)), ("mla_kv_lora_norm", (1, 64))]
SMALL = [("ln_gain", (2, 1024)), ("mem_norm", (2, 1024)), ("xq_norm", (2, 128)), ("xk_norm", (2, 128)),
         ("s5_lambda_re", (1, 96, 64)), ("s5_lambda_im", (1, 96, 64)), ("s5_log_step", (1, 96)),
         ("s5_b_re", (1, 96, 64, 16)), ("s5_b_im", (1, 96, 64, 16)), ("s5_c_re", (1, 96, 16, 64)), ("s5_c_im", (1, 96, 16, 64)),
         ("s5_d", (1, 1536)), ("mla_q_nope_norm", (1, 128)), ("mla_k_nope_norm", (1, 128)), ("mla_q_rope_norm", (1, 64)),
         ("mla_k_rope_norm", (1, 64))]
WEIGHT_ORDER = ["ln_gain", "w_out", "mem_norm", "w_mem_kv", "xq_norm", "xk_norm", "s5_w_in", "s5_lambda_re", "s5_lambda_im",
                "s5_log_step", "s5_b_re", "s5_b_im", "s5_c_re", "s5_c_im", "s5_d", "s5_w_glu", "mla_w_in", "mla_q_lora_norm",
                "mla_kv_lora_norm", "mla_w_uq", "mla_w_ukv", "mla_q_nope_norm", "mla_k_nope_norm", "mla_q_rope_norm", "mla_k_rope_norm"]
MINOR_LAST = {"mla_w_in": (0, 2, 1), "mla_w_uq": (0, 2, 1), "s5_b_re": (0, 2, 3, 1), "s5_b_im": (0, 2, 3, 1),
              "s5_c_re": (0, 2, 3, 1), "s5_c_im": (0, 2, 3, 1)}
SMALL_FULL = SMALL + [(n, (1, 4 * s[1])) for n, s in SHARDED_SMALL]
N_SMALL = sum(math.prod(s) for _, s in SMALL_FULL)
SMALL_ROWS, SMALL_LANES = 128, 1024

WIDE0_IN, WIDE0_OUT, WIDE0_MKV = 0, 1024, 1536
PAIR_OUT, PAIR_MKV, PAIR_ROWS = 0, 512, 768


def stack_shards(w, dtype):
    wide0 = jnp.concatenate([w["s5_w_in"][0], w["w_out"][0], w["w_mem_kv"][0]], axis=0)
    wide1 = jnp.concatenate([w["w_out"][1], w["w_mem_kv"][1]], axis=0)
    return ([wide0.astype(dtype), w["s5_w_glu"][0].astype(dtype)],
            [wide1.astype(dtype), w["mla_w_ukv"][0].astype(dtype), w["mla_w_in"][0].astype(dtype), w["mla_w_uq"][0].astype(dtype)])


def weight_views0(wide0, glu):
    return {"s5_w_in": Sharded(wide0, "col", WIDE0_IN, 1024), "w_out": Sharded(wide0, "row", WIDE0_OUT, 512),
            "w_mem_kv": Sharded(wide0, "row", WIDE0_MKV, 256), "s5_w_glu": Sharded(glu, "col", 0, 1536)}


def weight_views1(wide1, ukv):
    return {"w_out": Sharded(wide1, "row", PAIR_OUT, 512), "w_mem_kv": Sharded(wide1, "row", PAIR_MKV, 256), "mla_w_ukv": Sharded(ukv, "col", 0, 256)}


def grad_views():
    pair = SDS((4, PAIR_ROWS, 1024), BF16)
    return {"w_out": Sharded(pair, "row", PAIR_OUT, 512), "w_mem_kv": Sharded(pair, "row", PAIR_MKV, 256),
            "s5_w_in": Sharded(SDS((4, 1024, 1024), BF16), "col", 0, 1024), "s5_w_glu": Sharded(SDS((4, 1536, 768), BF16), "col", 0, 1536),
            "mla_w_ukv": Sharded(SDS((4, 256, 768), BF16), "col", 0, 256)}


def cols_to_shards(full):
    return full.reshape(full.shape[0], 4, full.shape[1] // 4).transpose(1, 0, 2)


def shards_to_cols(arr):
    return arr.transpose(1, 0, 2).reshape(arr.shape[1], 4 * arr.shape[2])


def mla_in_permute(w):
    o1, o2, o3, o4 = QL, QL + KVL, QL + KVL + ROPE, QL + KVL + ROPE + XQW
    return jnp.concatenate([w[:, o4:], w[:, :o1], w[:, o3:o4], w[:, o1:o2], w[:, o2:o3],
                            jnp.zeros((w.shape[0], MLA_IN_P - MLA_IN), w.dtype)], axis=1)


def mla_in_unpermute(d):
    return jnp.concatenate([d[:, 2048:2560], d[:, 3072:3328], d[:, 3328:3392], d[:, 2560:3072], d[:, :2048]], axis=1)


def uq_permute(w):
    w3 = w.reshape(w.shape[0], MH, NOPE + ROPE)
    return jnp.concatenate([w3[:, :, :NOPE].reshape(w.shape[0], MH * NOPE), w3[:, :, NOPE:].reshape(w.shape[0], MH * ROPE)], axis=1)


def uq_unpermute(d):
    dn = d[:, :MH * NOPE].reshape(d.shape[0], MH, NOPE)
    dr = d[:, MH * NOPE:].reshape(d.shape[0], MH, ROPE)
    return jnp.concatenate([dn, dr], axis=2).reshape(d.shape[0], MH * (NOPE + ROPE))


def time_permute(a):
    return a.reshape(SEG, SEG_LEN, a.shape[-1]).transpose(1, 0, 2).reshape(L, a.shape[-1])


def time_unpermute(a):
    return a.reshape(SEG_LEN, SEG, a.shape[-1]).transpose(1, 0, 2).reshape(L, a.shape[-1])


def mem_branch_fwd(tag, mem, mem_norm, w_mem_kv, xk_norm):
    mn = row_fwd(tag + "_mem_rms", fn_rms, ML, ML, [(mem, D, 0)], [mem_norm], [(D, BF16)])[0]
    kv = matmul(tag + "_mem_kv", mn, w_mem_kv, "nn")
    kn = row_fwd(tag + "_mem_knorm", fn_mem_k, ML, ML, [(kv, XQW, 0)], [xk_norm], [(XQW, F32)])[0]
    return mn, kv, kn


def mem_branch_bwd(tag, mem, mem_norm, w_mem_kv, xk_norm, mn, kv, dkn, dv, g_view, g_wide):
    dk, dxk = row_bwd(tag + "_mem_knorm_bwd", fn_mem_k, ML, ML, [(kv, XQW, 0)], [xk_norm], [(dkn, XQW, 0)], [True], [True])
    dkv = jnp.concatenate([dk, dv], axis=1)
    dmn = matmul(tag + "_mem_kv_dx", dkv, w_mem_kv, "nt")
    g_wide = matmul(tag + "_mem_kv_dw", mn, dkv, "tn", out=g_view, into=g_wide)
    dmem_norm = row_bwd(tag + "_mem_rms_bwd", fn_rms, ML, ML, [(mem, D, 0)], [mem_norm], [(dmn, D, 0)], [False], [True])[0]
    return g_wide, dmem_norm, dxk


def mem_attn_fwd(tag, proj, cb, kn, kv, xq_norm):
    return row_fwd(tag + "_mem_attn", fn_mem_attn, L, 256, [(proj, XQW, cb)], [kn, kv[:, XQW:], xq_norm], [(XQW, F32)])[0]


def mem_attn_bwd(tag, proj, cb, kn, kv, xq_norm, dmo, dproj):
    place = {"cols": proj.shape[1], "cb": cb, "into": dproj, "dtype": dproj.dtype}
    return row_bwd(tag + "_mem_attn_bwd", fn_mem_attn, L, 256, [(proj, XQW, cb)], [kn, kv[:, XQW:], xq_norm], [(dmo, XQW, 0)],
                   [place], [True, True, True])


def device_step(x, mem, positions, target, small, env, hooks=None):
    hooks = hooks or {}

    def plans_for(name):
        return hooks[("plans", name)](env) if ("plans", name) in hooks else ()

    def after(name):
        if ("after", name) in hooks:
            hooks[("after", name)](env)

    g = {}
    w0 = weight_views0(env["wide0"], env["glu"])
    gw = grad_views()
    ln, mem_norm, xq_norm, xk_norm = small["ln_gain"], small["mem_norm"], small["xq_norm"], small["xk_norm"]

    lre, lim = small["s5_lambda_re"][0], small["s5_lambda_im"][0]
    ls = small["s5_log_step"].reshape(SG, 1)
    one = pl.BlockSpec((SG, SP), lambda i: (0, 0))
    col = pl.BlockSpec((SG, 1), lambda i: (0, 0))
    disc_ins = [(lre, one), (lim, one), (ls, col)]
    a_re, a_im, coef_re, coef_im = stage("s5_disc", fn_s5_disc, (1,), disc_ins, [(SDS((SG, SP), F32), one)] * 4)
    b_re, b_im = small["s5_b_re"].reshape(SN, SC), small["s5_b_im"].reshape(SN, SC)
    c_re, c_im = small["s5_c_re"].reshape(PW, SP), small["s5_c_im"].reshape(PW, SP)
    bmat_rows = [(b_re, SC, 0), (b_im, SC, 0), (coef_re.reshape(SN, 1), 1, 0), (coef_im.reshape(SN, 1), 1, 0)]
    wb_re, wb_im = row_fwd("s5_bmat", fn_s5_bmat, SN, 512, bmat_rows, [], [(128, F32)] * 2)
    cmat_rows = [(c_re, SP, 0), (c_im, SP, 0)]
    wc_re, wc_im = row_fwd("s5_cmat", fn_s5_cmat, PW, 128, cmat_rows, [], [(512, F32)] * 2)
    a_re_v, a_im_v = a_re.reshape(1, SN), a_im.reshape(1, SN)
    s5_d = small["s5_d"]

    xp = time_permute(x)
    h0 = row_fwd("l0_rms", fn_rms, L, 256, [(xp, D, 0)], [ln[0:1]], [(D, BF16)])[0]
    proj0 = matmul("l0_in", h0, w0["s5_w_in"], "nn")
    s_re, s_im, g0 = s5_forward(proj0, wb_re, wb_im, wc_re, wc_im, a_re_v, a_im_v, s5_d, plans=plans_for("s5_forward"))
    z0 = matmul("l0_glu", g0, w0["s5_w_glu"], "nn", plans=plans_for("l0_glu"))
    after("l0_glu")
    mn0, kv0, kn0 = mem_branch_fwd("l0", mem, mem_norm[0:1], w0["w_mem_kv"], xk_norm[0:1])
    mo0 = mem_attn_fwd("l0", proj0, 3, kn0, kv0, xq_norm[0:1])
    o0 = row_fwd("l0_merge", fn_merge_glu, L, 256, [(z0, 2 * PW, 0), (mo0, XQW, 0), (proj0, BW, 1)], [], [(BW, BF16)])[0]
    x1p = matmul("l0_out", o0, w0["w_out"], "nn", add=xp)
    x1 = time_unpermute(x1p)

    w1 = weight_views1(env["wide1"], env["ukv"])
    w_in1, w_uq = env["w_in1"], env["w_uq"]
    h1 = row_fwd("l1_rms", fn_rms, L, 256, [(x1, D, 0)], [ln[1:2]], [(D, BF16)])[0]
    proj1 = matmul("l1_in", h1, w_in1, "nn")
    qln, kvln = env["q_lora_norm"].reshape(1, QL), env["kv_lora_norm"].reshape(1, KVL)
    cqn = row_fwd("l1_q_lora_rms", fn_rms, L, 256, [(proj1, QL, 4)], [qln], [(QL, BF16)])[0]
    ckvn = row_fwd("l1_kv_lora_rms", fn_rms, L, 256, [(proj1, KVL, 12)], [kvln], [(KVL, BF16)])[0]
    q = matmul("l1_uq", cqn, w_uq, "nn")
    kv = matmul("l1_ukv", ckvn, w1["mla_w_ukv"], "nn")
    inv_freq = ROPE_THETA ** (-jnp.arange(ROPE // 2, dtype=F32) / (ROPE // 2))
    ang = positions.astype(F32)[:, None] * inv_freq
    cos2 = jnp.tile(jnp.cos(ang), (1, 4))
    sin_signed = jnp.tile(jnp.concatenate([-jnp.sin(ang), jnp.sin(ang)], axis=1), (1, 2))
    qnn, knn = small["mla_q_nope_norm"], small["mla_k_nope_norm"]
    qrn, krn = jnp.tile(small["mla_q_rope_norm"], (1, 2)), jnp.tile(small["mla_k_rope_norm"], (1, 2))
    tp = 256
    prep_ins = [(q, rspec(tp, MH * (NOPE + ROPE))), (kv, rspec(tp, MH * 256)), (proj1, rspec(tp, 128, 26)),
                (cos2, rspec(tp, 128)), (sin_signed, rspec(tp, 128))] + [(a, cspec((1, 128))) for a in (qnn, knn, qrn, krn)]
    hq_spec = pl.BlockSpec((MH, tp, 256), lambda i: (0, i, 0))
    hv_spec = pl.BlockSpec((MH, tp, 128), lambda i: (0, i, 0))
    qf, kf, vh = stage("l1_mla_prep", fn_mla_prep, (L // tp,), prep_ins,
                       [(SDS((MH, L, 256), BF16), hq_spec), (SDS((MH, L, 256), BF16), hq_spec), (SDS((MH, L, 128), BF16), hv_spec)])
    attn = causal_attn(qf, kf, vh)
    mn1, kv1, kn1 = mem_branch_fwd("l1", mem, mem_norm[1:2], w1["w_mem_kv"], xk_norm[1:2])
    mo1 = mem_attn_fwd("l1", proj1, 5, kn1, kv1, xq_norm[1:2])
    o1 = row_fwd("l1_merge", fn_merge, L, 256, [(attn, PW, 0), (mo1, XQW, 0), (proj1, BW, 0)], [], [(BW, BF16)])[0]
    x2 = matmul("l1_out", o1, w1["w_out"], "nn", add=x1)
    dx2, loss = loss_and_grad(x2, target)

    do1 = matmul("l1_out_dx", dx2, w1["w_out"], "nt")
    g_pair1 = matmul("l1_out_dw", o1, dx2, "tn", out=gw["w_out"])
    dattn, dmo1, dproj1 = row_bwd("l1_merge_bwd", fn_merge, L, 256, [(attn, PW, 0), (mo1, XQW, 0), (proj1, BW, 0)], [],
                                  [(do1, BW, 0)], [True, True, {"cols": MLA_IN_P, "cb": 0, "dtype": BF16}], [])
    dproj1, dkn1, dv1, dxqn1 = mem_attn_bwd("l1", proj1, 5, kn1, kv1, xq_norm[1:2], dmo1, dproj1)
    env["g_pair1"], dmem_norm1, dxk1 = mem_branch_bwd("l1", mem, mem_norm[1:2], w1["w_mem_kv"], xk_norm[1:2], mn1, kv1, dkn1, dv1,
                                                      gw["w_mem_kv"], g_pair1)
    dqf, dkf, dvh = causal_attn_bwd(qf, kf, vh, dattn)
    prep_diffs = [("row", SDS((L, MH * (NOPE + ROPE)), BF16), rspec(tp, MH * (NOPE + ROPE))), ("row", SDS((L, MH * 256), BF16), rspec(tp, MH * 256)),
                  ("row", SDS((L, MLA_IN_P), BF16), rspec(tp, 128, 26), {"into": dproj1}), None, None] + [("acc", (0,))] * 4
    dq, dkv, dproj1, dqnn, dknn, dqrn, dkrn = stage_bwd("l1_mla_prep_bwd", fn_mla_prep, (L // tp,), prep_ins,
                                                        [(dqf, hq_spec), (dkf, hq_spec), (dvh, hv_spec)], prep_diffs)
    dcqn = matmul("l1_uq_dx", dq, w_uq, "nt")
    env["dw_uq"] = matmul("l1_uq_dw", cqn, dq, "tn")
    dckvn = matmul("l1_ukv_dx", dkv, w1["mla_w_ukv"], "nt")
    env["g_ukv"] = matmul("l1_ukv_dw", ckvn, dkv, "tn", out=gw["mla_w_ukv"])
    dproj1, dqln = row_bwd("l1_q_lora_rms_bwd", fn_rms, L, 256, [(proj1, QL, 4)], [qln], [(dcqn, QL, 0)],
                           [{"cols": MLA_IN_P, "cb": 4, "into": dproj1, "dtype": BF16}], [True])
    dproj1, dkvln = row_bwd("l1_kv_lora_rms_bwd", fn_rms, L, 256, [(proj1, KVL, 12)], [kvln], [(dckvn, KVL, 0)],
                            [{"cols": MLA_IN_P, "cb": 12, "into": dproj1, "dtype": BF16}], [True])
    dh1 = matmul("l1_in_dx", dproj1, w_in1, "nt")
    env["dw_in1"] = matmul("l1_in_dw", h1, dproj1, "tn")
    dx1, dln1 = row_bwd("l1_rms_bwd", fn_rms, L, 256, [(x1, D, 0)], [ln[1:2]], [(dh1, D, 0)], [{"add": (dx2, D, 0)}], [True],
                        plans=plans_for("l1_rms_bwd"))
    dx1p = time_permute(dx1)

    do0 = matmul("l0_out_dx", dx1p, w0["w_out"], "nt", plans=plans_for("l0_out_dx"))
    g_pair0 = matmul("l0_out_dw", o0, dx1p, "tn", out=gw["w_out"])
    dz0, dmo0, dproj0 = row_bwd("l0_merge_bwd", fn_merge_glu, L, 256, [(z0, 2 * PW, 0), (mo0, XQW, 0), (proj0, BW, 1)], [],
                                [(do0, BW, 0)], [{"dtype": BF16}, True, {"cols": 2 * BW, "cb": 1, "dtype": BF16}], [])
    dproj0, dkn0, dv0, dxqn0 = mem_attn_bwd("l0", proj0, 3, kn0, kv0, xq_norm[0:1], dmo0, dproj0)
    env["g_pair0"], dmem_norm0, dxk0 = mem_branch_bwd("l0", mem, mem_norm[0:1], w0["w_mem_kv"], xk_norm[0:1], mn0, kv0, dkn0, dv0,
                                                      gw["w_mem_kv"], g_pair0)
    env["g_glu"] = matmul("l0_glu_dw", g0, dz0, "tn", out=gw["s5_w_glu"], plans=plans_for("l0_glu_dw"))
    dg0 = matmul("l0_glu_dx", dz0, w0["s5_w_glu"], "nt", plans=plans_for("l0_glu_dx"))
    dproj0, dd, dwc_re, dwc_im, dwb_re, dwb_im, da_re, da_im = s5_backward(dg0, proj0, s_re, s_im, wb_re, wb_im, wc_re, wc_im,
                                                                           a_re_v, a_im_v, s5_d, dproj0, plans=plans_for("s5_backward"))
    env["g_in0"] = matmul("l0_in_dw", h0, dproj0, "tn", out=gw["s5_w_in"])
    dh0 = matmul("l0_in_dx", dproj0, w0["s5_w_in"], "nt", plans=plans_for("l0_in_dx"))
    dxp, dln0 = row_bwd("l0_rms_bwd", fn_rms, L, 256, [(xp, D, 0)], [ln[0:1]], [(dh0, D, 0)], [{"add": (dx1p, D, 0)}], [True])
    grad_x = time_unpermute(dxp)

    db_re, db_im, dcoef_re, dcoef_im = row_bwd("s5_bmat_bwd", fn_s5_bmat, SN, 512, bmat_rows, [], [(dwb_re, 128, 0), (dwb_im, 128, 0)],
                                               [True] * 4, [], plans=plans_for("s5_bmat_bwd"))
    dc_re, dc_im = row_bwd("s5_cmat_bwd", fn_s5_cmat, PW, 128, cmat_rows, [], [(dwc_re, 512, 0), (dwc_im, 512, 0)], [True] * 2, [],
                           plans=plans_for("s5_cmat_bwd"))
    disc_cts = [(da_re.reshape(SG, SP), one), (da_im.reshape(SG, SP), one), (dcoef_re.reshape(SG, SP), one), (dcoef_im.reshape(SG, SP), one)]
    dlre, dlim, dls = stage_bwd("s5_disc_bwd", fn_s5_disc, (1,), disc_ins, disc_cts, [("acc", (0,))] * 3)

    g["ln_gain"] = jnp.concatenate([dln0, dln1], axis=0)
    g["mem_norm"] = jnp.concatenate([dmem_norm0, dmem_norm1], axis=0)
    g["xq_norm"] = jnp.concatenate([dxqn0, dxqn1], axis=0)
    g["xk_norm"] = jnp.concatenate([dxk0, dxk1], axis=0)
    g["s5_lambda_re"], g["s5_lambda_im"], g["s5_log_step"] = dlre, dlim, dls
    g["s5_b_re"], g["s5_b_im"], g["s5_c_re"], g["s5_c_im"] = db_re, db_im, dc_re, dc_im
    g["s5_d"] = dd
    g["mla_q_lora_norm"], g["mla_kv_lora_norm"] = dqln, dkvln
    g["mla_q_nope_norm"], g["mla_k_nope_norm"] = dqnn, dknn
    g["mla_q_rope_norm"] = dqrn[:, :ROPE] + dqrn[:, ROPE:]
    g["mla_k_rope_norm"] = dkrn[:, :ROPE] + dkrn[:, ROPE:]
    return loss, grad_x, g


def kernel(x, mem, positions, ln_gain, w_out, mem_norm, w_mem_kv, xq_norm, xk_norm, s5_w_in, s5_lambda_re, s5_lambda_im, s5_log_step, s5_b_re, s5_b_im, s5_c_re, s5_c_im, s5_d, s5_w_glu, mla_w_in, mla_q_lora_norm, mla_kv_lora_norm, mla_w_uq, mla_w_ukv, mla_q_nope_norm, mla_k_nope_norm, mla_q_rope_norm, mla_k_rope_norm, loss_target, m_ln_gain, m_w_out, m_mem_norm, m_w_mem_kv, m_xq_norm, m_xk_norm, m_s5_w_in, m_s5_lambda_re, m_s5_lambda_im, m_s5_log_step, m_s5_b_re, m_s5_b_im, m_s5_c_re, m_s5_c_im, m_s5_d, m_s5_w_glu, m_mla_w_in, m_mla_q_lora_norm, m_mla_kv_lora_norm, m_mla_w_uq, m_mla_w_ukv, m_mla_q_nope_norm, m_mla_k_nope_norm, m_mla_q_rope_norm, m_mla_k_rope_norm, v_ln_gain, v_w_out, v_mem_norm, v_w_mem_kv, v_xq_norm, v_xk_norm, v_s5_w_in, v_s5_lambda_re, v_s5_lambda_im, v_s5_log_step, v_s5_b_re, v_s5_b_im, v_s5_c_re, v_s5_c_im, v_s5_d, v_s5_w_glu, v_mla_w_in, v_mla_q_lora_norm, v_mla_kv_lora_norm, v_mla_w_uq, v_mla_w_ukv, v_mla_q_nope_norm, v_mla_k_nope_norm, v_mla_q_rope_norm, v_mla_k_rope_norm):
    args = dict(locals())
    wts = {n: args[n] for n in WEIGHT_ORDER}
    mom = {n: args["m_" + n] for n in WEIGHT_ORDER}
    var = {n: args["v_" + n] for n in WEIGHT_ORDER}

    chip = 2 * lax.axis_index("x") + lax.axis_index("y")
    place = jnp.stack([chip, lax.axis_index("c")]).astype(jnp.int32)

    def own_slot(gathered, shards):
        return [lax.dynamic_update_slice(g, s[None], (chip, 0, 0)) for g, s in zip(gathered, shards)]

    shards0, shards1 = stack_shards(wts, BF16)
    shards1.append(jnp.concatenate([mla_q_lora_norm, jnp.pad(mla_kv_lora_norm, ((0, 0), (0, 64))), jnp.zeros((14, 128), F32)], axis=0))
    env = dict(zip(("wide0", "glu"), own_slot(all_gather_chips("gather_weights0", shards0), shards0)))
    over_ici = plan_gather_ici(shards1)
    hooks = {("plans", "s5_forward"): lambda env: [over_ici]}
    passed_on = []

    def pass_on(env):
        passed_on.append(plan_gather_pass(over_ici.results))
        return passed_on

    def layer1_weights(env):
        wide1, ukv, in1, uq, norms = own_slot(passed_on[0].results, shards1)
        env.update(wide1=wide1, ukv=ukv, w_in1=mla_in_permute(shards_to_cols(in1)), w_uq=uq_permute(shards_to_cols(uq)),
                   q_lora_norm=norms[:, 0, :], kv_lora_norm=norms[:, 1, :64])

    hooks["plans", "l0_glu"], hooks["after", "l0_glu"] = pass_on, layer1_weights

    rs = {}

    def swap(k, gs):
        rs[k, "g"], rs[k, "swap"] = gs, plan_pair_exchange(gs)
        return rs[k, "swap"]

    def scatter(k, part=slice(None)):
        if (k, "pairs") not in rs:
            rs[k, "pairs"], rs[k, "scatter"] = pair_adds(f"rs{k}", rs[k, "g"], rs[k, "swap"].results, place), []
        rs[k, "scatter"].append(plan_chip_scatter(rs[k, "pairs"][part]))
        return rs[k, "scatter"][-1]

    def join(k):
        rs[k, "join"] = plan_pair_join(chip_adds(f"rs{k}", rs[k, "pairs"], [q for p in rs[k, "scatter"] for q in p.results], place))
        return rs[k, "join"]

    hooks["plans", "l1_rms_bwd"] = lambda env: [swap(1, [env["g_pair1"], env["g_ukv"], cols_to_shards(mla_in_unpermute(env["dw_in1"])).astype(BF16),
                                                          cols_to_shards(uq_unpermute(env["dw_uq"])).astype(BF16)])]
    hooks["plans", "l0_glu_dw"] = lambda env: [scatter(1, slice(0, 2))]
    hooks["plans", "l0_glu_dx"] = lambda env: [scatter(1, slice(2, 4)), swap(0, [env["g_pair0"], env["g_glu"]])]
    hooks["plans", "s5_backward"] = lambda env: [scatter(0), join(1)]
    hooks["plans", "l0_in_dx"] = lambda env: [swap(2, [env["g_in0"]]), join(0)]
    hooks["plans", "s5_bmat_bwd"] = lambda env: [scatter(2)]
    hooks["plans", "s5_cmat_bwd"] = lambda env: [join(2)]

    small = {n: wts[n] for n, _ in SMALL}
    loss, grad_x, g = device_step(x[0], mem[0], positions[0], loss_target[0], small, env, hooks)
    loss = lax.psum(loss[0, 0], MESH_AXES)
    (r_pair1, r_ukv, r_in1, r_uq), (r_pair0, r_glu), (r_in0,) = (rs[k, "join"].results for k in (1, 0, 2))

    small_flat = jnp.concatenate([g[n].reshape(-1) for n, _ in SMALL_FULL])
    g_small = jnp.pad(small_flat, (0, 4 * SMALL_ROWS * SMALL_LANES - N_SMALL)).astype(BF16).reshape(4, SMALL_ROWS, SMALL_LANES)
    r_small = reduce_scatter_chips("rs3", [g_small], place)[0]
    small_all = own_slot(all_gather_chips("gather_small_grads", [r_small]), [r_small])[0].reshape(-1)[:N_SMALL]

    grads = {"w_out": jnp.stack([r_pair0[:PAIR_MKV], r_pair1[:PAIR_MKV]]), "w_mem_kv": jnp.stack([r_pair0[PAIR_MKV:], r_pair1[PAIR_MKV:]]),
             "s5_w_in": r_in0[None], "s5_w_glu": r_glu[None], "mla_w_ukv": r_ukv[None], "mla_w_in": r_in1[None], "mla_w_uq": r_uq[None]}
    off = 0
    for n, s in SMALL_FULL:
        grads[n] = small_all[off:off + math.prod(s)].reshape(s)
        off += math.prod(s)
    for n, s in SHARDED_SMALL:
        grads[n] = lax.dynamic_slice(grads[n], (0, chip * s[1]), s)

    delta, new_m, new_v = {}, {}, {}
    for n, s in BIG + [(n, s) for n, s in SMALL if len(s) == 4]:
        perm = MINOR_LAST.get(n, tuple(range(len(s))))
        turned = tuple(s[p] for p in perm)
        view = lambda a: jnp.transpose(a, perm).reshape(-1, turned[-1])
        res = adamw("adamw_" + n, view(wts[n]), view(grads[n]), view(mom[n]), view(var[n]))
        delta[n], new_m[n], new_v[n] = (jnp.transpose(r.reshape(turned), tuple(perm.index(i) for i in range(len(s)))) for r in res)
    small_names = [n for n, s in SMALL if len(s) < 4] + [n for n, _ in SHARDED_SMALL]
    n_own = sum(wts[n].size for n in small_names)
    rows_own = -(-n_own // (8 * 128)) * 8

    def pack_small(d):
        flat = jnp.concatenate([d[n].reshape(-1) for n in small_names])
        return jnp.pad(flat, (0, rows_own * 128 - n_own), constant_values=1.0).reshape(rows_own, 128)

    res = adamw("adamw_small", pack_small(wts), pack_small(grads), pack_small(mom), pack_small(var))
    off = 0
    for n in small_names:
        size = wts[n].size
        delta[n], new_m[n], new_v[n] = (r.reshape(-1)[off:off + size].reshape(wts[n].shape) for r in res)
        off += size

    return (loss, grad_x[None], *[grads[n] for n in WEIGHT_ORDER], *[delta[n] for n in WEIGHT_ORDER],
            *[new_m[n] for n in WEIGHT_ORDER], *[new_v[n] for n in WEIGHT_ORDER])
```

```python
import functools
import math

import jax
import jax.numpy as jnp
from jax import lax
from jax.experimental import pallas as pl
from jax.experimental.pallas import tpu as pltpu

F32, BF16 = jnp.float32, jnp.bfloat16
SDS = jax.ShapeDtypeStruct

D = 1024
L = 2048
ML = 256
BW = 2 * D
XQW = BW // 4
PW = BW - XQW
XH, XHD = 4, 128
SG, SC, SP = 96, 16, 64
SN = SG * SP
NOPE, ROPE, VD = 128, 64, 128
MH = 12
QL, KVL = 512, 256
EPS = 1e-6
ROPE_THETA = 10000.0
MLA_IN = QL + KVL + ROPE + XQW + BW
MLA_IN_P = 3456
ADAM_LR, ADAM_B1, ADAM_B2, ADAM_EPS, ADAM_WD, ADAM_STEP = 0.001, 0.9, 0.999, 1e-08, 0.01, 10

VMEM_LIMIT = 48 * 2**20
SEG = 8
SEG_LEN = L // SEG
MESH_AXES = ("x", "y", "c")


def _cparams():
    return pltpu.CompilerParams(vmem_limit_bytes=VMEM_LIMIT)


def _dg(a, b, ca, cb):
    return lax.dot_general(a.astype(BF16), b.astype(BF16), (((ca,), (cb,)), ((), ())), preferred_element_type=F32)


@jax.custom_vjp
def mm_nn(a, b):
    return _dg(a, b, 1, 0)


mm_nn.defvjp(lambda a, b: (_dg(a, b, 1, 0), (a, b)), lambda res, g: (_dg(g, res[1], 1, 1), _dg(res[0], g, 0, 0)))


@jax.custom_vjp
def mm_nt(a, b):
    return _dg(a, b, 1, 1)


mm_nt.defvjp(lambda a, b: (_dg(a, b, 1, 1), (a, b)), lambda res, g: (_dg(g, res[1], 1, 0), _dg(g, res[0], 0, 0)))


@functools.partial(jax.custom_vjp, nondiff_argnums=(1,))
def lane_roll(x, shift):
    return pltpu.roll(x, shift, 1)


lane_roll.defvjp(lambda x, shift: (pltpu.roll(x, shift, 1), None),
                 lambda shift, _, g: (pltpu.roll(g, (128 - shift) % 128, 1),))


def rms(x, g):
    return x * lax.rsqrt(jnp.mean(x * x, axis=-1, keepdims=True) + EPS) * g


@jax.custom_vjp
def softmax_rows(s):
    e = jnp.exp(s - jnp.max(s, axis=-1, keepdims=True))
    return e / jnp.sum(e, axis=-1, keepdims=True)


def _softmax_rows_fwd(s):
    p = softmax_rows(s)
    return p, p


def _softmax_rows_bwd(p, g):
    return (p * (g - jnp.sum(g * p, axis=-1, keepdims=True)),)


softmax_rows.defvjp(_softmax_rows_fwd, _softmax_rows_bwd)


def silu(x):
    return x * jax.nn.sigmoid(x)


ANY = pl.BlockSpec(memory_space=pl.ANY)
MESH_ID = pl.DeviceIdType.MESH


def _dma_sems(n):
    return [pltpu.SemaphoreType.DMA((n,)), pltpu.SemaphoreType.DMA((n,))]


class Plan:
    def __init__(self, operands, out_shape, aliases, n_sems, copies):
        self.operands, self.out_shape, self.aliases, self.n_sems, self.copies = list(operands), list(out_shape), aliases, n_sems, copies
        self.results = None


def hosted_call(kern, *, name, grid, in_specs, out_specs, out_shape, operands, scratch_shapes=(), aliases=None, cparams=None, plans=()):
    n_in, n_out, n_scr = len(in_specs), len(out_specs), len(scratch_shapes)
    p_in, p_out = [len(p.operands) for p in plans], [len(p.out_shape) for p in plans]
    all_aliases = dict(aliases or {})
    in_off, out_off = n_in, n_out
    for p, ni, no in zip(plans, p_in, p_out):
        all_aliases.update({in_off + i: out_off + o for i, o in p.aliases.items()})
        in_off, out_off = in_off + ni, out_off + no

    def body(*refs):
        pos, pins, pouts = n_in, [], []
        for ni in p_in:
            pins.append(refs[pos:pos + ni])
            pos += ni
        main_out = refs[pos:pos + n_out]
        pos += n_out
        for no in p_out:
            pouts.append(refs[pos:pos + no])
            pos += no
        main_scr = refs[pos:pos + n_scr]
        pos += n_scr
        if plans:
            ids = [pl.program_id(ax) for ax in range(len(grid))]
            first = functools.reduce(jnp.logical_and, [i == 0 for i in ids])
            last = functools.reduce(jnp.logical_and, [i == g - 1 for i, g in zip(ids, grid)])
            copies = [p.copies(pins[k], pouts[k], refs[pos + 2 * k], refs[pos + 2 * k + 1]) for k, p in enumerate(plans)]

            @pl.when(first)
            def _():
                for sends, _ in copies:
                    for cp in sends:
                        cp.start()

        kern(*refs[:n_in], *main_out, *main_scr)
        if plans:
            @pl.when(last)
            def _():
                for sends, recvs in copies:
                    for cp in recvs:
                        cp.wait_recv()
                    for cp in sends:
                        cp.wait_send()

    res = pl.pallas_call(body, grid=grid, in_specs=list(in_specs) + [ANY] * sum(p_in), out_specs=list(out_specs) + [ANY] * sum(p_out),
                         out_shape=list(out_shape) + [s for p in plans for s in p.out_shape],
                         scratch_shapes=list(scratch_shapes) + [s for p in plans for s in _dma_sems(p.n_sems)],
                         input_output_aliases=all_aliases, name=name, compiler_params=cparams or _cparams())(
        *operands, *[a for p in plans for a in p.operands])
    pos = n_out
    for p, no in zip(plans, p_out):
        p.results = list(res[pos:pos + no])
        pos += no
    return list(res[:n_out])


def stage(name, fn, grid, ins, outs):
    n_in = len(ins)

    def kern(*refs):
        res = fn(*[r[...] for r in refs[:n_in]])
        for r, v in zip(refs[n_in:], res):
            r[...] = v.astype(r.dtype)

    return pl.pallas_call(kern, grid=grid, in_specs=[s for _, s in ins], out_specs=[s for _, s in outs],
                          out_shape=[sd for sd, _ in outs], name=name, compiler_params=_cparams())(*[a for a, _ in ins])


def stage_bwd(name, fn, grid, ins, cts, diffs, plans=()):
    n_in, n_ct = len(ins), len(cts)
    didx = [i for i, d in enumerate(diffs) if d is not None]
    opts = {i: (diffs[i][3] if len(diffs[i]) > 3 else {}) for i in didx if diffs[i][0] == "row"}
    adds = [(i, opts[i]["add"]) for i in opts if "add" in opts[i]]
    intos = [(i, opts[i]["into"]) for i in opts if "into" in opts[i]]
    n_add, n_into = len(adds), len(intos)
    add_pos = {i: n_in + n_ct + k for k, (i, _) in enumerate(adds)}
    n_extra = n_in + n_ct + n_add + n_into

    def kern(*refs):
        vals = [r[...] for r in refs[:n_in]]

        def f(*dv):
            full = list(vals)
            for i, v in zip(didx, dv):
                full[i] = v
            return fn(*full)

        _, vjp = jax.vjp(f, *[vals[i].astype(F32) for i in didx])
        gs = vjp(tuple(c[...].astype(F32) for c in refs[n_in:n_in + n_ct]))
        for o_ref, i, g in zip(refs[n_extra:], didx, gs):
            if diffs[i][0] == "row":
                if i in add_pos:
                    g = g + refs[add_pos[i]][...].astype(F32)
                o_ref[...] = g.astype(o_ref.dtype)
            else:
                first = functools.reduce(jnp.logical_and, [pl.program_id(ax) == 0 for ax in diffs[i][1]])

                @pl.when(first)
                def _():
                    o_ref[...] = g

                @pl.when(jnp.logical_not(first))
                def _():
                    o_ref[...] += g

    out_shape, out_specs = [], []
    for i in didx:
        if diffs[i][0] == "row":
            out_shape.append(diffs[i][1])
            out_specs.append(diffs[i][2])
        else:
            out_shape.append(SDS(ins[i][0].shape, F32))
            out_specs.append(ins[i][1])
    aliases = {n_in + n_ct + n_add + k: didx.index(i) for k, (i, _) in enumerate(intos)}
    in_specs = [s for _, s in ins] + [s for _, s in cts] + [s for _, (_, s) in adds] + [ANY] * n_into
    operands = [a for a, _ in ins] + [a for a, _ in cts] + [a for _, (a, _) in adds] + [a for _, a in intos]
    return hosted_call(kern, name=name, grid=grid, in_specs=in_specs, out_specs=out_specs, out_shape=out_shape, operands=operands,
                       aliases=aliases, plans=plans)


def rspec(tl, w, cb=0):
    return pl.BlockSpec((tl, w), lambda i: (i, cb))


def cspec(shape):
    return pl.BlockSpec(shape, lambda i: (0,) * len(shape))


def row_fwd(name, fn, rows, tl, row_ins, consts, outs):
    ins = [(a, rspec(tl, w, cb)) for a, w, cb in row_ins] + [(a, cspec(a.shape)) for a in consts]
    return stage(name, fn, (rows // tl,), ins, [(SDS((rows, w), dt), rspec(tl, w)) for w, dt in outs])


def row_bwd(name, fn, rows, tl, row_ins, consts, cts, row_diff, const_diff, plans=()):
    ins = [(a, rspec(tl, w, cb)) for a, w, cb in row_ins] + [(a, cspec(a.shape)) for a in consts]
    diffs = []
    for (a, w, cb), d in zip(row_ins, row_diff):
        if not d:
            diffs.append(None)
            continue
        d = d if isinstance(d, dict) else {}
        opts = {}
        if "add" in d:
            opts["add"] = (d["add"][0], rspec(tl, d["add"][1], d["add"][2]))
        if d.get("into") is not None:
            opts["into"] = d["into"]
        diffs.append(("row", SDS((rows, d.get("cols", w)), d.get("dtype", F32)), rspec(tl, w, d.get("cb", 0)), opts))
    diffs += [("acc", (0,)) if d else None for d in const_diff]
    return stage_bwd(name, fn, (rows // tl,), ins, [(a, rspec(tl, w, cb)) for a, w, cb in cts], diffs, plans=plans)


MATMUL_VMEM = 36 * 2**20
ADAMW_VMEM = 28 * 2**20


class Sharded:
    def __init__(self, arr, kind, roff, rows):
        self.arr, self.kind, self.roff, self.rows, self.n = arr, kind, roff, rows, arr.shape[2]
        self.shape = (rows, 4 * self.n) if kind == "col" else (4 * rows, self.n)

    def fits(self, t0, t1):
        return self.roff % t0 == 0 and self.rows % t0 == 0 and self.n % t1 == 0

    def spec(self, t0, t1, bidx):
        assert self.fits(t0, t1), (self.kind, self.roff, self.rows, self.n, t0, t1)
        r0 = self.roff // t0
        if self.kind == "col":
            per = self.n // t1
            return pl.BlockSpec((None, t0, t1), lambda *g: (bidx(*g)[1] // per, r0 + bidx(*g)[0], bidx(*g)[1] % per))
        per = self.rows // t0
        return pl.BlockSpec((None, t0, t1), lambda *g: (bidx(*g)[0] // per, r0 + bidx(*g)[0] % per, bidx(*g)[1]))


def matmul(name, a, b, mode, out_dtype=F32, add=None, out=None, into=None, plans=()):
    if mode == "tn":
        k_dim, m = a.shape
    else:
        m, k_dim = a.shape
    n = b.shape[0] if mode == "nt" else b.shape[1]
    b_fit = b.fits if isinstance(b, Sharded) else (lambda t0, t1: True)
    o_fit = out.fits if out is not None else (lambda t0, t1: True)
    a_bytes, b_bytes = jnp.dtype(a.dtype).itemsize, jnp.dtype(b.arr.dtype if isinstance(b, Sharded) else b.dtype).itemsize
    o_bytes = jnp.dtype(out_dtype if out is None else out.arr.dtype).itemsize

    def vmem(tm, tn, tk):
        return 2 * (tm * tk * a_bytes + tk * tn * b_bytes + tm * tn * (o_bytes + (4 if add is not None else 0))) + 4 * tm * tn * (1 + (tk < k_dim))

    tiles = [(tm, tn, tk) for tm in (2048, 1024, 512, 256, 128) for tn in (1024, 768, 512, 384, 256, 128)
             for tk in sorted({k_dim, 1024, 768, 512, 384, 256, 128})
             if m % tm == 0 and n % tn == 0 and k_dim % tk == 0 and (b_fit(tn, tk) if mode == "nt" else b_fit(tk, tn)) and o_fit(tm, tn)
             and vmem(tm, tn, tk) <= MATMUL_VMEM]
    tm, tn, tk = max(tiles, key=lambda t: (t[2] == k_dim, t[0] * t[1] * t[2], t[0] * t[1]))
    nk = k_dim // tk
    a_spec = pl.BlockSpec((tk, tm), lambda i, j, k: (k, i)) if mode == "tn" else pl.BlockSpec((tm, tk), lambda i, j, k: (i, k))
    if isinstance(b, Sharded):
        b_spec = b.spec(tn, tk, lambda i, j, k: (j, k)) if mode == "nt" else b.spec(tk, tn, lambda i, j, k: (k, j))
        b = b.arr
    else:
        b_spec = pl.BlockSpec((tn, tk), lambda i, j, k: (j, k)) if mode == "nt" else pl.BlockSpec((tk, tn), lambda i, j, k: (k, j))
    o_spec = pl.BlockSpec((tm, tn), lambda i, j, k: (i, j))
    out_spec, out_shape = (o_spec, SDS((m, n), out_dtype)) if out is None else (out.spec(tm, tn, lambda i, j, k: (i, j)), out.arr)
    ca, cb = {"nn": (1, 0), "nt": (1, 1), "tn": (0, 0)}[mode]
    n_in = 2 + (add is not None)

    def finish(refs, o_ref, r):
        if add is not None:
            r = r + refs[2][...]
        o_ref[...] = r.astype(o_ref.dtype)

    def kern_whole(*refs):
        finish(refs, refs[-1], _dg(refs[0][...], refs[1][...], ca, cb))

    def kern_cut(*refs):
        o_ref, acc = refs[-2], refs[-1]
        k = pl.program_id(2)

        @pl.when(k == 0)
        def _():
            acc[...] = jnp.zeros_like(acc)

        acc[...] += _dg(refs[0][...], refs[1][...], ca, cb)

        @pl.when(k == nk - 1)
        def _():
            finish(refs, o_ref, acc[...])

    ins, specs = [a, b], [a_spec, b_spec]
    if add is not None:
        ins.append(add)
        specs.append(o_spec)
    if into is not None:
        ins.append(into)
        specs.append(ANY)
    return hosted_call(kern_whole if nk == 1 else kern_cut, name=name, grid=(m // tm, n // tn, nk), in_specs=specs, out_specs=[out_spec],
                       out_shape=[out_shape], operands=ins, scratch_shapes=[] if nk == 1 else [pltpu.VMEM((tm, tn), F32)],
                       aliases={} if into is None else {n_in: 0}, plans=plans)[0]


def _cmul(ar, ai, br, bi):
    return ar * br - ai * bi, ar * bi + ai * br


def _sub_shift(x, down):
    row = lax.broadcasted_iota(jnp.int32, x.shape, 0)
    if down:
        return jnp.where(row == 0, 0.0, pltpu.roll(x, 1, 0))
    return jnp.where(row == SEG - 1, 0.0, pltpu.roll(x, SEG - 1, 0))


def _pow_seg_len(ar, ai):
    for _ in range(int(math.log2(SEG_LEN))):
        ar, ai = _cmul(ar, ai, ar, ai)
    return ar, ai


def _scan_in_place(sr, si, a_re, a_im):
    lanes = sr.shape[1]
    ar = jnp.broadcast_to(a_re, (SEG, lanes))
    ai = jnp.broadcast_to(a_im, (SEG, lanes))
    zero = jnp.zeros((SEG, lanes), F32)

    def local(i, carry):
        rows = pl.ds(pl.multiple_of(i * SEG, SEG), SEG)
        mr, mi = _cmul(ar, ai, carry[0], carry[1])
        nr, ni = mr + sr[rows, :], mi + si[rows, :]
        sr[rows, :] = nr
        si[rows, :] = ni
        return nr, ni

    fr, fi = lax.fori_loop(0, SEG_LEN, local, (zero, zero))
    pr, pi = _pow_seg_len(ar, ai)
    ir, ii = zero, zero
    for _ in range(SEG - 1):
        mr, mi = _cmul(pr, pi, ir, ii)
        ir, ii = _sub_shift(mr + fr, True), _sub_shift(mi + fi, True)

    def carry_in(i, pw):
        rows = pl.ds(pl.multiple_of(i * SEG, SEG), SEG)
        cr, ci = _cmul(pw[0], pw[1], ir, ii)
        sr[rows, :] += cr
        si[rows, :] += ci
        return _cmul(pw[0], pw[1], ar, ai)

    lax.fori_loop(0, SEG_LEN, carry_in, (ar, ai))


S5_LANES = 8 * SP
S5_BLOCKS = SN // S5_LANES


def _s5_specs():
    u_spec = pl.BlockSpec((L, 8 * SC), lambda j: (0, j))
    s_spec = pl.BlockSpec((L, S5_LANES), lambda j: (0, j))
    wb_spec = pl.BlockSpec((S5_LANES, 8 * SC), lambda j: (j, 0))
    wc_spec = pl.BlockSpec((8 * SC, S5_LANES), lambda j: (j, 0))
    a_spec = pl.BlockSpec((1, S5_LANES), lambda j: (0, j))
    d_spec = pl.BlockSpec((1, 8 * SC), lambda j: (0, j))
    return u_spec, s_spec, wb_spec, wc_spec, a_spec, d_spec


def s5_forward(proj, wb_re, wb_im, wc_re, wc_im, a_re, a_im, d, plans=()):
    def kern(u_ref, wbr, wbi, wcr, wci, ar, ai, d_ref, sr, si, g_ref):
        u = u_ref[...]
        sr[...], si[...] = fn_s5_bu(u, wbr[...], wbi[...])
        _scan_in_place(sr, si, ar[...], ai[...])
        g_ref[...] = fn_s5_out(sr[...], si[...], u, d_ref[...], wcr[...], wci[...])[0].astype(g_ref.dtype)

    u_spec, s_spec, wb_spec, wc_spec, a_spec, d_spec = _s5_specs()
    return hosted_call(kern, name="s5_forward", grid=(S5_BLOCKS,), in_specs=[u_spec, wb_spec, wb_spec, wc_spec, wc_spec, a_spec, a_spec, d_spec],
                       out_specs=[s_spec, s_spec, u_spec], out_shape=[SDS((L, SN), F32)] * 2 + [SDS((L, PW), BF16)],
                       operands=[proj, wb_re, wb_im, wc_re, wc_im, a_re, a_im, d], plans=plans)


def _adjoint_scan_in_place(lr, li, sr, si, a_re, a_im):
    lanes = lr.shape[1]
    ar = jnp.broadcast_to(a_re, (SEG, lanes))
    ai = -jnp.broadcast_to(a_im, (SEG, lanes))
    zero = jnp.zeros((SEG, lanes), F32)

    def local(k, carry):
        i = SEG_LEN - 1 - k
        rows = pl.ds(pl.multiple_of(i * SEG, SEG), SEG)
        mr, mi = _cmul(ar, ai, carry[0], carry[1])
        nr, ni = mr + lr[rows, :], mi + li[rows, :]
        lr[rows, :] = nr
        li[rows, :] = ni
        return nr, ni

    fr, fi = lax.fori_loop(0, SEG_LEN, local, (zero, zero))
    pr, pi = _pow_seg_len(ar, ai)
    ir, ii = zero, zero
    for _ in range(SEG - 1):
        mr, mi = _cmul(pr, pi, ir, ii)
        ir, ii = _sub_shift(mr + fr, False), _sub_shift(mi + fi, False)

    def fix(rows, pw):
        cr, ci = _cmul(pw[0], pw[1], ir, ii)
        tr, ti = lr[rows, :] + cr, li[rows, :] + ci
        lr[rows, :] = tr
        li[rows, :] = ti
        return tr, ti

    def grad_a(tr, ti, spr, spi, acc):
        return acc[0] + tr * spr + ti * spi, acc[1] + ti * spr - tr * spi

    def carry_in(k, c):
        i = SEG_LEN - 1 - k
        rows = pl.ds(pl.multiple_of(i * SEG, SEG), SEG)
        prev = pl.ds(pl.multiple_of((i - 1) * SEG, SEG), SEG)
        tr, ti = fix(rows, (c[0], c[1]))
        acc = grad_a(tr, ti, sr[prev, :], si[prev, :], (c[2], c[3]))
        nr, ni = _cmul(c[0], c[1], ar, ai)
        return nr, ni, acc[0], acc[1]

    pwr, pwi, accr, acci = lax.fori_loop(0, SEG_LEN - 1, carry_in, (ar, ai, zero, zero))
    tr, ti = fix(pl.ds(0, SEG), (pwr, pwi))
    last = pl.ds((SEG_LEN - 1) * SEG, SEG)
    accr, acci = grad_a(tr, ti, _sub_shift(sr[last, :], True), _sub_shift(si[last, :], True), (accr, acci))
    return jnp.sum(accr, axis=0, keepdims=True), jnp.sum(acci, axis=0, keepdims=True)


S5_BWD_VMEM = 58 * 2**20


def s5_backward(dg, proj, s_re, s_im, wb_re, wb_im, wc_re, wc_im, a_re, a_im, d, dproj, plans=()):
    def kern(dg_ref, u_ref, sr, si, wbr, wbi, wcr, wci, ar, ai, d_ref, _, du_ref, dd_ref, dwcr, dwci, dwbr, dwbi, dar, dai, lr, li):
        u = u_ref[...]
        _, vjp_out = jax.vjp(fn_s5_out, sr[...], si[...], u, d_ref[...], wcr[...], wci[...])
        lr[...], li[...], du_out, dd_ref[...], dwcr[...], dwci[...] = vjp_out((dg_ref[...],))
        dar[...], dai[...] = _adjoint_scan_in_place(lr, li, sr, si, ar[...], ai[...])
        _, vjp_in = jax.vjp(fn_s5_bu, u, wbr[...], wbi[...])
        du_in, dwbr[...], dwbi[...] = vjp_in((lr[...], li[...]))
        du_ref[...] = (du_out + du_in).astype(du_ref.dtype)

    u_spec, s_spec, wb_spec, wc_spec, a_spec, d_spec = _s5_specs()
    outs = [(SDS(dproj.shape, dproj.dtype), u_spec), (SDS(d.shape, F32), d_spec), (SDS(wc_re.shape, F32), wc_spec), (SDS(wc_im.shape, F32), wc_spec),
            (SDS(wb_re.shape, F32), wb_spec), (SDS(wb_im.shape, F32), wb_spec), (SDS(a_re.shape, F32), a_spec), (SDS(a_im.shape, F32), a_spec)]
    return hosted_call(kern, name="s5_backward", grid=(S5_BLOCKS,),
                       in_specs=[u_spec, u_spec, s_spec, s_spec, wb_spec, wb_spec, wc_spec, wc_spec, a_spec, a_spec, d_spec, ANY],
                       out_specs=[sp for _, sp in outs], out_shape=[sd for sd, _ in outs], aliases={11: 0},
                       operands=[dg, proj, s_re, s_im, wb_re, wb_im, wc_re, wc_im, a_re, a_im, d, dproj],
                       scratch_shapes=[pltpu.VMEM((L, S5_LANES), F32)] * 2,
                       cparams=pltpu.CompilerParams(vmem_limit_bytes=S5_BWD_VMEM), plans=plans)


def fn_rms(x, g):
    return (rms(x, g),)


def fn_s5_disc(lre, lim, ls):
    step = jnp.exp(ls)
    e = jnp.exp(lre * step)
    a_re, a_im = e * jnp.cos(lim * step), e * jnp.sin(lim * step)
    den = lre * lre + lim * lim
    nr, ni = a_re - 1.0, a_im
    return a_re, a_im, (nr * lre + ni * lim) / den, (ni * lre - nr * lim) / den


def _group_mask(rows, cols, row_div, col_div):
    r = lax.broadcasted_iota(jnp.int32, (rows, cols), 0) // row_div % 8
    c = lax.broadcasted_iota(jnp.int32, (rows, cols), 1) // col_div
    return r == c


def _spread(x, mask):
    w = x.shape[1]
    copy = (lax.broadcasted_iota(jnp.int32, (w, 8 * w), 1) % w == lax.broadcasted_iota(jnp.int32, (w, 8 * w), 0)).astype(F32)
    return jnp.where(mask, jnp.dot(x, copy, precision=lax.Precision.HIGHEST, preferred_element_type=F32), 0.0)


def fn_s5_bmat(b_re, b_im, coef_re, coef_im):
    mask = _group_mask(b_re.shape[0], 8 * SC, SP, SC)
    return _spread(coef_re * b_re - coef_im * b_im, mask), _spread(coef_re * b_im + coef_im * b_re, mask)


def fn_s5_cmat(c_re, c_im):
    mask = _group_mask(c_re.shape[0], 8 * SP, SC, SP)
    return _spread(c_re, mask), _spread(c_im, mask)


def fn_s5_bu(u, wb_re, wb_im):
    return mm_nt(u, wb_re), mm_nt(u, wb_im)


def fn_s5_out(sr, si, u, d, wc_re, wc_im):
    y = mm_nt(sr, wc_re) - mm_nt(si, wc_im) + d * u
    return (jax.nn.gelu(y),)


def fn_merge_glu(z, mo, gate):
    yg = z[:, :PW] * jax.nn.sigmoid(z[:, PW:])
    return (jnp.concatenate([yg, mo], axis=1) * silu(gate),)


def fn_merge(prim, mo, gate):
    return (jnp.concatenate([prim, mo], axis=1) * silu(gate),)


def fn_mem_k(kv, g):
    return (jnp.concatenate([rms(kv[:, h * XHD:(h + 1) * XHD], g) for h in range(XH)], axis=1),)


def fn_mem_attn(xq, kn, v, g):
    outs = []
    for h in range(XH):
        sl = slice(h * XHD, (h + 1) * XHD)
        p = softmax_rows(mm_nt(rms(xq[:, sl], g), kn[:, sl]) * (XHD ** -0.5))
        outs.append(mm_nn(p, v[:, sl]))
    return (jnp.concatenate(outs, axis=1),)


def _half_rms(x, g):
    lo = lax.broadcasted_iota(jnp.int32, x.shape, 1) < ROPE
    x2 = x * x
    s_lo = jnp.sum(jnp.where(lo, x2, 0.0), axis=1, keepdims=True)
    s_hi = jnp.sum(jnp.where(lo, 0.0, x2), axis=1, keepdims=True)
    return x * lax.rsqrt(jnp.where(lo, s_lo, s_hi) / ROPE + EPS) * g


def _rope(x, cos2, sin_signed):
    first = lax.broadcasted_iota(jnp.int32, x.shape, 1) % ROPE < ROPE // 2
    return x * cos2 + jnp.where(first, lane_roll(x, 128 - ROPE // 2), lane_roll(x, ROPE // 2)) * sin_signed


def fn_mla_prep(q, kv, kr, cos2, sin_signed, qnn, knn, qrn, krn):
    lo = lax.broadcasted_iota(jnp.int32, kr.shape, 1) < ROPE
    kr_pad = jnp.where(lo, _rope(_half_rms(kr, krn), cos2, sin_signed), 0.0)
    qf, kf, vs = [], [], []
    for m in range(MH // 2):
        pair = _rope(_half_rms(q[:, MH * NOPE + 128 * m:MH * NOPE + 128 * (m + 1)], qrn), cos2, sin_signed)
        for h, rope_h in ((2 * m, pair), (2 * m + 1, lane_roll(pair, ROPE))):
            qf.append(jnp.concatenate([rms(q[:, NOPE * h:NOPE * (h + 1)], qnn), jnp.where(lo, rope_h, 0.0)], axis=1))
    for h in range(MH):
        kf.append(jnp.concatenate([rms(kv[:, 256 * h:256 * h + NOPE], knn), kr_pad], axis=1))
        vs.append(kv[:, 256 * h + NOPE:256 * (h + 1)])
    return jnp.stack(qf), jnp.stack(kf), jnp.stack(vs)


ATT_TQ = 256


def _attn_tile(q, kf, v):
    tq = q.shape[0]
    scale = (NOPE + ROPE) ** -0.5
    own = mm_nt(q, kf[-tq:]) * scale
    own = jnp.where(lax.broadcasted_iota(jnp.int32, own.shape, 1) <= lax.broadcasted_iota(jnp.int32, own.shape, 0), own, jnp.finfo(F32).min)
    s = own if kf.shape[0] == tq else jnp.concatenate([mm_nt(q, kf[:-tq]) * scale, own], axis=1)
    return mm_nn(softmax_rows(s), v)


def _attn_specs():
    q_spec = pl.BlockSpec((None, ATT_TQ, 256), lambda h, i: (h, i, 0))
    k_spec = pl.BlockSpec((None, L, 256), lambda h, i: (h, 0, 0))
    v_spec = pl.BlockSpec((None, L, 128), lambda h, i: (h, 0, 0))
    o_spec = pl.BlockSpec((ATT_TQ, 128), lambda h, i: (i, h))
    return q_spec, k_spec, v_spec, o_spec


def causal_attn(qf, kf, vh):
    n_tiles = L // ATT_TQ

    def kern(q_ref, k_ref, v_ref, o_ref):
        i = pl.program_id(1)
        for t in range(n_tiles):
            @pl.when(i == t)
            def _(t=t):
                keys = (t + 1) * ATT_TQ
                o_ref[...] = _attn_tile(q_ref[...], k_ref[:keys, :], v_ref[:keys, :])

    q_spec, k_spec, v_spec, o_spec = _attn_specs()
    return pl.pallas_call(kern, grid=(MH, n_tiles), in_specs=[q_spec, k_spec, v_spec], out_specs=o_spec,
                          out_shape=SDS((L, MH * VD), F32), name="l1_attn", compiler_params=_cparams())(qf, kf, vh)


def causal_attn_bwd(qf, kf, vh, dout):
    n_tiles = L // ATT_TQ

    def kern(q_ref, k_ref, v_ref, do_ref, dq_ref, dk_ref, dv_ref):
        i = pl.program_id(1)

        @pl.when(i == 0)
        def _():
            dk_ref[...] = jnp.zeros_like(dk_ref)
            dv_ref[...] = jnp.zeros_like(dv_ref)

        for t in range(n_tiles):
            @pl.when(i == t)
            def _(t=t):
                keys = (t + 1) * ATT_TQ
                _, vjp = jax.vjp(_attn_tile, q_ref[...].astype(F32), k_ref[:keys, :].astype(F32), v_ref[:keys, :].astype(F32))
                dq, dk, dv = vjp(do_ref[...])
                dq_ref[...] = dq
                dk_ref[:keys, :] += dk
                dv_ref[:keys, :] += dv

    q_spec, k_spec, v_spec, o_spec = _attn_specs()
    return pl.pallas_call(kern, grid=(MH, n_tiles), in_specs=[q_spec, k_spec, v_spec, o_spec], out_specs=[q_spec, k_spec, v_spec],
                          out_shape=[SDS(qf.shape, F32), SDS(kf.shape, F32), SDS(vh.shape, F32)], name="l1_attn_bwd",
                          compiler_params=_cparams())(qf, kf, vh, dout)


def loss_and_grad(y, target, tl=256):
    def kern(y_ref, t_ref, dy_ref, loss_ref):
        d = y_ref[...] - t_ref[...]
        dy_ref[...] = d / D

        @pl.when(pl.program_id(0) == 0)
        def _():
            loss_ref[...] = jnp.zeros_like(loss_ref)

        loss_ref[...] += 0.5 * jnp.sum(jnp.sum(d * d, axis=1, keepdims=True), axis=0, keepdims=True) / D

    return pl.pallas_call(kern, grid=(L // tl,), in_specs=[rspec(tl, D), rspec(tl, D)], out_specs=[rspec(tl, D), cspec((1, 1))],
                          out_shape=[SDS((L, D), F32), SDS((1, 1), F32)], name="loss", compiler_params=_cparams())(y, target)


def adamw(name, w, g, m, v):
    rows, cols = w.shape
    block_row_bytes = 7 * 2 * 4 * max(cols, 128)
    tr = _row_tile(rows, min(2048, ADAMW_VMEM // block_row_bytes // 8 * 8), 8)

    def kern(w_ref, g_ref, m_ref, v_ref, d_ref, nm_ref, nv_ref):
        gg = g_ref[...]
        nm = ADAM_B1 * m_ref[...] + (1.0 - ADAM_B1) * gg
        nv = ADAM_B2 * v_ref[...] + (1.0 - ADAM_B2) * jnp.square(gg)
        m_hat = nm / (1.0 - ADAM_B1 ** ADAM_STEP)
        v_hat = nv / (1.0 - ADAM_B2 ** ADAM_STEP)
        d_ref[...] = -ADAM_LR * (m_hat / (jnp.sqrt(v_hat) + ADAM_EPS) + ADAM_WD * w_ref[...])
        nm_ref[...] = nm
        nv_ref[...] = nv

    spec = rspec(tr, cols)
    return pl.pallas_call(kern, grid=(rows // tr,), in_specs=[spec] * 4, out_specs=[spec] * 3,
                          out_shape=[SDS((rows, cols), F32)] * 3, name=name, compiler_params=_cparams())(w, g, m, v)


def _row_tile(rows, cap=512, unit=16):
    return max(t for t in range(unit, cap + 1, unit) if rows % t == 0)


def _place():
    x, y, c = lax.axis_index("x"), lax.axis_index("y"), lax.axis_index("c")
    return x, y, c, [(1 - x, y), (x, 1 - y), (1 - x, 1 - y)]


def _row_chunks(rows, n, dtype):
    unit = 32 // jnp.dtype(dtype).itemsize
    base, extra = divmod(rows // unit, n)
    out, start = [], 0
    for k in range(n):
        size = (base + (k < extra)) * unit
        if size:
            out.append((start, size))
            start += size
    assert start == rows, (rows, unit)
    return out


PIECE_BYTES = 1 << 20


def _pieces(shapes_dtypes, rows_of):
    out = []
    for b, (shape, dtype) in enumerate(shapes_dtypes):
        rows = rows_of(shape)
        n = max(1, min(4, rows * shape[-1] * jnp.dtype(dtype).itemsize // PIECE_BYTES))
        out += [(b, st, sz) for st, sz in _row_chunks(rows, n, dtype)]
    return out


def all_gather_chips(name, shards):
    nb = len(shards)
    pieces = _pieces([(s.shape, s.dtype) for s in shards], lambda shape: shape[0] // 2)
    n = len(pieces)

    def body(*refs):
        x_refs, out_refs, send_sems, recv_sems = refs[:nb], refs[nb:2 * nb], refs[2 * nb], refs[2 * nb + 1]
        x, y, c, chips = _place()
        sibling = (x, y, 1 - c)
        mine = 2 * x + y

        def copy(sem, chip, cc, k, to, from_input=False):
            b, st, sz = pieces[k]
            rows_k = pl.ds(cc * (x_refs[b].shape[0] // 2) + st, sz)
            dst = out_refs[b].at[chip, rows_k, :]
            return pltpu.make_async_remote_copy(src_ref=x_refs[b].at[rows_k, :] if from_input else dst, dst_ref=dst,
                                                send_sem=send_sems.at[sem], recv_sem=recv_sems.at[sem], device_id=to, device_id_type=MESH_ID)

        order = [(k, j, 2 * cx + cy, (cx, cy, c)) for k in range(n) for j, (cx, cy) in enumerate(chips)]
        first = [copy(j * n + k, mine, c, k, to, from_input=True) for k, j, _, to in order]
        for cp in first:
            cp.start()
        passed = []
        for k, j, chip, _ in order:
            copy(j * n + k, chip, c, k, sibling).wait_recv()
            passed.append(copy((3 + j) * n + k, chip, c, k, sibling))
            passed[-1].start()
        for k, j, chip, _ in order:
            copy((3 + j) * n + k, chip, 1 - c, k, sibling).wait_recv()
        for cp in first + passed:
            cp.wait_send()

    return pl.pallas_call(body, in_specs=[ANY] * nb, out_specs=[ANY] * nb, out_shape=[SDS((4,) + s.shape, s.dtype) for s in shards],
                          scratch_shapes=_dma_sems(6 * n), name=name)(*shards)


def plan_gather_ici(shards):
    pieces = _pieces([(s.shape, s.dtype) for s in shards], lambda shape: shape[0] // 2)
    n = len(pieces)

    def copies(x_refs, out_refs, send_sems, recv_sems):
        x, y, c, chips = _place()
        mine = 2 * x + y

        def copy(j, k, chip, to, from_input):
            b, st, sz = pieces[k]
            rows_k = pl.ds(c * (x_refs[b].shape[0] // 2) + st, sz)
            dst = out_refs[b].at[chip, rows_k, :]
            return pltpu.make_async_remote_copy(src_ref=x_refs[b].at[rows_k, :] if from_input else dst, dst_ref=dst, send_sem=send_sems.at[j * n + k],
                                                recv_sem=recv_sems.at[j * n + k], device_id=to, device_id_type=MESH_ID)

        order = [(k, j, 2 * cx + cy, (cx, cy, c)) for k in range(n) for j, (cx, cy) in enumerate(chips)]
        return [copy(j, k, mine, to, True) for k, j, _, to in order], [copy(j, k, chip, to, False) for k, j, chip, to in order]

    return Plan(shards, [SDS((4,) + s.shape, s.dtype) for s in shards], {}, 3 * n, copies)


def plan_gather_pass(gathered):
    pieces = _pieces([(g.shape[1:], g.dtype) for g in gathered], lambda shape: shape[0] // 2)
    n = len(pieces)

    def copies(_, out_refs, send_sems, recv_sems):
        x, y, c, chips = _place()

        def copy(j, k, chip, cc):
            b, st, sz = pieces[k]
            rows_k = out_refs[b].at[chip, pl.ds(cc * (out_refs[b].shape[1] // 2) + st, sz), :]
            return pltpu.make_async_remote_copy(src_ref=rows_k, dst_ref=rows_k, send_sem=send_sems.at[j * n + k], recv_sem=recv_sems.at[j * n + k],
                                                device_id=(x, y, 1 - c), device_id_type=MESH_ID)

        order = [(k, j, 2 * cx + cy) for k in range(n) for j, (cx, cy) in enumerate(chips)]
        return [copy(j, k, chip, c) for k, j, chip in order], [copy(j, k, chip, 1 - c) for k, j, chip in order]

    return Plan(gathered, [SDS(g.shape, g.dtype) for g in gathered], {i: i for i in range(len(gathered))}, 3 * n, copies)


def plan_pair_exchange(gs):
    pieces = _pieces([(g.shape, g.dtype) for g in gs], lambda shape: shape[1] // 2)

    def copies(g_refs, got_refs, send_sems, recv_sems):
        x, y, c, _ = _place()
        swaps = [pltpu.make_async_remote_copy(src_ref=g_refs[b].at[:, pl.ds((1 - c) * (g_refs[b].shape[1] // 2) + st, sz), :],
                                              dst_ref=got_refs[b].at[:, pl.ds(st, sz), :], send_sem=send_sems.at[k], recv_sem=recv_sems.at[k],
                                              device_id=(x, y, 1 - c), device_id_type=MESH_ID)
                 for k, (b, st, sz) in enumerate(pieces)]
        return swaps, swaps

    return Plan(gs, [SDS((g.shape[0], g.shape[1] // 2, g.shape[2]), g.dtype) for g in gs], {}, len(pieces), copies)


def plan_chip_scatter(ps):
    pieces = _pieces([(p.shape, p.dtype) for p in ps], lambda shape: shape[1])
    n = len(pieces)

    def copies(p_refs, q_refs, send_sems, recv_sems):
        x, y, c, chips = _place()
        mine = 2 * x + y

        def copy(j, k, src_slot, dst_slot, to):
            b, st, sz = pieces[k]
            return pltpu.make_async_remote_copy(src_ref=p_refs[b].at[src_slot, pl.ds(st, sz), :], dst_ref=q_refs[b].at[dst_slot, pl.ds(st, sz), :],
                                                send_sem=send_sems.at[j * n + k], recv_sem=recv_sems.at[j * n + k], device_id=to,
                                                device_id_type=MESH_ID)

        order = [(k, j, 2 * cx + cy, (cx, cy, c)) for k in range(n) for j, (cx, cy) in enumerate(chips)]
        return [copy(j, k, chip, mine, to) for k, j, chip, to in order], [copy(j, k, mine, chip, to) for k, j, chip, to in order]

    return Plan(ps, [SDS(p.shape, p.dtype) for p in ps], {}, 3 * n, copies)


def plan_pair_join(bufs):
    pieces = _pieces([(b.shape, b.dtype) for b in bufs], lambda shape: shape[0] // 2)

    def copies(_, out_refs, send_sems, recv_sems):
        x, y, c, _ = _place()

        def copy(k, cc):
            b, st, sz = pieces[k]
            rows_k = out_refs[b].at[pl.ds(cc * (out_refs[b].shape[0] // 2) + st, sz), :]
            return pltpu.make_async_remote_copy(src_ref=rows_k, dst_ref=rows_k, send_sem=send_sems.at[k], recv_sem=recv_sems.at[k],
                                                device_id=(x, y, 1 - c), device_id_type=MESH_ID)

        return [copy(k, c) for k in range(len(pieces))], [copy(k, 1 - c) for k in range(len(pieces))]

    return Plan(bufs, [SDS(b.shape, b.dtype) for b in bufs], {i: i for i in range(len(bufs))}, len(pieces), copies)


def run_plan(name, plan):
    hosted_call(lambda: None, name=name, grid=(1,), in_specs=[], out_specs=[], out_shape=[], operands=[], plans=[plan])
    return plan.results


def pair_add(name, g, got, place):
    slots, rows, cols = g.shape
    half = rows // 2
    tr = _row_tile(half)
    nb = half // tr

    def kern(_, g_ref, t_ref, o_ref):
        o_ref[...] = (g_ref[...].astype(F32) + t_ref[...].astype(F32)).astype(o_ref.dtype)

    blk = pl.BlockSpec((None, tr, cols), lambda s, i, p: (s, i, 0))
    grid_spec = pltpu.PrefetchScalarGridSpec(
        num_scalar_prefetch=1, grid=(slots, nb),
        in_specs=[pl.BlockSpec((None, tr, cols), lambda s, i, p: (s, p[1] * nb + i, 0)), blk], out_specs=blk)
    return pl.pallas_call(kern, grid_spec=grid_spec, out_shape=SDS((slots, half, cols), g.dtype), name=name,
                          compiler_params=_cparams())(place, g, got)


def chip_add(name, p, q, place):
    slots, half, cols = p.shape
    tr = _row_tile(half)
    nb = half // tr

    def kern(_, p_ref, q1, q2, q3, o_ref):
        o_ref[...] = p_ref[...].astype(F32) + q1[...].astype(F32) + q2[...].astype(F32) + q3[...].astype(F32)

    def slot(k):
        return pl.BlockSpec((None, tr, cols), lambda i, pr: ((pr[0] + k) % slots, i, 0))

    grid_spec = pltpu.PrefetchScalarGridSpec(
        num_scalar_prefetch=1, grid=(nb,), in_specs=[slot(0), slot(1), slot(2), slot(3)],
        out_specs=pl.BlockSpec((tr, cols), lambda i, pr: (pr[1] * nb + i, 0)))
    return pl.pallas_call(kern, grid_spec=grid_spec, out_shape=SDS((2 * half, cols), F32), name=name,
                          compiler_params=_cparams())(place, p, q, q, q)


def pair_adds(tag, gs, gots, place):
    return [pair_add(f"{tag}_pair_add_{i}", g, got, place) for i, (g, got) in enumerate(zip(gs, gots))]


def chip_adds(tag, pairs, qs, place):
    return [chip_add(f"{tag}_chip_add_{i}", p, q, place) for i, (p, q) in enumerate(zip(pairs, qs))]


def reduce_scatter_chips(tag, gs, place):
    pairs = pair_adds(tag, gs, run_plan(tag + "_pair_exchange", plan_pair_exchange(gs)), place)
    return run_plan(tag + "_pair_join", plan_pair_join(chip_adds(tag, pairs, run_plan(tag + "_chip_scatter", plan_chip_scatter(pairs)), place)))


BIG = [("w_out", (2, 512, 1024)), ("w_mem_kv", (2, 256, 1024)), ("s5_w_in", (1, 1024, 1024)), ("s5_w_glu", (1, 1536, 768)),
       ("mla_w_in", (1, 1024, 848)), ("mla_w_uq", (1, 512, 576)), ("mla_w_ukv", (1, 256, 768))]
SHARDED_SMALL = [("mla_q_lora_norm", (1, 128)), ("mla_kv_lora_norm", (1, 64))]
SMALL = [("ln_gain", (2, 1024)), ("mem_norm", (2, 1024)), ("xq_norm", (2, 128)), ("xk_norm", (2, 128)),
         ("s5_lambda_re", (1, 96, 64)), ("s5_lambda_im", (1, 96, 64)), ("s5_log_step", (1, 96)),
         ("s5_b_re", (1, 96, 64, 16)), ("s5_b_im", (1, 96, 64, 16)), ("s5_c_re", (1, 96, 16, 64)), ("s5_c_im", (1, 96, 16, 64)),
         ("s5_d", (1, 1536)), ("mla_q_nope_norm", (1, 128)), ("mla_k_nope_norm", (1, 128)), ("mla_q_rope_norm", (1, 64)),
         ("mla_k_rope_norm", (1, 64))]
WEIGHT_ORDER = ["ln_gain", "w_out", "mem_norm", "w_mem_kv", "xq_norm", "xk_norm", "s5_w_in", "s5_lambda_re", "s5_lambda_im",
                "s5_log_step", "s5_b_re", "s5_b_im", "s5_c_re", "s5_c_im", "s5_d", "s5_w_glu", "mla_w_in", "mla_q_lora_norm",
                "mla_kv_lora_norm", "mla_w_uq", "mla_w_ukv", "mla_q_nope_norm", "mla_k_nope_norm", "mla_q_rope_norm", "mla_k_rope_norm"]
MINOR_LAST = {"mla_w_in": (0, 2, 1), "mla_w_uq": (0, 2, 1), "s5_b_re": (0, 2, 3, 1), "s5_b_im": (0, 2, 3, 1),
              "s5_c_re": (0, 2, 3, 1), "s5_c_im": (0, 2, 3, 1)}
SMALL_FULL = SMALL + [(n, (1, 4 * s[1])) for n, s in SHARDED_SMALL]
N_SMALL = sum(math.prod(s) for _, s in SMALL_FULL)
SMALL_ROWS, SMALL_LANES = 128, 1024

WIDE0_IN, WIDE0_OUT, WIDE0_MKV = 0, 1024, 1536
PAIR_OUT, PAIR_MKV, PAIR_ROWS = 0, 512, 768


def stack_shards(w, dtype):
    wide0 = jnp.concatenate([w["s5_w_in"][0], w["w_out"][0], w["w_mem_kv"][0]], axis=0)
    wide1 = jnp.concatenate([w["w_out"][1], w["w_mem_kv"][1]], axis=0)
    return ([wide0.astype(dtype), w["s5_w_glu"][0].astype(dtype)],
            [wide1.astype(dtype), w["mla_w_ukv"][0].astype(dtype), w["mla_w_in"][0].astype(dtype), w["mla_w_uq"][0].astype(dtype)])


def weight_views0(wide0, glu):
    return {"s5_w_in": Sharded(wide0, "col", WIDE0_IN, 1024), "w_out": Sharded(wide0, "row", WIDE0_OUT, 512),
            "w_mem_kv": Sharded(wide0, "row", WIDE0_MKV, 256), "s5_w_glu": Sharded(glu, "col", 0, 1536)}


def weight_views1(wide1, ukv):
    return {"w_out": Sharded(wide1, "row", PAIR_OUT, 512), "w_mem_kv": Sharded(wide1, "row", PAIR_MKV, 256), "mla_w_ukv": Sharded(ukv, "col", 0, 256)}


def grad_views():
    pair = SDS((4, PAIR_ROWS, 1024), BF16)
    return {"w_out": Sharded(pair, "row", PAIR_OUT, 512), "w_mem_kv": Sharded(pair, "row", PAIR_MKV, 256),
            "s5_w_in": Sharded(SDS((4, 1024, 1024), BF16), "col", 0, 1024), "s5_w_glu": Sharded(SDS((4, 1536, 768), BF16), "col", 0, 1536),
            "mla_w_ukv": Sharded(SDS((4, 256, 768), BF16), "col", 0, 256)}


def cols_to_shards(full):
    return full.reshape(full.shape[0], 4, full.shape[1] // 4).transpose(1, 0, 2)


def shards_to_cols(arr):
    return arr.transpose(1, 0, 2).reshape(arr.shape[1], 4 * arr.shape[2])


def mla_in_permute(w):
    o1, o2, o3, o4 = QL, QL + KVL, QL + KVL + ROPE, QL + KVL + ROPE + XQW
    return jnp.concatenate([w[:, o4:], w[:, :o1], w[:, o3:o4], w[:, o1:o2], w[:, o2:o3],
                            jnp.zeros((w.shape[0], MLA_IN_P - MLA_IN), w.dtype)], axis=1)


def mla_in_unpermute(d):
    return jnp.concatenate([d[:, 2048:2560], d[:, 3072:3328], d[:, 3328:3392], d[:, 2560:3072], d[:, :2048]], axis=1)


def uq_permute(w):
    w3 = w.reshape(w.shape[0], MH, NOPE + ROPE)
    return jnp.concatenate([w3[:, :, :NOPE].reshape(w.shape[0], MH * NOPE), w3[:, :, NOPE:].reshape(w.shape[0], MH * ROPE)], axis=1)


def uq_unpermute(d):
    dn = d[:, :MH * NOPE].reshape(d.shape[0], MH, NOPE)
    dr = d[:, MH * NOPE:].reshape(d.shape[0], MH, ROPE)
    return jnp.concatenate([dn, dr], axis=2).reshape(d.shape[0], MH * (NOPE + ROPE))


def time_permute(a):
    return a.reshape(SEG, SEG_LEN, a.shape[-1]).transpose(1, 0, 2).reshape(L, a.shape[-1])


def time_unpermute(a):
    return a.reshape(SEG_LEN, SEG, a.shape[-1]).transpose(1, 0, 2).reshape(L, a.shape[-1])


def mem_branch_fwd(tag, mem, mem_norm, w_mem_kv, xk_norm):
    mn = row_fwd(tag + "_mem_rms", fn_rms, ML, ML, [(mem, D, 0)], [mem_norm], [(D, BF16)])[0]
    kv = matmul(tag + "_mem_kv", mn, w_mem_kv, "nn")
    kn = row_fwd(tag + "_mem_knorm", fn_mem_k, ML, ML, [(kv, XQW, 0)], [xk_norm], [(XQW, F32)])[0]
    return mn, kv, kn


def mem_branch_bwd(tag, mem, mem_norm, w_mem_kv, xk_norm, mn, kv, dkn, dv, g_view, g_wide):
    dk, dxk = row_bwd(tag + "_mem_knorm_bwd", fn_mem_k, ML, ML, [(kv, XQW, 0)], [xk_norm], [(dkn, XQW, 0)], [True], [True])
    dkv = jnp.concatenate([dk, dv], axis=1)
    dmn = matmul(tag + "_mem_kv_dx", dkv, w_mem_kv, "nt")
    g_wide = matmul(tag + "_mem_kv_dw", mn, dkv, "tn", out=g_view, into=g_wide)
    dmem_norm = row_bwd(tag + "_mem_rms_bwd", fn_rms, ML, ML, [(mem, D, 0)], [mem_norm], [(dmn, D, 0)], [False], [True])[0]
    return g_wide, dmem_norm, dxk


def mem_attn_fwd(tag, proj, cb, kn, kv, xq_norm):
    return row_fwd(tag + "_mem_attn", fn_mem_attn, L, 256, [(proj, XQW, cb)], [kn, kv[:, XQW:], xq_norm], [(XQW, F32)])[0]


def mem_attn_bwd(tag, proj, cb, kn, kv, xq_norm, dmo, dproj):
    place = {"cols": proj.shape[1], "cb": cb, "into": dproj, "dtype": dproj.dtype}
    return row_bwd(tag + "_mem_attn_bwd", fn_mem_attn, L, 256, [(proj, XQW, cb)], [kn, kv[:, XQW:], xq_norm], [(dmo, XQW, 0)],
                   [place], [True, True, True])


def device_step(x, mem, positions, target, small, env, hooks=None):
    hooks = hooks or {}

    def plans_for(name):
        return hooks[("plans", name)](env) if ("plans", name) in hooks else ()

    def after(name):
        if ("after", name) in hooks:
            hooks[("after", name)](env)

    g = {}
    w0 = weight_views0(env["wide0"], env["glu"])
    gw = grad_views()
    ln, mem_norm, xq_norm, xk_norm = small["ln_gain"], small["mem_norm"], small["xq_norm"], small["xk_norm"]

    lre, lim = small["s5_lambda_re"][0], small["s5_lambda_im"][0]
    ls = small["s5_log_step"].reshape(SG, 1)
    one = pl.BlockSpec((SG, SP), lambda i: (0, 0))
    col = pl.BlockSpec((SG, 1), lambda i: (0, 0))
    disc_ins = [(lre, one), (lim, one), (ls, col)]
    a_re, a_im, coef_re, coef_im = stage("s5_disc", fn_s5_disc, (1,), disc_ins, [(SDS((SG, SP), F32), one)] * 4)
    b_re, b_im = small["s5_b_re"].reshape(SN, SC), small["s5_b_im"].reshape(SN, SC)
    c_re, c_im = small["s5_c_re"].reshape(PW, SP), small["s5_c_im"].reshape(PW, SP)
    bmat_rows = [(b_re, SC, 0), (b_im, SC, 0), (coef_re.reshape(SN, 1), 1, 0), (coef_im.reshape(SN, 1), 1, 0)]
    wb_re, wb_im = row_fwd("s5_bmat", fn_s5_bmat, SN, 512, bmat_rows, [], [(128, F32)] * 2)
    cmat_rows = [(c_re, SP, 0), (c_im, SP, 0)]
    wc_re, wc_im = row_fwd("s5_cmat", fn_s5_cmat, PW, 128, cmat_rows, [], [(512, F32)] * 2)
    a_re_v, a_im_v = a_re.reshape(1, SN), a_im.reshape(1, SN)
    s5_d = small["s5_d"]

    xp = time_permute(x)
    h0 = row_fwd("l0_rms", fn_rms, L, 256, [(xp, D, 0)], [ln[0:1]], [(D, BF16)])[0]
    proj0 = matmul("l0_in", h0, w0["s5_w_in"], "nn")
    s_re, s_im, g0 = s5_forward(proj0, wb_re, wb_im, wc_re, wc_im, a_re_v, a_im_v, s5_d, plans=plans_for("s5_forward"))
    z0 = matmul("l0_glu", g0, w0["s5_w_glu"], "nn", plans=plans_for("l0_glu"))
    after("l0_glu")
    mn0, kv0, kn0 = mem_branch_fwd("l0", mem, mem_norm[0:1], w0["w_mem_kv"], xk_norm[0:1])
    mo0 = mem_attn_fwd("l0", proj0, 3, kn0, kv0, xq_norm[0:1])
    o0 = row_fwd("l0_merge", fn_merge_glu, L, 256, [(z0, 2 * PW, 0), (mo0, XQW, 0), (proj0, BW, 1)], [], [(BW, BF16)])[0]
    x1p = matmul("l0_out", o0, w0["w_out"], "nn", add=xp)
    x1 = time_unpermute(x1p)

    w1 = weight_views1(env["wide1"], env["ukv"])
    w_in1, w_uq = env["w_in1"], env["w_uq"]
    h1 = row_fwd("l1_rms", fn_rms, L, 256, [(x1, D, 0)], [ln[1:2]], [(D, BF16)])[0]
    proj1 = matmul("l1_in", h1, w_in1, "nn")
    qln, kvln = env["q_lora_norm"].reshape(1, QL), env["kv_lora_norm"].reshape(1, KVL)
    cqn = row_fwd("l1_q_lora_rms", fn_rms, L, 256, [(proj1, QL, 4)], [qln], [(QL, BF16)])[0]
    ckvn = row_fwd("l1_kv_lora_rms", fn_rms, L, 256, [(proj1, KVL, 12)], [kvln], [(KVL, BF16)])[0]
    q = matmul("l1_uq", cqn, w_uq, "nn")
    kv = matmul("l1_ukv", ckvn, w1["mla_w_ukv"], "nn")
    inv_freq = ROPE_THETA ** (-jnp.arange(ROPE // 2, dtype=F32) / (ROPE // 2))
    ang = positions.astype(F32)[:, None] * inv_freq
    cos2 = jnp.tile(jnp.cos(ang), (1, 4))
    sin_signed = jnp.tile(jnp.concatenate([-jnp.sin(ang), jnp.sin(ang)], axis=1), (1, 2))
    qnn, knn = small["mla_q_nope_norm"], small["mla_k_nope_norm"]
    qrn, krn = jnp.tile(small["mla_q_rope_norm"], (1, 2)), jnp.tile(small["mla_k_rope_norm"], (1, 2))
    tp = 256
    prep_ins = [(q, rspec(tp, MH * (NOPE + ROPE))), (kv, rspec(tp, MH * 256)), (proj1, rspec(tp, 128, 26)),
                (cos2, rspec(tp, 128)), (sin_signed, rspec(tp, 128))] + [(a, cspec((1, 128))) for a in (qnn, knn, qrn, krn)]
    hq_spec = pl.BlockSpec((MH, tp, 256), lambda i: (0, i, 0))
    hv_spec = pl.BlockSpec((MH, tp, 128), lambda i: (0, i, 0))
    qf, kf, vh = stage("l1_mla_prep", fn_mla_prep, (L // tp,), prep_ins,
                       [(SDS((MH, L, 256), BF16), hq_spec), (SDS((MH, L, 256), BF16), hq_spec), (SDS((MH, L, 128), BF16), hv_spec)])
    attn = causal_attn(qf, kf, vh)
    mn1, kv1, kn1 = mem_branch_fwd("l1", mem, mem_norm[1:2], w1["w_mem_kv"], xk_norm[1:2])
    mo1 = mem_attn_fwd("l1", proj1, 5, kn1, kv1, xq_norm[1:2])
    o1 = row_fwd("l1_merge", fn_merge, L, 256, [(attn, PW, 0), (mo1, XQW, 0), (proj1, BW, 0)], [], [(BW, BF16)])[0]
    x2 = matmul("l1_out", o1, w1["w_out"], "nn", add=x1)
    dx2, loss = loss_and_grad(x2, target)

    do1 = matmul("l1_out_dx", dx2, w1["w_out"], "nt")
    g_pair1 = matmul("l1_out_dw", o1, dx2, "tn", out=gw["w_out"])
    dattn, dmo1, dproj1 = row_bwd("l1_merge_bwd", fn_merge, L, 256, [(attn, PW, 0), (mo1, XQW, 0), (proj1, BW, 0)], [],
                                  [(do1, BW, 0)], [True, True, {"cols": MLA_IN_P, "cb": 0, "dtype": BF16}], [])
    dproj1, dkn1, dv1, dxqn1 = mem_attn_bwd("l1", proj1, 5, kn1, kv1, xq_norm[1:2], dmo1, dproj1)
    env["g_pair1"], dmem_norm1, dxk1 = mem_branch_bwd("l1", mem, mem_norm[1:2], w1["w_mem_kv"], xk_norm[1:2], mn1, kv1, dkn1, dv1,
                                                      gw["w_mem_kv"], g_pair1)
    dqf, dkf, dvh = causal_attn_bwd(qf, kf, vh, dattn)
    prep_diffs = [("row", SDS((L, MH * (NOPE + ROPE)), BF16), rspec(tp, MH * (NOPE + ROPE))), ("row", SDS((L, MH * 256), BF16), rspec(tp, MH * 256)),
                  ("row", SDS((L, MLA_IN_P), BF16), rspec(tp, 128, 26), {"into": dproj1}), None, None] + [("acc", (0,))] * 4
    dq, dkv, dproj1, dqnn, dknn, dqrn, dkrn = stage_bwd("l1_mla_prep_bwd", fn_mla_prep, (L // tp,), prep_ins,
                                                        [(dqf, hq_spec), (dkf, hq_spec), (dvh, hv_spec)], prep_diffs)
    dcqn = matmul("l1_uq_dx", dq, w_uq, "nt")
    env["dw_uq"] = matmul("l1_uq_dw", cqn, dq, "tn")
    dckvn = matmul("l1_ukv_dx", dkv, w1["mla_w_ukv"], "nt")
    env["g_ukv"] = matmul("l1_ukv_dw", ckvn, dkv, "tn", out=gw["mla_w_ukv"])
    dproj1, dqln = row_bwd("l1_q_lora_rms_bwd", fn_rms, L, 256, [(proj1, QL, 4)], [qln], [(dcqn, QL, 0)],
                           [{"cols": MLA_IN_P, "cb": 4, "into": dproj1, "dtype": BF16}], [True])
    dproj1, dkvln = row_bwd("l1_kv_lora_rms_bwd", fn_rms, L, 256, [(proj1, KVL, 12)], [kvln], [(dckvn, KVL, 0)],
                            [{"cols": MLA_IN_P, "cb": 12, "into": dproj1, "dtype": BF16}], [True])
    dh1 = matmul("l1_in_dx", dproj1, w_in1, "nt")
    env["dw_in1"] = matmul("l1_in_dw", h1, dproj1, "tn")
    dx1, dln1 = row_bwd("l1_rms_bwd", fn_rms, L, 256, [(x1, D, 0)], [ln[1:2]], [(dh1, D, 0)], [{"add": (dx2, D, 0)}], [True],
                        plans=plans_for("l1_rms_bwd"))
    dx1p = time_permute(dx1)

    do0 = matmul("l0_out_dx", dx1p, w0["w_out"], "nt", plans=plans_for("l0_out_dx"))
    g_pair0 = matmul("l0_out_dw", o0, dx1p, "tn", out=gw["w_out"])
    dz0, dmo0, dproj0 = row_bwd("l0_merge_bwd", fn_merge_glu, L, 256, [(z0, 2 * PW, 0), (mo0, XQW, 0), (proj0, BW, 1)], [],
                                [(do0, BW, 0)], [{"dtype": BF16}, True, {"cols": 2 * BW, "cb": 1, "dtype": BF16}], [])
    dproj0, dkn0, dv0, dxqn0 = mem_attn_bwd("l0", proj0, 3, kn0, kv0, xq_norm[0:1], dmo0, dproj0)
    env["g_pair0"], dmem_norm0, dxk0 = mem_branch_bwd("l0", mem, mem_norm[0:1], w0["w_mem_kv"], xk_norm[0:1], mn0, kv0, dkn0, dv0,
                                                      gw["w_mem_kv"], g_pair0)
    env["g_glu"] = matmul("l0_glu_dw", g0, dz0, "tn", out=gw["s5_w_glu"], plans=plans_for("l0_glu_dw"))
    dg0 = matmul("l0_glu_dx", dz0, w0["s5_w_glu"], "nt", plans=plans_for("l0_glu_dx"))
    dproj0, dd, dwc_re, dwc_im, dwb_re, dwb_im, da_re, da_im = s5_backward(dg0, proj0, s_re, s_im, wb_re, wb_im, wc_re, wc_im,
                                                                           a_re_v, a_im_v, s5_d, dproj0, plans=plans_for("s5_backward"))
    env["g_in0"] = matmul("l0_in_dw", h0, dproj0, "tn", out=gw["s5_w_in"])
    dh0 = matmul("l0_in_dx", dproj0, w0["s5_w_in"], "nt", plans=plans_for("l0_in_dx"))
    dxp, dln0 = row_bwd("l0_rms_bwd", fn_rms, L, 256, [(xp, D, 0)], [ln[0:1]], [(dh0, D, 0)], [{"add": (dx1p, D, 0)}], [True])
    grad_x = time_unpermute(dxp)

    db_re, db_im, dcoef_re, dcoef_im = row_bwd("s5_bmat_bwd", fn_s5_bmat, SN, 512, bmat_rows, [], [(dwb_re, 128, 0), (dwb_im, 128, 0)],
                                               [True] * 4, [], plans=plans_for("s5_bmat_bwd"))
    dc_re, dc_im = row_bwd("s5_cmat_bwd", fn_s5_cmat, PW, 128, cmat_rows, [], [(dwc_re, 512, 0), (dwc_im, 512, 0)], [True] * 2, [],
                           plans=plans_for("s5_cmat_bwd"))
    disc_cts = [(da_re.reshape(SG, SP), one), (da_im.reshape(SG, SP), one), (dcoef_re.reshape(SG, SP), one), (dcoef_im.reshape(SG, SP), one)]
    dlre, dlim, dls = stage_bwd("s5_disc_bwd", fn_s5_disc, (1,), disc_ins, disc_cts, [("acc", (0,))] * 3)

    g["ln_gain"] = jnp.concatenate([dln0, dln1], axis=0)
    g["mem_norm"] = jnp.concatenate([dmem_norm0, dmem_norm1], axis=0)
    g["xq_norm"] = jnp.concatenate([dxqn0, dxqn1], axis=0)
    g["xk_norm"] = jnp.concatenate([dxk0, dxk1], axis=0)
    g["s5_lambda_re"], g["s5_lambda_im"], g["s5_log_step"] = dlre, dlim, dls
    g["s5_b_re"], g["s5_b_im"], g["s5_c_re"], g["s5_c_im"] = db_re, db_im, dc_re, dc_im
    g["s5_d"] = dd
    g["mla_q_lora_norm"], g["mla_kv_lora_norm"] = dqln, dkvln
    g["mla_q_nope_norm"], g["mla_k_nope_norm"] = dqnn, dknn
    g["mla_q_rope_norm"] = dqrn[:, :ROPE] + dqrn[:, ROPE:]
    g["mla_k_rope_norm"] = dkrn[:, :ROPE] + dkrn[:, ROPE:]
    return loss, grad_x, g


def kernel(x, mem, positions, ln_gain, w_out, mem_norm, w_mem_kv, xq_norm, xk_norm, s5_w_in, s5_lambda_re, s5_lambda_im, s5_log_step, s5_b_re, s5_b_im, s5_c_re, s5_c_im, s5_d, s5_w_glu, mla_w_in, mla_q_lora_norm, mla_kv_lora_norm, mla_w_uq, mla_w_ukv, mla_q_nope_norm, mla_k_nope_norm, mla_q_rope_norm, mla_k_rope_norm, loss_target, m_ln_gain, m_w_out, m_mem_norm, m_w_mem_kv, m_xq_norm, m_xk_norm, m_s5_w_in, m_s5_lambda_re, m_s5_lambda_im, m_s5_log_step, m_s5_b_re, m_s5_b_im, m_s5_c_re, m_s5_c_im, m_s5_d, m_s5_w_glu, m_mla_w_in, m_mla_q_lora_norm, m_mla_kv_lora_norm, m_mla_w_uq, m_mla_w_ukv, m_mla_q_nope_norm, m_mla_k_nope_norm, m_mla_q_rope_norm, m_mla_k_rope_norm, v_ln_gain, v_w_out, v_mem_norm, v_w_mem_kv, v_xq_norm, v_xk_norm, v_s5_w_in, v_s5_lambda_re, v_s5_lambda_im, v_s5_log_step, v_s5_b_re, v_s5_b_im, v_s5_c_re, v_s5_c_im, v_s5_d, v_s5_w_glu, v_mla_w_in, v_mla_q_lora_norm, v_mla_kv_lora_norm, v_mla_w_uq, v_mla_w_ukv, v_mla_q_nope_norm, v_mla_k_nope_norm, v_mla_q_rope_norm, v_mla_k_rope_norm):
    args = dict(locals())
    wts = {n: args[n] for n in WEIGHT_ORDER}
    mom = {n: args["m_" + n] for n in WEIGHT_ORDER}
    var = {n: args["v_" + n] for n in WEIGHT_ORDER}

    chip = 2 * lax.axis_index("x") + lax.axis_index("y")
    place = jnp.stack([chip, lax.axis_index("c")]).astype(jnp.int32)

    def own_slot(gathered, shards):
        return [lax.dynamic_update_slice(g, s[None], (chip, 0, 0)) for g, s in zip(gathered, shards)]

    shards0, shards1 = stack_shards(wts, BF16)
    shards1.append(jnp.concatenate([mla_q_lora_norm, jnp.pad(mla_kv_lora_norm, ((0, 0), (0, 64))), jnp.zeros((14, 128), F32)], axis=0))
    env = dict(zip(("wide0", "glu"), own_slot(all_gather_chips("gather_weights0", shards0), shards0)))
    over_ici = plan_gather_ici(shards1)
    hooks = {("plans", "s5_forward"): lambda env: [over_ici]}
    passed_on = []

    def pass_on(env):
        passed_on.append(plan_gather_pass(over_ici.results))
        return passed_on

    def layer1_weights(env):
        wide1, ukv, in1, uq, norms = own_slot(passed_on[0].results, shards1)
        env.update(wide1=wide1, ukv=ukv, w_in1=mla_in_permute(shards_to_cols(in1)), w_uq=uq_permute(shards_to_cols(uq)),
                   q_lora_norm=norms[:, 0, :], kv_lora_norm=norms[:, 1, :64])

    hooks["plans", "l0_glu"], hooks["after", "l0_glu"] = pass_on, layer1_weights

    rs = {}

    def swap(k, gs):
        rs[k, "g"], rs[k, "swap"] = gs, plan_pair_exchange(gs)
        return rs[k, "swap"]

    def scatter(k, part=slice(None)):
        if (k, "pairs") not in rs:
            rs[k, "pairs"], rs[k, "scatter"] = pair_adds(f"rs{k}", rs[k, "g"], rs[k, "swap"].results, place), []
        rs[k, "scatter"].append(plan_chip_scatter(rs[k, "pairs"][part]))
        return rs[k, "scatter"][-1]

    def join(k):
        rs[k, "join"] = plan_pair_join(chip_adds(f"rs{k}", rs[k, "pairs"], [q for p in rs[k, "scatter"] for q in p.results], place))
        return rs[k, "join"]

    hooks["plans", "l1_rms_bwd"] = lambda env: [swap(1, [env["g_pair1"], env["g_ukv"], cols_to_shards(mla_in_unpermute(env["dw_in1"])).astype(BF16),
                                                          cols_to_shards(uq_unpermute(env["dw_uq"])).astype(BF16)])]
    hooks["plans", "l0_glu_dw"] = lambda env: [scatter(1, slice(0, 2))]
    hooks["plans", "l0_glu_dx"] = lambda env: [scatter(1, slice(2, 4)), swap(0, [env["g_pair0"], env["g_glu"]])]
    hooks["plans", "s5_backward"] = lambda env: [scatter(0), join(1)]
    hooks["plans", "l0_in_dx"] = lambda env: [swap(2, [env["g_in0"]]), join(0)]
    hooks["plans", "s5_bmat_bwd"] = lambda env: [scatter(2)]
    hooks["plans", "s5_cmat_bwd"] = lambda env: [join(2)]

    small = {n: wts[n] for n, _ in SMALL}
    loss, grad_x, g = device_step(x[0], mem[0], positions[0], loss_target[0], small, env, hooks)
    loss = lax.psum(loss[0, 0], MESH_AXES)
    (r_pair1, r_ukv, r_in1, r_uq), (r_pair0, r_glu), (r_in0,) = (rs[k, "join"].results for k in (1, 0, 2))

    small_flat = jnp.concatenate([g[n].reshape(-1) for n, _ in SMALL_FULL])
    g_small = jnp.pad(small_flat, (0, 4 * SMALL_ROWS * SMALL_LANES - N_SMALL)).astype(BF16).reshape(4, SMALL_ROWS, SMALL_LANES)
    r_small = reduce_scatter_chips("rs3", [g_small], place)[0]
    small_all = own_slot(all_gather_chips("gather_small_grads", [r_small]), [r_small])[0].reshape(-1)[:N_SMALL]

    grads = {"w_out": jnp.stack([r_pair0[:PAIR_MKV], r_pair1[:PAIR_MKV]]), "w_mem_kv": jnp.stack([r_pair0[PAIR_MKV:], r_pair1[PAIR_MKV:]]),
             "s5_w_in": r_in0[None], "s5_w_glu": r_glu[None], "mla_w_ukv": r_ukv[None], "mla_w_in": r_in1[None], "mla_w_uq": r_uq[None]}
    off = 0
    for n, s in SMALL_FULL:
        grads[n] = small_all[off:off + math.prod(s)].reshape(s)
        off += math.prod(s)
    for n, s in SHARDED_SMALL:
        grads[n] = lax.dynamic_slice(grads[n], (0, chip * s[1]), s)

    delta, new_m, new_v = {}, {}, {}
    for n, s in BIG + [(n, s) for n, s in SMALL if len(s) == 4]:
        perm = MINOR_LAST.get(n, tuple(range(len(s))))
        turned = tuple(s[p] for p in perm)
        view = lambda a: jnp.transpose(a, perm).reshape(-1, turned[-1])
        res = adamw("adamw_" + n, view(wts[n]), view(grads[n]), view(mom[n]), view(var[n]))
        delta[n], new_m[n], new_v[n] = (jnp.transpose(r.reshape(turned), tuple(perm.index(i) for i in range(len(s)))) for r in res)
    small_names = [n for n, s in SMALL if len(s) < 4] + [n for n, _ in SHARDED_SMALL]
    n_own = sum(wts[n].size for n in small_names)
    rows_own = -(-n_own // (8 * 128)) * 8

    def pack_small(d):
        flat = jnp.concatenate([d[n].reshape(-1) for n in small_names])
        return jnp.pad(flat, (0, rows_own * 128 - n_own), constant_values=1.0).reshape(rows_own, 128)

    res = adamw("adamw_small", pack_small(wts), pack_small(grads), pack_small(mom), pack_small(var))
    off = 0
    for n in small_names:
        size = wts[n].size
        delta[n], new_m[n], new_v[n] = (r.reshape(-1)[off:off + size].reshape(wts[n].shape) for r in res)
        off += size

    return (loss, grad_x[None], *[grads[n] for n in WEIGHT_ORDER], *[delta[n] for n in WEIGHT_ORDER],
            *[new_m[n] for n in WEIGHT_ORDER], *[new_v[n] for n in WEIGHT_ORDER])
```

```python
import functools
import math

import jax
import jax.numpy as jnp
from jax import lax
from jax.experimental import pallas as pl
from jax.experimental.pallas import tpu as pltpu

F32, BF16 = jnp.float32, jnp.bfloat16
SDS = jax.ShapeDtypeStruct

D = 1024
L = 2048
ML = 256
BW = 2 * D
XQW = BW // 4
PW = BW - XQW
XH, XHD = 4, 128
SG, SC, SP = 96, 16, 64
SN = SG * SP
NOPE, ROPE, VD = 128, 64, 128
MH = 12
QL, KVL = 512, 256
EPS = 1e-6
ROPE_THETA = 10000.0
MLA_IN = QL + KVL + ROPE + XQW + BW
MLA_IN_P = 3456
ADAM_LR, ADAM_B1, ADAM_B2, ADAM_EPS, ADAM_WD, ADAM_STEP = 0.001, 0.9, 0.999, 1e-08, 0.01, 10

VMEM_LIMIT = 48 * 2**20
SEG = 8
SEG_LEN = L // SEG
MESH_AXES = ("x", "y", "c")


def _cparams():
    return pltpu.CompilerParams(vmem_limit_bytes=VMEM_LIMIT)


def _dg(a, b, ca, cb):
    return lax.dot_general(a.astype(BF16), b.astype(BF16), (((ca,), (cb,)), ((), ())), preferred_element_type=F32)


@jax.custom_vjp
def mm_nn(a, b):
    return _dg(a, b, 1, 0)


mm_nn.defvjp(lambda a, b: (_dg(a, b, 1, 0), (a, b)), lambda res, g: (_dg(g, res[1], 1, 1), _dg(res[0], g, 0, 0)))


@jax.custom_vjp
def mm_nt(a, b):
    return _dg(a, b, 1, 1)


mm_nt.defvjp(lambda a, b: (_dg(a, b, 1, 1), (a, b)), lambda res, g: (_dg(g, res[1], 1, 0), _dg(g, res[0], 0, 0)))


@functools.partial(jax.custom_vjp, nondiff_argnums=(1,))
def lane_roll(x, shift):
    return pltpu.roll(x, shift, 1)


lane_roll.defvjp(lambda x, shift: (pltpu.roll(x, shift, 1), None),
                 lambda shift, _, g: (pltpu.roll(g, (128 - shift) % 128, 1),))


def rms(x, g):
    return x * lax.rsqrt(jnp.mean(x * x, axis=-1, keepdims=True) + EPS) * g


@jax.custom_vjp
def softmax_rows(s):
    e = jnp.exp(s - jnp.max(s, axis=-1, keepdims=True))
    return e / jnp.sum(e, axis=-1, keepdims=True)


def _softmax_rows_fwd(s):
    p = softmax_rows(s)
    return p, p


def _softmax_rows_bwd(p, g):
    return (p * (g - jnp.sum(g * p, axis=-1, keepdims=True)),)


softmax_rows.defvjp(_softmax_rows_fwd, _softmax_rows_bwd)


def silu(x):
    return x * jax.nn.sigmoid(x)


ANY = pl.BlockSpec(memory_space=pl.ANY)
MESH_ID = pl.DeviceIdType.MESH


def _dma_sems(n):
    return [pltpu.SemaphoreType.DMA((n,)), pltpu.SemaphoreType.DMA((n,))]


class Plan:
    def __init__(self, operands, out_shape, aliases, n_sems, copies):
        self.operands, self.out_shape, self.aliases, self.n_sems, self.copies = list(operands), list(out_shape), aliases, n_sems, copies
        self.results = None


def hosted_call(kern, *, name, grid, in_specs, out_specs, out_shape, operands, scratch_shapes=(), aliases=None, cparams=None, plans=()):
    n_in, n_out, n_scr = len(in_specs), len(out_specs), len(scratch_shapes)
    p_in, p_out = [len(p.operands) for p in plans], [len(p.out_shape) for p in plans]
    all_aliases = dict(aliases or {})
    in_off, out_off = n_in, n_out
    for p, ni, no in zip(plans, p_in, p_out):
        all_aliases.update({in_off + i: out_off + o for i, o in p.aliases.items()})
        in_off, out_off = in_off + ni, out_off + no

    def body(*refs):
        pos, pins, pouts = n_in, [], []
        for ni in p_in:
            pins.append(refs[pos:pos + ni])
            pos += ni
        main_out = refs[pos:pos + n_out]
        pos += n_out
        for no in p_out:
            pouts.append(refs[pos:pos + no])
            pos += no
        main_scr = refs[pos:pos + n_scr]
        pos += n_scr
        if plans:
            ids = [pl.program_id(ax) for ax in range(len(grid))]
            first = functools.reduce(jnp.logical_and, [i == 0 for i in ids])
            last = functools.reduce(jnp.logical_and, [i == g - 1 for i, g in zip(ids, grid)])
            copies = [p.copies(pins[k], pouts[k], refs[pos + 2 * k], refs[pos + 2 * k + 1]) for k, p in enumerate(plans)]

            @pl.when(first)
            def _():
                for sends, _ in copies:
                    for cp in sends:
                        cp.start()

        kern(*refs[:n_in], *main_out, *main_scr)
        if plans:
            @pl.when(last)
            def _():
                for sends, recvs in copies:
                    for cp in recvs:
                        cp.wait_recv()
                    for cp in sends:
                        cp.wait_send()

    res = pl.pallas_call(body, grid=grid, in_specs=list(in_specs) + [ANY] * sum(p_in), out_specs=list(out_specs) + [ANY] * sum(p_out),
                         out_shape=list(out_shape) + [s for p in plans for s in p.out_shape],
                         scratch_shapes=list(scratch_shapes) + [s for p in plans for s in _dma_sems(p.n_sems)],
                         input_output_aliases=all_aliases, name=name, compiler_params=cparams or _cparams())(
        *operands, *[a for p in plans for a in p.operands])
    pos = n_out
    for p, no in zip(plans, p_out):
        p.results = list(res[pos:pos + no])
        pos += no
    return list(res[:n_out])


def stage(name, fn, grid, ins, outs):
    n_in = len(ins)

    def kern(*refs):
        res = fn(*[r[...] for r in refs[:n_in]])
        for r, v in zip(refs[n_in:], res):
            r[...] = v.astype(r.dtype)

    return pl.pallas_call(kern, grid=grid, in_specs=[s for _, s in ins], out_specs=[s for _, s in outs],
                          out_shape=[sd for sd, _ in outs], name=name, compiler_params=_cparams())(*[a for a, _ in ins])


def stage_bwd(name, fn, grid, ins, cts, diffs, plans=()):
    n_in, n_ct = len(ins), len(cts)
    didx = [i for i, d in enumerate(diffs) if d is not None]
    opts = {i: (diffs[i][3] if len(diffs[i]) > 3 else {}) for i in didx if diffs[i][0] == "row"}
    adds = [(i, opts[i]["add"]) for i in opts if "add" in opts[i]]
    intos = [(i, opts[i]["into"]) for i in opts if "into" in opts[i]]
    n_add, n_into = len(adds), len(intos)
    add_pos = {i: n_in + n_ct + k for k, (i, _) in enumerate(adds)}
    n_extra = n_in + n_ct + n_add + n_into

    def kern(*refs):
        vals = [r[...] for r in refs[:n_in]]

        def f(*dv):
            full = list(vals)
            for i, v in zip(didx, dv):
                full[i] = v
            return fn(*full)

        _, vjp = jax.vjp(f, *[vals[i].astype(F32) for i in didx])
        gs = vjp(tuple(c[...].astype(F32) for c in refs[n_in:n_in + n_ct]))
        for o_ref, i, g in zip(refs[n_extra:], didx, gs):
            if diffs[i][0] == "row":
                if i in add_pos:
                    g = g + refs[add_pos[i]][...].astype(F32)
                o_ref[...] = g.astype(o_ref.dtype)
            else:
                first = functools.reduce(jnp.logical_and, [pl.program_id(ax) == 0 for ax in diffs[i][1]])

                @pl.when(first)
                def _():
                    o_ref[...] = g

                @pl.when(jnp.logical_not(first))
                def _():
                    o_ref[...] += g

    out_shape, out_specs = [], []
    for i in didx:
        if diffs[i][0] == "row":
            out_shape.append(diffs[i][1])
            out_specs.append(diffs[i][2])
        else:
            out_shape.append(SDS(ins[i][0].shape, F32))
            out_specs.append(ins[i][1])
    aliases = {n_in + n_ct + n_add + k: didx.index(i) for k, (i, _) in enumerate(intos)}
    in_specs = [s for _, s in ins] + [s for _, s in cts] + [s for _, (_, s) in adds] + [ANY] * n_into
    operands = [a for a, _ in ins] + [a for a, _ in cts] + [a for _, (a, _) in adds] + [a for _, a in intos]
    return hosted_call(kern, name=name, grid=grid, in_specs=in_specs, out_specs=out_specs, out_shape=out_shape, operands=operands,
                       aliases=aliases, plans=plans)


def rspec(tl, w, cb=0):
    return pl.BlockSpec((tl, w), lambda i: (i, cb))


def cspec(shape):
    return pl.BlockSpec(shape, lambda i: (0,) * len(shape))


def row_fwd(name, fn, rows, tl, row_ins, consts, outs):
    ins = [(a, rspec(tl, w, cb)) for a, w, cb in row_ins] + [(a, cspec(a.shape)) for a in consts]
    return stage(name, fn, (rows // tl,), ins, [(SDS((rows, w), dt), rspec(tl, w)) for w, dt in outs])


def row_bwd(name, fn, rows, tl, row_ins, consts, cts, row_diff, const_diff, plans=()):
    ins = [(a, rspec(tl, w, cb)) for a, w, cb in row_ins] + [(a, cspec(a.shape)) for a in consts]
    diffs = []
    for (a, w, cb), d in zip(row_ins, row_diff):
        if not d:
            diffs.append(None)
            continue
        d = d if isinstance(d, dict) else {}
        opts = {}
        if "add" in d:
            opts["add"] = (d["add"][0], rspec(tl, d["add"][1], d["add"][2]))
        if d.get("into") is not None:
            opts["into"] = d["into"]
        diffs.append(("row", SDS((rows, d.get("cols", w)), d.get("dtype", F32)), rspec(tl, w, d.get("cb", 0)), opts))
    diffs += [("acc", (0,)) if d else None for d in const_diff]
    return stage_bwd(name, fn, (rows // tl,), ins, [(a, rspec(tl, w, cb)) for a, w, cb in cts], diffs, plans=plans)


MATMUL_VMEM = 36 * 2**20
ADAMW_VMEM = 28 * 2**20


class Sharded:
    def __init__(self, arr, kind, roff, rows):
        self.arr, self.kind, self.roff, self.rows, self.n = arr, kind, roff, rows, arr.shape[2]
        self.shape = (rows, 4 * self.n) if kind == "col" else (4 * rows, self.n)

    def fits(self, t0, t1):
        return self.roff % t0 == 0 and self.rows % t0 == 0 and self.n % t1 == 0

    def spec(self, t0, t1, bidx):
        assert self.fits(t0, t1), (self.kind, self.roff, self.rows, self.n, t0, t1)
        r0 = self.roff // t0
        if self.kind == "col":
            per = self.n // t1
            return pl.BlockSpec((None, t0, t1), lambda *g: (bidx(*g)[1] // per, r0 + bidx(*g)[0], bidx(*g)[1] % per))
        per = self.rows // t0
        return pl.BlockSpec((None, t0, t1), lambda *g: (bidx(*g)[0] // per, r0 + bidx(*g)[0] % per, bidx(*g)[1]))


def matmul(name, a, b, mode, out_dtype=F32, add=None, out=None, into=None, plans=()):
    if mode == "tn":
        k_dim, m = a.shape
    else:
        m, k_dim = a.shape
    n = b.shape[0] if mode == "nt" else b.shape[1]
    b_fit = b.fits if isinstance(b, Sharded) else (lambda t0, t1: True)
    o_fit = out.fits if out is not None else (lambda t0, t1: True)
    a_bytes, b_bytes = jnp.dtype(a.dtype).itemsize, jnp.dtype(b.arr.dtype if isinstance(b, Sharded) else b.dtype).itemsize
    o_bytes = jnp.dtype(out_dtype if out is None else out.arr.dtype).itemsize

    def vmem(tm, tn, tk):
        return 2 * (tm * tk * a_bytes + tk * tn * b_bytes + tm * tn * (o_bytes + (4 if add is not None else 0))) + 4 * tm * tn * (1 + (tk < k_dim))

    tiles = [(tm, tn, tk) for tm in (2048, 1024, 512, 256, 128) for tn in (1024, 768, 512, 384, 256, 128)
             for tk in sorted({k_dim, 1024, 768, 512, 384, 256, 128})
             if m % tm == 0 and n % tn == 0 and k_dim % tk == 0 and (b_fit(tn, tk) if mode == "nt" else b_fit(tk, tn)) and o_fit(tm, tn)
             and vmem(tm, tn, tk) <= MATMUL_VMEM]
    tm, tn, tk = max(tiles, key=lambda t: (t[2] == k_dim, t[0] * t[1] * t[2], t[0] * t[1]))
    nk = k_dim // tk
    a_spec = pl.BlockSpec((tk, tm), lambda i, j, k: (k, i)) if mode == "tn" else pl.BlockSpec((tm, tk), lambda i, j, k: (i, k))
    if isinstance(b, Sharded):
        b_spec = b.spec(tn, tk, lambda i, j, k: (j, k)) if mode == "nt" else b.spec(tk, tn, lambda i, j, k: (k, j))
        b = b.arr
    else:
        b_spec = pl.BlockSpec((tn, tk), lambda i, j, k: (j, k)) if mode == "nt" else pl.BlockSpec((tk, tn), lambda i, j, k: (k, j))
    o_spec = pl.BlockSpec((tm, tn), lambda i, j, k: (i, j))
    out_spec, out_shape = (o_spec, SDS((m, n), out_dtype)) if out is None else (out.spec(tm, tn, lambda i, j, k: (i, j)), out.arr)
    ca, cb = {"nn": (1, 0), "nt": (1, 1), "tn": (0, 0)}[mode]
    n_in = 2 + (add is not None)

    def finish(refs, o_ref, r):
        if add is not None:
            r = r + refs[2][...]
        o_ref[...] = r.astype(o_ref.dtype)

    def kern_whole(*refs):
        finish(refs, refs[-1], _dg(refs[0][...], refs[1][...], ca, cb))

    def kern_cut(*refs):
        o_ref, acc = refs[-2], refs[-1]
        k = pl.program_id(2)

        @pl.when(k == 0)
        def _():
            acc[...] = jnp.zeros_like(acc)

        acc[...] += _dg(refs[0][...], refs[1][...], ca, cb)

        @pl.when(k == nk - 1)
        def _():
            finish(refs, o_ref, acc[...])

    ins, specs = [a, b], [a_spec, b_spec]
    if add is not None:
        ins.append(add)
        specs.append(o_spec)
    if into is not None:
        ins.append(into)
        specs.append(ANY)
    return hosted_call(kern_whole if nk == 1 else kern_cut, name=name, grid=(m // tm, n // tn, nk), in_specs=specs, out_specs=[out_spec],
                       out_shape=[out_shape], operands=ins, scratch_shapes=[] if nk == 1 else [pltpu.VMEM((tm, tn), F32)],
                       aliases={} if into is None else {n_in: 0}, plans=plans)[0]


def _cmul(ar, ai, br, bi):
    return ar * br - ai * bi, ar * bi + ai * br


def _sub_shift(x, down):
    row = lax.broadcasted_iota(jnp.int32, x.shape, 0)
    if down:
        return jnp.where(row == 0, 0.0, pltpu.roll(x, 1, 0))
    return jnp.where(row == SEG - 1, 0.0, pltpu.roll(x, SEG - 1, 0))


def _pow_seg_len(ar, ai):
    for _ in range(int(math.log2(SEG_LEN))):
        ar, ai = _cmul(ar, ai, ar, ai)
    return ar, ai


def _scan_in_place(sr, si, a_re, a_im):
    lanes = sr.shape[1]
    ar = jnp.broadcast_to(a_re, (SEG, lanes))
    ai = jnp.broadcast_to(a_im, (SEG, lanes))
    zero = jnp.zeros((SEG, lanes), F32)

    def local(i, carry):
        rows = pl.ds(pl.multiple_of(i * SEG, SEG), SEG)
        mr, mi = _cmul(ar, ai, carry[0], carry[1])
        nr, ni = mr + sr[rows, :], mi + si[rows, :]
        sr[rows, :] = nr
        si[rows, :] = ni
        return nr, ni

    fr, fi = lax.fori_loop(0, SEG_LEN, local, (zero, zero))
    pr, pi = _pow_seg_len(ar, ai)
    ir, ii = zero, zero
    for _ in range(SEG - 1):
        mr, mi = _cmul(pr, pi, ir, ii)
        ir, ii = _sub_shift(mr + fr, True), _sub_shift(mi + fi, True)

    def carry_in(i, pw):
        rows = pl.ds(pl.multiple_of(i * SEG, SEG), SEG)
        cr, ci = _cmul(pw[0], pw[1], ir, ii)
        sr[rows, :] += cr
        si[rows, :] += ci
        return _cmul(pw[0], pw[1], ar, ai)

    lax.fori_loop(0, SEG_LEN, carry_in, (ar, ai))


S5_LANES = 8 * SP
S5_BLOCKS = SN // S5_LANES


def _s5_specs():
    u_spec = pl.BlockSpec((L, 8 * SC), lambda j: (0, j))
    s_spec = pl.BlockSpec((L, S5_LANES), lambda j: (0, j))
    wb_spec = pl.BlockSpec((S5_LANES, 8 * SC), lambda j: (j, 0))
    wc_spec = pl.BlockSpec((8 * SC, S5_LANES), lambda j: (j, 0))
    a_spec = pl.BlockSpec((1, S5_LANES), lambda j: (0, j))
    d_spec = pl.BlockSpec((1, 8 * SC), lambda j: (0, j))
    return u_spec, s_spec, wb_spec, wc_spec, a_spec, d_spec


def s5_forward(proj, wb_re, wb_im, wc_re, wc_im, a_re, a_im, d, plans=()):
    def kern(u_ref, wbr, wbi, wcr, wci, ar, ai, d_ref, sr, si, g_ref):
        u = u_ref[...]
        sr[...], si[...] = fn_s5_bu(u, wbr[...], wbi[...])
        _scan_in_place(sr, si, ar[...], ai[...])
        g_ref[...] = fn_s5_out(sr[...], si[...], u, d_ref[...], wcr[...], wci[...])[0].astype(g_ref.dtype)

    u_spec, s_spec, wb_spec, wc_spec, a_spec, d_spec = _s5_specs()
    return hosted_call(kern, name="s5_forward", grid=(S5_BLOCKS,), in_specs=[u_spec, wb_spec, wb_spec, wc_spec, wc_spec, a_spec, a_spec, d_spec],
                       out_specs=[s_spec, s_spec, u_spec], out_shape=[SDS((L, SN), F32)] * 2 + [SDS((L, PW), BF16)],
                       operands=[proj, wb_re, wb_im, wc_re, wc_im, a_re, a_im, d], plans=plans)


def _adjoint_scan_in_place(lr, li, sr, si, a_re, a_im):
    lanes = lr.shape[1]
    ar = jnp.broadcast_to(a_re, (SEG, lanes))
    ai = -jnp.broadcast_to(a_im, (SEG, lanes))
    zero = jnp.zeros((SEG, lanes), F32)

    def local(k, carry):
        i = SEG_LEN - 1 - k
        rows = pl.ds(pl.multiple_of(i * SEG, SEG), SEG)
        mr, mi = _cmul(ar, ai, carry[0], carry[1])
        nr, ni = mr + lr[rows, :], mi + li[rows, :]
        lr[rows, :] = nr
        li[rows, :] = ni
        return nr, ni

    fr, fi = lax.fori_loop(0, SEG_LEN, local, (zero, zero))
    pr, pi = _pow_seg_len(ar, ai)
    ir, ii = zero, zero
    for _ in range(SEG - 1):
        mr, mi = _cmul(pr, pi, ir, ii)
        ir, ii = _sub_shift(mr + fr, False), _sub_shift(mi + fi, False)

    def fix(rows, pw):
        cr, ci = _cmul(pw[0], pw[1], ir, ii)
        tr, ti = lr[rows, :] + cr, li[rows, :] + ci
        lr[rows, :] = tr
        li[rows, :] = ti
        return tr, ti

    def grad_a(tr, ti, spr, spi, acc):
        return acc[0] + tr * spr + ti * spi, acc[1] + ti * spr - tr * spi

    def carry_in(k, c):
        i = SEG_LEN - 1 - k
        rows = pl.ds(pl.multiple_of(i * SEG, SEG), SEG)
        prev = pl.ds(pl.multiple_of((i - 1) * SEG, SEG), SEG)
        tr, ti = fix(rows, (c[0], c[1]))
        acc = grad_a(tr, ti, sr[prev, :], si[prev, :], (c[2], c[3]))
        nr, ni = _cmul(c[0], c[1], ar, ai)
        return nr, ni, acc[0], acc[1]

    pwr, pwi, accr, acci = lax.fori_loop(0, SEG_LEN - 1, carry_in, (ar, ai, zero, zero))
    tr, ti = fix(pl.ds(0, SEG), (pwr, pwi))
    last = pl.ds((SEG_LEN - 1) * SEG, SEG)
    accr, acci = grad_a(tr, ti, _sub_shift(sr[last, :], True), _sub_shift(si[last, :], True), (accr, acci))
    return jnp.sum(accr, axis=0, keepdims=True), jnp.sum(acci, axis=0, keepdims=True)


S5_BWD_VMEM = 58 * 2**20


def s5_backward(dg, proj, s_re, s_im, wb_re, wb_im, wc_re, wc_im, a_re, a_im, d, dproj, plans=()):
    def kern(dg_ref, u_ref, sr, si, wbr, wbi, wcr, wci, ar, ai, d_ref, _, du_ref, dd_ref, dwcr, dwci, dwbr, dwbi, dar, dai, lr, li):
        u = u_ref[...]
        _, vjp_out = jax.vjp(fn_s5_out, sr[...], si[...], u, d_ref[...], wcr[...], wci[...])
        lr[...], li[...], du_out, dd_ref[...], dwcr[...], dwci[...] = vjp_out((dg_ref[...],))
        dar[...], dai[...] = _adjoint_scan_in_place(lr, li, sr, si, ar[...], ai[...])
        _, vjp_in = jax.vjp(fn_s5_bu, u, wbr[...], wbi[...])
        du_in, dwbr[...], dwbi[...] = vjp_in((lr[...], li[...]))
        du_ref[...] = (du_out + du_in).astype(du_ref.dtype)

    u_spec, s_spec, wb_spec, wc_spec, a_spec, d_spec = _s5_specs()
    outs = [(SDS(dproj.shape, dproj.dtype), u_spec), (SDS(d.shape, F32), d_spec), (SDS(wc_re.shape, F32), wc_spec), (SDS(wc_im.shape, F32), wc_spec),
            (SDS(wb_re.shape, F32), wb_spec), (SDS(wb_im.shape, F32), wb_spec), (SDS(a_re.shape, F32), a_spec), (SDS(a_im.shape, F32), a_spec)]
    return hosted_call(kern, name="s5_backward", grid=(S5_BLOCKS,),
                       in_specs=[u_spec, u_spec, s_spec, s_spec, wb_spec, wb_spec, wc_spec, wc_spec, a_spec, a_spec, d_spec, ANY],
                       out_specs=[sp for _, sp in outs], out_shape=[sd for sd, _ in outs], aliases={11: 0},
                       operands=[dg, proj, s_re, s_im, wb_re, wb_im, wc_re, wc_im, a_re, a_im, d, dproj],
                       scratch_shapes=[pltpu.VMEM((L, S5_LANES), F32)] * 2,
                       cparams=pltpu.CompilerParams(vmem_limit_bytes=S5_BWD_VMEM), plans=plans)


def fn_rms(x, g):
    return (rms(x, g),)


def fn_s5_disc(lre, lim, ls):
    step = jnp.exp(ls)
    e = jnp.exp(lre * step)
    a_re, a_im = e * jnp.cos(lim * step), e * jnp.sin(lim * step)
    den = lre * lre + lim * lim
    nr, ni = a_re - 1.0, a_im
    return a_re, a_im, (nr * lre + ni * lim) / den, (ni * lre - nr * lim) / den


def _group_mask(rows, cols, row_div, col_div):
    r = lax.broadcasted_iota(jnp.int32, (rows, cols), 0) // row_div % 8
    c = lax.broadcasted_iota(jnp.int32, (rows, cols), 1) // col_div
    return r == c


def _spread(x, mask):
    w = x.shape[1]
    copy = (lax.broadcasted_iota(jnp.int32, (w, 8 * w), 1) % w == lax.broadcasted_iota(jnp.int32, (w, 8 * w), 0)).astype(F32)
    return jnp.where(mask, jnp.dot(x, copy, precision=lax.Precision.HIGHEST, preferred_element_type=F32), 0.0)


def fn_s5_bmat(b_re, b_im, coef_re, coef_im):
    mask = _group_mask(b_re.shape[0], 8 * SC, SP, SC)
    return _spread(coef_re * b_re - coef_im * b_im, mask), _spread(coef_re * b_im + coef_im * b_re, mask)


def fn_s5_cmat(c_re, c_im):
    mask = _group_mask(c_re.shape[0], 8 * SP, SC, SP)
    return _spread(c_re, mask), _spread(c_im, mask)


def fn_s5_bu(u, wb_re, wb_im):
    return mm_nt(u, wb_re), mm_nt(u, wb_im)


def fn_s5_out(sr, si, u, d, wc_re, wc_im):
    y = mm_nt(sr, wc_re) - mm_nt(si, wc_im) + d * u
    return (jax.nn.gelu(y),)


def fn_merge_glu(z, mo, gate):
    yg = z[:, :PW] * jax.nn.sigmoid(z[:, PW:])
    return (jnp.concatenate([yg, mo], axis=1) * silu(gate),)


def fn_merge(prim, mo, gate):
    return (jnp.concatenate([prim, mo], axis=1) * silu(gate),)


def fn_mem_k(kv, g):
    return (jnp.concatenate([rms(kv[:, h * XHD:(h + 1) * XHD], g) for h in range(XH)], axis=1),)


def fn_mem_attn(xq, kn, v, g):
    outs = []
    for h in range(XH):
        sl = slice(h * XHD, (h + 1) * XHD)
        p = softmax_rows(mm_nt(rms(xq[:, sl], g), kn[:, sl]) * (XHD ** -0.5))
        outs.append(mm_nn(p, v[:, sl]))
    return (jnp.concatenate(outs, axis=1),)


def _half_rms(x, g):
    lo = lax.broadcasted_iota(jnp.int32, x.shape, 1) < ROPE
    x2 = x * x
    s_lo = jnp.sum(jnp.where(lo, x2, 0.0), axis=1, keepdims=True)
    s_hi = jnp.sum(jnp.where(lo, 0.0, x2), axis=1, keepdims=True)
    return x * lax.rsqrt(jnp.where(lo, s_lo, s_hi) / ROPE + EPS) * g


def _rope(x, cos2, sin_signed):
    first = lax.broadcasted_iota(jnp.int32, x.shape, 1) % ROPE < ROPE // 2
    return x * cos2 + jnp.where(first, lane_roll(x, 128 - ROPE // 2), lane_roll(x, ROPE // 2)) * sin_signed


def fn_mla_prep(q, kv, kr, cos2, sin_signed, qnn, knn, qrn, krn):
    lo = lax.broadcasted_iota(jnp.int32, kr.shape, 1) < ROPE
    kr_pad = jnp.where(lo, _rope(_half_rms(kr, krn), cos2, sin_signed), 0.0)
    qf, kf, vs = [], [], []
    for m in range(MH // 2):
        pair = _rope(_half_rms(q[:, MH * NOPE + 128 * m:MH * NOPE + 128 * (m + 1)], qrn), cos2, sin_signed)
        for h, rope_h in ((2 * m, pair), (2 * m + 1, lane_roll(pair, ROPE))):
            qf.append(jnp.concatenate([rms(q[:, NOPE * h:NOPE * (h + 1)], qnn), jnp.where(lo, rope_h, 0.0)], axis=1))
    for h in range(MH):
        kf.append(jnp.concatenate([rms(kv[:, 256 * h:256 * h + NOPE], knn), kr_pad], axis=1))
        vs.append(kv[:, 256 * h + NOPE:256 * (h + 1)])
    return jnp.stack(qf), jnp.stack(kf), jnp.stack(vs)


ATT_TQ = 512


def _attn_tile(q, kf, v):
    tq = q.shape[0]
    scale = (NOPE + ROPE) ** -0.5
    own = mm_nt(q, kf[-tq:]) * scale
    own = jnp.where(lax.broadcasted_iota(jnp.int32, own.shape, 1) <= lax.broadcasted_iota(jnp.int32, own.shape, 0), own, jnp.finfo(F32).min)
    s = own if kf.shape[0] == tq else jnp.concatenate([mm_nt(q, kf[:-tq]) * scale, own], axis=1)
    return mm_nn(softmax_rows(s), v)


def _attn_specs():
    q_spec = pl.BlockSpec((None, ATT_TQ, 256), lambda h, i: (h, i, 0))
    k_spec = pl.BlockSpec((None, L, 256), lambda h, i: (h, 0, 0))
    v_spec = pl.BlockSpec((None, L, 128), lambda h, i: (h, 0, 0))
    o_spec = pl.BlockSpec((ATT_TQ, 128), lambda h, i: (i, h))
    return q_spec, k_spec, v_spec, o_spec


def causal_attn(qf, kf, vh):
    n_tiles = L // ATT_TQ

    def kern(q_ref, k_ref, v_ref, o_ref):
        i = pl.program_id(1)
        for t in range(n_tiles):
            @pl.when(i == t)
            def _(t=t):
                keys = (t + 1) * ATT_TQ
                o_ref[...] = _attn_tile(q_ref[...], k_ref[:keys, :], v_ref[:keys, :])

    q_spec, k_spec, v_spec, o_spec = _attn_specs()
    return pl.pallas_call(kern, grid=(MH, n_tiles), in_specs=[q_spec, k_spec, v_spec], out_specs=o_spec,
                          out_shape=SDS((L, MH * VD), F32), name="l1_attn", compiler_params=_cparams())(qf, kf, vh)


def causal_attn_bwd(qf, kf, vh, dout):
    n_tiles = L // ATT_TQ

    def kern(q_ref, k_ref, v_ref, do_ref, dq_ref, dk_ref, dv_ref):
        i = pl.program_id(1)

        @pl.when(i == 0)
        def _():
            dk_ref[...] = jnp.zeros_like(dk_ref)
            dv_ref[...] = jnp.zeros_like(dv_ref)

        for t in range(n_tiles):
            @pl.when(i == t)
            def _(t=t):
                keys = (t + 1) * ATT_TQ
                _, vjp = jax.vjp(_attn_tile, q_ref[...].astype(F32), k_ref[:keys, :].astype(F32), v_ref[:keys, :].astype(F32))
                dq, dk, dv = vjp(do_ref[...])
                dq_ref[...] = dq
                dk_ref[:keys, :] += dk
                dv_ref[:keys, :] += dv

    q_spec, k_spec, v_spec, o_spec = _attn_specs()
    return pl.pallas_call(kern, grid=(MH, n_tiles), in_specs=[q_spec, k_spec, v_spec, o_spec], out_specs=[q_spec, k_spec, v_spec],
                          out_shape=[SDS(qf.shape, F32), SDS(kf.shape, F32), SDS(vh.shape, F32)], name="l1_attn_bwd",
                          compiler_params=_cparams())(qf, kf, vh, dout)


def loss_and_grad(y, target, tl=256):
    def kern(y_ref, t_ref, dy_ref, loss_ref):
        d = y_ref[...] - t_ref[...]
        dy_ref[...] = d / D

        @pl.when(pl.program_id(0) == 0)
        def _():
            loss_ref[...] = jnp.zeros_like(loss_ref)

        loss_ref[...] += 0.5 * jnp.sum(jnp.sum(d * d, axis=1, keepdims=True), axis=0, keepdims=True) / D

    return pl.pallas_call(kern, grid=(L // tl,), in_specs=[rspec(tl, D), rspec(tl, D)], out_specs=[rspec(tl, D), cspec((1, 1))],
                          out_shape=[SDS((L, D), F32), SDS((1, 1), F32)], name="loss", compiler_params=_cparams())(y, target)


def adamw(name, w, g, m, v):
    rows, cols = w.shape
    block_row_bytes = 7 * 2 * 4 * max(cols, 128)
    tr = _row_tile(rows, min(2048, ADAMW_VMEM // block_row_bytes // 8 * 8), 8)

    def kern(w_ref, g_ref, m_ref, v_ref, d_ref, nm_ref, nv_ref):
        gg = g_ref[...]
        nm = ADAM_B1 * m_ref[...] + (1.0 - ADAM_B1) * gg
        nv = ADAM_B2 * v_ref[...] + (1.0 - ADAM_B2) * jnp.square(gg)
        m_hat = nm / (1.0 - ADAM_B1 ** ADAM_STEP)
        v_hat = nv / (1.0 - ADAM_B2 ** ADAM_STEP)
        d_ref[...] = -ADAM_LR * (m_hat / (jnp.sqrt(v_hat) + ADAM_EPS) + ADAM_WD * w_ref[...])
        nm_ref[...] = nm
        nv_ref[...] = nv

    spec = rspec(tr, cols)
    return pl.pallas_call(kern, grid=(rows // tr,), in_specs=[spec] * 4, out_specs=[spec] * 3,
                          out_shape=[SDS((rows, cols), F32)] * 3, name=name, compiler_params=_cparams())(w, g, m, v)


def _row_tile(rows, cap=512, unit=16):
    return max(t for t in range(unit, cap + 1, unit) if rows % t == 0)


def _place():
    x, y, c = lax.axis_index("x"), lax.axis_index("y"), lax.axis_index("c")
    return x, y, c, [(1 - x, y), (x, 1 - y), (1 - x, 1 - y)]


def _row_chunks(rows, n, dtype):
    unit = 32 // jnp.dtype(dtype).itemsize
    base, extra = divmod(rows // unit, n)
    out, start = [], 0
    for k in range(n):
        size = (base + (k < extra)) * unit
        if size:
            out.append((start, size))
            start += size
    assert start == rows, (rows, unit)
    return out


PIECE_BYTES = 1 << 20


def _pieces(shapes_dtypes, rows_of):
    out = []
    for b, (shape, dtype) in enumerate(shapes_dtypes):
        rows = rows_of(shape)
        n = max(1, min(4, rows * shape[-1] * jnp.dtype(dtype).itemsize // PIECE_BYTES))
        out += [(b, st, sz) for st, sz in _row_chunks(rows, n, dtype)]
    return out


def all_gather_chips(name, shards):
    nb = len(shards)
    pieces = _pieces([(s.shape, s.dtype) for s in shards], lambda shape: shape[0] // 2)
    n = len(pieces)

    def body(*refs):
        x_refs, out_refs, send_sems, recv_sems = refs[:nb], refs[nb:2 * nb], refs[2 * nb], refs[2 * nb + 1]
        x, y, c, chips = _place()
        sibling = (x, y, 1 - c)
        mine = 2 * x + y

        def copy(sem, chip, cc, k, to, from_input=False):
            b, st, sz = pieces[k]
            rows_k = pl.ds(cc * (x_refs[b].shape[0] // 2) + st, sz)
            dst = out_refs[b].at[chip, rows_k, :]
            return pltpu.make_async_remote_copy(src_ref=x_refs[b].at[rows_k, :] if from_input else dst, dst_ref=dst,
                                                send_sem=send_sems.at[sem], recv_sem=recv_sems.at[sem], device_id=to, device_id_type=MESH_ID)

        order = [(k, j, 2 * cx + cy, (cx, cy, c)) for k in range(n) for j, (cx, cy) in enumerate(chips)]
        first = [copy(j * n + k, mine, c, k, to, from_input=True) for k, j, _, to in order]
        for cp in first:
            cp.start()
        passed = []
        for k, j, chip, _ in order:
            copy(j * n + k, chip, c, k, sibling).wait_recv()
            passed.append(copy((3 + j) * n + k, chip, c, k, sibling))
            passed[-1].start()
        for k, j, chip, _ in order:
            copy((3 + j) * n + k, chip, 1 - c, k, sibling).wait_recv()
        for cp in first + passed:
            cp.wait_send()

    return pl.pallas_call(body, in_specs=[ANY] * nb, out_specs=[ANY] * nb, out_shape=[SDS((4,) + s.shape, s.dtype) for s in shards],
                          scratch_shapes=_dma_sems(6 * n), name=name)(*shards)


def plan_gather_ici(shards):
    pieces = _pieces([(s.shape, s.dtype) for s in shards], lambda shape: shape[0] // 2)
    n = len(pieces)

    def copies(x_refs, out_refs, send_sems, recv_sems):
        x, y, c, chips = _place()
        mine = 2 * x + y

        def copy(j, k, chip, to, from_input):
            b, st, sz = pieces[k]
            rows_k = pl.ds(c * (x_refs[b].shape[0] // 2) + st, sz)
            dst = out_refs[b].at[chip, rows_k, :]
            return pltpu.make_async_remote_copy(src_ref=x_refs[b].at[rows_k, :] if from_input else dst, dst_ref=dst, send_sem=send_sems.at[j * n + k],
                                                recv_sem=recv_sems.at[j * n + k], device_id=to, device_id_type=MESH_ID)

        order = [(k, j, 2 * cx + cy, (cx, cy, c)) for k in range(n) for j, (cx, cy) in enumerate(chips)]
        return [copy(j, k, mine, to, True) for k, j, _, to in order], [copy(j, k, chip, to, False) for k, j, chip, to in order]

    return Plan(shards, [SDS((4,) + s.shape, s.dtype) for s in shards], {}, 3 * n, copies)


def plan_gather_pass(gathered):
    pieces = _pieces([(g.shape[1:], g.dtype) for g in gathered], lambda shape: shape[0] // 2)
    n = len(pieces)

    def copies(_, out_refs, send_sems, recv_sems):
        x, y, c, chips = _place()

        def copy(j, k, chip, cc):
            b, st, sz = pieces[k]
            rows_k = out_refs[b].at[chip, pl.ds(cc * (out_refs[b].shape[1] // 2) + st, sz), :]
            return pltpu.make_async_remote_copy(src_ref=rows_k, dst_ref=rows_k, send_sem=send_sems.at[j * n + k], recv_sem=recv_sems.at[j * n + k],
                                                device_id=(x, y, 1 - c), device_id_type=MESH_ID)

        order = [(k, j, 2 * cx + cy) for k in range(n) for j, (cx, cy) in enumerate(chips)]
        return [copy(j, k, chip, c) for k, j, chip in order], [copy(j, k, chip, 1 - c) for k, j, chip in order]

    return Plan(gathered, [SDS(g.shape, g.dtype) for g in gathered], {i: i for i in range(len(gathered))}, 3 * n, copies)


def plan_pair_exchange(gs):
    pieces = _pieces([(g.shape, g.dtype) for g in gs], lambda shape: shape[1] // 2)

    def copies(g_refs, got_refs, send_sems, recv_sems):
        x, y, c, _ = _place()
        swaps = [pltpu.make_async_remote_copy(src_ref=g_refs[b].at[:, pl.ds((1 - c) * (g_refs[b].shape[1] // 2) + st, sz), :],
                                              dst_ref=got_refs[b].at[:, pl.ds(st, sz), :], send_sem=send_sems.at[k], recv_sem=recv_sems.at[k],
                                              device_id=(x, y, 1 - c), device_id_type=MESH_ID)
                 for k, (b, st, sz) in enumerate(pieces)]
        return swaps, swaps

    return Plan(gs, [SDS((g.shape[0], g.shape[1] // 2, g.shape[2]), g.dtype) for g in gs], {}, len(pieces), copies)


def plan_chip_scatter(ps):
    pieces = _pieces([(p.shape, p.dtype) for p in ps], lambda shape: shape[1])
    n = len(pieces)

    def copies(p_refs, q_refs, send_sems, recv_sems):
        x, y, c, chips = _place()
        mine = 2 * x + y

        def copy(j, k, src_slot, dst_slot, to):
            b, st, sz = pieces[k]
            return pltpu.make_async_remote_copy(src_ref=p_refs[b].at[src_slot, pl.ds(st, sz), :], dst_ref=q_refs[b].at[dst_slot, pl.ds(st, sz), :],
                                                send_sem=send_sems.at[j * n + k], recv_sem=recv_sems.at[j * n + k], device_id=to,
                                                device_id_type=MESH_ID)

        order = [(k, j, 2 * cx + cy, (cx, cy, c)) for k in range(n) for j, (cx, cy) in enumerate(chips)]
        return [copy(j, k, chip, mine, to) for k, j, chip, to in order], [copy(j, k, mine, chip, to) for k, j, chip, to in order]

    return Plan(ps, [SDS(p.shape, p.dtype) for p in ps], {}, 3 * n, copies)


def plan_pair_join(bufs):
    pieces = _pieces([(b.shape, b.dtype) for b in bufs], lambda shape: shape[0] // 2)

    def copies(_, out_refs, send_sems, recv_sems):
        x, y, c, _ = _place()

        def copy(k, cc):
            b, st, sz = pieces[k]
            rows_k = out_refs[b].at[pl.ds(cc * (out_refs[b].shape[0] // 2) + st, sz), :]
            return pltpu.make_async_remote_copy(src_ref=rows_k, dst_ref=rows_k, send_sem=send_sems.at[k], recv_sem=recv_sems.at[k],
                                                device_id=(x, y, 1 - c), device_id_type=MESH_ID)

        return [copy(k, c) for k in range(len(pieces))], [copy(k, 1 - c) for k in range(len(pieces))]

    return Plan(bufs, [SDS(b.shape, b.dtype) for b in bufs], {i: i for i in range(len(bufs))}, len(pieces), copies)


def run_plan(name, plan):
    hosted_call(lambda: None, name=name, grid=(1,), in_specs=[], out_specs=[], out_shape=[], operands=[], plans=[plan])
    return plan.results


def pair_add(name, g, got, place):
    slots, rows, cols = g.shape
    half = rows // 2
    tr = _row_tile(half)
    nb = half // tr

    def kern(_, g_ref, t_ref, o_ref):
        o_ref[...] = (g_ref[...].astype(F32) + t_ref[...].astype(F32)).astype(o_ref.dtype)

    blk = pl.BlockSpec((None, tr, cols), lambda s, i, p: (s, i, 0))
    grid_spec = pltpu.PrefetchScalarGridSpec(
        num_scalar_prefetch=1, grid=(slots, nb),
        in_specs=[pl.BlockSpec((None, tr, cols), lambda s, i, p: (s, p[1] * nb + i, 0)), blk], out_specs=blk)
    return pl.pallas_call(kern, grid_spec=grid_spec, out_shape=SDS((slots, half, cols), g.dtype), name=name,
                          compiler_params=_cparams())(place, g, got)


def chip_add(name, p, q, place):
    slots, half, cols = p.shape
    tr = _row_tile(half)
    nb = half // tr

    def kern(_, p_ref, q1, q2, q3, o_ref):
        o_ref[...] = p_ref[...].astype(F32) + q1[...].astype(F32) + q2[...].astype(F32) + q3[...].astype(F32)

    def slot(k):
        return pl.BlockSpec((None, tr, cols), lambda i, pr: ((pr[0] + k) % slots, i, 0))

    grid_spec = pltpu.PrefetchScalarGridSpec(
        num_scalar_prefetch=1, grid=(nb,), in_specs=[slot(0), slot(1), slot(2), slot(3)],
        out_specs=pl.BlockSpec((tr, cols), lambda i, pr: (pr[1] * nb + i, 0)))
    return pl.pallas_call(kern, grid_spec=grid_spec, out_shape=SDS((2 * half, cols), F32), name=name,
                          compiler_params=_cparams())(place, p, q, q, q)


def pair_adds(tag, gs, gots, place):
    return [pair_add(f"{tag}_pair_add_{i}", g, got, place) for i, (g, got) in enumerate(zip(gs, gots))]


def chip_adds(tag, pairs, qs, place):
    return [chip_add(f"{tag}_chip_add_{i}", p, q, place) for i, (p, q) in enumerate(zip(pairs, qs))]


def reduce_scatter_chips(tag, gs, place):
    pairs = pair_adds(tag, gs, run_plan(tag + "_pair_exchange", plan_pair_exchange(gs)), place)
    return run_plan(tag + "_pair_join", plan_pair_join(chip_adds(tag, pairs, run_plan(tag + "_chip_scatter", plan_chip_scatter(pairs)), place)))


BIG = [("w_out", (2, 512, 1024)), ("w_mem_kv", (2, 256, 1024)), ("s5_w_in", (1, 1024, 1024)), ("s5_w_glu", (1, 1536, 768)),
       ("mla_w_in", (1, 1024, 848)), ("mla_w_uq", (1, 512, 576)), ("mla_w_ukv", (1, 256, 768))]
SHARDED_SMALL = [("mla_q_lora_norm", (1, 128)), ("mla_kv_lora_norm", (1, 64))]
SMALL = [("ln_gain", (2, 1024)), ("mem_norm", (2, 1024)), ("xq_norm", (2, 128)), ("xk_norm", (2, 128)),
         ("s5_lambda_re", (1, 96, 64)), ("s5_lambda_im", (1, 96, 64)), ("s5_log_step", (1, 96)),
         ("s5_b_re", (1, 96, 64, 16)), ("s5_b_im", (1, 96, 64, 16)), ("s5_c_re", (1, 96, 16, 64)), ("s5_c_im", (1, 96, 16, 64)),
         ("s5_d", (1, 1536)), ("mla_q_nope_norm", (1, 128)), ("mla_k_nope_norm", (1, 128)), ("mla_q_rope_norm", (1, 64)),
         ("mla_k_rope_norm", (1, 64))]
WEIGHT_ORDER = ["ln_gain", "w_out", "mem_norm", "w_mem_kv", "xq_norm", "xk_norm", "s5_w_in", "s5_lambda_re", "s5_lambda_im",
                "s5_log_step", "s5_b_re", "s5_b_im", "s5_c_re", "s5_c_im", "s5_d", "s5_w_glu", "mla_w_in", "mla_q_lora_norm",
                "mla_kv_lora_norm", "mla_w_uq", "mla_w_ukv", "mla_q_nope_norm", "mla_k_nope_norm", "mla_q_rope_norm", "mla_k_rope_norm"]
MINOR_LAST = {"mla_w_in": (0, 2, 1), "mla_w_uq": (0, 2, 1), "s5_b_re": (0, 2, 3, 1), "s5_b_im": (0, 2, 3, 1),
              "s5_c_re": (0, 2, 3, 1), "s5_c_im": (0, 2, 3, 1)}
SMALL_FULL = SMALL + [(n, (1, 4 * s[1])) for n, s in SHARDED_SMALL]
N_SMALL = sum(math.prod(s) for _, s in SMALL_FULL)
SMALL_ROWS, SMALL_LANES = 128, 1024

WIDE0_IN, WIDE0_OUT, WIDE0_MKV = 0, 1024, 1536
PAIR_OUT, PAIR_MKV, PAIR_ROWS = 0, 512, 768


def stack_shards(w, dtype):
    wide0 = jnp.concatenate([w["s5_w_in"][0], w["w_out"][0], w["w_mem_kv"][0]], axis=0)
    wide1 = jnp.concatenate([w["w_out"][1], w["w_mem_kv"][1]], axis=0)
    return ([wide0.astype(dtype), w["s5_w_glu"][0].astype(dtype)],
            [wide1.astype(dtype), w["mla_w_ukv"][0].astype(dtype), w["mla_w_in"][0].astype(dtype), w["mla_w_uq"][0].astype(dtype)])


def weight_views0(wide0, glu):
    return {"s5_w_in": Sharded(wide0, "col", WIDE0_IN, 1024), "w_out": Sharded(wide0, "row", WIDE0_OUT, 512),
            "w_mem_kv": Sharded(wide0, "row", WIDE0_MKV, 256), "s5_w_glu": Sharded(glu, "col", 0, 1536)}


def weight_views1(wide1, ukv):
    return {"w_out": Sharded(wide1, "row", PAIR_OUT, 512), "w_mem_kv": Sharded(wide1, "row", PAIR_MKV, 256), "mla_w_ukv": Sharded(ukv, "col", 0, 256)}


def grad_views():
    pair = SDS((4, PAIR_ROWS, 1024), BF16)
    return {"w_out": Sharded(pair, "row", PAIR_OUT, 512), "w_mem_kv": Sharded(pair, "row", PAIR_MKV, 256),
            "s5_w_in": Sharded(SDS((4, 1024, 1024), BF16), "col", 0, 1024), "s5_w_glu": Sharded(SDS((4, 1536, 768), BF16), "col", 0, 1536),
            "mla_w_ukv": Sharded(SDS((4, 256, 768), BF16), "col", 0, 256)}


def cols_to_shards(full):
    return full.reshape(full.shape[0], 4, full.shape[1] // 4).transpose(1, 0, 2)


def shards_to_cols(arr):
    return arr.transpose(1, 0, 2).reshape(arr.shape[1], 4 * arr.shape[2])


def mla_in_permute(w):
    o1, o2, o3, o4 = QL, QL + KVL, QL + KVL + ROPE, QL + KVL + ROPE + XQW
    return jnp.concatenate([w[:, o4:], w[:, :o1], w[:, o3:o4], w[:, o1:o2], w[:, o2:o3],
                            jnp.zeros((w.shape[0], MLA_IN_P - MLA_IN), w.dtype)], axis=1)


def mla_in_unpermute(d):
    return jnp.concatenate([d[:, 2048:2560], d[:, 3072:3328], d[:, 3328:3392], d[:, 2560:3072], d[:, :2048]], axis=1)


def uq_permute(w):
    w3 = w.reshape(w.shape[0], MH, NOPE + ROPE)
    return jnp.concatenate([w3[:, :, :NOPE].reshape(w.shape[0], MH * NOPE), w3[:, :, NOPE:].reshape(w.shape[0], MH * ROPE)], axis=1)


def uq_unpermute(d):
    dn = d[:, :MH * NOPE].reshape(d.shape[0], MH, NOPE)
    dr = d[:, MH * NOPE:].reshape(d.shape[0], MH, ROPE)
    return jnp.concatenate([dn, dr], axis=2).reshape(d.shape[0], MH * (NOPE + ROPE))


def time_permute(a):
    return a.reshape(SEG, SEG_LEN, a.shape[-1]).transpose(1, 0, 2).reshape(L, a.shape[-1])


def time_unpermute(a):
    return a.reshape(SEG_LEN, SEG, a.shape[-1]).transpose(1, 0, 2).reshape(L, a.shape[-1])


def mem_branch_fwd(tag, mem, mem_norm, w_mem_kv, xk_norm):
    mn = row_fwd(tag + "_mem_rms", fn_rms, ML, ML, [(mem, D, 0)], [mem_norm], [(D, BF16)])[0]
    kv = matmul(tag + "_mem_kv", mn, w_mem_kv, "nn")
    kn = row_fwd(tag + "_mem_knorm", fn_mem_k, ML, ML, [(kv, XQW, 0)], [xk_norm], [(XQW, F32)])[0]
    return mn, kv, kn


def mem_branch_bwd(tag, mem, mem_norm, w_mem_kv, xk_norm, mn, kv, dkn, dv, g_view, g_wide):
    dk, dxk = row_bwd(tag + "_mem_knorm_bwd", fn_mem_k, ML, ML, [(kv, XQW, 0)], [xk_norm], [(dkn, XQW, 0)], [True], [True])
    dkv = jnp.concatenate([dk, dv], axis=1)
    dmn = matmul(tag + "_mem_kv_dx", dkv, w_mem_kv, "nt")
    g_wide = matmul(tag + "_mem_kv_dw", mn, dkv, "tn", out=g_view, into=g_wide)
    dmem_norm = row_bwd(tag + "_mem_rms_bwd", fn_rms, ML, ML, [(mem, D, 0)], [mem_norm], [(dmn, D, 0)], [False], [True])[0]
    return g_wide, dmem_norm, dxk


def mem_attn_fwd(tag, proj, cb, kn, kv, xq_norm):
    return row_fwd(tag + "_mem_attn", fn_mem_attn, L, 256, [(proj, XQW, cb)], [kn, kv[:, XQW:], xq_norm], [(XQW, F32)])[0]


def mem_attn_bwd(tag, proj, cb, kn, kv, xq_norm, dmo, dproj):
    place = {"cols": proj.shape[1], "cb": cb, "into": dproj, "dtype": dproj.dtype}
    return row_bwd(tag + "_mem_attn_bwd", fn_mem_attn, L, 256, [(proj, XQW, cb)], [kn, kv[:, XQW:], xq_norm], [(dmo, XQW, 0)],
                   [place], [True, True, True])


def device_step(x, mem, positions, target, small, env, hooks=None):
    hooks = hooks or {}

    def plans_for(name):
        return hooks[("plans", name)](env) if ("plans", name) in hooks else ()

    def after(name):
        if ("after", name) in hooks:
            hooks[("after", name)](env)

    g = {}
    w0 = weight_views0(env["wide0"], env["glu"])
    gw = grad_views()
    ln, mem_norm, xq_norm, xk_norm = small["ln_gain"], small["mem_norm"], small["xq_norm"], small["xk_norm"]

    lre, lim = small["s5_lambda_re"][0], small["s5_lambda_im"][0]
    ls = small["s5_log_step"].reshape(SG, 1)
    one = pl.BlockSpec((SG, SP), lambda i: (0, 0))
    col = pl.BlockSpec((SG, 1), lambda i: (0, 0))
    disc_ins = [(lre, one), (lim, one), (ls, col)]
    a_re, a_im, coef_re, coef_im = stage("s5_disc", fn_s5_disc, (1,), disc_ins, [(SDS((SG, SP), F32), one)] * 4)
    b_re, b_im = small["s5_b_re"].reshape(SN, SC), small["s5_b_im"].reshape(SN, SC)
    c_re, c_im = small["s5_c_re"].reshape(PW, SP), small["s5_c_im"].reshape(PW, SP)
    bmat_rows = [(b_re, SC, 0), (b_im, SC, 0), (coef_re.reshape(SN, 1), 1, 0), (coef_im.reshape(SN, 1), 1, 0)]
    wb_re, wb_im = row_fwd("s5_bmat", fn_s5_bmat, SN, 512, bmat_rows, [], [(128, F32)] * 2)
    cmat_rows = [(c_re, SP, 0), (c_im, SP, 0)]
    wc_re, wc_im = row_fwd("s5_cmat", fn_s5_cmat, PW, 128, cmat_rows, [], [(512, F32)] * 2)
    a_re_v, a_im_v = a_re.reshape(1, SN), a_im.reshape(1, SN)
    s5_d = small["s5_d"]

    xp = time_permute(x)
    h0 = row_fwd("l0_rms", fn_rms, L, 256, [(xp, D, 0)], [ln[0:1]], [(D, BF16)])[0]
    proj0 = matmul("l0_in", h0, w0["s5_w_in"], "nn")
    s_re, s_im, g0 = s5_forward(proj0, wb_re, wb_im, wc_re, wc_im, a_re_v, a_im_v, s5_d, plans=plans_for("s5_forward"))
    z0 = matmul("l0_glu", g0, w0["s5_w_glu"], "nn", plans=plans_for("l0_glu"))
    after("l0_glu")
    mn0, kv0, kn0 = mem_branch_fwd("l0", mem, mem_norm[0:1], w0["w_mem_kv"], xk_norm[0:1])
    mo0 = mem_attn_fwd("l0", proj0, 3, kn0, kv0, xq_norm[0:1])
    o0 = row_fwd("l0_merge", fn_merge_glu, L, 256, [(z0, 2 * PW, 0), (mo0, XQW, 0), (proj0, BW, 1)], [], [(BW, BF16)])[0]
    x1p = matmul("l0_out", o0, w0["w_out"], "nn", add=xp)
    x1 = time_unpermute(x1p)

    w1 = weight_views1(env["wide1"], env["ukv"])
    w_in1, w_uq = env["w_in1"], env["w_uq"]
    h1 = row_fwd("l1_rms", fn_rms, L, 256, [(x1, D, 0)], [ln[1:2]], [(D, BF16)])[0]
    proj1 = matmul("l1_in", h1, w_in1, "nn")
    qln, kvln = env["q_lora_norm"].reshape(1, QL), env["kv_lora_norm"].reshape(1, KVL)
    cqn = row_fwd("l1_q_lora_rms", fn_rms, L, 256, [(proj1, QL, 4)], [qln], [(QL, BF16)])[0]
    ckvn = row_fwd("l1_kv_lora_rms", fn_rms, L, 256, [(proj1, KVL, 12)], [kvln], [(KVL, BF16)])[0]
    q = matmul("l1_uq", cqn, w_uq, "nn")
    kv = matmul("l1_ukv", ckvn, w1["mla_w_ukv"], "nn")
    inv_freq = ROPE_THETA ** (-jnp.arange(ROPE // 2, dtype=F32) / (ROPE // 2))
    ang = positions.astype(F32)[:, None] * inv_freq
    cos2 = jnp.tile(jnp.cos(ang), (1, 4))
    sin_signed = jnp.tile(jnp.concatenate([-jnp.sin(ang), jnp.sin(ang)], axis=1), (1, 2))
    qnn, knn = small["mla_q_nope_norm"], small["mla_k_nope_norm"]
    qrn, krn = jnp.tile(small["mla_q_rope_norm"], (1, 2)), jnp.tile(small["mla_k_rope_norm"], (1, 2))
    tp = 256
    prep_ins = [(q, rspec(tp, MH * (NOPE + ROPE))), (kv, rspec(tp, MH * 256)), (proj1, rspec(tp, 128, 26)),
                (cos2, rspec(tp, 128)), (sin_signed, rspec(tp, 128))] + [(a, cspec((1, 128))) for a in (qnn, knn, qrn, krn)]
    hq_spec = pl.BlockSpec((MH, tp, 256), lambda i: (0, i, 0))
    hv_spec = pl.BlockSpec((MH, tp, 128), lambda i: (0, i, 0))
    qf, kf, vh = stage("l1_mla_prep", fn_mla_prep, (L // tp,), prep_ins,
                       [(SDS((MH, L, 256), BF16), hq_spec), (SDS((MH, L, 256), BF16), hq_spec), (SDS((MH, L, 128), BF16), hv_spec)])
    attn = causal_attn(qf, kf, vh)
    mn1, kv1, kn1 = mem_branch_fwd("l1", mem, mem_norm[1:2], w1["w_mem_kv"], xk_norm[1:2])
    mo1 = mem_attn_fwd("l1", proj1, 5, kn1, kv1, xq_norm[1:2])
    o1 = row_fwd("l1_merge", fn_merge, L, 256, [(attn, PW, 0), (mo1, XQW, 0), (proj1, BW, 0)], [], [(BW, BF16)])[0]
    x2 = matmul("l1_out", o1, w1["w_out"], "nn", add=x1)
    dx2, loss = loss_and_grad(x2, target)

    do1 = matmul("l1_out_dx", dx2, w1["w_out"], "nt")
    g_pair1 = matmul("l1_out_dw", o1, dx2, "tn", out=gw["w_out"])
    dattn, dmo1, dproj1 = row_bwd("l1_merge_bwd", fn_merge, L, 256, [(attn, PW, 0), (mo1, XQW, 0), (proj1, BW, 0)], [],
                                  [(do1, BW, 0)], [True, True, {"cols": MLA_IN_P, "cb": 0, "dtype": BF16}], [])
    dproj1, dkn1, dv1, dxqn1 = mem_attn_bwd("l1", proj1, 5, kn1, kv1, xq_norm[1:2], dmo1, dproj1)
    env["g_pair1"], dmem_norm1, dxk1 = mem_branch_bwd("l1", mem, mem_norm[1:2], w1["w_mem_kv"], xk_norm[1:2], mn1, kv1, dkn1, dv1,
                                                      gw["w_mem_kv"], g_pair1)
    dqf, dkf, dvh = causal_attn_bwd(qf, kf, vh, dattn)
    prep_diffs = [("row", SDS((L, MH * (NOPE + ROPE)), BF16), rspec(tp, MH * (NOPE + ROPE))), ("row", SDS((L, MH * 256), BF16), rspec(tp, MH * 256)),
                  ("row", SDS((L, MLA_IN_P), BF16), rspec(tp, 128, 26), {"into": dproj1}), None, None] + [("acc", (0,))] * 4
    dq, dkv, dproj1, dqnn, dknn, dqrn, dkrn = stage_bwd("l1_mla_prep_bwd", fn_mla_prep, (L // tp,), prep_ins,
                                                        [(dqf, hq_spec), (dkf, hq_spec), (dvh, hv_spec)], prep_diffs)
    dcqn = matmul("l1_uq_dx", dq, w_uq, "nt")
    env["dw_uq"] = matmul("l1_uq_dw", cqn, dq, "tn")
    dckvn = matmul("l1_ukv_dx", dkv, w1["mla_w_ukv"], "nt")
    env["g_ukv"] = matmul("l1_ukv_dw", ckvn, dkv, "tn", out=gw["mla_w_ukv"])
    dproj1, dqln = row_bwd("l1_q_lora_rms_bwd", fn_rms, L, 256, [(proj1, QL, 4)], [qln], [(dcqn, QL, 0)],
                           [{"cols": MLA_IN_P, "cb": 4, "into": dproj1, "dtype": BF16}], [True])
    dproj1, dkvln = row_bwd("l1_kv_lora_rms_bwd", fn_rms, L, 256, [(proj1, KVL, 12)], [kvln], [(dckvn, KVL, 0)],
                            [{"cols": MLA_IN_P, "cb": 12, "into": dproj1, "dtype": BF16}], [True])
    dh1 = matmul("l1_in_dx", dproj1, w_in1, "nt")
    env["dw_in1"] = matmul("l1_in_dw", h1, dproj1, "tn")
    dx1, dln1 = row_bwd("l1_rms_bwd", fn_rms, L, 256, [(x1, D, 0)], [ln[1:2]], [(dh1, D, 0)], [{"add": (dx2, D, 0)}], [True],
                        plans=plans_for("l1_rms_bwd"))
    dx1p = time_permute(dx1)

    do0 = matmul("l0_out_dx", dx1p, w0["w_out"], "nt", plans=plans_for("l0_out_dx"))
    g_pair0 = matmul("l0_out_dw", o0, dx1p, "tn", out=gw["w_out"])
    dz0, dmo0, dproj0 = row_bwd("l0_merge_bwd", fn_merge_glu, L, 256, [(z0, 2 * PW, 0), (mo0, XQW, 0), (proj0, BW, 1)], [],
                                [(do0, BW, 0)], [{"dtype": BF16}, True, {"cols": 2 * BW, "cb": 1, "dtype": BF16}], [])
    dproj0, dkn0, dv0, dxqn0 = mem_attn_bwd("l0", proj0, 3, kn0, kv0, xq_norm[0:1], dmo0, dproj0)
    env["g_pair0"], dmem_norm0, dxk0 = mem_branch_bwd("l0", mem, mem_norm[0:1], w0["w_mem_kv"], xk_norm[0:1], mn0, kv0, dkn0, dv0,
                                                      gw["w_mem_kv"], g_pair0)
    env["g_glu"] = matmul("l0_glu_dw", g0, dz0, "tn", out=gw["s5_w_glu"], plans=plans_for("l0_glu_dw"))
    dg0 = matmul("l0_glu_dx", dz0, w0["s5_w_glu"], "nt", plans=plans_for("l0_glu_dx"))
    dproj0, dd, dwc_re, dwc_im, dwb_re, dwb_im, da_re, da_im = s5_backward(dg0, proj0, s_re, s_im, wb_re, wb_im, wc_re, wc_im,
                                                                           a_re_v, a_im_v, s5_d, dproj0, plans=plans_for("s5_backward"))
    env["g_in0"] = matmul("l0_in_dw", h0, dproj0, "tn", out=gw["s5_w_in"], plans=plans_for("l0_in_dw"))
    dh0 = matmul("l0_in_dx", dproj0, w0["s5_w_in"], "nt", plans=plans_for("l0_in_dx"))
    dxp, dln0 = row_bwd("l0_rms_bwd", fn_rms, L, 256, [(xp, D, 0)], [ln[0:1]], [(dh0, D, 0)], [{"add": (dx1p, D, 0)}], [True])
    grad_x = time_unpermute(dxp)

    db_re, db_im, dcoef_re, dcoef_im = row_bwd("s5_bmat_bwd", fn_s5_bmat, SN, 512, bmat_rows, [], [(dwb_re, 128, 0), (dwb_im, 128, 0)],
                                               [True] * 4, [], plans=plans_for("s5_bmat_bwd"))
    dc_re, dc_im = row_bwd("s5_cmat_bwd", fn_s5_cmat, PW, 128, cmat_rows, [], [(dwc_re, 512, 0), (dwc_im, 512, 0)], [True] * 2, [],
                           plans=plans_for("s5_cmat_bwd"))
    disc_cts = [(da_re.reshape(SG, SP), one), (da_im.reshape(SG, SP), one), (dcoef_re.reshape(SG, SP), one), (dcoef_im.reshape(SG, SP), one)]
    dlre, dlim, dls = stage_bwd("s5_disc_bwd", fn_s5_disc, (1,), disc_ins, disc_cts, [("acc", (0,))] * 3)

    g["ln_gain"] = jnp.concatenate([dln0, dln1], axis=0)
    g["mem_norm"] = jnp.concatenate([dmem_norm0, dmem_norm1], axis=0)
    g["xq_norm"] = jnp.concatenate([dxqn0, dxqn1], axis=0)
    g["xk_norm"] = jnp.concatenate([dxk0, dxk1], axis=0)
    g["s5_lambda_re"], g["s5_lambda_im"], g["s5_log_step"] = dlre, dlim, dls
    g["s5_b_re"], g["s5_b_im"], g["s5_c_re"], g["s5_c_im"] = db_re, db_im, dc_re, dc_im
    g["s5_d"] = dd
    g["mla_q_lora_norm"], g["mla_kv_lora_norm"] = dqln, dkvln
    g["mla_q_nope_norm"], g["mla_k_nope_norm"] = dqnn, dknn
    g["mla_q_rope_norm"] = dqrn[:, :ROPE] + dqrn[:, ROPE:]
    g["mla_k_rope_norm"] = dkrn[:, :ROPE] + dkrn[:, ROPE:]
    return loss, grad_x, g


def kernel(x, mem, positions, ln_gain, w_out, mem_norm, w_mem_kv, xq_norm, xk_norm, s5_w_in, s5_lambda_re, s5_lambda_im, s5_log_step, s5_b_re, s5_b_im, s5_c_re, s5_c_im, s5_d, s5_w_glu, mla_w_in, mla_q_lora_norm, mla_kv_lora_norm, mla_w_uq, mla_w_ukv, mla_q_nope_norm, mla_k_nope_norm, mla_q_rope_norm, mla_k_rope_norm, loss_target, m_ln_gain, m_w_out, m_mem_norm, m_w_mem_kv, m_xq_norm, m_xk_norm, m_s5_w_in, m_s5_lambda_re, m_s5_lambda_im, m_s5_log_step, m_s5_b_re, m_s5_b_im, m_s5_c_re, m_s5_c_im, m_s5_d, m_s5_w_glu, m_mla_w_in, m_mla_q_lora_norm, m_mla_kv_lora_norm, m_mla_w_uq, m_mla_w_ukv, m_mla_q_nope_norm, m_mla_k_nope_norm, m_mla_q_rope_norm, m_mla_k_rope_norm, v_ln_gain, v_w_out, v_mem_norm, v_w_mem_kv, v_xq_norm, v_xk_norm, v_s5_w_in, v_s5_lambda_re, v_s5_lambda_im, v_s5_log_step, v_s5_b_re, v_s5_b_im, v_s5_c_re, v_s5_c_im, v_s5_d, v_s5_w_glu, v_mla_w_in, v_mla_q_lora_norm, v_mla_kv_lora_norm, v_mla_w_uq, v_mla_w_ukv, v_mla_q_nope_norm, v_mla_k_nope_norm, v_mla_q_rope_norm, v_mla_k_rope_norm):
    args = dict(locals())
    wts = {n: args[n] for n in WEIGHT_ORDER}
    mom = {n: args["m_" + n] for n in WEIGHT_ORDER}
    var = {n: args["v_" + n] for n in WEIGHT_ORDER}

    chip = 2 * lax.axis_index("x") + lax.axis_index("y")
    place = jnp.stack([chip, lax.axis_index("c")]).astype(jnp.int32)

    def own_slot(gathered, shards):
        return [lax.dynamic_update_slice(g, s[None], (chip, 0, 0)) for g, s in zip(gathered, shards)]

    shards0, shards1 = stack_shards(wts, BF16)
    shards1.append(jnp.concatenate([mla_q_lora_norm, jnp.pad(mla_kv_lora_norm, ((0, 0), (0, 64))), jnp.zeros((14, 128), F32)], axis=0))
    env = dict(zip(("wide0", "glu"), own_slot(all_gather_chips("gather_weights0", shards0), shards0)))
    over_ici = plan_gather_ici(shards1)
    hooks = {("plans", "s5_forward"): lambda env: [over_ici]}
    passed_on = []

    def pass_on(env):
        passed_on.append(plan_gather_pass(over_ici.results))
        return passed_on

    def layer1_weights(env):
        wide1, ukv, in1, uq, norms = own_slot(passed_on[0].results, shards1)
        env.update(wide1=wide1, ukv=ukv, w_in1=mla_in_permute(shards_to_cols(in1)), w_uq=uq_permute(shards_to_cols(uq)),
                   q_lora_norm=norms[:, 0, :], kv_lora_norm=norms[:, 1, :64])

    hooks["plans", "l0_glu"], hooks["after", "l0_glu"] = pass_on, layer1_weights

    rs = {}

    def swap(k, gs):
        rs[k, "g"], rs[k, "swap"] = gs, plan_pair_exchange(gs)
        return rs[k, "swap"]

    def scatter(k, part=slice(None)):
        if (k, "pairs") not in rs:
            rs[k, "pairs"], rs[k, "scatter"] = pair_adds(f"rs{k}", rs[k, "g"], rs[k, "swap"].results, place), []
        rs[k, "scatter"].append(plan_chip_scatter(rs[k, "pairs"][part]))
        return rs[k, "scatter"][-1]

    def join(k):
        rs[k, "join"] = plan_pair_join(chip_adds(f"rs{k}", rs[k, "pairs"], [q for p in rs[k, "scatter"] for q in p.results], place))
        return rs[k, "join"]

    hooks["plans", "l1_rms_bwd"] = lambda env: [swap(1, [env["g_pair1"], env["g_ukv"], cols_to_shards(mla_in_unpermute(env["dw_in1"])).astype(BF16),
                                                          cols_to_shards(uq_unpermute(env["dw_uq"])).astype(BF16)])]
    hooks["plans", "l0_glu_dw"] = lambda env: [scatter(1, slice(0, 2))]
    hooks["plans", "l0_glu_dx"] = lambda env: [swap(0, [env["g_pair0"], env["g_glu"]])]
    hooks["plans", "s5_backward"] = lambda env: [scatter(1, slice(2, 4)), scatter(0)]
    hooks["plans", "l0_in_dw"] = lambda env: [join(1)]
    hooks["plans", "l0_in_dx"] = lambda env: [swap(2, [env["g_in0"]]), join(0)]
    hooks["plans", "s5_bmat_bwd"] = lambda env: [scatter(2)]
    hooks["plans", "s5_cmat_bwd"] = lambda env: [join(2)]

    small = {n: wts[n] for n, _ in SMALL}
    loss, grad_x, g = device_step(x[0], mem[0], positions[0], loss_target[0], small, env, hooks)
    loss = lax.psum(loss[0, 0], MESH_AXES)
    (r_pair1, r_ukv, r_in1, r_uq), (r_pair0, r_glu), (r_in0,) = (rs[k, "join"].results for k in (1, 0, 2))

    small_flat = jnp.concatenate([g[n].reshape(-1) for n, _ in SMALL_FULL])
    g_small = jnp.pad(small_flat, (0, 4 * SMALL_ROWS * SMALL_LANES - N_SMALL)).astype(BF16).reshape(4, SMALL_ROWS, SMALL_LANES)
    r_small = reduce_scatter_chips("rs3", [g_small], place)[0]
    small_all = own_slot(all_gather_chips("gather_small_grads", [r_small]), [r_small])[0].reshape(-1)[:N_SMALL]

    grads = {"w_out": jnp.stack([r_pair0[:PAIR_MKV], r_pair1[:PAIR_MKV]]), "w_mem_kv": jnp.stack([r_pair0[PAIR_MKV:], r_pair1[PAIR_MKV:]]),
             "s5_w_in": r_in0[None], "s5_w_glu": r_glu[None], "mla_w_ukv": r_ukv[None], "mla_w_in": r_in1[None], "mla_w_uq": r_uq[None]}
    off = 0
    for n, s in SMALL_FULL:
        grads[n] = small_all[off:off + math.prod(s)].reshape(s)
        off += math.prod(s)
    for n, s in SHARDED_SMALL:
        grads[n] = lax.dynamic_slice(grads[n], (0, chip * s[1]), s)

    delta, new_m, new_v = {}, {}, {}
    for n, s in BIG + [(n, s) for n, s in SMALL if len(s) == 4]:
        perm = MINOR_LAST.get(n, tuple(range(len(s))))
        turned = tuple(s[p] for p in perm)
        view = lambda a: jnp.transpose(a, perm).reshape(-1, turned[-1])
        res = adamw("adamw_" + n, view(wts[n]), view(grads[n]), view(mom[n]), view(var[n]))
        delta[n], new_m[n], new_v[n] = (jnp.transpose(r.reshape(turned), tuple(perm.index(i) for i in range(len(s)))) for r in res)
    small_names = [n for n, s in SMALL if len(s) < 4] + [n for n, _ in SHARDED_SMALL]
    n_own = sum(wts[n].size for n in small_names)
    rows_own = -(-n_own // (8 * 128)) * 8

    def pack_small(d):
        flat = jnp.concatenate([d[n].reshape(-1) for n in small_names])
        return jnp.pad(flat, (0, rows_own * 128 - n_own), constant_values=1.0).reshape(rows_own, 128)

    res = adamw("adamw_small", pack_small(wts), pack_small(grads), pack_small(mom), pack_small(var))
    off = 0
    for n in small_names:
        size = wts[n].size
        delta[n], new_m[n], new_v[n] = (r.reshape(-1)[off:off + size].reshape(wts[n].shape) for r in res)
        off += size

    return (loss, grad_x[None], *[grads[n] for n in WEIGHT_ORDER], *[delta[n] for n in WEIGHT_ORDER],
            *[new_m[n] for n in WEIGHT_ORDER], *[new_v[n] for n in WEIGHT_ORDER])
```

```python
import functools
import math

import jax
import jax.numpy as jnp
from jax import lax
from jax.experimental import pallas as pl
from jax.experimental.pallas import tpu as pltpu

F32, BF16 = jnp.float32, jnp.bfloat16
SDS = jax.ShapeDtypeStruct

D = 1024
L = 2048
ML = 256
BW = 2 * D
XQW = BW // 4
PW = BW - XQW
XH, XHD = 4, 128
SG, SC, SP = 96, 16, 64
SN = SG * SP
NOPE, ROPE, VD = 128, 64, 128
MH = 12
QL, KVL = 512, 256
EPS = 1e-6
ROPE_THETA = 10000.0
MLA_IN = QL + KVL + ROPE + XQW + BW
MLA_IN_P = 3456
ADAM_LR, ADAM_B1, ADAM_B2, ADAM_EPS, ADAM_WD, ADAM_STEP = 0.001, 0.9, 0.999, 1e-08, 0.01, 10

VMEM_LIMIT = 48 * 2**20
SEG = 8
SEG_LEN = L // SEG
MESH_AXES = ("x", "y", "c")


def _cparams():
    return pltpu.CompilerParams(vmem_limit_bytes=VMEM_LIMIT)


def _dg(a, b, ca, cb):
    return lax.dot_general(a.astype(BF16), b.astype(BF16), (((ca,), (cb,)), ((), ())), preferred_element_type=F32)


@jax.custom_vjp
def mm_nn(a, b):
    return _dg(a, b, 1, 0)


mm_nn.defvjp(lambda a, b: (_dg(a, b, 1, 0), (a, b)), lambda res, g: (_dg(g, res[1], 1, 1), _dg(res[0], g, 0, 0)))


@jax.custom_vjp
def mm_nt(a, b):
    return _dg(a, b, 1, 1)


mm_nt.defvjp(lambda a, b: (_dg(a, b, 1, 1), (a, b)), lambda res, g: (_dg(g, res[1], 1, 0), _dg(g, res[0], 0, 0)))


@functools.partial(jax.custom_vjp, nondiff_argnums=(1,))
def lane_roll(x, shift):
    return pltpu.roll(x, shift, 1)


lane_roll.defvjp(lambda x, shift: (pltpu.roll(x, shift, 1), None),
                 lambda shift, _, g: (pltpu.roll(g, (128 - shift) % 128, 1),))


def rms(x, g):
    return x * lax.rsqrt(jnp.mean(x * x, axis=-1, keepdims=True) + EPS) * g


@jax.custom_vjp
def softmax_rows(s):
    e = jnp.exp(s - jnp.max(s, axis=-1, keepdims=True))
    return e / jnp.sum(e, axis=-1, keepdims=True)


def _softmax_rows_fwd(s):
    p = softmax_rows(s)
    return p, p


def _softmax_rows_bwd(p, g):
    return (p * (g - jnp.sum(g * p, axis=-1, keepdims=True)),)


softmax_rows.defvjp(_softmax_rows_fwd, _softmax_rows_bwd)


def silu(x):
    return x * jax.nn.sigmoid(x)


ANY = pl.BlockSpec(memory_space=pl.ANY)
MESH_ID = pl.DeviceIdType.MESH


def _dma_sems(n):
    return [pltpu.SemaphoreType.DMA((n,)), pltpu.SemaphoreType.DMA((n,))]


class Plan:
    def __init__(self, operands, out_shape, aliases, n_sems, copies):
        self.operands, self.out_shape, self.aliases, self.n_sems, self.copies = list(operands), list(out_shape), aliases, n_sems, copies
        self.results = None


def hosted_call(kern, *, name, grid, in_specs, out_specs, out_shape, operands, scratch_shapes=(), aliases=None, cparams=None, plans=()):
    n_in, n_out, n_scr = len(in_specs), len(out_specs), len(scratch_shapes)
    p_in, p_out = [len(p.operands) for p in plans], [len(p.out_shape) for p in plans]
    all_aliases = dict(aliases or {})
    in_off, out_off = n_in, n_out
    for p, ni, no in zip(plans, p_in, p_out):
        all_aliases.update({in_off + i: out_off + o for i, o in p.aliases.items()})
        in_off, out_off = in_off + ni, out_off + no

    def body(*refs):
        pos, pins, pouts = n_in, [], []
        for ni in p_in:
            pins.append(refs[pos:pos + ni])
            pos += ni
        main_out = refs[pos:pos + n_out]
        pos += n_out
        for no in p_out:
            pouts.append(refs[pos:pos + no])
            pos += no
        main_scr = refs[pos:pos + n_scr]
        pos += n_scr
        if plans:
            ids = [pl.program_id(ax) for ax in range(len(grid))]
            first = functools.reduce(jnp.logical_and, [i == 0 for i in ids])
            last = functools.reduce(jnp.logical_and, [i == g - 1 for i, g in zip(ids, grid)])
            copies = [p.copies(pins[k], pouts[k], refs[pos + 2 * k], refs[pos + 2 * k + 1]) for k, p in enumerate(plans)]

            @pl.when(first)
            def _():
                for sends, _ in copies:
                    for cp in sends:
                        cp.start()

        kern(*refs[:n_in], *main_out, *main_scr)
        if plans:
            @pl.when(last)
            def _():
                for sends, recvs in copies:
                    for cp in recvs:
                        cp.wait_recv()
                    for cp in sends:
                        cp.wait_send()

    res = pl.pallas_call(body, grid=grid, in_specs=list(in_specs) + [ANY] * sum(p_in), out_specs=list(out_specs) + [ANY] * sum(p_out),
                         out_shape=list(out_shape) + [s for p in plans for s in p.out_shape],
                         scratch_shapes=list(scratch_shapes) + [s for p in plans for s in _dma_sems(p.n_sems)],
                         input_output_aliases=all_aliases, name=name, compiler_params=cparams or _cparams())(
        *operands, *[a for p in plans for a in p.operands])
    pos = n_out
    for p, no in zip(plans, p_out):
        p.results = list(res[pos:pos + no])
        pos += no
    return list(res[:n_out])


def _wide(v):
    return v.astype(F32) if v.dtype == BF16 else v


def stage(name, fn, grid, ins, outs):
    n_in = len(ins)

    def kern(*refs):
        res = fn(*[_wide(r[...]) for r in refs[:n_in]])
        for r, v in zip(refs[n_in:], res):
            r[...] = v.astype(r.dtype)

    return pl.pallas_call(kern, grid=grid, in_specs=[s for _, s in ins], out_specs=[s for _, s in outs],
                          out_shape=[sd for sd, _ in outs], name=name, compiler_params=_cparams())(*[a for a, _ in ins])


def stage_bwd(name, fn, grid, ins, cts, diffs, plans=()):
    n_in, n_ct = len(ins), len(cts)
    didx = [i for i, d in enumerate(diffs) if d is not None]
    opts = {i: (diffs[i][3] if len(diffs[i]) > 3 else {}) for i in didx if diffs[i][0] == "row"}
    adds = [(i, opts[i]["add"]) for i in opts if "add" in opts[i]]
    intos = [(i, opts[i]["into"]) for i in opts if "into" in opts[i]]
    n_add, n_into = len(adds), len(intos)
    add_pos = {i: n_in + n_ct + k for k, (i, _) in enumerate(adds)}
    n_extra = n_in + n_ct + n_add + n_into

    def kern(*refs):
        vals = [_wide(r[...]) for r in refs[:n_in]]

        def f(*dv):
            full = list(vals)
            for i, v in zip(didx, dv):
                full[i] = v
            return fn(*full)

        _, vjp = jax.vjp(f, *[vals[i].astype(F32) for i in didx])
        gs = vjp(tuple(c[...].astype(F32) for c in refs[n_in:n_in + n_ct]))
        for o_ref, i, g in zip(refs[n_extra:], didx, gs):
            if diffs[i][0] == "row":
                if i in add_pos:
                    g = g + refs[add_pos[i]][...].astype(F32)
                o_ref[...] = g.astype(o_ref.dtype)
            else:
                first = functools.reduce(jnp.logical_and, [pl.program_id(ax) == 0 for ax in diffs[i][1]])

                @pl.when(first)
                def _():
                    o_ref[...] = g

                @pl.when(jnp.logical_not(first))
                def _():
                    o_ref[...] += g

    out_shape, out_specs = [], []
    for i in didx:
        if diffs[i][0] == "row":
            out_shape.append(diffs[i][1])
            out_specs.append(diffs[i][2])
        else:
            out_shape.append(SDS(ins[i][0].shape, F32))
            out_specs.append(ins[i][1])
    aliases = {n_in + n_ct + n_add + k: didx.index(i) for k, (i, _) in enumerate(intos)}
    in_specs = [s for _, s in ins] + [s for _, s in cts] + [s for _, (_, s) in adds] + [ANY] * n_into
    operands = [a for a, _ in ins] + [a for a, _ in cts] + [a for _, (a, _) in adds] + [a for _, a in intos]
    return hosted_call(kern, name=name, grid=grid, in_specs=in_specs, out_specs=out_specs, out_shape=out_shape, operands=operands,
                       aliases=aliases, plans=plans)


def rspec(tl, w, cb=0):
    return pl.BlockSpec((tl, w), lambda i: (i, cb))


def cspec(shape):
    return pl.BlockSpec(shape, lambda i: (0,) * len(shape))


def row_fwd(name, fn, rows, tl, row_ins, consts, outs):
    ins = [(a, rspec(tl, w, cb)) for a, w, cb in row_ins] + [(a, cspec(a.shape)) for a in consts]
    return stage(name, fn, (rows // tl,), ins, [(SDS((rows, w), dt), rspec(tl, w)) for w, dt in outs])


def row_bwd(name, fn, rows, tl, row_ins, consts, cts, row_diff, const_diff, plans=()):
    ins = [(a, rspec(tl, w, cb)) for a, w, cb in row_ins] + [(a, cspec(a.shape)) for a in consts]
    diffs = []
    for (a, w, cb), d in zip(row_ins, row_diff):
        if not d:
            diffs.append(None)
            continue
        d = d if isinstance(d, dict) else {}
        opts = {}
        if "add" in d:
            opts["add"] = (d["add"][0], rspec(tl, d["add"][1], d["add"][2]))
        if d.get("into") is not None:
            opts["into"] = d["into"]
        diffs.append(("row", SDS((rows, d.get("cols", w)), d.get("dtype", F32)), rspec(tl, w, d.get("cb", 0)), opts))
    diffs += [("acc", (0,)) if d else None for d in const_diff]
    return stage_bwd(name, fn, (rows // tl,), ins, [(a, rspec(tl, w, cb)) for a, w, cb in cts], diffs, plans=plans)


MATMUL_VMEM = 36 * 2**20
ADAMW_VMEM = 28 * 2**20


class Sharded:
    def __init__(self, arr, kind, roff, rows):
        self.arr, self.kind, self.roff, self.rows, self.n = arr, kind, roff, rows, arr.shape[2]
        self.shape = (rows, 4 * self.n) if kind == "col" else (4 * rows, self.n)

    def fits(self, t0, t1):
        return self.roff % t0 == 0 and self.rows % t0 == 0 and self.n % t1 == 0

    def spec(self, t0, t1, bidx):
        assert self.fits(t0, t1), (self.kind, self.roff, self.rows, self.n, t0, t1)
        r0 = self.roff // t0
        if self.kind == "col":
            per = self.n // t1
            return pl.BlockSpec((None, t0, t1), lambda *g: (bidx(*g)[1] // per, r0 + bidx(*g)[0], bidx(*g)[1] % per))
        per = self.rows // t0
        return pl.BlockSpec((None, t0, t1), lambda *g: (bidx(*g)[0] // per, r0 + bidx(*g)[0] % per, bidx(*g)[1]))


def matmul(name, a, b, mode, out_dtype=BF16, add=None, out=None, into=None, plans=()):
    if mode == "tn":
        k_dim, m = a.shape
    else:
        m, k_dim = a.shape
    n = b.shape[0] if mode == "nt" else b.shape[1]
    b_fit = b.fits if isinstance(b, Sharded) else (lambda t0, t1: True)
    o_fit = out.fits if out is not None else (lambda t0, t1: True)
    a_bytes, b_bytes = jnp.dtype(a.dtype).itemsize, jnp.dtype(b.arr.dtype if isinstance(b, Sharded) else b.dtype).itemsize
    o_bytes = jnp.dtype(out_dtype if out is None else out.arr.dtype).itemsize

    def vmem(tm, tn, tk):
        return 2 * (tm * tk * a_bytes + tk * tn * b_bytes + tm * tn * (o_bytes + (4 if add is not None else 0))) + 4 * tm * tn * (1 + (tk < k_dim))

    tiles = [(tm, tn, tk) for tm in (2048, 1024, 512, 256, 128) for tn in (1024, 768, 512, 384, 256, 128)
             for tk in sorted({k_dim, 1024, 768, 512, 384, 256, 128})
             if m % tm == 0 and n % tn == 0 and k_dim % tk == 0 and (b_fit(tn, tk) if mode == "nt" else b_fit(tk, tn)) and o_fit(tm, tn)
             and vmem(tm, tn, tk) <= MATMUL_VMEM]
    tm, tn, tk = max(tiles, key=lambda t: (t[2] == k_dim, t[0] * t[1] * t[2], t[0] * t[1]))
    nk = k_dim // tk
    a_spec = pl.BlockSpec((tk, tm), lambda i, j, k: (k, i)) if mode == "tn" else pl.BlockSpec((tm, tk), lambda i, j, k: (i, k))
    if isinstance(b, Sharded):
        b_spec = b.spec(tn, tk, lambda i, j, k: (j, k)) if mode == "nt" else b.spec(tk, tn, lambda i, j, k: (k, j))
        b = b.arr
    else:
        b_spec = pl.BlockSpec((tn, tk), lambda i, j, k: (j, k)) if mode == "nt" else pl.BlockSpec((tk, tn), lambda i, j, k: (k, j))
    o_spec = pl.BlockSpec((tm, tn), lambda i, j, k: (i, j))
    out_spec, out_shape = (o_spec, SDS((m, n), out_dtype)) if out is None else (out.spec(tm, tn, lambda i, j, k: (i, j)), out.arr)
    ca, cb = {"nn": (1, 0), "nt": (1, 1), "tn": (0, 0)}[mode]
    n_in = 2 + (add is not None)

    def finish(refs, o_ref, r):
        if add is not None:
            r = r + refs[2][...]
        o_ref[...] = r.astype(o_ref.dtype)

    def kern_whole(*refs):
        finish(refs, refs[-1], _dg(refs[0][...], refs[1][...], ca, cb))

    def kern_cut(*refs):
        o_ref, acc = refs[-2], refs[-1]
        k = pl.program_id(2)

        @pl.when(k == 0)
        def _():
            acc[...] = jnp.zeros_like(acc)

        acc[...] += _dg(refs[0][...], refs[1][...], ca, cb)

        @pl.when(k == nk - 1)
        def _():
            finish(refs, o_ref, acc[...])

    ins, specs = [a, b], [a_spec, b_spec]
    if add is not None:
        ins.append(add)
        specs.append(o_spec)
    if into is not None:
        ins.append(into)
        specs.append(ANY)
    return hosted_call(kern_whole if nk == 1 else kern_cut, name=name, grid=(m // tm, n // tn, nk), in_specs=specs, out_specs=[out_spec],
                       out_shape=[out_shape], operands=ins, scratch_shapes=[] if nk == 1 else [pltpu.VMEM((tm, tn), F32)],
                       aliases={} if into is None else {n_in: 0}, plans=plans)[0]


def _cmul(ar, ai, br, bi):
    return ar * br - ai * bi, ar * bi + ai * br


def _sub_shift(x, down):
    row = lax.broadcasted_iota(jnp.int32, x.shape, 0)
    if down:
        return jnp.where(row == 0, 0.0, pltpu.roll(x, 1, 0))
    return jnp.where(row == SEG - 1, 0.0, pltpu.roll(x, SEG - 1, 0))


def _pow_seg_len(ar, ai):
    for _ in range(int(math.log2(SEG_LEN))):
        ar, ai = _cmul(ar, ai, ar, ai)
    return ar, ai


def _scan_in_place(sr, si, a_re, a_im):
    lanes = sr.shape[1]
    ar = jnp.broadcast_to(a_re, (SEG, lanes))
    ai = jnp.broadcast_to(a_im, (SEG, lanes))
    zero = jnp.zeros((SEG, lanes), F32)

    def local(i, carry):
        rows = pl.ds(pl.multiple_of(i * SEG, SEG), SEG)
        mr, mi = _cmul(ar, ai, carry[0], carry[1])
        nr, ni = mr + sr[rows, :], mi + si[rows, :]
        sr[rows, :] = nr
        si[rows, :] = ni
        return nr, ni

    fr, fi = lax.fori_loop(0, SEG_LEN, local, (zero, zero))
    pr, pi = _pow_seg_len(ar, ai)
    ir, ii = zero, zero
    for _ in range(SEG - 1):
        mr, mi = _cmul(pr, pi, ir, ii)
        ir, ii = _sub_shift(mr + fr, True), _sub_shift(mi + fi, True)

    def carry_in(i, pw):
        rows = pl.ds(pl.multiple_of(i * SEG, SEG), SEG)
        cr, ci = _cmul(pw[0], pw[1], ir, ii)
        sr[rows, :] += cr
        si[rows, :] += ci
        return _cmul(pw[0], pw[1], ar, ai)

    lax.fori_loop(0, SEG_LEN, carry_in, (ar, ai))


S5_LANES = 8 * SP
S5_BLOCKS = SN // S5_LANES


def _s5_specs():
    u_spec = pl.BlockSpec((L, 8 * SC), lambda j: (0, j))
    s_spec = pl.BlockSpec((L, S5_LANES), lambda j: (0, j))
    wb_spec = pl.BlockSpec((S5_LANES, 8 * SC), lambda j: (j, 0))
    wc_spec = pl.BlockSpec((8 * SC, S5_LANES), lambda j: (j, 0))
    a_spec = pl.BlockSpec((1, S5_LANES), lambda j: (0, j))
    d_spec = pl.BlockSpec((1, 8 * SC), lambda j: (0, j))
    return u_spec, s_spec, wb_spec, wc_spec, a_spec, d_spec


def s5_forward(proj, wb_re, wb_im, wc_re, wc_im, a_re, a_im, d, plans=()):
    def kern(u_ref, wbr, wbi, wcr, wci, ar, ai, d_ref, sr_out, si_out, g_ref, sr, si):
        u = _wide(u_ref[...])
        sr[...], si[...] = fn_s5_bu(u, wbr[...], wbi[...])
        _scan_in_place(sr, si, ar[...], ai[...])
        g_ref[...] = fn_s5_out(sr[...], si[...], u, d_ref[...], wcr[...], wci[...])[0].astype(g_ref.dtype)
        sr_out[...] = sr[...].astype(sr_out.dtype)
        si_out[...] = si[...].astype(si_out.dtype)

    u_spec, s_spec, wb_spec, wc_spec, a_spec, d_spec = _s5_specs()
    return hosted_call(kern, name="s5_forward", grid=(S5_BLOCKS,), in_specs=[u_spec, wb_spec, wb_spec, wc_spec, wc_spec, a_spec, a_spec, d_spec],
                       out_specs=[s_spec, s_spec, u_spec], out_shape=[SDS((L, SN), BF16)] * 2 + [SDS((L, PW), BF16)],
                       operands=[proj, wb_re, wb_im, wc_re, wc_im, a_re, a_im, d], scratch_shapes=[pltpu.VMEM((L, S5_LANES), F32)] * 2,
                       plans=plans)


def _adjoint_scan_in_place(lr, li, sr, si, a_re, a_im):
    lanes = lr.shape[1]
    ar = jnp.broadcast_to(a_re, (SEG, lanes))
    ai = -jnp.broadcast_to(a_im, (SEG, lanes))
    zero = jnp.zeros((SEG, lanes), F32)

    def local(k, carry):
        i = SEG_LEN - 1 - k
        rows = pl.ds(pl.multiple_of(i * SEG, SEG), SEG)
        mr, mi = _cmul(ar, ai, carry[0], carry[1])
        nr, ni = mr + lr[rows, :], mi + li[rows, :]
        lr[rows, :] = nr
        li[rows, :] = ni
        return nr, ni

    fr, fi = lax.fori_loop(0, SEG_LEN, local, (zero, zero))
    pr, pi = _pow_seg_len(ar, ai)
    ir, ii = zero, zero
    for _ in range(SEG - 1):
        mr, mi = _cmul(pr, pi, ir, ii)
        ir, ii = _sub_shift(mr + fr, False), _sub_shift(mi + fi, False)

    def fix(rows, pw):
        cr, ci = _cmul(pw[0], pw[1], ir, ii)
        tr, ti = lr[rows, :] + cr, li[rows, :] + ci
        lr[rows, :] = tr
        li[rows, :] = ti
        return tr, ti

    def grad_a(tr, ti, spr, spi, acc):
        return acc[0] + tr * spr + ti * spi, acc[1] + ti * spr - tr * spi

    def carry_in(k, c):
        i = SEG_LEN - 1 - k
        rows = pl.ds(pl.multiple_of(i * SEG, SEG), SEG)
        prev = pl.ds(pl.multiple_of((i - 1) * SEG, SEG), SEG)
        tr, ti = fix(rows, (c[0], c[1]))
        acc = grad_a(tr, ti, sr[prev, :], si[prev, :], (c[2], c[3]))
        nr, ni = _cmul(c[0], c[1], ar, ai)
        return nr, ni, acc[0], acc[1]

    pwr, pwi, accr, acci = lax.fori_loop(0, SEG_LEN - 1, carry_in, (ar, ai, zero, zero))
    tr, ti = fix(pl.ds(0, SEG), (pwr, pwi))
    last = pl.ds((SEG_LEN - 1) * SEG, SEG)
    accr, acci = grad_a(tr, ti, _sub_shift(sr[last, :], True), _sub_shift(si[last, :], True), (accr, acci))
    return jnp.sum(accr, axis=0, keepdims=True), jnp.sum(acci, axis=0, keepdims=True)


S5_BWD_VMEM = 58 * 2**20


def s5_backward(dg, proj, s_re, s_im, wb_re, wb_im, wc_re, wc_im, a_re, a_im, d, dproj, plans=()):
    def kern(dg_ref, u_ref, sr_in, si_in, wbr, wbi, wcr, wci, ar, ai, d_ref, _, du_ref, dd_ref, dwcr, dwci, dwbr, dwbi, dar, dai, lr, li, sr, si):
        u = _wide(u_ref[...])
        sr[...], si[...] = _wide(sr_in[...]), _wide(si_in[...])
        _, vjp_out = jax.vjp(fn_s5_out, sr[...], si[...], u, d_ref[...], wcr[...], wci[...])
        lr[...], li[...], du_out, dd_ref[...], dwcr[...], dwci[...] = vjp_out((_wide(dg_ref[...]),))
        dar[...], dai[...] = _adjoint_scan_in_place(lr, li, sr, si, ar[...], ai[...])
        _, vjp_in = jax.vjp(fn_s5_bu, u, wbr[...], wbi[...])
        du_in, dwbr[...], dwbi[...] = vjp_in((lr[...], li[...]))
        du_ref[...] = (du_out + du_in).astype(du_ref.dtype)

    u_spec, s_spec, wb_spec, wc_spec, a_spec, d_spec = _s5_specs()
    outs = [(SDS(dproj.shape, dproj.dtype), u_spec), (SDS(d.shape, F32), d_spec), (SDS(wc_re.shape, F32), wc_spec), (SDS(wc_im.shape, F32), wc_spec),
            (SDS(wb_re.shape, F32), wb_spec), (SDS(wb_im.shape, F32), wb_spec), (SDS(a_re.shape, F32), a_spec), (SDS(a_im.shape, F32), a_spec)]
    return hosted_call(kern, name="s5_backward", grid=(S5_BLOCKS,),
                       in_specs=[u_spec, u_spec, s_spec, s_spec, wb_spec, wb_spec, wc_spec, wc_spec, a_spec, a_spec, d_spec, ANY],
                       out_specs=[sp for _, sp in outs], out_shape=[sd for sd, _ in outs], aliases={11: 0},
                       operands=[dg, proj, s_re, s_im, wb_re, wb_im, wc_re, wc_im, a_re, a_im, d, dproj],
                       scratch_shapes=[pltpu.VMEM((L, S5_LANES), F32)] * 4,
                       cparams=pltpu.CompilerParams(vmem_limit_bytes=S5_BWD_VMEM), plans=plans)


def fn_rms(x, g):
    return (rms(x, g),)


def fn_s5_disc(lre, lim, ls):
    step = jnp.exp(ls)
    e = jnp.exp(lre * step)
    a_re, a_im = e * jnp.cos(lim * step), e * jnp.sin(lim * step)
    den = lre * lre + lim * lim
    nr, ni = a_re - 1.0, a_im
    return a_re, a_im, (nr * lre + ni * lim) / den, (ni * lre - nr * lim) / den


def _group_mask(rows, cols, row_div, col_div):
    r = lax.broadcasted_iota(jnp.int32, (rows, cols), 0) // row_div % 8
    c = lax.broadcasted_iota(jnp.int32, (rows, cols), 1) // col_div
    return r == c


def _spread(x, mask):
    w = x.shape[1]
    copy = (lax.broadcasted_iota(jnp.int32, (w, 8 * w), 1) % w == lax.broadcasted_iota(jnp.int32, (w, 8 * w), 0)).astype(F32)
    return jnp.where(mask, jnp.dot(x, copy, precision=lax.Precision.HIGHEST, preferred_element_type=F32), 0.0)


def fn_s5_bmat(b_re, b_im, coef_re, coef_im):
    mask = _group_mask(b_re.shape[0], 8 * SC, SP, SC)
    return _spread(coef_re * b_re - coef_im * b_im, mask), _spread(coef_re * b_im + coef_im * b_re, mask)


def fn_s5_cmat(c_re, c_im):
    mask = _group_mask(c_re.shape[0], 8 * SP, SC, SP)
    return _spread(c_re, mask), _spread(c_im, mask)


def fn_s5_bu(u, wb_re, wb_im):
    return mm_nt(u, wb_re), mm_nt(u, wb_im)


def fn_s5_out(sr, si, u, d, wc_re, wc_im):
    y = mm_nt(sr, wc_re) - mm_nt(si, wc_im) + d * u
    return (jax.nn.gelu(y),)


def fn_merge_glu(z, mo, gate):
    yg = z[:, :PW] * jax.nn.sigmoid(z[:, PW:])
    return (jnp.concatenate([yg, mo], axis=1) * silu(gate),)


def fn_merge(prim, mo, gate):
    return (jnp.concatenate([prim, mo], axis=1) * silu(gate),)


def fn_mem_k(kv, g):
    return (jnp.concatenate([rms(kv[:, h * XHD:(h + 1) * XHD], g) for h in range(XH)], axis=1),)


def fn_mem_attn(xq, kn, v, g):
    outs = []
    for h in range(XH):
        sl = slice(h * XHD, (h + 1) * XHD)
        p = softmax_rows(mm_nt(rms(xq[:, sl], g), kn[:, sl]) * (XHD ** -0.5))
        outs.append(mm_nn(p, v[:, sl]))
    return (jnp.concatenate(outs, axis=1),)


def _half_rms(x, g):
    lo = lax.broadcasted_iota(jnp.int32, x.shape, 1) < ROPE
    x2 = x * x
    s_lo = jnp.sum(jnp.where(lo, x2, 0.0), axis=1, keepdims=True)
    s_hi = jnp.sum(jnp.where(lo, 0.0, x2), axis=1, keepdims=True)
    return x * lax.rsqrt(jnp.where(lo, s_lo, s_hi) / ROPE + EPS) * g


def _rope(x, cos2, sin_signed):
    first = lax.broadcasted_iota(jnp.int32, x.shape, 1) % ROPE < ROPE // 2
    return x * cos2 + jnp.where(first, lane_roll(x, 128 - ROPE // 2), lane_roll(x, ROPE // 2)) * sin_signed


def fn_mla_prep(q, kv, kr, cos2, sin_signed, qnn, knn, qrn, krn):
    lo = lax.broadcasted_iota(jnp.int32, kr.shape, 1) < ROPE
    kr_pad = jnp.where(lo, _rope(_half_rms(kr, krn), cos2, sin_signed), 0.0)
    qf, kf, vs = [], [], []
    for m in range(MH // 2):
        pair = _rope(_half_rms(q[:, MH * NOPE + 128 * m:MH * NOPE + 128 * (m + 1)], qrn), cos2, sin_signed)
        for h, rope_h in ((2 * m, pair), (2 * m + 1, lane_roll(pair, ROPE))):
            qf.append(jnp.concatenate([rms(q[:, NOPE * h:NOPE * (h + 1)], qnn), jnp.where(lo, rope_h, 0.0)], axis=1))
    for h in range(MH):
        kf.append(jnp.concatenate([rms(kv[:, 256 * h:256 * h + NOPE], knn), kr_pad], axis=1))
        vs.append(kv[:, 256 * h + NOPE:256 * (h + 1)])
    return jnp.stack(qf), jnp.stack(kf), jnp.stack(vs)


ATT_TQ = 512


def _attn_tile(q, kf, v):
    tq = q.shape[0]
    scale = (NOPE + ROPE) ** -0.5
    own = mm_nt(q, kf[-tq:]) * scale
    own = jnp.where(lax.broadcasted_iota(jnp.int32, own.shape, 1) <= lax.broadcasted_iota(jnp.int32, own.shape, 0), own, jnp.finfo(F32).min)
    s = own if kf.shape[0] == tq else jnp.concatenate([mm_nt(q, kf[:-tq]) * scale, own], axis=1)
    return mm_nn(softmax_rows(s), v)


def _attn_specs():
    q_spec = pl.BlockSpec((None, ATT_TQ, 256), lambda h, i: (h, i, 0))
    k_spec = pl.BlockSpec((None, L, 256), lambda h, i: (h, 0, 0))
    v_spec = pl.BlockSpec((None, L, 128), lambda h, i: (h, 0, 0))
    o_spec = pl.BlockSpec((ATT_TQ, 128), lambda h, i: (i, h))
    return q_spec, k_spec, v_spec, o_spec


def causal_attn(qf, kf, vh):
    n_tiles = L // ATT_TQ

    def kern(q_ref, k_ref, v_ref, o_ref):
        i = pl.program_id(1)
        for t in range(n_tiles):
            @pl.when(i == t)
            def _(t=t):
                keys = (t + 1) * ATT_TQ
                o_ref[...] = _attn_tile(q_ref[...], k_ref[:keys, :], v_ref[:keys, :]).astype(o_ref.dtype)

    q_spec, k_spec, v_spec, o_spec = _attn_specs()
    return pl.pallas_call(kern, grid=(MH, n_tiles), in_specs=[q_spec, k_spec, v_spec], out_specs=o_spec,
                          out_shape=SDS((L, MH * VD), BF16), name="l1_attn", compiler_params=_cparams())(qf, kf, vh)


def causal_attn_bwd(qf, kf, vh, dout):
    n_tiles = L // ATT_TQ

    def kern(q_ref, k_ref, v_ref, do_ref, dq_ref, dk_ref, dv_ref):
        i = pl.program_id(1)

        @pl.when(i == 0)
        def _():
            dk_ref[...] = jnp.zeros_like(dk_ref)
            dv_ref[...] = jnp.zeros_like(dv_ref)

        for t in range(n_tiles):
            @pl.when(i == t)
            def _(t=t):
                keys = (t + 1) * ATT_TQ
                _, vjp = jax.vjp(_attn_tile, q_ref[...].astype(F32), k_ref[:keys, :].astype(F32), v_ref[:keys, :].astype(F32))
                dq, dk, dv = vjp(do_ref[...])
                dq_ref[...] = dq
                dk_ref[:keys, :] += dk
                dv_ref[:keys, :] += dv

    q_spec, k_spec, v_spec, o_spec = _attn_specs()
    return pl.pallas_call(kern, grid=(MH, n_tiles), in_specs=[q_spec, k_spec, v_spec, o_spec], out_specs=[q_spec, k_spec, v_spec],
                          out_shape=[SDS(qf.shape, F32), SDS(kf.shape, F32), SDS(vh.shape, F32)], name="l1_attn_bwd",
                          compiler_params=_cparams())(qf, kf, vh, dout)


def loss_and_grad(y, target, tl=256):
    def kern(y_ref, t_ref, dy_ref, loss_ref):
        d = y_ref[...] - t_ref[...]
        dy_ref[...] = d / D

        @pl.when(pl.program_id(0) == 0)
        def _():
            loss_ref[...] = jnp.zeros_like(loss_ref)

        loss_ref[...] += 0.5 * jnp.sum(jnp.sum(d * d, axis=1, keepdims=True), axis=0, keepdims=True) / D

    return pl.pallas_call(kern, grid=(L // tl,), in_specs=[rspec(tl, D), rspec(tl, D)], out_specs=[rspec(tl, D), cspec((1, 1))],
                          out_shape=[SDS((L, D), F32), SDS((1, 1), F32)], name="loss", compiler_params=_cparams())(y, target)


def adamw(name, w, g, m, v):
    rows, cols = w.shape
    block_row_bytes = 7 * 2 * 4 * max(cols, 128)
    tr = _row_tile(rows, min(2048, ADAMW_VMEM // block_row_bytes // 8 * 8), 8)

    def kern(w_ref, g_ref, m_ref, v_ref, d_ref, nm_ref, nv_ref):
        gg = g_ref[...]
        nm = ADAM_B1 * m_ref[...] + (1.0 - ADAM_B1) * gg
        nv = ADAM_B2 * v_ref[...] + (1.0 - ADAM_B2) * jnp.square(gg)
        m_hat = nm / (1.0 - ADAM_B1 ** ADAM_STEP)
        v_hat = nv / (1.0 - ADAM_B2 ** ADAM_STEP)
        d_ref[...] = -ADAM_LR * (m_hat / (jnp.sqrt(v_hat) + ADAM_EPS) + ADAM_WD * w_ref[...])
        nm_ref[...] = nm
        nv_ref[...] = nv

    spec = rspec(tr, cols)
    return pl.pallas_call(kern, grid=(rows // tr,), in_specs=[spec] * 4, out_specs=[spec] * 3,
                          out_shape=[SDS((rows, cols), F32)] * 3, name=name, compiler_params=_cparams())(w, g, m, v)


def _row_tile(rows, cap=512, unit=16):
    return max(t for t in range(unit, cap + 1, unit) if rows % t == 0)


def _place():
    x, y, c = lax.axis_index("x"), lax.axis_index("y"), lax.axis_index("c")
    return x, y, c, [(1 - x, y), (x, 1 - y), (1 - x, 1 - y)]


def _row_chunks(rows, n, dtype):
    unit = 32 // jnp.dtype(dtype).itemsize
    base, extra = divmod(rows // unit, n)
    out, start = [], 0
    for k in range(n):
        size = (base + (k < extra)) * unit
        if size:
            out.append((start, size))
            start += size
    assert start == rows, (rows, unit)
    return out


PIECE_BYTES = 1 << 20


def _pieces(shapes_dtypes, rows_of):
    out = []
    for b, (shape, dtype) in enumerate(shapes_dtypes):
        rows = rows_of(shape)
        n = max(1, min(4, rows * shape[-1] * jnp.dtype(dtype).itemsize // PIECE_BYTES))
        out += [(b, st, sz) for st, sz in _row_chunks(rows, n, dtype)]
    return out


def all_gather_chips(name, shards):
    nb = len(shards)
    pieces = _pieces([(s.shape, s.dtype) for s in shards], lambda shape: shape[0] // 2)
    n = len(pieces)

    def body(*refs):
        x_refs, out_refs, send_sems, recv_sems = refs[:nb], refs[nb:2 * nb], refs[2 * nb], refs[2 * nb + 1]
        x, y, c, chips = _place()
        sibling = (x, y, 1 - c)
        mine = 2 * x + y

        def copy(sem, chip, cc, k, to, from_input=False):
            b, st, sz = pieces[k]
            rows_k = pl.ds(cc * (x_refs[b].shape[0] // 2) + st, sz)
            dst = out_refs[b].at[chip, rows_k, :]
            return pltpu.make_async_remote_copy(src_ref=x_refs[b].at[rows_k, :] if from_input else dst, dst_ref=dst,
                                                send_sem=send_sems.at[sem], recv_sem=recv_sems.at[sem], device_id=to, device_id_type=MESH_ID)

        order = [(k, j, 2 * cx + cy, (cx, cy, c)) for k in range(n) for j, (cx, cy) in enumerate(chips)]
        first = [copy(j * n + k, mine, c, k, to, from_input=True) for k, j, _, to in order]
        for cp in first:
            cp.start()
        passed = []
        for k, j, chip, _ in order:
            copy(j * n + k, chip, c, k, sibling).wait_recv()
            passed.append(copy((3 + j) * n + k, chip, c, k, sibling))
            passed[-1].start()
        for k, j, chip, _ in order:
            copy((3 + j) * n + k, chip, 1 - c, k, sibling).wait_recv()
        for cp in first + passed:
            cp.wait_send()

    return pl.pallas_call(body, in_specs=[ANY] * nb, out_specs=[ANY] * nb, out_shape=[SDS((4,) + s.shape, s.dtype) for s in shards],
                          scratch_shapes=_dma_sems(6 * n), name=name)(*shards)


def plan_gather_ici(shards):
    pieces = _pieces([(s.shape, s.dtype) for s in shards], lambda shape: shape[0] // 2)
    n = len(pieces)

    def copies(x_refs, out_refs, send_sems, recv_sems):
        x, y, c, chips = _place()
        mine = 2 * x + y

        def copy(j, k, chip, to, from_input):
            b, st, sz = pieces[k]
            rows_k = pl.ds(c * (x_refs[b].shape[0] // 2) + st, sz)
            dst = out_refs[b].at[chip, rows_k, :]
            return pltpu.make_async_remote_copy(src_ref=x_refs[b].at[rows_k, :] if from_input else dst, dst_ref=dst, send_sem=send_sems.at[j * n + k],
                                                recv_sem=recv_sems.at[j * n + k], device_id=to, device_id_type=MESH_ID)

        order = [(k, j, 2 * cx + cy, (cx, cy, c)) for k in range(n) for j, (cx, cy) in enumerate(chips)]
        return [copy(j, k, mine, to, True) for k, j, _, to in order], [copy(j, k, chip, to, False) for k, j, chip, to in order]

    return Plan(shards, [SDS((4,) + s.shape, s.dtype) for s in shards], {}, 3 * n, copies)


def plan_gather_pass(gathered):
    pieces = _pieces([(g.shape[1:], g.dtype) for g in gathered], lambda shape: shape[0] // 2)
    n = len(pieces)

    def copies(_, out_refs, send_sems, recv_sems):
        x, y, c, chips = _place()

        def copy(j, k, chip, cc):
            b, st, sz = pieces[k]
            rows_k = out_refs[b].at[chip, pl.ds(cc * (out_refs[b].shape[1] // 2) + st, sz), :]
            return pltpu.make_async_remote_copy(src_ref=rows_k, dst_ref=rows_k, send_sem=send_sems.at[j * n + k], recv_sem=recv_sems.at[j * n + k],
                                                device_id=(x, y, 1 - c), device_id_type=MESH_ID)

        order = [(k, j, 2 * cx + cy) for k in range(n) for j, (cx, cy) in enumerate(chips)]
        return [copy(j, k, chip, c) for k, j, chip in order], [copy(j, k, chip, 1 - c) for k, j, chip in order]

    return Plan(gathered, [SDS(g.shape, g.dtype) for g in gathered], {i: i for i in range(len(gathered))}, 3 * n, copies)


def plan_pair_exchange(gs):
    pieces = _pieces([(g.shape, g.dtype) for g in gs], lambda shape: shape[1] // 2)

    def copies(g_refs, got_refs, send_sems, recv_sems):
        x, y, c, _ = _place()
        swaps = [pltpu.make_async_remote_copy(src_ref=g_refs[b].at[:, pl.ds((1 - c) * (g_refs[b].shape[1] // 2) + st, sz), :],
                                              dst_ref=got_refs[b].at[:, pl.ds(st, sz), :], send_sem=send_sems.at[k], recv_sem=recv_sems.at[k],
                                              device_id=(x, y, 1 - c), device_id_type=MESH_ID)
                 for k, (b, st, sz) in enumerate(pieces)]
        return swaps, swaps

    return Plan(gs, [SDS((g.shape[0], g.shape[1] // 2, g.shape[2]), g.dtype) for g in gs], {}, len(pieces), copies)


def plan_chip_scatter(ps):
    pieces = _pieces([(p.shape, p.dtype) for p in ps], lambda shape: shape[1])
    n = len(pieces)

    def copies(p_refs, q_refs, send_sems, recv_sems):
        x, y, c, chips = _place()
        mine = 2 * x + y

        def copy(j, k, src_slot, dst_slot, to):
            b, st, sz = pieces[k]
            return pltpu.make_async_remote_copy(src_ref=p_refs[b].at[src_slot, pl.ds(st, sz), :], dst_ref=q_refs[b].at[dst_slot, pl.ds(st, sz), :],
                                                send_sem=send_sems.at[j * n + k], recv_sem=recv_sems.at[j * n + k], device_id=to,
                                                device_id_type=MESH_ID)

        order = [(k, j, 2 * cx + cy, (cx, cy, c)) for k in range(n) for j, (cx, cy) in enumerate(chips)]
        return [copy(j, k, chip, mine, to) for k, j, chip, to in order], [copy(j, k, mine, chip, to) for k, j, chip, to in order]

    return Plan(ps, [SDS(p.shape, p.dtype) for p in ps], {}, 3 * n, copies)


def plan_pair_join(bufs):
    pieces = _pieces([(b.shape, b.dtype) for b in bufs], lambda shape: shape[0] // 2)

    def copies(_, out_refs, send_sems, recv_sems):
        x, y, c, _ = _place()

        def copy(k, cc):
            b, st, sz = pieces[k]
            rows_k = out_refs[b].at[pl.ds(cc * (out_refs[b].shape[0] // 2) + st, sz), :]
            return pltpu.make_async_remote_copy(src_ref=rows_k, dst_ref=rows_k, send_sem=send_sems.at[k], recv_sem=recv_sems.at[k],
                                                device_id=(x, y, 1 - c), device_id_type=MESH_ID)

        return [copy(k, c) for k in range(len(pieces))], [copy(k, 1 - c) for k in range(len(pieces))]

    return Plan(bufs, [SDS(b.shape, b.dtype) for b in bufs], {i: i for i in range(len(bufs))}, len(pieces), copies)


def run_plan(name, plan):
    hosted_call(lambda: None, name=name, grid=(1,), in_specs=[], out_specs=[], out_shape=[], operands=[], plans=[plan])
    return plan.results


def pair_add(name, g, got, place):
    slots, rows, cols = g.shape
    half = rows // 2
    tr = _row_tile(half)
    nb = half // tr

    def kern(_, g_ref, t_ref, o_ref):
        o_ref[...] = (g_ref[...].astype(F32) + t_ref[...].astype(F32)).astype(o_ref.dtype)

    blk = pl.BlockSpec((None, tr, cols), lambda s, i, p: (s, i, 0))
    grid_spec = pltpu.PrefetchScalarGridSpec(
        num_scalar_prefetch=1, grid=(slots, nb),
        in_specs=[pl.BlockSpec((None, tr, cols), lambda s, i, p: (s, p[1] * nb + i, 0)), blk], out_specs=blk)
    return pl.pallas_call(kern, grid_spec=grid_spec, out_shape=SDS((slots, half, cols), g.dtype), name=name,
                          compiler_params=_cparams())(place, g, got)


def chip_add(name, p, q, place):
    slots, half, cols = p.shape
    tr = _row_tile(half)
    nb = half // tr

    def kern(_, p_ref, q1, q2, q3, o_ref):
        o_ref[...] = p_ref[...].astype(F32) + q1[...].astype(F32) + q2[...].astype(F32) + q3[...].astype(F32)

    def slot(k):
        return pl.BlockSpec((None, tr, cols), lambda i, pr: ((pr[0] + k) % slots, i, 0))

    grid_spec = pltpu.PrefetchScalarGridSpec(
        num_scalar_prefetch=1, grid=(nb,), in_specs=[slot(0), slot(1), slot(2), slot(3)],
        out_specs=pl.BlockSpec((tr, cols), lambda i, pr: (pr[1] * nb + i, 0)))
    return pl.pallas_call(kern, grid_spec=grid_spec, out_shape=SDS((2 * half, cols), F32), name=name,
                          compiler_params=_cparams())(place, p, q, q, q)


def pair_adds(tag, gs, gots, place):
    return [pair_add(f"{tag}_pair_add_{i}", g, got, place) for i, (g, got) in enumerate(zip(gs, gots))]


def chip_adds(tag, pairs, qs, place):
    return [chip_add(f"{tag}_chip_add_{i}", p, q, place) for i, (p, q) in enumerate(zip(pairs, qs))]


def reduce_scatter_chips(tag, gs, place):
    pairs = pair_adds(tag, gs, run_plan(tag + "_pair_exchange", plan_pair_exchange(gs)), place)
    return run_plan(tag + "_pair_join", plan_pair_join(chip_adds(tag, pairs, run_plan(tag + "_chip_scatter", plan_chip_scatter(pairs)), place)))


BIG = [("w_out", (2, 512, 1024)), ("w_mem_kv", (2, 256, 1024)), ("s5_w_in", (1, 1024, 1024)), ("s5_w_glu", (1, 1536, 768)),
       ("mla_w_in", (1, 1024, 848)), ("mla_w_uq", (1, 512, 576)), ("mla_w_ukv", (1, 256, 768))]
SHARDED_SMALL = [("mla_q_lora_norm", (1, 128)), ("mla_kv_lora_norm", (1, 64))]
SMALL = [("ln_gain", (2, 1024)), ("mem_norm", (2, 1024)), ("xq_norm", (2, 128)), ("xk_norm", (2, 128)),
         ("s5_lambda_re", (1, 96, 64)), ("s5_lambda_im", (1, 96, 64)), ("s5_log_step", (1, 96)),
         ("s5_b_re", (1, 96, 64, 16)), ("s5_b_im", (1, 96, 64, 16)), ("s5_c_re", (1, 96, 16, 64)), ("s5_c_im", (1, 96, 16, 64)),
         ("s5_d", (1, 1536)), ("mla_q_nope_norm", (1, 128)), ("mla_k_nope_norm", (1, 128)), ("mla_q_rope_norm", (1, 64)),
         ("mla_k_rope_norm", (1, 64))]
WEIGHT_ORDER = ["ln_gain", "w_out", "mem_norm", "w_mem_kv", "xq_norm", "xk_norm", "s5_w_in", "s5_lambda_re", "s5_lambda_im",
                "s5_log_step", "s5_b_re", "s5_b_im", "s5_c_re", "s5_c_im", "s5_d", "s5_w_glu", "mla_w_in", "mla_q_lora_norm",
                "mla_kv_lora_norm", "mla_w_uq", "mla_w_ukv", "mla_q_nope_norm", "mla_k_nope_norm", "mla_q_rope_norm", "mla_k_rope_norm"]
MINOR_LAST = {"mla_w_in": (0, 2, 1), "mla_w_uq": (0, 2, 1), "s5_b_re": (0, 2, 3, 1), "s5_b_im": (0, 2, 3, 1),
              "s5_c_re": (0, 2, 3, 1), "s5_c_im": (0, 2, 3, 1)}
SMALL_FULL = SMALL + [(n, (1, 4 * s[1])) for n, s in SHARDED_SMALL]
N_SMALL = sum(math.prod(s) for _, s in SMALL_FULL)
SMALL_ROWS, SMALL_LANES = 128, 1024

WIDE0_IN, WIDE0_OUT, WIDE0_MKV = 0, 1024, 1536
PAIR_OUT, PAIR_MKV, PAIR_ROWS = 0, 512, 768


def stack_shards(w, dtype):
    wide0 = jnp.concatenate([w["s5_w_in"][0], w["w_out"][0], w["w_mem_kv"][0]], axis=0)
    wide1 = jnp.concatenate([w["w_out"][1], w["w_mem_kv"][1]], axis=0)
    return ([wide0.astype(dtype), w["s5_w_glu"][0].astype(dtype)],
            [wide1.astype(dtype), w["mla_w_ukv"][0].astype(dtype), w["mla_w_in"][0].astype(dtype), w["mla_w_uq"][0].astype(dtype)])


def weight_views0(wide0, glu):
    return {"s5_w_in": Sharded(wide0, "col", WIDE0_IN, 1024), "w_out": Sharded(wide0, "row", WIDE0_OUT, 512),
            "w_mem_kv": Sharded(wide0, "row", WIDE0_MKV, 256), "s5_w_glu": Sharded(glu, "col", 0, 1536)}


def weight_views1(wide1, ukv):
    return {"w_out": Sharded(wide1, "row", PAIR_OUT, 512), "w_mem_kv": Sharded(wide1, "row", PAIR_MKV, 256), "mla_w_ukv": Sharded(ukv, "col", 0, 256)}


def grad_views():
    pair = SDS((4, PAIR_ROWS, 1024), BF16)
    return {"w_out": Sharded(pair, "row", PAIR_OUT, 512), "w_mem_kv": Sharded(pair, "row", PAIR_MKV, 256),
            "s5_w_in": Sharded(SDS((4, 1024, 1024), BF16), "col", 0, 1024), "s5_w_glu": Sharded(SDS((4, 1536, 768), BF16), "col", 0, 1536),
            "mla_w_ukv": Sharded(SDS((4, 256, 768), BF16), "col", 0, 256)}


def cols_to_shards(full):
    return full.reshape(full.shape[0], 4, full.shape[1] // 4).transpose(1, 0, 2)


def shards_to_cols(arr):
    return arr.transpose(1, 0, 2).reshape(arr.shape[1], 4 * arr.shape[2])


def mla_in_permute(w):
    o1, o2, o3, o4 = QL, QL + KVL, QL + KVL + ROPE, QL + KVL + ROPE + XQW
    return jnp.concatenate([w[:, o4:], w[:, :o1], w[:, o3:o4], w[:, o1:o2], w[:, o2:o3],
                            jnp.zeros((w.shape[0], MLA_IN_P - MLA_IN), w.dtype)], axis=1)


def mla_in_unpermute(d):
    return jnp.concatenate([d[:, 2048:2560], d[:, 3072:3328], d[:, 3328:3392], d[:, 2560:3072], d[:, :2048]], axis=1)


def uq_permute(w):
    w3 = w.reshape(w.shape[0], MH, NOPE + ROPE)
    return jnp.concatenate([w3[:, :, :NOPE].reshape(w.shape[0], MH * NOPE), w3[:, :, NOPE:].reshape(w.shape[0], MH * ROPE)], axis=1)


def uq_unpermute(d):
    dn = d[:, :MH * NOPE].reshape(d.shape[0], MH, NOPE)
    dr = d[:, MH * NOPE:].reshape(d.shape[0], MH, ROPE)
    return jnp.concatenate([dn, dr], axis=2).reshape(d.shape[0], MH * (NOPE + ROPE))


def time_permute(a):
    return a.reshape(SEG, SEG_LEN, a.shape[-1]).transpose(1, 0, 2).reshape(L, a.shape[-1])


def time_unpermute(a):
    return a.reshape(SEG_LEN, SEG, a.shape[-1]).transpose(1, 0, 2).reshape(L, a.shape[-1])


def mem_branch_fwd(tag, mem, mem_norm, w_mem_kv, xk_norm):
    mn = row_fwd(tag + "_mem_rms", fn_rms, ML, ML, [(mem, D, 0)], [mem_norm], [(D, BF16)])[0]
    kv = matmul(tag + "_mem_kv", mn, w_mem_kv, "nn", F32)
    kn = row_fwd(tag + "_mem_knorm", fn_mem_k, ML, ML, [(kv, XQW, 0)], [xk_norm], [(XQW, F32)])[0]
    return mn, kv, kn


def mem_branch_bwd(tag, mem, mem_norm, w_mem_kv, xk_norm, mn, kv, dkn, dv, g_view, g_wide):
    dk, dxk = row_bwd(tag + "_mem_knorm_bwd", fn_mem_k, ML, ML, [(kv, XQW, 0)], [xk_norm], [(dkn, XQW, 0)], [True], [True])
    dkv = jnp.concatenate([dk, dv], axis=1)
    dmn = matmul(tag + "_mem_kv_dx", dkv, w_mem_kv, "nt", F32)
    g_wide = matmul(tag + "_mem_kv_dw", mn, dkv, "tn", out=g_view, into=g_wide)
    dmem_norm = row_bwd(tag + "_mem_rms_bwd", fn_rms, ML, ML, [(mem, D, 0)], [mem_norm], [(dmn, D, 0)], [False], [True])[0]
    return g_wide, dmem_norm, dxk


def mem_attn_fwd(tag, proj, cb, kn, kv, xq_norm):
    return row_fwd(tag + "_mem_attn", fn_mem_attn, L, 256, [(proj, XQW, cb)], [kn, kv[:, XQW:], xq_norm], [(XQW, F32)])[0]


def mem_attn_bwd(tag, proj, cb, kn, kv, xq_norm, dmo, dproj):
    place = {"cols": proj.shape[1], "cb": cb, "into": dproj, "dtype": dproj.dtype}
    return row_bwd(tag + "_mem_attn_bwd", fn_mem_attn, L, 256, [(proj, XQW, cb)], [kn, kv[:, XQW:], xq_norm], [(dmo, XQW, 0)],
                   [place], [True, True, True])


def device_step(x, mem, positions, target, small, env, hooks=None):
    hooks = hooks or {}

    def plans_for(name):
        return hooks[("plans", name)](env) if ("plans", name) in hooks else ()

    def after(name):
        if ("after", name) in hooks:
            hooks[("after", name)](env)

    g = {}
    w0 = weight_views0(env["wide0"], env["glu"])
    gw = grad_views()
    ln, mem_norm, xq_norm, xk_norm = small["ln_gain"], small["mem_norm"], small["xq_norm"], small["xk_norm"]

    lre, lim = small["s5_lambda_re"][0], small["s5_lambda_im"][0]
    ls = small["s5_log_step"].reshape(SG, 1)
    one = pl.BlockSpec((SG, SP), lambda i: (0, 0))
    col = pl.BlockSpec((SG, 1), lambda i: (0, 0))
    disc_ins = [(lre, one), (lim, one), (ls, col)]
    a_re, a_im, coef_re, coef_im = stage("s5_disc", fn_s5_disc, (1,), disc_ins, [(SDS((SG, SP), F32), one)] * 4)
    b_re, b_im = small["s5_b_re"].reshape(SN, SC), small["s5_b_im"].reshape(SN, SC)
    c_re, c_im = small["s5_c_re"].reshape(PW, SP), small["s5_c_im"].reshape(PW, SP)
    bmat_rows = [(b_re, SC, 0), (b_im, SC, 0), (coef_re.reshape(SN, 1), 1, 0), (coef_im.reshape(SN, 1), 1, 0)]
    wb_re, wb_im = row_fwd("s5_bmat", fn_s5_bmat, SN, 512, bmat_rows, [], [(128, F32)] * 2)
    cmat_rows = [(c_re, SP, 0), (c_im, SP, 0)]
    wc_re, wc_im = row_fwd("s5_cmat", fn_s5_cmat, PW, 128, cmat_rows, [], [(512, F32)] * 2)
    a_re_v, a_im_v = a_re.reshape(1, SN), a_im.reshape(1, SN)
    s5_d = small["s5_d"]

    xp = time_permute(x)
    h0 = row_fwd("l0_rms", fn_rms, L, 256, [(xp, D, 0)], [ln[0:1]], [(D, BF16)])[0]
    proj0 = matmul("l0_in", h0, w0["s5_w_in"], "nn")
    s_re, s_im, g0 = s5_forward(proj0, wb_re, wb_im, wc_re, wc_im, a_re_v, a_im_v, s5_d, plans=plans_for("s5_forward"))
    z0 = matmul("l0_glu", g0, w0["s5_w_glu"], "nn", plans=plans_for("l0_glu"))
    after("l0_glu")
    mn0, kv0, kn0 = mem_branch_fwd("l0", mem, mem_norm[0:1], w0["w_mem_kv"], xk_norm[0:1])
    mo0 = mem_attn_fwd("l0", proj0, 3, kn0, kv0, xq_norm[0:1])
    o0 = row_fwd("l0_merge", fn_merge_glu, L, 256, [(z0, 2 * PW, 0), (mo0, XQW, 0), (proj0, BW, 1)], [], [(BW, BF16)])[0]
    x1p = matmul("l0_out", o0, w0["w_out"], "nn", F32, add=xp)
    x1 = time_unpermute(x1p)

    w1 = weight_views1(env["wide1"], env["ukv"])
    w_in1, w_uq = env["w_in1"], env["w_uq"]
    h1 = row_fwd("l1_rms", fn_rms, L, 256, [(x1, D, 0)], [ln[1:2]], [(D, BF16)])[0]
    proj1 = matmul("l1_in", h1, w_in1, "nn")
    qln, kvln = env["q_lora_norm"].reshape(1, QL), env["kv_lora_norm"].reshape(1, KVL)
    cqn = row_fwd("l1_q_lora_rms", fn_rms, L, 256, [(proj1, QL, 4)], [qln], [(QL, BF16)])[0]
    ckvn = row_fwd("l1_kv_lora_rms", fn_rms, L, 256, [(proj1, KVL, 12)], [kvln], [(KVL, BF16)])[0]
    q = matmul("l1_uq", cqn, w_uq, "nn")
    kv = matmul("l1_ukv", ckvn, w1["mla_w_ukv"], "nn")
    inv_freq = ROPE_THETA ** (-jnp.arange(ROPE // 2, dtype=F32) / (ROPE // 2))
    ang = positions.astype(F32)[:, None] * inv_freq
    cos2 = jnp.tile(jnp.cos(ang), (1, 4))
    sin_signed = jnp.tile(jnp.concatenate([-jnp.sin(ang), jnp.sin(ang)], axis=1), (1, 2))
    qnn, knn = small["mla_q_nope_norm"], small["mla_k_nope_norm"]
    qrn, krn = jnp.tile(small["mla_q_rope_norm"], (1, 2)), jnp.tile(small["mla_k_rope_norm"], (1, 2))
    tp = 256
    prep_ins = [(q, rspec(tp, MH * (NOPE + ROPE))), (kv, rspec(tp, MH * 256)), (proj1, rspec(tp, 128, 26)),
                (cos2, rspec(tp, 128)), (sin_signed, rspec(tp, 128))] + [(a, cspec((1, 128))) for a in (qnn, knn, qrn, krn)]
    hq_spec = pl.BlockSpec((MH, tp, 256), lambda i: (0, i, 0))
    hv_spec = pl.BlockSpec((MH, tp, 128), lambda i: (0, i, 0))
    qf, kf, vh = stage("l1_mla_prep", fn_mla_prep, (L // tp,), prep_ins,
                       [(SDS((MH, L, 256), BF16), hq_spec), (SDS((MH, L, 256), BF16), hq_spec), (SDS((MH, L, 128), BF16), hv_spec)])
    attn = causal_attn(qf, kf, vh)
    mn1, kv1, kn1 = mem_branch_fwd("l1", mem, mem_norm[1:2], w1["w_mem_kv"], xk_norm[1:2])
    mo1 = mem_attn_fwd("l1", proj1, 5, kn1, kv1, xq_norm[1:2])
    o1 = row_fwd("l1_merge", fn_merge, L, 256, [(attn, PW, 0), (mo1, XQW, 0), (proj1, BW, 0)], [], [(BW, BF16)])[0]
    x2 = matmul("l1_out", o1, w1["w_out"], "nn", F32, add=x1)
    dx2, loss = loss_and_grad(x2, target)

    do1 = matmul("l1_out_dx", dx2, w1["w_out"], "nt")
    g_pair1 = matmul("l1_out_dw", o1, dx2, "tn", out=gw["w_out"])
    dattn, dmo1, dproj1 = row_bwd("l1_merge_bwd", fn_merge, L, 256, [(attn, PW, 0), (mo1, XQW, 0), (proj1, BW, 0)], [],
                                  [(do1, BW, 0)], [True, True, {"cols": MLA_IN_P, "cb": 0, "dtype": BF16}], [])
    dproj1, dkn1, dv1, dxqn1 = mem_attn_bwd("l1", proj1, 5, kn1, kv1, xq_norm[1:2], dmo1, dproj1)
    env["g_pair1"], dmem_norm1, dxk1 = mem_branch_bwd("l1", mem, mem_norm[1:2], w1["w_mem_kv"], xk_norm[1:2], mn1, kv1, dkn1, dv1,
                                                      gw["w_mem_kv"], g_pair1)
    dqf, dkf, dvh = causal_attn_bwd(qf, kf, vh, dattn)
    prep_diffs = [("row", SDS((L, MH * (NOPE + ROPE)), BF16), rspec(tp, MH * (NOPE + ROPE))), ("row", SDS((L, MH * 256), BF16), rspec(tp, MH * 256)),
                  ("row", SDS((L, MLA_IN_P), BF16), rspec(tp, 128, 26), {"into": dproj1}), None, None] + [("acc", (0,))] * 4
    dq, dkv, dproj1, dqnn, dknn, dqrn, dkrn = stage_bwd("l1_mla_prep_bwd", fn_mla_prep, (L // tp,), prep_ins,
                                                        [(dqf, hq_spec), (dkf, hq_spec), (dvh, hv_spec)], prep_diffs)
    dcqn = matmul("l1_uq_dx", dq, w_uq, "nt")
    env["dw_uq"] = matmul("l1_uq_dw", cqn, dq, "tn")
    dckvn = matmul("l1_ukv_dx", dkv, w1["mla_w_ukv"], "nt")
    env["g_ukv"] = matmul("l1_ukv_dw", ckvn, dkv, "tn", out=gw["mla_w_ukv"])
    dproj1, dqln = row_bwd("l1_q_lora_rms_bwd", fn_rms, L, 256, [(proj1, QL, 4)], [qln], [(dcqn, QL, 0)],
                           [{"cols": MLA_IN_P, "cb": 4, "into": dproj1, "dtype": BF16}], [True])
    dproj1, dkvln = row_bwd("l1_kv_lora_rms_bwd", fn_rms, L, 256, [(proj1, KVL, 12)], [kvln], [(dckvn, KVL, 0)],
                            [{"cols": MLA_IN_P, "cb": 12, "into": dproj1, "dtype": BF16}], [True])
    dh1 = matmul("l1_in_dx", dproj1, w_in1, "nt")
    env["dw_in1"] = matmul("l1_in_dw", h1, dproj1, "tn")
    dx1, dln1 = row_bwd("l1_rms_bwd", fn_rms, L, 256, [(x1, D, 0)], [ln[1:2]], [(dh1, D, 0)], [{"add": (dx2, D, 0)}], [True],
                        plans=plans_for("l1_rms_bwd"))
    dx1p = time_permute(dx1)

    do0 = matmul("l0_out_dx", dx1p, w0["w_out"], "nt", plans=plans_for("l0_out_dx"))
    g_pair0 = matmul("l0_out_dw", o0, dx1p, "tn", out=gw["w_out"])
    dz0, dmo0, dproj0 = row_bwd("l0_merge_bwd", fn_merge_glu, L, 256, [(z0, 2 * PW, 0), (mo0, XQW, 0), (proj0, BW, 1)], [],
                                [(do0, BW, 0)], [{"dtype": BF16}, True, {"cols": 2 * BW, "cb": 1, "dtype": BF16}], [])
    dproj0, dkn0, dv0, dxqn0 = mem_attn_bwd("l0", proj0, 3, kn0, kv0, xq_norm[0:1], dmo0, dproj0)
    env["g_pair0"], dmem_norm0, dxk0 = mem_branch_bwd("l0", mem, mem_norm[0:1], w0["w_mem_kv"], xk_norm[0:1], mn0, kv0, dkn0, dv0,
                                                      gw["w_mem_kv"], g_pair0)
    env["g_glu"] = matmul("l0_glu_dw", g0, dz0, "tn", out=gw["s5_w_glu"], plans=plans_for("l0_glu_dw"))
    dg0 = matmul("l0_glu_dx", dz0, w0["s5_w_glu"], "nt", plans=plans_for("l0_glu_dx"))
    dproj0, dd, dwc_re, dwc_im, dwb_re, dwb_im, da_re, da_im = s5_backward(dg0, proj0, s_re, s_im, wb_re, wb_im, wc_re, wc_im,
                                                                           a_re_v, a_im_v, s5_d, dproj0, plans=plans_for("s5_backward"))
    env["g_in0"] = matmul("l0_in_dw", h0, dproj0, "tn", out=gw["s5_w_in"], plans=plans_for("l0_in_dw"))
    dh0 = matmul("l0_in_dx", dproj0, w0["s5_w_in"], "nt", plans=plans_for("l0_in_dx"))
    dxp, dln0 = row_bwd("l0_rms_bwd", fn_rms, L, 256, [(xp, D, 0)], [ln[0:1]], [(dh0, D, 0)], [{"add": (dx1p, D, 0)}], [True])
    grad_x = time_unpermute(dxp)

    db_re, db_im, dcoef_re, dcoef_im = row_bwd("s5_bmat_bwd", fn_s5_bmat, SN, 512, bmat_rows, [], [(dwb_re, 128, 0), (dwb_im, 128, 0)],
                                               [True] * 4, [], plans=plans_for("s5_bmat_bwd"))
    dc_re, dc_im = row_bwd("s5_cmat_bwd", fn_s5_cmat, PW, 128, cmat_rows, [], [(dwc_re, 512, 0), (dwc_im, 512, 0)], [True] * 2, [],
                           plans=plans_for("s5_cmat_bwd"))
    disc_cts = [(da_re.reshape(SG, SP), one), (da_im.reshape(SG, SP), one), (dcoef_re.reshape(SG, SP), one), (dcoef_im.reshape(SG, SP), one)]
    dlre, dlim, dls = stage_bwd("s5_disc_bwd", fn_s5_disc, (1,), disc_ins, disc_cts, [("acc", (0,))] * 3)

    g["ln_gain"] = jnp.concatenate([dln0, dln1], axis=0)
    g["mem_norm"] = jnp.concatenate([dmem_norm0, dmem_norm1], axis=0)
    g["xq_norm"] = jnp.concatenate([dxqn0, dxqn1], axis=0)
    g["xk_norm"] = jnp.concatenate([dxk0, dxk1], axis=0)
    g["s5_lambda_re"], g["s5_lambda_im"], g["s5_log_step"] = dlre, dlim, dls
    g["s5_b_re"], g["s5_b_im"], g["s5_c_re"], g["s5_c_im"] = db_re, db_im, dc_re, dc_im
    g["s5_d"] = dd
    g["mla_q_lora_norm"], g["mla_kv_lora_norm"] = dqln, dkvln
    g["mla_q_nope_norm"], g["mla_k_nope_norm"] = dqnn, dknn
    g["mla_q_rope_norm"] = dqrn[:, :ROPE] + dqrn[:, ROPE:]
    g["mla_k_rope_norm"] = dkrn[:, :ROPE] + dkrn[:, ROPE:]
    return loss, grad_x, g


def kernel(x, mem, positions, ln_gain, w_out, mem_norm, w_mem_kv, xq_norm, xk_norm, s5_w_in, s5_lambda_re, s5_lambda_im, s5_log_step, s5_b_re, s5_b_im, s5_c_re, s5_c_im, s5_d, s5_w_glu, mla_w_in, mla_q_lora_norm, mla_kv_lora_norm, mla_w_uq, mla_w_ukv, mla_q_nope_norm, mla_k_nope_norm, mla_q_rope_norm, mla_k_rope_norm, loss_target, m_ln_gain, m_w_out, m_mem_norm, m_w_mem_kv, m_xq_norm, m_xk_norm, m_s5_w_in, m_s5_lambda_re, m_s5_lambda_im, m_s5_log_step, m_s5_b_re, m_s5_b_im, m_s5_c_re, m_s5_c_im, m_s5_d, m_s5_w_glu, m_mla_w_in, m_mla_q_lora_norm, m_mla_kv_lora_norm, m_mla_w_uq, m_mla_w_ukv, m_mla_q_nope_norm, m_mla_k_nope_norm, m_mla_q_rope_norm, m_mla_k_rope_norm, v_ln_gain, v_w_out, v_mem_norm, v_w_mem_kv, v_xq_norm, v_xk_norm, v_s5_w_in, v_s5_lambda_re, v_s5_lambda_im, v_s5_log_step, v_s5_b_re, v_s5_b_im, v_s5_c_re, v_s5_c_im, v_s5_d, v_s5_w_glu, v_mla_w_in, v_mla_q_lora_norm, v_mla_kv_lora_norm, v_mla_w_uq, v_mla_w_ukv, v_mla_q_nope_norm, v_mla_k_nope_norm, v_mla_q_rope_norm, v_mla_k_rope_norm):
    args = dict(locals())
    wts = {n: args[n] for n in WEIGHT_ORDER}
    mom = {n: args["m_" + n] for n in WEIGHT_ORDER}
    var = {n: args["v_" + n] for n in WEIGHT_ORDER}

    chip = 2 * lax.axis_index("x") + lax.axis_index("y")
    place = jnp.stack([chip, lax.axis_index("c")]).astype(jnp.int32)

    def own_slot(gathered, shards):
        return [lax.dynamic_update_slice(g, s[None], (chip, 0, 0)) for g, s in zip(gathered, shards)]

    shards0, shards1 = stack_shards(wts, BF16)
    shards1.append(jnp.concatenate([mla_q_lora_norm, jnp.pad(mla_kv_lora_norm, ((0, 0), (0, 64))), jnp.zeros((14, 128), F32)], axis=0))
    env = dict(zip(("wide0", "glu"), own_slot(all_gather_chips("gather_weights0", shards0), shards0)))
    over_ici = plan_gather_ici(shards1)
    hooks = {("plans", "s5_forward"): lambda env: [over_ici]}
    passed_on = []

    def pass_on(env):
        passed_on.append(plan_gather_pass(over_ici.results))
        return passed_on

    def layer1_weights(env):
        wide1, ukv, in1, uq, norms = own_slot(passed_on[0].results, shards1)
        env.update(wide1=wide1, ukv=ukv, w_in1=mla_in_permute(shards_to_cols(in1)), w_uq=uq_permute(shards_to_cols(uq)),
                   q_lora_norm=norms[:, 0, :], kv_lora_norm=norms[:, 1, :64])

    hooks["plans", "l0_glu"], hooks["after", "l0_glu"] = pass_on, layer1_weights

    rs = {}

    def swap(k, gs):
        rs[k, "g"], rs[k, "swap"] = gs, plan_pair_exchange(gs)
        return rs[k, "swap"]

    def scatter(k, part=slice(None)):
        if (k, "pairs") not in rs:
            rs[k, "pairs"], rs[k, "scatter"] = pair_adds(f"rs{k}", rs[k, "g"], rs[k, "swap"].results, place), []
        rs[k, "scatter"].append(plan_chip_scatter(rs[k, "pairs"][part]))
        return rs[k, "scatter"][-1]

    def join(k):
        rs[k, "join"] = plan_pair_join(chip_adds(f"rs{k}", rs[k, "pairs"], [q for p in rs[k, "scatter"] for q in p.results], place))
        return rs[k, "join"]

    hooks["plans", "l1_rms_bwd"] = lambda env: [swap(1, [env["g_pair1"], env["g_ukv"], cols_to_shards(mla_in_unpermute(env["dw_in1"])).astype(BF16),
                                                          cols_to_shards(uq_unpermute(env["dw_uq"])).astype(BF16)])]
    hooks["plans", "l0_glu_dw"] = lambda env: [scatter(1, slice(0, 2))]
    hooks["plans", "l0_glu_dx"] = lambda env: [swap(0, [env["g_pair0"], env["g_glu"]])]
    hooks["plans", "s5_backward"] = lambda env: [scatter(1, slice(2, 4)), scatter(0)]
    hooks["plans", "l0_in_dw"] = lambda env: [join(1)]
    hooks["plans", "l0_in_dx"] = lambda env: [swap(2, [env["g_in0"]]), join(0)]
    hooks["plans", "s5_bmat_bwd"] = lambda env: [scatter(2)]
    hooks["plans", "s5_cmat_bwd"] = lambda env: [join(2)]

    small = {n: wts[n] for n, _ in SMALL}
    loss, grad_x, g = device_step(x[0], mem[0], positions[0], loss_target[0], small, env, hooks)
    loss = lax.psum(loss[0, 0], MESH_AXES)
    (r_pair1, r_ukv, r_in1, r_uq), (r_pair0, r_glu), (r_in0,) = (rs[k, "join"].results for k in (1, 0, 2))

    small_flat = jnp.concatenate([g[n].reshape(-1) for n, _ in SMALL_FULL])
    g_small = jnp.pad(small_flat, (0, 4 * SMALL_ROWS * SMALL_LANES - N_SMALL)).astype(BF16).reshape(4, SMALL_ROWS, SMALL_LANES)
    r_small = reduce_scatter_chips("rs3", [g_small], place)[0]
    small_all = own_slot(all_gather_chips("gather_small_grads", [r_small]), [r_small])[0].reshape(-1)[:N_SMALL]

    grads = {"w_out": jnp.stack([r_pair0[:PAIR_MKV], r_pair1[:PAIR_MKV]]), "w_mem_kv": jnp.stack([r_pair0[PAIR_MKV:], r_pair1[PAIR_MKV:]]),
             "s5_w_in": r_in0[None], "s5_w_glu": r_glu[None], "mla_w_ukv": r_ukv[None], "mla_w_in": r_in1[None], "mla_w_uq": r_uq[None]}
    off = 0
    for n, s in SMALL_FULL:
        grads[n] = small_all[off:off + math.prod(s)].reshape(s)
        off += math.prod(s)
    for n, s in SHARDED_SMALL:
        grads[n] = lax.dynamic_slice(grads[n], (0, chip * s[1]), s)

    delta, new_m, new_v = {}, {}, {}
    for n, s in BIG + [(n, s) for n, s in SMALL if len(s) == 4]:
        perm = MINOR_LAST.get(n, tuple(range(len(s))))
        turned = tuple(s[p] for p in perm)
        view = lambda a: jnp.transpose(a, perm).reshape(-1, turned[-1])
        res = adamw("adamw_" + n, view(wts[n]), view(grads[n]), view(mom[n]), view(var[n]))
        delta[n], new_m[n], new_v[n] = (jnp.transpose(r.reshape(turned), tuple(perm.index(i) for i in range(len(s)))) for r in res)
    small_names = [n for n, s in SMALL if len(s) < 4] + [n for n, _ in SHARDED_SMALL]
    n_own = sum(wts[n].size for n in small_names)
    rows_own = -(-n_own // (8 * 128)) * 8

    def pack_small(d):
        flat = jnp.concatenate([d[n].reshape(-1) for n in small_names])
        return jnp.pad(flat, (0, rows_own * 128 - n_own), constant_values=1.0).reshape(rows_own, 128)

    res = adamw("adamw_small", pack_small(wts), pack_small(grads), pack_small(mom), pack_small(var))
    off = 0
    for n in small_names:
        size = wts[n].size
        delta[n], new_m[n], new_v[n] = (r.reshape(-1)[off:off + size].reshape(wts[n].shape) for r in res)
        off += size

    return (loss, grad_x[None], *[grads[n] for n in WEIGHT_ORDER], *[delta[n] for n in WEIGHT_ORDER],
            *[new_m[n] for n in WEIGHT_ORDER], *[new_v[n] for n in WEIGHT_ORDER])
```

```python
import functools
import math

import jax
import jax.numpy as jnp
from jax import lax
from jax.experimental import pallas as pl
from jax.experimental.pallas import tpu as pltpu

F32, BF16 = jnp.float32, jnp.bfloat16
SDS = jax.ShapeDtypeStruct

D = 1024
L = 2048
ML = 256
BW = 2 * D
XQW = BW // 4
PW = BW - XQW
XH, XHD = 4, 128
SG, SC, SP = 96, 16, 64
SN = SG * SP
NOPE, ROPE, VD = 128, 64, 128
MH = 12
QL, KVL = 512, 256
EPS = 1e-6
ROPE_THETA = 10000.0
MLA_IN = QL + KVL + ROPE + XQW + BW
MLA_IN_P = 3456
ADAM_LR, ADAM_B1, ADAM_B2, ADAM_EPS, ADAM_WD, ADAM_STEP = 0.001, 0.9, 0.999, 1e-08, 0.01, 10

VMEM_LIMIT = 48 * 2**20
SEG = 8
SEG_LEN = L // SEG
MESH_AXES = ("x", "y", "c")


def _cparams():
    return pltpu.CompilerParams(vmem_limit_bytes=VMEM_LIMIT)


def _dg(a, b, ca, cb):
    return lax.dot_general(a.astype(BF16), b.astype(BF16), (((ca,), (cb,)), ((), ())), preferred_element_type=F32)


@jax.custom_vjp
def mm_nn(a, b):
    return _dg(a, b, 1, 0)


mm_nn.defvjp(lambda a, b: (_dg(a, b, 1, 0), (a, b)), lambda res, g: (_dg(g, res[1], 1, 1), _dg(res[0], g, 0, 0)))


@jax.custom_vjp
def mm_nt(a, b):
    return _dg(a, b, 1, 1)


mm_nt.defvjp(lambda a, b: (_dg(a, b, 1, 1), (a, b)), lambda res, g: (_dg(g, res[1], 1, 0), _dg(g, res[0], 0, 0)))


@functools.partial(jax.custom_vjp, nondiff_argnums=(1,))
def lane_roll(x, shift):
    return pltpu.roll(x, shift, 1)


lane_roll.defvjp(lambda x, shift: (pltpu.roll(x, shift, 1), None),
                 lambda shift, _, g: (pltpu.roll(g, (128 - shift) % 128, 1),))


def rms(x, g):
    return x * lax.rsqrt(jnp.mean(x * x, axis=-1, keepdims=True) + EPS) * g


@jax.custom_vjp
def softmax_rows(s):
    e = jnp.exp(s - jnp.max(s, axis=-1, keepdims=True))
    return e / jnp.sum(e, axis=-1, keepdims=True)


def _softmax_rows_fwd(s):
    p = softmax_rows(s)
    return p, p


def _softmax_rows_bwd(p, g):
    return (p * (g - jnp.sum(g * p, axis=-1, keepdims=True)),)


softmax_rows.defvjp(_softmax_rows_fwd, _softmax_rows_bwd)


def silu(x):
    return x * jax.nn.sigmoid(x)


ANY = pl.BlockSpec(memory_space=pl.ANY)
MESH_ID = pl.DeviceIdType.MESH


def _dma_sems(n):
    return [pltpu.SemaphoreType.DMA((n,)), pltpu.SemaphoreType.DMA((n,))]


class Plan:
    def __init__(self, operands, out_shape, aliases, n_sems, copies):
        self.operands, self.out_shape, self.aliases, self.n_sems, self.copies = list(operands), list(out_shape), aliases, n_sems, copies
        self.results = None


def hosted_call(kern, *, name, grid, in_specs, out_specs, out_shape, operands, scratch_shapes=(), aliases=None, cparams=None, plans=()):
    n_in, n_out, n_scr = len(in_specs), len(out_specs), len(scratch_shapes)
    p_in, p_out = [len(p.operands) for p in plans], [len(p.out_shape) for p in plans]
    all_aliases = dict(aliases or {})
    in_off, out_off = n_in, n_out
    for p, ni, no in zip(plans, p_in, p_out):
        all_aliases.update({in_off + i: out_off + o for i, o in p.aliases.items()})
        in_off, out_off = in_off + ni, out_off + no

    def body(*refs):
        pos, pins, pouts = n_in, [], []
        for ni in p_in:
            pins.append(refs[pos:pos + ni])
            pos += ni
        main_out = refs[pos:pos + n_out]
        pos += n_out
        for no in p_out:
            pouts.append(refs[pos:pos + no])
            pos += no
        main_scr = refs[pos:pos + n_scr]
        pos += n_scr
        if plans:
            ids = [pl.program_id(ax) for ax in range(len(grid))]
            first = functools.reduce(jnp.logical_and, [i == 0 for i in ids])
            last = functools.reduce(jnp.logical_and, [i == g - 1 for i, g in zip(ids, grid)])
            copies = [p.copies(pins[k], pouts[k], refs[pos + 2 * k], refs[pos + 2 * k + 1]) for k, p in enumerate(plans)]

            @pl.when(first)
            def _():
                for sends, _ in copies:
                    for cp in sends:
                        cp.start()

        kern(*refs[:n_in], *main_out, *main_scr)
        if plans:
            @pl.when(last)
            def _():
                for sends, recvs in copies:
                    for cp in recvs:
                        cp.wait_recv()
                    for cp in sends:
                        cp.wait_send()

    res = pl.pallas_call(body, grid=grid, in_specs=list(in_specs) + [ANY] * sum(p_in), out_specs=list(out_specs) + [ANY] * sum(p_out),
                         out_shape=list(out_shape) + [s for p in plans for s in p.out_shape],
                         scratch_shapes=list(scratch_shapes) + [s for p in plans for s in _dma_sems(p.n_sems)],
                         input_output_aliases=all_aliases, name=name, compiler_params=cparams or _cparams())(
        *operands, *[a for p in plans for a in p.operands])
    pos = n_out
    for p, no in zip(plans, p_out):
        p.results = list(res[pos:pos + no])
        pos += no
    return list(res[:n_out])


def _wide(v):
    return v.astype(F32) if v.dtype == BF16 else v


def stage(name, fn, grid, ins, outs):
    n_in = len(ins)

    def kern(*refs):
        res = fn(*[_wide(r[...]) for r in refs[:n_in]])
        for r, v in zip(refs[n_in:], res):
            r[...] = v.astype(r.dtype)

    return pl.pallas_call(kern, grid=grid, in_specs=[s for _, s in ins], out_specs=[s for _, s in outs],
                          out_shape=[sd for sd, _ in outs], name=name, compiler_params=_cparams())(*[a for a, _ in ins])


def stage_bwd(name, fn, grid, ins, cts, diffs, plans=()):
    n_in, n_ct = len(ins), len(cts)
    didx = [i for i, d in enumerate(diffs) if d is not None]
    opts = {i: (diffs[i][3] if len(diffs[i]) > 3 else {}) for i in didx if diffs[i][0] == "row"}
    adds = [(i, opts[i]["add"]) for i in opts if "add" in opts[i]]
    intos = [(i, opts[i]["into"]) for i in opts if "into" in opts[i]]
    n_add, n_into = len(adds), len(intos)
    add_pos = {i: n_in + n_ct + k for k, (i, _) in enumerate(adds)}
    n_extra = n_in + n_ct + n_add + n_into

    def kern(*refs):
        vals = [_wide(r[...]) for r in refs[:n_in]]

        def f(*dv):
            full = list(vals)
            for i, v in zip(didx, dv):
                full[i] = v
            return fn(*full)

        _, vjp = jax.vjp(f, *[vals[i].astype(F32) for i in didx])
        gs = vjp(tuple(c[...].astype(F32) for c in refs[n_in:n_in + n_ct]))
        for o_ref, i, g in zip(refs[n_extra:], didx, gs):
            if diffs[i][0] == "row":
                if i in add_pos:
                    g = g + refs[add_pos[i]][...].astype(F32)
                o_ref[...] = g.astype(o_ref.dtype)
            else:
                first = functools.reduce(jnp.logical_and, [pl.program_id(ax) == 0 for ax in diffs[i][1]])

                @pl.when(first)
                def _():
                    o_ref[...] = g

                @pl.when(jnp.logical_not(first))
                def _():
                    o_ref[...] += g

    out_shape, out_specs = [], []
    for i in didx:
        if diffs[i][0] == "row":
            out_shape.append(diffs[i][1])
            out_specs.append(diffs[i][2])
        else:
            out_shape.append(SDS(ins[i][0].shape, F32))
            out_specs.append(ins[i][1])
    aliases = {n_in + n_ct + n_add + k: didx.index(i) for k, (i, _) in enumerate(intos)}
    in_specs = [s for _, s in ins] + [s for _, s in cts] + [s for _, (_, s) in adds] + [ANY] * n_into
    operands = [a for a, _ in ins] + [a for a, _ in cts] + [a for _, (a, _) in adds] + [a for _, a in intos]
    return hosted_call(kern, name=name, grid=grid, in_specs=in_specs, out_specs=out_specs, out_shape=out_shape, operands=operands,
                       aliases=aliases, plans=plans)


def rspec(tl, w, cb=0):
    return pl.BlockSpec((tl, w), lambda i: (i, cb))


def cspec(shape):
    return pl.BlockSpec(shape, lambda i: (0,) * len(shape))


def row_fwd(name, fn, rows, tl, row_ins, consts, outs):
    ins = [(a, rspec(tl, w, cb)) for a, w, cb in row_ins] + [(a, cspec(a.shape)) for a in consts]
    return stage(name, fn, (rows // tl,), ins, [(SDS((rows, w), dt), rspec(tl, w)) for w, dt in outs])


def row_bwd(name, fn, rows, tl, row_ins, consts, cts, row_diff, const_diff, plans=()):
    ins = [(a, rspec(tl, w, cb)) for a, w, cb in row_ins] + [(a, cspec(a.shape)) for a in consts]
    diffs = []
    for (a, w, cb), d in zip(row_ins, row_diff):
        if not d:
            diffs.append(None)
            continue
        d = d if isinstance(d, dict) else {}
        opts = {}
        if "add" in d:
            opts["add"] = (d["add"][0], rspec(tl, d["add"][1], d["add"][2]))
        if d.get("into") is not None:
            opts["into"] = d["into"]
        diffs.append(("row", SDS((rows, d.get("cols", w)), d.get("dtype", F32)), rspec(tl, w, d.get("cb", 0)), opts))
    diffs += [("acc", (0,)) if d else None for d in const_diff]
    return stage_bwd(name, fn, (rows // tl,), ins, [(a, rspec(tl, w, cb)) for a, w, cb in cts], diffs, plans=plans)


MATMUL_VMEM = 36 * 2**20
ADAMW_VMEM = 28 * 2**20


class Sharded:
    def __init__(self, arr, kind, roff, rows):
        self.arr, self.kind, self.roff, self.rows, self.n = arr, kind, roff, rows, arr.shape[2]
        self.shape = (rows, 4 * self.n) if kind == "col" else (4 * rows, self.n)

    def fits(self, t0, t1):
        return self.roff % t0 == 0 and self.rows % t0 == 0 and self.n % t1 == 0

    def spec(self, t0, t1, bidx):
        assert self.fits(t0, t1), (self.kind, self.roff, self.rows, self.n, t0, t1)
        r0 = self.roff // t0
        if self.kind == "col":
            per = self.n // t1
            return pl.BlockSpec((None, t0, t1), lambda *g: (bidx(*g)[1] // per, r0 + bidx(*g)[0], bidx(*g)[1] % per))
        per = self.rows // t0
        return pl.BlockSpec((None, t0, t1), lambda *g: (bidx(*g)[0] // per, r0 + bidx(*g)[0] % per, bidx(*g)[1]))


def matmul(name, a, b, mode, out_dtype=BF16, add=None, out=None, into=None, plans=()):
    if mode == "tn":
        k_dim, m = a.shape
    else:
        m, k_dim = a.shape
    n = b.shape[0] if mode == "nt" else b.shape[1]
    b_fit = b.fits if isinstance(b, Sharded) else (lambda t0, t1: True)
    o_fit = out.fits if out is not None else (lambda t0, t1: True)
    a_bytes, b_bytes = jnp.dtype(a.dtype).itemsize, jnp.dtype(b.arr.dtype if isinstance(b, Sharded) else b.dtype).itemsize
    o_bytes = jnp.dtype(out_dtype if out is None else out.arr.dtype).itemsize

    def vmem(tm, tn, tk):
        return 2 * (tm * tk * a_bytes + tk * tn * b_bytes + tm * tn * (o_bytes + (4 if add is not None else 0))) + 4 * tm * tn * (1 + (tk < k_dim))

    tiles = [(tm, tn, tk) for tm in (2048, 1024, 512, 256, 128) for tn in (1024, 768, 512, 384, 256, 128)
             for tk in sorted({k_dim, 1024, 768, 512, 384, 256, 128})
             if m % tm == 0 and n % tn == 0 and k_dim % tk == 0 and (b_fit(tn, tk) if mode == "nt" else b_fit(tk, tn)) and o_fit(tm, tn)
             and vmem(tm, tn, tk) <= MATMUL_VMEM]
    tm, tn, tk = max(tiles, key=lambda t: (t[2] == k_dim, t[0] * t[1] * t[2], t[0] * t[1]))
    nk = k_dim // tk
    a_spec = pl.BlockSpec((tk, tm), lambda i, j, k: (k, i)) if mode == "tn" else pl.BlockSpec((tm, tk), lambda i, j, k: (i, k))
    if isinstance(b, Sharded):
        b_spec = b.spec(tn, tk, lambda i, j, k: (j, k)) if mode == "nt" else b.spec(tk, tn, lambda i, j, k: (k, j))
        b = b.arr
    else:
        b_spec = pl.BlockSpec((tn, tk), lambda i, j, k: (j, k)) if mode == "nt" else pl.BlockSpec((tk, tn), lambda i, j, k: (k, j))
    o_spec = pl.BlockSpec((tm, tn), lambda i, j, k: (i, j))
    out_spec, out_shape = (o_spec, SDS((m, n), out_dtype)) if out is None else (out.spec(tm, tn, lambda i, j, k: (i, j)), out.arr)
    ca, cb = {"nn": (1, 0), "nt": (1, 1), "tn": (0, 0)}[mode]
    n_in = 2 + (add is not None)

    def finish(refs, o_ref, r):
        if add is not None:
            r = r + refs[2][...]
        o_ref[...] = r.astype(o_ref.dtype)

    def kern_whole(*refs):
        finish(refs, refs[-1], _dg(refs[0][...], refs[1][...], ca, cb))

    def kern_cut(*refs):
        o_ref, acc = refs[-2], refs[-1]
        k = pl.program_id(2)

        @pl.when(k == 0)
        def _():
            acc[...] = jnp.zeros_like(acc)

        acc[...] += _dg(refs[0][...], refs[1][...], ca, cb)

        @pl.when(k == nk - 1)
        def _():
            finish(refs, o_ref, acc[...])

    ins, specs = [a, b], [a_spec, b_spec]
    if add is not None:
        ins.append(add)
        specs.append(o_spec)
    if into is not None:
        ins.append(into)
        specs.append(ANY)
    return hosted_call(kern_whole if nk == 1 else kern_cut, name=name, grid=(m // tm, n // tn, nk), in_specs=specs, out_specs=[out_spec],
                       out_shape=[out_shape], operands=ins, scratch_shapes=[] if nk == 1 else [pltpu.VMEM((tm, tn), F32)],
                       aliases={} if into is None else {n_in: 0}, plans=plans)[0]


def _cmul(ar, ai, br, bi):
    return ar * br - ai * bi, ar * bi + ai * br


def _sub_shift(x, down):
    row = lax.broadcasted_iota(jnp.int32, x.shape, 0)
    if down:
        return jnp.where(row == 0, 0.0, pltpu.roll(x, 1, 0))
    return jnp.where(row == SEG - 1, 0.0, pltpu.roll(x, SEG - 1, 0))


def _pow_seg_len(ar, ai):
    for _ in range(int(math.log2(SEG_LEN))):
        ar, ai = _cmul(ar, ai, ar, ai)
    return ar, ai


def _scan_in_place(sr, si, a_re, a_im):
    lanes = sr.shape[1]
    ar = jnp.broadcast_to(a_re, (SEG, lanes))
    ai = jnp.broadcast_to(a_im, (SEG, lanes))
    zero = jnp.zeros((SEG, lanes), F32)

    def local(i, carry):
        rows = pl.ds(pl.multiple_of(i * SEG, SEG), SEG)
        mr, mi = _cmul(ar, ai, carry[0], carry[1])
        nr, ni = mr + sr[rows, :], mi + si[rows, :]
        sr[rows, :] = nr
        si[rows, :] = ni
        return nr, ni

    fr, fi = lax.fori_loop(0, SEG_LEN, local, (zero, zero))
    pr, pi = _pow_seg_len(ar, ai)
    ir, ii = zero, zero
    for _ in range(SEG - 1):
        mr, mi = _cmul(pr, pi, ir, ii)
        ir, ii = _sub_shift(mr + fr, True), _sub_shift(mi + fi, True)

    def carry_in(i, pw):
        rows = pl.ds(pl.multiple_of(i * SEG, SEG), SEG)
        cr, ci = _cmul(pw[0], pw[1], ir, ii)
        sr[rows, :] += cr
        si[rows, :] += ci
        return _cmul(pw[0], pw[1], ar, ai)

    lax.fori_loop(0, SEG_LEN, carry_in, (ar, ai))


S5_LANES = 8 * SP
S5_BLOCKS = SN // S5_LANES


def _s5_specs():
    u_spec = pl.BlockSpec((L, 8 * SC), lambda j: (0, j))
    s_spec = pl.BlockSpec((L, S5_LANES), lambda j: (0, j))
    wb_spec = pl.BlockSpec((S5_LANES, 8 * SC), lambda j: (j, 0))
    wc_spec = pl.BlockSpec((8 * SC, S5_LANES), lambda j: (j, 0))
    a_spec = pl.BlockSpec((1, S5_LANES), lambda j: (0, j))
    d_spec = pl.BlockSpec((1, 8 * SC), lambda j: (0, j))
    return u_spec, s_spec, wb_spec, wc_spec, a_spec, d_spec


def s5_forward(proj, wb_re, wb_im, wc_re, wc_im, a_re, a_im, d, plans=()):
    def kern(u_ref, wbr, wbi, wcr, wci, ar, ai, d_ref, sr_out, si_out, g_ref, sr, si):
        u = _wide(u_ref[...])
        sr[...], si[...] = fn_s5_bu(u, wbr[...], wbi[...])
        _scan_in_place(sr, si, ar[...], ai[...])
        g_ref[...] = fn_s5_out(sr[...], si[...], u, d_ref[...], wcr[...], wci[...])[0].astype(g_ref.dtype)
        sr_out[...] = sr[...].astype(sr_out.dtype)
        si_out[...] = si[...].astype(si_out.dtype)

    u_spec, s_spec, wb_spec, wc_spec, a_spec, d_spec = _s5_specs()
    return hosted_call(kern, name="s5_forward", grid=(S5_BLOCKS,), in_specs=[u_spec, wb_spec, wb_spec, wc_spec, wc_spec, a_spec, a_spec, d_spec],
                       out_specs=[s_spec, s_spec, u_spec], out_shape=[SDS((L, SN), BF16)] * 2 + [SDS((L, PW), BF16)],
                       operands=[proj, wb_re, wb_im, wc_re, wc_im, a_re, a_im, d], scratch_shapes=[pltpu.VMEM((L, S5_LANES), F32)] * 2,
                       plans=plans)


def _adjoint_scan_in_place(lr, li, sr, si, a_re, a_im):
    lanes = lr.shape[1]
    ar = jnp.broadcast_to(a_re, (SEG, lanes))
    ai = -jnp.broadcast_to(a_im, (SEG, lanes))
    zero = jnp.zeros((SEG, lanes), F32)

    def local(k, carry):
        i = SEG_LEN - 1 - k
        rows = pl.ds(pl.multiple_of(i * SEG, SEG), SEG)
        mr, mi = _cmul(ar, ai, carry[0], carry[1])
        nr, ni = mr + lr[rows, :], mi + li[rows, :]
        lr[rows, :] = nr
        li[rows, :] = ni
        return nr, ni

    fr, fi = lax.fori_loop(0, SEG_LEN, local, (zero, zero))
    pr, pi = _pow_seg_len(ar, ai)
    ir, ii = zero, zero
    for _ in range(SEG - 1):
        mr, mi = _cmul(pr, pi, ir, ii)
        ir, ii = _sub_shift(mr + fr, False), _sub_shift(mi + fi, False)

    def fix(rows, pw):
        cr, ci = _cmul(pw[0], pw[1], ir, ii)
        tr, ti = lr[rows, :] + cr, li[rows, :] + ci
        lr[rows, :] = tr
        li[rows, :] = ti
        return tr, ti

    def grad_a(tr, ti, spr, spi, acc):
        return acc[0] + tr * spr + ti * spi, acc[1] + ti * spr - tr * spi

    def carry_in(k, c):
        i = SEG_LEN - 1 - k
        rows = pl.ds(pl.multiple_of(i * SEG, SEG), SEG)
        prev = pl.ds(pl.multiple_of((i - 1) * SEG, SEG), SEG)
        tr, ti = fix(rows, (c[0], c[1]))
        acc = grad_a(tr, ti, sr[prev, :], si[prev, :], (c[2], c[3]))
        nr, ni = _cmul(c[0], c[1], ar, ai)
        return nr, ni, acc[0], acc[1]

    pwr, pwi, accr, acci = lax.fori_loop(0, SEG_LEN - 1, carry_in, (ar, ai, zero, zero))
    tr, ti = fix(pl.ds(0, SEG), (pwr, pwi))
    last = pl.ds((SEG_LEN - 1) * SEG, SEG)
    accr, acci = grad_a(tr, ti, _sub_shift(sr[last, :], True), _sub_shift(si[last, :], True), (accr, acci))
    return jnp.sum(accr, axis=0, keepdims=True), jnp.sum(acci, axis=0, keepdims=True)


S5_BWD_VMEM = 58 * 2**20


def s5_backward(dg, proj, s_re, s_im, wb_re, wb_im, wc_re, wc_im, a_re, a_im, d, dproj, plans=()):
    def kern(dg_ref, u_ref, sr_in, si_in, wbr, wbi, wcr, wci, ar, ai, d_ref, _, du_ref, dd_ref, dwcr, dwci, dwbr, dwbi, dar, dai, lr, li, sr, si):
        u = _wide(u_ref[...])
        sr[...], si[...] = _wide(sr_in[...]), _wide(si_in[...])
        _, vjp_out = jax.vjp(fn_s5_out, sr[...], si[...], u, d_ref[...], wcr[...], wci[...])
        lr[...], li[...], du_out, dd_ref[...], dwcr[...], dwci[...] = vjp_out((_wide(dg_ref[...]),))
        dar[...], dai[...] = _adjoint_scan_in_place(lr, li, sr, si, ar[...], ai[...])
        _, vjp_in = jax.vjp(fn_s5_bu, u, wbr[...], wbi[...])
        du_in, dwbr[...], dwbi[...] = vjp_in((lr[...], li[...]))
        du_ref[...] = (du_out + du_in).astype(du_ref.dtype)

    u_spec, s_spec, wb_spec, wc_spec, a_spec, d_spec = _s5_specs()
    outs = [(SDS(dproj.shape, dproj.dtype), u_spec), (SDS(d.shape, F32), d_spec), (SDS(wc_re.shape, F32), wc_spec), (SDS(wc_im.shape, F32), wc_spec),
            (SDS(wb_re.shape, F32), wb_spec), (SDS(wb_im.shape, F32), wb_spec), (SDS(a_re.shape, F32), a_spec), (SDS(a_im.shape, F32), a_spec)]
    return hosted_call(kern, name="s5_backward", grid=(S5_BLOCKS,),
                       in_specs=[u_spec, u_spec, s_spec, s_spec, wb_spec, wb_spec, wc_spec, wc_spec, a_spec, a_spec, d_spec, ANY],
                       out_specs=[sp for _, sp in outs], out_shape=[sd for sd, _ in outs], aliases={11: 0},
                       operands=[dg, proj, s_re, s_im, wb_re, wb_im, wc_re, wc_im, a_re, a_im, d, dproj],
                       scratch_shapes=[pltpu.VMEM((L, S5_LANES), F32)] * 4,
                       cparams=pltpu.CompilerParams(vmem_limit_bytes=S5_BWD_VMEM), plans=plans)


def fn_rms(x, g):
    return (rms(x, g),)


def fn_s5_disc(lre, lim, ls):
    step = jnp.exp(ls)
    e = jnp.exp(lre * step)
    a_re, a_im = e * jnp.cos(lim * step), e * jnp.sin(lim * step)
    den = lre * lre + lim * lim
    nr, ni = a_re - 1.0, a_im
    return a_re, a_im, (nr * lre + ni * lim) / den, (ni * lre - nr * lim) / den


def _group_mask(rows, cols, row_div, col_div):
    r = lax.broadcasted_iota(jnp.int32, (rows, cols), 0) // row_div % 8
    c = lax.broadcasted_iota(jnp.int32, (rows, cols), 1) // col_div
    return r == c


def _spread(x, mask):
    w = x.shape[1]
    copy = (lax.broadcasted_iota(jnp.int32, (w, 8 * w), 1) % w == lax.broadcasted_iota(jnp.int32, (w, 8 * w), 0)).astype(F32)
    return jnp.where(mask, jnp.dot(x, copy, precision=lax.Precision.HIGHEST, preferred_element_type=F32), 0.0)


def fn_s5_bmat(b_re, b_im, coef_re, coef_im):
    mask = _group_mask(b_re.shape[0], 8 * SC, SP, SC)
    return _spread(coef_re * b_re - coef_im * b_im, mask), _spread(coef_re * b_im + coef_im * b_re, mask)


def fn_s5_cmat(c_re, c_im):
    mask = _group_mask(c_re.shape[0], 8 * SP, SC, SP)
    return _spread(c_re, mask), _spread(c_im, mask)


def fn_s5_bu(u, wb_re, wb_im):
    return mm_nt(u, wb_re), mm_nt(u, wb_im)


def fn_s5_out(sr, si, u, d, wc_re, wc_im):
    y = mm_nt(sr, wc_re) - mm_nt(si, wc_im) + d * u
    return (jax.nn.gelu(y),)


def fn_merge_glu(z, mo, gate):
    yg = z[:, :PW] * jax.nn.sigmoid(z[:, PW:])
    return (jnp.concatenate([yg, mo], axis=1) * silu(gate),)


def fn_merge(prim, mo, gate):
    return (jnp.concatenate([prim, mo], axis=1) * silu(gate),)


def fn_mem_k(kv, g):
    return (jnp.concatenate([rms(kv[:, h * XHD:(h + 1) * XHD], g) for h in range(XH)], axis=1),)


def fn_mem_attn(xq, kn, v, g):
    outs = []
    for h in range(XH):
        sl = slice(h * XHD, (h + 1) * XHD)
        p = softmax_rows(mm_nt(rms(xq[:, sl], g), kn[:, sl]) * (XHD ** -0.5))
        outs.append(mm_nn(p, v[:, sl]))
    return (jnp.concatenate(outs, axis=1),)


def _half_rms(x, g):
    lo = lax.broadcasted_iota(jnp.int32, x.shape, 1) < ROPE
    x2 = x * x
    s_lo = jnp.sum(jnp.where(lo, x2, 0.0), axis=1, keepdims=True)
    s_hi = jnp.sum(jnp.where(lo, 0.0, x2), axis=1, keepdims=True)
    return x * lax.rsqrt(jnp.where(lo, s_lo, s_hi) / ROPE + EPS) * g


def _rope(x, cos2, sin_signed):
    first = lax.broadcasted_iota(jnp.int32, x.shape, 1) % ROPE < ROPE // 2
    return x * cos2 + jnp.where(first, lane_roll(x, 128 - ROPE // 2), lane_roll(x, ROPE // 2)) * sin_signed


def fn_mla_prep(q, kv, kr, cos2, sin_signed, qnn, knn, qrn, krn):
    lo = lax.broadcasted_iota(jnp.int32, kr.shape, 1) < ROPE
    kr_pad = jnp.where(lo, _rope(_half_rms(kr, krn), cos2, sin_signed), 0.0)
    qf, kf, vs = [], [], []
    for m in range(MH // 2):
        pair = _rope(_half_rms(q[:, MH * NOPE + 128 * m:MH * NOPE + 128 * (m + 1)], qrn), cos2, sin_signed)
        for h, rope_h in ((2 * m, pair), (2 * m + 1, lane_roll(pair, ROPE))):
            qf.append(jnp.concatenate([rms(q[:, NOPE * h:NOPE * (h + 1)], qnn), jnp.where(lo, rope_h, 0.0)], axis=1))
    for h in range(MH):
        kf.append(jnp.concatenate([rms(kv[:, 256 * h:256 * h + NOPE], knn), kr_pad], axis=1))
        vs.append(kv[:, 256 * h + NOPE:256 * (h + 1)])
    return jnp.stack(qf), jnp.stack(kf), jnp.stack(vs)


ATT_TQ = 512


def _attn_tile(q, kf, v):
    tq = q.shape[0]
    scale = (NOPE + ROPE) ** -0.5
    own = mm_nt(q, kf[-tq:]) * scale
    own = jnp.where(lax.broadcasted_iota(jnp.int32, own.shape, 1) <= lax.broadcasted_iota(jnp.int32, own.shape, 0), own, jnp.finfo(F32).min)
    s = own if kf.shape[0] == tq else jnp.concatenate([mm_nt(q, kf[:-tq]) * scale, own], axis=1)
    return mm_nn(softmax_rows(s), v)


def _attn_specs():
    q_spec = pl.BlockSpec((None, ATT_TQ, 256), lambda h, i: (h, i, 0))
    k_spec = pl.BlockSpec((None, L, 256), lambda h, i: (h, 0, 0))
    v_spec = pl.BlockSpec((None, L, 128), lambda h, i: (h, 0, 0))
    o_spec = pl.BlockSpec((ATT_TQ, 128), lambda h, i: (i, h))
    return q_spec, k_spec, v_spec, o_spec


def causal_attn(qf, kf, vh):
    n_tiles = L // ATT_TQ

    def kern(q_ref, k_ref, v_ref, o_ref):
        i = pl.program_id(1)
        for t in range(n_tiles):
            @pl.when(i == t)
            def _(t=t):
                keys = (t + 1) * ATT_TQ
                o_ref[...] = _attn_tile(q_ref[...], k_ref[:keys, :], v_ref[:keys, :]).astype(o_ref.dtype)

    q_spec, k_spec, v_spec, o_spec = _attn_specs()
    return pl.pallas_call(kern, grid=(MH, n_tiles), in_specs=[q_spec, k_spec, v_spec], out_specs=o_spec,
                          out_shape=SDS((L, MH * VD), BF16), name="l1_attn", compiler_params=_cparams())(qf, kf, vh)


def causal_attn_bwd(qf, kf, vh, dout):
    n_tiles = L // ATT_TQ

    def kern(q_ref, k_ref, v_ref, do_ref, dq_ref, dk_ref, dv_ref):
        i = pl.program_id(1)

        @pl.when(i == 0)
        def _():
            dk_ref[...] = jnp.zeros_like(dk_ref)
            dv_ref[...] = jnp.zeros_like(dv_ref)

        for t in range(n_tiles):
            @pl.when(i == t)
            def _(t=t):
                keys = (t + 1) * ATT_TQ
                _, vjp = jax.vjp(_attn_tile, q_ref[...].astype(F32), k_ref[:keys, :].astype(F32), v_ref[:keys, :].astype(F32))
                dq, dk, dv = vjp(do_ref[...])
                dq_ref[...] = dq
                dk_ref[:keys, :] += dk
                dv_ref[:keys, :] += dv

    q_spec, k_spec, v_spec, o_spec = _attn_specs()
    return pl.pallas_call(kern, grid=(MH, n_tiles), in_specs=[q_spec, k_spec, v_spec, o_spec], out_specs=[q_spec, k_spec, v_spec],
                          out_shape=[SDS(qf.shape, F32), SDS(kf.shape, F32), SDS(vh.shape, F32)], name="l1_attn_bwd",
                          compiler_params=_cparams())(qf, kf, vh, dout)


def loss_and_grad(y, target, tl=256):
    def kern(y_ref, t_ref, dy_ref, loss_ref):
        d = y_ref[...] - t_ref[...]
        dy_ref[...] = d / D

        @pl.when(pl.program_id(0) == 0)
        def _():
            loss_ref[...] = jnp.zeros_like(loss_ref)

        loss_ref[...] += 0.5 * jnp.sum(jnp.sum(d * d, axis=1, keepdims=True), axis=0, keepdims=True) / D

    return pl.pallas_call(kern, grid=(L // tl,), in_specs=[rspec(tl, D), rspec(tl, D)], out_specs=[rspec(tl, D), cspec((1, 1))],
                          out_shape=[SDS((L, D), F32), SDS((1, 1), F32)], name="loss", compiler_params=_cparams())(y, target)


def adamw(name, w, g, m, v):
    rows, cols = w.shape
    block_row_bytes = 7 * 2 * 4 * max(cols, 128)
    tr = _row_tile(rows, min(2048, ADAMW_VMEM // block_row_bytes // 8 * 8), 8)

    def kern(w_ref, g_ref, m_ref, v_ref, d_ref, nm_ref, nv_ref):
        gg = g_ref[...]
        nm = ADAM_B1 * m_ref[...] + (1.0 - ADAM_B1) * gg
        nv = ADAM_B2 * v_ref[...] + (1.0 - ADAM_B2) * jnp.square(gg)
        m_hat = nm / (1.0 - ADAM_B1 ** ADAM_STEP)
        v_hat = nv / (1.0 - ADAM_B2 ** ADAM_STEP)
        d_ref[...] = -ADAM_LR * (m_hat / (jnp.sqrt(v_hat) + ADAM_EPS) + ADAM_WD * w_ref[...])
        nm_ref[...] = nm
        nv_ref[...] = nv

    spec = rspec(tr, cols)
    return pl.pallas_call(kern, grid=(rows // tr,), in_specs=[spec] * 4, out_specs=[spec] * 3,
                          out_shape=[SDS((rows, cols), F32)] * 3, name=name, compiler_params=_cparams())(w, g, m, v)


def _row_tile(rows, cap=512, unit=16):
    return max(t for t in range(unit, cap + 1, unit) if rows % t == 0)


def _place():
    x, y, c = lax.axis_index("x"), lax.axis_index("y"), lax.axis_index("c")
    return x, y, c, [(1 - x, y), (x, 1 - y), (1 - x, 1 - y)]


def _row_chunks(rows, n, dtype):
    unit = 32 // jnp.dtype(dtype).itemsize
    base, extra = divmod(rows // unit, n)
    out, start = [], 0
    for k in range(n):
        size = (base + (k < extra)) * unit
        if size:
            out.append((start, size))
            start += size
    assert start == rows, (rows, unit)
    return out


PIECE_BYTES = 1 << 20


def _pieces(shapes_dtypes, rows_of):
    out = []
    for b, (shape, dtype) in enumerate(shapes_dtypes):
        rows = rows_of(shape)
        n = max(1, min(4, rows * shape[-1] * jnp.dtype(dtype).itemsize // PIECE_BYTES))
        out += [(b, st, sz) for st, sz in _row_chunks(rows, n, dtype)]
    return out


def all_gather_chips(name, shards):
    nb = len(shards)
    pieces = _pieces([(s.shape, s.dtype) for s in shards], lambda shape: shape[0] // 2)
    n = len(pieces)

    def body(*refs):
        x_refs, out_refs, send_sems, recv_sems = refs[:nb], refs[nb:2 * nb], refs[2 * nb], refs[2 * nb + 1]
        x, y, c, chips = _place()
        sibling = (x, y, 1 - c)
        mine = 2 * x + y

        def copy(sem, chip, cc, k, to, from_input=False):
            b, st, sz = pieces[k]
            rows_k = pl.ds(cc * (x_refs[b].shape[0] // 2) + st, sz)
            dst = out_refs[b].at[chip, rows_k, :]
            return pltpu.make_async_remote_copy(src_ref=x_refs[b].at[rows_k, :] if from_input else dst, dst_ref=dst,
                                                send_sem=send_sems.at[sem], recv_sem=recv_sems.at[sem], device_id=to, device_id_type=MESH_ID)

        order = [(k, j, 2 * cx + cy, (cx, cy, c)) for k in range(n) for j, (cx, cy) in enumerate(chips)]
        first = [copy(j * n + k, mine, c, k, to, from_input=True) for k, j, _, to in order]
        for cp in first:
            cp.start()
        passed = []
        for k, j, chip, _ in order:
            copy(j * n + k, chip, c, k, sibling).wait_recv()
            passed.append(copy((3 + j) * n + k, chip, c, k, sibling))
            passed[-1].start()
        for k, j, chip, _ in order:
            copy((3 + j) * n + k, chip, 1 - c, k, sibling).wait_recv()
        for cp in first + passed:
            cp.wait_send()

    return pl.pallas_call(body, in_specs=[ANY] * nb, out_specs=[ANY] * nb, out_shape=[SDS((4,) + s.shape, s.dtype) for s in shards],
                          scratch_shapes=_dma_sems(6 * n), name=name)(*shards)


def plan_gather_ici(shards):
    pieces = _pieces([(s.shape, s.dtype) for s in shards], lambda shape: shape[0] // 2)
    n = len(pieces)

    def copies(x_refs, out_refs, send_sems, recv_sems):
        x, y, c, chips = _place()
        mine = 2 * x + y

        def copy(j, k, chip, to, from_input):
            b, st, sz = pieces[k]
            rows_k = pl.ds(c * (x_refs[b].shape[0] // 2) + st, sz)
            dst = out_refs[b].at[chip, rows_k, :]
            return pltpu.make_async_remote_copy(src_ref=x_refs[b].at[rows_k, :] if from_input else dst, dst_ref=dst, send_sem=send_sems.at[j * n + k],
                                                recv_sem=recv_sems.at[j * n + k], device_id=to, device_id_type=MESH_ID)

        order = [(k, j, 2 * cx + cy, (cx, cy, c)) for k in range(n) for j, (cx, cy) in enumerate(chips)]
        return [copy(j, k, mine, to, True) for k, j, _, to in order], [copy(j, k, chip, to, False) for k, j, chip, to in order]

    return Plan(shards, [SDS((4,) + s.shape, s.dtype) for s in shards], {}, 3 * n, copies)


def plan_gather_pass(gathered):
    pieces = _pieces([(g.shape[1:], g.dtype) for g in gathered], lambda shape: shape[0] // 2)
    n = len(pieces)

    def copies(_, out_refs, send_sems, recv_sems):
        x, y, c, chips = _place()

        def copy(j, k, chip, cc):
            b, st, sz = pieces[k]
            rows_k = out_refs[b].at[chip, pl.ds(cc * (out_refs[b].shape[1] // 2) + st, sz), :]
            return pltpu.make_async_remote_copy(src_ref=rows_k, dst_ref=rows_k, send_sem=send_sems.at[j * n + k], recv_sem=recv_sems.at[j * n + k],
                                                device_id=(x, y, 1 - c), device_id_type=MESH_ID)

        order = [(k, j, 2 * cx + cy) for k in range(n) for j, (cx, cy) in enumerate(chips)]
        return [copy(j, k, chip, c) for k, j, chip in order], [copy(j, k, chip, 1 - c) for k, j, chip in order]

    return Plan(gathered, [SDS(g.shape, g.dtype) for g in gathered], {i: i for i in range(len(gathered))}, 3 * n, copies)


def plan_pair_exchange(gs):
    pieces = _pieces([(g.shape, g.dtype) for g in gs], lambda shape: shape[1] // 2)

    def copies(g_refs, got_refs, send_sems, recv_sems):
        x, y, c, _ = _place()
        swaps = [pltpu.make_async_remote_copy(src_ref=g_refs[b].at[:, pl.ds((1 - c) * (g_refs[b].shape[1] // 2) + st, sz), :],
                                              dst_ref=got_refs[b].at[:, pl.ds(st, sz), :], send_sem=send_sems.at[k], recv_sem=recv_sems.at[k],
                                              device_id=(x, y, 1 - c), device_id_type=MESH_ID)
                 for k, (b, st, sz) in enumerate(pieces)]
        return swaps, swaps

    return Plan(gs, [SDS((g.shape[0], g.shape[1] // 2, g.shape[2]), g.dtype) for g in gs], {}, len(pieces), copies)


def plan_chip_scatter(ps):
    pieces = _pieces([(p.shape, p.dtype) for p in ps], lambda shape: shape[1])
    n = len(pieces)

    def copies(p_refs, q_refs, send_sems, recv_sems):
        x, y, c, chips = _place()
        mine = 2 * x + y

        def copy(j, k, src_slot, dst_slot, to):
            b, st, sz = pieces[k]
            return pltpu.make_async_remote_copy(src_ref=p_refs[b].at[src_slot, pl.ds(st, sz), :], dst_ref=q_refs[b].at[dst_slot, pl.ds(st, sz), :],
                                                send_sem=send_sems.at[j * n + k], recv_sem=recv_sems.at[j * n + k], device_id=to,
                                                device_id_type=MESH_ID)

        order = [(k, j, 2 * cx + cy, (cx, cy, c)) for k in range(n) for j, (cx, cy) in enumerate(chips)]
        return [copy(j, k, chip, mine, to) for k, j, chip, to in order], [copy(j, k, mine, chip, to) for k, j, chip, to in order]

    return Plan(ps, [SDS(p.shape, p.dtype) for p in ps], {}, 3 * n, copies)


def plan_pair_join(bufs):
    pieces = _pieces([(b.shape, b.dtype) for b in bufs], lambda shape: shape[0] // 2)

    def copies(_, out_refs, send_sems, recv_sems):
        x, y, c, _ = _place()

        def copy(k, cc):
            b, st, sz = pieces[k]
            rows_k = out_refs[b].at[pl.ds(cc * (out_refs[b].shape[0] // 2) + st, sz), :]
            return pltpu.make_async_remote_copy(src_ref=rows_k, dst_ref=rows_k, send_sem=send_sems.at[k], recv_sem=recv_sems.at[k],
                                                device_id=(x, y, 1 - c), device_id_type=MESH_ID)

        return [copy(k, c) for k in range(len(pieces))], [copy(k, 1 - c) for k in range(len(pieces))]

    return Plan(bufs, [SDS(b.shape, b.dtype) for b in bufs], {i: i for i in range(len(bufs))}, len(pieces), copies)


def run_plan(name, plan):
    hosted_call(lambda: None, name=name, grid=(1,), in_specs=[], out_specs=[], out_shape=[], operands=[], plans=[plan])
    return plan.results


HBM = pl.BlockSpec(memory_space=pltpu.HBM)
SEMS = pl.BlockSpec(memory_space=pltpu.SEMAPHORE)
SPLIT_PARAMS = dict(has_side_effects=pltpu.SideEffectType.DATAFLOW_SIDE_EFFECTING)


def _plan_buffers(plan):
    in_place = {o: i for i, o in plan.aliases.items()}
    bufs = [pltpu.with_memory_space_constraint(a, pltpu.HBM) for a in plan.operands]
    where = []
    for o, sd in enumerate(plan.out_shape):
        if o in in_place:
            where.append(in_place[o])
        else:
            where.append(len(bufs))
            bufs.append(pltpu.with_memory_space_constraint(lax.empty(sd.shape, sd.dtype), pltpu.HBM))
    return bufs, where


def split_start(name, plans):
    layout = [_plan_buffers(p) for p in plans]
    counts = [len(b) for b, _ in layout]
    n_buf = sum(counts)

    def body(*refs):
        sems, token = refs[n_buf:n_buf + 2 * len(plans)], refs[-1]
        pos = 0
        for k, (p, (_, where)) in enumerate(zip(plans, layout)):
            mine = refs[pos:pos + counts[k]]
            pos += counts[k]
            sends, _ = p.copies(mine[:len(p.operands)], [mine[w] for w in where], sems[2 * k], sems[2 * k + 1])
            for cp in sends:
                cp.start()
        token[...] = jnp.zeros_like(token)

    bufs = [b for bs, _ in layout for b in bs]
    res = pl.pallas_call(
        body, name=name, in_specs=[HBM] * n_buf,
        out_specs=[SEMS] * (2 * len(plans)) + [HBM] * n_buf + [pl.BlockSpec(memory_space=pltpu.VMEM)],
        out_shape=[pltpu.SemaphoreType.DMA((p.n_sems,)) for p in plans for _ in range(2)] + [pltpu.HBM(b.shape, b.dtype) for b in bufs]
        + [SDS((8, 128), F32)],
        input_output_aliases={i: 2 * len(plans) + i for i in range(n_buf)}, compiler_params=pltpu.CompilerParams(**SPLIT_PARAMS))(*bufs)
    pos = 2 * len(plans)
    for k, p in enumerate(plans):
        p.in_flight = (res[2 * k], res[2 * k + 1], list(res[pos:pos + counts[k]]), layout[k][1])
        pos += counts[k]
    return res[-1]


def split_wait(name, plan, after):
    send_sems, recv_sems, bufs, where = plan.in_flight
    n_buf = len(bufs)

    def body(*refs):
        mine = refs[:n_buf]
        sends, recvs = plan.copies(mine[:len(plan.operands)], [mine[w] for w in where], refs[n_buf], refs[n_buf + 1])
        for cp in recvs:
            cp.wait_recv()
        for cp in sends:
            cp.wait_send()

    res = pl.pallas_call(body, name=name, in_specs=[HBM] * n_buf + [SEMS, SEMS, ANY], out_specs=[HBM] * n_buf,
                         out_shape=[pltpu.HBM(b.shape, b.dtype) for b in bufs], input_output_aliases={i: i for i in range(n_buf)},
                         compiler_params=pltpu.CompilerParams(**SPLIT_PARAMS))(*bufs, send_sems, recv_sems, after)
    plan.results = [res[w] for w in where]
    return plan.results


def pair_add(name, g, got, place):
    slots, rows, cols = g.shape
    half = rows // 2
    tr = _row_tile(half)
    nb = half // tr

    def kern(_, g_ref, t_ref, o_ref):
        o_ref[...] = (g_ref[...].astype(F32) + t_ref[...].astype(F32)).astype(o_ref.dtype)

    blk = pl.BlockSpec((None, tr, cols), lambda s, i, p: (s, i, 0))
    grid_spec = pltpu.PrefetchScalarGridSpec(
        num_scalar_prefetch=1, grid=(slots, nb),
        in_specs=[pl.BlockSpec((None, tr, cols), lambda s, i, p: (s, p[1] * nb + i, 0)), blk], out_specs=blk)
    return pl.pallas_call(kern, grid_spec=grid_spec, out_shape=SDS((slots, half, cols), g.dtype), name=name,
                          compiler_params=_cparams())(place, g, got)


def chip_add(name, p, q, place):
    slots, half, cols = p.shape
    tr = _row_tile(half)
    nb = half // tr

    def kern(_, p_ref, q1, q2, q3, o_ref):
        o_ref[...] = p_ref[...].astype(F32) + q1[...].astype(F32) + q2[...].astype(F32) + q3[...].astype(F32)

    def slot(k):
        return pl.BlockSpec((None, tr, cols), lambda i, pr: ((pr[0] + k) % slots, i, 0))

    grid_spec = pltpu.PrefetchScalarGridSpec(
        num_scalar_prefetch=1, grid=(nb,), in_specs=[slot(0), slot(1), slot(2), slot(3)],
        out_specs=pl.BlockSpec((tr, cols), lambda i, pr: (pr[1] * nb + i, 0)))
    return pl.pallas_call(kern, grid_spec=grid_spec, out_shape=SDS((2 * half, cols), F32), name=name,
                          compiler_params=_cparams())(place, p, q, q, q)


def pair_adds(tag, gs, gots, place):
    return [pair_add(f"{tag}_pair_add_{i}", g, got, place) for i, (g, got) in enumerate(zip(gs, gots))]


def chip_adds(tag, pairs, qs, place):
    return [chip_add(f"{tag}_chip_add_{i}", p, q, place) for i, (p, q) in enumerate(zip(pairs, qs))]


def reduce_scatter_chips(tag, gs, place):
    pairs = pair_adds(tag, gs, run_plan(tag + "_pair_exchange", plan_pair_exchange(gs)), place)
    return run_plan(tag + "_pair_join", plan_pair_join(chip_adds(tag, pairs, run_plan(tag + "_chip_scatter", plan_chip_scatter(pairs)), place)))


BIG = [("w_out", (2, 512, 1024)), ("w_mem_kv", (2, 256, 1024)), ("s5_w_in", (1, 1024, 1024)), ("s5_w_glu", (1, 1536, 768)),
       ("mla_w_in", (1, 1024, 848)), ("mla_w_uq", (1, 512, 576)), ("mla_w_ukv", (1, 256, 768))]
SHARDED_SMALL = [("mla_q_lora_norm", (1, 128)), ("mla_kv_lora_norm", (1, 64))]
SMALL = [("ln_gain", (2, 1024)), ("mem_norm", (2, 1024)), ("xq_norm", (2, 128)), ("xk_norm", (2, 128)),
         ("s5_lambda_re", (1, 96, 64)), ("s5_lambda_im", (1, 96, 64)), ("s5_log_step", (1, 96)),
         ("s5_b_re", (1, 96, 64, 16)), ("s5_b_im", (1, 96, 64, 16)), ("s5_c_re", (1, 96, 16, 64)), ("s5_c_im", (1, 96, 16, 64)),
         ("s5_d", (1, 1536)), ("mla_q_nope_norm", (1, 128)), ("mla_k_nope_norm", (1, 128)), ("mla_q_rope_norm", (1, 64)),
         ("mla_k_rope_norm", (1, 64))]
WEIGHT_ORDER = ["ln_gain", "w_out", "mem_norm", "w_mem_kv", "xq_norm", "xk_norm", "s5_w_in", "s5_lambda_re", "s5_lambda_im",
                "s5_log_step", "s5_b_re", "s5_b_im", "s5_c_re", "s5_c_im", "s5_d", "s5_w_glu", "mla_w_in", "mla_q_lora_norm",
                "mla_kv_lora_norm", "mla_w_uq", "mla_w_ukv", "mla_q_nope_norm", "mla_k_nope_norm", "mla_q_rope_norm", "mla_k_rope_norm"]
MINOR_LAST = {"mla_w_in": (0, 2, 1), "mla_w_uq": (0, 2, 1), "s5_b_re": (0, 2, 3, 1), "s5_b_im": (0, 2, 3, 1),
              "s5_c_re": (0, 2, 3, 1), "s5_c_im": (0, 2, 3, 1)}
SMALL_FULL = SMALL + [(n, (1, 4 * s[1])) for n, s in SHARDED_SMALL]
N_SMALL = sum(math.prod(s) for _, s in SMALL_FULL)
SMALL_ROWS, SMALL_LANES = 128, 1024

PAIR_OUT, PAIR_MKV, PAIR_ROWS = 0, 512, 768


def stack_shards(w, dtype):
    pairs = [jnp.concatenate([w["w_out"][l], w["w_mem_kv"][l]], axis=0).astype(dtype) for l in range(2)]
    return ([w["s5_w_in"][0].astype(dtype)], [pairs[0], w["s5_w_glu"][0].astype(dtype)],
            [pairs[1], w["mla_w_ukv"][0].astype(dtype), w["mla_w_in"][0].astype(dtype), w["mla_w_uq"][0].astype(dtype)])


def pair_views(pair):
    return {"w_out": Sharded(pair, "row", PAIR_OUT, 512), "w_mem_kv": Sharded(pair, "row", PAIR_MKV, 256)}


def grad_views():
    pair = SDS((4, PAIR_ROWS, 1024), BF16)
    return {"w_out": Sharded(pair, "row", PAIR_OUT, 512), "w_mem_kv": Sharded(pair, "row", PAIR_MKV, 256),
            "s5_w_in": Sharded(SDS((4, 1024, 1024), BF16), "col", 0, 1024), "s5_w_glu": Sharded(SDS((4, 1536, 768), BF16), "col", 0, 1536),
            "mla_w_ukv": Sharded(SDS((4, 256, 768), BF16), "col", 0, 256)}


def cols_to_shards(full):
    return full.reshape(full.shape[0], 4, full.shape[1] // 4).transpose(1, 0, 2)


def shards_to_cols(arr):
    return arr.transpose(1, 0, 2).reshape(arr.shape[1], 4 * arr.shape[2])


def mla_in_permute(w):
    o1, o2, o3, o4 = QL, QL + KVL, QL + KVL + ROPE, QL + KVL + ROPE + XQW
    return jnp.concatenate([w[:, o4:], w[:, :o1], w[:, o3:o4], w[:, o1:o2], w[:, o2:o3],
                            jnp.zeros((w.shape[0], MLA_IN_P - MLA_IN), w.dtype)], axis=1)


def mla_in_unpermute(d):
    return jnp.concatenate([d[:, 2048:2560], d[:, 3072:3328], d[:, 3328:3392], d[:, 2560:3072], d[:, :2048]], axis=1)


def uq_permute(w):
    w3 = w.reshape(w.shape[0], MH, NOPE + ROPE)
    return jnp.concatenate([w3[:, :, :NOPE].reshape(w.shape[0], MH * NOPE), w3[:, :, NOPE:].reshape(w.shape[0], MH * ROPE)], axis=1)


def uq_unpermute(d):
    dn = d[:, :MH * NOPE].reshape(d.shape[0], MH, NOPE)
    dr = d[:, MH * NOPE:].reshape(d.shape[0], MH, ROPE)
    return jnp.concatenate([dn, dr], axis=2).reshape(d.shape[0], MH * (NOPE + ROPE))


def time_permute(a):
    return a.reshape(SEG, SEG_LEN, a.shape[-1]).transpose(1, 0, 2).reshape(L, a.shape[-1])


def time_unpermute(a):
    return a.reshape(SEG_LEN, SEG, a.shape[-1]).transpose(1, 0, 2).reshape(L, a.shape[-1])


def mem_branch_fwd(tag, mem, mem_norm, w_mem_kv, xk_norm):
    mn = row_fwd(tag + "_mem_rms", fn_rms, ML, ML, [(mem, D, 0)], [mem_norm], [(D, BF16)])[0]
    kv = matmul(tag + "_mem_kv", mn, w_mem_kv, "nn", F32)
    kn = row_fwd(tag + "_mem_knorm", fn_mem_k, ML, ML, [(kv, XQW, 0)], [xk_norm], [(XQW, F32)])[0]
    return mn, kv, kn


def mem_branch_bwd(tag, mem, mem_norm, w_mem_kv, xk_norm, mn, kv, dkn, dv, g_view, g_wide):
    dk, dxk = row_bwd(tag + "_mem_knorm_bwd", fn_mem_k, ML, ML, [(kv, XQW, 0)], [xk_norm], [(dkn, XQW, 0)], [True], [True])
    dkv = jnp.concatenate([dk, dv], axis=1)
    dmn = matmul(tag + "_mem_kv_dx", dkv, w_mem_kv, "nt", F32)
    g_wide = matmul(tag + "_mem_kv_dw", mn, dkv, "tn", out=g_view, into=g_wide)
    dmem_norm = row_bwd(tag + "_mem_rms_bwd", fn_rms, ML, ML, [(mem, D, 0)], [mem_norm], [(dmn, D, 0)], [False], [True])[0]
    return g_wide, dmem_norm, dxk


def mem_attn_fwd(tag, proj, cb, kn, kv, xq_norm):
    return row_fwd(tag + "_mem_attn", fn_mem_attn, L, 256, [(proj, XQW, cb)], [kn, kv[:, XQW:], xq_norm], [(XQW, F32)])[0]


def mem_attn_bwd(tag, proj, cb, kn, kv, xq_norm, dmo, dproj):
    place = {"cols": proj.shape[1], "cb": cb, "into": dproj, "dtype": dproj.dtype}
    return row_bwd(tag + "_mem_attn_bwd", fn_mem_attn, L, 256, [(proj, XQW, cb)], [kn, kv[:, XQW:], xq_norm], [(dmo, XQW, 0)],
                   [place], [True, True, True])


def device_step(x, mem, positions, target, small, env, hooks=None):
    hooks = hooks or {}

    def plans_for(name):
        return hooks[("plans", name)](env) if ("plans", name) in hooks else ()

    def around(when, name, last=None):
        if (when, name) in hooks:
            hooks[(when, name)](env, last)

    g = {}
    gw = grad_views()
    ln, mem_norm, xq_norm, xk_norm = small["ln_gain"], small["mem_norm"], small["xq_norm"], small["xk_norm"]

    lre, lim = small["s5_lambda_re"][0], small["s5_lambda_im"][0]
    if "token" in env:
        lre = lre + env["token"][0, 0]
    ls = small["s5_log_step"].reshape(SG, 1)
    one = pl.BlockSpec((SG, SP), lambda i: (0, 0))
    col = pl.BlockSpec((SG, 1), lambda i: (0, 0))
    disc_ins = [(lre, one), (lim, one), (ls, col)]
    a_re, a_im, coef_re, coef_im = stage("s5_disc", fn_s5_disc, (1,), disc_ins, [(SDS((SG, SP), F32), one)] * 4)
    b_re, b_im = small["s5_b_re"].reshape(SN, SC), small["s5_b_im"].reshape(SN, SC)
    c_re, c_im = small["s5_c_re"].reshape(PW, SP), small["s5_c_im"].reshape(PW, SP)
    bmat_rows = [(b_re, SC, 0), (b_im, SC, 0), (coef_re.reshape(SN, 1), 1, 0), (coef_im.reshape(SN, 1), 1, 0)]
    wb_re, wb_im = row_fwd("s5_bmat", fn_s5_bmat, SN, 512, bmat_rows, [], [(128, F32)] * 2)
    cmat_rows = [(c_re, SP, 0), (c_im, SP, 0)]
    wc_re, wc_im = row_fwd("s5_cmat", fn_s5_cmat, PW, 128, cmat_rows, [], [(512, F32)] * 2)
    a_re_v, a_im_v = a_re.reshape(1, SN), a_im.reshape(1, SN)
    s5_d = small["s5_d"]

    xp = time_permute(x)
    h0 = row_fwd("l0_rms", fn_rms, L, 256, [(xp, D, 0)], [ln[0:1]], [(D, BF16)])[0]
    around("before", "l0_in", wb_re)
    w_in0 = Sharded(env["in0"], "col", 0, 1024)
    proj0 = matmul("l0_in", h0, w_in0, "nn")
    s_re, s_im, g0 = s5_forward(proj0, wb_re, wb_im, wc_re, wc_im, a_re_v, a_im_v, s5_d, plans=plans_for("s5_forward"))
    around("before", "l0_glu", g0)
    w0 = dict(pair_views(env["pair0"]), s5_w_glu=Sharded(env["glu"], "col", 0, 1536))
    z0 = matmul("l0_glu", g0, w0["s5_w_glu"], "nn", plans=plans_for("l0_glu"))
    mn0, kv0, kn0 = mem_branch_fwd("l0", mem, mem_norm[0:1], w0["w_mem_kv"], xk_norm[0:1])
    mo0 = mem_attn_fwd("l0", proj0, 3, kn0, kv0, xq_norm[0:1])
    o0 = row_fwd("l0_merge", fn_merge_glu, L, 256, [(z0, 2 * PW, 0), (mo0, XQW, 0), (proj0, BW, 1)], [], [(BW, BF16)])[0]
    around("before", "l0_out", o0)
    x1p = matmul("l0_out", o0, w0["w_out"], "nn", F32, add=xp, plans=plans_for("l0_out"))
    around("after", "l0_out", x1p)
    x1 = time_unpermute(x1p)

    w1 = dict(pair_views(env["pair1"]), mla_w_ukv=Sharded(env["ukv"], "col", 0, 256))
    w_in1, w_uq = env["w_in1"], env["w_uq"]
    h1 = row_fwd("l1_rms", fn_rms, L, 256, [(x1, D, 0)], [ln[1:2]], [(D, BF16)])[0]
    proj1 = matmul("l1_in", h1, w_in1, "nn")
    qln, kvln = env["q_lora_norm"].reshape(1, QL), env["kv_lora_norm"].reshape(1, KVL)
    cqn = row_fwd("l1_q_lora_rms", fn_rms, L, 256, [(proj1, QL, 4)], [qln], [(QL, BF16)])[0]
    ckvn = row_fwd("l1_kv_lora_rms", fn_rms, L, 256, [(proj1, KVL, 12)], [kvln], [(KVL, BF16)])[0]
    q = matmul("l1_uq", cqn, w_uq, "nn")
    kv = matmul("l1_ukv", ckvn, w1["mla_w_ukv"], "nn")
    inv_freq = ROPE_THETA ** (-jnp.arange(ROPE // 2, dtype=F32) / (ROPE // 2))
    ang = positions.astype(F32)[:, None] * inv_freq
    cos2 = jnp.tile(jnp.cos(ang), (1, 4))
    sin_signed = jnp.tile(jnp.concatenate([-jnp.sin(ang), jnp.sin(ang)], axis=1), (1, 2))
    qnn, knn = small["mla_q_nope_norm"], small["mla_k_nope_norm"]
    qrn, krn = jnp.tile(small["mla_q_rope_norm"], (1, 2)), jnp.tile(small["mla_k_rope_norm"], (1, 2))
    tp = 256
    prep_ins = [(q, rspec(tp, MH * (NOPE + ROPE))), (kv, rspec(tp, MH * 256)), (proj1, rspec(tp, 128, 26)),
                (cos2, rspec(tp, 128)), (sin_signed, rspec(tp, 128))] + [(a, cspec((1, 128))) for a in (qnn, knn, qrn, krn)]
    hq_spec = pl.BlockSpec((MH, tp, 256), lambda i: (0, i, 0))
    hv_spec = pl.BlockSpec((MH, tp, 128), lambda i: (0, i, 0))
    qf, kf, vh = stage("l1_mla_prep", fn_mla_prep, (L // tp,), prep_ins,
                       [(SDS((MH, L, 256), BF16), hq_spec), (SDS((MH, L, 256), BF16), hq_spec), (SDS((MH, L, 128), BF16), hv_spec)])
    attn = causal_attn(qf, kf, vh)
    mn1, kv1, kn1 = mem_branch_fwd("l1", mem, mem_norm[1:2], w1["w_mem_kv"], xk_norm[1:2])
    mo1 = mem_attn_fwd("l1", proj1, 5, kn1, kv1, xq_norm[1:2])
    o1 = row_fwd("l1_merge", fn_merge, L, 256, [(attn, PW, 0), (mo1, XQW, 0), (proj1, BW, 0)], [], [(BW, BF16)])[0]
    x2 = matmul("l1_out", o1, w1["w_out"], "nn", F32, add=x1)
    dx2, loss = loss_and_grad(x2, target)

    do1 = matmul("l1_out_dx", dx2, w1["w_out"], "nt")
    g_pair1 = matmul("l1_out_dw", o1, dx2, "tn", out=gw["w_out"])
    dattn, dmo1, dproj1 = row_bwd("l1_merge_bwd", fn_merge, L, 256, [(attn, PW, 0), (mo1, XQW, 0), (proj1, BW, 0)], [],
                                  [(do1, BW, 0)], [True, True, {"cols": MLA_IN_P, "cb": 0, "dtype": BF16}], [])
    dproj1, dkn1, dv1, dxqn1 = mem_attn_bwd("l1", proj1, 5, kn1, kv1, xq_norm[1:2], dmo1, dproj1)
    env["g_pair1"], dmem_norm1, dxk1 = mem_branch_bwd("l1", mem, mem_norm[1:2], w1["w_mem_kv"], xk_norm[1:2], mn1, kv1, dkn1, dv1,
                                                      gw["w_mem_kv"], g_pair1)
    dqf, dkf, dvh = causal_attn_bwd(qf, kf, vh, dattn)
    prep_diffs = [("row", SDS((L, MH * (NOPE + ROPE)), BF16), rspec(tp, MH * (NOPE + ROPE))), ("row", SDS((L, MH * 256), BF16), rspec(tp, MH * 256)),
                  ("row", SDS((L, MLA_IN_P), BF16), rspec(tp, 128, 26), {"into": dproj1}), None, None] + [("acc", (0,))] * 4
    dq, dkv, dproj1, dqnn, dknn, dqrn, dkrn = stage_bwd("l1_mla_prep_bwd", fn_mla_prep, (L // tp,), prep_ins,
                                                        [(dqf, hq_spec), (dkf, hq_spec), (dvh, hv_spec)], prep_diffs)
    dcqn = matmul("l1_uq_dx", dq, w_uq, "nt")
    env["dw_uq"] = matmul("l1_uq_dw", cqn, dq, "tn")
    dckvn = matmul("l1_ukv_dx", dkv, w1["mla_w_ukv"], "nt")
    env["g_ukv"] = matmul("l1_ukv_dw", ckvn, dkv, "tn", out=gw["mla_w_ukv"])
    dproj1, dqln = row_bwd("l1_q_lora_rms_bwd", fn_rms, L, 256, [(proj1, QL, 4)], [qln], [(dcqn, QL, 0)],
                           [{"cols": MLA_IN_P, "cb": 4, "into": dproj1, "dtype": BF16}], [True])
    dproj1, dkvln = row_bwd("l1_kv_lora_rms_bwd", fn_rms, L, 256, [(proj1, KVL, 12)], [kvln], [(dckvn, KVL, 0)],
                            [{"cols": MLA_IN_P, "cb": 12, "into": dproj1, "dtype": BF16}], [True])
    dh1 = matmul("l1_in_dx", dproj1, w_in1, "nt")
    env["dw_in1"] = matmul("l1_in_dw", h1, dproj1, "tn")
    dx1, dln1 = row_bwd("l1_rms_bwd", fn_rms, L, 256, [(x1, D, 0)], [ln[1:2]], [(dh1, D, 0)], [{"add": (dx2, D, 0)}], [True],
                        plans=plans_for("l1_rms_bwd"))
    dx1p = time_permute(dx1)

    do0 = matmul("l0_out_dx", dx1p, w0["w_out"], "nt", plans=plans_for("l0_out_dx"))
    g_pair0 = matmul("l0_out_dw", o0, dx1p, "tn", out=gw["w_out"])
    dz0, dmo0, dproj0 = row_bwd("l0_merge_bwd", fn_merge_glu, L, 256, [(z0, 2 * PW, 0), (mo0, XQW, 0), (proj0, BW, 1)], [],
                                [(do0, BW, 0)], [{"dtype": BF16}, True, {"cols": 2 * BW, "cb": 1, "dtype": BF16}], [])
    dproj0, dkn0, dv0, dxqn0 = mem_attn_bwd("l0", proj0, 3, kn0, kv0, xq_norm[0:1], dmo0, dproj0)
    env["g_pair0"], dmem_norm0, dxk0 = mem_branch_bwd("l0", mem, mem_norm[0:1], w0["w_mem_kv"], xk_norm[0:1], mn0, kv0, dkn0, dv0,
                                                      gw["w_mem_kv"], g_pair0)
    env["g_glu"] = matmul("l0_glu_dw", g0, dz0, "tn", out=gw["s5_w_glu"], plans=plans_for("l0_glu_dw"))
    dg0 = matmul("l0_glu_dx", dz0, w0["s5_w_glu"], "nt", plans=plans_for("l0_glu_dx"))
    dproj0, dd, dwc_re, dwc_im, dwb_re, dwb_im, da_re, da_im = s5_backward(dg0, proj0, s_re, s_im, wb_re, wb_im, wc_re, wc_im,
                                                                           a_re_v, a_im_v, s5_d, dproj0, plans=plans_for("s5_backward"))
    env["g_in0"] = matmul("l0_in_dw", h0, dproj0, "tn", out=gw["s5_w_in"], plans=plans_for("l0_in_dw"))
    dh0 = matmul("l0_in_dx", dproj0, w_in0, "nt", plans=plans_for("l0_in_dx"))
    dxp, dln0 = row_bwd("l0_rms_bwd", fn_rms, L, 256, [(xp, D, 0)], [ln[0:1]], [(dh0, D, 0)], [{"add": (dx1p, D, 0)}], [True])
    grad_x = time_unpermute(dxp)

    db_re, db_im, dcoef_re, dcoef_im = row_bwd("s5_bmat_bwd", fn_s5_bmat, SN, 512, bmat_rows, [], [(dwb_re, 128, 0), (dwb_im, 128, 0)],
                                               [True] * 4, [], plans=plans_for("s5_bmat_bwd"))
    dc_re, dc_im = row_bwd("s5_cmat_bwd", fn_s5_cmat, PW, 128, cmat_rows, [], [(dwc_re, 512, 0), (dwc_im, 512, 0)], [True] * 2, [],
                           plans=plans_for("s5_cmat_bwd"))
    disc_cts = [(da_re.reshape(SG, SP), one), (da_im.reshape(SG, SP), one), (dcoef_re.reshape(SG, SP), one), (dcoef_im.reshape(SG, SP), one)]
    dlre, dlim, dls = stage_bwd("s5_disc_bwd", fn_s5_disc, (1,), disc_ins, disc_cts, [("acc", (0,))] * 3)

    g["ln_gain"] = jnp.concatenate([dln0, dln1], axis=0)
    g["mem_norm"] = jnp.concatenate([dmem_norm0, dmem_norm1], axis=0)
    g["xq_norm"] = jnp.concatenate([dxqn0, dxqn1], axis=0)
    g["xk_norm"] = jnp.concatenate([dxk0, dxk1], axis=0)
    g["s5_lambda_re"], g["s5_lambda_im"], g["s5_log_step"] = dlre, dlim, dls
    g["s5_b_re"], g["s5_b_im"], g["s5_c_re"], g["s5_c_im"] = db_re, db_im, dc_re, dc_im
    g["s5_d"] = dd
    g["mla_q_lora_norm"], g["mla_kv_lora_norm"] = dqln, dkvln
    g["mla_q_nope_norm"], g["mla_k_nope_norm"] = dqnn, dknn
    g["mla_q_rope_norm"] = dqrn[:, :ROPE] + dqrn[:, ROPE:]
    g["mla_k_rope_norm"] = dkrn[:, :ROPE] + dkrn[:, ROPE:]
    return loss, grad_x, g


def kernel(x, mem, positions, ln_gain, w_out, mem_norm, w_mem_kv, xq_norm, xk_norm, s5_w_in, s5_lambda_re, s5_lambda_im, s5_log_step, s5_b_re, s5_b_im, s5_c_re, s5_c_im, s5_d, s5_w_glu, mla_w_in, mla_q_lora_norm, mla_kv_lora_norm, mla_w_uq, mla_w_ukv, mla_q_nope_norm, mla_k_nope_norm, mla_q_rope_norm, mla_k_rope_norm, loss_target, m_ln_gain, m_w_out, m_mem_norm, m_w_mem_kv, m_xq_norm, m_xk_norm, m_s5_w_in, m_s5_lambda_re, m_s5_lambda_im, m_s5_log_step, m_s5_b_re, m_s5_b_im, m_s5_c_re, m_s5_c_im, m_s5_d, m_s5_w_glu, m_mla_w_in, m_mla_q_lora_norm, m_mla_kv_lora_norm, m_mla_w_uq, m_mla_w_ukv, m_mla_q_nope_norm, m_mla_k_nope_norm, m_mla_q_rope_norm, m_mla_k_rope_norm, v_ln_gain, v_w_out, v_mem_norm, v_w_mem_kv, v_xq_norm, v_xk_norm, v_s5_w_in, v_s5_lambda_re, v_s5_lambda_im, v_s5_log_step, v_s5_b_re, v_s5_b_im, v_s5_c_re, v_s5_c_im, v_s5_d, v_s5_w_glu, v_mla_w_in, v_mla_q_lora_norm, v_mla_kv_lora_norm, v_mla_w_uq, v_mla_w_ukv, v_mla_q_nope_norm, v_mla_k_nope_norm, v_mla_q_rope_norm, v_mla_k_rope_norm):
    args = dict(locals())
    wts = {n: args[n] for n in WEIGHT_ORDER}
    mom = {n: args["m_" + n] for n in WEIGHT_ORDER}
    var = {n: args["v_" + n] for n in WEIGHT_ORDER}

    chip = 2 * lax.axis_index("x") + lax.axis_index("y")
    place = jnp.stack([chip, lax.axis_index("c")]).astype(jnp.int32)

    def own_slot(gathered, shards):
        return [lax.dynamic_update_slice(g, s[None], (chip, 0, 0)) for g, s in zip(gathered, shards)]

    groups = list(stack_shards(wts, BF16))
    groups[2].append(jnp.concatenate([mla_q_lora_norm, jnp.pad(mla_kv_lora_norm, ((0, 0), (0, 64))), jnp.zeros((14, 128), F32)], axis=0))
    over_ici = [plan_gather_ici(shards) for shards in groups]
    env = {"token": split_start("gather_start", over_ici)}
    hooks, passed_on = {}, {}

    def arrived(k, after, pass_now):
        passing = plan_gather_pass(split_wait(f"gather_wait_{k}", over_ici[k], after))
        passed_on[k] = passing
        return own_slot(run_plan(f"gather_pass_{k}", passing), groups[k]) if pass_now else None

    def need_in0(env, last):
        env["in0"], = arrived(0, last, True)

    def need_layer0(env, last):
        env["pair0"], env["glu"] = arrived(1, last, True)

    def need_layer1(env, last):
        arrived(2, last, False)

    def layer1_weights(env, last):
        pair1, ukv, in1, uq, norms = own_slot(passed_on[2].results, groups[2])
        env.update(pair1=pair1, ukv=ukv, w_in1=mla_in_permute(shards_to_cols(in1)), w_uq=uq_permute(shards_to_cols(uq)),
                   q_lora_norm=norms[:, 0, :], kv_lora_norm=norms[:, 1, :64])

    hooks["before", "l0_in"], hooks["before", "l0_glu"], hooks["before", "l0_out"] = need_in0, need_layer0, need_layer1
    hooks["plans", "l0_out"], hooks["after", "l0_out"] = (lambda env: [passed_on[2]]), layer1_weights

    rs = {}

    def swap(k, gs):
        rs[k, "g"], rs[k, "swap"] = gs, plan_pair_exchange(gs)
        return rs[k, "swap"]

    def scatter(k, part=slice(None)):
        if (k, "pairs") not in rs:
            rs[k, "pairs"], rs[k, "scatter"] = pair_adds(f"rs{k}", rs[k, "g"], rs[k, "swap"].results, place), []
        rs[k, "scatter"].append(plan_chip_scatter(rs[k, "pairs"][part]))
        return rs[k, "scatter"][-1]

    def join(k):
        rs[k, "join"] = plan_pair_join(chip_adds(f"rs{k}", rs[k, "pairs"], [q for p in rs[k, "scatter"] for q in p.results], place))
        return rs[k, "join"]

    hooks["plans", "l1_rms_bwd"] = lambda env: [swap(1, [env["g_pair1"], env["g_ukv"], cols_to_shards(mla_in_unpermute(env["dw_in1"])).astype(BF16),
                                                          cols_to_shards(uq_unpermute(env["dw_uq"])).astype(BF16)])]
    hooks["plans", "l0_glu_dw"] = lambda env: [scatter(1, slice(0, 2))]
    hooks["plans", "l0_glu_dx"] = lambda env: [swap(0, [env["g_pair0"], env["g_glu"]])]
    hooks["plans", "s5_backward"] = lambda env: [scatter(1, slice(2, 4)), scatter(0)]
    hooks["plans", "l0_in_dw"] = lambda env: [join(1)]
    hooks["plans", "l0_in_dx"] = lambda env: [swap(2, [env["g_in0"]]), join(0)]
    hooks["plans", "s5_bmat_bwd"] = lambda env: [scatter(2)]
    hooks["plans", "s5_cmat_bwd"] = lambda env: [join(2)]

    small = {n: wts[n] for n, _ in SMALL}
    loss, grad_x, g = device_step(x[0], mem[0], positions[0], loss_target[0], small, env, hooks)
    loss = lax.psum(loss[0, 0], MESH_AXES)
    (r_pair1, r_ukv, r_in1, r_uq), (r_pair0, r_glu), (r_in0,) = (rs[k, "join"].results for k in (1, 0, 2))

    small_flat = jnp.concatenate([g[n].reshape(-1) for n, _ in SMALL_FULL])
    g_small = jnp.pad(small_flat, (0, 4 * SMALL_ROWS * SMALL_LANES - N_SMALL)).astype(BF16).reshape(4, SMALL_ROWS, SMALL_LANES)
    r_small = reduce_scatter_chips("rs3", [g_small], place)[0]
    small_all = own_slot(all_gather_chips("gather_small_grads", [r_small]), [r_small])[0].reshape(-1)[:N_SMALL]

    grads = {"w_out": jnp.stack([r_pair0[:PAIR_MKV], r_pair1[:PAIR_MKV]]), "w_mem_kv": jnp.stack([r_pair0[PAIR_MKV:], r_pair1[PAIR_MKV:]]),
             "s5_w_in": r_in0[None], "s5_w_glu": r_glu[None], "mla_w_ukv": r_ukv[None], "mla_w_in": r_in1[None], "mla_w_uq": r_uq[None]}
    off = 0
    for n, s in SMALL_FULL:
        grads[n] = small_all[off:off + math.prod(s)].reshape(s)
        off += math.prod(s)
    for n, s in SHARDED_SMALL:
        grads[n] = lax.dynamic_slice(grads[n], (0, chip * s[1]), s)

    delta, new_m, new_v = {}, {}, {}
    for n, s in BIG + [(n, s) for n, s in SMALL if len(s) == 4]:
        perm = MINOR_LAST.get(n, tuple(range(len(s))))
        turned = tuple(s[p] for p in perm)
        view = lambda a: jnp.transpose(a, perm).reshape(-1, turned[-1])
        res = adamw("adamw_" + n, view(wts[n]), view(grads[n]), view(mom[n]), view(var[n]))
        delta[n], new_m[n], new_v[n] = (jnp.transpose(r.reshape(turned), tuple(perm.index(i) for i in range(len(s)))) for r in res)
    small_names = [n for n, s in SMALL if len(s) < 4] + [n for n, _ in SHARDED_SMALL]
    n_own = sum(wts[n].size for n in small_names)
    rows_own = -(-n_own // (8 * 128)) * 8

    def pack_small(d):
        flat = jnp.concatenate([d[n].reshape(-1) for n in small_names])
        return jnp.pad(flat, (0, rows_own * 128 - n_own), constant_values=1.0).reshape(rows_own, 128)

    res = adamw("adamw_small", pack_small(wts), pack_small(grads), pack_small(mom), pack_small(var))
    off = 0
    for n in small_names:
        size = wts[n].size
        delta[n], new_m[n], new_v[n] = (r.reshape(-1)[off:off + size].reshape(wts[n].shape) for r in res)
        off += size

    return (loss, grad_x[None], *[grads[n] for n in WEIGHT_ORDER], *[delta[n] for n in WEIGHT_ORDER],
            *[new_m[n] for n in WEIGHT_ORDER], *[new_v[n] for n in WEIGHT_ORDER])
```

```python
import functools
import math

import jax
import jax.numpy as jnp
from jax import lax
from jax.experimental import pallas as pl
from jax.experimental.pallas import tpu as pltpu

F32, BF16 = jnp.float32, jnp.bfloat16
SDS = jax.ShapeDtypeStruct

D = 1024
L = 2048
ML = 256
BW = 2 * D
XQW = BW // 4
PW = BW - XQW
XH, XHD = 4, 128
SG, SC, SP = 96, 16, 64
SN = SG * SP
NOPE, ROPE, VD = 128, 64, 128
MH = 12
QL, KVL = 512, 256
EPS = 1e-6
ROPE_THETA = 10000.0
MLA_IN = QL + KVL + ROPE + XQW + BW
MLA_IN_P = 3456
ADAM_LR, ADAM_B1, ADAM_B2, ADAM_EPS, ADAM_WD, ADAM_STEP = 0.001, 0.9, 0.999, 1e-08, 0.01, 10

VMEM_LIMIT = 48 * 2**20
SEG = 8
SEG_LEN = L // SEG
MESH_AXES = ("x", "y", "c")


def _cparams():
    return pltpu.CompilerParams(vmem_limit_bytes=VMEM_LIMIT)


def _dg(a, b, ca, cb):
    return lax.dot_general(a.astype(BF16), b.astype(BF16), (((ca,), (cb,)), ((), ())), preferred_element_type=F32)


@jax.custom_vjp
def mm_nn(a, b):
    return _dg(a, b, 1, 0)


mm_nn.defvjp(lambda a, b: (_dg(a, b, 1, 0), (a, b)), lambda res, g: (_dg(g, res[1], 1, 1), _dg(res[0], g, 0, 0)))


@jax.custom_vjp
def mm_nt(a, b):
    return _dg(a, b, 1, 1)


mm_nt.defvjp(lambda a, b: (_dg(a, b, 1, 1), (a, b)), lambda res, g: (_dg(g, res[1], 1, 0), _dg(g, res[0], 0, 0)))


@functools.partial(jax.custom_vjp, nondiff_argnums=(1,))
def lane_roll(x, shift):
    return pltpu.roll(x, shift, 1)


lane_roll.defvjp(lambda x, shift: (pltpu.roll(x, shift, 1), None),
                 lambda shift, _, g: (pltpu.roll(g, (128 - shift) % 128, 1),))


def rms(x, g):
    return x * lax.rsqrt(jnp.mean(x * x, axis=-1, keepdims=True) + EPS) * g


@jax.custom_vjp
def softmax_rows(s):
    e = jnp.exp(s - jnp.max(s, axis=-1, keepdims=True))
    return e / jnp.sum(e, axis=-1, keepdims=True)


def _softmax_rows_fwd(s):
    p = softmax_rows(s)
    return p, p


def _softmax_rows_bwd(p, g):
    return (p * (g - jnp.sum(g * p, axis=-1, keepdims=True)),)


softmax_rows.defvjp(_softmax_rows_fwd, _softmax_rows_bwd)


def silu(x):
    return x * jax.nn.sigmoid(x)


ANY = pl.BlockSpec(memory_space=pl.ANY)
MESH_ID = pl.DeviceIdType.MESH


def _dma_sems(n):
    return [pltpu.SemaphoreType.DMA((n,)), pltpu.SemaphoreType.DMA((n,))]


class Plan:
    def __init__(self, operands, out_shape, aliases, n_sems, copies):
        self.operands, self.out_shape, self.aliases, self.n_sems, self.copies = list(operands), list(out_shape), aliases, n_sems, copies
        self.results = None


_SCHEDULE_BEHIND = []


def schedule_behind(token):
    _SCHEDULE_BEHIND.append(token)


def hosted_call(kern, *, name, grid, in_specs, out_specs, out_shape, operands, scratch_shapes=(), aliases=None, cparams=None, plans=(), deps=()):
    n_in, n_out, n_scr = len(in_specs), len(out_specs), len(scratch_shapes)
    p_in, p_out = [len(p.operands) for p in plans], [len(p.out_shape) for p in plans]
    deps = tuple(deps) + tuple(_SCHEDULE_BEHIND)
    _SCHEDULE_BEHIND.clear()
    all_aliases = dict(aliases or {})
    in_off, out_off = n_in, n_out
    for p, ni, no in zip(plans, p_in, p_out):
        all_aliases.update({in_off + i: out_off + o for i, o in p.aliases.items()})
        in_off, out_off = in_off + ni, out_off + no

    def body(*refs):
        pos, pins, pouts = n_in, [], []
        for ni in p_in:
            pins.append(refs[pos:pos + ni])
            pos += ni
        pos += len(deps)
        main_out = refs[pos:pos + n_out]
        pos += n_out
        for no in p_out:
            pouts.append(refs[pos:pos + no])
            pos += no
        main_scr = refs[pos:pos + n_scr]
        pos += n_scr
        if plans:
            ids = [pl.program_id(ax) for ax in range(len(grid))]
            first = functools.reduce(jnp.logical_and, [i == 0 for i in ids])
            last = functools.reduce(jnp.logical_and, [i == g - 1 for i, g in zip(ids, grid)])
            copies = [p.copies(pins[k], pouts[k], refs[pos + 2 * k], refs[pos + 2 * k + 1]) for k, p in enumerate(plans)]

            @pl.when(first)
            def _():
                for sends, _ in copies:
                    for cp in sends:
                        cp.start()

        kern(*refs[:n_in], *main_out, *main_scr)
        if plans:
            @pl.when(last)
            def _():
                for sends, recvs in copies:
                    for cp in recvs:
                        cp.wait_recv()
                    for cp in sends:
                        cp.wait_send()

    res = pl.pallas_call(body, grid=grid, in_specs=list(in_specs) + [ANY] * (sum(p_in) + len(deps)),
                         out_specs=list(out_specs) + [ANY] * sum(p_out), out_shape=list(out_shape) + [s for p in plans for s in p.out_shape],
                         scratch_shapes=list(scratch_shapes) + [s for p in plans for s in _dma_sems(p.n_sems)],
                         input_output_aliases=all_aliases, name=name, compiler_params=cparams or _cparams())(
        *operands, *[a for p in plans for a in p.operands], *deps)
    pos = n_out
    for p, no in zip(plans, p_out):
        p.results = list(res[pos:pos + no])
        pos += no
    return list(res[:n_out])


def _wide(v):
    return v.astype(F32) if v.dtype == BF16 else v


def stage(name, fn, grid, ins, outs):
    n_in = len(ins)

    def kern(*refs):
        res = fn(*[_wide(r[...]) for r in refs[:n_in]])
        for r, v in zip(refs[n_in:], res):
            r[...] = v.astype(r.dtype)

    return hosted_call(kern, name=name, grid=grid, in_specs=[s for _, s in ins], out_specs=[s for _, s in outs],
                       out_shape=[sd for sd, _ in outs], operands=[a for a, _ in ins])


def stage_bwd(name, fn, grid, ins, cts, diffs, plans=()):
    n_in, n_ct = len(ins), len(cts)
    didx = [i for i, d in enumerate(diffs) if d is not None]
    opts = {i: (diffs[i][3] if len(diffs[i]) > 3 else {}) for i in didx if diffs[i][0] == "row"}
    adds = [(i, opts[i]["add"]) for i in opts if "add" in opts[i]]
    intos = [(i, opts[i]["into"]) for i in opts if "into" in opts[i]]
    n_add, n_into = len(adds), len(intos)
    add_pos = {i: n_in + n_ct + k for k, (i, _) in enumerate(adds)}
    n_extra = n_in + n_ct + n_add + n_into

    def kern(*refs):
        vals = [_wide(r[...]) for r in refs[:n_in]]

        def f(*dv):
            full = list(vals)
            for i, v in zip(didx, dv):
                full[i] = v
            return fn(*full)

        _, vjp = jax.vjp(f, *[vals[i].astype(F32) for i in didx])
        gs = vjp(tuple(c[...].astype(F32) for c in refs[n_in:n_in + n_ct]))
        for o_ref, i, g in zip(refs[n_extra:], didx, gs):
            if diffs[i][0] == "row":
                if i in add_pos:
                    g = g + refs[add_pos[i]][...].astype(F32)
                o_ref[...] = g.astype(o_ref.dtype)
            else:
                first = functools.reduce(jnp.logical_and, [pl.program_id(ax) == 0 for ax in diffs[i][1]])

                @pl.when(first)
                def _():
                    o_ref[...] = g

                @pl.when(jnp.logical_not(first))
                def _():
                    o_ref[...] += g

    out_shape, out_specs = [], []
    for i in didx:
        if diffs[i][0] == "row":
            out_shape.append(diffs[i][1])
            out_specs.append(diffs[i][2])
        else:
            out_shape.append(SDS(ins[i][0].shape, F32))
            out_specs.append(ins[i][1])
    aliases = {n_in + n_ct + n_add + k: didx.index(i) for k, (i, _) in enumerate(intos)}
    in_specs = [s for _, s in ins] + [s for _, s in cts] + [s for _, (_, s) in adds] + [ANY] * n_into
    operands = [a for a, _ in ins] + [a for a, _ in cts] + [a for _, (a, _) in adds] + [a for _, a in intos]
    return hosted_call(kern, name=name, grid=grid, in_specs=in_specs, out_specs=out_specs, out_shape=out_shape, operands=operands,
                       aliases=aliases, plans=plans)


def rspec(tl, w, cb=0):
    return pl.BlockSpec((tl, w), lambda i: (i, cb))


def cspec(shape):
    return pl.BlockSpec(shape, lambda i: (0,) * len(shape))


def row_fwd(name, fn, rows, tl, row_ins, consts, outs):
    ins = [(a, rspec(tl, w, cb)) for a, w, cb in row_ins] + [(a, cspec(a.shape)) for a in consts]
    return stage(name, fn, (rows // tl,), ins, [(SDS((rows, w), dt), rspec(tl, w)) for w, dt in outs])


def row_bwd(name, fn, rows, tl, row_ins, consts, cts, row_diff, const_diff, plans=()):
    ins = [(a, rspec(tl, w, cb)) for a, w, cb in row_ins] + [(a, cspec(a.shape)) for a in consts]
    diffs = []
    for (a, w, cb), d in zip(row_ins, row_diff):
        if not d:
            diffs.append(None)
            continue
        d = d if isinstance(d, dict) else {}
        opts = {}
        if "add" in d:
            opts["add"] = (d["add"][0], rspec(tl, d["add"][1], d["add"][2]))
        if d.get("into") is not None:
            opts["into"] = d["into"]
        diffs.append(("row", SDS((rows, d.get("cols", w)), d.get("dtype", F32)), rspec(tl, w, d.get("cb", 0)), opts))
    diffs += [("acc", (0,)) if d else None for d in const_diff]
    return stage_bwd(name, fn, (rows // tl,), ins, [(a, rspec(tl, w, cb)) for a, w, cb in cts], diffs, plans=plans)


MATMUL_VMEM = 36 * 2**20
ADAMW_VMEM = 28 * 2**20


class Sharded:
    def __init__(self, arr, kind, roff, rows):
        self.arr, self.kind, self.roff, self.rows, self.n = arr, kind, roff, rows, arr.shape[2]
        self.shape = (rows, 4 * self.n) if kind == "col" else (4 * rows, self.n)

    def fits(self, t0, t1):
        return self.roff % t0 == 0 and self.rows % t0 == 0 and self.n % t1 == 0

    def spec(self, t0, t1, bidx):
        assert self.fits(t0, t1), (self.kind, self.roff, self.rows, self.n, t0, t1)
        r0 = self.roff // t0
        if self.kind == "col":
            per = self.n // t1
            return pl.BlockSpec((None, t0, t1), lambda *g: (bidx(*g)[1] // per, r0 + bidx(*g)[0], bidx(*g)[1] % per))
        per = self.rows // t0
        return pl.BlockSpec((None, t0, t1), lambda *g: (bidx(*g)[0] // per, r0 + bidx(*g)[0] % per, bidx(*g)[1]))


def matmul(name, a, b, mode, out_dtype=BF16, add=None, out=None, into=None, plans=()):
    if mode == "tn":
        k_dim, m = a.shape
    else:
        m, k_dim = a.shape
    n = b.shape[0] if mode == "nt" else b.shape[1]
    b_fit = b.fits if isinstance(b, Sharded) else (lambda t0, t1: True)
    o_fit = out.fits if out is not None else (lambda t0, t1: True)
    a_bytes, b_bytes = jnp.dtype(a.dtype).itemsize, jnp.dtype(b.arr.dtype if isinstance(b, Sharded) else b.dtype).itemsize
    o_bytes = jnp.dtype(out_dtype if out is None else out.arr.dtype).itemsize

    def vmem(tm, tn, tk):
        return 2 * (tm * tk * a_bytes + tk * tn * b_bytes + tm * tn * (o_bytes + (4 if add is not None else 0))) + 4 * tm * tn * (1 + (tk < k_dim))

    tiles = [(tm, tn, tk) for tm in (2048, 1024, 512, 256, 128) for tn in (1024, 768, 512, 384, 256, 128)
             for tk in sorted({k_dim, 1024, 768, 512, 384, 256, 128})
             if m % tm == 0 and n % tn == 0 and k_dim % tk == 0 and (b_fit(tn, tk) if mode == "nt" else b_fit(tk, tn)) and o_fit(tm, tn)
             and vmem(tm, tn, tk) <= MATMUL_VMEM]
    tm, tn, tk = max(tiles, key=lambda t: (t[2] == k_dim, t[0] * t[1] * t[2], t[0] * t[1]))
    nk = k_dim // tk
    a_spec = pl.BlockSpec((tk, tm), lambda i, j, k: (k, i)) if mode == "tn" else pl.BlockSpec((tm, tk), lambda i, j, k: (i, k))
    if isinstance(b, Sharded):
        b_spec = b.spec(tn, tk, lambda i, j, k: (j, k)) if mode == "nt" else b.spec(tk, tn, lambda i, j, k: (k, j))
        b = b.arr
    else:
        b_spec = pl.BlockSpec((tn, tk), lambda i, j, k: (j, k)) if mode == "nt" else pl.BlockSpec((tk, tn), lambda i, j, k: (k, j))
    o_spec = pl.BlockSpec((tm, tn), lambda i, j, k: (i, j))
    out_spec, out_shape = (o_spec, SDS((m, n), out_dtype)) if out is None else (out.spec(tm, tn, lambda i, j, k: (i, j)), out.arr)
    ca, cb = {"nn": (1, 0), "nt": (1, 1), "tn": (0, 0)}[mode]
    n_in = 2 + (add is not None)

    def finish(refs, o_ref, r):
        if add is not None:
            r = r + refs[2][...]
        o_ref[...] = r.astype(o_ref.dtype)

    def kern_whole(*refs):
        finish(refs, refs[-1], _dg(refs[0][...], refs[1][...], ca, cb))

    def kern_cut(*refs):
        o_ref, acc = refs[-2], refs[-1]
        k = pl.program_id(2)

        @pl.when(k == 0)
        def _():
            acc[...] = jnp.zeros_like(acc)

        acc[...] += _dg(refs[0][...], refs[1][...], ca, cb)

        @pl.when(k == nk - 1)
        def _():
            finish(refs, o_ref, acc[...])

    ins, specs = [a, b], [a_spec, b_spec]
    if add is not None:
        ins.append(add)
        specs.append(o_spec)
    if into is not None:
        ins.append(into)
        specs.append(ANY)
    return hosted_call(kern_whole if nk == 1 else kern_cut, name=name, grid=(m // tm, n // tn, nk), in_specs=specs, out_specs=[out_spec],
                       out_shape=[out_shape], operands=ins, scratch_shapes=[] if nk == 1 else [pltpu.VMEM((tm, tn), F32)],
                       aliases={} if into is None else {n_in: 0}, plans=plans)[0]


def _cmul(ar, ai, br, bi):
    return ar * br - ai * bi, ar * bi + ai * br


def _sub_shift(x, down):
    row = lax.broadcasted_iota(jnp.int32, x.shape, 0)
    if down:
        return jnp.where(row == 0, 0.0, pltpu.roll(x, 1, 0))
    return jnp.where(row == SEG - 1, 0.0, pltpu.roll(x, SEG - 1, 0))


def _pow_seg_len(ar, ai):
    for _ in range(int(math.log2(SEG_LEN))):
        ar, ai = _cmul(ar, ai, ar, ai)
    return ar, ai


def _scan_in_place(sr, si, a_re, a_im):
    lanes = sr.shape[1]
    ar = jnp.broadcast_to(a_re, (SEG, lanes))
    ai = jnp.broadcast_to(a_im, (SEG, lanes))
    zero = jnp.zeros((SEG, lanes), F32)

    def local(i, carry):
        rows = pl.ds(pl.multiple_of(i * SEG, SEG), SEG)
        mr, mi = _cmul(ar, ai, carry[0], carry[1])
        nr, ni = mr + sr[rows, :], mi + si[rows, :]
        sr[rows, :] = nr
        si[rows, :] = ni
        return nr, ni

    fr, fi = lax.fori_loop(0, SEG_LEN, local, (zero, zero))
    pr, pi = _pow_seg_len(ar, ai)
    ir, ii = zero, zero
    for _ in range(SEG - 1):
        mr, mi = _cmul(pr, pi, ir, ii)
        ir, ii = _sub_shift(mr + fr, True), _sub_shift(mi + fi, True)

    def carry_in(i, pw):
        rows = pl.ds(pl.multiple_of(i * SEG, SEG), SEG)
        cr, ci = _cmul(pw[0], pw[1], ir, ii)
        sr[rows, :] += cr
        si[rows, :] += ci
        return _cmul(pw[0], pw[1], ar, ai)

    lax.fori_loop(0, SEG_LEN, carry_in, (ar, ai))


S5_LANES = 8 * SP
S5_BLOCKS = SN // S5_LANES


def _s5_specs():
    u_spec = pl.BlockSpec((L, 8 * SC), lambda j: (0, j))
    s_spec = pl.BlockSpec((L, S5_LANES), lambda j: (0, j))
    wb_spec = pl.BlockSpec((S5_LANES, 8 * SC), lambda j: (j, 0))
    wc_spec = pl.BlockSpec((8 * SC, S5_LANES), lambda j: (j, 0))
    a_spec = pl.BlockSpec((1, S5_LANES), lambda j: (0, j))
    d_spec = pl.BlockSpec((1, 8 * SC), lambda j: (0, j))
    return u_spec, s_spec, wb_spec, wc_spec, a_spec, d_spec


def s5_forward(proj, wb_re, wb_im, wc_re, wc_im, a_re, a_im, d, plans=()):
    def kern(u_ref, wbr, wbi, wcr, wci, ar, ai, d_ref, sr_out, si_out, g_ref, sr, si):
        u = _wide(u_ref[...])
        sr[...], si[...] = fn_s5_bu(u, wbr[...], wbi[...])
        _scan_in_place(sr, si, ar[...], ai[...])
        g_ref[...] = fn_s5_out(sr[...], si[...], u, d_ref[...], wcr[...], wci[...])[0].astype(g_ref.dtype)
        sr_out[...] = sr[...].astype(sr_out.dtype)
        si_out[...] = si[...].astype(si_out.dtype)

    u_spec, s_spec, wb_spec, wc_spec, a_spec, d_spec = _s5_specs()
    return hosted_call(kern, name="s5_forward", grid=(S5_BLOCKS,), in_specs=[u_spec, wb_spec, wb_spec, wc_spec, wc_spec, a_spec, a_spec, d_spec],
                       out_specs=[s_spec, s_spec, u_spec], out_shape=[SDS((L, SN), BF16)] * 2 + [SDS((L, PW), BF16)],
                       operands=[proj, wb_re, wb_im, wc_re, wc_im, a_re, a_im, d], scratch_shapes=[pltpu.VMEM((L, S5_LANES), F32)] * 2,
                       plans=plans)


def _adjoint_scan_in_place(lr, li, sr, si, a_re, a_im):
    lanes = lr.shape[1]
    ar = jnp.broadcast_to(a_re, (SEG, lanes))
    ai = -jnp.broadcast_to(a_im, (SEG, lanes))
    zero = jnp.zeros((SEG, lanes), F32)

    def local(k, carry):
        i = SEG_LEN - 1 - k
        rows = pl.ds(pl.multiple_of(i * SEG, SEG), SEG)
        mr, mi = _cmul(ar, ai, carry[0], carry[1])
        nr, ni = mr + lr[rows, :], mi + li[rows, :]
        lr[rows, :] = nr
        li[rows, :] = ni
        return nr, ni

    fr, fi = lax.fori_loop(0, SEG_LEN, local, (zero, zero))
    pr, pi = _pow_seg_len(ar, ai)
    ir, ii = zero, zero
    for _ in range(SEG - 1):
        mr, mi = _cmul(pr, pi, ir, ii)
        ir, ii = _sub_shift(mr + fr, False), _sub_shift(mi + fi, False)

    def fix(rows, pw):
        cr, ci = _cmul(pw[0], pw[1], ir, ii)
        tr, ti = lr[rows, :] + cr, li[rows, :] + ci
        lr[rows, :] = tr
        li[rows, :] = ti
        return tr, ti

    def grad_a(tr, ti, spr, spi, acc):
        return acc[0] + tr * spr + ti * spi, acc[1] + ti * spr - tr * spi

    def carry_in(k, c):
        i = SEG_LEN - 1 - k
        rows = pl.ds(pl.multiple_of(i * SEG, SEG), SEG)
        prev = pl.ds(pl.multiple_of((i - 1) * SEG, SEG), SEG)
        tr, ti = fix(rows, (c[0], c[1]))
        acc = grad_a(tr, ti, sr[prev, :], si[prev, :], (c[2], c[3]))
        nr, ni = _cmul(c[0], c[1], ar, ai)
        return nr, ni, acc[0], acc[1]

    pwr, pwi, accr, acci = lax.fori_loop(0, SEG_LEN - 1, carry_in, (ar, ai, zero, zero))
    tr, ti = fix(pl.ds(0, SEG), (pwr, pwi))
    last = pl.ds((SEG_LEN - 1) * SEG, SEG)
    accr, acci = grad_a(tr, ti, _sub_shift(sr[last, :], True), _sub_shift(si[last, :], True), (accr, acci))
    return jnp.sum(accr, axis=0, keepdims=True), jnp.sum(acci, axis=0, keepdims=True)


S5_BWD_VMEM = 58 * 2**20


def s5_backward(dg, proj, s_re, s_im, wb_re, wb_im, wc_re, wc_im, a_re, a_im, d, dproj, plans=()):
    def kern(dg_ref, u_ref, sr_in, si_in, wbr, wbi, wcr, wci, ar, ai, d_ref, _, du_ref, dd_ref, dwcr, dwci, dwbr, dwbi, dar, dai, lr, li, sr, si):
        u = _wide(u_ref[...])
        sr[...], si[...] = _wide(sr_in[...]), _wide(si_in[...])
        _, vjp_out = jax.vjp(fn_s5_out, sr[...], si[...], u, d_ref[...], wcr[...], wci[...])
        lr[...], li[...], du_out, dd_ref[...], dwcr[...], dwci[...] = vjp_out((_wide(dg_ref[...]),))
        dar[...], dai[...] = _adjoint_scan_in_place(lr, li, sr, si, ar[...], ai[...])
        _, vjp_in = jax.vjp(fn_s5_bu, u, wbr[...], wbi[...])
        du_in, dwbr[...], dwbi[...] = vjp_in((lr[...], li[...]))
        du_ref[...] = (du_out + du_in).astype(du_ref.dtype)

    u_spec, s_spec, wb_spec, wc_spec, a_spec, d_spec = _s5_specs()
    outs = [(SDS(dproj.shape, dproj.dtype), u_spec), (SDS(d.shape, F32), d_spec), (SDS(wc_re.shape, F32), wc_spec), (SDS(wc_im.shape, F32), wc_spec),
            (SDS(wb_re.shape, F32), wb_spec), (SDS(wb_im.shape, F32), wb_spec), (SDS(a_re.shape, F32), a_spec), (SDS(a_im.shape, F32), a_spec)]
    return hosted_call(kern, name="s5_backward", grid=(S5_BLOCKS,),
                       in_specs=[u_spec, u_spec, s_spec, s_spec, wb_spec, wb_spec, wc_spec, wc_spec, a_spec, a_spec, d_spec, ANY],
                       out_specs=[sp for _, sp in outs], out_shape=[sd for sd, _ in outs], aliases={11: 0},
                       operands=[dg, proj, s_re, s_im, wb_re, wb_im, wc_re, wc_im, a_re, a_im, d, dproj],
                       scratch_shapes=[pltpu.VMEM((L, S5_LANES), F32)] * 4,
                       cparams=pltpu.CompilerParams(vmem_limit_bytes=S5_BWD_VMEM), plans=plans)


def fn_rms(x, g):
    return (rms(x, g),)


def fn_s5_disc(lre, lim, ls):
    step = jnp.exp(ls)
    e = jnp.exp(lre * step)
    a_re, a_im = e * jnp.cos(lim * step), e * jnp.sin(lim * step)
    den = lre * lre + lim * lim
    nr, ni = a_re - 1.0, a_im
    return a_re, a_im, (nr * lre + ni * lim) / den, (ni * lre - nr * lim) / den


def _group_mask(rows, cols, row_div, col_div):
    r = lax.broadcasted_iota(jnp.int32, (rows, cols), 0) // row_div % 8
    c = lax.broadcasted_iota(jnp.int32, (rows, cols), 1) // col_div
    return r == c


def _spread(x, mask):
    w = x.shape[1]
    copy = (lax.broadcasted_iota(jnp.int32, (w, 8 * w), 1) % w == lax.broadcasted_iota(jnp.int32, (w, 8 * w), 0)).astype(F32)
    return jnp.where(mask, jnp.dot(x, copy, precision=lax.Precision.HIGHEST, preferred_element_type=F32), 0.0)


def fn_s5_bmat(b_re, b_im, coef_re, coef_im):
    mask = _group_mask(b_re.shape[0], 8 * SC, SP, SC)
    return _spread(coef_re * b_re - coef_im * b_im, mask), _spread(coef_re * b_im + coef_im * b_re, mask)


def fn_s5_cmat(c_re, c_im):
    mask = _group_mask(c_re.shape[0], 8 * SP, SC, SP)
    return _spread(c_re, mask), _spread(c_im, mask)


def fn_s5_bu(u, wb_re, wb_im):
    return mm_nt(u, wb_re), mm_nt(u, wb_im)


def fn_s5_out(sr, si, u, d, wc_re, wc_im):
    y = mm_nt(sr, wc_re) - mm_nt(si, wc_im) + d * u
    return (jax.nn.gelu(y),)


def fn_merge_glu(z, mo, gate):
    yg = z[:, :PW] * jax.nn.sigmoid(z[:, PW:])
    return (jnp.concatenate([yg, mo], axis=1) * silu(gate),)


def fn_merge(prim, mo, gate):
    return (jnp.concatenate([prim, mo], axis=1) * silu(gate),)


def fn_mem_k(kv, g):
    return (jnp.concatenate([rms(kv[:, h * XHD:(h + 1) * XHD], g) for h in range(XH)], axis=1),)


def fn_mem_attn(xq, kn, v, g):
    outs = []
    for h in range(XH):
        sl = slice(h * XHD, (h + 1) * XHD)
        p = softmax_rows(mm_nt(rms(xq[:, sl], g), kn[:, sl]) * (XHD ** -0.5))
        outs.append(mm_nn(p, v[:, sl]))
    return (jnp.concatenate(outs, axis=1),)


def _half_rms(x, g):
    lo = lax.broadcasted_iota(jnp.int32, x.shape, 1) < ROPE
    x2 = x * x
    s_lo = jnp.sum(jnp.where(lo, x2, 0.0), axis=1, keepdims=True)
    s_hi = jnp.sum(jnp.where(lo, 0.0, x2), axis=1, keepdims=True)
    return x * lax.rsqrt(jnp.where(lo, s_lo, s_hi) / ROPE + EPS) * g


def _rope(x, cos2, sin_signed):
    first = lax.broadcasted_iota(jnp.int32, x.shape, 1) % ROPE < ROPE // 2
    return x * cos2 + jnp.where(first, lane_roll(x, 128 - ROPE // 2), lane_roll(x, ROPE // 2)) * sin_signed


def fn_mla_prep(q, kv, kr, cos2, sin_signed, qnn, knn, qrn, krn):
    lo = lax.broadcasted_iota(jnp.int32, kr.shape, 1) < ROPE
    kr_pad = jnp.where(lo, _rope(_half_rms(kr, krn), cos2, sin_signed), 0.0)
    qf, kf, vs = [], [], []
    for m in range(MH // 2):
        pair = _rope(_half_rms(q[:, MH * NOPE + 128 * m:MH * NOPE + 128 * (m + 1)], qrn), cos2, sin_signed)
        for h, rope_h in ((2 * m, pair), (2 * m + 1, lane_roll(pair, ROPE))):
            qf.append(jnp.concatenate([rms(q[:, NOPE * h:NOPE * (h + 1)], qnn), jnp.where(lo, rope_h, 0.0)], axis=1))
    for h in range(MH):
        kf.append(jnp.concatenate([rms(kv[:, 256 * h:256 * h + NOPE], knn), kr_pad], axis=1))
        vs.append(kv[:, 256 * h + NOPE:256 * (h + 1)])
    return jnp.stack(qf), jnp.stack(kf), jnp.stack(vs)


ATT_TQ = 512


def _attn_tile(q, kf, v):
    tq = q.shape[0]
    scale = (NOPE + ROPE) ** -0.5
    own = mm_nt(q, kf[-tq:]) * scale
    own = jnp.where(lax.broadcasted_iota(jnp.int32, own.shape, 1) <= lax.broadcasted_iota(jnp.int32, own.shape, 0), own, jnp.finfo(F32).min)
    s = own if kf.shape[0] == tq else jnp.concatenate([mm_nt(q, kf[:-tq]) * scale, own], axis=1)
    return mm_nn(softmax_rows(s), v)


def _attn_specs():
    q_spec = pl.BlockSpec((None, ATT_TQ, 256), lambda h, i: (h, i, 0))
    k_spec = pl.BlockSpec((None, L, 256), lambda h, i: (h, 0, 0))
    v_spec = pl.BlockSpec((None, L, 128), lambda h, i: (h, 0, 0))
    o_spec = pl.BlockSpec((ATT_TQ, 128), lambda h, i: (i, h))
    return q_spec, k_spec, v_spec, o_spec


def causal_attn(qf, kf, vh):
    n_tiles = L // ATT_TQ

    def kern(q_ref, k_ref, v_ref, o_ref):
        i = pl.program_id(1)
        for t in range(n_tiles):
            @pl.when(i == t)
            def _(t=t):
                keys = (t + 1) * ATT_TQ
                o_ref[...] = _attn_tile(q_ref[...], k_ref[:keys, :], v_ref[:keys, :]).astype(o_ref.dtype)

    q_spec, k_spec, v_spec, o_spec = _attn_specs()
    return pl.pallas_call(kern, grid=(MH, n_tiles), in_specs=[q_spec, k_spec, v_spec], out_specs=o_spec,
                          out_shape=SDS((L, MH * VD), BF16), name="l1_attn", compiler_params=_cparams())(qf, kf, vh)


def causal_attn_bwd(qf, kf, vh, dout):
    n_tiles = L // ATT_TQ

    def kern(q_ref, k_ref, v_ref, do_ref, dq_ref, dk_ref, dv_ref):
        i = pl.program_id(1)

        @pl.when(i == 0)
        def _():
            dk_ref[...] = jnp.zeros_like(dk_ref)
            dv_ref[...] = jnp.zeros_like(dv_ref)

        for t in range(n_tiles):
            @pl.when(i == t)
            def _(t=t):
                keys = (t + 1) * ATT_TQ
                _, vjp = jax.vjp(_attn_tile, q_ref[...].astype(F32), k_ref[:keys, :].astype(F32), v_ref[:keys, :].astype(F32))
                dq, dk, dv = vjp(do_ref[...])
                dq_ref[...] = dq
                dk_ref[:keys, :] += dk
                dv_ref[:keys, :] += dv

    q_spec, k_spec, v_spec, o_spec = _attn_specs()
    return pl.pallas_call(kern, grid=(MH, n_tiles), in_specs=[q_spec, k_spec, v_spec, o_spec], out_specs=[q_spec, k_spec, v_spec],
                          out_shape=[SDS(qf.shape, F32), SDS(kf.shape, F32), SDS(vh.shape, F32)], name="l1_attn_bwd",
                          compiler_params=_cparams())(qf, kf, vh, dout)


def loss_and_grad(y, target, tl=256):
    def kern(y_ref, t_ref, dy_ref, loss_ref):
        d = y_ref[...] - t_ref[...]
        dy_ref[...] = d / D

        @pl.when(pl.program_id(0) == 0)
        def _():
            loss_ref[...] = jnp.zeros_like(loss_ref)

        loss_ref[...] += 0.5 * jnp.sum(jnp.sum(d * d, axis=1, keepdims=True), axis=0, keepdims=True) / D

    return pl.pallas_call(kern, grid=(L // tl,), in_specs=[rspec(tl, D), rspec(tl, D)], out_specs=[rspec(tl, D), cspec((1, 1))],
                          out_shape=[SDS((L, D), F32), SDS((1, 1), F32)], name="loss", compiler_params=_cparams())(y, target)


def adamw(name, w, g, m, v):
    rows, cols = w.shape
    block_row_bytes = 7 * 2 * 4 * max(cols, 128)
    tr = _row_tile(rows, min(2048, ADAMW_VMEM // block_row_bytes // 8 * 8), 8)

    def kern(w_ref, g_ref, m_ref, v_ref, d_ref, nm_ref, nv_ref):
        gg = g_ref[...]
        nm = ADAM_B1 * m_ref[...] + (1.0 - ADAM_B1) * gg
        nv = ADAM_B2 * v_ref[...] + (1.0 - ADAM_B2) * jnp.square(gg)
        m_hat = nm / (1.0 - ADAM_B1 ** ADAM_STEP)
        v_hat = nv / (1.0 - ADAM_B2 ** ADAM_STEP)
        d_ref[...] = -ADAM_LR * (m_hat / (jnp.sqrt(v_hat) + ADAM_EPS) + ADAM_WD * w_ref[...])
        nm_ref[...] = nm
        nv_ref[...] = nv

    spec = rspec(tr, cols)
    return hosted_call(kern, name=name, grid=(rows // tr,), in_specs=[spec] * 4, out_specs=[spec] * 3,
                       out_shape=[SDS((rows, cols), F32)] * 3, operands=[w, g, m, v])


def _row_tile(rows, cap=512, unit=16):
    return max(t for t in range(unit, cap + 1, unit) if rows % t == 0)


def _place():
    x, y, c = lax.axis_index("x"), lax.axis_index("y"), lax.axis_index("c")
    return x, y, c, [(1 - x, y), (x, 1 - y), (1 - x, 1 - y)]


def _row_chunks(rows, n, dtype):
    unit = 32 // jnp.dtype(dtype).itemsize
    base, extra = divmod(rows // unit, n)
    out, start = [], 0
    for k in range(n):
        size = (base + (k < extra)) * unit
        if size:
            out.append((start, size))
            start += size
    assert start == rows, (rows, unit)
    return out


PIECE_BYTES = 1 << 20


def _pieces(shapes_dtypes, rows_of):
    out = []
    for b, (shape, dtype) in enumerate(shapes_dtypes):
        rows = rows_of(shape)
        n = max(1, min(4, rows * shape[-1] * jnp.dtype(dtype).itemsize // PIECE_BYTES))
        out += [(b, st, sz) for st, sz in _row_chunks(rows, n, dtype)]
    return out


def all_gather_chips(name, shards):
    nb = len(shards)
    pieces = _pieces([(s.shape, s.dtype) for s in shards], lambda shape: shape[0] // 2)
    n = len(pieces)

    def body(*refs):
        x_refs, out_refs, send_sems, recv_sems = refs[:nb], refs[nb:2 * nb], refs[2 * nb], refs[2 * nb + 1]
        x, y, c, chips = _place()
        sibling = (x, y, 1 - c)
        mine = 2 * x + y

        def copy(sem, chip, cc, k, to, from_input=False):
            b, st, sz = pieces[k]
            rows_k = pl.ds(cc * (x_refs[b].shape[0] // 2) + st, sz)
            dst = out_refs[b].at[chip, rows_k, :]
            return pltpu.make_async_remote_copy(src_ref=x_refs[b].at[rows_k, :] if from_input else dst, dst_ref=dst,
                                                send_sem=send_sems.at[sem], recv_sem=recv_sems.at[sem], device_id=to, device_id_type=MESH_ID)

        order = [(k, j, 2 * cx + cy, (cx, cy, c)) for k in range(n) for j, (cx, cy) in enumerate(chips)]
        first = [copy(j * n + k, mine, c, k, to, from_input=True) for k, j, _, to in order]
        for cp in first:
            cp.start()
        passed = []
        for k, j, chip, _ in order:
            copy(j * n + k, chip, c, k, sibling).wait_recv()
            passed.append(copy((3 + j) * n + k, chip, c, k, sibling))
            passed[-1].start()
        for k, j, chip, _ in order:
            copy((3 + j) * n + k, chip, 1 - c, k, sibling).wait_recv()
        for cp in first + passed:
            cp.wait_send()

    return pl.pallas_call(body, in_specs=[ANY] * nb, out_specs=[ANY] * nb, out_shape=[SDS((4,) + s.shape, s.dtype) for s in shards],
                          scratch_shapes=_dma_sems(6 * n), name=name)(*shards)


def plan_gather_ici(shards):
    pieces = _pieces([(s.shape, s.dtype) for s in shards], lambda shape: shape[0] // 2)
    n = len(pieces)

    def copies(x_refs, out_refs, send_sems, recv_sems):
        x, y, c, chips = _place()
        mine = 2 * x + y

        def copy(j, k, chip, to, from_input):
            b, st, sz = pieces[k]
            rows_k = pl.ds(c * (x_refs[b].shape[0] // 2) + st, sz)
            dst = out_refs[b].at[chip, rows_k, :]
            return pltpu.make_async_remote_copy(src_ref=x_refs[b].at[rows_k, :] if from_input else dst, dst_ref=dst, send_sem=send_sems.at[j * n + k],
                                                recv_sem=recv_sems.at[j * n + k], device_id=to, device_id_type=MESH_ID)

        order = [(k, j, 2 * cx + cy, (cx, cy, c)) for k in range(n) for j, (cx, cy) in enumerate(chips)]
        return [copy(j, k, mine, to, True) for k, j, _, to in order], [copy(j, k, chip, to, False) for k, j, chip, to in order]

    return Plan(shards, [SDS((4,) + s.shape, s.dtype) for s in shards], {}, 3 * n, copies)


def plan_gather_pass(gathered):
    pieces = _pieces([(g.shape[1:], g.dtype) for g in gathered], lambda shape: shape[0] // 2)
    n = len(pieces)

    def copies(_, out_refs, send_sems, recv_sems):
        x, y, c, chips = _place()

        def copy(j, k, chip, cc):
            b, st, sz = pieces[k]
            rows_k = out_refs[b].at[chip, pl.ds(cc * (out_refs[b].shape[1] // 2) + st, sz), :]
            return pltpu.make_async_remote_copy(src_ref=rows_k, dst_ref=rows_k, send_sem=send_sems.at[j * n + k], recv_sem=recv_sems.at[j * n + k],
                                                device_id=(x, y, 1 - c), device_id_type=MESH_ID)

        order = [(k, j, 2 * cx + cy) for k in range(n) for j, (cx, cy) in enumerate(chips)]
        return [copy(j, k, chip, c) for k, j, chip in order], [copy(j, k, chip, 1 - c) for k, j, chip in order]

    return Plan(gathered, [SDS(g.shape, g.dtype) for g in gathered], {i: i for i in range(len(gathered))}, 3 * n, copies)


def plan_pair_exchange(gs):
    pieces = _pieces([(g.shape, g.dtype) for g in gs], lambda shape: shape[1] // 2)

    def copies(g_refs, got_refs, send_sems, recv_sems):
        x, y, c, _ = _place()
        swaps = [pltpu.make_async_remote_copy(src_ref=g_refs[b].at[:, pl.ds((1 - c) * (g_refs[b].shape[1] // 2) + st, sz), :],
                                              dst_ref=got_refs[b].at[:, pl.ds(st, sz), :], send_sem=send_sems.at[k], recv_sem=recv_sems.at[k],
                                              device_id=(x, y, 1 - c), device_id_type=MESH_ID)
                 for k, (b, st, sz) in enumerate(pieces)]
        return swaps, swaps

    return Plan(gs, [SDS((g.shape[0], g.shape[1] // 2, g.shape[2]), g.dtype) for g in gs], {}, len(pieces), copies)


def plan_chip_scatter(ps):
    pieces = _pieces([(p.shape, p.dtype) for p in ps], lambda shape: shape[1])
    n = len(pieces)

    def copies(p_refs, q_refs, send_sems, recv_sems):
        x, y, c, chips = _place()
        mine = 2 * x + y

        def copy(j, k, src_slot, dst_slot, to):
            b, st, sz = pieces[k]
            return pltpu.make_async_remote_copy(src_ref=p_refs[b].at[src_slot, pl.ds(st, sz), :], dst_ref=q_refs[b].at[dst_slot, pl.ds(st, sz), :],
                                                send_sem=send_sems.at[j * n + k], recv_sem=recv_sems.at[j * n + k], device_id=to,
                                                device_id_type=MESH_ID)

        order = [(k, j, 2 * cx + cy, (cx, cy, c)) for k in range(n) for j, (cx, cy) in enumerate(chips)]
        return [copy(j, k, chip, mine, to) for k, j, chip, to in order], [copy(j, k, mine, chip, to) for k, j, chip, to in order]

    return Plan(ps, [SDS(p.shape, p.dtype) for p in ps], {}, 3 * n, copies)


def plan_pair_join(bufs):
    pieces = _pieces([(b.shape, b.dtype) for b in bufs], lambda shape: shape[0] // 2)

    def copies(_, out_refs, send_sems, recv_sems):
        x, y, c, _ = _place()

        def copy(k, cc):
            b, st, sz = pieces[k]
            rows_k = out_refs[b].at[pl.ds(cc * (out_refs[b].shape[0] // 2) + st, sz), :]
            return pltpu.make_async_remote_copy(src_ref=rows_k, dst_ref=rows_k, send_sem=send_sems.at[k], recv_sem=recv_sems.at[k],
                                                device_id=(x, y, 1 - c), device_id_type=MESH_ID)

        return [copy(k, c) for k in range(len(pieces))], [copy(k, 1 - c) for k in range(len(pieces))]

    return Plan(bufs, [SDS(b.shape, b.dtype) for b in bufs], {i: i for i in range(len(bufs))}, len(pieces), copies)


def run_plan(name, plan):
    hosted_call(lambda: None, name=name, grid=(1,), in_specs=[], out_specs=[], out_shape=[], operands=[], plans=[plan])
    return plan.results


HBM = pl.BlockSpec(memory_space=pltpu.HBM)
SEMS = pl.BlockSpec(memory_space=pltpu.SEMAPHORE)
SPLIT_PARAMS = dict(has_side_effects=pltpu.SideEffectType.DATAFLOW_SIDE_EFFECTING)


def _plan_buffers(plan):
    in_place = {o: i for i, o in plan.aliases.items()}
    bufs = [pltpu.with_memory_space_constraint(a, pltpu.HBM) for a in plan.operands]
    where = []
    for o, sd in enumerate(plan.out_shape):
        if o in in_place:
            where.append(in_place[o])
        else:
            where.append(len(bufs))
            bufs.append(pltpu.with_memory_space_constraint(lax.empty(sd.shape, sd.dtype), pltpu.HBM))
    return bufs, where


def split_start(name, plans):
    layout = [_plan_buffers(p) for p in plans]
    counts = [len(b) for b, _ in layout]
    n_buf = sum(counts)

    def body(*refs):
        sems, token = refs[n_buf:n_buf + 2 * len(plans)], refs[-1]
        pos = 0
        for k, (p, (_, where)) in enumerate(zip(plans, layout)):
            mine = refs[pos:pos + counts[k]]
            pos += counts[k]
            sends, _ = p.copies(mine[:len(p.operands)], [mine[w] for w in where], sems[2 * k], sems[2 * k + 1])
            for cp in sends:
                cp.start()
        token[...] = jnp.zeros_like(token)

    bufs = [b for bs, _ in layout for b in bs]
    res = pl.pallas_call(
        body, name=name, in_specs=[HBM] * n_buf,
        out_specs=[SEMS] * (2 * len(plans)) + [HBM] * n_buf + [pl.BlockSpec(memory_space=pltpu.VMEM)],
        out_shape=[pltpu.SemaphoreType.DMA((p.n_sems,)) for p in plans for _ in range(2)] + [pltpu.HBM(b.shape, b.dtype) for b in bufs]
        + [SDS((8, 128), F32)],
        input_output_aliases={i: 2 * len(plans) + i for i in range(n_buf)}, compiler_params=pltpu.CompilerParams(**SPLIT_PARAMS))(*bufs)
    pos = 2 * len(plans)
    for k, p in enumerate(plans):
        p.in_flight = (res[2 * k], res[2 * k + 1], list(res[pos:pos + counts[k]]), layout[k][1])
        pos += counts[k]
    return res[-1]


def split_wait(name, plan, after):
    send_sems, recv_sems, bufs, where = plan.in_flight
    n_buf = len(bufs)

    def body(*refs):
        mine = refs[:n_buf]
        sends, recvs = plan.copies(mine[:len(plan.operands)], [mine[w] for w in where], refs[n_buf], refs[n_buf + 1])
        for cp in recvs:
            cp.wait_recv()
        for cp in sends:
            cp.wait_send()

    res = pl.pallas_call(body, name=name, in_specs=[HBM] * n_buf + [SEMS, SEMS, ANY], out_specs=[HBM] * n_buf,
                         out_shape=[pltpu.HBM(b.shape, b.dtype) for b in bufs], input_output_aliases={i: i for i in range(n_buf)},
                         compiler_params=pltpu.CompilerParams(**SPLIT_PARAMS))(*bufs, send_sems, recv_sems, after)
    plan.results = [res[w] for w in where]
    return plan.results


def pair_add(name, g, got, place):
    slots, rows, cols = g.shape
    half = rows // 2
    tr = _row_tile(half)
    nb = half // tr

    def kern(_, g_ref, t_ref, o_ref):
        o_ref[...] = (g_ref[...].astype(F32) + t_ref[...].astype(F32)).astype(o_ref.dtype)

    blk = pl.BlockSpec((None, tr, cols), lambda s, i, p: (s, i, 0))
    grid_spec = pltpu.PrefetchScalarGridSpec(
        num_scalar_prefetch=1, grid=(slots, nb),
        in_specs=[pl.BlockSpec((None, tr, cols), lambda s, i, p: (s, p[1] * nb + i, 0)), blk], out_specs=blk)
    return pl.pallas_call(kern, grid_spec=grid_spec, out_shape=SDS((slots, half, cols), g.dtype), name=name,
                          compiler_params=_cparams())(place, g, got)


def chip_add(name, p, q, place):
    slots, half, cols = p.shape
    tr = _row_tile(half)
    nb = half // tr

    def kern(_, p_ref, q1, q2, q3, o_ref):
        o_ref[...] = p_ref[...].astype(F32) + q1[...].astype(F32) + q2[...].astype(F32) + q3[...].astype(F32)

    def slot(k):
        return pl.BlockSpec((None, tr, cols), lambda i, pr: ((pr[0] + k) % slots, i, 0))

    grid_spec = pltpu.PrefetchScalarGridSpec(
        num_scalar_prefetch=1, grid=(nb,), in_specs=[slot(0), slot(1), slot(2), slot(3)],
        out_specs=pl.BlockSpec((tr, cols), lambda i, pr: (pr[1] * nb + i, 0)))
    return pl.pallas_call(kern, grid_spec=grid_spec, out_shape=SDS((2 * half, cols), F32), name=name,
                          compiler_params=_cparams())(place, p, q, q, q)


def pair_adds(tag, gs, gots, place):
    return [pair_add(f"{tag}_pair_add_{i}", g, got, place) for i, (g, got) in enumerate(zip(gs, gots))]


def chip_adds(tag, pairs, qs, place):
    return [chip_add(f"{tag}_chip_add_{i}", p, q, place) for i, (p, q) in enumerate(zip(pairs, qs))]


def reduce_scatter_chips(tag, gs, place):
    pairs = pair_adds(tag, gs, run_plan(tag + "_pair_exchange", plan_pair_exchange(gs)), place)
    return run_plan(tag + "_pair_join", plan_pair_join(chip_adds(tag, pairs, run_plan(tag + "_chip_scatter", plan_chip_scatter(pairs)), place)))


BIG = [("w_out", (2, 512, 1024)), ("w_mem_kv", (2, 256, 1024)), ("s5_w_in", (1, 1024, 1024)), ("s5_w_glu", (1, 1536, 768)),
       ("mla_w_in", (1, 1024, 848)), ("mla_w_uq", (1, 512, 576)), ("mla_w_ukv", (1, 256, 768))]
SHARDED_SMALL = [("mla_q_lora_norm", (1, 128)), ("mla_kv_lora_norm", (1, 64))]
SMALL = [("ln_gain", (2, 1024)), ("mem_norm", (2, 1024)), ("xq_norm", (2, 128)), ("xk_norm", (2, 128)),
         ("s5_lambda_re", (1, 96, 64)), ("s5_lambda_im", (1, 96, 64)), ("s5_log_step", (1, 96)),
         ("s5_b_re", (1, 96, 64, 16)), ("s5_b_im", (1, 96, 64, 16)), ("s5_c_re", (1, 96, 16, 64)), ("s5_c_im", (1, 96, 16, 64)),
         ("s5_d", (1, 1536)), ("mla_q_nope_norm", (1, 128)), ("mla_k_nope_norm", (1, 128)), ("mla_q_rope_norm", (1, 64)),
         ("mla_k_rope_norm", (1, 64))]
WEIGHT_ORDER = ["ln_gain", "w_out", "mem_norm", "w_mem_kv", "xq_norm", "xk_norm", "s5_w_in", "s5_lambda_re", "s5_lambda_im",
                "s5_log_step", "s5_b_re", "s5_b_im", "s5_c_re", "s5_c_im", "s5_d", "s5_w_glu", "mla_w_in", "mla_q_lora_norm",
                "mla_kv_lora_norm", "mla_w_uq", "mla_w_ukv", "mla_q_nope_norm", "mla_k_nope_norm", "mla_q_rope_norm", "mla_k_rope_norm"]
MINOR_LAST = {"mla_w_in": (0, 2, 1), "mla_w_uq": (0, 2, 1), "s5_b_re": (0, 2, 3, 1), "s5_b_im": (0, 2, 3, 1),
              "s5_c_re": (0, 2, 3, 1), "s5_c_im": (0, 2, 3, 1)}
SMALL_FULL = SMALL + [(n, (1, 4 * s[1])) for n, s in SHARDED_SMALL]
N_SMALL = sum(math.prod(s) for _, s in SMALL_FULL)
SMALL_ROWS, SMALL_LANES = 128, 1024

PAIR_OUT, PAIR_MKV, PAIR_ROWS = 0, 512, 768


def stack_shards(w, dtype):
    pairs = [jnp.concatenate([w["w_out"][l], w["w_mem_kv"][l]], axis=0).astype(dtype) for l in range(2)]
    return ([w["s5_w_in"][0].astype(dtype)], [pairs[0], w["s5_w_glu"][0].astype(dtype)],
            [pairs[1], w["mla_w_ukv"][0].astype(dtype), w["mla_w_in"][0].astype(dtype), w["mla_w_uq"][0].astype(dtype)])


def pair_views(pair):
    return {"w_out": Sharded(pair, "row", PAIR_OUT, 512), "w_mem_kv": Sharded(pair, "row", PAIR_MKV, 256)}


def grad_views():
    pair = SDS((4, PAIR_ROWS, 1024), BF16)
    return {"w_out": Sharded(pair, "row", PAIR_OUT, 512), "w_mem_kv": Sharded(pair, "row", PAIR_MKV, 256),
            "s5_w_in": Sharded(SDS((4, 1024, 1024), BF16), "col", 0, 1024), "s5_w_glu": Sharded(SDS((4, 1536, 768), BF16), "col", 0, 1536),
            "mla_w_ukv": Sharded(SDS((4, 256, 768), BF16), "col", 0, 256)}


def cols_to_shards(full):
    return full.reshape(full.shape[0], 4, full.shape[1] // 4).transpose(1, 0, 2)


def shards_to_cols(arr):
    return arr.transpose(1, 0, 2).reshape(arr.shape[1], 4 * arr.shape[2])


def mla_in_permute(w):
    o1, o2, o3, o4 = QL, QL + KVL, QL + KVL + ROPE, QL + KVL + ROPE + XQW
    return jnp.concatenate([w[:, o4:], w[:, :o1], w[:, o3:o4], w[:, o1:o2], w[:, o2:o3],
                            jnp.zeros((w.shape[0], MLA_IN_P - MLA_IN), w.dtype)], axis=1)


def mla_in_unpermute(d):
    return jnp.concatenate([d[:, 2048:2560], d[:, 3072:3328], d[:, 3328:3392], d[:, 2560:3072], d[:, :2048]], axis=1)


def uq_permute(w):
    w3 = w.reshape(w.shape[0], MH, NOPE + ROPE)
    return jnp.concatenate([w3[:, :, :NOPE].reshape(w.shape[0], MH * NOPE), w3[:, :, NOPE:].reshape(w.shape[0], MH * ROPE)], axis=1)


def uq_unpermute(d):
    dn = d[:, :MH * NOPE].reshape(d.shape[0], MH, NOPE)
    dr = d[:, MH * NOPE:].reshape(d.shape[0], MH, ROPE)
    return jnp.concatenate([dn, dr], axis=2).reshape(d.shape[0], MH * (NOPE + ROPE))


def time_permute(a):
    return a.reshape(SEG, SEG_LEN, a.shape[-1]).transpose(1, 0, 2).reshape(L, a.shape[-1])


def time_unpermute(a):
    return a.reshape(SEG_LEN, SEG, a.shape[-1]).transpose(1, 0, 2).reshape(L, a.shape[-1])


def mem_branch_fwd(tag, mem, mem_norm, w_mem_kv, xk_norm):
    mn = row_fwd(tag + "_mem_rms", fn_rms, ML, ML, [(mem, D, 0)], [mem_norm], [(D, BF16)])[0]
    kv = matmul(tag + "_mem_kv", mn, w_mem_kv, "nn", F32)
    kn = row_fwd(tag + "_mem_knorm", fn_mem_k, ML, ML, [(kv, XQW, 0)], [xk_norm], [(XQW, F32)])[0]
    return mn, kv, kn


def mem_branch_bwd(tag, mem, mem_norm, w_mem_kv, xk_norm, mn, kv, dkn, dv, g_view, g_wide):
    dk, dxk = row_bwd(tag + "_mem_knorm_bwd", fn_mem_k, ML, ML, [(kv, XQW, 0)], [xk_norm], [(dkn, XQW, 0)], [True], [True])
    dkv = jnp.concatenate([dk, dv], axis=1)
    dmn = matmul(tag + "_mem_kv_dx", dkv, w_mem_kv, "nt", F32)
    g_wide = matmul(tag + "_mem_kv_dw", mn, dkv, "tn", out=g_view, into=g_wide)
    dmem_norm = row_bwd(tag + "_mem_rms_bwd", fn_rms, ML, ML, [(mem, D, 0)], [mem_norm], [(dmn, D, 0)], [False], [True])[0]
    return g_wide, dmem_norm, dxk


def mem_attn_fwd(tag, proj, cb, kn, kv, xq_norm):
    return row_fwd(tag + "_mem_attn", fn_mem_attn, L, 256, [(proj, XQW, cb)], [kn, kv[:, XQW:], xq_norm], [(XQW, F32)])[0]


def mem_attn_bwd(tag, proj, cb, kn, kv, xq_norm, dmo, dproj):
    place = {"cols": proj.shape[1], "cb": cb, "into": dproj, "dtype": dproj.dtype}
    return row_bwd(tag + "_mem_attn_bwd", fn_mem_attn, L, 256, [(proj, XQW, cb)], [kn, kv[:, XQW:], xq_norm], [(dmo, XQW, 0)],
                   [place], [True, True, True])


def device_step(x, mem, positions, target, small, env, hooks=None):
    hooks = hooks or {}

    def plans_for(name):
        return hooks[("plans", name)](env) if ("plans", name) in hooks else ()

    def around(when, name, last=None):
        if (when, name) in hooks:
            hooks[(when, name)](env, last)

    g = {}
    gw = grad_views()
    ln, mem_norm, xq_norm, xk_norm = small["ln_gain"], small["mem_norm"], small["xq_norm"], small["xk_norm"]

    lre, lim = small["s5_lambda_re"][0], small["s5_lambda_im"][0]
    ls = small["s5_log_step"].reshape(SG, 1)
    one = pl.BlockSpec((SG, SP), lambda i: (0, 0))
    col = pl.BlockSpec((SG, 1), lambda i: (0, 0))
    disc_ins = [(lre, one), (lim, one), (ls, col)]
    a_re, a_im, coef_re, coef_im = stage("s5_disc", fn_s5_disc, (1,), disc_ins, [(SDS((SG, SP), F32), one)] * 4)
    b_re, b_im = small["s5_b_re"].reshape(SN, SC), small["s5_b_im"].reshape(SN, SC)
    c_re, c_im = small["s5_c_re"].reshape(PW, SP), small["s5_c_im"].reshape(PW, SP)
    bmat_rows = [(b_re, SC, 0), (b_im, SC, 0), (coef_re.reshape(SN, 1), 1, 0), (coef_im.reshape(SN, 1), 1, 0)]
    wb_re, wb_im = row_fwd("s5_bmat", fn_s5_bmat, SN, 512, bmat_rows, [], [(128, F32)] * 2)
    cmat_rows = [(c_re, SP, 0), (c_im, SP, 0)]
    wc_re, wc_im = row_fwd("s5_cmat", fn_s5_cmat, PW, 128, cmat_rows, [], [(512, F32)] * 2)
    a_re_v, a_im_v = a_re.reshape(1, SN), a_im.reshape(1, SN)
    s5_d = small["s5_d"]

    xp = time_permute(x)
    h0 = row_fwd("l0_rms", fn_rms, L, 256, [(xp, D, 0)], [ln[0:1]], [(D, BF16)])[0]
    around("before", "l0_in", wb_re)
    w_in0 = Sharded(env["in0"], "col", 0, 1024)
    proj0 = matmul("l0_in", h0, w_in0, "nn")
    s_re, s_im, g0 = s5_forward(proj0, wb_re, wb_im, wc_re, wc_im, a_re_v, a_im_v, s5_d, plans=plans_for("s5_forward"))
    around("before", "l0_glu", g0)
    w0 = dict(pair_views(env["pair0"]), s5_w_glu=Sharded(env["glu"], "col", 0, 1536))
    z0 = matmul("l0_glu", g0, w0["s5_w_glu"], "nn", plans=plans_for("l0_glu"))
    mn0, kv0, kn0 = mem_branch_fwd("l0", mem, mem_norm[0:1], w0["w_mem_kv"], xk_norm[0:1])
    mo0 = mem_attn_fwd("l0", proj0, 3, kn0, kv0, xq_norm[0:1])
    o0 = row_fwd("l0_merge", fn_merge_glu, L, 256, [(z0, 2 * PW, 0), (mo0, XQW, 0), (proj0, BW, 1)], [], [(BW, BF16)])[0]
    around("before", "l0_out", o0)
    x1p = matmul("l0_out", o0, w0["w_out"], "nn", F32, add=xp, plans=plans_for("l0_out"))
    around("after", "l0_out", x1p)
    x1 = time_unpermute(x1p)

    w1 = dict(pair_views(env["pair1"]), mla_w_ukv=Sharded(env["ukv"], "col", 0, 256))
    w_in1, w_uq = env["w_in1"], env["w_uq"]
    h1 = row_fwd("l1_rms", fn_rms, L, 256, [(x1, D, 0)], [ln[1:2]], [(D, BF16)])[0]
    proj1 = matmul("l1_in", h1, w_in1, "nn")
    qln, kvln = env["q_lora_norm"].reshape(1, QL), env["kv_lora_norm"].reshape(1, KVL)
    cqn = row_fwd("l1_q_lora_rms", fn_rms, L, 256, [(proj1, QL, 4)], [qln], [(QL, BF16)])[0]
    ckvn = row_fwd("l1_kv_lora_rms", fn_rms, L, 256, [(proj1, KVL, 12)], [kvln], [(KVL, BF16)])[0]
    q = matmul("l1_uq", cqn, w_uq, "nn")
    kv = matmul("l1_ukv", ckvn, w1["mla_w_ukv"], "nn")
    inv_freq = ROPE_THETA ** (-jnp.arange(ROPE // 2, dtype=F32) / (ROPE // 2))
    ang = positions.astype(F32)[:, None] * inv_freq
    cos2 = jnp.tile(jnp.cos(ang), (1, 4))
    sin_signed = jnp.tile(jnp.concatenate([-jnp.sin(ang), jnp.sin(ang)], axis=1), (1, 2))
    qnn, knn = small["mla_q_nope_norm"], small["mla_k_nope_norm"]
    qrn, krn = jnp.tile(small["mla_q_rope_norm"], (1, 2)), jnp.tile(small["mla_k_rope_norm"], (1, 2))
    tp = 256
    prep_ins = [(q, rspec(tp, MH * (NOPE + ROPE))), (kv, rspec(tp, MH * 256)), (proj1, rspec(tp, 128, 26)),
                (cos2, rspec(tp, 128)), (sin_signed, rspec(tp, 128))] + [(a, cspec((1, 128))) for a in (qnn, knn, qrn, krn)]
    hq_spec = pl.BlockSpec((MH, tp, 256), lambda i: (0, i, 0))
    hv_spec = pl.BlockSpec((MH, tp, 128), lambda i: (0, i, 0))
    qf, kf, vh = stage("l1_mla_prep", fn_mla_prep, (L // tp,), prep_ins,
                       [(SDS((MH, L, 256), BF16), hq_spec), (SDS((MH, L, 256), BF16), hq_spec), (SDS((MH, L, 128), BF16), hv_spec)])
    attn = causal_attn(qf, kf, vh)
    mn1, kv1, kn1 = mem_branch_fwd("l1", mem, mem_norm[1:2], w1["w_mem_kv"], xk_norm[1:2])
    mo1 = mem_attn_fwd("l1", proj1, 5, kn1, kv1, xq_norm[1:2])
    o1 = row_fwd("l1_merge", fn_merge, L, 256, [(attn, PW, 0), (mo1, XQW, 0), (proj1, BW, 0)], [], [(BW, BF16)])[0]
    x2 = matmul("l1_out", o1, w1["w_out"], "nn", F32, add=x1)
    dx2, loss = loss_and_grad(x2, target)

    do1 = matmul("l1_out_dx", dx2, w1["w_out"], "nt")
    g_pair1 = matmul("l1_out_dw", o1, dx2, "tn", out=gw["w_out"])
    dattn, dmo1, dproj1 = row_bwd("l1_merge_bwd", fn_merge, L, 256, [(attn, PW, 0), (mo1, XQW, 0), (proj1, BW, 0)], [],
                                  [(do1, BW, 0)], [True, True, {"cols": MLA_IN_P, "cb": 0, "dtype": BF16}], [])
    dproj1, dkn1, dv1, dxqn1 = mem_attn_bwd("l1", proj1, 5, kn1, kv1, xq_norm[1:2], dmo1, dproj1)
    env["g_pair1"], dmem_norm1, dxk1 = mem_branch_bwd("l1", mem, mem_norm[1:2], w1["w_mem_kv"], xk_norm[1:2], mn1, kv1, dkn1, dv1,
                                                      gw["w_mem_kv"], g_pair1)
    dqf, dkf, dvh = causal_attn_bwd(qf, kf, vh, dattn)
    prep_diffs = [("row", SDS((L, MH * (NOPE + ROPE)), BF16), rspec(tp, MH * (NOPE + ROPE))), ("row", SDS((L, MH * 256), BF16), rspec(tp, MH * 256)),
                  ("row", SDS((L, MLA_IN_P), BF16), rspec(tp, 128, 26), {"into": dproj1}), None, None] + [("acc", (0,))] * 4
    dq, dkv, dproj1, dqnn, dknn, dqrn, dkrn = stage_bwd("l1_mla_prep_bwd", fn_mla_prep, (L // tp,), prep_ins,
                                                        [(dqf, hq_spec), (dkf, hq_spec), (dvh, hv_spec)], prep_diffs)
    dcqn = matmul("l1_uq_dx", dq, w_uq, "nt")
    env["dw_uq"] = matmul("l1_uq_dw", cqn, dq, "tn")
    dckvn = matmul("l1_ukv_dx", dkv, w1["mla_w_ukv"], "nt")
    env["g_ukv"] = matmul("l1_ukv_dw", ckvn, dkv, "tn", out=gw["mla_w_ukv"])
    dproj1, dqln = row_bwd("l1_q_lora_rms_bwd", fn_rms, L, 256, [(proj1, QL, 4)], [qln], [(dcqn, QL, 0)],
                           [{"cols": MLA_IN_P, "cb": 4, "into": dproj1, "dtype": BF16}], [True])
    dproj1, dkvln = row_bwd("l1_kv_lora_rms_bwd", fn_rms, L, 256, [(proj1, KVL, 12)], [kvln], [(dckvn, KVL, 0)],
                            [{"cols": MLA_IN_P, "cb": 12, "into": dproj1, "dtype": BF16}], [True])
    dh1 = matmul("l1_in_dx", dproj1, w_in1, "nt")
    env["dw_in1"] = matmul("l1_in_dw", h1, dproj1, "tn")
    dx1, dln1 = row_bwd("l1_rms_bwd", fn_rms, L, 256, [(x1, D, 0)], [ln[1:2]], [(dh1, D, 0)], [{"add": (dx2, D, 0)}], [True],
                        plans=plans_for("l1_rms_bwd"))
    around("after", "l1_rms_bwd", dx1)
    dx1p = time_permute(dx1)

    do0 = matmul("l0_out_dx", dx1p, w0["w_out"], "nt", plans=plans_for("l0_out_dx"))
    g_pair0 = matmul("l0_out_dw", o0, dx1p, "tn", out=gw["w_out"])
    dz0, dmo0, dproj0 = row_bwd("l0_merge_bwd", fn_merge_glu, L, 256, [(z0, 2 * PW, 0), (mo0, XQW, 0), (proj0, BW, 1)], [],
                                [(do0, BW, 0)], [{"dtype": BF16}, True, {"cols": 2 * BW, "cb": 1, "dtype": BF16}], [])
    dproj0, dkn0, dv0, dxqn0 = mem_attn_bwd("l0", proj0, 3, kn0, kv0, xq_norm[0:1], dmo0, dproj0)
    env["g_pair0"], dmem_norm0, dxk0 = mem_branch_bwd("l0", mem, mem_norm[0:1], w0["w_mem_kv"], xk_norm[0:1], mn0, kv0, dkn0, dv0,
                                                      gw["w_mem_kv"], g_pair0)
    env["g_glu"] = matmul("l0_glu_dw", g0, dz0, "tn", out=gw["s5_w_glu"], plans=plans_for("l0_glu_dw"))
    dg0 = matmul("l0_glu_dx", dz0, w0["s5_w_glu"], "nt", plans=plans_for("l0_glu_dx"))
    around("before", "s5_backward", dg0)
    dproj0, dd, dwc_re, dwc_im, dwb_re, dwb_im, da_re, da_im = s5_backward(dg0, proj0, s_re, s_im, wb_re, wb_im, wc_re, wc_im,
                                                                           a_re_v, a_im_v, s5_d, dproj0, plans=plans_for("s5_backward"))
    around("after", "s5_backward", dd)
    env["g_in0"] = matmul("l0_in_dw", h0, dproj0, "tn", out=gw["s5_w_in"], plans=plans_for("l0_in_dw"))
    dh0 = matmul("l0_in_dx", dproj0, w_in0, "nt", plans=plans_for("l0_in_dx"))
    around("after", "l0_in_dx", dh0)
    dxp, dln0 = row_bwd("l0_rms_bwd", fn_rms, L, 256, [(xp, D, 0)], [ln[0:1]], [(dh0, D, 0)], [{"add": (dx1p, D, 0)}], [True])
    grad_x = time_unpermute(dxp)

    db_re, db_im, dcoef_re, dcoef_im = row_bwd("s5_bmat_bwd", fn_s5_bmat, SN, 512, bmat_rows, [], [(dwb_re, 128, 0), (dwb_im, 128, 0)],
                                               [True] * 4, [], plans=plans_for("s5_bmat_bwd"))
    dc_re, dc_im = row_bwd("s5_cmat_bwd", fn_s5_cmat, PW, 128, cmat_rows, [], [(dwc_re, 512, 0), (dwc_im, 512, 0)], [True] * 2, [],
                           plans=plans_for("s5_cmat_bwd"))
    disc_cts = [(da_re.reshape(SG, SP), one), (da_im.reshape(SG, SP), one), (dcoef_re.reshape(SG, SP), one), (dcoef_im.reshape(SG, SP), one)]
    dlre, dlim, dls = stage_bwd("s5_disc_bwd", fn_s5_disc, (1,), disc_ins, disc_cts, [("acc", (0,))] * 3)

    g["ln_gain"] = jnp.concatenate([dln0, dln1], axis=0)
    g["mem_norm"] = jnp.concatenate([dmem_norm0, dmem_norm1], axis=0)
    g["xq_norm"] = jnp.concatenate([dxqn0, dxqn1], axis=0)
    g["xk_norm"] = jnp.concatenate([dxk0, dxk1], axis=0)
    g["s5_lambda_re"], g["s5_lambda_im"], g["s5_log_step"] = dlre, dlim, dls
    g["s5_b_re"], g["s5_b_im"], g["s5_c_re"], g["s5_c_im"] = db_re, db_im, dc_re, dc_im
    g["s5_d"] = dd
    g["mla_q_lora_norm"], g["mla_kv_lora_norm"] = dqln, dkvln
    g["mla_q_nope_norm"], g["mla_k_nope_norm"] = dqnn, dknn
    g["mla_q_rope_norm"] = dqrn[:, :ROPE] + dqrn[:, ROPE:]
    g["mla_k_rope_norm"] = dkrn[:, :ROPE] + dkrn[:, ROPE:]
    return loss, grad_x, g


def kernel(x, mem, positions, ln_gain, w_out, mem_norm, w_mem_kv, xq_norm, xk_norm, s5_w_in, s5_lambda_re, s5_lambda_im, s5_log_step, s5_b_re, s5_b_im, s5_c_re, s5_c_im, s5_d, s5_w_glu, mla_w_in, mla_q_lora_norm, mla_kv_lora_norm, mla_w_uq, mla_w_ukv, mla_q_nope_norm, mla_k_nope_norm, mla_q_rope_norm, mla_k_rope_norm, loss_target, m_ln_gain, m_w_out, m_mem_norm, m_w_mem_kv, m_xq_norm, m_xk_norm, m_s5_w_in, m_s5_lambda_re, m_s5_lambda_im, m_s5_log_step, m_s5_b_re, m_s5_b_im, m_s5_c_re, m_s5_c_im, m_s5_d, m_s5_w_glu, m_mla_w_in, m_mla_q_lora_norm, m_mla_kv_lora_norm, m_mla_w_uq, m_mla_w_ukv, m_mla_q_nope_norm, m_mla_k_nope_norm, m_mla_q_rope_norm, m_mla_k_rope_norm, v_ln_gain, v_w_out, v_mem_norm, v_w_mem_kv, v_xq_norm, v_xk_norm, v_s5_w_in, v_s5_lambda_re, v_s5_lambda_im, v_s5_log_step, v_s5_b_re, v_s5_b_im, v_s5_c_re, v_s5_c_im, v_s5_d, v_s5_w_glu, v_mla_w_in, v_mla_q_lora_norm, v_mla_kv_lora_norm, v_mla_w_uq, v_mla_w_ukv, v_mla_q_nope_norm, v_mla_k_nope_norm, v_mla_q_rope_norm, v_mla_k_rope_norm):
    args = dict(locals())
    wts = {n: args[n] for n in WEIGHT_ORDER}
    mom = {n: args["m_" + n] for n in WEIGHT_ORDER}
    var = {n: args["v_" + n] for n in WEIGHT_ORDER}

    chip = 2 * lax.axis_index("x") + lax.axis_index("y")
    place = jnp.stack([chip, lax.axis_index("c")]).astype(jnp.int32)

    def own_slot(gathered, shards):
        return [lax.dynamic_update_slice(g, s[None], (chip, 0, 0)) for g, s in zip(gathered, shards)]

    groups = list(stack_shards(wts, BF16))
    groups[2].append(jnp.concatenate([mla_q_lora_norm, jnp.pad(mla_kv_lora_norm, ((0, 0), (0, 64))), jnp.zeros((14, 128), F32)], axis=0))
    over_ici = [plan_gather_ici(shards) for shards in groups]
    _SCHEDULE_BEHIND.clear()
    schedule_behind(split_start("gather_start", over_ici))
    env, hooks, passed_on = {}, {}, {}

    def arrived(k, after, pass_now):
        passing = plan_gather_pass(split_wait(f"gather_wait_{k}", over_ici[k], after))
        passed_on[k] = passing
        return own_slot(run_plan(f"gather_pass_{k}", passing), groups[k]) if pass_now else None

    def need_in0(env, last):
        env["in0"], = arrived(0, last, True)

    def need_layer0(env, last):
        env["pair0"], env["glu"] = arrived(1, last, True)

    def need_layer1(env, last):
        arrived(2, last, False)

    def layer1_weights(env, last):
        pair1, ukv, in1, uq, norms = own_slot(passed_on[2].results, groups[2])
        env.update(pair1=pair1, ukv=ukv, w_in1=mla_in_permute(shards_to_cols(in1)), w_uq=uq_permute(shards_to_cols(uq)),
                   q_lora_norm=norms[:, 0, :], kv_lora_norm=norms[:, 1, :64])

    hooks["before", "l0_in"], hooks["before", "l0_glu"], hooks["before", "l0_out"] = need_in0, need_layer0, need_layer1
    hooks["plans", "l0_out"], hooks["after", "l0_out"] = (lambda env: [passed_on[2]]), layer1_weights

    rs = {}

    def swap(k, gs):
        rs[k, "g"], rs[k, "swap"] = gs, plan_pair_exchange(gs)
        return rs[k, "swap"]

    def start_scatter(k):
        rs[k, "pairs"] = pair_adds(f"rs{k}", rs[k, "g"], rs[k, "swap"].results, place)
        rs[k, "scatter"] = plan_chip_scatter(rs[k, "pairs"])
        schedule_behind(split_start(f"rs{k}_scatter_start", [rs[k, "scatter"]]))

    def join(k, after):
        rs[k, "join"] = plan_pair_join(chip_adds(f"rs{k}", rs[k, "pairs"], split_wait(f"rs{k}_scatter_wait", rs[k, "scatter"], after), place))
        return rs[k, "join"]

    hooks["plans", "l1_rms_bwd"] = lambda env: [swap(1, [env["g_pair1"], env["g_ukv"], cols_to_shards(mla_in_unpermute(env["dw_in1"])).astype(BF16),
                                                          cols_to_shards(uq_unpermute(env["dw_uq"])).astype(BF16)])]
    hooks["after", "l1_rms_bwd"] = lambda env, last: start_scatter(1)
    hooks["plans", "l0_glu_dx"] = lambda env: [swap(0, [env["g_pair0"], env["g_glu"]])]

    def before_s5_backward(env, last):
        start_scatter(0)
        env["join1"] = join(1, last)

    hooks["before", "s5_backward"] = before_s5_backward
    hooks["plans", "s5_backward"] = lambda env: [env["join1"]]
    hooks["after", "s5_backward"] = lambda env, last: env.update(join0=join(0, last))
    hooks["plans", "l0_in_dx"] = lambda env: [swap(2, [env["g_in0"]]), env["join0"]]
    hooks["after", "l0_in_dx"] = lambda env, last: start_scatter(2)

    small = {n: wts[n] for n, _ in SMALL}
    loss, grad_x, g = device_step(x[0], mem[0], positions[0], loss_target[0], small, env, hooks)
    loss = lax.psum(loss[0, 0], MESH_AXES)
    r_in0, = run_plan("rs2_pair_join", join(2, g["s5_log_step"]))
    (r_pair1, r_ukv, r_in1, r_uq), (r_pair0, r_glu) = (rs[k, "join"].results for k in (1, 0))

    small_flat = jnp.concatenate([g[n].reshape(-1) for n, _ in SMALL_FULL])
    g_small = jnp.pad(small_flat, (0, 4 * SMALL_ROWS * SMALL_LANES - N_SMALL)).astype(BF16).reshape(4, SMALL_ROWS, SMALL_LANES)
    r_small = reduce_scatter_chips("rs3", [g_small], place)[0]
    small_all = own_slot(all_gather_chips("gather_small_grads", [r_small]), [r_small])[0].reshape(-1)[:N_SMALL]

    grads = {"w_out": jnp.stack([r_pair0[:PAIR_MKV], r_pair1[:PAIR_MKV]]), "w_mem_kv": jnp.stack([r_pair0[PAIR_MKV:], r_pair1[PAIR_MKV:]]),
             "s5_w_in": r_in0[None], "s5_w_glu": r_glu[None], "mla_w_ukv": r_ukv[None], "mla_w_in": r_in1[None], "mla_w_uq": r_uq[None]}
    off = 0
    for n, s in SMALL_FULL:
        grads[n] = small_all[off:off + math.prod(s)].reshape(s)
        off += math.prod(s)
    for n, s in SHARDED_SMALL:
        grads[n] = lax.dynamic_slice(grads[n], (0, chip * s[1]), s)

    delta, new_m, new_v = {}, {}, {}
    for n, s in BIG + [(n, s) for n, s in SMALL if len(s) == 4]:
        perm = MINOR_LAST.get(n, tuple(range(len(s))))
        turned = tuple(s[p] for p in perm)
        view = lambda a: jnp.transpose(a, perm).reshape(-1, turned[-1])
        res = adamw("adamw_" + n, view(wts[n]), view(grads[n]), view(mom[n]), view(var[n]))
        delta[n], new_m[n], new_v[n] = (jnp.transpose(r.reshape(turned), tuple(perm.index(i) for i in range(len(s)))) for r in res)
    small_names = [n for n, s in SMALL if len(s) < 4] + [n for n, _ in SHARDED_SMALL]
    n_own = sum(wts[n].size for n in small_names)
    rows_own = -(-n_own // (8 * 128)) * 8

    def pack_small(d):
        flat = jnp.concatenate([d[n].reshape(-1) for n in small_names])
        return jnp.pad(flat, (0, rows_own * 128 - n_own), constant_values=1.0).reshape(rows_own, 128)

    res = adamw("adamw_small", pack_small(wts), pack_small(grads), pack_small(mom), pack_small(var))
    off = 0
    for n in small_names:
        size = wts[n].size
        delta[n], new_m[n], new_v[n] = (r.reshape(-1)[off:off + size].reshape(wts[n].shape) for r in res)
        off += size

    return (loss, grad_x[None], *[grads[n] for n in WEIGHT_ORDER], *[delta[n] for n in WEIGHT_ORDER],
            *[new_m[n] for n in WEIGHT_ORDER], *[new_v[n] for n in WEIGHT_ORDER])
```

```python
import functools
import math

import jax
import jax.numpy as jnp
from jax import lax
from jax.experimental import pallas as pl
from jax.experimental.pallas import tpu as pltpu

F32, BF16 = jnp.float32, jnp.bfloat16
SDS = jax.ShapeDtypeStruct

D = 1024
L = 2048
ML = 256
BW = 2 * D
XQW = BW // 4
PW = BW - XQW
XH, XHD = 4, 128
SG, SC, SP = 96, 16, 64
SN = SG * SP
NOPE, ROPE, VD = 128, 64, 128
MH = 12
QL, KVL = 512, 256
EPS = 1e-6
ROPE_THETA = 10000.0
MLA_IN = QL + KVL + ROPE + XQW + BW
MLA_IN_P = 3456
ADAM_LR, ADAM_B1, ADAM_B2, ADAM_EPS, ADAM_WD, ADAM_STEP = 0.001, 0.9, 0.999, 1e-08, 0.01, 10

VMEM_LIMIT = 48 * 2**20
SEG = 8
SEG_LEN = L // SEG
MESH_AXES = ("x", "y", "c")


def _cparams():
    return pltpu.CompilerParams(vmem_limit_bytes=VMEM_LIMIT)


def _dg(a, b, ca, cb):
    return lax.dot_general(a.astype(BF16), b.astype(BF16), (((ca,), (cb,)), ((), ())), preferred_element_type=F32)


@jax.custom_vjp
def mm_nn(a, b):
    return _dg(a, b, 1, 0)


mm_nn.defvjp(lambda a, b: (_dg(a, b, 1, 0), (a, b)), lambda res, g: (_dg(g, res[1], 1, 1), _dg(res[0], g, 0, 0)))


@jax.custom_vjp
def mm_nt(a, b):
    return _dg(a, b, 1, 1)


mm_nt.defvjp(lambda a, b: (_dg(a, b, 1, 1), (a, b)), lambda res, g: (_dg(g, res[1], 1, 0), _dg(g, res[0], 0, 0)))


@functools.partial(jax.custom_vjp, nondiff_argnums=(1,))
def lane_roll(x, shift):
    return pltpu.roll(x, shift, 1)


lane_roll.defvjp(lambda x, shift: (pltpu.roll(x, shift, 1), None),
                 lambda shift, _, g: (pltpu.roll(g, (128 - shift) % 128, 1),))


def rms(x, g):
    return x * lax.rsqrt(jnp.mean(x * x, axis=-1, keepdims=True) + EPS) * g


@jax.custom_vjp
def softmax_rows(s):
    e = jnp.exp(s - jnp.max(s, axis=-1, keepdims=True))
    return e / jnp.sum(e, axis=-1, keepdims=True)


def _softmax_rows_fwd(s):
    p = softmax_rows(s)
    return p, p


def _softmax_rows_bwd(p, g):
    return (p * (g - jnp.sum(g * p, axis=-1, keepdims=True)),)


softmax_rows.defvjp(_softmax_rows_fwd, _softmax_rows_bwd)


def silu(x):
    return x * jax.nn.sigmoid(x)


ANY = pl.BlockSpec(memory_space=pl.ANY)
MESH_ID = pl.DeviceIdType.MESH


def _dma_sems(n):
    return [pltpu.SemaphoreType.DMA((n,)), pltpu.SemaphoreType.DMA((n,))]


class Plan:
    def __init__(self, operands, out_shape, aliases, n_sems, copies):
        self.operands, self.out_shape, self.aliases, self.n_sems, self.copies = list(operands), list(out_shape), aliases, n_sems, copies
        self.results = None


_SCHEDULE_BEHIND = []


def schedule_behind(token):
    _SCHEDULE_BEHIND.append(token)


def hosted_call(kern, *, name, grid, in_specs, out_specs, out_shape, operands, scratch_shapes=(), aliases=None, cparams=None, plans=(), deps=()):
    n_in, n_out, n_scr = len(in_specs), len(out_specs), len(scratch_shapes)
    p_in, p_out = [len(p.operands) for p in plans], [len(p.out_shape) for p in plans]
    deps = tuple(deps) + tuple(_SCHEDULE_BEHIND)
    _SCHEDULE_BEHIND.clear()
    all_aliases = dict(aliases or {})
    in_off, out_off = n_in, n_out
    for p, ni, no in zip(plans, p_in, p_out):
        all_aliases.update({in_off + i: out_off + o for i, o in p.aliases.items()})
        in_off, out_off = in_off + ni, out_off + no

    def body(*refs):
        pos, pins, pouts = n_in, [], []
        for ni in p_in:
            pins.append(refs[pos:pos + ni])
            pos += ni
        pos += len(deps)
        main_out = refs[pos:pos + n_out]
        pos += n_out
        for no in p_out:
            pouts.append(refs[pos:pos + no])
            pos += no
        main_scr = refs[pos:pos + n_scr]
        pos += n_scr
        if plans:
            ids = [pl.program_id(ax) for ax in range(len(grid))]
            first = functools.reduce(jnp.logical_and, [i == 0 for i in ids])
            last = functools.reduce(jnp.logical_and, [i == g - 1 for i, g in zip(ids, grid)])
            copies = [p.copies(pins[k], pouts[k], refs[pos + 2 * k], refs[pos + 2 * k + 1]) for k, p in enumerate(plans)]

            @pl.when(first)
            def _():
                for sends, _ in copies:
                    for cp in sends:
                        cp.start()

        kern(*refs[:n_in], *main_out, *main_scr)
        if plans:
            @pl.when(last)
            def _():
                for sends, recvs in copies:
                    for cp in recvs:
                        cp.wait_recv()
                    for cp in sends:
                        cp.wait_send()

    res = pl.pallas_call(body, grid=grid, in_specs=list(in_specs) + [ANY] * (sum(p_in) + len(deps)),
                         out_specs=list(out_specs) + [ANY] * sum(p_out), out_shape=list(out_shape) + [s for p in plans for s in p.out_shape],
                         scratch_shapes=list(scratch_shapes) + [s for p in plans for s in _dma_sems(p.n_sems)],
                         input_output_aliases=all_aliases, name=name, compiler_params=cparams or _cparams())(
        *operands, *[a for p in plans for a in p.operands], *deps)
    pos = n_out
    for p, no in zip(plans, p_out):
        p.results = list(res[pos:pos + no])
        pos += no
    return list(res[:n_out])


def _wide(v):
    return v.astype(F32) if v.dtype == BF16 else v


def stage(name, fn, grid, ins, outs):
    n_in = len(ins)

    def kern(*refs):
        res = fn(*[_wide(r[...]) for r in refs[:n_in]])
        for r, v in zip(refs[n_in:], res):
            r[...] = v.astype(r.dtype)

    return hosted_call(kern, name=name, grid=grid, in_specs=[s for _, s in ins], out_specs=[s for _, s in outs],
                       out_shape=[sd for sd, _ in outs], operands=[a for a, _ in ins])


def stage_bwd(name, fn, grid, ins, cts, diffs, plans=()):
    n_in, n_ct = len(ins), len(cts)
    didx = [i for i, d in enumerate(diffs) if d is not None]
    opts = {i: (diffs[i][3] if len(diffs[i]) > 3 else {}) for i in didx if diffs[i][0] == "row"}
    adds = [(i, opts[i]["add"]) for i in opts if "add" in opts[i]]
    intos = [(i, opts[i]["into"]) for i in opts if "into" in opts[i]]
    n_add, n_into = len(adds), len(intos)
    add_pos = {i: n_in + n_ct + k for k, (i, _) in enumerate(adds)}
    n_extra = n_in + n_ct + n_add + n_into

    def kern(*refs):
        vals = [_wide(r[...]) for r in refs[:n_in]]

        def f(*dv):
            full = list(vals)
            for i, v in zip(didx, dv):
                full[i] = v
            return fn(*full)

        _, vjp = jax.vjp(f, *[vals[i].astype(F32) for i in didx])
        gs = vjp(tuple(c[...].astype(F32) for c in refs[n_in:n_in + n_ct]))
        for o_ref, i, g in zip(refs[n_extra:], didx, gs):
            if diffs[i][0] == "row":
                if i in add_pos:
                    g = g + refs[add_pos[i]][...].astype(F32)
                o_ref[...] = g.astype(o_ref.dtype)
            else:
                first = functools.reduce(jnp.logical_and, [pl.program_id(ax) == 0 for ax in diffs[i][1]])

                @pl.when(first)
                def _():
                    o_ref[...] = g

                @pl.when(jnp.logical_not(first))
                def _():
                    o_ref[...] += g

    out_shape, out_specs = [], []
    for i in didx:
        if diffs[i][0] == "row":
            out_shape.append(diffs[i][1])
            out_specs.append(diffs[i][2])
        else:
            out_shape.append(SDS(ins[i][0].shape, F32))
            out_specs.append(ins[i][1])
    aliases = {n_in + n_ct + n_add + k: didx.index(i) for k, (i, _) in enumerate(intos)}
    in_specs = [s for _, s in ins] + [s for _, s in cts] + [s for _, (_, s) in adds] + [ANY] * n_into
    operands = [a for a, _ in ins] + [a for a, _ in cts] + [a for _, (a, _) in adds] + [a for _, a in intos]
    return hosted_call(kern, name=name, grid=grid, in_specs=in_specs, out_specs=out_specs, out_shape=out_shape, operands=operands,
                       aliases=aliases, plans=plans)


def rspec(tl, w, cb=0):
    return pl.BlockSpec((tl, w), lambda i: (i, cb))


def cspec(shape):
    return pl.BlockSpec(shape, lambda i: (0,) * len(shape))


def row_fwd(name, fn, rows, tl, row_ins, consts, outs):
    ins = [(a, rspec(tl, w, cb)) for a, w, cb in row_ins] + [(a, cspec(a.shape)) for a in consts]
    return stage(name, fn, (rows // tl,), ins, [(SDS((rows, w), dt), rspec(tl, w)) for w, dt in outs])


def row_bwd(name, fn, rows, tl, row_ins, consts, cts, row_diff, const_diff, plans=()):
    ins = [(a, rspec(tl, w, cb)) for a, w, cb in row_ins] + [(a, cspec(a.shape)) for a in consts]
    diffs = []
    for (a, w, cb), d in zip(row_ins, row_diff):
        if not d:
            diffs.append(None)
            continue
        d = d if isinstance(d, dict) else {}
        opts = {}
        if "add" in d:
            opts["add"] = (d["add"][0], rspec(tl, d["add"][1], d["add"][2]))
        if d.get("into") is not None:
            opts["into"] = d["into"]
        diffs.append(("row", SDS((rows, d.get("cols", w)), d.get("dtype", F32)), rspec(tl, w, d.get("cb", 0)), opts))
    diffs += [("acc", (0,)) if d else None for d in const_diff]
    return stage_bwd(name, fn, (rows // tl,), ins, [(a, rspec(tl, w, cb)) for a, w, cb in cts], diffs, plans=plans)


MATMUL_VMEM = 36 * 2**20
ADAMW_VMEM = 28 * 2**20


class Sharded:
    def __init__(self, arr, kind, roff, rows):
        self.arr, self.kind, self.roff, self.rows, self.n = arr, kind, roff, rows, arr.shape[2]
        self.shape = (rows, 4 * self.n) if kind == "col" else (4 * rows, self.n)

    def fits(self, t0, t1):
        return self.roff % t0 == 0 and self.rows % t0 == 0 and self.n % t1 == 0

    def spec(self, t0, t1, bidx):
        assert self.fits(t0, t1), (self.kind, self.roff, self.rows, self.n, t0, t1)
        r0 = self.roff // t0
        if self.kind == "col":
            per = self.n // t1
            return pl.BlockSpec((None, t0, t1), lambda *g: (bidx(*g)[1] // per, r0 + bidx(*g)[0], bidx(*g)[1] % per))
        per = self.rows // t0
        return pl.BlockSpec((None, t0, t1), lambda *g: (bidx(*g)[0] // per, r0 + bidx(*g)[0] % per, bidx(*g)[1]))


def matmul(name, a, b, mode, out_dtype=BF16, add=None, out=None, into=None, plans=()):
    if mode == "tn":
        k_dim, m = a.shape
    else:
        m, k_dim = a.shape
    n = b.shape[0] if mode == "nt" else b.shape[1]
    b_fit = b.fits if isinstance(b, Sharded) else (lambda t0, t1: True)
    o_fit = out.fits if out is not None else (lambda t0, t1: True)
    a_bytes, b_bytes = jnp.dtype(a.dtype).itemsize, jnp.dtype(b.arr.dtype if isinstance(b, Sharded) else b.dtype).itemsize
    o_bytes = jnp.dtype(out_dtype if out is None else out.arr.dtype).itemsize

    def vmem(tm, tn, tk):
        return 2 * (tm * tk * a_bytes + tk * tn * b_bytes + tm * tn * (o_bytes + (4 if add is not None else 0))) + 4 * tm * tn * (1 + (tk < k_dim))

    tiles = [(tm, tn, tk) for tm in (2048, 1024, 512, 256, 128) for tn in (1024, 768, 512, 384, 256, 128)
             for tk in sorted({k_dim, 1024, 768, 512, 384, 256, 128})
             if m % tm == 0 and n % tn == 0 and k_dim % tk == 0 and (b_fit(tn, tk) if mode == "nt" else b_fit(tk, tn)) and o_fit(tm, tn)
             and vmem(tm, tn, tk) <= MATMUL_VMEM]
    tm, tn, tk = max(tiles, key=lambda t: (t[2] == k_dim, t[0] * t[1] * t[2], t[0] * t[1]))
    nk = k_dim // tk
    a_spec = pl.BlockSpec((tk, tm), lambda i, j, k: (k, i)) if mode == "tn" else pl.BlockSpec((tm, tk), lambda i, j, k: (i, k))
    if isinstance(b, Sharded):
        b_spec = b.spec(tn, tk, lambda i, j, k: (j, k)) if mode == "nt" else b.spec(tk, tn, lambda i, j, k: (k, j))
        b = b.arr
    else:
        b_spec = pl.BlockSpec((tn, tk), lambda i, j, k: (j, k)) if mode == "nt" else pl.BlockSpec((tk, tn), lambda i, j, k: (k, j))
    o_spec = pl.BlockSpec((tm, tn), lambda i, j, k: (i, j))
    out_spec, out_shape = (o_spec, SDS((m, n), out_dtype)) if out is None else (out.spec(tm, tn, lambda i, j, k: (i, j)), out.arr)
    ca, cb = {"nn": (1, 0), "nt": (1, 1), "tn": (0, 0)}[mode]
    n_in = 2 + (add is not None)

    def finish(refs, o_ref, r):
        if add is not None:
            r = r + refs[2][...]
        o_ref[...] = r.astype(o_ref.dtype)

    def kern_whole(*refs):
        finish(refs, refs[-1], _dg(refs[0][...], refs[1][...], ca, cb))

    def kern_cut(*refs):
        o_ref, acc = refs[-2], refs[-1]
        k = pl.program_id(2)

        @pl.when(k == 0)
        def _():
            acc[...] = jnp.zeros_like(acc)

        acc[...] += _dg(refs[0][...], refs[1][...], ca, cb)

        @pl.when(k == nk - 1)
        def _():
            finish(refs, o_ref, acc[...])

    ins, specs = [a, b], [a_spec, b_spec]
    if add is not None:
        ins.append(add)
        specs.append(o_spec)
    if into is not None:
        ins.append(into)
        specs.append(ANY)
    return hosted_call(kern_whole if nk == 1 else kern_cut, name=name, grid=(m // tm, n // tn, nk), in_specs=specs, out_specs=[out_spec],
                       out_shape=[out_shape], operands=ins, scratch_shapes=[] if nk == 1 else [pltpu.VMEM((tm, tn), F32)],
                       aliases={} if into is None else {n_in: 0}, plans=plans)[0]


def _cmul(ar, ai, br, bi):
    return ar * br - ai * bi, ar * bi + ai * br


def _sub_shift(x, down):
    row = lax.broadcasted_iota(jnp.int32, x.shape, 0)
    if down:
        return jnp.where(row == 0, 0.0, pltpu.roll(x, 1, 0))
    return jnp.where(row == SEG - 1, 0.0, pltpu.roll(x, SEG - 1, 0))


def _pow_seg_len(ar, ai):
    for _ in range(int(math.log2(SEG_LEN))):
        ar, ai = _cmul(ar, ai, ar, ai)
    return ar, ai


def _scan_in_place(sr, si, a_re, a_im):
    lanes = sr.shape[1]
    ar = jnp.broadcast_to(a_re, (SEG, lanes))
    ai = jnp.broadcast_to(a_im, (SEG, lanes))
    zero = jnp.zeros((SEG, lanes), F32)

    def local(i, carry):
        rows = pl.ds(pl.multiple_of(i * SEG, SEG), SEG)
        mr, mi = _cmul(ar, ai, carry[0], carry[1])
        nr, ni = mr + sr[rows, :], mi + si[rows, :]
        sr[rows, :] = nr
        si[rows, :] = ni
        return nr, ni

    fr, fi = lax.fori_loop(0, SEG_LEN, local, (zero, zero))
    pr, pi = _pow_seg_len(ar, ai)
    ir, ii = zero, zero
    for _ in range(SEG - 1):
        mr, mi = _cmul(pr, pi, ir, ii)
        ir, ii = _sub_shift(mr + fr, True), _sub_shift(mi + fi, True)

    def carry_in(i, pw):
        rows = pl.ds(pl.multiple_of(i * SEG, SEG), SEG)
        cr, ci = _cmul(pw[0], pw[1], ir, ii)
        sr[rows, :] += cr
        si[rows, :] += ci
        return _cmul(pw[0], pw[1], ar, ai)

    lax.fori_loop(0, SEG_LEN, carry_in, (ar, ai))


S5_LANES = 8 * SP
S5_BLOCKS = SN // S5_LANES


def _s5_specs():
    u_spec = pl.BlockSpec((L, 8 * SC), lambda j: (0, j))
    s_spec = pl.BlockSpec((L, S5_LANES), lambda j: (0, j))
    wb_spec = pl.BlockSpec((S5_LANES, 8 * SC), lambda j: (j, 0))
    wc_spec = pl.BlockSpec((8 * SC, S5_LANES), lambda j: (j, 0))
    a_spec = pl.BlockSpec((1, S5_LANES), lambda j: (0, j))
    d_spec = pl.BlockSpec((1, 8 * SC), lambda j: (0, j))
    return u_spec, s_spec, wb_spec, wc_spec, a_spec, d_spec


def s5_forward(proj, wb_re, wb_im, wc_re, wc_im, a_re, a_im, d, plans=()):
    def kern(u_ref, wbr, wbi, wcr, wci, ar, ai, d_ref, sr_out, si_out, g_ref, sr, si):
        u = _wide(u_ref[...])
        sr[...], si[...] = fn_s5_bu(u, wbr[...], wbi[...])
        _scan_in_place(sr, si, ar[...], ai[...])
        g_ref[...] = fn_s5_out(sr[...], si[...], u, d_ref[...], wcr[...], wci[...])[0].astype(g_ref.dtype)
        sr_out[...] = sr[...].astype(sr_out.dtype)
        si_out[...] = si[...].astype(si_out.dtype)

    u_spec, s_spec, wb_spec, wc_spec, a_spec, d_spec = _s5_specs()
    return hosted_call(kern, name="s5_forward", grid=(S5_BLOCKS,), in_specs=[u_spec, wb_spec, wb_spec, wc_spec, wc_spec, a_spec, a_spec, d_spec],
                       out_specs=[s_spec, s_spec, u_spec], out_shape=[SDS((L, SN), BF16)] * 2 + [SDS((L, PW), BF16)],
                       operands=[proj, wb_re, wb_im, wc_re, wc_im, a_re, a_im, d], scratch_shapes=[pltpu.VMEM((L, S5_LANES), F32)] * 2,
                       plans=plans)


def _adjoint_scan_in_place(lr, li, sr, si, a_re, a_im):
    lanes = lr.shape[1]
    ar = jnp.broadcast_to(a_re, (SEG, lanes))
    ai = -jnp.broadcast_to(a_im, (SEG, lanes))
    zero = jnp.zeros((SEG, lanes), F32)

    def local(k, carry):
        i = SEG_LEN - 1 - k
        rows = pl.ds(pl.multiple_of(i * SEG, SEG), SEG)
        mr, mi = _cmul(ar, ai, carry[0], carry[1])
        nr, ni = mr + lr[rows, :], mi + li[rows, :]
        lr[rows, :] = nr
        li[rows, :] = ni
        return nr, ni

    fr, fi = lax.fori_loop(0, SEG_LEN, local, (zero, zero))
    pr, pi = _pow_seg_len(ar, ai)
    ir, ii = zero, zero
    for _ in range(SEG - 1):
        mr, mi = _cmul(pr, pi, ir, ii)
        ir, ii = _sub_shift(mr + fr, False), _sub_shift(mi + fi, False)

    def fix(rows, pw):
        cr, ci = _cmul(pw[0], pw[1], ir, ii)
        tr, ti = lr[rows, :] + cr, li[rows, :] + ci
        lr[rows, :] = tr
        li[rows, :] = ti
        return tr, ti

    def grad_a(tr, ti, spr, spi, acc):
        return acc[0] + tr * spr + ti * spi, acc[1] + ti * spr - tr * spi

    def carry_in(k, c):
        i = SEG_LEN - 1 - k
        rows = pl.ds(pl.multiple_of(i * SEG, SEG), SEG)
        prev = pl.ds(pl.multiple_of((i - 1) * SEG, SEG), SEG)
        tr, ti = fix(rows, (c[0], c[1]))
        acc = grad_a(tr, ti, sr[prev, :], si[prev, :], (c[2], c[3]))
        nr, ni = _cmul(c[0], c[1], ar, ai)
        return nr, ni, acc[0], acc[1]

    pwr, pwi, accr, acci = lax.fori_loop(0, SEG_LEN - 1, carry_in, (ar, ai, zero, zero))
    tr, ti = fix(pl.ds(0, SEG), (pwr, pwi))
    last = pl.ds((SEG_LEN - 1) * SEG, SEG)
    accr, acci = grad_a(tr, ti, _sub_shift(sr[last, :], True), _sub_shift(si[last, :], True), (accr, acci))
    return jnp.sum(accr, axis=0, keepdims=True), jnp.sum(acci, axis=0, keepdims=True)


S5_BWD_VMEM = 58 * 2**20


def s5_backward(dg, proj, s_re, s_im, wb_re, wb_im, wc_re, wc_im, a_re, a_im, d, dproj, plans=()):
    def kern(dg_ref, u_ref, sr_in, si_in, wbr, wbi, wcr, wci, ar, ai, d_ref, _, du_ref, dd_ref, dwcr, dwci, dwbr, dwbi, dar, dai, lr, li, sr, si):
        u = _wide(u_ref[...])
        sr[...], si[...] = _wide(sr_in[...]), _wide(si_in[...])
        _, vjp_out = jax.vjp(fn_s5_out, sr[...], si[...], u, d_ref[...], wcr[...], wci[...])
        lr[...], li[...], du_out, dd_ref[...], dwcr[...], dwci[...] = vjp_out((_wide(dg_ref[...]),))
        dar[...], dai[...] = _adjoint_scan_in_place(lr, li, sr, si, ar[...], ai[...])
        _, vjp_in = jax.vjp(fn_s5_bu, u, wbr[...], wbi[...])
        du_in, dwbr[...], dwbi[...] = vjp_in((lr[...], li[...]))
        du_ref[...] = (du_out + du_in).astype(du_ref.dtype)

    u_spec, s_spec, wb_spec, wc_spec, a_spec, d_spec = _s5_specs()
    outs = [(SDS(dproj.shape, dproj.dtype), u_spec), (SDS(d.shape, F32), d_spec), (SDS(wc_re.shape, F32), wc_spec), (SDS(wc_im.shape, F32), wc_spec),
            (SDS(wb_re.shape, F32), wb_spec), (SDS(wb_im.shape, F32), wb_spec), (SDS(a_re.shape, F32), a_spec), (SDS(a_im.shape, F32), a_spec)]
    return hosted_call(kern, name="s5_backward", grid=(S5_BLOCKS,),
                       in_specs=[u_spec, u_spec, s_spec, s_spec, wb_spec, wb_spec, wc_spec, wc_spec, a_spec, a_spec, d_spec, ANY],
                       out_specs=[sp for _, sp in outs], out_shape=[sd for sd, _ in outs], aliases={11: 0},
                       operands=[dg, proj, s_re, s_im, wb_re, wb_im, wc_re, wc_im, a_re, a_im, d, dproj],
                       scratch_shapes=[pltpu.VMEM((L, S5_LANES), F32)] * 4,
                       cparams=pltpu.CompilerParams(vmem_limit_bytes=S5_BWD_VMEM), plans=plans)


def fn_rms(x, g):
    return (rms(x, g),)


def fn_s5_disc(lre, lim, ls):
    step = jnp.exp(ls)
    e = jnp.exp(lre * step)
    a_re, a_im = e * jnp.cos(lim * step), e * jnp.sin(lim * step)
    den = lre * lre + lim * lim
    nr, ni = a_re - 1.0, a_im
    return a_re, a_im, (nr * lre + ni * lim) / den, (ni * lre - nr * lim) / den


def _group_mask(rows, cols, row_div, col_div):
    r = lax.broadcasted_iota(jnp.int32, (rows, cols), 0) // row_div % 8
    c = lax.broadcasted_iota(jnp.int32, (rows, cols), 1) // col_div
    return r == c


def _spread(x, mask):
    w = x.shape[1]
    copy = (lax.broadcasted_iota(jnp.int32, (w, 8 * w), 1) % w == lax.broadcasted_iota(jnp.int32, (w, 8 * w), 0)).astype(F32)
    return jnp.where(mask, jnp.dot(x, copy, precision=lax.Precision.HIGHEST, preferred_element_type=F32), 0.0)


def fn_s5_bmat(b_re, b_im, coef_re, coef_im):
    mask = _group_mask(b_re.shape[0], 8 * SC, SP, SC)
    return _spread(coef_re * b_re - coef_im * b_im, mask), _spread(coef_re * b_im + coef_im * b_re, mask)


def fn_s5_cmat(c_re, c_im):
    mask = _group_mask(c_re.shape[0], 8 * SP, SC, SP)
    return _spread(c_re, mask), _spread(c_im, mask)


def fn_s5_bu(u, wb_re, wb_im):
    return mm_nt(u, wb_re), mm_nt(u, wb_im)


def fn_s5_out(sr, si, u, d, wc_re, wc_im):
    y = mm_nt(sr, wc_re) - mm_nt(si, wc_im) + d * u
    return (jax.nn.gelu(y),)


def fn_merge_glu(z, mo, gate):
    yg = z[:, :PW] * jax.nn.sigmoid(z[:, PW:])
    return (jnp.concatenate([yg, mo], axis=1) * silu(gate),)


def fn_merge(prim, mo, gate):
    return (jnp.concatenate([prim, mo], axis=1) * silu(gate),)


def fn_mem_k(kv, g):
    return (jnp.concatenate([rms(kv[:, h * XHD:(h + 1) * XHD], g) for h in range(XH)], axis=1),)


def fn_mem_attn(xq, kn, v, g):
    outs = []
    for h in range(XH):
        sl = slice(h * XHD, (h + 1) * XHD)
        p = softmax_rows(mm_nt(rms(xq[:, sl], g), kn[:, sl]) * (XHD ** -0.5))
        outs.append(mm_nn(p, v[:, sl]))
    return (jnp.concatenate(outs, axis=1),)


def _half_rms(x, g):
    lo = lax.broadcasted_iota(jnp.int32, x.shape, 1) < ROPE
    x2 = x * x
    s_lo = jnp.sum(jnp.where(lo, x2, 0.0), axis=1, keepdims=True)
    s_hi = jnp.sum(jnp.where(lo, 0.0, x2), axis=1, keepdims=True)
    return x * lax.rsqrt(jnp.where(lo, s_lo, s_hi) / ROPE + EPS) * g


def _rope(x, cos2, sin_signed):
    first = lax.broadcasted_iota(jnp.int32, x.shape, 1) % ROPE < ROPE // 2
    return x * cos2 + jnp.where(first, lane_roll(x, 128 - ROPE // 2), lane_roll(x, ROPE // 2)) * sin_signed


def fn_mla_prep(q, kv, kr, cos2, sin_signed, qnn, knn, qrn, krn):
    lo = lax.broadcasted_iota(jnp.int32, kr.shape, 1) < ROPE
    kr_pad = jnp.where(lo, _rope(_half_rms(kr, krn), cos2, sin_signed), 0.0)
    qf, kf, vs = [], [], []
    for m in range(MH // 2):
        pair = _rope(_half_rms(q[:, MH * NOPE + 128 * m:MH * NOPE + 128 * (m + 1)], qrn), cos2, sin_signed)
        for h, rope_h in ((2 * m, pair), (2 * m + 1, lane_roll(pair, ROPE))):
            qf.append(jnp.concatenate([rms(q[:, NOPE * h:NOPE * (h + 1)], qnn), jnp.where(lo, rope_h, 0.0)], axis=1))
    for h in range(MH):
        kf.append(jnp.concatenate([rms(kv[:, 256 * h:256 * h + NOPE], knn), kr_pad], axis=1))
        vs.append(kv[:, 256 * h + NOPE:256 * (h + 1)])
    return jnp.stack(qf), jnp.stack(kf), jnp.stack(vs)


ATT_TQ = 512


def _attn_scores(q, kf):
    tq = q.shape[0]
    scale = (NOPE + ROPE) ** -0.5
    own = _dg(q, kf[-tq:], 1, 1) * scale
    own = jnp.where(lax.broadcasted_iota(jnp.int32, own.shape, 1) <= lax.broadcasted_iota(jnp.int32, own.shape, 0), own, jnp.finfo(F32).min)
    return own if kf.shape[0] == tq else jnp.concatenate([_dg(q, kf[:-tq], 1, 1) * scale, own], axis=1)


def _attn_specs():
    q_spec = pl.BlockSpec((None, ATT_TQ, 256), lambda h, i: (h, i, 0))
    k_spec = pl.BlockSpec((None, L, 256), lambda h, i: (h, 0, 0))
    v_spec = pl.BlockSpec((None, L, 128), lambda h, i: (h, 0, 0))
    o_spec = pl.BlockSpec((ATT_TQ, 128), lambda h, i: (i, h))
    lse_spec = pl.BlockSpec((None, ATT_TQ, 1), lambda h, i: (h, i, 0))
    return q_spec, k_spec, v_spec, o_spec, lse_spec


def causal_attn(qf, kf, vh):
    n_tiles = L // ATT_TQ

    def kern(q_ref, k_ref, v_ref, o_ref, lse_ref):
        i = pl.program_id(1)
        for t in range(n_tiles):
            @pl.when(i == t)
            def _(t=t):
                keys = (t + 1) * ATT_TQ
                s = _attn_scores(q_ref[...], k_ref[:keys, :])
                m = jnp.max(s, axis=-1, keepdims=True)
                e = jnp.exp(s - m)
                total = jnp.sum(e, axis=-1, keepdims=True)
                o_ref[...] = (_dg(e, v_ref[:keys, :], 1, 0) / total).astype(o_ref.dtype)
                lse_ref[...] = m + jnp.log(total)

    q_spec, k_spec, v_spec, o_spec, lse_spec = _attn_specs()
    return pl.pallas_call(kern, grid=(MH, n_tiles), in_specs=[q_spec, k_spec, v_spec], out_specs=[o_spec, lse_spec],
                          out_shape=[SDS((L, MH * VD), F32), SDS((MH, L, 1), F32)], name="l1_attn", compiler_params=_cparams())(qf, kf, vh)


def causal_attn_bwd(qf, kf, vh, out, lse, dout):
    n_tiles = L // ATT_TQ
    scale = (NOPE + ROPE) ** -0.5

    def kern(q_ref, k_ref, v_ref, o_ref, lse_ref, do_ref, dq_ref, dk_ref, dv_ref):
        i = pl.program_id(1)

        @pl.when(i == 0)
        def _():
            dk_ref[...] = jnp.zeros_like(dk_ref)
            dv_ref[...] = jnp.zeros_like(dv_ref)

        for t in range(n_tiles):
            @pl.when(i == t)
            def _(t=t):
                keys = (t + 1) * ATT_TQ
                q, k, v, do = q_ref[...], k_ref[:keys, :], v_ref[:keys, :], do_ref[...]
                p = jnp.exp(_attn_scores(q, k) - lse_ref[...])
                delta = jnp.sum(do * _wide(o_ref[...]), axis=-1, keepdims=True)
                dv_ref[:keys, :] += _dg(p, do, 0, 0)
                ds = p * (_dg(do, v, 1, 1) - delta) * scale
                dq_ref[...] = _dg(ds, k, 1, 0)
                dk_ref[:keys, :] += _dg(ds, q, 0, 0)

    q_spec, k_spec, v_spec, o_spec, lse_spec = _attn_specs()
    return pl.pallas_call(kern, grid=(MH, n_tiles), in_specs=[q_spec, k_spec, v_spec, o_spec, lse_spec, o_spec],
                          out_specs=[q_spec, k_spec, v_spec], out_shape=[SDS(qf.shape, F32), SDS(kf.shape, F32), SDS(vh.shape, F32)],
                          name="l1_attn_bwd", compiler_params=_cparams())(qf, kf, vh, out, lse, dout)


def loss_and_grad(y, target, tl=256):
    def kern(y_ref, t_ref, dy_ref, loss_ref):
        d = y_ref[...] - t_ref[...]
        dy_ref[...] = d / D

        @pl.when(pl.program_id(0) == 0)
        def _():
            loss_ref[...] = jnp.zeros_like(loss_ref)

        loss_ref[...] += 0.5 * jnp.sum(jnp.sum(d * d, axis=1, keepdims=True), axis=0, keepdims=True) / D

    return pl.pallas_call(kern, grid=(L // tl,), in_specs=[rspec(tl, D), rspec(tl, D)], out_specs=[rspec(tl, D), cspec((1, 1))],
                          out_shape=[SDS((L, D), F32), SDS((1, 1), F32)], name="loss", compiler_params=_cparams())(y, target)


def adamw(name, w, g, m, v):
    rows, cols = w.shape
    block_row_bytes = 7 * 2 * 4 * max(cols, 128)
    tr = _row_tile(rows, min(2048, ADAMW_VMEM // block_row_bytes // 8 * 8), 8)

    def kern(w_ref, g_ref, m_ref, v_ref, d_ref, nm_ref, nv_ref):
        gg = g_ref[...]
        nm = ADAM_B1 * m_ref[...] + (1.0 - ADAM_B1) * gg
        nv = ADAM_B2 * v_ref[...] + (1.0 - ADAM_B2) * jnp.square(gg)
        m_hat = nm / (1.0 - ADAM_B1 ** ADAM_STEP)
        v_hat = nv / (1.0 - ADAM_B2 ** ADAM_STEP)
        d_ref[...] = -ADAM_LR * (m_hat / (jnp.sqrt(v_hat) + ADAM_EPS) + ADAM_WD * w_ref[...])
        nm_ref[...] = nm
        nv_ref[...] = nv

    spec = rspec(tr, cols)
    return hosted_call(kern, name=name, grid=(rows // tr,), in_specs=[spec] * 4, out_specs=[spec] * 3,
                       out_shape=[SDS((rows, cols), F32)] * 3, operands=[w, g, m, v])


def _row_tile(rows, cap=512, unit=16):
    return max(t for t in range(unit, cap + 1, unit) if rows % t == 0)


def _place():
    x, y, c = lax.axis_index("x"), lax.axis_index("y"), lax.axis_index("c")
    return x, y, c, [(1 - x, y), (x, 1 - y), (1 - x, 1 - y)]


def _row_chunks(rows, n, dtype):
    unit = 32 // jnp.dtype(dtype).itemsize
    base, extra = divmod(rows // unit, n)
    out, start = [], 0
    for k in range(n):
        size = (base + (k < extra)) * unit
        if size:
            out.append((start, size))
            start += size
    assert start == rows, (rows, unit)
    return out


PIECE_BYTES = 1 << 20


def _pieces(shapes_dtypes, rows_of):
    out = []
    for b, (shape, dtype) in enumerate(shapes_dtypes):
        rows = rows_of(shape)
        n = max(1, min(4, rows * shape[-1] * jnp.dtype(dtype).itemsize // PIECE_BYTES))
        out += [(b, st, sz) for st, sz in _row_chunks(rows, n, dtype)]
    return out


def all_gather_chips(name, shards):
    nb = len(shards)
    pieces = _pieces([(s.shape, s.dtype) for s in shards], lambda shape: shape[0] // 2)
    n = len(pieces)

    def body(*refs):
        x_refs, out_refs, send_sems, recv_sems = refs[:nb], refs[nb:2 * nb], refs[2 * nb], refs[2 * nb + 1]
        x, y, c, chips = _place()
        sibling = (x, y, 1 - c)
        mine = 2 * x + y

        def copy(sem, chip, cc, k, to, from_input=False):
            b, st, sz = pieces[k]
            rows_k = pl.ds(cc * (x_refs[b].shape[0] // 2) + st, sz)
            dst = out_refs[b].at[chip, rows_k, :]
            return pltpu.make_async_remote_copy(src_ref=x_refs[b].at[rows_k, :] if from_input else dst, dst_ref=dst,
                                                send_sem=send_sems.at[sem], recv_sem=recv_sems.at[sem], device_id=to, device_id_type=MESH_ID)

        order = [(k, j, 2 * cx + cy, (cx, cy, c)) for k in range(n) for j, (cx, cy) in enumerate(chips)]
        first = [copy(j * n + k, mine, c, k, to, from_input=True) for k, j, _, to in order]
        for cp in first:
            cp.start()
        passed = []
        for k, j, chip, _ in order:
            copy(j * n + k, chip, c, k, sibling).wait_recv()
            passed.append(copy((3 + j) * n + k, chip, c, k, sibling))
            passed[-1].start()
        for k, j, chip, _ in order:
            copy((3 + j) * n + k, chip, 1 - c, k, sibling).wait_recv()
        for cp in first + passed:
            cp.wait_send()

    return pl.pallas_call(body, in_specs=[ANY] * nb, out_specs=[ANY] * nb, out_shape=[SDS((4,) + s.shape, s.dtype) for s in shards],
                          scratch_shapes=_dma_sems(6 * n), name=name)(*shards)


def plan_gather_ici(shards):
    pieces = _pieces([(s.shape, s.dtype) for s in shards], lambda shape: shape[0] // 2)
    n = len(pieces)

    def copies(x_refs, out_refs, send_sems, recv_sems):
        x, y, c, chips = _place()
        mine = 2 * x + y

        def copy(j, k, chip, to, from_input):
            b, st, sz = pieces[k]
            rows_k = pl.ds(c * (x_refs[b].shape[0] // 2) + st, sz)
            dst = out_refs[b].at[chip, rows_k, :]
            return pltpu.make_async_remote_copy(src_ref=x_refs[b].at[rows_k, :] if from_input else dst, dst_ref=dst, send_sem=send_sems.at[j * n + k],
                                                recv_sem=recv_sems.at[j * n + k], device_id=to, device_id_type=MESH_ID)

        order = [(k, j, 2 * cx + cy, (cx, cy, c)) for k in range(n) for j, (cx, cy) in enumerate(chips)]
        return [copy(j, k, mine, to, True) for k, j, _, to in order], [copy(j, k, chip, to, False) for k, j, chip, to in order]

    return Plan(shards, [SDS((4,) + s.shape, s.dtype) for s in shards], {}, 3 * n, copies)


def plan_gather_pass(gathered):
    pieces = _pieces([(g.shape[1:], g.dtype) for g in gathered], lambda shape: shape[0] // 2)
    n = len(pieces)

    def copies(_, out_refs, send_sems, recv_sems):
        x, y, c, chips = _place()

        def copy(j, k, chip, cc):
            b, st, sz = pieces[k]
            rows_k = out_refs[b].at[chip, pl.ds(cc * (out_refs[b].shape[1] // 2) + st, sz), :]
            return pltpu.make_async_remote_copy(src_ref=rows_k, dst_ref=rows_k, send_sem=send_sems.at[j * n + k], recv_sem=recv_sems.at[j * n + k],
                                                device_id=(x, y, 1 - c), device_id_type=MESH_ID)

        order = [(k, j, 2 * cx + cy) for k in range(n) for j, (cx, cy) in enumerate(chips)]
        return [copy(j, k, chip, c) for k, j, chip in order], [copy(j, k, chip, 1 - c) for k, j, chip in order]

    return Plan(gathered, [SDS(g.shape, g.dtype) for g in gathered], {i: i for i in range(len(gathered))}, 3 * n, copies)


def plan_pair_exchange(gs):
    pieces = _pieces([(g.shape, g.dtype) for g in gs], lambda shape: shape[1] // 2)

    def copies(g_refs, got_refs, send_sems, recv_sems):
        x, y, c, _ = _place()
        swaps = [pltpu.make_async_remote_copy(src_ref=g_refs[b].at[:, pl.ds((1 - c) * (g_refs[b].shape[1] // 2) + st, sz), :],
                                              dst_ref=got_refs[b].at[:, pl.ds(st, sz), :], send_sem=send_sems.at[k], recv_sem=recv_sems.at[k],
                                              device_id=(x, y, 1 - c), device_id_type=MESH_ID)
                 for k, (b, st, sz) in enumerate(pieces)]
        return swaps, swaps

    return Plan(gs, [SDS((g.shape[0], g.shape[1] // 2, g.shape[2]), g.dtype) for g in gs], {}, len(pieces), copies)


def plan_chip_scatter(ps):
    pieces = _pieces([(p.shape, p.dtype) for p in ps], lambda shape: shape[1])
    n = len(pieces)

    def copies(p_refs, q_refs, send_sems, recv_sems):
        x, y, c, chips = _place()
        mine = 2 * x + y

        def copy(j, k, src_slot, dst_slot, to):
            b, st, sz = pieces[k]
            return pltpu.make_async_remote_copy(src_ref=p_refs[b].at[src_slot, pl.ds(st, sz), :], dst_ref=q_refs[b].at[dst_slot, pl.ds(st, sz), :],
                                                send_sem=send_sems.at[j * n + k], recv_sem=recv_sems.at[j * n + k], device_id=to,
                                                device_id_type=MESH_ID)

        order = [(k, j, 2 * cx + cy, (cx, cy, c)) for k in range(n) for j, (cx, cy) in enumerate(chips)]
        return [copy(j, k, chip, mine, to) for k, j, chip, to in order], [copy(j, k, mine, chip, to) for k, j, chip, to in order]

    return Plan(ps, [SDS(p.shape, p.dtype) for p in ps], {}, 3 * n, copies)


def plan_pair_join(bufs):
    pieces = _pieces([(b.shape, b.dtype) for b in bufs], lambda shape: shape[0] // 2)

    def copies(_, out_refs, send_sems, recv_sems):
        x, y, c, _ = _place()

        def copy(k, cc):
            b, st, sz = pieces[k]
            rows_k = out_refs[b].at[pl.ds(cc * (out_refs[b].shape[0] // 2) + st, sz), :]
            return pltpu.make_async_remote_copy(src_ref=rows_k, dst_ref=rows_k, send_sem=send_sems.at[k], recv_sem=recv_sems.at[k],
                                                device_id=(x, y, 1 - c), device_id_type=MESH_ID)

        return [copy(k, c) for k in range(len(pieces))], [copy(k, 1 - c) for k in range(len(pieces))]

    return Plan(bufs, [SDS(b.shape, b.dtype) for b in bufs], {i: i for i in range(len(bufs))}, len(pieces), copies)


def run_plan(name, plan):
    hosted_call(lambda: None, name=name, grid=(1,), in_specs=[], out_specs=[], out_shape=[], operands=[], plans=[plan])
    return plan.results


HBM = pl.BlockSpec(memory_space=pltpu.HBM)
SEMS = pl.BlockSpec(memory_space=pltpu.SEMAPHORE)
SPLIT_PARAMS = dict(has_side_effects=pltpu.SideEffectType.DATAFLOW_SIDE_EFFECTING)


def _plan_buffers(plan):
    in_place = {o: i for i, o in plan.aliases.items()}
    bufs = [pltpu.with_memory_space_constraint(a, pltpu.HBM) for a in plan.operands]
    where = []
    for o, sd in enumerate(plan.out_shape):
        if o in in_place:
            where.append(in_place[o])
        else:
            where.append(len(bufs))
            bufs.append(pltpu.with_memory_space_constraint(lax.empty(sd.shape, sd.dtype), pltpu.HBM))
    return bufs, where


def split_start(name, plans):
    layout = [_plan_buffers(p) for p in plans]
    counts = [len(b) for b, _ in layout]
    n_buf = sum(counts)

    def body(*refs):
        sems, token = refs[n_buf:n_buf + 2 * len(plans)], refs[-1]
        pos = 0
        for k, (p, (_, where)) in enumerate(zip(plans, layout)):
            mine = refs[pos:pos + counts[k]]
            pos += counts[k]
            sends, _ = p.copies(mine[:len(p.operands)], [mine[w] for w in where], sems[2 * k], sems[2 * k + 1])
            for cp in sends:
                cp.start()
        token[...] = jnp.zeros_like(token)

    bufs = [b for bs, _ in layout for b in bs]
    res = pl.pallas_call(
        body, name=name, in_specs=[HBM] * n_buf,
        out_specs=[SEMS] * (2 * len(plans)) + [HBM] * n_buf + [pl.BlockSpec(memory_space=pltpu.VMEM)],
        out_shape=[pltpu.SemaphoreType.DMA((p.n_sems,)) for p in plans for _ in range(2)] + [pltpu.HBM(b.shape, b.dtype) for b in bufs]
        + [SDS((8, 128), F32)],
        input_output_aliases={i: 2 * len(plans) + i for i in range(n_buf)}, compiler_params=pltpu.CompilerParams(**SPLIT_PARAMS))(*bufs)
    pos = 2 * len(plans)
    for k, p in enumerate(plans):
        p.in_flight = (res[2 * k], res[2 * k + 1], list(res[pos:pos + counts[k]]), layout[k][1])
        pos += counts[k]
    return res[-1]


def split_wait(name, plan, after):
    send_sems, recv_sems, bufs, where = plan.in_flight
    n_buf = len(bufs)

    def body(*refs):
        mine = refs[:n_buf]
        sends, recvs = plan.copies(mine[:len(plan.operands)], [mine[w] for w in where], refs[n_buf], refs[n_buf + 1])
        for cp in recvs:
            cp.wait_recv()
        for cp in sends:
            cp.wait_send()

    res = pl.pallas_call(body, name=name, in_specs=[HBM] * n_buf + [SEMS, SEMS, ANY], out_specs=[HBM] * n_buf,
                         out_shape=[pltpu.HBM(b.shape, b.dtype) for b in bufs], input_output_aliases={i: i for i in range(n_buf)},
                         compiler_params=pltpu.CompilerParams(**SPLIT_PARAMS))(*bufs, send_sems, recv_sems, after)
    plan.results = [res[w] for w in where]
    return plan.results


def pair_add(name, g, got, place):
    slots, rows, cols = g.shape
    half = rows // 2
    tr = _row_tile(half)
    nb = half // tr

    def kern(_, g_ref, t_ref, o_ref):
        o_ref[...] = (g_ref[...].astype(F32) + t_ref[...].astype(F32)).astype(o_ref.dtype)

    blk = pl.BlockSpec((None, tr, cols), lambda s, i, p: (s, i, 0))
    grid_spec = pltpu.PrefetchScalarGridSpec(
        num_scalar_prefetch=1, grid=(slots, nb),
        in_specs=[pl.BlockSpec((None, tr, cols), lambda s, i, p: (s, p[1] * nb + i, 0)), blk], out_specs=blk)
    return pl.pallas_call(kern, grid_spec=grid_spec, out_shape=SDS((slots, half, cols), g.dtype), name=name,
                          compiler_params=_cparams())(place, g, got)


def chip_add(name, p, q, place):
    slots, half, cols = p.shape
    tr = _row_tile(half)
    nb = half // tr

    def kern(_, p_ref, q1, q2, q3, o_ref):
        o_ref[...] = p_ref[...].astype(F32) + q1[...].astype(F32) + q2[...].astype(F32) + q3[...].astype(F32)

    def slot(k):
        return pl.BlockSpec((None, tr, cols), lambda i, pr: ((pr[0] + k) % slots, i, 0))

    grid_spec = pltpu.PrefetchScalarGridSpec(
        num_scalar_prefetch=1, grid=(nb,), in_specs=[slot(0), slot(1), slot(2), slot(3)],
        out_specs=pl.BlockSpec((tr, cols), lambda i, pr: (pr[1] * nb + i, 0)))
    return pl.pallas_call(kern, grid_spec=grid_spec, out_shape=SDS((2 * half, cols), F32), name=name,
                          compiler_params=_cparams())(place, p, q, q, q)


def pair_adds(tag, gs, gots, place):
    return [pair_add(f"{tag}_pair_add_{i}", g, got, place) for i, (g, got) in enumerate(zip(gs, gots))]


def chip_adds(tag, pairs, qs, place):
    return [chip_add(f"{tag}_chip_add_{i}", p, q, place) for i, (p, q) in enumerate(zip(pairs, qs))]


def reduce_scatter_chips(tag, gs, place):
    pairs = pair_adds(tag, gs, run_plan(tag + "_pair_exchange", plan_pair_exchange(gs)), place)
    return run_plan(tag + "_pair_join", plan_pair_join(chip_adds(tag, pairs, run_plan(tag + "_chip_scatter", plan_chip_scatter(pairs)), place)))


BIG = [("w_out", (2, 512, 1024)), ("w_mem_kv", (2, 256, 1024)), ("s5_w_in", (1, 1024, 1024)), ("s5_w_glu", (1, 1536, 768)),
       ("mla_w_in", (1, 1024, 848)), ("mla_w_uq", (1, 512, 576)), ("mla_w_ukv", (1, 256, 768))]
SHARDED_SMALL = [("mla_q_lora_norm", (1, 128)), ("mla_kv_lora_norm", (1, 64))]
SMALL = [("ln_gain", (2, 1024)), ("mem_norm", (2, 1024)), ("xq_norm", (2, 128)), ("xk_norm", (2, 128)),
         ("s5_lambda_re", (1, 96, 64)), ("s5_lambda_im", (1, 96, 64)), ("s5_log_step", (1, 96)),
         ("s5_b_re", (1, 96, 64, 16)), ("s5_b_im", (1, 96, 64, 16)), ("s5_c_re", (1, 96, 16, 64)), ("s5_c_im", (1, 96, 16, 64)),
         ("s5_d", (1, 1536)), ("mla_q_nope_norm", (1, 128)), ("mla_k_nope_norm", (1, 128)), ("mla_q_rope_norm", (1, 64)),
         ("mla_k_rope_norm", (1, 64))]
WEIGHT_ORDER = ["ln_gain", "w_out", "mem_norm", "w_mem_kv", "xq_norm", "xk_norm", "s5_w_in", "s5_lambda_re", "s5_lambda_im",
                "s5_log_step", "s5_b_re", "s5_b_im", "s5_c_re", "s5_c_im", "s5_d", "s5_w_glu", "mla_w_in", "mla_q_lora_norm",
                "mla_kv_lora_norm", "mla_w_uq", "mla_w_ukv", "mla_q_nope_norm", "mla_k_nope_norm", "mla_q_rope_norm", "mla_k_rope_norm"]
MINOR_LAST = {"mla_w_in": (0, 2, 1), "mla_w_uq": (0, 2, 1), "s5_b_re": (0, 2, 3, 1), "s5_b_im": (0, 2, 3, 1),
              "s5_c_re": (0, 2, 3, 1), "s5_c_im": (0, 2, 3, 1)}
SMALL_FULL = SMALL + [(n, (1, 4 * s[1])) for n, s in SHARDED_SMALL]
N_SMALL = sum(math.prod(s) for _, s in SMALL_FULL)
SMALL_ROWS, SMALL_LANES = 128, 1024

PAIR_OUT, PAIR_MKV, PAIR_ROWS = 0, 512, 768


def stack_shards(w, dtype):
    pairs = [jnp.concatenate([w["w_out"][l], w["w_mem_kv"][l]], axis=0).astype(dtype) for l in range(2)]
    return ([w["s5_w_in"][0].astype(dtype)], [pairs[0], w["s5_w_glu"][0].astype(dtype)],
            [pairs[1], w["mla_w_ukv"][0].astype(dtype), w["mla_w_in"][0].astype(dtype), w["mla_w_uq"][0].astype(dtype)])


def pair_views(pair):
    return {"w_out": Sharded(pair, "row", PAIR_OUT, 512), "w_mem_kv": Sharded(pair, "row", PAIR_MKV, 256)}


def grad_views():
    pair = SDS((4, PAIR_ROWS, 1024), BF16)
    return {"w_out": Sharded(pair, "row", PAIR_OUT, 512), "w_mem_kv": Sharded(pair, "row", PAIR_MKV, 256),
            "s5_w_in": Sharded(SDS((4, 1024, 1024), BF16), "col", 0, 1024), "s5_w_glu": Sharded(SDS((4, 1536, 768), BF16), "col", 0, 1536),
            "mla_w_ukv": Sharded(SDS((4, 256, 768), BF16), "col", 0, 256)}


def cols_to_shards(full):
    return full.reshape(full.shape[0], 4, full.shape[1] // 4).transpose(1, 0, 2)


def shards_to_cols(arr):
    return arr.transpose(1, 0, 2).reshape(arr.shape[1], 4 * arr.shape[2])


def mla_in_permute(w):
    o1, o2, o3, o4 = QL, QL + KVL, QL + KVL + ROPE, QL + KVL + ROPE + XQW
    return jnp.concatenate([w[:, o4:], w[:, :o1], w[:, o3:o4], w[:, o1:o2], w[:, o2:o3],
                            jnp.zeros((w.shape[0], MLA_IN_P - MLA_IN), w.dtype)], axis=1)


def mla_in_unpermute(d):
    return jnp.concatenate([d[:, 2048:2560], d[:, 3072:3328], d[:, 3328:3392], d[:, 2560:3072], d[:, :2048]], axis=1)


def uq_permute(w):
    w3 = w.reshape(w.shape[0], MH, NOPE + ROPE)
    return jnp.concatenate([w3[:, :, :NOPE].reshape(w.shape[0], MH * NOPE), w3[:, :, NOPE:].reshape(w.shape[0], MH * ROPE)], axis=1)


def uq_unpermute(d):
    dn = d[:, :MH * NOPE].reshape(d.shape[0], MH, NOPE)
    dr = d[:, MH * NOPE:].reshape(d.shape[0], MH, ROPE)
    return jnp.concatenate([dn, dr], axis=2).reshape(d.shape[0], MH * (NOPE + ROPE))


def time_permute(a):
    return a.reshape(SEG, SEG_LEN, a.shape[-1]).transpose(1, 0, 2).reshape(L, a.shape[-1])


def time_unpermute(a):
    return a.reshape(SEG_LEN, SEG, a.shape[-1]).transpose(1, 0, 2).reshape(L, a.shape[-1])


def mem_branch_fwd(tag, mem, mem_norm, w_mem_kv, xk_norm):
    mn = row_fwd(tag + "_mem_rms", fn_rms, ML, ML, [(mem, D, 0)], [mem_norm], [(D, BF16)])[0]
    kv = matmul(tag + "_mem_kv", mn, w_mem_kv, "nn", F32)
    kn = row_fwd(tag + "_mem_knorm", fn_mem_k, ML, ML, [(kv, XQW, 0)], [xk_norm], [(XQW, F32)])[0]
    return mn, kv, kn


def mem_branch_bwd(tag, mem, mem_norm, w_mem_kv, xk_norm, mn, kv, dkn, dv, g_view, g_wide):
    dk, dxk = row_bwd(tag + "_mem_knorm_bwd", fn_mem_k, ML, ML, [(kv, XQW, 0)], [xk_norm], [(dkn, XQW, 0)], [True], [True])
    dkv = jnp.concatenate([dk, dv], axis=1)
    dmn = matmul(tag + "_mem_kv_dx", dkv, w_mem_kv, "nt", F32)
    g_wide = matmul(tag + "_mem_kv_dw", mn, dkv, "tn", out=g_view, into=g_wide)
    dmem_norm = row_bwd(tag + "_mem_rms_bwd", fn_rms, ML, ML, [(mem, D, 0)], [mem_norm], [(dmn, D, 0)], [False], [True])[0]
    return g_wide, dmem_norm, dxk


def mem_attn_fwd(tag, proj, cb, kn, kv, xq_norm):
    return row_fwd(tag + "_mem_attn", fn_mem_attn, L, 256, [(proj, XQW, cb)], [kn, kv[:, XQW:], xq_norm], [(XQW, F32)])[0]


def mem_attn_bwd(tag, proj, cb, kn, kv, xq_norm, dmo, dproj):
    place = {"cols": proj.shape[1], "cb": cb, "into": dproj, "dtype": dproj.dtype}
    return row_bwd(tag + "_mem_attn_bwd", fn_mem_attn, L, 256, [(proj, XQW, cb)], [kn, kv[:, XQW:], xq_norm], [(dmo, XQW, 0)],
                   [place], [True, True, True])


def device_step(x, mem, positions, target, small, env, hooks=None):
    hooks = hooks or {}

    def plans_for(name):
        return hooks[("plans", name)](env) if ("plans", name) in hooks else ()

    def around(when, name, last=None):
        if (when, name) in hooks:
            hooks[(when, name)](env, last)

    g = {}
    gw = grad_views()
    ln, mem_norm, xq_norm, xk_norm = small["ln_gain"], small["mem_norm"], small["xq_norm"], small["xk_norm"]

    lre, lim = small["s5_lambda_re"][0], small["s5_lambda_im"][0]
    ls = small["s5_log_step"].reshape(SG, 1)
    one = pl.BlockSpec((SG, SP), lambda i: (0, 0))
    col = pl.BlockSpec((SG, 1), lambda i: (0, 0))
    disc_ins = [(lre, one), (lim, one), (ls, col)]
    a_re, a_im, coef_re, coef_im = stage("s5_disc", fn_s5_disc, (1,), disc_ins, [(SDS((SG, SP), F32), one)] * 4)
    b_re, b_im = small["s5_b_re"].reshape(SN, SC), small["s5_b_im"].reshape(SN, SC)
    c_re, c_im = small["s5_c_re"].reshape(PW, SP), small["s5_c_im"].reshape(PW, SP)
    bmat_rows = [(b_re, SC, 0), (b_im, SC, 0), (coef_re.reshape(SN, 1), 1, 0), (coef_im.reshape(SN, 1), 1, 0)]
    wb_re, wb_im = row_fwd("s5_bmat", fn_s5_bmat, SN, 512, bmat_rows, [], [(128, F32)] * 2)
    cmat_rows = [(c_re, SP, 0), (c_im, SP, 0)]
    wc_re, wc_im = row_fwd("s5_cmat", fn_s5_cmat, PW, 128, cmat_rows, [], [(512, F32)] * 2)
    a_re_v, a_im_v = a_re.reshape(1, SN), a_im.reshape(1, SN)
    s5_d = small["s5_d"]

    xp = time_permute(x)
    h0 = row_fwd("l0_rms", fn_rms, L, 256, [(xp, D, 0)], [ln[0:1]], [(D, BF16)])[0]
    around("before", "l0_in", wb_re)
    w_in0 = Sharded(env["in0"], "col", 0, 1024)
    proj0 = matmul("l0_in", h0, w_in0, "nn")
    s_re, s_im, g0 = s5_forward(proj0, wb_re, wb_im, wc_re, wc_im, a_re_v, a_im_v, s5_d, plans=plans_for("s5_forward"))
    around("before", "l0_glu", g0)
    w0 = dict(pair_views(env["pair0"]), s5_w_glu=Sharded(env["glu"], "col", 0, 1536))
    z0 = matmul("l0_glu", g0, w0["s5_w_glu"], "nn", plans=plans_for("l0_glu"))
    mn0, kv0, kn0 = mem_branch_fwd("l0", mem, mem_norm[0:1], w0["w_mem_kv"], xk_norm[0:1])
    mo0 = mem_attn_fwd("l0", proj0, 3, kn0, kv0, xq_norm[0:1])
    o0 = row_fwd("l0_merge", fn_merge_glu, L, 256, [(z0, 2 * PW, 0), (mo0, XQW, 0), (proj0, BW, 1)], [], [(BW, BF16)])[0]
    around("before", "l0_out", o0)
    x1p = matmul("l0_out", o0, w0["w_out"], "nn", F32, add=xp, plans=plans_for("l0_out"))
    around("after", "l0_out", x1p)
    x1 = time_unpermute(x1p)

    w1 = dict(pair_views(env["pair1"]), mla_w_ukv=Sharded(env["ukv"], "col", 0, 256))
    w_in1, w_uq = env["w_in1"], env["w_uq"]
    h1 = row_fwd("l1_rms", fn_rms, L, 256, [(x1, D, 0)], [ln[1:2]], [(D, BF16)])[0]
    proj1 = matmul("l1_in", h1, w_in1, "nn")
    qln, kvln = env["q_lora_norm"].reshape(1, QL), env["kv_lora_norm"].reshape(1, KVL)
    cqn = row_fwd("l1_q_lora_rms", fn_rms, L, 256, [(proj1, QL, 4)], [qln], [(QL, BF16)])[0]
    ckvn = row_fwd("l1_kv_lora_rms", fn_rms, L, 256, [(proj1, KVL, 12)], [kvln], [(KVL, BF16)])[0]
    q = matmul("l1_uq", cqn, w_uq, "nn")
    kv = matmul("l1_ukv", ckvn, w1["mla_w_ukv"], "nn")
    inv_freq = ROPE_THETA ** (-jnp.arange(ROPE // 2, dtype=F32) / (ROPE // 2))
    ang = positions.astype(F32)[:, None] * inv_freq
    cos2 = jnp.tile(jnp.cos(ang), (1, 4))
    sin_signed = jnp.tile(jnp.concatenate([-jnp.sin(ang), jnp.sin(ang)], axis=1), (1, 2))
    qnn, knn = small["mla_q_nope_norm"], small["mla_k_nope_norm"]
    qrn, krn = jnp.tile(small["mla_q_rope_norm"], (1, 2)), jnp.tile(small["mla_k_rope_norm"], (1, 2))
    tp = 256
    prep_ins = [(q, rspec(tp, MH * (NOPE + ROPE))), (kv, rspec(tp, MH * 256)), (proj1, rspec(tp, 128, 26)),
                (cos2, rspec(tp, 128)), (sin_signed, rspec(tp, 128))] + [(a, cspec((1, 128))) for a in (qnn, knn, qrn, krn)]
    hq_spec = pl.BlockSpec((MH, tp, 256), lambda i: (0, i, 0))
    hv_spec = pl.BlockSpec((MH, tp, 128), lambda i: (0, i, 0))
    qf, kf, vh = stage("l1_mla_prep", fn_mla_prep, (L // tp,), prep_ins,
                       [(SDS((MH, L, 256), BF16), hq_spec), (SDS((MH, L, 256), BF16), hq_spec), (SDS((MH, L, 128), BF16), hv_spec)])
    attn, attn_lse = causal_attn(qf, kf, vh)
    mn1, kv1, kn1 = mem_branch_fwd("l1", mem, mem_norm[1:2], w1["w_mem_kv"], xk_norm[1:2])
    mo1 = mem_attn_fwd("l1", proj1, 5, kn1, kv1, xq_norm[1:2])
    o1 = row_fwd("l1_merge", fn_merge, L, 256, [(attn, PW, 0), (mo1, XQW, 0), (proj1, BW, 0)], [], [(BW, BF16)])[0]
    x2 = matmul("l1_out", o1, w1["w_out"], "nn", F32, add=x1)
    dx2, loss = loss_and_grad(x2, target)

    do1 = matmul("l1_out_dx", dx2, w1["w_out"], "nt")
    g_pair1 = matmul("l1_out_dw", o1, dx2, "tn", out=gw["w_out"])
    dattn, dmo1, dproj1 = row_bwd("l1_merge_bwd", fn_merge, L, 256, [(attn, PW, 0), (mo1, XQW, 0), (proj1, BW, 0)], [],
                                  [(do1, BW, 0)], [True, True, {"cols": MLA_IN_P, "cb": 0, "dtype": BF16}], [])
    dproj1, dkn1, dv1, dxqn1 = mem_attn_bwd("l1", proj1, 5, kn1, kv1, xq_norm[1:2], dmo1, dproj1)
    env["g_pair1"], dmem_norm1, dxk1 = mem_branch_bwd("l1", mem, mem_norm[1:2], w1["w_mem_kv"], xk_norm[1:2], mn1, kv1, dkn1, dv1,
                                                      gw["w_mem_kv"], g_pair1)
    dqf, dkf, dvh = causal_attn_bwd(qf, kf, vh, attn, attn_lse, dattn)
    prep_diffs = [("row", SDS((L, MH * (NOPE + ROPE)), BF16), rspec(tp, MH * (NOPE + ROPE))), ("row", SDS((L, MH * 256), BF16), rspec(tp, MH * 256)),
                  ("row", SDS((L, MLA_IN_P), BF16), rspec(tp, 128, 26), {"into": dproj1}), None, None] + [("acc", (0,))] * 4
    dq, dkv, dproj1, dqnn, dknn, dqrn, dkrn = stage_bwd("l1_mla_prep_bwd", fn_mla_prep, (L // tp,), prep_ins,
                                                        [(dqf, hq_spec), (dkf, hq_spec), (dvh, hv_spec)], prep_diffs)
    dcqn = matmul("l1_uq_dx", dq, w_uq, "nt")
    env["dw_uq"] = matmul("l1_uq_dw", cqn, dq, "tn")
    dckvn = matmul("l1_ukv_dx", dkv, w1["mla_w_ukv"], "nt")
    env["g_ukv"] = matmul("l1_ukv_dw", ckvn, dkv, "tn", out=gw["mla_w_ukv"])
    dproj1, dqln = row_bwd("l1_q_lora_rms_bwd", fn_rms, L, 256, [(proj1, QL, 4)], [qln], [(dcqn, QL, 0)],
                           [{"cols": MLA_IN_P, "cb": 4, "into": dproj1, "dtype": BF16}], [True])
    dproj1, dkvln = row_bwd("l1_kv_lora_rms_bwd", fn_rms, L, 256, [(proj1, KVL, 12)], [kvln], [(dckvn, KVL, 0)],
                            [{"cols": MLA_IN_P, "cb": 12, "into": dproj1, "dtype": BF16}], [True])
    dh1 = matmul("l1_in_dx", dproj1, w_in1, "nt")
    env["dw_in1"] = matmul("l1_in_dw", h1, dproj1, "tn")
    dx1, dln1 = row_bwd("l1_rms_bwd", fn_rms, L, 256, [(x1, D, 0)], [ln[1:2]], [(dh1, D, 0)], [{"add": (dx2, D, 0)}], [True],
                        plans=plans_for("l1_rms_bwd"))
    around("after", "l1_rms_bwd", dx1)
    dx1p = time_permute(dx1)

    do0 = matmul("l0_out_dx", dx1p, w0["w_out"], "nt", plans=plans_for("l0_out_dx"))
    g_pair0 = matmul("l0_out_dw", o0, dx1p, "tn", out=gw["w_out"])
    dz0, dmo0, dproj0 = row_bwd("l0_merge_bwd", fn_merge_glu, L, 256, [(z0, 2 * PW, 0), (mo0, XQW, 0), (proj0, BW, 1)], [],
                                [(do0, BW, 0)], [{"dtype": BF16}, True, {"cols": 2 * BW, "cb": 1, "dtype": BF16}], [])
    dproj0, dkn0, dv0, dxqn0 = mem_attn_bwd("l0", proj0, 3, kn0, kv0, xq_norm[0:1], dmo0, dproj0)
    env["g_pair0"], dmem_norm0, dxk0 = mem_branch_bwd("l0", mem, mem_norm[0:1], w0["w_mem_kv"], xk_norm[0:1], mn0, kv0, dkn0, dv0,
                                                      gw["w_mem_kv"], g_pair0)
    env["g_glu"] = matmul("l0_glu_dw", g0, dz0, "tn", out=gw["s5_w_glu"], plans=plans_for("l0_glu_dw"))
    dg0 = matmul("l0_glu_dx", dz0, w0["s5_w_glu"], "nt", plans=plans_for("l0_glu_dx"))
    around("before", "s5_backward", dg0)
    dproj0, dd, dwc_re, dwc_im, dwb_re, dwb_im, da_re, da_im = s5_backward(dg0, proj0, s_re, s_im, wb_re, wb_im, wc_re, wc_im,
                                                                           a_re_v, a_im_v, s5_d, dproj0, plans=plans_for("s5_backward"))
    around("after", "s5_backward", dd)
    env["g_in0"] = matmul("l0_in_dw", h0, dproj0, "tn", out=gw["s5_w_in"], plans=plans_for("l0_in_dw"))
    dh0 = matmul("l0_in_dx", dproj0, w_in0, "nt", plans=plans_for("l0_in_dx"))
    around("after", "l0_in_dx", dh0)
    dxp, dln0 = row_bwd("l0_rms_bwd", fn_rms, L, 256, [(xp, D, 0)], [ln[0:1]], [(dh0, D, 0)], [{"add": (dx1p, D, 0)}], [True])
    grad_x = time_unpermute(dxp)

    db_re, db_im, dcoef_re, dcoef_im = row_bwd("s5_bmat_bwd", fn_s5_bmat, SN, 512, bmat_rows, [], [(dwb_re, 128, 0), (dwb_im, 128, 0)],
                                               [True] * 4, [], plans=plans_for("s5_bmat_bwd"))
    dc_re, dc_im = row_bwd("s5_cmat_bwd", fn_s5_cmat, PW, 128, cmat_rows, [], [(dwc_re, 512, 0), (dwc_im, 512, 0)], [True] * 2, [],
                           plans=plans_for("s5_cmat_bwd"))
    disc_cts = [(da_re.reshape(SG, SP), one), (da_im.reshape(SG, SP), one), (dcoef_re.reshape(SG, SP), one), (dcoef_im.reshape(SG, SP), one)]
    dlre, dlim, dls = stage_bwd("s5_disc_bwd", fn_s5_disc, (1,), disc_ins, disc_cts, [("acc", (0,))] * 3)

    g["ln_gain"] = jnp.concatenate([dln0, dln1], axis=0)
    g["mem_norm"] = jnp.concatenate([dmem_norm0, dmem_norm1], axis=0)
    g["xq_norm"] = jnp.concatenate([dxqn0, dxqn1], axis=0)
    g["xk_norm"] = jnp.concatenate([dxk0, dxk1], axis=0)
    g["s5_lambda_re"], g["s5_lambda_im"], g["s5_log_step"] = dlre, dlim, dls
    g["s5_b_re"], g["s5_b_im"], g["s5_c_re"], g["s5_c_im"] = db_re, db_im, dc_re, dc_im
    g["s5_d"] = dd
    g["mla_q_lora_norm"], g["mla_kv_lora_norm"] = dqln, dkvln
    g["mla_q_nope_norm"], g["mla_k_nope_norm"] = dqnn, dknn
    g["mla_q_rope_norm"] = dqrn[:, :ROPE] + dqrn[:, ROPE:]
    g["mla_k_rope_norm"] = dkrn[:, :ROPE] + dkrn[:, ROPE:]
    return loss, grad_x, g


def kernel(x, mem, positions, ln_gain, w_out, mem_norm, w_mem_kv, xq_norm, xk_norm, s5_w_in, s5_lambda_re, s5_lambda_im, s5_log_step, s5_b_re, s5_b_im, s5_c_re, s5_c_im, s5_d, s5_w_glu, mla_w_in, mla_q_lora_norm, mla_kv_lora_norm, mla_w_uq, mla_w_ukv, mla_q_nope_norm, mla_k_nope_norm, mla_q_rope_norm, mla_k_rope_norm, loss_target, m_ln_gain, m_w_out, m_mem_norm, m_w_mem_kv, m_xq_norm, m_xk_norm, m_s5_w_in, m_s5_lambda_re, m_s5_lambda_im, m_s5_log_step, m_s5_b_re, m_s5_b_im, m_s5_c_re, m_s5_c_im, m_s5_d, m_s5_w_glu, m_mla_w_in, m_mla_q_lora_norm, m_mla_kv_lora_norm, m_mla_w_uq, m_mla_w_ukv, m_mla_q_nope_norm, m_mla_k_nope_norm, m_mla_q_rope_norm, m_mla_k_rope_norm, v_ln_gain, v_w_out, v_mem_norm, v_w_mem_kv, v_xq_norm, v_xk_norm, v_s5_w_in, v_s5_lambda_re, v_s5_lambda_im, v_s5_log_step, v_s5_b_re, v_s5_b_im, v_s5_c_re, v_s5_c_im, v_s5_d, v_s5_w_glu, v_mla_w_in, v_mla_q_lora_norm, v_mla_kv_lora_norm, v_mla_w_uq, v_mla_w_ukv, v_mla_q_nope_norm, v_mla_k_nope_norm, v_mla_q_rope_norm, v_mla_k_rope_norm):
    args = dict(locals())
    wts = {n: args[n] for n in WEIGHT_ORDER}
    mom = {n: args["m_" + n] for n in WEIGHT_ORDER}
    var = {n: args["v_" + n] for n in WEIGHT_ORDER}

    chip = 2 * lax.axis_index("x") + lax.axis_index("y")
    place = jnp.stack([chip, lax.axis_index("c")]).astype(jnp.int32)

    def own_slot(gathered, shards):
        return [lax.dynamic_update_slice(g, s[None], (chip, 0, 0)) for g, s in zip(gathered, shards)]

    groups = list(stack_shards(wts, BF16))
    groups[2].append(jnp.concatenate([mla_q_lora_norm, jnp.pad(mla_kv_lora_norm, ((0, 0), (0, 64))), jnp.zeros((14, 128), F32)], axis=0))
    over_ici = [plan_gather_ici(shards) for shards in groups]
    _SCHEDULE_BEHIND.clear()
    schedule_behind(split_start("gather_start", over_ici))
    env, hooks, passed_on = {}, {}, {}

    def arrived(k, after, pass_now):
        passing = plan_gather_pass(split_wait(f"gather_wait_{k}", over_ici[k], after))
        passed_on[k] = passing
        return own_slot(run_plan(f"gather_pass_{k}", passing), groups[k]) if pass_now else None

    def need_in0(env, last):
        env["in0"], = arrived(0, last, True)

    def need_layer0(env, last):
        env["pair0"], env["glu"] = arrived(1, last, True)

    def need_layer1(env, last):
        arrived(2, last, False)

    def layer1_weights(env, last):
        pair1, ukv, in1, uq, norms = own_slot(passed_on[2].results, groups[2])
        env.update(pair1=pair1, ukv=ukv, w_in1=mla_in_permute(shards_to_cols(in1)), w_uq=uq_permute(shards_to_cols(uq)),
                   q_lora_norm=norms[:, 0, :], kv_lora_norm=norms[:, 1, :64])

    hooks["before", "l0_in"], hooks["before", "l0_glu"], hooks["before", "l0_out"] = need_in0, need_layer0, need_layer1
    hooks["plans", "l0_out"], hooks["after", "l0_out"] = (lambda env: [passed_on[2]]), layer1_weights

    rs = {}

    def swap(k, gs):
        rs[k, "g"], rs[k, "swap"] = gs, plan_pair_exchange(gs)
        return rs[k, "swap"]

    def start_scatter(k):
        rs[k, "pairs"] = pair_adds(f"rs{k}", rs[k, "g"], rs[k, "swap"].results, place)
        rs[k, "scatter"] = plan_chip_scatter(rs[k, "pairs"])
        schedule_behind(split_start(f"rs{k}_scatter_start", [rs[k, "scatter"]]))

    def join(k, after):
        rs[k, "join"] = plan_pair_join(chip_adds(f"rs{k}", rs[k, "pairs"], split_wait(f"rs{k}_scatter_wait", rs[k, "scatter"], after), place))
        return rs[k, "join"]

    hooks["plans", "l1_rms_bwd"] = lambda env: [swap(1, [env["g_pair1"], env["g_ukv"], cols_to_shards(mla_in_unpermute(env["dw_in1"])).astype(BF16),
                                                          cols_to_shards(uq_unpermute(env["dw_uq"])).astype(BF16)])]
    hooks["after", "l1_rms_bwd"] = lambda env, last: start_scatter(1)
    hooks["plans", "l0_glu_dx"] = lambda env: [swap(0, [env["g_pair0"], env["g_glu"]])]

    def before_s5_backward(env, last):
        start_scatter(0)
        env["join1"] = join(1, last)

    hooks["before", "s5_backward"] = before_s5_backward
    hooks["plans", "s5_backward"] = lambda env: [env["join1"]]
    hooks["after", "s5_backward"] = lambda env, last: env.update(join0=join(0, last))
    hooks["plans", "l0_in_dx"] = lambda env: [swap(2, [env["g_in0"]]), env["join0"]]
    hooks["after", "l0_in_dx"] = lambda env, last: start_scatter(2)

    small = {n: wts[n] for n, _ in SMALL}
    loss, grad_x, g = device_step(x[0], mem[0], positions[0], loss_target[0], small, env, hooks)
    loss = lax.psum(loss[0, 0], MESH_AXES)
    r_in0, = run_plan("rs2_pair_join", join(2, g["s5_log_step"]))
    (r_pair1, r_ukv, r_in1, r_uq), (r_pair0, r_glu) = (rs[k, "join"].results for k in (1, 0))

    small_flat = jnp.concatenate([g[n].reshape(-1) for n, _ in SMALL_FULL])
    g_small = jnp.pad(small_flat, (0, 4 * SMALL_ROWS * SMALL_LANES - N_SMALL)).astype(BF16).reshape(4, SMALL_ROWS, SMALL_LANES)
    r_small = reduce_scatter_chips("rs3", [g_small], place)[0]
    small_all = own_slot(all_gather_chips("gather_small_grads", [r_small]), [r_small])[0].reshape(-1)[:N_SMALL]

    grads = {"w_out": jnp.stack([r_pair0[:PAIR_MKV], r_pair1[:PAIR_MKV]]), "w_mem_kv": jnp.stack([r_pair0[PAIR_MKV:], r_pair1[PAIR_MKV:]]),
             "s5_w_in": r_in0[None], "s5_w_glu": r_glu[None], "mla_w_ukv": r_ukv[None], "mla_w_in": r_in1[None], "mla_w_uq": r_uq[None]}
    off = 0
    for n, s in SMALL_FULL:
        grads[n] = small_all[off:off + math.prod(s)].reshape(s)
        off += math.prod(s)
    for n, s in SHARDED_SMALL:
        grads[n] = lax.dynamic_slice(grads[n], (0, chip * s[1]), s)

    delta, new_m, new_v = {}, {}, {}
    for n, s in BIG + [(n, s) for n, s in SMALL if len(s) == 4]:
        perm = MINOR_LAST.get(n, tuple(range(len(s))))
        turned = tuple(s[p] for p in perm)
        view = lambda a: jnp.transpose(a, perm).reshape(-1, turned[-1])
        res = adamw("adamw_" + n, view(wts[n]), view(grads[n]), view(mom[n]), view(var[n]))
        delta[n], new_m[n], new_v[n] = (jnp.transpose(r.reshape(turned), tuple(perm.index(i) for i in range(len(s)))) for r in res)
    small_names = [n for n, s in SMALL if len(s) < 4] + [n for n, _ in SHARDED_SMALL]
    n_own = sum(wts[n].size for n in small_names)
    rows_own = -(-n_own // (8 * 128)) * 8

    def pack_small(d):
        flat = jnp.concatenate([d[n].reshape(-1) for n in small_names])
        return jnp.pad(flat, (0, rows_own * 128 - n_own), constant_values=1.0).reshape(rows_own, 128)

    res = adamw("adamw_small", pack_small(wts), pack_small(grads), pack_small(mom), pack_small(var))
    off = 0
    for n in small_names:
        size = wts[n].size
        delta[n], new_m[n], new_v[n] = (r.reshape(-1)[off:off + size].reshape(wts[n].shape) for r in res)
        off += size

    return (loss, grad_x[None], *[grads[n] for n in WEIGHT_ORDER], *[delta[n] for n in WEIGHT_ORDER],
            *[new_m[n] for n in WEIGHT_ORDER], *[new_v[n] for n in WEIGHT_ORDER])
```

```python
import functools
import math

import jax
import jax.numpy as jnp
from jax import lax
from jax.experimental import pallas as pl
from jax.experimental.pallas import tpu as pltpu

F32, BF16 = jnp.float32, jnp.bfloat16
SDS = jax.ShapeDtypeStruct

D = 1024
L = 2048
ML = 256
BW = 2 * D
XQW = BW // 4
PW = BW - XQW
XH, XHD = 4, 128
SG, SC, SP = 96, 16, 64
SN = SG * SP
NOPE, ROPE, VD = 128, 64, 128
MH = 12
QL, KVL = 512, 256
EPS = 1e-6
ROPE_THETA = 10000.0
MLA_IN = QL + KVL + ROPE + XQW + BW
MLA_IN_P = 3456
ADAM_LR, ADAM_B1, ADAM_B2, ADAM_EPS, ADAM_WD, ADAM_STEP = 0.001, 0.9, 0.999, 1e-08, 0.01, 10

VMEM_LIMIT = 48 * 2**20
SEG = 8
SEG_LEN = L // SEG
MESH_AXES = ("x", "y", "c")


def _cparams():
    return pltpu.CompilerParams(vmem_limit_bytes=VMEM_LIMIT)


def _dg(a, b, ca, cb):
    return lax.dot_general(a.astype(BF16), b.astype(BF16), (((ca,), (cb,)), ((), ())), preferred_element_type=F32)


@jax.custom_vjp
def mm_nn(a, b):
    return _dg(a, b, 1, 0)


mm_nn.defvjp(lambda a, b: (_dg(a, b, 1, 0), (a, b)), lambda res, g: (_dg(g, res[1], 1, 1), _dg(res[0], g, 0, 0)))


@jax.custom_vjp
def mm_nt(a, b):
    return _dg(a, b, 1, 1)


mm_nt.defvjp(lambda a, b: (_dg(a, b, 1, 1), (a, b)), lambda res, g: (_dg(g, res[1], 1, 0), _dg(g, res[0], 0, 0)))


@functools.partial(jax.custom_vjp, nondiff_argnums=(1,))
def lane_roll(x, shift):
    return pltpu.roll(x, shift, 1)


lane_roll.defvjp(lambda x, shift: (pltpu.roll(x, shift, 1), None),
                 lambda shift, _, g: (pltpu.roll(g, (128 - shift) % 128, 1),))


def rms(x, g):
    return x * lax.rsqrt(jnp.mean(x * x, axis=-1, keepdims=True) + EPS) * g


@jax.custom_vjp
def softmax_rows(s):
    e = jnp.exp(s - jnp.max(s, axis=-1, keepdims=True))
    return e / jnp.sum(e, axis=-1, keepdims=True)


def _softmax_rows_fwd(s):
    p = softmax_rows(s)
    return p, p


def _softmax_rows_bwd(p, g):
    return (p * (g - jnp.sum(g * p, axis=-1, keepdims=True)),)


softmax_rows.defvjp(_softmax_rows_fwd, _softmax_rows_bwd)


def silu(x):
    return x * jax.nn.sigmoid(x)


ANY = pl.BlockSpec(memory_space=pl.ANY)
MESH_ID = pl.DeviceIdType.MESH


def _dma_sems(n):
    return [pltpu.SemaphoreType.DMA((n,)), pltpu.SemaphoreType.DMA((n,))]


class Plan:
    def __init__(self, operands, out_shape, aliases, n_sems, copies):
        self.operands, self.out_shape, self.aliases, self.n_sems, self.copies = list(operands), list(out_shape), aliases, n_sems, copies
        self.results = None


_SCHEDULE_BEHIND = []


def schedule_behind(token):
    _SCHEDULE_BEHIND.append(token)


def hosted_call(kern, *, name, grid, in_specs, out_specs, out_shape, operands, scratch_shapes=(), aliases=None, cparams=None, plans=(), deps=()):
    n_in, n_out, n_scr = len(in_specs), len(out_specs), len(scratch_shapes)
    p_in, p_out = [len(p.operands) for p in plans], [len(p.out_shape) for p in plans]
    deps = tuple(deps) + tuple(_SCHEDULE_BEHIND)
    _SCHEDULE_BEHIND.clear()
    all_aliases = dict(aliases or {})
    in_off, out_off = n_in, n_out
    for p, ni, no in zip(plans, p_in, p_out):
        all_aliases.update({in_off + i: out_off + o for i, o in p.aliases.items()})
        in_off, out_off = in_off + ni, out_off + no

    def body(*refs):
        pos, pins, pouts = n_in, [], []
        for ni in p_in:
            pins.append(refs[pos:pos + ni])
            pos += ni
        pos += len(deps)
        main_out = refs[pos:pos + n_out]
        pos += n_out
        for no in p_out:
            pouts.append(refs[pos:pos + no])
            pos += no
        main_scr = refs[pos:pos + n_scr]
        pos += n_scr
        if plans:
            ids = [pl.program_id(ax) for ax in range(len(grid))]
            first = functools.reduce(jnp.logical_and, [i == 0 for i in ids])
            last = functools.reduce(jnp.logical_and, [i == g - 1 for i, g in zip(ids, grid)])
            copies = [p.copies(pins[k], pouts[k], refs[pos + 2 * k], refs[pos + 2 * k + 1]) for k, p in enumerate(plans)]

            @pl.when(first)
            def _():
                for sends, _ in copies:
                    for cp in sends:
                        cp.start()

        kern(*refs[:n_in], *main_out, *main_scr)
        if plans:
            @pl.when(last)
            def _():
                for sends, recvs in copies:
                    for cp in recvs:
                        cp.wait_recv()
                    for cp in sends:
                        cp.wait_send()

    res = pl.pallas_call(body, grid=grid, in_specs=list(in_specs) + [ANY] * (sum(p_in) + len(deps)),
                         out_specs=list(out_specs) + [ANY] * sum(p_out), out_shape=list(out_shape) + [s for p in plans for s in p.out_shape],
                         scratch_shapes=list(scratch_shapes) + [s for p in plans for s in _dma_sems(p.n_sems)],
                         input_output_aliases=all_aliases, name=name, compiler_params=cparams or _cparams())(
        *operands, *[a for p in plans for a in p.operands], *deps)
    pos = n_out
    for p, no in zip(plans, p_out):
        p.results = list(res[pos:pos + no])
        pos += no
    return list(res[:n_out])


def _wide(v):
    return v.astype(F32) if v.dtype == BF16 else v


def stage(name, fn, grid, ins, outs):
    n_in = len(ins)

    def kern(*refs):
        res = fn(*[_wide(r[...]) for r in refs[:n_in]])
        for r, v in zip(refs[n_in:], res):
            r[...] = v.astype(r.dtype)

    return hosted_call(kern, name=name, grid=grid, in_specs=[s for _, s in ins], out_specs=[s for _, s in outs],
                       out_shape=[sd for sd, _ in outs], operands=[a for a, _ in ins])


def stage_bwd(name, fn, grid, ins, cts, diffs, plans=()):
    n_in, n_ct = len(ins), len(cts)
    didx = [i for i, d in enumerate(diffs) if d is not None]
    opts = {i: (diffs[i][3] if len(diffs[i]) > 3 else {}) for i in didx if diffs[i][0] == "row"}
    adds = [(i, opts[i]["add"]) for i in opts if "add" in opts[i]]
    intos = [(i, opts[i]["into"]) for i in opts if "into" in opts[i]]
    n_add, n_into = len(adds), len(intos)
    add_pos = {i: n_in + n_ct + k for k, (i, _) in enumerate(adds)}
    n_extra = n_in + n_ct + n_add + n_into

    def kern(*refs):
        vals = [_wide(r[...]) for r in refs[:n_in]]

        def f(*dv):
            full = list(vals)
            for i, v in zip(didx, dv):
                full[i] = v
            return fn(*full)

        _, vjp = jax.vjp(f, *[vals[i].astype(F32) for i in didx])
        gs = vjp(tuple(c[...].astype(F32) for c in refs[n_in:n_in + n_ct]))
        for o_ref, i, g in zip(refs[n_extra:], didx, gs):
            if diffs[i][0] == "row":
                if i in add_pos:
                    g = g + refs[add_pos[i]][...].astype(F32)
                o_ref[...] = g.astype(o_ref.dtype)
            else:
                first = functools.reduce(jnp.logical_and, [pl.program_id(ax) == 0 for ax in diffs[i][1]])

                @pl.when(first)
                def _():
                    o_ref[...] = g

                @pl.when(jnp.logical_not(first))
                def _():
                    o_ref[...] += g

    out_shape, out_specs = [], []
    for i in didx:
        if diffs[i][0] == "row":
            out_shape.append(diffs[i][1])
            out_specs.append(diffs[i][2])
        else:
            out_shape.append(SDS(ins[i][0].shape, F32))
            out_specs.append(ins[i][1])
    aliases = {n_in + n_ct + n_add + k: didx.index(i) for k, (i, _) in enumerate(intos)}
    in_specs = [s for _, s in ins] + [s for _, s in cts] + [s for _, (_, s) in adds] + [ANY] * n_into
    operands = [a for a, _ in ins] + [a for a, _ in cts] + [a for _, (a, _) in adds] + [a for _, a in intos]
    return hosted_call(kern, name=name, grid=grid, in_specs=in_specs, out_specs=out_specs, out_shape=out_shape, operands=operands,
                       aliases=aliases, plans=plans)


def rspec(tl, w, cb=0):
    return pl.BlockSpec((tl, w), lambda i: (i, cb))


def cspec(shape):
    return pl.BlockSpec(shape, lambda i: (0,) * len(shape))


def row_fwd(name, fn, rows, tl, row_ins, consts, outs):
    ins = [(a, rspec(tl, w, cb)) for a, w, cb in row_ins] + [(a, cspec(a.shape)) for a in consts]
    return stage(name, fn, (rows // tl,), ins, [(SDS((rows, w), dt), rspec(tl, w)) for w, dt in outs])


def row_bwd(name, fn, rows, tl, row_ins, consts, cts, row_diff, const_diff, plans=()):
    ins = [(a, rspec(tl, w, cb)) for a, w, cb in row_ins] + [(a, cspec(a.shape)) for a in consts]
    diffs = []
    for (a, w, cb), d in zip(row_ins, row_diff):
        if not d:
            diffs.append(None)
            continue
        d = d if isinstance(d, dict) else {}
        opts = {}
        if "add" in d:
            opts["add"] = (d["add"][0], rspec(tl, d["add"][1], d["add"][2]))
        if d.get("into") is not None:
            opts["into"] = d["into"]
        diffs.append(("row", SDS((rows, d.get("cols", w)), d.get("dtype", F32)), rspec(tl, w, d.get("cb", 0)), opts))
    diffs += [("acc", (0,)) if d else None for d in const_diff]
    return stage_bwd(name, fn, (rows // tl,), ins, [(a, rspec(tl, w, cb)) for a, w, cb in cts], diffs, plans=plans)


MATMUL_VMEM = 36 * 2**20
ADAMW_VMEM = 28 * 2**20


class Sharded:
    def __init__(self, arr, kind, roff, rows):
        self.arr, self.kind, self.roff, self.rows, self.n = arr, kind, roff, rows, arr.shape[2]
        self.shape = (rows, 4 * self.n) if kind == "col" else (4 * rows, self.n)

    def fits(self, t0, t1):
        return self.roff % t0 == 0 and self.rows % t0 == 0 and self.n % t1 == 0

    def spec(self, t0, t1, bidx):
        assert self.fits(t0, t1), (self.kind, self.roff, self.rows, self.n, t0, t1)
        r0 = self.roff // t0
        if self.kind == "col":
            per = self.n // t1
            return pl.BlockSpec((None, t0, t1), lambda *g: (bidx(*g)[1] // per, r0 + bidx(*g)[0], bidx(*g)[1] % per))
        per = self.rows // t0
        return pl.BlockSpec((None, t0, t1), lambda *g: (bidx(*g)[0] // per, r0 + bidx(*g)[0] % per, bidx(*g)[1]))


def matmul(name, a, b, mode, out_dtype=BF16, add=None, out=None, into=None, plans=()):
    if mode == "tn":
        k_dim, m = a.shape
    else:
        m, k_dim = a.shape
    n = b.shape[0] if mode == "nt" else b.shape[1]
    b_fit = b.fits if isinstance(b, Sharded) else (lambda t0, t1: True)
    o_fit = out.fits if out is not None else (lambda t0, t1: True)
    a_bytes, b_bytes = jnp.dtype(a.dtype).itemsize, jnp.dtype(b.arr.dtype if isinstance(b, Sharded) else b.dtype).itemsize
    o_bytes = jnp.dtype(out_dtype if out is None else out.arr.dtype).itemsize

    def vmem(tm, tn, tk):
        return 2 * (tm * tk * a_bytes + tk * tn * b_bytes + tm * tn * (o_bytes + (4 if add is not None else 0))) + 4 * tm * tn * (1 + (tk < k_dim))

    tiles = [(tm, tn, tk) for tm in (2048, 1024, 512, 256, 128) for tn in (1024, 768, 512, 384, 256, 128)
             for tk in sorted({k_dim, 1024, 768, 512, 384, 256, 128})
             if m % tm == 0 and n % tn == 0 and k_dim % tk == 0 and (b_fit(tn, tk) if mode == "nt" else b_fit(tk, tn)) and o_fit(tm, tn)
             and vmem(tm, tn, tk) <= MATMUL_VMEM]
    tm, tn, tk = max(tiles, key=lambda t: (t[2] == k_dim, t[0] * t[1] * t[2], t[0] * t[1]))
    nk = k_dim // tk
    a_spec = pl.BlockSpec((tk, tm), lambda i, j, k: (k, i)) if mode == "tn" else pl.BlockSpec((tm, tk), lambda i, j, k: (i, k))
    if isinstance(b, Sharded):
        b_spec = b.spec(tn, tk, lambda i, j, k: (j, k)) if mode == "nt" else b.spec(tk, tn, lambda i, j, k: (k, j))
        b = b.arr
    else:
        b_spec = pl.BlockSpec((tn, tk), lambda i, j, k: (j, k)) if mode == "nt" else pl.BlockSpec((tk, tn), lambda i, j, k: (k, j))
    o_spec = pl.BlockSpec((tm, tn), lambda i, j, k: (i, j))
    out_spec, out_shape = (o_spec, SDS((m, n), out_dtype)) if out is None else (out.spec(tm, tn, lambda i, j, k: (i, j)), out.arr)
    ca, cb = {"nn": (1, 0), "nt": (1, 1), "tn": (0, 0)}[mode]
    n_in = 2 + (add is not None)

    def finish(refs, o_ref, r):
        if add is not None:
            r = r + refs[2][...]
        o_ref[...] = r.astype(o_ref.dtype)

    def kern_whole(*refs):
        finish(refs, refs[-1], _dg(refs[0][...], refs[1][...], ca, cb))

    def kern_cut(*refs):
        o_ref, acc = refs[-2], refs[-1]
        k = pl.program_id(2)

        @pl.when(k == 0)
        def _():
            acc[...] = jnp.zeros_like(acc)

        acc[...] += _dg(refs[0][...], refs[1][...], ca, cb)

        @pl.when(k == nk - 1)
        def _():
            finish(refs, o_ref, acc[...])

    ins, specs = [a, b], [a_spec, b_spec]
    if add is not None:
        ins.append(add)
        specs.append(o_spec)
    if into is not None:
        ins.append(into)
        specs.append(ANY)
    return hosted_call(kern_whole if nk == 1 else kern_cut, name=name, grid=(m // tm, n // tn, nk), in_specs=specs, out_specs=[out_spec],
                       out_shape=[out_shape], operands=ins, scratch_shapes=[] if nk == 1 else [pltpu.VMEM((tm, tn), F32)],
                       aliases={} if into is None else {n_in: 0}, plans=plans)[0]


def _cmul(ar, ai, br, bi):
    return ar * br - ai * bi, ar * bi + ai * br


def _sub_shift(x, down):
    row = lax.broadcasted_iota(jnp.int32, x.shape, 0)
    if down:
        return jnp.where(row == 0, 0.0, pltpu.roll(x, 1, 0))
    return jnp.where(row == SEG - 1, 0.0, pltpu.roll(x, SEG - 1, 0))


def _pow_seg_len(ar, ai):
    for _ in range(int(math.log2(SEG_LEN))):
        ar, ai = _cmul(ar, ai, ar, ai)
    return ar, ai


def _scan_in_place(sr, si, a_re, a_im):
    lanes = sr.shape[1]
    ar = jnp.broadcast_to(a_re, (SEG, lanes))
    ai = jnp.broadcast_to(a_im, (SEG, lanes))
    zero = jnp.zeros((SEG, lanes), F32)

    def local(i, carry):
        rows = pl.ds(pl.multiple_of(i * SEG, SEG), SEG)
        mr, mi = _cmul(ar, ai, carry[0], carry[1])
        nr, ni = mr + sr[rows, :], mi + si[rows, :]
        sr[rows, :] = nr
        si[rows, :] = ni
        return nr, ni

    fr, fi = lax.fori_loop(0, SEG_LEN, local, (zero, zero))
    pr, pi = _pow_seg_len(ar, ai)
    ir, ii = zero, zero
    for _ in range(SEG - 1):
        mr, mi = _cmul(pr, pi, ir, ii)
        ir, ii = _sub_shift(mr + fr, True), _sub_shift(mi + fi, True)

    def carry_in(i, pw):
        rows = pl.ds(pl.multiple_of(i * SEG, SEG), SEG)
        cr, ci = _cmul(pw[0], pw[1], ir, ii)
        sr[rows, :] += cr
        si[rows, :] += ci
        return _cmul(pw[0], pw[1], ar, ai)

    lax.fori_loop(0, SEG_LEN, carry_in, (ar, ai))


S5_LANES = 8 * SP
S5_BLOCKS = SN // S5_LANES


def _s5_specs():
    u_spec = pl.BlockSpec((L, 8 * SC), lambda j: (0, j))
    s_spec = pl.BlockSpec((L, S5_LANES), lambda j: (0, j))
    wb_spec = pl.BlockSpec((S5_LANES, 8 * SC), lambda j: (j, 0))
    wc_spec = pl.BlockSpec((8 * SC, S5_LANES), lambda j: (j, 0))
    a_spec = pl.BlockSpec((1, S5_LANES), lambda j: (0, j))
    d_spec = pl.BlockSpec((1, 8 * SC), lambda j: (0, j))
    return u_spec, s_spec, wb_spec, wc_spec, a_spec, d_spec


def s5_forward(proj, wb_re, wb_im, wc_re, wc_im, a_re, a_im, d, plans=()):
    def kern(u_ref, wbr, wbi, wcr, wci, ar, ai, d_ref, sr_out, si_out, g_ref, sr, si):
        u = _wide(u_ref[...])
        sr[...], si[...] = fn_s5_bu(u, wbr[...], wbi[...])
        _scan_in_place(sr, si, ar[...], ai[...])
        g_ref[...] = fn_s5_out(sr[...], si[...], u, d_ref[...], wcr[...], wci[...])[0].astype(g_ref.dtype)
        sr_out[...] = sr[...].astype(sr_out.dtype)
        si_out[...] = si[...].astype(si_out.dtype)

    u_spec, s_spec, wb_spec, wc_spec, a_spec, d_spec = _s5_specs()
    return hosted_call(kern, name="s5_forward", grid=(S5_BLOCKS,), in_specs=[u_spec, wb_spec, wb_spec, wc_spec, wc_spec, a_spec, a_spec, d_spec],
                       out_specs=[s_spec, s_spec, u_spec], out_shape=[SDS((L, SN), BF16)] * 2 + [SDS((L, PW), BF16)],
                       operands=[proj, wb_re, wb_im, wc_re, wc_im, a_re, a_im, d], scratch_shapes=[pltpu.VMEM((L, S5_LANES), F32)] * 2,
                       plans=plans)


def _adjoint_scan_in_place(lr, li, sr, si, a_re, a_im):
    lanes = lr.shape[1]
    ar = jnp.broadcast_to(a_re, (SEG, lanes))
    ai = -jnp.broadcast_to(a_im, (SEG, lanes))
    zero = jnp.zeros((SEG, lanes), F32)

    def local(k, carry):
        i = SEG_LEN - 1 - k
        rows = pl.ds(pl.multiple_of(i * SEG, SEG), SEG)
        mr, mi = _cmul(ar, ai, carry[0], carry[1])
        nr, ni = mr + lr[rows, :], mi + li[rows, :]
        lr[rows, :] = nr
        li[rows, :] = ni
        return nr, ni

    fr, fi = lax.fori_loop(0, SEG_LEN, local, (zero, zero))
    pr, pi = _pow_seg_len(ar, ai)
    ir, ii = zero, zero
    for _ in range(SEG - 1):
        mr, mi = _cmul(pr, pi, ir, ii)
        ir, ii = _sub_shift(mr + fr, False), _sub_shift(mi + fi, False)

    def fix(rows, pw):
        cr, ci = _cmul(pw[0], pw[1], ir, ii)
        tr, ti = lr[rows, :] + cr, li[rows, :] + ci
        lr[rows, :] = tr
        li[rows, :] = ti
        return tr, ti

    def grad_a(tr, ti, spr, spi, acc):
        return acc[0] + tr * spr + ti * spi, acc[1] + ti * spr - tr * spi

    def carry_in(k, c):
        i = SEG_LEN - 1 - k
        rows = pl.ds(pl.multiple_of(i * SEG, SEG), SEG)
        prev = pl.ds(pl.multiple_of((i - 1) * SEG, SEG), SEG)
        tr, ti = fix(rows, (c[0], c[1]))
        acc = grad_a(tr, ti, sr[prev, :], si[prev, :], (c[2], c[3]))
        nr, ni = _cmul(c[0], c[1], ar, ai)
        return nr, ni, acc[0], acc[1]

    pwr, pwi, accr, acci = lax.fori_loop(0, SEG_LEN - 1, carry_in, (ar, ai, zero, zero))
    tr, ti = fix(pl.ds(0, SEG), (pwr, pwi))
    last = pl.ds((SEG_LEN - 1) * SEG, SEG)
    accr, acci = grad_a(tr, ti, _sub_shift(sr[last, :], True), _sub_shift(si[last, :], True), (accr, acci))
    return jnp.sum(accr, axis=0, keepdims=True), jnp.sum(acci, axis=0, keepdims=True)


S5_BWD_VMEM = 58 * 2**20


def s5_backward(dg, proj, s_re, s_im, wb_re, wb_im, wc_re, wc_im, a_re, a_im, d, dproj, plans=()):
    def kern(dg_ref, u_ref, sr_in, si_in, wbr, wbi, wcr, wci, ar, ai, d_ref, _, du_ref, dd_ref, dwcr, dwci, dwbr, dwbi, dar, dai, lr, li, sr, si):
        u = _wide(u_ref[...])
        sr[...], si[...] = _wide(sr_in[...]), _wide(si_in[...])
        _, vjp_out = jax.vjp(fn_s5_out, sr[...], si[...], u, d_ref[...], wcr[...], wci[...])
        lr[...], li[...], du_out, dd_ref[...], dwcr[...], dwci[...] = vjp_out((_wide(dg_ref[...]),))
        dar[...], dai[...] = _adjoint_scan_in_place(lr, li, sr, si, ar[...], ai[...])
        _, vjp_in = jax.vjp(fn_s5_bu, u, wbr[...], wbi[...])
        du_in, dwbr[...], dwbi[...] = vjp_in((lr[...], li[...]))
        du_ref[...] = (du_out + du_in).astype(du_ref.dtype)

    u_spec, s_spec, wb_spec, wc_spec, a_spec, d_spec = _s5_specs()
    outs = [(SDS(dproj.shape, dproj.dtype), u_spec), (SDS(d.shape, F32), d_spec), (SDS(wc_re.shape, F32), wc_spec), (SDS(wc_im.shape, F32), wc_spec),
            (SDS(wb_re.shape, F32), wb_spec), (SDS(wb_im.shape, F32), wb_spec), (SDS(a_re.shape, F32), a_spec), (SDS(a_im.shape, F32), a_spec)]
    return hosted_call(kern, name="s5_backward", grid=(S5_BLOCKS,),
                       in_specs=[u_spec, u_spec, s_spec, s_spec, wb_spec, wb_spec, wc_spec, wc_spec, a_spec, a_spec, d_spec, ANY],
                       out_specs=[sp for _, sp in outs], out_shape=[sd for sd, _ in outs], aliases={11: 0},
                       operands=[dg, proj, s_re, s_im, wb_re, wb_im, wc_re, wc_im, a_re, a_im, d, dproj],
                       scratch_shapes=[pltpu.VMEM((L, S5_LANES), F32)] * 4,
                       cparams=pltpu.CompilerParams(vmem_limit_bytes=S5_BWD_VMEM), plans=plans)


def fn_rms(x, g):
    return (rms(x, g),)


def fn_s5_disc(lre, lim, ls):
    step = jnp.exp(ls)
    e = jnp.exp(lre * step)
    a_re, a_im = e * jnp.cos(lim * step), e * jnp.sin(lim * step)
    den = lre * lre + lim * lim
    nr, ni = a_re - 1.0, a_im
    return a_re, a_im, (nr * lre + ni * lim) / den, (ni * lre - nr * lim) / den


def _group_mask(rows, cols, row_div, col_div):
    r = lax.broadcasted_iota(jnp.int32, (rows, cols), 0) // row_div % 8
    c = lax.broadcasted_iota(jnp.int32, (rows, cols), 1) // col_div
    return r == c


def _spread(x, mask):
    w = x.shape[1]
    copy = (lax.broadcasted_iota(jnp.int32, (w, 8 * w), 1) % w == lax.broadcasted_iota(jnp.int32, (w, 8 * w), 0)).astype(F32)
    return jnp.where(mask, jnp.dot(x, copy, precision=lax.Precision.HIGHEST, preferred_element_type=F32), 0.0)


def fn_s5_bmat(b_re, b_im, coef_re, coef_im):
    mask = _group_mask(b_re.shape[0], 8 * SC, SP, SC)
    return _spread(coef_re * b_re - coef_im * b_im, mask), _spread(coef_re * b_im + coef_im * b_re, mask)


def fn_s5_cmat(c_re, c_im):
    mask = _group_mask(c_re.shape[0], 8 * SP, SC, SP)
    return _spread(c_re, mask), _spread(c_im, mask)


def fn_s5_bu(u, wb_re, wb_im):
    return mm_nt(u, wb_re), mm_nt(u, wb_im)


def fn_s5_out(sr, si, u, d, wc_re, wc_im):
    y = mm_nt(sr, wc_re) - mm_nt(si, wc_im) + d * u
    return (jax.nn.gelu(y),)


def fn_merge_glu(z, mo, gate):
    yg = z[:, :PW] * jax.nn.sigmoid(z[:, PW:])
    return (jnp.concatenate([yg, mo], axis=1) * silu(gate),)


def fn_merge(prim, mo, gate):
    return (jnp.concatenate([prim, mo], axis=1) * silu(gate),)


def fn_mem_k(kv, g):
    return (jnp.concatenate([rms(kv[:, h * XHD:(h + 1) * XHD], g) for h in range(XH)], axis=1),)


def fn_mem_attn(xq, kn, v, g):
    outs = []
    for h in range(XH):
        sl = slice(h * XHD, (h + 1) * XHD)
        p = softmax_rows(mm_nt(rms(xq[:, sl], g), kn[:, sl]) * (XHD ** -0.5))
        outs.append(mm_nn(p, v[:, sl]))
    return (jnp.concatenate(outs, axis=1),)


def _half_rms(x, g):
    lo = lax.broadcasted_iota(jnp.int32, x.shape, 1) < ROPE
    x2 = x * x
    s_lo = jnp.sum(jnp.where(lo, x2, 0.0), axis=1, keepdims=True)
    s_hi = jnp.sum(jnp.where(lo, 0.0, x2), axis=1, keepdims=True)
    return x * lax.rsqrt(jnp.where(lo, s_lo, s_hi) / ROPE + EPS) * g


def _rope(x, cos2, sin_signed):
    first = lax.broadcasted_iota(jnp.int32, x.shape, 1) % ROPE < ROPE // 2
    return x * cos2 + jnp.where(first, lane_roll(x, 128 - ROPE // 2), lane_roll(x, ROPE // 2)) * sin_signed


def fn_mla_prep(q, kv, kr, cos2, sin_signed, qnn, knn, qrn, krn):
    lo = lax.broadcasted_iota(jnp.int32, kr.shape, 1) < ROPE
    kr_pad = jnp.where(lo, _rope(_half_rms(kr, krn), cos2, sin_signed), 0.0)
    qf, kf, vs = [], [], []
    for m in range(MH // 2):
        pair = _rope(_half_rms(q[:, MH * NOPE + 128 * m:MH * NOPE + 128 * (m + 1)], qrn), cos2, sin_signed)
        for h, rope_h in ((2 * m, pair), (2 * m + 1, lane_roll(pair, ROPE))):
            qf.append(jnp.concatenate([rms(q[:, NOPE * h:NOPE * (h + 1)], qnn), jnp.where(lo, rope_h, 0.0)], axis=1))
    for h in range(MH):
        kf.append(jnp.concatenate([rms(kv[:, 256 * h:256 * h + NOPE], knn), kr_pad], axis=1))
        vs.append(kv[:, 256 * h + NOPE:256 * (h + 1)])
    return jnp.stack(qf), jnp.stack(kf), jnp.stack(vs)


ATT_TQ = 512


def _attn_scores(q, kf):
    tq = q.shape[0]
    scale = (NOPE + ROPE) ** -0.5
    own = _dg(q, kf[-tq:], 1, 1) * scale
    own = jnp.where(lax.broadcasted_iota(jnp.int32, own.shape, 1) <= lax.broadcasted_iota(jnp.int32, own.shape, 0), own, jnp.finfo(F32).min)
    return own if kf.shape[0] == tq else jnp.concatenate([_dg(q, kf[:-tq], 1, 1) * scale, own], axis=1)


def _attn_specs():
    q_spec = pl.BlockSpec((None, ATT_TQ, 256), lambda h, i: (h, i, 0))
    k_spec = pl.BlockSpec((None, L, 256), lambda h, i: (h, 0, 0))
    v_spec = pl.BlockSpec((None, L, 128), lambda h, i: (h, 0, 0))
    o_spec = pl.BlockSpec((ATT_TQ, 128), lambda h, i: (i, h))
    lse_spec = pl.BlockSpec((None, ATT_TQ, 1), lambda h, i: (h, i, 0))
    return q_spec, k_spec, v_spec, o_spec, lse_spec


def causal_attn(qf, kf, vh):
    n_tiles = L // ATT_TQ

    def kern(q_ref, k_ref, v_ref, o_ref, lse_ref):
        i = pl.program_id(1)
        for t in range(n_tiles):
            @pl.when(i == t)
            def _(t=t):
                keys = (t + 1) * ATT_TQ
                s = _attn_scores(q_ref[...], k_ref[:keys, :])
                m = jnp.max(s, axis=-1, keepdims=True)
                e = jnp.exp(s - m)
                total = jnp.sum(e, axis=-1, keepdims=True)
                o_ref[...] = (_dg(e, v_ref[:keys, :], 1, 0) / total).astype(o_ref.dtype)
                lse_ref[...] = m + jnp.log(total)

    q_spec, k_spec, v_spec, o_spec, lse_spec = _attn_specs()
    return pl.pallas_call(kern, grid=(MH, n_tiles), in_specs=[q_spec, k_spec, v_spec], out_specs=[o_spec, lse_spec],
                          out_shape=[SDS((L, MH * VD), F32), SDS((MH, L, 1), F32)], name="l1_attn", compiler_params=_cparams())(qf, kf, vh)


def causal_attn_bwd(qf, kf, vh, out, lse, dout):
    n_tiles = L // ATT_TQ
    scale = (NOPE + ROPE) ** -0.5

    def kern(q_ref, k_ref, v_ref, o_ref, lse_ref, do_ref, dq_ref, dk_ref, dv_ref):
        i = pl.program_id(1)

        @pl.when(i == 0)
        def _():
            dk_ref[...] = jnp.zeros_like(dk_ref)
            dv_ref[...] = jnp.zeros_like(dv_ref)

        for t in range(n_tiles):
            @pl.when(i == t)
            def _(t=t):
                keys = (t + 1) * ATT_TQ
                q, k, v, do = q_ref[...], k_ref[:keys, :], v_ref[:keys, :], do_ref[...]
                p = jnp.exp(_attn_scores(q, k) - lse_ref[...])
                delta = jnp.sum(do * _wide(o_ref[...]), axis=-1, keepdims=True)
                dv_ref[:keys, :] += _dg(p, do, 0, 0)
                ds = p * (_dg(do, v, 1, 1) - delta) * scale
                dq_ref[...] = _dg(ds, k, 1, 0)
                dk_ref[:keys, :] += _dg(ds, q, 0, 0)

    q_spec, k_spec, v_spec, o_spec, lse_spec = _attn_specs()
    return pl.pallas_call(kern, grid=(MH, n_tiles), in_specs=[q_spec, k_spec, v_spec, o_spec, lse_spec, o_spec],
                          out_specs=[q_spec, k_spec, v_spec], out_shape=[SDS(qf.shape, F32), SDS(kf.shape, F32), SDS(vh.shape, F32)],
                          name="l1_attn_bwd", compiler_params=_cparams())(qf, kf, vh, out, lse, dout)


def loss_and_grad(y, target, tl=256):
    def kern(y_ref, t_ref, dy_ref, loss_ref):
        d = y_ref[...] - t_ref[...]
        dy_ref[...] = d / D

        @pl.when(pl.program_id(0) == 0)
        def _():
            loss_ref[...] = jnp.zeros_like(loss_ref)

        loss_ref[...] += 0.5 * jnp.sum(jnp.sum(d * d, axis=1, keepdims=True), axis=0, keepdims=True) / D

    return pl.pallas_call(kern, grid=(L // tl,), in_specs=[rspec(tl, D), rspec(tl, D)], out_specs=[rspec(tl, D), cspec((1, 1))],
                          out_shape=[SDS((L, D), F32), SDS((1, 1), F32)], name="loss", compiler_params=_cparams())(y, target)


def adamw(name, w, g, m, v):
    rows, cols = w.shape
    block_row_bytes = 7 * 2 * 4 * max(cols, 128)
    tr = _row_tile(rows, min(2048, ADAMW_VMEM // block_row_bytes // 8 * 8), 8)

    def kern(w_ref, g_ref, m_ref, v_ref, d_ref, nm_ref, nv_ref):
        gg = g_ref[...]
        nm = ADAM_B1 * m_ref[...] + (1.0 - ADAM_B1) * gg
        nv = ADAM_B2 * v_ref[...] + (1.0 - ADAM_B2) * jnp.square(gg)
        m_hat = nm / (1.0 - ADAM_B1 ** ADAM_STEP)
        v_hat = nv / (1.0 - ADAM_B2 ** ADAM_STEP)
        d_ref[...] = -ADAM_LR * (m_hat / (jnp.sqrt(v_hat) + ADAM_EPS) + ADAM_WD * w_ref[...])
        nm_ref[...] = nm
        nv_ref[...] = nv

    spec = rspec(tr, cols)
    return hosted_call(kern, name=name, grid=(rows // tr,), in_specs=[spec] * 4, out_specs=[spec] * 3,
                       out_shape=[SDS((rows, cols), F32)] * 3, operands=[w, g, m, v])


def _row_tile(rows, cap=512, unit=16):
    return max(t for t in range(unit, cap + 1, unit) if rows % t == 0)


def _place():
    x, y, c = lax.axis_index("x"), lax.axis_index("y"), lax.axis_index("c")
    return x, y, c, [(1 - x, y), (x, 1 - y), (1 - x, 1 - y)]


def _row_chunks(rows, n, dtype):
    unit = 32 // jnp.dtype(dtype).itemsize
    base, extra = divmod(rows // unit, n)
    out, start = [], 0
    for k in range(n):
        size = (base + (k < extra)) * unit
        if size:
            out.append((start, size))
            start += size
    assert start == rows, (rows, unit)
    return out


PIECE_BYTES = 1 << 20


def _pieces(shapes_dtypes, rows_of):
    out = []
    for b, (shape, dtype) in enumerate(shapes_dtypes):
        rows = rows_of(shape)
        n = max(1, min(4, rows * shape[-1] * jnp.dtype(dtype).itemsize // PIECE_BYTES))
        out += [(b, st, sz) for st, sz in _row_chunks(rows, n, dtype)]
    return out


def all_gather_chips(name, shards):
    nb = len(shards)
    pieces = _pieces([(s.shape, s.dtype) for s in shards], lambda shape: shape[0] // 2)
    n = len(pieces)

    def body(*refs):
        x_refs, out_refs, send_sems, recv_sems = refs[:nb], refs[nb:2 * nb], refs[2 * nb], refs[2 * nb + 1]
        x, y, c, chips = _place()
        sibling = (x, y, 1 - c)
        mine = 2 * x + y

        def copy(sem, chip, cc, k, to, from_input=False):
            b, st, sz = pieces[k]
            rows_k = pl.ds(cc * (x_refs[b].shape[0] // 2) + st, sz)
            dst = out_refs[b].at[chip, rows_k, :]
            return pltpu.make_async_remote_copy(src_ref=x_refs[b].at[rows_k, :] if from_input else dst, dst_ref=dst,
                                                send_sem=send_sems.at[sem], recv_sem=recv_sems.at[sem], device_id=to, device_id_type=MESH_ID)

        order = [(k, j, 2 * cx + cy, (cx, cy, c)) for k in range(n) for j, (cx, cy) in enumerate(chips)]
        first = [copy(j * n + k, mine, c, k, to, from_input=True) for k, j, _, to in order]
        for cp in first:
            cp.start()
        passed = []
        for k, j, chip, _ in order:
            copy(j * n + k, chip, c, k, sibling).wait_recv()
            passed.append(copy((3 + j) * n + k, chip, c, k, sibling))
            passed[-1].start()
        for k, j, chip, _ in order:
            copy((3 + j) * n + k, chip, 1 - c, k, sibling).wait_recv()
        for cp in first + passed:
            cp.wait_send()

    return pl.pallas_call(body, in_specs=[ANY] * nb, out_specs=[ANY] * nb, out_shape=[SDS((4,) + s.shape, s.dtype) for s in shards],
                          scratch_shapes=_dma_sems(6 * n), name=name)(*shards)


def plan_gather_ici(shards):
    pieces = _pieces([(s.shape, s.dtype) for s in shards], lambda shape: shape[0] // 2)
    n = len(pieces)

    def copies(x_refs, out_refs, send_sems, recv_sems):
        x, y, c, chips = _place()
        mine = 2 * x + y

        def copy(j, k, chip, to, from_input):
            b, st, sz = pieces[k]
            rows_k = pl.ds(c * (x_refs[b].shape[0] // 2) + st, sz)
            dst = out_refs[b].at[chip, rows_k, :]
            return pltpu.make_async_remote_copy(src_ref=x_refs[b].at[rows_k, :] if from_input else dst, dst_ref=dst, send_sem=send_sems.at[j * n + k],
                                                recv_sem=recv_sems.at[j * n + k], device_id=to, device_id_type=MESH_ID)

        order = [(k, j, 2 * cx + cy, (cx, cy, c)) for k in range(n) for j, (cx, cy) in enumerate(chips)]
        return [copy(j, k, mine, to, True) for k, j, _, to in order], [copy(j, k, chip, to, False) for k, j, chip, to in order]

    return Plan(shards, [SDS((4,) + s.shape, s.dtype) for s in shards], {}, 3 * n, copies)


def plan_gather_pass(gathered):
    pieces = _pieces([(g.shape[1:], g.dtype) for g in gathered], lambda shape: shape[0] // 2)
    n = len(pieces)

    def copies(_, out_refs, send_sems, recv_sems):
        x, y, c, chips = _place()

        def copy(j, k, chip, cc):
            b, st, sz = pieces[k]
            rows_k = out_refs[b].at[chip, pl.ds(cc * (out_refs[b].shape[1] // 2) + st, sz), :]
            return pltpu.make_async_remote_copy(src_ref=rows_k, dst_ref=rows_k, send_sem=send_sems.at[j * n + k], recv_sem=recv_sems.at[j * n + k],
                                                device_id=(x, y, 1 - c), device_id_type=MESH_ID)

        order = [(k, j, 2 * cx + cy) for k in range(n) for j, (cx, cy) in enumerate(chips)]
        return [copy(j, k, chip, c) for k, j, chip in order], [copy(j, k, chip, 1 - c) for k, j, chip in order]

    return Plan(gathered, [SDS(g.shape, g.dtype) for g in gathered], {i: i for i in range(len(gathered))}, 3 * n, copies)


def plan_pair_exchange(gs):
    pieces = _pieces([(g.shape, g.dtype) for g in gs], lambda shape: shape[1] // 2)

    def copies(g_refs, got_refs, send_sems, recv_sems):
        x, y, c, _ = _place()
        swaps = [pltpu.make_async_remote_copy(src_ref=g_refs[b].at[:, pl.ds((1 - c) * (g_refs[b].shape[1] // 2) + st, sz), :],
                                              dst_ref=got_refs[b].at[:, pl.ds(st, sz), :], send_sem=send_sems.at[k], recv_sem=recv_sems.at[k],
                                              device_id=(x, y, 1 - c), device_id_type=MESH_ID)
                 for k, (b, st, sz) in enumerate(pieces)]
        return swaps, swaps

    return Plan(gs, [SDS((g.shape[0], g.shape[1] // 2, g.shape[2]), g.dtype) for g in gs], {}, len(pieces), copies)


def plan_chip_scatter(ps):
    pieces = _pieces([(p.shape, p.dtype) for p in ps], lambda shape: shape[1])
    n = len(pieces)

    def copies(p_refs, q_refs, send_sems, recv_sems):
        x, y, c, chips = _place()
        mine = 2 * x + y

        def copy(j, k, src_slot, dst_slot, to):
            b, st, sz = pieces[k]
            return pltpu.make_async_remote_copy(src_ref=p_refs[b].at[src_slot, pl.ds(st, sz), :], dst_ref=q_refs[b].at[dst_slot, pl.ds(st, sz), :],
                                                send_sem=send_sems.at[j * n + k], recv_sem=recv_sems.at[j * n + k], device_id=to,
                                                device_id_type=MESH_ID)

        order = [(k, j, 2 * cx + cy, (cx, cy, c)) for k in range(n) for j, (cx, cy) in enumerate(chips)]
        return [copy(j, k, chip, mine, to) for k, j, chip, to in order], [copy(j, k, mine, chip, to) for k, j, chip, to in order]

    return Plan(ps, [SDS(p.shape, p.dtype) for p in ps], {}, 3 * n, copies)


def plan_pair_join(bufs):
    pieces = _pieces([(b.shape, b.dtype) for b in bufs], lambda shape: shape[0] // 2)

    def copies(_, out_refs, send_sems, recv_sems):
        x, y, c, _ = _place()

        def copy(k, cc):
            b, st, sz = pieces[k]
            rows_k = out_refs[b].at[pl.ds(cc * (out_refs[b].shape[0] // 2) + st, sz), :]
            return pltpu.make_async_remote_copy(src_ref=rows_k, dst_ref=rows_k, send_sem=send_sems.at[k], recv_sem=recv_sems.at[k],
                                                device_id=(x, y, 1 - c), device_id_type=MESH_ID)

        return [copy(k, c) for k in range(len(pieces))], [copy(k, 1 - c) for k in range(len(pieces))]

    return Plan(bufs, [SDS(b.shape, b.dtype) for b in bufs], {i: i for i in range(len(bufs))}, len(pieces), copies)


def run_plan(name, plan):
    hosted_call(lambda: None, name=name, grid=(1,), in_specs=[], out_specs=[], out_shape=[], operands=[], plans=[plan])
    return plan.results


HBM = pl.BlockSpec(memory_space=pltpu.HBM)
SEMS = pl.BlockSpec(memory_space=pltpu.SEMAPHORE)
SPLIT_PARAMS = dict(has_side_effects=pltpu.SideEffectType.DATAFLOW_SIDE_EFFECTING)


def _plan_buffers(plan):
    in_place = {o: i for i, o in plan.aliases.items()}
    bufs = [pltpu.with_memory_space_constraint(a, pltpu.HBM) for a in plan.operands]
    where = []
    for o, sd in enumerate(plan.out_shape):
        if o in in_place:
            where.append(in_place[o])
        else:
            where.append(len(bufs))
            bufs.append(pltpu.with_memory_space_constraint(lax.empty(sd.shape, sd.dtype), pltpu.HBM))
    return bufs, where


def split_start(name, plans):
    layout = [_plan_buffers(p) for p in plans]
    counts = [len(b) for b, _ in layout]
    n_buf = sum(counts)

    def body(*refs):
        sems, token = refs[n_buf:n_buf + 2 * len(plans)], refs[-1]
        pos = 0
        for k, (p, (_, where)) in enumerate(zip(plans, layout)):
            mine = refs[pos:pos + counts[k]]
            pos += counts[k]
            sends, _ = p.copies(mine[:len(p.operands)], [mine[w] for w in where], sems[2 * k], sems[2 * k + 1])
            for cp in sends:
                cp.start()
        token[...] = jnp.zeros_like(token)

    bufs = [b for bs, _ in layout for b in bs]
    res = pl.pallas_call(
        body, name=name, in_specs=[HBM] * n_buf,
        out_specs=[SEMS] * (2 * len(plans)) + [HBM] * n_buf + [pl.BlockSpec(memory_space=pltpu.VMEM)],
        out_shape=[pltpu.SemaphoreType.DMA((p.n_sems,)) for p in plans for _ in range(2)] + [pltpu.HBM(b.shape, b.dtype) for b in bufs]
        + [SDS((8, 128), F32)],
        input_output_aliases={i: 2 * len(plans) + i for i in range(n_buf)}, compiler_params=pltpu.CompilerParams(**SPLIT_PARAMS))(*bufs)
    pos = 2 * len(plans)
    for k, p in enumerate(plans):
        p.in_flight = (res[2 * k], res[2 * k + 1], list(res[pos:pos + counts[k]]), layout[k][1])
        pos += counts[k]
    return res[-1]


def split_wait(name, plan, after):
    send_sems, recv_sems, bufs, where = plan.in_flight
    n_buf = len(bufs)

    def body(*refs):
        mine = refs[:n_buf]
        sends, recvs = plan.copies(mine[:len(plan.operands)], [mine[w] for w in where], refs[n_buf], refs[n_buf + 1])
        for cp in recvs:
            cp.wait_recv()
        for cp in sends:
            cp.wait_send()

    res = pl.pallas_call(body, name=name, in_specs=[HBM] * n_buf + [SEMS, SEMS, ANY], out_specs=[HBM] * n_buf,
                         out_shape=[pltpu.HBM(b.shape, b.dtype) for b in bufs], input_output_aliases={i: i for i in range(n_buf)},
                         compiler_params=pltpu.CompilerParams(**SPLIT_PARAMS))(*bufs, send_sems, recv_sems, after)
    plan.results = [res[w] for w in where]
    return plan.results


def pair_add(name, g, got, place):
    slots, rows, cols = g.shape
    half = rows // 2
    tr = _row_tile(half)
    nb = half // tr

    def kern(_, g_ref, t_ref, o_ref):
        o_ref[...] = (g_ref[...].astype(F32) + t_ref[...].astype(F32)).astype(o_ref.dtype)

    blk = pl.BlockSpec((None, tr, cols), lambda s, i, p: (s, i, 0))
    grid_spec = pltpu.PrefetchScalarGridSpec(
        num_scalar_prefetch=1, grid=(slots, nb),
        in_specs=[pl.BlockSpec((None, tr, cols), lambda s, i, p: (s, p[1] * nb + i, 0)), blk], out_specs=blk)
    return pl.pallas_call(kern, grid_spec=grid_spec, out_shape=SDS((slots, half, cols), g.dtype), name=name,
                          compiler_params=_cparams())(place, g, got)


def chip_add(name, p, q, place):
    slots, half, cols = p.shape
    tr = _row_tile(half)
    nb = half // tr

    def kern(_, p_ref, q1, q2, q3, o_ref):
        o_ref[...] = p_ref[...].astype(F32) + q1[...].astype(F32) + q2[...].astype(F32) + q3[...].astype(F32)

    def slot(k):
        return pl.BlockSpec((None, tr, cols), lambda i, pr: ((pr[0] + k) % slots, i, 0))

    grid_spec = pltpu.PrefetchScalarGridSpec(
        num_scalar_prefetch=1, grid=(nb,), in_specs=[slot(0), slot(1), slot(2), slot(3)],
        out_specs=pl.BlockSpec((tr, cols), lambda i, pr: (pr[1] * nb + i, 0)))
    return pl.pallas_call(kern, grid_spec=grid_spec, out_shape=SDS((2 * half, cols), F32), name=name,
                          compiler_params=_cparams())(place, p, q, q, q)


def pair_adds(tag, gs, gots, place):
    return [pair_add(f"{tag}_pair_add_{i}", g, got, place) for i, (g, got) in enumerate(zip(gs, gots))]


def chip_adds(tag, pairs, qs, place):
    return [chip_add(f"{tag}_chip_add_{i}", p, q, place) for i, (p, q) in enumerate(zip(pairs, qs))]


def reduce_scatter_chips(tag, gs, place):
    pairs = pair_adds(tag, gs, run_plan(tag + "_pair_exchange", plan_pair_exchange(gs)), place)
    return run_plan(tag + "_pair_join", plan_pair_join(chip_adds(tag, pairs, run_plan(tag + "_chip_scatter", plan_chip_scatter(pairs)), place)))


BIG = [("w_out", (2, 512, 1024)), ("w_mem_kv", (2, 256, 1024)), ("s5_w_in", (1, 1024, 1024)), ("s5_w_glu", (1, 1536, 768)),
       ("mla_w_in", (1, 1024, 848)), ("mla_w_uq", (1, 512, 576)), ("mla_w_ukv", (1, 256, 768))]
SHARDED_SMALL = [("mla_q_lora_norm", (1, 128)), ("mla_kv_lora_norm", (1, 64))]
SMALL = [("ln_gain", (2, 1024)), ("mem_norm", (2, 1024)), ("xq_norm", (2, 128)), ("xk_norm", (2, 128)),
         ("s5_lambda_re", (1, 96, 64)), ("s5_lambda_im", (1, 96, 64)), ("s5_log_step", (1, 96)),
         ("s5_b_re", (1, 96, 64, 16)), ("s5_b_im", (1, 96, 64, 16)), ("s5_c_re", (1, 96, 16, 64)), ("s5_c_im", (1, 96, 16, 64)),
         ("s5_d", (1, 1536)), ("mla_q_nope_norm", (1, 128)), ("mla_k_nope_norm", (1, 128)), ("mla_q_rope_norm", (1, 64)),
         ("mla_k_rope_norm", (1, 64))]
WEIGHT_ORDER = ["ln_gain", "w_out", "mem_norm", "w_mem_kv", "xq_norm", "xk_norm", "s5_w_in", "s5_lambda_re", "s5_lambda_im",
                "s5_log_step", "s5_b_re", "s5_b_im", "s5_c_re", "s5_c_im", "s5_d", "s5_w_glu", "mla_w_in", "mla_q_lora_norm",
                "mla_kv_lora_norm", "mla_w_uq", "mla_w_ukv", "mla_q_nope_norm", "mla_k_nope_norm", "mla_q_rope_norm", "mla_k_rope_norm"]
MINOR_LAST = {"mla_w_in": (0, 2, 1), "mla_w_uq": (0, 2, 1), "s5_b_re": (0, 2, 3, 1), "s5_b_im": (0, 2, 3, 1),
              "s5_c_re": (0, 2, 3, 1), "s5_c_im": (0, 2, 3, 1)}
SMALL_FULL = SMALL + [(n, (1, 4 * s[1])) for n, s in SHARDED_SMALL]
N_SMALL = sum(math.prod(s) for _, s in SMALL_FULL)
SMALL_ROWS, SMALL_LANES = 128, 1024

PAIR_OUT, PAIR_MKV, PAIR_ROWS = 0, 512, 768


def stack_shards(w, dtype):
    pairs = [jnp.concatenate([w["w_out"][l], w["w_mem_kv"][l]], axis=0).astype(dtype) for l in range(2)]
    return ([w["s5_w_in"][0].astype(dtype)], [pairs[0], w["s5_w_glu"][0].astype(dtype)],
            [pairs[1], w["mla_w_ukv"][0].astype(dtype), w["mla_w_in"][0].astype(dtype), w["mla_w_uq"][0].astype(dtype)])


def pair_views(pair):
    return {"w_out": Sharded(pair, "row", PAIR_OUT, 512), "w_mem_kv": Sharded(pair, "row", PAIR_MKV, 256)}


def grad_views():
    pair = SDS((4, PAIR_ROWS, 1024), BF16)
    return {"w_out": Sharded(pair, "row", PAIR_OUT, 512), "w_mem_kv": Sharded(pair, "row", PAIR_MKV, 256),
            "s5_w_in": Sharded(SDS((4, 1024, 1024), BF16), "col", 0, 1024), "s5_w_glu": Sharded(SDS((4, 1536, 768), BF16), "col", 0, 1536),
            "mla_w_ukv": Sharded(SDS((4, 256, 768), BF16), "col", 0, 256)}


def cols_to_shards(full):
    return full.reshape(full.shape[0], 4, full.shape[1] // 4).transpose(1, 0, 2)


def shards_to_cols(arr):
    return arr.transpose(1, 0, 2).reshape(arr.shape[1], 4 * arr.shape[2])


def mla_in_permute(w):
    o1, o2, o3, o4 = QL, QL + KVL, QL + KVL + ROPE, QL + KVL + ROPE + XQW
    return jnp.concatenate([w[:, o4:], w[:, :o1], w[:, o3:o4], w[:, o1:o2], w[:, o2:o3],
                            jnp.zeros((w.shape[0], MLA_IN_P - MLA_IN), w.dtype)], axis=1)


def mla_in_unpermute(d):
    return jnp.concatenate([d[:, 2048:2560], d[:, 3072:3328], d[:, 3328:3392], d[:, 2560:3072], d[:, :2048]], axis=1)


def uq_permute(w):
    w3 = w.reshape(w.shape[0], MH, NOPE + ROPE)
    return jnp.concatenate([w3[:, :, :NOPE].reshape(w.shape[0], MH * NOPE), w3[:, :, NOPE:].reshape(w.shape[0], MH * ROPE)], axis=1)


def uq_unpermute(d):
    dn = d[:, :MH * NOPE].reshape(d.shape[0], MH, NOPE)
    dr = d[:, MH * NOPE:].reshape(d.shape[0], MH, ROPE)
    return jnp.concatenate([dn, dr], axis=2).reshape(d.shape[0], MH * (NOPE + ROPE))


def time_permute(a):
    return a.reshape(SEG, SEG_LEN, a.shape[-1]).transpose(1, 0, 2).reshape(L, a.shape[-1])


def time_unpermute(a):
    return a.reshape(SEG_LEN, SEG, a.shape[-1]).transpose(1, 0, 2).reshape(L, a.shape[-1])


def mem_branch_fwd(tag, mem, mem_norm, w_mem_kv, xk_norm):
    mn = row_fwd(tag + "_mem_rms", fn_rms, ML, ML, [(mem, D, 0)], [mem_norm], [(D, BF16)])[0]
    kv = matmul(tag + "_mem_kv", mn, w_mem_kv, "nn", F32)
    kn = row_fwd(tag + "_mem_knorm", fn_mem_k, ML, ML, [(kv, XQW, 0)], [xk_norm], [(XQW, F32)])[0]
    return mn, kv, kn


def mem_branch_bwd(tag, mem, mem_norm, w_mem_kv, xk_norm, mn, kv, dkn, dv, g_view, g_wide):
    dk, dxk = row_bwd(tag + "_mem_knorm_bwd", fn_mem_k, ML, ML, [(kv, XQW, 0)], [xk_norm], [(dkn, XQW, 0)], [True], [True])
    dkv = jnp.concatenate([dk, dv], axis=1)
    dmn = matmul(tag + "_mem_kv_dx", dkv, w_mem_kv, "nt", F32)
    g_wide = matmul(tag + "_mem_kv_dw", mn, dkv, "tn", out=g_view, into=g_wide)
    dmem_norm = row_bwd(tag + "_mem_rms_bwd", fn_rms, ML, ML, [(mem, D, 0)], [mem_norm], [(dmn, D, 0)], [False], [True])[0]
    return g_wide, dmem_norm, dxk


def mem_attn_fwd(tag, proj, cb, kn, kv, xq_norm):
    return row_fwd(tag + "_mem_attn", fn_mem_attn, L, 256, [(proj, XQW, cb)], [kn, kv[:, XQW:], xq_norm], [(XQW, F32)])[0]


def mem_attn_bwd(tag, proj, cb, kn, kv, xq_norm, dmo, dproj):
    place = {"cols": proj.shape[1], "cb": cb, "into": dproj, "dtype": dproj.dtype}
    return row_bwd(tag + "_mem_attn_bwd", fn_mem_attn, L, 256, [(proj, XQW, cb)], [kn, kv[:, XQW:], xq_norm], [(dmo, XQW, 0)],
                   [place], [True, True, True])


def device_step(x, mem, positions, target, small, env, hooks=None):
    hooks = hooks or {}

    def plans_for(name):
        return hooks[("plans", name)](env) if ("plans", name) in hooks else ()

    def around(when, name, last=None):
        if (when, name) in hooks:
            hooks[(when, name)](env, last)

    g = {}
    gw = grad_views()
    ln, mem_norm, xq_norm, xk_norm = small["ln_gain"], small["mem_norm"], small["xq_norm"], small["xk_norm"]

    lre, lim = small["s5_lambda_re"][0], small["s5_lambda_im"][0]
    ls = small["s5_log_step"].reshape(SG, 1)
    one = pl.BlockSpec((SG, SP), lambda i: (0, 0))
    col = pl.BlockSpec((SG, 1), lambda i: (0, 0))
    disc_ins = [(lre, one), (lim, one), (ls, col)]
    a_re, a_im, coef_re, coef_im = stage("s5_disc", fn_s5_disc, (1,), disc_ins, [(SDS((SG, SP), F32), one)] * 4)
    b_re, b_im = small["s5_b_re"].reshape(SN, SC), small["s5_b_im"].reshape(SN, SC)
    c_re, c_im = small["s5_c_re"].reshape(PW, SP), small["s5_c_im"].reshape(PW, SP)
    bmat_rows = [(b_re, SC, 0), (b_im, SC, 0), (coef_re.reshape(SN, 1), 1, 0), (coef_im.reshape(SN, 1), 1, 0)]
    wb_re, wb_im = row_fwd("s5_bmat", fn_s5_bmat, SN, 512, bmat_rows, [], [(128, F32)] * 2)
    cmat_rows = [(c_re, SP, 0), (c_im, SP, 0)]
    wc_re, wc_im = row_fwd("s5_cmat", fn_s5_cmat, PW, 128, cmat_rows, [], [(512, F32)] * 2)
    a_re_v, a_im_v = a_re.reshape(1, SN), a_im.reshape(1, SN)
    s5_d = small["s5_d"]

    xp = time_permute(x)
    h0 = row_fwd("l0_rms", fn_rms, L, 256, [(xp, D, 0)], [ln[0:1]], [(D, BF16)])[0]
    around("before", "l0_in", wb_re)
    w_in0 = Sharded(env["in0"], "col", 0, 1024)
    proj0 = matmul("l0_in", h0, w_in0, "nn")
    s_re, s_im, g0 = s5_forward(proj0, wb_re, wb_im, wc_re, wc_im, a_re_v, a_im_v, s5_d, plans=plans_for("s5_forward"))
    around("before", "l0_glu", g0)
    w0 = dict(pair_views(env["pair0"]), s5_w_glu=Sharded(env["glu"], "col", 0, 1536))
    z0 = matmul("l0_glu", g0, w0["s5_w_glu"], "nn", plans=plans_for("l0_glu"))
    mn0, kv0, kn0 = mem_branch_fwd("l0", mem, mem_norm[0:1], w0["w_mem_kv"], xk_norm[0:1])
    mo0 = mem_attn_fwd("l0", proj0, 3, kn0, kv0, xq_norm[0:1])
    o0 = row_fwd("l0_merge", fn_merge_glu, L, 256, [(z0, 2 * PW, 0), (mo0, XQW, 0), (proj0, BW, 1)], [], [(BW, BF16)])[0]
    around("before", "l0_out", o0)
    x1p = matmul("l0_out", o0, w0["w_out"], "nn", F32, add=xp, plans=plans_for("l0_out"))
    around("after", "l0_out", x1p)
    x1 = time_unpermute(x1p)

    w1 = dict(pair_views(env["pair1"]), mla_w_ukv=Sharded(env["ukv"], "col", 0, 256))
    w_in1, w_uq = env["w_in1"], env["w_uq"]
    h1 = row_fwd("l1_rms", fn_rms, L, 256, [(x1, D, 0)], [ln[1:2]], [(D, BF16)])[0]
    proj1 = matmul("l1_in", h1, w_in1, "nn")
    qln, kvln = env["q_lora_norm"].reshape(1, QL), env["kv_lora_norm"].reshape(1, KVL)
    cqn = row_fwd("l1_q_lora_rms", fn_rms, L, 256, [(proj1, QL, 4)], [qln], [(QL, BF16)])[0]
    ckvn = row_fwd("l1_kv_lora_rms", fn_rms, L, 256, [(proj1, KVL, 12)], [kvln], [(KVL, BF16)])[0]
    q = matmul("l1_uq", cqn, w_uq, "nn")
    kv = matmul("l1_ukv", ckvn, w1["mla_w_ukv"], "nn")
    inv_freq = ROPE_THETA ** (-jnp.arange(ROPE // 2, dtype=F32) / (ROPE // 2))
    ang = positions.astype(F32)[:, None] * inv_freq
    cos2 = jnp.tile(jnp.cos(ang), (1, 4))
    sin_signed = jnp.tile(jnp.concatenate([-jnp.sin(ang), jnp.sin(ang)], axis=1), (1, 2))
    qnn, knn = small["mla_q_nope_norm"], small["mla_k_nope_norm"]
    qrn, krn = jnp.tile(small["mla_q_rope_norm"], (1, 2)), jnp.tile(small["mla_k_rope_norm"], (1, 2))
    tp = 256
    prep_ins = [(q, rspec(tp, MH * (NOPE + ROPE))), (kv, rspec(tp, MH * 256)), (proj1, rspec(tp, 128, 26)),
                (cos2, rspec(tp, 128)), (sin_signed, rspec(tp, 128))] + [(a, cspec((1, 128))) for a in (qnn, knn, qrn, krn)]
    hq_spec = pl.BlockSpec((MH, tp, 256), lambda i: (0, i, 0))
    hv_spec = pl.BlockSpec((MH, tp, 128), lambda i: (0, i, 0))
    qf, kf, vh = stage("l1_mla_prep", fn_mla_prep, (L // tp,), prep_ins,
                       [(SDS((MH, L, 256), BF16), hq_spec), (SDS((MH, L, 256), BF16), hq_spec), (SDS((MH, L, 128), BF16), hv_spec)])
    attn, attn_lse = causal_attn(qf, kf, vh)
    mn1, kv1, kn1 = mem_branch_fwd("l1", mem, mem_norm[1:2], w1["w_mem_kv"], xk_norm[1:2])
    mo1 = mem_attn_fwd("l1", proj1, 5, kn1, kv1, xq_norm[1:2])
    o1 = row_fwd("l1_merge", fn_merge, L, 256, [(attn, PW, 0), (mo1, XQW, 0), (proj1, BW, 0)], [], [(BW, BF16)])[0]
    x2 = matmul("l1_out", o1, w1["w_out"], "nn", F32, add=x1)
    dx2, loss = loss_and_grad(x2, target)

    do1 = matmul("l1_out_dx", dx2, w1["w_out"], "nt")
    g_pair1 = matmul("l1_out_dw", o1, dx2, "tn", out=gw["w_out"])
    dattn, dmo1, dproj1 = row_bwd("l1_merge_bwd", fn_merge, L, 256, [(attn, PW, 0), (mo1, XQW, 0), (proj1, BW, 0)], [],
                                  [(do1, BW, 0)], [True, True, {"cols": MLA_IN_P, "cb": 0, "dtype": BF16}], [])
    dproj1, dkn1, dv1, dxqn1 = mem_attn_bwd("l1", proj1, 5, kn1, kv1, xq_norm[1:2], dmo1, dproj1)
    env["g_pair1"], dmem_norm1, dxk1 = mem_branch_bwd("l1", mem, mem_norm[1:2], w1["w_mem_kv"], xk_norm[1:2], mn1, kv1, dkn1, dv1,
                                                      gw["w_mem_kv"], g_pair1)
    dqf, dkf, dvh = causal_attn_bwd(qf, kf, vh, attn, attn_lse, dattn)
    prep_diffs = [("row", SDS((L, MH * (NOPE + ROPE)), BF16), rspec(tp, MH * (NOPE + ROPE))), ("row", SDS((L, MH * 256), BF16), rspec(tp, MH * 256)),
                  ("row", SDS((L, MLA_IN_P), BF16), rspec(tp, 128, 26), {"into": dproj1}), None, None] + [("acc", (0,))] * 4
    dq, dkv, dproj1, dqnn, dknn, dqrn, dkrn = stage_bwd("l1_mla_prep_bwd", fn_mla_prep, (L // tp,), prep_ins,
                                                        [(dqf, hq_spec), (dkf, hq_spec), (dvh, hv_spec)], prep_diffs)
    dcqn = matmul("l1_uq_dx", dq, w_uq, "nt")
    env["dw_uq"] = matmul("l1_uq_dw", cqn, dq, "tn")
    dckvn = matmul("l1_ukv_dx", dkv, w1["mla_w_ukv"], "nt")
    env["g_ukv"] = matmul("l1_ukv_dw", ckvn, dkv, "tn", out=gw["mla_w_ukv"])
    dproj1, dqln = row_bwd("l1_q_lora_rms_bwd", fn_rms, L, 256, [(proj1, QL, 4)], [qln], [(dcqn, QL, 0)],
                           [{"cols": MLA_IN_P, "cb": 4, "into": dproj1, "dtype": BF16}], [True])
    dproj1, dkvln = row_bwd("l1_kv_lora_rms_bwd", fn_rms, L, 256, [(proj1, KVL, 12)], [kvln], [(dckvn, KVL, 0)],
                            [{"cols": MLA_IN_P, "cb": 12, "into": dproj1, "dtype": BF16}], [True])
    dh1 = matmul("l1_in_dx", dproj1, w_in1, "nt")
    env["dw_in1"] = matmul("l1_in_dw", h1, dproj1, "tn")
    dx1, dln1 = row_bwd("l1_rms_bwd", fn_rms, L, 256, [(x1, D, 0)], [ln[1:2]], [(dh1, D, 0)], [{"add": (dx2, D, 0)}], [True],
                        plans=plans_for("l1_rms_bwd"))
    around("after", "l1_rms_bwd", dx1)
    dx1p = time_permute(dx1)

    do0 = matmul("l0_out_dx", dx1p, w0["w_out"], "nt", plans=plans_for("l0_out_dx"))
    g_pair0 = matmul("l0_out_dw", o0, dx1p, "tn", out=gw["w_out"])
    dz0, dmo0, dproj0 = row_bwd("l0_merge_bwd", fn_merge_glu, L, 256, [(z0, 2 * PW, 0), (mo0, XQW, 0), (proj0, BW, 1)], [],
                                [(do0, BW, 0)], [{"dtype": BF16}, True, {"cols": 2 * BW, "cb": 1, "dtype": BF16}], [])
    dproj0, dkn0, dv0, dxqn0 = mem_attn_bwd("l0", proj0, 3, kn0, kv0, xq_norm[0:1], dmo0, dproj0)
    env["g_pair0"], dmem_norm0, dxk0 = mem_branch_bwd("l0", mem, mem_norm[0:1], w0["w_mem_kv"], xk_norm[0:1], mn0, kv0, dkn0, dv0,
                                                      gw["w_mem_kv"], g_pair0)
    env["g_glu"] = matmul("l0_glu_dw", g0, dz0, "tn", out=gw["s5_w_glu"], plans=plans_for("l0_glu_dw"))
    dg0 = matmul("l0_glu_dx", dz0, w0["s5_w_glu"], "nt", plans=plans_for("l0_glu_dx"))
    around("before", "s5_backward", dg0)
    dproj0, dd, dwc_re, dwc_im, dwb_re, dwb_im, da_re, da_im = s5_backward(dg0, proj0, s_re, s_im, wb_re, wb_im, wc_re, wc_im,
                                                                           a_re_v, a_im_v, s5_d, dproj0, plans=plans_for("s5_backward"))
    around("after", "s5_backward", dd)
    env["g_in0"] = matmul("l0_in_dw", h0, dproj0, "tn", out=gw["s5_w_in"], plans=plans_for("l0_in_dw"))
    dh0 = matmul("l0_in_dx", dproj0, w_in0, "nt", plans=plans_for("l0_in_dx"))
    around("after", "l0_in_dx", dh0)
    dxp, dln0 = row_bwd("l0_rms_bwd", fn_rms, L, 256, [(xp, D, 0)], [ln[0:1]], [(dh0, D, 0)], [{"add": (dx1p, D, 0)}], [True])
    grad_x = time_unpermute(dxp)

    db_re, db_im, dcoef_re, dcoef_im = row_bwd("s5_bmat_bwd", fn_s5_bmat, SN, 512, bmat_rows, [], [(dwb_re, 128, 0), (dwb_im, 128, 0)],
                                               [True] * 4, [], plans=plans_for("s5_bmat_bwd"))
    dc_re, dc_im = row_bwd("s5_cmat_bwd", fn_s5_cmat, PW, 128, cmat_rows, [], [(dwc_re, 512, 0), (dwc_im, 512, 0)], [True] * 2, [],
                           plans=plans_for("s5_cmat_bwd"))
    disc_cts = [(da_re.reshape(SG, SP), one), (da_im.reshape(SG, SP), one), (dcoef_re.reshape(SG, SP), one), (dcoef_im.reshape(SG, SP), one)]
    dlre, dlim, dls = stage_bwd("s5_disc_bwd", fn_s5_disc, (1,), disc_ins, disc_cts, [("acc", (0,))] * 3)

    g["ln_gain"] = jnp.concatenate([dln0, dln1], axis=0)
    g["mem_norm"] = jnp.concatenate([dmem_norm0, dmem_norm1], axis=0)
    g["xq_norm"] = jnp.concatenate([dxqn0, dxqn1], axis=0)
    g["xk_norm"] = jnp.concatenate([dxk0, dxk1], axis=0)
    g["s5_lambda_re"], g["s5_lambda_im"], g["s5_log_step"] = dlre, dlim, dls
    g["s5_b_re"], g["s5_b_im"], g["s5_c_re"], g["s5_c_im"] = db_re, db_im, dc_re, dc_im
    g["s5_d"] = dd
    g["mla_q_lora_norm"], g["mla_kv_lora_norm"] = dqln, dkvln
    g["mla_q_nope_norm"], g["mla_k_nope_norm"] = dqnn, dknn
    g["mla_q_rope_norm"] = dqrn[:, :ROPE] + dqrn[:, ROPE:]
    g["mla_k_rope_norm"] = dkrn[:, :ROPE] + dkrn[:, ROPE:]
    return loss, grad_x, g


def kernel(x, mem, positions, ln_gain, w_out, mem_norm, w_mem_kv, xq_norm, xk_norm, s5_w_in, s5_lambda_re, s5_lambda_im, s5_log_step, s5_b_re, s5_b_im, s5_c_re, s5_c_im, s5_d, s5_w_glu, mla_w_in, mla_q_lora_norm, mla_kv_lora_norm, mla_w_uq, mla_w_ukv, mla_q_nope_norm, mla_k_nope_norm, mla_q_rope_norm, mla_k_rope_norm, loss_target, m_ln_gain, m_w_out, m_mem_norm, m_w_mem_kv, m_xq_norm, m_xk_norm, m_s5_w_in, m_s5_lambda_re, m_s5_lambda_im, m_s5_log_step, m_s5_b_re, m_s5_b_im, m_s5_c_re, m_s5_c_im, m_s5_d, m_s5_w_glu, m_mla_w_in, m_mla_q_lora_norm, m_mla_kv_lora_norm, m_mla_w_uq, m_mla_w_ukv, m_mla_q_nope_norm, m_mla_k_nope_norm, m_mla_q_rope_norm, m_mla_k_rope_norm, v_ln_gain, v_w_out, v_mem_norm, v_w_mem_kv, v_xq_norm, v_xk_norm, v_s5_w_in, v_s5_lambda_re, v_s5_lambda_im, v_s5_log_step, v_s5_b_re, v_s5_b_im, v_s5_c_re, v_s5_c_im, v_s5_d, v_s5_w_glu, v_mla_w_in, v_mla_q_lora_norm, v_mla_kv_lora_norm, v_mla_w_uq, v_mla_w_ukv, v_mla_q_nope_norm, v_mla_k_nope_norm, v_mla_q_rope_norm, v_mla_k_rope_norm):
    args = dict(locals())
    wts = {n: args[n] for n in WEIGHT_ORDER}
    mom = {n: args["m_" + n] for n in WEIGHT_ORDER}
    var = {n: args["v_" + n] for n in WEIGHT_ORDER}

    chip = 2 * lax.axis_index("x") + lax.axis_index("y")
    place = jnp.stack([chip, lax.axis_index("c")]).astype(jnp.int32)

    def own_slot(gathered, shards):
        return [lax.dynamic_update_slice(g, s[None], (chip, 0, 0)) for g, s in zip(gathered, shards)]

    groups = list(stack_shards(wts, BF16))
    groups[2].append(jnp.concatenate([mla_q_lora_norm, jnp.pad(mla_kv_lora_norm, ((0, 0), (0, 64))), jnp.zeros((14, 128), F32)], axis=0))
    over_ici = [plan_gather_ici(shards) for shards in groups]
    _SCHEDULE_BEHIND.clear()
    schedule_behind(split_start("gather_start", over_ici))
    env, hooks, passed_on = {}, {}, {}

    def arrived(k, after, pass_now):
        passing = plan_gather_pass(split_wait(f"gather_wait_{k}", over_ici[k], after))
        passed_on[k] = passing
        return own_slot(run_plan(f"gather_pass_{k}", passing), groups[k]) if pass_now else None

    def need_in0(env, last):
        env["in0"], = arrived(0, last, True)

    def need_layer0(env, last):
        env["pair0"], env["glu"] = arrived(1, last, True)

    def need_layer1(env, last):
        arrived(2, last, False)

    def layer1_weights(env, last):
        pair1, ukv, in1, uq, norms = own_slot(passed_on[2].results, groups[2])
        env.update(pair1=pair1, ukv=ukv, w_in1=mla_in_permute(shards_to_cols(in1)), w_uq=uq_permute(shards_to_cols(uq)),
                   q_lora_norm=norms[:, 0, :], kv_lora_norm=norms[:, 1, :64])

    hooks["before", "l0_in"], hooks["before", "l0_glu"], hooks["before", "l0_out"] = need_in0, need_layer0, need_layer1
    hooks["plans", "l0_out"], hooks["after", "l0_out"] = (lambda env: [passed_on[2]]), layer1_weights

    rs = {}

    def swap(k, gs):
        rs[k, "g"], rs[k, "swap"] = gs, plan_pair_exchange(gs)
        return rs[k, "swap"]

    def start_scatter(k):
        rs[k, "pairs"] = pair_adds(f"rs{k}", rs[k, "g"], rs[k, "swap"].results, place)
        rs[k, "scatter"] = plan_chip_scatter(rs[k, "pairs"])
        schedule_behind(split_start(f"rs{k}_scatter_start", [rs[k, "scatter"]]))

    def join(k, after):
        rs[k, "join"] = plan_pair_join(chip_adds(f"rs{k}", rs[k, "pairs"], split_wait(f"rs{k}_scatter_wait", rs[k, "scatter"], after), place))
        return rs[k, "join"]

    hooks["plans", "l1_rms_bwd"] = lambda env: [swap(1, [env["g_pair1"], env["g_ukv"], cols_to_shards(mla_in_unpermute(env["dw_in1"])).astype(BF16),
                                                          cols_to_shards(uq_unpermute(env["dw_uq"])).astype(BF16)])]
    hooks["after", "l1_rms_bwd"] = lambda env, last: start_scatter(1)
    hooks["plans", "l0_glu_dx"] = lambda env: [swap(0, [env["g_pair0"], env["g_glu"]])]

    def before_s5_backward(env, last):
        start_scatter(0)
        env["join1"] = join(1, last)

    hooks["before", "s5_backward"] = before_s5_backward
    hooks["plans", "s5_backward"] = lambda env: [env["join1"]]
    hooks["after", "s5_backward"] = lambda env, last: env.update(join0=join(0, last))
    hooks["plans", "l0_in_dx"] = lambda env: [swap(2, [env["g_in0"]]), env["join0"]]
    hooks["after", "l0_in_dx"] = lambda env, last: start_scatter(2)

    small = {n: wts[n] for n, _ in SMALL}
    loss, grad_x, g = device_step(x[0], mem[0], positions[0], loss_target[0], small, env, hooks)
    loss = lax.psum(loss[0, 0], MESH_AXES)
    (r_pair1, r_ukv, r_in1, r_uq), (r_pair0, r_glu) = (rs[k, "join"].results for k in (1, 0))
    grads = {"w_out": jnp.stack([r_pair0[:PAIR_MKV], r_pair1[:PAIR_MKV]]), "w_mem_kv": jnp.stack([r_pair0[PAIR_MKV:], r_pair1[PAIR_MKV:]]),
             "s5_w_glu": r_glu[None], "mla_w_ukv": r_ukv[None], "mla_w_in": r_in1[None], "mla_w_uq": r_uq[None]}
    delta, new_m, new_v = {}, {}, {}
    shapes = dict(BIG + SMALL)

    def update(n):
        s = shapes[n]
        perm = MINOR_LAST.get(n, tuple(range(len(s))))
        turned = tuple(s[p] for p in perm)
        view = lambda a: jnp.transpose(a, perm).reshape(-1, turned[-1])
        res = adamw("adamw_" + n, view(wts[n]), view(grads[n]), view(mom[n]), view(var[n]))
        delta[n], new_m[n], new_v[n] = (jnp.transpose(r.reshape(turned), tuple(perm.index(i) for i in range(len(s)))) for r in res)
        return delta[n]

    def beside(name, plan, work):
        schedule_behind(split_start(name + "_start", [plan]))
        return split_wait(name + "_wait", plan, [update(n) for n in work][-1])

    small_flat = jnp.concatenate([g[n].reshape(-1) for n, _ in SMALL_FULL])
    g_small = jnp.pad(small_flat, (0, 4 * SMALL_ROWS * SMALL_LANES - N_SMALL)).astype(BF16).reshape(4, SMALL_ROWS, SMALL_LANES)
    pairs = pair_adds("rs3", [g_small], beside("rs3_swap", plan_pair_exchange([g_small]), ["s5_w_glu"]), place)
    halves = chip_adds("rs3", pairs, beside("rs3_scatter", plan_chip_scatter(pairs), ["w_out", "mla_w_in"]), place)
    halves += chip_adds("rs2", rs[2, "pairs"], split_wait("rs2_scatter_wait", rs[2, "scatter"], delta["mla_w_in"]), place)
    r_small, r_in0 = beside("rs23_join", plan_pair_join(halves), ["w_mem_kv", "mla_w_uq"])
    grads["s5_w_in"] = r_in0[None]
    landed = beside("small_gather", plan_gather_ici([r_small]), ["mla_w_ukv", "s5_w_in"])
    small_all = own_slot(run_plan("small_gather_pass", plan_gather_pass(landed)), [r_small])[0].reshape(-1)[:N_SMALL]

    off = 0
    for n, s in SMALL_FULL:
        grads[n] = small_all[off:off + math.prod(s)].reshape(s)
        off += math.prod(s)
    for n, s in SHARDED_SMALL:
        grads[n] = lax.dynamic_slice(grads[n], (0, chip * s[1]), s)
    for n, s in SMALL:
        if len(s) == 4:
            update(n)
    small_names = [n for n, s in SMALL if len(s) < 4] + [n for n, _ in SHARDED_SMALL]
    n_own = sum(wts[n].size for n in small_names)
    rows_own = -(-n_own // (8 * 128)) * 8

    def pack_small(d):
        flat = jnp.concatenate([d[n].reshape(-1) for n in small_names])
        return jnp.pad(flat, (0, rows_own * 128 - n_own), constant_values=1.0).reshape(rows_own, 128)

    res = adamw("adamw_small", pack_small(wts), pack_small(grads), pack_small(mom), pack_small(var))
    off = 0
    for n in small_names:
        size = wts[n].size
        delta[n], new_m[n], new_v[n] = (r.reshape(-1)[off:off + size].reshape(wts[n].shape) for r in res)
        off += size

    return (loss, grad_x[None], *[grads[n] for n in WEIGHT_ORDER], *[delta[n] for n in WEIGHT_ORDER],
            *[new_m[n] for n in WEIGHT_ORDER], *[new_v[n] for n in WEIGHT_ORDER])
```

```python
import functools
import math

import jax
import jax.numpy as jnp
from jax import lax
from jax.experimental import pallas as pl
from jax.experimental.pallas import tpu as pltpu

F32, BF16 = jnp.float32, jnp.bfloat16
SDS = jax.ShapeDtypeStruct

D = 1024
L = 2048
ML = 256
BW = 2 * D
XQW = BW // 4
PW = BW - XQW
XH, XHD = 4, 128
SG, SC, SP = 96, 16, 64
SN = SG * SP
NOPE, ROPE, VD = 128, 64, 128
MH = 12
QL, KVL = 512, 256
EPS = 1e-6
ROPE_THETA = 10000.0
MLA_IN = QL + KVL + ROPE + XQW + BW
MLA_IN_P = 3456
ADAM_LR, ADAM_B1, ADAM_B2, ADAM_EPS, ADAM_WD, ADAM_STEP = 0.001, 0.9, 0.999, 1e-08, 0.01, 10

VMEM_LIMIT = 48 * 2**20
SEG = 8
SEG_LEN = L // SEG
MESH_AXES = ("x", "y", "c")


def _cparams():
    return pltpu.CompilerParams(vmem_limit_bytes=VMEM_LIMIT)


def _dg(a, b, ca, cb):
    return lax.dot_general(a.astype(BF16), b.astype(BF16), (((ca,), (cb,)), ((), ())), preferred_element_type=F32)


@jax.custom_vjp
def mm_nn(a, b):
    return _dg(a, b, 1, 0)


mm_nn.defvjp(lambda a, b: (_dg(a, b, 1, 0), (a, b)), lambda res, g: (_dg(g, res[1], 1, 1), _dg(res[0], g, 0, 0)))


@jax.custom_vjp
def mm_nt(a, b):
    return _dg(a, b, 1, 1)


mm_nt.defvjp(lambda a, b: (_dg(a, b, 1, 1), (a, b)), lambda res, g: (_dg(g, res[1], 1, 0), _dg(g, res[0], 0, 0)))


@functools.partial(jax.custom_vjp, nondiff_argnums=(1,))
def lane_roll(x, shift):
    return pltpu.roll(x, shift, 1)


lane_roll.defvjp(lambda x, shift: (pltpu.roll(x, shift, 1), None),
                 lambda shift, _, g: (pltpu.roll(g, (128 - shift) % 128, 1),))


def rms(x, g):
    return x * lax.rsqrt(jnp.mean(x * x, axis=-1, keepdims=True) + EPS) * g


@jax.custom_vjp
def softmax_rows(s):
    e = jnp.exp(s - jnp.max(s, axis=-1, keepdims=True))
    return e / jnp.sum(e, axis=-1, keepdims=True)


def _softmax_rows_fwd(s):
    p = softmax_rows(s)
    return p, p


def _softmax_rows_bwd(p, g):
    return (p * (g - jnp.sum(g * p, axis=-1, keepdims=True)),)


softmax_rows.defvjp(_softmax_rows_fwd, _softmax_rows_bwd)


def silu(x):
    return x * jax.nn.sigmoid(x)


ANY = pl.BlockSpec(memory_space=pl.ANY)
MESH_ID = pl.DeviceIdType.MESH


def _dma_sems(n):
    return [pltpu.SemaphoreType.DMA((n,)), pltpu.SemaphoreType.DMA((n,))]


class Plan:
    def __init__(self, operands, out_shape, aliases, n_sems, copies):
        self.operands, self.out_shape, self.aliases, self.n_sems, self.copies = list(operands), list(out_shape), aliases, n_sems, copies
        self.results = None


_SCHEDULE_BEHIND = []


def schedule_behind(token):
    _SCHEDULE_BEHIND.append(token)


def hosted_call(kern, *, name, grid, in_specs, out_specs, out_shape, operands, scratch_shapes=(), aliases=None, cparams=None, plans=(), deps=()):
    n_in, n_out, n_scr = len(in_specs), len(out_specs), len(scratch_shapes)
    p_in, p_out = [len(p.operands) for p in plans], [len(p.out_shape) for p in plans]
    deps = tuple(deps) + tuple(_SCHEDULE_BEHIND)
    _SCHEDULE_BEHIND.clear()
    all_aliases = dict(aliases or {})
    in_off, out_off = n_in, n_out
    for p, ni, no in zip(plans, p_in, p_out):
        all_aliases.update({in_off + i: out_off + o for i, o in p.aliases.items()})
        in_off, out_off = in_off + ni, out_off + no

    def body(*refs):
        pos, pins, pouts = n_in, [], []
        for ni in p_in:
            pins.append(refs[pos:pos + ni])
            pos += ni
        pos += len(deps)
        main_out = refs[pos:pos + n_out]
        pos += n_out
        for no in p_out:
            pouts.append(refs[pos:pos + no])
            pos += no
        main_scr = refs[pos:pos + n_scr]
        pos += n_scr
        if plans:
            ids = [pl.program_id(ax) for ax in range(len(grid))]
            first = functools.reduce(jnp.logical_and, [i == 0 for i in ids])
            last = functools.reduce(jnp.logical_and, [i == g - 1 for i, g in zip(ids, grid)])
            copies = [p.copies(pins[k], pouts[k], refs[pos + 2 * k], refs[pos + 2 * k + 1]) for k, p in enumerate(plans)]

            @pl.when(first)
            def _():
                for sends, _ in copies:
                    for cp in sends:
                        cp.start()

        kern(*refs[:n_in], *main_out, *main_scr)
        if plans:
            @pl.when(last)
            def _():
                for sends, recvs in copies:
                    for cp in recvs:
                        cp.wait_recv()
                    for cp in sends:
                        cp.wait_send()

    res = pl.pallas_call(body, grid=grid, in_specs=list(in_specs) + [ANY] * (sum(p_in) + len(deps)),
                         out_specs=list(out_specs) + [ANY] * sum(p_out), out_shape=list(out_shape) + [s for p in plans for s in p.out_shape],
                         scratch_shapes=list(scratch_shapes) + [s for p in plans for s in _dma_sems(p.n_sems)],
                         input_output_aliases=all_aliases, name=name, compiler_params=cparams or _cparams())(
        *operands, *[a for p in plans for a in p.operands], *deps)
    pos = n_out
    for p, no in zip(plans, p_out):
        p.results = list(res[pos:pos + no])
        pos += no
    return list(res[:n_out])


def _wide(v):
    return v.astype(F32) if v.dtype == BF16 else v


def stage(name, fn, grid, ins, outs):
    n_in = len(ins)

    def kern(*refs):
        res = fn(*[_wide(r[...]) for r in refs[:n_in]])
        for r, v in zip(refs[n_in:], res):
            r[...] = v.astype(r.dtype)

    return hosted_call(kern, name=name, grid=grid, in_specs=[s for _, s in ins], out_specs=[s for _, s in outs],
                       out_shape=[sd for sd, _ in outs], operands=[a for a, _ in ins])


def stage_bwd(name, fn, grid, ins, cts, diffs, plans=()):
    n_in, n_ct = len(ins), len(cts)
    didx = [i for i, d in enumerate(diffs) if d is not None]
    opts = {i: (diffs[i][3] if len(diffs[i]) > 3 else {}) for i in didx if diffs[i][0] == "row"}
    adds = [(i, opts[i]["add"]) for i in opts if "add" in opts[i]]
    intos = [(i, opts[i]["into"]) for i in opts if "into" in opts[i]]
    n_add, n_into = len(adds), len(intos)
    add_pos = {i: n_in + n_ct + k for k, (i, _) in enumerate(adds)}
    n_extra = n_in + n_ct + n_add + n_into

    def kern(*refs):
        vals = [_wide(r[...]) for r in refs[:n_in]]

        def f(*dv):
            full = list(vals)
            for i, v in zip(didx, dv):
                full[i] = v
            return fn(*full)

        _, vjp = jax.vjp(f, *[vals[i].astype(F32) for i in didx])
        gs = vjp(tuple(c[...].astype(F32) for c in refs[n_in:n_in + n_ct]))
        for o_ref, i, g in zip(refs[n_extra:], didx, gs):
            if diffs[i][0] == "row":
                if i in add_pos:
                    g = g + refs[add_pos[i]][...].astype(F32)
                o_ref[...] = g.astype(o_ref.dtype)
            else:
                first = functools.reduce(jnp.logical_and, [pl.program_id(ax) == 0 for ax in diffs[i][1]])

                @pl.when(first)
                def _():
                    o_ref[...] = g

                @pl.when(jnp.logical_not(first))
                def _():
                    o_ref[...] += g

    out_shape, out_specs = [], []
    for i in didx:
        if diffs[i][0] == "row":
            out_shape.append(diffs[i][1])
            out_specs.append(diffs[i][2])
        else:
            out_shape.append(SDS(ins[i][0].shape, F32))
            out_specs.append(ins[i][1])
    aliases = {n_in + n_ct + n_add + k: didx.index(i) for k, (i, _) in enumerate(intos)}
    in_specs = [s for _, s in ins] + [s for _, s in cts] + [s for _, (_, s) in adds] + [ANY] * n_into
    operands = [a for a, _ in ins] + [a for a, _ in cts] + [a for _, (a, _) in adds] + [a for _, a in intos]
    return hosted_call(kern, name=name, grid=grid, in_specs=in_specs, out_specs=out_specs, out_shape=out_shape, operands=operands,
                       aliases=aliases, plans=plans)


def rspec(tl, w, cb=0):
    return pl.BlockSpec((tl, w), lambda i: (i, cb))


def cspec(shape):
    return pl.BlockSpec(shape, lambda i: (0,) * len(shape))


def row_fwd(name, fn, rows, tl, row_ins, consts, outs):
    ins = [(a, rspec(tl, w, cb)) for a, w, cb in row_ins] + [(a, cspec(a.shape)) for a in consts]
    return stage(name, fn, (rows // tl,), ins, [(SDS((rows, w), dt), rspec(tl, w)) for w, dt in outs])


def row_bwd(name, fn, rows, tl, row_ins, consts, cts, row_diff, const_diff, plans=()):
    ins = [(a, rspec(tl, w, cb)) for a, w, cb in row_ins] + [(a, cspec(a.shape)) for a in consts]
    diffs = []
    for (a, w, cb), d in zip(row_ins, row_diff):
        if not d:
            diffs.append(None)
            continue
        d = d if isinstance(d, dict) else {}
        opts = {}
        if "add" in d:
            opts["add"] = (d["add"][0], rspec(tl, d["add"][1], d["add"][2]))
        if d.get("into") is not None:
            opts["into"] = d["into"]
        diffs.append(("row", SDS((rows, d.get("cols", w)), d.get("dtype", F32)), rspec(tl, w, d.get("cb", 0)), opts))
    diffs += [("acc", (0,)) if d else None for d in const_diff]
    return stage_bwd(name, fn, (rows // tl,), ins, [(a, rspec(tl, w, cb)) for a, w, cb in cts], diffs, plans=plans)


MATMUL_VMEM = 36 * 2**20
ADAMW_VMEM = 28 * 2**20


class Sharded:
    def __init__(self, arr, kind, roff, rows):
        self.arr, self.kind, self.roff, self.rows, self.n = arr, kind, roff, rows, arr.shape[2]
        self.shape = (rows, 4 * self.n) if kind == "col" else (4 * rows, self.n)

    def fits(self, t0, t1):
        return self.roff % t0 == 0 and self.rows % t0 == 0 and self.n % t1 == 0

    def spec(self, t0, t1, bidx):
        assert self.fits(t0, t1), (self.kind, self.roff, self.rows, self.n, t0, t1)
        r0 = self.roff // t0
        if self.kind == "col":
            per = self.n // t1
            return pl.BlockSpec((None, t0, t1), lambda *g: (bidx(*g)[1] // per, r0 + bidx(*g)[0], bidx(*g)[1] % per))
        per = self.rows // t0
        return pl.BlockSpec((None, t0, t1), lambda *g: (bidx(*g)[0] // per, r0 + bidx(*g)[0] % per, bidx(*g)[1]))


def matmul(name, a, b, mode, out_dtype=BF16, add=None, out=None, into=None, plans=()):
    if mode == "tn":
        k_dim, m = a.shape
    else:
        m, k_dim = a.shape
    n = b.shape[0] if mode == "nt" else b.shape[1]
    b_fit = b.fits if isinstance(b, Sharded) else (lambda t0, t1: True)
    o_fit = out.fits if out is not None else (lambda t0, t1: True)
    a_bytes, b_bytes = jnp.dtype(a.dtype).itemsize, jnp.dtype(b.arr.dtype if isinstance(b, Sharded) else b.dtype).itemsize
    o_bytes = jnp.dtype(out_dtype if out is None else out.arr.dtype).itemsize

    def vmem(tm, tn, tk):
        return 2 * (tm * tk * a_bytes + tk * tn * b_bytes + tm * tn * (o_bytes + (4 if add is not None else 0))) + 4 * tm * tn * (1 + (tk < k_dim))

    tiles = [(tm, tn, tk) for tm in (2048, 1024, 512, 256, 128) for tn in (1024, 768, 512, 384, 256, 128)
             for tk in sorted({k_dim, 1024, 768, 512, 384, 256, 128})
             if m % tm == 0 and n % tn == 0 and k_dim % tk == 0 and (b_fit(tn, tk) if mode == "nt" else b_fit(tk, tn)) and o_fit(tm, tn)
             and vmem(tm, tn, tk) <= MATMUL_VMEM]
    tm, tn, tk = max(tiles, key=lambda t: (t[2] == k_dim, t[0] * t[1] * t[2], t[0] * t[1]))
    nk = k_dim // tk
    a_spec = pl.BlockSpec((tk, tm), lambda i, j, k: (k, i)) if mode == "tn" else pl.BlockSpec((tm, tk), lambda i, j, k: (i, k))
    if isinstance(b, Sharded):
        b_spec = b.spec(tn, tk, lambda i, j, k: (j, k)) if mode == "nt" else b.spec(tk, tn, lambda i, j, k: (k, j))
        b = b.arr
    else:
        b_spec = pl.BlockSpec((tn, tk), lambda i, j, k: (j, k)) if mode == "nt" else pl.BlockSpec((tk, tn), lambda i, j, k: (k, j))
    o_spec = pl.BlockSpec((tm, tn), lambda i, j, k: (i, j))
    out_spec, out_shape = (o_spec, SDS((m, n), out_dtype)) if out is None else (out.spec(tm, tn, lambda i, j, k: (i, j)), out.arr)
    ca, cb = {"nn": (1, 0), "nt": (1, 1), "tn": (0, 0)}[mode]
    n_in = 2 + (add is not None)

    def finish(refs, o_ref, r):
        if add is not None:
            r = r + refs[2][...]
        o_ref[...] = r.astype(o_ref.dtype)

    def kern_whole(*refs):
        finish(refs, refs[-1], _dg(refs[0][...], refs[1][...], ca, cb))

    def kern_cut(*refs):
        o_ref, acc = refs[-2], refs[-1]
        k = pl.program_id(2)

        @pl.when(k == 0)
        def _():
            acc[...] = jnp.zeros_like(acc)

        acc[...] += _dg(refs[0][...], refs[1][...], ca, cb)

        @pl.when(k == nk - 1)
        def _():
            finish(refs, o_ref, acc[...])

    ins, specs = [a, b], [a_spec, b_spec]
    if add is not None:
        ins.append(add)
        specs.append(o_spec)
    if into is not None:
        ins.append(into)
        specs.append(ANY)
    return hosted_call(kern_whole if nk == 1 else kern_cut, name=name, grid=(m // tm, n // tn, nk), in_specs=specs, out_specs=[out_spec],
                       out_shape=[out_shape], operands=ins, scratch_shapes=[] if nk == 1 else [pltpu.VMEM((tm, tn), F32)],
                       aliases={} if into is None else {n_in: 0}, plans=plans)[0]


def _cmul(ar, ai, br, bi):
    return ar * br - ai * bi, ar * bi + ai * br


def _sub_shift(x, down):
    row = lax.broadcasted_iota(jnp.int32, x.shape, 0)
    if down:
        return jnp.where(row == 0, 0.0, pltpu.roll(x, 1, 0))
    return jnp.where(row == SEG - 1, 0.0, pltpu.roll(x, SEG - 1, 0))


def _pow_seg_len(ar, ai):
    for _ in range(int(math.log2(SEG_LEN))):
        ar, ai = _cmul(ar, ai, ar, ai)
    return ar, ai


def _scan_in_place(sr, si, a_re, a_im):
    lanes = sr.shape[1]
    ar = jnp.broadcast_to(a_re, (SEG, lanes))
    ai = jnp.broadcast_to(a_im, (SEG, lanes))
    zero = jnp.zeros((SEG, lanes), F32)

    def local(i, carry):
        rows = pl.ds(pl.multiple_of(i * SEG, SEG), SEG)
        mr, mi = _cmul(ar, ai, carry[0], carry[1])
        nr, ni = mr + sr[rows, :], mi + si[rows, :]
        sr[rows, :] = nr
        si[rows, :] = ni
        return nr, ni

    fr, fi = lax.fori_loop(0, SEG_LEN, local, (zero, zero))
    pr, pi = _pow_seg_len(ar, ai)
    ir, ii = zero, zero
    for _ in range(SEG - 1):
        mr, mi = _cmul(pr, pi, ir, ii)
        ir, ii = _sub_shift(mr + fr, True), _sub_shift(mi + fi, True)

    def carry_in(i, pw):
        rows = pl.ds(pl.multiple_of(i * SEG, SEG), SEG)
        cr, ci = _cmul(pw[0], pw[1], ir, ii)
        sr[rows, :] += cr
        si[rows, :] += ci
        return _cmul(pw[0], pw[1], ar, ai)

    lax.fori_loop(0, SEG_LEN, carry_in, (ar, ai))


S5_LANES = 8 * SP
S5_BLOCKS = SN // S5_LANES


def _s5_specs():
    u_spec = pl.BlockSpec((L, 8 * SC), lambda j: (0, j))
    s_spec = pl.BlockSpec((L, S5_LANES), lambda j: (0, j))
    wb_spec = pl.BlockSpec((S5_LANES, 8 * SC), lambda j: (j, 0))
    wc_spec = pl.BlockSpec((8 * SC, S5_LANES), lambda j: (j, 0))
    a_spec = pl.BlockSpec((1, S5_LANES), lambda j: (0, j))
    d_spec = pl.BlockSpec((1, 8 * SC), lambda j: (0, j))
    return u_spec, s_spec, wb_spec, wc_spec, a_spec, d_spec


def s5_forward(proj, wb_re, wb_im, wc_re, wc_im, a_re, a_im, d, plans=()):
    def kern(u_ref, wbr, wbi, wcr, wci, ar, ai, d_ref, sr_out, si_out, g_ref, sr, si):
        u = _wide(u_ref[...])
        sr[...], si[...] = fn_s5_bu(u, wbr[...], wbi[...])
        _scan_in_place(sr, si, ar[...], ai[...])
        g_ref[...] = fn_s5_out(sr[...], si[...], u, d_ref[...], wcr[...], wci[...])[0].astype(g_ref.dtype)
        sr_out[...] = sr[...].astype(sr_out.dtype)
        si_out[...] = si[...].astype(si_out.dtype)

    u_spec, s_spec, wb_spec, wc_spec, a_spec, d_spec = _s5_specs()
    return hosted_call(kern, name="s5_forward", grid=(S5_BLOCKS,), in_specs=[u_spec, wb_spec, wb_spec, wc_spec, wc_spec, a_spec, a_spec, d_spec],
                       out_specs=[s_spec, s_spec, u_spec], out_shape=[SDS((L, SN), BF16)] * 2 + [SDS((L, PW), BF16)],
                       operands=[proj, wb_re, wb_im, wc_re, wc_im, a_re, a_im, d], scratch_shapes=[pltpu.VMEM((L, S5_LANES), F32)] * 2,
                       plans=plans)


def _adjoint_scan_in_place(lr, li, sr, si, a_re, a_im):
    lanes = lr.shape[1]
    ar = jnp.broadcast_to(a_re, (SEG, lanes))
    ai = -jnp.broadcast_to(a_im, (SEG, lanes))
    zero = jnp.zeros((SEG, lanes), F32)

    def local(k, carry):
        i = SEG_LEN - 1 - k
        rows = pl.ds(pl.multiple_of(i * SEG, SEG), SEG)
        mr, mi = _cmul(ar, ai, carry[0], carry[1])
        nr, ni = mr + lr[rows, :], mi + li[rows, :]
        lr[rows, :] = nr
        li[rows, :] = ni
        return nr, ni

    fr, fi = lax.fori_loop(0, SEG_LEN, local, (zero, zero))
    pr, pi = _pow_seg_len(ar, ai)
    ir, ii = zero, zero
    for _ in range(SEG - 1):
        mr, mi = _cmul(pr, pi, ir, ii)
        ir, ii = _sub_shift(mr + fr, False), _sub_shift(mi + fi, False)

    def fix(rows, pw):
        cr, ci = _cmul(pw[0], pw[1], ir, ii)
        tr, ti = lr[rows, :] + cr, li[rows, :] + ci
        lr[rows, :] = tr
        li[rows, :] = ti
        return tr, ti

    def grad_a(tr, ti, spr, spi, acc):
        return acc[0] + tr * spr + ti * spi, acc[1] + ti * spr - tr * spi

    def carry_in(k, c):
        i = SEG_LEN - 1 - k
        rows = pl.ds(pl.multiple_of(i * SEG, SEG), SEG)
        prev = pl.ds(pl.multiple_of((i - 1) * SEG, SEG), SEG)
        tr, ti = fix(rows, (c[0], c[1]))
        acc = grad_a(tr, ti, sr[prev, :], si[prev, :], (c[2], c[3]))
        nr, ni = _cmul(c[0], c[1], ar, ai)
        return nr, ni, acc[0], acc[1]

    pwr, pwi, accr, acci = lax.fori_loop(0, SEG_LEN - 1, carry_in, (ar, ai, zero, zero))
    tr, ti = fix(pl.ds(0, SEG), (pwr, pwi))
    last = pl.ds((SEG_LEN - 1) * SEG, SEG)
    accr, acci = grad_a(tr, ti, _sub_shift(sr[last, :], True), _sub_shift(si[last, :], True), (accr, acci))
    return jnp.sum(accr, axis=0, keepdims=True), jnp.sum(acci, axis=0, keepdims=True)


S5_BWD_VMEM = 58 * 2**20


def s5_backward(dg, proj, s_re, s_im, wb_re, wb_im, wc_re, wc_im, a_re, a_im, d, dproj, plans=()):
    def kern(dg_ref, u_ref, sr_in, si_in, wbr, wbi, wcr, wci, ar, ai, d_ref, _, du_ref, dd_ref, dwcr, dwci, dwbr, dwbi, dar, dai, lr, li, sr, si):
        u = _wide(u_ref[...])
        sr[...], si[...] = _wide(sr_in[...]), _wide(si_in[...])
        _, vjp_out = jax.vjp(fn_s5_out, sr[...], si[...], u, d_ref[...], wcr[...], wci[...])
        lr[...], li[...], du_out, dd_ref[...], dwcr[...], dwci[...] = vjp_out((_wide(dg_ref[...]),))
        dar[...], dai[...] = _adjoint_scan_in_place(lr, li, sr, si, ar[...], ai[...])
        _, vjp_in = jax.vjp(fn_s5_bu, u, wbr[...], wbi[...])
        du_in, dwbr[...], dwbi[...] = vjp_in((lr[...], li[...]))
        du_ref[...] = (du_out + du_in).astype(du_ref.dtype)

    u_spec, s_spec, wb_spec, wc_spec, a_spec, d_spec = _s5_specs()
    outs = [(SDS(dproj.shape, dproj.dtype), u_spec), (SDS(d.shape, F32), d_spec), (SDS(wc_re.shape, F32), wc_spec), (SDS(wc_im.shape, F32), wc_spec),
            (SDS(wb_re.shape, F32), wb_spec), (SDS(wb_im.shape, F32), wb_spec), (SDS(a_re.shape, F32), a_spec), (SDS(a_im.shape, F32), a_spec)]
    return hosted_call(kern, name="s5_backward", grid=(S5_BLOCKS,),
                       in_specs=[u_spec, u_spec, s_spec, s_spec, wb_spec, wb_spec, wc_spec, wc_spec, a_spec, a_spec, d_spec, ANY],
                       out_specs=[sp for _, sp in outs], out_shape=[sd for sd, _ in outs], aliases={11: 0},
                       operands=[dg, proj, s_re, s_im, wb_re, wb_im, wc_re, wc_im, a_re, a_im, d, dproj],
                       scratch_shapes=[pltpu.VMEM((L, S5_LANES), F32)] * 4,
                       cparams=pltpu.CompilerParams(vmem_limit_bytes=S5_BWD_VMEM), plans=plans)


def fn_rms(x, g):
    return (rms(x, g),)


def fn_s5_disc(lre, lim, ls):
    step = jnp.exp(ls)
    e = jnp.exp(lre * step)
    a_re, a_im = e * jnp.cos(lim * step), e * jnp.sin(lim * step)
    den = lre * lre + lim * lim
    nr, ni = a_re - 1.0, a_im
    return a_re, a_im, (nr * lre + ni * lim) / den, (ni * lre - nr * lim) / den


def _group_mask(rows, cols, row_div, col_div):
    r = lax.broadcasted_iota(jnp.int32, (rows, cols), 0) // row_div % 8
    c = lax.broadcasted_iota(jnp.int32, (rows, cols), 1) // col_div
    return r == c


def _spread(x, mask):
    w = x.shape[1]
    copy = (lax.broadcasted_iota(jnp.int32, (w, 8 * w), 1) % w == lax.broadcasted_iota(jnp.int32, (w, 8 * w), 0)).astype(F32)
    return jnp.where(mask, jnp.dot(x, copy, precision=lax.Precision.HIGHEST, preferred_element_type=F32), 0.0)


def fn_s5_bmat(b_re, b_im, coef_re, coef_im):
    mask = _group_mask(b_re.shape[0], 8 * SC, SP, SC)
    return _spread(coef_re * b_re - coef_im * b_im, mask), _spread(coef_re * b_im + coef_im * b_re, mask)


def fn_s5_cmat(c_re, c_im):
    mask = _group_mask(c_re.shape[0], 8 * SP, SC, SP)
    return _spread(c_re, mask), _spread(c_im, mask)


def fn_s5_bu(u, wb_re, wb_im):
    return mm_nt(u, wb_re), mm_nt(u, wb_im)


def fn_s5_out(sr, si, u, d, wc_re, wc_im):
    y = mm_nt(sr, wc_re) - mm_nt(si, wc_im) + d * u
    return (jax.nn.gelu(y),)


def fn_merge_glu(z, mo, gate):
    yg = z[:, :PW] * jax.nn.sigmoid(z[:, PW:])
    return (jnp.concatenate([yg, mo], axis=1) * silu(gate),)


def fn_merge(prim, mo, gate):
    return (jnp.concatenate([prim, mo], axis=1) * silu(gate),)


def fn_mem_k(kv, g):
    return (jnp.concatenate([rms(kv[:, h * XHD:(h + 1) * XHD], g) for h in range(XH)], axis=1),)


def fn_mem_attn(xq, kn, v, g):
    outs = []
    for h in range(XH):
        sl = slice(h * XHD, (h + 1) * XHD)
        p = softmax_rows(mm_nt(rms(xq[:, sl], g), kn[:, sl]) * (XHD ** -0.5))
        outs.append(mm_nn(p, v[:, sl]))
    return (jnp.concatenate(outs, axis=1),)


def _half_rms(x, g):
    lo = lax.broadcasted_iota(jnp.int32, x.shape, 1) < ROPE
    x2 = x * x
    s_lo = jnp.sum(jnp.where(lo, x2, 0.0), axis=1, keepdims=True)
    s_hi = jnp.sum(jnp.where(lo, 0.0, x2), axis=1, keepdims=True)
    return x * lax.rsqrt(jnp.where(lo, s_lo, s_hi) / ROPE + EPS) * g


def _rope(x, cos2, sin_signed):
    first = lax.broadcasted_iota(jnp.int32, x.shape, 1) % ROPE < ROPE // 2
    return x * cos2 + jnp.where(first, lane_roll(x, 128 - ROPE // 2), lane_roll(x, ROPE // 2)) * sin_signed


def fn_mla_prep(q, kv, kr, cos2, sin_signed, qnn, knn, qrn, krn):
    lo = lax.broadcasted_iota(jnp.int32, kr.shape, 1) < ROPE
    kr_pad = jnp.where(lo, _rope(_half_rms(kr, krn), cos2, sin_signed), 0.0)
    qf, kf, vs = [], [], []
    for m in range(MH // 2):
        pair = _rope(_half_rms(q[:, MH * NOPE + 128 * m:MH * NOPE + 128 * (m + 1)], qrn), cos2, sin_signed)
        for h, rope_h in ((2 * m, pair), (2 * m + 1, lane_roll(pair, ROPE))):
            qf.append(jnp.concatenate([rms(q[:, NOPE * h:NOPE * (h + 1)], qnn), jnp.where(lo, rope_h, 0.0)], axis=1))
    for h in range(MH):
        kf.append(jnp.concatenate([rms(kv[:, 256 * h:256 * h + NOPE], knn), kr_pad], axis=1))
        vs.append(kv[:, 256 * h + NOPE:256 * (h + 1)])
    return jnp.stack(qf), jnp.stack(kf), jnp.stack(vs)


ATT_TQ = 512


def _attn_scores(q, kf):
    tq = q.shape[0]
    scale = (NOPE + ROPE) ** -0.5
    own = _dg(q, kf[-tq:], 1, 1) * scale
    own = jnp.where(lax.broadcasted_iota(jnp.int32, own.shape, 1) <= lax.broadcasted_iota(jnp.int32, own.shape, 0), own, jnp.finfo(F32).min)
    return own if kf.shape[0] == tq else jnp.concatenate([_dg(q, kf[:-tq], 1, 1) * scale, own], axis=1)


def _attn_specs():
    q_spec = pl.BlockSpec((None, ATT_TQ, 256), lambda h, i: (h, i, 0))
    k_spec = pl.BlockSpec((None, L, 256), lambda h, i: (h, 0, 0))
    v_spec = pl.BlockSpec((None, L, 128), lambda h, i: (h, 0, 0))
    o_spec = pl.BlockSpec((ATT_TQ, 128), lambda h, i: (i, h))
    lse_spec = pl.BlockSpec((None, ATT_TQ, 1), lambda h, i: (h, i, 0))
    return q_spec, k_spec, v_spec, o_spec, lse_spec


def causal_attn(qf, kf, vh):
    n_tiles = L // ATT_TQ

    def kern(q_ref, k_ref, v_ref, o_ref, lse_ref):
        i = pl.program_id(1)
        for t in range(n_tiles):
            @pl.when(i == t)
            def _(t=t):
                keys = (t + 1) * ATT_TQ
                s = _attn_scores(q_ref[...], k_ref[:keys, :])
                m = jnp.max(s, axis=-1, keepdims=True)
                e = jnp.exp(s - m)
                total = jnp.sum(e, axis=-1, keepdims=True)
                o_ref[...] = (_dg(e, v_ref[:keys, :], 1, 0) / total).astype(o_ref.dtype)
                lse_ref[...] = m + jnp.log(total)

    q_spec, k_spec, v_spec, o_spec, lse_spec = _attn_specs()
    return pl.pallas_call(kern, grid=(MH, n_tiles), in_specs=[q_spec, k_spec, v_spec], out_specs=[o_spec, lse_spec],
                          out_shape=[SDS((L, MH * VD), F32), SDS((MH, L, 1), F32)], name="l1_attn", compiler_params=_cparams())(qf, kf, vh)


def causal_attn_bwd(qf, kf, vh, out, lse, dout):
    n_tiles = L // ATT_TQ
    scale = (NOPE + ROPE) ** -0.5

    def kern(q_ref, k_ref, v_ref, o_ref, lse_ref, do_ref, dq_ref, dk_ref, dv_ref):
        i = pl.program_id(1)

        @pl.when(i == 0)
        def _():
            dk_ref[...] = jnp.zeros_like(dk_ref)
            dv_ref[...] = jnp.zeros_like(dv_ref)

        for t in range(n_tiles):
            @pl.when(i == t)
            def _(t=t):
                keys = (t + 1) * ATT_TQ
                q, k, v, do = q_ref[...], k_ref[:keys, :], v_ref[:keys, :], do_ref[...]
                p = jnp.exp(_attn_scores(q, k) - lse_ref[...])
                delta = jnp.sum(do * _wide(o_ref[...]), axis=-1, keepdims=True)
                dv_ref[:keys, :] += _dg(p, do, 0, 0)
                ds = p * (_dg(do, v, 1, 1) - delta) * scale
                dq_ref[...] = _dg(ds, k, 1, 0)
                dk_ref[:keys, :] += _dg(ds, q, 0, 0)

    q_spec, k_spec, v_spec, o_spec, lse_spec = _attn_specs()
    return pl.pallas_call(kern, grid=(MH, n_tiles), in_specs=[q_spec, k_spec, v_spec, o_spec, lse_spec, o_spec],
                          out_specs=[q_spec, k_spec, v_spec], out_shape=[SDS(qf.shape, F32), SDS(kf.shape, F32), SDS(vh.shape, F32)],
                          name="l1_attn_bwd", compiler_params=_cparams())(qf, kf, vh, out, lse, dout)


def loss_and_grad(y, target, tl=256):
    def kern(y_ref, t_ref, dy_ref, loss_ref):
        d = y_ref[...] - t_ref[...]
        dy_ref[...] = d / D

        @pl.when(pl.program_id(0) == 0)
        def _():
            loss_ref[...] = jnp.zeros_like(loss_ref)

        loss_ref[...] += 0.5 * jnp.sum(jnp.sum(d * d, axis=1, keepdims=True), axis=0, keepdims=True) / D

    return pl.pallas_call(kern, grid=(L // tl,), in_specs=[rspec(tl, D), rspec(tl, D)], out_specs=[rspec(tl, D), cspec((1, 1))],
                          out_shape=[SDS((L, D), F32), SDS((1, 1), F32)], name="loss", compiler_params=_cparams())(y, target)


def adamw(name, w, g, m, v):
    rows, cols = w.shape
    block_row_bytes = 7 * 2 * 4 * max(cols, 128)
    tr = _row_tile(rows, min(2048, ADAMW_VMEM // block_row_bytes // 8 * 8), 8)

    def kern(w_ref, g_ref, m_ref, v_ref, d_ref, nm_ref, nv_ref):
        gg = g_ref[...]
        nm = ADAM_B1 * m_ref[...] + (1.0 - ADAM_B1) * gg
        nv = ADAM_B2 * v_ref[...] + (1.0 - ADAM_B2) * jnp.square(gg)
        m_hat = nm / (1.0 - ADAM_B1 ** ADAM_STEP)
        v_hat = nv / (1.0 - ADAM_B2 ** ADAM_STEP)
        d_ref[...] = -ADAM_LR * (m_hat / (jnp.sqrt(v_hat) + ADAM_EPS) + ADAM_WD * w_ref[...])
        nm_ref[...] = nm
        nv_ref[...] = nv

    spec = rspec(tr, cols)
    return hosted_call(kern, name=name, grid=(rows // tr,), in_specs=[spec] * 4, out_specs=[spec] * 3,
                       out_shape=[SDS((rows, cols), F32)] * 3, operands=[w, g, m, v])


def _row_tile(rows, cap=512, unit=16):
    return max(t for t in range(unit, cap + 1, unit) if rows % t == 0)


def _place():
    x, y, c = lax.axis_index("x"), lax.axis_index("y"), lax.axis_index("c")
    return x, y, c, [(1 - x, y), (x, 1 - y), (1 - x, 1 - y)]


def _row_chunks(rows, n, dtype):
    unit = 32 // jnp.dtype(dtype).itemsize
    base, extra = divmod(rows // unit, n)
    out, start = [], 0
    for k in range(n):
        size = (base + (k < extra)) * unit
        if size:
            out.append((start, size))
            start += size
    assert start == rows, (rows, unit)
    return out


PIECE_BYTES = 1 << 20


def _pieces(shapes_dtypes, rows_of):
    out = []
    for b, (shape, dtype) in enumerate(shapes_dtypes):
        rows = rows_of(shape)
        n = max(1, min(4, rows * shape[-1] * jnp.dtype(dtype).itemsize // PIECE_BYTES))
        out += [(b, st, sz) for st, sz in _row_chunks(rows, n, dtype)]
    return out


def all_gather_chips(name, shards):
    nb = len(shards)
    pieces = _pieces([(s.shape, s.dtype) for s in shards], lambda shape: shape[0] // 2)
    n = len(pieces)

    def body(*refs):
        x_refs, out_refs, send_sems, recv_sems = refs[:nb], refs[nb:2 * nb], refs[2 * nb], refs[2 * nb + 1]
        x, y, c, chips = _place()
        sibling = (x, y, 1 - c)
        mine = 2 * x + y

        def copy(sem, chip, cc, k, to, from_input=False):
            b, st, sz = pieces[k]
            rows_k = pl.ds(cc * (x_refs[b].shape[0] // 2) + st, sz)
            dst = out_refs[b].at[chip, rows_k, :]
            return pltpu.make_async_remote_copy(src_ref=x_refs[b].at[rows_k, :] if from_input else dst, dst_ref=dst,
                                                send_sem=send_sems.at[sem], recv_sem=recv_sems.at[sem], device_id=to, device_id_type=MESH_ID)

        order = [(k, j, 2 * cx + cy, (cx, cy, c)) for k in range(n) for j, (cx, cy) in enumerate(chips)]
        first = [copy(j * n + k, mine, c, k, to, from_input=True) for k, j, _, to in order]
        for cp in first:
            cp.start()
        passed = []
        for k, j, chip, _ in order:
            copy(j * n + k, chip, c, k, sibling).wait_recv()
            passed.append(copy((3 + j) * n + k, chip, c, k, sibling))
            passed[-1].start()
        for k, j, chip, _ in order:
            copy((3 + j) * n + k, chip, 1 - c, k, sibling).wait_recv()
        for cp in first + passed:
            cp.wait_send()

    return pl.pallas_call(body, in_specs=[ANY] * nb, out_specs=[ANY] * nb, out_shape=[SDS((4,) + s.shape, s.dtype) for s in shards],
                          scratch_shapes=_dma_sems(6 * n), name=name)(*shards)


def plan_gather_ici(shards):
    pieces = _pieces([(s.shape, s.dtype) for s in shards], lambda shape: shape[0] // 2)
    n = len(pieces)

    def copies(x_refs, out_refs, send_sems, recv_sems):
        x, y, c, chips = _place()
        mine = 2 * x + y

        def copy(j, k, chip, to, from_input):
            b, st, sz = pieces[k]
            rows_k = pl.ds(c * (x_refs[b].shape[0] // 2) + st, sz)
            dst = out_refs[b].at[chip, rows_k, :]
            return pltpu.make_async_remote_copy(src_ref=x_refs[b].at[rows_k, :] if from_input else dst, dst_ref=dst, send_sem=send_sems.at[j * n + k],
                                                recv_sem=recv_sems.at[j * n + k], device_id=to, device_id_type=MESH_ID)

        order = [(k, j, 2 * cx + cy, (cx, cy, c)) for k in range(n) for j, (cx, cy) in enumerate(chips)]
        return [copy(j, k, mine, to, True) for k, j, _, to in order], [copy(j, k, chip, to, False) for k, j, chip, to in order]

    return Plan(shards, [SDS((4,) + s.shape, s.dtype) for s in shards], {}, 3 * n, copies)


def plan_gather_pass(gathered):
    pieces = _pieces([(g.shape[1:], g.dtype) for g in gathered], lambda shape: shape[0] // 2)
    n = len(pieces)

    def copies(_, out_refs, send_sems, recv_sems):
        x, y, c, chips = _place()

        def copy(j, k, chip, cc):
            b, st, sz = pieces[k]
            rows_k = out_refs[b].at[chip, pl.ds(cc * (out_refs[b].shape[1] // 2) + st, sz), :]
            return pltpu.make_async_remote_copy(src_ref=rows_k, dst_ref=rows_k, send_sem=send_sems.at[j * n + k], recv_sem=recv_sems.at[j * n + k],
                                                device_id=(x, y, 1 - c), device_id_type=MESH_ID)

        order = [(k, j, 2 * cx + cy) for k in range(n) for j, (cx, cy) in enumerate(chips)]
        return [copy(j, k, chip, c) for k, j, chip in order], [copy(j, k, chip, 1 - c) for k, j, chip in order]

    return Plan(gathered, [SDS(g.shape, g.dtype) for g in gathered], {i: i for i in range(len(gathered))}, 3 * n, copies)


def plan_pair_exchange(gs):
    pieces = _pieces([(g.shape, g.dtype) for g in gs], lambda shape: shape[1] // 2)

    def copies(g_refs, got_refs, send_sems, recv_sems):
        x, y, c, _ = _place()
        swaps = [pltpu.make_async_remote_copy(src_ref=g_refs[b].at[:, pl.ds((1 - c) * (g_refs[b].shape[1] // 2) + st, sz), :],
                                              dst_ref=got_refs[b].at[:, pl.ds(st, sz), :], send_sem=send_sems.at[k], recv_sem=recv_sems.at[k],
                                              device_id=(x, y, 1 - c), device_id_type=MESH_ID)
                 for k, (b, st, sz) in enumerate(pieces)]
        return swaps, swaps

    return Plan(gs, [SDS((g.shape[0], g.shape[1] // 2, g.shape[2]), g.dtype) for g in gs], {}, len(pieces), copies)


def plan_chip_scatter(ps):
    pieces = _pieces([(p.shape, p.dtype) for p in ps], lambda shape: shape[1])
    n = len(pieces)

    def copies(p_refs, q_refs, send_sems, recv_sems):
        x, y, c, chips = _place()
        mine = 2 * x + y

        def copy(j, k, src_slot, dst_slot, to):
            b, st, sz = pieces[k]
            return pltpu.make_async_remote_copy(src_ref=p_refs[b].at[src_slot, pl.ds(st, sz), :], dst_ref=q_refs[b].at[dst_slot, pl.ds(st, sz), :],
                                                send_sem=send_sems.at[j * n + k], recv_sem=recv_sems.at[j * n + k], device_id=to,
                                                device_id_type=MESH_ID)

        order = [(k, j, 2 * cx + cy, (cx, cy, c)) for k in range(n) for j, (cx, cy) in enumerate(chips)]
        return [copy(j, k, chip, mine, to) for k, j, chip, to in order], [copy(j, k, mine, chip, to) for k, j, chip, to in order]

    return Plan(ps, [SDS(p.shape, p.dtype) for p in ps], {}, 3 * n, copies)


def plan_pair_join(bufs):
    pieces = _pieces([(b.shape, b.dtype) for b in bufs], lambda shape: shape[0] // 2)

    def copies(_, out_refs, send_sems, recv_sems):
        x, y, c, _ = _place()

        def copy(k, cc):
            b, st, sz = pieces[k]
            rows_k = out_refs[b].at[pl.ds(cc * (out_refs[b].shape[0] // 2) + st, sz), :]
            return pltpu.make_async_remote_copy(src_ref=rows_k, dst_ref=rows_k, send_sem=send_sems.at[k], recv_sem=recv_sems.at[k],
                                                device_id=(x, y, 1 - c), device_id_type=MESH_ID)

        return [copy(k, c) for k in range(len(pieces))], [copy(k, 1 - c) for k in range(len(pieces))]

    return Plan(bufs, [SDS(b.shape, b.dtype) for b in bufs], {i: i for i in range(len(bufs))}, len(pieces), copies)


def run_plan(name, plan):
    hosted_call(lambda: None, name=name, grid=(1,), in_specs=[], out_specs=[], out_shape=[], operands=[], plans=[plan])
    return plan.results


HBM = pl.BlockSpec(memory_space=pltpu.HBM)
SEMS = pl.BlockSpec(memory_space=pltpu.SEMAPHORE)
SPLIT_PARAMS = dict(has_side_effects=pltpu.SideEffectType.DATAFLOW_SIDE_EFFECTING)


def _plan_buffers(plan):
    in_place = {o: i for i, o in plan.aliases.items()}
    bufs = [pltpu.with_memory_space_constraint(a, pltpu.HBM) for a in plan.operands]
    where = []
    for o, sd in enumerate(plan.out_shape):
        if o in in_place:
            where.append(in_place[o])
        else:
            where.append(len(bufs))
            bufs.append(pltpu.with_memory_space_constraint(lax.empty(sd.shape, sd.dtype), pltpu.HBM))
    return bufs, where


def split_start(name, plans):
    layout = [_plan_buffers(p) for p in plans]
    counts = [len(b) for b, _ in layout]
    n_buf = sum(counts)

    def body(*refs):
        sems, token = refs[n_buf:n_buf + 2 * len(plans)], refs[-1]
        pos = 0
        for k, (p, (_, where)) in enumerate(zip(plans, layout)):
            mine = refs[pos:pos + counts[k]]
            pos += counts[k]
            sends, _ = p.copies(mine[:len(p.operands)], [mine[w] for w in where], sems[2 * k], sems[2 * k + 1])
            for cp in sends:
                cp.start()
        token[...] = jnp.zeros_like(token)

    bufs = [b for bs, _ in layout for b in bs]
    res = pl.pallas_call(
        body, name=name, in_specs=[HBM] * n_buf,
        out_specs=[SEMS] * (2 * len(plans)) + [HBM] * n_buf + [pl.BlockSpec(memory_space=pltpu.VMEM)],
        out_shape=[pltpu.SemaphoreType.DMA((p.n_sems,)) for p in plans for _ in range(2)] + [pltpu.HBM(b.shape, b.dtype) for b in bufs]
        + [SDS((8, 128), F32)],
        input_output_aliases={i: 2 * len(plans) + i for i in range(n_buf)}, compiler_params=pltpu.CompilerParams(**SPLIT_PARAMS))(*bufs)
    pos = 2 * len(plans)
    for k, p in enumerate(plans):
        p.in_flight = (res[2 * k], res[2 * k + 1], list(res[pos:pos + counts[k]]), layout[k][1])
        pos += counts[k]
    return res[-1]


def split_wait(name, plan, after):
    send_sems, recv_sems, bufs, where = plan.in_flight
    n_buf = len(bufs)

    def body(*refs):
        mine = refs[:n_buf]
        sends, recvs = plan.copies(mine[:len(plan.operands)], [mine[w] for w in where], refs[n_buf], refs[n_buf + 1])
        for cp in recvs:
            cp.wait_recv()
        for cp in sends:
            cp.wait_send()

    res = pl.pallas_call(body, name=name, in_specs=[HBM] * n_buf + [SEMS, SEMS, ANY], out_specs=[HBM] * n_buf,
                         out_shape=[pltpu.HBM(b.shape, b.dtype) for b in bufs], input_output_aliases={i: i for i in range(n_buf)},
                         compiler_params=pltpu.CompilerParams(**SPLIT_PARAMS))(*bufs, send_sems, recv_sems, after)
    plan.results = [res[w] for w in where]
    return plan.results


def pair_add(name, g, got, place):
    slots, rows, cols = g.shape
    half = rows // 2
    tr = _row_tile(half)
    nb = half // tr

    def kern(_, g_ref, t_ref, o_ref):
        o_ref[...] = (g_ref[...].astype(F32) + t_ref[...].astype(F32)).astype(o_ref.dtype)

    blk = pl.BlockSpec((None, tr, cols), lambda s, i, p: (s, i, 0))
    grid_spec = pltpu.PrefetchScalarGridSpec(
        num_scalar_prefetch=1, grid=(slots, nb),
        in_specs=[pl.BlockSpec((None, tr, cols), lambda s, i, p: (s, p[1] * nb + i, 0)), blk], out_specs=blk)
    return pl.pallas_call(kern, grid_spec=grid_spec, out_shape=SDS((slots, half, cols), g.dtype), name=name,
                          compiler_params=_cparams())(place, g, got)


def chip_add(name, p, q, place):
    slots, half, cols = p.shape
    tr = _row_tile(half)
    nb = half // tr

    def kern(_, p_ref, q1, q2, q3, o_ref):
        o_ref[...] = p_ref[...].astype(F32) + q1[...].astype(F32) + q2[...].astype(F32) + q3[...].astype(F32)

    def slot(k):
        return pl.BlockSpec((None, tr, cols), lambda i, pr: ((pr[0] + k) % slots, i, 0))

    grid_spec = pltpu.PrefetchScalarGridSpec(
        num_scalar_prefetch=1, grid=(nb,), in_specs=[slot(0), slot(1), slot(2), slot(3)],
        out_specs=pl.BlockSpec((tr, cols), lambda i, pr: (pr[1] * nb + i, 0)))
    return pl.pallas_call(kern, grid_spec=grid_spec, out_shape=SDS((2 * half, cols), F32), name=name,
                          compiler_params=_cparams())(place, p, q, q, q)


def pair_adds(tag, gs, gots, place):
    return [pair_add(f"{tag}_pair_add_{i}", g, got, place) for i, (g, got) in enumerate(zip(gs, gots))]


def chip_adds(tag, pairs, qs, place):
    return [chip_add(f"{tag}_chip_add_{i}", p, q, place) for i, (p, q) in enumerate(zip(pairs, qs))]


def reduce_scatter_chips(tag, gs, place):
    pairs = pair_adds(tag, gs, run_plan(tag + "_pair_exchange", plan_pair_exchange(gs)), place)
    return run_plan(tag + "_pair_join", plan_pair_join(chip_adds(tag, pairs, run_plan(tag + "_chip_scatter", plan_chip_scatter(pairs)), place)))


BIG = [("w_out", (2, 512, 1024)), ("w_mem_kv", (2, 256, 1024)), ("s5_w_in", (1, 1024, 1024)), ("s5_w_glu", (1, 1536, 768)),
       ("mla_w_in", (1, 1024, 848)), ("mla_w_uq", (1, 512, 576)), ("mla_w_ukv", (1, 256, 768))]
SHARDED_SMALL = [("mla_q_lora_norm", (1, 128)), ("mla_kv_lora_norm", (1, 64))]
SMALL = [("ln_gain", (2, 1024)), ("mem_norm", (2, 1024)), ("xq_norm", (2, 128)), ("xk_norm", (2, 128)),
         ("s5_lambda_re", (1, 96, 64)), ("s5_lambda_im", (1, 96, 64)), ("s5_log_step", (1, 96)),
         ("s5_b_re", (1, 96, 64, 16)), ("s5_b_im", (1, 96, 64, 16)), ("s5_c_re", (1, 96, 16, 64)), ("s5_c_im", (1, 96, 16, 64)),
         ("s5_d", (1, 1536)), ("mla_q_nope_norm", (1, 128)), ("mla_k_nope_norm", (1, 128)), ("mla_q_rope_norm", (1, 64)),
         ("mla_k_rope_norm", (1, 64))]
WEIGHT_ORDER = ["ln_gain", "w_out", "mem_norm", "w_mem_kv", "xq_norm", "xk_norm", "s5_w_in", "s5_lambda_re", "s5_lambda_im",
                "s5_log_step", "s5_b_re", "s5_b_im", "s5_c_re", "s5_c_im", "s5_d", "s5_w_glu", "mla_w_in", "mla_q_lora_norm",
                "mla_kv_lora_norm", "mla_w_uq", "mla_w_ukv", "mla_q_nope_norm", "mla_k_nope_norm", "mla_q_rope_norm", "mla_k_rope_norm"]
MINOR_LAST = {"mla_w_in": (0, 2, 1), "mla_w_uq": (0, 2, 1), "s5_b_re": (0, 2, 3, 1), "s5_b_im": (0, 2, 3, 1),
              "s5_c_re": (0, 2, 3, 1), "s5_c_im": (0, 2, 3, 1)}
SMALL_FULL = SMALL + [(n, (1, 4 * s[1])) for n, s in SHARDED_SMALL]
N_SMALL = sum(math.prod(s) for _, s in SMALL_FULL)
SMALL_ROWS, SMALL_LANES = 128, 1024

PAIR_OUT, PAIR_MKV, PAIR_ROWS = 0, 512, 768


def stack_shards(w, dtype):
    pairs = [jnp.concatenate([w["w_out"][l], w["w_mem_kv"][l]], axis=0).astype(dtype) for l in range(2)]
    return ([w["s5_w_in"][0].astype(dtype)], [pairs[0], w["s5_w_glu"][0].astype(dtype)],
            [pairs[1], w["mla_w_ukv"][0].astype(dtype), w["mla_w_in"][0].astype(dtype), w["mla_w_uq"][0].astype(dtype)])


def pair_views(pair):
    return {"w_out": Sharded(pair, "row", PAIR_OUT, 512), "w_mem_kv": Sharded(pair, "row", PAIR_MKV, 256)}


def grad_views():
    pair = SDS((4, PAIR_ROWS, 1024), BF16)
    return {"w_out": Sharded(pair, "row", PAIR_OUT, 512), "w_mem_kv": Sharded(pair, "row", PAIR_MKV, 256),
            "s5_w_in": Sharded(SDS((4, 1024, 1024), BF16), "col", 0, 1024), "s5_w_glu": Sharded(SDS((4, 1536, 768), BF16), "col", 0, 1536),
            "mla_w_ukv": Sharded(SDS((4, 256, 768), BF16), "col", 0, 256)}


def cols_to_shards(full):
    return full.reshape(full.shape[0], 4, full.shape[1] // 4).transpose(1, 0, 2)


def shards_to_cols(arr):
    return arr.transpose(1, 0, 2).reshape(arr.shape[1], 4 * arr.shape[2])


def mla_in_permute(w):
    o1, o2, o3, o4 = QL, QL + KVL, QL + KVL + ROPE, QL + KVL + ROPE + XQW
    return jnp.concatenate([w[:, o4:], w[:, :o1], w[:, o3:o4], w[:, o1:o2], w[:, o2:o3],
                            jnp.zeros((w.shape[0], MLA_IN_P - MLA_IN), w.dtype)], axis=1)


def mla_in_unpermute(d):
    return jnp.concatenate([d[:, 2048:2560], d[:, 3072:3328], d[:, 3328:3392], d[:, 2560:3072], d[:, :2048]], axis=1)


def uq_permute(w):
    w3 = w.reshape(w.shape[0], MH, NOPE + ROPE)
    return jnp.concatenate([w3[:, :, :NOPE].reshape(w.shape[0], MH * NOPE), w3[:, :, NOPE:].reshape(w.shape[0], MH * ROPE)], axis=1)


def uq_unpermute(d):
    dn = d[:, :MH * NOPE].reshape(d.shape[0], MH, NOPE)
    dr = d[:, MH * NOPE:].reshape(d.shape[0], MH, ROPE)
    return jnp.concatenate([dn, dr], axis=2).reshape(d.shape[0], MH * (NOPE + ROPE))


def time_permute(a):
    return a.reshape(SEG, SEG_LEN, a.shape[-1]).transpose(1, 0, 2).reshape(L, a.shape[-1])


def time_unpermute(a):
    return a.reshape(SEG_LEN, SEG, a.shape[-1]).transpose(1, 0, 2).reshape(L, a.shape[-1])


def mem_branch_fwd(tag, mem, mem_norm, w_mem_kv, xk_norm):
    mn = row_fwd(tag + "_mem_rms", fn_rms, ML, ML, [(mem, D, 0)], [mem_norm], [(D, BF16)])[0]
    kv = matmul(tag + "_mem_kv", mn, w_mem_kv, "nn", F32)
    kn = row_fwd(tag + "_mem_knorm", fn_mem_k, ML, ML, [(kv, XQW, 0)], [xk_norm], [(XQW, F32)])[0]
    return mn, kv, kn


def mem_branch_bwd(tag, mem, mem_norm, w_mem_kv, xk_norm, mn, kv, dkn, dv, g_view, g_wide):
    dk, dxk = row_bwd(tag + "_mem_knorm_bwd", fn_mem_k, ML, ML, [(kv, XQW, 0)], [xk_norm], [(dkn, XQW, 0)], [True], [True])
    dkv = jnp.concatenate([dk, dv], axis=1)
    dmn = matmul(tag + "_mem_kv_dx", dkv, w_mem_kv, "nt", F32)
    g_wide = matmul(tag + "_mem_kv_dw", mn, dkv, "tn", out=g_view, into=g_wide)
    dmem_norm = row_bwd(tag + "_mem_rms_bwd", fn_rms, ML, ML, [(mem, D, 0)], [mem_norm], [(dmn, D, 0)], [False], [True])[0]
    return g_wide, dmem_norm, dxk


def mem_attn_fwd(tag, proj, cb, kn, kv, xq_norm):
    return row_fwd(tag + "_mem_attn", fn_mem_attn, L, 256, [(proj, XQW, cb)], [kn, kv[:, XQW:], xq_norm], [(XQW, F32)])[0]


def mem_attn_bwd(tag, proj, cb, kn, kv, xq_norm, dmo, dproj):
    place = {"cols": proj.shape[1], "cb": cb, "into": dproj, "dtype": dproj.dtype}
    return row_bwd(tag + "_mem_attn_bwd", fn_mem_attn, L, 256, [(proj, XQW, cb)], [kn, kv[:, XQW:], xq_norm], [(dmo, XQW, 0)],
                   [place], [True, True, True])


def device_step(x, mem, positions, target, small, env, hooks=None):
    hooks = hooks or {}

    def plans_for(name):
        return hooks[("plans", name)](env) if ("plans", name) in hooks else ()

    def around(when, name, last=None):
        if (when, name) in hooks:
            hooks[(when, name)](env, last)

    g = {}
    gw = grad_views()
    ln, mem_norm, xq_norm, xk_norm = small["ln_gain"], small["mem_norm"], small["xq_norm"], small["xk_norm"]

    lre, lim = small["s5_lambda_re"][0], small["s5_lambda_im"][0]
    ls = small["s5_log_step"].reshape(SG, 1)
    one = pl.BlockSpec((SG, SP), lambda i: (0, 0))
    col = pl.BlockSpec((SG, 1), lambda i: (0, 0))
    disc_ins = [(lre, one), (lim, one), (ls, col)]
    a_re, a_im, coef_re, coef_im = stage("s5_disc", fn_s5_disc, (1,), disc_ins, [(SDS((SG, SP), F32), one)] * 4)
    b_re, b_im = small["s5_b_re"].reshape(SN, SC), small["s5_b_im"].reshape(SN, SC)
    c_re, c_im = small["s5_c_re"].reshape(PW, SP), small["s5_c_im"].reshape(PW, SP)
    bmat_rows = [(b_re, SC, 0), (b_im, SC, 0), (coef_re.reshape(SN, 1), 1, 0), (coef_im.reshape(SN, 1), 1, 0)]
    wb_re, wb_im = row_fwd("s5_bmat", fn_s5_bmat, SN, 512, bmat_rows, [], [(128, F32)] * 2)
    cmat_rows = [(c_re, SP, 0), (c_im, SP, 0)]
    wc_re, wc_im = row_fwd("s5_cmat", fn_s5_cmat, PW, 128, cmat_rows, [], [(512, F32)] * 2)
    a_re_v, a_im_v = a_re.reshape(1, SN), a_im.reshape(1, SN)
    s5_d = small["s5_d"]

    xp = time_permute(x)
    h0 = row_fwd("l0_rms", fn_rms, L, 256, [(xp, D, 0)], [ln[0:1]], [(D, BF16)])[0]
    around("before", "l0_in", wb_re)
    w_in0 = Sharded(env["in0"], "col", 0, 1024)
    proj0 = matmul("l0_in", h0, w_in0, "nn")
    s_re, s_im, g0 = s5_forward(proj0, wb_re, wb_im, wc_re, wc_im, a_re_v, a_im_v, s5_d, plans=plans_for("s5_forward"))
    around("before", "l0_glu", g0)
    w0 = dict(pair_views(env["pair0"]), s5_w_glu=Sharded(env["glu"], "col", 0, 1536))
    z0 = matmul("l0_glu", g0, w0["s5_w_glu"], "nn", plans=plans_for("l0_glu"))
    mn0, kv0, kn0 = mem_branch_fwd("l0", mem, mem_norm[0:1], w0["w_mem_kv"], xk_norm[0:1])
    mo0 = mem_attn_fwd("l0", proj0, 3, kn0, kv0, xq_norm[0:1])
    o0 = row_fwd("l0_merge", fn_merge_glu, L, 256, [(z0, 2 * PW, 0), (mo0, XQW, 0), (proj0, BW, 1)], [], [(BW, BF16)])[0]
    around("before", "l0_out", o0)
    x1p = matmul("l0_out", o0, w0["w_out"], "nn", F32, add=xp, plans=plans_for("l0_out"))
    around("after", "l0_out", x1p)
    x1 = time_unpermute(x1p)

    w1 = dict(pair_views(env["pair1"]), mla_w_ukv=Sharded(env["ukv"], "col", 0, 256))
    w_in1, w_uq = env["w_in1"], env["w_uq"]
    h1 = row_fwd("l1_rms", fn_rms, L, 256, [(x1, D, 0)], [ln[1:2]], [(D, BF16)])[0]
    proj1 = matmul("l1_in", h1, w_in1, "nn")
    qln, kvln = env["q_lora_norm"].reshape(1, QL), env["kv_lora_norm"].reshape(1, KVL)
    cqn = row_fwd("l1_q_lora_rms", fn_rms, L, 256, [(proj1, QL, 4)], [qln], [(QL, BF16)])[0]
    ckvn = row_fwd("l1_kv_lora_rms", fn_rms, L, 256, [(proj1, KVL, 12)], [kvln], [(KVL, BF16)])[0]
    q = matmul("l1_uq", cqn, w_uq, "nn")
    kv = matmul("l1_ukv", ckvn, w1["mla_w_ukv"], "nn")
    inv_freq = ROPE_THETA ** (-jnp.arange(ROPE // 2, dtype=F32) / (ROPE // 2))
    ang = positions.astype(F32)[:, None] * inv_freq
    cos2 = jnp.tile(jnp.cos(ang), (1, 4))
    sin_signed = jnp.tile(jnp.concatenate([-jnp.sin(ang), jnp.sin(ang)], axis=1), (1, 2))
    qnn, knn = small["mla_q_nope_norm"], small["mla_k_nope_norm"]
    qrn, krn = jnp.tile(small["mla_q_rope_norm"], (1, 2)), jnp.tile(small["mla_k_rope_norm"], (1, 2))
    tp = 256
    prep_ins = [(q, rspec(tp, MH * (NOPE + ROPE))), (kv, rspec(tp, MH * 256)), (proj1, rspec(tp, 128, 26)),
                (cos2, rspec(tp, 128)), (sin_signed, rspec(tp, 128))] + [(a, cspec((1, 128))) for a in (qnn, knn, qrn, krn)]
    hq_spec = pl.BlockSpec((MH, tp, 256), lambda i: (0, i, 0))
    hv_spec = pl.BlockSpec((MH, tp, 128), lambda i: (0, i, 0))
    qf, kf, vh = stage("l1_mla_prep", fn_mla_prep, (L // tp,), prep_ins,
                       [(SDS((MH, L, 256), BF16), hq_spec), (SDS((MH, L, 256), BF16), hq_spec), (SDS((MH, L, 128), BF16), hv_spec)])
    attn, attn_lse = causal_attn(qf, kf, vh)
    mn1, kv1, kn1 = mem_branch_fwd("l1", mem, mem_norm[1:2], w1["w_mem_kv"], xk_norm[1:2])
    mo1 = mem_attn_fwd("l1", proj1, 5, kn1, kv1, xq_norm[1:2])
    o1 = row_fwd("l1_merge", fn_merge, L, 256, [(attn, PW, 0), (mo1, XQW, 0), (proj1, BW, 0)], [], [(BW, BF16)])[0]
    x2 = matmul("l1_out", o1, w1["w_out"], "nn", F32, add=x1)
    dx2, loss = loss_and_grad(x2, target)

    do1 = matmul("l1_out_dx", dx2, w1["w_out"], "nt")
    g_pair1 = matmul("l1_out_dw", o1, dx2, "tn", out=gw["w_out"])
    dattn, dmo1, dproj1 = row_bwd("l1_merge_bwd", fn_merge, L, 256, [(attn, PW, 0), (mo1, XQW, 0), (proj1, BW, 0)], [],
                                  [(do1, BW, 0)], [True, True, {"cols": MLA_IN_P, "cb": 0, "dtype": BF16}], [])
    dproj1, dkn1, dv1, dxqn1 = mem_attn_bwd("l1", proj1, 5, kn1, kv1, xq_norm[1:2], dmo1, dproj1)
    env["g_pair1"], dmem_norm1, dxk1 = mem_branch_bwd("l1", mem, mem_norm[1:2], w1["w_mem_kv"], xk_norm[1:2], mn1, kv1, dkn1, dv1,
                                                      gw["w_mem_kv"], g_pair1)
    dqf, dkf, dvh = causal_attn_bwd(qf, kf, vh, attn, attn_lse, dattn)
    prep_diffs = [("row", SDS((L, MH * (NOPE + ROPE)), BF16), rspec(tp, MH * (NOPE + ROPE))), ("row", SDS((L, MH * 256), BF16), rspec(tp, MH * 256)),
                  ("row", SDS((L, MLA_IN_P), BF16), rspec(tp, 128, 26), {"into": dproj1}), None, None] + [("acc", (0,))] * 4
    dq, dkv, dproj1, dqnn, dknn, dqrn, dkrn = stage_bwd("l1_mla_prep_bwd", fn_mla_prep, (L // tp,), prep_ins,
                                                        [(dqf, hq_spec), (dkf, hq_spec), (dvh, hv_spec)], prep_diffs)
    dcqn = matmul("l1_uq_dx", dq, w_uq, "nt")
    env["dw_uq"] = matmul("l1_uq_dw", cqn, dq, "tn")
    dckvn = matmul("l1_ukv_dx", dkv, w1["mla_w_ukv"], "nt")
    env["g_ukv"] = matmul("l1_ukv_dw", ckvn, dkv, "tn", out=gw["mla_w_ukv"])
    dproj1, dqln = row_bwd("l1_q_lora_rms_bwd", fn_rms, L, 256, [(proj1, QL, 4)], [qln], [(dcqn, QL, 0)],
                           [{"cols": MLA_IN_P, "cb": 4, "into": dproj1, "dtype": BF16}], [True])
    dproj1, dkvln = row_bwd("l1_kv_lora_rms_bwd", fn_rms, L, 256, [(proj1, KVL, 12)], [kvln], [(dckvn, KVL, 0)],
                            [{"cols": MLA_IN_P, "cb": 12, "into": dproj1, "dtype": BF16}], [True])
    dh1 = matmul("l1_in_dx", dproj1, w_in1, "nt")
    env["dw_in1"] = matmul("l1_in_dw", h1, dproj1, "tn")
    dx1, dln1 = row_bwd("l1_rms_bwd", fn_rms, L, 256, [(x1, D, 0)], [ln[1:2]], [(dh1, D, 0)], [{"add": (dx2, D, 0)}], [True],
                        plans=plans_for("l1_rms_bwd"))
    around("after", "l1_rms_bwd", dx1)
    dx1p = time_permute(dx1)

    do0 = matmul("l0_out_dx", dx1p, w0["w_out"], "nt", plans=plans_for("l0_out_dx"))
    g_pair0 = matmul("l0_out_dw", o0, dx1p, "tn", out=gw["w_out"])
    dz0, dmo0, dproj0 = row_bwd("l0_merge_bwd", fn_merge_glu, L, 256, [(z0, 2 * PW, 0), (mo0, XQW, 0), (proj0, BW, 1)], [],
                                [(do0, BW, 0)], [{"dtype": BF16}, True, {"cols": 2 * BW, "cb": 1, "dtype": BF16}], [])
    dproj0, dkn0, dv0, dxqn0 = mem_attn_bwd("l0", proj0, 3, kn0, kv0, xq_norm[0:1], dmo0, dproj0)
    env["g_pair0"], dmem_norm0, dxk0 = mem_branch_bwd("l0", mem, mem_norm[0:1], w0["w_mem_kv"], xk_norm[0:1], mn0, kv0, dkn0, dv0,
                                                      gw["w_mem_kv"], g_pair0)
    env["g_glu"] = matmul("l0_glu_dw", g0, dz0, "tn", out=gw["s5_w_glu"], plans=plans_for("l0_glu_dw"))
    dg0 = matmul("l0_glu_dx", dz0, w0["s5_w_glu"], "nt", plans=plans_for("l0_glu_dx"))
    around("before", "s5_backward", dg0)
    dproj0, dd, dwc_re, dwc_im, dwb_re, dwb_im, da_re, da_im = s5_backward(dg0, proj0, s_re, s_im, wb_re, wb_im, wc_re, wc_im,
                                                                           a_re_v, a_im_v, s5_d, dproj0, plans=plans_for("s5_backward"))
    around("after", "s5_backward", dd)
    env["g_in0"] = matmul("l0_in_dw", h0, dproj0, "tn", out=gw["s5_w_in"], plans=plans_for("l0_in_dw"))
    dh0 = matmul("l0_in_dx", dproj0, w_in0, "nt", plans=plans_for("l0_in_dx"))
    around("after", "l0_in_dx", dh0)
    dxp, dln0 = row_bwd("l0_rms_bwd", fn_rms, L, 256, [(xp, D, 0)], [ln[0:1]], [(dh0, D, 0)], [{"add": (dx1p, D, 0)}], [True])
    grad_x = time_unpermute(dxp)

    db_re, db_im, dcoef_re, dcoef_im = row_bwd("s5_bmat_bwd", fn_s5_bmat, SN, 512, bmat_rows, [], [(dwb_re, 128, 0), (dwb_im, 128, 0)],
                                               [True] * 4, [], plans=plans_for("s5_bmat_bwd"))
    dc_re, dc_im = row_bwd("s5_cmat_bwd", fn_s5_cmat, PW, 128, cmat_rows, [], [(dwc_re, 512, 0), (dwc_im, 512, 0)], [True] * 2, [],
                           plans=plans_for("s5_cmat_bwd"))
    disc_cts = [(da_re.reshape(SG, SP), one), (da_im.reshape(SG, SP), one), (dcoef_re.reshape(SG, SP), one), (dcoef_im.reshape(SG, SP), one)]
    dlre, dlim, dls = stage_bwd("s5_disc_bwd", fn_s5_disc, (1,), disc_ins, disc_cts, [("acc", (0,))] * 3)

    g["ln_gain"] = jnp.concatenate([dln0, dln1], axis=0)
    g["mem_norm"] = jnp.concatenate([dmem_norm0, dmem_norm1], axis=0)
    g["xq_norm"] = jnp.concatenate([dxqn0, dxqn1], axis=0)
    g["xk_norm"] = jnp.concatenate([dxk0, dxk1], axis=0)
    g["s5_lambda_re"], g["s5_lambda_im"], g["s5_log_step"] = dlre, dlim, dls
    g["s5_b_re"], g["s5_b_im"], g["s5_c_re"], g["s5_c_im"] = db_re, db_im, dc_re, dc_im
    g["s5_d"] = dd
    g["mla_q_lora_norm"], g["mla_kv_lora_norm"] = dqln, dkvln
    g["mla_q_nope_norm"], g["mla_k_nope_norm"] = dqnn, dknn
    g["mla_q_rope_norm"] = dqrn[:, :ROPE] + dqrn[:, ROPE:]
    g["mla_k_rope_norm"] = dkrn[:, :ROPE] + dkrn[:, ROPE:]
    return loss, grad_x, g


def kernel(x, mem, positions, ln_gain, w_out, mem_norm, w_mem_kv, xq_norm, xk_norm, s5_w_in, s5_lambda_re, s5_lambda_im, s5_log_step, s5_b_re, s5_b_im, s5_c_re, s5_c_im, s5_d, s5_w_glu, mla_w_in, mla_q_lora_norm, mla_kv_lora_norm, mla_w_uq, mla_w_ukv, mla_q_nope_norm, mla_k_nope_norm, mla_q_rope_norm, mla_k_rope_norm, loss_target, m_ln_gain, m_w_out, m_mem_norm, m_w_mem_kv, m_xq_norm, m_xk_norm, m_s5_w_in, m_s5_lambda_re, m_s5_lambda_im, m_s5_log_step, m_s5_b_re, m_s5_b_im, m_s5_c_re, m_s5_c_im, m_s5_d, m_s5_w_glu, m_mla_w_in, m_mla_q_lora_norm, m_mla_kv_lora_norm, m_mla_w_uq, m_mla_w_ukv, m_mla_q_nope_norm, m_mla_k_nope_norm, m_mla_q_rope_norm, m_mla_k_rope_norm, v_ln_gain, v_w_out, v_mem_norm, v_w_mem_kv, v_xq_norm, v_xk_norm, v_s5_w_in, v_s5_lambda_re, v_s5_lambda_im, v_s5_log_step, v_s5_b_re, v_s5_b_im, v_s5_c_re, v_s5_c_im, v_s5_d, v_s5_w_glu, v_mla_w_in, v_mla_q_lora_norm, v_mla_kv_lora_norm, v_mla_w_uq, v_mla_w_ukv, v_mla_q_nope_norm, v_mla_k_nope_norm, v_mla_q_rope_norm, v_mla_k_rope_norm):
    args = dict(locals())
    wts = {n: args[n] for n in WEIGHT_ORDER}
    mom = {n: args["m_" + n] for n in WEIGHT_ORDER}
    var = {n: args["v_" + n] for n in WEIGHT_ORDER}

    chip = 2 * lax.axis_index("x") + lax.axis_index("y")
    place = jnp.stack([chip, lax.axis_index("c")]).astype(jnp.int32)

    def own_slot(gathered, shards):
        return [lax.dynamic_update_slice(g, s[None], (chip, 0, 0)) for g, s in zip(gathered, shards)]

    groups = list(stack_shards(wts, BF16))
    groups[2].append(jnp.concatenate([mla_q_lora_norm, jnp.pad(mla_kv_lora_norm, ((0, 0), (0, 64))), jnp.zeros((14, 128), F32)], axis=0))
    over_ici = [plan_gather_ici(shards) for shards in groups]
    _SCHEDULE_BEHIND.clear()
    schedule_behind(split_start("gather_start", over_ici))
    env, hooks, passed_on = {}, {}, {}

    def arrived(k, after, pass_now):
        passing = plan_gather_pass(split_wait(f"gather_wait_{k}", over_ici[k], after))
        passed_on[k] = passing
        return own_slot(run_plan(f"gather_pass_{k}", passing), groups[k]) if pass_now else None

    def need_in0(env, last):
        env["in0"], = arrived(0, last, True)

    def need_layer0(env, last):
        env["pair0"], env["glu"] = arrived(1, last, True)

    def need_layer1(env, last):
        arrived(2, last, False)

    def layer1_weights(env, last):
        pair1, ukv, in1, uq, norms = own_slot(passed_on[2].results, groups[2])
        env.update(pair1=pair1, ukv=ukv, w_in1=mla_in_permute(shards_to_cols(in1)), w_uq=uq_permute(shards_to_cols(uq)),
                   q_lora_norm=norms[:, 0, :], kv_lora_norm=norms[:, 1, :64])

    hooks["before", "l0_in"], hooks["before", "l0_glu"], hooks["before", "l0_out"] = need_in0, need_layer0, need_layer1
    hooks["plans", "l0_out"], hooks["after", "l0_out"] = (lambda env: [passed_on[2]]), layer1_weights

    rs = {}

    def swap(k, gs):
        rs[k, "g"], rs[k, "swap"] = gs, plan_pair_exchange(gs)
        return rs[k, "swap"]

    def start_scatter(k):
        rs[k, "pairs"] = pair_adds(f"rs{k}", rs[k, "g"], rs[k, "swap"].results, place)
        rs[k, "scatter"] = plan_chip_scatter(rs[k, "pairs"])
        schedule_behind(split_start(f"rs{k}_scatter_start", [rs[k, "scatter"]]))

    def join(k, after):
        rs[k, "join"] = plan_pair_join(chip_adds(f"rs{k}", rs[k, "pairs"], split_wait(f"rs{k}_scatter_wait", rs[k, "scatter"], after), place))
        return rs[k, "join"]

    hooks["plans", "l1_rms_bwd"] = lambda env: [swap(1, [env["g_pair1"], env["g_ukv"], cols_to_shards(mla_in_unpermute(env["dw_in1"])).astype(BF16),
                                                          cols_to_shards(uq_unpermute(env["dw_uq"])).astype(BF16)])]
    hooks["after", "l1_rms_bwd"] = lambda env, last: start_scatter(1)
    hooks["plans", "l0_glu_dx"] = lambda env: [swap(0, [env["g_pair0"], env["g_glu"]])]

    def before_s5_backward(env, last):
        start_scatter(0)
        env["join1"] = join(1, last)

    hooks["before", "s5_backward"] = before_s5_backward
    hooks["plans", "s5_backward"] = lambda env: [env["join1"]]
    hooks["after", "s5_backward"] = lambda env, last: env.update(join0=join(0, last))
    hooks["plans", "l0_in_dx"] = lambda env: [swap(2, [env["g_in0"]]), env["join0"]]
    hooks["after", "l0_in_dx"] = lambda env, last: start_scatter(2)

    small = {n: wts[n] for n, _ in SMALL}
    loss, grad_x, g = device_step(x[0], mem[0], positions[0], loss_target[0], small, env, hooks)
    loss = lax.psum(loss[0, 0], MESH_AXES)
    (r_pair1, r_ukv, r_in1, r_uq), (r_pair0, r_glu) = (rs[k, "join"].results for k in (1, 0))
    grads = {"w_out": jnp.stack([r_pair0[:PAIR_MKV], r_pair1[:PAIR_MKV]]), "w_mem_kv": jnp.stack([r_pair0[PAIR_MKV:], r_pair1[PAIR_MKV:]]),
             "s5_w_glu": r_glu[None], "mla_w_ukv": r_ukv[None], "mla_w_in": r_in1[None], "mla_w_uq": r_uq[None]}
    delta, new_m, new_v = {}, {}, {}
    shapes = dict(BIG + SMALL)

    def update(n):
        s = shapes[n]
        perm = MINOR_LAST.get(n, tuple(range(len(s))))
        turned = tuple(s[p] for p in perm)
        view = lambda a: jnp.transpose(a, perm).reshape(-1, turned[-1])
        res = adamw("adamw_" + n, view(wts[n]), view(grads[n]), view(mom[n]), view(var[n]))
        delta[n], new_m[n], new_v[n] = (jnp.transpose(r.reshape(turned), tuple(perm.index(i) for i in range(len(s)))) for r in res)
        return delta[n]

    def beside(name, plan, work):
        schedule_behind(split_start(name + "_start", [plan]))
        return split_wait(name + "_wait", plan, [update(n) for n in work][-1])

    small_flat = jnp.concatenate([g[n].reshape(-1) for n, _ in SMALL_FULL])
    g_small = jnp.pad(small_flat, (0, 4 * SMALL_ROWS * SMALL_LANES - N_SMALL)).astype(BF16).reshape(4, SMALL_ROWS, SMALL_LANES)
    pairs = pair_adds("rs3", [g_small], beside("rs3_swap", plan_pair_exchange([g_small]), ["s5_w_glu"]), place)
    halves = chip_adds("rs3", pairs, beside("rs3_scatter", plan_chip_scatter(pairs), ["w_out", "mla_w_in"]), place)
    halves += chip_adds("rs2", rs[2, "pairs"], split_wait("rs2_scatter_wait", rs[2, "scatter"], pairs[0]), place)
    r_small, r_in0 = beside("rs23_join", plan_pair_join(halves), ["w_mem_kv", "mla_w_uq"])
    grads["s5_w_in"] = r_in0[None]
    landed = beside("small_gather", plan_gather_ici([r_small]), ["mla_w_ukv", "s5_w_in"])
    small_all = own_slot(run_plan("small_gather_pass", plan_gather_pass(landed)), [r_small])[0].reshape(-1)[:N_SMALL]

    off = 0
    for n, s in SMALL_FULL:
        grads[n] = small_all[off:off + math.prod(s)].reshape(s)
        off += math.prod(s)
    for n, s in SHARDED_SMALL:
        grads[n] = lax.dynamic_slice(grads[n], (0, chip * s[1]), s)
    for n, s in SMALL:
        if len(s) == 4:
            update(n)
    small_names = [n for n, s in SMALL if len(s) < 4] + [n for n, _ in SHARDED_SMALL]
    n_own = sum(wts[n].size for n in small_names)
    rows_own = -(-n_own // (8 * 128)) * 8

    def pack_small(d):
        flat = jnp.concatenate([d[n].reshape(-1) for n in small_names])
        return jnp.pad(flat, (0, rows_own * 128 - n_own), constant_values=1.0).reshape(rows_own, 128)

    res = adamw("adamw_small", pack_small(wts), pack_small(grads), pack_small(mom), pack_small(var))
    off = 0
    for n in small_names:
        size = wts[n].size
        delta[n], new_m[n], new_v[n] = (r.reshape(-1)[off:off + size].reshape(wts[n].shape) for r in res)
        off += size

    return (loss, grad_x[None], *[grads[n] for n in WEIGHT_ORDER], *[delta[n] for n in WEIGHT_ORDER],
            *[new_m[n] for n in WEIGHT_ORDER], *[new_v[n] for n in WEIGHT_ORDER])
```

```python
import functools
import math

import jax
import jax.numpy as jnp
from jax import lax
from jax.experimental import pallas as pl
from jax.experimental.pallas import tpu as pltpu

F32, BF16 = jnp.float32, jnp.bfloat16
SDS = jax.ShapeDtypeStruct

D = 1024
L = 2048
ML = 256
BW = 2 * D
XQW = BW // 4
PW = BW - XQW
XH, XHD = 4, 128
SG, SC, SP = 96, 16, 64
SN = SG * SP
NOPE, ROPE, VD = 128, 64, 128
MH = 12
QL, KVL = 512, 256
EPS = 1e-6
ROPE_THETA = 10000.0
MLA_IN = QL + KVL + ROPE + XQW + BW
MLA_IN_P = 3456
ADAM_LR, ADAM_B1, ADAM_B2, ADAM_EPS, ADAM_WD, ADAM_STEP = 0.001, 0.9, 0.999, 1e-08, 0.01, 10

VMEM_LIMIT = 48 * 2**20
SEG = 8
SEG_LEN = L // SEG
MESH_AXES = ("x", "y", "c")


def _cparams():
    return pltpu.CompilerParams(vmem_limit_bytes=VMEM_LIMIT)


def _dg(a, b, ca, cb):
    return lax.dot_general(a.astype(BF16), b.astype(BF16), (((ca,), (cb,)), ((), ())), preferred_element_type=F32)


@jax.custom_vjp
def mm_nn(a, b):
    return _dg(a, b, 1, 0)


mm_nn.defvjp(lambda a, b: (_dg(a, b, 1, 0), (a, b)), lambda res, g: (_dg(g, res[1], 1, 1), _dg(res[0], g, 0, 0)))


@jax.custom_vjp
def mm_nt(a, b):
    return _dg(a, b, 1, 1)


mm_nt.defvjp(lambda a, b: (_dg(a, b, 1, 1), (a, b)), lambda res, g: (_dg(g, res[1], 1, 0), _dg(g, res[0], 0, 0)))


@functools.partial(jax.custom_vjp, nondiff_argnums=(1,))
def lane_roll(x, shift):
    return pltpu.roll(x, shift, 1)


lane_roll.defvjp(lambda x, shift: (pltpu.roll(x, shift, 1), None),
                 lambda shift, _, g: (pltpu.roll(g, (128 - shift) % 128, 1),))


def rms(x, g):
    return x * lax.rsqrt(jnp.mean(x * x, axis=-1, keepdims=True) + EPS) * g


@jax.custom_vjp
def softmax_rows(s):
    e = jnp.exp(s - jnp.max(s, axis=-1, keepdims=True))
    return e / jnp.sum(e, axis=-1, keepdims=True)


def _softmax_rows_fwd(s):
    p = softmax_rows(s)
    return p, p


def _softmax_rows_bwd(p, g):
    return (p * (g - jnp.sum(g * p, axis=-1, keepdims=True)),)


softmax_rows.defvjp(_softmax_rows_fwd, _softmax_rows_bwd)


def silu(x):
    return x * jax.nn.sigmoid(x)


ANY = pl.BlockSpec(memory_space=pl.ANY)
MESH_ID = pl.DeviceIdType.MESH


def _dma_sems(n):
    return [pltpu.SemaphoreType.DMA((n,)), pltpu.SemaphoreType.DMA((n,))]


class Plan:
    def __init__(self, operands, out_shape, aliases, n_sems, copies):
        self.operands, self.out_shape, self.aliases, self.n_sems, self.copies = list(operands), list(out_shape), aliases, n_sems, copies
        self.results = None


_SCHEDULE_BEHIND = []


def schedule_behind(token):
    _SCHEDULE_BEHIND.append(token)


def hosted_call(kern, *, name, grid, in_specs, out_specs, out_shape, operands, scratch_shapes=(), aliases=None, cparams=None, plans=(), deps=()):
    n_in, n_out, n_scr = len(in_specs), len(out_specs), len(scratch_shapes)
    p_in, p_out = [len(p.operands) for p in plans], [len(p.out_shape) for p in plans]
    deps = tuple(deps) + tuple(_SCHEDULE_BEHIND)
    _SCHEDULE_BEHIND.clear()
    all_aliases = dict(aliases or {})
    in_off, out_off = n_in, n_out
    for p, ni, no in zip(plans, p_in, p_out):
        all_aliases.update({in_off + i: out_off + o for i, o in p.aliases.items()})
        in_off, out_off = in_off + ni, out_off + no

    def body(*refs):
        pos, pins, pouts = n_in, [], []
        for ni in p_in:
            pins.append(refs[pos:pos + ni])
            pos += ni
        pos += len(deps)
        main_out = refs[pos:pos + n_out]
        pos += n_out
        for no in p_out:
            pouts.append(refs[pos:pos + no])
            pos += no
        main_scr = refs[pos:pos + n_scr]
        pos += n_scr
        if plans:
            ids = [pl.program_id(ax) for ax in range(len(grid))]
            first = functools.reduce(jnp.logical_and, [i == 0 for i in ids])
            last = functools.reduce(jnp.logical_and, [i == g - 1 for i, g in zip(ids, grid)])
            copies = [p.copies(pins[k], pouts[k], refs[pos + 2 * k], refs[pos + 2 * k + 1]) for k, p in enumerate(plans)]

            @pl.when(first)
            def _():
                for sends, _ in copies:
                    for cp in sends:
                        cp.start()

        kern(*refs[:n_in], *main_out, *main_scr)
        if plans:
            @pl.when(last)
            def _():
                for sends, recvs in copies:
                    for cp in recvs:
                        cp.wait_recv()
                    for cp in sends:
                        cp.wait_send()

    res = pl.pallas_call(body, grid=grid, in_specs=list(in_specs) + [ANY] * (sum(p_in) + len(deps)),
                         out_specs=list(out_specs) + [ANY] * sum(p_out), out_shape=list(out_shape) + [s for p in plans for s in p.out_shape],
                         scratch_shapes=list(scratch_shapes) + [s for p in plans for s in _dma_sems(p.n_sems)],
                         input_output_aliases=all_aliases, name=name, compiler_params=cparams or _cparams())(
        *operands, *[a for p in plans for a in p.operands], *deps)
    pos = n_out
    for p, no in zip(plans, p_out):
        p.results = list(res[pos:pos + no])
        pos += no
    return list(res[:n_out])


def _wide(v):
    return v.astype(F32) if v.dtype == BF16 else v


def stage(name, fn, grid, ins, outs):
    n_in = len(ins)

    def kern(*refs):
        res = fn(*[_wide(r[...]) for r in refs[:n_in]])
        for r, v in zip(refs[n_in:], res):
            r[...] = v.astype(r.dtype)

    return hosted_call(kern, name=name, grid=grid, in_specs=[s for _, s in ins], out_specs=[s for _, s in outs],
                       out_shape=[sd for sd, _ in outs], operands=[a for a, _ in ins])


def stage_bwd(name, fn, grid, ins, cts, diffs, plans=()):
    n_in, n_ct = len(ins), len(cts)
    didx = [i for i, d in enumerate(diffs) if d is not None]
    opts = {i: (diffs[i][3] if len(diffs[i]) > 3 else {}) for i in didx if diffs[i][0] == "row"}
    adds = [(i, opts[i]["add"]) for i in opts if "add" in opts[i]]
    intos = [(i, opts[i]["into"]) for i in opts if "into" in opts[i]]
    n_add, n_into = len(adds), len(intos)
    add_pos = {i: n_in + n_ct + k for k, (i, _) in enumerate(adds)}
    n_extra = n_in + n_ct + n_add + n_into

    def kern(*refs):
        vals = [_wide(r[...]) for r in refs[:n_in]]

        def f(*dv):
            full = list(vals)
            for i, v in zip(didx, dv):
                full[i] = v
            return fn(*full)

        _, vjp = jax.vjp(f, *[vals[i].astype(F32) for i in didx])
        gs = vjp(tuple(c[...].astype(F32) for c in refs[n_in:n_in + n_ct]))
        for o_ref, i, g in zip(refs[n_extra:], didx, gs):
            if diffs[i][0] == "row":
                if i in add_pos:
                    g = g + refs[add_pos[i]][...].astype(F32)
                o_ref[...] = g.astype(o_ref.dtype)
            else:
                first = functools.reduce(jnp.logical_and, [pl.program_id(ax) == 0 for ax in diffs[i][1]])

                @pl.when(first)
                def _():
                    o_ref[...] = g

                @pl.when(jnp.logical_not(first))
                def _():
                    o_ref[...] += g

    out_shape, out_specs = [], []
    for i in didx:
        if diffs[i][0] == "row":
            out_shape.append(diffs[i][1])
            out_specs.append(diffs[i][2])
        else:
            out_shape.append(SDS(ins[i][0].shape, F32))
            out_specs.append(ins[i][1])
    aliases = {n_in + n_ct + n_add + k: didx.index(i) for k, (i, _) in enumerate(intos)}
    in_specs = [s for _, s in ins] + [s for _, s in cts] + [s for _, (_, s) in adds] + [ANY] * n_into
    operands = [a for a, _ in ins] + [a for a, _ in cts] + [a for _, (a, _) in adds] + [a for _, a in intos]
    return hosted_call(kern, name=name, grid=grid, in_specs=in_specs, out_specs=out_specs, out_shape=out_shape, operands=operands,
                       aliases=aliases, plans=plans)


def rspec(tl, w, cb=0):
    return pl.BlockSpec((tl, w), lambda i: (i, cb))


def cspec(shape):
    return pl.BlockSpec(shape, lambda i: (0,) * len(shape))


def row_fwd(name, fn, rows, tl, row_ins, consts, outs):
    ins = [(a, rspec(tl, w, cb)) for a, w, cb in row_ins] + [(a, cspec(a.shape)) for a in consts]
    return stage(name, fn, (rows // tl,), ins, [(SDS((rows, w), dt), rspec(tl, w)) for w, dt in outs])


def row_bwd(name, fn, rows, tl, row_ins, consts, cts, row_diff, const_diff, plans=()):
    ins = [(a, rspec(tl, w, cb)) for a, w, cb in row_ins] + [(a, cspec(a.shape)) for a in consts]
    diffs = []
    for (a, w, cb), d in zip(row_ins, row_diff):
        if not d:
            diffs.append(None)
            continue
        d = d if isinstance(d, dict) else {}
        opts = {}
        if "add" in d:
            opts["add"] = (d["add"][0], rspec(tl, d["add"][1], d["add"][2]))
        if d.get("into") is not None:
            opts["into"] = d["into"]
        diffs.append(("row", SDS((rows, d.get("cols", w)), d.get("dtype", F32)), rspec(tl, w, d.get("cb", 0)), opts))
    diffs += [("acc", (0,)) if d else None for d in const_diff]
    return stage_bwd(name, fn, (rows // tl,), ins, [(a, rspec(tl, w, cb)) for a, w, cb in cts], diffs, plans=plans)


MATMUL_VMEM = 36 * 2**20
ADAMW_VMEM = 28 * 2**20


class Sharded:
    def __init__(self, arr, kind, roff, rows):
        self.arr, self.kind, self.roff, self.rows, self.n = arr, kind, roff, rows, arr.shape[2]
        self.shape = (rows, 4 * self.n) if kind == "col" else (4 * rows, self.n)

    def fits(self, t0, t1):
        return self.roff % t0 == 0 and self.rows % t0 == 0 and self.n % t1 == 0

    def spec(self, t0, t1, bidx):
        assert self.fits(t0, t1), (self.kind, self.roff, self.rows, self.n, t0, t1)
        r0 = self.roff // t0
        if self.kind == "col":
            per = self.n // t1
            return pl.BlockSpec((None, t0, t1), lambda *g: (bidx(*g)[1] // per, r0 + bidx(*g)[0], bidx(*g)[1] % per))
        per = self.rows // t0
        return pl.BlockSpec((None, t0, t1), lambda *g: (bidx(*g)[0] // per, r0 + bidx(*g)[0] % per, bidx(*g)[1]))


def matmul(name, a, b, mode, out_dtype=BF16, add=None, out=None, into=None, plans=()):
    if mode == "tn":
        k_dim, m = a.shape
    else:
        m, k_dim = a.shape
    n = b.shape[0] if mode == "nt" else b.shape[1]
    b_fit = b.fits if isinstance(b, Sharded) else (lambda t0, t1: True)
    o_fit = out.fits if out is not None else (lambda t0, t1: True)
    a_bytes, b_bytes = jnp.dtype(a.dtype).itemsize, jnp.dtype(b.arr.dtype if isinstance(b, Sharded) else b.dtype).itemsize
    o_bytes = jnp.dtype(out_dtype if out is None else out.arr.dtype).itemsize

    def vmem(tm, tn, tk):
        return 2 * (tm * tk * a_bytes + tk * tn * b_bytes + tm * tn * (o_bytes + (4 if add is not None else 0))) + 4 * tm * tn * (1 + (tk < k_dim))

    tiles = [(tm, tn, tk) for tm in (2048, 1024, 512, 256, 128) for tn in (1024, 768, 512, 384, 256, 128)
             for tk in sorted({k_dim, 1024, 768, 512, 384, 256, 128})
             if m % tm == 0 and n % tn == 0 and k_dim % tk == 0 and (b_fit(tn, tk) if mode == "nt" else b_fit(tk, tn)) and o_fit(tm, tn)
             and vmem(tm, tn, tk) <= MATMUL_VMEM]
    tm, tn, tk = max(tiles, key=lambda t: (t[2] == k_dim, t[0] * t[1] * t[2], t[0] * t[1]))
    nk = k_dim // tk
    a_spec = pl.BlockSpec((tk, tm), lambda i, j, k: (k, i)) if mode == "tn" else pl.BlockSpec((tm, tk), lambda i, j, k: (i, k))
    if isinstance(b, Sharded):
        b_spec = b.spec(tn, tk, lambda i, j, k: (j, k)) if mode == "nt" else b.spec(tk, tn, lambda i, j, k: (k, j))
        b = b.arr
    else:
        b_spec = pl.BlockSpec((tn, tk), lambda i, j, k: (j, k)) if mode == "nt" else pl.BlockSpec((tk, tn), lambda i, j, k: (k, j))
    o_spec = pl.BlockSpec((tm, tn), lambda i, j, k: (i, j))
    out_spec, out_shape = (o_spec, SDS((m, n), out_dtype)) if out is None else (out.spec(tm, tn, lambda i, j, k: (i, j)), out.arr)
    ca, cb = {"nn": (1, 0), "nt": (1, 1), "tn": (0, 0)}[mode]
    n_in = 2 + (add is not None)

    def finish(refs, o_ref, r):
        if add is not None:
            r = r + refs[2][...]
        o_ref[...] = r.astype(o_ref.dtype)

    def kern_whole(*refs):
        finish(refs, refs[-1], _dg(refs[0][...], refs[1][...], ca, cb))

    def kern_cut(*refs):
        o_ref, acc = refs[-2], refs[-1]
        k = pl.program_id(2)

        @pl.when(k == 0)
        def _():
            acc[...] = jnp.zeros_like(acc)

        acc[...] += _dg(refs[0][...], refs[1][...], ca, cb)

        @pl.when(k == nk - 1)
        def _():
            finish(refs, o_ref, acc[...])

    ins, specs = [a, b], [a_spec, b_spec]
    if add is not None:
        ins.append(add)
        specs.append(o_spec)
    if into is not None:
        ins.append(into)
        specs.append(ANY)
    return hosted_call(kern_whole if nk == 1 else kern_cut, name=name, grid=(m // tm, n // tn, nk), in_specs=specs, out_specs=[out_spec],
                       out_shape=[out_shape], operands=ins, scratch_shapes=[] if nk == 1 else [pltpu.VMEM((tm, tn), F32)],
                       aliases={} if into is None else {n_in: 0}, plans=plans)[0]


SCAN_UNROLL = 8


def _cmul(ar, ai, br, bi):
    return ar * br - ai * bi, ar * bi + ai * br


def _sub_shift(x, down):
    row = lax.broadcasted_iota(jnp.int32, x.shape, 0)
    if down:
        return jnp.where(row == 0, 0.0, pltpu.roll(x, 1, 0))
    return jnp.where(row == SEG - 1, 0.0, pltpu.roll(x, SEG - 1, 0))


def _pow_seg_len(ar, ai):
    for _ in range(int(math.log2(SEG_LEN))):
        ar, ai = _cmul(ar, ai, ar, ai)
    return ar, ai


def _scan_in_place(sr, si, a_re, a_im):
    lanes = sr.shape[1]
    ar = jnp.broadcast_to(a_re, (SEG, lanes))
    ai = jnp.broadcast_to(a_im, (SEG, lanes))
    zero = jnp.zeros((SEG, lanes), F32)

    def local(i, carry):
        rows = pl.ds(pl.multiple_of(i * SEG, SEG), SEG)
        mr, mi = _cmul(ar, ai, carry[0], carry[1])
        nr, ni = mr + sr[rows, :], mi + si[rows, :]
        sr[rows, :] = nr
        si[rows, :] = ni
        return nr, ni

    fr, fi = lax.fori_loop(0, SEG_LEN, local, (zero, zero), unroll=SCAN_UNROLL)
    pr, pi = _pow_seg_len(ar, ai)
    ir, ii = zero, zero
    for _ in range(SEG - 1):
        mr, mi = _cmul(pr, pi, ir, ii)
        ir, ii = _sub_shift(mr + fr, True), _sub_shift(mi + fi, True)

    def carry_in(i, pw):
        rows = pl.ds(pl.multiple_of(i * SEG, SEG), SEG)
        cr, ci = _cmul(pw[0], pw[1], ir, ii)
        sr[rows, :] += cr
        si[rows, :] += ci
        return _cmul(pw[0], pw[1], ar, ai)

    lax.fori_loop(0, SEG_LEN, carry_in, (ar, ai), unroll=SCAN_UNROLL)


S5_LANES = 8 * SP
S5_BLOCKS = SN // S5_LANES


def _s5_specs():
    u_spec = pl.BlockSpec((L, 8 * SC), lambda j: (0, j))
    s_spec = pl.BlockSpec((L, S5_LANES), lambda j: (0, j))
    wb_spec = pl.BlockSpec((S5_LANES, 8 * SC), lambda j: (j, 0))
    wc_spec = pl.BlockSpec((8 * SC, S5_LANES), lambda j: (j, 0))
    a_spec = pl.BlockSpec((1, S5_LANES), lambda j: (0, j))
    d_spec = pl.BlockSpec((1, 8 * SC), lambda j: (0, j))
    return u_spec, s_spec, wb_spec, wc_spec, a_spec, d_spec


def s5_forward(proj, wb_re, wb_im, wc_re, wc_im, a_re, a_im, d, plans=()):
    def kern(u_ref, wbr, wbi, wcr, wci, ar, ai, d_ref, sr_out, si_out, g_ref, sr, si):
        u = _wide(u_ref[...])
        sr[...], si[...] = fn_s5_bu(u, wbr[...], wbi[...])
        _scan_in_place(sr, si, ar[...], ai[...])
        g_ref[...] = fn_s5_out(sr[...], si[...], u, d_ref[...], wcr[...], wci[...])[0].astype(g_ref.dtype)
        sr_out[...] = sr[...].astype(sr_out.dtype)
        si_out[...] = si[...].astype(si_out.dtype)

    u_spec, s_spec, wb_spec, wc_spec, a_spec, d_spec = _s5_specs()
    return hosted_call(kern, name="s5_forward", grid=(S5_BLOCKS,), in_specs=[u_spec, wb_spec, wb_spec, wc_spec, wc_spec, a_spec, a_spec, d_spec],
                       out_specs=[s_spec, s_spec, u_spec], out_shape=[SDS((L, SN), BF16)] * 2 + [SDS((L, PW), BF16)],
                       operands=[proj, wb_re, wb_im, wc_re, wc_im, a_re, a_im, d], scratch_shapes=[pltpu.VMEM((L, S5_LANES), F32)] * 2,
                       plans=plans)


def _adjoint_scan_in_place(lr, li, sr, si, a_re, a_im):
    lanes = lr.shape[1]
    ar = jnp.broadcast_to(a_re, (SEG, lanes))
    ai = -jnp.broadcast_to(a_im, (SEG, lanes))
    zero = jnp.zeros((SEG, lanes), F32)

    def local(k, carry):
        i = SEG_LEN - 1 - k
        rows = pl.ds(pl.multiple_of(i * SEG, SEG), SEG)
        mr, mi = _cmul(ar, ai, carry[0], carry[1])
        nr, ni = mr + lr[rows, :], mi + li[rows, :]
        lr[rows, :] = nr
        li[rows, :] = ni
        return nr, ni

    fr, fi = lax.fori_loop(0, SEG_LEN, local, (zero, zero), unroll=SCAN_UNROLL)
    pr, pi = _pow_seg_len(ar, ai)
    ir, ii = zero, zero
    for _ in range(SEG - 1):
        mr, mi = _cmul(pr, pi, ir, ii)
        ir, ii = _sub_shift(mr + fr, False), _sub_shift(mi + fi, False)

    def fix(rows, pw):
        cr, ci = _cmul(pw[0], pw[1], ir, ii)
        tr, ti = lr[rows, :] + cr, li[rows, :] + ci
        lr[rows, :] = tr
        li[rows, :] = ti
        return tr, ti

    def grad_a(tr, ti, spr, spi, acc):
        return acc[0] + tr * spr + ti * spi, acc[1] + ti * spr - tr * spi

    def carry_in(k, c):
        i = SEG_LEN - 1 - k
        rows = pl.ds(pl.multiple_of(i * SEG, SEG), SEG)
        prev = pl.ds(pl.multiple_of((i - 1) * SEG, SEG), SEG)
        tr, ti = fix(rows, (c[0], c[1]))
        acc = grad_a(tr, ti, sr[prev, :], si[prev, :], (c[2], c[3]))
        nr, ni = _cmul(c[0], c[1], ar, ai)
        return nr, ni, acc[0], acc[1]

    pwr, pwi, accr, acci = lax.fori_loop(0, SEG_LEN - 1, carry_in, (ar, ai, zero, zero), unroll=5)
    tr, ti = fix(pl.ds(0, SEG), (pwr, pwi))
    last = pl.ds((SEG_LEN - 1) * SEG, SEG)
    accr, acci = grad_a(tr, ti, _sub_shift(sr[last, :], True), _sub_shift(si[last, :], True), (accr, acci))
    return jnp.sum(accr, axis=0, keepdims=True), jnp.sum(acci, axis=0, keepdims=True)


S5_BWD_VMEM = 58 * 2**20


def s5_backward(dg, proj, s_re, s_im, wb_re, wb_im, wc_re, wc_im, a_re, a_im, d, dproj, plans=()):
    def kern(dg_ref, u_ref, sr_in, si_in, wbr, wbi, wcr, wci, ar, ai, d_ref, _, du_ref, dd_ref, dwcr, dwci, dwbr, dwbi, dar, dai, lr, li, sr, si):
        u = _wide(u_ref[...])
        sr[...], si[...] = _wide(sr_in[...]), _wide(si_in[...])
        _, vjp_out = jax.vjp(fn_s5_out, sr[...], si[...], u, d_ref[...], wcr[...], wci[...])
        lr[...], li[...], du_out, dd_ref[...], dwcr[...], dwci[...] = vjp_out((_wide(dg_ref[...]),))
        dar[...], dai[...] = _adjoint_scan_in_place(lr, li, sr, si, ar[...], ai[...])
        _, vjp_in = jax.vjp(fn_s5_bu, u, wbr[...], wbi[...])
        du_in, dwbr[...], dwbi[...] = vjp_in((lr[...], li[...]))
        du_ref[...] = (du_out + du_in).astype(du_ref.dtype)

    u_spec, s_spec, wb_spec, wc_spec, a_spec, d_spec = _s5_specs()
    outs = [(SDS(dproj.shape, dproj.dtype), u_spec), (SDS(d.shape, F32), d_spec), (SDS(wc_re.shape, F32), wc_spec), (SDS(wc_im.shape, F32), wc_spec),
            (SDS(wb_re.shape, F32), wb_spec), (SDS(wb_im.shape, F32), wb_spec), (SDS(a_re.shape, F32), a_spec), (SDS(a_im.shape, F32), a_spec)]
    return hosted_call(kern, name="s5_backward", grid=(S5_BLOCKS,),
                       in_specs=[u_spec, u_spec, s_spec, s_spec, wb_spec, wb_spec, wc_spec, wc_spec, a_spec, a_spec, d_spec, ANY],
                       out_specs=[sp for _, sp in outs], out_shape=[sd for sd, _ in outs], aliases={11: 0},
                       operands=[dg, proj, s_re, s_im, wb_re, wb_im, wc_re, wc_im, a_re, a_im, d, dproj],
                       scratch_shapes=[pltpu.VMEM((L, S5_LANES), F32)] * 4,
                       cparams=pltpu.CompilerParams(vmem_limit_bytes=S5_BWD_VMEM), plans=plans)


def fn_rms(x, g):
    return (rms(x, g),)


def fn_s5_disc(lre, lim, ls):
    step = jnp.exp(ls)
    e = jnp.exp(lre * step)
    a_re, a_im = e * jnp.cos(lim * step), e * jnp.sin(lim * step)
    den = lre * lre + lim * lim
    nr, ni = a_re - 1.0, a_im
    return a_re, a_im, (nr * lre + ni * lim) / den, (ni * lre - nr * lim) / den


def _group_mask(rows, cols, row_div, col_div):
    r = lax.broadcasted_iota(jnp.int32, (rows, cols), 0) // row_div % 8
    c = lax.broadcasted_iota(jnp.int32, (rows, cols), 1) // col_div
    return r == c


def _spread(x, mask):
    w = x.shape[1]
    copy = (lax.broadcasted_iota(jnp.int32, (w, 8 * w), 1) % w == lax.broadcasted_iota(jnp.int32, (w, 8 * w), 0)).astype(F32)
    return jnp.where(mask, jnp.dot(x, copy, precision=lax.Precision.HIGHEST, preferred_element_type=F32), 0.0)


def fn_s5_bmat(b_re, b_im, coef_re, coef_im):
    mask = _group_mask(b_re.shape[0], 8 * SC, SP, SC)
    return _spread(coef_re * b_re - coef_im * b_im, mask), _spread(coef_re * b_im + coef_im * b_re, mask)


def fn_s5_cmat(c_re, c_im):
    mask = _group_mask(c_re.shape[0], 8 * SP, SC, SP)
    return _spread(c_re, mask), _spread(c_im, mask)


def fn_s5_bu(u, wb_re, wb_im):
    return mm_nt(u, wb_re), mm_nt(u, wb_im)


def fn_s5_out(sr, si, u, d, wc_re, wc_im):
    y = mm_nt(sr, wc_re) - mm_nt(si, wc_im) + d * u
    return (jax.nn.gelu(y),)


def fn_merge_glu(z, mo, gate):
    yg = z[:, :PW] * jax.nn.sigmoid(z[:, PW:])
    return (jnp.concatenate([yg, mo], axis=1) * silu(gate),)


def fn_merge(prim, mo, gate):
    return (jnp.concatenate([prim, mo], axis=1) * silu(gate),)


def fn_mem_k(kv, g):
    return (jnp.concatenate([rms(kv[:, h * XHD:(h + 1) * XHD], g) for h in range(XH)], axis=1),)


def fn_mem_attn(xq, kn, v, g):
    outs = []
    for h in range(XH):
        sl = slice(h * XHD, (h + 1) * XHD)
        p = softmax_rows(mm_nt(rms(xq[:, sl], g), kn[:, sl]) * (XHD ** -0.5))
        outs.append(mm_nn(p, v[:, sl]))
    return (jnp.concatenate(outs, axis=1),)


def _half_rms(x, g):
    lo = lax.broadcasted_iota(jnp.int32, x.shape, 1) < ROPE
    x2 = x * x
    s_lo = jnp.sum(jnp.where(lo, x2, 0.0), axis=1, keepdims=True)
    s_hi = jnp.sum(jnp.where(lo, 0.0, x2), axis=1, keepdims=True)
    return x * lax.rsqrt(jnp.where(lo, s_lo, s_hi) / ROPE + EPS) * g


def _rope(x, cos2, sin_signed):
    first = lax.broadcasted_iota(jnp.int32, x.shape, 1) % ROPE < ROPE // 2
    return x * cos2 + jnp.where(first, lane_roll(x, 128 - ROPE // 2), lane_roll(x, ROPE // 2)) * sin_signed


def fn_mla_prep(q, kv, kr, cos2, sin_signed, qnn, knn, qrn, krn):
    lo = lax.broadcasted_iota(jnp.int32, kr.shape, 1) < ROPE
    kr_pad = jnp.where(lo, _rope(_half_rms(kr, krn), cos2, sin_signed), 0.0)
    qf, kf, vs = [], [], []
    for m in range(MH // 2):
        pair = _rope(_half_rms(q[:, MH * NOPE + 128 * m:MH * NOPE + 128 * (m + 1)], qrn), cos2, sin_signed)
        for h, rope_h in ((2 * m, pair), (2 * m + 1, lane_roll(pair, ROPE))):
            qf.append(jnp.concatenate([rms(q[:, NOPE * h:NOPE * (h + 1)], qnn), jnp.where(lo, rope_h, 0.0)], axis=1))
    for h in range(MH):
        kf.append(jnp.concatenate([rms(kv[:, 256 * h:256 * h + NOPE], knn), kr_pad], axis=1))
        vs.append(kv[:, 256 * h + NOPE:256 * (h + 1)])
    return jnp.stack(qf), jnp.stack(kf), jnp.stack(vs)


ATT_TQ = 512


def _attn_scores(q, kf):
    tq = q.shape[0]
    scale = (NOPE + ROPE) ** -0.5
    own = _dg(q, kf[-tq:], 1, 1) * scale
    own = jnp.where(lax.broadcasted_iota(jnp.int32, own.shape, 1) <= lax.broadcasted_iota(jnp.int32, own.shape, 0), own, jnp.finfo(F32).min)
    return own if kf.shape[0] == tq else jnp.concatenate([_dg(q, kf[:-tq], 1, 1) * scale, own], axis=1)


def _attn_specs():
    q_spec = pl.BlockSpec((None, ATT_TQ, 256), lambda h, i: (h, i, 0))
    k_spec = pl.BlockSpec((None, L, 256), lambda h, i: (h, 0, 0))
    v_spec = pl.BlockSpec((None, L, 128), lambda h, i: (h, 0, 0))
    o_spec = pl.BlockSpec((ATT_TQ, 128), lambda h, i: (i, h))
    lse_spec = pl.BlockSpec((None, ATT_TQ, 1), lambda h, i: (h, i, 0))
    return q_spec, k_spec, v_spec, o_spec, lse_spec


def causal_attn(qf, kf, vh):
    n_tiles = L // ATT_TQ

    def kern(q_ref, k_ref, v_ref, o_ref, lse_ref):
        i = pl.program_id(1)
        for t in range(n_tiles):
            @pl.when(i == t)
            def _(t=t):
                keys = (t + 1) * ATT_TQ
                s = _attn_scores(q_ref[...], k_ref[:keys, :])
                m = jnp.max(s, axis=-1, keepdims=True)
                e = jnp.exp(s - m)
                total = jnp.sum(e, axis=-1, keepdims=True)
                o_ref[...] = (_dg(e, v_ref[:keys, :], 1, 0) / total).astype(o_ref.dtype)
                lse_ref[...] = m + jnp.log(total)

    q_spec, k_spec, v_spec, o_spec, lse_spec = _attn_specs()
    return pl.pallas_call(kern, grid=(MH, n_tiles), in_specs=[q_spec, k_spec, v_spec], out_specs=[o_spec, lse_spec],
                          out_shape=[SDS((L, MH * VD), F32), SDS((MH, L, 1), F32)], name="l1_attn", compiler_params=_cparams())(qf, kf, vh)


def causal_attn_bwd(qf, kf, vh, out, lse, dout):
    n_tiles = L // ATT_TQ
    scale = (NOPE + ROPE) ** -0.5

    def kern(q_ref, k_ref, v_ref, o_ref, lse_ref, do_ref, dq_ref, dk_ref, dv_ref):
        i = pl.program_id(1)

        @pl.when(i == 0)
        def _():
            dk_ref[...] = jnp.zeros_like(dk_ref)
            dv_ref[...] = jnp.zeros_like(dv_ref)

        for t in range(n_tiles):
            @pl.when(i == t)
            def _(t=t):
                keys = (t + 1) * ATT_TQ
                q, k, v, do = q_ref[...], k_ref[:keys, :], v_ref[:keys, :], do_ref[...]
                p = jnp.exp(_attn_scores(q, k) - lse_ref[...])
                delta = jnp.sum(do * _wide(o_ref[...]), axis=-1, keepdims=True)
                dv_ref[:keys, :] += _dg(p, do, 0, 0)
                ds = p * (_dg(do, v, 1, 1) - delta) * scale
                dq_ref[...] = _dg(ds, k, 1, 0)
                dk_ref[:keys, :] += _dg(ds, q, 0, 0)

    q_spec, k_spec, v_spec, o_spec, lse_spec = _attn_specs()
    return pl.pallas_call(kern, grid=(MH, n_tiles), in_specs=[q_spec, k_spec, v_spec, o_spec, lse_spec, o_spec],
                          out_specs=[q_spec, k_spec, v_spec], out_shape=[SDS(qf.shape, F32), SDS(kf.shape, F32), SDS(vh.shape, F32)],
                          name="l1_attn_bwd", compiler_params=_cparams())(qf, kf, vh, out, lse, dout)


def loss_and_grad(y, target, tl=256):
    def kern(y_ref, t_ref, dy_ref, loss_ref):
        d = y_ref[...] - t_ref[...]
        dy_ref[...] = d / D

        @pl.when(pl.program_id(0) == 0)
        def _():
            loss_ref[...] = jnp.zeros_like(loss_ref)

        loss_ref[...] += 0.5 * jnp.sum(jnp.sum(d * d, axis=1, keepdims=True), axis=0, keepdims=True) / D

    return pl.pallas_call(kern, grid=(L // tl,), in_specs=[rspec(tl, D), rspec(tl, D)], out_specs=[rspec(tl, D), cspec((1, 1))],
                          out_shape=[SDS((L, D), F32), SDS((1, 1), F32)], name="loss", compiler_params=_cparams())(y, target)


def adamw(name, w, g, m, v):
    rows, cols = w.shape
    block_row_bytes = 7 * 2 * 4 * max(cols, 128)
    tr = _row_tile(rows, min(2048, ADAMW_VMEM // block_row_bytes // 8 * 8), 8)

    def kern(w_ref, g_ref, m_ref, v_ref, d_ref, nm_ref, nv_ref):
        gg = g_ref[...]
        nm = ADAM_B1 * m_ref[...] + (1.0 - ADAM_B1) * gg
        nv = ADAM_B2 * v_ref[...] + (1.0 - ADAM_B2) * jnp.square(gg)
        m_hat = nm / (1.0 - ADAM_B1 ** ADAM_STEP)
        v_hat = nv / (1.0 - ADAM_B2 ** ADAM_STEP)
        d_ref[...] = -ADAM_LR * (m_hat / (jnp.sqrt(v_hat) + ADAM_EPS) + ADAM_WD * w_ref[...])
        nm_ref[...] = nm
        nv_ref[...] = nv

    spec = rspec(tr, cols)
    return hosted_call(kern, name=name, grid=(rows // tr,), in_specs=[spec] * 4, out_specs=[spec] * 3,
                       out_shape=[SDS((rows, cols), F32)] * 3, operands=[w, g, m, v])


def _row_tile(rows, cap=512, unit=16):
    return max(t for t in range(unit, cap + 1, unit) if rows % t == 0)


def _place():
    x, y, c = lax.axis_index("x"), lax.axis_index("y"), lax.axis_index("c")
    return x, y, c, [(1 - x, y), (x, 1 - y), (1 - x, 1 - y)]


def _row_chunks(rows, n, dtype):
    unit = 32 // jnp.dtype(dtype).itemsize
    base, extra = divmod(rows // unit, n)
    out, start = [], 0
    for k in range(n):
        size = (base + (k < extra)) * unit
        if size:
            out.append((start, size))
            start += size
    assert start == rows, (rows, unit)
    return out


PIECE_BYTES = 1 << 20


def _pieces(shapes_dtypes, rows_of):
    out = []
    for b, (shape, dtype) in enumerate(shapes_dtypes):
        rows = rows_of(shape)
        n = max(1, min(4, rows * shape[-1] * jnp.dtype(dtype).itemsize // PIECE_BYTES))
        out += [(b, st, sz) for st, sz in _row_chunks(rows, n, dtype)]
    return out


def all_gather_chips(name, shards):
    nb = len(shards)
    pieces = _pieces([(s.shape, s.dtype) for s in shards], lambda shape: shape[0] // 2)
    n = len(pieces)

    def body(*refs):
        x_refs, out_refs, send_sems, recv_sems = refs[:nb], refs[nb:2 * nb], refs[2 * nb], refs[2 * nb + 1]
        x, y, c, chips = _place()
        sibling = (x, y, 1 - c)
        mine = 2 * x + y

        def copy(sem, chip, cc, k, to, from_input=False):
            b, st, sz = pieces[k]
            rows_k = pl.ds(cc * (x_refs[b].shape[0] // 2) + st, sz)
            dst = out_refs[b].at[chip, rows_k, :]
            return pltpu.make_async_remote_copy(src_ref=x_refs[b].at[rows_k, :] if from_input else dst, dst_ref=dst,
                                                send_sem=send_sems.at[sem], recv_sem=recv_sems.at[sem], device_id=to, device_id_type=MESH_ID)

        order = [(k, j, 2 * cx + cy, (cx, cy, c)) for k in range(n) for j, (cx, cy) in enumerate(chips)]
        first = [copy(j * n + k, mine, c, k, to, from_input=True) for k, j, _, to in order]
        for cp in first:
            cp.start()
        passed = []
        for k, j, chip, _ in order:
            copy(j * n + k, chip, c, k, sibling).wait_recv()
            passed.append(copy((3 + j) * n + k, chip, c, k, sibling))
            passed[-1].start()
        for k, j, chip, _ in order:
            copy((3 + j) * n + k, chip, 1 - c, k, sibling).wait_recv()
        for cp in first + passed:
            cp.wait_send()

    return pl.pallas_call(body, in_specs=[ANY] * nb, out_specs=[ANY] * nb, out_shape=[SDS((4,) + s.shape, s.dtype) for s in shards],
                          scratch_shapes=_dma_sems(6 * n), name=name)(*shards)


def plan_gather_ici(shards):
    pieces = _pieces([(s.shape, s.dtype) for s in shards], lambda shape: shape[0] // 2)
    n = len(pieces)

    def copies(x_refs, out_refs, send_sems, recv_sems):
        x, y, c, chips = _place()
        mine = 2 * x + y

        def copy(j, k, chip, to, from_input):
            b, st, sz = pieces[k]
            rows_k = pl.ds(c * (x_refs[b].shape[0] // 2) + st, sz)
            dst = out_refs[b].at[chip, rows_k, :]
            return pltpu.make_async_remote_copy(src_ref=x_refs[b].at[rows_k, :] if from_input else dst, dst_ref=dst, send_sem=send_sems.at[j * n + k],
                                                recv_sem=recv_sems.at[j * n + k], device_id=to, device_id_type=MESH_ID)

        order = [(k, j, 2 * cx + cy, (cx, cy, c)) for k in range(n) for j, (cx, cy) in enumerate(chips)]
        return [copy(j, k, mine, to, True) for k, j, _, to in order], [copy(j, k, chip, to, False) for k, j, chip, to in order]

    return Plan(shards, [SDS((4,) + s.shape, s.dtype) for s in shards], {}, 3 * n, copies)


def plan_gather_pass(gathered):
    pieces = _pieces([(g.shape[1:], g.dtype) for g in gathered], lambda shape: shape[0] // 2)
    n = len(pieces)

    def copies(_, out_refs, send_sems, recv_sems):
        x, y, c, chips = _place()

        def copy(j, k, chip, cc):
            b, st, sz = pieces[k]
            rows_k = out_refs[b].at[chip, pl.ds(cc * (out_refs[b].shape[1] // 2) + st, sz), :]
            return pltpu.make_async_remote_copy(src_ref=rows_k, dst_ref=rows_k, send_sem=send_sems.at[j * n + k], recv_sem=recv_sems.at[j * n + k],
                                                device_id=(x, y, 1 - c), device_id_type=MESH_ID)

        order = [(k, j, 2 * cx + cy) for k in range(n) for j, (cx, cy) in enumerate(chips)]
        return [copy(j, k, chip, c) for k, j, chip in order], [copy(j, k, chip, 1 - c) for k, j, chip in order]

    return Plan(gathered, [SDS(g.shape, g.dtype) for g in gathered], {i: i for i in range(len(gathered))}, 3 * n, copies)


def plan_pair_exchange(gs):
    pieces = _pieces([(g.shape, g.dtype) for g in gs], lambda shape: shape[1] // 2)

    def copies(g_refs, got_refs, send_sems, recv_sems):
        x, y, c, _ = _place()
        swaps = [pltpu.make_async_remote_copy(src_ref=g_refs[b].at[:, pl.ds((1 - c) * (g_refs[b].shape[1] // 2) + st, sz), :],
                                              dst_ref=got_refs[b].at[:, pl.ds(st, sz), :], send_sem=send_sems.at[k], recv_sem=recv_sems.at[k],
                                              device_id=(x, y, 1 - c), device_id_type=MESH_ID)
                 for k, (b, st, sz) in enumerate(pieces)]
        return swaps, swaps

    return Plan(gs, [SDS((g.shape[0], g.shape[1] // 2, g.shape[2]), g.dtype) for g in gs], {}, len(pieces), copies)


def plan_chip_scatter(ps):
    pieces = _pieces([(p.shape, p.dtype) for p in ps], lambda shape: shape[1])
    n = len(pieces)

    def copies(p_refs, q_refs, send_sems, recv_sems):
        x, y, c, chips = _place()
        mine = 2 * x + y

        def copy(j, k, src_slot, dst_slot, to):
            b, st, sz = pieces[k]
            return pltpu.make_async_remote_copy(src_ref=p_refs[b].at[src_slot, pl.ds(st, sz), :], dst_ref=q_refs[b].at[dst_slot, pl.ds(st, sz), :],
                                                send_sem=send_sems.at[j * n + k], recv_sem=recv_sems.at[j * n + k], device_id=to,
                                                device_id_type=MESH_ID)

        order = [(k, j, 2 * cx + cy, (cx, cy, c)) for k in range(n) for j, (cx, cy) in enumerate(chips)]
        return [copy(j, k, chip, mine, to) for k, j, chip, to in order], [copy(j, k, mine, chip, to) for k, j, chip, to in order]

    return Plan(ps, [SDS(p.shape, p.dtype) for p in ps], {}, 3 * n, copies)


def plan_pair_join(bufs):
    pieces = _pieces([(b.shape, b.dtype) for b in bufs], lambda shape: shape[0] // 2)

    def copies(_, out_refs, send_sems, recv_sems):
        x, y, c, _ = _place()

        def copy(k, cc):
            b, st, sz = pieces[k]
            rows_k = out_refs[b].at[pl.ds(cc * (out_refs[b].shape[0] // 2) + st, sz), :]
            return pltpu.make_async_remote_copy(src_ref=rows_k, dst_ref=rows_k, send_sem=send_sems.at[k], recv_sem=recv_sems.at[k],
                                                device_id=(x, y, 1 - c), device_id_type=MESH_ID)

        return [copy(k, c) for k in range(len(pieces))], [copy(k, 1 - c) for k in range(len(pieces))]

    return Plan(bufs, [SDS(b.shape, b.dtype) for b in bufs], {i: i for i in range(len(bufs))}, len(pieces), copies)


def run_plan(name, plan):
    hosted_call(lambda: None, name=name, grid=(1,), in_specs=[], out_specs=[], out_shape=[], operands=[], plans=[plan])
    return plan.results


HBM = pl.BlockSpec(memory_space=pltpu.HBM)
SEMS = pl.BlockSpec(memory_space=pltpu.SEMAPHORE)
SPLIT_PARAMS = dict(has_side_effects=pltpu.SideEffectType.DATAFLOW_SIDE_EFFECTING)


def _plan_buffers(plan):
    in_place = {o: i for i, o in plan.aliases.items()}
    bufs = [pltpu.with_memory_space_constraint(a, pltpu.HBM) for a in plan.operands]
    where = []
    for o, sd in enumerate(plan.out_shape):
        if o in in_place:
            where.append(in_place[o])
        else:
            where.append(len(bufs))
            bufs.append(pltpu.with_memory_space_constraint(lax.empty(sd.shape, sd.dtype), pltpu.HBM))
    return bufs, where


def split_start(name, plans):
    layout = [_plan_buffers(p) for p in plans]
    counts = [len(b) for b, _ in layout]
    n_buf = sum(counts)

    def body(*refs):
        sems, token = refs[n_buf:n_buf + 2 * len(plans)], refs[-1]
        pos = 0
        for k, (p, (_, where)) in enumerate(zip(plans, layout)):
            mine = refs[pos:pos + counts[k]]
            pos += counts[k]
            sends, _ = p.copies(mine[:len(p.operands)], [mine[w] for w in where], sems[2 * k], sems[2 * k + 1])
            for cp in sends:
                cp.start()
        token[...] = jnp.zeros_like(token)

    bufs = [b for bs, _ in layout for b in bs]
    res = pl.pallas_call(
        body, name=name, in_specs=[HBM] * n_buf,
        out_specs=[SEMS] * (2 * len(plans)) + [HBM] * n_buf + [pl.BlockSpec(memory_space=pltpu.VMEM)],
        out_shape=[pltpu.SemaphoreType.DMA((p.n_sems,)) for p in plans for _ in range(2)] + [pltpu.HBM(b.shape, b.dtype) for b in bufs]
        + [SDS((8, 128), F32)],
        input_output_aliases={i: 2 * len(plans) + i for i in range(n_buf)}, compiler_params=pltpu.CompilerParams(**SPLIT_PARAMS))(*bufs)
    pos = 2 * len(plans)
    for k, p in enumerate(plans):
        p.in_flight = (res[2 * k], res[2 * k + 1], list(res[pos:pos + counts[k]]), layout[k][1])
        pos += counts[k]
    return res[-1]


def split_wait(name, plan, after):
    send_sems, recv_sems, bufs, where = plan.in_flight
    n_buf = len(bufs)

    def body(*refs):
        mine = refs[:n_buf]
        sends, recvs = plan.copies(mine[:len(plan.operands)], [mine[w] for w in where], refs[n_buf], refs[n_buf + 1])
        for cp in recvs:
            cp.wait_recv()
        for cp in sends:
            cp.wait_send()

    res = pl.pallas_call(body, name=name, in_specs=[HBM] * n_buf + [SEMS, SEMS, ANY], out_specs=[HBM] * n_buf,
                         out_shape=[pltpu.HBM(b.shape, b.dtype) for b in bufs], input_output_aliases={i: i for i in range(n_buf)},
                         compiler_params=pltpu.CompilerParams(**SPLIT_PARAMS))(*bufs, send_sems, recv_sems, after)
    plan.results = [res[w] for w in where]
    return plan.results


def pair_add(name, g, got, place):
    slots, rows, cols = g.shape
    half = rows // 2
    tr = _row_tile(half)
    nb = half // tr

    def kern(_, g_ref, t_ref, o_ref):
        o_ref[...] = (g_ref[...].astype(F32) + t_ref[...].astype(F32)).astype(o_ref.dtype)

    blk = pl.BlockSpec((None, tr, cols), lambda s, i, p: (s, i, 0))
    grid_spec = pltpu.PrefetchScalarGridSpec(
        num_scalar_prefetch=1, grid=(slots, nb),
        in_specs=[pl.BlockSpec((None, tr, cols), lambda s, i, p: (s, p[1] * nb + i, 0)), blk], out_specs=blk)
    return pl.pallas_call(kern, grid_spec=grid_spec, out_shape=SDS((slots, half, cols), g.dtype), name=name,
                          compiler_params=_cparams())(place, g, got)


def chip_add(name, p, q, place):
    slots, half, cols = p.shape
    tr = _row_tile(half)
    nb = half // tr

    def kern(_, p_ref, q1, q2, q3, o_ref):
        o_ref[...] = p_ref[...].astype(F32) + q1[...].astype(F32) + q2[...].astype(F32) + q3[...].astype(F32)

    def slot(k):
        return pl.BlockSpec((None, tr, cols), lambda i, pr: ((pr[0] + k) % slots, i, 0))

    grid_spec = pltpu.PrefetchScalarGridSpec(
        num_scalar_prefetch=1, grid=(nb,), in_specs=[slot(0), slot(1), slot(2), slot(3)],
        out_specs=pl.BlockSpec((tr, cols), lambda i, pr: (pr[1] * nb + i, 0)))
    return pl.pallas_call(kern, grid_spec=grid_spec, out_shape=SDS((2 * half, cols), F32), name=name,
                          compiler_params=_cparams())(place, p, q, q, q)


def pair_adds(tag, gs, gots, place):
    return [pair_add(f"{tag}_pair_add_{i}", g, got, place) for i, (g, got) in enumerate(zip(gs, gots))]


def chip_adds(tag, pairs, qs, place):
    return [chip_add(f"{tag}_chip_add_{i}", p, q, place) for i, (p, q) in enumerate(zip(pairs, qs))]


def reduce_scatter_chips(tag, gs, place):
    pairs = pair_adds(tag, gs, run_plan(tag + "_pair_exchange", plan_pair_exchange(gs)), place)
    return run_plan(tag + "_pair_join", plan_pair_join(chip_adds(tag, pairs, run_plan(tag + "_chip_scatter", plan_chip_scatter(pairs)), place)))


BIG = [("w_out", (2, 512, 1024)), ("w_mem_kv", (2, 256, 1024)), ("s5_w_in", (1, 1024, 1024)), ("s5_w_glu", (1, 1536, 768)),
       ("mla_w_in", (1, 1024, 848)), ("mla_w_uq", (1, 512, 576)), ("mla_w_ukv", (1, 256, 768))]
SHARDED_SMALL = [("mla_q_lora_norm", (1, 128)), ("mla_kv_lora_norm", (1, 64))]
SMALL = [("ln_gain", (2, 1024)), ("mem_norm", (2, 1024)), ("xq_norm", (2, 128)), ("xk_norm", (2, 128)),
         ("s5_lambda_re", (1, 96, 64)), ("s5_lambda_im", (1, 96, 64)), ("s5_log_step", (1, 96)),
         ("s5_b_re", (1, 96, 64, 16)), ("s5_b_im", (1, 96, 64, 16)), ("s5_c_re", (1, 96, 16, 64)), ("s5_c_im", (1, 96, 16, 64)),
         ("s5_d", (1, 1536)), ("mla_q_nope_norm", (1, 128)), ("mla_k_nope_norm", (1, 128)), ("mla_q_rope_norm", (1, 64)),
         ("mla_k_rope_norm", (1, 64))]
WEIGHT_ORDER = ["ln_gain", "w_out", "mem_norm", "w_mem_kv", "xq_norm", "xk_norm", "s5_w_in", "s5_lambda_re", "s5_lambda_im",
                "s5_log_step", "s5_b_re", "s5_b_im", "s5_c_re", "s5_c_im", "s5_d", "s5_w_glu", "mla_w_in", "mla_q_lora_norm",
                "mla_kv_lora_norm", "mla_w_uq", "mla_w_ukv", "mla_q_nope_norm", "mla_k_nope_norm", "mla_q_rope_norm", "mla_k_rope_norm"]
MINOR_LAST = {"mla_w_in": (0, 2, 1), "mla_w_uq": (0, 2, 1), "s5_b_re": (0, 2, 3, 1), "s5_b_im": (0, 2, 3, 1),
              "s5_c_re": (0, 2, 3, 1), "s5_c_im": (0, 2, 3, 1)}
SMALL_FULL = SMALL + [(n, (1, 4 * s[1])) for n, s in SHARDED_SMALL]
N_SMALL = sum(math.prod(s) for _, s in SMALL_FULL)
SMALL_ROWS, SMALL_LANES = 128, 1024

PAIR_OUT, PAIR_MKV, PAIR_ROWS = 0, 512, 768


def stack_shards(w, dtype):
    pairs = [jnp.concatenate([w["w_out"][l], w["w_mem_kv"][l]], axis=0).astype(dtype) for l in range(2)]
    return ([w["s5_w_in"][0].astype(dtype)], [pairs[0], w["s5_w_glu"][0].astype(dtype)],
            [pairs[1], w["mla_w_ukv"][0].astype(dtype), w["mla_w_in"][0].astype(dtype), w["mla_w_uq"][0].astype(dtype)])


def pair_views(pair):
    return {"w_out": Sharded(pair, "row", PAIR_OUT, 512), "w_mem_kv": Sharded(pair, "row", PAIR_MKV, 256)}


def grad_views():
    pair = SDS((4, PAIR_ROWS, 1024), BF16)
    return {"w_out": Sharded(pair, "row", PAIR_OUT, 512), "w_mem_kv": Sharded(pair, "row", PAIR_MKV, 256),
            "s5_w_in": Sharded(SDS((4, 1024, 1024), BF16), "col", 0, 1024), "s5_w_glu": Sharded(SDS((4, 1536, 768), BF16), "col", 0, 1536),
            "mla_w_ukv": Sharded(SDS((4, 256, 768), BF16), "col", 0, 256)}


def cols_to_shards(full):
    return full.reshape(full.shape[0], 4, full.shape[1] // 4).transpose(1, 0, 2)


def shards_to_cols(arr):
    return arr.transpose(1, 0, 2).reshape(arr.shape[1], 4 * arr.shape[2])


def mla_in_permute(w):
    o1, o2, o3, o4 = QL, QL + KVL, QL + KVL + ROPE, QL + KVL + ROPE + XQW
    return jnp.concatenate([w[:, o4:], w[:, :o1], w[:, o3:o4], w[:, o1:o2], w[:, o2:o3],
                            jnp.zeros((w.shape[0], MLA_IN_P - MLA_IN), w.dtype)], axis=1)


def mla_in_unpermute(d):
    return jnp.concatenate([d[:, 2048:2560], d[:, 3072:3328], d[:, 3328:3392], d[:, 2560:3072], d[:, :2048]], axis=1)


def uq_permute(w):
    w3 = w.reshape(w.shape[0], MH, NOPE + ROPE)
    return jnp.concatenate([w3[:, :, :NOPE].reshape(w.shape[0], MH * NOPE), w3[:, :, NOPE:].reshape(w.shape[0], MH * ROPE)], axis=1)


def uq_unpermute(d):
    dn = d[:, :MH * NOPE].reshape(d.shape[0], MH, NOPE)
    dr = d[:, MH * NOPE:].reshape(d.shape[0], MH, ROPE)
    return jnp.concatenate([dn, dr], axis=2).reshape(d.shape[0], MH * (NOPE + ROPE))


def time_permute(a):
    return a.reshape(SEG, SEG_LEN, a.shape[-1]).transpose(1, 0, 2).reshape(L, a.shape[-1])


def time_unpermute(a):
    return a.reshape(SEG_LEN, SEG, a.shape[-1]).transpose(1, 0, 2).reshape(L, a.shape[-1])


def mem_branch_fwd(tag, mem, mem_norm, w_mem_kv, xk_norm):
    mn = row_fwd(tag + "_mem_rms", fn_rms, ML, ML, [(mem, D, 0)], [mem_norm], [(D, BF16)])[0]
    kv = matmul(tag + "_mem_kv", mn, w_mem_kv, "nn", F32)
    kn = row_fwd(tag + "_mem_knorm", fn_mem_k, ML, ML, [(kv, XQW, 0)], [xk_norm], [(XQW, F32)])[0]
    return mn, kv, kn


def mem_branch_bwd(tag, mem, mem_norm, w_mem_kv, xk_norm, mn, kv, dkn, dv, g_view, g_wide):
    dk, dxk = row_bwd(tag + "_mem_knorm_bwd", fn_mem_k, ML, ML, [(kv, XQW, 0)], [xk_norm], [(dkn, XQW, 0)], [True], [True])
    dkv = jnp.concatenate([dk, dv], axis=1)
    dmn = matmul(tag + "_mem_kv_dx", dkv, w_mem_kv, "nt", F32)
    g_wide = matmul(tag + "_mem_kv_dw", mn, dkv, "tn", out=g_view, into=g_wide)
    dmem_norm = row_bwd(tag + "_mem_rms_bwd", fn_rms, ML, ML, [(mem, D, 0)], [mem_norm], [(dmn, D, 0)], [False], [True])[0]
    return g_wide, dmem_norm, dxk


def mem_attn_fwd(tag, proj, cb, kn, kv, xq_norm):
    return row_fwd(tag + "_mem_attn", fn_mem_attn, L, 256, [(proj, XQW, cb)], [kn, kv[:, XQW:], xq_norm], [(XQW, F32)])[0]


def mem_attn_bwd(tag, proj, cb, kn, kv, xq_norm, dmo, dproj):
    place = {"cols": proj.shape[1], "cb": cb, "into": dproj, "dtype": dproj.dtype}
    return row_bwd(tag + "_mem_attn_bwd", fn_mem_attn, L, 256, [(proj, XQW, cb)], [kn, kv[:, XQW:], xq_norm], [(dmo, XQW, 0)],
                   [place], [True, True, True])


def device_step(x, mem, positions, target, small, env, hooks=None):
    hooks = hooks or {}

    def plans_for(name):
        return hooks[("plans", name)](env) if ("plans", name) in hooks else ()

    def around(when, name, last=None):
        if (when, name) in hooks:
            hooks[(when, name)](env, last)

    g = {}
    gw = grad_views()
    ln, mem_norm, xq_norm, xk_norm = small["ln_gain"], small["mem_norm"], small["xq_norm"], small["xk_norm"]

    lre, lim = small["s5_lambda_re"][0], small["s5_lambda_im"][0]
    ls = small["s5_log_step"].reshape(SG, 1)
    one = pl.BlockSpec((SG, SP), lambda i: (0, 0))
    col = pl.BlockSpec((SG, 1), lambda i: (0, 0))
    disc_ins = [(lre, one), (lim, one), (ls, col)]
    a_re, a_im, coef_re, coef_im = stage("s5_disc", fn_s5_disc, (1,), disc_ins, [(SDS((SG, SP), F32), one)] * 4)
    b_re, b_im = small["s5_b_re"].reshape(SN, SC), small["s5_b_im"].reshape(SN, SC)
    c_re, c_im = small["s5_c_re"].reshape(PW, SP), small["s5_c_im"].reshape(PW, SP)
    bmat_rows = [(b_re, SC, 0), (b_im, SC, 0), (coef_re.reshape(SN, 1), 1, 0), (coef_im.reshape(SN, 1), 1, 0)]
    wb_re, wb_im = row_fwd("s5_bmat", fn_s5_bmat, SN, 512, bmat_rows, [], [(128, F32)] * 2)
    cmat_rows = [(c_re, SP, 0), (c_im, SP, 0)]
    wc_re, wc_im = row_fwd("s5_cmat", fn_s5_cmat, PW, 128, cmat_rows, [], [(512, F32)] * 2)
    a_re_v, a_im_v = a_re.reshape(1, SN), a_im.reshape(1, SN)
    s5_d = small["s5_d"]

    xp = time_permute(x)
    h0 = row_fwd("l0_rms", fn_rms, L, 256, [(xp, D, 0)], [ln[0:1]], [(D, BF16)])[0]
    around("before", "l0_in", wb_re)
    w_in0 = Sharded(env["in0"], "col", 0, 1024)
    proj0 = matmul("l0_in", h0, w_in0, "nn")
    s_re, s_im, g0 = s5_forward(proj0, wb_re, wb_im, wc_re, wc_im, a_re_v, a_im_v, s5_d, plans=plans_for("s5_forward"))
    around("before", "l0_glu", g0)
    w0 = dict(pair_views(env["pair0"]), s5_w_glu=Sharded(env["glu"], "col", 0, 1536))
    z0 = matmul("l0_glu", g0, w0["s5_w_glu"], "nn", plans=plans_for("l0_glu"))
    mn0, kv0, kn0 = mem_branch_fwd("l0", mem, mem_norm[0:1], w0["w_mem_kv"], xk_norm[0:1])
    mo0 = mem_attn_fwd("l0", proj0, 3, kn0, kv0, xq_norm[0:1])
    o0 = row_fwd("l0_merge", fn_merge_glu, L, 256, [(z0, 2 * PW, 0), (mo0, XQW, 0), (proj0, BW, 1)], [], [(BW, BF16)])[0]
    around("before", "l0_out", o0)
    x1p = matmul("l0_out", o0, w0["w_out"], "nn", F32, add=xp, plans=plans_for("l0_out"))
    around("after", "l0_out", x1p)
    x1 = time_unpermute(x1p)

    w1 = dict(pair_views(env["pair1"]), mla_w_ukv=Sharded(env["ukv"], "col", 0, 256))
    w_in1, w_uq = env["w_in1"], env["w_uq"]
    h1 = row_fwd("l1_rms", fn_rms, L, 256, [(x1, D, 0)], [ln[1:2]], [(D, BF16)])[0]
    proj1 = matmul("l1_in", h1, w_in1, "nn")
    qln, kvln = env["q_lora_norm"].reshape(1, QL), env["kv_lora_norm"].reshape(1, KVL)
    cqn = row_fwd("l1_q_lora_rms", fn_rms, L, 256, [(proj1, QL, 4)], [qln], [(QL, BF16)])[0]
    ckvn = row_fwd("l1_kv_lora_rms", fn_rms, L, 256, [(proj1, KVL, 12)], [kvln], [(KVL, BF16)])[0]
    q = matmul("l1_uq", cqn, w_uq, "nn")
    kv = matmul("l1_ukv", ckvn, w1["mla_w_ukv"], "nn")
    inv_freq = ROPE_THETA ** (-jnp.arange(ROPE // 2, dtype=F32) / (ROPE // 2))
    ang = positions.astype(F32)[:, None] * inv_freq
    cos2 = jnp.tile(jnp.cos(ang), (1, 4))
    sin_signed = jnp.tile(jnp.concatenate([-jnp.sin(ang), jnp.sin(ang)], axis=1), (1, 2))
    qnn, knn = small["mla_q_nope_norm"], small["mla_k_nope_norm"]
    qrn, krn = jnp.tile(small["mla_q_rope_norm"], (1, 2)), jnp.tile(small["mla_k_rope_norm"], (1, 2))
    tp = 256
    prep_ins = [(q, rspec(tp, MH * (NOPE + ROPE))), (kv, rspec(tp, MH * 256)), (proj1, rspec(tp, 128, 26)),
                (cos2, rspec(tp, 128)), (sin_signed, rspec(tp, 128))] + [(a, cspec((1, 128))) for a in (qnn, knn, qrn, krn)]
    hq_spec = pl.BlockSpec((MH, tp, 256), lambda i: (0, i, 0))
    hv_spec = pl.BlockSpec((MH, tp, 128), lambda i: (0, i, 0))
    qf, kf, vh = stage("l1_mla_prep", fn_mla_prep, (L // tp,), prep_ins,
                       [(SDS((MH, L, 256), BF16), hq_spec), (SDS((MH, L, 256), BF16), hq_spec), (SDS((MH, L, 128), BF16), hv_spec)])
    attn, attn_lse = causal_attn(qf, kf, vh)
    mn1, kv1, kn1 = mem_branch_fwd("l1", mem, mem_norm[1:2], w1["w_mem_kv"], xk_norm[1:2])
    mo1 = mem_attn_fwd("l1", proj1, 5, kn1, kv1, xq_norm[1:2])
    o1 = row_fwd("l1_merge", fn_merge, L, 256, [(attn, PW, 0), (mo1, XQW, 0), (proj1, BW, 0)], [], [(BW, BF16)])[0]
    x2 = matmul("l1_out", o1, w1["w_out"], "nn", F32, add=x1)
    dx2, loss = loss_and_grad(x2, target)

    do1 = matmul("l1_out_dx", dx2, w1["w_out"], "nt")
    g_pair1 = matmul("l1_out_dw", o1, dx2, "tn", out=gw["w_out"])
    dattn, dmo1, dproj1 = row_bwd("l1_merge_bwd", fn_merge, L, 256, [(attn, PW, 0), (mo1, XQW, 0), (proj1, BW, 0)], [],
                                  [(do1, BW, 0)], [True, True, {"cols": MLA_IN_P, "cb": 0, "dtype": BF16}], [])
    dproj1, dkn1, dv1, dxqn1 = mem_attn_bwd("l1", proj1, 5, kn1, kv1, xq_norm[1:2], dmo1, dproj1)
    env["g_pair1"], dmem_norm1, dxk1 = mem_branch_bwd("l1", mem, mem_norm[1:2], w1["w_mem_kv"], xk_norm[1:2], mn1, kv1, dkn1, dv1,
                                                      gw["w_mem_kv"], g_pair1)
    dqf, dkf, dvh = causal_attn_bwd(qf, kf, vh, attn, attn_lse, dattn)
    prep_diffs = [("row", SDS((L, MH * (NOPE + ROPE)), BF16), rspec(tp, MH * (NOPE + ROPE))), ("row", SDS((L, MH * 256), BF16), rspec(tp, MH * 256)),
                  ("row", SDS((L, MLA_IN_P), BF16), rspec(tp, 128, 26), {"into": dproj1}), None, None] + [("acc", (0,))] * 4
    dq, dkv, dproj1, dqnn, dknn, dqrn, dkrn = stage_bwd("l1_mla_prep_bwd", fn_mla_prep, (L // tp,), prep_ins,
                                                        [(dqf, hq_spec), (dkf, hq_spec), (dvh, hv_spec)], prep_diffs)
    dcqn = matmul("l1_uq_dx", dq, w_uq, "nt")
    env["dw_uq"] = matmul("l1_uq_dw", cqn, dq, "tn")
    dckvn = matmul("l1_ukv_dx", dkv, w1["mla_w_ukv"], "nt")
    env["g_ukv"] = matmul("l1_ukv_dw", ckvn, dkv, "tn", out=gw["mla_w_ukv"])
    dproj1, dqln = row_bwd("l1_q_lora_rms_bwd", fn_rms, L, 256, [(proj1, QL, 4)], [qln], [(dcqn, QL, 0)],
                           [{"cols": MLA_IN_P, "cb": 4, "into": dproj1, "dtype": BF16}], [True])
    dproj1, dkvln = row_bwd("l1_kv_lora_rms_bwd", fn_rms, L, 256, [(proj1, KVL, 12)], [kvln], [(dckvn, KVL, 0)],
                            [{"cols": MLA_IN_P, "cb": 12, "into": dproj1, "dtype": BF16}], [True])
    dh1 = matmul("l1_in_dx", dproj1, w_in1, "nt")
    env["dw_in1"] = matmul("l1_in_dw", h1, dproj1, "tn")
    dx1, dln1 = row_bwd("l1_rms_bwd", fn_rms, L, 256, [(x1, D, 0)], [ln[1:2]], [(dh1, D, 0)], [{"add": (dx2, D, 0)}], [True],
                        plans=plans_for("l1_rms_bwd"))
    around("after", "l1_rms_bwd", dx1)
    dx1p = time_permute(dx1)

    do0 = matmul("l0_out_dx", dx1p, w0["w_out"], "nt", plans=plans_for("l0_out_dx"))
    g_pair0 = matmul("l0_out_dw", o0, dx1p, "tn", out=gw["w_out"])
    dz0, dmo0, dproj0 = row_bwd("l0_merge_bwd", fn_merge_glu, L, 256, [(z0, 2 * PW, 0), (mo0, XQW, 0), (proj0, BW, 1)], [],
                                [(do0, BW, 0)], [{"dtype": BF16}, True, {"cols": 2 * BW, "cb": 1, "dtype": BF16}], [])
    dproj0, dkn0, dv0, dxqn0 = mem_attn_bwd("l0", proj0, 3, kn0, kv0, xq_norm[0:1], dmo0, dproj0)
    env["g_pair0"], dmem_norm0, dxk0 = mem_branch_bwd("l0", mem, mem_norm[0:1], w0["w_mem_kv"], xk_norm[0:1], mn0, kv0, dkn0, dv0,
                                                      gw["w_mem_kv"], g_pair0)
    env["g_glu"] = matmul("l0_glu_dw", g0, dz0, "tn", out=gw["s5_w_glu"], plans=plans_for("l0_glu_dw"))
    dg0 = matmul("l0_glu_dx", dz0, w0["s5_w_glu"], "nt", plans=plans_for("l0_glu_dx"))
    around("before", "s5_backward", dg0)
    dproj0, dd, dwc_re, dwc_im, dwb_re, dwb_im, da_re, da_im = s5_backward(dg0, proj0, s_re, s_im, wb_re, wb_im, wc_re, wc_im,
                                                                           a_re_v, a_im_v, s5_d, dproj0, plans=plans_for("s5_backward"))
    around("after", "s5_backward", dd)
    env["g_in0"] = matmul("l0_in_dw", h0, dproj0, "tn", out=gw["s5_w_in"], plans=plans_for("l0_in_dw"))
    dh0 = matmul("l0_in_dx", dproj0, w_in0, "nt", plans=plans_for("l0_in_dx"))
    around("after", "l0_in_dx", dh0)
    dxp, dln0 = row_bwd("l0_rms_bwd", fn_rms, L, 256, [(xp, D, 0)], [ln[0:1]], [(dh0, D, 0)], [{"add": (dx1p, D, 0)}], [True])
    grad_x = time_unpermute(dxp)

    db_re, db_im, dcoef_re, dcoef_im = row_bwd("s5_bmat_bwd", fn_s5_bmat, SN, 512, bmat_rows, [], [(dwb_re, 128, 0), (dwb_im, 128, 0)],
                                               [True] * 4, [], plans=plans_for("s5_bmat_bwd"))
    dc_re, dc_im = row_bwd("s5_cmat_bwd", fn_s5_cmat, PW, 128, cmat_rows, [], [(dwc_re, 512, 0), (dwc_im, 512, 0)], [True] * 2, [],
                           plans=plans_for("s5_cmat_bwd"))
    disc_cts = [(da_re.reshape(SG, SP), one), (da_im.reshape(SG, SP), one), (dcoef_re.reshape(SG, SP), one), (dcoef_im.reshape(SG, SP), one)]
    dlre, dlim, dls = stage_bwd("s5_disc_bwd", fn_s5_disc, (1,), disc_ins, disc_cts, [("acc", (0,))] * 3)

    g["ln_gain"] = jnp.concatenate([dln0, dln1], axis=0)
    g["mem_norm"] = jnp.concatenate([dmem_norm0, dmem_norm1], axis=0)
    g["xq_norm"] = jnp.concatenate([dxqn0, dxqn1], axis=0)
    g["xk_norm"] = jnp.concatenate([dxk0, dxk1], axis=0)
    g["s5_lambda_re"], g["s5_lambda_im"], g["s5_log_step"] = dlre, dlim, dls
    g["s5_b_re"], g["s5_b_im"], g["s5_c_re"], g["s5_c_im"] = db_re, db_im, dc_re, dc_im
    g["s5_d"] = dd
    g["mla_q_lora_norm"], g["mla_kv_lora_norm"] = dqln, dkvln
    g["mla_q_nope_norm"], g["mla_k_nope_norm"] = dqnn, dknn
    g["mla_q_rope_norm"] = dqrn[:, :ROPE] + dqrn[:, ROPE:]
    g["mla_k_rope_norm"] = dkrn[:, :ROPE] + dkrn[:, ROPE:]
    return loss, grad_x, g


def kernel(x, mem, positions, ln_gain, w_out, mem_norm, w_mem_kv, xq_norm, xk_norm, s5_w_in, s5_lambda_re, s5_lambda_im, s5_log_step, s5_b_re, s5_b_im, s5_c_re, s5_c_im, s5_d, s5_w_glu, mla_w_in, mla_q_lora_norm, mla_kv_lora_norm, mla_w_uq, mla_w_ukv, mla_q_nope_norm, mla_k_nope_norm, mla_q_rope_norm, mla_k_rope_norm, loss_target, m_ln_gain, m_w_out, m_mem_norm, m_w_mem_kv, m_xq_norm, m_xk_norm, m_s5_w_in, m_s5_lambda_re, m_s5_lambda_im, m_s5_log_step, m_s5_b_re, m_s5_b_im, m_s5_c_re, m_s5_c_im, m_s5_d, m_s5_w_glu, m_mla_w_in, m_mla_q_lora_norm, m_mla_kv_lora_norm, m_mla_w_uq, m_mla_w_ukv, m_mla_q_nope_norm, m_mla_k_nope_norm, m_mla_q_rope_norm, m_mla_k_rope_norm, v_ln_gain, v_w_out, v_mem_norm, v_w_mem_kv, v_xq_norm, v_xk_norm, v_s5_w_in, v_s5_lambda_re, v_s5_lambda_im, v_s5_log_step, v_s5_b_re, v_s5_b_im, v_s5_c_re, v_s5_c_im, v_s5_d, v_s5_w_glu, v_mla_w_in, v_mla_q_lora_norm, v_mla_kv_lora_norm, v_mla_w_uq, v_mla_w_ukv, v_mla_q_nope_norm, v_mla_k_nope_norm, v_mla_q_rope_norm, v_mla_k_rope_norm):
    args = dict(locals())
    wts = {n: args[n] for n in WEIGHT_ORDER}
    mom = {n: args["m_" + n] for n in WEIGHT_ORDER}
    var = {n: args["v_" + n] for n in WEIGHT_ORDER}

    chip = 2 * lax.axis_index("x") + lax.axis_index("y")
    place = jnp.stack([chip, lax.axis_index("c")]).astype(jnp.int32)

    def own_slot(gathered, shards):
        return [lax.dynamic_update_slice(g, s[None], (chip, 0, 0)) for g, s in zip(gathered, shards)]

    groups = list(stack_shards(wts, BF16))
    groups[2].append(jnp.concatenate([mla_q_lora_norm, jnp.pad(mla_kv_lora_norm, ((0, 0), (0, 64))), jnp.zeros((14, 128), F32)], axis=0))
    over_ici = [plan_gather_ici(shards) for shards in groups]
    _SCHEDULE_BEHIND.clear()
    schedule_behind(split_start("gather_start", over_ici))
    env, hooks, passed_on = {}, {}, {}

    def arrived(k, after, pass_now):
        passing = plan_gather_pass(split_wait(f"gather_wait_{k}", over_ici[k], after))
        passed_on[k] = passing
        return own_slot(run_plan(f"gather_pass_{k}", passing), groups[k]) if pass_now else None

    def need_in0(env, last):
        env["in0"], = arrived(0, last, True)

    def need_layer0(env, last):
        env["pair0"], env["glu"] = arrived(1, last, True)

    def need_layer1(env, last):
        arrived(2, last, False)

    def layer1_weights(env, last):
        pair1, ukv, in1, uq, norms = own_slot(passed_on[2].results, groups[2])
        env.update(pair1=pair1, ukv=ukv, w_in1=mla_in_permute(shards_to_cols(in1)), w_uq=uq_permute(shards_to_cols(uq)),
                   q_lora_norm=norms[:, 0, :], kv_lora_norm=norms[:, 1, :64])

    hooks["before", "l0_in"], hooks["before", "l0_glu"], hooks["before", "l0_out"] = need_in0, need_layer0, need_layer1
    hooks["plans", "l0_out"], hooks["after", "l0_out"] = (lambda env: [passed_on[2]]), layer1_weights

    rs = {}

    def swap(k, gs):
        rs[k, "g"], rs[k, "swap"] = gs, plan_pair_exchange(gs)
        return rs[k, "swap"]

    def start_scatter(k):
        rs[k, "pairs"] = pair_adds(f"rs{k}", rs[k, "g"], rs[k, "swap"].results, place)
        rs[k, "scatter"] = plan_chip_scatter(rs[k, "pairs"])
        schedule_behind(split_start(f"rs{k}_scatter_start", [rs[k, "scatter"]]))

    def join(k, after):
        rs[k, "join"] = plan_pair_join(chip_adds(f"rs{k}", rs[k, "pairs"], split_wait(f"rs{k}_scatter_wait", rs[k, "scatter"], after), place))
        return rs[k, "join"]

    hooks["plans", "l1_rms_bwd"] = lambda env: [swap(1, [env["g_pair1"], env["g_ukv"], cols_to_shards(mla_in_unpermute(env["dw_in1"])).astype(BF16),
                                                          cols_to_shards(uq_unpermute(env["dw_uq"])).astype(BF16)])]
    hooks["after", "l1_rms_bwd"] = lambda env, last: start_scatter(1)
    hooks["plans", "l0_glu_dx"] = lambda env: [swap(0, [env["g_pair0"], env["g_glu"]])]

    def before_s5_backward(env, last):
        start_scatter(0)
        env["join1"] = join(1, last)

    hooks["before", "s5_backward"] = before_s5_backward
    hooks["plans", "s5_backward"] = lambda env: [env["join1"]]
    hooks["after", "s5_backward"] = lambda env, last: env.update(join0=join(0, last))
    hooks["plans", "l0_in_dx"] = lambda env: [swap(2, [env["g_in0"]]), env["join0"]]
    hooks["after", "l0_in_dx"] = lambda env, last: start_scatter(2)

    small = {n: wts[n] for n, _ in SMALL}
    loss, grad_x, g = device_step(x[0], mem[0], positions[0], loss_target[0], small, env, hooks)
    loss = lax.psum(loss[0, 0], MESH_AXES)
    r_in0, = run_plan("rs2_pair_join", join(2, g["s5_log_step"]))
    (r_pair1, r_ukv, r_in1, r_uq), (r_pair0, r_glu) = (rs[k, "join"].results for k in (1, 0))

    small_flat = jnp.concatenate([g[n].reshape(-1) for n, _ in SMALL_FULL])
    g_small = jnp.pad(small_flat, (0, 4 * SMALL_ROWS * SMALL_LANES - N_SMALL)).astype(BF16).reshape(4, SMALL_ROWS, SMALL_LANES)
    r_small = reduce_scatter_chips("rs3", [g_small], place)[0]
    small_all = own_slot(all_gather_chips("gather_small_grads", [r_small]), [r_small])[0].reshape(-1)[:N_SMALL]

    grads = {"w_out": jnp.stack([r_pair0[:PAIR_MKV], r_pair1[:PAIR_MKV]]), "w_mem_kv": jnp.stack([r_pair0[PAIR_MKV:], r_pair1[PAIR_MKV:]]),
             "s5_w_in": r_in0[None], "s5_w_glu": r_glu[None], "mla_w_ukv": r_ukv[None], "mla_w_in": r_in1[None], "mla_w_uq": r_uq[None]}
    off = 0
    for n, s in SMALL_FULL:
        grads[n] = small_all[off:off + math.prod(s)].reshape(s)
        off += math.prod(s)
    for n, s in SHARDED_SMALL:
        grads[n] = lax.dynamic_slice(grads[n], (0, chip * s[1]), s)

    delta, new_m, new_v = {}, {}, {}
    for n, s in BIG + [(n, s) for n, s in SMALL if len(s) == 4]:
        perm = MINOR_LAST.get(n, tuple(range(len(s))))
        turned = tuple(s[p] for p in perm)
        view = lambda a: jnp.transpose(a, perm).reshape(-1, turned[-1])
        res = adamw("adamw_" + n, view(wts[n]), view(grads[n]), view(mom[n]), view(var[n]))
        delta[n], new_m[n], new_v[n] = (jnp.transpose(r.reshape(turned), tuple(perm.index(i) for i in range(len(s)))) for r in res)
    small_names = [n for n, s in SMALL if len(s) < 4] + [n for n, _ in SHARDED_SMALL]
    n_own = sum(wts[n].size for n in small_names)
    rows_own = -(-n_own // (8 * 128)) * 8

    def pack_small(d):
        flat = jnp.concatenate([d[n].reshape(-1) for n in small_names])
        return jnp.pad(flat, (0, rows_own * 128 - n_own), constant_values=1.0).reshape(rows_own, 128)

    res = adamw("adamw_small", pack_small(wts), pack_small(grads), pack_small(mom), pack_small(var))
    off = 0
    for n in small_names:
        size = wts[n].size
        delta[n], new_m[n], new_v[n] = (r.reshape(-1)[off:off + size].reshape(wts[n].shape) for r in res)
        off += size

    return (loss, grad_x[None], *[grads[n] for n in WEIGHT_ORDER], *[delta[n] for n in WEIGHT_ORDER],
            *[new_m[n] for n in WEIGHT_ORDER], *[new_v[n] for n in WEIGHT_ORDER])
```

```python
import functools
import math

import jax
import jax.numpy as jnp
from jax import lax
from jax.experimental import pallas as pl
from jax.experimental.pallas import tpu as pltpu

F32, BF16 = jnp.float32, jnp.bfloat16
SDS = jax.ShapeDtypeStruct

D = 1024
L = 2048
ML = 256
BW = 2 * D
XQW = BW // 4
PW = BW - XQW
XH, XHD = 4, 128
SG, SC, SP = 96, 16, 64
SN = SG * SP
NOPE, ROPE, VD = 128, 64, 128
MH = 12
QL, KVL = 512, 256
EPS = 1e-6
ROPE_THETA = 10000.0
MLA_IN = QL + KVL + ROPE + XQW + BW
MLA_IN_P = 3456
ADAM_LR, ADAM_B1, ADAM_B2, ADAM_EPS, ADAM_WD, ADAM_STEP = 0.001, 0.9, 0.999, 1e-08, 0.01, 10

VMEM_LIMIT = 48 * 2**20
SEG = 8
SEG_LEN = L // SEG
MESH_AXES = ("x", "y", "c")


def _cparams():
    return pltpu.CompilerParams(vmem_limit_bytes=VMEM_LIMIT)


def _dg(a, b, ca, cb):
    return lax.dot_general(a.astype(BF16), b.astype(BF16), (((ca,), (cb,)), ((), ())), preferred_element_type=F32)


@jax.custom_vjp
def mm_nn(a, b):
    return _dg(a, b, 1, 0)


mm_nn.defvjp(lambda a, b: (_dg(a, b, 1, 0), (a, b)), lambda res, g: (_dg(g, res[1], 1, 1), _dg(res[0], g, 0, 0)))


@jax.custom_vjp
def mm_nt(a, b):
    return _dg(a, b, 1, 1)


mm_nt.defvjp(lambda a, b: (_dg(a, b, 1, 1), (a, b)), lambda res, g: (_dg(g, res[1], 1, 0), _dg(g, res[0], 0, 0)))


@functools.partial(jax.custom_vjp, nondiff_argnums=(1,))
def lane_roll(x, shift):
    return pltpu.roll(x, shift, 1)


lane_roll.defvjp(lambda x, shift: (pltpu.roll(x, shift, 1), None),
                 lambda shift, _, g: (pltpu.roll(g, (128 - shift) % 128, 1),))


def rms(x, g):
    return x * lax.rsqrt(jnp.mean(x * x, axis=-1, keepdims=True) + EPS) * g


@jax.custom_vjp
def softmax_rows(s):
    e = jnp.exp(s - jnp.max(s, axis=-1, keepdims=True))
    return e / jnp.sum(e, axis=-1, keepdims=True)


def _softmax_rows_fwd(s):
    p = softmax_rows(s)
    return p, p


def _softmax_rows_bwd(p, g):
    return (p * (g - jnp.sum(g * p, axis=-1, keepdims=True)),)


softmax_rows.defvjp(_softmax_rows_fwd, _softmax_rows_bwd)


def silu(x):
    return x * jax.nn.sigmoid(x)


ANY = pl.BlockSpec(memory_space=pl.ANY)
MESH_ID = pl.DeviceIdType.MESH


def _dma_sems(n):
    return [pltpu.SemaphoreType.DMA((n,)), pltpu.SemaphoreType.DMA((n,))]


class Plan:
    def __init__(self, operands, out_shape, aliases, n_sems, copies):
        self.operands, self.out_shape, self.aliases, self.n_sems, self.copies = list(operands), list(out_shape), aliases, n_sems, copies
        self.results = None


_SCHEDULE_BEHIND = []


def schedule_behind(token):
    _SCHEDULE_BEHIND.append(token)


def hosted_call(kern, *, name, grid, in_specs, out_specs, out_shape, operands, scratch_shapes=(), aliases=None, cparams=None, plans=(), deps=()):
    n_in, n_out, n_scr = len(in_specs), len(out_specs), len(scratch_shapes)
    p_in, p_out = [len(p.operands) for p in plans], [len(p.out_shape) for p in plans]
    deps = tuple(deps) + tuple(_SCHEDULE_BEHIND)
    _SCHEDULE_BEHIND.clear()
    all_aliases = dict(aliases or {})
    in_off, out_off = n_in, n_out
    for p, ni, no in zip(plans, p_in, p_out):
        all_aliases.update({in_off + i: out_off + o for i, o in p.aliases.items()})
        in_off, out_off = in_off + ni, out_off + no

    def body(*refs):
        pos, pins, pouts = n_in, [], []
        for ni in p_in:
            pins.append(refs[pos:pos + ni])
            pos += ni
        pos += len(deps)
        main_out = refs[pos:pos + n_out]
        pos += n_out
        for no in p_out:
            pouts.append(refs[pos:pos + no])
            pos += no
        main_scr = refs[pos:pos + n_scr]
        pos += n_scr
        if plans:
            ids = [pl.program_id(ax) for ax in range(len(grid))]
            first = functools.reduce(jnp.logical_and, [i == 0 for i in ids])
            last = functools.reduce(jnp.logical_and, [i == g - 1 for i, g in zip(ids, grid)])
            copies = [p.copies(pins[k], pouts[k], refs[pos + 2 * k], refs[pos + 2 * k + 1]) for k, p in enumerate(plans)]

            @pl.when(first)
            def _():
                for sends, _ in copies:
                    for cp in sends:
                        cp.start()

        kern(*refs[:n_in], *main_out, *main_scr)
        if plans:
            @pl.when(last)
            def _():
                for sends, recvs in copies:
                    for cp in recvs:
                        cp.wait_recv()
                    for cp in sends:
                        cp.wait_send()

    res = pl.pallas_call(body, grid=grid, in_specs=list(in_specs) + [ANY] * (sum(p_in) + len(deps)),
                         out_specs=list(out_specs) + [ANY] * sum(p_out), out_shape=list(out_shape) + [s for p in plans for s in p.out_shape],
                         scratch_shapes=list(scratch_shapes) + [s for p in plans for s in _dma_sems(p.n_sems)],
                         input_output_aliases=all_aliases, name=name, compiler_params=cparams or _cparams())(
        *operands, *[a for p in plans for a in p.operands], *deps)
    pos = n_out
    for p, no in zip(plans, p_out):
        p.results = list(res[pos:pos + no])
        pos += no
    return list(res[:n_out])


def _wide(v):
    return v.astype(F32) if v.dtype == BF16 else v


def stage(name, fn, grid, ins, outs):
    n_in = len(ins)

    def kern(*refs):
        res = fn(*[_wide(r[...]) for r in refs[:n_in]])
        for r, v in zip(refs[n_in:], res):
            r[...] = v.astype(r.dtype)

    return hosted_call(kern, name=name, grid=grid, in_specs=[s for _, s in ins], out_specs=[s for _, s in outs],
                       out_shape=[sd for sd, _ in outs], operands=[a for a, _ in ins])


def stage_bwd(name, fn, grid, ins, cts, diffs, plans=()):
    n_in, n_ct = len(ins), len(cts)
    didx = [i for i, d in enumerate(diffs) if d is not None]
    opts = {i: (diffs[i][3] if len(diffs[i]) > 3 else {}) for i in didx if diffs[i][0] == "row"}
    adds = [(i, opts[i]["add"]) for i in opts if "add" in opts[i]]
    intos = [(i, opts[i]["into"]) for i in opts if "into" in opts[i]]
    n_add, n_into = len(adds), len(intos)
    add_pos = {i: n_in + n_ct + k for k, (i, _) in enumerate(adds)}
    n_extra = n_in + n_ct + n_add + n_into

    def kern(*refs):
        vals = [_wide(r[...]) for r in refs[:n_in]]

        def f(*dv):
            full = list(vals)
            for i, v in zip(didx, dv):
                full[i] = v
            return fn(*full)

        _, vjp = jax.vjp(f, *[vals[i].astype(F32) for i in didx])
        gs = vjp(tuple(c[...].astype(F32) for c in refs[n_in:n_in + n_ct]))
        for o_ref, i, g in zip(refs[n_extra:], didx, gs):
            if diffs[i][0] == "row":
                if i in add_pos:
                    g = g + refs[add_pos[i]][...].astype(F32)
                o_ref[...] = g.astype(o_ref.dtype)
            else:
                first = functools.reduce(jnp.logical_and, [pl.program_id(ax) == 0 for ax in diffs[i][1]])

                @pl.when(first)
                def _():
                    o_ref[...] = g

                @pl.when(jnp.logical_not(first))
                def _():
                    o_ref[...] += g

    out_shape, out_specs = [], []
    for i in didx:
        if diffs[i][0] == "row":
            out_shape.append(diffs[i][1])
            out_specs.append(diffs[i][2])
        else:
            out_shape.append(SDS(ins[i][0].shape, F32))
            out_specs.append(ins[i][1])
    aliases = {n_in + n_ct + n_add + k: didx.index(i) for k, (i, _) in enumerate(intos)}
    in_specs = [s for _, s in ins] + [s for _, s in cts] + [s for _, (_, s) in adds] + [ANY] * n_into
    operands = [a for a, _ in ins] + [a for a, _ in cts] + [a for _, (a, _) in adds] + [a for _, a in intos]
    return hosted_call(kern, name=name, grid=grid, in_specs=in_specs, out_specs=out_specs, out_shape=out_shape, operands=operands,
                       aliases=aliases, plans=plans)


def rspec(tl, w, cb=0):
    return pl.BlockSpec((tl, w), lambda i: (i, cb))


def cspec(shape):
    return pl.BlockSpec(shape, lambda i: (0,) * len(shape))


def row_fwd(name, fn, rows, tl, row_ins, consts, outs):
    ins = [(a, rspec(tl, w, cb)) for a, w, cb in row_ins] + [(a, cspec(a.shape)) for a in consts]
    return stage(name, fn, (rows // tl,), ins, [(SDS((rows, w), dt), rspec(tl, w)) for w, dt in outs])


def row_bwd(name, fn, rows, tl, row_ins, consts, cts, row_diff, const_diff, plans=()):
    ins = [(a, rspec(tl, w, cb)) for a, w, cb in row_ins] + [(a, cspec(a.shape)) for a in consts]
    diffs = []
    for (a, w, cb), d in zip(row_ins, row_diff):
        if not d:
            diffs.append(None)
            continue
        d = d if isinstance(d, dict) else {}
        opts = {}
        if "add" in d:
            opts["add"] = (d["add"][0], rspec(tl, d["add"][1], d["add"][2]))
        if d.get("into") is not None:
            opts["into"] = d["into"]
        diffs.append(("row", SDS((rows, d.get("cols", w)), d.get("dtype", F32)), rspec(tl, w, d.get("cb", 0)), opts))
    diffs += [("acc", (0,)) if d else None for d in const_diff]
    return stage_bwd(name, fn, (rows // tl,), ins, [(a, rspec(tl, w, cb)) for a, w, cb in cts], diffs, plans=plans)


MATMUL_VMEM = 36 * 2**20
ADAMW_VMEM = 28 * 2**20


class Sharded:
    def __init__(self, arr, kind, roff, rows):
        self.arr, self.kind, self.roff, self.rows, self.n = arr, kind, roff, rows, arr.shape[2]
        self.shape = (rows, 4 * self.n) if kind == "col" else (4 * rows, self.n)

    def fits(self, t0, t1):
        return self.roff % t0 == 0 and self.rows % t0 == 0 and self.n % t1 == 0

    def spec(self, t0, t1, bidx):
        assert self.fits(t0, t1), (self.kind, self.roff, self.rows, self.n, t0, t1)
        r0 = self.roff // t0
        if self.kind == "col":
            per = self.n // t1
            return pl.BlockSpec((None, t0, t1), lambda *g: (bidx(*g)[1] // per, r0 + bidx(*g)[0], bidx(*g)[1] % per))
        per = self.rows // t0
        return pl.BlockSpec((None, t0, t1), lambda *g: (bidx(*g)[0] // per, r0 + bidx(*g)[0] % per, bidx(*g)[1]))


def matmul(name, a, b, mode, out_dtype=BF16, add=None, out=None, into=None, plans=()):
    if mode == "tn":
        k_dim, m = a.shape
    else:
        m, k_dim = a.shape
    n = b.shape[0] if mode == "nt" else b.shape[1]
    b_fit = b.fits if isinstance(b, Sharded) else (lambda t0, t1: True)
    o_fit = out.fits if out is not None else (lambda t0, t1: True)
    a_bytes, b_bytes = jnp.dtype(a.dtype).itemsize, jnp.dtype(b.arr.dtype if isinstance(b, Sharded) else b.dtype).itemsize
    o_bytes = jnp.dtype(out_dtype if out is None else out.arr.dtype).itemsize

    def vmem(tm, tn, tk):
        return 2 * (tm * tk * a_bytes + tk * tn * b_bytes + tm * tn * (o_bytes + (4 if add is not None else 0))) + 4 * tm * tn * (1 + (tk < k_dim))

    tiles = [(tm, tn, tk) for tm in (2048, 1024, 512, 256, 128) for tn in (1024, 768, 512, 384, 256, 128)
             for tk in sorted({k_dim, 1024, 768, 512, 384, 256, 128})
             if m % tm == 0 and n % tn == 0 and k_dim % tk == 0 and (b_fit(tn, tk) if mode == "nt" else b_fit(tk, tn)) and o_fit(tm, tn)
             and vmem(tm, tn, tk) <= MATMUL_VMEM]
    tm, tn, tk = max(tiles, key=lambda t: (t[2] == k_dim, t[0] * t[1] * t[2], t[0] * t[1]))
    nk = k_dim // tk
    a_spec = pl.BlockSpec((tk, tm), lambda i, j, k: (k, i)) if mode == "tn" else pl.BlockSpec((tm, tk), lambda i, j, k: (i, k))
    if isinstance(b, Sharded):
        b_spec = b.spec(tn, tk, lambda i, j, k: (j, k)) if mode == "nt" else b.spec(tk, tn, lambda i, j, k: (k, j))
        b = b.arr
    else:
        b_spec = pl.BlockSpec((tn, tk), lambda i, j, k: (j, k)) if mode == "nt" else pl.BlockSpec((tk, tn), lambda i, j, k: (k, j))
    o_spec = pl.BlockSpec((tm, tn), lambda i, j, k: (i, j))
    out_spec, out_shape = (o_spec, SDS((m, n), out_dtype)) if out is None else (out.spec(tm, tn, lambda i, j, k: (i, j)), out.arr)
    ca, cb = {"nn": (1, 0), "nt": (1, 1), "tn": (0, 0)}[mode]
    n_in = 2 + (add is not None)

    def finish(refs, o_ref, r):
        if add is not None:
            r = r + refs[2][...]
        o_ref[...] = r.astype(o_ref.dtype)

    def kern_whole(*refs):
        finish(refs, refs[-1], _dg(refs[0][...], refs[1][...], ca, cb))

    def kern_cut(*refs):
        o_ref, acc = refs[-2], refs[-1]
        k = pl.program_id(2)

        @pl.when(k == 0)
        def _():
            acc[...] = jnp.zeros_like(acc)

        acc[...] += _dg(refs[0][...], refs[1][...], ca, cb)

        @pl.when(k == nk - 1)
        def _():
            finish(refs, o_ref, acc[...])

    ins, specs = [a, b], [a_spec, b_spec]
    if add is not None:
        ins.append(add)
        specs.append(o_spec)
    if into is not None:
        ins.append(into)
        specs.append(ANY)
    return hosted_call(kern_whole if nk == 1 else kern_cut, name=name, grid=(m // tm, n // tn, nk), in_specs=specs, out_specs=[out_spec],
                       out_shape=[out_shape], operands=ins, scratch_shapes=[] if nk == 1 else [pltpu.VMEM((tm, tn), F32)],
                       aliases={} if into is None else {n_in: 0}, plans=plans)[0]


SCAN_UNROLL = 8


def _cmul(ar, ai, br, bi):
    return ar * br - ai * bi, ar * bi + ai * br


def _sub_shift(x, down):
    row = lax.broadcasted_iota(jnp.int32, x.shape, 0)
    if down:
        return jnp.where(row == 0, 0.0, pltpu.roll(x, 1, 0))
    return jnp.where(row == SEG - 1, 0.0, pltpu.roll(x, SEG - 1, 0))


def _pow_seg_len(ar, ai):
    for _ in range(int(math.log2(SEG_LEN))):
        ar, ai = _cmul(ar, ai, ar, ai)
    return ar, ai


def _scan_in_place(sr, si, a_re, a_im):
    lanes = sr.shape[1]
    ar = jnp.broadcast_to(a_re, (SEG, lanes))
    ai = jnp.broadcast_to(a_im, (SEG, lanes))
    zero = jnp.zeros((SEG, lanes), F32)

    def local(i, carry):
        rows = pl.ds(pl.multiple_of(i * SEG, SEG), SEG)
        mr, mi = _cmul(ar, ai, carry[0], carry[1])
        nr, ni = mr + sr[rows, :], mi + si[rows, :]
        sr[rows, :] = nr
        si[rows, :] = ni
        return nr, ni

    fr, fi = lax.fori_loop(0, SEG_LEN, local, (zero, zero), unroll=SCAN_UNROLL)
    pr, pi = _pow_seg_len(ar, ai)
    ir, ii = zero, zero
    for _ in range(SEG - 1):
        mr, mi = _cmul(pr, pi, ir, ii)
        ir, ii = _sub_shift(mr + fr, True), _sub_shift(mi + fi, True)

    def carry_in(i, pw):
        rows = pl.ds(pl.multiple_of(i * SEG, SEG), SEG)
        cr, ci = _cmul(pw[0], pw[1], ir, ii)
        sr[rows, :] += cr
        si[rows, :] += ci
        return _cmul(pw[0], pw[1], ar, ai)

    lax.fori_loop(0, SEG_LEN, carry_in, (ar, ai), unroll=SCAN_UNROLL)


S5_LANES = 8 * SP
S5_BLOCKS = SN // S5_LANES


def _s5_specs():
    u_spec = pl.BlockSpec((L, 8 * SC), lambda j: (0, j))
    s_spec = pl.BlockSpec((L, S5_LANES), lambda j: (0, j))
    wb_spec = pl.BlockSpec((S5_LANES, 8 * SC), lambda j: (j, 0))
    wc_spec = pl.BlockSpec((8 * SC, S5_LANES), lambda j: (j, 0))
    a_spec = pl.BlockSpec((1, S5_LANES), lambda j: (0, j))
    d_spec = pl.BlockSpec((1, 8 * SC), lambda j: (0, j))
    return u_spec, s_spec, wb_spec, wc_spec, a_spec, d_spec


def s5_forward(proj, wb_re, wb_im, wc_re, wc_im, a_re, a_im, d, plans=()):
    def kern(u_ref, wbr, wbi, wcr, wci, ar, ai, d_ref, sr_out, si_out, g_ref, sr, si):
        u = _wide(u_ref[...])
        sr[...], si[...] = fn_s5_bu(u, wbr[...], wbi[...])
        _scan_in_place(sr, si, ar[...], ai[...])
        g_ref[...] = fn_s5_out(sr[...], si[...], u, d_ref[...], wcr[...], wci[...])[0].astype(g_ref.dtype)
        sr_out[...] = sr[...].astype(sr_out.dtype)
        si_out[...] = si[...].astype(si_out.dtype)

    u_spec, s_spec, wb_spec, wc_spec, a_spec, d_spec = _s5_specs()
    return hosted_call(kern, name="s5_forward", grid=(S5_BLOCKS,), in_specs=[u_spec, wb_spec, wb_spec, wc_spec, wc_spec, a_spec, a_spec, d_spec],
                       out_specs=[s_spec, s_spec, u_spec], out_shape=[SDS((L, SN), BF16)] * 2 + [SDS((L, PW), BF16)],
                       operands=[proj, wb_re, wb_im, wc_re, wc_im, a_re, a_im, d], scratch_shapes=[pltpu.VMEM((L, S5_LANES), F32)] * 2,
                       plans=plans)


def _adjoint_scan_in_place(lr, li, sr, si, a_re, a_im):
    lanes = lr.shape[1]
    ar = jnp.broadcast_to(a_re, (SEG, lanes))
    ai = -jnp.broadcast_to(a_im, (SEG, lanes))
    zero = jnp.zeros((SEG, lanes), F32)

    def local(k, carry):
        i = SEG_LEN - 1 - k
        rows = pl.ds(pl.multiple_of(i * SEG, SEG), SEG)
        mr, mi = _cmul(ar, ai, carry[0], carry[1])
        nr, ni = mr + lr[rows, :], mi + li[rows, :]
        lr[rows, :] = nr
        li[rows, :] = ni
        return nr, ni

    fr, fi = lax.fori_loop(0, SEG_LEN, local, (zero, zero), unroll=SCAN_UNROLL)
    pr, pi = _pow_seg_len(ar, ai)
    ir, ii = zero, zero
    for _ in range(SEG - 1):
        mr, mi = _cmul(pr, pi, ir, ii)
        ir, ii = _sub_shift(mr + fr, False), _sub_shift(mi + fi, False)

    def fix(rows, pw):
        cr, ci = _cmul(pw[0], pw[1], ir, ii)
        tr, ti = lr[rows, :] + cr, li[rows, :] + ci
        lr[rows, :] = tr
        li[rows, :] = ti
        return tr, ti

    def grad_a(tr, ti, spr, spi, acc):
        return acc[0] + tr * spr + ti * spi, acc[1] + ti * spr - tr * spi

    def carry_in(k, c):
        i = SEG_LEN - 1 - k
        rows = pl.ds(pl.multiple_of(i * SEG, SEG), SEG)
        prev = pl.ds(pl.multiple_of((i - 1) * SEG, SEG), SEG)
        tr, ti = fix(rows, (c[0], c[1]))
        acc = grad_a(tr, ti, sr[prev, :], si[prev, :], (c[2], c[3]))
        nr, ni = _cmul(c[0], c[1], ar, ai)
        return nr, ni, acc[0], acc[1]

    pwr, pwi, accr, acci = lax.fori_loop(0, SEG_LEN - 1, carry_in, (ar, ai, zero, zero), unroll=5)
    tr, ti = fix(pl.ds(0, SEG), (pwr, pwi))
    last = pl.ds((SEG_LEN - 1) * SEG, SEG)
    accr, acci = grad_a(tr, ti, _sub_shift(sr[last, :], True), _sub_shift(si[last, :], True), (accr, acci))
    return jnp.sum(accr, axis=0, keepdims=True), jnp.sum(acci, axis=0, keepdims=True)


S5_BWD_VMEM = 58 * 2**20


def s5_backward(dg, proj, s_re, s_im, wb_re, wb_im, wc_re, wc_im, a_re, a_im, d, dproj, plans=()):
    def kern(dg_ref, u_ref, sr_in, si_in, wbr, wbi, wcr, wci, ar, ai, d_ref, _, du_ref, dd_ref, dwcr, dwci, dwbr, dwbi, dar, dai, lr, li, sr, si):
        u = _wide(u_ref[...])
        sr[...], si[...] = _wide(sr_in[...]), _wide(si_in[...])
        _, vjp_out = jax.vjp(fn_s5_out, sr[...], si[...], u, d_ref[...], wcr[...], wci[...])
        lr[...], li[...], du_out, dd_ref[...], dwcr[...], dwci[...] = vjp_out((_wide(dg_ref[...]),))
        dar[...], dai[...] = _adjoint_scan_in_place(lr, li, sr, si, ar[...], ai[...])
        _, vjp_in = jax.vjp(fn_s5_bu, u, wbr[...], wbi[...])
        du_in, dwbr[...], dwbi[...] = vjp_in((lr[...], li[...]))
        du_ref[...] = (du_out + du_in).astype(du_ref.dtype)

    u_spec, s_spec, wb_spec, wc_spec, a_spec, d_spec = _s5_specs()
    outs = [(SDS(dproj.shape, dproj.dtype), u_spec), (SDS(d.shape, F32), d_spec), (SDS(wc_re.shape, F32), wc_spec), (SDS(wc_im.shape, F32), wc_spec),
            (SDS(wb_re.shape, F32), wb_spec), (SDS(wb_im.shape, F32), wb_spec), (SDS(a_re.shape, F32), a_spec), (SDS(a_im.shape, F32), a_spec)]
    return hosted_call(kern, name="s5_backward", grid=(S5_BLOCKS,),
                       in_specs=[u_spec, u_spec, s_spec, s_spec, wb_spec, wb_spec, wc_spec, wc_spec, a_spec, a_spec, d_spec, ANY],
                       out_specs=[sp for _, sp in outs], out_shape=[sd for sd, _ in outs], aliases={11: 0},
                       operands=[dg, proj, s_re, s_im, wb_re, wb_im, wc_re, wc_im, a_re, a_im, d, dproj],
                       scratch_shapes=[pltpu.VMEM((L, S5_LANES), F32)] * 4,
                       cparams=pltpu.CompilerParams(vmem_limit_bytes=S5_BWD_VMEM), plans=plans)


def fn_rms(x, g):
    return (rms(x, g),)


def fn_s5_disc(lre, lim, ls):
    step = jnp.exp(ls)
    e = jnp.exp(lre * step)
    a_re, a_im = e * jnp.cos(lim * step), e * jnp.sin(lim * step)
    den = lre * lre + lim * lim
    nr, ni = a_re - 1.0, a_im
    return a_re, a_im, (nr * lre + ni * lim) / den, (ni * lre - nr * lim) / den


def _group_mask(rows, cols, row_div, col_div):
    r = lax.broadcasted_iota(jnp.int32, (rows, cols), 0) // row_div % 8
    c = lax.broadcasted_iota(jnp.int32, (rows, cols), 1) // col_div
    return r == c


def _spread(x, mask):
    w = x.shape[1]
    copy = (lax.broadcasted_iota(jnp.int32, (w, 8 * w), 1) % w == lax.broadcasted_iota(jnp.int32, (w, 8 * w), 0)).astype(F32)
    return jnp.where(mask, jnp.dot(x, copy, precision=lax.Precision.HIGHEST, preferred_element_type=F32), 0.0)


def fn_s5_bmat(b_re, b_im, coef_re, coef_im):
    mask = _group_mask(b_re.shape[0], 8 * SC, SP, SC)
    return _spread(coef_re * b_re - coef_im * b_im, mask), _spread(coef_re * b_im + coef_im * b_re, mask)


def fn_s5_cmat(c_re, c_im):
    mask = _group_mask(c_re.shape[0], 8 * SP, SC, SP)
    return _spread(c_re, mask), _spread(c_im, mask)


def fn_s5_bu(u, wb_re, wb_im):
    return mm_nt(u, wb_re), mm_nt(u, wb_im)


def fn_s5_out(sr, si, u, d, wc_re, wc_im):
    y = mm_nt(sr, wc_re) - mm_nt(si, wc_im) + d * u
    return (jax.nn.gelu(y),)


def fn_merge_glu(z, mo, gate):
    yg = z[:, :PW] * jax.nn.sigmoid(z[:, PW:])
    return (jnp.concatenate([yg, mo], axis=1) * silu(gate),)


def fn_merge(prim, mo, gate):
    return (jnp.concatenate([prim, mo], axis=1) * silu(gate),)


def fn_mem_k(kv, g):
    return (jnp.concatenate([rms(kv[:, h * XHD:(h + 1) * XHD], g) for h in range(XH)], axis=1),)


def fn_mem_attn(xq, kn, v, g):
    outs = []
    for h in range(XH):
        sl = slice(h * XHD, (h + 1) * XHD)
        p = softmax_rows(mm_nt(rms(xq[:, sl], g), kn[:, sl]) * (XHD ** -0.5))
        outs.append(mm_nn(p, v[:, sl]))
    return (jnp.concatenate(outs, axis=1),)


def _half_rms(x, g):
    lo = lax.broadcasted_iota(jnp.int32, x.shape, 1) < ROPE
    x2 = x * x
    s_lo = jnp.sum(jnp.where(lo, x2, 0.0), axis=1, keepdims=True)
    s_hi = jnp.sum(jnp.where(lo, 0.0, x2), axis=1, keepdims=True)
    return x * lax.rsqrt(jnp.where(lo, s_lo, s_hi) / ROPE + EPS) * g


def _rope(x, cos2, sin_signed):
    first = lax.broadcasted_iota(jnp.int32, x.shape, 1) % ROPE < ROPE // 2
    return x * cos2 + jnp.where(first, lane_roll(x, 128 - ROPE // 2), lane_roll(x, ROPE // 2)) * sin_signed


def fn_mla_prep(q, kv, kr, cos2, sin_signed, qnn, knn, qrn, krn):
    lo = lax.broadcasted_iota(jnp.int32, kr.shape, 1) < ROPE
    kr_pad = jnp.where(lo, _rope(_half_rms(kr, krn), cos2, sin_signed), 0.0)
    qf, kf, vs = [], [], []
    for m in range(MH // 2):
        pair = _rope(_half_rms(q[:, MH * NOPE + 128 * m:MH * NOPE + 128 * (m + 1)], qrn), cos2, sin_signed)
        for h, rope_h in ((2 * m, pair), (2 * m + 1, lane_roll(pair, ROPE))):
            qf.append(jnp.concatenate([rms(q[:, NOPE * h:NOPE * (h + 1)], qnn), jnp.where(lo, rope_h, 0.0)], axis=1))
    for h in range(MH):
        kf.append(jnp.concatenate([rms(kv[:, 256 * h:256 * h + NOPE], knn), kr_pad], axis=1))
        vs.append(kv[:, 256 * h + NOPE:256 * (h + 1)])
    return jnp.stack(qf), jnp.stack(kf), jnp.stack(vs)


ATT_TQ = 512


def _attn_scores(q, kf):
    tq = q.shape[0]
    scale = (NOPE + ROPE) ** -0.5
    own = _dg(q, kf[-tq:], 1, 1) * scale
    own = jnp.where(lax.broadcasted_iota(jnp.int32, own.shape, 1) <= lax.broadcasted_iota(jnp.int32, own.shape, 0), own, jnp.finfo(F32).min)
    return own if kf.shape[0] == tq else jnp.concatenate([_dg(q, kf[:-tq], 1, 1) * scale, own], axis=1)


def _attn_specs():
    q_spec = pl.BlockSpec((None, ATT_TQ, 256), lambda h, i: (h, i, 0))
    k_spec = pl.BlockSpec((None, L, 256), lambda h, i: (h, 0, 0))
    v_spec = pl.BlockSpec((None, L, 128), lambda h, i: (h, 0, 0))
    o_spec = pl.BlockSpec((ATT_TQ, 128), lambda h, i: (i, h))
    lse_spec = pl.BlockSpec((None, ATT_TQ, 1), lambda h, i: (h, i, 0))
    return q_spec, k_spec, v_spec, o_spec, lse_spec


def causal_attn(qf, kf, vh):
    n_tiles = L // ATT_TQ

    def kern(q_ref, k_ref, v_ref, o_ref, lse_ref):
        i = pl.program_id(1)
        for t in range(n_tiles):
            @pl.when(i == t)
            def _(t=t):
                keys = (t + 1) * ATT_TQ
                s = _attn_scores(q_ref[...], k_ref[:keys, :])
                m = jnp.max(s, axis=-1, keepdims=True)
                e = jnp.exp(s - m)
                total = jnp.sum(e, axis=-1, keepdims=True)
                o_ref[...] = (_dg(e, v_ref[:keys, :], 1, 0) / total).astype(o_ref.dtype)
                lse_ref[...] = m + jnp.log(total)

    q_spec, k_spec, v_spec, o_spec, lse_spec = _attn_specs()
    return pl.pallas_call(kern, grid=(MH, n_tiles), in_specs=[q_spec, k_spec, v_spec], out_specs=[o_spec, lse_spec],
                          out_shape=[SDS((L, MH * VD), F32), SDS((MH, L, 1), F32)], name="l1_attn", compiler_params=_cparams())(qf, kf, vh)


def causal_attn_bwd(qf, kf, vh, out, lse, dout):
    n_tiles = L // ATT_TQ
    scale = (NOPE + ROPE) ** -0.5

    def kern(q_ref, k_ref, v_ref, o_ref, lse_ref, do_ref, dq_ref, dk_ref, dv_ref):
        i = pl.program_id(1)

        @pl.when(i == 0)
        def _():
            dk_ref[...] = jnp.zeros_like(dk_ref)
            dv_ref[...] = jnp.zeros_like(dv_ref)

        for t in range(n_tiles):
            @pl.when(i == t)
            def _(t=t):
                keys = (t + 1) * ATT_TQ
                q, k, v, do = q_ref[...], k_ref[:keys, :], v_ref[:keys, :], do_ref[...]
                p = jnp.exp(_attn_scores(q, k) - lse_ref[...])
                delta = jnp.sum(do * _wide(o_ref[...]), axis=-1, keepdims=True)
                dv_ref[:keys, :] += _dg(p, do, 0, 0)
                ds = p * (_dg(do, v, 1, 1) - delta) * scale
                dq_ref[...] = _dg(ds, k, 1, 0)
                dk_ref[:keys, :] += _dg(ds, q, 0, 0)

    q_spec, k_spec, v_spec, o_spec, lse_spec = _attn_specs()
    return pl.pallas_call(kern, grid=(MH, n_tiles), in_specs=[q_spec, k_spec, v_spec, o_spec, lse_spec, o_spec],
                          out_specs=[q_spec, k_spec, v_spec], out_shape=[SDS(qf.shape, F32), SDS(kf.shape, F32), SDS(vh.shape, F32)],
                          name="l1_attn_bwd", compiler_params=_cparams())(qf, kf, vh, out, lse, dout)


def loss_and_grad(y, target, tl=256):
    def kern(y_ref, t_ref, dy_ref, loss_ref):
        d = y_ref[...] - t_ref[...]
        dy_ref[...] = d / D

        @pl.when(pl.program_id(0) == 0)
        def _():
            loss_ref[...] = jnp.zeros_like(loss_ref)

        loss_ref[...] += 0.5 * jnp.sum(jnp.sum(d * d, axis=1, keepdims=True), axis=0, keepdims=True) / D

    return pl.pallas_call(kern, grid=(L // tl,), in_specs=[rspec(tl, D), rspec(tl, D)], out_specs=[rspec(tl, D), cspec((1, 1))],
                          out_shape=[SDS((L, D), F32), SDS((1, 1), F32)], name="loss", compiler_params=_cparams())(y, target)


def adamw(name, w, g, m, v):
    rows, cols = w.shape
    block_row_bytes = 7 * 2 * 4 * max(cols, 128)
    tr = _row_tile(rows, min(2048, ADAMW_VMEM // block_row_bytes // 8 * 8), 8)

    def kern(w_ref, g_ref, m_ref, v_ref, d_ref, nm_ref, nv_ref):
        gg = g_ref[...]
        nm = ADAM_B1 * m_ref[...] + (1.0 - ADAM_B1) * gg
        nv = ADAM_B2 * v_ref[...] + (1.0 - ADAM_B2) * jnp.square(gg)
        m_hat = nm / (1.0 - ADAM_B1 ** ADAM_STEP)
        v_hat = nv / (1.0 - ADAM_B2 ** ADAM_STEP)
        d_ref[...] = -ADAM_LR * (m_hat / (jnp.sqrt(v_hat) + ADAM_EPS) + ADAM_WD * w_ref[...])
        nm_ref[...] = nm
        nv_ref[...] = nv

    spec = rspec(tr, cols)
    return hosted_call(kern, name=name, grid=(rows // tr,), in_specs=[spec] * 4, out_specs=[spec] * 3,
                       out_shape=[SDS((rows, cols), F32)] * 3, operands=[w, g, m, v])


def _row_tile(rows, cap=512, unit=16):
    return max(t for t in range(unit, cap + 1, unit) if rows % t == 0)


def _place():
    x, y, c = lax.axis_index("x"), lax.axis_index("y"), lax.axis_index("c")
    return x, y, c, [(1 - x, y), (x, 1 - y), (1 - x, 1 - y)]


def _row_chunks(rows, n, dtype):
    unit = 32 // jnp.dtype(dtype).itemsize
    base, extra = divmod(rows // unit, n)
    out, start = [], 0
    for k in range(n):
        size = (base + (k < extra)) * unit
        if size:
            out.append((start, size))
            start += size
    assert start == rows, (rows, unit)
    return out


PIECE_BYTES = 1 << 20


def _pieces(shapes_dtypes, rows_of):
    out = []
    for b, (shape, dtype) in enumerate(shapes_dtypes):
        rows = rows_of(shape)
        n = max(1, min(4, rows * shape[-1] * jnp.dtype(dtype).itemsize // PIECE_BYTES))
        out += [(b, st, sz) for st, sz in _row_chunks(rows, n, dtype)]
    return out


def all_gather_chips(name, shards):
    nb = len(shards)
    pieces = _pieces([(s.shape, s.dtype) for s in shards], lambda shape: shape[0] // 2)
    n = len(pieces)

    def body(*refs):
        x_refs, out_refs, send_sems, recv_sems = refs[:nb], refs[nb:2 * nb], refs[2 * nb], refs[2 * nb + 1]
        x, y, c, chips = _place()
        sibling = (x, y, 1 - c)
        mine = 2 * x + y

        def copy(sem, chip, cc, k, to, from_input=False):
            b, st, sz = pieces[k]
            rows_k = pl.ds(cc * (x_refs[b].shape[0] // 2) + st, sz)
            dst = out_refs[b].at[chip, rows_k, :]
            return pltpu.make_async_remote_copy(src_ref=x_refs[b].at[rows_k, :] if from_input else dst, dst_ref=dst,
                                                send_sem=send_sems.at[sem], recv_sem=recv_sems.at[sem], device_id=to, device_id_type=MESH_ID)

        order = [(k, j, 2 * cx + cy, (cx, cy, c)) for k in range(n) for j, (cx, cy) in enumerate(chips)]
        first = [copy(j * n + k, mine, c, k, to, from_input=True) for k, j, _, to in order]
        for cp in first:
            cp.start()
        passed = []
        for k, j, chip, _ in order:
            copy(j * n + k, chip, c, k, sibling).wait_recv()
            passed.append(copy((3 + j) * n + k, chip, c, k, sibling))
            passed[-1].start()
        for k, j, chip, _ in order:
            copy((3 + j) * n + k, chip, 1 - c, k, sibling).wait_recv()
        for cp in first + passed:
            cp.wait_send()

    return pl.pallas_call(body, in_specs=[ANY] * nb, out_specs=[ANY] * nb, out_shape=[SDS((4,) + s.shape, s.dtype) for s in shards],
                          scratch_shapes=_dma_sems(6 * n), name=name)(*shards)


def plan_gather_ici(shards):
    pieces = _pieces([(s.shape, s.dtype) for s in shards], lambda shape: shape[0] // 2)
    n = len(pieces)

    def copies(x_refs, out_refs, send_sems, recv_sems):
        x, y, c, chips = _place()
        mine = 2 * x + y

        def copy(j, k, chip, to, from_input):
            b, st, sz = pieces[k]
            rows_k = pl.ds(c * (x_refs[b].shape[0] // 2) + st, sz)
            dst = out_refs[b].at[chip, rows_k, :]
            return pltpu.make_async_remote_copy(src_ref=x_refs[b].at[rows_k, :] if from_input else dst, dst_ref=dst, send_sem=send_sems.at[j * n + k],
                                                recv_sem=recv_sems.at[j * n + k], device_id=to, device_id_type=MESH_ID)

        order = [(k, j, 2 * cx + cy, (cx, cy, c)) for k in range(n) for j, (cx, cy) in enumerate(chips)]
        return [copy(j, k, mine, to, True) for k, j, _, to in order], [copy(j, k, chip, to, False) for k, j, chip, to in order]

    return Plan(shards, [SDS((4,) + s.shape, s.dtype) for s in shards], {}, 3 * n, copies)


def plan_gather_pass(gathered):
    pieces = _pieces([(g.shape[1:], g.dtype) for g in gathered], lambda shape: shape[0] // 2)
    n = len(pieces)

    def copies(_, out_refs, send_sems, recv_sems):
        x, y, c, chips = _place()

        def copy(j, k, chip, cc):
            b, st, sz = pieces[k]
            rows_k = out_refs[b].at[chip, pl.ds(cc * (out_refs[b].shape[1] // 2) + st, sz), :]
            return pltpu.make_async_remote_copy(src_ref=rows_k, dst_ref=rows_k, send_sem=send_sems.at[j * n + k], recv_sem=recv_sems.at[j * n + k],
                                                device_id=(x, y, 1 - c), device_id_type=MESH_ID)

        order = [(k, j, 2 * cx + cy) for k in range(n) for j, (cx, cy) in enumerate(chips)]
        return [copy(j, k, chip, c) for k, j, chip in order], [copy(j, k, chip, 1 - c) for k, j, chip in order]

    return Plan(gathered, [SDS(g.shape, g.dtype) for g in gathered], {i: i for i in range(len(gathered))}, 3 * n, copies)


def plan_pair_exchange(gs):
    pieces = _pieces([(g.shape, g.dtype) for g in gs], lambda shape: shape[1] // 2)

    def copies(g_refs, got_refs, send_sems, recv_sems):
        x, y, c, _ = _place()
        swaps = [pltpu.make_async_remote_copy(src_ref=g_refs[b].at[:, pl.ds((1 - c) * (g_refs[b].shape[1] // 2) + st, sz), :],
                                              dst_ref=got_refs[b].at[:, pl.ds(st, sz), :], send_sem=send_sems.at[k], recv_sem=recv_sems.at[k],
                                              device_id=(x, y, 1 - c), device_id_type=MESH_ID)
                 for k, (b, st, sz) in enumerate(pieces)]
        return swaps, swaps

    return Plan(gs, [SDS((g.shape[0], g.shape[1] // 2, g.shape[2]), g.dtype) for g in gs], {}, len(pieces), copies)


def plan_chip_scatter(ps):
    pieces = _pieces([(p.shape, p.dtype) for p in ps], lambda shape: shape[1])
    n = len(pieces)

    def copies(p_refs, q_refs, send_sems, recv_sems):
        x, y, c, chips = _place()
        mine = 2 * x + y

        def copy(j, k, src_slot, dst_slot, to):
            b, st, sz = pieces[k]
            return pltpu.make_async_remote_copy(src_ref=p_refs[b].at[src_slot, pl.ds(st, sz), :], dst_ref=q_refs[b].at[dst_slot, pl.ds(st, sz), :],
                                                send_sem=send_sems.at[j * n + k], recv_sem=recv_sems.at[j * n + k], device_id=to,
                                                device_id_type=MESH_ID)

        order = [(k, j, 2 * cx + cy, (cx, cy, c)) for k in range(n) for j, (cx, cy) in enumerate(chips)]
        return [copy(j, k, chip, mine, to) for k, j, chip, to in order], [copy(j, k, mine, chip, to) for k, j, chip, to in order]

    return Plan(ps, [SDS(p.shape, p.dtype) for p in ps], {}, 3 * n, copies)


def plan_pair_join(bufs):
    pieces = _pieces([(b.shape, b.dtype) for b in bufs], lambda shape: shape[0] // 2)

    def copies(_, out_refs, send_sems, recv_sems):
        x, y, c, _ = _place()

        def copy(k, cc):
            b, st, sz = pieces[k]
            rows_k = out_refs[b].at[pl.ds(cc * (out_refs[b].shape[0] // 2) + st, sz), :]
            return pltpu.make_async_remote_copy(src_ref=rows_k, dst_ref=rows_k, send_sem=send_sems.at[k], recv_sem=recv_sems.at[k],
                                                device_id=(x, y, 1 - c), device_id_type=MESH_ID)

        return [copy(k, c) for k in range(len(pieces))], [copy(k, 1 - c) for k in range(len(pieces))]

    return Plan(bufs, [SDS(b.shape, b.dtype) for b in bufs], {i: i for i in range(len(bufs))}, len(pieces), copies)


def run_plan(name, plan):
    hosted_call(lambda: None, name=name, grid=(1,), in_specs=[], out_specs=[], out_shape=[], operands=[], plans=[plan])
    return plan.results


HBM = pl.BlockSpec(memory_space=pltpu.HBM)
SEMS = pl.BlockSpec(memory_space=pltpu.SEMAPHORE)
SPLIT_PARAMS = dict(has_side_effects=pltpu.SideEffectType.DATAFLOW_SIDE_EFFECTING)


def _plan_buffers(plan):
    in_place = {o: i for i, o in plan.aliases.items()}
    bufs = [pltpu.with_memory_space_constraint(a, pltpu.HBM) for a in plan.operands]
    where = []
    for o, sd in enumerate(plan.out_shape):
        if o in in_place:
            where.append(in_place[o])
        else:
            where.append(len(bufs))
            bufs.append(pltpu.with_memory_space_constraint(lax.empty(sd.shape, sd.dtype), pltpu.HBM))
    return bufs, where


def split_start(name, plans):
    layout = [_plan_buffers(p) for p in plans]
    counts = [len(b) for b, _ in layout]
    n_buf = sum(counts)

    def body(*refs):
        sems, token = refs[n_buf:n_buf + 2 * len(plans)], refs[-1]
        pos = 0
        for k, (p, (_, where)) in enumerate(zip(plans, layout)):
            mine = refs[pos:pos + counts[k]]
            pos += counts[k]
            sends, _ = p.copies(mine[:len(p.operands)], [mine[w] for w in where], sems[2 * k], sems[2 * k + 1])
            for cp in sends:
                cp.start()
        token[...] = jnp.zeros_like(token)

    bufs = [b for bs, _ in layout for b in bs]
    res = pl.pallas_call(
        body, name=name, in_specs=[HBM] * n_buf,
        out_specs=[SEMS] * (2 * len(plans)) + [HBM] * n_buf + [pl.BlockSpec(memory_space=pltpu.VMEM)],
        out_shape=[pltpu.SemaphoreType.DMA((p.n_sems,)) for p in plans for _ in range(2)] + [pltpu.HBM(b.shape, b.dtype) for b in bufs]
        + [SDS((8, 128), F32)],
        input_output_aliases={i: 2 * len(plans) + i for i in range(n_buf)}, compiler_params=pltpu.CompilerParams(**SPLIT_PARAMS))(*bufs)
    pos = 2 * len(plans)
    for k, p in enumerate(plans):
        p.in_flight = (res[2 * k], res[2 * k + 1], list(res[pos:pos + counts[k]]), layout[k][1])
        pos += counts[k]
    return res[-1]


def split_wait(name, plan, after):
    send_sems, recv_sems, bufs, where = plan.in_flight
    n_buf = len(bufs)

    def body(*refs):
        mine = refs[:n_buf]
        sends, recvs = plan.copies(mine[:len(plan.operands)], [mine[w] for w in where], refs[n_buf], refs[n_buf + 1])
        for cp in recvs:
            cp.wait_recv()
        for cp in sends:
            cp.wait_send()

    res = pl.pallas_call(body, name=name, in_specs=[HBM] * n_buf + [SEMS, SEMS, ANY], out_specs=[HBM] * n_buf,
                         out_shape=[pltpu.HBM(b.shape, b.dtype) for b in bufs], input_output_aliases={i: i for i in range(n_buf)},
                         compiler_params=pltpu.CompilerParams(**SPLIT_PARAMS))(*bufs, send_sems, recv_sems, after)
    plan.results = [res[w] for w in where]
    return plan.results


def pair_add(name, g, got, place):
    slots, rows, cols = g.shape
    half = rows // 2
    tr = _row_tile(half)
    nb = half // tr

    def kern(_, g_ref, t_ref, o_ref):
        o_ref[...] = (g_ref[...].astype(F32) + t_ref[...].astype(F32)).astype(o_ref.dtype)

    blk = pl.BlockSpec((None, tr, cols), lambda s, i, p: (s, i, 0))
    grid_spec = pltpu.PrefetchScalarGridSpec(
        num_scalar_prefetch=1, grid=(slots, nb),
        in_specs=[pl.BlockSpec((None, tr, cols), lambda s, i, p: (s, p[1] * nb + i, 0)), blk], out_specs=blk)
    return pl.pallas_call(kern, grid_spec=grid_spec, out_shape=SDS((slots, half, cols), g.dtype), name=name,
                          compiler_params=_cparams())(place, g, got)


def chip_add(name, p, q, place):
    slots, half, cols = p.shape
    tr = _row_tile(half)
    nb = half // tr

    def kern(_, p_ref, q1, q2, q3, o_ref):
        o_ref[...] = p_ref[...].astype(F32) + q1[...].astype(F32) + q2[...].astype(F32) + q3[...].astype(F32)

    def slot(k):
        return pl.BlockSpec((None, tr, cols), lambda i, pr: ((pr[0] + k) % slots, i, 0))

    grid_spec = pltpu.PrefetchScalarGridSpec(
        num_scalar_prefetch=1, grid=(nb,), in_specs=[slot(0), slot(1), slot(2), slot(3)],
        out_specs=pl.BlockSpec((tr, cols), lambda i, pr: (pr[1] * nb + i, 0)))
    return pl.pallas_call(kern, grid_spec=grid_spec, out_shape=SDS((2 * half, cols), F32), name=name,
                          compiler_params=_cparams())(place, p, q, q, q)


def pair_adds(tag, gs, gots, place):
    return [pair_add(f"{tag}_pair_add_{i}", g, got, place) for i, (g, got) in enumerate(zip(gs, gots))]


def chip_adds(tag, pairs, qs, place):
    return [chip_add(f"{tag}_chip_add_{i}", p, q, place) for i, (p, q) in enumerate(zip(pairs, qs))]


def reduce_scatter_chips(tag, gs, place):
    pairs = pair_adds(tag, gs, run_plan(tag + "_pair_exchange", plan_pair_exchange(gs)), place)
    return run_plan(tag + "_pair_join", plan_pair_join(chip_adds(tag, pairs, run_plan(tag + "_chip_scatter", plan_chip_scatter(pairs)), place)))


BIG = [("w_out", (2, 512, 1024)), ("w_mem_kv", (2, 256, 1024)), ("s5_w_in", (1, 1024, 1024)), ("s5_w_glu", (1, 1536, 768)),
       ("mla_w_in", (1, 1024, 848)), ("mla_w_uq", (1, 512, 576)), ("mla_w_ukv", (1, 256, 768))]
SHARDED_SMALL = [("mla_q_lora_norm", (1, 128)), ("mla_kv_lora_norm", (1, 64))]
SMALL = [("ln_gain", (2, 1024)), ("mem_norm", (2, 1024)), ("xq_norm", (2, 128)), ("xk_norm", (2, 128)),
         ("s5_lambda_re", (1, 96, 64)), ("s5_lambda_im", (1, 96, 64)), ("s5_log_step", (1, 96)),
         ("s5_b_re", (1, 96, 64, 16)), ("s5_b_im", (1, 96, 64, 16)), ("s5_c_re", (1, 96, 16, 64)), ("s5_c_im", (1, 96, 16, 64)),
         ("s5_d", (1, 1536)), ("mla_q_nope_norm", (1, 128)), ("mla_k_nope_norm", (1, 128)), ("mla_q_rope_norm", (1, 64)),
         ("mla_k_rope_norm", (1, 64))]
WEIGHT_ORDER = ["ln_gain", "w_out", "mem_norm", "w_mem_kv", "xq_norm", "xk_norm", "s5_w_in", "s5_lambda_re", "s5_lambda_im",
                "s5_log_step", "s5_b_re", "s5_b_im", "s5_c_re", "s5_c_im", "s5_d", "s5_w_glu", "mla_w_in", "mla_q_lora_norm",
                "mla_kv_lora_norm", "mla_w_uq", "mla_w_ukv", "mla_q_nope_norm", "mla_k_nope_norm", "mla_q_rope_norm", "mla_k_rope_norm"]
MINOR_LAST = {"mla_w_in": (0, 2, 1), "mla_w_uq": (0, 2, 1), "s5_b_re": (0, 2, 3, 1), "s5_b_im": (0, 2, 3, 1),
              "s5_c_re": (0, 2, 3, 1), "s5_c_im": (0, 2, 3, 1)}
SMALL_FULL = SMALL + [(n, (1, 4 * s[1])) for n, s in SHARDED_SMALL]
N_SMALL = sum(math.prod(s) for _, s in SMALL_FULL)
SMALL_ROWS, SMALL_LANES = 128, 1024

PAIR_OUT, PAIR_MKV, PAIR_ROWS = 0, 512, 768


def stack_shards(w, dtype):
    pairs = [jnp.concatenate([w["w_out"][l], w["w_mem_kv"][l]], axis=0).astype(dtype) for l in range(2)]
    return ([w["s5_w_in"][0].astype(dtype)], [pairs[0], w["s5_w_glu"][0].astype(dtype)],
            [pairs[1], w["mla_w_ukv"][0].astype(dtype), w["mla_w_in"][0].astype(dtype), w["mla_w_uq"][0].astype(dtype)])


def pair_views(pair):
    return {"w_out": Sharded(pair, "row", PAIR_OUT, 512), "w_mem_kv": Sharded(pair, "row", PAIR_MKV, 256)}


def grad_views():
    pair = SDS((4, PAIR_ROWS, 1024), BF16)
    return {"w_out": Sharded(pair, "row", PAIR_OUT, 512), "w_mem_kv": Sharded(pair, "row", PAIR_MKV, 256),
            "s5_w_in": Sharded(SDS((4, 1024, 1024), BF16), "col", 0, 1024), "s5_w_glu": Sharded(SDS((4, 1536, 768), BF16), "col", 0, 1536),
            "mla_w_ukv": Sharded(SDS((4, 256, 768), BF16), "col", 0, 256)}


def cols_to_shards(full):
    return full.reshape(full.shape[0], 4, full.shape[1] // 4).transpose(1, 0, 2)


def shards_to_cols(arr):
    return arr.transpose(1, 0, 2).reshape(arr.shape[1], 4 * arr.shape[2])


def mla_in_permute(w):
    o1, o2, o3, o4 = QL, QL + KVL, QL + KVL + ROPE, QL + KVL + ROPE + XQW
    return jnp.concatenate([w[:, o4:], w[:, :o1], w[:, o3:o4], w[:, o1:o2], w[:, o2:o3],
                            jnp.zeros((w.shape[0], MLA_IN_P - MLA_IN), w.dtype)], axis=1)


def mla_in_unpermute(d):
    return jnp.concatenate([d[:, 2048:2560], d[:, 3072:3328], d[:, 3328:3392], d[:, 2560:3072], d[:, :2048]], axis=1)


def uq_permute(w):
    w3 = w.reshape(w.shape[0], MH, NOPE + ROPE)
    return jnp.concatenate([w3[:, :, :NOPE].reshape(w.shape[0], MH * NOPE), w3[:, :, NOPE:].reshape(w.shape[0], MH * ROPE)], axis=1)


def uq_unpermute(d):
    dn = d[:, :MH * NOPE].reshape(d.shape[0], MH, NOPE)
    dr = d[:, MH * NOPE:].reshape(d.shape[0], MH, ROPE)
    return jnp.concatenate([dn, dr], axis=2).reshape(d.shape[0], MH * (NOPE + ROPE))


def time_permute(a):
    return a.reshape(SEG, SEG_LEN, a.shape[-1]).transpose(1, 0, 2).reshape(L, a.shape[-1])


def time_unpermute(a):
    return a.reshape(SEG_LEN, SEG, a.shape[-1]).transpose(1, 0, 2).reshape(L, a.shape[-1])


def mem_branch_fwd(tag, mem, mem_norm, w_mem_kv, xk_norm):
    mn = row_fwd(tag + "_mem_rms", fn_rms, ML, ML, [(mem, D, 0)], [mem_norm], [(D, BF16)])[0]
    kv = matmul(tag + "_mem_kv", mn, w_mem_kv, "nn", F32)
    kn = row_fwd(tag + "_mem_knorm", fn_mem_k, ML, ML, [(kv, XQW, 0)], [xk_norm], [(XQW, F32)])[0]
    return mn, kv, kn


def mem_branch_bwd(tag, mem, mem_norm, w_mem_kv, xk_norm, mn, kv, dkn, dv, g_view, g_wide):
    dk, dxk = row_bwd(tag + "_mem_knorm_bwd", fn_mem_k, ML, ML, [(kv, XQW, 0)], [xk_norm], [(dkn, XQW, 0)], [True], [True])
    dkv = jnp.concatenate([dk, dv], axis=1)
    dmn = matmul(tag + "_mem_kv_dx", dkv, w_mem_kv, "nt", F32)
    g_wide = matmul(tag + "_mem_kv_dw", mn, dkv, "tn", out=g_view, into=g_wide)
    dmem_norm = row_bwd(tag + "_mem_rms_bwd", fn_rms, ML, ML, [(mem, D, 0)], [mem_norm], [(dmn, D, 0)], [False], [True])[0]
    return g_wide, dmem_norm, dxk


def mem_attn_fwd(tag, proj, cb, kn, kv, xq_norm):
    return row_fwd(tag + "_mem_attn", fn_mem_attn, L, 256, [(proj, XQW, cb)], [kn, kv[:, XQW:], xq_norm], [(XQW, F32)])[0]


def mem_attn_bwd(tag, proj, cb, kn, kv, xq_norm, dmo, dproj):
    place = {"cols": proj.shape[1], "cb": cb, "into": dproj, "dtype": dproj.dtype}
    return row_bwd(tag + "_mem_attn_bwd", fn_mem_attn, L, 256, [(proj, XQW, cb)], [kn, kv[:, XQW:], xq_norm], [(dmo, XQW, 0)],
                   [place], [True, True, True])


def device_step(x, mem, positions, target, small, env, hooks=None):
    hooks = hooks or {}

    def plans_for(name):
        return hooks[("plans", name)](env) if ("plans", name) in hooks else ()

    def around(when, name, last=None):
        if (when, name) in hooks:
            hooks[(when, name)](env, last)

    g = {}
    gw = grad_views()
    ln, mem_norm, xq_norm, xk_norm = small["ln_gain"], small["mem_norm"], small["xq_norm"], small["xk_norm"]

    lre, lim = small["s5_lambda_re"][0], small["s5_lambda_im"][0]
    ls = small["s5_log_step"].reshape(SG, 1)
    one = pl.BlockSpec((SG, SP), lambda i: (0, 0))
    col = pl.BlockSpec((SG, 1), lambda i: (0, 0))
    disc_ins = [(lre, one), (lim, one), (ls, col)]
    a_re, a_im, coef_re, coef_im = stage("s5_disc", fn_s5_disc, (1,), disc_ins, [(SDS((SG, SP), F32), one)] * 4)
    b_re, b_im = small["s5_b_re"].reshape(SN, SC), small["s5_b_im"].reshape(SN, SC)
    c_re, c_im = small["s5_c_re"].reshape(PW, SP), small["s5_c_im"].reshape(PW, SP)
    bmat_rows = [(b_re, SC, 0), (b_im, SC, 0), (coef_re.reshape(SN, 1), 1, 0), (coef_im.reshape(SN, 1), 1, 0)]
    wb_re, wb_im = row_fwd("s5_bmat", fn_s5_bmat, SN, 512, bmat_rows, [], [(128, F32)] * 2)
    cmat_rows = [(c_re, SP, 0), (c_im, SP, 0)]
    wc_re, wc_im = row_fwd("s5_cmat", fn_s5_cmat, PW, 128, cmat_rows, [], [(512, F32)] * 2)
    a_re_v, a_im_v = a_re.reshape(1, SN), a_im.reshape(1, SN)
    s5_d = small["s5_d"]

    xp = time_permute(x)
    h0 = row_fwd("l0_rms", fn_rms, L, 256, [(xp, D, 0)], [ln[0:1]], [(D, BF16)])[0]
    around("before", "l0_in", wb_re)
    w_in0 = Sharded(env["in0"], "col", 0, 1024)
    proj0 = matmul("l0_in", h0, w_in0, "nn")
    s_re, s_im, g0 = s5_forward(proj0, wb_re, wb_im, wc_re, wc_im, a_re_v, a_im_v, s5_d, plans=plans_for("s5_forward"))
    around("before", "l0_glu", g0)
    w0 = dict(pair_views(env["pair0"]), s5_w_glu=Sharded(env["glu"], "col", 0, 1536))
    z0 = matmul("l0_glu", g0, w0["s5_w_glu"], "nn", plans=plans_for("l0_glu"))
    mn0, kv0, kn0 = mem_branch_fwd("l0", mem, mem_norm[0:1], w0["w_mem_kv"], xk_norm[0:1])
    mo0 = mem_attn_fwd("l0", proj0, 3, kn0, kv0, xq_norm[0:1])
    o0 = row_fwd("l0_merge", fn_merge_glu, L, 256, [(z0, 2 * PW, 0), (mo0, XQW, 0), (proj0, BW, 1)], [], [(BW, BF16)])[0]
    around("before", "l0_out", o0)
    x1p = matmul("l0_out", o0, w0["w_out"], "nn", F32, add=xp, plans=plans_for("l0_out"))
    around("after", "l0_out", x1p)
    x1 = time_unpermute(x1p)

    w1 = dict(pair_views(env["pair1"]), mla_w_ukv=Sharded(env["ukv"], "col", 0, 256))
    w_in1, w_uq = env["w_in1"], env["w_uq"]
    h1 = row_fwd("l1_rms", fn_rms, L, 256, [(x1, D, 0)], [ln[1:2]], [(D, BF16)])[0]
    proj1 = matmul("l1_in", h1, w_in1, "nn")
    qln, kvln = env["q_lora_norm"].reshape(1, QL), env["kv_lora_norm"].reshape(1, KVL)
    cqn = row_fwd("l1_q_lora_rms", fn_rms, L, 256, [(proj1, QL, 4)], [qln], [(QL, BF16)])[0]
    ckvn = row_fwd("l1_kv_lora_rms", fn_rms, L, 256, [(proj1, KVL, 12)], [kvln], [(KVL, BF16)])[0]
    q = matmul("l1_uq", cqn, w_uq, "nn")
    kv = matmul("l1_ukv", ckvn, w1["mla_w_ukv"], "nn")
    inv_freq = ROPE_THETA ** (-jnp.arange(ROPE // 2, dtype=F32) / (ROPE // 2))
    ang = positions.astype(F32)[:, None] * inv_freq
    cos2 = jnp.tile(jnp.cos(ang), (1, 4))
    sin_signed = jnp.tile(jnp.concatenate([-jnp.sin(ang), jnp.sin(ang)], axis=1), (1, 2))
    qnn, knn = small["mla_q_nope_norm"], small["mla_k_nope_norm"]
    qrn, krn = jnp.tile(small["mla_q_rope_norm"], (1, 2)), jnp.tile(small["mla_k_rope_norm"], (1, 2))
    tp = 256
    prep_ins = [(q, rspec(tp, MH * (NOPE + ROPE))), (kv, rspec(tp, MH * 256)), (proj1, rspec(tp, 128, 26)),
                (cos2, rspec(tp, 128)), (sin_signed, rspec(tp, 128))] + [(a, cspec((1, 128))) for a in (qnn, knn, qrn, krn)]
    hq_spec = pl.BlockSpec((MH, tp, 256), lambda i: (0, i, 0))
    hv_spec = pl.BlockSpec((MH, tp, 128), lambda i: (0, i, 0))
    qf, kf, vh = stage("l1_mla_prep", fn_mla_prep, (L // tp,), prep_ins,
                       [(SDS((MH, L, 256), BF16), hq_spec), (SDS((MH, L, 256), BF16), hq_spec), (SDS((MH, L, 128), BF16), hv_spec)])
    attn, attn_lse = causal_attn(qf, kf, vh)
    mn1, kv1, kn1 = mem_branch_fwd("l1", mem, mem_norm[1:2], w1["w_mem_kv"], xk_norm[1:2])
    mo1 = mem_attn_fwd("l1", proj1, 5, kn1, kv1, xq_norm[1:2])
    o1 = row_fwd("l1_merge", fn_merge, L, 256, [(attn, PW, 0), (mo1, XQW, 0), (proj1, BW, 0)], [], [(BW, BF16)])[0]
    x2 = matmul("l1_out", o1, w1["w_out"], "nn", F32, add=x1)
    dx2, loss = loss_and_grad(x2, target)

    do1 = matmul("l1_out_dx", dx2, w1["w_out"], "nt")
    g_pair1 = matmul("l1_out_dw", o1, dx2, "tn", out=gw["w_out"])
    dattn, dmo1, dproj1 = row_bwd("l1_merge_bwd", fn_merge, L, 256, [(attn, PW, 0), (mo1, XQW, 0), (proj1, BW, 0)], [],
                                  [(do1, BW, 0)], [True, True, {"cols": MLA_IN_P, "cb": 0, "dtype": BF16}], [])
    dproj1, dkn1, dv1, dxqn1 = mem_attn_bwd("l1", proj1, 5, kn1, kv1, xq_norm[1:2], dmo1, dproj1)
    env["g_pair1"], dmem_norm1, dxk1 = mem_branch_bwd("l1", mem, mem_norm[1:2], w1["w_mem_kv"], xk_norm[1:2], mn1, kv1, dkn1, dv1,
                                                      gw["w_mem_kv"], g_pair1)
    dqf, dkf, dvh = causal_attn_bwd(qf, kf, vh, attn, attn_lse, dattn)
    prep_diffs = [("row", SDS((L, MH * (NOPE + ROPE)), BF16), rspec(tp, MH * (NOPE + ROPE))), ("row", SDS((L, MH * 256), BF16), rspec(tp, MH * 256)),
                  ("row", SDS((L, MLA_IN_P), BF16), rspec(tp, 128, 26), {"into": dproj1}), None, None] + [("acc", (0,))] * 4
    dq, dkv, dproj1, dqnn, dknn, dqrn, dkrn = stage_bwd("l1_mla_prep_bwd", fn_mla_prep, (L // tp,), prep_ins,
                                                        [(dqf, hq_spec), (dkf, hq_spec), (dvh, hv_spec)], prep_diffs)
    dcqn = matmul("l1_uq_dx", dq, w_uq, "nt")
    env["dw_uq"] = matmul("l1_uq_dw", cqn, dq, "tn")
    dckvn = matmul("l1_ukv_dx", dkv, w1["mla_w_ukv"], "nt")
    env["g_ukv"] = matmul("l1_ukv_dw", ckvn, dkv, "tn", out=gw["mla_w_ukv"])
    dproj1, dqln = row_bwd("l1_q_lora_rms_bwd", fn_rms, L, 256, [(proj1, QL, 4)], [qln], [(dcqn, QL, 0)],
                           [{"cols": MLA_IN_P, "cb": 4, "into": dproj1, "dtype": BF16}], [True])
    dproj1, dkvln = row_bwd("l1_kv_lora_rms_bwd", fn_rms, L, 256, [(proj1, KVL, 12)], [kvln], [(dckvn, KVL, 0)],
                            [{"cols": MLA_IN_P, "cb": 12, "into": dproj1, "dtype": BF16}], [True])
    env["dw_in1"] = matmul("l1_in_dw", h1, dproj1, "tn")
    dh1 = matmul("l1_in_dx", dproj1, w_in1, "nt", plans=plans_for("l1_in_dx"))
    around("after", "l1_in_dx", dh1)
    dx1, dln1 = row_bwd("l1_rms_bwd", fn_rms, L, 256, [(x1, D, 0)], [ln[1:2]], [(dh1, D, 0)], [{"add": (dx2, D, 0)}], [True])
    dx1p = time_permute(dx1)

    do0 = matmul("l0_out_dx", dx1p, w0["w_out"], "nt", plans=plans_for("l0_out_dx"))
    g_pair0 = matmul("l0_out_dw", o0, dx1p, "tn", out=gw["w_out"])
    dz0, dmo0, dproj0 = row_bwd("l0_merge_bwd", fn_merge_glu, L, 256, [(z0, 2 * PW, 0), (mo0, XQW, 0), (proj0, BW, 1)], [],
                                [(do0, BW, 0)], [{"dtype": BF16}, True, {"cols": 2 * BW, "cb": 1, "dtype": BF16}], [])
    dproj0, dkn0, dv0, dxqn0 = mem_attn_bwd("l0", proj0, 3, kn0, kv0, xq_norm[0:1], dmo0, dproj0)
    env["g_pair0"], dmem_norm0, dxk0 = mem_branch_bwd("l0", mem, mem_norm[0:1], w0["w_mem_kv"], xk_norm[0:1], mn0, kv0, dkn0, dv0,
                                                      gw["w_mem_kv"], g_pair0)
    env["g_glu"] = matmul("l0_glu_dw", g0, dz0, "tn", out=gw["s5_w_glu"], plans=plans_for("l0_glu_dw"))
    dg0 = matmul("l0_glu_dx", dz0, w0["s5_w_glu"], "nt", plans=plans_for("l0_glu_dx"))
    around("before", "s5_backward", dg0)
    dproj0, dd, dwc_re, dwc_im, dwb_re, dwb_im, da_re, da_im = s5_backward(dg0, proj0, s_re, s_im, wb_re, wb_im, wc_re, wc_im,
                                                                           a_re_v, a_im_v, s5_d, dproj0, plans=plans_for("s5_backward"))
    around("after", "s5_backward", dd)
    env["g_in0"] = matmul("l0_in_dw", h0, dproj0, "tn", out=gw["s5_w_in"], plans=plans_for("l0_in_dw"))
    dh0 = matmul("l0_in_dx", dproj0, w_in0, "nt", plans=plans_for("l0_in_dx"))
    around("after", "l0_in_dx", dh0)
    dxp, dln0 = row_bwd("l0_rms_bwd", fn_rms, L, 256, [(xp, D, 0)], [ln[0:1]], [(dh0, D, 0)], [{"add": (dx1p, D, 0)}], [True])
    grad_x = time_unpermute(dxp)

    db_re, db_im, dcoef_re, dcoef_im = row_bwd("s5_bmat_bwd", fn_s5_bmat, SN, 512, bmat_rows, [], [(dwb_re, 128, 0), (dwb_im, 128, 0)],
                                               [True] * 4, [], plans=plans_for("s5_bmat_bwd"))
    dc_re, dc_im = row_bwd("s5_cmat_bwd", fn_s5_cmat, PW, 128, cmat_rows, [], [(dwc_re, 512, 0), (dwc_im, 512, 0)], [True] * 2, [],
                           plans=plans_for("s5_cmat_bwd"))
    disc_cts = [(da_re.reshape(SG, SP), one), (da_im.reshape(SG, SP), one), (dcoef_re.reshape(SG, SP), one), (dcoef_im.reshape(SG, SP), one)]
    dlre, dlim, dls = stage_bwd("s5_disc_bwd", fn_s5_disc, (1,), disc_ins, disc_cts, [("acc", (0,))] * 3)

    g["ln_gain"] = jnp.concatenate([dln0, dln1], axis=0)
    g["mem_norm"] = jnp.concatenate([dmem_norm0, dmem_norm1], axis=0)
    g["xq_norm"] = jnp.concatenate([dxqn0, dxqn1], axis=0)
    g["xk_norm"] = jnp.concatenate([dxk0, dxk1], axis=0)
    g["s5_lambda_re"], g["s5_lambda_im"], g["s5_log_step"] = dlre, dlim, dls
    g["s5_b_re"], g["s5_b_im"], g["s5_c_re"], g["s5_c_im"] = db_re, db_im, dc_re, dc_im
    g["s5_d"] = dd
    g["mla_q_lora_norm"], g["mla_kv_lora_norm"] = dqln, dkvln
    g["mla_q_nope_norm"], g["mla_k_nope_norm"] = dqnn, dknn
    g["mla_q_rope_norm"] = dqrn[:, :ROPE] + dqrn[:, ROPE:]
    g["mla_k_rope_norm"] = dkrn[:, :ROPE] + dkrn[:, ROPE:]
    return loss, grad_x, g


def kernel(x, mem, positions, ln_gain, w_out, mem_norm, w_mem_kv, xq_norm, xk_norm, s5_w_in, s5_lambda_re, s5_lambda_im, s5_log_step, s5_b_re, s5_b_im, s5_c_re, s5_c_im, s5_d, s5_w_glu, mla_w_in, mla_q_lora_norm, mla_kv_lora_norm, mla_w_uq, mla_w_ukv, mla_q_nope_norm, mla_k_nope_norm, mla_q_rope_norm, mla_k_rope_norm, loss_target, m_ln_gain, m_w_out, m_mem_norm, m_w_mem_kv, m_xq_norm, m_xk_norm, m_s5_w_in, m_s5_lambda_re, m_s5_lambda_im, m_s5_log_step, m_s5_b_re, m_s5_b_im, m_s5_c_re, m_s5_c_im, m_s5_d, m_s5_w_glu, m_mla_w_in, m_mla_q_lora_norm, m_mla_kv_lora_norm, m_mla_w_uq, m_mla_w_ukv, m_mla_q_nope_norm, m_mla_k_nope_norm, m_mla_q_rope_norm, m_mla_k_rope_norm, v_ln_gain, v_w_out, v_mem_norm, v_w_mem_kv, v_xq_norm, v_xk_norm, v_s5_w_in, v_s5_lambda_re, v_s5_lambda_im, v_s5_log_step, v_s5_b_re, v_s5_b_im, v_s5_c_re, v_s5_c_im, v_s5_d, v_s5_w_glu, v_mla_w_in, v_mla_q_lora_norm, v_mla_kv_lora_norm, v_mla_w_uq, v_mla_w_ukv, v_mla_q_nope_norm, v_mla_k_nope_norm, v_mla_q_rope_norm, v_mla_k_rope_norm):
    args = dict(locals())
    wts = {n: args[n] for n in WEIGHT_ORDER}
    mom = {n: args["m_" + n] for n in WEIGHT_ORDER}
    var = {n: args["v_" + n] for n in WEIGHT_ORDER}

    chip = 2 * lax.axis_index("x") + lax.axis_index("y")
    place = jnp.stack([chip, lax.axis_index("c")]).astype(jnp.int32)

    def own_slot(gathered, shards):
        return [lax.dynamic_update_slice(g, s[None], (chip, 0, 0)) for g, s in zip(gathered, shards)]

    groups = list(stack_shards(wts, BF16))
    groups[2].append(jnp.concatenate([mla_q_lora_norm, jnp.pad(mla_kv_lora_norm, ((0, 0), (0, 64))), jnp.zeros((14, 128), F32)], axis=0))
    over_ici = [plan_gather_ici(shards) for shards in groups]
    _SCHEDULE_BEHIND.clear()
    schedule_behind(split_start("gather_start", over_ici))
    env, hooks, passed_on = {}, {}, {}

    def arrived(k, after, pass_now):
        passing = plan_gather_pass(split_wait(f"gather_wait_{k}", over_ici[k], after))
        passed_on[k] = passing
        return own_slot(run_plan(f"gather_pass_{k}", passing), groups[k]) if pass_now else None

    def need_in0(env, last):
        env["in0"], = arrived(0, last, True)

    def need_layer0(env, last):
        env["pair0"], env["glu"] = arrived(1, last, True)

    def need_layer1(env, last):
        arrived(2, last, False)

    def layer1_weights(env, last):
        pair1, ukv, in1, uq, norms = own_slot(passed_on[2].results, groups[2])
        env.update(pair1=pair1, ukv=ukv, w_in1=mla_in_permute(shards_to_cols(in1)), w_uq=uq_permute(shards_to_cols(uq)),
                   q_lora_norm=norms[:, 0, :], kv_lora_norm=norms[:, 1, :64])

    hooks["before", "l0_in"], hooks["before", "l0_glu"], hooks["before", "l0_out"] = need_in0, need_layer0, need_layer1
    hooks["plans", "l0_out"], hooks["after", "l0_out"] = (lambda env: [passed_on[2]]), layer1_weights

    rs = {}

    def swap(k, gs):
        rs[k, "g"], rs[k, "swap"] = gs, plan_pair_exchange(gs)
        return rs[k, "swap"]

    def start_scatter(k):
        rs[k, "pairs"] = pair_adds(f"rs{k}", rs[k, "g"], rs[k, "swap"].results, place)
        rs[k, "scatter"] = plan_chip_scatter(rs[k, "pairs"])
        schedule_behind(split_start(f"rs{k}_scatter_start", [rs[k, "scatter"]]))

    def join(k, after):
        rs[k, "join"] = plan_pair_join(chip_adds(f"rs{k}", rs[k, "pairs"], split_wait(f"rs{k}_scatter_wait", rs[k, "scatter"], after), place))
        return rs[k, "join"]

    hooks["plans", "l1_in_dx"] = lambda env: [swap(1, [env["g_pair1"], env["g_ukv"], cols_to_shards(mla_in_unpermute(env["dw_in1"])).astype(BF16),
                                                        cols_to_shards(uq_unpermute(env["dw_uq"])).astype(BF16)])]
    hooks["after", "l1_in_dx"] = lambda env, last: start_scatter(1)
    hooks["plans", "l0_glu_dx"] = lambda env: [swap(0, [env["g_pair0"], env["g_glu"]])]

    def before_s5_backward(env, last):
        start_scatter(0)
        env["join1"] = join(1, last)

    hooks["before", "s5_backward"] = before_s5_backward
    hooks["plans", "s5_backward"] = lambda env: [env["join1"]]
    hooks["after", "s5_backward"] = lambda env, last: env.update(join0=join(0, last))
    hooks["plans", "l0_in_dx"] = lambda env: [swap(2, [env["g_in0"]]), env["join0"]]
    hooks["after", "l0_in_dx"] = lambda env, last: start_scatter(2)

    small = {n: wts[n] for n, _ in SMALL}
    loss, grad_x, g = device_step(x[0], mem[0], positions[0], loss_target[0], small, env, hooks)
    loss = lax.psum(loss[0, 0], MESH_AXES)
    r_in0, = run_plan("rs2_pair_join", join(2, g["s5_log_step"]))
    (r_pair1, r_ukv, r_in1, r_uq), (r_pair0, r_glu) = (rs[k, "join"].results for k in (1, 0))

    small_flat = jnp.concatenate([g[n].reshape(-1) for n, _ in SMALL_FULL])
    g_small = jnp.pad(small_flat, (0, 4 * SMALL_ROWS * SMALL_LANES - N_SMALL)).astype(BF16).reshape(4, SMALL_ROWS, SMALL_LANES)
    r_small = reduce_scatter_chips("rs3", [g_small], place)[0]
    small_all = own_slot(all_gather_chips("gather_small_grads", [r_small]), [r_small])[0].reshape(-1)[:N_SMALL]

    grads = {"w_out": jnp.stack([r_pair0[:PAIR_MKV], r_pair1[:PAIR_MKV]]), "w_mem_kv": jnp.stack([r_pair0[PAIR_MKV:], r_pair1[PAIR_MKV:]]),
             "s5_w_in": r_in0[None], "s5_w_glu": r_glu[None], "mla_w_ukv": r_ukv[None], "mla_w_in": r_in1[None], "mla_w_uq": r_uq[None]}
    off = 0
    for n, s in SMALL_FULL:
        grads[n] = small_all[off:off + math.prod(s)].reshape(s)
        off += math.prod(s)
    for n, s in SHARDED_SMALL:
        grads[n] = lax.dynamic_slice(grads[n], (0, chip * s[1]), s)

    delta, new_m, new_v = {}, {}, {}
    for n, s in BIG + [(n, s) for n, s in SMALL if len(s) == 4]:
        perm = MINOR_LAST.get(n, tuple(range(len(s))))
        turned = tuple(s[p] for p in perm)
        view = lambda a: jnp.transpose(a, perm).reshape(-1, turned[-1])
        res = adamw("adamw_" + n, view(wts[n]), view(grads[n]), view(mom[n]), view(var[n]))
        delta[n], new_m[n], new_v[n] = (jnp.transpose(r.reshape(turned), tuple(perm.index(i) for i in range(len(s)))) for r in res)
    small_names = [n for n, s in SMALL if len(s) < 4] + [n for n, _ in SHARDED_SMALL]
    n_own = sum(wts[n].size for n in small_names)
    rows_own = -(-n_own // (8 * 128)) * 8

    def pack_small(d):
        flat = jnp.concatenate([d[n].reshape(-1) for n in small_names])
        return jnp.pad(flat, (0, rows_own * 128 - n_own), constant_values=1.0).reshape(rows_own, 128)

    res = adamw("adamw_small", pack_small(wts), pack_small(grads), pack_small(mom), pack_small(var))
    off = 0
    for n in small_names:
        size = wts[n].size
        delta[n], new_m[n], new_v[n] = (r.reshape(-1)[off:off + size].reshape(wts[n].shape) for r in res)
        off += size

    return (loss, grad_x[None], *[grads[n] for n in WEIGHT_ORDER], *[delta[n] for n in WEIGHT_ORDER],
            *[new_m[n] for n in WEIGHT_ORDER], *[new_v[n] for n in WEIGHT_ORDER])
```

```python
import functools
import math

import jax
import jax.numpy as jnp
from jax import lax
from jax.experimental import pallas as pl
from jax.experimental.pallas import tpu as pltpu

F32, BF16 = jnp.float32, jnp.bfloat16
SDS = jax.ShapeDtypeStruct

D = 1024
L = 2048
ML = 256
BW = 2 * D
XQW = BW // 4
PW = BW - XQW
XH, XHD = 4, 128
SG, SC, SP = 96, 16, 64
SN = SG * SP
NOPE, ROPE, VD = 128, 64, 128
MH = 12
QL, KVL = 512, 256
EPS = 1e-6
ROPE_THETA = 10000.0
MLA_IN = QL + KVL + ROPE + XQW + BW
MLA_IN_P = 3456
ADAM_LR, ADAM_B1, ADAM_B2, ADAM_EPS, ADAM_WD, ADAM_STEP = 0.001, 0.9, 0.999, 1e-08, 0.01, 10

VMEM_LIMIT = 48 * 2**20
SEG = 8
SEG_LEN = L // SEG
MESH_AXES = ("x", "y", "c")


def _cparams():
    return pltpu.CompilerParams(vmem_limit_bytes=VMEM_LIMIT)


def _dg(a, b, ca, cb):
    return lax.dot_general(a.astype(BF16), b.astype(BF16), (((ca,), (cb,)), ((), ())), preferred_element_type=F32)


@jax.custom_vjp
def mm_nn(a, b):
    return _dg(a, b, 1, 0)


mm_nn.defvjp(lambda a, b: (_dg(a, b, 1, 0), (a, b)), lambda res, g: (_dg(g, res[1], 1, 1), _dg(res[0], g, 0, 0)))


@jax.custom_vjp
def mm_nt(a, b):
    return _dg(a, b, 1, 1)


mm_nt.defvjp(lambda a, b: (_dg(a, b, 1, 1), (a, b)), lambda res, g: (_dg(g, res[1], 1, 0), _dg(g, res[0], 0, 0)))


@functools.partial(jax.custom_vjp, nondiff_argnums=(1,))
def lane_roll(x, shift):
    return pltpu.roll(x, shift, 1)


lane_roll.defvjp(lambda x, shift: (pltpu.roll(x, shift, 1), None),
                 lambda shift, _, g: (pltpu.roll(g, (128 - shift) % 128, 1),))


def rms(x, g):
    return x * lax.rsqrt(jnp.mean(x * x, axis=-1, keepdims=True) + EPS) * g


@jax.custom_vjp
def softmax_rows(s):
    e = jnp.exp(s - jnp.max(s, axis=-1, keepdims=True))
    return e / jnp.sum(e, axis=-1, keepdims=True)


def _softmax_rows_fwd(s):
    p = softmax_rows(s)
    return p, p


def _softmax_rows_bwd(p, g):
    return (p * (g - jnp.sum(g * p, axis=-1, keepdims=True)),)


softmax_rows.defvjp(_softmax_rows_fwd, _softmax_rows_bwd)


def silu(x):
    return x * jax.nn.sigmoid(x)


ANY = pl.BlockSpec(memory_space=pl.ANY)
MESH_ID = pl.DeviceIdType.MESH


def _dma_sems(n):
    return [pltpu.SemaphoreType.DMA((n,)), pltpu.SemaphoreType.DMA((n,))]


class Plan:
    def __init__(self, operands, out_shape, aliases, n_sems, copies):
        self.operands, self.out_shape, self.aliases, self.n_sems, self.copies = list(operands), list(out_shape), aliases, n_sems, copies
        self.results = None


_SCHEDULE_BEHIND = []


def schedule_behind(token):
    _SCHEDULE_BEHIND.append(token)


def hosted_call(kern, *, name, grid, in_specs, out_specs, out_shape, operands, scratch_shapes=(), aliases=None, cparams=None, plans=(), deps=()):
    n_in, n_out, n_scr = len(in_specs), len(out_specs), len(scratch_shapes)
    p_in, p_out = [len(p.operands) for p in plans], [len(p.out_shape) for p in plans]
    deps = tuple(deps) + tuple(_SCHEDULE_BEHIND)
    _SCHEDULE_BEHIND.clear()
    all_aliases = dict(aliases or {})
    in_off, out_off = n_in, n_out
    for p, ni, no in zip(plans, p_in, p_out):
        all_aliases.update({in_off + i: out_off + o for i, o in p.aliases.items()})
        in_off, out_off = in_off + ni, out_off + no

    def body(*refs):
        pos, pins, pouts = n_in, [], []
        for ni in p_in:
            pins.append(refs[pos:pos + ni])
            pos += ni
        pos += len(deps)
        main_out = refs[pos:pos + n_out]
        pos += n_out
        for no in p_out:
            pouts.append(refs[pos:pos + no])
            pos += no
        main_scr = refs[pos:pos + n_scr]
        pos += n_scr
        if plans:
            ids = [pl.program_id(ax) for ax in range(len(grid))]
            first = functools.reduce(jnp.logical_and, [i == 0 for i in ids])
            last = functools.reduce(jnp.logical_and, [i == g - 1 for i, g in zip(ids, grid)])
            copies = [p.copies(pins[k], pouts[k], refs[pos + 2 * k], refs[pos + 2 * k + 1]) for k, p in enumerate(plans)]

            @pl.when(first)
            def _():
                for sends, _ in copies:
                    for cp in sends:
                        cp.start()

        kern(*refs[:n_in], *main_out, *main_scr)
        if plans:
            @pl.when(last)
            def _():
                for sends, recvs in copies:
                    for cp in recvs:
                        cp.wait_recv()
                    for cp in sends:
                        cp.wait_send()

    res = pl.pallas_call(body, grid=grid, in_specs=list(in_specs) + [ANY] * (sum(p_in) + len(deps)),
                         out_specs=list(out_specs) + [ANY] * sum(p_out), out_shape=list(out_shape) + [s for p in plans for s in p.out_shape],
                         scratch_shapes=list(scratch_shapes) + [s for p in plans for s in _dma_sems(p.n_sems)],
                         input_output_aliases=all_aliases, name=name, compiler_params=cparams or _cparams())(
        *operands, *[a for p in plans for a in p.operands], *deps)
    pos = n_out
    for p, no in zip(plans, p_out):
        p.results = list(res[pos:pos + no])
        pos += no
    return list(res[:n_out])


def _wide(v):
    return v.astype(F32) if v.dtype == BF16 else v


def stage(name, fn, grid, ins, outs):
    n_in = len(ins)

    def kern(*refs):
        res = fn(*[_wide(r[...]) for r in refs[:n_in]])
        for r, v in zip(refs[n_in:], res):
            r[...] = v.astype(r.dtype)

    return hosted_call(kern, name=name, grid=grid, in_specs=[s for _, s in ins], out_specs=[s for _, s in outs],
                       out_shape=[sd for sd, _ in outs], operands=[a for a, _ in ins])


def stage_bwd(name, fn, grid, ins, cts, diffs, plans=()):
    n_in, n_ct = len(ins), len(cts)
    didx = [i for i, d in enumerate(diffs) if d is not None]
    opts = {i: (diffs[i][3] if len(diffs[i]) > 3 else {}) for i in didx if diffs[i][0] == "row"}
    adds = [(i, opts[i]["add"]) for i in opts if "add" in opts[i]]
    intos = [(i, opts[i]["into"]) for i in opts if "into" in opts[i]]
    n_add, n_into = len(adds), len(intos)
    add_pos = {i: n_in + n_ct + k for k, (i, _) in enumerate(adds)}
    n_extra = n_in + n_ct + n_add + n_into

    def kern(*refs):
        vals = [_wide(r[...]) for r in refs[:n_in]]

        def f(*dv):
            full = list(vals)
            for i, v in zip(didx, dv):
                full[i] = v
            return fn(*full)

        _, vjp = jax.vjp(f, *[vals[i].astype(F32) for i in didx])
        gs = vjp(tuple(c[...].astype(F32) for c in refs[n_in:n_in + n_ct]))
        for o_ref, i, g in zip(refs[n_extra:], didx, gs):
            if diffs[i][0] == "row":
                if i in add_pos:
                    g = g + refs[add_pos[i]][...].astype(F32)
                o_ref[...] = g.astype(o_ref.dtype)
            else:
                first = functools.reduce(jnp.logical_and, [pl.program_id(ax) == 0 for ax in diffs[i][1]])

                @pl.when(first)
                def _():
                    o_ref[...] = g

                @pl.when(jnp.logical_not(first))
                def _():
                    o_ref[...] += g

    out_shape, out_specs = [], []
    for i in didx:
        if diffs[i][0] == "row":
            out_shape.append(diffs[i][1])
            out_specs.append(diffs[i][2])
        else:
            out_shape.append(SDS(ins[i][0].shape, F32))
            out_specs.append(ins[i][1])
    aliases = {n_in + n_ct + n_add + k: didx.index(i) for k, (i, _) in enumerate(intos)}
    in_specs = [s for _, s in ins] + [s for _, s in cts] + [s for _, (_, s) in adds] + [ANY] * n_into
    operands = [a for a, _ in ins] + [a for a, _ in cts] + [a for _, (a, _) in adds] + [a for _, a in intos]
    return hosted_call(kern, name=name, grid=grid, in_specs=in_specs, out_specs=out_specs, out_shape=out_shape, operands=operands,
                       aliases=aliases, plans=plans)


def rspec(tl, w, cb=0):
    return pl.BlockSpec((tl, w), lambda i: (i, cb))


def cspec(shape):
    return pl.BlockSpec(shape, lambda i: (0,) * len(shape))


def row_fwd(name, fn, rows, tl, row_ins, consts, outs):
    ins = [(a, rspec(tl, w, cb)) for a, w, cb in row_ins] + [(a, cspec(a.shape)) for a in consts]
    return stage(name, fn, (rows // tl,), ins, [(SDS((rows, w), dt), rspec(tl, w)) for w, dt in outs])


def row_bwd(name, fn, rows, tl, row_ins, consts, cts, row_diff, const_diff, plans=()):
    ins = [(a, rspec(tl, w, cb)) for a, w, cb in row_ins] + [(a, cspec(a.shape)) for a in consts]
    diffs = []
    for (a, w, cb), d in zip(row_ins, row_diff):
        if not d:
            diffs.append(None)
            continue
        d = d if isinstance(d, dict) else {}
        opts = {}
        if "add" in d:
            opts["add"] = (d["add"][0], rspec(tl, d["add"][1], d["add"][2]))
        if d.get("into") is not None:
            opts["into"] = d["into"]
        diffs.append(("row", SDS((rows, d.get("cols", w)), d.get("dtype", F32)), rspec(tl, w, d.get("cb", 0)), opts))
    diffs += [("acc", (0,)) if d else None for d in const_diff]
    return stage_bwd(name, fn, (rows // tl,), ins, [(a, rspec(tl, w, cb)) for a, w, cb in cts], diffs, plans=plans)


MATMUL_VMEM = 36 * 2**20
ADAMW_VMEM = 28 * 2**20


class Sharded:
    def __init__(self, arr, kind, roff, rows):
        self.arr, self.kind, self.roff, self.rows, self.n = arr, kind, roff, rows, arr.shape[2]
        self.shape = (rows, 4 * self.n) if kind == "col" else (4 * rows, self.n)

    def fits(self, t0, t1):
        return self.roff % t0 == 0 and self.rows % t0 == 0 and self.n % t1 == 0

    def spec(self, t0, t1, bidx):
        assert self.fits(t0, t1), (self.kind, self.roff, self.rows, self.n, t0, t1)
        r0 = self.roff // t0
        if self.kind == "col":
            per = self.n // t1
            return pl.BlockSpec((None, t0, t1), lambda *g: (bidx(*g)[1] // per, r0 + bidx(*g)[0], bidx(*g)[1] % per))
        per = self.rows // t0
        return pl.BlockSpec((None, t0, t1), lambda *g: (bidx(*g)[0] // per, r0 + bidx(*g)[0] % per, bidx(*g)[1]))


def matmul(name, a, b, mode, out_dtype=BF16, add=None, out=None, into=None, plans=()):
    if mode == "tn":
        k_dim, m = a.shape
    else:
        m, k_dim = a.shape
    n = b.shape[0] if mode == "nt" else b.shape[1]
    b_fit = b.fits if isinstance(b, Sharded) else (lambda t0, t1: True)
    o_fit = out.fits if out is not None else (lambda t0, t1: True)
    a_bytes, b_bytes = jnp.dtype(a.dtype).itemsize, jnp.dtype(b.arr.dtype if isinstance(b, Sharded) else b.dtype).itemsize
    o_bytes = jnp.dtype(out_dtype if out is None else out.arr.dtype).itemsize

    def vmem(tm, tn, tk):
        return 2 * (tm * tk * a_bytes + tk * tn * b_bytes + tm * tn * (o_bytes + (4 if add is not None else 0))) + 4 * tm * tn * (1 + (tk < k_dim))

    tiles = [(tm, tn, tk) for tm in (2048, 1024, 512, 256, 128) for tn in (1024, 768, 512, 384, 256, 128)
             for tk in sorted({k_dim, 1024, 768, 512, 384, 256, 128})
             if m % tm == 0 and n % tn == 0 and k_dim % tk == 0 and (b_fit(tn, tk) if mode == "nt" else b_fit(tk, tn)) and o_fit(tm, tn)
             and vmem(tm, tn, tk) <= MATMUL_VMEM]
    tm, tn, tk = max(tiles, key=lambda t: (t[2] == k_dim, t[0] * t[1] * t[2], t[0] * t[1]))
    nk = k_dim // tk
    a_spec = pl.BlockSpec((tk, tm), lambda i, j, k: (k, i)) if mode == "tn" else pl.BlockSpec((tm, tk), lambda i, j, k: (i, k))
    if isinstance(b, Sharded):
        b_spec = b.spec(tn, tk, lambda i, j, k: (j, k)) if mode == "nt" else b.spec(tk, tn, lambda i, j, k: (k, j))
        b = b.arr
    else:
        b_spec = pl.BlockSpec((tn, tk), lambda i, j, k: (j, k)) if mode == "nt" else pl.BlockSpec((tk, tn), lambda i, j, k: (k, j))
    o_spec = pl.BlockSpec((tm, tn), lambda i, j, k: (i, j))
    out_spec, out_shape = (o_spec, SDS((m, n), out_dtype)) if out is None else (out.spec(tm, tn, lambda i, j, k: (i, j)), out.arr)
    ca, cb = {"nn": (1, 0), "nt": (1, 1), "tn": (0, 0)}[mode]
    n_in = 2 + (add is not None)

    def finish(refs, o_ref, r):
        if add is not None:
            r = r + refs[2][...]
        o_ref[...] = r.astype(o_ref.dtype)

    def kern_whole(*refs):
        finish(refs, refs[-1], _dg(refs[0][...], refs[1][...], ca, cb))

    def kern_cut(*refs):
        o_ref, acc = refs[-2], refs[-1]
        k = pl.program_id(2)

        @pl.when(k == 0)
        def _():
            acc[...] = jnp.zeros_like(acc)

        acc[...] += _dg(refs[0][...], refs[1][...], ca, cb)

        @pl.when(k == nk - 1)
        def _():
            finish(refs, o_ref, acc[...])

    ins, specs = [a, b], [a_spec, b_spec]
    if add is not None:
        ins.append(add)
        specs.append(o_spec)
    if into is not None:
        ins.append(into)
        specs.append(ANY)
    return hosted_call(kern_whole if nk == 1 else kern_cut, name=name, grid=(m // tm, n // tn, nk), in_specs=specs, out_specs=[out_spec],
                       out_shape=[out_shape], operands=ins, scratch_shapes=[] if nk == 1 else [pltpu.VMEM((tm, tn), F32)],
                       aliases={} if into is None else {n_in: 0}, plans=plans)[0]


SCAN_UNROLL = 8


def _cmul(ar, ai, br, bi):
    return ar * br - ai * bi, ar * bi + ai * br


def _sub_shift(x, down):
    row = lax.broadcasted_iota(jnp.int32, x.shape, 0)
    if down:
        return jnp.where(row == 0, 0.0, pltpu.roll(x, 1, 0))
    return jnp.where(row == SEG - 1, 0.0, pltpu.roll(x, SEG - 1, 0))


def _pow_seg_len(ar, ai):
    for _ in range(int(math.log2(SEG_LEN))):
        ar, ai = _cmul(ar, ai, ar, ai)
    return ar, ai


def _scan_in_place(sr, si, a_re, a_im):
    lanes = sr.shape[1]
    ar = jnp.broadcast_to(a_re, (SEG, lanes))
    ai = jnp.broadcast_to(a_im, (SEG, lanes))
    zero = jnp.zeros((SEG, lanes), F32)

    def local(i, carry):
        rows = pl.ds(pl.multiple_of(i * SEG, SEG), SEG)
        mr, mi = _cmul(ar, ai, carry[0], carry[1])
        nr, ni = mr + sr[rows, :], mi + si[rows, :]
        sr[rows, :] = nr
        si[rows, :] = ni
        return nr, ni

    fr, fi = lax.fori_loop(0, SEG_LEN, local, (zero, zero), unroll=SCAN_UNROLL)
    pr, pi = _pow_seg_len(ar, ai)
    ir, ii = zero, zero
    for _ in range(SEG - 1):
        mr, mi = _cmul(pr, pi, ir, ii)
        ir, ii = _sub_shift(mr + fr, True), _sub_shift(mi + fi, True)

    def carry_in(i, pw):
        rows = pl.ds(pl.multiple_of(i * SEG, SEG), SEG)
        cr, ci = _cmul(pw[0], pw[1], ir, ii)
        sr[rows, :] += cr
        si[rows, :] += ci
        return _cmul(pw[0], pw[1], ar, ai)

    lax.fori_loop(0, SEG_LEN, carry_in, (ar, ai), unroll=SCAN_UNROLL)


S5_LANES = 8 * SP
S5_BLOCKS = SN // S5_LANES


def _s5_specs():
    u_spec = pl.BlockSpec((L, 8 * SC), lambda j: (0, j))
    s_spec = pl.BlockSpec((L, S5_LANES), lambda j: (0, j))
    wb_spec = pl.BlockSpec((S5_LANES, 8 * SC), lambda j: (j, 0))
    wc_spec = pl.BlockSpec((8 * SC, S5_LANES), lambda j: (j, 0))
    a_spec = pl.BlockSpec((1, S5_LANES), lambda j: (0, j))
    d_spec = pl.BlockSpec((1, 8 * SC), lambda j: (0, j))
    return u_spec, s_spec, wb_spec, wc_spec, a_spec, d_spec


def s5_forward(proj, wb_re, wb_im, wc_re, wc_im, a_re, a_im, d, plans=()):
    def kern(u_ref, wbr, wbi, wcr, wci, ar, ai, d_ref, sr_out, si_out, g_ref, sr, si):
        u = _wide(u_ref[...])
        sr[...], si[...] = fn_s5_bu(u, wbr[...], wbi[...])
        _scan_in_place(sr, si, ar[...], ai[...])
        g_ref[...] = fn_s5_out(sr[...], si[...], u, d_ref[...], wcr[...], wci[...])[0].astype(g_ref.dtype)
        sr_out[...] = sr[...].astype(sr_out.dtype)
        si_out[...] = si[...].astype(si_out.dtype)

    u_spec, s_spec, wb_spec, wc_spec, a_spec, d_spec = _s5_specs()
    return hosted_call(kern, name="s5_forward", grid=(S5_BLOCKS,), in_specs=[u_spec, wb_spec, wb_spec, wc_spec, wc_spec, a_spec, a_spec, d_spec],
                       out_specs=[s_spec, s_spec, u_spec], out_shape=[SDS((L, SN), BF16)] * 2 + [SDS((L, PW), BF16)],
                       operands=[proj, wb_re, wb_im, wc_re, wc_im, a_re, a_im, d], scratch_shapes=[pltpu.VMEM((L, S5_LANES), F32)] * 2,
                       plans=plans)


def _adjoint_scan_in_place(lr, li, sr, si, a_re, a_im):
    lanes = lr.shape[1]
    ar = jnp.broadcast_to(a_re, (SEG, lanes))
    ai = -jnp.broadcast_to(a_im, (SEG, lanes))
    zero = jnp.zeros((SEG, lanes), F32)

    def local(k, carry):
        i = SEG_LEN - 1 - k
        rows = pl.ds(pl.multiple_of(i * SEG, SEG), SEG)
        mr, mi = _cmul(ar, ai, carry[0], carry[1])
        nr, ni = mr + lr[rows, :], mi + li[rows, :]
        lr[rows, :] = nr
        li[rows, :] = ni
        return nr, ni

    fr, fi = lax.fori_loop(0, SEG_LEN, local, (zero, zero), unroll=SCAN_UNROLL)
    pr, pi = _pow_seg_len(ar, ai)
    ir, ii = zero, zero
    for _ in range(SEG - 1):
        mr, mi = _cmul(pr, pi, ir, ii)
        ir, ii = _sub_shift(mr + fr, False), _sub_shift(mi + fi, False)

    def fix(rows, pw):
        cr, ci = _cmul(pw[0], pw[1], ir, ii)
        tr, ti = lr[rows, :] + cr, li[rows, :] + ci
        lr[rows, :] = tr
        li[rows, :] = ti
        return tr, ti

    def grad_a(tr, ti, spr, spi, acc):
        return acc[0] + tr * spr + ti * spi, acc[1] + ti * spr - tr * spi

    def carry_in(k, c):
        i = SEG_LEN - 1 - k
        rows = pl.ds(pl.multiple_of(i * SEG, SEG), SEG)
        prev = pl.ds(pl.multiple_of((i - 1) * SEG, SEG), SEG)
        tr, ti = fix(rows, (c[0], c[1]))
        acc = grad_a(tr, ti, sr[prev, :], si[prev, :], (c[2], c[3]))
        nr, ni = _cmul(c[0], c[1], ar, ai)
        return nr, ni, acc[0], acc[1]

    pwr, pwi, accr, acci = lax.fori_loop(0, SEG_LEN - 1, carry_in, (ar, ai, zero, zero), unroll=5)
    tr, ti = fix(pl.ds(0, SEG), (pwr, pwi))
    last = pl.ds((SEG_LEN - 1) * SEG, SEG)
    accr, acci = grad_a(tr, ti, _sub_shift(sr[last, :], True), _sub_shift(si[last, :], True), (accr, acci))
    return jnp.sum(accr, axis=0, keepdims=True), jnp.sum(acci, axis=0, keepdims=True)


S5_BWD_VMEM = 58 * 2**20


def s5_backward(dg, proj, s_re, s_im, wb_re, wb_im, wc_re, wc_im, a_re, a_im, d, dproj, plans=()):
    def kern(dg_ref, u_ref, sr_in, si_in, wbr, wbi, wcr, wci, ar, ai, d_ref, _, du_ref, dd_ref, dwcr, dwci, dwbr, dwbi, dar, dai, lr, li, sr, si):
        u = _wide(u_ref[...])
        sr[...], si[...] = _wide(sr_in[...]), _wide(si_in[...])
        _, vjp_out = jax.vjp(fn_s5_out, sr[...], si[...], u, d_ref[...], wcr[...], wci[...])
        lr[...], li[...], du_out, dd_ref[...], dwcr[...], dwci[...] = vjp_out((_wide(dg_ref[...]),))
        dar[...], dai[...] = _adjoint_scan_in_place(lr, li, sr, si, ar[...], ai[...])
        _, vjp_in = jax.vjp(fn_s5_bu, u, wbr[...], wbi[...])
        du_in, dwbr[...], dwbi[...] = vjp_in((lr[...], li[...]))
        du_ref[...] = (du_out + du_in).astype(du_ref.dtype)

    u_spec, s_spec, wb_spec, wc_spec, a_spec, d_spec = _s5_specs()
    outs = [(SDS(dproj.shape, dproj.dtype), u_spec), (SDS(d.shape, F32), d_spec), (SDS(wc_re.shape, F32), wc_spec), (SDS(wc_im.shape, F32), wc_spec),
            (SDS(wb_re.shape, F32), wb_spec), (SDS(wb_im.shape, F32), wb_spec), (SDS(a_re.shape, F32), a_spec), (SDS(a_im.shape, F32), a_spec)]
    return hosted_call(kern, name="s5_backward", grid=(S5_BLOCKS,),
                       in_specs=[u_spec, u_spec, s_spec, s_spec, wb_spec, wb_spec, wc_spec, wc_spec, a_spec, a_spec, d_spec, ANY],
                       out_specs=[sp for _, sp in outs], out_shape=[sd for sd, _ in outs], aliases={11: 0},
                       operands=[dg, proj, s_re, s_im, wb_re, wb_im, wc_re, wc_im, a_re, a_im, d, dproj],
                       scratch_shapes=[pltpu.VMEM((L, S5_LANES), F32)] * 4,
                       cparams=pltpu.CompilerParams(vmem_limit_bytes=S5_BWD_VMEM), plans=plans)


def fn_rms(x, g):
    return (rms(x, g),)


def fn_s5_disc(lre, lim, ls):
    step = jnp.exp(ls)
    e = jnp.exp(lre * step)
    a_re, a_im = e * jnp.cos(lim * step), e * jnp.sin(lim * step)
    den = lre * lre + lim * lim
    nr, ni = a_re - 1.0, a_im
    return a_re, a_im, (nr * lre + ni * lim) / den, (ni * lre - nr * lim) / den


def _group_mask(rows, cols, row_div, col_div):
    r = lax.broadcasted_iota(jnp.int32, (rows, cols), 0) // row_div % 8
    c = lax.broadcasted_iota(jnp.int32, (rows, cols), 1) // col_div
    return r == c


def _spread(x, mask):
    w = x.shape[1]
    copy = (lax.broadcasted_iota(jnp.int32, (w, 8 * w), 1) % w == lax.broadcasted_iota(jnp.int32, (w, 8 * w), 0)).astype(F32)
    return jnp.where(mask, jnp.dot(x, copy, precision=lax.Precision.HIGHEST, preferred_element_type=F32), 0.0)


def fn_s5_bmat(b_re, b_im, coef_re, coef_im):
    mask = _group_mask(b_re.shape[0], 8 * SC, SP, SC)
    return _spread(coef_re * b_re - coef_im * b_im, mask), _spread(coef_re * b_im + coef_im * b_re, mask)


def fn_s5_cmat(c_re, c_im):
    mask = _group_mask(c_re.shape[0], 8 * SP, SC, SP)
    return _spread(c_re, mask), _spread(c_im, mask)


def fn_s5_bu(u, wb_re, wb_im):
    return mm_nt(u, wb_re), mm_nt(u, wb_im)


def fn_s5_out(sr, si, u, d, wc_re, wc_im):
    y = mm_nt(sr, wc_re) - mm_nt(si, wc_im) + d * u
    return (jax.nn.gelu(y),)


def fn_merge_glu(z, mo, gate):
    yg = z[:, :PW] * jax.nn.sigmoid(z[:, PW:])
    return (jnp.concatenate([yg, mo], axis=1) * silu(gate),)


def fn_merge(prim, mo, gate):
    return (jnp.concatenate([prim, mo], axis=1) * silu(gate),)


def fn_mem_k(kv, g):
    return (jnp.concatenate([rms(kv[:, h * XHD:(h + 1) * XHD], g) for h in range(XH)], axis=1),)


def fn_mem_attn(xq, kn, v, g):
    outs = []
    for h in range(XH):
        sl = slice(h * XHD, (h + 1) * XHD)
        p = softmax_rows(mm_nt(rms(xq[:, sl], g), kn[:, sl]) * (XHD ** -0.5))
        outs.append(mm_nn(p, v[:, sl]))
    return (jnp.concatenate(outs, axis=1),)


def _half_rms(x, g):
    lo = lax.broadcasted_iota(jnp.int32, x.shape, 1) < ROPE
    x2 = x * x
    s_lo = jnp.sum(jnp.where(lo, x2, 0.0), axis=1, keepdims=True)
    s_hi = jnp.sum(jnp.where(lo, 0.0, x2), axis=1, keepdims=True)
    return x * lax.rsqrt(jnp.where(lo, s_lo, s_hi) / ROPE + EPS) * g


def _rope(x, cos2, sin_signed):
    first = lax.broadcasted_iota(jnp.int32, x.shape, 1) % ROPE < ROPE // 2
    return x * cos2 + jnp.where(first, lane_roll(x, 128 - ROPE // 2), lane_roll(x, ROPE // 2)) * sin_signed


def fn_mla_prep(q, kv, kr, cos2, sin_signed, qnn, knn, qrn, krn):
    lo = lax.broadcasted_iota(jnp.int32, kr.shape, 1) < ROPE
    kr_pad = jnp.where(lo, _rope(_half_rms(kr, krn), cos2, sin_signed), 0.0)
    qf, kf, vs = [], [], []
    for m in range(MH // 2):
        pair = _rope(_half_rms(q[:, MH * NOPE + 128 * m:MH * NOPE + 128 * (m + 1)], qrn), cos2, sin_signed)
        for h, rope_h in ((2 * m, pair), (2 * m + 1, lane_roll(pair, ROPE))):
            qf.append(jnp.concatenate([rms(q[:, NOPE * h:NOPE * (h + 1)], qnn), jnp.where(lo, rope_h, 0.0)], axis=1))
    for h in range(MH):
        kf.append(jnp.concatenate([rms(kv[:, 256 * h:256 * h + NOPE], knn), kr_pad], axis=1))
        vs.append(kv[:, 256 * h + NOPE:256 * (h + 1)])
    return jnp.stack(qf), jnp.stack(kf), jnp.stack(vs)


ATT_TQ = 512


def _attn_scores(q, kf):
    tq = q.shape[0]
    scale = (NOPE + ROPE) ** -0.5
    own = _dg(q, kf[-tq:], 1, 1) * scale
    own = jnp.where(lax.broadcasted_iota(jnp.int32, own.shape, 1) <= lax.broadcasted_iota(jnp.int32, own.shape, 0), own, jnp.finfo(F32).min)
    return own if kf.shape[0] == tq else jnp.concatenate([_dg(q, kf[:-tq], 1, 1) * scale, own], axis=1)


def _attn_specs():
    q_spec = pl.BlockSpec((None, ATT_TQ, 256), lambda h, i: (h, i, 0))
    k_spec = pl.BlockSpec((None, L, 256), lambda h, i: (h, 0, 0))
    v_spec = pl.BlockSpec((None, L, 128), lambda h, i: (h, 0, 0))
    o_spec = pl.BlockSpec((ATT_TQ, 128), lambda h, i: (i, h))
    lse_spec = pl.BlockSpec((None, ATT_TQ, 1), lambda h, i: (h, i, 0))
    return q_spec, k_spec, v_spec, o_spec, lse_spec


def causal_attn(qf, kf, vh):
    n_tiles = L // ATT_TQ

    def kern(q_ref, k_ref, v_ref, o_ref, lse_ref):
        i = pl.program_id(1)
        for t in range(n_tiles):
            @pl.when(i == t)
            def _(t=t):
                keys = (t + 1) * ATT_TQ
                s = _attn_scores(q_ref[...], k_ref[:keys, :])
                m = jnp.max(s, axis=-1, keepdims=True)
                e = jnp.exp(s - m)
                total = jnp.sum(e, axis=-1, keepdims=True)
                o_ref[...] = (_dg(e, v_ref[:keys, :], 1, 0) / total).astype(o_ref.dtype)
                lse_ref[...] = m + jnp.log(total)

    q_spec, k_spec, v_spec, o_spec, lse_spec = _attn_specs()
    return pl.pallas_call(kern, grid=(MH, n_tiles), in_specs=[q_spec, k_spec, v_spec], out_specs=[o_spec, lse_spec],
                          out_shape=[SDS((L, MH * VD), F32), SDS((MH, L, 1), F32)], name="l1_attn", compiler_params=_cparams())(qf, kf, vh)


def causal_attn_bwd(qf, kf, vh, out, lse, dout):
    n_tiles = L // ATT_TQ
    scale = (NOPE + ROPE) ** -0.5

    def kern(q_ref, k_ref, v_ref, o_ref, lse_ref, do_ref, dq_ref, dk_ref, dv_ref):
        i = pl.program_id(1)

        @pl.when(i == 0)
        def _():
            dk_ref[...] = jnp.zeros_like(dk_ref)
            dv_ref[...] = jnp.zeros_like(dv_ref)

        for t in range(n_tiles):
            @pl.when(i == t)
            def _(t=t):
                keys = (t + 1) * ATT_TQ
                q, k, v, do = q_ref[...], k_ref[:keys, :], v_ref[:keys, :], do_ref[...]
                p = jnp.exp(_attn_scores(q, k) - lse_ref[...])
                delta = jnp.sum(do * _wide(o_ref[...]), axis=-1, keepdims=True)
                dv_ref[:keys, :] += _dg(p, do, 0, 0)
                ds = p * (_dg(do, v, 1, 1) - delta) * scale
                dq_ref[...] = _dg(ds, k, 1, 0)
                dk_ref[:keys, :] += _dg(ds, q, 0, 0)

    q_spec, k_spec, v_spec, o_spec, lse_spec = _attn_specs()
    return pl.pallas_call(kern, grid=(MH, n_tiles), in_specs=[q_spec, k_spec, v_spec, o_spec, lse_spec, o_spec],
                          out_specs=[q_spec, k_spec, v_spec], out_shape=[SDS(qf.shape, F32), SDS(kf.shape, F32), SDS(vh.shape, F32)],
                          name="l1_attn_bwd", compiler_params=_cparams())(qf, kf, vh, out, lse, dout)


def loss_and_grad(y, target, tl=256):
    def kern(y_ref, t_ref, dy_ref, loss_ref):
        d = y_ref[...] - t_ref[...]
        dy_ref[...] = d / D

        @pl.when(pl.program_id(0) == 0)
        def _():
            loss_ref[...] = jnp.zeros_like(loss_ref)

        loss_ref[...] += 0.5 * jnp.sum(jnp.sum(d * d, axis=1, keepdims=True), axis=0, keepdims=True) / D

    return pl.pallas_call(kern, grid=(L // tl,), in_specs=[rspec(tl, D), rspec(tl, D)], out_specs=[rspec(tl, D), cspec((1, 1))],
                          out_shape=[SDS((L, D), F32), SDS((1, 1), F32)], name="loss", compiler_params=_cparams())(y, target)


def adamw(name, w, g, m, v):
    rows, cols = w.shape
    block_row_bytes = 7 * 2 * 4 * max(cols, 128)
    tr = _row_tile(rows, min(2048, ADAMW_VMEM // block_row_bytes // 8 * 8), 8)

    def kern(w_ref, g_ref, m_ref, v_ref, d_ref, nm_ref, nv_ref):
        gg = g_ref[...]
        nm = ADAM_B1 * m_ref[...] + (1.0 - ADAM_B1) * gg
        nv = ADAM_B2 * v_ref[...] + (1.0 - ADAM_B2) * jnp.square(gg)
        m_hat = nm / (1.0 - ADAM_B1 ** ADAM_STEP)
        v_hat = nv / (1.0 - ADAM_B2 ** ADAM_STEP)
        d_ref[...] = -ADAM_LR * (m_hat / (jnp.sqrt(v_hat) + ADAM_EPS) + ADAM_WD * w_ref[...])
        nm_ref[...] = nm
        nv_ref[...] = nv

    spec = rspec(tr, cols)
    return hosted_call(kern, name=name, grid=(rows // tr,), in_specs=[spec] * 4, out_specs=[spec] * 3,
                       out_shape=[SDS((rows, cols), F32)] * 3, operands=[w, g, m, v])


def _row_tile(rows, cap=512, unit=16):
    return max(t for t in range(unit, cap + 1, unit) if rows % t == 0)


def _place():
    x, y, c = lax.axis_index("x"), lax.axis_index("y"), lax.axis_index("c")
    return x, y, c, [(1 - x, y), (x, 1 - y), (1 - x, 1 - y)]


def _row_chunks(rows, n, dtype):
    unit = 32 // jnp.dtype(dtype).itemsize
    base, extra = divmod(rows // unit, n)
    out, start = [], 0
    for k in range(n):
        size = (base + (k < extra)) * unit
        if size:
            out.append((start, size))
            start += size
    assert start == rows, (rows, unit)
    return out


PIECE_BYTES = 1 << 20


def _pieces(shapes_dtypes, rows_of):
    out = []
    for b, (shape, dtype) in enumerate(shapes_dtypes):
        rows = rows_of(shape)
        n = max(1, min(4, rows * shape[-1] * jnp.dtype(dtype).itemsize // PIECE_BYTES))
        out += [(b, st, sz) for st, sz in _row_chunks(rows, n, dtype)]
    return out


def all_gather_chips(name, shards):
    nb = len(shards)
    pieces = _pieces([(s.shape, s.dtype) for s in shards], lambda shape: shape[0] // 2)
    n = len(pieces)

    def body(*refs):
        x_refs, out_refs, send_sems, recv_sems = refs[:nb], refs[nb:2 * nb], refs[2 * nb], refs[2 * nb + 1]
        x, y, c, chips = _place()
        sibling = (x, y, 1 - c)
        mine = 2 * x + y

        def copy(sem, chip, cc, k, to, from_input=False):
            b, st, sz = pieces[k]
            rows_k = pl.ds(cc * (x_refs[b].shape[0] // 2) + st, sz)
            dst = out_refs[b].at[chip, rows_k, :]
            return pltpu.make_async_remote_copy(src_ref=x_refs[b].at[rows_k, :] if from_input else dst, dst_ref=dst,
                                                send_sem=send_sems.at[sem], recv_sem=recv_sems.at[sem], device_id=to, device_id_type=MESH_ID)

        order = [(k, j, 2 * cx + cy, (cx, cy, c)) for k in range(n) for j, (cx, cy) in enumerate(chips)]
        first = [copy(j * n + k, mine, c, k, to, from_input=True) for k, j, _, to in order]
        for cp in first:
            cp.start()
        passed = []
        for k, j, chip, _ in order:
            copy(j * n + k, chip, c, k, sibling).wait_recv()
            passed.append(copy((3 + j) * n + k, chip, c, k, sibling))
            passed[-1].start()
        for k, j, chip, _ in order:
            copy((3 + j) * n + k, chip, 1 - c, k, sibling).wait_recv()
        for cp in first + passed:
            cp.wait_send()

    return pl.pallas_call(body, in_specs=[ANY] * nb, out_specs=[ANY] * nb, out_shape=[SDS((4,) + s.shape, s.dtype) for s in shards],
                          scratch_shapes=_dma_sems(6 * n), name=name)(*shards)


def plan_gather_ici(shards):
    pieces = _pieces([(s.shape, s.dtype) for s in shards], lambda shape: shape[0] // 2)
    n = len(pieces)

    def copies(x_refs, out_refs, send_sems, recv_sems):
        x, y, c, chips = _place()
        mine = 2 * x + y

        def copy(j, k, chip, to, from_input):
            b, st, sz = pieces[k]
            rows_k = pl.ds(c * (x_refs[b].shape[0] // 2) + st, sz)
            dst = out_refs[b].at[chip, rows_k, :]
            return pltpu.make_async_remote_copy(src_ref=x_refs[b].at[rows_k, :] if from_input else dst, dst_ref=dst, send_sem=send_sems.at[j * n + k],
                                                recv_sem=recv_sems.at[j * n + k], device_id=to, device_id_type=MESH_ID)

        order = [(k, j, 2 * cx + cy, (cx, cy, c)) for k in range(n) for j, (cx, cy) in enumerate(chips)]
        return [copy(j, k, mine, to, True) for k, j, _, to in order], [copy(j, k, chip, to, False) for k, j, chip, to in order]

    return Plan(shards, [SDS((4,) + s.shape, s.dtype) for s in shards], {}, 3 * n, copies)


def plan_gather_pass(gathered):
    pieces = _pieces([(g.shape[1:], g.dtype) for g in gathered], lambda shape: shape[0] // 2)
    n = len(pieces)

    def copies(_, out_refs, send_sems, recv_sems):
        x, y, c, chips = _place()

        def copy(j, k, chip, cc):
            b, st, sz = pieces[k]
            rows_k = out_refs[b].at[chip, pl.ds(cc * (out_refs[b].shape[1] // 2) + st, sz), :]
            return pltpu.make_async_remote_copy(src_ref=rows_k, dst_ref=rows_k, send_sem=send_sems.at[j * n + k], recv_sem=recv_sems.at[j * n + k],
                                                device_id=(x, y, 1 - c), device_id_type=MESH_ID)

        order = [(k, j, 2 * cx + cy) for k in range(n) for j, (cx, cy) in enumerate(chips)]
        return [copy(j, k, chip, c) for k, j, chip in order], [copy(j, k, chip, 1 - c) for k, j, chip in order]

    return Plan(gathered, [SDS(g.shape, g.dtype) for g in gathered], {i: i for i in range(len(gathered))}, 3 * n, copies)


def plan_pair_exchange(gs):
    pieces = _pieces([(g.shape, g.dtype) for g in gs], lambda shape: shape[1] // 2)

    def copies(g_refs, got_refs, send_sems, recv_sems):
        x, y, c, _ = _place()
        swaps = [pltpu.make_async_remote_copy(src_ref=g_refs[b].at[:, pl.ds((1 - c) * (g_refs[b].shape[1] // 2) + st, sz), :],
                                              dst_ref=got_refs[b].at[:, pl.ds(st, sz), :], send_sem=send_sems.at[k], recv_sem=recv_sems.at[k],
                                              device_id=(x, y, 1 - c), device_id_type=MESH_ID)
                 for k, (b, st, sz) in enumerate(pieces)]
        return swaps, swaps

    return Plan(gs, [SDS((g.shape[0], g.shape[1] // 2, g.shape[2]), g.dtype) for g in gs], {}, len(pieces), copies)


def plan_chip_scatter(ps):
    pieces = _pieces([(p.shape, p.dtype) for p in ps], lambda shape: shape[1])
    n = len(pieces)

    def copies(p_refs, q_refs, send_sems, recv_sems):
        x, y, c, chips = _place()
        mine = 2 * x + y

        def copy(j, k, src_slot, dst_slot, to):
            b, st, sz = pieces[k]
            return pltpu.make_async_remote_copy(src_ref=p_refs[b].at[src_slot, pl.ds(st, sz), :], dst_ref=q_refs[b].at[dst_slot, pl.ds(st, sz), :],
                                                send_sem=send_sems.at[j * n + k], recv_sem=recv_sems.at[j * n + k], device_id=to,
                                                device_id_type=MESH_ID)

        order = [(k, j, 2 * cx + cy, (cx, cy, c)) for k in range(n) for j, (cx, cy) in enumerate(chips)]
        return [copy(j, k, chip, mine, to) for k, j, chip, to in order], [copy(j, k, mine, chip, to) for k, j, chip, to in order]

    return Plan(ps, [SDS(p.shape, p.dtype) for p in ps], {}, 3 * n, copies)


def plan_pair_join(bufs):
    pieces = _pieces([(b.shape, b.dtype) for b in bufs], lambda shape: shape[0] // 2)

    def copies(_, out_refs, send_sems, recv_sems):
        x, y, c, _ = _place()

        def copy(k, cc):
            b, st, sz = pieces[k]
            rows_k = out_refs[b].at[pl.ds(cc * (out_refs[b].shape[0] // 2) + st, sz), :]
            return pltpu.make_async_remote_copy(src_ref=rows_k, dst_ref=rows_k, send_sem=send_sems.at[k], recv_sem=recv_sems.at[k],
                                                device_id=(x, y, 1 - c), device_id_type=MESH_ID)

        return [copy(k, c) for k in range(len(pieces))], [copy(k, 1 - c) for k in range(len(pieces))]

    return Plan(bufs, [SDS(b.shape, b.dtype) for b in bufs], {i: i for i in range(len(bufs))}, len(pieces), copies)


def run_plan(name, plan):
    hosted_call(lambda: None, name=name, grid=(1,), in_specs=[], out_specs=[], out_shape=[], operands=[], plans=[plan])
    return plan.results


HBM = pl.BlockSpec(memory_space=pltpu.HBM)
SEMS = pl.BlockSpec(memory_space=pltpu.SEMAPHORE)
SPLIT_PARAMS = dict(has_side_effects=pltpu.SideEffectType.DATAFLOW_SIDE_EFFECTING)


def _plan_buffers(plan):
    in_place = {o: i for i, o in plan.aliases.items()}
    bufs = [pltpu.with_memory_space_constraint(a, pltpu.HBM) for a in plan.operands]
    where = []
    for o, sd in enumerate(plan.out_shape):
        if o in in_place:
            where.append(in_place[o])
        else:
            where.append(len(bufs))
            bufs.append(pltpu.with_memory_space_constraint(lax.empty(sd.shape, sd.dtype), pltpu.HBM))
    return bufs, where


def split_start(name, plans):
    layout = [_plan_buffers(p) for p in plans]
    counts = [len(b) for b, _ in layout]
    n_buf = sum(counts)

    def body(*refs):
        sems, token = refs[n_buf:n_buf + 2 * len(plans)], refs[-1]
        pos = 0
        for k, (p, (_, where)) in enumerate(zip(plans, layout)):
            mine = refs[pos:pos + counts[k]]
            pos += counts[k]
            sends, _ = p.copies(mine[:len(p.operands)], [mine[w] for w in where], sems[2 * k], sems[2 * k + 1])
            for cp in sends:
                cp.start()
        token[...] = jnp.zeros_like(token)

    bufs = [b for bs, _ in layout for b in bs]
    res = pl.pallas_call(
        body, name=name, in_specs=[HBM] * n_buf,
        out_specs=[SEMS] * (2 * len(plans)) + [HBM] * n_buf + [pl.BlockSpec(memory_space=pltpu.VMEM)],
        out_shape=[pltpu.SemaphoreType.DMA((p.n_sems,)) for p in plans for _ in range(2)] + [pltpu.HBM(b.shape, b.dtype) for b in bufs]
        + [SDS((8, 128), F32)],
        input_output_aliases={i: 2 * len(plans) + i for i in range(n_buf)}, compiler_params=pltpu.CompilerParams(**SPLIT_PARAMS))(*bufs)
    pos = 2 * len(plans)
    for k, p in enumerate(plans):
        p.in_flight = (res[2 * k], res[2 * k + 1], list(res[pos:pos + counts[k]]), layout[k][1])
        pos += counts[k]
    return res[-1]


def split_wait(name, plan, after):
    send_sems, recv_sems, bufs, where = plan.in_flight
    n_buf = len(bufs)

    def body(*refs):
        mine = refs[:n_buf]
        sends, recvs = plan.copies(mine[:len(plan.operands)], [mine[w] for w in where], refs[n_buf], refs[n_buf + 1])
        for cp in recvs:
            cp.wait_recv()
        for cp in sends:
            cp.wait_send()

    res = pl.pallas_call(body, name=name, in_specs=[HBM] * n_buf + [SEMS, SEMS, ANY], out_specs=[HBM] * n_buf,
                         out_shape=[pltpu.HBM(b.shape, b.dtype) for b in bufs], input_output_aliases={i: i for i in range(n_buf)},
                         compiler_params=pltpu.CompilerParams(**SPLIT_PARAMS))(*bufs, send_sems, recv_sems, after)
    plan.results = [res[w] for w in where]
    return plan.results


def pair_add(name, g, got, place):
    slots, rows, cols = g.shape
    half = rows // 2
    tr = _row_tile(half)
    nb = half // tr

    def kern(_, g_ref, t_ref, o_ref):
        o_ref[...] = (g_ref[...].astype(F32) + t_ref[...].astype(F32)).astype(o_ref.dtype)

    blk = pl.BlockSpec((None, tr, cols), lambda s, i, p: (s, i, 0))
    grid_spec = pltpu.PrefetchScalarGridSpec(
        num_scalar_prefetch=1, grid=(slots, nb),
        in_specs=[pl.BlockSpec((None, tr, cols), lambda s, i, p: (s, p[1] * nb + i, 0)), blk], out_specs=blk)
    return pl.pallas_call(kern, grid_spec=grid_spec, out_shape=SDS((slots, half, cols), g.dtype), name=name,
                          compiler_params=_cparams())(place, g, got)


def chip_add(name, p, q, place):
    slots, half, cols = p.shape
    tr = _row_tile(half)
    nb = half // tr

    def kern(_, p_ref, q1, q2, q3, o_ref):
        o_ref[...] = p_ref[...].astype(F32) + q1[...].astype(F32) + q2[...].astype(F32) + q3[...].astype(F32)

    def slot(k):
        return pl.BlockSpec((None, tr, cols), lambda i, pr: ((pr[0] + k) % slots, i, 0))

    grid_spec = pltpu.PrefetchScalarGridSpec(
        num_scalar_prefetch=1, grid=(nb,), in_specs=[slot(0), slot(1), slot(2), slot(3)],
        out_specs=pl.BlockSpec((tr, cols), lambda i, pr: (pr[1] * nb + i, 0)))
    return pl.pallas_call(kern, grid_spec=grid_spec, out_shape=SDS((2 * half, cols), F32), name=name,
                          compiler_params=_cparams())(place, p, q, q, q)


def pair_adds(tag, gs, gots, place):
    return [pair_add(f"{tag}_pair_add_{i}", g, got, place) for i, (g, got) in enumerate(zip(gs, gots))]


def chip_adds(tag, pairs, qs, place):
    return [chip_add(f"{tag}_chip_add_{i}", p, q, place) for i, (p, q) in enumerate(zip(pairs, qs))]


def reduce_scatter_chips(tag, gs, place):
    pairs = pair_adds(tag, gs, run_plan(tag + "_pair_exchange", plan_pair_exchange(gs)), place)
    return run_plan(tag + "_pair_join", plan_pair_join(chip_adds(tag, pairs, run_plan(tag + "_chip_scatter", plan_chip_scatter(pairs)), place)))


BIG = [("w_out", (2, 512, 1024)), ("w_mem_kv", (2, 256, 1024)), ("s5_w_in", (1, 1024, 1024)), ("s5_w_glu", (1, 1536, 768)),
       ("mla_w_in", (1, 1024, 848)), ("mla_w_uq", (1, 512, 576)), ("mla_w_ukv", (1, 256, 768))]
SHARDED_SMALL = [("mla_q_lora_norm", (1, 128)), ("mla_kv_lora_norm", (1, 64))]
SMALL = [("ln_gain", (2, 1024)), ("mem_norm", (2, 1024)), ("xq_norm", (2, 128)), ("xk_norm", (2, 128)),
         ("s5_lambda_re", (1, 96, 64)), ("s5_lambda_im", (1, 96, 64)), ("s5_log_step", (1, 96)),
         ("s5_b_re", (1, 96, 64, 16)), ("s5_b_im", (1, 96, 64, 16)), ("s5_c_re", (1, 96, 16, 64)), ("s5_c_im", (1, 96, 16, 64)),
         ("s5_d", (1, 1536)), ("mla_q_nope_norm", (1, 128)), ("mla_k_nope_norm", (1, 128)), ("mla_q_rope_norm", (1, 64)),
         ("mla_k_rope_norm", (1, 64))]
WEIGHT_ORDER = ["ln_gain", "w_out", "mem_norm", "w_mem_kv", "xq_norm", "xk_norm", "s5_w_in", "s5_lambda_re", "s5_lambda_im",
                "s5_log_step", "s5_b_re", "s5_b_im", "s5_c_re", "s5_c_im", "s5_d", "s5_w_glu", "mla_w_in", "mla_q_lora_norm",
                "mla_kv_lora_norm", "mla_w_uq", "mla_w_ukv", "mla_q_nope_norm", "mla_k_nope_norm", "mla_q_rope_norm", "mla_k_rope_norm"]
MINOR_LAST = {"mla_w_in": (0, 2, 1), "mla_w_uq": (0, 2, 1), "s5_b_re": (0, 2, 3, 1), "s5_b_im": (0, 2, 3, 1),
              "s5_c_re": (0, 2, 3, 1), "s5_c_im": (0, 2, 3, 1)}
SMALL_FULL = SMALL + [(n, (1, 4 * s[1])) for n, s in SHARDED_SMALL]
N_SMALL = sum(math.prod(s) for _, s in SMALL_FULL)
SMALL_ROWS, SMALL_LANES = 128, 1024

PAIR_OUT, PAIR_MKV, PAIR_ROWS = 0, 512, 768


def stack_shards(w, dtype):
    pairs = [jnp.concatenate([w["w_out"][l], w["w_mem_kv"][l]], axis=0).astype(dtype) for l in range(2)]
    return ([w["s5_w_in"][0].astype(dtype)], [pairs[0], w["s5_w_glu"][0].astype(dtype)],
            [pairs[1], w["mla_w_ukv"][0].astype(dtype), w["mla_w_in"][0].astype(dtype), w["mla_w_uq"][0].astype(dtype)])


def pair_views(pair):
    return {"w_out": Sharded(pair, "row", PAIR_OUT, 512), "w_mem_kv": Sharded(pair, "row", PAIR_MKV, 256)}


def grad_views():
    pair = SDS((4, PAIR_ROWS, 1024), BF16)
    return {"w_out": Sharded(pair, "row", PAIR_OUT, 512), "w_mem_kv": Sharded(pair, "row", PAIR_MKV, 256),
            "s5_w_in": Sharded(SDS((4, 1024, 1024), BF16), "col", 0, 1024), "s5_w_glu": Sharded(SDS((4, 1536, 768), BF16), "col", 0, 1536),
            "mla_w_ukv": Sharded(SDS((4, 256, 768), BF16), "col", 0, 256)}


def cols_to_shards(full):
    return full.reshape(full.shape[0], 4, full.shape[1] // 4).transpose(1, 0, 2)


def shards_to_cols(arr):
    return arr.transpose(1, 0, 2).reshape(arr.shape[1], 4 * arr.shape[2])


def mla_in_permute(w):
    o1, o2, o3, o4 = QL, QL + KVL, QL + KVL + ROPE, QL + KVL + ROPE + XQW
    return jnp.concatenate([w[:, o4:], w[:, :o1], w[:, o3:o4], w[:, o1:o2], w[:, o2:o3],
                            jnp.zeros((w.shape[0], MLA_IN_P - MLA_IN), w.dtype)], axis=1)


def mla_in_unpermute(d):
    return jnp.concatenate([d[:, 2048:2560], d[:, 3072:3328], d[:, 3328:3392], d[:, 2560:3072], d[:, :2048]], axis=1)


def uq_permute(w):
    w3 = w.reshape(w.shape[0], MH, NOPE + ROPE)
    return jnp.concatenate([w3[:, :, :NOPE].reshape(w.shape[0], MH * NOPE), w3[:, :, NOPE:].reshape(w.shape[0], MH * ROPE)], axis=1)


def uq_unpermute(d):
    dn = d[:, :MH * NOPE].reshape(d.shape[0], MH, NOPE)
    dr = d[:, MH * NOPE:].reshape(d.shape[0], MH, ROPE)
    return jnp.concatenate([dn, dr], axis=2).reshape(d.shape[0], MH * (NOPE + ROPE))


def time_permute(a):
    return a.reshape(SEG, SEG_LEN, a.shape[-1]).transpose(1, 0, 2).reshape(L, a.shape[-1])


def time_unpermute(a):
    return a.reshape(SEG_LEN, SEG, a.shape[-1]).transpose(1, 0, 2).reshape(L, a.shape[-1])


def mem_branch_fwd(tag, mem, mem_norm, w_mem_kv, xk_norm):
    mn = row_fwd(tag + "_mem_rms", fn_rms, ML, ML, [(mem, D, 0)], [mem_norm], [(D, BF16)])[0]
    kv = matmul(tag + "_mem_kv", mn, w_mem_kv, "nn", F32)
    kn = row_fwd(tag + "_mem_knorm", fn_mem_k, ML, ML, [(kv, XQW, 0)], [xk_norm], [(XQW, F32)])[0]
    return mn, kv, kn


def mem_branch_bwd(tag, mem, mem_norm, w_mem_kv, xk_norm, mn, kv, dkn, dv, g_view, g_wide):
    dk, dxk = row_bwd(tag + "_mem_knorm_bwd", fn_mem_k, ML, ML, [(kv, XQW, 0)], [xk_norm], [(dkn, XQW, 0)], [True], [True])
    dkv = jnp.concatenate([dk, dv], axis=1)
    dmn = matmul(tag + "_mem_kv_dx", dkv, w_mem_kv, "nt", F32)
    g_wide = matmul(tag + "_mem_kv_dw", mn, dkv, "tn", out=g_view, into=g_wide)
    dmem_norm = row_bwd(tag + "_mem_rms_bwd", fn_rms, ML, ML, [(mem, D, 0)], [mem_norm], [(dmn, D, 0)], [False], [True])[0]
    return g_wide, dmem_norm, dxk


def mem_attn_fwd(tag, proj, cb, kn, kv, xq_norm):
    return row_fwd(tag + "_mem_attn", fn_mem_attn, L, 256, [(proj, XQW, cb)], [kn, kv[:, XQW:], xq_norm], [(XQW, F32)])[0]


def mem_attn_bwd(tag, proj, cb, kn, kv, xq_norm, dmo, dproj):
    place = {"cols": proj.shape[1], "cb": cb, "into": dproj, "dtype": dproj.dtype}
    return row_bwd(tag + "_mem_attn_bwd", fn_mem_attn, L, 256, [(proj, XQW, cb)], [kn, kv[:, XQW:], xq_norm], [(dmo, XQW, 0)],
                   [place], [True, True, True])


def device_step(x, mem, positions, target, small, env, hooks=None):
    hooks = hooks or {}

    def plans_for(name):
        return hooks[("plans", name)](env) if ("plans", name) in hooks else ()

    def around(when, name, last=None):
        if (when, name) in hooks:
            hooks[(when, name)](env, last)

    g = {}
    gw = grad_views()
    ln, mem_norm, xq_norm, xk_norm = small["ln_gain"], small["mem_norm"], small["xq_norm"], small["xk_norm"]

    lre, lim = small["s5_lambda_re"][0], small["s5_lambda_im"][0]
    ls = small["s5_log_step"].reshape(SG, 1)
    one = pl.BlockSpec((SG, SP), lambda i: (0, 0))
    col = pl.BlockSpec((SG, 1), lambda i: (0, 0))
    disc_ins = [(lre, one), (lim, one), (ls, col)]
    a_re, a_im, coef_re, coef_im = stage("s5_disc", fn_s5_disc, (1,), disc_ins, [(SDS((SG, SP), F32), one)] * 4)
    b_re, b_im = small["s5_b_re"].reshape(SN, SC), small["s5_b_im"].reshape(SN, SC)
    c_re, c_im = small["s5_c_re"].reshape(PW, SP), small["s5_c_im"].reshape(PW, SP)
    bmat_rows = [(b_re, SC, 0), (b_im, SC, 0), (coef_re.reshape(SN, 1), 1, 0), (coef_im.reshape(SN, 1), 1, 0)]
    wb_re, wb_im = row_fwd("s5_bmat", fn_s5_bmat, SN, 512, bmat_rows, [], [(128, F32)] * 2)
    cmat_rows = [(c_re, SP, 0), (c_im, SP, 0)]
    wc_re, wc_im = row_fwd("s5_cmat", fn_s5_cmat, PW, 128, cmat_rows, [], [(512, F32)] * 2)
    a_re_v, a_im_v = a_re.reshape(1, SN), a_im.reshape(1, SN)
    s5_d = small["s5_d"]

    xp = time_permute(x)
    h0 = row_fwd("l0_rms", fn_rms, L, 256, [(xp, D, 0)], [ln[0:1]], [(D, BF16)])[0]
    around("before", "l0_in", wb_re)
    w_in0 = Sharded(env["in0"], "col", 0, 1024)
    proj0 = matmul("l0_in", h0, w_in0, "nn")
    s_re, s_im, g0 = s5_forward(proj0, wb_re, wb_im, wc_re, wc_im, a_re_v, a_im_v, s5_d, plans=plans_for("s5_forward"))
    around("before", "l0_glu", g0)
    w0 = dict(pair_views(env["pair0"]), s5_w_glu=Sharded(env["glu"], "col", 0, 1536))
    z0 = matmul("l0_glu", g0, w0["s5_w_glu"], "nn", plans=plans_for("l0_glu"))
    mn0, kv0, kn0 = mem_branch_fwd("l0", mem, mem_norm[0:1], w0["w_mem_kv"], xk_norm[0:1])
    mo0 = mem_attn_fwd("l0", proj0, 3, kn0, kv0, xq_norm[0:1])
    o0 = row_fwd("l0_merge", fn_merge_glu, L, 256, [(z0, 2 * PW, 0), (mo0, XQW, 0), (proj0, BW, 1)], [], [(BW, BF16)])[0]
    around("before", "l0_out", o0)
    x1p = matmul("l0_out", o0, w0["w_out"], "nn", F32, add=xp, plans=plans_for("l0_out"))
    around("after", "l0_out", x1p)
    x1 = time_unpermute(x1p)

    w1 = dict(pair_views(env["pair1"]), mla_w_ukv=Sharded(env["ukv"], "col", 0, 256))
    w_in1, w_uq = env["w_in1"], env["w_uq"]
    h1 = row_fwd("l1_rms", fn_rms, L, 256, [(x1, D, 0)], [ln[1:2]], [(D, BF16)])[0]
    proj1 = matmul("l1_in", h1, w_in1, "nn")
    qln, kvln = env["q_lora_norm"].reshape(1, QL), env["kv_lora_norm"].reshape(1, KVL)
    cqn = row_fwd("l1_q_lora_rms", fn_rms, L, 256, [(proj1, QL, 4)], [qln], [(QL, BF16)])[0]
    ckvn = row_fwd("l1_kv_lora_rms", fn_rms, L, 256, [(proj1, KVL, 12)], [kvln], [(KVL, BF16)])[0]
    q = matmul("l1_uq", cqn, w_uq, "nn")
    kv = matmul("l1_ukv", ckvn, w1["mla_w_ukv"], "nn")
    inv_freq = ROPE_THETA ** (-jnp.arange(ROPE // 2, dtype=F32) / (ROPE // 2))
    ang = positions.astype(F32)[:, None] * inv_freq
    cos2 = jnp.tile(jnp.cos(ang), (1, 4))
    sin_signed = jnp.tile(jnp.concatenate([-jnp.sin(ang), jnp.sin(ang)], axis=1), (1, 2))
    qnn, knn = small["mla_q_nope_norm"], small["mla_k_nope_norm"]
    qrn, krn = jnp.tile(small["mla_q_rope_norm"], (1, 2)), jnp.tile(small["mla_k_rope_norm"], (1, 2))
    tp = 256
    prep_ins = [(q, rspec(tp, MH * (NOPE + ROPE))), (kv, rspec(tp, MH * 256)), (proj1, rspec(tp, 128, 26)),
                (cos2, rspec(tp, 128)), (sin_signed, rspec(tp, 128))] + [(a, cspec((1, 128))) for a in (qnn, knn, qrn, krn)]
    hq_spec = pl.BlockSpec((MH, tp, 256), lambda i: (0, i, 0))
    hv_spec = pl.BlockSpec((MH, tp, 128), lambda i: (0, i, 0))
    qf, kf, vh = stage("l1_mla_prep", fn_mla_prep, (L // tp,), prep_ins,
                       [(SDS((MH, L, 256), BF16), hq_spec), (SDS((MH, L, 256), BF16), hq_spec), (SDS((MH, L, 128), BF16), hv_spec)])
    attn, attn_lse = causal_attn(qf, kf, vh)
    mn1, kv1, kn1 = mem_branch_fwd("l1", mem, mem_norm[1:2], w1["w_mem_kv"], xk_norm[1:2])
    mo1 = mem_attn_fwd("l1", proj1, 5, kn1, kv1, xq_norm[1:2])
    o1 = row_fwd("l1_merge", fn_merge, L, 256, [(attn, PW, 0), (mo1, XQW, 0), (proj1, BW, 0)], [], [(BW, BF16)])[0]
    x2 = matmul("l1_out", o1, w1["w_out"], "nn", F32, add=x1)
    dx2, loss = loss_and_grad(x2, target)
    around("after", "loss", loss)

    do1 = matmul("l1_out_dx", dx2, w1["w_out"], "nt")
    g_pair1 = matmul("l1_out_dw", o1, dx2, "tn", out=gw["w_out"])
    dattn, dmo1, dproj1 = row_bwd("l1_merge_bwd", fn_merge, L, 256, [(attn, PW, 0), (mo1, XQW, 0), (proj1, BW, 0)], [],
                                  [(do1, BW, 0)], [True, True, {"cols": MLA_IN_P, "cb": 0, "dtype": BF16}], [])
    dproj1, dkn1, dv1, dxqn1 = mem_attn_bwd("l1", proj1, 5, kn1, kv1, xq_norm[1:2], dmo1, dproj1)
    env["g_pair1"], dmem_norm1, dxk1 = mem_branch_bwd("l1", mem, mem_norm[1:2], w1["w_mem_kv"], xk_norm[1:2], mn1, kv1, dkn1, dv1,
                                                      gw["w_mem_kv"], g_pair1)
    dqf, dkf, dvh = causal_attn_bwd(qf, kf, vh, attn, attn_lse, dattn)
    prep_diffs = [("row", SDS((L, MH * (NOPE + ROPE)), BF16), rspec(tp, MH * (NOPE + ROPE))), ("row", SDS((L, MH * 256), BF16), rspec(tp, MH * 256)),
                  ("row", SDS((L, MLA_IN_P), BF16), rspec(tp, 128, 26), {"into": dproj1}), None, None] + [("acc", (0,))] * 4
    dq, dkv, dproj1, dqnn, dknn, dqrn, dkrn = stage_bwd("l1_mla_prep_bwd", fn_mla_prep, (L // tp,), prep_ins,
                                                        [(dqf, hq_spec), (dkf, hq_spec), (dvh, hv_spec)], prep_diffs)
    dcqn = matmul("l1_uq_dx", dq, w_uq, "nt")
    env["dw_uq"] = matmul("l1_uq_dw", cqn, dq, "tn")
    dckvn = matmul("l1_ukv_dx", dkv, w1["mla_w_ukv"], "nt")
    env["g_ukv"] = matmul("l1_ukv_dw", ckvn, dkv, "tn", out=gw["mla_w_ukv"])
    dproj1, dqln = row_bwd("l1_q_lora_rms_bwd", fn_rms, L, 256, [(proj1, QL, 4)], [qln], [(dcqn, QL, 0)],
                           [{"cols": MLA_IN_P, "cb": 4, "into": dproj1, "dtype": BF16}], [True])
    dproj1, dkvln = row_bwd("l1_kv_lora_rms_bwd", fn_rms, L, 256, [(proj1, KVL, 12)], [kvln], [(dckvn, KVL, 0)],
                            [{"cols": MLA_IN_P, "cb": 12, "into": dproj1, "dtype": BF16}], [True])
    env["dw_in1"] = matmul("l1_in_dw", h1, dproj1, "tn")
    dh1 = matmul("l1_in_dx", dproj1, w_in1, "nt", plans=plans_for("l1_in_dx"))
    around("after", "l1_in_dx", dh1)
    dx1, dln1 = row_bwd("l1_rms_bwd", fn_rms, L, 256, [(x1, D, 0)], [ln[1:2]], [(dh1, D, 0)], [{"add": (dx2, D, 0)}], [True])
    dx1p = time_permute(dx1)

    do0 = matmul("l0_out_dx", dx1p, w0["w_out"], "nt", plans=plans_for("l0_out_dx"))
    g_pair0 = matmul("l0_out_dw", o0, dx1p, "tn", out=gw["w_out"])
    dz0, dmo0, dproj0 = row_bwd("l0_merge_bwd", fn_merge_glu, L, 256, [(z0, 2 * PW, 0), (mo0, XQW, 0), (proj0, BW, 1)], [],
                                [(do0, BW, 0)], [{"dtype": BF16}, True, {"cols": 2 * BW, "cb": 1, "dtype": BF16}], [])
    dproj0, dkn0, dv0, dxqn0 = mem_attn_bwd("l0", proj0, 3, kn0, kv0, xq_norm[0:1], dmo0, dproj0)
    env["g_pair0"], dmem_norm0, dxk0 = mem_branch_bwd("l0", mem, mem_norm[0:1], w0["w_mem_kv"], xk_norm[0:1], mn0, kv0, dkn0, dv0,
                                                      gw["w_mem_kv"], g_pair0)
    env["g_glu"] = matmul("l0_glu_dw", g0, dz0, "tn", out=gw["s5_w_glu"], plans=plans_for("l0_glu_dw"))
    dg0 = matmul("l0_glu_dx", dz0, w0["s5_w_glu"], "nt", plans=plans_for("l0_glu_dx"))
    around("before", "s5_backward", dg0)
    dproj0, dd, dwc_re, dwc_im, dwb_re, dwb_im, da_re, da_im = s5_backward(dg0, proj0, s_re, s_im, wb_re, wb_im, wc_re, wc_im,
                                                                           a_re_v, a_im_v, s5_d, dproj0, plans=plans_for("s5_backward"))
    around("after", "s5_backward", dd)
    env["g_in0"] = matmul("l0_in_dw", h0, dproj0, "tn", out=gw["s5_w_in"], plans=plans_for("l0_in_dw"))
    dh0 = matmul("l0_in_dx", dproj0, w_in0, "nt", plans=plans_for("l0_in_dx"))
    around("after", "l0_in_dx", dh0)
    dxp, dln0 = row_bwd("l0_rms_bwd", fn_rms, L, 256, [(xp, D, 0)], [ln[0:1]], [(dh0, D, 0)], [{"add": (dx1p, D, 0)}], [True])
    grad_x = time_unpermute(dxp)

    db_re, db_im, dcoef_re, dcoef_im = row_bwd("s5_bmat_bwd", fn_s5_bmat, SN, 512, bmat_rows, [], [(dwb_re, 128, 0), (dwb_im, 128, 0)],
                                               [True] * 4, [], plans=plans_for("s5_bmat_bwd"))
    dc_re, dc_im = row_bwd("s5_cmat_bwd", fn_s5_cmat, PW, 128, cmat_rows, [], [(dwc_re, 512, 0), (dwc_im, 512, 0)], [True] * 2, [],
                           plans=plans_for("s5_cmat_bwd"))
    disc_cts = [(da_re.reshape(SG, SP), one), (da_im.reshape(SG, SP), one), (dcoef_re.reshape(SG, SP), one), (dcoef_im.reshape(SG, SP), one)]
    dlre, dlim, dls = stage_bwd("s5_disc_bwd", fn_s5_disc, (1,), disc_ins, disc_cts, [("acc", (0,))] * 3)

    g["ln_gain"] = jnp.concatenate([dln0, dln1], axis=0)
    g["mem_norm"] = jnp.concatenate([dmem_norm0, dmem_norm1], axis=0)
    g["xq_norm"] = jnp.concatenate([dxqn0, dxqn1], axis=0)
    g["xk_norm"] = jnp.concatenate([dxk0, dxk1], axis=0)
    g["s5_lambda_re"], g["s5_lambda_im"], g["s5_log_step"] = dlre, dlim, dls
    g["s5_b_re"], g["s5_b_im"], g["s5_c_re"], g["s5_c_im"] = db_re, db_im, dc_re, dc_im
    g["s5_d"] = dd
    g["mla_q_lora_norm"], g["mla_kv_lora_norm"] = dqln, dkvln
    g["mla_q_nope_norm"], g["mla_k_nope_norm"] = dqnn, dknn
    g["mla_q_rope_norm"] = dqrn[:, :ROPE] + dqrn[:, ROPE:]
    g["mla_k_rope_norm"] = dkrn[:, :ROPE] + dkrn[:, ROPE:]
    return loss, grad_x, g


def kernel(x, mem, positions, ln_gain, w_out, mem_norm, w_mem_kv, xq_norm, xk_norm, s5_w_in, s5_lambda_re, s5_lambda_im, s5_log_step, s5_b_re, s5_b_im, s5_c_re, s5_c_im, s5_d, s5_w_glu, mla_w_in, mla_q_lora_norm, mla_kv_lora_norm, mla_w_uq, mla_w_ukv, mla_q_nope_norm, mla_k_nope_norm, mla_q_rope_norm, mla_k_rope_norm, loss_target, m_ln_gain, m_w_out, m_mem_norm, m_w_mem_kv, m_xq_norm, m_xk_norm, m_s5_w_in, m_s5_lambda_re, m_s5_lambda_im, m_s5_log_step, m_s5_b_re, m_s5_b_im, m_s5_c_re, m_s5_c_im, m_s5_d, m_s5_w_glu, m_mla_w_in, m_mla_q_lora_norm, m_mla_kv_lora_norm, m_mla_w_uq, m_mla_w_ukv, m_mla_q_nope_norm, m_mla_k_nope_norm, m_mla_q_rope_norm, m_mla_k_rope_norm, v_ln_gain, v_w_out, v_mem_norm, v_w_mem_kv, v_xq_norm, v_xk_norm, v_s5_w_in, v_s5_lambda_re, v_s5_lambda_im, v_s5_log_step, v_s5_b_re, v_s5_b_im, v_s5_c_re, v_s5_c_im, v_s5_d, v_s5_w_glu, v_mla_w_in, v_mla_q_lora_norm, v_mla_kv_lora_norm, v_mla_w_uq, v_mla_w_ukv, v_mla_q_nope_norm, v_mla_k_nope_norm, v_mla_q_rope_norm, v_mla_k_rope_norm):
    args = dict(locals())
    wts = {n: args[n] for n in WEIGHT_ORDER}
    mom = {n: args["m_" + n] for n in WEIGHT_ORDER}
    var = {n: args["v_" + n] for n in WEIGHT_ORDER}

    chip = 2 * lax.axis_index("x") + lax.axis_index("y")
    place = jnp.stack([chip, lax.axis_index("c")]).astype(jnp.int32)

    def own_slot(gathered, shards):
        return [lax.dynamic_update_slice(g, s[None], (chip, 0, 0)) for g, s in zip(gathered, shards)]

    groups = list(stack_shards(wts, BF16))
    groups[2].append(jnp.concatenate([mla_q_lora_norm, jnp.pad(mla_kv_lora_norm, ((0, 0), (0, 64))), jnp.zeros((14, 128), F32)], axis=0))
    over_ici = [plan_gather_ici(shards) for shards in groups]
    _SCHEDULE_BEHIND.clear()
    schedule_behind(split_start("gather_start", over_ici))
    env, hooks, passed_on = {}, {}, {}

    def arrived(k, after, pass_now):
        passing = plan_gather_pass(split_wait(f"gather_wait_{k}", over_ici[k], after))
        passed_on[k] = passing
        return own_slot(run_plan(f"gather_pass_{k}", passing), groups[k]) if pass_now else None

    def need_in0(env, last):
        env["in0"], = arrived(0, last, True)

    def need_layer0(env, last):
        env["pair0"], env["glu"] = arrived(1, last, True)

    def need_layer1(env, last):
        arrived(2, last, False)

    def layer1_weights(env, last):
        pair1, ukv, in1, uq, norms = own_slot(passed_on[2].results, groups[2])
        env.update(pair1=pair1, ukv=ukv, w_in1=mla_in_permute(shards_to_cols(in1)), w_uq=uq_permute(shards_to_cols(uq)),
                   q_lora_norm=norms[:, 0, :], kv_lora_norm=norms[:, 1, :64])

    hooks["before", "l0_in"], hooks["before", "l0_glu"], hooks["before", "l0_out"] = need_in0, need_layer0, need_layer1
    hooks["plans", "l0_out"], hooks["after", "l0_out"] = (lambda env: [passed_on[2]]), layer1_weights

    rs = {}

    def swap(k, gs):
        rs[k, "g"], rs[k, "swap"] = gs, plan_pair_exchange(gs)
        return rs[k, "swap"]

    def start_scatter(k):
        rs[k, "pairs"] = pair_adds(f"rs{k}", rs[k, "g"], rs[k, "swap"].results, place)
        rs[k, "scatter"] = plan_chip_scatter(rs[k, "pairs"])
        schedule_behind(split_start(f"rs{k}_scatter_start", [rs[k, "scatter"]]))

    def join(k, after):
        rs[k, "join"] = plan_pair_join(chip_adds(f"rs{k}", rs[k, "pairs"], split_wait(f"rs{k}_scatter_wait", rs[k, "scatter"], after), place))
        return rs[k, "join"]

    hooks["plans", "l1_in_dx"] = lambda env: [swap(1, [env["g_pair1"], env["g_ukv"], cols_to_shards(mla_in_unpermute(env["dw_in1"])).astype(BF16),
                                                        cols_to_shards(uq_unpermute(env["dw_uq"])).astype(BF16)])]
    hooks["after", "l1_in_dx"] = lambda env, last: start_scatter(1)
    hooks["plans", "l0_glu_dx"] = lambda env: [swap(0, [env["g_pair0"], env["g_glu"]])]

    def before_s5_backward(env, last):
        start_scatter(0)
        env["join1"] = join(1, last)

    hooks["before", "s5_backward"] = before_s5_backward
    hooks["plans", "s5_backward"] = lambda env: [env["join1"]]
    hooks["after", "s5_backward"] = lambda env, last: env.update(join0=join(0, last))
    hooks["plans", "l0_in_dx"] = lambda env: [swap(2, [env["g_in0"]]), env["join0"]]
    hooks["after", "l0_in_dx"] = lambda env, last: start_scatter(2)

    def total_loss(env, local):
        env["loss"] = lax.psum(local[0, 0], MESH_AXES)
        schedule_behind(env["loss"].reshape(1, 1))

    hooks["after", "loss"] = total_loss
    small = {n: wts[n] for n, _ in SMALL}
    loss, grad_x, g = device_step(x[0], mem[0], positions[0], loss_target[0], small, env, hooks)
    loss = env["loss"]
    r_in0, = run_plan("rs2_pair_join", join(2, g["s5_log_step"]))
    (r_pair1, r_ukv, r_in1, r_uq), (r_pair0, r_glu) = (rs[k, "join"].results for k in (1, 0))

    small_flat = jnp.concatenate([g[n].reshape(-1) for n, _ in SMALL_FULL])
    g_small = jnp.pad(small_flat, (0, 4 * SMALL_ROWS * SMALL_LANES - N_SMALL)).astype(BF16).reshape(4, SMALL_ROWS, SMALL_LANES)
    r_small = reduce_scatter_chips("rs3", [g_small], place)[0]
    small_all = own_slot(all_gather_chips("gather_small_grads", [r_small]), [r_small])[0].reshape(-1)[:N_SMALL]

    grads = {"w_out": jnp.stack([r_pair0[:PAIR_MKV], r_pair1[:PAIR_MKV]]), "w_mem_kv": jnp.stack([r_pair0[PAIR_MKV:], r_pair1[PAIR_MKV:]]),
             "s5_w_in": r_in0[None], "s5_w_glu": r_glu[None], "mla_w_ukv": r_ukv[None], "mla_w_in": r_in1[None], "mla_w_uq": r_uq[None]}
    off = 0
    for n, s in SMALL_FULL:
        grads[n] = small_all[off:off + math.prod(s)].reshape(s)
        off += math.prod(s)
    for n, s in SHARDED_SMALL:
        grads[n] = lax.dynamic_slice(grads[n], (0, chip * s[1]), s)

    delta, new_m, new_v = {}, {}, {}
    for n, s in BIG + [(n, s) for n, s in SMALL if len(s) == 4]:
        perm = MINOR_LAST.get(n, tuple(range(len(s))))
        turned = tuple(s[p] for p in perm)
        view = lambda a: jnp.transpose(a, perm).reshape(-1, turned[-1])
        res = adamw("adamw_" + n, view(wts[n]), view(grads[n]), view(mom[n]), view(var[n]))
        delta[n], new_m[n], new_v[n] = (jnp.transpose(r.reshape(turned), tuple(perm.index(i) for i in range(len(s)))) for r in res)
    small_names = [n for n, s in SMALL if len(s) < 4] + [n for n, _ in SHARDED_SMALL]
    n_own = sum(wts[n].size for n in small_names)
    rows_own = -(-n_own // (8 * 128)) * 8

    def pack_small(d):
        flat = jnp.concatenate([d[n].reshape(-1) for n in small_names])
        return jnp.pad(flat, (0, rows_own * 128 - n_own), constant_values=1.0).reshape(rows_own, 128)

    res = adamw("adamw_small", pack_small(wts), pack_small(grads), pack_small(mom), pack_small(var))
    off = 0
    for n in small_names:
        size = wts[n].size
        delta[n], new_m[n], new_v[n] = (r.reshape(-1)[off:off + size].reshape(wts[n].shape) for r in res)
        off += size

    return (loss, grad_x[None], *[grads[n] for n in WEIGHT_ORDER], *[delta[n] for n in WEIGHT_ORDER],
            *[new_m[n] for n in WEIGHT_ORDER], *[new_v[n] for n in WEIGHT_ORDER])
```

```python
import functools
import math

import jax
import jax.numpy as jnp
from jax import lax
from jax.experimental import pallas as pl
from jax.experimental.pallas import tpu as pltpu

F32, BF16 = jnp.float32, jnp.bfloat16
SDS = jax.ShapeDtypeStruct

D = 1024
L = 2048
ML = 256
BW = 2 * D
XQW = BW // 4
PW = BW - XQW
XH, XHD = 4, 128
SG, SC, SP = 96, 16, 64
SN = SG * SP
NOPE, ROPE, VD = 128, 64, 128
MH = 12
QL, KVL = 512, 256
EPS = 1e-6
ROPE_THETA = 10000.0
MLA_IN = QL + KVL + ROPE + XQW + BW
MLA_IN_P = 3456
ADAM_LR, ADAM_B1, ADAM_B2, ADAM_EPS, ADAM_WD, ADAM_STEP = 0.001, 0.9, 0.999, 1e-08, 0.01, 10

VMEM_LIMIT = 48 * 2**20
ROW_TILE = 512
SEG = 8
SEG_LEN = L // SEG
MESH_AXES = ("x", "y", "c")


def _cparams():
    return pltpu.CompilerParams(vmem_limit_bytes=VMEM_LIMIT)


def _dg(a, b, ca, cb):
    return lax.dot_general(a.astype(BF16), b.astype(BF16), (((ca,), (cb,)), ((), ())), preferred_element_type=F32)


@jax.custom_vjp
def mm_nn(a, b):
    return _dg(a, b, 1, 0)


mm_nn.defvjp(lambda a, b: (_dg(a, b, 1, 0), (a, b)), lambda res, g: (_dg(g, res[1], 1, 1), _dg(res[0], g, 0, 0)))


@jax.custom_vjp
def mm_nt(a, b):
    return _dg(a, b, 1, 1)


mm_nt.defvjp(lambda a, b: (_dg(a, b, 1, 1), (a, b)), lambda res, g: (_dg(g, res[1], 1, 0), _dg(g, res[0], 0, 0)))


@functools.partial(jax.custom_vjp, nondiff_argnums=(1,))
def lane_roll(x, shift):
    return pltpu.roll(x, shift, 1)


lane_roll.defvjp(lambda x, shift: (pltpu.roll(x, shift, 1), None),
                 lambda shift, _, g: (pltpu.roll(g, (128 - shift) % 128, 1),))


def rms(x, g):
    return x * lax.rsqrt(jnp.mean(x * x, axis=-1, keepdims=True) + EPS) * g


@jax.custom_vjp
def softmax_rows(s):
    e = jnp.exp(s - jnp.max(s, axis=-1, keepdims=True))
    return e / jnp.sum(e, axis=-1, keepdims=True)


def _softmax_rows_fwd(s):
    p = softmax_rows(s)
    return p, p


def _softmax_rows_bwd(p, g):
    return (p * (g - jnp.sum(g * p, axis=-1, keepdims=True)),)


softmax_rows.defvjp(_softmax_rows_fwd, _softmax_rows_bwd)


def silu(x):
    return x * jax.nn.sigmoid(x)


ANY = pl.BlockSpec(memory_space=pl.ANY)
MESH_ID = pl.DeviceIdType.MESH


def _dma_sems(n):
    return [pltpu.SemaphoreType.DMA((n,)), pltpu.SemaphoreType.DMA((n,))]


class Plan:
    def __init__(self, operands, out_shape, aliases, n_sems, copies):
        self.operands, self.out_shape, self.aliases, self.n_sems, self.copies = list(operands), list(out_shape), aliases, n_sems, copies
        self.results = None


_SCHEDULE_BEHIND = []


def schedule_behind(token):
    _SCHEDULE_BEHIND.append(token)


def hosted_call(kern, *, name, grid, in_specs, out_specs, out_shape, operands, scratch_shapes=(), aliases=None, cparams=None, plans=(), deps=()):
    n_in, n_out, n_scr = len(in_specs), len(out_specs), len(scratch_shapes)
    p_in, p_out = [len(p.operands) for p in plans], [len(p.out_shape) for p in plans]
    deps = tuple(deps) + tuple(_SCHEDULE_BEHIND)
    _SCHEDULE_BEHIND.clear()
    all_aliases = dict(aliases or {})
    in_off, out_off = n_in, n_out
    for p, ni, no in zip(plans, p_in, p_out):
        all_aliases.update({in_off + i: out_off + o for i, o in p.aliases.items()})
        in_off, out_off = in_off + ni, out_off + no

    def body(*refs):
        pos, pins, pouts = n_in, [], []
        for ni in p_in:
            pins.append(refs[pos:pos + ni])
            pos += ni
        pos += len(deps)
        main_out = refs[pos:pos + n_out]
        pos += n_out
        for no in p_out:
            pouts.append(refs[pos:pos + no])
            pos += no
        main_scr = refs[pos:pos + n_scr]
        pos += n_scr
        if plans:
            ids = [pl.program_id(ax) for ax in range(len(grid))]
            first = functools.reduce(jnp.logical_and, [i == 0 for i in ids])
            last = functools.reduce(jnp.logical_and, [i == g - 1 for i, g in zip(ids, grid)])
            copies = [p.copies(pins[k], pouts[k], refs[pos + 2 * k], refs[pos + 2 * k + 1]) for k, p in enumerate(plans)]

            @pl.when(first)
            def _():
                for sends, _ in copies:
                    for cp in sends:
                        cp.start()

        kern(*refs[:n_in], *main_out, *main_scr)
        if plans:
            @pl.when(last)
            def _():
                for sends, recvs in copies:
                    for cp in recvs:
                        cp.wait_recv()
                    for cp in sends:
                        cp.wait_send()

    res = pl.pallas_call(body, grid=grid, in_specs=list(in_specs) + [ANY] * (sum(p_in) + len(deps)),
                         out_specs=list(out_specs) + [ANY] * sum(p_out), out_shape=list(out_shape) + [s for p in plans for s in p.out_shape],
                         scratch_shapes=list(scratch_shapes) + [s for p in plans for s in _dma_sems(p.n_sems)],
                         input_output_aliases=all_aliases, name=name, compiler_params=cparams or _cparams())(
        *operands, *[a for p in plans for a in p.operands], *deps)
    pos = n_out
    for p, no in zip(plans, p_out):
        p.results = list(res[pos:pos + no])
        pos += no
    return list(res[:n_out])


def _wide(v):
    return v.astype(F32) if v.dtype == BF16 else v


def stage(name, fn, grid, ins, outs):
    n_in = len(ins)

    def kern(*refs):
        res = fn(*[_wide(r[...]) for r in refs[:n_in]])
        for r, v in zip(refs[n_in:], res):
            r[...] = v.astype(r.dtype)

    return hosted_call(kern, name=name, grid=grid, in_specs=[s for _, s in ins], out_specs=[s for _, s in outs],
                       out_shape=[sd for sd, _ in outs], operands=[a for a, _ in ins])


def stage_bwd(name, fn, grid, ins, cts, diffs, plans=()):
    n_in, n_ct = len(ins), len(cts)
    didx = [i for i, d in enumerate(diffs) if d is not None]
    opts = {i: (diffs[i][3] if len(diffs[i]) > 3 else {}) for i in didx if diffs[i][0] == "row"}
    adds = [(i, opts[i]["add"]) for i in opts if "add" in opts[i]]
    intos = [(i, opts[i]["into"]) for i in opts if "into" in opts[i]]
    n_add, n_into = len(adds), len(intos)
    add_pos = {i: n_in + n_ct + k for k, (i, _) in enumerate(adds)}
    n_extra = n_in + n_ct + n_add + n_into

    def kern(*refs):
        vals = [_wide(r[...]) for r in refs[:n_in]]

        def f(*dv):
            full = list(vals)
            for i, v in zip(didx, dv):
                full[i] = v
            return fn(*full)

        _, vjp = jax.vjp(f, *[vals[i].astype(F32) for i in didx])
        gs = vjp(tuple(c[...].astype(F32) for c in refs[n_in:n_in + n_ct]))
        for o_ref, i, g in zip(refs[n_extra:], didx, gs):
            if diffs[i][0] == "row":
                if i in add_pos:
                    g = g + refs[add_pos[i]][...].astype(F32)
                o_ref[...] = g.astype(o_ref.dtype)
            else:
                first = functools.reduce(jnp.logical_and, [pl.program_id(ax) == 0 for ax in diffs[i][1]])

                @pl.when(first)
                def _():
                    o_ref[...] = g

                @pl.when(jnp.logical_not(first))
                def _():
                    o_ref[...] += g

    out_shape, out_specs = [], []
    for i in didx:
        if diffs[i][0] == "row":
            out_shape.append(diffs[i][1])
            out_specs.append(diffs[i][2])
        else:
            out_shape.append(SDS(ins[i][0].shape, F32))
            out_specs.append(ins[i][1])
    aliases = {n_in + n_ct + n_add + k: didx.index(i) for k, (i, _) in enumerate(intos)}
    in_specs = [s for _, s in ins] + [s for _, s in cts] + [s for _, (_, s) in adds] + [ANY] * n_into
    operands = [a for a, _ in ins] + [a for a, _ in cts] + [a for _, (a, _) in adds] + [a for _, a in intos]
    return hosted_call(kern, name=name, grid=grid, in_specs=in_specs, out_specs=out_specs, out_shape=out_shape, operands=operands,
                       aliases=aliases, plans=plans)


def rspec(tl, w, cb=0):
    return pl.BlockSpec((tl, w), lambda i: (i, cb))


def cspec(shape):
    return pl.BlockSpec(shape, lambda i: (0,) * len(shape))


def row_fwd(name, fn, rows, tl, row_ins, consts, outs):
    ins = [(a, rspec(tl, w, cb)) for a, w, cb in row_ins] + [(a, cspec(a.shape)) for a in consts]
    return stage(name, fn, (rows // tl,), ins, [(SDS((rows, w), dt), rspec(tl, w)) for w, dt in outs])


def row_bwd(name, fn, rows, tl, row_ins, consts, cts, row_diff, const_diff, plans=()):
    ins = [(a, rspec(tl, w, cb)) for a, w, cb in row_ins] + [(a, cspec(a.shape)) for a in consts]
    diffs = []
    for (a, w, cb), d in zip(row_ins, row_diff):
        if not d:
            diffs.append(None)
            continue
        d = d if isinstance(d, dict) else {}
        opts = {}
        if "add" in d:
            opts["add"] = (d["add"][0], rspec(tl, d["add"][1], d["add"][2]))
        if d.get("into") is not None:
            opts["into"] = d["into"]
        diffs.append(("row", SDS((rows, d.get("cols", w)), d.get("dtype", F32)), rspec(tl, w, d.get("cb", 0)), opts))
    diffs += [("acc", (0,)) if d else None for d in const_diff]
    return stage_bwd(name, fn, (rows // tl,), ins, [(a, rspec(tl, w, cb)) for a, w, cb in cts], diffs, plans=plans)


MATMUL_VMEM = 36 * 2**20
ADAMW_VMEM = 28 * 2**20


class Sharded:
    def __init__(self, arr, kind, roff, rows):
        self.arr, self.kind, self.roff, self.rows, self.n = arr, kind, roff, rows, arr.shape[2]
        self.shape = (rows, 4 * self.n) if kind == "col" else (4 * rows, self.n)

    def fits(self, t0, t1):
        return self.roff % t0 == 0 and self.rows % t0 == 0 and self.n % t1 == 0

    def spec(self, t0, t1, bidx):
        assert self.fits(t0, t1), (self.kind, self.roff, self.rows, self.n, t0, t1)
        r0 = self.roff // t0
        if self.kind == "col":
            per = self.n // t1
            return pl.BlockSpec((None, t0, t1), lambda *g: (bidx(*g)[1] // per, r0 + bidx(*g)[0], bidx(*g)[1] % per))
        per = self.rows // t0
        return pl.BlockSpec((None, t0, t1), lambda *g: (bidx(*g)[0] // per, r0 + bidx(*g)[0] % per, bidx(*g)[1]))


def matmul(name, a, b, mode, out_dtype=BF16, add=None, out=None, into=None, plans=()):
    if mode == "tn":
        k_dim, m = a.shape
    else:
        m, k_dim = a.shape
    n = b.shape[0] if mode == "nt" else b.shape[1]
    b_fit = b.fits if isinstance(b, Sharded) else (lambda t0, t1: True)
    o_fit = out.fits if out is not None else (lambda t0, t1: True)
    a_bytes, b_bytes = jnp.dtype(a.dtype).itemsize, jnp.dtype(b.arr.dtype if isinstance(b, Sharded) else b.dtype).itemsize
    o_bytes = jnp.dtype(out_dtype if out is None else out.arr.dtype).itemsize

    def vmem(tm, tn, tk):
        return 2 * (tm * tk * a_bytes + tk * tn * b_bytes + tm * tn * (o_bytes + (4 if add is not None else 0))) + 4 * tm * tn * (1 + (tk < k_dim))

    tiles = [(tm, tn, tk) for tm in (2048, 1024, 512, 256, 128) for tn in (1024, 768, 512, 384, 256, 128)
             for tk in sorted({k_dim, 1024, 768, 512, 384, 256, 128})
             if m % tm == 0 and n % tn == 0 and k_dim % tk == 0 and (b_fit(tn, tk) if mode == "nt" else b_fit(tk, tn)) and o_fit(tm, tn)
             and vmem(tm, tn, tk) <= MATMUL_VMEM]
    tm, tn, tk = max(tiles, key=lambda t: (t[2] == k_dim, t[0] * t[1] * t[2], t[0] * t[1]))
    nk = k_dim // tk
    a_spec = pl.BlockSpec((tk, tm), lambda i, j, k: (k, i)) if mode == "tn" else pl.BlockSpec((tm, tk), lambda i, j, k: (i, k))
    if isinstance(b, Sharded):
        b_spec = b.spec(tn, tk, lambda i, j, k: (j, k)) if mode == "nt" else b.spec(tk, tn, lambda i, j, k: (k, j))
        b = b.arr
    else:
        b_spec = pl.BlockSpec((tn, tk), lambda i, j, k: (j, k)) if mode == "nt" else pl.BlockSpec((tk, tn), lambda i, j, k: (k, j))
    o_spec = pl.BlockSpec((tm, tn), lambda i, j, k: (i, j))
    out_spec, out_shape = (o_spec, SDS((m, n), out_dtype)) if out is None else (out.spec(tm, tn, lambda i, j, k: (i, j)), out.arr)
    ca, cb = {"nn": (1, 0), "nt": (1, 1), "tn": (0, 0)}[mode]
    n_in = 2 + (add is not None)

    def finish(refs, o_ref, r):
        if add is not None:
            r = r + refs[2][...]
        o_ref[...] = r.astype(o_ref.dtype)

    def kern_whole(*refs):
        finish(refs, refs[-1], _dg(refs[0][...], refs[1][...], ca, cb))

    def kern_cut(*refs):
        o_ref, acc = refs[-2], refs[-1]
        k = pl.program_id(2)

        @pl.when(k == 0)
        def _():
            acc[...] = jnp.zeros_like(acc)

        acc[...] += _dg(refs[0][...], refs[1][...], ca, cb)

        @pl.when(k == nk - 1)
        def _():
            finish(refs, o_ref, acc[...])

    ins, specs = [a, b], [a_spec, b_spec]
    if add is not None:
        ins.append(add)
        specs.append(o_spec)
    if into is not None:
        ins.append(into)
        specs.append(ANY)
    return hosted_call(kern_whole if nk == 1 else kern_cut, name=name, grid=(m // tm, n // tn, nk), in_specs=specs, out_specs=[out_spec],
                       out_shape=[out_shape], operands=ins, scratch_shapes=[] if nk == 1 else [pltpu.VMEM((tm, tn), F32)],
                       aliases={} if into is None else {n_in: 0}, plans=plans)[0]


SCAN_UNROLL = 8


def _cmul(ar, ai, br, bi):
    return ar * br - ai * bi, ar * bi + ai * br


def _sub_shift(x, down):
    row = lax.broadcasted_iota(jnp.int32, x.shape, 0)
    if down:
        return jnp.where(row == 0, 0.0, pltpu.roll(x, 1, 0))
    return jnp.where(row == SEG - 1, 0.0, pltpu.roll(x, SEG - 1, 0))


def _pow_seg_len(ar, ai):
    for _ in range(int(math.log2(SEG_LEN))):
        ar, ai = _cmul(ar, ai, ar, ai)
    return ar, ai


def _scan_in_place(sr, si, a_re, a_im):
    lanes = sr.shape[1]
    ar = jnp.broadcast_to(a_re, (SEG, lanes))
    ai = jnp.broadcast_to(a_im, (SEG, lanes))
    zero = jnp.zeros((SEG, lanes), F32)

    def local(i, carry):
        rows = pl.ds(pl.multiple_of(i * SEG, SEG), SEG)
        mr, mi = _cmul(ar, ai, carry[0], carry[1])
        nr, ni = mr + sr[rows, :], mi + si[rows, :]
        sr[rows, :] = nr
        si[rows, :] = ni
        return nr, ni

    fr, fi = lax.fori_loop(0, SEG_LEN, local, (zero, zero), unroll=SCAN_UNROLL)
    pr, pi = _pow_seg_len(ar, ai)
    ir, ii = zero, zero
    for _ in range(SEG - 1):
        mr, mi = _cmul(pr, pi, ir, ii)
        ir, ii = _sub_shift(mr + fr, True), _sub_shift(mi + fi, True)

    def carry_in(i, pw):
        rows = pl.ds(pl.multiple_of(i * SEG, SEG), SEG)
        cr, ci = _cmul(pw[0], pw[1], ir, ii)
        sr[rows, :] += cr
        si[rows, :] += ci
        return _cmul(pw[0], pw[1], ar, ai)

    lax.fori_loop(0, SEG_LEN, carry_in, (ar, ai), unroll=SCAN_UNROLL)


S5_LANES = 8 * SP
S5_BLOCKS = SN // S5_LANES


def _s5_specs():
    u_spec = pl.BlockSpec((L, 8 * SC), lambda j: (0, j))
    s_spec = pl.BlockSpec((L, S5_LANES), lambda j: (0, j))
    wb_spec = pl.BlockSpec((S5_LANES, 8 * SC), lambda j: (j, 0))
    wc_spec = pl.BlockSpec((8 * SC, S5_LANES), lambda j: (j, 0))
    a_spec = pl.BlockSpec((1, S5_LANES), lambda j: (0, j))
    d_spec = pl.BlockSpec((1, 8 * SC), lambda j: (0, j))
    return u_spec, s_spec, wb_spec, wc_spec, a_spec, d_spec


def s5_forward(proj, wb_re, wb_im, wc_re, wc_im, a_re, a_im, d, plans=()):
    def kern(u_ref, wbr, wbi, wcr, wci, ar, ai, d_ref, sr_out, si_out, g_ref, sr, si):
        u = _wide(u_ref[...])
        sr[...], si[...] = fn_s5_bu(u, wbr[...], wbi[...])
        _scan_in_place(sr, si, ar[...], ai[...])
        g_ref[...] = fn_s5_out(sr[...], si[...], u, d_ref[...], wcr[...], wci[...])[0].astype(g_ref.dtype)
        sr_out[...] = sr[...].astype(sr_out.dtype)
        si_out[...] = si[...].astype(si_out.dtype)

    u_spec, s_spec, wb_spec, wc_spec, a_spec, d_spec = _s5_specs()
    return hosted_call(kern, name="s5_forward", grid=(S5_BLOCKS,), in_specs=[u_spec, wb_spec, wb_spec, wc_spec, wc_spec, a_spec, a_spec, d_spec],
                       out_specs=[s_spec, s_spec, u_spec], out_shape=[SDS((L, SN), BF16)] * 2 + [SDS((L, PW), BF16)],
                       operands=[proj, wb_re, wb_im, wc_re, wc_im, a_re, a_im, d], scratch_shapes=[pltpu.VMEM((L, S5_LANES), F32)] * 2,
                       plans=plans)


def _adjoint_scan_in_place(lr, li, sr, si, a_re, a_im):
    lanes = lr.shape[1]
    ar = jnp.broadcast_to(a_re, (SEG, lanes))
    ai = -jnp.broadcast_to(a_im, (SEG, lanes))
    zero = jnp.zeros((SEG, lanes), F32)

    def local(k, carry):
        i = SEG_LEN - 1 - k
        rows = pl.ds(pl.multiple_of(i * SEG, SEG), SEG)
        mr, mi = _cmul(ar, ai, carry[0], carry[1])
        nr, ni = mr + lr[rows, :], mi + li[rows, :]
        lr[rows, :] = nr
        li[rows, :] = ni
        return nr, ni

    fr, fi = lax.fori_loop(0, SEG_LEN, local, (zero, zero), unroll=SCAN_UNROLL)
    pr, pi = _pow_seg_len(ar, ai)
    ir, ii = zero, zero
    for _ in range(SEG - 1):
        mr, mi = _cmul(pr, pi, ir, ii)
        ir, ii = _sub_shift(mr + fr, False), _sub_shift(mi + fi, False)

    def fix(rows, pw):
        cr, ci = _cmul(pw[0], pw[1], ir, ii)
        tr, ti = lr[rows, :] + cr, li[rows, :] + ci
        lr[rows, :] = tr
        li[rows, :] = ti
        return tr, ti

    def grad_a(tr, ti, spr, spi, acc):
        return acc[0] + tr * spr + ti * spi, acc[1] + ti * spr - tr * spi

    def carry_in(k, c):
        i = SEG_LEN - 1 - k
        rows = pl.ds(pl.multiple_of(i * SEG, SEG), SEG)
        prev = pl.ds(pl.multiple_of((i - 1) * SEG, SEG), SEG)
        tr, ti = fix(rows, (c[0], c[1]))
        acc = grad_a(tr, ti, sr[prev, :], si[prev, :], (c[2], c[3]))
        nr, ni = _cmul(c[0], c[1], ar, ai)
        return nr, ni, acc[0], acc[1]

    pwr, pwi, accr, acci = lax.fori_loop(0, SEG_LEN - 1, carry_in, (ar, ai, zero, zero), unroll=5)
    tr, ti = fix(pl.ds(0, SEG), (pwr, pwi))
    last = pl.ds((SEG_LEN - 1) * SEG, SEG)
    accr, acci = grad_a(tr, ti, _sub_shift(sr[last, :], True), _sub_shift(si[last, :], True), (accr, acci))
    return jnp.sum(accr, axis=0, keepdims=True), jnp.sum(acci, axis=0, keepdims=True)


S5_BWD_VMEM = 58 * 2**20


def s5_backward(dg, proj, s_re, s_im, wb_re, wb_im, wc_re, wc_im, a_re, a_im, d, dproj, plans=()):
    def kern(dg_ref, u_ref, sr_in, si_in, wbr, wbi, wcr, wci, ar, ai, d_ref, _, du_ref, dd_ref, dwcr, dwci, dwbr, dwbi, dar, dai, lr, li, sr, si):
        u = _wide(u_ref[...])
        sr[...], si[...] = _wide(sr_in[...]), _wide(si_in[...])
        _, vjp_out = jax.vjp(fn_s5_out, sr[...], si[...], u, d_ref[...], wcr[...], wci[...])
        lr[...], li[...], du_out, dd_ref[...], dwcr[...], dwci[...] = vjp_out((_wide(dg_ref[...]),))
        dar[...], dai[...] = _adjoint_scan_in_place(lr, li, sr, si, ar[...], ai[...])
        _, vjp_in = jax.vjp(fn_s5_bu, u, wbr[...], wbi[...])
        du_in, dwbr[...], dwbi[...] = vjp_in((lr[...], li[...]))
        du_ref[...] = (du_out + du_in).astype(du_ref.dtype)

    u_spec, s_spec, wb_spec, wc_spec, a_spec, d_spec = _s5_specs()
    outs = [(SDS(dproj.shape, dproj.dtype), u_spec), (SDS(d.shape, F32), d_spec), (SDS(wc_re.shape, F32), wc_spec), (SDS(wc_im.shape, F32), wc_spec),
            (SDS(wb_re.shape, F32), wb_spec), (SDS(wb_im.shape, F32), wb_spec), (SDS(a_re.shape, F32), a_spec), (SDS(a_im.shape, F32), a_spec)]
    return hosted_call(kern, name="s5_backward", grid=(S5_BLOCKS,),
                       in_specs=[u_spec, u_spec, s_spec, s_spec, wb_spec, wb_spec, wc_spec, wc_spec, a_spec, a_spec, d_spec, ANY],
                       out_specs=[sp for _, sp in outs], out_shape=[sd for sd, _ in outs], aliases={11: 0},
                       operands=[dg, proj, s_re, s_im, wb_re, wb_im, wc_re, wc_im, a_re, a_im, d, dproj],
                       scratch_shapes=[pltpu.VMEM((L, S5_LANES), F32)] * 4,
                       cparams=pltpu.CompilerParams(vmem_limit_bytes=S5_BWD_VMEM), plans=plans)


def fn_rms(x, g):
    return (rms(x, g),)


def fn_s5_disc(lre, lim, ls):
    step = jnp.exp(ls)
    e = jnp.exp(lre * step)
    a_re, a_im = e * jnp.cos(lim * step), e * jnp.sin(lim * step)
    den = lre * lre + lim * lim
    nr, ni = a_re - 1.0, a_im
    return a_re, a_im, (nr * lre + ni * lim) / den, (ni * lre - nr * lim) / den


def _group_mask(rows, cols, row_div, col_div):
    r = lax.broadcasted_iota(jnp.int32, (rows, cols), 0) // row_div % 8
    c = lax.broadcasted_iota(jnp.int32, (rows, cols), 1) // col_div
    return r == c


def _spread(x, mask):
    w = x.shape[1]
    copy = (lax.broadcasted_iota(jnp.int32, (w, 8 * w), 1) % w == lax.broadcasted_iota(jnp.int32, (w, 8 * w), 0)).astype(F32)
    return jnp.where(mask, jnp.dot(x, copy, precision=lax.Precision.HIGHEST, preferred_element_type=F32), 0.0)


def fn_s5_bmat(b_re, b_im, coef_re, coef_im):
    mask = _group_mask(b_re.shape[0], 8 * SC, SP, SC)
    return _spread(coef_re * b_re - coef_im * b_im, mask), _spread(coef_re * b_im + coef_im * b_re, mask)


def fn_s5_cmat(c_re, c_im):
    mask = _group_mask(c_re.shape[0], 8 * SP, SC, SP)
    return _spread(c_re, mask), _spread(c_im, mask)


def fn_s5_bu(u, wb_re, wb_im):
    return mm_nt(u, wb_re), mm_nt(u, wb_im)


def fn_s5_out(sr, si, u, d, wc_re, wc_im):
    y = mm_nt(sr, wc_re) - mm_nt(si, wc_im) + d * u
    return (jax.nn.gelu(y),)


def fn_merge_glu(z, mo, gate):
    yg = z[:, :PW] * jax.nn.sigmoid(z[:, PW:])
    return (jnp.concatenate([yg, mo], axis=1) * silu(gate),)


def fn_merge(prim, mo, gate):
    return (jnp.concatenate([prim, mo], axis=1) * silu(gate),)


def fn_mem_k(kv, g):
    return (jnp.concatenate([rms(kv[:, h * XHD:(h + 1) * XHD], g) for h in range(XH)], axis=1),)


def fn_mem_attn(xq, kn, v, g):
    outs = []
    for h in range(XH):
        sl = slice(h * XHD, (h + 1) * XHD)
        p = softmax_rows(mm_nt(rms(xq[:, sl], g), kn[:, sl]) * (XHD ** -0.5))
        outs.append(mm_nn(p, v[:, sl]))
    return (jnp.concatenate(outs, axis=1),)


def _half_rms(x, g):
    lo = lax.broadcasted_iota(jnp.int32, x.shape, 1) < ROPE
    x2 = x * x
    s_lo = jnp.sum(jnp.where(lo, x2, 0.0), axis=1, keepdims=True)
    s_hi = jnp.sum(jnp.where(lo, 0.0, x2), axis=1, keepdims=True)
    return x * lax.rsqrt(jnp.where(lo, s_lo, s_hi) / ROPE + EPS) * g


def _rope(x, cos2, sin_signed):
    first = lax.broadcasted_iota(jnp.int32, x.shape, 1) % ROPE < ROPE // 2
    return x * cos2 + jnp.where(first, lane_roll(x, 128 - ROPE // 2), lane_roll(x, ROPE // 2)) * sin_signed


def fn_mla_prep(q, kv, kr, cos2, sin_signed, qnn, knn, qrn, krn):
    lo = lax.broadcasted_iota(jnp.int32, kr.shape, 1) < ROPE
    kr_pad = jnp.where(lo, _rope(_half_rms(kr, krn), cos2, sin_signed), 0.0)
    qf, kf, vs = [], [], []
    for m in range(MH // 2):
        pair = _rope(_half_rms(q[:, MH * NOPE + 128 * m:MH * NOPE + 128 * (m + 1)], qrn), cos2, sin_signed)
        for h, rope_h in ((2 * m, pair), (2 * m + 1, lane_roll(pair, ROPE))):
            qf.append(jnp.concatenate([rms(q[:, NOPE * h:NOPE * (h + 1)], qnn), jnp.where(lo, rope_h, 0.0)], axis=1))
    for h in range(MH):
        kf.append(jnp.concatenate([rms(kv[:, 256 * h:256 * h + NOPE], knn), kr_pad], axis=1))
        vs.append(kv[:, 256 * h + NOPE:256 * (h + 1)])
    return jnp.stack(qf), jnp.stack(kf), jnp.stack(vs)


ATT_TQ = 512


def _attn_scores(q, kf):
    tq = q.shape[0]
    scale = (NOPE + ROPE) ** -0.5
    own = _dg(q, kf[-tq:], 1, 1) * scale
    own = jnp.where(lax.broadcasted_iota(jnp.int32, own.shape, 1) <= lax.broadcasted_iota(jnp.int32, own.shape, 0), own, jnp.finfo(F32).min)
    return own if kf.shape[0] == tq else jnp.concatenate([_dg(q, kf[:-tq], 1, 1) * scale, own], axis=1)


def _attn_specs():
    q_spec = pl.BlockSpec((None, ATT_TQ, 256), lambda h, i: (h, i, 0))
    k_spec = pl.BlockSpec((None, L, 256), lambda h, i: (h, 0, 0))
    v_spec = pl.BlockSpec((None, L, 128), lambda h, i: (h, 0, 0))
    o_spec = pl.BlockSpec((ATT_TQ, 128), lambda h, i: (i, h))
    lse_spec = pl.BlockSpec((None, ATT_TQ, 1), lambda h, i: (h, i, 0))
    return q_spec, k_spec, v_spec, o_spec, lse_spec


def causal_attn(qf, kf, vh):
    n_tiles = L // ATT_TQ

    def kern(q_ref, k_ref, v_ref, o_ref, lse_ref):
        i = pl.program_id(1)
        for t in range(n_tiles):
            @pl.when(i == t)
            def _(t=t):
                keys = (t + 1) * ATT_TQ
                s = _attn_scores(q_ref[...], k_ref[:keys, :])
                m = jnp.max(s, axis=-1, keepdims=True)
                e = jnp.exp(s - m)
                total = jnp.sum(e, axis=-1, keepdims=True)
                o_ref[...] = (_dg(e, v_ref[:keys, :], 1, 0) / total).astype(o_ref.dtype)
                lse_ref[...] = m + jnp.log(total)

    q_spec, k_spec, v_spec, o_spec, lse_spec = _attn_specs()
    return pl.pallas_call(kern, grid=(MH, n_tiles), in_specs=[q_spec, k_spec, v_spec], out_specs=[o_spec, lse_spec],
                          out_shape=[SDS((L, MH * VD), F32), SDS((MH, L, 1), F32)], name="l1_attn", compiler_params=_cparams())(qf, kf, vh)


def causal_attn_bwd(qf, kf, vh, out, lse, dout):
    n_tiles = L // ATT_TQ
    scale = (NOPE + ROPE) ** -0.5

    def kern(q_ref, k_ref, v_ref, o_ref, lse_ref, do_ref, dq_ref, dk_ref, dv_ref):
        i = pl.program_id(1)

        @pl.when(i == 0)
        def _():
            dk_ref[...] = jnp.zeros_like(dk_ref)
            dv_ref[...] = jnp.zeros_like(dv_ref)

        for t in range(n_tiles):
            @pl.when(i == t)
            def _(t=t):
                keys = (t + 1) * ATT_TQ
                q, k, v, do = q_ref[...], k_ref[:keys, :], v_ref[:keys, :], do_ref[...]
                p = jnp.exp(_attn_scores(q, k) - lse_ref[...])
                delta = jnp.sum(do * _wide(o_ref[...]), axis=-1, keepdims=True)
                dv_ref[:keys, :] += _dg(p, do, 0, 0)
                ds = p * (_dg(do, v, 1, 1) - delta) * scale
                dq_ref[...] = _dg(ds, k, 1, 0)
                dk_ref[:keys, :] += _dg(ds, q, 0, 0)

    q_spec, k_spec, v_spec, o_spec, lse_spec = _attn_specs()
    return pl.pallas_call(kern, grid=(MH, n_tiles), in_specs=[q_spec, k_spec, v_spec, o_spec, lse_spec, o_spec],
                          out_specs=[q_spec, k_spec, v_spec], out_shape=[SDS(qf.shape, F32), SDS(kf.shape, F32), SDS(vh.shape, F32)],
                          name="l1_attn_bwd", compiler_params=_cparams())(qf, kf, vh, out, lse, dout)


def loss_and_grad(y, target, tl=512):
    def kern(y_ref, t_ref, dy_ref, loss_ref):
        d = y_ref[...] - t_ref[...]
        dy_ref[...] = d / D

        @pl.when(pl.program_id(0) == 0)
        def _():
            loss_ref[...] = jnp.zeros_like(loss_ref)

        loss_ref[...] += 0.5 * jnp.sum(jnp.sum(d * d, axis=1, keepdims=True), axis=0, keepdims=True) / D

    return pl.pallas_call(kern, grid=(L // tl,), in_specs=[rspec(tl, D), rspec(tl, D)], out_specs=[rspec(tl, D), cspec((1, 1))],
                          out_shape=[SDS((L, D), F32), SDS((1, 1), F32)], name="loss", compiler_params=_cparams())(y, target)


def adamw(name, w, g, m, v):
    rows, cols = w.shape
    block_row_bytes = 7 * 2 * 4 * max(cols, 128)
    tr = _row_tile(rows, min(2048, ADAMW_VMEM // block_row_bytes // 8 * 8), 8)

    def kern(w_ref, g_ref, m_ref, v_ref, d_ref, nm_ref, nv_ref):
        gg = g_ref[...]
        nm = ADAM_B1 * m_ref[...] + (1.0 - ADAM_B1) * gg
        nv = ADAM_B2 * v_ref[...] + (1.0 - ADAM_B2) * jnp.square(gg)
        m_hat = nm / (1.0 - ADAM_B1 ** ADAM_STEP)
        v_hat = nv / (1.0 - ADAM_B2 ** ADAM_STEP)
        d_ref[...] = -ADAM_LR * (m_hat / (jnp.sqrt(v_hat) + ADAM_EPS) + ADAM_WD * w_ref[...])
        nm_ref[...] = nm
        nv_ref[...] = nv

    spec = rspec(tr, cols)
    return hosted_call(kern, name=name, grid=(rows // tr,), in_specs=[spec] * 4, out_specs=[spec] * 3,
                       out_shape=[SDS((rows, cols), F32)] * 3, operands=[w, g, m, v])


def _row_tile(rows, cap=512, unit=16):
    return max(t for t in range(unit, cap + 1, unit) if rows % t == 0)


def _place():
    x, y, c = lax.axis_index("x"), lax.axis_index("y"), lax.axis_index("c")
    return x, y, c, [(1 - x, y), (x, 1 - y), (1 - x, 1 - y)]


def _row_chunks(rows, n, dtype):
    unit = 32 // jnp.dtype(dtype).itemsize
    base, extra = divmod(rows // unit, n)
    out, start = [], 0
    for k in range(n):
        size = (base + (k < extra)) * unit
        if size:
            out.append((start, size))
            start += size
    assert start == rows, (rows, unit)
    return out


PIECE_BYTES = 1 << 20


def _pieces(shapes_dtypes, rows_of):
    out = []
    for b, (shape, dtype) in enumerate(shapes_dtypes):
        rows = rows_of(shape)
        n = max(1, min(4, rows * shape[-1] * jnp.dtype(dtype).itemsize // PIECE_BYTES))
        out += [(b, st, sz) for st, sz in _row_chunks(rows, n, dtype)]
    return out


def all_gather_chips(name, shards):
    nb = len(shards)
    pieces = _pieces([(s.shape, s.dtype) for s in shards], lambda shape: shape[0] // 2)
    n = len(pieces)

    def body(*refs):
        x_refs, out_refs, send_sems, recv_sems = refs[:nb], refs[nb:2 * nb], refs[2 * nb], refs[2 * nb + 1]
        x, y, c, chips = _place()
        sibling = (x, y, 1 - c)
        mine = 2 * x + y

        def copy(sem, chip, cc, k, to, from_input=False):
            b, st, sz = pieces[k]
            rows_k = pl.ds(cc * (x_refs[b].shape[0] // 2) + st, sz)
            dst = out_refs[b].at[chip, rows_k, :]
            return pltpu.make_async_remote_copy(src_ref=x_refs[b].at[rows_k, :] if from_input else dst, dst_ref=dst,
                                                send_sem=send_sems.at[sem], recv_sem=recv_sems.at[sem], device_id=to, device_id_type=MESH_ID)

        order = [(k, j, 2 * cx + cy, (cx, cy, c)) for k in range(n) for j, (cx, cy) in enumerate(chips)]
        first = [copy(j * n + k, mine, c, k, to, from_input=True) for k, j, _, to in order]
        for cp in first:
            cp.start()
        passed = []
        for k, j, chip, _ in order:
            copy(j * n + k, chip, c, k, sibling).wait_recv()
            passed.append(copy((3 + j) * n + k, chip, c, k, sibling))
            passed[-1].start()
        for k, j, chip, _ in order:
            copy((3 + j) * n + k, chip, 1 - c, k, sibling).wait_recv()
        for cp in first + passed:
            cp.wait_send()

    return pl.pallas_call(body, in_specs=[ANY] * nb, out_specs=[ANY] * nb, out_shape=[SDS((4,) + s.shape, s.dtype) for s in shards],
                          scratch_shapes=_dma_sems(6 * n), name=name)(*shards)


def plan_gather_ici(shards):
    pieces = _pieces([(s.shape, s.dtype) for s in shards], lambda shape: shape[0] // 2)
    n = len(pieces)

    def copies(x_refs, out_refs, send_sems, recv_sems):
        x, y, c, chips = _place()
        mine = 2 * x + y

        def copy(j, k, chip, to, from_input):
            b, st, sz = pieces[k]
            rows_k = pl.ds(c * (x_refs[b].shape[0] // 2) + st, sz)
            dst = out_refs[b].at[chip, rows_k, :]
            return pltpu.make_async_remote_copy(src_ref=x_refs[b].at[rows_k, :] if from_input else dst, dst_ref=dst, send_sem=send_sems.at[j * n + k],
                                                recv_sem=recv_sems.at[j * n + k], device_id=to, device_id_type=MESH_ID)

        order = [(k, j, 2 * cx + cy, (cx, cy, c)) for k in range(n) for j, (cx, cy) in enumerate(chips)]
        return [copy(j, k, mine, to, True) for k, j, _, to in order], [copy(j, k, chip, to, False) for k, j, chip, to in order]

    return Plan(shards, [SDS((4,) + s.shape, s.dtype) for s in shards], {}, 3 * n, copies)


def plan_gather_pass(gathered):
    pieces = _pieces([(g.shape[1:], g.dtype) for g in gathered], lambda shape: shape[0] // 2)
    n = len(pieces)

    def copies(_, out_refs, send_sems, recv_sems):
        x, y, c, chips = _place()

        def copy(j, k, chip, cc):
            b, st, sz = pieces[k]
            rows_k = out_refs[b].at[chip, pl.ds(cc * (out_refs[b].shape[1] // 2) + st, sz), :]
            return pltpu.make_async_remote_copy(src_ref=rows_k, dst_ref=rows_k, send_sem=send_sems.at[j * n + k], recv_sem=recv_sems.at[j * n + k],
                                                device_id=(x, y, 1 - c), device_id_type=MESH_ID)

        order = [(k, j, 2 * cx + cy) for k in range(n) for j, (cx, cy) in enumerate(chips)]
        return [copy(j, k, chip, c) for k, j, chip in order], [copy(j, k, chip, 1 - c) for k, j, chip in order]

    return Plan(gathered, [SDS(g.shape, g.dtype) for g in gathered], {i: i for i in range(len(gathered))}, 3 * n, copies)


def plan_pair_exchange(gs):
    pieces = _pieces([(g.shape, g.dtype) for g in gs], lambda shape: shape[1] // 2)

    def copies(g_refs, got_refs, send_sems, recv_sems):
        x, y, c, _ = _place()
        swaps = [pltpu.make_async_remote_copy(src_ref=g_refs[b].at[:, pl.ds((1 - c) * (g_refs[b].shape[1] // 2) + st, sz), :],
                                              dst_ref=got_refs[b].at[:, pl.ds(st, sz), :], send_sem=send_sems.at[k], recv_sem=recv_sems.at[k],
                                              device_id=(x, y, 1 - c), device_id_type=MESH_ID)
                 for k, (b, st, sz) in enumerate(pieces)]
        return swaps, swaps

    return Plan(gs, [SDS((g.shape[0], g.shape[1] // 2, g.shape[2]), g.dtype) for g in gs], {}, len(pieces), copies)


def plan_chip_scatter(ps):
    pieces = _pieces([(p.shape, p.dtype) for p in ps], lambda shape: shape[1])
    n = len(pieces)

    def copies(p_refs, q_refs, send_sems, recv_sems):
        x, y, c, chips = _place()
        mine = 2 * x + y

        def copy(j, k, src_slot, dst_slot, to):
            b, st, sz = pieces[k]
            return pltpu.make_async_remote_copy(src_ref=p_refs[b].at[src_slot, pl.ds(st, sz), :], dst_ref=q_refs[b].at[dst_slot, pl.ds(st, sz), :],
                                                send_sem=send_sems.at[j * n + k], recv_sem=recv_sems.at[j * n + k], device_id=to,
                                                device_id_type=MESH_ID)

        order = [(k, j, 2 * cx + cy, (cx, cy, c)) for k in range(n) for j, (cx, cy) in enumerate(chips)]
        return [copy(j, k, chip, mine, to) for k, j, chip, to in order], [copy(j, k, mine, chip, to) for k, j, chip, to in order]

    return Plan(ps, [SDS(p.shape, p.dtype) for p in ps], {}, 3 * n, copies)


def plan_pair_join(bufs):
    pieces = _pieces([(b.shape, b.dtype) for b in bufs], lambda shape: shape[0] // 2)

    def copies(_, out_refs, send_sems, recv_sems):
        x, y, c, _ = _place()

        def copy(k, cc):
            b, st, sz = pieces[k]
            rows_k = out_refs[b].at[pl.ds(cc * (out_refs[b].shape[0] // 2) + st, sz), :]
            return pltpu.make_async_remote_copy(src_ref=rows_k, dst_ref=rows_k, send_sem=send_sems.at[k], recv_sem=recv_sems.at[k],
                                                device_id=(x, y, 1 - c), device_id_type=MESH_ID)

        return [copy(k, c) for k in range(len(pieces))], [copy(k, 1 - c) for k in range(len(pieces))]

    return Plan(bufs, [SDS(b.shape, b.dtype) for b in bufs], {i: i for i in range(len(bufs))}, len(pieces), copies)


def run_plan(name, plan):
    hosted_call(lambda: None, name=name, grid=(1,), in_specs=[], out_specs=[], out_shape=[], operands=[], plans=[plan])
    return plan.results


HBM = pl.BlockSpec(memory_space=pltpu.HBM)
SEMS = pl.BlockSpec(memory_space=pltpu.SEMAPHORE)
SPLIT_PARAMS = dict(has_side_effects=pltpu.SideEffectType.DATAFLOW_SIDE_EFFECTING)


def _plan_buffers(plan):
    in_place = {o: i for i, o in plan.aliases.items()}
    bufs = [pltpu.with_memory_space_constraint(a, pltpu.HBM) for a in plan.operands]
    where = []
    for o, sd in enumerate(plan.out_shape):
        if o in in_place:
            where.append(in_place[o])
        else:
            where.append(len(bufs))
            bufs.append(pltpu.with_memory_space_constraint(lax.empty(sd.shape, sd.dtype), pltpu.HBM))
    return bufs, where


def split_start(name, plans):
    layout = [_plan_buffers(p) for p in plans]
    counts = [len(b) for b, _ in layout]
    n_buf = sum(counts)

    def body(*refs):
        sems, token = refs[n_buf:n_buf + 2 * len(plans)], refs[-1]
        pos = 0
        for k, (p, (_, where)) in enumerate(zip(plans, layout)):
            mine = refs[pos:pos + counts[k]]
            pos += counts[k]
            sends, _ = p.copies(mine[:len(p.operands)], [mine[w] for w in where], sems[2 * k], sems[2 * k + 1])
            for cp in sends:
                cp.start()
        token[...] = jnp.zeros_like(token)

    bufs = [b for bs, _ in layout for b in bs]
    res = pl.pallas_call(
        body, name=name, in_specs=[HBM] * n_buf,
        out_specs=[SEMS] * (2 * len(plans)) + [HBM] * n_buf + [pl.BlockSpec(memory_space=pltpu.VMEM)],
        out_shape=[pltpu.SemaphoreType.DMA((p.n_sems,)) for p in plans for _ in range(2)] + [pltpu.HBM(b.shape, b.dtype) for b in bufs]
        + [SDS((8, 128), F32)],
        input_output_aliases={i: 2 * len(plans) + i for i in range(n_buf)}, compiler_params=pltpu.CompilerParams(**SPLIT_PARAMS))(*bufs)
    pos = 2 * len(plans)
    for k, p in enumerate(plans):
        p.in_flight = (res[2 * k], res[2 * k + 1], list(res[pos:pos + counts[k]]), layout[k][1])
        pos += counts[k]
    return res[-1]


def split_wait(name, plan, after):
    send_sems, recv_sems, bufs, where = plan.in_flight
    n_buf = len(bufs)

    def body(*refs):
        mine = refs[:n_buf]
        sends, recvs = plan.copies(mine[:len(plan.operands)], [mine[w] for w in where], refs[n_buf], refs[n_buf + 1])
        for cp in recvs:
            cp.wait_recv()
        for cp in sends:
            cp.wait_send()

    res = pl.pallas_call(body, name=name, in_specs=[HBM] * n_buf + [SEMS, SEMS, ANY], out_specs=[HBM] * n_buf,
                         out_shape=[pltpu.HBM(b.shape, b.dtype) for b in bufs], input_output_aliases={i: i for i in range(n_buf)},
                         compiler_params=pltpu.CompilerParams(**SPLIT_PARAMS))(*bufs, send_sems, recv_sems, after)
    plan.results = [res[w] for w in where]
    return plan.results


def pair_add(name, g, got, place):
    slots, rows, cols = g.shape
    half = rows // 2
    tr = _row_tile(half)
    nb = half // tr

    def kern(_, g_ref, t_ref, o_ref):
        o_ref[...] = (g_ref[...].astype(F32) + t_ref[...].astype(F32)).astype(o_ref.dtype)

    blk = pl.BlockSpec((None, tr, cols), lambda s, i, p: (s, i, 0))
    grid_spec = pltpu.PrefetchScalarGridSpec(
        num_scalar_prefetch=1, grid=(slots, nb),
        in_specs=[pl.BlockSpec((None, tr, cols), lambda s, i, p: (s, p[1] * nb + i, 0)), blk], out_specs=blk)
    return pl.pallas_call(kern, grid_spec=grid_spec, out_shape=SDS((slots, half, cols), g.dtype), name=name,
                          compiler_params=_cparams())(place, g, got)


def chip_add(name, p, q, place):
    slots, half, cols = p.shape
    tr = _row_tile(half)
    nb = half // tr

    def kern(_, p_ref, q1, q2, q3, o_ref):
        o_ref[...] = p_ref[...].astype(F32) + q1[...].astype(F32) + q2[...].astype(F32) + q3[...].astype(F32)

    def slot(k):
        return pl.BlockSpec((None, tr, cols), lambda i, pr: ((pr[0] + k) % slots, i, 0))

    grid_spec = pltpu.PrefetchScalarGridSpec(
        num_scalar_prefetch=1, grid=(nb,), in_specs=[slot(0), slot(1), slot(2), slot(3)],
        out_specs=pl.BlockSpec((tr, cols), lambda i, pr: (pr[1] * nb + i, 0)))
    return pl.pallas_call(kern, grid_spec=grid_spec, out_shape=SDS((2 * half, cols), F32), name=name,
                          compiler_params=_cparams())(place, p, q, q, q)


def pair_adds(tag, gs, gots, place):
    return [pair_add(f"{tag}_pair_add_{i}", g, got, place) for i, (g, got) in enumerate(zip(gs, gots))]


def chip_adds(tag, pairs, qs, place):
    return [chip_add(f"{tag}_chip_add_{i}", p, q, place) for i, (p, q) in enumerate(zip(pairs, qs))]


def reduce_scatter_chips(tag, gs, place):
    pairs = pair_adds(tag, gs, run_plan(tag + "_pair_exchange", plan_pair_exchange(gs)), place)
    return run_plan(tag + "_pair_join", plan_pair_join(chip_adds(tag, pairs, run_plan(tag + "_chip_scatter", plan_chip_scatter(pairs)), place)))


BIG = [("w_out", (2, 512, 1024)), ("w_mem_kv", (2, 256, 1024)), ("s5_w_in", (1, 1024, 1024)), ("s5_w_glu", (1, 1536, 768)),
       ("mla_w_in", (1, 1024, 848)), ("mla_w_uq", (1, 512, 576)), ("mla_w_ukv", (1, 256, 768))]
SHARDED_SMALL = [("mla_q_lora_norm", (1, 128)), ("mla_kv_lora_norm", (1, 64))]
SMALL = [("ln_gain", (2, 1024)), ("mem_norm", (2, 1024)), ("xq_norm", (2, 128)), ("xk_norm", (2, 128)),
         ("s5_lambda_re", (1, 96, 64)), ("s5_lambda_im", (1, 96, 64)), ("s5_log_step", (1, 96)),
         ("s5_b_re", (1, 96, 64, 16)), ("s5_b_im", (1, 96, 64, 16)), ("s5_c_re", (1, 96, 16, 64)), ("s5_c_im", (1, 96, 16, 64)),
         ("s5_d", (1, 1536)), ("mla_q_nope_norm", (1, 128)), ("mla_k_nope_norm", (1, 128)), ("mla_q_rope_norm", (1, 64)),
         ("mla_k_rope_norm", (1, 64))]
WEIGHT_ORDER = ["ln_gain", "w_out", "mem_norm", "w_mem_kv", "xq_norm", "xk_norm", "s5_w_in", "s5_lambda_re", "s5_lambda_im",
                "s5_log_step", "s5_b_re", "s5_b_im", "s5_c_re", "s5_c_im", "s5_d", "s5_w_glu", "mla_w_in", "mla_q_lora_norm",
                "mla_kv_lora_norm", "mla_w_uq", "mla_w_ukv", "mla_q_nope_norm", "mla_k_nope_norm", "mla_q_rope_norm", "mla_k_rope_norm"]
MINOR_LAST = {"mla_w_in": (0, 2, 1), "mla_w_uq": (0, 2, 1), "s5_b_re": (0, 2, 3, 1), "s5_b_im": (0, 2, 3, 1),
              "s5_c_re": (0, 2, 3, 1), "s5_c_im": (0, 2, 3, 1)}
SMALL_FULL = SMALL + [(n, (1, 4 * s[1])) for n, s in SHARDED_SMALL]
N_SMALL = sum(math.prod(s) for _, s in SMALL_FULL)
SMALL_ROWS, SMALL_LANES = 128, 1024

PAIR_OUT, PAIR_MKV, PAIR_ROWS = 0, 512, 768


def stack_shards(w, dtype):
    pairs = [jnp.concatenate([w["w_out"][l], w["w_mem_kv"][l]], axis=0).astype(dtype) for l in range(2)]
    return ([w["s5_w_in"][0].astype(dtype)], [pairs[0], w["s5_w_glu"][0].astype(dtype)],
            [pairs[1], w["mla_w_ukv"][0].astype(dtype), w["mla_w_in"][0].astype(dtype), w["mla_w_uq"][0].astype(dtype)])


def pair_views(pair):
    return {"w_out": Sharded(pair, "row", PAIR_OUT, 512), "w_mem_kv": Sharded(pair, "row", PAIR_MKV, 256)}


def grad_views():
    pair = SDS((4, PAIR_ROWS, 1024), BF16)
    return {"w_out": Sharded(pair, "row", PAIR_OUT, 512), "w_mem_kv": Sharded(pair, "row", PAIR_MKV, 256),
            "s5_w_in": Sharded(SDS((4, 1024, 1024), BF16), "col", 0, 1024), "s5_w_glu": Sharded(SDS((4, 1536, 768), BF16), "col", 0, 1536),
            "mla_w_ukv": Sharded(SDS((4, 256, 768), BF16), "col", 0, 256)}


def cols_to_shards(full):
    return full.reshape(full.shape[0], 4, full.shape[1] // 4).transpose(1, 0, 2)


def shards_to_cols(arr):
    return arr.transpose(1, 0, 2).reshape(arr.shape[1], 4 * arr.shape[2])


def mla_in_permute(w):
    o1, o2, o3, o4 = QL, QL + KVL, QL + KVL + ROPE, QL + KVL + ROPE + XQW
    return jnp.concatenate([w[:, o4:], w[:, :o1], w[:, o3:o4], w[:, o1:o2], w[:, o2:o3],
                            jnp.zeros((w.shape[0], MLA_IN_P - MLA_IN), w.dtype)], axis=1)


def mla_in_unpermute(d):
    return jnp.concatenate([d[:, 2048:2560], d[:, 3072:3328], d[:, 3328:3392], d[:, 2560:3072], d[:, :2048]], axis=1)


def uq_permute(w):
    w3 = w.reshape(w.shape[0], MH, NOPE + ROPE)
    return jnp.concatenate([w3[:, :, :NOPE].reshape(w.shape[0], MH * NOPE), w3[:, :, NOPE:].reshape(w.shape[0], MH * ROPE)], axis=1)


def uq_unpermute(d):
    dn = d[:, :MH * NOPE].reshape(d.shape[0], MH, NOPE)
    dr = d[:, MH * NOPE:].reshape(d.shape[0], MH, ROPE)
    return jnp.concatenate([dn, dr], axis=2).reshape(d.shape[0], MH * (NOPE + ROPE))


def time_permute(a):
    return a.reshape(SEG, SEG_LEN, a.shape[-1]).transpose(1, 0, 2).reshape(L, a.shape[-1])


def time_unpermute(a):
    return a.reshape(SEG_LEN, SEG, a.shape[-1]).transpose(1, 0, 2).reshape(L, a.shape[-1])


def mem_branch_fwd(tag, mem, mem_norm, w_mem_kv, xk_norm):
    mn = row_fwd(tag + "_mem_rms", fn_rms, ML, ML, [(mem, D, 0)], [mem_norm], [(D, BF16)])[0]
    kv = matmul(tag + "_mem_kv", mn, w_mem_kv, "nn", F32)
    kn = row_fwd(tag + "_mem_knorm", fn_mem_k, ML, ML, [(kv, XQW, 0)], [xk_norm], [(XQW, F32)])[0]
    return mn, kv, kn


def mem_branch_bwd(tag, mem, mem_norm, w_mem_kv, xk_norm, mn, kv, dkn, dv, g_view, g_wide):
    dk, dxk = row_bwd(tag + "_mem_knorm_bwd", fn_mem_k, ML, ML, [(kv, XQW, 0)], [xk_norm], [(dkn, XQW, 0)], [True], [True])
    dkv = jnp.concatenate([dk, dv], axis=1)
    dmn = matmul(tag + "_mem_kv_dx", dkv, w_mem_kv, "nt", F32)
    g_wide = matmul(tag + "_mem_kv_dw", mn, dkv, "tn", out=g_view, into=g_wide)
    dmem_norm = row_bwd(tag + "_mem_rms_bwd", fn_rms, ML, ML, [(mem, D, 0)], [mem_norm], [(dmn, D, 0)], [False], [True])[0]
    return g_wide, dmem_norm, dxk


def mem_attn_fwd(tag, proj, cb, kn, kv, xq_norm):
    return row_fwd(tag + "_mem_attn", fn_mem_attn, L, ROW_TILE, [(proj, XQW, cb)], [kn, kv[:, XQW:], xq_norm], [(XQW, F32)])[0]


def mem_attn_bwd(tag, proj, cb, kn, kv, xq_norm, dmo, dproj):
    place = {"cols": proj.shape[1], "cb": cb, "into": dproj, "dtype": dproj.dtype}
    return row_bwd(tag + "_mem_attn_bwd", fn_mem_attn, L, ROW_TILE, [(proj, XQW, cb)], [kn, kv[:, XQW:], xq_norm], [(dmo, XQW, 0)],
                   [place], [True, True, True])


def device_step(x, mem, positions, target, small, env, hooks=None):
    hooks = hooks or {}

    def plans_for(name):
        return hooks[("plans", name)](env) if ("plans", name) in hooks else ()

    def around(when, name, last=None):
        if (when, name) in hooks:
            hooks[(when, name)](env, last)

    g = {}
    gw = grad_views()
    ln, mem_norm, xq_norm, xk_norm = small["ln_gain"], small["mem_norm"], small["xq_norm"], small["xk_norm"]

    lre, lim = small["s5_lambda_re"][0], small["s5_lambda_im"][0]
    ls = small["s5_log_step"].reshape(SG, 1)
    one = pl.BlockSpec((SG, SP), lambda i: (0, 0))
    col = pl.BlockSpec((SG, 1), lambda i: (0, 0))
    disc_ins = [(lre, one), (lim, one), (ls, col)]
    a_re, a_im, coef_re, coef_im = stage("s5_disc", fn_s5_disc, (1,), disc_ins, [(SDS((SG, SP), F32), one)] * 4)
    b_re, b_im = small["s5_b_re"].reshape(SN, SC), small["s5_b_im"].reshape(SN, SC)
    c_re, c_im = small["s5_c_re"].reshape(PW, SP), small["s5_c_im"].reshape(PW, SP)
    bmat_rows = [(b_re, SC, 0), (b_im, SC, 0), (coef_re.reshape(SN, 1), 1, 0), (coef_im.reshape(SN, 1), 1, 0)]
    wb_re, wb_im = row_fwd("s5_bmat", fn_s5_bmat, SN, 512, bmat_rows, [], [(128, F32)] * 2)
    cmat_rows = [(c_re, SP, 0), (c_im, SP, 0)]
    wc_re, wc_im = row_fwd("s5_cmat", fn_s5_cmat, PW, 128, cmat_rows, [], [(512, F32)] * 2)
    a_re_v, a_im_v = a_re.reshape(1, SN), a_im.reshape(1, SN)
    s5_d = small["s5_d"]

    xp = time_permute(x)
    h0 = row_fwd("l0_rms", fn_rms, L, ROW_TILE, [(xp, D, 0)], [ln[0:1]], [(D, BF16)])[0]
    around("before", "l0_in", wb_re)
    w_in0 = Sharded(env["in0"], "col", 0, 1024)
    proj0 = matmul("l0_in", h0, w_in0, "nn")
    s_re, s_im, g0 = s5_forward(proj0, wb_re, wb_im, wc_re, wc_im, a_re_v, a_im_v, s5_d, plans=plans_for("s5_forward"))
    around("before", "l0_glu", g0)
    w0 = dict(pair_views(env["pair0"]), s5_w_glu=Sharded(env["glu"], "col", 0, 1536))
    z0 = matmul("l0_glu", g0, w0["s5_w_glu"], "nn", plans=plans_for("l0_glu"))
    mn0, kv0, kn0 = mem_branch_fwd("l0", mem, mem_norm[0:1], w0["w_mem_kv"], xk_norm[0:1])
    mo0 = mem_attn_fwd("l0", proj0, 3, kn0, kv0, xq_norm[0:1])
    o0 = row_fwd("l0_merge", fn_merge_glu, L, ROW_TILE, [(z0, 2 * PW, 0), (mo0, XQW, 0), (proj0, BW, 1)], [], [(BW, BF16)])[0]
    around("before", "l0_out", o0)
    x1p = matmul("l0_out", o0, w0["w_out"], "nn", F32, add=xp, plans=plans_for("l0_out"))
    around("after", "l0_out", x1p)
    x1 = time_unpermute(x1p)

    w1 = dict(pair_views(env["pair1"]), mla_w_ukv=Sharded(env["ukv"], "col", 0, 256))
    w_in1, w_uq = env["w_in1"], env["w_uq"]
    h1 = row_fwd("l1_rms", fn_rms, L, ROW_TILE, [(x1, D, 0)], [ln[1:2]], [(D, BF16)])[0]
    proj1 = matmul("l1_in", h1, w_in1, "nn")
    qln, kvln = env["q_lora_norm"].reshape(1, QL), env["kv_lora_norm"].reshape(1, KVL)
    cqn = row_fwd("l1_q_lora_rms", fn_rms, L, ROW_TILE, [(proj1, QL, 4)], [qln], [(QL, BF16)])[0]
    ckvn = row_fwd("l1_kv_lora_rms", fn_rms, L, ROW_TILE, [(proj1, KVL, 12)], [kvln], [(KVL, BF16)])[0]
    q = matmul("l1_uq", cqn, w_uq, "nn")
    kv = matmul("l1_ukv", ckvn, w1["mla_w_ukv"], "nn")
    inv_freq = ROPE_THETA ** (-jnp.arange(ROPE // 2, dtype=F32) / (ROPE // 2))
    ang = positions.astype(F32)[:, None] * inv_freq
    cos2 = jnp.tile(jnp.cos(ang), (1, 4))
    sin_signed = jnp.tile(jnp.concatenate([-jnp.sin(ang), jnp.sin(ang)], axis=1), (1, 2))
    qnn, knn = small["mla_q_nope_norm"], small["mla_k_nope_norm"]
    qrn, krn = jnp.tile(small["mla_q_rope_norm"], (1, 2)), jnp.tile(small["mla_k_rope_norm"], (1, 2))
    tp = 256
    prep_ins = [(q, rspec(tp, MH * (NOPE + ROPE))), (kv, rspec(tp, MH * 256)), (proj1, rspec(tp, 128, 26)),
                (cos2, rspec(tp, 128)), (sin_signed, rspec(tp, 128))] + [(a, cspec((1, 128))) for a in (qnn, knn, qrn, krn)]
    hq_spec = pl.BlockSpec((MH, tp, 256), lambda i: (0, i, 0))
    hv_spec = pl.BlockSpec((MH, tp, 128), lambda i: (0, i, 0))
    qf, kf, vh = stage("l1_mla_prep", fn_mla_prep, (L // tp,), prep_ins,
                       [(SDS((MH, L, 256), BF16), hq_spec), (SDS((MH, L, 256), BF16), hq_spec), (SDS((MH, L, 128), BF16), hv_spec)])
    attn, attn_lse = causal_attn(qf, kf, vh)
    mn1, kv1, kn1 = mem_branch_fwd("l1", mem, mem_norm[1:2], w1["w_mem_kv"], xk_norm[1:2])
    mo1 = mem_attn_fwd("l1", proj1, 5, kn1, kv1, xq_norm[1:2])
    o1 = row_fwd("l1_merge", fn_merge, L, ROW_TILE, [(attn, PW, 0), (mo1, XQW, 0), (proj1, BW, 0)], [], [(BW, BF16)])[0]
    x2 = matmul("l1_out", o1, w1["w_out"], "nn", F32, add=x1)
    dx2, loss = loss_and_grad(x2, target)
    around("after", "loss", loss)

    do1 = matmul("l1_out_dx", dx2, w1["w_out"], "nt")
    g_pair1 = matmul("l1_out_dw", o1, dx2, "tn", out=gw["w_out"])
    dattn, dmo1, dproj1 = row_bwd("l1_merge_bwd", fn_merge, L, ROW_TILE, [(attn, PW, 0), (mo1, XQW, 0), (proj1, BW, 0)], [],
                                  [(do1, BW, 0)], [True, True, {"cols": MLA_IN_P, "cb": 0, "dtype": BF16}], [])
    dproj1, dkn1, dv1, dxqn1 = mem_attn_bwd("l1", proj1, 5, kn1, kv1, xq_norm[1:2], dmo1, dproj1)
    env["g_pair1"], dmem_norm1, dxk1 = mem_branch_bwd("l1", mem, mem_norm[1:2], w1["w_mem_kv"], xk_norm[1:2], mn1, kv1, dkn1, dv1,
                                                      gw["w_mem_kv"], g_pair1)
    dqf, dkf, dvh = causal_attn_bwd(qf, kf, vh, attn, attn_lse, dattn)
    prep_diffs = [("row", SDS((L, MH * (NOPE + ROPE)), BF16), rspec(tp, MH * (NOPE + ROPE))), ("row", SDS((L, MH * 256), BF16), rspec(tp, MH * 256)),
                  ("row", SDS((L, MLA_IN_P), BF16), rspec(tp, 128, 26), {"into": dproj1}), None, None] + [("acc", (0,))] * 4
    dq, dkv, dproj1, dqnn, dknn, dqrn, dkrn = stage_bwd("l1_mla_prep_bwd", fn_mla_prep, (L // tp,), prep_ins,
                                                        [(dqf, hq_spec), (dkf, hq_spec), (dvh, hv_spec)], prep_diffs)
    dcqn = matmul("l1_uq_dx", dq, w_uq, "nt")
    env["dw_uq"] = matmul("l1_uq_dw", cqn, dq, "tn")
    dckvn = matmul("l1_ukv_dx", dkv, w1["mla_w_ukv"], "nt")
    env["g_ukv"] = matmul("l1_ukv_dw", ckvn, dkv, "tn", out=gw["mla_w_ukv"])
    dproj1, dqln = row_bwd("l1_q_lora_rms_bwd", fn_rms, L, ROW_TILE, [(proj1, QL, 4)], [qln], [(dcqn, QL, 0)],
                           [{"cols": MLA_IN_P, "cb": 4, "into": dproj1, "dtype": BF16}], [True])
    dproj1, dkvln = row_bwd("l1_kv_lora_rms_bwd", fn_rms, L, ROW_TILE, [(proj1, KVL, 12)], [kvln], [(dckvn, KVL, 0)],
                            [{"cols": MLA_IN_P, "cb": 12, "into": dproj1, "dtype": BF16}], [True])
    env["dw_in1"] = matmul("l1_in_dw", h1, dproj1, "tn")
    dh1 = matmul("l1_in_dx", dproj1, w_in1, "nt", plans=plans_for("l1_in_dx"))
    around("after", "l1_in_dx", dh1)
    dx1, dln1 = row_bwd("l1_rms_bwd", fn_rms, L, ROW_TILE, [(x1, D, 0)], [ln[1:2]], [(dh1, D, 0)], [{"add": (dx2, D, 0)}], [True])
    dx1p = time_permute(dx1)

    do0 = matmul("l0_out_dx", dx1p, w0["w_out"], "nt", plans=plans_for("l0_out_dx"))
    g_pair0 = matmul("l0_out_dw", o0, dx1p, "tn", out=gw["w_out"])
    dz0, dmo0, dproj0 = row_bwd("l0_merge_bwd", fn_merge_glu, L, ROW_TILE, [(z0, 2 * PW, 0), (mo0, XQW, 0), (proj0, BW, 1)], [],
                                [(do0, BW, 0)], [{"dtype": BF16}, True, {"cols": 2 * BW, "cb": 1, "dtype": BF16}], [])
    dproj0, dkn0, dv0, dxqn0 = mem_attn_bwd("l0", proj0, 3, kn0, kv0, xq_norm[0:1], dmo0, dproj0)
    env["g_pair0"], dmem_norm0, dxk0 = mem_branch_bwd("l0", mem, mem_norm[0:1], w0["w_mem_kv"], xk_norm[0:1], mn0, kv0, dkn0, dv0,
                                                      gw["w_mem_kv"], g_pair0)
    env["g_glu"] = matmul("l0_glu_dw", g0, dz0, "tn", out=gw["s5_w_glu"], plans=plans_for("l0_glu_dw"))
    dg0 = matmul("l0_glu_dx", dz0, w0["s5_w_glu"], "nt", plans=plans_for("l0_glu_dx"))
    around("before", "s5_backward", dg0)
    dproj0, dd, dwc_re, dwc_im, dwb_re, dwb_im, da_re, da_im = s5_backward(dg0, proj0, s_re, s_im, wb_re, wb_im, wc_re, wc_im,
                                                                           a_re_v, a_im_v, s5_d, dproj0, plans=plans_for("s5_backward"))
    around("after", "s5_backward", dd)
    env["g_in0"] = matmul("l0_in_dw", h0, dproj0, "tn", out=gw["s5_w_in"], plans=plans_for("l0_in_dw"))
    dh0 = matmul("l0_in_dx", dproj0, w_in0, "nt", plans=plans_for("l0_in_dx"))
    around("after", "l0_in_dx", dh0)
    dxp, dln0 = row_bwd("l0_rms_bwd", fn_rms, L, ROW_TILE, [(xp, D, 0)], [ln[0:1]], [(dh0, D, 0)], [{"add": (dx1p, D, 0)}], [True])
    grad_x = time_unpermute(dxp)

    db_re, db_im, dcoef_re, dcoef_im = row_bwd("s5_bmat_bwd", fn_s5_bmat, SN, 512, bmat_rows, [], [(dwb_re, 128, 0), (dwb_im, 128, 0)],
                                               [True] * 4, [], plans=plans_for("s5_bmat_bwd"))
    dc_re, dc_im = row_bwd("s5_cmat_bwd", fn_s5_cmat, PW, 128, cmat_rows, [], [(dwc_re, 512, 0), (dwc_im, 512, 0)], [True] * 2, [],
                           plans=plans_for("s5_cmat_bwd"))
    disc_cts = [(da_re.reshape(SG, SP), one), (da_im.reshape(SG, SP), one), (dcoef_re.reshape(SG, SP), one), (dcoef_im.reshape(SG, SP), one)]
    dlre, dlim, dls = stage_bwd("s5_disc_bwd", fn_s5_disc, (1,), disc_ins, disc_cts, [("acc", (0,))] * 3)

    g["ln_gain"] = jnp.concatenate([dln0, dln1], axis=0)
    g["mem_norm"] = jnp.concatenate([dmem_norm0, dmem_norm1], axis=0)
    g["xq_norm"] = jnp.concatenate([dxqn0, dxqn1], axis=0)
    g["xk_norm"] = jnp.concatenate([dxk0, dxk1], axis=0)
    g["s5_lambda_re"], g["s5_lambda_im"], g["s5_log_step"] = dlre, dlim, dls
    g["s5_b_re"], g["s5_b_im"], g["s5_c_re"], g["s5_c_im"] = db_re, db_im, dc_re, dc_im
    g["s5_d"] = dd
    g["mla_q_lora_norm"], g["mla_kv_lora_norm"] = dqln, dkvln
    g["mla_q_nope_norm"], g["mla_k_nope_norm"] = dqnn, dknn
    g["mla_q_rope_norm"] = dqrn[:, :ROPE] + dqrn[:, ROPE:]
    g["mla_k_rope_norm"] = dkrn[:, :ROPE] + dkrn[:, ROPE:]
    return loss, grad_x, g


def kernel(x, mem, positions, ln_gain, w_out, mem_norm, w_mem_kv, xq_norm, xk_norm, s5_w_in, s5_lambda_re, s5_lambda_im, s5_log_step, s5_b_re, s5_b_im, s5_c_re, s5_c_im, s5_d, s5_w_glu, mla_w_in, mla_q_lora_norm, mla_kv_lora_norm, mla_w_uq, mla_w_ukv, mla_q_nope_norm, mla_k_nope_norm, mla_q_rope_norm, mla_k_rope_norm, loss_target, m_ln_gain, m_w_out, m_mem_norm, m_w_mem_kv, m_xq_norm, m_xk_norm, m_s5_w_in, m_s5_lambda_re, m_s5_lambda_im, m_s5_log_step, m_s5_b_re, m_s5_b_im, m_s5_c_re, m_s5_c_im, m_s5_d, m_s5_w_glu, m_mla_w_in, m_mla_q_lora_norm, m_mla_kv_lora_norm, m_mla_w_uq, m_mla_w_ukv, m_mla_q_nope_norm, m_mla_k_nope_norm, m_mla_q_rope_norm, m_mla_k_rope_norm, v_ln_gain, v_w_out, v_mem_norm, v_w_mem_kv, v_xq_norm, v_xk_norm, v_s5_w_in, v_s5_lambda_re, v_s5_lambda_im, v_s5_log_step, v_s5_b_re, v_s5_b_im, v_s5_c_re, v_s5_c_im, v_s5_d, v_s5_w_glu, v_mla_w_in, v_mla_q_lora_norm, v_mla_kv_lora_norm, v_mla_w_uq, v_mla_w_ukv, v_mla_q_nope_norm, v_mla_k_nope_norm, v_mla_q_rope_norm, v_mla_k_rope_norm):
    args = dict(locals())
    wts = {n: args[n] for n in WEIGHT_ORDER}
    mom = {n: args["m_" + n] for n in WEIGHT_ORDER}
    var = {n: args["v_" + n] for n in WEIGHT_ORDER}

    chip = 2 * lax.axis_index("x") + lax.axis_index("y")
    place = jnp.stack([chip, lax.axis_index("c")]).astype(jnp.int32)

    def own_slot(gathered, shards):
        return [lax.dynamic_update_slice(g, s[None], (chip, 0, 0)) for g, s in zip(gathered, shards)]

    groups = list(stack_shards(wts, BF16))
    groups[2].append(jnp.concatenate([mla_q_lora_norm, jnp.pad(mla_kv_lora_norm, ((0, 0), (0, 64))), jnp.zeros((14, 128), F32)], axis=0))
    over_ici = [plan_gather_ici(shards) for shards in groups]
    _SCHEDULE_BEHIND.clear()
    schedule_behind(split_start("gather_start", over_ici))
    env, hooks, passed_on = {}, {}, {}

    def arrived(k, after, pass_now):
        passing = plan_gather_pass(split_wait(f"gather_wait_{k}", over_ici[k], after))
        passed_on[k] = passing
        return own_slot(run_plan(f"gather_pass_{k}", passing), groups[k]) if pass_now else None

    def need_in0(env, last):
        env["in0"], = arrived(0, last, True)

    def need_layer0(env, last):
        env["pair0"], env["glu"] = arrived(1, last, True)

    def need_layer1(env, last):
        arrived(2, last, False)

    def layer1_weights(env, last):
        pair1, ukv, in1, uq, norms = own_slot(passed_on[2].results, groups[2])
        env.update(pair1=pair1, ukv=ukv, w_in1=mla_in_permute(shards_to_cols(in1)), w_uq=uq_permute(shards_to_cols(uq)),
                   q_lora_norm=norms[:, 0, :], kv_lora_norm=norms[:, 1, :64])

    hooks["before", "l0_in"], hooks["before", "l0_glu"], hooks["before", "l0_out"] = need_in0, need_layer0, need_layer1
    hooks["plans", "l0_out"], hooks["after", "l0_out"] = (lambda env: [passed_on[2]]), layer1_weights

    rs = {}

    def swap(k, gs):
        rs[k, "g"], rs[k, "swap"] = gs, plan_pair_exchange(gs)
        return rs[k, "swap"]

    def start_scatter(k):
        rs[k, "pairs"] = pair_adds(f"rs{k}", rs[k, "g"], rs[k, "swap"].results, place)
        rs[k, "scatter"] = plan_chip_scatter(rs[k, "pairs"])
        schedule_behind(split_start(f"rs{k}_scatter_start", [rs[k, "scatter"]]))

    def join(k, after):
        rs[k, "join"] = plan_pair_join(chip_adds(f"rs{k}", rs[k, "pairs"], split_wait(f"rs{k}_scatter_wait", rs[k, "scatter"], after), place))
        return rs[k, "join"]

    hooks["plans", "l1_in_dx"] = lambda env: [swap(1, [env["g_pair1"], env["g_ukv"], cols_to_shards(mla_in_unpermute(env["dw_in1"])).astype(BF16),
                                                        cols_to_shards(uq_unpermute(env["dw_uq"])).astype(BF16)])]
    hooks["after", "l1_in_dx"] = lambda env, last: start_scatter(1)
    hooks["plans", "l0_glu_dx"] = lambda env: [swap(0, [env["g_pair0"], env["g_glu"]])]

    def before_s5_backward(env, last):
        start_scatter(0)
        env["join1"] = join(1, last)

    hooks["before", "s5_backward"] = before_s5_backward
    hooks["plans", "s5_backward"] = lambda env: [env["join1"]]
    hooks["after", "s5_backward"] = lambda env, last: env.update(join0=join(0, last))
    hooks["plans", "l0_in_dx"] = lambda env: [swap(2, [env["g_in0"]]), env["join0"]]
    hooks["after", "l0_in_dx"] = lambda env, last: start_scatter(2)

    def total_loss(env, local):
        env["loss"] = lax.psum(local[0, 0], MESH_AXES)
        schedule_behind(env["loss"].reshape(1, 1))

    hooks["after", "loss"] = total_loss
    small = {n: wts[n] for n, _ in SMALL}
    loss, grad_x, g = device_step(x[0], mem[0], positions[0], loss_target[0], small, env, hooks)
    loss = env["loss"]
    r_in0, = run_plan("rs2_pair_join", join(2, g["s5_log_step"]))
    (r_pair1, r_ukv, r_in1, r_uq), (r_pair0, r_glu) = (rs[k, "join"].results for k in (1, 0))

    small_flat = jnp.concatenate([g[n].reshape(-1) for n, _ in SMALL_FULL])
    g_small = jnp.pad(small_flat, (0, 4 * SMALL_ROWS * SMALL_LANES - N_SMALL)).astype(BF16).reshape(4, SMALL_ROWS, SMALL_LANES)
    r_small = reduce_scatter_chips("rs3", [g_small], place)[0]
    small_all = own_slot(all_gather_chips("gather_small_grads", [r_small]), [r_small])[0].reshape(-1)[:N_SMALL]

    grads = {"w_out": jnp.stack([r_pair0[:PAIR_MKV], r_pair1[:PAIR_MKV]]), "w_mem_kv": jnp.stack([r_pair0[PAIR_MKV:], r_pair1[PAIR_MKV:]]),
             "s5_w_in": r_in0[None], "s5_w_glu": r_glu[None], "mla_w_ukv": r_ukv[None], "mla_w_in": r_in1[None], "mla_w_uq": r_uq[None]}
    off = 0
    for n, s in SMALL_FULL:
        grads[n] = small_all[off:off + math.prod(s)].reshape(s)
        off += math.prod(s)
    for n, s in SHARDED_SMALL:
        grads[n] = lax.dynamic_slice(grads[n], (0, chip * s[1]), s)

    delta, new_m, new_v = {}, {}, {}
    for n, s in BIG + [(n, s) for n, s in SMALL if len(s) == 4]:
        perm = MINOR_LAST.get(n, tuple(range(len(s))))
        turned = tuple(s[p] for p in perm)
        view = lambda a: jnp.transpose(a, perm).reshape(-1, turned[-1])
        res = adamw("adamw_" + n, view(wts[n]), view(grads[n]), view(mom[n]), view(var[n]))
        delta[n], new_m[n], new_v[n] = (jnp.transpose(r.reshape(turned), tuple(perm.index(i) for i in range(len(s)))) for r in res)
    small_names = [n for n, s in SMALL if len(s) < 4] + [n for n, _ in SHARDED_SMALL]
    n_own = sum(wts[n].size for n in small_names)
    rows_own = -(-n_own // (8 * 128)) * 8

    def pack_small(d):
        flat = jnp.concatenate([d[n].reshape(-1) for n in small_names])
        return jnp.pad(flat, (0, rows_own * 128 - n_own), constant_values=1.0).reshape(rows_own, 128)

    res = adamw("adamw_small", pack_small(wts), pack_small(grads), pack_small(mom), pack_small(var))
    off = 0
    for n in small_names:
        size = wts[n].size
        delta[n], new_m[n], new_v[n] = (r.reshape(-1)[off:off + size].reshape(wts[n].shape) for r in res)
        off += size

    return (loss, grad_x[None], *[grads[n] for n in WEIGHT_ORDER], *[delta[n] for n in WEIGHT_ORDER],
            *[new_m[n] for n in WEIGHT_ORDER], *[new_v[n] for n in WEIGHT_ORDER])
```

```python
import functools
import math

import jax
import jax.numpy as jnp
from jax import lax
from jax.experimental import pallas as pl
from jax.experimental.pallas import tpu as pltpu

F32, BF16 = jnp.float32, jnp.bfloat16
SDS = jax.ShapeDtypeStruct

D = 1024
L = 2048
ML = 256
BW = 2 * D
XQW = BW // 4
PW = BW - XQW
XH, XHD = 4, 128
SG, SC, SP = 96, 16, 64
SN = SG * SP
NOPE, ROPE, VD = 128, 64, 128
MH = 12
QL, KVL = 512, 256
EPS = 1e-6
ROPE_THETA = 10000.0
MLA_IN = QL + KVL + ROPE + XQW + BW
MLA_IN_P = 3456
ADAM_LR, ADAM_B1, ADAM_B2, ADAM_EPS, ADAM_WD, ADAM_STEP = 0.001, 0.9, 0.999, 1e-08, 0.01, 10

VMEM_LIMIT = 48 * 2**20
ROW_TILE = 512
SEG = 8
SEG_LEN = L // SEG
MESH_AXES = ("x", "y", "c")


def _cparams():
    return pltpu.CompilerParams(vmem_limit_bytes=VMEM_LIMIT)


def _dg(a, b, ca, cb):
    return lax.dot_general(a.astype(BF16), b.astype(BF16), (((ca,), (cb,)), ((), ())), preferred_element_type=F32)


@jax.custom_vjp
def mm_nn(a, b):
    return _dg(a, b, 1, 0)


mm_nn.defvjp(lambda a, b: (_dg(a, b, 1, 0), (a, b)), lambda res, g: (_dg(g, res[1], 1, 1), _dg(res[0], g, 0, 0)))


@jax.custom_vjp
def mm_nt(a, b):
    return _dg(a, b, 1, 1)


mm_nt.defvjp(lambda a, b: (_dg(a, b, 1, 1), (a, b)), lambda res, g: (_dg(g, res[1], 1, 0), _dg(g, res[0], 0, 0)))


@functools.partial(jax.custom_vjp, nondiff_argnums=(1,))
def lane_roll(x, shift):
    return pltpu.roll(x, shift, 1)


lane_roll.defvjp(lambda x, shift: (pltpu.roll(x, shift, 1), None),
                 lambda shift, _, g: (pltpu.roll(g, (128 - shift) % 128, 1),))


def rms(x, g):
    return x * lax.rsqrt(jnp.mean(x * x, axis=-1, keepdims=True) + EPS) * g


@jax.custom_vjp
def softmax_rows(s):
    e = jnp.exp(s - jnp.max(s, axis=-1, keepdims=True))
    return e / jnp.sum(e, axis=-1, keepdims=True)


def _softmax_rows_fwd(s):
    p = softmax_rows(s)
    return p, p


def _softmax_rows_bwd(p, g):
    return (p * (g - jnp.sum(g * p, axis=-1, keepdims=True)),)


softmax_rows.defvjp(_softmax_rows_fwd, _softmax_rows_bwd)


def silu(x):
    return x * jax.nn.sigmoid(x)


ANY = pl.BlockSpec(memory_space=pl.ANY)
MESH_ID = pl.DeviceIdType.MESH


def _dma_sems(n):
    return [pltpu.SemaphoreType.DMA((n,)), pltpu.SemaphoreType.DMA((n,))]


class Plan:
    def __init__(self, operands, out_shape, aliases, n_sems, copies):
        self.operands, self.out_shape, self.aliases, self.n_sems, self.copies = list(operands), list(out_shape), aliases, n_sems, copies
        self.results = None


_SCHEDULE_BEHIND = []


def schedule_behind(token):
    _SCHEDULE_BEHIND.append(token)


def hosted_call(kern, *, name, grid, in_specs, out_specs, out_shape, operands, scratch_shapes=(), aliases=None, cparams=None, plans=(), deps=()):
    n_in, n_out, n_scr = len(in_specs), len(out_specs), len(scratch_shapes)
    p_in, p_out = [len(p.operands) for p in plans], [len(p.out_shape) for p in plans]
    deps = tuple(deps) + tuple(_SCHEDULE_BEHIND)
    _SCHEDULE_BEHIND.clear()
    all_aliases = dict(aliases or {})
    in_off, out_off = n_in, n_out
    for p, ni, no in zip(plans, p_in, p_out):
        all_aliases.update({in_off + i: out_off + o for i, o in p.aliases.items()})
        in_off, out_off = in_off + ni, out_off + no

    def body(*refs):
        pos, pins, pouts = n_in, [], []
        for ni in p_in:
            pins.append(refs[pos:pos + ni])
            pos += ni
        pos += len(deps)
        main_out = refs[pos:pos + n_out]
        pos += n_out
        for no in p_out:
            pouts.append(refs[pos:pos + no])
            pos += no
        main_scr = refs[pos:pos + n_scr]
        pos += n_scr
        if plans:
            ids = [pl.program_id(ax) for ax in range(len(grid))]
            first = functools.reduce(jnp.logical_and, [i == 0 for i in ids])
            last = functools.reduce(jnp.logical_and, [i == g - 1 for i, g in zip(ids, grid)])
            copies = [p.copies(pins[k], pouts[k], refs[pos + 2 * k], refs[pos + 2 * k + 1]) for k, p in enumerate(plans)]

            @pl.when(first)
            def _():
                for sends, _ in copies:
                    for cp in sends:
                        cp.start()

        kern(*refs[:n_in], *main_out, *main_scr)
        if plans:
            @pl.when(last)
            def _():
                for sends, recvs in copies:
                    for cp in recvs:
                        cp.wait_recv()
                    for cp in sends:
                        cp.wait_send()

    res = pl.pallas_call(body, grid=grid, in_specs=list(in_specs) + [ANY] * (sum(p_in) + len(deps)),
                         out_specs=list(out_specs) + [ANY] * sum(p_out), out_shape=list(out_shape) + [s for p in plans for s in p.out_shape],
                         scratch_shapes=list(scratch_shapes) + [s for p in plans for s in _dma_sems(p.n_sems)],
                         input_output_aliases=all_aliases, name=name, compiler_params=cparams or _cparams())(
        *operands, *[a for p in plans for a in p.operands], *deps)
    pos = n_out
    for p, no in zip(plans, p_out):
        p.results = list(res[pos:pos + no])
        pos += no
    return list(res[:n_out])


def _wide(v):
    return v.astype(F32) if v.dtype == BF16 else v


def stage(name, fn, grid, ins, outs):
    n_in = len(ins)

    def kern(*refs):
        res = fn(*[_wide(r[...]) for r in refs[:n_in]])
        for r, v in zip(refs[n_in:], res):
            r[...] = v.astype(r.dtype)

    return hosted_call(kern, name=name, grid=grid, in_specs=[s for _, s in ins], out_specs=[s for _, s in outs],
                       out_shape=[sd for sd, _ in outs], operands=[a for a, _ in ins])


def stage_bwd(name, fn, grid, ins, cts, diffs, plans=()):
    n_in, n_ct = len(ins), len(cts)
    didx = [i for i, d in enumerate(diffs) if d is not None]
    opts = {i: (diffs[i][3] if len(diffs[i]) > 3 else {}) for i in didx if diffs[i][0] == "row"}
    adds = [(i, opts[i]["add"]) for i in opts if "add" in opts[i]]
    intos = [(i, opts[i]["into"]) for i in opts if "into" in opts[i]]
    n_add, n_into = len(adds), len(intos)
    add_pos = {i: n_in + n_ct + k for k, (i, _) in enumerate(adds)}
    n_extra = n_in + n_ct + n_add + n_into

    def kern(*refs):
        vals = [_wide(r[...]) for r in refs[:n_in]]

        def f(*dv):
            full = list(vals)
            for i, v in zip(didx, dv):
                full[i] = v
            return fn(*full)

        _, vjp = jax.vjp(f, *[vals[i].astype(F32) for i in didx])
        gs = vjp(tuple(c[...].astype(F32) for c in refs[n_in:n_in + n_ct]))
        for o_ref, i, g in zip(refs[n_extra:], didx, gs):
            if diffs[i][0] == "row":
                if i in add_pos:
                    g = g + refs[add_pos[i]][...].astype(F32)
                o_ref[...] = g.astype(o_ref.dtype)
            else:
                first = functools.reduce(jnp.logical_and, [pl.program_id(ax) == 0 for ax in diffs[i][1]])

                @pl.when(first)
                def _():
                    o_ref[...] = g

                @pl.when(jnp.logical_not(first))
                def _():
                    o_ref[...] += g

    out_shape, out_specs = [], []
    for i in didx:
        if diffs[i][0] == "row":
            out_shape.append(diffs[i][1])
            out_specs.append(diffs[i][2])
        else:
            out_shape.append(SDS(ins[i][0].shape, F32))
            out_specs.append(ins[i][1])
    aliases = {n_in + n_ct + n_add + k: didx.index(i) for k, (i, _) in enumerate(intos)}
    in_specs = [s for _, s in ins] + [s for _, s in cts] + [s for _, (_, s) in adds] + [ANY] * n_into
    operands = [a for a, _ in ins] + [a for a, _ in cts] + [a for _, (a, _) in adds] + [a for _, a in intos]
    return hosted_call(kern, name=name, grid=grid, in_specs=in_specs, out_specs=out_specs, out_shape=out_shape, operands=operands,
                       aliases=aliases, plans=plans)


def rspec(tl, w, cb=0):
    return pl.BlockSpec((tl, w), lambda i: (i, cb))


def cspec(shape):
    return pl.BlockSpec(shape, lambda i: (0,) * len(shape))


def row_fwd(name, fn, rows, tl, row_ins, consts, outs):
    ins = [(a, rspec(tl, w, cb)) for a, w, cb in row_ins] + [(a, cspec(a.shape)) for a in consts]
    return stage(name, fn, (rows // tl,), ins, [(SDS((rows, w), dt), rspec(tl, w)) for w, dt in outs])


def row_bwd(name, fn, rows, tl, row_ins, consts, cts, row_diff, const_diff, plans=()):
    ins = [(a, rspec(tl, w, cb)) for a, w, cb in row_ins] + [(a, cspec(a.shape)) for a in consts]
    diffs = []
    for (a, w, cb), d in zip(row_ins, row_diff):
        if not d:
            diffs.append(None)
            continue
        d = d if isinstance(d, dict) else {}
        opts = {}
        if "add" in d:
            opts["add"] = (d["add"][0], rspec(tl, d["add"][1], d["add"][2]))
        if d.get("into") is not None:
            opts["into"] = d["into"]
        diffs.append(("row", SDS((rows, d.get("cols", w)), d.get("dtype", F32)), rspec(tl, w, d.get("cb", 0)), opts))
    diffs += [("acc", (0,)) if d else None for d in const_diff]
    return stage_bwd(name, fn, (rows // tl,), ins, [(a, rspec(tl, w, cb)) for a, w, cb in cts], diffs, plans=plans)


MATMUL_VMEM = 36 * 2**20
ADAMW_VMEM = 28 * 2**20


class Sharded:
    def __init__(self, arr, kind, roff, rows):
        self.arr, self.kind, self.roff, self.rows, self.n = arr, kind, roff, rows, arr.shape[2]
        self.shape = (rows, 4 * self.n) if kind == "col" else (4 * rows, self.n)

    def fits(self, t0, t1):
        return self.roff % t0 == 0 and self.rows % t0 == 0 and self.n % t1 == 0

    def spec(self, t0, t1, bidx):
        assert self.fits(t0, t1), (self.kind, self.roff, self.rows, self.n, t0, t1)
        r0 = self.roff // t0
        if self.kind == "col":
            per = self.n // t1
            return pl.BlockSpec((None, t0, t1), lambda *g: (bidx(*g)[1] // per, r0 + bidx(*g)[0], bidx(*g)[1] % per))
        per = self.rows // t0
        return pl.BlockSpec((None, t0, t1), lambda *g: (bidx(*g)[0] // per, r0 + bidx(*g)[0] % per, bidx(*g)[1]))


def matmul(name, a, b, mode, out_dtype=BF16, add=None, out=None, into=None, plans=()):
    if mode == "tn":
        k_dim, m = a.shape
    else:
        m, k_dim = a.shape
    n = b.shape[0] if mode == "nt" else b.shape[1]
    b_fit = b.fits if isinstance(b, Sharded) else (lambda t0, t1: True)
    o_fit = out.fits if out is not None else (lambda t0, t1: True)
    a_bytes, b_bytes = jnp.dtype(a.dtype).itemsize, jnp.dtype(b.arr.dtype if isinstance(b, Sharded) else b.dtype).itemsize
    o_bytes = jnp.dtype(out_dtype if out is None else out.arr.dtype).itemsize

    def vmem(tm, tn, tk):
        return 2 * (tm * tk * a_bytes + tk * tn * b_bytes + tm * tn * (o_bytes + (4 if add is not None else 0))) + 4 * tm * tn * (1 + (tk < k_dim))

    tiles = [(tm, tn, tk) for tm in (2048, 1024, 512, 256, 128) for tn in (1024, 768, 512, 384, 256, 128)
             for tk in sorted({k_dim, 1024, 768, 512, 384, 256, 128})
             if m % tm == 0 and n % tn == 0 and k_dim % tk == 0 and (b_fit(tn, tk) if mode == "nt" else b_fit(tk, tn)) and o_fit(tm, tn)
             and vmem(tm, tn, tk) <= MATMUL_VMEM]
    tm, tn, tk = max(tiles, key=lambda t: (t[2] == k_dim, t[0] * t[1] * t[2], t[0] * t[1]))
    nk = k_dim // tk
    a_spec = pl.BlockSpec((tk, tm), lambda i, j, k: (k, i)) if mode == "tn" else pl.BlockSpec((tm, tk), lambda i, j, k: (i, k))
    if isinstance(b, Sharded):
        b_spec = b.spec(tn, tk, lambda i, j, k: (j, k)) if mode == "nt" else b.spec(tk, tn, lambda i, j, k: (k, j))
        b = b.arr
    else:
        b_spec = pl.BlockSpec((tn, tk), lambda i, j, k: (j, k)) if mode == "nt" else pl.BlockSpec((tk, tn), lambda i, j, k: (k, j))
    o_spec = pl.BlockSpec((tm, tn), lambda i, j, k: (i, j))
    out_spec, out_shape = (o_spec, SDS((m, n), out_dtype)) if out is None else (out.spec(tm, tn, lambda i, j, k: (i, j)), out.arr)
    ca, cb = {"nn": (1, 0), "nt": (1, 1), "tn": (0, 0)}[mode]
    n_in = 2 + (add is not None)

    def finish(refs, o_ref, r):
        if add is not None:
            r = r + refs[2][...]
        o_ref[...] = r.astype(o_ref.dtype)

    def kern_whole(*refs):
        finish(refs, refs[-1], _dg(refs[0][...], refs[1][...], ca, cb))

    def kern_cut(*refs):
        o_ref, acc = refs[-2], refs[-1]
        k = pl.program_id(2)

        @pl.when(k == 0)
        def _():
            acc[...] = jnp.zeros_like(acc)

        acc[...] += _dg(refs[0][...], refs[1][...], ca, cb)

        @pl.when(k == nk - 1)
        def _():
            finish(refs, o_ref, acc[...])

    ins, specs = [a, b], [a_spec, b_spec]
    if add is not None:
        ins.append(add)
        specs.append(o_spec)
    if into is not None:
        ins.append(into)
        specs.append(ANY)
    return hosted_call(kern_whole if nk == 1 else kern_cut, name=name, grid=(m // tm, n // tn, nk), in_specs=specs, out_specs=[out_spec],
                       out_shape=[out_shape], operands=ins, scratch_shapes=[] if nk == 1 else [pltpu.VMEM((tm, tn), F32)],
                       aliases={} if into is None else {n_in: 0}, plans=plans)[0]


SCAN_UNROLL = 8


def _cmul(ar, ai, br, bi):
    return ar * br - ai * bi, ar * bi + ai * br


def _sub_shift(x, down):
    row = lax.broadcasted_iota(jnp.int32, x.shape, 0)
    if down:
        return jnp.where(row == 0, 0.0, pltpu.roll(x, 1, 0))
    return jnp.where(row == SEG - 1, 0.0, pltpu.roll(x, SEG - 1, 0))


def _pow_seg_len(ar, ai):
    for _ in range(int(math.log2(SEG_LEN))):
        ar, ai = _cmul(ar, ai, ar, ai)
    return ar, ai


def _scan_in_place(sr, si, a_re, a_im):
    lanes = sr.shape[1]
    ar = jnp.broadcast_to(a_re, (SEG, lanes))
    ai = jnp.broadcast_to(a_im, (SEG, lanes))
    zero = jnp.zeros((SEG, lanes), F32)

    def local(i, carry):
        rows = pl.ds(pl.multiple_of(i * SEG, SEG), SEG)
        mr, mi = _cmul(ar, ai, carry[0], carry[1])
        nr, ni = mr + sr[rows, :], mi + si[rows, :]
        sr[rows, :] = nr
        si[rows, :] = ni
        return nr, ni

    fr, fi = lax.fori_loop(0, SEG_LEN, local, (zero, zero), unroll=SCAN_UNROLL)
    pr, pi = _pow_seg_len(ar, ai)
    ir, ii = zero, zero
    for _ in range(SEG - 1):
        mr, mi = _cmul(pr, pi, ir, ii)
        ir, ii = _sub_shift(mr + fr, True), _sub_shift(mi + fi, True)

    def carry_in(i, pw):
        rows = pl.ds(pl.multiple_of(i * SEG, SEG), SEG)
        cr, ci = _cmul(pw[0], pw[1], ir, ii)
        sr[rows, :] += cr
        si[rows, :] += ci
        return _cmul(pw[0], pw[1], ar, ai)

    lax.fori_loop(0, SEG_LEN, carry_in, (ar, ai), unroll=SCAN_UNROLL)


S5_LANES = 8 * SP
S5_BLOCKS = SN // S5_LANES


def _s5_specs():
    u_spec = pl.BlockSpec((L, 8 * SC), lambda j: (0, j))
    s_spec = pl.BlockSpec((L, S5_LANES), lambda j: (0, j))
    wb_spec = pl.BlockSpec((S5_LANES, 8 * SC), lambda j: (j, 0))
    wc_spec = pl.BlockSpec((8 * SC, S5_LANES), lambda j: (j, 0))
    a_spec = pl.BlockSpec((1, S5_LANES), lambda j: (0, j))
    d_spec = pl.BlockSpec((1, 8 * SC), lambda j: (0, j))
    return u_spec, s_spec, wb_spec, wc_spec, a_spec, d_spec


def s5_forward(proj, wb_re, wb_im, wc_re, wc_im, a_re, a_im, d, plans=()):
    def kern(u_ref, wbr, wbi, wcr, wci, ar, ai, d_ref, sr_out, si_out, g_ref, sr, si):
        u = _wide(u_ref[...])
        sr[...], si[...] = fn_s5_bu(u, wbr[...], wbi[...])
        _scan_in_place(sr, si, ar[...], ai[...])
        g_ref[...] = fn_s5_out(sr[...], si[...], u, d_ref[...], wcr[...], wci[...])[0].astype(g_ref.dtype)
        sr_out[...] = sr[...].astype(sr_out.dtype)
        si_out[...] = si[...].astype(si_out.dtype)

    u_spec, s_spec, wb_spec, wc_spec, a_spec, d_spec = _s5_specs()
    return hosted_call(kern, name="s5_forward", grid=(S5_BLOCKS,), in_specs=[u_spec, wb_spec, wb_spec, wc_spec, wc_spec, a_spec, a_spec, d_spec],
                       out_specs=[s_spec, s_spec, u_spec], out_shape=[SDS((L, SN), BF16)] * 2 + [SDS((L, PW), BF16)],
                       operands=[proj, wb_re, wb_im, wc_re, wc_im, a_re, a_im, d], scratch_shapes=[pltpu.VMEM((L, S5_LANES), F32)] * 2,
                       plans=plans)


def _adjoint_scan_in_place(lr, li, sr, si, a_re, a_im):
    lanes = lr.shape[1]
    ar = jnp.broadcast_to(a_re, (SEG, lanes))
    ai = -jnp.broadcast_to(a_im, (SEG, lanes))
    zero = jnp.zeros((SEG, lanes), F32)

    def local(k, carry):
        i = SEG_LEN - 1 - k
        rows = pl.ds(pl.multiple_of(i * SEG, SEG), SEG)
        mr, mi = _cmul(ar, ai, carry[0], carry[1])
        nr, ni = mr + lr[rows, :], mi + li[rows, :]
        lr[rows, :] = nr
        li[rows, :] = ni
        return nr, ni

    fr, fi = lax.fori_loop(0, SEG_LEN, local, (zero, zero), unroll=SCAN_UNROLL)
    pr, pi = _pow_seg_len(ar, ai)
    ir, ii = zero, zero
    for _ in range(SEG - 1):
        mr, mi = _cmul(pr, pi, ir, ii)
        ir, ii = _sub_shift(mr + fr, False), _sub_shift(mi + fi, False)

    def fix(rows, pw):
        cr, ci = _cmul(pw[0], pw[1], ir, ii)
        tr, ti = lr[rows, :] + cr, li[rows, :] + ci
        lr[rows, :] = tr
        li[rows, :] = ti
        return tr, ti

    def grad_a(tr, ti, spr, spi, acc):
        return acc[0] + tr * spr + ti * spi, acc[1] + ti * spr - tr * spi

    def carry_in(k, c):
        i = SEG_LEN - 1 - k
        rows = pl.ds(pl.multiple_of(i * SEG, SEG), SEG)
        prev = pl.ds(pl.multiple_of((i - 1) * SEG, SEG), SEG)
        tr, ti = fix(rows, (c[0], c[1]))
        acc = grad_a(tr, ti, sr[prev, :], si[prev, :], (c[2], c[3]))
        nr, ni = _cmul(c[0], c[1], ar, ai)
        return nr, ni, acc[0], acc[1]

    pwr, pwi, accr, acci = lax.fori_loop(0, SEG_LEN - 1, carry_in, (ar, ai, zero, zero), unroll=5)
    tr, ti = fix(pl.ds(0, SEG), (pwr, pwi))
    last = pl.ds((SEG_LEN - 1) * SEG, SEG)
    accr, acci = grad_a(tr, ti, _sub_shift(sr[last, :], True), _sub_shift(si[last, :], True), (accr, acci))
    return jnp.sum(accr, axis=0, keepdims=True), jnp.sum(acci, axis=0, keepdims=True)


S5_BWD_VMEM = 58 * 2**20


def s5_backward(dg, proj, s_re, s_im, wb_re, wb_im, wc_re, wc_im, a_re, a_im, d, dproj, plans=()):
    def kern(dg_ref, u_ref, sr_in, si_in, wbr, wbi, wcr, wci, ar, ai, d_ref, _, du_ref, dd_ref, dwcr, dwci, dwbr, dwbi, dar, dai, lr, li, sr, si):
        u = _wide(u_ref[...])
        sr[...], si[...] = _wide(sr_in[...]), _wide(si_in[...])
        _, vjp_out = jax.vjp(fn_s5_out, sr[...], si[...], u, d_ref[...], wcr[...], wci[...])
        lr[...], li[...], du_out, dd_ref[...], dwcr[...], dwci[...] = vjp_out((_wide(dg_ref[...]),))
        dar[...], dai[...] = _adjoint_scan_in_place(lr, li, sr, si, ar[...], ai[...])
        _, vjp_in = jax.vjp(fn_s5_bu, u, wbr[...], wbi[...])
        du_in, dwbr[...], dwbi[...] = vjp_in((lr[...], li[...]))
        du_ref[...] = (du_out + du_in).astype(du_ref.dtype)

    u_spec, s_spec, wb_spec, wc_spec, a_spec, d_spec = _s5_specs()
    outs = [(SDS(dproj.shape, dproj.dtype), u_spec), (SDS(d.shape, F32), d_spec), (SDS(wc_re.shape, F32), wc_spec), (SDS(wc_im.shape, F32), wc_spec),
            (SDS(wb_re.shape, F32), wb_spec), (SDS(wb_im.shape, F32), wb_spec), (SDS(a_re.shape, F32), a_spec), (SDS(a_im.shape, F32), a_spec)]
    return hosted_call(kern, name="s5_backward", grid=(S5_BLOCKS,),
                       in_specs=[u_spec, u_spec, s_spec, s_spec, wb_spec, wb_spec, wc_spec, wc_spec, a_spec, a_spec, d_spec, ANY],
                       out_specs=[sp for _, sp in outs], out_shape=[sd for sd, _ in outs], aliases={11: 0},
                       operands=[dg, proj, s_re, s_im, wb_re, wb_im, wc_re, wc_im, a_re, a_im, d, dproj],
                       scratch_shapes=[pltpu.VMEM((L, S5_LANES), F32)] * 4,
                       cparams=pltpu.CompilerParams(vmem_limit_bytes=S5_BWD_VMEM), plans=plans)


def fn_rms(x, g):
    return (rms(x, g),)


def fn_s5_disc(lre, lim, ls):
    step = jnp.exp(ls)
    e = jnp.exp(lre * step)
    a_re, a_im = e * jnp.cos(lim * step), e * jnp.sin(lim * step)
    den = lre * lre + lim * lim
    nr, ni = a_re - 1.0, a_im
    return a_re, a_im, (nr * lre + ni * lim) / den, (ni * lre - nr * lim) / den


def _group_mask(rows, cols, row_div, col_div):
    r = lax.broadcasted_iota(jnp.int32, (rows, cols), 0) // row_div % 8
    c = lax.broadcasted_iota(jnp.int32, (rows, cols), 1) // col_div
    return r == c


def _spread(x, mask):
    w = x.shape[1]
    copy = (lax.broadcasted_iota(jnp.int32, (w, 8 * w), 1) % w == lax.broadcasted_iota(jnp.int32, (w, 8 * w), 0)).astype(F32)
    return jnp.where(mask, jnp.dot(x, copy, precision=lax.Precision.HIGHEST, preferred_element_type=F32), 0.0)


def fn_s5_bmat(b_re, b_im, coef_re, coef_im):
    mask = _group_mask(b_re.shape[0], 8 * SC, SP, SC)
    return _spread(coef_re * b_re - coef_im * b_im, mask), _spread(coef_re * b_im + coef_im * b_re, mask)


def fn_s5_cmat(c_re, c_im):
    mask = _group_mask(c_re.shape[0], 8 * SP, SC, SP)
    return _spread(c_re, mask), _spread(c_im, mask)


def fn_s5_bu(u, wb_re, wb_im):
    return mm_nt(u, wb_re), mm_nt(u, wb_im)


def fn_s5_out(sr, si, u, d, wc_re, wc_im):
    y = mm_nt(sr, wc_re) - mm_nt(si, wc_im) + d * u
    return (jax.nn.gelu(y),)


def fn_merge_glu(z, mo, gate):
    yg = z[:, :PW] * jax.nn.sigmoid(z[:, PW:])
    return (jnp.concatenate([yg, mo], axis=1) * silu(gate),)


def fn_merge(prim, mo, gate):
    return (jnp.concatenate([prim, mo], axis=1) * silu(gate),)


def fn_mem_k(kv, g):
    return (jnp.concatenate([rms(kv[:, h * XHD:(h + 1) * XHD], g) for h in range(XH)], axis=1),)


def fn_mem_attn(xq, kn, v, g):
    outs = []
    for h in range(XH):
        sl = slice(h * XHD, (h + 1) * XHD)
        p = softmax_rows(mm_nt(rms(xq[:, sl], g), kn[:, sl]) * (XHD ** -0.5))
        outs.append(mm_nn(p, v[:, sl]))
    return (jnp.concatenate(outs, axis=1),)


def _half_rms(x, g):
    lo = lax.broadcasted_iota(jnp.int32, x.shape, 1) < ROPE
    x2 = x * x
    s_lo = jnp.sum(jnp.where(lo, x2, 0.0), axis=1, keepdims=True)
    s_hi = jnp.sum(jnp.where(lo, 0.0, x2), axis=1, keepdims=True)
    return x * lax.rsqrt(jnp.where(lo, s_lo, s_hi) / ROPE + EPS) * g


def _rope(x, cos2, sin_signed):
    first = lax.broadcasted_iota(jnp.int32, x.shape, 1) % ROPE < ROPE // 2
    return x * cos2 + jnp.where(first, lane_roll(x, 128 - ROPE // 2), lane_roll(x, ROPE // 2)) * sin_signed


def fn_mla_prep(q, kv, kr, cos2, sin_signed, qnn, knn, qrn, krn):
    lo = lax.broadcasted_iota(jnp.int32, kr.shape, 1) < ROPE
    kr_pad = jnp.where(lo, _rope(_half_rms(kr, krn), cos2, sin_signed), 0.0)
    qf, kf, vs = [], [], []
    for m in range(MH // 2):
        pair = _rope(_half_rms(q[:, MH * NOPE + 128 * m:MH * NOPE + 128 * (m + 1)], qrn), cos2, sin_signed)
        for h, rope_h in ((2 * m, pair), (2 * m + 1, lane_roll(pair, ROPE))):
            qf.append(jnp.concatenate([rms(q[:, NOPE * h:NOPE * (h + 1)], qnn), jnp.where(lo, rope_h, 0.0)], axis=1))
    for h in range(MH):
        kf.append(jnp.concatenate([rms(kv[:, 256 * h:256 * h + NOPE], knn), kr_pad], axis=1))
        vs.append(kv[:, 256 * h + NOPE:256 * (h + 1)])
    return jnp.stack(qf), jnp.stack(kf), jnp.stack(vs)


ATT_TQ = 512


def _attn_scores(q, kf):
    tq = q.shape[0]
    scale = (NOPE + ROPE) ** -0.5
    own = _dg(q, kf[-tq:], 1, 1) * scale
    own = jnp.where(lax.broadcasted_iota(jnp.int32, own.shape, 1) <= lax.broadcasted_iota(jnp.int32, own.shape, 0), own, jnp.finfo(F32).min)
    return own if kf.shape[0] == tq else jnp.concatenate([_dg(q, kf[:-tq], 1, 1) * scale, own], axis=1)


ATT_HEADS = 2


def _attn_specs():
    q_spec = pl.BlockSpec((ATT_HEADS, ATT_TQ, 256), lambda h, i: (h, i, 0))
    k_spec = pl.BlockSpec((ATT_HEADS, L, 256), lambda h, i: (h, 0, 0))
    v_spec = pl.BlockSpec((ATT_HEADS, L, 128), lambda h, i: (h, 0, 0))
    o_spec = pl.BlockSpec((ATT_TQ, ATT_HEADS * 128), lambda h, i: (i, h))
    lse_spec = pl.BlockSpec((ATT_HEADS, ATT_TQ, 1), lambda h, i: (h, i, 0))
    return q_spec, k_spec, v_spec, o_spec, lse_spec


def _attn_branches(body):
    i = pl.program_id(1)
    for t in range(L // ATT_TQ):
        @pl.when(i == t)
        def _(t=t):
            for hh in range(ATT_HEADS):
                body(hh, slice(hh * 128, (hh + 1) * 128), (t + 1) * ATT_TQ)


def causal_attn(qf, kf, vh):
    def kern(q_ref, k_ref, v_ref, o_ref, lse_ref):
        def body(hh, lanes, keys):
            s = _attn_scores(q_ref[hh], k_ref[hh, :keys, :])
            m = jnp.max(s, axis=-1, keepdims=True)
            e = jnp.exp(s - m)
            total = jnp.sum(e, axis=-1, keepdims=True)
            o_ref[:, lanes] = (_dg(e, v_ref[hh, :keys, :], 1, 0) / total).astype(o_ref.dtype)
            lse_ref[hh] = m + jnp.log(total)

        _attn_branches(body)

    q_spec, k_spec, v_spec, o_spec, lse_spec = _attn_specs()
    return pl.pallas_call(kern, grid=(MH // ATT_HEADS, L // ATT_TQ), in_specs=[q_spec, k_spec, v_spec], out_specs=[o_spec, lse_spec],
                          out_shape=[SDS((L, MH * VD), F32), SDS((MH, L, 1), F32)], name="l1_attn", compiler_params=_cparams())(qf, kf, vh)


def causal_attn_bwd(qf, kf, vh, out, lse, dout):
    scale = (NOPE + ROPE) ** -0.5

    def kern(q_ref, k_ref, v_ref, o_ref, lse_ref, do_ref, dq_ref, dk_ref, dv_ref):
        @pl.when(pl.program_id(1) == 0)
        def _():
            dk_ref[...] = jnp.zeros_like(dk_ref)
            dv_ref[...] = jnp.zeros_like(dv_ref)

        def body(hh, lanes, keys):
            q, k, v, do = q_ref[hh], k_ref[hh, :keys, :], v_ref[hh, :keys, :], do_ref[:, lanes]
            p = jnp.exp(_attn_scores(q, k) - lse_ref[hh])
            delta = jnp.sum(do * _wide(o_ref[:, lanes]), axis=-1, keepdims=True)
            dv_ref[hh, :keys, :] += _dg(p, do, 0, 0)
            ds = p * (_dg(do, v, 1, 1) - delta) * scale
            dq_ref[hh] = _dg(ds, k, 1, 0)
            dk_ref[hh, :keys, :] += _dg(ds, q, 0, 0)

        _attn_branches(body)

    q_spec, k_spec, v_spec, o_spec, lse_spec = _attn_specs()
    return pl.pallas_call(kern, grid=(MH // ATT_HEADS, L // ATT_TQ), in_specs=[q_spec, k_spec, v_spec, o_spec, lse_spec, o_spec],
                          out_specs=[q_spec, k_spec, v_spec], out_shape=[SDS(qf.shape, F32), SDS(kf.shape, F32), SDS(vh.shape, F32)],
                          name="l1_attn_bwd", compiler_params=_cparams())(qf, kf, vh, out, lse, dout)


def loss_and_grad(y, target, tl=512):
    def kern(y_ref, t_ref, dy_ref, loss_ref):
        d = y_ref[...] - t_ref[...]
        dy_ref[...] = d / D

        @pl.when(pl.program_id(0) == 0)
        def _():
            loss_ref[...] = jnp.zeros_like(loss_ref)

        loss_ref[...] += 0.5 * jnp.sum(jnp.sum(d * d, axis=1, keepdims=True), axis=0, keepdims=True) / D

    return pl.pallas_call(kern, grid=(L // tl,), in_specs=[rspec(tl, D), rspec(tl, D)], out_specs=[rspec(tl, D), cspec((1, 1))],
                          out_shape=[SDS((L, D), F32), SDS((1, 1), F32)], name="loss", compiler_params=_cparams())(y, target)


def adamw(name, w, g, m, v):
    rows, cols = w.shape
    block_row_bytes = 7 * 2 * 4 * max(cols, 128)
    tr = _row_tile(rows, min(2048, ADAMW_VMEM // block_row_bytes // 8 * 8), 8)

    def kern(w_ref, g_ref, m_ref, v_ref, d_ref, nm_ref, nv_ref):
        gg = g_ref[...]
        nm = ADAM_B1 * m_ref[...] + (1.0 - ADAM_B1) * gg
        nv = ADAM_B2 * v_ref[...] + (1.0 - ADAM_B2) * jnp.square(gg)
        m_hat = nm / (1.0 - ADAM_B1 ** ADAM_STEP)
        v_hat = nv / (1.0 - ADAM_B2 ** ADAM_STEP)
        d_ref[...] = -ADAM_LR * (m_hat / (jnp.sqrt(v_hat) + ADAM_EPS) + ADAM_WD * w_ref[...])
        nm_ref[...] = nm
        nv_ref[...] = nv

    spec = rspec(tr, cols)
    return hosted_call(kern, name=name, grid=(rows // tr,), in_specs=[spec] * 4, out_specs=[spec] * 3,
                       out_shape=[SDS((rows, cols), F32)] * 3, operands=[w, g, m, v])


def _row_tile(rows, cap=512, unit=16):
    return max(t for t in range(unit, cap + 1, unit) if rows % t == 0)


def _place():
    x, y, c = lax.axis_index("x"), lax.axis_index("y"), lax.axis_index("c")
    return x, y, c, [(1 - x, y), (x, 1 - y), (1 - x, 1 - y)]


def _row_chunks(rows, n, dtype):
    unit = 32 // jnp.dtype(dtype).itemsize
    base, extra = divmod(rows // unit, n)
    out, start = [], 0
    for k in range(n):
        size = (base + (k < extra)) * unit
        if size:
            out.append((start, size))
            start += size
    assert start == rows, (rows, unit)
    return out


PIECE_BYTES = 1 << 20


def _pieces(shapes_dtypes, rows_of):
    out = []
    for b, (shape, dtype) in enumerate(shapes_dtypes):
        rows = rows_of(shape)
        n = max(1, min(4, rows * shape[-1] * jnp.dtype(dtype).itemsize // PIECE_BYTES))
        out += [(b, st, sz) for st, sz in _row_chunks(rows, n, dtype)]
    return out


def all_gather_chips(name, shards):
    nb = len(shards)
    pieces = _pieces([(s.shape, s.dtype) for s in shards], lambda shape: shape[0] // 2)
    n = len(pieces)

    def body(*refs):
        x_refs, out_refs, send_sems, recv_sems = refs[:nb], refs[nb:2 * nb], refs[2 * nb], refs[2 * nb + 1]
        x, y, c, chips = _place()
        sibling = (x, y, 1 - c)
        mine = 2 * x + y

        def copy(sem, chip, cc, k, to, from_input=False):
            b, st, sz = pieces[k]
            rows_k = pl.ds(cc * (x_refs[b].shape[0] // 2) + st, sz)
            dst = out_refs[b].at[chip, rows_k, :]
            return pltpu.make_async_remote_copy(src_ref=x_refs[b].at[rows_k, :] if from_input else dst, dst_ref=dst,
                                                send_sem=send_sems.at[sem], recv_sem=recv_sems.at[sem], device_id=to, device_id_type=MESH_ID)

        order = [(k, j, 2 * cx + cy, (cx, cy, c)) for k in range(n) for j, (cx, cy) in enumerate(chips)]
        first = [copy(j * n + k, mine, c, k, to, from_input=True) for k, j, _, to in order]
        for cp in first:
            cp.start()
        passed = []
        for k, j, chip, _ in order:
            copy(j * n + k, chip, c, k, sibling).wait_recv()
            passed.append(copy((3 + j) * n + k, chip, c, k, sibling))
            passed[-1].start()
        for k, j, chip, _ in order:
            copy((3 + j) * n + k, chip, 1 - c, k, sibling).wait_recv()
        for cp in first + passed:
            cp.wait_send()

    return pl.pallas_call(body, in_specs=[ANY] * nb, out_specs=[ANY] * nb, out_shape=[SDS((4,) + s.shape, s.dtype) for s in shards],
                          scratch_shapes=_dma_sems(6 * n), name=name)(*shards)


def plan_gather_ici(shards):
    pieces = _pieces([(s.shape, s.dtype) for s in shards], lambda shape: shape[0] // 2)
    n = len(pieces)

    def copies(x_refs, out_refs, send_sems, recv_sems):
        x, y, c, chips = _place()
        mine = 2 * x + y

        def copy(j, k, chip, to, from_input):
            b, st, sz = pieces[k]
            rows_k = pl.ds(c * (x_refs[b].shape[0] // 2) + st, sz)
            dst = out_refs[b].at[chip, rows_k, :]
            return pltpu.make_async_remote_copy(src_ref=x_refs[b].at[rows_k, :] if from_input else dst, dst_ref=dst, send_sem=send_sems.at[j * n + k],
                                                recv_sem=recv_sems.at[j * n + k], device_id=to, device_id_type=MESH_ID)

        order = [(k, j, 2 * cx + cy, (cx, cy, c)) for k in range(n) for j, (cx, cy) in enumerate(chips)]
        return [copy(j, k, mine, to, True) for k, j, _, to in order], [copy(j, k, chip, to, False) for k, j, chip, to in order]

    return Plan(shards, [SDS((4,) + s.shape, s.dtype) for s in shards], {}, 3 * n, copies)


def plan_gather_pass(gathered):
    pieces = _pieces([(g.shape[1:], g.dtype) for g in gathered], lambda shape: shape[0] // 2)
    n = len(pieces)

    def copies(_, out_refs, send_sems, recv_sems):
        x, y, c, chips = _place()

        def copy(j, k, chip, cc):
            b, st, sz = pieces[k]
            rows_k = out_refs[b].at[chip, pl.ds(cc * (out_refs[b].shape[1] // 2) + st, sz), :]
            return pltpu.make_async_remote_copy(src_ref=rows_k, dst_ref=rows_k, send_sem=send_sems.at[j * n + k], recv_sem=recv_sems.at[j * n + k],
                                                device_id=(x, y, 1 - c), device_id_type=MESH_ID)

        order = [(k, j, 2 * cx + cy) for k in range(n) for j, (cx, cy) in enumerate(chips)]
        return [copy(j, k, chip, c) for k, j, chip in order], [copy(j, k, chip, 1 - c) for k, j, chip in order]

    return Plan(gathered, [SDS(g.shape, g.dtype) for g in gathered], {i: i for i in range(len(gathered))}, 3 * n, copies)


def plan_pair_exchange(gs):
    pieces = _pieces([(g.shape, g.dtype) for g in gs], lambda shape: shape[1] // 2)

    def copies(g_refs, got_refs, send_sems, recv_sems):
        x, y, c, _ = _place()
        swaps = [pltpu.make_async_remote_copy(src_ref=g_refs[b].at[:, pl.ds((1 - c) * (g_refs[b].shape[1] // 2) + st, sz), :],
                                              dst_ref=got_refs[b].at[:, pl.ds(st, sz), :], send_sem=send_sems.at[k], recv_sem=recv_sems.at[k],
                                              device_id=(x, y, 1 - c), device_id_type=MESH_ID)
                 for k, (b, st, sz) in enumerate(pieces)]
        return swaps, swaps

    return Plan(gs, [SDS((g.shape[0], g.shape[1] // 2, g.shape[2]), g.dtype) for g in gs], {}, len(pieces), copies)


def plan_chip_scatter(ps):
    pieces = _pieces([(p.shape, p.dtype) for p in ps], lambda shape: shape[1])
    n = len(pieces)

    def copies(p_refs, q_refs, send_sems, recv_sems):
        x, y, c, chips = _place()
        mine = 2 * x + y

        def copy(j, k, src_slot, dst_slot, to):
            b, st, sz = pieces[k]
            return pltpu.make_async_remote_copy(src_ref=p_refs[b].at[src_slot, pl.ds(st, sz), :], dst_ref=q_refs[b].at[dst_slot, pl.ds(st, sz), :],
                                                send_sem=send_sems.at[j * n + k], recv_sem=recv_sems.at[j * n + k], device_id=to,
                                                device_id_type=MESH_ID)

        order = [(k, j, 2 * cx + cy, (cx, cy, c)) for k in range(n) for j, (cx, cy) in enumerate(chips)]
        return [copy(j, k, chip, mine, to) for k, j, chip, to in order], [copy(j, k, mine, chip, to) for k, j, chip, to in order]

    return Plan(ps, [SDS(p.shape, p.dtype) for p in ps], {}, 3 * n, copies)


def plan_pair_join(bufs):
    pieces = _pieces([(b.shape, b.dtype) for b in bufs], lambda shape: shape[0] // 2)

    def copies(_, out_refs, send_sems, recv_sems):
        x, y, c, _ = _place()

        def copy(k, cc):
            b, st, sz = pieces[k]
            rows_k = out_refs[b].at[pl.ds(cc * (out_refs[b].shape[0] // 2) + st, sz), :]
            return pltpu.make_async_remote_copy(src_ref=rows_k, dst_ref=rows_k, send_sem=send_sems.at[k], recv_sem=recv_sems.at[k],
                                                device_id=(x, y, 1 - c), device_id_type=MESH_ID)

        return [copy(k, c) for k in range(len(pieces))], [copy(k, 1 - c) for k in range(len(pieces))]

    return Plan(bufs, [SDS(b.shape, b.dtype) for b in bufs], {i: i for i in range(len(bufs))}, len(pieces), copies)


def run_plan(name, plan):
    hosted_call(lambda: None, name=name, grid=(1,), in_specs=[], out_specs=[], out_shape=[], operands=[], plans=[plan])
    return plan.results


HBM = pl.BlockSpec(memory_space=pltpu.HBM)
SEMS = pl.BlockSpec(memory_space=pltpu.SEMAPHORE)
SPLIT_PARAMS = dict(has_side_effects=pltpu.SideEffectType.DATAFLOW_SIDE_EFFECTING)


def _plan_buffers(plan):
    in_place = {o: i for i, o in plan.aliases.items()}
    bufs = [pltpu.with_memory_space_constraint(a, pltpu.HBM) for a in plan.operands]
    where = []
    for o, sd in enumerate(plan.out_shape):
        if o in in_place:
            where.append(in_place[o])
        else:
            where.append(len(bufs))
            bufs.append(pltpu.with_memory_space_constraint(lax.empty(sd.shape, sd.dtype), pltpu.HBM))
    return bufs, where


def split_start(name, plans):
    layout = [_plan_buffers(p) for p in plans]
    counts = [len(b) for b, _ in layout]
    n_buf = sum(counts)

    def body(*refs):
        sems, token = refs[n_buf:n_buf + 2 * len(plans)], refs[-1]
        pos = 0
        for k, (p, (_, where)) in enumerate(zip(plans, layout)):
            mine = refs[pos:pos + counts[k]]
            pos += counts[k]
            sends, _ = p.copies(mine[:len(p.operands)], [mine[w] for w in where], sems[2 * k], sems[2 * k + 1])
            for cp in sends:
                cp.start()
        token[...] = jnp.zeros_like(token)

    bufs = [b for bs, _ in layout for b in bs]
    res = pl.pallas_call(
        body, name=name, in_specs=[HBM] * n_buf,
        out_specs=[SEMS] * (2 * len(plans)) + [HBM] * n_buf + [pl.BlockSpec(memory_space=pltpu.VMEM)],
        out_shape=[pltpu.SemaphoreType.DMA((p.n_sems,)) for p in plans for _ in range(2)] + [pltpu.HBM(b.shape, b.dtype) for b in bufs]
        + [SDS((8, 128), F32)],
        input_output_aliases={i: 2 * len(plans) + i for i in range(n_buf)}, compiler_params=pltpu.CompilerParams(**SPLIT_PARAMS))(*bufs)
    pos = 2 * len(plans)
    for k, p in enumerate(plans):
        p.in_flight = (res[2 * k], res[2 * k + 1], list(res[pos:pos + counts[k]]), layout[k][1])
        pos += counts[k]
    return res[-1]


def split_wait(name, plan, after):
    send_sems, recv_sems, bufs, where = plan.in_flight
    n_buf = len(bufs)

    def body(*refs):
        mine = refs[:n_buf]
        sends, recvs = plan.copies(mine[:len(plan.operands)], [mine[w] for w in where], refs[n_buf], refs[n_buf + 1])
        for cp in recvs:
            cp.wait_recv()
        for cp in sends:
            cp.wait_send()

    res = pl.pallas_call(body, name=name, in_specs=[HBM] * n_buf + [SEMS, SEMS, ANY], out_specs=[HBM] * n_buf,
                         out_shape=[pltpu.HBM(b.shape, b.dtype) for b in bufs], input_output_aliases={i: i for i in range(n_buf)},
                         compiler_params=pltpu.CompilerParams(**SPLIT_PARAMS))(*bufs, send_sems, recv_sems, after)
    plan.results = [res[w] for w in where]
    return plan.results


def pair_add(name, g, got, place):
    slots, rows, cols = g.shape
    half = rows // 2
    tr = _row_tile(half)
    nb = half // tr

    def kern(_, g_ref, t_ref, o_ref):
        o_ref[...] = (g_ref[...].astype(F32) + t_ref[...].astype(F32)).astype(o_ref.dtype)

    blk = pl.BlockSpec((None, tr, cols), lambda s, i, p: (s, i, 0))
    grid_spec = pltpu.PrefetchScalarGridSpec(
        num_scalar_prefetch=1, grid=(slots, nb),
        in_specs=[pl.BlockSpec((None, tr, cols), lambda s, i, p: (s, p[1] * nb + i, 0)), blk], out_specs=blk)
    return pl.pallas_call(kern, grid_spec=grid_spec, out_shape=SDS((slots, half, cols), g.dtype), name=name,
                          compiler_params=_cparams())(place, g, got)


def chip_add(name, p, q, place):
    slots, half, cols = p.shape
    tr = _row_tile(half)
    nb = half // tr

    def kern(_, p_ref, q1, q2, q3, o_ref):
        o_ref[...] = p_ref[...].astype(F32) + q1[...].astype(F32) + q2[...].astype(F32) + q3[...].astype(F32)

    def slot(k):
        return pl.BlockSpec((None, tr, cols), lambda i, pr: ((pr[0] + k) % slots, i, 0))

    grid_spec = pltpu.PrefetchScalarGridSpec(
        num_scalar_prefetch=1, grid=(nb,), in_specs=[slot(0), slot(1), slot(2), slot(3)],
        out_specs=pl.BlockSpec((tr, cols), lambda i, pr: (pr[1] * nb + i, 0)))
    return pl.pallas_call(kern, grid_spec=grid_spec, out_shape=SDS((2 * half, cols), F32), name=name,
                          compiler_params=_cparams())(place, p, q, q, q)


def pair_adds(tag, gs, gots, place):
    return [pair_add(f"{tag}_pair_add_{i}", g, got, place) for i, (g, got) in enumerate(zip(gs, gots))]


def chip_adds(tag, pairs, qs, place):
    return [chip_add(f"{tag}_chip_add_{i}", p, q, place) for i, (p, q) in enumerate(zip(pairs, qs))]


def reduce_scatter_chips(tag, gs, place):
    pairs = pair_adds(tag, gs, run_plan(tag + "_pair_exchange", plan_pair_exchange(gs)), place)
    return run_plan(tag + "_pair_join", plan_pair_join(chip_adds(tag, pairs, run_plan(tag + "_chip_scatter", plan_chip_scatter(pairs)), place)))


BIG = [("w_out", (2, 512, 1024)), ("w_mem_kv", (2, 256, 1024)), ("s5_w_in", (1, 1024, 1024)), ("s5_w_glu", (1, 1536, 768)),
       ("mla_w_in", (1, 1024, 848)), ("mla_w_uq", (1, 512, 576)), ("mla_w_ukv", (1, 256, 768))]
SHARDED_SMALL = [("mla_q_lora_norm", (1, 128)), ("mla_kv_lora_norm", (1, 64))]
SMALL = [("ln_gain", (2, 1024)), ("mem_norm", (2, 1024)), ("xq_norm", (2, 128)), ("xk_norm", (2, 128)),
         ("s5_lambda_re", (1, 96, 64)), ("s5_lambda_im", (1, 96, 64)), ("s5_log_step", (1, 96)),
         ("s5_b_re", (1, 96, 64, 16)), ("s5_b_im", (1, 96, 64, 16)), ("s5_c_re", (1, 96, 16, 64)), ("s5_c_im", (1, 96, 16, 64)),
         ("s5_d", (1, 1536)), ("mla_q_nope_norm", (1, 128)), ("mla_k_nope_norm", (1, 128)), ("mla_q_rope_norm", (1, 64)),
         ("mla_k_rope_norm", (1, 64))]
WEIGHT_ORDER = ["ln_gain", "w_out", "mem_norm", "w_mem_kv", "xq_norm", "xk_norm", "s5_w_in", "s5_lambda_re", "s5_lambda_im",
                "s5_log_step", "s5_b_re", "s5_b_im", "s5_c_re", "s5_c_im", "s5_d", "s5_w_glu", "mla_w_in", "mla_q_lora_norm",
                "mla_kv_lora_norm", "mla_w_uq", "mla_w_ukv", "mla_q_nope_norm", "mla_k_nope_norm", "mla_q_rope_norm", "mla_k_rope_norm"]
MINOR_LAST = {"mla_w_in": (0, 2, 1), "mla_w_uq": (0, 2, 1), "s5_b_re": (0, 2, 3, 1), "s5_b_im": (0, 2, 3, 1),
              "s5_c_re": (0, 2, 3, 1), "s5_c_im": (0, 2, 3, 1)}
SMALL_FULL = SMALL + [(n, (1, 4 * s[1])) for n, s in SHARDED_SMALL]
N_SMALL = sum(math.prod(s) for _, s in SMALL_FULL)
SMALL_ROWS, SMALL_LANES = 128, 1024

PAIR_OUT, PAIR_MKV, PAIR_ROWS = 0, 512, 768


def stack_shards(w, dtype):
    pairs = [jnp.concatenate([w["w_out"][l], w["w_mem_kv"][l]], axis=0).astype(dtype) for l in range(2)]
    return ([w["s5_w_in"][0].astype(dtype)], [pairs[0], w["s5_w_glu"][0].astype(dtype)],
            [pairs[1], w["mla_w_ukv"][0].astype(dtype), w["mla_w_in"][0].astype(dtype), w["mla_w_uq"][0].astype(dtype)])


def pair_views(pair):
    return {"w_out": Sharded(pair, "row", PAIR_OUT, 512), "w_mem_kv": Sharded(pair, "row", PAIR_MKV, 256)}


def grad_views():
    pair = SDS((4, PAIR_ROWS, 1024), BF16)
    return {"w_out": Sharded(pair, "row", PAIR_OUT, 512), "w_mem_kv": Sharded(pair, "row", PAIR_MKV, 256),
            "s5_w_in": Sharded(SDS((4, 1024, 1024), BF16), "col", 0, 1024), "s5_w_glu": Sharded(SDS((4, 1536, 768), BF16), "col", 0, 1536),
            "mla_w_ukv": Sharded(SDS((4, 256, 768), BF16), "col", 0, 256)}


def cols_to_shards(full):
    return full.reshape(full.shape[0], 4, full.shape[1] // 4).transpose(1, 0, 2)


def shards_to_cols(arr):
    return arr.transpose(1, 0, 2).reshape(arr.shape[1], 4 * arr.shape[2])


def mla_in_permute(w):
    o1, o2, o3, o4 = QL, QL + KVL, QL + KVL + ROPE, QL + KVL + ROPE + XQW
    return jnp.concatenate([w[:, o4:], w[:, :o1], w[:, o3:o4], w[:, o1:o2], w[:, o2:o3],
                            jnp.zeros((w.shape[0], MLA_IN_P - MLA_IN), w.dtype)], axis=1)


def mla_in_unpermute(d):
    return jnp.concatenate([d[:, 2048:2560], d[:, 3072:3328], d[:, 3328:3392], d[:, 2560:3072], d[:, :2048]], axis=1)


def uq_permute(w):
    w3 = w.reshape(w.shape[0], MH, NOPE + ROPE)
    return jnp.concatenate([w3[:, :, :NOPE].reshape(w.shape[0], MH * NOPE), w3[:, :, NOPE:].reshape(w.shape[0], MH * ROPE)], axis=1)


def uq_unpermute(d):
    dn = d[:, :MH * NOPE].reshape(d.shape[0], MH, NOPE)
    dr = d[:, MH * NOPE:].reshape(d.shape[0], MH, ROPE)
    return jnp.concatenate([dn, dr], axis=2).reshape(d.shape[0], MH * (NOPE + ROPE))


def time_permute(a):
    return a.reshape(SEG, SEG_LEN, a.shape[-1]).transpose(1, 0, 2).reshape(L, a.shape[-1])


def time_unpermute(a):
    return a.reshape(SEG_LEN, SEG, a.shape[-1]).transpose(1, 0, 2).reshape(L, a.shape[-1])


def mem_branch_fwd(tag, mem, mem_norm, w_mem_kv, xk_norm):
    mn = row_fwd(tag + "_mem_rms", fn_rms, ML, ML, [(mem, D, 0)], [mem_norm], [(D, BF16)])[0]
    kv = matmul(tag + "_mem_kv", mn, w_mem_kv, "nn", F32)
    kn = row_fwd(tag + "_mem_knorm", fn_mem_k, ML, ML, [(kv, XQW, 0)], [xk_norm], [(XQW, F32)])[0]
    return mn, kv, kn


def mem_branch_bwd(tag, mem, mem_norm, w_mem_kv, xk_norm, mn, kv, dkn, dv, g_view, g_wide):
    dk, dxk = row_bwd(tag + "_mem_knorm_bwd", fn_mem_k, ML, ML, [(kv, XQW, 0)], [xk_norm], [(dkn, XQW, 0)], [True], [True])
    dkv = jnp.concatenate([dk, dv], axis=1)
    dmn = matmul(tag + "_mem_kv_dx", dkv, w_mem_kv, "nt", F32)
    g_wide = matmul(tag + "_mem_kv_dw", mn, dkv, "tn", out=g_view, into=g_wide)
    dmem_norm = row_bwd(tag + "_mem_rms_bwd", fn_rms, ML, ML, [(mem, D, 0)], [mem_norm], [(dmn, D, 0)], [False], [True])[0]
    return g_wide, dmem_norm, dxk


def mem_attn_fwd(tag, proj, cb, kn, kv, xq_norm):
    return row_fwd(tag + "_mem_attn", fn_mem_attn, L, ROW_TILE, [(proj, XQW, cb)], [kn, kv[:, XQW:], xq_norm], [(XQW, F32)])[0]


def mem_attn_bwd(tag, proj, cb, kn, kv, xq_norm, dmo, dproj):
    place = {"cols": proj.shape[1], "cb": cb, "into": dproj, "dtype": dproj.dtype}
    return row_bwd(tag + "_mem_attn_bwd", fn_mem_attn, L, ROW_TILE, [(proj, XQW, cb)], [kn, kv[:, XQW:], xq_norm], [(dmo, XQW, 0)],
                   [place], [True, True, True])


def device_step(x, mem, positions, target, small, env, hooks=None):
    hooks = hooks or {}

    def plans_for(name):
        return hooks[("plans", name)](env) if ("plans", name) in hooks else ()

    def around(when, name, last=None):
        if (when, name) in hooks:
            hooks[(when, name)](env, last)

    g = {}
    gw = grad_views()
    ln, mem_norm, xq_norm, xk_norm = small["ln_gain"], small["mem_norm"], small["xq_norm"], small["xk_norm"]

    lre, lim = small["s5_lambda_re"][0], small["s5_lambda_im"][0]
    ls = small["s5_log_step"].reshape(SG, 1)
    one = pl.BlockSpec((SG, SP), lambda i: (0, 0))
    col = pl.BlockSpec((SG, 1), lambda i: (0, 0))
    disc_ins = [(lre, one), (lim, one), (ls, col)]
    a_re, a_im, coef_re, coef_im = stage("s5_disc", fn_s5_disc, (1,), disc_ins, [(SDS((SG, SP), F32), one)] * 4)
    b_re, b_im = small["s5_b_re"].reshape(SN, SC), small["s5_b_im"].reshape(SN, SC)
    c_re, c_im = small["s5_c_re"].reshape(PW, SP), small["s5_c_im"].reshape(PW, SP)
    bmat_rows = [(b_re, SC, 0), (b_im, SC, 0), (coef_re.reshape(SN, 1), 1, 0), (coef_im.reshape(SN, 1), 1, 0)]
    wb_re, wb_im = row_fwd("s5_bmat", fn_s5_bmat, SN, 512, bmat_rows, [], [(128, F32)] * 2)
    cmat_rows = [(c_re, SP, 0), (c_im, SP, 0)]
    wc_re, wc_im = row_fwd("s5_cmat", fn_s5_cmat, PW, 128, cmat_rows, [], [(512, F32)] * 2)
    a_re_v, a_im_v = a_re.reshape(1, SN), a_im.reshape(1, SN)
    s5_d = small["s5_d"]

    xp = time_permute(x)
    h0 = row_fwd("l0_rms", fn_rms, L, ROW_TILE, [(xp, D, 0)], [ln[0:1]], [(D, BF16)])[0]
    around("before", "l0_in", wb_re)
    w_in0 = Sharded(env["in0"], "col", 0, 1024)
    proj0 = matmul("l0_in", h0, w_in0, "nn")
    s_re, s_im, g0 = s5_forward(proj0, wb_re, wb_im, wc_re, wc_im, a_re_v, a_im_v, s5_d, plans=plans_for("s5_forward"))
    around("before", "l0_glu", g0)
    w0 = dict(pair_views(env["pair0"]), s5_w_glu=Sharded(env["glu"], "col", 0, 1536))
    z0 = matmul("l0_glu", g0, w0["s5_w_glu"], "nn", plans=plans_for("l0_glu"))
    mn0, kv0, kn0 = mem_branch_fwd("l0", mem, mem_norm[0:1], w0["w_mem_kv"], xk_norm[0:1])
    mo0 = mem_attn_fwd("l0", proj0, 3, kn0, kv0, xq_norm[0:1])
    o0 = row_fwd("l0_merge", fn_merge_glu, L, ROW_TILE, [(z0, 2 * PW, 0), (mo0, XQW, 0), (proj0, BW, 1)], [], [(BW, BF16)])[0]
    around("before", "l0_out", o0)
    x1p = matmul("l0_out", o0, w0["w_out"], "nn", F32, add=xp, plans=plans_for("l0_out"))
    around("after", "l0_out", x1p)
    x1 = time_unpermute(x1p)

    w1 = dict(pair_views(env["pair1"]), mla_w_ukv=Sharded(env["ukv"], "col", 0, 256))
    w_in1, w_uq = env["w_in1"], env["w_uq"]
    h1 = row_fwd("l1_rms", fn_rms, L, ROW_TILE, [(x1, D, 0)], [ln[1:2]], [(D, BF16)])[0]
    proj1 = matmul("l1_in", h1, w_in1, "nn")
    qln, kvln = env["q_lora_norm"].reshape(1, QL), env["kv_lora_norm"].reshape(1, KVL)
    cqn = row_fwd("l1_q_lora_rms", fn_rms, L, ROW_TILE, [(proj1, QL, 4)], [qln], [(QL, BF16)])[0]
    ckvn = row_fwd("l1_kv_lora_rms", fn_rms, L, ROW_TILE, [(proj1, KVL, 12)], [kvln], [(KVL, BF16)])[0]
    q = matmul("l1_uq", cqn, w_uq, "nn")
    kv = matmul("l1_ukv", ckvn, w1["mla_w_ukv"], "nn")
    inv_freq = ROPE_THETA ** (-jnp.arange(ROPE // 2, dtype=F32) / (ROPE // 2))
    ang = positions.astype(F32)[:, None] * inv_freq
    cos2 = jnp.tile(jnp.cos(ang), (1, 4))
    sin_signed = jnp.tile(jnp.concatenate([-jnp.sin(ang), jnp.sin(ang)], axis=1), (1, 2))
    qnn, knn = small["mla_q_nope_norm"], small["mla_k_nope_norm"]
    qrn, krn = jnp.tile(small["mla_q_rope_norm"], (1, 2)), jnp.tile(small["mla_k_rope_norm"], (1, 2))
    tp = 256
    prep_ins = [(q, rspec(tp, MH * (NOPE + ROPE))), (kv, rspec(tp, MH * 256)), (proj1, rspec(tp, 128, 26)),
                (cos2, rspec(tp, 128)), (sin_signed, rspec(tp, 128))] + [(a, cspec((1, 128))) for a in (qnn, knn, qrn, krn)]
    hq_spec = pl.BlockSpec((MH, tp, 256), lambda i: (0, i, 0))
    hv_spec = pl.BlockSpec((MH, tp, 128), lambda i: (0, i, 0))
    qf, kf, vh = stage("l1_mla_prep", fn_mla_prep, (L // tp,), prep_ins,
                       [(SDS((MH, L, 256), BF16), hq_spec), (SDS((MH, L, 256), BF16), hq_spec), (SDS((MH, L, 128), BF16), hv_spec)])
    attn, attn_lse = causal_attn(qf, kf, vh)
    mn1, kv1, kn1 = mem_branch_fwd("l1", mem, mem_norm[1:2], w1["w_mem_kv"], xk_norm[1:2])
    mo1 = mem_attn_fwd("l1", proj1, 5, kn1, kv1, xq_norm[1:2])
    o1 = row_fwd("l1_merge", fn_merge, L, ROW_TILE, [(attn, PW, 0), (mo1, XQW, 0), (proj1, BW, 0)], [], [(BW, BF16)])[0]
    x2 = matmul("l1_out", o1, w1["w_out"], "nn", F32, add=x1)
    dx2, loss = loss_and_grad(x2, target)
    around("after", "loss", loss)

    do1 = matmul("l1_out_dx", dx2, w1["w_out"], "nt")
    g_pair1 = matmul("l1_out_dw", o1, dx2, "tn", out=gw["w_out"])
    dattn, dmo1, dproj1 = row_bwd("l1_merge_bwd", fn_merge, L, ROW_TILE, [(attn, PW, 0), (mo1, XQW, 0), (proj1, BW, 0)], [],
                                  [(do1, BW, 0)], [True, True, {"cols": MLA_IN_P, "cb": 0, "dtype": BF16}], [])
    dproj1, dkn1, dv1, dxqn1 = mem_attn_bwd("l1", proj1, 5, kn1, kv1, xq_norm[1:2], dmo1, dproj1)
    env["g_pair1"], dmem_norm1, dxk1 = mem_branch_bwd("l1", mem, mem_norm[1:2], w1["w_mem_kv"], xk_norm[1:2], mn1, kv1, dkn1, dv1,
                                                      gw["w_mem_kv"], g_pair1)
    dqf, dkf, dvh = causal_attn_bwd(qf, kf, vh, attn, attn_lse, dattn)
    prep_diffs = [("row", SDS((L, MH * (NOPE + ROPE)), BF16), rspec(tp, MH * (NOPE + ROPE))), ("row", SDS((L, MH * 256), BF16), rspec(tp, MH * 256)),
                  ("row", SDS((L, MLA_IN_P), BF16), rspec(tp, 128, 26), {"into": dproj1}), None, None] + [("acc", (0,))] * 4
    dq, dkv, dproj1, dqnn, dknn, dqrn, dkrn = stage_bwd("l1_mla_prep_bwd", fn_mla_prep, (L // tp,), prep_ins,
                                                        [(dqf, hq_spec), (dkf, hq_spec), (dvh, hv_spec)], prep_diffs)
    dcqn = matmul("l1_uq_dx", dq, w_uq, "nt")
    env["dw_uq"] = matmul("l1_uq_dw", cqn, dq, "tn")
    dckvn = matmul("l1_ukv_dx", dkv, w1["mla_w_ukv"], "nt")
    env["g_ukv"] = matmul("l1_ukv_dw", ckvn, dkv, "tn", out=gw["mla_w_ukv"])
    dproj1, dqln = row_bwd("l1_q_lora_rms_bwd", fn_rms, L, ROW_TILE, [(proj1, QL, 4)], [qln], [(dcqn, QL, 0)],
                           [{"cols": MLA_IN_P, "cb": 4, "into": dproj1, "dtype": BF16}], [True])
    dproj1, dkvln = row_bwd("l1_kv_lora_rms_bwd", fn_rms, L, ROW_TILE, [(proj1, KVL, 12)], [kvln], [(dckvn, KVL, 0)],
                            [{"cols": MLA_IN_P, "cb": 12, "into": dproj1, "dtype": BF16}], [True])
    env["dw_in1"] = matmul("l1_in_dw", h1, dproj1, "tn")
    dh1 = matmul("l1_in_dx", dproj1, w_in1, "nt", plans=plans_for("l1_in_dx"))
    around("after", "l1_in_dx", dh1)
    dx1, dln1 = row_bwd("l1_rms_bwd", fn_rms, L, ROW_TILE, [(x1, D, 0)], [ln[1:2]], [(dh1, D, 0)], [{"add": (dx2, D, 0)}], [True])
    dx1p = time_permute(dx1)

    do0 = matmul("l0_out_dx", dx1p, w0["w_out"], "nt", plans=plans_for("l0_out_dx"))
    g_pair0 = matmul("l0_out_dw", o0, dx1p, "tn", out=gw["w_out"])
    dz0, dmo0, dproj0 = row_bwd("l0_merge_bwd", fn_merge_glu, L, ROW_TILE, [(z0, 2 * PW, 0), (mo0, XQW, 0), (proj0, BW, 1)], [],
                                [(do0, BW, 0)], [{"dtype": BF16}, True, {"cols": 2 * BW, "cb": 1, "dtype": BF16}], [])
    dproj0, dkn0, dv0, dxqn0 = mem_attn_bwd("l0", proj0, 3, kn0, kv0, xq_norm[0:1], dmo0, dproj0)
    env["g_pair0"], dmem_norm0, dxk0 = mem_branch_bwd("l0", mem, mem_norm[0:1], w0["w_mem_kv"], xk_norm[0:1], mn0, kv0, dkn0, dv0,
                                                      gw["w_mem_kv"], g_pair0)
    env["g_glu"] = matmul("l0_glu_dw", g0, dz0, "tn", out=gw["s5_w_glu"], plans=plans_for("l0_glu_dw"))
    dg0 = matmul("l0_glu_dx", dz0, w0["s5_w_glu"], "nt", plans=plans_for("l0_glu_dx"))
    around("before", "s5_backward", dg0)
    dproj0, dd, dwc_re, dwc_im, dwb_re, dwb_im, da_re, da_im = s5_backward(dg0, proj0, s_re, s_im, wb_re, wb_im, wc_re, wc_im,
                                                                           a_re_v, a_im_v, s5_d, dproj0, plans=plans_for("s5_backward"))
    around("after", "s5_backward", dd)
    env["g_in0"] = matmul("l0_in_dw", h0, dproj0, "tn", out=gw["s5_w_in"], plans=plans_for("l0_in_dw"))
    dh0 = matmul("l0_in_dx", dproj0, w_in0, "nt", plans=plans_for("l0_in_dx"))
    around("after", "l0_in_dx", dh0)
    dxp, dln0 = row_bwd("l0_rms_bwd", fn_rms, L, ROW_TILE, [(xp, D, 0)], [ln[0:1]], [(dh0, D, 0)], [{"add": (dx1p, D, 0)}], [True])
    grad_x = time_unpermute(dxp)

    db_re, db_im, dcoef_re, dcoef_im = row_bwd("s5_bmat_bwd", fn_s5_bmat, SN, 512, bmat_rows, [], [(dwb_re, 128, 0), (dwb_im, 128, 0)],
                                               [True] * 4, [], plans=plans_for("s5_bmat_bwd"))
    dc_re, dc_im = row_bwd("s5_cmat_bwd", fn_s5_cmat, PW, 128, cmat_rows, [], [(dwc_re, 512, 0), (dwc_im, 512, 0)], [True] * 2, [],
                           plans=plans_for("s5_cmat_bwd"))
    disc_cts = [(da_re.reshape(SG, SP), one), (da_im.reshape(SG, SP), one), (dcoef_re.reshape(SG, SP), one), (dcoef_im.reshape(SG, SP), one)]
    dlre, dlim, dls = stage_bwd("s5_disc_bwd", fn_s5_disc, (1,), disc_ins, disc_cts, [("acc", (0,))] * 3)

    g["ln_gain"] = jnp.concatenate([dln0, dln1], axis=0)
    g["mem_norm"] = jnp.concatenate([dmem_norm0, dmem_norm1], axis=0)
    g["xq_norm"] = jnp.concatenate([dxqn0, dxqn1], axis=0)
    g["xk_norm"] = jnp.concatenate([dxk0, dxk1], axis=0)
    g["s5_lambda_re"], g["s5_lambda_im"], g["s5_log_step"] = dlre, dlim, dls
    g["s5_b_re"], g["s5_b_im"], g["s5_c_re"], g["s5_c_im"] = db_re, db_im, dc_re, dc_im
    g["s5_d"] = dd
    g["mla_q_lora_norm"], g["mla_kv_lora_norm"] = dqln, dkvln
    g["mla_q_nope_norm"], g["mla_k_nope_norm"] = dqnn, dknn
    g["mla_q_rope_norm"] = dqrn[:, :ROPE] + dqrn[:, ROPE:]
    g["mla_k_rope_norm"] = dkrn[:, :ROPE] + dkrn[:, ROPE:]
    return loss, grad_x, g


def kernel(x, mem, positions, ln_gain, w_out, mem_norm, w_mem_kv, xq_norm, xk_norm, s5_w_in, s5_lambda_re, s5_lambda_im, s5_log_step, s5_b_re, s5_b_im, s5_c_re, s5_c_im, s5_d, s5_w_glu, mla_w_in, mla_q_lora_norm, mla_kv_lora_norm, mla_w_uq, mla_w_ukv, mla_q_nope_norm, mla_k_nope_norm, mla_q_rope_norm, mla_k_rope_norm, loss_target, m_ln_gain, m_w_out, m_mem_norm, m_w_mem_kv, m_xq_norm, m_xk_norm, m_s5_w_in, m_s5_lambda_re, m_s5_lambda_im, m_s5_log_step, m_s5_b_re, m_s5_b_im, m_s5_c_re, m_s5_c_im, m_s5_d, m_s5_w_glu, m_mla_w_in, m_mla_q_lora_norm, m_mla_kv_lora_norm, m_mla_w_uq, m_mla_w_ukv, m_mla_q_nope_norm, m_mla_k_nope_norm, m_mla_q_rope_norm, m_mla_k_rope_norm, v_ln_gain, v_w_out, v_mem_norm, v_w_mem_kv, v_xq_norm, v_xk_norm, v_s5_w_in, v_s5_lambda_re, v_s5_lambda_im, v_s5_log_step, v_s5_b_re, v_s5_b_im, v_s5_c_re, v_s5_c_im, v_s5_d, v_s5_w_glu, v_mla_w_in, v_mla_q_lora_norm, v_mla_kv_lora_norm, v_mla_w_uq, v_mla_w_ukv, v_mla_q_nope_norm, v_mla_k_nope_norm, v_mla_q_rope_norm, v_mla_k_rope_norm):
    args = dict(locals())
    wts = {n: args[n] for n in WEIGHT_ORDER}
    mom = {n: args["m_" + n] for n in WEIGHT_ORDER}
    var = {n: args["v_" + n] for n in WEIGHT_ORDER}

    chip = 2 * lax.axis_index("x") + lax.axis_index("y")
    place = jnp.stack([chip, lax.axis_index("c")]).astype(jnp.int32)

    def own_slot(gathered, shards):
        return [lax.dynamic_update_slice(g, s[None], (chip, 0, 0)) for g, s in zip(gathered, shards)]

    groups = list(stack_shards(wts, BF16))
    groups[2].append(jnp.concatenate([mla_q_lora_norm, jnp.pad(mla_kv_lora_norm, ((0, 0), (0, 64))), jnp.zeros((14, 128), F32)], axis=0))
    over_ici = [plan_gather_ici(shards) for shards in groups]
    _SCHEDULE_BEHIND.clear()
    schedule_behind(split_start("gather_start", over_ici))
    env, hooks, passed_on = {}, {}, {}

    def arrived(k, after, pass_now):
        passing = plan_gather_pass(split_wait(f"gather_wait_{k}", over_ici[k], after))
        passed_on[k] = passing
        return own_slot(run_plan(f"gather_pass_{k}", passing), groups[k]) if pass_now else None

    def need_in0(env, last):
        env["in0"], = arrived(0, last, True)

    def need_layer0(env, last):
        env["pair0"], env["glu"] = arrived(1, last, True)

    def need_layer1(env, last):
        arrived(2, last, False)

    def layer1_weights(env, last):
        pair1, ukv, in1, uq, norms = own_slot(passed_on[2].results, groups[2])
        env.update(pair1=pair1, ukv=ukv, w_in1=mla_in_permute(shards_to_cols(in1)), w_uq=uq_permute(shards_to_cols(uq)),
                   q_lora_norm=norms[:, 0, :], kv_lora_norm=norms[:, 1, :64])

    hooks["before", "l0_in"], hooks["before", "l0_glu"], hooks["before", "l0_out"] = need_in0, need_layer0, need_layer1
    hooks["plans", "l0_out"], hooks["after", "l0_out"] = (lambda env: [passed_on[2]]), layer1_weights

    rs = {}

    def swap(k, gs):
        rs[k, "g"], rs[k, "swap"] = gs, plan_pair_exchange(gs)
        return rs[k, "swap"]

    def start_scatter(k):
        rs[k, "pairs"] = pair_adds(f"rs{k}", rs[k, "g"], rs[k, "swap"].results, place)
        rs[k, "scatter"] = plan_chip_scatter(rs[k, "pairs"])
        schedule_behind(split_start(f"rs{k}_scatter_start", [rs[k, "scatter"]]))

    def join(k, after):
        rs[k, "join"] = plan_pair_join(chip_adds(f"rs{k}", rs[k, "pairs"], split_wait(f"rs{k}_scatter_wait", rs[k, "scatter"], after), place))
        return rs[k, "join"]

    hooks["plans", "l1_in_dx"] = lambda env: [swap(1, [env["g_pair1"], env["g_ukv"], cols_to_shards(mla_in_unpermute(env["dw_in1"])).astype(BF16),
                                                        cols_to_shards(uq_unpermute(env["dw_uq"])).astype(BF16)])]
    hooks["after", "l1_in_dx"] = lambda env, last: start_scatter(1)
    hooks["plans", "l0_glu_dx"] = lambda env: [swap(0, [env["g_pair0"], env["g_glu"]])]

    def before_s5_backward(env, last):
        start_scatter(0)
        env["join1"] = join(1, last)

    hooks["before", "s5_backward"] = before_s5_backward
    hooks["plans", "s5_backward"] = lambda env: [env["join1"]]
    hooks["after", "s5_backward"] = lambda env, last: env.update(join0=join(0, last))
    hooks["plans", "l0_in_dx"] = lambda env: [swap(2, [env["g_in0"]]), env["join0"]]
    hooks["after", "l0_in_dx"] = lambda env, last: start_scatter(2)

    def total_loss(env, local):
        env["loss"] = lax.psum(local[0, 0], MESH_AXES)
        schedule_behind(env["loss"].reshape(1, 1))

    hooks["after", "loss"] = total_loss
    small = {n: wts[n] for n, _ in SMALL}
    loss, grad_x, g = device_step(x[0], mem[0], positions[0], loss_target[0], small, env, hooks)
    loss = env["loss"]
    r_in0, = run_plan("rs2_pair_join", join(2, g["s5_log_step"]))
    (r_pair1, r_ukv, r_in1, r_uq), (r_pair0, r_glu) = (rs[k, "join"].results for k in (1, 0))

    small_flat = jnp.concatenate([g[n].reshape(-1) for n, _ in SMALL_FULL])
    g_small = jnp.pad(small_flat, (0, 4 * SMALL_ROWS * SMALL_LANES - N_SMALL)).astype(BF16).reshape(4, SMALL_ROWS, SMALL_LANES)
    r_small = reduce_scatter_chips("rs3", [g_small], place)[0]
    small_all = own_slot(all_gather_chips("gather_small_grads", [r_small]), [r_small])[0].reshape(-1)[:N_SMALL]

    grads = {"w_out": jnp.stack([r_pair0[:PAIR_MKV], r_pair1[:PAIR_MKV]]), "w_mem_kv": jnp.stack([r_pair0[PAIR_MKV:], r_pair1[PAIR_MKV:]]),
             "s5_w_in": r_in0[None], "s5_w_glu": r_glu[None], "mla_w_ukv": r_ukv[None], "mla_w_in": r_in1[None], "mla_w_uq": r_uq[None]}
    off = 0
    for n, s in SMALL_FULL:
        grads[n] = small_all[off:off + math.prod(s)].reshape(s)
        off += math.prod(s)
    for n, s in SHARDED_SMALL:
        grads[n] = lax.dynamic_slice(grads[n], (0, chip * s[1]), s)

    delta, new_m, new_v = {}, {}, {}
    for n, s in BIG + [(n, s) for n, s in SMALL if len(s) == 4]:
        perm = MINOR_LAST.get(n, tuple(range(len(s))))
        turned = tuple(s[p] for p in perm)
        view = lambda a: jnp.transpose(a, perm).reshape(-1, turned[-1])
        res = adamw("adamw_" + n, view(wts[n]), view(grads[n]), view(mom[n]), view(var[n]))
        delta[n], new_m[n], new_v[n] = (jnp.transpose(r.reshape(turned), tuple(perm.index(i) for i in range(len(s)))) for r in res)
    small_names = [n for n, s in SMALL if len(s) < 4] + [n for n, _ in SHARDED_SMALL]
    n_own = sum(wts[n].size for n in small_names)
    rows_own = -(-n_own // (8 * 128)) * 8

    def pack_small(d):
        flat = jnp.concatenate([d[n].reshape(-1) for n in small_names])
        return jnp.pad(flat, (0, rows_own * 128 - n_own), constant_values=1.0).reshape(rows_own, 128)

    res = adamw("adamw_small", pack_small(wts), pack_small(grads), pack_small(mom), pack_small(var))
    off = 0
    for n in small_names:
        size = wts[n].size
        delta[n], new_m[n], new_v[n] = (r.reshape(-1)[off:off + size].reshape(wts[n].shape) for r in res)
        off += size

    return (loss, grad_x[None], *[grads[n] for n in WEIGHT_ORDER], *[delta[n] for n in WEIGHT_ORDER],
            *[new_m[n] for n in WEIGHT_ORDER], *[new_v[n] for n in WEIGHT_ORDER])
```

```python
import functools
import math

import jax
import jax.numpy as jnp
from jax import lax
from jax.experimental import pallas as pl
from jax.experimental.pallas import tpu as pltpu

F32, BF16 = jnp.float32, jnp.bfloat16
SDS = jax.ShapeDtypeStruct

D = 1024
L = 2048
ML = 256
BW = 2 * D
XQW = BW // 4
PW = BW - XQW
XH, XHD = 4, 128
SG, SC, SP = 96, 16, 64
SN = SG * SP
NOPE, ROPE, VD = 128, 64, 128
MH = 12
QL, KVL = 512, 256
EPS = 1e-6
ROPE_THETA = 10000.0
MLA_IN = QL + KVL + ROPE + XQW + BW
MLA_IN_P = 3456
ADAM_LR, ADAM_B1, ADAM_B2, ADAM_EPS, ADAM_WD, ADAM_STEP = 0.001, 0.9, 0.999, 1e-08, 0.01, 10

VMEM_LIMIT = 48 * 2**20
ROW_TILE = 512
SEG = 8
SEG_LEN = L // SEG
MESH_AXES = ("x", "y", "c")


def _cparams():
    return pltpu.CompilerParams(vmem_limit_bytes=VMEM_LIMIT)


def _dg(a, b, ca, cb):
    return lax.dot_general(a.astype(BF16), b.astype(BF16), (((ca,), (cb,)), ((), ())), preferred_element_type=F32)


@jax.custom_vjp
def mm_nn(a, b):
    return _dg(a, b, 1, 0)


mm_nn.defvjp(lambda a, b: (_dg(a, b, 1, 0), (a, b)), lambda res, g: (_dg(g, res[1], 1, 1), _dg(res[0], g, 0, 0)))


@jax.custom_vjp
def mm_nt(a, b):
    return _dg(a, b, 1, 1)


mm_nt.defvjp(lambda a, b: (_dg(a, b, 1, 1), (a, b)), lambda res, g: (_dg(g, res[1], 1, 0), _dg(g, res[0], 0, 0)))


@functools.partial(jax.custom_vjp, nondiff_argnums=(1,))
def lane_roll(x, shift):
    return pltpu.roll(x, shift, 1)


lane_roll.defvjp(lambda x, shift: (pltpu.roll(x, shift, 1), None),
                 lambda shift, _, g: (pltpu.roll(g, (128 - shift) % 128, 1),))


def rms(x, g):
    return x * lax.rsqrt(jnp.mean(x * x, axis=-1, keepdims=True) + EPS) * g


@jax.custom_vjp
def softmax_rows(s):
    e = jnp.exp(s - jnp.max(s, axis=-1, keepdims=True))
    return e / jnp.sum(e, axis=-1, keepdims=True)


def _softmax_rows_fwd(s):
    p = softmax_rows(s)
    return p, p


def _softmax_rows_bwd(p, g):
    return (p * (g - jnp.sum(g * p, axis=-1, keepdims=True)),)


softmax_rows.defvjp(_softmax_rows_fwd, _softmax_rows_bwd)


def silu(x):
    return x * jax.nn.sigmoid(x)


ANY = pl.BlockSpec(memory_space=pl.ANY)
MESH_ID = pl.DeviceIdType.MESH


def _dma_sems(n):
    return [pltpu.SemaphoreType.DMA((n,)), pltpu.SemaphoreType.DMA((n,))]


class Plan:
    def __init__(self, operands, out_shape, aliases, n_sems, copies):
        self.operands, self.out_shape, self.aliases, self.n_sems, self.copies = list(operands), list(out_shape), aliases, n_sems, copies
        self.results = None


_SCHEDULE_BEHIND = []


def schedule_behind(token):
    _SCHEDULE_BEHIND.append(token)


def hosted_call(kern, *, name, grid, in_specs, out_specs, out_shape, operands, scratch_shapes=(), aliases=None, cparams=None, plans=(), deps=()):
    n_in, n_out, n_scr = len(in_specs), len(out_specs), len(scratch_shapes)
    p_in, p_out = [len(p.operands) for p in plans], [len(p.out_shape) for p in plans]
    deps = tuple(deps) + tuple(_SCHEDULE_BEHIND)
    _SCHEDULE_BEHIND.clear()
    all_aliases = dict(aliases or {})
    in_off, out_off = n_in, n_out
    for p, ni, no in zip(plans, p_in, p_out):
        all_aliases.update({in_off + i: out_off + o for i, o in p.aliases.items()})
        in_off, out_off = in_off + ni, out_off + no

    def body(*refs):
        pos, pins, pouts = n_in, [], []
        for ni in p_in:
            pins.append(refs[pos:pos + ni])
            pos += ni
        pos += len(deps)
        main_out = refs[pos:pos + n_out]
        pos += n_out
        for no in p_out:
            pouts.append(refs[pos:pos + no])
            pos += no
        main_scr = refs[pos:pos + n_scr]
        pos += n_scr
        if plans:
            ids = [pl.program_id(ax) for ax in range(len(grid))]
            first = functools.reduce(jnp.logical_and, [i == 0 for i in ids])
            last = functools.reduce(jnp.logical_and, [i == g - 1 for i, g in zip(ids, grid)])
            copies = [p.copies(pins[k], pouts[k], refs[pos + 2 * k], refs[pos + 2 * k + 1]) for k, p in enumerate(plans)]

            @pl.when(first)
            def _():
                for sends, _ in copies:
                    for cp in sends:
                        cp.start()

        kern(*refs[:n_in], *main_out, *main_scr)
        if plans:
            @pl.when(last)
            def _():
                for sends, recvs in copies:
                    for cp in recvs:
                        cp.wait_recv()
                    for cp in sends:
                        cp.wait_send()

    res = pl.pallas_call(body, grid=grid, in_specs=list(in_specs) + [ANY] * (sum(p_in) + len(deps)),
                         out_specs=list(out_specs) + [ANY] * sum(p_out), out_shape=list(out_shape) + [s for p in plans for s in p.out_shape],
                         scratch_shapes=list(scratch_shapes) + [s for p in plans for s in _dma_sems(p.n_sems)],
                         input_output_aliases=all_aliases, name=name, compiler_params=cparams or _cparams())(
        *operands, *[a for p in plans for a in p.operands], *deps)
    pos = n_out
    for p, no in zip(plans, p_out):
        p.results = list(res[pos:pos + no])
        pos += no
    return list(res[:n_out])


def _wide(v):
    return v.astype(F32) if v.dtype == BF16 else v


def stage(name, fn, grid, ins, outs):
    n_in = len(ins)

    def kern(*refs):
        res = fn(*[_wide(r[...]) for r in refs[:n_in]])
        for r, v in zip(refs[n_in:], res):
            r[...] = v.astype(r.dtype)

    return hosted_call(kern, name=name, grid=grid, in_specs=[s for _, s in ins], out_specs=[s for _, s in outs],
                       out_shape=[sd for sd, _ in outs], operands=[a for a, _ in ins])


def stage_bwd(name, fn, grid, ins, cts, diffs, plans=()):
    n_in, n_ct = len(ins), len(cts)
    didx = [i for i, d in enumerate(diffs) if d is not None]
    opts = {i: (diffs[i][3] if len(diffs[i]) > 3 else {}) for i in didx if diffs[i][0] == "row"}
    adds = [(i, opts[i]["add"]) for i in opts if "add" in opts[i]]
    intos = [(i, opts[i]["into"]) for i in opts if "into" in opts[i]]
    n_add, n_into = len(adds), len(intos)
    add_pos = {i: n_in + n_ct + k for k, (i, _) in enumerate(adds)}
    n_extra = n_in + n_ct + n_add + n_into

    def kern(*refs):
        vals = [_wide(r[...]) for r in refs[:n_in]]

        def f(*dv):
            full = list(vals)
            for i, v in zip(didx, dv):
                full[i] = v
            return fn(*full)

        _, vjp = jax.vjp(f, *[vals[i].astype(F32) for i in didx])
        gs = vjp(tuple(c[...].astype(F32) for c in refs[n_in:n_in + n_ct]))
        for o_ref, i, g in zip(refs[n_extra:], didx, gs):
            if diffs[i][0] == "row":
                if i in add_pos:
                    g = g + refs[add_pos[i]][...].astype(F32)
                o_ref[...] = g.astype(o_ref.dtype)
            else:
                first = functools.reduce(jnp.logical_and, [pl.program_id(ax) == 0 for ax in diffs[i][1]])

                @pl.when(first)
                def _():
                    o_ref[...] = g

                @pl.when(jnp.logical_not(first))
                def _():
                    o_ref[...] += g

    out_shape, out_specs = [], []
    for i in didx:
        if diffs[i][0] == "row":
            out_shape.append(diffs[i][1])
            out_specs.append(diffs[i][2])
        else:
            out_shape.append(SDS(ins[i][0].shape, F32))
            out_specs.append(ins[i][1])
    aliases = {n_in + n_ct + n_add + k: didx.index(i) for k, (i, _) in enumerate(intos)}
    in_specs = [s for _, s in ins] + [s for _, s in cts] + [s for _, (_, s) in adds] + [ANY] * n_into
    operands = [a for a, _ in ins] + [a for a, _ in cts] + [a for _, (a, _) in adds] + [a for _, a in intos]
    return hosted_call(kern, name=name, grid=grid, in_specs=in_specs, out_specs=out_specs, out_shape=out_shape, operands=operands,
                       aliases=aliases, plans=plans)


def rspec(tl, w, cb=0):
    return pl.BlockSpec((tl, w), lambda i: (i, cb))


def cspec(shape):
    return pl.BlockSpec(shape, lambda i: (0,) * len(shape))


def row_fwd(name, fn, rows, tl, row_ins, consts, outs):
    ins = [(a, rspec(tl, w, cb)) for a, w, cb in row_ins] + [(a, cspec(a.shape)) for a in consts]
    return stage(name, fn, (rows // tl,), ins, [(SDS((rows, w), dt), rspec(tl, w)) for w, dt in outs])


def row_bwd(name, fn, rows, tl, row_ins, consts, cts, row_diff, const_diff, plans=()):
    ins = [(a, rspec(tl, w, cb)) for a, w, cb in row_ins] + [(a, cspec(a.shape)) for a in consts]
    diffs = []
    for (a, w, cb), d in zip(row_ins, row_diff):
        if not d:
            diffs.append(None)
            continue
        d = d if isinstance(d, dict) else {}
        opts = {}
        if "add" in d:
            opts["add"] = (d["add"][0], rspec(tl, d["add"][1], d["add"][2]))
        if d.get("into") is not None:
            opts["into"] = d["into"]
        diffs.append(("row", SDS((rows, d.get("cols", w)), d.get("dtype", F32)), rspec(tl, w, d.get("cb", 0)), opts))
    diffs += [("acc", (0,)) if d else None for d in const_diff]
    return stage_bwd(name, fn, (rows // tl,), ins, [(a, rspec(tl, w, cb)) for a, w, cb in cts], diffs, plans=plans)


MATMUL_VMEM = 36 * 2**20
ADAMW_VMEM = 28 * 2**20


class Sharded:
    def __init__(self, arr, kind, roff, rows):
        self.arr, self.kind, self.roff, self.rows, self.n = arr, kind, roff, rows, arr.shape[2]
        self.shape = (rows, 4 * self.n) if kind == "col" else (4 * rows, self.n)

    def fits(self, t0, t1):
        return self.roff % t0 == 0 and self.rows % t0 == 0 and self.n % t1 == 0

    def spec(self, t0, t1, bidx):
        assert self.fits(t0, t1), (self.kind, self.roff, self.rows, self.n, t0, t1)
        r0 = self.roff // t0
        if self.kind == "col":
            per = self.n // t1
            return pl.BlockSpec((None, t0, t1), lambda *g: (bidx(*g)[1] // per, r0 + bidx(*g)[0], bidx(*g)[1] % per))
        per = self.rows // t0
        return pl.BlockSpec((None, t0, t1), lambda *g: (bidx(*g)[0] // per, r0 + bidx(*g)[0] % per, bidx(*g)[1]))


def matmul(name, a, b, mode, out_dtype=BF16, add=None, out=None, into=None, plans=()):
    if mode == "tn":
        k_dim, m = a.shape
    else:
        m, k_dim = a.shape
    n = b.shape[0] if mode == "nt" else b.shape[1]
    b_fit = b.fits if isinstance(b, Sharded) else (lambda t0, t1: True)
    o_fit = out.fits if out is not None else (lambda t0, t1: True)
    a_bytes, b_bytes = jnp.dtype(a.dtype).itemsize, jnp.dtype(b.arr.dtype if isinstance(b, Sharded) else b.dtype).itemsize
    o_bytes = jnp.dtype(out_dtype if out is None else out.arr.dtype).itemsize

    def vmem(tm, tn, tk):
        return 2 * (tm * tk * a_bytes + tk * tn * b_bytes + tm * tn * (o_bytes + (4 if add is not None else 0))) + 4 * tm * tn * (1 + (tk < k_dim))

    tiles = [(tm, tn, tk) for tm in (2048, 1024, 512, 256, 128) for tn in (1024, 768, 512, 384, 256, 128)
             for tk in sorted({k_dim, 1024, 768, 512, 384, 256, 128})
             if m % tm == 0 and n % tn == 0 and k_dim % tk == 0 and (b_fit(tn, tk) if mode == "nt" else b_fit(tk, tn)) and o_fit(tm, tn)
             and vmem(tm, tn, tk) <= MATMUL_VMEM]
    tm, tn, tk = max(tiles, key=lambda t: (t[2] == k_dim, t[0] * t[1] * t[2], t[0] * t[1]))
    nk = k_dim // tk
    a_spec = pl.BlockSpec((tk, tm), lambda i, j, k: (k, i)) if mode == "tn" else pl.BlockSpec((tm, tk), lambda i, j, k: (i, k))
    if isinstance(b, Sharded):
        b_spec = b.spec(tn, tk, lambda i, j, k: (j, k)) if mode == "nt" else b.spec(tk, tn, lambda i, j, k: (k, j))
        b = b.arr
    else:
        b_spec = pl.BlockSpec((tn, tk), lambda i, j, k: (j, k)) if mode == "nt" else pl.BlockSpec((tk, tn), lambda i, j, k: (k, j))
    o_spec = pl.BlockSpec((tm, tn), lambda i, j, k: (i, j))
    out_spec, out_shape = (o_spec, SDS((m, n), out_dtype)) if out is None else (out.spec(tm, tn, lambda i, j, k: (i, j)), out.arr)
    ca, cb = {"nn": (1, 0), "nt": (1, 1), "tn": (0, 0)}[mode]
    n_in = 2 + (add is not None)

    def finish(refs, o_ref, r):
        if add is not None:
            r = r + refs[2][...]
        o_ref[...] = r.astype(o_ref.dtype)

    def kern_whole(*refs):
        finish(refs, refs[-1], _dg(refs[0][...], refs[1][...], ca, cb))

    def kern_cut(*refs):
        o_ref, acc = refs[-2], refs[-1]
        k = pl.program_id(2)

        @pl.when(k == 0)
        def _():
            acc[...] = jnp.zeros_like(acc)

        acc[...] += _dg(refs[0][...], refs[1][...], ca, cb)

        @pl.when(k == nk - 1)
        def _():
            finish(refs, o_ref, acc[...])

    ins, specs = [a, b], [a_spec, b_spec]
    if add is not None:
        ins.append(add)
        specs.append(o_spec)
    if into is not None:
        ins.append(into)
        specs.append(ANY)
    return hosted_call(kern_whole if nk == 1 else kern_cut, name=name, grid=(m // tm, n // tn, nk), in_specs=specs, out_specs=[out_spec],
                       out_shape=[out_shape], operands=ins, scratch_shapes=[] if nk == 1 else [pltpu.VMEM((tm, tn), F32)],
                       aliases={} if into is None else {n_in: 0}, plans=plans)[0]


SCAN_UNROLL = 8


def _cmul(ar, ai, br, bi):
    return ar * br - ai * bi, ar * bi + ai * br


def _sub_shift(x, down):
    row = lax.broadcasted_iota(jnp.int32, x.shape, 0)
    if down:
        return jnp.where(row == 0, 0.0, pltpu.roll(x, 1, 0))
    return jnp.where(row == SEG - 1, 0.0, pltpu.roll(x, SEG - 1, 0))


def _pow_seg_len(ar, ai):
    for _ in range(int(math.log2(SEG_LEN))):
        ar, ai = _cmul(ar, ai, ar, ai)
    return ar, ai


def _scan_in_place(sr, si, a_re, a_im):
    lanes = sr.shape[1]
    ar = jnp.broadcast_to(a_re, (SEG, lanes))
    ai = jnp.broadcast_to(a_im, (SEG, lanes))
    zero = jnp.zeros((SEG, lanes), F32)

    def local(i, carry):
        rows = pl.ds(pl.multiple_of(i * SEG, SEG), SEG)
        mr, mi = _cmul(ar, ai, carry[0], carry[1])
        nr, ni = mr + sr[rows, :], mi + si[rows, :]
        sr[rows, :] = nr
        si[rows, :] = ni
        return nr, ni

    fr, fi = lax.fori_loop(0, SEG_LEN, local, (zero, zero), unroll=SCAN_UNROLL)
    pr, pi = _pow_seg_len(ar, ai)
    ir, ii = zero, zero
    for _ in range(SEG - 1):
        mr, mi = _cmul(pr, pi, ir, ii)
        ir, ii = _sub_shift(mr + fr, True), _sub_shift(mi + fi, True)

    def carry_in(i, pw):
        rows = pl.ds(pl.multiple_of(i * SEG, SEG), SEG)
        cr, ci = _cmul(pw[0], pw[1], ir, ii)
        sr[rows, :] += cr
        si[rows, :] += ci
        return _cmul(pw[0], pw[1], ar, ai)

    lax.fori_loop(0, SEG_LEN, carry_in, (ar, ai), unroll=SCAN_UNROLL)


S5_LANES = 8 * SP
S5_BLOCKS = SN // S5_LANES


def _s5_specs():
    u_spec = pl.BlockSpec((L, 8 * SC), lambda j: (0, j))
    s_spec = pl.BlockSpec((L, S5_LANES), lambda j: (0, j))
    wb_spec = pl.BlockSpec((S5_LANES, 8 * SC), lambda j: (j, 0))
    wc_spec = pl.BlockSpec((8 * SC, S5_LANES), lambda j: (j, 0))
    a_spec = pl.BlockSpec((1, S5_LANES), lambda j: (0, j))
    d_spec = pl.BlockSpec((1, 8 * SC), lambda j: (0, j))
    return u_spec, s_spec, wb_spec, wc_spec, a_spec, d_spec


def s5_forward(proj, wb_re, wb_im, wc_re, wc_im, a_re, a_im, d, plans=()):
    def kern(u_ref, wbr, wbi, wcr, wci, ar, ai, d_ref, sr_out, si_out, g_ref, sr, si):
        u = _wide(u_ref[...])
        sr[...], si[...] = fn_s5_bu(u, wbr[...], wbi[...])
        _scan_in_place(sr, si, ar[...], ai[...])
        g_ref[...] = fn_s5_out(sr[...], si[...], u, d_ref[...], wcr[...], wci[...])[0].astype(g_ref.dtype)
        sr_out[...] = sr[...].astype(sr_out.dtype)
        si_out[...] = si[...].astype(si_out.dtype)

    u_spec, s_spec, wb_spec, wc_spec, a_spec, d_spec = _s5_specs()
    return hosted_call(kern, name="s5_forward", grid=(S5_BLOCKS,), in_specs=[u_spec, wb_spec, wb_spec, wc_spec, wc_spec, a_spec, a_spec, d_spec],
                       out_specs=[s_spec, s_spec, u_spec], out_shape=[SDS((L, SN), BF16)] * 2 + [SDS((L, PW), BF16)],
                       operands=[proj, wb_re, wb_im, wc_re, wc_im, a_re, a_im, d], scratch_shapes=[pltpu.VMEM((L, S5_LANES), F32)] * 2,
                       plans=plans)


def _adjoint_scan_in_place(lr, li, sr, si, a_re, a_im):
    lanes = lr.shape[1]
    ar = jnp.broadcast_to(a_re, (SEG, lanes))
    ai = -jnp.broadcast_to(a_im, (SEG, lanes))
    zero = jnp.zeros((SEG, lanes), F32)

    def local(k, carry):
        i = SEG_LEN - 1 - k
        rows = pl.ds(pl.multiple_of(i * SEG, SEG), SEG)
        mr, mi = _cmul(ar, ai, carry[0], carry[1])
        nr, ni = mr + lr[rows, :], mi + li[rows, :]
        lr[rows, :] = nr
        li[rows, :] = ni
        return nr, ni

    fr, fi = lax.fori_loop(0, SEG_LEN, local, (zero, zero), unroll=SCAN_UNROLL)
    pr, pi = _pow_seg_len(ar, ai)
    ir, ii = zero, zero
    for _ in range(SEG - 1):
        mr, mi = _cmul(pr, pi, ir, ii)
        ir, ii = _sub_shift(mr + fr, False), _sub_shift(mi + fi, False)

    def fix(rows, pw):
        cr, ci = _cmul(pw[0], pw[1], ir, ii)
        tr, ti = lr[rows, :] + cr, li[rows, :] + ci
        lr[rows, :] = tr
        li[rows, :] = ti
        return tr, ti

    def grad_a(tr, ti, spr, spi, acc):
        return acc[0] + tr * spr + ti * spi, acc[1] + ti * spr - tr * spi

    def carry_in(k, c):
        i = SEG_LEN - 1 - k
        rows = pl.ds(pl.multiple_of(i * SEG, SEG), SEG)
        prev = pl.ds(pl.multiple_of((i - 1) * SEG, SEG), SEG)
        tr, ti = fix(rows, (c[0], c[1]))
        acc = grad_a(tr, ti, sr[prev, :], si[prev, :], (c[2], c[3]))
        nr, ni = _cmul(c[0], c[1], ar, ai)
        return nr, ni, acc[0], acc[1]

    pwr, pwi, accr, acci = lax.fori_loop(0, SEG_LEN - 1, carry_in, (ar, ai, zero, zero), unroll=5)
    tr, ti = fix(pl.ds(0, SEG), (pwr, pwi))
    last = pl.ds((SEG_LEN - 1) * SEG, SEG)
    accr, acci = grad_a(tr, ti, _sub_shift(sr[last, :], True), _sub_shift(si[last, :], True), (accr, acci))
    return jnp.sum(accr, axis=0, keepdims=True), jnp.sum(acci, axis=0, keepdims=True)


S5_BWD_VMEM = 58 * 2**20


def s5_backward(dg, proj, s_re, s_im, wb_re, wb_im, wc_re, wc_im, a_re, a_im, d, dproj, plans=()):
    def kern(dg_ref, u_ref, sr_in, si_in, wbr, wbi, wcr, wci, ar, ai, d_ref, _, du_ref, dd_ref, dwcr, dwci, dwbr, dwbi, dar, dai, lr, li, sr, si):
        u = _wide(u_ref[...])
        sr[...], si[...] = _wide(sr_in[...]), _wide(si_in[...])
        _, vjp_out = jax.vjp(fn_s5_out, sr[...], si[...], u, d_ref[...], wcr[...], wci[...])
        lr[...], li[...], du_out, dd_ref[...], dwcr[...], dwci[...] = vjp_out((_wide(dg_ref[...]),))
        dar[...], dai[...] = _adjoint_scan_in_place(lr, li, sr, si, ar[...], ai[...])
        _, vjp_in = jax.vjp(fn_s5_bu, u, wbr[...], wbi[...])
        du_in, dwbr[...], dwbi[...] = vjp_in((lr[...], li[...]))
        du_ref[...] = (du_out + du_in).astype(du_ref.dtype)

    u_spec, s_spec, wb_spec, wc_spec, a_spec, d_spec = _s5_specs()
    outs = [(SDS(dproj.shape, dproj.dtype), u_spec), (SDS(d.shape, F32), d_spec), (SDS(wc_re.shape, F32), wc_spec), (SDS(wc_im.shape, F32), wc_spec),
            (SDS(wb_re.shape, F32), wb_spec), (SDS(wb_im.shape, F32), wb_spec), (SDS(a_re.shape, F32), a_spec), (SDS(a_im.shape, F32), a_spec)]
    return hosted_call(kern, name="s5_backward", grid=(S5_BLOCKS,),
                       in_specs=[u_spec, u_spec, s_spec, s_spec, wb_spec, wb_spec, wc_spec, wc_spec, a_spec, a_spec, d_spec, ANY],
                       out_specs=[sp for _, sp in outs], out_shape=[sd for sd, _ in outs], aliases={11: 0},
                       operands=[dg, proj, s_re, s_im, wb_re, wb_im, wc_re, wc_im, a_re, a_im, d, dproj],
                       scratch_shapes=[pltpu.VMEM((L, S5_LANES), F32)] * 4,
                       cparams=pltpu.CompilerParams(vmem_limit_bytes=S5_BWD_VMEM), plans=plans)


def fn_rms(x, g):
    return (rms(x, g),)


def fn_s5_disc(lre, lim, ls):
    step = jnp.exp(ls)
    e = jnp.exp(lre * step)
    a_re, a_im = e * jnp.cos(lim * step), e * jnp.sin(lim * step)
    den = lre * lre + lim * lim
    nr, ni = a_re - 1.0, a_im
    return a_re, a_im, (nr * lre + ni * lim) / den, (ni * lre - nr * lim) / den


def _group_mask(rows, cols, row_div, col_div):
    r = lax.broadcasted_iota(jnp.int32, (rows, cols), 0) // row_div % 8
    c = lax.broadcasted_iota(jnp.int32, (rows, cols), 1) // col_div
    return r == c


def _spread(x, mask):
    w = x.shape[1]
    copy = (lax.broadcasted_iota(jnp.int32, (w, 8 * w), 1) % w == lax.broadcasted_iota(jnp.int32, (w, 8 * w), 0)).astype(F32)
    return jnp.where(mask, jnp.dot(x, copy, precision=lax.Precision.HIGHEST, preferred_element_type=F32), 0.0)


def fn_s5_bmat(b_re, b_im, coef_re, coef_im):
    mask = _group_mask(b_re.shape[0], 8 * SC, SP, SC)
    return _spread(coef_re * b_re - coef_im * b_im, mask), _spread(coef_re * b_im + coef_im * b_re, mask)


def fn_s5_cmat(c_re, c_im):
    mask = _group_mask(c_re.shape[0], 8 * SP, SC, SP)
    return _spread(c_re, mask), _spread(c_im, mask)


def fn_s5_bu(u, wb_re, wb_im):
    return mm_nt(u, wb_re), mm_nt(u, wb_im)


def fn_s5_out(sr, si, u, d, wc_re, wc_im):
    y = mm_nt(sr, wc_re) - mm_nt(si, wc_im) + d * u
    return (jax.nn.gelu(y),)


def fn_merge_glu(z, mo, gate):
    yg = z[:, :PW] * jax.nn.sigmoid(z[:, PW:])
    return (jnp.concatenate([yg, mo], axis=1) * silu(gate),)


def fn_merge(prim, mo, gate):
    return (jnp.concatenate([prim, mo], axis=1) * silu(gate),)


def fn_mem_k(kv, g):
    return (jnp.concatenate([rms(kv[:, h * XHD:(h + 1) * XHD], g) for h in range(XH)], axis=1),)


def fn_mem_attn(xq, kn, v, g):
    outs = []
    for h in range(XH):
        sl = slice(h * XHD, (h + 1) * XHD)
        p = softmax_rows(mm_nt(rms(xq[:, sl], g), kn[:, sl]) * (XHD ** -0.5))
        outs.append(mm_nn(p, v[:, sl]))
    return (jnp.concatenate(outs, axis=1),)


def _half_rms(x, g):
    lo = lax.broadcasted_iota(jnp.int32, x.shape, 1) < ROPE
    x2 = x * x
    s_lo = jnp.sum(jnp.where(lo, x2, 0.0), axis=1, keepdims=True)
    s_hi = jnp.sum(jnp.where(lo, 0.0, x2), axis=1, keepdims=True)
    return x * lax.rsqrt(jnp.where(lo, s_lo, s_hi) / ROPE + EPS) * g


def _rope(x, cos2, sin_signed):
    first = lax.broadcasted_iota(jnp.int32, x.shape, 1) % ROPE < ROPE // 2
    return x * cos2 + jnp.where(first, lane_roll(x, 128 - ROPE // 2), lane_roll(x, ROPE // 2)) * sin_signed


def fn_mla_prep(q, kv, kr, cos2, sin_signed, qnn, knn, qrn, krn):
    lo = lax.broadcasted_iota(jnp.int32, kr.shape, 1) < ROPE
    kr_pad = jnp.where(lo, _rope(_half_rms(kr, krn), cos2, sin_signed), 0.0)
    qf, kf, vs = [], [], []
    for m in range(MH // 2):
        pair = _rope(_half_rms(q[:, MH * NOPE + 128 * m:MH * NOPE + 128 * (m + 1)], qrn), cos2, sin_signed)
        for h, rope_h in ((2 * m, pair), (2 * m + 1, lane_roll(pair, ROPE))):
            qf.append(jnp.concatenate([rms(q[:, NOPE * h:NOPE * (h + 1)], qnn), jnp.where(lo, rope_h, 0.0)], axis=1))
    for h in range(MH):
        kf.append(jnp.concatenate([rms(kv[:, 256 * h:256 * h + NOPE], knn), kr_pad], axis=1))
        vs.append(kv[:, 256 * h + NOPE:256 * (h + 1)])
    return jnp.stack(qf), jnp.stack(kf), jnp.stack(vs)


ATT_TQ = 512


def _attn_scores(q, kf):
    tq = q.shape[0]
    scale = (NOPE + ROPE) ** -0.5
    own = _dg(q, kf[-tq:], 1, 1) * scale
    own = jnp.where(lax.broadcasted_iota(jnp.int32, own.shape, 1) <= lax.broadcasted_iota(jnp.int32, own.shape, 0), own, jnp.finfo(F32).min)
    return own if kf.shape[0] == tq else jnp.concatenate([_dg(q, kf[:-tq], 1, 1) * scale, own], axis=1)


ATT_FWD_HEADS, ATT_BWD_HEADS = 4, 3
ATT_BWD_VMEM = 58 * 2**20


def _attn_specs(heads):
    q_spec = pl.BlockSpec((heads, ATT_TQ, 256), lambda h, i: (h, i, 0))
    k_spec = pl.BlockSpec((heads, L, 256), lambda h, i: (h, 0, 0))
    v_spec = pl.BlockSpec((heads, L, 128), lambda h, i: (h, 0, 0))
    o_spec = pl.BlockSpec((ATT_TQ, heads * 128), lambda h, i: (i, h))
    lse_spec = pl.BlockSpec((heads, ATT_TQ, 1), lambda h, i: (h, i, 0))
    return q_spec, k_spec, v_spec, o_spec, lse_spec


def _attn_branches(heads, body):
    i = pl.program_id(1)
    for t in range(L // ATT_TQ):
        @pl.when(i == t)
        def _(t=t):
            for hh in range(heads):
                body(hh, slice(hh * 128, (hh + 1) * 128), (t + 1) * ATT_TQ)


def causal_attn(qf, kf, vh):
    def kern(q_ref, k_ref, v_ref, o_ref, lse_ref):
        def body(hh, lanes, keys):
            s = _attn_scores(q_ref[hh], k_ref[hh, :keys, :])
            m = jnp.max(s, axis=-1, keepdims=True)
            e = jnp.exp(s - m)
            total = jnp.sum(e, axis=-1, keepdims=True)
            o_ref[:, lanes] = (_dg(e, v_ref[hh, :keys, :], 1, 0) / total).astype(o_ref.dtype)
            lse_ref[hh] = m + jnp.log(total)

        _attn_branches(ATT_FWD_HEADS, body)

    q_spec, k_spec, v_spec, o_spec, lse_spec = _attn_specs(ATT_FWD_HEADS)
    return pl.pallas_call(kern, grid=(MH // ATT_FWD_HEADS, L // ATT_TQ), in_specs=[q_spec, k_spec, v_spec], out_specs=[o_spec, lse_spec],
                          out_shape=[SDS((L, MH * VD), F32), SDS((MH, L, 1), F32)], name="l1_attn", compiler_params=_cparams())(qf, kf, vh)


def causal_attn_bwd(qf, kf, vh, out, lse, dout):
    scale = (NOPE + ROPE) ** -0.5

    def kern(q_ref, k_ref, v_ref, o_ref, lse_ref, do_ref, dq_ref, dk_ref, dv_ref):
        @pl.when(pl.program_id(1) == 0)
        def _():
            dk_ref[...] = jnp.zeros_like(dk_ref)
            dv_ref[...] = jnp.zeros_like(dv_ref)

        def body(hh, lanes, keys):
            q, k, v, do = q_ref[hh], k_ref[hh, :keys, :], v_ref[hh, :keys, :], do_ref[:, lanes]
            p = jnp.exp(_attn_scores(q, k) - lse_ref[hh])
            delta = jnp.sum(do * _wide(o_ref[:, lanes]), axis=-1, keepdims=True)
            dv_ref[hh, :keys, :] += _dg(p, do, 0, 0)
            ds = p * (_dg(do, v, 1, 1) - delta) * scale
            dq_ref[hh] = _dg(ds, k, 1, 0)
            dk_ref[hh, :keys, :] += _dg(ds, q, 0, 0)

        _attn_branches(ATT_BWD_HEADS, body)

    q_spec, k_spec, v_spec, o_spec, lse_spec = _attn_specs(ATT_BWD_HEADS)
    return pl.pallas_call(kern, grid=(MH // ATT_BWD_HEADS, L // ATT_TQ), in_specs=[q_spec, k_spec, v_spec, o_spec, lse_spec, o_spec],
                          out_specs=[q_spec, k_spec, v_spec], out_shape=[SDS(qf.shape, F32), SDS(kf.shape, F32), SDS(vh.shape, F32)],
                          name="l1_attn_bwd", compiler_params=pltpu.CompilerParams(vmem_limit_bytes=ATT_BWD_VMEM))(qf, kf, vh, out, lse, dout)


def loss_and_grad(y, target, tl=512):
    def kern(y_ref, t_ref, dy_ref, loss_ref):
        d = y_ref[...] - t_ref[...]
        dy_ref[...] = d / D

        @pl.when(pl.program_id(0) == 0)
        def _():
            loss_ref[...] = jnp.zeros_like(loss_ref)

        loss_ref[...] += 0.5 * jnp.sum(jnp.sum(d * d, axis=1, keepdims=True), axis=0, keepdims=True) / D

    return pl.pallas_call(kern, grid=(L // tl,), in_specs=[rspec(tl, D), rspec(tl, D)], out_specs=[rspec(tl, D), cspec((1, 1))],
                          out_shape=[SDS((L, D), F32), SDS((1, 1), F32)], name="loss", compiler_params=_cparams())(y, target)


def adamw(name, w, g, m, v):
    rows, cols = w.shape
    block_row_bytes = 7 * 2 * 4 * max(cols, 128)
    tr = _row_tile(rows, min(2048, ADAMW_VMEM // block_row_bytes // 8 * 8), 8)

    def kern(w_ref, g_ref, m_ref, v_ref, d_ref, nm_ref, nv_ref):
        gg = g_ref[...]
        nm = ADAM_B1 * m_ref[...] + (1.0 - ADAM_B1) * gg
        nv = ADAM_B2 * v_ref[...] + (1.0 - ADAM_B2) * jnp.square(gg)
        m_hat = nm / (1.0 - ADAM_B1 ** ADAM_STEP)
        v_hat = nv / (1.0 - ADAM_B2 ** ADAM_STEP)
        d_ref[...] = -ADAM_LR * (m_hat / (jnp.sqrt(v_hat) + ADAM_EPS) + ADAM_WD * w_ref[...])
        nm_ref[...] = nm
        nv_ref[...] = nv

    spec = rspec(tr, cols)
    return hosted_call(kern, name=name, grid=(rows // tr,), in_specs=[spec] * 4, out_specs=[spec] * 3,
                       out_shape=[SDS((rows, cols), F32)] * 3, operands=[w, g, m, v])


def _row_tile(rows, cap=512, unit=16):
    return max(t for t in range(unit, cap + 1, unit) if rows % t == 0)


def _place():
    x, y, c = lax.axis_index("x"), lax.axis_index("y"), lax.axis_index("c")
    return x, y, c, [(1 - x, y), (x, 1 - y), (1 - x, 1 - y)]


def _row_chunks(rows, n, dtype):
    unit = 32 // jnp.dtype(dtype).itemsize
    base, extra = divmod(rows // unit, n)
    out, start = [], 0
    for k in range(n):
        size = (base + (k < extra)) * unit
        if size:
            out.append((start, size))
            start += size
    assert start == rows, (rows, unit)
    return out


PIECE_BYTES = 1 << 20


def _pieces(shapes_dtypes, rows_of):
    out = []
    for b, (shape, dtype) in enumerate(shapes_dtypes):
        rows = rows_of(shape)
        n = max(1, min(4, rows * shape[-1] * jnp.dtype(dtype).itemsize // PIECE_BYTES))
        out += [(b, st, sz) for st, sz in _row_chunks(rows, n, dtype)]
    return out


def all_gather_chips(name, shards):
    nb = len(shards)
    pieces = _pieces([(s.shape, s.dtype) for s in shards], lambda shape: shape[0] // 2)
    n = len(pieces)

    def body(*refs):
        x_refs, out_refs, send_sems, recv_sems = refs[:nb], refs[nb:2 * nb], refs[2 * nb], refs[2 * nb + 1]
        x, y, c, chips = _place()
        sibling = (x, y, 1 - c)
        mine = 2 * x + y

        def copy(sem, chip, cc, k, to, from_input=False):
            b, st, sz = pieces[k]
            rows_k = pl.ds(cc * (x_refs[b].shape[0] // 2) + st, sz)
            dst = out_refs[b].at[chip, rows_k, :]
            return pltpu.make_async_remote_copy(src_ref=x_refs[b].at[rows_k, :] if from_input else dst, dst_ref=dst,
                                                send_sem=send_sems.at[sem], recv_sem=recv_sems.at[sem], device_id=to, device_id_type=MESH_ID)

        order = [(k, j, 2 * cx + cy, (cx, cy, c)) for k in range(n) for j, (cx, cy) in enumerate(chips)]
        first = [copy(j * n + k, mine, c, k, to, from_input=True) for k, j, _, to in order]
        for cp in first:
            cp.start()
        passed = []
        for k, j, chip, _ in order:
            copy(j * n + k, chip, c, k, sibling).wait_recv()
            passed.append(copy((3 + j) * n + k, chip, c, k, sibling))
            passed[-1].start()
        for k, j, chip, _ in order:
            copy((3 + j) * n + k, chip, 1 - c, k, sibling).wait_recv()
        for cp in first + passed:
            cp.wait_send()

    return pl.pallas_call(body, in_specs=[ANY] * nb, out_specs=[ANY] * nb, out_shape=[SDS((4,) + s.shape, s.dtype) for s in shards],
                          scratch_shapes=_dma_sems(6 * n), name=name)(*shards)


def plan_gather_ici(shards):
    pieces = _pieces([(s.shape, s.dtype) for s in shards], lambda shape: shape[0] // 2)
    n = len(pieces)

    def copies(x_refs, out_refs, send_sems, recv_sems):
        x, y, c, chips = _place()
        mine = 2 * x + y

        def copy(j, k, chip, to, from_input):
            b, st, sz = pieces[k]
            rows_k = pl.ds(c * (x_refs[b].shape[0] // 2) + st, sz)
            dst = out_refs[b].at[chip, rows_k, :]
            return pltpu.make_async_remote_copy(src_ref=x_refs[b].at[rows_k, :] if from_input else dst, dst_ref=dst, send_sem=send_sems.at[j * n + k],
                                                recv_sem=recv_sems.at[j * n + k], device_id=to, device_id_type=MESH_ID)

        order = [(k, j, 2 * cx + cy, (cx, cy, c)) for k in range(n) for j, (cx, cy) in enumerate(chips)]
        return [copy(j, k, mine, to, True) for k, j, _, to in order], [copy(j, k, chip, to, False) for k, j, chip, to in order]

    return Plan(shards, [SDS((4,) + s.shape, s.dtype) for s in shards], {}, 3 * n, copies)


def plan_gather_pass(gathered):
    pieces = _pieces([(g.shape[1:], g.dtype) for g in gathered], lambda shape: shape[0] // 2)
    n = len(pieces)

    def copies(_, out_refs, send_sems, recv_sems):
        x, y, c, chips = _place()

        def copy(j, k, chip, cc):
            b, st, sz = pieces[k]
            rows_k = out_refs[b].at[chip, pl.ds(cc * (out_refs[b].shape[1] // 2) + st, sz), :]
            return pltpu.make_async_remote_copy(src_ref=rows_k, dst_ref=rows_k, send_sem=send_sems.at[j * n + k], recv_sem=recv_sems.at[j * n + k],
                                                device_id=(x, y, 1 - c), device_id_type=MESH_ID)

        order = [(k, j, 2 * cx + cy) for k in range(n) for j, (cx, cy) in enumerate(chips)]
        return [copy(j, k, chip, c) for k, j, chip in order], [copy(j, k, chip, 1 - c) for k, j, chip in order]

    return Plan(gathered, [SDS(g.shape, g.dtype) for g in gathered], {i: i for i in range(len(gathered))}, 3 * n, copies)


def plan_pair_exchange(gs):
    pieces = _pieces([(g.shape, g.dtype) for g in gs], lambda shape: shape[1] // 2)

    def copies(g_refs, got_refs, send_sems, recv_sems):
        x, y, c, _ = _place()
        swaps = [pltpu.make_async_remote_copy(src_ref=g_refs[b].at[:, pl.ds((1 - c) * (g_refs[b].shape[1] // 2) + st, sz), :],
                                              dst_ref=got_refs[b].at[:, pl.ds(st, sz), :], send_sem=send_sems.at[k], recv_sem=recv_sems.at[k],
                                              device_id=(x, y, 1 - c), device_id_type=MESH_ID)
                 for k, (b, st, sz) in enumerate(pieces)]
        return swaps, swaps

    return Plan(gs, [SDS((g.shape[0], g.shape[1] // 2, g.shape[2]), g.dtype) for g in gs], {}, len(pieces), copies)


def plan_chip_scatter(ps):
    pieces = _pieces([(p.shape, p.dtype) for p in ps], lambda shape: shape[1])
    n = len(pieces)

    def copies(p_refs, q_refs, send_sems, recv_sems):
        x, y, c, chips = _place()
        mine = 2 * x + y

        def copy(j, k, src_slot, dst_slot, to):
            b, st, sz = pieces[k]
            return pltpu.make_async_remote_copy(src_ref=p_refs[b].at[src_slot, pl.ds(st, sz), :], dst_ref=q_refs[b].at[dst_slot, pl.ds(st, sz), :],
                                                send_sem=send_sems.at[j * n + k], recv_sem=recv_sems.at[j * n + k], device_id=to,
                                                device_id_type=MESH_ID)

        order = [(k, j, 2 * cx + cy, (cx, cy, c)) for k in range(n) for j, (cx, cy) in enumerate(chips)]
        return [copy(j, k, chip, mine, to) for k, j, chip, to in order], [copy(j, k, mine, chip, to) for k, j, chip, to in order]

    return Plan(ps, [SDS(p.shape, p.dtype) for p in ps], {}, 3 * n, copies)


def plan_pair_join(bufs):
    pieces = _pieces([(b.shape, b.dtype) for b in bufs], lambda shape: shape[0] // 2)

    def copies(_, out_refs, send_sems, recv_sems):
        x, y, c, _ = _place()

        def copy(k, cc):
            b, st, sz = pieces[k]
            rows_k = out_refs[b].at[pl.ds(cc * (out_refs[b].shape[0] // 2) + st, sz), :]
            return pltpu.make_async_remote_copy(src_ref=rows_k, dst_ref=rows_k, send_sem=send_sems.at[k], recv_sem=recv_sems.at[k],
                                                device_id=(x, y, 1 - c), device_id_type=MESH_ID)

        return [copy(k, c) for k in range(len(pieces))], [copy(k, 1 - c) for k in range(len(pieces))]

    return Plan(bufs, [SDS(b.shape, b.dtype) for b in bufs], {i: i for i in range(len(bufs))}, len(pieces), copies)


def run_plan(name, plan):
    hosted_call(lambda: None, name=name, grid=(1,), in_specs=[], out_specs=[], out_shape=[], operands=[], plans=[plan])
    return plan.results


HBM = pl.BlockSpec(memory_space=pltpu.HBM)
SEMS = pl.BlockSpec(memory_space=pltpu.SEMAPHORE)
SPLIT_PARAMS = dict(has_side_effects=pltpu.SideEffectType.DATAFLOW_SIDE_EFFECTING)


def _plan_buffers(plan):
    in_place = {o: i for i, o in plan.aliases.items()}
    bufs = [pltpu.with_memory_space_constraint(a, pltpu.HBM) for a in plan.operands]
    where = []
    for o, sd in enumerate(plan.out_shape):
        if o in in_place:
            where.append(in_place[o])
        else:
            where.append(len(bufs))
            bufs.append(pltpu.with_memory_space_constraint(lax.empty(sd.shape, sd.dtype), pltpu.HBM))
    return bufs, where


def split_start(name, plans):
    layout = [_plan_buffers(p) for p in plans]
    counts = [len(b) for b, _ in layout]
    n_buf = sum(counts)

    def body(*refs):
        sems, token = refs[n_buf:n_buf + 2 * len(plans)], refs[-1]
        pos = 0
        for k, (p, (_, where)) in enumerate(zip(plans, layout)):
            mine = refs[pos:pos + counts[k]]
            pos += counts[k]
            sends, _ = p.copies(mine[:len(p.operands)], [mine[w] for w in where], sems[2 * k], sems[2 * k + 1])
            for cp in sends:
                cp.start()
        token[...] = jnp.zeros_like(token)

    bufs = [b for bs, _ in layout for b in bs]
    res = pl.pallas_call(
        body, name=name, in_specs=[HBM] * n_buf,
        out_specs=[SEMS] * (2 * len(plans)) + [HBM] * n_buf + [pl.BlockSpec(memory_space=pltpu.VMEM)],
        out_shape=[pltpu.SemaphoreType.DMA((p.n_sems,)) for p in plans for _ in range(2)] + [pltpu.HBM(b.shape, b.dtype) for b in bufs]
        + [SDS((8, 128), F32)],
        input_output_aliases={i: 2 * len(plans) + i for i in range(n_buf)}, compiler_params=pltpu.CompilerParams(**SPLIT_PARAMS))(*bufs)
    pos = 2 * len(plans)
    for k, p in enumerate(plans):
        p.in_flight = (res[2 * k], res[2 * k + 1], list(res[pos:pos + counts[k]]), layout[k][1])
        pos += counts[k]
    return res[-1]


def split_wait(name, plan, after):
    send_sems, recv_sems, bufs, where = plan.in_flight
    n_buf = len(bufs)

    def body(*refs):
        mine = refs[:n_buf]
        sends, recvs = plan.copies(mine[:len(plan.operands)], [mine[w] for w in where], refs[n_buf], refs[n_buf + 1])
        for cp in recvs:
            cp.wait_recv()
        for cp in sends:
            cp.wait_send()

    res = pl.pallas_call(body, name=name, in_specs=[HBM] * n_buf + [SEMS, SEMS, ANY], out_specs=[HBM] * n_buf,
                         out_shape=[pltpu.HBM(b.shape, b.dtype) for b in bufs], input_output_aliases={i: i for i in range(n_buf)},
                         compiler_params=pltpu.CompilerParams(**SPLIT_PARAMS))(*bufs, send_sems, recv_sems, after)
    plan.results = [res[w] for w in where]
    return plan.results


def pair_add(name, g, got, place):
    slots, rows, cols = g.shape
    half = rows // 2
    tr = _row_tile(half)
    nb = half // tr

    def kern(_, g_ref, t_ref, o_ref):
        o_ref[...] = (g_ref[...].astype(F32) + t_ref[...].astype(F32)).astype(o_ref.dtype)

    blk = pl.BlockSpec((None, tr, cols), lambda s, i, p: (s, i, 0))
    grid_spec = pltpu.PrefetchScalarGridSpec(
        num_scalar_prefetch=1, grid=(slots, nb),
        in_specs=[pl.BlockSpec((None, tr, cols), lambda s, i, p: (s, p[1] * nb + i, 0)), blk], out_specs=blk)
    return pl.pallas_call(kern, grid_spec=grid_spec, out_shape=SDS((slots, half, cols), g.dtype), name=name,
                          compiler_params=_cparams())(place, g, got)


def chip_add(name, p, q, place):
    slots, half, cols = p.shape
    tr = _row_tile(half)
    nb = half // tr

    def kern(_, p_ref, q1, q2, q3, o_ref):
        o_ref[...] = p_ref[...].astype(F32) + q1[...].astype(F32) + q2[...].astype(F32) + q3[...].astype(F32)

    def slot(k):
        return pl.BlockSpec((None, tr, cols), lambda i, pr: ((pr[0] + k) % slots, i, 0))

    grid_spec = pltpu.PrefetchScalarGridSpec(
        num_scalar_prefetch=1, grid=(nb,), in_specs=[slot(0), slot(1), slot(2), slot(3)],
        out_specs=pl.BlockSpec((tr, cols), lambda i, pr: (pr[1] * nb + i, 0)))
    return pl.pallas_call(kern, grid_spec=grid_spec, out_shape=SDS((2 * half, cols), F32), name=name,
                          compiler_params=_cparams())(place, p, q, q, q)


def pair_adds(tag, gs, gots, place):
    return [pair_add(f"{tag}_pair_add_{i}", g, got, place) for i, (g, got) in enumerate(zip(gs, gots))]


def chip_adds(tag, pairs, qs, place):
    return [chip_add(f"{tag}_chip_add_{i}", p, q, place) for i, (p, q) in enumerate(zip(pairs, qs))]


def reduce_scatter_chips(tag, gs, place):
    pairs = pair_adds(tag, gs, run_plan(tag + "_pair_exchange", plan_pair_exchange(gs)), place)
    return run_plan(tag + "_pair_join", plan_pair_join(chip_adds(tag, pairs, run_plan(tag + "_chip_scatter", plan_chip_scatter(pairs)), place)))


BIG = [("w_out", (2, 512, 1024)), ("w_mem_kv", (2, 256, 1024)), ("s5_w_in", (1, 1024, 1024)), ("s5_w_glu", (1, 1536, 768)),
       ("mla_w_in", (1, 1024, 848)), ("mla_w_uq", (1, 512, 576)), ("mla_w_ukv", (1, 256, 768))]
SHARDED_SMALL = [("mla_q_lora_norm", (1, 128)), ("mla_kv_lora_norm", (1, 64))]
SMALL = [("ln_gain", (2, 1024)), ("mem_norm", (2, 1024)), ("xq_norm", (2, 128)), ("xk_norm", (2, 128)),
         ("s5_lambda_re", (1, 96, 64)), ("s5_lambda_im", (1, 96, 64)), ("s5_log_step", (1, 96)),
         ("s5_b_re", (1, 96, 64, 16)), ("s5_b_im", (1, 96, 64, 16)), ("s5_c_re", (1, 96, 16, 64)), ("s5_c_im", (1, 96, 16, 64)),
         ("s5_d", (1, 1536)), ("mla_q_nope_norm", (1, 128)), ("mla_k_nope_norm", (1, 128)), ("mla_q_rope_norm", (1, 64)),
         ("mla_k_rope_norm", (1, 64))]
WEIGHT_ORDER = ["ln_gain", "w_out", "mem_norm", "w_mem_kv", "xq_norm", "xk_norm", "s5_w_in", "s5_lambda_re", "s5_lambda_im",
                "s5_log_step", "s5_b_re", "s5_b_im", "s5_c_re", "s5_c_im", "s5_d", "s5_w_glu", "mla_w_in", "mla_q_lora_norm",
                "mla_kv_lora_norm", "mla_w_uq", "mla_w_ukv", "mla_q_nope_norm", "mla_k_nope_norm", "mla_q_rope_norm", "mla_k_rope_norm"]
MINOR_LAST = {"mla_w_in": (0, 2, 1), "mla_w_uq": (0, 2, 1), "s5_b_re": (0, 2, 3, 1), "s5_b_im": (0, 2, 3, 1),
              "s5_c_re": (0, 2, 3, 1), "s5_c_im": (0, 2, 3, 1)}
SMALL_FULL = SMALL + [(n, (1, 4 * s[1])) for n, s in SHARDED_SMALL]
N_SMALL = sum(math.prod(s) for _, s in SMALL_FULL)
SMALL_ROWS, SMALL_LANES = 128, 1024

PAIR_OUT, PAIR_MKV, PAIR_ROWS = 0, 512, 768


def stack_shards(w, dtype):
    pairs = [jnp.concatenate([w["w_out"][l], w["w_mem_kv"][l]], axis=0).astype(dtype) for l in range(2)]
    return ([w["s5_w_in"][0].astype(dtype)], [pairs[0], w["s5_w_glu"][0].astype(dtype)],
            [pairs[1], w["mla_w_ukv"][0].astype(dtype), w["mla_w_in"][0].astype(dtype), w["mla_w_uq"][0].astype(dtype)])


def pair_views(pair):
    return {"w_out": Sharded(pair, "row", PAIR_OUT, 512), "w_mem_kv": Sharded(pair, "row", PAIR_MKV, 256)}


def grad_views():
    pair = SDS((4, PAIR_ROWS, 1024), BF16)
    return {"w_out": Sharded(pair, "row", PAIR_OUT, 512), "w_mem_kv": Sharded(pair, "row", PAIR_MKV, 256),
            "s5_w_in": Sharded(SDS((4, 1024, 1024), BF16), "col", 0, 1024), "s5_w_glu": Sharded(SDS((4, 1536, 768), BF16), "col", 0, 1536),
            "mla_w_ukv": Sharded(SDS((4, 256, 768), BF16), "col", 0, 256)}


def cols_to_shards(full):
    return full.reshape(full.shape[0], 4, full.shape[1] // 4).transpose(1, 0, 2)


def shards_to_cols(arr):
    return arr.transpose(1, 0, 2).reshape(arr.shape[1], 4 * arr.shape[2])


def mla_in_permute(w):
    o1, o2, o3, o4 = QL, QL + KVL, QL + KVL + ROPE, QL + KVL + ROPE + XQW
    return jnp.concatenate([w[:, o4:], w[:, :o1], w[:, o3:o4], w[:, o1:o2], w[:, o2:o3],
                            jnp.zeros((w.shape[0], MLA_IN_P - MLA_IN), w.dtype)], axis=1)


def mla_in_unpermute(d):
    return jnp.concatenate([d[:, 2048:2560], d[:, 3072:3328], d[:, 3328:3392], d[:, 2560:3072], d[:, :2048]], axis=1)


def uq_permute(w):
    w3 = w.reshape(w.shape[0], MH, NOPE + ROPE)
    return jnp.concatenate([w3[:, :, :NOPE].reshape(w.shape[0], MH * NOPE), w3[:, :, NOPE:].reshape(w.shape[0], MH * ROPE)], axis=1)


def uq_unpermute(d):
    dn = d[:, :MH * NOPE].reshape(d.shape[0], MH, NOPE)
    dr = d[:, MH * NOPE:].reshape(d.shape[0], MH, ROPE)
    return jnp.concatenate([dn, dr], axis=2).reshape(d.shape[0], MH * (NOPE + ROPE))


def time_permute(a):
    return a.reshape(SEG, SEG_LEN, a.shape[-1]).transpose(1, 0, 2).reshape(L, a.shape[-1])


def time_unpermute(a):
    return a.reshape(SEG_LEN, SEG, a.shape[-1]).transpose(1, 0, 2).reshape(L, a.shape[-1])


def mem_branch_fwd(tag, mem, mem_norm, w_mem_kv, xk_norm):
    mn = row_fwd(tag + "_mem_rms", fn_rms, ML, ML, [(mem, D, 0)], [mem_norm], [(D, BF16)])[0]
    kv = matmul(tag + "_mem_kv", mn, w_mem_kv, "nn", F32)
    kn = row_fwd(tag + "_mem_knorm", fn_mem_k, ML, ML, [(kv, XQW, 0)], [xk_norm], [(XQW, F32)])[0]
    return mn, kv, kn


def mem_branch_bwd(tag, mem, mem_norm, w_mem_kv, xk_norm, mn, kv, dkn, dv, g_view, g_wide):
    dk, dxk = row_bwd(tag + "_mem_knorm_bwd", fn_mem_k, ML, ML, [(kv, XQW, 0)], [xk_norm], [(dkn, XQW, 0)], [True], [True])
    dkv = jnp.concatenate([dk, dv], axis=1)
    dmn = matmul(tag + "_mem_kv_dx", dkv, w_mem_kv, "nt", F32)
    g_wide = matmul(tag + "_mem_kv_dw", mn, dkv, "tn", out=g_view, into=g_wide)
    dmem_norm = row_bwd(tag + "_mem_rms_bwd", fn_rms, ML, ML, [(mem, D, 0)], [mem_norm], [(dmn, D, 0)], [False], [True])[0]
    return g_wide, dmem_norm, dxk


def mem_attn_fwd(tag, proj, cb, kn, kv, xq_norm):
    return row_fwd(tag + "_mem_attn", fn_mem_attn, L, ROW_TILE, [(proj, XQW, cb)], [kn, kv[:, XQW:], xq_norm], [(XQW, F32)])[0]


def mem_attn_bwd(tag, proj, cb, kn, kv, xq_norm, dmo, dproj):
    place = {"cols": proj.shape[1], "cb": cb, "into": dproj, "dtype": dproj.dtype}
    return row_bwd(tag + "_mem_attn_bwd", fn_mem_attn, L, ROW_TILE, [(proj, XQW, cb)], [kn, kv[:, XQW:], xq_norm], [(dmo, XQW, 0)],
                   [place], [True, True, True])


def device_step(x, mem, positions, target, small, env, hooks=None):
    hooks = hooks or {}

    def plans_for(name):
        return hooks[("plans", name)](env) if ("plans", name) in hooks else ()

    def around(when, name, last=None):
        if (when, name) in hooks:
            hooks[(when, name)](env, last)

    g = {}
    gw = grad_views()
    ln, mem_norm, xq_norm, xk_norm = small["ln_gain"], small["mem_norm"], small["xq_norm"], small["xk_norm"]

    lre, lim = small["s5_lambda_re"][0], small["s5_lambda_im"][0]
    ls = small["s5_log_step"].reshape(SG, 1)
    one = pl.BlockSpec((SG, SP), lambda i: (0, 0))
    col = pl.BlockSpec((SG, 1), lambda i: (0, 0))
    disc_ins = [(lre, one), (lim, one), (ls, col)]
    a_re, a_im, coef_re, coef_im = stage("s5_disc", fn_s5_disc, (1,), disc_ins, [(SDS((SG, SP), F32), one)] * 4)
    b_re, b_im = small["s5_b_re"].reshape(SN, SC), small["s5_b_im"].reshape(SN, SC)
    c_re, c_im = small["s5_c_re"].reshape(PW, SP), small["s5_c_im"].reshape(PW, SP)
    bmat_rows = [(b_re, SC, 0), (b_im, SC, 0), (coef_re.reshape(SN, 1), 1, 0), (coef_im.reshape(SN, 1), 1, 0)]
    wb_re, wb_im = row_fwd("s5_bmat", fn_s5_bmat, SN, 512, bmat_rows, [], [(128, F32)] * 2)
    cmat_rows = [(c_re, SP, 0), (c_im, SP, 0)]
    wc_re, wc_im = row_fwd("s5_cmat", fn_s5_cmat, PW, 128, cmat_rows, [], [(512, F32)] * 2)
    a_re_v, a_im_v = a_re.reshape(1, SN), a_im.reshape(1, SN)
    s5_d = small["s5_d"]

    xp = time_permute(x)
    h0 = row_fwd("l0_rms", fn_rms, L, ROW_TILE, [(xp, D, 0)], [ln[0:1]], [(D, BF16)])[0]
    around("before", "l0_in", wb_re)
    w_in0 = Sharded(env["in0"], "col", 0, 1024)
    proj0 = matmul("l0_in", h0, w_in0, "nn")
    s_re, s_im, g0 = s5_forward(proj0, wb_re, wb_im, wc_re, wc_im, a_re_v, a_im_v, s5_d, plans=plans_for("s5_forward"))
    around("before", "l0_glu", g0)
    w0 = dict(pair_views(env["pair0"]), s5_w_glu=Sharded(env["glu"], "col", 0, 1536))
    z0 = matmul("l0_glu", g0, w0["s5_w_glu"], "nn", plans=plans_for("l0_glu"))
    mn0, kv0, kn0 = mem_branch_fwd("l0", mem, mem_norm[0:1], w0["w_mem_kv"], xk_norm[0:1])
    mo0 = mem_attn_fwd("l0", proj0, 3, kn0, kv0, xq_norm[0:1])
    o0 = row_fwd("l0_merge", fn_merge_glu, L, ROW_TILE, [(z0, 2 * PW, 0), (mo0, XQW, 0), (proj0, BW, 1)], [], [(BW, BF16)])[0]
    around("before", "l0_out", o0)
    x1p = matmul("l0_out", o0, w0["w_out"], "nn", F32, add=xp, plans=plans_for("l0_out"))
    around("after", "l0_out", x1p)
    x1 = time_unpermute(x1p)

    w1 = dict(pair_views(env["pair1"]), mla_w_ukv=Sharded(env["ukv"], "col", 0, 256))
    w_in1, w_uq = env["w_in1"], env["w_uq"]
    h1 = row_fwd("l1_rms", fn_rms, L, ROW_TILE, [(x1, D, 0)], [ln[1:2]], [(D, BF16)])[0]
    proj1 = matmul("l1_in", h1, w_in1, "nn")
    qln, kvln = env["q_lora_norm"].reshape(1, QL), env["kv_lora_norm"].reshape(1, KVL)
    cqn = row_fwd("l1_q_lora_rms", fn_rms, L, ROW_TILE, [(proj1, QL, 4)], [qln], [(QL, BF16)])[0]
    ckvn = row_fwd("l1_kv_lora_rms", fn_rms, L, ROW_TILE, [(proj1, KVL, 12)], [kvln], [(KVL, BF16)])[0]
    q = matmul("l1_uq", cqn, w_uq, "nn")
    kv = matmul("l1_ukv", ckvn, w1["mla_w_ukv"], "nn")
    inv_freq = ROPE_THETA ** (-jnp.arange(ROPE // 2, dtype=F32) / (ROPE // 2))
    ang = positions.astype(F32)[:, None] * inv_freq
    cos2 = jnp.tile(jnp.cos(ang), (1, 4))
    sin_signed = jnp.tile(jnp.concatenate([-jnp.sin(ang), jnp.sin(ang)], axis=1), (1, 2))
    qnn, knn = small["mla_q_nope_norm"], small["mla_k_nope_norm"]
    qrn, krn = jnp.tile(small["mla_q_rope_norm"], (1, 2)), jnp.tile(small["mla_k_rope_norm"], (1, 2))
    tp = 256
    prep_ins = [(q, rspec(tp, MH * (NOPE + ROPE))), (kv, rspec(tp, MH * 256)), (proj1, rspec(tp, 128, 26)),
                (cos2, rspec(tp, 128)), (sin_signed, rspec(tp, 128))] + [(a, cspec((1, 128))) for a in (qnn, knn, qrn, krn)]
    hq_spec = pl.BlockSpec((MH, tp, 256), lambda i: (0, i, 0))
    hv_spec = pl.BlockSpec((MH, tp, 128), lambda i: (0, i, 0))
    qf, kf, vh = stage("l1_mla_prep", fn_mla_prep, (L // tp,), prep_ins,
                       [(SDS((MH, L, 256), BF16), hq_spec), (SDS((MH, L, 256), BF16), hq_spec), (SDS((MH, L, 128), BF16), hv_spec)])
    attn, attn_lse = causal_attn(qf, kf, vh)
    mn1, kv1, kn1 = mem_branch_fwd("l1", mem, mem_norm[1:2], w1["w_mem_kv"], xk_norm[1:2])
    mo1 = mem_attn_fwd("l1", proj1, 5, kn1, kv1, xq_norm[1:2])
    o1 = row_fwd("l1_merge", fn_merge, L, ROW_TILE, [(attn, PW, 0), (mo1, XQW, 0), (proj1, BW, 0)], [], [(BW, BF16)])[0]
    x2 = matmul("l1_out", o1, w1["w_out"], "nn", F32, add=x1)
    dx2, loss = loss_and_grad(x2, target)
    around("after", "loss", loss)

    do1 = matmul("l1_out_dx", dx2, w1["w_out"], "nt")
    g_pair1 = matmul("l1_out_dw", o1, dx2, "tn", out=gw["w_out"])
    dattn, dmo1, dproj1 = row_bwd("l1_merge_bwd", fn_merge, L, ROW_TILE, [(attn, PW, 0), (mo1, XQW, 0), (proj1, BW, 0)], [],
                                  [(do1, BW, 0)], [True, True, {"cols": MLA_IN_P, "cb": 0, "dtype": BF16}], [])
    dproj1, dkn1, dv1, dxqn1 = mem_attn_bwd("l1", proj1, 5, kn1, kv1, xq_norm[1:2], dmo1, dproj1)
    env["g_pair1"], dmem_norm1, dxk1 = mem_branch_bwd("l1", mem, mem_norm[1:2], w1["w_mem_kv"], xk_norm[1:2], mn1, kv1, dkn1, dv1,
                                                      gw["w_mem_kv"], g_pair1)
    dqf, dkf, dvh = causal_attn_bwd(qf, kf, vh, attn, attn_lse, dattn)
    prep_diffs = [("row", SDS((L, MH * (NOPE + ROPE)), BF16), rspec(tp, MH * (NOPE + ROPE))), ("row", SDS((L, MH * 256), BF16), rspec(tp, MH * 256)),
                  ("row", SDS((L, MLA_IN_P), BF16), rspec(tp, 128, 26), {"into": dproj1}), None, None] + [("acc", (0,))] * 4
    dq, dkv, dproj1, dqnn, dknn, dqrn, dkrn = stage_bwd("l1_mla_prep_bwd", fn_mla_prep, (L // tp,), prep_ins,
                                                        [(dqf, hq_spec), (dkf, hq_spec), (dvh, hv_spec)], prep_diffs)
    dcqn = matmul("l1_uq_dx", dq, w_uq, "nt")
    env["dw_uq"] = matmul("l1_uq_dw", cqn, dq, "tn")
    dckvn = matmul("l1_ukv_dx", dkv, w1["mla_w_ukv"], "nt")
    env["g_ukv"] = matmul("l1_ukv_dw", ckvn, dkv, "tn", out=gw["mla_w_ukv"])
    dproj1, dqln = row_bwd("l1_q_lora_rms_bwd", fn_rms, L, ROW_TILE, [(proj1, QL, 4)], [qln], [(dcqn, QL, 0)],
                           [{"cols": MLA_IN_P, "cb": 4, "into": dproj1, "dtype": BF16}], [True])
    dproj1, dkvln = row_bwd("l1_kv_lora_rms_bwd", fn_rms, L, ROW_TILE, [(proj1, KVL, 12)], [kvln], [(dckvn, KVL, 0)],
                            [{"cols": MLA_IN_P, "cb": 12, "into": dproj1, "dtype": BF16}], [True])
    env["dw_in1"] = matmul("l1_in_dw", h1, dproj1, "tn")
    dh1 = matmul("l1_in_dx", dproj1, w_in1, "nt", plans=plans_for("l1_in_dx"))
    around("after", "l1_in_dx", dh1)
    dx1, dln1 = row_bwd("l1_rms_bwd", fn_rms, L, ROW_TILE, [(x1, D, 0)], [ln[1:2]], [(dh1, D, 0)], [{"add": (dx2, D, 0)}], [True])
    dx1p = time_permute(dx1)

    do0 = matmul("l0_out_dx", dx1p, w0["w_out"], "nt", plans=plans_for("l0_out_dx"))
    g_pair0 = matmul("l0_out_dw", o0, dx1p, "tn", out=gw["w_out"])
    dz0, dmo0, dproj0 = row_bwd("l0_merge_bwd", fn_merge_glu, L, ROW_TILE, [(z0, 2 * PW, 0), (mo0, XQW, 0), (proj0, BW, 1)], [],
                                [(do0, BW, 0)], [{"dtype": BF16}, True, {"cols": 2 * BW, "cb": 1, "dtype": BF16}], [])
    dproj0, dkn0, dv0, dxqn0 = mem_attn_bwd("l0", proj0, 3, kn0, kv0, xq_norm[0:1], dmo0, dproj0)
    env["g_pair0"], dmem_norm0, dxk0 = mem_branch_bwd("l0", mem, mem_norm[0:1], w0["w_mem_kv"], xk_norm[0:1], mn0, kv0, dkn0, dv0,
                                                      gw["w_mem_kv"], g_pair0)
    env["g_glu"] = matmul("l0_glu_dw", g0, dz0, "tn", out=gw["s5_w_glu"], plans=plans_for("l0_glu_dw"))
    dg0 = matmul("l0_glu_dx", dz0, w0["s5_w_glu"], "nt", plans=plans_for("l0_glu_dx"))
    around("before", "s5_backward", dg0)
    dproj0, dd, dwc_re, dwc_im, dwb_re, dwb_im, da_re, da_im = s5_backward(dg0, proj0, s_re, s_im, wb_re, wb_im, wc_re, wc_im,
                                                                           a_re_v, a_im_v, s5_d, dproj0, plans=plans_for("s5_backward"))
    around("after", "s5_backward", dd)
    env["g_in0"] = matmul("l0_in_dw", h0, dproj0, "tn", out=gw["s5_w_in"], plans=plans_for("l0_in_dw"))
    dh0 = matmul("l0_in_dx", dproj0, w_in0, "nt", plans=plans_for("l0_in_dx"))
    around("after", "l0_in_dx", dh0)
    dxp, dln0 = row_bwd("l0_rms_bwd", fn_rms, L, ROW_TILE, [(xp, D, 0)], [ln[0:1]], [(dh0, D, 0)], [{"add": (dx1p, D, 0)}], [True])
    grad_x = time_unpermute(dxp)

    db_re, db_im, dcoef_re, dcoef_im = row_bwd("s5_bmat_bwd", fn_s5_bmat, SN, 512, bmat_rows, [], [(dwb_re, 128, 0), (dwb_im, 128, 0)],
                                               [True] * 4, [], plans=plans_for("s5_bmat_bwd"))
    dc_re, dc_im = row_bwd("s5_cmat_bwd", fn_s5_cmat, PW, 128, cmat_rows, [], [(dwc_re, 512, 0), (dwc_im, 512, 0)], [True] * 2, [],
                           plans=plans_for("s5_cmat_bwd"))
    disc_cts = [(da_re.reshape(SG, SP), one), (da_im.reshape(SG, SP), one), (dcoef_re.reshape(SG, SP), one), (dcoef_im.reshape(SG, SP), one)]
    dlre, dlim, dls = stage_bwd("s5_disc_bwd", fn_s5_disc, (1,), disc_ins, disc_cts, [("acc", (0,))] * 3)

    g["ln_gain"] = jnp.concatenate([dln0, dln1], axis=0)
    g["mem_norm"] = jnp.concatenate([dmem_norm0, dmem_norm1], axis=0)
    g["xq_norm"] = jnp.concatenate([dxqn0, dxqn1], axis=0)
    g["xk_norm"] = jnp.concatenate([dxk0, dxk1], axis=0)
    g["s5_lambda_re"], g["s5_lambda_im"], g["s5_log_step"] = dlre, dlim, dls
    g["s5_b_re"], g["s5_b_im"], g["s5_c_re"], g["s5_c_im"] = db_re, db_im, dc_re, dc_im
    g["s5_d"] = dd
    g["mla_q_lora_norm"], g["mla_kv_lora_norm"] = dqln, dkvln
    g["mla_q_nope_norm"], g["mla_k_nope_norm"] = dqnn, dknn
    g["mla_q_rope_norm"] = dqrn[:, :ROPE] + dqrn[:, ROPE:]
    g["mla_k_rope_norm"] = dkrn[:, :ROPE] + dkrn[:, ROPE:]
    return loss, grad_x, g


def kernel(x, mem, positions, ln_gain, w_out, mem_norm, w_mem_kv, xq_norm, xk_norm, s5_w_in, s5_lambda_re, s5_lambda_im, s5_log_step, s5_b_re, s5_b_im, s5_c_re, s5_c_im, s5_d, s5_w_glu, mla_w_in, mla_q_lora_norm, mla_kv_lora_norm, mla_w_uq, mla_w_ukv, mla_q_nope_norm, mla_k_nope_norm, mla_q_rope_norm, mla_k_rope_norm, loss_target, m_ln_gain, m_w_out, m_mem_norm, m_w_mem_kv, m_xq_norm, m_xk_norm, m_s5_w_in, m_s5_lambda_re, m_s5_lambda_im, m_s5_log_step, m_s5_b_re, m_s5_b_im, m_s5_c_re, m_s5_c_im, m_s5_d, m_s5_w_glu, m_mla_w_in, m_mla_q_lora_norm, m_mla_kv_lora_norm, m_mla_w_uq, m_mla_w_ukv, m_mla_q_nope_norm, m_mla_k_nope_norm, m_mla_q_rope_norm, m_mla_k_rope_norm, v_ln_gain, v_w_out, v_mem_norm, v_w_mem_kv, v_xq_norm, v_xk_norm, v_s5_w_in, v_s5_lambda_re, v_s5_lambda_im, v_s5_log_step, v_s5_b_re, v_s5_b_im, v_s5_c_re, v_s5_c_im, v_s5_d, v_s5_w_glu, v_mla_w_in, v_mla_q_lora_norm, v_mla_kv_lora_norm, v_mla_w_uq, v_mla_w_ukv, v_mla_q_nope_norm, v_mla_k_nope_norm, v_mla_q_rope_norm, v_mla_k_rope_norm):
    args = dict(locals())
    wts = {n: args[n] for n in WEIGHT_ORDER}
    mom = {n: args["m_" + n] for n in WEIGHT_ORDER}
    var = {n: args["v_" + n] for n in WEIGHT_ORDER}

    chip = 2 * lax.axis_index("x") + lax.axis_index("y")
    place = jnp.stack([chip, lax.axis_index("c")]).astype(jnp.int32)

    def own_slot(gathered, shards):
        return [lax.dynamic_update_slice(g, s[None], (chip, 0, 0)) for g, s in zip(gathered, shards)]

    groups = list(stack_shards(wts, BF16))
    groups[2].append(jnp.concatenate([mla_q_lora_norm, jnp.pad(mla_kv_lora_norm, ((0, 0), (0, 64))), jnp.zeros((14, 128), F32)], axis=0))
    over_ici = [plan_gather_ici(shards) for shards in groups]
    _SCHEDULE_BEHIND.clear()
    schedule_behind(split_start("gather_start", over_ici))
    env, hooks, passed_on = {}, {}, {}

    def arrived(k, after, pass_now):
        passing = plan_gather_pass(split_wait(f"gather_wait_{k}", over_ici[k], after))
        passed_on[k] = passing
        return own_slot(run_plan(f"gather_pass_{k}", passing), groups[k]) if pass_now else None

    def need_in0(env, last):
        env["in0"], = arrived(0, last, True)

    def need_layer0(env, last):
        env["pair0"], env["glu"] = arrived(1, last, True)

    def need_layer1(env, last):
        arrived(2, last, False)

    def layer1_weights(env, last):
        pair1, ukv, in1, uq, norms = own_slot(passed_on[2].results, groups[2])
        env.update(pair1=pair1, ukv=ukv, w_in1=mla_in_permute(shards_to_cols(in1)), w_uq=uq_permute(shards_to_cols(uq)),
                   q_lora_norm=norms[:, 0, :], kv_lora_norm=norms[:, 1, :64])

    hooks["before", "l0_in"], hooks["before", "l0_glu"], hooks["before", "l0_out"] = need_in0, need_layer0, need_layer1
    hooks["plans", "l0_out"], hooks["after", "l0_out"] = (lambda env: [passed_on[2]]), layer1_weights

    rs = {}

    def swap(k, gs):
        rs[k, "g"], rs[k, "swap"] = gs, plan_pair_exchange(gs)
        return rs[k, "swap"]

    def start_scatter(k):
        rs[k, "pairs"] = pair_adds(f"rs{k}", rs[k, "g"], rs[k, "swap"].results, place)
        rs[k, "scatter"] = plan_chip_scatter(rs[k, "pairs"])
        schedule_behind(split_start(f"rs{k}_scatter_start", [rs[k, "scatter"]]))

    def join(k, after):
        rs[k, "join"] = plan_pair_join(chip_adds(f"rs{k}", rs[k, "pairs"], split_wait(f"rs{k}_scatter_wait", rs[k, "scatter"], after), place))
        return rs[k, "join"]

    hooks["plans", "l1_in_dx"] = lambda env: [swap(1, [env["g_pair1"], env["g_ukv"], cols_to_shards(mla_in_unpermute(env["dw_in1"])).astype(BF16),
                                                        cols_to_shards(uq_unpermute(env["dw_uq"])).astype(BF16)])]
    hooks["after", "l1_in_dx"] = lambda env, last: start_scatter(1)
    hooks["plans", "l0_glu_dx"] = lambda env: [swap(0, [env["g_pair0"], env["g_glu"]])]

    def before_s5_backward(env, last):
        start_scatter(0)
        env["join1"] = join(1, last)

    hooks["before", "s5_backward"] = before_s5_backward
    hooks["plans", "s5_backward"] = lambda env: [env["join1"]]
    hooks["after", "s5_backward"] = lambda env, last: env.update(join0=join(0, last))
    hooks["plans", "l0_in_dx"] = lambda env: [swap(2, [env["g_in0"]]), env["join0"]]
    hooks["after", "l0_in_dx"] = lambda env, last: start_scatter(2)

    def total_loss(env, local):
        env["loss"] = lax.psum(local[0, 0], MESH_AXES)
        schedule_behind(env["loss"].reshape(1, 1))

    hooks["after", "loss"] = total_loss
    small = {n: wts[n] for n, _ in SMALL}
    loss, grad_x, g = device_step(x[0], mem[0], positions[0], loss_target[0], small, env, hooks)
    loss = env["loss"]
    r_in0, = run_plan("rs2_pair_join", join(2, g["s5_log_step"]))
    (r_pair1, r_ukv, r_in1, r_uq), (r_pair0, r_glu) = (rs[k, "join"].results for k in (1, 0))

    small_flat = jnp.concatenate([g[n].reshape(-1) for n, _ in SMALL_FULL])
    g_small = jnp.pad(small_flat, (0, 4 * SMALL_ROWS * SMALL_LANES - N_SMALL)).astype(BF16).reshape(4, SMALL_ROWS, SMALL_LANES)
    r_small = reduce_scatter_chips("rs3", [g_small], place)[0]
    small_all = own_slot(all_gather_chips("gather_small_grads", [r_small]), [r_small])[0].reshape(-1)[:N_SMALL]

    grads = {"w_out": jnp.stack([r_pair0[:PAIR_MKV], r_pair1[:PAIR_MKV]]), "w_mem_kv": jnp.stack([r_pair0[PAIR_MKV:], r_pair1[PAIR_MKV:]]),
             "s5_w_in": r_in0[None], "s5_w_glu": r_glu[None], "mla_w_ukv": r_ukv[None], "mla_w_in": r_in1[None], "mla_w_uq": r_uq[None]}
    off = 0
    for n, s in SMALL_FULL:
        grads[n] = small_all[off:off + math.prod(s)].reshape(s)
        off += math.prod(s)
    for n, s in SHARDED_SMALL:
        grads[n] = lax.dynamic_slice(grads[n], (0, chip * s[1]), s)

    delta, new_m, new_v = {}, {}, {}
    for n, s in BIG + [(n, s) for n, s in SMALL if len(s) == 4]:
        perm = MINOR_LAST.get(n, tuple(range(len(s))))
        turned = tuple(s[p] for p in perm)
        view = lambda a: jnp.transpose(a, perm).reshape(-1, turned[-1])
        res = adamw("adamw_" + n, view(wts[n]), view(grads[n]), view(mom[n]), view(var[n]))
        delta[n], new_m[n], new_v[n] = (jnp.transpose(r.reshape(turned), tuple(perm.index(i) for i in range(len(s)))) for r in res)
    small_names = [n for n, s in SMALL if len(s) < 4] + [n for n, _ in SHARDED_SMALL]
    n_own = sum(wts[n].size for n in small_names)
    rows_own = -(-n_own // (8 * 128)) * 8

    def pack_small(d):
        flat = jnp.concatenate([d[n].reshape(-1) for n in small_names])
        return jnp.pad(flat, (0, rows_own * 128 - n_own), constant_values=1.0).reshape(rows_own, 128)

    res = adamw("adamw_small", pack_small(wts), pack_small(grads), pack_small(mom), pack_small(var))
    off = 0
    for n in small_names:
        size = wts[n].size
        delta[n], new_m[n], new_v[n] = (r.reshape(-1)[off:off + size].reshape(wts[n].shape) for r in res)
        off += size

    return (loss, grad_x[None], *[grads[n] for n in WEIGHT_ORDER], *[delta[n] for n in WEIGHT_ORDER],
            *[new_m[n] for n in WEIGHT_ORDER], *[new_v[n] for n in WEIGHT_ORDER])
```

```python
import functools
import math

import jax
import jax.numpy as jnp
from jax import lax
from jax.experimental import pallas as pl
from jax.experimental.pallas import tpu as pltpu

F32, BF16 = jnp.float32, jnp.bfloat16
SDS = jax.ShapeDtypeStruct

D = 1024
L = 2048
ML = 256
BW = 2 * D
XQW = BW // 4
PW = BW - XQW
XH, XHD = 4, 128
SG, SC, SP = 96, 16, 64
SN = SG * SP
NOPE, ROPE, VD = 128, 64, 128
MH = 12
QL, KVL = 512, 256
EPS = 1e-6
ROPE_THETA = 10000.0
MLA_IN = QL + KVL + ROPE + XQW + BW
MLA_IN_P = 3456
ADAM_LR, ADAM_B1, ADAM_B2, ADAM_EPS, ADAM_WD, ADAM_STEP = 0.001, 0.9, 0.999, 1e-08, 0.01, 10

VMEM_LIMIT = 48 * 2**20
ROW_TILE = 512
SEG = 8
SEG_LEN = L // SEG
MESH_AXES = ("x", "y", "c")


def _cparams():
    return pltpu.CompilerParams(vmem_limit_bytes=VMEM_LIMIT)


def _dg(a, b, ca, cb):
    return lax.dot_general(a.astype(BF16), b.astype(BF16), (((ca,), (cb,)), ((), ())), preferred_element_type=F32)


@jax.custom_vjp
def mm_nn(a, b):
    return _dg(a, b, 1, 0)


mm_nn.defvjp(lambda a, b: (_dg(a, b, 1, 0), (a, b)), lambda res, g: (_dg(g, res[1], 1, 1), _dg(res[0], g, 0, 0)))


@jax.custom_vjp
def mm_nt(a, b):
    return _dg(a, b, 1, 1)


mm_nt.defvjp(lambda a, b: (_dg(a, b, 1, 1), (a, b)), lambda res, g: (_dg(g, res[1], 1, 0), _dg(g, res[0], 0, 0)))


@functools.partial(jax.custom_vjp, nondiff_argnums=(1,))
def lane_roll(x, shift):
    return pltpu.roll(x, shift, 1)


lane_roll.defvjp(lambda x, shift: (pltpu.roll(x, shift, 1), None),
                 lambda shift, _, g: (pltpu.roll(g, (128 - shift) % 128, 1),))


def rms(x, g):
    return x * lax.rsqrt(jnp.mean(x * x, axis=-1, keepdims=True) + EPS) * g


@jax.custom_vjp
def softmax_rows(s):
    e = jnp.exp(s - jnp.max(s, axis=-1, keepdims=True))
    return e / jnp.sum(e, axis=-1, keepdims=True)


def _softmax_rows_fwd(s):
    p = softmax_rows(s)
    return p, p


def _softmax_rows_bwd(p, g):
    return (p * (g - jnp.sum(g * p, axis=-1, keepdims=True)),)


softmax_rows.defvjp(_softmax_rows_fwd, _softmax_rows_bwd)


def silu(x):
    return x * jax.nn.sigmoid(x)


ANY = pl.BlockSpec(memory_space=pl.ANY)
MESH_ID = pl.DeviceIdType.MESH


def _dma_sems(n):
    return [pltpu.SemaphoreType.DMA((n,)), pltpu.SemaphoreType.DMA((n,))]


class Plan:
    def __init__(self, operands, out_shape, aliases, n_sems, copies):
        self.operands, self.out_shape, self.aliases, self.n_sems, self.copies = list(operands), list(out_shape), aliases, n_sems, copies
        self.results = None


_SCHEDULE_BEHIND = []


def schedule_behind(token):
    _SCHEDULE_BEHIND.append(token)


def hosted_call(kern, *, name, grid, in_specs, out_specs, out_shape, operands, scratch_shapes=(), aliases=None, cparams=None, plans=(), deps=()):
    n_in, n_out, n_scr = len(in_specs), len(out_specs), len(scratch_shapes)
    p_in, p_out = [len(p.operands) for p in plans], [len(p.out_shape) for p in plans]
    deps = tuple(deps) + tuple(_SCHEDULE_BEHIND)
    _SCHEDULE_BEHIND.clear()
    all_aliases = dict(aliases or {})
    in_off, out_off = n_in, n_out
    for p, ni, no in zip(plans, p_in, p_out):
        all_aliases.update({in_off + i: out_off + o for i, o in p.aliases.items()})
        in_off, out_off = in_off + ni, out_off + no

    def body(*refs):
        pos, pins, pouts = n_in, [], []
        for ni in p_in:
            pins.append(refs[pos:pos + ni])
            pos += ni
        pos += len(deps)
        main_out = refs[pos:pos + n_out]
        pos += n_out
        for no in p_out:
            pouts.append(refs[pos:pos + no])
            pos += no
        main_scr = refs[pos:pos + n_scr]
        pos += n_scr
        if plans:
            ids = [pl.program_id(ax) for ax in range(len(grid))]
            first = functools.reduce(jnp.logical_and, [i == 0 for i in ids])
            last = functools.reduce(jnp.logical_and, [i == g - 1 for i, g in zip(ids, grid)])
            copies = [p.copies(pins[k], pouts[k], refs[pos + 2 * k], refs[pos + 2 * k + 1]) for k, p in enumerate(plans)]

            @pl.when(first)
            def _():
                for sends, _ in copies:
                    for cp in sends:
                        cp.start()

        kern(*refs[:n_in], *main_out, *main_scr)
        if plans:
            @pl.when(last)
            def _():
                for sends, recvs in copies:
                    for cp in recvs:
                        cp.wait_recv()
                    for cp in sends:
                        cp.wait_send()

    res = pl.pallas_call(body, grid=grid, in_specs=list(in_specs) + [ANY] * (sum(p_in) + len(deps)),
                         out_specs=list(out_specs) + [ANY] * sum(p_out), out_shape=list(out_shape) + [s for p in plans for s in p.out_shape],
                         scratch_shapes=list(scratch_shapes) + [s for p in plans for s in _dma_sems(p.n_sems)],
                         input_output_aliases=all_aliases, name=name, compiler_params=cparams or _cparams())(
        *operands, *[a for p in plans for a in p.operands], *deps)
    pos = n_out
    for p, no in zip(plans, p_out):
        p.results = list(res[pos:pos + no])
        pos += no
    return list(res[:n_out])


def _wide(v):
    return v.astype(F32) if v.dtype == BF16 else v


def stage(name, fn, grid, ins, outs):
    n_in = len(ins)

    def kern(*refs):
        res = fn(*[_wide(r[...]) for r in refs[:n_in]])
        for r, v in zip(refs[n_in:], res):
            r[...] = v.astype(r.dtype)

    return hosted_call(kern, name=name, grid=grid, in_specs=[s for _, s in ins], out_specs=[s for _, s in outs],
                       out_shape=[sd for sd, _ in outs], operands=[a for a, _ in ins])


def stage_bwd(name, fn, grid, ins, cts, diffs, plans=()):
    n_in, n_ct = len(ins), len(cts)
    didx = [i for i, d in enumerate(diffs) if d is not None]
    opts = {i: (diffs[i][3] if len(diffs[i]) > 3 else {}) for i in didx if diffs[i][0] == "row"}
    adds = [(i, opts[i]["add"]) for i in opts if "add" in opts[i]]
    intos = [(i, opts[i]["into"]) for i in opts if "into" in opts[i]]
    n_add, n_into = len(adds), len(intos)
    add_pos = {i: n_in + n_ct + k for k, (i, _) in enumerate(adds)}
    n_extra = n_in + n_ct + n_add + n_into

    def kern(*refs):
        vals = [_wide(r[...]) for r in refs[:n_in]]

        def f(*dv):
            full = list(vals)
            for i, v in zip(didx, dv):
                full[i] = v
            return fn(*full)

        _, vjp = jax.vjp(f, *[vals[i].astype(F32) for i in didx])
        gs = vjp(tuple(c[...].astype(F32) for c in refs[n_in:n_in + n_ct]))
        for o_ref, i, g in zip(refs[n_extra:], didx, gs):
            if diffs[i][0] == "row":
                if i in add_pos:
                    g = g + refs[add_pos[i]][...].astype(F32)
                o_ref[...] = g.astype(o_ref.dtype)
            else:
                first = functools.reduce(jnp.logical_and, [pl.program_id(ax) == 0 for ax in diffs[i][1]])

                @pl.when(first)
                def _():
                    o_ref[...] = g

                @pl.when(jnp.logical_not(first))
                def _():
                    o_ref[...] += g

    out_shape, out_specs = [], []
    for i in didx:
        if diffs[i][0] == "row":
            out_shape.append(diffs[i][1])
            out_specs.append(diffs[i][2])
        else:
            out_shape.append(SDS(ins[i][0].shape, F32))
            out_specs.append(ins[i][1])
    aliases = {n_in + n_ct + n_add + k: didx.index(i) for k, (i, _) in enumerate(intos)}
    in_specs = [s for _, s in ins] + [s for _, s in cts] + [s for _, (_, s) in adds] + [ANY] * n_into
    operands = [a for a, _ in ins] + [a for a, _ in cts] + [a for _, (a, _) in adds] + [a for _, a in intos]
    return hosted_call(kern, name=name, grid=grid, in_specs=in_specs, out_specs=out_specs, out_shape=out_shape, operands=operands,
                       aliases=aliases, plans=plans)


def rspec(tl, w, cb=0):
    return pl.BlockSpec((tl, w), lambda i: (i, cb))


def cspec(shape):
    return pl.BlockSpec(shape, lambda i: (0,) * len(shape))


def row_fwd(name, fn, rows, tl, row_ins, consts, outs):
    ins = [(a, rspec(tl, w, cb)) for a, w, cb in row_ins] + [(a, cspec(a.shape)) for a in consts]
    return stage(name, fn, (rows // tl,), ins, [(SDS((rows, w), dt), rspec(tl, w)) for w, dt in outs])


def row_bwd(name, fn, rows, tl, row_ins, consts, cts, row_diff, const_diff, plans=()):
    ins = [(a, rspec(tl, w, cb)) for a, w, cb in row_ins] + [(a, cspec(a.shape)) for a in consts]
    diffs = []
    for (a, w, cb), d in zip(row_ins, row_diff):
        if not d:
            diffs.append(None)
            continue
        d = d if isinstance(d, dict) else {}
        opts = {}
        if "add" in d:
            opts["add"] = (d["add"][0], rspec(tl, d["add"][1], d["add"][2]))
        if d.get("into") is not None:
            opts["into"] = d["into"]
        diffs.append(("row", SDS((rows, d.get("cols", w)), d.get("dtype", F32)), rspec(tl, w, d.get("cb", 0)), opts))
    diffs += [("acc", (0,)) if d else None for d in const_diff]
    return stage_bwd(name, fn, (rows // tl,), ins, [(a, rspec(tl, w, cb)) for a, w, cb in cts], diffs, plans=plans)


MATMUL_VMEM = 36 * 2**20
ADAMW_VMEM = 28 * 2**20


class Sharded:
    def __init__(self, arr, kind, roff, rows):
        self.arr, self.kind, self.roff, self.rows, self.n = arr, kind, roff, rows, arr.shape[2]
        self.shape = (rows, 4 * self.n) if kind == "col" else (4 * rows, self.n)

    def fits(self, t0, t1):
        return self.roff % t0 == 0 and self.rows % t0 == 0 and self.n % t1 == 0

    def spec(self, t0, t1, bidx):
        assert self.fits(t0, t1), (self.kind, self.roff, self.rows, self.n, t0, t1)
        r0 = self.roff // t0
        if self.kind == "col":
            per = self.n // t1
            return pl.BlockSpec((None, t0, t1), lambda *g: (bidx(*g)[1] // per, r0 + bidx(*g)[0], bidx(*g)[1] % per))
        per = self.rows // t0
        return pl.BlockSpec((None, t0, t1), lambda *g: (bidx(*g)[0] // per, r0 + bidx(*g)[0] % per, bidx(*g)[1]))


def matmul(name, a, b, mode, out_dtype=BF16, add=None, out=None, into=None, plans=()):
    if mode == "tn":
        k_dim, m = a.shape
    else:
        m, k_dim = a.shape
    n = b.shape[0] if mode == "nt" else b.shape[1]
    b_fit = b.fits if isinstance(b, Sharded) else (lambda t0, t1: True)
    o_fit = out.fits if out is not None else (lambda t0, t1: True)
    a_bytes, b_bytes = jnp.dtype(a.dtype).itemsize, jnp.dtype(b.arr.dtype if isinstance(b, Sharded) else b.dtype).itemsize
    o_bytes = jnp.dtype(out_dtype if out is None else out.arr.dtype).itemsize

    def vmem(tm, tn, tk):
        return 2 * (tm * tk * a_bytes + tk * tn * b_bytes + tm * tn * (o_bytes + (4 if add is not None else 0))) + 4 * tm * tn * (1 + (tk < k_dim))

    tiles = [(tm, tn, tk) for tm in (2048, 1024, 512, 256, 128) for tn in (1024, 768, 512, 384, 256, 128)
             for tk in sorted({k_dim, 1024, 768, 512, 384, 256, 128})
             if m % tm == 0 and n % tn == 0 and k_dim % tk == 0 and (b_fit(tn, tk) if mode == "nt" else b_fit(tk, tn)) and o_fit(tm, tn)
             and vmem(tm, tn, tk) <= MATMUL_VMEM]
    tm, tn, tk = max(tiles, key=lambda t: (t[2] == k_dim, t[0] * t[1] * t[2], t[0] * t[1]))
    nk = k_dim // tk
    a_spec = pl.BlockSpec((tk, tm), lambda i, j, k: (k, i)) if mode == "tn" else pl.BlockSpec((tm, tk), lambda i, j, k: (i, k))
    if isinstance(b, Sharded):
        b_spec = b.spec(tn, tk, lambda i, j, k: (j, k)) if mode == "nt" else b.spec(tk, tn, lambda i, j, k: (k, j))
        b = b.arr
    else:
        b_spec = pl.BlockSpec((tn, tk), lambda i, j, k: (j, k)) if mode == "nt" else pl.BlockSpec((tk, tn), lambda i, j, k: (k, j))
    o_spec = pl.BlockSpec((tm, tn), lambda i, j, k: (i, j))
    out_spec, out_shape = (o_spec, SDS((m, n), out_dtype)) if out is None else (out.spec(tm, tn, lambda i, j, k: (i, j)), out.arr)
    ca, cb = {"nn": (1, 0), "nt": (1, 1), "tn": (0, 0)}[mode]
    n_in = 2 + (add is not None)

    def finish(refs, o_ref, r):
        if add is not None:
            r = r + refs[2][...]
        o_ref[...] = r.astype(o_ref.dtype)

    def kern_whole(*refs):
        finish(refs, refs[-1], _dg(refs[0][...], refs[1][...], ca, cb))

    def kern_cut(*refs):
        o_ref, acc = refs[-2], refs[-1]
        k = pl.program_id(2)

        @pl.when(k == 0)
        def _():
            acc[...] = jnp.zeros_like(acc)

        acc[...] += _dg(refs[0][...], refs[1][...], ca, cb)

        @pl.when(k == nk - 1)
        def _():
            finish(refs, o_ref, acc[...])

    ins, specs = [a, b], [a_spec, b_spec]
    if add is not None:
        ins.append(add)
        specs.append(o_spec)
    if into is not None:
        ins.append(into)
        specs.append(ANY)
    return hosted_call(kern_whole if nk == 1 else kern_cut, name=name, grid=(m // tm, n // tn, nk), in_specs=specs, out_specs=[out_spec],
                       out_shape=[out_shape], operands=ins, scratch_shapes=[] if nk == 1 else [pltpu.VMEM((tm, tn), F32)],
                       aliases={} if into is None else {n_in: 0}, plans=plans)[0]


SCAN_UNROLL = 8


def _cmul(ar, ai, br, bi):
    return ar * br - ai * bi, ar * bi + ai * br


def _sub_shift(x, down):
    row = lax.broadcasted_iota(jnp.int32, x.shape, 0)
    if down:
        return jnp.where(row == 0, 0.0, pltpu.roll(x, 1, 0))
    return jnp.where(row == SEG - 1, 0.0, pltpu.roll(x, SEG - 1, 0))


def _pow_seg_len(ar, ai):
    for _ in range(int(math.log2(SEG_LEN))):
        ar, ai = _cmul(ar, ai, ar, ai)
    return ar, ai


def _scan_in_place(sr, si, a_re, a_im):
    lanes = sr.shape[1]
    ar = jnp.broadcast_to(a_re, (SEG, lanes))
    ai = jnp.broadcast_to(a_im, (SEG, lanes))
    zero = jnp.zeros((SEG, lanes), F32)

    def local(i, carry):
        rows = pl.ds(pl.multiple_of(i * SEG, SEG), SEG)
        mr, mi = _cmul(ar, ai, carry[0], carry[1])
        nr, ni = mr + sr[rows, :], mi + si[rows, :]
        sr[rows, :] = nr
        si[rows, :] = ni
        return nr, ni

    fr, fi = lax.fori_loop(0, SEG_LEN, local, (zero, zero), unroll=SCAN_UNROLL)
    pr, pi = _pow_seg_len(ar, ai)
    ir, ii = zero, zero
    for _ in range(SEG - 1):
        mr, mi = _cmul(pr, pi, ir, ii)
        ir, ii = _sub_shift(mr + fr, True), _sub_shift(mi + fi, True)

    def carry_in(i, pw):
        rows = pl.ds(pl.multiple_of(i * SEG, SEG), SEG)
        cr, ci = _cmul(pw[0], pw[1], ir, ii)
        sr[rows, :] += cr
        si[rows, :] += ci
        return _cmul(pw[0], pw[1], ar, ai)

    lax.fori_loop(0, SEG_LEN, carry_in, (ar, ai), unroll=SCAN_UNROLL)


S5_LANES = 8 * SP
S5_BLOCKS = SN // S5_LANES


def _s5_specs():
    u_spec = pl.BlockSpec((L, 8 * SC), lambda j: (0, j))
    s_spec = pl.BlockSpec((L, S5_LANES), lambda j: (0, j))
    wb_spec = pl.BlockSpec((S5_LANES, 8 * SC), lambda j: (j, 0))
    wc_spec = pl.BlockSpec((8 * SC, S5_LANES), lambda j: (j, 0))
    a_spec = pl.BlockSpec((1, S5_LANES), lambda j: (0, j))
    d_spec = pl.BlockSpec((1, 8 * SC), lambda j: (0, j))
    return u_spec, s_spec, wb_spec, wc_spec, a_spec, d_spec


def s5_forward(proj, wb_re, wb_im, wc_re, wc_im, a_re, a_im, d, plans=()):
    def kern(u_ref, wbr, wbi, wcr, wci, ar, ai, d_ref, sr_out, si_out, g_ref, sr, si):
        u = _wide(u_ref[...])
        sr[...], si[...] = fn_s5_bu(u, wbr[...], wbi[...])
        _scan_in_place(sr, si, ar[...], ai[...])
        g_ref[...] = fn_s5_out(sr[...], si[...], u, d_ref[...], wcr[...], wci[...])[0].astype(g_ref.dtype)
        sr_out[...] = sr[...].astype(sr_out.dtype)
        si_out[...] = si[...].astype(si_out.dtype)

    u_spec, s_spec, wb_spec, wc_spec, a_spec, d_spec = _s5_specs()
    return hosted_call(kern, name="s5_forward", grid=(S5_BLOCKS,), in_specs=[u_spec, wb_spec, wb_spec, wc_spec, wc_spec, a_spec, a_spec, d_spec],
                       out_specs=[s_spec, s_spec, u_spec], out_shape=[SDS((L, SN), BF16)] * 2 + [SDS((L, PW), BF16)],
                       operands=[proj, wb_re, wb_im, wc_re, wc_im, a_re, a_im, d], scratch_shapes=[pltpu.VMEM((L, S5_LANES), F32)] * 2,
                       plans=plans)


def _adjoint_scan_in_place(lr, li, sr, si, a_re, a_im):
    lanes = lr.shape[1]
    ar = jnp.broadcast_to(a_re, (SEG, lanes))
    ai = -jnp.broadcast_to(a_im, (SEG, lanes))
    zero = jnp.zeros((SEG, lanes), F32)

    def local(k, carry):
        i = SEG_LEN - 1 - k
        rows = pl.ds(pl.multiple_of(i * SEG, SEG), SEG)
        mr, mi = _cmul(ar, ai, carry[0], carry[1])
        nr, ni = mr + lr[rows, :], mi + li[rows, :]
        lr[rows, :] = nr
        li[rows, :] = ni
        return nr, ni

    fr, fi = lax.fori_loop(0, SEG_LEN, local, (zero, zero), unroll=SCAN_UNROLL)
    pr, pi = _pow_seg_len(ar, ai)
    ir, ii = zero, zero
    for _ in range(SEG - 1):
        mr, mi = _cmul(pr, pi, ir, ii)
        ir, ii = _sub_shift(mr + fr, False), _sub_shift(mi + fi, False)

    def fix(rows, pw):
        cr, ci = _cmul(pw[0], pw[1], ir, ii)
        tr, ti = lr[rows, :] + cr, li[rows, :] + ci
        lr[rows, :] = tr
        li[rows, :] = ti
        return tr, ti

    def grad_a(tr, ti, spr, spi, acc):
        return acc[0] + tr * spr + ti * spi, acc[1] + ti * spr - tr * spi

    def carry_in(k, c):
        i = SEG_LEN - 1 - k
        rows = pl.ds(pl.multiple_of(i * SEG, SEG), SEG)
        prev = pl.ds(pl.multiple_of((i - 1) * SEG, SEG), SEG)
        tr, ti = fix(rows, (c[0], c[1]))
        acc = grad_a(tr, ti, sr[prev, :], si[prev, :], (c[2], c[3]))
        nr, ni = _cmul(c[0], c[1], ar, ai)
        return nr, ni, acc[0], acc[1]

    pwr, pwi, accr, acci = lax.fori_loop(0, SEG_LEN - 1, carry_in, (ar, ai, zero, zero), unroll=5)
    tr, ti = fix(pl.ds(0, SEG), (pwr, pwi))
    last = pl.ds((SEG_LEN - 1) * SEG, SEG)
    accr, acci = grad_a(tr, ti, _sub_shift(sr[last, :], True), _sub_shift(si[last, :], True), (accr, acci))
    return jnp.sum(accr, axis=0, keepdims=True), jnp.sum(acci, axis=0, keepdims=True)


S5_BWD_VMEM = 58 * 2**20


def s5_backward(dg, proj, s_re, s_im, wb_re, wb_im, wc_re, wc_im, a_re, a_im, d, dproj, plans=()):
    def kern(dg_ref, u_ref, sr_in, si_in, wbr, wbi, wcr, wci, ar, ai, d_ref, _, du_ref, dd_ref, dwcr, dwci, dwbr, dwbi, dar, dai, lr, li, sr, si):
        u = _wide(u_ref[...])
        sr[...], si[...] = _wide(sr_in[...]), _wide(si_in[...])
        _, vjp_out = jax.vjp(fn_s5_out, sr[...], si[...], u, d_ref[...], wcr[...], wci[...])
        lr[...], li[...], du_out, dd_ref[...], dwcr[...], dwci[...] = vjp_out((_wide(dg_ref[...]),))
        dar[...], dai[...] = _adjoint_scan_in_place(lr, li, sr, si, ar[...], ai[...])
        _, vjp_in = jax.vjp(fn_s5_bu, u, wbr[...], wbi[...])
        du_in, dwbr[...], dwbi[...] = vjp_in((lr[...], li[...]))
        du_ref[...] = (du_out + du_in).astype(du_ref.dtype)

    u_spec, s_spec, wb_spec, wc_spec, a_spec, d_spec = _s5_specs()
    outs = [(SDS(dproj.shape, dproj.dtype), u_spec), (SDS(d.shape, F32), d_spec), (SDS(wc_re.shape, F32), wc_spec), (SDS(wc_im.shape, F32), wc_spec),
            (SDS(wb_re.shape, F32), wb_spec), (SDS(wb_im.shape, F32), wb_spec), (SDS(a_re.shape, F32), a_spec), (SDS(a_im.shape, F32), a_spec)]
    return hosted_call(kern, name="s5_backward", grid=(S5_BLOCKS,),
                       in_specs=[u_spec, u_spec, s_spec, s_spec, wb_spec, wb_spec, wc_spec, wc_spec, a_spec, a_spec, d_spec, ANY],
                       out_specs=[sp for _, sp in outs], out_shape=[sd for sd, _ in outs], aliases={11: 0},
                       operands=[dg, proj, s_re, s_im, wb_re, wb_im, wc_re, wc_im, a_re, a_im, d, dproj],
                       scratch_shapes=[pltpu.VMEM((L, S5_LANES), F32)] * 4,
                       cparams=pltpu.CompilerParams(vmem_limit_bytes=S5_BWD_VMEM), plans=plans)


def fn_rms(x, g):
    return (rms(x, g),)


def fn_s5_disc(lre, lim, ls):
    step = jnp.exp(ls)
    e = jnp.exp(lre * step)
    a_re, a_im = e * jnp.cos(lim * step), e * jnp.sin(lim * step)
    den = lre * lre + lim * lim
    nr, ni = a_re - 1.0, a_im
    return a_re, a_im, (nr * lre + ni * lim) / den, (ni * lre - nr * lim) / den


def _group_mask(rows, cols, row_div, col_div):
    r = lax.broadcasted_iota(jnp.int32, (rows, cols), 0) // row_div % 8
    c = lax.broadcasted_iota(jnp.int32, (rows, cols), 1) // col_div
    return r == c


def _spread(x, mask):
    w = x.shape[1]
    copy = (lax.broadcasted_iota(jnp.int32, (w, 8 * w), 1) % w == lax.broadcasted_iota(jnp.int32, (w, 8 * w), 0)).astype(F32)
    return jnp.where(mask, jnp.dot(x, copy, precision=lax.Precision.HIGHEST, preferred_element_type=F32), 0.0)


def fn_s5_bmat(b_re, b_im, coef_re, coef_im):
    mask = _group_mask(b_re.shape[0], 8 * SC, SP, SC)
    return _spread(coef_re * b_re - coef_im * b_im, mask), _spread(coef_re * b_im + coef_im * b_re, mask)


def fn_s5_cmat(c_re, c_im):
    mask = _group_mask(c_re.shape[0], 8 * SP, SC, SP)
    return _spread(c_re, mask), _spread(c_im, mask)


def fn_s5_bu(u, wb_re, wb_im):
    return mm_nt(u, wb_re), mm_nt(u, wb_im)


def fn_s5_out(sr, si, u, d, wc_re, wc_im):
    y = mm_nt(sr, wc_re) - mm_nt(si, wc_im) + d * u
    return (jax.nn.gelu(y),)


def fn_merge_glu(z, mo, gate):
    yg = z[:, :PW] * jax.nn.sigmoid(z[:, PW:])
    return (jnp.concatenate([yg, mo], axis=1) * silu(gate),)


def fn_merge(prim, mo, gate):
    return (jnp.concatenate([prim, mo], axis=1) * silu(gate),)


def fn_mem_k(kv, g):
    return (jnp.concatenate([rms(kv[:, h * XHD:(h + 1) * XHD], g) for h in range(XH)], axis=1),)


def fn_mem_attn(xq, kn, v, g):
    outs = []
    for h in range(XH):
        sl = slice(h * XHD, (h + 1) * XHD)
        p = softmax_rows(mm_nt(rms(xq[:, sl], g), kn[:, sl]) * (XHD ** -0.5))
        outs.append(mm_nn(p, v[:, sl]))
    return (jnp.concatenate(outs, axis=1),)


def _half_rms(x, g):
    lo = lax.broadcasted_iota(jnp.int32, x.shape, 1) < ROPE
    x2 = x * x
    s_lo = jnp.sum(jnp.where(lo, x2, 0.0), axis=1, keepdims=True)
    s_hi = jnp.sum(jnp.where(lo, 0.0, x2), axis=1, keepdims=True)
    return x * lax.rsqrt(jnp.where(lo, s_lo, s_hi) / ROPE + EPS) * g


def _rope(x, cos2, sin_signed):
    first = lax.broadcasted_iota(jnp.int32, x.shape, 1) % ROPE < ROPE // 2
    return x * cos2 + jnp.where(first, lane_roll(x, 128 - ROPE // 2), lane_roll(x, ROPE // 2)) * sin_signed


def fn_mla_prep(q, kv, kr, cos2, sin_signed, qnn, knn, qrn, krn):
    lo = lax.broadcasted_iota(jnp.int32, kr.shape, 1) < ROPE
    kr_pad = jnp.where(lo, _rope(_half_rms(kr, krn), cos2, sin_signed), 0.0)
    qf, kf, vs = [], [], []
    for m in range(MH // 2):
        pair = _rope(_half_rms(q[:, MH * NOPE + 128 * m:MH * NOPE + 128 * (m + 1)], qrn), cos2, sin_signed)
        for h, rope_h in ((2 * m, pair), (2 * m + 1, lane_roll(pair, ROPE))):
            qf.append(jnp.concatenate([rms(q[:, NOPE * h:NOPE * (h + 1)], qnn), jnp.where(lo, rope_h, 0.0)], axis=1))
    for h in range(MH):
        kf.append(jnp.concatenate([rms(kv[:, 256 * h:256 * h + NOPE], knn), kr_pad], axis=1))
        vs.append(kv[:, 256 * h + NOPE:256 * (h + 1)])
    return jnp.stack(qf), jnp.stack(kf), jnp.stack(vs)


ATT_TQ = 512


def _attn_scores(q, kf):
    tq = q.shape[0]
    scale = (NOPE + ROPE) ** -0.5
    own = _dg(q, kf[-tq:], 1, 1) * scale
    own = jnp.where(lax.broadcasted_iota(jnp.int32, own.shape, 1) <= lax.broadcasted_iota(jnp.int32, own.shape, 0), own, jnp.finfo(F32).min)
    return own if kf.shape[0] == tq else jnp.concatenate([_dg(q, kf[:-tq], 1, 1) * scale, own], axis=1)


ATT_FWD_HEADS, ATT_BWD_HEADS = 6, 2


def _attn_specs(heads):
    q_spec = pl.BlockSpec((heads, ATT_TQ, 256), lambda h, i: (h, i, 0))
    k_spec = pl.BlockSpec((heads, L, 256), lambda h, i: (h, 0, 0))
    v_spec = pl.BlockSpec((heads, L, 128), lambda h, i: (h, 0, 0))
    o_spec = pl.BlockSpec((ATT_TQ, heads * 128), lambda h, i: (i, h))
    lse_spec = pl.BlockSpec((heads, ATT_TQ, 1), lambda h, i: (h, i, 0))
    return q_spec, k_spec, v_spec, o_spec, lse_spec


def _attn_branches(heads, body):
    i = pl.program_id(1)
    for t in range(L // ATT_TQ):
        @pl.when(i == t)
        def _(t=t):
            for hh in range(heads):
                body(hh, slice(hh * 128, (hh + 1) * 128), (t + 1) * ATT_TQ)


def causal_attn(qf, kf, vh):
    def kern(q_ref, k_ref, v_ref, o_ref, lse_ref):
        def body(hh, lanes, keys):
            s = _attn_scores(q_ref[hh], k_ref[hh, :keys, :])
            m = jnp.max(s, axis=-1, keepdims=True)
            e = jnp.exp(s - m)
            total = jnp.sum(e, axis=-1, keepdims=True)
            o_ref[:, lanes] = (_dg(e, v_ref[hh, :keys, :], 1, 0) / total).astype(o_ref.dtype)
            lse_ref[hh] = m + jnp.log(total)

        _attn_branches(ATT_FWD_HEADS, body)

    q_spec, k_spec, v_spec, o_spec, lse_spec = _attn_specs(ATT_FWD_HEADS)
    return pl.pallas_call(kern, grid=(MH // ATT_FWD_HEADS, L // ATT_TQ), in_specs=[q_spec, k_spec, v_spec], out_specs=[o_spec, lse_spec],
                          out_shape=[SDS((L, MH * VD), F32), SDS((MH, L, 1), F32)], name="l1_attn", compiler_params=_cparams())(qf, kf, vh)


def causal_attn_bwd(qf, kf, vh, out, lse, dout):
    scale = (NOPE + ROPE) ** -0.5

    def kern(q_ref, k_ref, v_ref, o_ref, lse_ref, do_ref, dq_ref, dk_ref, dv_ref):
        @pl.when(pl.program_id(1) == 0)
        def _():
            dk_ref[...] = jnp.zeros_like(dk_ref)
            dv_ref[...] = jnp.zeros_like(dv_ref)

        def body(hh, lanes, keys):
            q, k, v, do = q_ref[hh], k_ref[hh, :keys, :], v_ref[hh, :keys, :], do_ref[:, lanes]
            p = jnp.exp(_attn_scores(q, k) - lse_ref[hh])
            delta = jnp.sum(do * _wide(o_ref[:, lanes]), axis=-1, keepdims=True)
            dv_ref[hh, :keys, :] += _dg(p, do, 0, 0)
            ds = p * (_dg(do, v, 1, 1) - delta) * scale
            dq_ref[hh] = _dg(ds, k, 1, 0)
            dk_ref[hh, :keys, :] += _dg(ds, q, 0, 0)

        _attn_branches(ATT_BWD_HEADS, body)

    q_spec, k_spec, v_spec, o_spec, lse_spec = _attn_specs(ATT_BWD_HEADS)
    return pl.pallas_call(kern, grid=(MH // ATT_BWD_HEADS, L // ATT_TQ), in_specs=[q_spec, k_spec, v_spec, o_spec, lse_spec, o_spec],
                          out_specs=[q_spec, k_spec, v_spec], out_shape=[SDS(qf.shape, F32), SDS(kf.shape, F32), SDS(vh.shape, F32)],
                          name="l1_attn_bwd", compiler_params=_cparams())(qf, kf, vh, out, lse, dout)


def loss_and_grad(y, target, tl=512):
    def kern(y_ref, t_ref, dy_ref, loss_ref):
        d = y_ref[...] - t_ref[...]
        dy_ref[...] = d / D

        @pl.when(pl.program_id(0) == 0)
        def _():
            loss_ref[...] = jnp.zeros_like(loss_ref)

        loss_ref[...] += 0.5 * jnp.sum(jnp.sum(d * d, axis=1, keepdims=True), axis=0, keepdims=True) / D

    return pl.pallas_call(kern, grid=(L // tl,), in_specs=[rspec(tl, D), rspec(tl, D)], out_specs=[rspec(tl, D), cspec((1, 1))],
                          out_shape=[SDS((L, D), F32), SDS((1, 1), F32)], name="loss", compiler_params=_cparams())(y, target)


def adamw(name, w, g, m, v):
    rows, cols = w.shape
    block_row_bytes = 7 * 2 * 4 * max(cols, 128)
    tr = _row_tile(rows, min(2048, ADAMW_VMEM // block_row_bytes // 8 * 8), 8)

    def kern(w_ref, g_ref, m_ref, v_ref, d_ref, nm_ref, nv_ref):
        gg = g_ref[...]
        nm = ADAM_B1 * m_ref[...] + (1.0 - ADAM_B1) * gg
        nv = ADAM_B2 * v_ref[...] + (1.0 - ADAM_B2) * jnp.square(gg)
        m_hat = nm / (1.0 - ADAM_B1 ** ADAM_STEP)
        v_hat = nv / (1.0 - ADAM_B2 ** ADAM_STEP)
        d_ref[...] = -ADAM_LR * (m_hat / (jnp.sqrt(v_hat) + ADAM_EPS) + ADAM_WD * w_ref[...])
        nm_ref[...] = nm
        nv_ref[...] = nv

    spec = rspec(tr, cols)
    return hosted_call(kern, name=name, grid=(rows // tr,), in_specs=[spec] * 4, out_specs=[spec] * 3,
                       out_shape=[SDS((rows, cols), F32)] * 3, operands=[w, g, m, v])


def _row_tile(rows, cap=512, unit=16):
    return max(t for t in range(unit, cap + 1, unit) if rows % t == 0)


def _place():
    x, y, c = lax.axis_index("x"), lax.axis_index("y"), lax.axis_index("c")
    return x, y, c, [(1 - x, y), (x, 1 - y), (1 - x, 1 - y)]


def _row_chunks(rows, n, dtype):
    unit = 32 // jnp.dtype(dtype).itemsize
    base, extra = divmod(rows // unit, n)
    out, start = [], 0
    for k in range(n):
        size = (base + (k < extra)) * unit
        if size:
            out.append((start, size))
            start += size
    assert start == rows, (rows, unit)
    return out


PIECE_BYTES = 1 << 20


def _pieces(shapes_dtypes, rows_of):
    out = []
    for b, (shape, dtype) in enumerate(shapes_dtypes):
        rows = rows_of(shape)
        n = max(1, min(4, rows * shape[-1] * jnp.dtype(dtype).itemsize // PIECE_BYTES))
        out += [(b, st, sz) for st, sz in _row_chunks(rows, n, dtype)]
    return out


def all_gather_chips(name, shards):
    nb = len(shards)
    pieces = _pieces([(s.shape, s.dtype) for s in shards], lambda shape: shape[0] // 2)
    n = len(pieces)

    def body(*refs):
        x_refs, out_refs, send_sems, recv_sems = refs[:nb], refs[nb:2 * nb], refs[2 * nb], refs[2 * nb + 1]
        x, y, c, chips = _place()
        sibling = (x, y, 1 - c)
        mine = 2 * x + y

        def copy(sem, chip, cc, k, to, from_input=False):
            b, st, sz = pieces[k]
            rows_k = pl.ds(cc * (x_refs[b].shape[0] // 2) + st, sz)
            dst = out_refs[b].at[chip, rows_k, :]
            return pltpu.make_async_remote_copy(src_ref=x_refs[b].at[rows_k, :] if from_input else dst, dst_ref=dst,
                                                send_sem=send_sems.at[sem], recv_sem=recv_sems.at[sem], device_id=to, device_id_type=MESH_ID)

        order = [(k, j, 2 * cx + cy, (cx, cy, c)) for k in range(n) for j, (cx, cy) in enumerate(chips)]
        first = [copy(j * n + k, mine, c, k, to, from_input=True) for k, j, _, to in order]
        for cp in first:
            cp.start()
        passed = []
        for k, j, chip, _ in order:
            copy(j * n + k, chip, c, k, sibling).wait_recv()
            passed.append(copy((3 + j) * n + k, chip, c, k, sibling))
            passed[-1].start()
        for k, j, chip, _ in order:
            copy((3 + j) * n + k, chip, 1 - c, k, sibling).wait_recv()
        for cp in first + passed:
            cp.wait_send()

    return pl.pallas_call(body, in_specs=[ANY] * nb, out_specs=[ANY] * nb, out_shape=[SDS((4,) + s.shape, s.dtype) for s in shards],
                          scratch_shapes=_dma_sems(6 * n), name=name)(*shards)


def plan_gather_ici(shards):
    pieces = _pieces([(s.shape, s.dtype) for s in shards], lambda shape: shape[0] // 2)
    n = len(pieces)

    def copies(x_refs, out_refs, send_sems, recv_sems):
        x, y, c, chips = _place()
        mine = 2 * x + y

        def copy(j, k, chip, to, from_input):
            b, st, sz = pieces[k]
            rows_k = pl.ds(c * (x_refs[b].shape[0] // 2) + st, sz)
            dst = out_refs[b].at[chip, rows_k, :]
            return pltpu.make_async_remote_copy(src_ref=x_refs[b].at[rows_k, :] if from_input else dst, dst_ref=dst, send_sem=send_sems.at[j * n + k],
                                                recv_sem=recv_sems.at[j * n + k], device_id=to, device_id_type=MESH_ID)

        order = [(k, j, 2 * cx + cy, (cx, cy, c)) for k in range(n) for j, (cx, cy) in enumerate(chips)]
        return [copy(j, k, mine, to, True) for k, j, _, to in order], [copy(j, k, chip, to, False) for k, j, chip, to in order]

    return Plan(shards, [SDS((4,) + s.shape, s.dtype) for s in shards], {}, 3 * n, copies)


def plan_gather_pass(gathered):
    pieces = _pieces([(g.shape[1:], g.dtype) for g in gathered], lambda shape: shape[0] // 2)
    n = len(pieces)

    def copies(_, out_refs, send_sems, recv_sems):
        x, y, c, chips = _place()

        def copy(j, k, chip, cc):
            b, st, sz = pieces[k]
            rows_k = out_refs[b].at[chip, pl.ds(cc * (out_refs[b].shape[1] // 2) + st, sz), :]
            return pltpu.make_async_remote_copy(src_ref=rows_k, dst_ref=rows_k, send_sem=send_sems.at[j * n + k], recv_sem=recv_sems.at[j * n + k],
                                                device_id=(x, y, 1 - c), device_id_type=MESH_ID)

        order = [(k, j, 2 * cx + cy) for k in range(n) for j, (cx, cy) in enumerate(chips)]
        return [copy(j, k, chip, c) for k, j, chip in order], [copy(j, k, chip, 1 - c) for k, j, chip in order]

    return Plan(gathered, [SDS(g.shape, g.dtype) for g in gathered], {i: i for i in range(len(gathered))}, 3 * n, copies)


def plan_pair_exchange(gs):
    pieces = _pieces([(g.shape, g.dtype) for g in gs], lambda shape: shape[1] // 2)

    def copies(g_refs, got_refs, send_sems, recv_sems):
        x, y, c, _ = _place()
        swaps = [pltpu.make_async_remote_copy(src_ref=g_refs[b].at[:, pl.ds((1 - c) * (g_refs[b].shape[1] // 2) + st, sz), :],
                                              dst_ref=got_refs[b].at[:, pl.ds(st, sz), :], send_sem=send_sems.at[k], recv_sem=recv_sems.at[k],
                                              device_id=(x, y, 1 - c), device_id_type=MESH_ID)
                 for k, (b, st, sz) in enumerate(pieces)]
        return swaps, swaps

    return Plan(gs, [SDS((g.shape[0], g.shape[1] // 2, g.shape[2]), g.dtype) for g in gs], {}, len(pieces), copies)


def plan_chip_scatter(ps):
    pieces = _pieces([(p.shape, p.dtype) for p in ps], lambda shape: shape[1])
    n = len(pieces)

    def copies(p_refs, q_refs, send_sems, recv_sems):
        x, y, c, chips = _place()
        mine = 2 * x + y

        def copy(j, k, src_slot, dst_slot, to):
            b, st, sz = pieces[k]
            return pltpu.make_async_remote_copy(src_ref=p_refs[b].at[src_slot, pl.ds(st, sz), :], dst_ref=q_refs[b].at[dst_slot, pl.ds(st, sz), :],
                                                send_sem=send_sems.at[j * n + k], recv_sem=recv_sems.at[j * n + k], device_id=to,
                                                device_id_type=MESH_ID)

        order = [(k, j, 2 * cx + cy, (cx, cy, c)) for k in range(n) for j, (cx, cy) in enumerate(chips)]
        return [copy(j, k, chip, mine, to) for k, j, chip, to in order], [copy(j, k, mine, chip, to) for k, j, chip, to in order]

    return Plan(ps, [SDS(p.shape, p.dtype) for p in ps], {}, 3 * n, copies)


def plan_pair_join(bufs):
    pieces = _pieces([(b.shape, b.dtype) for b in bufs], lambda shape: shape[0] // 2)

    def copies(_, out_refs, send_sems, recv_sems):
        x, y, c, _ = _place()

        def copy(k, cc):
            b, st, sz = pieces[k]
            rows_k = out_refs[b].at[pl.ds(cc * (out_refs[b].shape[0] // 2) + st, sz), :]
            return pltpu.make_async_remote_copy(src_ref=rows_k, dst_ref=rows_k, send_sem=send_sems.at[k], recv_sem=recv_sems.at[k],
                                                device_id=(x, y, 1 - c), device_id_type=MESH_ID)

        return [copy(k, c) for k in range(len(pieces))], [copy(k, 1 - c) for k in range(len(pieces))]

    return Plan(bufs, [SDS(b.shape, b.dtype) for b in bufs], {i: i for i in range(len(bufs))}, len(pieces), copies)


def run_plan(name, plan):
    hosted_call(lambda: None, name=name, grid=(1,), in_specs=[], out_specs=[], out_shape=[], operands=[], plans=[plan])
    return plan.results


HBM = pl.BlockSpec(memory_space=pltpu.HBM)
SEMS = pl.BlockSpec(memory_space=pltpu.SEMAPHORE)
SPLIT_PARAMS = dict(has_side_effects=pltpu.SideEffectType.DATAFLOW_SIDE_EFFECTING)


def _plan_buffers(plan):
    in_place = {o: i for i, o in plan.aliases.items()}
    bufs = [pltpu.with_memory_space_constraint(a, pltpu.HBM) for a in plan.operands]
    where = []
    for o, sd in enumerate(plan.out_shape):
        if o in in_place:
            where.append(in_place[o])
        else:
            where.append(len(bufs))
            bufs.append(pltpu.with_memory_space_constraint(lax.empty(sd.shape, sd.dtype), pltpu.HBM))
    return bufs, where


def split_start(name, plans):
    layout = [_plan_buffers(p) for p in plans]
    counts = [len(b) for b, _ in layout]
    n_buf = sum(counts)

    def body(*refs):
        sems, token = refs[n_buf:n_buf + 2 * len(plans)], refs[-1]
        pos = 0
        for k, (p, (_, where)) in enumerate(zip(plans, layout)):
            mine = refs[pos:pos + counts[k]]
            pos += counts[k]
            sends, _ = p.copies(mine[:len(p.operands)], [mine[w] for w in where], sems[2 * k], sems[2 * k + 1])
            for cp in sends:
                cp.start()
        token[...] = jnp.zeros_like(token)

    bufs = [b for bs, _ in layout for b in bs]
    res = pl.pallas_call(
        body, name=name, in_specs=[HBM] * n_buf,
        out_specs=[SEMS] * (2 * len(plans)) + [HBM] * n_buf + [pl.BlockSpec(memory_space=pltpu.VMEM)],
        out_shape=[pltpu.SemaphoreType.DMA((p.n_sems,)) for p in plans for _ in range(2)] + [pltpu.HBM(b.shape, b.dtype) for b in bufs]
        + [SDS((8, 128), F32)],
        input_output_aliases={i: 2 * len(plans) + i for i in range(n_buf)}, compiler_params=pltpu.CompilerParams(**SPLIT_PARAMS))(*bufs)
    pos = 2 * len(plans)
    for k, p in enumerate(plans):
        p.in_flight = (res[2 * k], res[2 * k + 1], list(res[pos:pos + counts[k]]), layout[k][1])
        pos += counts[k]
    return res[-1]


def split_wait(name, plan, after):
    send_sems, recv_sems, bufs, where = plan.in_flight
    n_buf = len(bufs)

    def body(*refs):
        mine = refs[:n_buf]
        sends, recvs = plan.copies(mine[:len(plan.operands)], [mine[w] for w in where], refs[n_buf], refs[n_buf + 1])
        for cp in recvs:
            cp.wait_recv()
        for cp in sends:
            cp.wait_send()

    res = pl.pallas_call(body, name=name, in_specs=[HBM] * n_buf + [SEMS, SEMS, ANY], out_specs=[HBM] * n_buf,
                         out_shape=[pltpu.HBM(b.shape, b.dtype) for b in bufs], input_output_aliases={i: i for i in range(n_buf)},
                         compiler_params=pltpu.CompilerParams(**SPLIT_PARAMS))(*bufs, send_sems, recv_sems, after)
    plan.results = [res[w] for w in where]
    return plan.results


def pair_add(name, g, got, place):
    slots, rows, cols = g.shape
    half = rows // 2
    tr = _row_tile(half)
    nb = half // tr

    def kern(_, g_ref, t_ref, o_ref):
        o_ref[...] = (g_ref[...].astype(F32) + t_ref[...].astype(F32)).astype(o_ref.dtype)

    blk = pl.BlockSpec((None, tr, cols), lambda s, i, p: (s, i, 0))
    grid_spec = pltpu.PrefetchScalarGridSpec(
        num_scalar_prefetch=1, grid=(slots, nb),
        in_specs=[pl.BlockSpec((None, tr, cols), lambda s, i, p: (s, p[1] * nb + i, 0)), blk], out_specs=blk)
    return pl.pallas_call(kern, grid_spec=grid_spec, out_shape=SDS((slots, half, cols), g.dtype), name=name,
                          compiler_params=_cparams())(place, g, got)


def chip_add(name, p, q, place):
    slots, half, cols = p.shape
    tr = _row_tile(half)
    nb = half // tr

    def kern(_, p_ref, q1, q2, q3, o_ref):
        o_ref[...] = p_ref[...].astype(F32) + q1[...].astype(F32) + q2[...].astype(F32) + q3[...].astype(F32)

    def slot(k):
        return pl.BlockSpec((None, tr, cols), lambda i, pr: ((pr[0] + k) % slots, i, 0))

    grid_spec = pltpu.PrefetchScalarGridSpec(
        num_scalar_prefetch=1, grid=(nb,), in_specs=[slot(0), slot(1), slot(2), slot(3)],
        out_specs=pl.BlockSpec((tr, cols), lambda i, pr: (pr[1] * nb + i, 0)))
    return pl.pallas_call(kern, grid_spec=grid_spec, out_shape=SDS((2 * half, cols), F32), name=name,
                          compiler_params=_cparams())(place, p, q, q, q)


def pair_adds(tag, gs, gots, place):
    return [pair_add(f"{tag}_pair_add_{i}", g, got, place) for i, (g, got) in enumerate(zip(gs, gots))]


def chip_adds(tag, pairs, qs, place):
    return [chip_add(f"{tag}_chip_add_{i}", p, q, place) for i, (p, q) in enumerate(zip(pairs, qs))]


def reduce_scatter_chips(tag, gs, place):
    pairs = pair_adds(tag, gs, run_plan(tag + "_pair_exchange", plan_pair_exchange(gs)), place)
    return run_plan(tag + "_pair_join", plan_pair_join(chip_adds(tag, pairs, run_plan(tag + "_chip_scatter", plan_chip_scatter(pairs)), place)))


BIG = [("w_out", (2, 512, 1024)), ("w_mem_kv", (2, 256, 1024)), ("s5_w_in", (1, 1024, 1024)), ("s5_w_glu", (1, 1536, 768)),
       ("mla_w_in", (1, 1024, 848)), ("mla_w_uq", (1, 512, 576)), ("mla_w_ukv", (1, 256, 768))]
SHARDED_SMALL = [("mla_q_lora_norm", (1, 128)), ("mla_kv_lora_norm", (1, 64))]
SMALL = [("ln_gain", (2, 1024)), ("mem_norm", (2, 1024)), ("xq_norm", (2, 128)), ("xk_norm", (2, 128)),
         ("s5_lambda_re", (1, 96, 64)), ("s5_lambda_im", (1, 96, 64)), ("s5_log_step", (1, 96)),
         ("s5_b_re", (1, 96, 64, 16)), ("s5_b_im", (1, 96, 64, 16)), ("s5_c_re", (1, 96, 16, 64)), ("s5_c_im", (1, 96, 16, 64)),
         ("s5_d", (1, 1536)), ("mla_q_nope_norm", (1, 128)), ("mla_k_nope_norm", (1, 128)), ("mla_q_rope_norm", (1, 64)),
         ("mla_k_rope_norm", (1, 64))]
WEIGHT_ORDER = ["ln_gain", "w_out", "mem_norm", "w_mem_kv", "xq_norm", "xk_norm", "s5_w_in", "s5_lambda_re", "s5_lambda_im",
                "s5_log_step", "s5_b_re", "s5_b_im", "s5_c_re", "s5_c_im", "s5_d", "s5_w_glu", "mla_w_in", "mla_q_lora_norm",
                "mla_kv_lora_norm", "mla_w_uq", "mla_w_ukv", "mla_q_nope_norm", "mla_k_nope_norm", "mla_q_rope_norm", "mla_k_rope_norm"]
MINOR_LAST = {"mla_w_in": (0, 2, 1), "mla_w_uq": (0, 2, 1), "s5_b_re": (0, 2, 3, 1), "s5_b_im": (0, 2, 3, 1),
              "s5_c_re": (0, 2, 3, 1), "s5_c_im": (0, 2, 3, 1)}
SMALL_FULL = SMALL + [(n, (1, 4 * s[1])) for n, s in SHARDED_SMALL]
N_SMALL = sum(math.prod(s) for _, s in SMALL_FULL)
SMALL_ROWS, SMALL_LANES = 128, 1024

PAIR_OUT, PAIR_MKV, PAIR_ROWS = 0, 512, 768


def stack_shards(w, dtype):
    pairs = [jnp.concatenate([w["w_out"][l], w["w_mem_kv"][l]], axis=0).astype(dtype) for l in range(2)]
    return ([w["s5_w_in"][0].astype(dtype)], [pairs[0], w["s5_w_glu"][0].astype(dtype)],
            [pairs[1], w["mla_w_ukv"][0].astype(dtype), w["mla_w_in"][0].astype(dtype), w["mla_w_uq"][0].astype(dtype)])


def pair_views(pair):
    return {"w_out": Sharded(pair, "row", PAIR_OUT, 512), "w_mem_kv": Sharded(pair, "row", PAIR_MKV, 256)}


def grad_views():
    pair = SDS((4, PAIR_ROWS, 1024), BF16)
    return {"w_out": Sharded(pair, "row", PAIR_OUT, 512), "w_mem_kv": Sharded(pair, "row", PAIR_MKV, 256),
            "s5_w_in": Sharded(SDS((4, 1024, 1024), BF16), "col", 0, 1024), "s5_w_glu": Sharded(SDS((4, 1536, 768), BF16), "col", 0, 1536),
            "mla_w_ukv": Sharded(SDS((4, 256, 768), BF16), "col", 0, 256)}


def cols_to_shards(full):
    return full.reshape(full.shape[0], 4, full.shape[1] // 4).transpose(1, 0, 2)


def shards_to_cols(arr):
    return arr.transpose(1, 0, 2).reshape(arr.shape[1], 4 * arr.shape[2])


def mla_in_permute(w):
    o1, o2, o3, o4 = QL, QL + KVL, QL + KVL + ROPE, QL + KVL + ROPE + XQW
    return jnp.concatenate([w[:, o4:], w[:, :o1], w[:, o3:o4], w[:, o1:o2], w[:, o2:o3],
                            jnp.zeros((w.shape[0], MLA_IN_P - MLA_IN), w.dtype)], axis=1)


def mla_in_unpermute(d):
    return jnp.concatenate([d[:, 2048:2560], d[:, 3072:3328], d[:, 3328:3392], d[:, 2560:3072], d[:, :2048]], axis=1)


def uq_permute(w):
    w3 = w.reshape(w.shape[0], MH, NOPE + ROPE)
    return jnp.concatenate([w3[:, :, :NOPE].reshape(w.shape[0], MH * NOPE), w3[:, :, NOPE:].reshape(w.shape[0], MH * ROPE)], axis=1)


def uq_unpermute(d):
    dn = d[:, :MH * NOPE].reshape(d.shape[0], MH, NOPE)
    dr = d[:, MH * NOPE:].reshape(d.shape[0], MH, ROPE)
    return jnp.concatenate([dn, dr], axis=2).reshape(d.shape[0], MH * (NOPE + ROPE))


def time_permute(a):
    return a.reshape(SEG, SEG_LEN, a.shape[-1]).transpose(1, 0, 2).reshape(L, a.shape[-1])


def time_unpermute(a):
    return a.reshape(SEG_LEN, SEG, a.shape[-1]).transpose(1, 0, 2).reshape(L, a.shape[-1])


def mem_branch_fwd(tag, mem, mem_norm, w_mem_kv, xk_norm):
    mn = row_fwd(tag + "_mem_rms", fn_rms, ML, ML, [(mem, D, 0)], [mem_norm], [(D, BF16)])[0]
    kv = matmul(tag + "_mem_kv", mn, w_mem_kv, "nn", F32)
    kn = row_fwd(tag + "_mem_knorm", fn_mem_k, ML, ML, [(kv, XQW, 0)], [xk_norm], [(XQW, F32)])[0]
    return mn, kv, kn


def mem_branch_bwd(tag, mem, mem_norm, w_mem_kv, xk_norm, mn, kv, dkn, dv, g_view, g_wide):
    dk, dxk = row_bwd(tag + "_mem_knorm_bwd", fn_mem_k, ML, ML, [(kv, XQW, 0)], [xk_norm], [(dkn, XQW, 0)], [True], [True])
    dkv = jnp.concatenate([dk, dv], axis=1)
    dmn = matmul(tag + "_mem_kv_dx", dkv, w_mem_kv, "nt", F32)
    g_wide = matmul(tag + "_mem_kv_dw", mn, dkv, "tn", out=g_view, into=g_wide)
    dmem_norm = row_bwd(tag + "_mem_rms_bwd", fn_rms, ML, ML, [(mem, D, 0)], [mem_norm], [(dmn, D, 0)], [False], [True])[0]
    return g_wide, dmem_norm, dxk


def mem_attn_fwd(tag, proj, cb, kn, kv, xq_norm):
    return row_fwd(tag + "_mem_attn", fn_mem_attn, L, ROW_TILE, [(proj, XQW, cb)], [kn, kv[:, XQW:], xq_norm], [(XQW, F32)])[0]


def mem_attn_bwd(tag, proj, cb, kn, kv, xq_norm, dmo, dproj):
    place = {"cols": proj.shape[1], "cb": cb, "into": dproj, "dtype": dproj.dtype}
    return row_bwd(tag + "_mem_attn_bwd", fn_mem_attn, L, ROW_TILE, [(proj, XQW, cb)], [kn, kv[:, XQW:], xq_norm], [(dmo, XQW, 0)],
                   [place], [True, True, True])


def device_step(x, mem, positions, target, small, env, hooks=None):
    hooks = hooks or {}

    def plans_for(name):
        return hooks[("plans", name)](env) if ("plans", name) in hooks else ()

    def around(when, name, last=None):
        if (when, name) in hooks:
            hooks[(when, name)](env, last)

    g = {}
    gw = grad_views()
    ln, mem_norm, xq_norm, xk_norm = small["ln_gain"], small["mem_norm"], small["xq_norm"], small["xk_norm"]

    lre, lim = small["s5_lambda_re"][0], small["s5_lambda_im"][0]
    ls = small["s5_log_step"].reshape(SG, 1)
    one = pl.BlockSpec((SG, SP), lambda i: (0, 0))
    col = pl.BlockSpec((SG, 1), lambda i: (0, 0))
    disc_ins = [(lre, one), (lim, one), (ls, col)]
    a_re, a_im, coef_re, coef_im = stage("s5_disc", fn_s5_disc, (1,), disc_ins, [(SDS((SG, SP), F32), one)] * 4)
    b_re, b_im = small["s5_b_re"].reshape(SN, SC), small["s5_b_im"].reshape(SN, SC)
    c_re, c_im = small["s5_c_re"].reshape(PW, SP), small["s5_c_im"].reshape(PW, SP)
    bmat_rows = [(b_re, SC, 0), (b_im, SC, 0), (coef_re.reshape(SN, 1), 1, 0), (coef_im.reshape(SN, 1), 1, 0)]
    wb_re, wb_im = row_fwd("s5_bmat", fn_s5_bmat, SN, 512, bmat_rows, [], [(128, F32)] * 2)
    cmat_rows = [(c_re, SP, 0), (c_im, SP, 0)]
    wc_re, wc_im = row_fwd("s5_cmat", fn_s5_cmat, PW, 128, cmat_rows, [], [(512, F32)] * 2)
    a_re_v, a_im_v = a_re.reshape(1, SN), a_im.reshape(1, SN)
    s5_d = small["s5_d"]

    xp = time_permute(x)
    h0 = row_fwd("l0_rms", fn_rms, L, ROW_TILE, [(xp, D, 0)], [ln[0:1]], [(D, BF16)])[0]
    around("before", "l0_in", wb_re)
    w_in0 = Sharded(env["in0"], "col", 0, 1024)
    proj0 = matmul("l0_in", h0, w_in0, "nn")
    s_re, s_im, g0 = s5_forward(proj0, wb_re, wb_im, wc_re, wc_im, a_re_v, a_im_v, s5_d, plans=plans_for("s5_forward"))
    around("before", "l0_glu", g0)
    w0 = dict(pair_views(env["pair0"]), s5_w_glu=Sharded(env["glu"], "col", 0, 1536))
    z0 = matmul("l0_glu", g0, w0["s5_w_glu"], "nn", plans=plans_for("l0_glu"))
    mn0, kv0, kn0 = mem_branch_fwd("l0", mem, mem_norm[0:1], w0["w_mem_kv"], xk_norm[0:1])
    mo0 = mem_attn_fwd("l0", proj0, 3, kn0, kv0, xq_norm[0:1])
    o0 = row_fwd("l0_merge", fn_merge_glu, L, ROW_TILE, [(z0, 2 * PW, 0), (mo0, XQW, 0), (proj0, BW, 1)], [], [(BW, BF16)])[0]
    around("before", "l0_out", o0)
    x1p = matmul("l0_out", o0, w0["w_out"], "nn", F32, add=xp, plans=plans_for("l0_out"))
    around("after", "l0_out", x1p)
    x1 = time_unpermute(x1p)

    w1 = dict(pair_views(env["pair1"]), mla_w_ukv=Sharded(env["ukv"], "col", 0, 256))
    w_in1, w_uq = env["w_in1"], env["w_uq"]
    h1 = row_fwd("l1_rms", fn_rms, L, ROW_TILE, [(x1, D, 0)], [ln[1:2]], [(D, BF16)])[0]
    proj1 = matmul("l1_in", h1, w_in1, "nn")
    qln, kvln = env["q_lora_norm"].reshape(1, QL), env["kv_lora_norm"].reshape(1, KVL)
    cqn = row_fwd("l1_q_lora_rms", fn_rms, L, ROW_TILE, [(proj1, QL, 4)], [qln], [(QL, BF16)])[0]
    ckvn = row_fwd("l1_kv_lora_rms", fn_rms, L, ROW_TILE, [(proj1, KVL, 12)], [kvln], [(KVL, BF16)])[0]
    q = matmul("l1_uq", cqn, w_uq, "nn")
    kv = matmul("l1_ukv", ckvn, w1["mla_w_ukv"], "nn")
    inv_freq = ROPE_THETA ** (-jnp.arange(ROPE // 2, dtype=F32) / (ROPE // 2))
    ang = positions.astype(F32)[:, None] * inv_freq
    cos2 = jnp.tile(jnp.cos(ang), (1, 4))
    sin_signed = jnp.tile(jnp.concatenate([-jnp.sin(ang), jnp.sin(ang)], axis=1), (1, 2))
    qnn, knn = small["mla_q_nope_norm"], small["mla_k_nope_norm"]
    qrn, krn = jnp.tile(small["mla_q_rope_norm"], (1, 2)), jnp.tile(small["mla_k_rope_norm"], (1, 2))
    tp = 256
    prep_ins = [(q, rspec(tp, MH * (NOPE + ROPE))), (kv, rspec(tp, MH * 256)), (proj1, rspec(tp, 128, 26)),
                (cos2, rspec(tp, 128)), (sin_signed, rspec(tp, 128))] + [(a, cspec((1, 128))) for a in (qnn, knn, qrn, krn)]
    hq_spec = pl.BlockSpec((MH, tp, 256), lambda i: (0, i, 0))
    hv_spec = pl.BlockSpec((MH, tp, 128), lambda i: (0, i, 0))
    qf, kf, vh = stage("l1_mla_prep", fn_mla_prep, (L // tp,), prep_ins,
                       [(SDS((MH, L, 256), BF16), hq_spec), (SDS((MH, L, 256), BF16), hq_spec), (SDS((MH, L, 128), BF16), hv_spec)])
    attn, attn_lse = causal_attn(qf, kf, vh)
    mn1, kv1, kn1 = mem_branch_fwd("l1", mem, mem_norm[1:2], w1["w_mem_kv"], xk_norm[1:2])
    mo1 = mem_attn_fwd("l1", proj1, 5, kn1, kv1, xq_norm[1:2])
    o1 = row_fwd("l1_merge", fn_merge, L, ROW_TILE, [(attn, PW, 0), (mo1, XQW, 0), (proj1, BW, 0)], [], [(BW, BF16)])[0]
    x2 = matmul("l1_out", o1, w1["w_out"], "nn", F32, add=x1)
    dx2, loss = loss_and_grad(x2, target)
    around("after", "loss", loss)

    do1 = matmul("l1_out_dx", dx2, w1["w_out"], "nt")
    g_pair1 = matmul("l1_out_dw", o1, dx2, "tn", out=gw["w_out"])
    dattn, dmo1, dproj1 = row_bwd("l1_merge_bwd", fn_merge, L, ROW_TILE, [(attn, PW, 0), (mo1, XQW, 0), (proj1, BW, 0)], [],
                                  [(do1, BW, 0)], [True, True, {"cols": MLA_IN_P, "cb": 0, "dtype": BF16}], [])
    dproj1, dkn1, dv1, dxqn1 = mem_attn_bwd("l1", proj1, 5, kn1, kv1, xq_norm[1:2], dmo1, dproj1)
    env["g_pair1"], dmem_norm1, dxk1 = mem_branch_bwd("l1", mem, mem_norm[1:2], w1["w_mem_kv"], xk_norm[1:2], mn1, kv1, dkn1, dv1,
                                                      gw["w_mem_kv"], g_pair1)
    dqf, dkf, dvh = causal_attn_bwd(qf, kf, vh, attn, attn_lse, dattn)
    prep_diffs = [("row", SDS((L, MH * (NOPE + ROPE)), BF16), rspec(tp, MH * (NOPE + ROPE))), ("row", SDS((L, MH * 256), BF16), rspec(tp, MH * 256)),
                  ("row", SDS((L, MLA_IN_P), BF16), rspec(tp, 128, 26), {"into": dproj1}), None, None] + [("acc", (0,))] * 4
    dq, dkv, dproj1, dqnn, dknn, dqrn, dkrn = stage_bwd("l1_mla_prep_bwd", fn_mla_prep, (L // tp,), prep_ins,
                                                        [(dqf, hq_spec), (dkf, hq_spec), (dvh, hv_spec)], prep_diffs)
    dcqn = matmul("l1_uq_dx", dq, w_uq, "nt")
    env["dw_uq"] = matmul("l1_uq_dw", cqn, dq, "tn")
    dckvn = matmul("l1_ukv_dx", dkv, w1["mla_w_ukv"], "nt")
    env["g_ukv"] = matmul("l1_ukv_dw", ckvn, dkv, "tn", out=gw["mla_w_ukv"])
    dproj1, dqln = row_bwd("l1_q_lora_rms_bwd", fn_rms, L, ROW_TILE, [(proj1, QL, 4)], [qln], [(dcqn, QL, 0)],
                           [{"cols": MLA_IN_P, "cb": 4, "into": dproj1, "dtype": BF16}], [True])
    dproj1, dkvln = row_bwd("l1_kv_lora_rms_bwd", fn_rms, L, ROW_TILE, [(proj1, KVL, 12)], [kvln], [(dckvn, KVL, 0)],
                            [{"cols": MLA_IN_P, "cb": 12, "into": dproj1, "dtype": BF16}], [True])
    env["dw_in1"] = matmul("l1_in_dw", h1, dproj1, "tn")
    dh1 = matmul("l1_in_dx", dproj1, w_in1, "nt", plans=plans_for("l1_in_dx"))
    around("after", "l1_in_dx", dh1)
    dx1, dln1 = row_bwd("l1_rms_bwd", fn_rms, L, ROW_TILE, [(x1, D, 0)], [ln[1:2]], [(dh1, D, 0)], [{"add": (dx2, D, 0)}], [True])
    dx1p = time_permute(dx1)

    do0 = matmul("l0_out_dx", dx1p, w0["w_out"], "nt", plans=plans_for("l0_out_dx"))
    g_pair0 = matmul("l0_out_dw", o0, dx1p, "tn", out=gw["w_out"])
    dz0, dmo0, dproj0 = row_bwd("l0_merge_bwd", fn_merge_glu, L, ROW_TILE, [(z0, 2 * PW, 0), (mo0, XQW, 0), (proj0, BW, 1)], [],
                                [(do0, BW, 0)], [{"dtype": BF16}, True, {"cols": 2 * BW, "cb": 1, "dtype": BF16}], [])
    dproj0, dkn0, dv0, dxqn0 = mem_attn_bwd("l0", proj0, 3, kn0, kv0, xq_norm[0:1], dmo0, dproj0)
    env["g_pair0"], dmem_norm0, dxk0 = mem_branch_bwd("l0", mem, mem_norm[0:1], w0["w_mem_kv"], xk_norm[0:1], mn0, kv0, dkn0, dv0,
                                                      gw["w_mem_kv"], g_pair0)
    env["g_glu"] = matmul("l0_glu_dw", g0, dz0, "tn", out=gw["s5_w_glu"], plans=plans_for("l0_glu_dw"))
    dg0 = matmul("l0_glu_dx", dz0, w0["s5_w_glu"], "nt", plans=plans_for("l0_glu_dx"))
    around("before", "s5_backward", dg0)
    dproj0, dd, dwc_re, dwc_im, dwb_re, dwb_im, da_re, da_im = s5_backward(dg0, proj0, s_re, s_im, wb_re, wb_im, wc_re, wc_im,
                                                                           a_re_v, a_im_v, s5_d, dproj0, plans=plans_for("s5_backward"))
    around("after", "s5_backward", dd)
    env["g_in0"] = matmul("l0_in_dw", h0, dproj0, "tn", out=gw["s5_w_in"], plans=plans_for("l0_in_dw"))
    dh0 = matmul("l0_in_dx", dproj0, w_in0, "nt", plans=plans_for("l0_in_dx"))
    around("after", "l0_in_dx", dh0)
    dxp, dln0 = row_bwd("l0_rms_bwd", fn_rms, L, ROW_TILE, [(xp, D, 0)], [ln[0:1]], [(dh0, D, 0)], [{"add": (dx1p, D, 0)}], [True])
    grad_x = time_unpermute(dxp)

    db_re, db_im, dcoef_re, dcoef_im = row_bwd("s5_bmat_bwd", fn_s5_bmat, SN, 512, bmat_rows, [], [(dwb_re, 128, 0), (dwb_im, 128, 0)],
                                               [True] * 4, [], plans=plans_for("s5_bmat_bwd"))
    dc_re, dc_im = row_bwd("s5_cmat_bwd", fn_s5_cmat, PW, 128, cmat_rows, [], [(dwc_re, 512, 0), (dwc_im, 512, 0)], [True] * 2, [],
                           plans=plans_for("s5_cmat_bwd"))
    disc_cts = [(da_re.reshape(SG, SP), one), (da_im.reshape(SG, SP), one), (dcoef_re.reshape(SG, SP), one), (dcoef_im.reshape(SG, SP), one)]
    dlre, dlim, dls = stage_bwd("s5_disc_bwd", fn_s5_disc, (1,), disc_ins, disc_cts, [("acc", (0,))] * 3)

    g["ln_gain"] = jnp.concatenate([dln0, dln1], axis=0)
    g["mem_norm"] = jnp.concatenate([dmem_norm0, dmem_norm1], axis=0)
    g["xq_norm"] = jnp.concatenate([dxqn0, dxqn1], axis=0)
    g["xk_norm"] = jnp.concatenate([dxk0, dxk1], axis=0)
    g["s5_lambda_re"], g["s5_lambda_im"], g["s5_log_step"] = dlre, dlim, dls
    g["s5_b_re"], g["s5_b_im"], g["s5_c_re"], g["s5_c_im"] = db_re, db_im, dc_re, dc_im
    g["s5_d"] = dd
    g["mla_q_lora_norm"], g["mla_kv_lora_norm"] = dqln, dkvln
    g["mla_q_nope_norm"], g["mla_k_nope_norm"] = dqnn, dknn
    g["mla_q_rope_norm"] = dqrn[:, :ROPE] + dqrn[:, ROPE:]
    g["mla_k_rope_norm"] = dkrn[:, :ROPE] + dkrn[:, ROPE:]
    return loss, grad_x, g


def kernel(x, mem, positions, ln_gain, w_out, mem_norm, w_mem_kv, xq_norm, xk_norm, s5_w_in, s5_lambda_re, s5_lambda_im, s5_log_step, s5_b_re, s5_b_im, s5_c_re, s5_c_im, s5_d, s5_w_glu, mla_w_in, mla_q_lora_norm, mla_kv_lora_norm, mla_w_uq, mla_w_ukv, mla_q_nope_norm, mla_k_nope_norm, mla_q_rope_norm, mla_k_rope_norm, loss_target, m_ln_gain, m_w_out, m_mem_norm, m_w_mem_kv, m_xq_norm, m_xk_norm, m_s5_w_in, m_s5_lambda_re, m_s5_lambda_im, m_s5_log_step, m_s5_b_re, m_s5_b_im, m_s5_c_re, m_s5_c_im, m_s5_d, m_s5_w_glu, m_mla_w_in, m_mla_q_lora_norm, m_mla_kv_lora_norm, m_mla_w_uq, m_mla_w_ukv, m_mla_q_nope_norm, m_mla_k_nope_norm, m_mla_q_rope_norm, m_mla_k_rope_norm, v_ln_gain, v_w_out, v_mem_norm, v_w_mem_kv, v_xq_norm, v_xk_norm, v_s5_w_in, v_s5_lambda_re, v_s5_lambda_im, v_s5_log_step, v_s5_b_re, v_s5_b_im, v_s5_c_re, v_s5_c_im, v_s5_d, v_s5_w_glu, v_mla_w_in, v_mla_q_lora_norm, v_mla_kv_lora_norm, v_mla_w_uq, v_mla_w_ukv, v_mla_q_nope_norm, v_mla_k_nope_norm, v_mla_q_rope_norm, v_mla_k_rope_norm):
    args = dict(locals())
    wts = {n: args[n] for n in WEIGHT_ORDER}
    mom = {n: args["m_" + n] for n in WEIGHT_ORDER}
    var = {n: args["v_" + n] for n in WEIGHT_ORDER}

    chip = 2 * lax.axis_index("x") + lax.axis_index("y")
    place = jnp.stack([chip, lax.axis_index("c")]).astype(jnp.int32)

    def own_slot(gathered, shards):
        return [lax.dynamic_update_slice(g, s[None], (chip, 0, 0)) for g, s in zip(gathered, shards)]

    groups = list(stack_shards(wts, BF16))
    groups[2].append(jnp.concatenate([mla_q_lora_norm, jnp.pad(mla_kv_lora_norm, ((0, 0), (0, 64))), jnp.zeros((14, 128), F32)], axis=0))
    over_ici = [plan_gather_ici(shards) for shards in groups]
    _SCHEDULE_BEHIND.clear()
    schedule_behind(split_start("gather_start", over_ici))
    env, hooks, passed_on = {}, {}, {}

    def arrived(k, after, pass_now):
        passing = plan_gather_pass(split_wait(f"gather_wait_{k}", over_ici[k], after))
        passed_on[k] = passing
        return own_slot(run_plan(f"gather_pass_{k}", passing), groups[k]) if pass_now else None

    def need_in0(env, last):
        env["in0"], = arrived(0, last, True)

    def need_layer0(env, last):
        env["pair0"], env["glu"] = arrived(1, last, True)

    def need_layer1(env, last):
        arrived(2, last, False)

    def layer1_weights(env, last):
        pair1, ukv, in1, uq, norms = own_slot(passed_on[2].results, groups[2])
        env.update(pair1=pair1, ukv=ukv, w_in1=mla_in_permute(shards_to_cols(in1)), w_uq=uq_permute(shards_to_cols(uq)),
                   q_lora_norm=norms[:, 0, :], kv_lora_norm=norms[:, 1, :64])

    hooks["before", "l0_in"], hooks["before", "l0_glu"], hooks["before", "l0_out"] = need_in0, need_layer0, need_layer1
    hooks["plans", "l0_out"], hooks["after", "l0_out"] = (lambda env: [passed_on[2]]), layer1_weights

    rs = {}

    def swap(k, gs):
        rs[k, "g"], rs[k, "swap"] = gs, plan_pair_exchange(gs)
        return rs[k, "swap"]

    def start_scatter(k):
        rs[k, "pairs"] = pair_adds(f"rs{k}", rs[k, "g"], rs[k, "swap"].results, place)
        rs[k, "scatter"] = plan_chip_scatter(rs[k, "pairs"])
        schedule_behind(split_start(f"rs{k}_scatter_start", [rs[k, "scatter"]]))

    def join(k, after):
        rs[k, "join"] = plan_pair_join(chip_adds(f"rs{k}", rs[k, "pairs"], split_wait(f"rs{k}_scatter_wait", rs[k, "scatter"], after), place))
        return rs[k, "join"]

    hooks["plans", "l1_in_dx"] = lambda env: [swap(1, [env["g_pair1"], env["g_ukv"], cols_to_shards(mla_in_unpermute(env["dw_in1"])).astype(BF16),
                                                        cols_to_shards(uq_unpermute(env["dw_uq"])).astype(BF16)])]
    hooks["after", "l1_in_dx"] = lambda env, last: start_scatter(1)
    hooks["plans", "l0_glu_dx"] = lambda env: [swap(0, [env["g_pair0"], env["g_glu"]])]

    def before_s5_backward(env, last):
        start_scatter(0)
        env["join1"] = join(1, last)

    hooks["before", "s5_backward"] = before_s5_backward
    hooks["plans", "s5_backward"] = lambda env: [env["join1"]]
    hooks["after", "s5_backward"] = lambda env, last: env.update(join0=join(0, last))
    hooks["plans", "l0_in_dx"] = lambda env: [swap(2, [env["g_in0"]]), env["join0"]]
    hooks["after", "l0_in_dx"] = lambda env, last: start_scatter(2)

    def total_loss(env, local):
        env["loss"] = lax.psum(local[0, 0], MESH_AXES)
        schedule_behind(env["loss"].reshape(1, 1))

    hooks["after", "loss"] = total_loss
    small = {n: wts[n] for n, _ in SMALL}
    loss, grad_x, g = device_step(x[0], mem[0], positions[0], loss_target[0], small, env, hooks)
    loss = env["loss"]
    r_in0, = run_plan("rs2_pair_join", join(2, g["s5_log_step"]))
    (r_pair1, r_ukv, r_in1, r_uq), (r_pair0, r_glu) = (rs[k, "join"].results for k in (1, 0))

    small_flat = jnp.concatenate([g[n].reshape(-1) for n, _ in SMALL_FULL])
    g_small = jnp.pad(small_flat, (0, 4 * SMALL_ROWS * SMALL_LANES - N_SMALL)).astype(BF16).reshape(4, SMALL_ROWS, SMALL_LANES)
    r_small = reduce_scatter_chips("rs3", [g_small], place)[0]
    small_all = own_slot(all_gather_chips("gather_small_grads", [r_small]), [r_small])[0].reshape(-1)[:N_SMALL]

    grads = {"w_out": jnp.stack([r_pair0[:PAIR_MKV], r_pair1[:PAIR_MKV]]), "w_mem_kv": jnp.stack([r_pair0[PAIR_MKV:], r_pair1[PAIR_MKV:]]),
             "s5_w_in": r_in0[None], "s5_w_glu": r_glu[None], "mla_w_ukv": r_ukv[None], "mla_w_in": r_in1[None], "mla_w_uq": r_uq[None]}
    off = 0
    for n, s in SMALL_FULL:
        grads[n] = small_all[off:off + math.prod(s)].reshape(s)
        off += math.prod(s)
    for n, s in SHARDED_SMALL:
        grads[n] = lax.dynamic_slice(grads[n], (0, chip * s[1]), s)

    delta, new_m, new_v = {}, {}, {}
    for n, s in BIG + [(n, s) for n, s in SMALL if len(s) == 4]:
        perm = MINOR_LAST.get(n, tuple(range(len(s))))
        turned = tuple(s[p] for p in perm)
        view = lambda a: jnp.transpose(a, perm).reshape(-1, turned[-1])
        res = adamw("adamw_" + n, view(wts[n]), view(grads[n]), view(mom[n]), view(var[n]))
        delta[n], new_m[n], new_v[n] = (jnp.transpose(r.reshape(turned), tuple(perm.index(i) for i in range(len(s)))) for r in res)
    small_names = [n for n, s in SMALL if len(s) < 4] + [n for n, _ in SHARDED_SMALL]
    n_own = sum(wts[n].size for n in small_names)
    rows_own = -(-n_own // (8 * 128)) * 8

    def pack_small(d):
        flat = jnp.concatenate([d[n].reshape(-1) for n in small_names])
        return jnp.pad(flat, (0, rows_own * 128 - n_own), constant_values=1.0).reshape(rows_own, 128)

    res = adamw("adamw_small", pack_small(wts), pack_small(grads), pack_small(mom), pack_small(var))
    off = 0
    for n in small_names:
        size = wts[n].size
        delta[n], new_m[n], new_v[n] = (r.reshape(-1)[off:off + size].reshape(wts[n].shape) for r in res)
        off += size

    return (loss, grad_x[None], *[grads[n] for n in WEIGHT_ORDER], *[delta[n] for n in WEIGHT_ORDER],
            *[new_m[n] for n in WEIGHT_ORDER], *[new_v[n] for n in WEIGHT_ORDER])
```

```python
import functools
import math

import jax
import jax.numpy as jnp
from jax import lax
from jax.experimental import pallas as pl
from jax.experimental.pallas import tpu as pltpu

F32, BF16 = jnp.float32, jnp.bfloat16
SDS = jax.ShapeDtypeStruct

D = 1024
L = 2048
ML = 256
BW = 2 * D
XQW = BW // 4
PW = BW - XQW
XH, XHD = 4, 128
SG, SC, SP = 96, 16, 64
SN = SG * SP
NOPE, ROPE, VD = 128, 64, 128
MH = 12
QL, KVL = 512, 256
EPS = 1e-6
ROPE_THETA = 10000.0
MLA_IN = QL + KVL + ROPE + XQW + BW
MLA_IN_P = 3456
ADAM_LR, ADAM_B1, ADAM_B2, ADAM_EPS, ADAM_WD, ADAM_STEP = 0.001, 0.9, 0.999, 1e-08, 0.01, 10

VMEM_LIMIT = 48 * 2**20
ROW_TILE = 512
BMAT_TILE, CMAT_TILE = 2048, 512
SEG = 8
SEG_LEN = L // SEG
MESH_AXES = ("x", "y", "c")


def _cparams():
    return pltpu.CompilerParams(vmem_limit_bytes=VMEM_LIMIT)


def _dg(a, b, ca, cb):
    return lax.dot_general(a.astype(BF16), b.astype(BF16), (((ca,), (cb,)), ((), ())), preferred_element_type=F32)


@jax.custom_vjp
def mm_nn(a, b):
    return _dg(a, b, 1, 0)


mm_nn.defvjp(lambda a, b: (_dg(a, b, 1, 0), (a, b)), lambda res, g: (_dg(g, res[1], 1, 1), _dg(res[0], g, 0, 0)))


@jax.custom_vjp
def mm_nt(a, b):
    return _dg(a, b, 1, 1)


mm_nt.defvjp(lambda a, b: (_dg(a, b, 1, 1), (a, b)), lambda res, g: (_dg(g, res[1], 1, 0), _dg(g, res[0], 0, 0)))


@functools.partial(jax.custom_vjp, nondiff_argnums=(1,))
def lane_roll(x, shift):
    return pltpu.roll(x, shift, 1)


lane_roll.defvjp(lambda x, shift: (pltpu.roll(x, shift, 1), None),
                 lambda shift, _, g: (pltpu.roll(g, (128 - shift) % 128, 1),))


def rms(x, g):
    return x * lax.rsqrt(jnp.mean(x * x, axis=-1, keepdims=True) + EPS) * g


@jax.custom_vjp
def softmax_rows(s):
    e = jnp.exp(s - jnp.max(s, axis=-1, keepdims=True))
    return e / jnp.sum(e, axis=-1, keepdims=True)


def _softmax_rows_fwd(s):
    p = softmax_rows(s)
    return p, p


def _softmax_rows_bwd(p, g):
    return (p * (g - jnp.sum(g * p, axis=-1, keepdims=True)),)


softmax_rows.defvjp(_softmax_rows_fwd, _softmax_rows_bwd)


def silu(x):
    return x * jax.nn.sigmoid(x)


ANY = pl.BlockSpec(memory_space=pl.ANY)
MESH_ID = pl.DeviceIdType.MESH


def _dma_sems(n):
    return [pltpu.SemaphoreType.DMA((n,)), pltpu.SemaphoreType.DMA((n,))]


class Plan:
    def __init__(self, operands, out_shape, aliases, n_sems, copies):
        self.operands, self.out_shape, self.aliases, self.n_sems, self.copies = list(operands), list(out_shape), aliases, n_sems, copies
        self.results = None


_SCHEDULE_BEHIND = []


def schedule_behind(token):
    _SCHEDULE_BEHIND.append(token)


def hosted_call(kern, *, name, grid, in_specs, out_specs, out_shape, operands, scratch_shapes=(), aliases=None, cparams=None, plans=(), deps=()):
    n_in, n_out, n_scr = len(in_specs), len(out_specs), len(scratch_shapes)
    p_in, p_out = [len(p.operands) for p in plans], [len(p.out_shape) for p in plans]
    deps = tuple(deps) + tuple(_SCHEDULE_BEHIND)
    _SCHEDULE_BEHIND.clear()
    all_aliases = dict(aliases or {})
    in_off, out_off = n_in, n_out
    for p, ni, no in zip(plans, p_in, p_out):
        all_aliases.update({in_off + i: out_off + o for i, o in p.aliases.items()})
        in_off, out_off = in_off + ni, out_off + no

    def body(*refs):
        pos, pins, pouts = n_in, [], []
        for ni in p_in:
            pins.append(refs[pos:pos + ni])
            pos += ni
        pos += len(deps)
        main_out = refs[pos:pos + n_out]
        pos += n_out
        for no in p_out:
            pouts.append(refs[pos:pos + no])
            pos += no
        main_scr = refs[pos:pos + n_scr]
        pos += n_scr
        if plans:
            ids = [pl.program_id(ax) for ax in range(len(grid))]
            first = functools.reduce(jnp.logical_and, [i == 0 for i in ids])
            last = functools.reduce(jnp.logical_and, [i == g - 1 for i, g in zip(ids, grid)])
            copies = [p.copies(pins[k], pouts[k], refs[pos + 2 * k], refs[pos + 2 * k + 1]) for k, p in enumerate(plans)]

            @pl.when(first)
            def _():
                for sends, _ in copies:
                    for cp in sends:
                        cp.start()

        kern(*refs[:n_in], *main_out, *main_scr)
        if plans:
            @pl.when(last)
            def _():
                for sends, recvs in copies:
                    for cp in recvs:
                        cp.wait_recv()
                    for cp in sends:
                        cp.wait_send()

    res = pl.pallas_call(body, grid=grid, in_specs=list(in_specs) + [ANY] * (sum(p_in) + len(deps)),
                         out_specs=list(out_specs) + [ANY] * sum(p_out), out_shape=list(out_shape) + [s for p in plans for s in p.out_shape],
                         scratch_shapes=list(scratch_shapes) + [s for p in plans for s in _dma_sems(p.n_sems)],
                         input_output_aliases=all_aliases, name=name, compiler_params=cparams or _cparams())(
        *operands, *[a for p in plans for a in p.operands], *deps)
    pos = n_out
    for p, no in zip(plans, p_out):
        p.results = list(res[pos:pos + no])
        pos += no
    return list(res[:n_out])


def _wide(v):
    return v.astype(F32) if v.dtype == BF16 else v


def stage(name, fn, grid, ins, outs):
    n_in = len(ins)

    def kern(*refs):
        res = fn(*[_wide(r[...]) for r in refs[:n_in]])
        for r, v in zip(refs[n_in:], res):
            r[...] = v.astype(r.dtype)

    return hosted_call(kern, name=name, grid=grid, in_specs=[s for _, s in ins], out_specs=[s for _, s in outs],
                       out_shape=[sd for sd, _ in outs], operands=[a for a, _ in ins])


def stage_bwd(name, fn, grid, ins, cts, diffs, plans=()):
    n_in, n_ct = len(ins), len(cts)
    didx = [i for i, d in enumerate(diffs) if d is not None]
    opts = {i: (diffs[i][3] if len(diffs[i]) > 3 else {}) for i in didx if diffs[i][0] == "row"}
    adds = [(i, opts[i]["add"]) for i in opts if "add" in opts[i]]
    intos = [(i, opts[i]["into"]) for i in opts if "into" in opts[i]]
    n_add, n_into = len(adds), len(intos)
    add_pos = {i: n_in + n_ct + k for k, (i, _) in enumerate(adds)}
    n_extra = n_in + n_ct + n_add + n_into

    def kern(*refs):
        vals = [_wide(r[...]) for r in refs[:n_in]]

        def f(*dv):
            full = list(vals)
            for i, v in zip(didx, dv):
                full[i] = v
            return fn(*full)

        _, vjp = jax.vjp(f, *[vals[i].astype(F32) for i in didx])
        gs = vjp(tuple(c[...].astype(F32) for c in refs[n_in:n_in + n_ct]))
        for o_ref, i, g in zip(refs[n_extra:], didx, gs):
            if diffs[i][0] == "row":
                if i in add_pos:
                    g = g + refs[add_pos[i]][...].astype(F32)
                o_ref[...] = g.astype(o_ref.dtype)
            else:
                first = functools.reduce(jnp.logical_and, [pl.program_id(ax) == 0 for ax in diffs[i][1]])

                @pl.when(first)
                def _():
                    o_ref[...] = g

                @pl.when(jnp.logical_not(first))
                def _():
                    o_ref[...] += g

    out_shape, out_specs = [], []
    for i in didx:
        if diffs[i][0] == "row":
            out_shape.append(diffs[i][1])
            out_specs.append(diffs[i][2])
        else:
            out_shape.append(SDS(ins[i][0].shape, F32))
            out_specs.append(ins[i][1])
    aliases = {n_in + n_ct + n_add + k: didx.index(i) for k, (i, _) in enumerate(intos)}
    in_specs = [s for _, s in ins] + [s for _, s in cts] + [s for _, (_, s) in adds] + [ANY] * n_into
    operands = [a for a, _ in ins] + [a for a, _ in cts] + [a for _, (a, _) in adds] + [a for _, a in intos]
    return hosted_call(kern, name=name, grid=grid, in_specs=in_specs, out_specs=out_specs, out_shape=out_shape, operands=operands,
                       aliases=aliases, plans=plans)


def rspec(tl, w, cb=0):
    return pl.BlockSpec((tl, w), lambda i: (i, cb))


def cspec(shape):
    return pl.BlockSpec(shape, lambda i: (0,) * len(shape))


def row_fwd(name, fn, rows, tl, row_ins, consts, outs):
    ins = [(a, rspec(tl, w, cb)) for a, w, cb in row_ins] + [(a, cspec(a.shape)) for a in consts]
    return stage(name, fn, (rows // tl,), ins, [(SDS((rows, w), dt), rspec(tl, w)) for w, dt in outs])


def row_bwd(name, fn, rows, tl, row_ins, consts, cts, row_diff, const_diff, plans=()):
    ins = [(a, rspec(tl, w, cb)) for a, w, cb in row_ins] + [(a, cspec(a.shape)) for a in consts]
    diffs = []
    for (a, w, cb), d in zip(row_ins, row_diff):
        if not d:
            diffs.append(None)
            continue
        d = d if isinstance(d, dict) else {}
        opts = {}
        if "add" in d:
            opts["add"] = (d["add"][0], rspec(tl, d["add"][1], d["add"][2]))
        if d.get("into") is not None:
            opts["into"] = d["into"]
        diffs.append(("row", SDS((rows, d.get("cols", w)), d.get("dtype", F32)), rspec(tl, w, d.get("cb", 0)), opts))
    diffs += [("acc", (0,)) if d else None for d in const_diff]
    return stage_bwd(name, fn, (rows // tl,), ins, [(a, rspec(tl, w, cb)) for a, w, cb in cts], diffs, plans=plans)


MATMUL_VMEM = 36 * 2**20
ADAMW_VMEM = 28 * 2**20


class Sharded:
    def __init__(self, arr, kind, roff, rows):
        self.arr, self.kind, self.roff, self.rows, self.n = arr, kind, roff, rows, arr.shape[2]
        self.shape = (rows, 4 * self.n) if kind == "col" else (4 * rows, self.n)

    def fits(self, t0, t1):
        return self.roff % t0 == 0 and self.rows % t0 == 0 and self.n % t1 == 0

    def spec(self, t0, t1, bidx):
        assert self.fits(t0, t1), (self.kind, self.roff, self.rows, self.n, t0, t1)
        r0 = self.roff // t0
        if self.kind == "col":
            per = self.n // t1
            return pl.BlockSpec((None, t0, t1), lambda *g: (bidx(*g)[1] // per, r0 + bidx(*g)[0], bidx(*g)[1] % per))
        per = self.rows // t0
        return pl.BlockSpec((None, t0, t1), lambda *g: (bidx(*g)[0] // per, r0 + bidx(*g)[0] % per, bidx(*g)[1]))


def matmul(name, a, b, mode, out_dtype=BF16, add=None, out=None, into=None, plans=()):
    if mode == "tn":
        k_dim, m = a.shape
    else:
        m, k_dim = a.shape
    n = b.shape[0] if mode == "nt" else b.shape[1]
    b_fit = b.fits if isinstance(b, Sharded) else (lambda t0, t1: True)
    o_fit = out.fits if out is not None else (lambda t0, t1: True)
    a_bytes, b_bytes = jnp.dtype(a.dtype).itemsize, jnp.dtype(b.arr.dtype if isinstance(b, Sharded) else b.dtype).itemsize
    o_bytes = jnp.dtype(out_dtype if out is None else out.arr.dtype).itemsize

    def vmem(tm, tn, tk):
        return 2 * (tm * tk * a_bytes + tk * tn * b_bytes + tm * tn * (o_bytes + (4 if add is not None else 0))) + 4 * tm * tn * (1 + (tk < k_dim))

    tiles = [(tm, tn, tk) for tm in (2048, 1024, 512, 256, 128) for tn in (1024, 768, 512, 384, 256, 128)
             for tk in sorted({k_dim, 1024, 768, 512, 384, 256, 128})
             if m % tm == 0 and n % tn == 0 and k_dim % tk == 0 and (b_fit(tn, tk) if mode == "nt" else b_fit(tk, tn)) and o_fit(tm, tn)
             and vmem(tm, tn, tk) <= MATMUL_VMEM]
    tm, tn, tk = max(tiles, key=lambda t: (t[2] == k_dim, t[0] * t[1] * t[2], t[0] * t[1]))
    nk = k_dim // tk
    a_spec = pl.BlockSpec((tk, tm), lambda i, j, k: (k, i)) if mode == "tn" else pl.BlockSpec((tm, tk), lambda i, j, k: (i, k))
    if isinstance(b, Sharded):
        b_spec = b.spec(tn, tk, lambda i, j, k: (j, k)) if mode == "nt" else b.spec(tk, tn, lambda i, j, k: (k, j))
        b = b.arr
    else:
        b_spec = pl.BlockSpec((tn, tk), lambda i, j, k: (j, k)) if mode == "nt" else pl.BlockSpec((tk, tn), lambda i, j, k: (k, j))
    o_spec = pl.BlockSpec((tm, tn), lambda i, j, k: (i, j))
    out_spec, out_shape = (o_spec, SDS((m, n), out_dtype)) if out is None else (out.spec(tm, tn, lambda i, j, k: (i, j)), out.arr)
    ca, cb = {"nn": (1, 0), "nt": (1, 1), "tn": (0, 0)}[mode]
    n_in = 2 + (add is not None)

    def finish(refs, o_ref, r):
        if add is not None:
            r = r + refs[2][...]
        o_ref[...] = r.astype(o_ref.dtype)

    def kern_whole(*refs):
        finish(refs, refs[-1], _dg(refs[0][...], refs[1][...], ca, cb))

    def kern_cut(*refs):
        o_ref, acc = refs[-2], refs[-1]
        k = pl.program_id(2)

        @pl.when(k == 0)
        def _():
            acc[...] = jnp.zeros_like(acc)

        acc[...] += _dg(refs[0][...], refs[1][...], ca, cb)

        @pl.when(k == nk - 1)
        def _():
            finish(refs, o_ref, acc[...])

    ins, specs = [a, b], [a_spec, b_spec]
    if add is not None:
        ins.append(add)
        specs.append(o_spec)
    if into is not None:
        ins.append(into)
        specs.append(ANY)
    return hosted_call(kern_whole if nk == 1 else kern_cut, name=name, grid=(m // tm, n // tn, nk), in_specs=specs, out_specs=[out_spec],
                       out_shape=[out_shape], operands=ins, scratch_shapes=[] if nk == 1 else [pltpu.VMEM((tm, tn), F32)],
                       aliases={} if into is None else {n_in: 0}, plans=plans)[0]


SCAN_UNROLL = 8


def _cmul(ar, ai, br, bi):
    return ar * br - ai * bi, ar * bi + ai * br


def _sub_shift(x, down):
    row = lax.broadcasted_iota(jnp.int32, x.shape, 0)
    if down:
        return jnp.where(row == 0, 0.0, pltpu.roll(x, 1, 0))
    return jnp.where(row == SEG - 1, 0.0, pltpu.roll(x, SEG - 1, 0))


def _pow_seg_len(ar, ai):
    for _ in range(int(math.log2(SEG_LEN))):
        ar, ai = _cmul(ar, ai, ar, ai)
    return ar, ai


def _scan_in_place(sr, si, a_re, a_im):
    lanes = sr.shape[1]
    ar = jnp.broadcast_to(a_re, (SEG, lanes))
    ai = jnp.broadcast_to(a_im, (SEG, lanes))
    zero = jnp.zeros((SEG, lanes), F32)

    def local(i, carry):
        rows = pl.ds(pl.multiple_of(i * SEG, SEG), SEG)
        mr, mi = _cmul(ar, ai, carry[0], carry[1])
        nr, ni = mr + sr[rows, :], mi + si[rows, :]
        sr[rows, :] = nr
        si[rows, :] = ni
        return nr, ni

    fr, fi = lax.fori_loop(0, SEG_LEN, local, (zero, zero), unroll=SCAN_UNROLL)
    pr, pi = _pow_seg_len(ar, ai)
    ir, ii = zero, zero
    for _ in range(SEG - 1):
        mr, mi = _cmul(pr, pi, ir, ii)
        ir, ii = _sub_shift(mr + fr, True), _sub_shift(mi + fi, True)

    def carry_in(i, pw):
        rows = pl.ds(pl.multiple_of(i * SEG, SEG), SEG)
        cr, ci = _cmul(pw[0], pw[1], ir, ii)
        sr[rows, :] += cr
        si[rows, :] += ci
        return _cmul(pw[0], pw[1], ar, ai)

    lax.fori_loop(0, SEG_LEN, carry_in, (ar, ai), unroll=SCAN_UNROLL)


S5_LANES = 8 * SP
S5_BLOCKS = SN // S5_LANES


def _s5_specs():
    u_spec = pl.BlockSpec((L, 8 * SC), lambda j: (0, j))
    s_spec = pl.BlockSpec((L, S5_LANES), lambda j: (0, j))
    wb_spec = pl.BlockSpec((S5_LANES, 8 * SC), lambda j: (j, 0))
    wc_spec = pl.BlockSpec((8 * SC, S5_LANES), lambda j: (j, 0))
    a_spec = pl.BlockSpec((1, S5_LANES), lambda j: (0, j))
    d_spec = pl.BlockSpec((1, 8 * SC), lambda j: (0, j))
    return u_spec, s_spec, wb_spec, wc_spec, a_spec, d_spec


def s5_forward(proj, wb_re, wb_im, wc_re, wc_im, a_re, a_im, d, plans=()):
    def kern(u_ref, wbr, wbi, wcr, wci, ar, ai, d_ref, sr_out, si_out, g_ref, sr, si):
        u = _wide(u_ref[...])
        sr[...], si[...] = fn_s5_bu(u, wbr[...], wbi[...])
        _scan_in_place(sr, si, ar[...], ai[...])
        g_ref[...] = fn_s5_out(sr[...], si[...], u, d_ref[...], wcr[...], wci[...])[0].astype(g_ref.dtype)
        sr_out[...] = sr[...].astype(sr_out.dtype)
        si_out[...] = si[...].astype(si_out.dtype)

    u_spec, s_spec, wb_spec, wc_spec, a_spec, d_spec = _s5_specs()
    return hosted_call(kern, name="s5_forward", grid=(S5_BLOCKS,), in_specs=[u_spec, wb_spec, wb_spec, wc_spec, wc_spec, a_spec, a_spec, d_spec],
                       out_specs=[s_spec, s_spec, u_spec], out_shape=[SDS((L, SN), BF16)] * 2 + [SDS((L, PW), BF16)],
                       operands=[proj, wb_re, wb_im, wc_re, wc_im, a_re, a_im, d], scratch_shapes=[pltpu.VMEM((L, S5_LANES), F32)] * 2,
                       plans=plans)


def _adjoint_scan_in_place(lr, li, sr, si, a_re, a_im):
    lanes = lr.shape[1]
    ar = jnp.broadcast_to(a_re, (SEG, lanes))
    ai = -jnp.broadcast_to(a_im, (SEG, lanes))
    zero = jnp.zeros((SEG, lanes), F32)

    def local(k, carry):
        i = SEG_LEN - 1 - k
        rows = pl.ds(pl.multiple_of(i * SEG, SEG), SEG)
        mr, mi = _cmul(ar, ai, carry[0], carry[1])
        nr, ni = mr + lr[rows, :], mi + li[rows, :]
        lr[rows, :] = nr
        li[rows, :] = ni
        return nr, ni

    fr, fi = lax.fori_loop(0, SEG_LEN, local, (zero, zero), unroll=SCAN_UNROLL)
    pr, pi = _pow_seg_len(ar, ai)
    ir, ii = zero, zero
    for _ in range(SEG - 1):
        mr, mi = _cmul(pr, pi, ir, ii)
        ir, ii = _sub_shift(mr + fr, False), _sub_shift(mi + fi, False)

    def fix(rows, pw):
        cr, ci = _cmul(pw[0], pw[1], ir, ii)
        tr, ti = lr[rows, :] + cr, li[rows, :] + ci
        lr[rows, :] = tr
        li[rows, :] = ti
        return tr, ti

    def grad_a(tr, ti, spr, spi, acc):
        return acc[0] + tr * spr + ti * spi, acc[1] + ti * spr - tr * spi

    def carry_in(k, c):
        i = SEG_LEN - 1 - k
        rows = pl.ds(pl.multiple_of(i * SEG, SEG), SEG)
        prev = pl.ds(pl.multiple_of((i - 1) * SEG, SEG), SEG)
        tr, ti = fix(rows, (c[0], c[1]))
        acc = grad_a(tr, ti, sr[prev, :], si[prev, :], (c[2], c[3]))
        nr, ni = _cmul(c[0], c[1], ar, ai)
        return nr, ni, acc[0], acc[1]

    pwr, pwi, accr, acci = lax.fori_loop(0, SEG_LEN - 1, carry_in, (ar, ai, zero, zero), unroll=5)
    tr, ti = fix(pl.ds(0, SEG), (pwr, pwi))
    last = pl.ds((SEG_LEN - 1) * SEG, SEG)
    accr, acci = grad_a(tr, ti, _sub_shift(sr[last, :], True), _sub_shift(si[last, :], True), (accr, acci))
    return jnp.sum(accr, axis=0, keepdims=True), jnp.sum(acci, axis=0, keepdims=True)


S5_BWD_VMEM = 58 * 2**20


def s5_backward(dg, proj, s_re, s_im, wb_re, wb_im, wc_re, wc_im, a_re, a_im, d, dproj, plans=()):
    def kern(dg_ref, u_ref, sr_in, si_in, wbr, wbi, wcr, wci, ar, ai, d_ref, _, du_ref, dd_ref, dwcr, dwci, dwbr, dwbi, dar, dai, lr, li, sr, si):
        u = _wide(u_ref[...])
        sr[...], si[...] = _wide(sr_in[...]), _wide(si_in[...])
        _, vjp_out = jax.vjp(fn_s5_out, sr[...], si[...], u, d_ref[...], wcr[...], wci[...])
        lr[...], li[...], du_out, dd_ref[...], dwcr[...], dwci[...] = vjp_out((_wide(dg_ref[...]),))
        dar[...], dai[...] = _adjoint_scan_in_place(lr, li, sr, si, ar[...], ai[...])
        _, vjp_in = jax.vjp(fn_s5_bu, u, wbr[...], wbi[...])
        du_in, dwbr[...], dwbi[...] = vjp_in((lr[...], li[...]))
        du_ref[...] = (du_out + du_in).astype(du_ref.dtype)

    u_spec, s_spec, wb_spec, wc_spec, a_spec, d_spec = _s5_specs()
    outs = [(SDS(dproj.shape, dproj.dtype), u_spec), (SDS(d.shape, F32), d_spec), (SDS(wc_re.shape, F32), wc_spec), (SDS(wc_im.shape, F32), wc_spec),
            (SDS(wb_re.shape, F32), wb_spec), (SDS(wb_im.shape, F32), wb_spec), (SDS(a_re.shape, F32), a_spec), (SDS(a_im.shape, F32), a_spec)]
    return hosted_call(kern, name="s5_backward", grid=(S5_BLOCKS,),
                       in_specs=[u_spec, u_spec, s_spec, s_spec, wb_spec, wb_spec, wc_spec, wc_spec, a_spec, a_spec, d_spec, ANY],
                       out_specs=[sp for _, sp in outs], out_shape=[sd for sd, _ in outs], aliases={11: 0},
                       operands=[dg, proj, s_re, s_im, wb_re, wb_im, wc_re, wc_im, a_re, a_im, d, dproj],
                       scratch_shapes=[pltpu.VMEM((L, S5_LANES), F32)] * 4,
                       cparams=pltpu.CompilerParams(vmem_limit_bytes=S5_BWD_VMEM), plans=plans)


def fn_rms(x, g):
    return (rms(x, g),)


def fn_s5_disc(lre, lim, ls):
    step = jnp.exp(ls)
    e = jnp.exp(lre * step)
    a_re, a_im = e * jnp.cos(lim * step), e * jnp.sin(lim * step)
    den = lre * lre + lim * lim
    nr, ni = a_re - 1.0, a_im
    return a_re, a_im, (nr * lre + ni * lim) / den, (ni * lre - nr * lim) / den


def _group_mask(rows, cols, row_div, col_div):
    r = lax.broadcasted_iota(jnp.int32, (rows, cols), 0) // row_div % 8
    c = lax.broadcasted_iota(jnp.int32, (rows, cols), 1) // col_div
    return r == c


def _spread(x, mask):
    w = x.shape[1]
    copy = (lax.broadcasted_iota(jnp.int32, (w, 8 * w), 1) % w == lax.broadcasted_iota(jnp.int32, (w, 8 * w), 0)).astype(F32)
    return jnp.where(mask, jnp.dot(x, copy, precision=lax.Precision.HIGHEST, preferred_element_type=F32), 0.0)


def fn_s5_bmat(b_re, b_im, coef_re, coef_im):
    mask = _group_mask(b_re.shape[0], 8 * SC, SP, SC)
    return _spread(coef_re * b_re - coef_im * b_im, mask), _spread(coef_re * b_im + coef_im * b_re, mask)


def fn_s5_cmat(c_re, c_im):
    mask = _group_mask(c_re.shape[0], 8 * SP, SC, SP)
    return _spread(c_re, mask), _spread(c_im, mask)


def fn_s5_bu(u, wb_re, wb_im):
    return mm_nt(u, wb_re), mm_nt(u, wb_im)


def fn_s5_out(sr, si, u, d, wc_re, wc_im):
    y = mm_nt(sr, wc_re) - mm_nt(si, wc_im) + d * u
    return (jax.nn.gelu(y),)


def fn_merge_glu(z, mo, gate):
    yg = z[:, :PW] * jax.nn.sigmoid(z[:, PW:])
    return (jnp.concatenate([yg, mo], axis=1) * silu(gate),)


def fn_merge(prim, mo, gate):
    return (jnp.concatenate([prim, mo], axis=1) * silu(gate),)


def fn_mem_k(kv, g):
    return (jnp.concatenate([rms(kv[:, h * XHD:(h + 1) * XHD], g) for h in range(XH)], axis=1),)


def fn_mem_attn(xq, kn, v, g):
    outs = []
    for h in range(XH):
        sl = slice(h * XHD, (h + 1) * XHD)
        p = softmax_rows(mm_nt(rms(xq[:, sl], g), kn[:, sl]) * (XHD ** -0.5))
        outs.append(mm_nn(p, v[:, sl]))
    return (jnp.concatenate(outs, axis=1),)


def _half_rms(x, g):
    lo = lax.broadcasted_iota(jnp.int32, x.shape, 1) < ROPE
    x2 = x * x
    s_lo = jnp.sum(jnp.where(lo, x2, 0.0), axis=1, keepdims=True)
    s_hi = jnp.sum(jnp.where(lo, 0.0, x2), axis=1, keepdims=True)
    return x * lax.rsqrt(jnp.where(lo, s_lo, s_hi) / ROPE + EPS) * g


def _rope(x, cos2, sin_signed):
    first = lax.broadcasted_iota(jnp.int32, x.shape, 1) % ROPE < ROPE // 2
    return x * cos2 + jnp.where(first, lane_roll(x, 128 - ROPE // 2), lane_roll(x, ROPE // 2)) * sin_signed


def fn_mla_prep(q, kv, kr, cos2, sin_signed, qnn, knn, qrn, krn):
    lo = lax.broadcasted_iota(jnp.int32, kr.shape, 1) < ROPE
    kr_pad = jnp.where(lo, _rope(_half_rms(kr, krn), cos2, sin_signed), 0.0)
    qf, kf, vs = [], [], []
    for m in range(MH // 2):
        pair = _rope(_half_rms(q[:, MH * NOPE + 128 * m:MH * NOPE + 128 * (m + 1)], qrn), cos2, sin_signed)
        for h, rope_h in ((2 * m, pair), (2 * m + 1, lane_roll(pair, ROPE))):
            qf.append(jnp.concatenate([rms(q[:, NOPE * h:NOPE * (h + 1)], qnn), jnp.where(lo, rope_h, 0.0)], axis=1))
    for h in range(MH):
        kf.append(jnp.concatenate([rms(kv[:, 256 * h:256 * h + NOPE], knn), kr_pad], axis=1))
        vs.append(kv[:, 256 * h + NOPE:256 * (h + 1)])
    return jnp.stack(qf), jnp.stack(kf), jnp.stack(vs)


ATT_TQ = 512


def _attn_scores(q, kf):
    tq = q.shape[0]
    scale = (NOPE + ROPE) ** -0.5
    own = _dg(q, kf[-tq:], 1, 1) * scale
    own = jnp.where(lax.broadcasted_iota(jnp.int32, own.shape, 1) <= lax.broadcasted_iota(jnp.int32, own.shape, 0), own, jnp.finfo(F32).min)
    return own if kf.shape[0] == tq else jnp.concatenate([_dg(q, kf[:-tq], 1, 1) * scale, own], axis=1)


ATT_FWD_HEADS, ATT_BWD_HEADS = 4, 2


def _attn_specs(heads):
    q_spec = pl.BlockSpec((heads, ATT_TQ, 256), lambda h, i: (h, i, 0))
    k_spec = pl.BlockSpec((heads, L, 256), lambda h, i: (h, 0, 0))
    v_spec = pl.BlockSpec((heads, L, 128), lambda h, i: (h, 0, 0))
    o_spec = pl.BlockSpec((ATT_TQ, heads * 128), lambda h, i: (i, h))
    lse_spec = pl.BlockSpec((heads, ATT_TQ, 1), lambda h, i: (h, i, 0))
    return q_spec, k_spec, v_spec, o_spec, lse_spec


def _attn_branches(heads, body):
    i = pl.program_id(1)
    for t in range(L // ATT_TQ):
        @pl.when(i == t)
        def _(t=t):
            for hh in range(heads):
                body(hh, slice(hh * 128, (hh + 1) * 128), (t + 1) * ATT_TQ)


def causal_attn(qf, kf, vh):
    def kern(q_ref, k_ref, v_ref, o_ref, lse_ref):
        def body(hh, lanes, keys):
            s = _attn_scores(q_ref[hh], k_ref[hh, :keys, :])
            m = jnp.max(s, axis=-1, keepdims=True)
            e = jnp.exp(s - m)
            total = jnp.sum(e, axis=-1, keepdims=True)
            o_ref[:, lanes] = (_dg(e, v_ref[hh, :keys, :], 1, 0) / total).astype(o_ref.dtype)
            lse_ref[hh] = m + jnp.log(total)

        _attn_branches(ATT_FWD_HEADS, body)

    q_spec, k_spec, v_spec, o_spec, lse_spec = _attn_specs(ATT_FWD_HEADS)
    return pl.pallas_call(kern, grid=(MH // ATT_FWD_HEADS, L // ATT_TQ), in_specs=[q_spec, k_spec, v_spec], out_specs=[o_spec, lse_spec],
                          out_shape=[SDS((L, MH * VD), F32), SDS((MH, L, 1), F32)], name="l1_attn", compiler_params=_cparams())(qf, kf, vh)


def causal_attn_bwd(qf, kf, vh, out, lse, dout):
    scale = (NOPE + ROPE) ** -0.5

    def kern(q_ref, k_ref, v_ref, o_ref, lse_ref, do_ref, dq_ref, dk_ref, dv_ref):
        @pl.when(pl.program_id(1) == 0)
        def _():
            dk_ref[...] = jnp.zeros_like(dk_ref)
            dv_ref[...] = jnp.zeros_like(dv_ref)

        def body(hh, lanes, keys):
            q, k, v, do = q_ref[hh], k_ref[hh, :keys, :], v_ref[hh, :keys, :], do_ref[:, lanes]
            p = jnp.exp(_attn_scores(q, k) - lse_ref[hh])
            delta = jnp.sum(do * _wide(o_ref[:, lanes]), axis=-1, keepdims=True)
            dv_ref[hh, :keys, :] += _dg(p, do, 0, 0)
            ds = p * (_dg(do, v, 1, 1) - delta) * scale
            dq_ref[hh] = _dg(ds, k, 1, 0)
            dk_ref[hh, :keys, :] += _dg(ds, q, 0, 0)

        _attn_branches(ATT_BWD_HEADS, body)

    q_spec, k_spec, v_spec, o_spec, lse_spec = _attn_specs(ATT_BWD_HEADS)
    return pl.pallas_call(kern, grid=(MH // ATT_BWD_HEADS, L // ATT_TQ), in_specs=[q_spec, k_spec, v_spec, o_spec, lse_spec, o_spec],
                          out_specs=[q_spec, k_spec, v_spec], out_shape=[SDS(qf.shape, F32), SDS(kf.shape, F32), SDS(vh.shape, F32)],
                          name="l1_attn_bwd", compiler_params=_cparams())(qf, kf, vh, out, lse, dout)


def loss_and_grad(y, target, tl=512):
    def kern(y_ref, t_ref, dy_ref, loss_ref):
        d = y_ref[...] - t_ref[...]
        dy_ref[...] = d / D

        @pl.when(pl.program_id(0) == 0)
        def _():
            loss_ref[...] = jnp.zeros_like(loss_ref)

        loss_ref[...] += 0.5 * jnp.sum(jnp.sum(d * d, axis=1, keepdims=True), axis=0, keepdims=True) / D

    return pl.pallas_call(kern, grid=(L // tl,), in_specs=[rspec(tl, D), rspec(tl, D)], out_specs=[rspec(tl, D), cspec((1, 1))],
                          out_shape=[SDS((L, D), F32), SDS((1, 1), F32)], name="loss", compiler_params=_cparams())(y, target)


def adamw(name, w, g, m, v):
    rows, cols = w.shape
    block_row_bytes = 7 * 2 * 4 * max(cols, 128)
    tr = _row_tile(rows, min(2048, ADAMW_VMEM // block_row_bytes // 8 * 8), 8)

    def kern(w_ref, g_ref, m_ref, v_ref, d_ref, nm_ref, nv_ref):
        gg = g_ref[...]
        nm = ADAM_B1 * m_ref[...] + (1.0 - ADAM_B1) * gg
        nv = ADAM_B2 * v_ref[...] + (1.0 - ADAM_B2) * jnp.square(gg)
        m_hat = nm / (1.0 - ADAM_B1 ** ADAM_STEP)
        v_hat = nv / (1.0 - ADAM_B2 ** ADAM_STEP)
        d_ref[...] = -ADAM_LR * (m_hat / (jnp.sqrt(v_hat) + ADAM_EPS) + ADAM_WD * w_ref[...])
        nm_ref[...] = nm
        nv_ref[...] = nv

    spec = rspec(tr, cols)
    return hosted_call(kern, name=name, grid=(rows // tr,), in_specs=[spec] * 4, out_specs=[spec] * 3,
                       out_shape=[SDS((rows, cols), F32)] * 3, operands=[w, g, m, v])


def _row_tile(rows, cap=512, unit=16):
    return max(t for t in range(unit, cap + 1, unit) if rows % t == 0)


def _place():
    x, y, c = lax.axis_index("x"), lax.axis_index("y"), lax.axis_index("c")
    return x, y, c, [(1 - x, y), (x, 1 - y), (1 - x, 1 - y)]


def _row_chunks(rows, n, dtype):
    unit = 32 // jnp.dtype(dtype).itemsize
    base, extra = divmod(rows // unit, n)
    out, start = [], 0
    for k in range(n):
        size = (base + (k < extra)) * unit
        if size:
            out.append((start, size))
            start += size
    assert start == rows, (rows, unit)
    return out


PIECE_BYTES = 1 << 20


def _pieces(shapes_dtypes, rows_of):
    out = []
    for b, (shape, dtype) in enumerate(shapes_dtypes):
        rows = rows_of(shape)
        n = max(1, min(4, rows * shape[-1] * jnp.dtype(dtype).itemsize // PIECE_BYTES))
        out += [(b, st, sz) for st, sz in _row_chunks(rows, n, dtype)]
    return out


def all_gather_chips(name, shards):
    nb = len(shards)
    pieces = _pieces([(s.shape, s.dtype) for s in shards], lambda shape: shape[0] // 2)
    n = len(pieces)

    def body(*refs):
        x_refs, out_refs, send_sems, recv_sems = refs[:nb], refs[nb:2 * nb], refs[2 * nb], refs[2 * nb + 1]
        x, y, c, chips = _place()
        sibling = (x, y, 1 - c)
        mine = 2 * x + y

        def copy(sem, chip, cc, k, to, from_input=False):
            b, st, sz = pieces[k]
            rows_k = pl.ds(cc * (x_refs[b].shape[0] // 2) + st, sz)
            dst = out_refs[b].at[chip, rows_k, :]
            return pltpu.make_async_remote_copy(src_ref=x_refs[b].at[rows_k, :] if from_input else dst, dst_ref=dst,
                                                send_sem=send_sems.at[sem], recv_sem=recv_sems.at[sem], device_id=to, device_id_type=MESH_ID)

        order = [(k, j, 2 * cx + cy, (cx, cy, c)) for k in range(n) for j, (cx, cy) in enumerate(chips)]
        first = [copy(j * n + k, mine, c, k, to, from_input=True) for k, j, _, to in order]
        for cp in first:
            cp.start()
        passed = []
        for k, j, chip, _ in order:
            copy(j * n + k, chip, c, k, sibling).wait_recv()
            passed.append(copy((3 + j) * n + k, chip, c, k, sibling))
            passed[-1].start()
        for k, j, chip, _ in order:
            copy((3 + j) * n + k, chip, 1 - c, k, sibling).wait_recv()
        for cp in first + passed:
            cp.wait_send()

    return pl.pallas_call(body, in_specs=[ANY] * nb, out_specs=[ANY] * nb, out_shape=[SDS((4,) + s.shape, s.dtype) for s in shards],
                          scratch_shapes=_dma_sems(6 * n), name=name)(*shards)


def plan_gather_ici(shards):
    pieces = _pieces([(s.shape, s.dtype) for s in shards], lambda shape: shape[0] // 2)
    n = len(pieces)

    def copies(x_refs, out_refs, send_sems, recv_sems):
        x, y, c, chips = _place()
        mine = 2 * x + y

        def copy(j, k, chip, to, from_input):
            b, st, sz = pieces[k]
            rows_k = pl.ds(c * (x_refs[b].shape[0] // 2) + st, sz)
            dst = out_refs[b].at[chip, rows_k, :]
            return pltpu.make_async_remote_copy(src_ref=x_refs[b].at[rows_k, :] if from_input else dst, dst_ref=dst, send_sem=send_sems.at[j * n + k],
                                                recv_sem=recv_sems.at[j * n + k], device_id=to, device_id_type=MESH_ID)

        order = [(k, j, 2 * cx + cy, (cx, cy, c)) for k in range(n) for j, (cx, cy) in enumerate(chips)]
        return [copy(j, k, mine, to, True) for k, j, _, to in order], [copy(j, k, chip, to, False) for k, j, chip, to in order]

    return Plan(shards, [SDS((4,) + s.shape, s.dtype) for s in shards], {}, 3 * n, copies)


def plan_gather_pass(gathered):
    pieces = _pieces([(g.shape[1:], g.dtype) for g in gathered], lambda shape: shape[0] // 2)
    n = len(pieces)

    def copies(_, out_refs, send_sems, recv_sems):
        x, y, c, chips = _place()

        def copy(j, k, chip, cc):
            b, st, sz = pieces[k]
            rows_k = out_refs[b].at[chip, pl.ds(cc * (out_refs[b].shape[1] // 2) + st, sz), :]
            return pltpu.make_async_remote_copy(src_ref=rows_k, dst_ref=rows_k, send_sem=send_sems.at[j * n + k], recv_sem=recv_sems.at[j * n + k],
                                                device_id=(x, y, 1 - c), device_id_type=MESH_ID)

        order = [(k, j, 2 * cx + cy) for k in range(n) for j, (cx, cy) in enumerate(chips)]
        return [copy(j, k, chip, c) for k, j, chip in order], [copy(j, k, chip, 1 - c) for k, j, chip in order]

    return Plan(gathered, [SDS(g.shape, g.dtype) for g in gathered], {i: i for i in range(len(gathered))}, 3 * n, copies)


def plan_pair_exchange(gs):
    pieces = _pieces([(g.shape, g.dtype) for g in gs], lambda shape: shape[1] // 2)

    def copies(g_refs, got_refs, send_sems, recv_sems):
        x, y, c, _ = _place()
        swaps = [pltpu.make_async_remote_copy(src_ref=g_refs[b].at[:, pl.ds((1 - c) * (g_refs[b].shape[1] // 2) + st, sz), :],
                                              dst_ref=got_refs[b].at[:, pl.ds(st, sz), :], send_sem=send_sems.at[k], recv_sem=recv_sems.at[k],
                                              device_id=(x, y, 1 - c), device_id_type=MESH_ID)
                 for k, (b, st, sz) in enumerate(pieces)]
        return swaps, swaps

    return Plan(gs, [SDS((g.shape[0], g.shape[1] // 2, g.shape[2]), g.dtype) for g in gs], {}, len(pieces), copies)


def plan_chip_scatter(ps):
    pieces = _pieces([(p.shape, p.dtype) for p in ps], lambda shape: shape[1])
    n = len(pieces)

    def copies(p_refs, q_refs, send_sems, recv_sems):
        x, y, c, chips = _place()
        mine = 2 * x + y

        def copy(j, k, src_slot, dst_slot, to):
            b, st, sz = pieces[k]
            return pltpu.make_async_remote_copy(src_ref=p_refs[b].at[src_slot, pl.ds(st, sz), :], dst_ref=q_refs[b].at[dst_slot, pl.ds(st, sz), :],
                                                send_sem=send_sems.at[j * n + k], recv_sem=recv_sems.at[j * n + k], device_id=to,
                                                device_id_type=MESH_ID)

        order = [(k, j, 2 * cx + cy, (cx, cy, c)) for k in range(n) for j, (cx, cy) in enumerate(chips)]
        return [copy(j, k, chip, mine, to) for k, j, chip, to in order], [copy(j, k, mine, chip, to) for k, j, chip, to in order]

    return Plan(ps, [SDS(p.shape, p.dtype) for p in ps], {}, 3 * n, copies)


def plan_pair_join(bufs):
    pieces = _pieces([(b.shape, b.dtype) for b in bufs], lambda shape: shape[0] // 2)

    def copies(_, out_refs, send_sems, recv_sems):
        x, y, c, _ = _place()

        def copy(k, cc):
            b, st, sz = pieces[k]
            rows_k = out_refs[b].at[pl.ds(cc * (out_refs[b].shape[0] // 2) + st, sz), :]
            return pltpu.make_async_remote_copy(src_ref=rows_k, dst_ref=rows_k, send_sem=send_sems.at[k], recv_sem=recv_sems.at[k],
                                                device_id=(x, y, 1 - c), device_id_type=MESH_ID)

        return [copy(k, c) for k in range(len(pieces))], [copy(k, 1 - c) for k in range(len(pieces))]

    return Plan(bufs, [SDS(b.shape, b.dtype) for b in bufs], {i: i for i in range(len(bufs))}, len(pieces), copies)


def run_plan(name, plan):
    hosted_call(lambda: None, name=name, grid=(1,), in_specs=[], out_specs=[], out_shape=[], operands=[], plans=[plan])
    return plan.results


HBM = pl.BlockSpec(memory_space=pltpu.HBM)
SEMS = pl.BlockSpec(memory_space=pltpu.SEMAPHORE)
SPLIT_PARAMS = dict(has_side_effects=pltpu.SideEffectType.DATAFLOW_SIDE_EFFECTING)


def _plan_buffers(plan):
    in_place = {o: i for i, o in plan.aliases.items()}
    bufs = [pltpu.with_memory_space_constraint(a, pltpu.HBM) for a in plan.operands]
    where = []
    for o, sd in enumerate(plan.out_shape):
        if o in in_place:
            where.append(in_place[o])
        else:
            where.append(len(bufs))
            bufs.append(pltpu.with_memory_space_constraint(lax.empty(sd.shape, sd.dtype), pltpu.HBM))
    return bufs, where


def split_start(name, plans):
    layout = [_plan_buffers(p) for p in plans]
    counts = [len(b) for b, _ in layout]
    n_buf = sum(counts)

    def body(*refs):
        sems, token = refs[n_buf:n_buf + 2 * len(plans)], refs[-1]
        pos = 0
        for k, (p, (_, where)) in enumerate(zip(plans, layout)):
            mine = refs[pos:pos + counts[k]]
            pos += counts[k]
            sends, _ = p.copies(mine[:len(p.operands)], [mine[w] for w in where], sems[2 * k], sems[2 * k + 1])
            for cp in sends:
                cp.start()
        token[...] = jnp.zeros_like(token)

    bufs = [b for bs, _ in layout for b in bs]
    res = pl.pallas_call(
        body, name=name, in_specs=[HBM] * n_buf,
        out_specs=[SEMS] * (2 * len(plans)) + [HBM] * n_buf + [pl.BlockSpec(memory_space=pltpu.VMEM)],
        out_shape=[pltpu.SemaphoreType.DMA((p.n_sems,)) for p in plans for _ in range(2)] + [pltpu.HBM(b.shape, b.dtype) for b in bufs]
        + [SDS((8, 128), F32)],
        input_output_aliases={i: 2 * len(plans) + i for i in range(n_buf)}, compiler_params=pltpu.CompilerParams(**SPLIT_PARAMS))(*bufs)
    pos = 2 * len(plans)
    for k, p in enumerate(plans):
        p.in_flight = (res[2 * k], res[2 * k + 1], list(res[pos:pos + counts[k]]), layout[k][1])
        pos += counts[k]
    return res[-1]


def split_wait(name, plan, after):
    send_sems, recv_sems, bufs, where = plan.in_flight
    n_buf = len(bufs)

    def body(*refs):
        mine = refs[:n_buf]
        sends, recvs = plan.copies(mine[:len(plan.operands)], [mine[w] for w in where], refs[n_buf], refs[n_buf + 1])
        for cp in recvs:
            cp.wait_recv()
        for cp in sends:
            cp.wait_send()

    res = pl.pallas_call(body, name=name, in_specs=[HBM] * n_buf + [SEMS, SEMS, ANY], out_specs=[HBM] * n_buf,
                         out_shape=[pltpu.HBM(b.shape, b.dtype) for b in bufs], input_output_aliases={i: i for i in range(n_buf)},
                         compiler_params=pltpu.CompilerParams(**SPLIT_PARAMS))(*bufs, send_sems, recv_sems, after)
    plan.results = [res[w] for w in where]
    return plan.results


def pair_add(name, g, got, place):
    slots, rows, cols = g.shape
    half = rows // 2
    tr = _row_tile(half)
    nb = half // tr

    def kern(_, g_ref, t_ref, o_ref):
        o_ref[...] = (g_ref[...].astype(F32) + t_ref[...].astype(F32)).astype(o_ref.dtype)

    blk = pl.BlockSpec((None, tr, cols), lambda s, i, p: (s, i, 0))
    grid_spec = pltpu.PrefetchScalarGridSpec(
        num_scalar_prefetch=1, grid=(slots, nb),
        in_specs=[pl.BlockSpec((None, tr, cols), lambda s, i, p: (s, p[1] * nb + i, 0)), blk], out_specs=blk)
    return pl.pallas_call(kern, grid_spec=grid_spec, out_shape=SDS((slots, half, cols), g.dtype), name=name,
                          compiler_params=_cparams())(place, g, got)


def chip_add(name, p, q, place):
    slots, half, cols = p.shape
    tr = _row_tile(half)
    nb = half // tr

    def kern(_, p_ref, q1, q2, q3, o_ref):
        o_ref[...] = p_ref[...].astype(F32) + q1[...].astype(F32) + q2[...].astype(F32) + q3[...].astype(F32)

    def slot(k):
        return pl.BlockSpec((None, tr, cols), lambda i, pr: ((pr[0] + k) % slots, i, 0))

    grid_spec = pltpu.PrefetchScalarGridSpec(
        num_scalar_prefetch=1, grid=(nb,), in_specs=[slot(0), slot(1), slot(2), slot(3)],
        out_specs=pl.BlockSpec((tr, cols), lambda i, pr: (pr[1] * nb + i, 0)))
    return pl.pallas_call(kern, grid_spec=grid_spec, out_shape=SDS((2 * half, cols), F32), name=name,
                          compiler_params=_cparams())(place, p, q, q, q)


def pair_adds(tag, gs, gots, place):
    return [pair_add(f"{tag}_pair_add_{i}", g, got, place) for i, (g, got) in enumerate(zip(gs, gots))]


def chip_adds(tag, pairs, qs, place):
    return [chip_add(f"{tag}_chip_add_{i}", p, q, place) for i, (p, q) in enumerate(zip(pairs, qs))]


def reduce_scatter_chips(tag, gs, place):
    pairs = pair_adds(tag, gs, run_plan(tag + "_pair_exchange", plan_pair_exchange(gs)), place)
    return run_plan(tag + "_pair_join", plan_pair_join(chip_adds(tag, pairs, run_plan(tag + "_chip_scatter", plan_chip_scatter(pairs)), place)))


BIG = [("w_out", (2, 512, 1024)), ("w_mem_kv", (2, 256, 1024)), ("s5_w_in", (1, 1024, 1024)), ("s5_w_glu", (1, 1536, 768)),
       ("mla_w_in", (1, 1024, 848)), ("mla_w_uq", (1, 512, 576)), ("mla_w_ukv", (1, 256, 768))]
SHARDED_SMALL = [("mla_q_lora_norm", (1, 128)), ("mla_kv_lora_norm", (1, 64))]
SMALL = [("ln_gain", (2, 1024)), ("mem_norm", (2, 1024)), ("xq_norm", (2, 128)), ("xk_norm", (2, 128)),
         ("s5_lambda_re", (1, 96, 64)), ("s5_lambda_im", (1, 96, 64)), ("s5_log_step", (1, 96)),
         ("s5_b_re", (1, 96, 64, 16)), ("s5_b_im", (1, 96, 64, 16)), ("s5_c_re", (1, 96, 16, 64)), ("s5_c_im", (1, 96, 16, 64)),
         ("s5_d", (1, 1536)), ("mla_q_nope_norm", (1, 128)), ("mla_k_nope_norm", (1, 128)), ("mla_q_rope_norm", (1, 64)),
         ("mla_k_rope_norm", (1, 64))]
WEIGHT_ORDER = ["ln_gain", "w_out", "mem_norm", "w_mem_kv", "xq_norm", "xk_norm", "s5_w_in", "s5_lambda_re", "s5_lambda_im",
                "s5_log_step", "s5_b_re", "s5_b_im", "s5_c_re", "s5_c_im", "s5_d", "s5_w_glu", "mla_w_in", "mla_q_lora_norm",
                "mla_kv_lora_norm", "mla_w_uq", "mla_w_ukv", "mla_q_nope_norm", "mla_k_nope_norm", "mla_q_rope_norm", "mla_k_rope_norm"]
MINOR_LAST = {"mla_w_in": (0, 2, 1), "mla_w_uq": (0, 2, 1), "s5_b_re": (0, 2, 3, 1), "s5_b_im": (0, 2, 3, 1),
              "s5_c_re": (0, 2, 3, 1), "s5_c_im": (0, 2, 3, 1)}
SMALL_FULL = SMALL + [(n, (1, 4 * s[1])) for n, s in SHARDED_SMALL]
N_SMALL = sum(math.prod(s) for _, s in SMALL_FULL)
SMALL_ROWS, SMALL_LANES = 128, 1024

PAIR_OUT, PAIR_MKV, PAIR_ROWS = 0, 512, 768


def stack_shards(w, dtype):
    pairs = [jnp.concatenate([w["w_out"][l], w["w_mem_kv"][l]], axis=0).astype(dtype) for l in range(2)]
    return ([w["s5_w_in"][0].astype(dtype)], [pairs[0], w["s5_w_glu"][0].astype(dtype)],
            [pairs[1], w["mla_w_ukv"][0].astype(dtype), w["mla_w_in"][0].astype(dtype), w["mla_w_uq"][0].astype(dtype)])


def pair_views(pair):
    return {"w_out": Sharded(pair, "row", PAIR_OUT, 512), "w_mem_kv": Sharded(pair, "row", PAIR_MKV, 256)}


def grad_views():
    pair = SDS((4, PAIR_ROWS, 1024), BF16)
    return {"w_out": Sharded(pair, "row", PAIR_OUT, 512), "w_mem_kv": Sharded(pair, "row", PAIR_MKV, 256),
            "s5_w_in": Sharded(SDS((4, 1024, 1024), BF16), "col", 0, 1024), "s5_w_glu": Sharded(SDS((4, 1536, 768), BF16), "col", 0, 1536),
            "mla_w_ukv": Sharded(SDS((4, 256, 768), BF16), "col", 0, 256)}


def cols_to_shards(full):
    return full.reshape(full.shape[0], 4, full.shape[1] // 4).transpose(1, 0, 2)


def shards_to_cols(arr):
    return arr.transpose(1, 0, 2).reshape(arr.shape[1], 4 * arr.shape[2])


def mla_in_permute(w):
    o1, o2, o3, o4 = QL, QL + KVL, QL + KVL + ROPE, QL + KVL + ROPE + XQW
    return jnp.concatenate([w[:, o4:], w[:, :o1], w[:, o3:o4], w[:, o1:o2], w[:, o2:o3],
                            jnp.zeros((w.shape[0], MLA_IN_P - MLA_IN), w.dtype)], axis=1)


def mla_in_unpermute(d):
    return jnp.concatenate([d[:, 2048:2560], d[:, 3072:3328], d[:, 3328:3392], d[:, 2560:3072], d[:, :2048]], axis=1)


def uq_permute(w):
    w3 = w.reshape(w.shape[0], MH, NOPE + ROPE)
    return jnp.concatenate([w3[:, :, :NOPE].reshape(w.shape[0], MH * NOPE), w3[:, :, NOPE:].reshape(w.shape[0], MH * ROPE)], axis=1)


def uq_unpermute(d):
    dn = d[:, :MH * NOPE].reshape(d.shape[0], MH, NOPE)
    dr = d[:, MH * NOPE:].reshape(d.shape[0], MH, ROPE)
    return jnp.concatenate([dn, dr], axis=2).reshape(d.shape[0], MH * (NOPE + ROPE))


def time_permute(a):
    return a.reshape(SEG, SEG_LEN, a.shape[-1]).transpose(1, 0, 2).reshape(L, a.shape[-1])


def time_unpermute(a):
    return a.reshape(SEG_LEN, SEG, a.shape[-1]).transpose(1, 0, 2).reshape(L, a.shape[-1])


def mem_branch_fwd(tag, mem, mem_norm, w_mem_kv, xk_norm):
    mn = row_fwd(tag + "_mem_rms", fn_rms, ML, ML, [(mem, D, 0)], [mem_norm], [(D, BF16)])[0]
    kv = matmul(tag + "_mem_kv", mn, w_mem_kv, "nn", F32)
    kn = row_fwd(tag + "_mem_knorm", fn_mem_k, ML, ML, [(kv, XQW, 0)], [xk_norm], [(XQW, F32)])[0]
    return mn, kv, kn


def mem_branch_bwd(tag, mem, mem_norm, w_mem_kv, xk_norm, mn, kv, dkn, dv, g_view, g_wide):
    dk, dxk = row_bwd(tag + "_mem_knorm_bwd", fn_mem_k, ML, ML, [(kv, XQW, 0)], [xk_norm], [(dkn, XQW, 0)], [True], [True])
    dkv = jnp.concatenate([dk, dv], axis=1)
    dmn = matmul(tag + "_mem_kv_dx", dkv, w_mem_kv, "nt", F32)
    g_wide = matmul(tag + "_mem_kv_dw", mn, dkv, "tn", out=g_view, into=g_wide)
    dmem_norm = row_bwd(tag + "_mem_rms_bwd", fn_rms, ML, ML, [(mem, D, 0)], [mem_norm], [(dmn, D, 0)], [False], [True])[0]
    return g_wide, dmem_norm, dxk


def mem_attn_fwd(tag, proj, cb, kn, kv, xq_norm):
    return row_fwd(tag + "_mem_attn", fn_mem_attn, L, ROW_TILE, [(proj, XQW, cb)], [kn, kv[:, XQW:], xq_norm], [(XQW, F32)])[0]


def mem_attn_bwd(tag, proj, cb, kn, kv, xq_norm, dmo, dproj):
    place = {"cols": proj.shape[1], "cb": cb, "into": dproj, "dtype": dproj.dtype}
    return row_bwd(tag + "_mem_attn_bwd", fn_mem_attn, L, ROW_TILE, [(proj, XQW, cb)], [kn, kv[:, XQW:], xq_norm], [(dmo, XQW, 0)],
                   [place], [True, True, True])


def device_step(x, mem, positions, target, small, env, hooks=None):
    hooks = hooks or {}

    def plans_for(name):
        return hooks[("plans", name)](env) if ("plans", name) in hooks else ()

    def around(when, name, last=None):
        if (when, name) in hooks:
            hooks[(when, name)](env, last)

    g = {}
    gw = grad_views()
    ln, mem_norm, xq_norm, xk_norm = small["ln_gain"], small["mem_norm"], small["xq_norm"], small["xk_norm"]

    lre, lim = small["s5_lambda_re"][0], small["s5_lambda_im"][0]
    ls = small["s5_log_step"].reshape(SG, 1)
    one = pl.BlockSpec((SG, SP), lambda i: (0, 0))
    col = pl.BlockSpec((SG, 1), lambda i: (0, 0))
    disc_ins = [(lre, one), (lim, one), (ls, col)]
    a_re, a_im, coef_re, coef_im = stage("s5_disc", fn_s5_disc, (1,), disc_ins, [(SDS((SG, SP), F32), one)] * 4)
    b_re, b_im = small["s5_b_re"].reshape(SN, SC), small["s5_b_im"].reshape(SN, SC)
    c_re, c_im = small["s5_c_re"].reshape(PW, SP), small["s5_c_im"].reshape(PW, SP)
    bmat_rows = [(b_re, SC, 0), (b_im, SC, 0), (coef_re.reshape(SN, 1), 1, 0), (coef_im.reshape(SN, 1), 1, 0)]
    wb_re, wb_im = row_fwd("s5_bmat", fn_s5_bmat, SN, BMAT_TILE, bmat_rows, [], [(128, F32)] * 2)
    cmat_rows = [(c_re, SP, 0), (c_im, SP, 0)]
    wc_re, wc_im = row_fwd("s5_cmat", fn_s5_cmat, PW, CMAT_TILE, cmat_rows, [], [(512, F32)] * 2)
    a_re_v, a_im_v = a_re.reshape(1, SN), a_im.reshape(1, SN)
    s5_d = small["s5_d"]

    xp = time_permute(x)
    h0 = row_fwd("l0_rms", fn_rms, L, ROW_TILE, [(xp, D, 0)], [ln[0:1]], [(D, BF16)])[0]
    around("before", "l0_in", wb_re)
    w_in0 = Sharded(env["in0"], "col", 0, 1024)
    proj0 = matmul("l0_in", h0, w_in0, "nn")
    s_re, s_im, g0 = s5_forward(proj0, wb_re, wb_im, wc_re, wc_im, a_re_v, a_im_v, s5_d, plans=plans_for("s5_forward"))
    around("before", "l0_glu", g0)
    w0 = dict(pair_views(env["pair0"]), s5_w_glu=Sharded(env["glu"], "col", 0, 1536))
    z0 = matmul("l0_glu", g0, w0["s5_w_glu"], "nn", plans=plans_for("l0_glu"))
    mn0, kv0, kn0 = mem_branch_fwd("l0", mem, mem_norm[0:1], w0["w_mem_kv"], xk_norm[0:1])
    mo0 = mem_attn_fwd("l0", proj0, 3, kn0, kv0, xq_norm[0:1])
    o0 = row_fwd("l0_merge", fn_merge_glu, L, ROW_TILE, [(z0, 2 * PW, 0), (mo0, XQW, 0), (proj0, BW, 1)], [], [(BW, BF16)])[0]
    around("before", "l0_out", o0)
    x1p = matmul("l0_out", o0, w0["w_out"], "nn", F32, add=xp, plans=plans_for("l0_out"))
    around("after", "l0_out", x1p)
    x1 = time_unpermute(x1p)

    w1 = dict(pair_views(env["pair1"]), mla_w_ukv=Sharded(env["ukv"], "col", 0, 256))
    w_in1, w_uq = env["w_in1"], env["w_uq"]
    h1 = row_fwd("l1_rms", fn_rms, L, ROW_TILE, [(x1, D, 0)], [ln[1:2]], [(D, BF16)])[0]
    proj1 = matmul("l1_in", h1, w_in1, "nn")
    qln, kvln = env["q_lora_norm"].reshape(1, QL), env["kv_lora_norm"].reshape(1, KVL)
    cqn = row_fwd("l1_q_lora_rms", fn_rms, L, ROW_TILE, [(proj1, QL, 4)], [qln], [(QL, BF16)])[0]
    ckvn = row_fwd("l1_kv_lora_rms", fn_rms, L, ROW_TILE, [(proj1, KVL, 12)], [kvln], [(KVL, BF16)])[0]
    q = matmul("l1_uq", cqn, w_uq, "nn")
    kv = matmul("l1_ukv", ckvn, w1["mla_w_ukv"], "nn")
    inv_freq = ROPE_THETA ** (-jnp.arange(ROPE // 2, dtype=F32) / (ROPE // 2))
    ang = positions.astype(F32)[:, None] * inv_freq
    cos2 = jnp.tile(jnp.cos(ang), (1, 4))
    sin_signed = jnp.tile(jnp.concatenate([-jnp.sin(ang), jnp.sin(ang)], axis=1), (1, 2))
    qnn, knn = small["mla_q_nope_norm"], small["mla_k_nope_norm"]
    qrn, krn = jnp.tile(small["mla_q_rope_norm"], (1, 2)), jnp.tile(small["mla_k_rope_norm"], (1, 2))
    tp = 256
    prep_ins = [(q, rspec(tp, MH * (NOPE + ROPE))), (kv, rspec(tp, MH * 256)), (proj1, rspec(tp, 128, 26)),
                (cos2, rspec(tp, 128)), (sin_signed, rspec(tp, 128))] + [(a, cspec((1, 128))) for a in (qnn, knn, qrn, krn)]
    hq_spec = pl.BlockSpec((MH, tp, 256), lambda i: (0, i, 0))
    hv_spec = pl.BlockSpec((MH, tp, 128), lambda i: (0, i, 0))
    qf, kf, vh = stage("l1_mla_prep", fn_mla_prep, (L // tp,), prep_ins,
                       [(SDS((MH, L, 256), BF16), hq_spec), (SDS((MH, L, 256), BF16), hq_spec), (SDS((MH, L, 128), BF16), hv_spec)])
    attn, attn_lse = causal_attn(qf, kf, vh)
    mn1, kv1, kn1 = mem_branch_fwd("l1", mem, mem_norm[1:2], w1["w_mem_kv"], xk_norm[1:2])
    mo1 = mem_attn_fwd("l1", proj1, 5, kn1, kv1, xq_norm[1:2])
    o1 = row_fwd("l1_merge", fn_merge, L, ROW_TILE, [(attn, PW, 0), (mo1, XQW, 0), (proj1, BW, 0)], [], [(BW, BF16)])[0]
    x2 = matmul("l1_out", o1, w1["w_out"], "nn", F32, add=x1)
    dx2, loss = loss_and_grad(x2, target)
    around("after", "loss", loss)

    do1 = matmul("l1_out_dx", dx2, w1["w_out"], "nt")
    g_pair1 = matmul("l1_out_dw", o1, dx2, "tn", out=gw["w_out"])
    dattn, dmo1, dproj1 = row_bwd("l1_merge_bwd", fn_merge, L, ROW_TILE, [(attn, PW, 0), (mo1, XQW, 0), (proj1, BW, 0)], [],
                                  [(do1, BW, 0)], [True, True, {"cols": MLA_IN_P, "cb": 0, "dtype": BF16}], [])
    dproj1, dkn1, dv1, dxqn1 = mem_attn_bwd("l1", proj1, 5, kn1, kv1, xq_norm[1:2], dmo1, dproj1)
    env["g_pair1"], dmem_norm1, dxk1 = mem_branch_bwd("l1", mem, mem_norm[1:2], w1["w_mem_kv"], xk_norm[1:2], mn1, kv1, dkn1, dv1,
                                                      gw["w_mem_kv"], g_pair1)
    dqf, dkf, dvh = causal_attn_bwd(qf, kf, vh, attn, attn_lse, dattn)
    prep_diffs = [("row", SDS((L, MH * (NOPE + ROPE)), BF16), rspec(tp, MH * (NOPE + ROPE))), ("row", SDS((L, MH * 256), BF16), rspec(tp, MH * 256)),
                  ("row", SDS((L, MLA_IN_P), BF16), rspec(tp, 128, 26), {"into": dproj1}), None, None] + [("acc", (0,))] * 4
    dq, dkv, dproj1, dqnn, dknn, dqrn, dkrn = stage_bwd("l1_mla_prep_bwd", fn_mla_prep, (L // tp,), prep_ins,
                                                        [(dqf, hq_spec), (dkf, hq_spec), (dvh, hv_spec)], prep_diffs)
    dcqn = matmul("l1_uq_dx", dq, w_uq, "nt")
    env["dw_uq"] = matmul("l1_uq_dw", cqn, dq, "tn")
    dckvn = matmul("l1_ukv_dx", dkv, w1["mla_w_ukv"], "nt")
    env["g_ukv"] = matmul("l1_ukv_dw", ckvn, dkv, "tn", out=gw["mla_w_ukv"])
    dproj1, dqln = row_bwd("l1_q_lora_rms_bwd", fn_rms, L, ROW_TILE, [(proj1, QL, 4)], [qln], [(dcqn, QL, 0)],
                           [{"cols": MLA_IN_P, "cb": 4, "into": dproj1, "dtype": BF16}], [True])
    dproj1, dkvln = row_bwd("l1_kv_lora_rms_bwd", fn_rms, L, ROW_TILE, [(proj1, KVL, 12)], [kvln], [(dckvn, KVL, 0)],
                            [{"cols": MLA_IN_P, "cb": 12, "into": dproj1, "dtype": BF16}], [True])
    env["dw_in1"] = matmul("l1_in_dw", h1, dproj1, "tn")
    dh1 = matmul("l1_in_dx", dproj1, w_in1, "nt", plans=plans_for("l1_in_dx"))
    around("after", "l1_in_dx", dh1)
    dx1, dln1 = row_bwd("l1_rms_bwd", fn_rms, L, ROW_TILE, [(x1, D, 0)], [ln[1:2]], [(dh1, D, 0)], [{"add": (dx2, D, 0)}], [True])
    dx1p = time_permute(dx1)

    do0 = matmul("l0_out_dx", dx1p, w0["w_out"], "nt", plans=plans_for("l0_out_dx"))
    g_pair0 = matmul("l0_out_dw", o0, dx1p, "tn", out=gw["w_out"])
    dz0, dmo0, dproj0 = row_bwd("l0_merge_bwd", fn_merge_glu, L, ROW_TILE, [(z0, 2 * PW, 0), (mo0, XQW, 0), (proj0, BW, 1)], [],
                                [(do0, BW, 0)], [{"dtype": BF16}, True, {"cols": 2 * BW, "cb": 1, "dtype": BF16}], [])
    dproj0, dkn0, dv0, dxqn0 = mem_attn_bwd("l0", proj0, 3, kn0, kv0, xq_norm[0:1], dmo0, dproj0)
    env["g_pair0"], dmem_norm0, dxk0 = mem_branch_bwd("l0", mem, mem_norm[0:1], w0["w_mem_kv"], xk_norm[0:1], mn0, kv0, dkn0, dv0,
                                                      gw["w_mem_kv"], g_pair0)
    env["g_glu"] = matmul("l0_glu_dw", g0, dz0, "tn", out=gw["s5_w_glu"], plans=plans_for("l0_glu_dw"))
    dg0 = matmul("l0_glu_dx", dz0, w0["s5_w_glu"], "nt", plans=plans_for("l0_glu_dx"))
    around("before", "s5_backward", dg0)
    dproj0, dd, dwc_re, dwc_im, dwb_re, dwb_im, da_re, da_im = s5_backward(dg0, proj0, s_re, s_im, wb_re, wb_im, wc_re, wc_im,
                                                                           a_re_v, a_im_v, s5_d, dproj0, plans=plans_for("s5_backward"))
    around("after", "s5_backward", dd)
    env["g_in0"] = matmul("l0_in_dw", h0, dproj0, "tn", out=gw["s5_w_in"], plans=plans_for("l0_in_dw"))
    dh0 = matmul("l0_in_dx", dproj0, w_in0, "nt", plans=plans_for("l0_in_dx"))
    around("after", "l0_in_dx", dh0)
    dxp, dln0 = row_bwd("l0_rms_bwd", fn_rms, L, ROW_TILE, [(xp, D, 0)], [ln[0:1]], [(dh0, D, 0)], [{"add": (dx1p, D, 0)}], [True])
    grad_x = time_unpermute(dxp)

    db_re, db_im, dcoef_re, dcoef_im = row_bwd("s5_bmat_bwd", fn_s5_bmat, SN, BMAT_TILE, bmat_rows, [], [(dwb_re, 128, 0), (dwb_im, 128, 0)],
                                               [True] * 4, [], plans=plans_for("s5_bmat_bwd"))
    dc_re, dc_im = row_bwd("s5_cmat_bwd", fn_s5_cmat, PW, CMAT_TILE, cmat_rows, [], [(dwc_re, 512, 0), (dwc_im, 512, 0)], [True] * 2, [],
                           plans=plans_for("s5_cmat_bwd"))
    disc_cts = [(da_re.reshape(SG, SP), one), (da_im.reshape(SG, SP), one), (dcoef_re.reshape(SG, SP), one), (dcoef_im.reshape(SG, SP), one)]
    dlre, dlim, dls = stage_bwd("s5_disc_bwd", fn_s5_disc, (1,), disc_ins, disc_cts, [("acc", (0,))] * 3)

    g["ln_gain"] = jnp.concatenate([dln0, dln1], axis=0)
    g["mem_norm"] = jnp.concatenate([dmem_norm0, dmem_norm1], axis=0)
    g["xq_norm"] = jnp.concatenate([dxqn0, dxqn1], axis=0)
    g["xk_norm"] = jnp.concatenate([dxk0, dxk1], axis=0)
    g["s5_lambda_re"], g["s5_lambda_im"], g["s5_log_step"] = dlre, dlim, dls
    g["s5_b_re"], g["s5_b_im"], g["s5_c_re"], g["s5_c_im"] = db_re, db_im, dc_re, dc_im
    g["s5_d"] = dd
    g["mla_q_lora_norm"], g["mla_kv_lora_norm"] = dqln, dkvln
    g["mla_q_nope_norm"], g["mla_k_nope_norm"] = dqnn, dknn
    g["mla_q_rope_norm"] = dqrn[:, :ROPE] + dqrn[:, ROPE:]
    g["mla_k_rope_norm"] = dkrn[:, :ROPE] + dkrn[:, ROPE:]
    return loss, grad_x, g


def kernel(x, mem, positions, ln_gain, w_out, mem_norm, w_mem_kv, xq_norm, xk_norm, s5_w_in, s5_lambda_re, s5_lambda_im, s5_log_step, s5_b_re, s5_b_im, s5_c_re, s5_c_im, s5_d, s5_w_glu, mla_w_in, mla_q_lora_norm, mla_kv_lora_norm, mla_w_uq, mla_w_ukv, mla_q_nope_norm, mla_k_nope_norm, mla_q_rope_norm, mla_k_rope_norm, loss_target, m_ln_gain, m_w_out, m_mem_norm, m_w_mem_kv, m_xq_norm, m_xk_norm, m_s5_w_in, m_s5_lambda_re, m_s5_lambda_im, m_s5_log_step, m_s5_b_re, m_s5_b_im, m_s5_c_re, m_s5_c_im, m_s5_d, m_s5_w_glu, m_mla_w_in, m_mla_q_lora_norm, m_mla_kv_lora_norm, m_mla_w_uq, m_mla_w_ukv, m_mla_q_nope_norm, m_mla_k_nope_norm, m_mla_q_rope_norm, m_mla_k_rope_norm, v_ln_gain, v_w_out, v_mem_norm, v_w_mem_kv, v_xq_norm, v_xk_norm, v_s5_w_in, v_s5_lambda_re, v_s5_lambda_im, v_s5_log_step, v_s5_b_re, v_s5_b_im, v_s5_c_re, v_s5_c_im, v_s5_d, v_s5_w_glu, v_mla_w_in, v_mla_q_lora_norm, v_mla_kv_lora_norm, v_mla_w_uq, v_mla_w_ukv, v_mla_q_nope_norm, v_mla_k_nope_norm, v_mla_q_rope_norm, v_mla_k_rope_norm):
    args = dict(locals())
    wts = {n: args[n] for n in WEIGHT_ORDER}
    mom = {n: args["m_" + n] for n in WEIGHT_ORDER}
    var = {n: args["v_" + n] for n in WEIGHT_ORDER}

    chip = 2 * lax.axis_index("x") + lax.axis_index("y")
    place = jnp.stack([chip, lax.axis_index("c")]).astype(jnp.int32)

    def own_slot(gathered, shards):
        return [lax.dynamic_update_slice(g, s[None], (chip, 0, 0)) for g, s in zip(gathered, shards)]

    groups = list(stack_shards(wts, BF16))
    groups[2].append(jnp.concatenate([mla_q_lora_norm, jnp.pad(mla_kv_lora_norm, ((0, 0), (0, 64))), jnp.zeros((14, 128), F32)], axis=0))
    over_ici = [plan_gather_ici(shards) for shards in groups]
    _SCHEDULE_BEHIND.clear()
    schedule_behind(split_start("gather_start", over_ici))
    env, hooks, passed_on = {}, {}, {}

    def arrived(k, after, pass_now):
        passing = plan_gather_pass(split_wait(f"gather_wait_{k}", over_ici[k], after))
        passed_on[k] = passing
        return own_slot(run_plan(f"gather_pass_{k}", passing), groups[k]) if pass_now else None

    def need_in0(env, last):
        env["in0"], = arrived(0, last, True)

    def need_layer0(env, last):
        env["pair0"], env["glu"] = arrived(1, last, True)

    def need_layer1(env, last):
        arrived(2, last, False)

    def layer1_weights(env, last):
        pair1, ukv, in1, uq, norms = own_slot(passed_on[2].results, groups[2])
        env.update(pair1=pair1, ukv=ukv, w_in1=mla_in_permute(shards_to_cols(in1)), w_uq=uq_permute(shards_to_cols(uq)),
                   q_lora_norm=norms[:, 0, :], kv_lora_norm=norms[:, 1, :64])

    hooks["before", "l0_in"], hooks["before", "l0_glu"], hooks["before", "l0_out"] = need_in0, need_layer0, need_layer1
    hooks["plans", "l0_out"], hooks["after", "l0_out"] = (lambda env: [passed_on[2]]), layer1_weights

    rs = {}

    def swap(k, gs):
        rs[k, "g"], rs[k, "swap"] = gs, plan_pair_exchange(gs)
        return rs[k, "swap"]

    def start_scatter(k):
        rs[k, "pairs"] = pair_adds(f"rs{k}", rs[k, "g"], rs[k, "swap"].results, place)
        rs[k, "scatter"] = plan_chip_scatter(rs[k, "pairs"])
        schedule_behind(split_start(f"rs{k}_scatter_start", [rs[k, "scatter"]]))

    def join(k, after):
        rs[k, "join"] = plan_pair_join(chip_adds(f"rs{k}", rs[k, "pairs"], split_wait(f"rs{k}_scatter_wait", rs[k, "scatter"], after), place))
        return rs[k, "join"]

    hooks["plans", "l1_in_dx"] = lambda env: [swap(1, [env["g_pair1"], env["g_ukv"], cols_to_shards(mla_in_unpermute(env["dw_in1"])).astype(BF16),
                                                        cols_to_shards(uq_unpermute(env["dw_uq"])).astype(BF16)])]
    hooks["after", "l1_in_dx"] = lambda env, last: start_scatter(1)
    hooks["plans", "l0_glu_dx"] = lambda env: [swap(0, [env["g_pair0"], env["g_glu"]])]

    def before_s5_backward(env, last):
        start_scatter(0)
        env["join1"] = join(1, last)

    hooks["before", "s5_backward"] = before_s5_backward
    hooks["plans", "s5_backward"] = lambda env: [env["join1"]]
    hooks["after", "s5_backward"] = lambda env, last: env.update(join0=join(0, last))
    hooks["plans", "l0_in_dx"] = lambda env: [swap(2, [env["g_in0"]]), env["join0"]]
    hooks["after", "l0_in_dx"] = lambda env, last: start_scatter(2)

    def total_loss(env, local):
        env["loss"] = lax.psum(local[0, 0], MESH_AXES)
        schedule_behind(env["loss"].reshape(1, 1))

    hooks["after", "loss"] = total_loss
    small = {n: wts[n] for n, _ in SMALL}
    loss, grad_x, g = device_step(x[0], mem[0], positions[0], loss_target[0], small, env, hooks)
    loss = env["loss"]
    r_in0, = run_plan("rs2_pair_join", join(2, g["s5_log_step"]))
    (r_pair1, r_ukv, r_in1, r_uq), (r_pair0, r_glu) = (rs[k, "join"].results for k in (1, 0))

    small_flat = jnp.concatenate([g[n].reshape(-1) for n, _ in SMALL_FULL])
    g_small = jnp.pad(small_flat, (0, 4 * SMALL_ROWS * SMALL_LANES - N_SMALL)).astype(BF16).reshape(4, SMALL_ROWS, SMALL_LANES)
    r_small = reduce_scatter_chips("rs3", [g_small], place)[0]
    small_all = own_slot(all_gather_chips("gather_small_grads", [r_small]), [r_small])[0].reshape(-1)[:N_SMALL]

    grads = {"w_out": jnp.stack([r_pair0[:PAIR_MKV], r_pair1[:PAIR_MKV]]), "w_mem_kv": jnp.stack([r_pair0[PAIR_MKV:], r_pair1[PAIR_MKV:]]),
             "s5_w_in": r_in0[None], "s5_w_glu": r_glu[None], "mla_w_ukv": r_ukv[None], "mla_w_in": r_in1[None], "mla_w_uq": r_uq[None]}
    off = 0
    for n, s in SMALL_FULL:
        grads[n] = small_all[off:off + math.prod(s)].reshape(s)
        off += math.prod(s)
    for n, s in SHARDED_SMALL:
        grads[n] = lax.dynamic_slice(grads[n], (0, chip * s[1]), s)

    delta, new_m, new_v = {}, {}, {}
    for n, s in BIG + [(n, s) for n, s in SMALL if len(s) == 4]:
        perm = MINOR_LAST.get(n, tuple(range(len(s))))
        turned = tuple(s[p] for p in perm)
        view = lambda a: jnp.transpose(a, perm).reshape(-1, turned[-1])
        res = adamw("adamw_" + n, view(wts[n]), view(grads[n]), view(mom[n]), view(var[n]))
        delta[n], new_m[n], new_v[n] = (jnp.transpose(r.reshape(turned), tuple(perm.index(i) for i in range(len(s)))) for r in res)
    small_names = [n for n, s in SMALL if len(s) < 4] + [n for n, _ in SHARDED_SMALL]
    n_own = sum(wts[n].size for n in small_names)
    rows_own = -(-n_own // (8 * 128)) * 8

    def pack_small(d):
        flat = jnp.concatenate([d[n].reshape(-1) for n in small_names])
        return jnp.pad(flat, (0, rows_own * 128 - n_own), constant_values=1.0).reshape(rows_own, 128)

    res = adamw("adamw_small", pack_small(wts), pack_small(grads), pack_small(mom), pack_small(var))
    off = 0
    for n in small_names:
        size = wts[n].size
        delta[n], new_m[n], new_v[n] = (r.reshape(-1)[off:off + size].reshape(wts[n].shape) for r in res)
        off += size

    return (loss, grad_x[None], *[grads[n] for n in WEIGHT_ORDER], *[delta[n] for n in WEIGHT_ORDER],
            *[new_m[n] for n in WEIGHT_ORDER], *[new_v[n] for n in WEIGHT_ORDER])
```

```python
import functools
import math

import jax
import jax.numpy as jnp
from jax import lax
from jax.experimental import pallas as pl
from jax.experimental.pallas import tpu as pltpu

F32, BF16 = jnp.float32, jnp.bfloat16
SDS = jax.ShapeDtypeStruct

D = 1024
L = 2048
ML = 256
BW = 2 * D
XQW = BW // 4
PW = BW - XQW
XH, XHD = 4, 128
SG, SC, SP = 96, 16, 64
SN = SG * SP
NOPE, ROPE, VD = 128, 64, 128
MH = 12
QL, KVL = 512, 256
EPS = 1e-6
ROPE_THETA = 10000.0
MLA_IN = QL + KVL + ROPE + XQW + BW
MLA_IN_P = 3456
ADAM_LR, ADAM_B1, ADAM_B2, ADAM_EPS, ADAM_WD, ADAM_STEP = 0.001, 0.9, 0.999, 1e-08, 0.01, 10

VMEM_LIMIT = 48 * 2**20
ROW_TILE = 512
BMAT_TILE, CMAT_TILE = 2048, 512
SEG = 8
SEG_LEN = L // SEG
MESH_AXES = ("x", "y", "c")


def _cparams():
    return pltpu.CompilerParams(vmem_limit_bytes=VMEM_LIMIT)


def _dg(a, b, ca, cb):
    return lax.dot_general(a.astype(BF16), b.astype(BF16), (((ca,), (cb,)), ((), ())), preferred_element_type=F32)


@jax.custom_vjp
def mm_nn(a, b):
    return _dg(a, b, 1, 0)


mm_nn.defvjp(lambda a, b: (_dg(a, b, 1, 0), (a, b)), lambda res, g: (_dg(g, res[1], 1, 1), _dg(res[0], g, 0, 0)))


@jax.custom_vjp
def mm_nt(a, b):
    return _dg(a, b, 1, 1)


mm_nt.defvjp(lambda a, b: (_dg(a, b, 1, 1), (a, b)), lambda res, g: (_dg(g, res[1], 1, 0), _dg(g, res[0], 0, 0)))


@functools.partial(jax.custom_vjp, nondiff_argnums=(1,))
def lane_roll(x, shift):
    return pltpu.roll(x, shift, 1)


lane_roll.defvjp(lambda x, shift: (pltpu.roll(x, shift, 1), None),
                 lambda shift, _, g: (pltpu.roll(g, (128 - shift) % 128, 1),))


def rms(x, g):
    return x * lax.rsqrt(jnp.mean(x * x, axis=-1, keepdims=True) + EPS) * g


@jax.custom_vjp
def softmax_rows(s):
    e = jnp.exp(s - jnp.max(s, axis=-1, keepdims=True))
    return e / jnp.sum(e, axis=-1, keepdims=True)


def _softmax_rows_fwd(s):
    p = softmax_rows(s)
    return p, p


def _softmax_rows_bwd(p, g):
    return (p * (g - jnp.sum(g * p, axis=-1, keepdims=True)),)


softmax_rows.defvjp(_softmax_rows_fwd, _softmax_rows_bwd)


def silu(x):
    return x * jax.nn.sigmoid(x)


ANY = pl.BlockSpec(memory_space=pl.ANY)
MESH_ID = pl.DeviceIdType.MESH


def _dma_sems(n):
    return [pltpu.SemaphoreType.DMA((n,)), pltpu.SemaphoreType.DMA((n,))]


class Plan:
    def __init__(self, operands, out_shape, aliases, n_sems, copies):
        self.operands, self.out_shape, self.aliases, self.n_sems, self.copies = list(operands), list(out_shape), aliases, n_sems, copies
        self.results = None


_SCHEDULE_BEHIND = []


def schedule_behind(token):
    _SCHEDULE_BEHIND.append(token)


def hosted_call(kern, *, name, grid, in_specs, out_specs, out_shape, operands, scratch_shapes=(), aliases=None, cparams=None, plans=(), deps=()):
    n_in, n_out, n_scr = len(in_specs), len(out_specs), len(scratch_shapes)
    p_in, p_out = [len(p.operands) for p in plans], [len(p.out_shape) for p in plans]
    deps = tuple(deps) + tuple(_SCHEDULE_BEHIND)
    _SCHEDULE_BEHIND.clear()
    all_aliases = dict(aliases or {})
    in_off, out_off = n_in, n_out
    for p, ni, no in zip(plans, p_in, p_out):
        all_aliases.update({in_off + i: out_off + o for i, o in p.aliases.items()})
        in_off, out_off = in_off + ni, out_off + no

    def body(*refs):
        pos, pins, pouts = n_in, [], []
        for ni in p_in:
            pins.append(refs[pos:pos + ni])
            pos += ni
        pos += len(deps)
        main_out = refs[pos:pos + n_out]
        pos += n_out
        for no in p_out:
            pouts.append(refs[pos:pos + no])
            pos += no
        main_scr = refs[pos:pos + n_scr]
        pos += n_scr
        if plans:
            ids = [pl.program_id(ax) for ax in range(len(grid))]
            first = functools.reduce(jnp.logical_and, [i == 0 for i in ids])
            last = functools.reduce(jnp.logical_and, [i == g - 1 for i, g in zip(ids, grid)])
            copies = [p.copies(pins[k], pouts[k], refs[pos + 2 * k], refs[pos + 2 * k + 1]) for k, p in enumerate(plans)]

            @pl.when(first)
            def _():
                for sends, _ in copies:
                    for cp in sends:
                        cp.start()

        kern(*refs[:n_in], *main_out, *main_scr)
        if plans:
            @pl.when(last)
            def _():
                for sends, recvs in copies:
                    for cp in recvs:
                        cp.wait_recv()
                    for cp in sends:
                        cp.wait_send()

    res = pl.pallas_call(body, grid=grid, in_specs=list(in_specs) + [ANY] * (sum(p_in) + len(deps)),
                         out_specs=list(out_specs) + [ANY] * sum(p_out), out_shape=list(out_shape) + [s for p in plans for s in p.out_shape],
                         scratch_shapes=list(scratch_shapes) + [s for p in plans for s in _dma_sems(p.n_sems)],
                         input_output_aliases=all_aliases, name=name, compiler_params=cparams or _cparams())(
        *operands, *[a for p in plans for a in p.operands], *deps)
    pos = n_out
    for p, no in zip(plans, p_out):
        p.results = list(res[pos:pos + no])
        pos += no
    return list(res[:n_out])


def _wide(v):
    return v.astype(F32) if v.dtype == BF16 else v


def stage(name, fn, grid, ins, outs):
    n_in = len(ins)

    def kern(*refs):
        res = fn(*[_wide(r[...]) for r in refs[:n_in]])
        for r, v in zip(refs[n_in:], res):
            r[...] = v.astype(r.dtype)

    return hosted_call(kern, name=name, grid=grid, in_specs=[s for _, s in ins], out_specs=[s for _, s in outs],
                       out_shape=[sd for sd, _ in outs], operands=[a for a, _ in ins])


def stage_bwd(name, fn, grid, ins, cts, diffs, plans=()):
    n_in, n_ct = len(ins), len(cts)
    didx = [i for i, d in enumerate(diffs) if d is not None]
    opts = {i: (diffs[i][3] if len(diffs[i]) > 3 else {}) for i in didx if diffs[i][0] == "row"}
    adds = [(i, opts[i]["add"]) for i in opts if "add" in opts[i]]
    intos = [(i, opts[i]["into"]) for i in opts if "into" in opts[i]]
    n_add, n_into = len(adds), len(intos)
    add_pos = {i: n_in + n_ct + k for k, (i, _) in enumerate(adds)}
    n_extra = n_in + n_ct + n_add + n_into

    def kern(*refs):
        vals = [_wide(r[...]) for r in refs[:n_in]]

        def f(*dv):
            full = list(vals)
            for i, v in zip(didx, dv):
                full[i] = v
            return fn(*full)

        _, vjp = jax.vjp(f, *[vals[i].astype(F32) for i in didx])
        gs = vjp(tuple(c[...].astype(F32) for c in refs[n_in:n_in + n_ct]))
        for o_ref, i, g in zip(refs[n_extra:], didx, gs):
            if diffs[i][0] == "row":
                if i in add_pos:
                    g = g + refs[add_pos[i]][...].astype(F32)
                o_ref[...] = g.astype(o_ref.dtype)
            else:
                first = functools.reduce(jnp.logical_and, [pl.program_id(ax) == 0 for ax in diffs[i][1]])

                @pl.when(first)
                def _():
                    o_ref[...] = g

                @pl.when(jnp.logical_not(first))
                def _():
                    o_ref[...] += g

    out_shape, out_specs = [], []
    for i in didx:
        if diffs[i][0] == "row":
            out_shape.append(diffs[i][1])
            out_specs.append(diffs[i][2])
        else:
            out_shape.append(SDS(ins[i][0].shape, F32))
            out_specs.append(ins[i][1])
    aliases = {n_in + n_ct + n_add + k: didx.index(i) for k, (i, _) in enumerate(intos)}
    in_specs = [s for _, s in ins] + [s for _, s in cts] + [s for _, (_, s) in adds] + [ANY] * n_into
    operands = [a for a, _ in ins] + [a for a, _ in cts] + [a for _, (a, _) in adds] + [a for _, a in intos]
    return hosted_call(kern, name=name, grid=grid, in_specs=in_specs, out_specs=out_specs, out_shape=out_shape, operands=operands,
                       aliases=aliases, plans=plans)


def rspec(tl, w, cb=0):
    return pl.BlockSpec((tl, w), lambda i: (i, cb))


def cspec(shape):
    return pl.BlockSpec(shape, lambda i: (0,) * len(shape))


def row_fwd(name, fn, rows, tl, row_ins, consts, outs):
    ins = [(a, rspec(tl, w, cb)) for a, w, cb in row_ins] + [(a, cspec(a.shape)) for a in consts]
    return stage(name, fn, (rows // tl,), ins, [(SDS((rows, w), dt), rspec(tl, w)) for w, dt in outs])


def row_bwd(name, fn, rows, tl, row_ins, consts, cts, row_diff, const_diff, plans=()):
    ins = [(a, rspec(tl, w, cb)) for a, w, cb in row_ins] + [(a, cspec(a.shape)) for a in consts]
    diffs = []
    for (a, w, cb), d in zip(row_ins, row_diff):
        if not d:
            diffs.append(None)
            continue
        d = d if isinstance(d, dict) else {}
        opts = {}
        if "add" in d:
            opts["add"] = (d["add"][0], rspec(tl, d["add"][1], d["add"][2]))
        if d.get("into") is not None:
            opts["into"] = d["into"]
        diffs.append(("row", SDS((rows, d.get("cols", w)), d.get("dtype", F32)), rspec(tl, w, d.get("cb", 0)), opts))
    diffs += [("acc", (0,)) if d else None for d in const_diff]
    return stage_bwd(name, fn, (rows // tl,), ins, [(a, rspec(tl, w, cb)) for a, w, cb in cts], diffs, plans=plans)


MATMUL_VMEM = 36 * 2**20
ADAMW_VMEM = 28 * 2**20


class Sharded:
    def __init__(self, arr, kind, roff, rows):
        self.arr, self.kind, self.roff, self.rows, self.n = arr, kind, roff, rows, arr.shape[2]
        self.shape = (rows, 4 * self.n) if kind == "col" else (4 * rows, self.n)

    def fits(self, t0, t1):
        return self.roff % t0 == 0 and self.rows % t0 == 0 and self.n % t1 == 0

    def spec(self, t0, t1, bidx):
        assert self.fits(t0, t1), (self.kind, self.roff, self.rows, self.n, t0, t1)
        r0 = self.roff // t0
        if self.kind == "col":
            per = self.n // t1
            return pl.BlockSpec((None, t0, t1), lambda *g: (bidx(*g)[1] // per, r0 + bidx(*g)[0], bidx(*g)[1] % per))
        per = self.rows // t0
        return pl.BlockSpec((None, t0, t1), lambda *g: (bidx(*g)[0] // per, r0 + bidx(*g)[0] % per, bidx(*g)[1]))


def matmul(name, a, b, mode, out_dtype=BF16, add=None, out=None, into=None, plans=()):
    if mode == "tn":
        k_dim, m = a.shape
    else:
        m, k_dim = a.shape
    n = b.shape[0] if mode == "nt" else b.shape[1]
    b_fit = b.fits if isinstance(b, Sharded) else (lambda t0, t1: True)
    o_fit = out.fits if out is not None else (lambda t0, t1: True)
    a_bytes, b_bytes = jnp.dtype(a.dtype).itemsize, jnp.dtype(b.arr.dtype if isinstance(b, Sharded) else b.dtype).itemsize
    o_bytes = jnp.dtype(out_dtype if out is None else out.arr.dtype).itemsize

    def vmem(tm, tn, tk):
        return 2 * (tm * tk * a_bytes + tk * tn * b_bytes + tm * tn * (o_bytes + (4 if add is not None else 0))) + 4 * tm * tn * (1 + (tk < k_dim))

    tiles = [(tm, tn, tk) for tm in (2048, 1024, 512, 256, 128) for tn in (1024, 768, 512, 384, 256, 128)
             for tk in sorted({k_dim, 1024, 768, 512, 384, 256, 128})
             if m % tm == 0 and n % tn == 0 and k_dim % tk == 0 and (b_fit(tn, tk) if mode == "nt" else b_fit(tk, tn)) and o_fit(tm, tn)
             and vmem(tm, tn, tk) <= MATMUL_VMEM]
    tm, tn, tk = max(tiles, key=lambda t: (t[2] == k_dim, t[0] * t[1] * t[2], t[0] * t[1]))
    nk = k_dim // tk
    a_spec = pl.BlockSpec((tk, tm), lambda i, j, k: (k, i)) if mode == "tn" else pl.BlockSpec((tm, tk), lambda i, j, k: (i, k))
    if isinstance(b, Sharded):
        b_spec = b.spec(tn, tk, lambda i, j, k: (j, k)) if mode == "nt" else b.spec(tk, tn, lambda i, j, k: (k, j))
        b = b.arr
    else:
        b_spec = pl.BlockSpec((tn, tk), lambda i, j, k: (j, k)) if mode == "nt" else pl.BlockSpec((tk, tn), lambda i, j, k: (k, j))
    o_spec = pl.BlockSpec((tm, tn), lambda i, j, k: (i, j))
    out_spec, out_shape = (o_spec, SDS((m, n), out_dtype)) if out is None else (out.spec(tm, tn, lambda i, j, k: (i, j)), out.arr)
    ca, cb = {"nn": (1, 0), "nt": (1, 1), "tn": (0, 0)}[mode]
    n_in = 2 + (add is not None)

    def finish(refs, o_ref, r):
        if add is not None:
            r = r + refs[2][...]
        o_ref[...] = r.astype(o_ref.dtype)

    def kern_whole(*refs):
        finish(refs, refs[-1], _dg(refs[0][...], refs[1][...], ca, cb))

    def kern_cut(*refs):
        o_ref, acc = refs[-2], refs[-1]
        k = pl.program_id(2)

        @pl.when(k == 0)
        def _():
            acc[...] = jnp.zeros_like(acc)

        acc[...] += _dg(refs[0][...], refs[1][...], ca, cb)

        @pl.when(k == nk - 1)
        def _():
            finish(refs, o_ref, acc[...])

    ins, specs = [a, b], [a_spec, b_spec]
    if add is not None:
        ins.append(add)
        specs.append(o_spec)
    if into is not None:
        ins.append(into)
        specs.append(ANY)
    return hosted_call(kern_whole if nk == 1 else kern_cut, name=name, grid=(m // tm, n // tn, nk), in_specs=specs, out_specs=[out_spec],
                       out_shape=[out_shape], operands=ins, scratch_shapes=[] if nk == 1 else [pltpu.VMEM((tm, tn), F32)],
                       aliases={} if into is None else {n_in: 0}, plans=plans)[0]


SCAN_UNROLL = 8


def _cmul(ar, ai, br, bi):
    return ar * br - ai * bi, ar * bi + ai * br


def _sub_shift(x, down):
    row = lax.broadcasted_iota(jnp.int32, x.shape, 0)
    if down:
        return jnp.where(row == 0, 0.0, pltpu.roll(x, 1, 0))
    return jnp.where(row == SEG - 1, 0.0, pltpu.roll(x, SEG - 1, 0))


def _pow_seg_len(ar, ai):
    for _ in range(int(math.log2(SEG_LEN))):
        ar, ai = _cmul(ar, ai, ar, ai)
    return ar, ai


def _scan_in_place(sr, si, a_re, a_im):
    lanes = sr.shape[1]
    ar = jnp.broadcast_to(a_re, (SEG, lanes))
    ai = jnp.broadcast_to(a_im, (SEG, lanes))
    zero = jnp.zeros((SEG, lanes), F32)

    def local(i, carry):
        rows = pl.ds(pl.multiple_of(i * SEG, SEG), SEG)
        mr, mi = _cmul(ar, ai, carry[0], carry[1])
        nr, ni = mr + sr[rows, :], mi + si[rows, :]
        sr[rows, :] = nr
        si[rows, :] = ni
        return nr, ni

    fr, fi = lax.fori_loop(0, SEG_LEN, local, (zero, zero), unroll=SCAN_UNROLL)
    pr, pi = _pow_seg_len(ar, ai)
    ir, ii = zero, zero
    for _ in range(SEG - 1):
        mr, mi = _cmul(pr, pi, ir, ii)
        ir, ii = _sub_shift(mr + fr, True), _sub_shift(mi + fi, True)

    def carry_in(i, pw):
        rows = pl.ds(pl.multiple_of(i * SEG, SEG), SEG)
        cr, ci = _cmul(pw[0], pw[1], ir, ii)
        sr[rows, :] += cr
        si[rows, :] += ci
        return _cmul(pw[0], pw[1], ar, ai)

    lax.fori_loop(0, SEG_LEN, carry_in, (ar, ai), unroll=SCAN_UNROLL)


S5_LANES = 8 * SP
S5_BLOCKS = SN // S5_LANES


def _s5_specs(n=1):
    u_spec = pl.BlockSpec((L, n * 8 * SC), lambda j: (0, j))
    s_spec = pl.BlockSpec((L, n * S5_LANES), lambda j: (0, j))
    wb_spec = pl.BlockSpec((n * S5_LANES, 8 * SC), lambda j: (j, 0))
    wc_spec = pl.BlockSpec((n * 8 * SC, S5_LANES), lambda j: (j, 0))
    a_spec = pl.BlockSpec((1, n * S5_LANES), lambda j: (0, j))
    d_spec = pl.BlockSpec((1, n * 8 * SC), lambda j: (0, j))
    return u_spec, s_spec, wb_spec, wc_spec, a_spec, d_spec


S5_FWD_BLOCKS = 2


def s5_forward(proj, wb_re, wb_im, wc_re, wc_im, a_re, a_im, d, plans=()):
    n = S5_FWD_BLOCKS
    cols, lanes = 8 * SC, S5_LANES

    def kern(u_ref, wbr, wbi, wcr, wci, ar, ai, d_ref, sr_out, si_out, g_ref, sr, si):
        u = _wide(u_ref[...])
        for b in range(n):
            sr[:, b * lanes:(b + 1) * lanes], si[:, b * lanes:(b + 1) * lanes] = fn_s5_bu(
                u[:, b * cols:(b + 1) * cols], wbr[b * lanes:(b + 1) * lanes, :], wbi[b * lanes:(b + 1) * lanes, :])
        _scan_in_place(sr, si, ar[...], ai[...])
        for b in range(n):
            g_ref[:, b * cols:(b + 1) * cols] = fn_s5_out(
                sr[:, b * lanes:(b + 1) * lanes], si[:, b * lanes:(b + 1) * lanes], u[:, b * cols:(b + 1) * cols],
                d_ref[:, b * cols:(b + 1) * cols], wcr[b * cols:(b + 1) * cols, :], wci[b * cols:(b + 1) * cols, :])[0].astype(g_ref.dtype)
        sr_out[...] = sr[...].astype(sr_out.dtype)
        si_out[...] = si[...].astype(si_out.dtype)

    u_spec, s_spec, wb_spec, wc_spec, a_spec, d_spec = _s5_specs(n)
    return hosted_call(kern, name="s5_forward", grid=(S5_BLOCKS // n,), in_specs=[u_spec, wb_spec, wb_spec, wc_spec, wc_spec, a_spec, a_spec, d_spec],
                       out_specs=[s_spec, s_spec, u_spec], out_shape=[SDS((L, SN), BF16)] * 2 + [SDS((L, PW), BF16)],
                       operands=[proj, wb_re, wb_im, wc_re, wc_im, a_re, a_im, d], scratch_shapes=[pltpu.VMEM((L, n * lanes), F32)] * 2,
                       plans=plans)


def _adjoint_scan_in_place(lr, li, sr, si, a_re, a_im):
    lanes = lr.shape[1]
    ar = jnp.broadcast_to(a_re, (SEG, lanes))
    ai = -jnp.broadcast_to(a_im, (SEG, lanes))
    zero = jnp.zeros((SEG, lanes), F32)

    def local(k, carry):
        i = SEG_LEN - 1 - k
        rows = pl.ds(pl.multiple_of(i * SEG, SEG), SEG)
        mr, mi = _cmul(ar, ai, carry[0], carry[1])
        nr, ni = mr + lr[rows, :], mi + li[rows, :]
        lr[rows, :] = nr
        li[rows, :] = ni
        return nr, ni

    fr, fi = lax.fori_loop(0, SEG_LEN, local, (zero, zero), unroll=SCAN_UNROLL)
    pr, pi = _pow_seg_len(ar, ai)
    ir, ii = zero, zero
    for _ in range(SEG - 1):
        mr, mi = _cmul(pr, pi, ir, ii)
        ir, ii = _sub_shift(mr + fr, False), _sub_shift(mi + fi, False)

    def fix(rows, pw):
        cr, ci = _cmul(pw[0], pw[1], ir, ii)
        tr, ti = lr[rows, :] + cr, li[rows, :] + ci
        lr[rows, :] = tr
        li[rows, :] = ti
        return tr, ti

    def grad_a(tr, ti, spr, spi, acc):
        return acc[0] + tr * spr + ti * spi, acc[1] + ti * spr - tr * spi

    def carry_in(k, c):
        i = SEG_LEN - 1 - k
        rows = pl.ds(pl.multiple_of(i * SEG, SEG), SEG)
        prev = pl.ds(pl.multiple_of((i - 1) * SEG, SEG), SEG)
        tr, ti = fix(rows, (c[0], c[1]))
        acc = grad_a(tr, ti, sr[prev, :], si[prev, :], (c[2], c[3]))
        nr, ni = _cmul(c[0], c[1], ar, ai)
        return nr, ni, acc[0], acc[1]

    pwr, pwi, accr, acci = lax.fori_loop(0, SEG_LEN - 1, carry_in, (ar, ai, zero, zero), unroll=5)
    tr, ti = fix(pl.ds(0, SEG), (pwr, pwi))
    last = pl.ds((SEG_LEN - 1) * SEG, SEG)
    accr, acci = grad_a(tr, ti, _sub_shift(sr[last, :], True), _sub_shift(si[last, :], True), (accr, acci))
    return jnp.sum(accr, axis=0, keepdims=True), jnp.sum(acci, axis=0, keepdims=True)


S5_BWD_VMEM = 58 * 2**20


def s5_backward(dg, proj, s_re, s_im, wb_re, wb_im, wc_re, wc_im, a_re, a_im, d, dproj, plans=()):
    def kern(dg_ref, u_ref, sr_in, si_in, wbr, wbi, wcr, wci, ar, ai, d_ref, _, du_ref, dd_ref, dwcr, dwci, dwbr, dwbi, dar, dai, lr, li, sr, si):
        u = _wide(u_ref[...])
        sr[...], si[...] = _wide(sr_in[...]), _wide(si_in[...])
        _, vjp_out = jax.vjp(fn_s5_out, sr[...], si[...], u, d_ref[...], wcr[...], wci[...])
        lr[...], li[...], du_out, dd_ref[...], dwcr[...], dwci[...] = vjp_out((_wide(dg_ref[...]),))
        dar[...], dai[...] = _adjoint_scan_in_place(lr, li, sr, si, ar[...], ai[...])
        _, vjp_in = jax.vjp(fn_s5_bu, u, wbr[...], wbi[...])
        du_in, dwbr[...], dwbi[...] = vjp_in((lr[...], li[...]))
        du_ref[...] = (du_out + du_in).astype(du_ref.dtype)

    u_spec, s_spec, wb_spec, wc_spec, a_spec, d_spec = _s5_specs()
    outs = [(SDS(dproj.shape, dproj.dtype), u_spec), (SDS(d.shape, F32), d_spec), (SDS(wc_re.shape, F32), wc_spec), (SDS(wc_im.shape, F32), wc_spec),
            (SDS(wb_re.shape, F32), wb_spec), (SDS(wb_im.shape, F32), wb_spec), (SDS(a_re.shape, F32), a_spec), (SDS(a_im.shape, F32), a_spec)]
    return hosted_call(kern, name="s5_backward", grid=(S5_BLOCKS,),
                       in_specs=[u_spec, u_spec, s_spec, s_spec, wb_spec, wb_spec, wc_spec, wc_spec, a_spec, a_spec, d_spec, ANY],
                       out_specs=[sp for _, sp in outs], out_shape=[sd for sd, _ in outs], aliases={11: 0},
                       operands=[dg, proj, s_re, s_im, wb_re, wb_im, wc_re, wc_im, a_re, a_im, d, dproj],
                       scratch_shapes=[pltpu.VMEM((L, S5_LANES), F32)] * 4,
                       cparams=pltpu.CompilerParams(vmem_limit_bytes=S5_BWD_VMEM), plans=plans)


def fn_rms(x, g):
    return (rms(x, g),)


def fn_s5_disc(lre, lim, ls):
    step = jnp.exp(ls)
    e = jnp.exp(lre * step)
    a_re, a_im = e * jnp.cos(lim * step), e * jnp.sin(lim * step)
    den = lre * lre + lim * lim
    nr, ni = a_re - 1.0, a_im
    return a_re, a_im, (nr * lre + ni * lim) / den, (ni * lre - nr * lim) / den


def _group_mask(rows, cols, row_div, col_div):
    r = lax.broadcasted_iota(jnp.int32, (rows, cols), 0) // row_div % 8
    c = lax.broadcasted_iota(jnp.int32, (rows, cols), 1) // col_div
    return r == c


def _spread(x, mask):
    w = x.shape[1]
    copy = (lax.broadcasted_iota(jnp.int32, (w, 8 * w), 1) % w == lax.broadcasted_iota(jnp.int32, (w, 8 * w), 0)).astype(F32)
    return jnp.where(mask, jnp.dot(x, copy, precision=lax.Precision.HIGHEST, preferred_element_type=F32), 0.0)


def fn_s5_bmat(b_re, b_im, coef_re, coef_im):
    mask = _group_mask(b_re.shape[0], 8 * SC, SP, SC)
    return _spread(coef_re * b_re - coef_im * b_im, mask), _spread(coef_re * b_im + coef_im * b_re, mask)


def fn_s5_cmat(c_re, c_im):
    mask = _group_mask(c_re.shape[0], 8 * SP, SC, SP)
    return _spread(c_re, mask), _spread(c_im, mask)


def fn_s5_bu(u, wb_re, wb_im):
    return mm_nt(u, wb_re), mm_nt(u, wb_im)


def fn_s5_out(sr, si, u, d, wc_re, wc_im):
    y = mm_nt(sr, wc_re) - mm_nt(si, wc_im) + d * u
    return (jax.nn.gelu(y),)


def fn_merge_glu(z, mo, gate):
    yg = z[:, :PW] * jax.nn.sigmoid(z[:, PW:])
    return (jnp.concatenate([yg, mo], axis=1) * silu(gate),)


def fn_merge(prim, mo, gate):
    return (jnp.concatenate([prim, mo], axis=1) * silu(gate),)


def fn_mem_k(kv, g):
    return (jnp.concatenate([rms(kv[:, h * XHD:(h + 1) * XHD], g) for h in range(XH)], axis=1),)


def fn_mem_attn(xq, kn, v, g):
    outs = []
    for h in range(XH):
        sl = slice(h * XHD, (h + 1) * XHD)
        p = softmax_rows(mm_nt(rms(xq[:, sl], g), kn[:, sl]) * (XHD ** -0.5))
        outs.append(mm_nn(p, v[:, sl]))
    return (jnp.concatenate(outs, axis=1),)


def _half_rms(x, g):
    lo = lax.broadcasted_iota(jnp.int32, x.shape, 1) < ROPE
    x2 = x * x
    s_lo = jnp.sum(jnp.where(lo, x2, 0.0), axis=1, keepdims=True)
    s_hi = jnp.sum(jnp.where(lo, 0.0, x2), axis=1, keepdims=True)
    return x * lax.rsqrt(jnp.where(lo, s_lo, s_hi) / ROPE + EPS) * g


def _rope(x, cos2, sin_signed):
    first = lax.broadcasted_iota(jnp.int32, x.shape, 1) % ROPE < ROPE // 2
    return x * cos2 + jnp.where(first, lane_roll(x, 128 - ROPE // 2), lane_roll(x, ROPE // 2)) * sin_signed


def fn_mla_prep(q, kv, kr, cos2, sin_signed, qnn, knn, qrn, krn):
    lo = lax.broadcasted_iota(jnp.int32, kr.shape, 1) < ROPE
    kr_pad = jnp.where(lo, _rope(_half_rms(kr, krn), cos2, sin_signed), 0.0)
    qf, kf, vs = [], [], []
    for m in range(MH // 2):
        pair = _rope(_half_rms(q[:, MH * NOPE + 128 * m:MH * NOPE + 128 * (m + 1)], qrn), cos2, sin_signed)
        for h, rope_h in ((2 * m, pair), (2 * m + 1, lane_roll(pair, ROPE))):
            qf.append(jnp.concatenate([rms(q[:, NOPE * h:NOPE * (h + 1)], qnn), jnp.where(lo, rope_h, 0.0)], axis=1))
    for h in range(MH):
        kf.append(jnp.concatenate([rms(kv[:, 256 * h:256 * h + NOPE], knn), kr_pad], axis=1))
        vs.append(kv[:, 256 * h + NOPE:256 * (h + 1)])
    return jnp.stack(qf), jnp.stack(kf), jnp.stack(vs)


ATT_TQ = 512


def _attn_scores(q, kf):
    tq = q.shape[0]
    scale = (NOPE + ROPE) ** -0.5
    own = _dg(q, kf[-tq:], 1, 1) * scale
    own = jnp.where(lax.broadcasted_iota(jnp.int32, own.shape, 1) <= lax.broadcasted_iota(jnp.int32, own.shape, 0), own, jnp.finfo(F32).min)
    return own if kf.shape[0] == tq else jnp.concatenate([_dg(q, kf[:-tq], 1, 1) * scale, own], axis=1)


ATT_FWD_HEADS, ATT_BWD_HEADS = 4, 2


def _attn_specs(heads):
    q_spec = pl.BlockSpec((heads, ATT_TQ, 256), lambda h, i: (h, i, 0))
    k_spec = pl.BlockSpec((heads, L, 256), lambda h, i: (h, 0, 0))
    v_spec = pl.BlockSpec((heads, L, 128), lambda h, i: (h, 0, 0))
    o_spec = pl.BlockSpec((ATT_TQ, heads * 128), lambda h, i: (i, h))
    lse_spec = pl.BlockSpec((heads, ATT_TQ, 1), lambda h, i: (h, i, 0))
    return q_spec, k_spec, v_spec, o_spec, lse_spec


def _attn_branches(heads, body):
    i = pl.program_id(1)
    for t in range(L // ATT_TQ):
        @pl.when(i == t)
        def _(t=t):
            for hh in range(heads):
                body(hh, slice(hh * 128, (hh + 1) * 128), (t + 1) * ATT_TQ)


def causal_attn(qf, kf, vh):
    def kern(q_ref, k_ref, v_ref, o_ref, lse_ref):
        def body(hh, lanes, keys):
            s = _attn_scores(q_ref[hh], k_ref[hh, :keys, :])
            m = jnp.max(s, axis=-1, keepdims=True)
            e = jnp.exp(s - m)
            total = jnp.sum(e, axis=-1, keepdims=True)
            o_ref[:, lanes] = (_dg(e, v_ref[hh, :keys, :], 1, 0) / total).astype(o_ref.dtype)
            lse_ref[hh] = m + jnp.log(total)

        _attn_branches(ATT_FWD_HEADS, body)

    q_spec, k_spec, v_spec, o_spec, lse_spec = _attn_specs(ATT_FWD_HEADS)
    return pl.pallas_call(kern, grid=(MH // ATT_FWD_HEADS, L // ATT_TQ), in_specs=[q_spec, k_spec, v_spec], out_specs=[o_spec, lse_spec],
                          out_shape=[SDS((L, MH * VD), F32), SDS((MH, L, 1), F32)], name="l1_attn", compiler_params=_cparams())(qf, kf, vh)


def causal_attn_bwd(qf, kf, vh, out, lse, dout):
    scale = (NOPE + ROPE) ** -0.5

    def kern(q_ref, k_ref, v_ref, o_ref, lse_ref, do_ref, dq_ref, dk_ref, dv_ref):
        @pl.when(pl.program_id(1) == 0)
        def _():
            dk_ref[...] = jnp.zeros_like(dk_ref)
            dv_ref[...] = jnp.zeros_like(dv_ref)

        def body(hh, lanes, keys):
            q, k, v, do = q_ref[hh], k_ref[hh, :keys, :], v_ref[hh, :keys, :], do_ref[:, lanes]
            p = jnp.exp(_attn_scores(q, k) - lse_ref[hh])
            delta = jnp.sum(do * _wide(o_ref[:, lanes]), axis=-1, keepdims=True)
            dv_ref[hh, :keys, :] += _dg(p, do, 0, 0)
            ds = p * (_dg(do, v, 1, 1) - delta) * scale
            dq_ref[hh] = _dg(ds, k, 1, 0)
            dk_ref[hh, :keys, :] += _dg(ds, q, 0, 0)

        _attn_branches(ATT_BWD_HEADS, body)

    q_spec, k_spec, v_spec, o_spec, lse_spec = _attn_specs(ATT_BWD_HEADS)
    return pl.pallas_call(kern, grid=(MH // ATT_BWD_HEADS, L // ATT_TQ), in_specs=[q_spec, k_spec, v_spec, o_spec, lse_spec, o_spec],
                          out_specs=[q_spec, k_spec, v_spec], out_shape=[SDS(qf.shape, F32), SDS(kf.shape, F32), SDS(vh.shape, F32)],
                          name="l1_attn_bwd", compiler_params=_cparams())(qf, kf, vh, out, lse, dout)


def loss_and_grad(y, target, tl=512):
    def kern(y_ref, t_ref, dy_ref, loss_ref):
        d = y_ref[...] - t_ref[...]
        dy_ref[...] = d / D

        @pl.when(pl.program_id(0) == 0)
        def _():
            loss_ref[...] = jnp.zeros_like(loss_ref)

        loss_ref[...] += 0.5 * jnp.sum(jnp.sum(d * d, axis=1, keepdims=True), axis=0, keepdims=True) / D

    return pl.pallas_call(kern, grid=(L // tl,), in_specs=[rspec(tl, D), rspec(tl, D)], out_specs=[rspec(tl, D), cspec((1, 1))],
                          out_shape=[SDS((L, D), F32), SDS((1, 1), F32)], name="loss", compiler_params=_cparams())(y, target)


def adamw(name, w, g, m, v):
    rows, cols = w.shape
    block_row_bytes = 7 * 2 * 4 * max(cols, 128)
    tr = _row_tile(rows, min(2048, ADAMW_VMEM // block_row_bytes // 8 * 8), 8)

    def kern(w_ref, g_ref, m_ref, v_ref, d_ref, nm_ref, nv_ref):
        gg = g_ref[...]
        nm = ADAM_B1 * m_ref[...] + (1.0 - ADAM_B1) * gg
        nv = ADAM_B2 * v_ref[...] + (1.0 - ADAM_B2) * jnp.square(gg)
        m_hat = nm / (1.0 - ADAM_B1 ** ADAM_STEP)
        v_hat = nv / (1.0 - ADAM_B2 ** ADAM_STEP)
        d_ref[...] = -ADAM_LR * (m_hat / (jnp.sqrt(v_hat) + ADAM_EPS) + ADAM_WD * w_ref[...])
        nm_ref[...] = nm
        nv_ref[...] = nv

    spec = rspec(tr, cols)
    return hosted_call(kern, name=name, grid=(rows // tr,), in_specs=[spec] * 4, out_specs=[spec] * 3,
                       out_shape=[SDS((rows, cols), F32)] * 3, operands=[w, g, m, v])


def _row_tile(rows, cap=512, unit=16):
    return max(t for t in range(unit, cap + 1, unit) if rows % t == 0)


def _place():
    x, y, c = lax.axis_index("x"), lax.axis_index("y"), lax.axis_index("c")
    return x, y, c, [(1 - x, y), (x, 1 - y), (1 - x, 1 - y)]


def _row_chunks(rows, n, dtype):
    unit = 32 // jnp.dtype(dtype).itemsize
    base, extra = divmod(rows // unit, n)
    out, start = [], 0
    for k in range(n):
        size = (base + (k < extra)) * unit
        if size:
            out.append((start, size))
            start += size
    assert start == rows, (rows, unit)
    return out


PIECE_BYTES = 1 << 20


def _pieces(shapes_dtypes, rows_of):
    out = []
    for b, (shape, dtype) in enumerate(shapes_dtypes):
        rows = rows_of(shape)
        n = max(1, min(4, rows * shape[-1] * jnp.dtype(dtype).itemsize // PIECE_BYTES))
        out += [(b, st, sz) for st, sz in _row_chunks(rows, n, dtype)]
    return out


def all_gather_chips(name, shards):
    nb = len(shards)
    pieces = _pieces([(s.shape, s.dtype) for s in shards], lambda shape: shape[0] // 2)
    n = len(pieces)

    def body(*refs):
        x_refs, out_refs, send_sems, recv_sems = refs[:nb], refs[nb:2 * nb], refs[2 * nb], refs[2 * nb + 1]
        x, y, c, chips = _place()
        sibling = (x, y, 1 - c)
        mine = 2 * x + y

        def copy(sem, chip, cc, k, to, from_input=False):
            b, st, sz = pieces[k]
            rows_k = pl.ds(cc * (x_refs[b].shape[0] // 2) + st, sz)
            dst = out_refs[b].at[chip, rows_k, :]
            return pltpu.make_async_remote_copy(src_ref=x_refs[b].at[rows_k, :] if from_input else dst, dst_ref=dst,
                                                send_sem=send_sems.at[sem], recv_sem=recv_sems.at[sem], device_id=to, device_id_type=MESH_ID)

        order = [(k, j, 2 * cx + cy, (cx, cy, c)) for k in range(n) for j, (cx, cy) in enumerate(chips)]
        first = [copy(j * n + k, mine, c, k, to, from_input=True) for k, j, _, to in order]
        for cp in first:
            cp.start()
        passed = []
        for k, j, chip, _ in order:
            copy(j * n + k, chip, c, k, sibling).wait_recv()
            passed.append(copy((3 + j) * n + k, chip, c, k, sibling))
            passed[-1].start()
        for k, j, chip, _ in order:
            copy((3 + j) * n + k, chip, 1 - c, k, sibling).wait_recv()
        for cp in first + passed:
            cp.wait_send()

    return pl.pallas_call(body, in_specs=[ANY] * nb, out_specs=[ANY] * nb, out_shape=[SDS((4,) + s.shape, s.dtype) for s in shards],
                          scratch_shapes=_dma_sems(6 * n), name=name)(*shards)


def plan_gather_ici(shards):
    pieces = _pieces([(s.shape, s.dtype) for s in shards], lambda shape: shape[0] // 2)
    n = len(pieces)

    def copies(x_refs, out_refs, send_sems, recv_sems):
        x, y, c, chips = _place()
        mine = 2 * x + y

        def copy(j, k, chip, to, from_input):
            b, st, sz = pieces[k]
            rows_k = pl.ds(c * (x_refs[b].shape[0] // 2) + st, sz)
            dst = out_refs[b].at[chip, rows_k, :]
            return pltpu.make_async_remote_copy(src_ref=x_refs[b].at[rows_k, :] if from_input else dst, dst_ref=dst, send_sem=send_sems.at[j * n + k],
                                                recv_sem=recv_sems.at[j * n + k], device_id=to, device_id_type=MESH_ID)

        order = [(k, j, 2 * cx + cy, (cx, cy, c)) for k in range(n) for j, (cx, cy) in enumerate(chips)]
        return [copy(j, k, mine, to, True) for k, j, _, to in order], [copy(j, k, chip, to, False) for k, j, chip, to in order]

    return Plan(shards, [SDS((4,) + s.shape, s.dtype) for s in shards], {}, 3 * n, copies)


def plan_gather_pass(gathered):
    pieces = _pieces([(g.shape[1:], g.dtype) for g in gathered], lambda shape: shape[0] // 2)
    n = len(pieces)

    def copies(_, out_refs, send_sems, recv_sems):
        x, y, c, chips = _place()

        def copy(j, k, chip, cc):
            b, st, sz = pieces[k]
            rows_k = out_refs[b].at[chip, pl.ds(cc * (out_refs[b].shape[1] // 2) + st, sz), :]
            return pltpu.make_async_remote_copy(src_ref=rows_k, dst_ref=rows_k, send_sem=send_sems.at[j * n + k], recv_sem=recv_sems.at[j * n + k],
                                                device_id=(x, y, 1 - c), device_id_type=MESH_ID)

        order = [(k, j, 2 * cx + cy) for k in range(n) for j, (cx, cy) in enumerate(chips)]
        return [copy(j, k, chip, c) for k, j, chip in order], [copy(j, k, chip, 1 - c) for k, j, chip in order]

    return Plan(gathered, [SDS(g.shape, g.dtype) for g in gathered], {i: i for i in range(len(gathered))}, 3 * n, copies)


def plan_pair_exchange(gs):
    pieces = _pieces([(g.shape, g.dtype) for g in gs], lambda shape: shape[1] // 2)

    def copies(g_refs, got_refs, send_sems, recv_sems):
        x, y, c, _ = _place()
        swaps = [pltpu.make_async_remote_copy(src_ref=g_refs[b].at[:, pl.ds((1 - c) * (g_refs[b].shape[1] // 2) + st, sz), :],
                                              dst_ref=got_refs[b].at[:, pl.ds(st, sz), :], send_sem=send_sems.at[k], recv_sem=recv_sems.at[k],
                                              device_id=(x, y, 1 - c), device_id_type=MESH_ID)
                 for k, (b, st, sz) in enumerate(pieces)]
        return swaps, swaps

    return Plan(gs, [SDS((g.shape[0], g.shape[1] // 2, g.shape[2]), g.dtype) for g in gs], {}, len(pieces), copies)


def plan_chip_scatter(ps):
    pieces = _pieces([(p.shape, p.dtype) for p in ps], lambda shape: shape[1])
    n = len(pieces)

    def copies(p_refs, q_refs, send_sems, recv_sems):
        x, y, c, chips = _place()
        mine = 2 * x + y

        def copy(j, k, src_slot, dst_slot, to):
            b, st, sz = pieces[k]
            return pltpu.make_async_remote_copy(src_ref=p_refs[b].at[src_slot, pl.ds(st, sz), :], dst_ref=q_refs[b].at[dst_slot, pl.ds(st, sz), :],
                                                send_sem=send_sems.at[j * n + k], recv_sem=recv_sems.at[j * n + k], device_id=to,
                                                device_id_type=MESH_ID)

        order = [(k, j, 2 * cx + cy, (cx, cy, c)) for k in range(n) for j, (cx, cy) in enumerate(chips)]
        return [copy(j, k, chip, mine, to) for k, j, chip, to in order], [copy(j, k, mine, chip, to) for k, j, chip, to in order]

    return Plan(ps, [SDS(p.shape, p.dtype) for p in ps], {}, 3 * n, copies)


def plan_pair_join(bufs):
    pieces = _pieces([(b.shape, b.dtype) for b in bufs], lambda shape: shape[0] // 2)

    def copies(_, out_refs, send_sems, recv_sems):
        x, y, c, _ = _place()

        def copy(k, cc):
            b, st, sz = pieces[k]
            rows_k = out_refs[b].at[pl.ds(cc * (out_refs[b].shape[0] // 2) + st, sz), :]
            return pltpu.make_async_remote_copy(src_ref=rows_k, dst_ref=rows_k, send_sem=send_sems.at[k], recv_sem=recv_sems.at[k],
                                                device_id=(x, y, 1 - c), device_id_type=MESH_ID)

        return [copy(k, c) for k in range(len(pieces))], [copy(k, 1 - c) for k in range(len(pieces))]

    return Plan(bufs, [SDS(b.shape, b.dtype) for b in bufs], {i: i for i in range(len(bufs))}, len(pieces), copies)


def run_plan(name, plan):
    hosted_call(lambda: None, name=name, grid=(1,), in_specs=[], out_specs=[], out_shape=[], operands=[], plans=[plan])
    return plan.results


HBM = pl.BlockSpec(memory_space=pltpu.HBM)
SEMS = pl.BlockSpec(memory_space=pltpu.SEMAPHORE)
SPLIT_PARAMS = dict(has_side_effects=pltpu.SideEffectType.DATAFLOW_SIDE_EFFECTING)


def _plan_buffers(plan):
    in_place = {o: i for i, o in plan.aliases.items()}
    bufs = [pltpu.with_memory_space_constraint(a, pltpu.HBM) for a in plan.operands]
    where = []
    for o, sd in enumerate(plan.out_shape):
        if o in in_place:
            where.append(in_place[o])
        else:
            where.append(len(bufs))
            bufs.append(pltpu.with_memory_space_constraint(lax.empty(sd.shape, sd.dtype), pltpu.HBM))
    return bufs, where


def split_start(name, plans):
    layout = [_plan_buffers(p) for p in plans]
    counts = [len(b) for b, _ in layout]
    n_buf = sum(counts)

    def body(*refs):
        sems, token = refs[n_buf:n_buf + 2 * len(plans)], refs[-1]
        pos = 0
        for k, (p, (_, where)) in enumerate(zip(plans, layout)):
            mine = refs[pos:pos + counts[k]]
            pos += counts[k]
            sends, _ = p.copies(mine[:len(p.operands)], [mine[w] for w in where], sems[2 * k], sems[2 * k + 1])
            for cp in sends:
                cp.start()
        token[...] = jnp.zeros_like(token)

    bufs = [b for bs, _ in layout for b in bs]
    res = pl.pallas_call(
        body, name=name, in_specs=[HBM] * n_buf,
        out_specs=[SEMS] * (2 * len(plans)) + [HBM] * n_buf + [pl.BlockSpec(memory_space=pltpu.VMEM)],
        out_shape=[pltpu.SemaphoreType.DMA((p.n_sems,)) for p in plans for _ in range(2)] + [pltpu.HBM(b.shape, b.dtype) for b in bufs]
        + [SDS((8, 128), F32)],
        input_output_aliases={i: 2 * len(plans) + i for i in range(n_buf)}, compiler_params=pltpu.CompilerParams(**SPLIT_PARAMS))(*bufs)
    pos = 2 * len(plans)
    for k, p in enumerate(plans):
        p.in_flight = (res[2 * k], res[2 * k + 1], list(res[pos:pos + counts[k]]), layout[k][1])
        pos += counts[k]
    return res[-1]


def split_wait(name, plan, after):
    send_sems, recv_sems, bufs, where = plan.in_flight
    n_buf = len(bufs)

    def body(*refs):
        mine = refs[:n_buf]
        sends, recvs = plan.copies(mine[:len(plan.operands)], [mine[w] for w in where], refs[n_buf], refs[n_buf + 1])
        for cp in recvs:
            cp.wait_recv()
        for cp in sends:
            cp.wait_send()

    res = pl.pallas_call(body, name=name, in_specs=[HBM] * n_buf + [SEMS, SEMS, ANY], out_specs=[HBM] * n_buf,
                         out_shape=[pltpu.HBM(b.shape, b.dtype) for b in bufs], input_output_aliases={i: i for i in range(n_buf)},
                         compiler_params=pltpu.CompilerParams(**SPLIT_PARAMS))(*bufs, send_sems, recv_sems, after)
    plan.results = [res[w] for w in where]
    return plan.results


def pair_add(name, g, got, place):
    slots, rows, cols = g.shape
    half = rows // 2
    tr = _row_tile(half)
    nb = half // tr

    def kern(_, g_ref, t_ref, o_ref):
        o_ref[...] = (g_ref[...].astype(F32) + t_ref[...].astype(F32)).astype(o_ref.dtype)

    blk = pl.BlockSpec((None, tr, cols), lambda s, i, p: (s, i, 0))
    grid_spec = pltpu.PrefetchScalarGridSpec(
        num_scalar_prefetch=1, grid=(slots, nb),
        in_specs=[pl.BlockSpec((None, tr, cols), lambda s, i, p: (s, p[1] * nb + i, 0)), blk], out_specs=blk)
    return pl.pallas_call(kern, grid_spec=grid_spec, out_shape=SDS((slots, half, cols), g.dtype), name=name,
                          compiler_params=_cparams())(place, g, got)


def chip_add(name, p, q, place):
    slots, half, cols = p.shape
    tr = _row_tile(half)
    nb = half // tr

    def kern(_, p_ref, q1, q2, q3, o_ref):
        o_ref[...] = p_ref[...].astype(F32) + q1[...].astype(F32) + q2[...].astype(F32) + q3[...].astype(F32)

    def slot(k):
        return pl.BlockSpec((None, tr, cols), lambda i, pr: ((pr[0] + k) % slots, i, 0))

    grid_spec = pltpu.PrefetchScalarGridSpec(
        num_scalar_prefetch=1, grid=(nb,), in_specs=[slot(0), slot(1), slot(2), slot(3)],
        out_specs=pl.BlockSpec((tr, cols), lambda i, pr: (pr[1] * nb + i, 0)))
    return pl.pallas_call(kern, grid_spec=grid_spec, out_shape=SDS((2 * half, cols), F32), name=name,
                          compiler_params=_cparams())(place, p, q, q, q)


def pair_adds(tag, gs, gots, place):
    return [pair_add(f"{tag}_pair_add_{i}", g, got, place) for i, (g, got) in enumerate(zip(gs, gots))]


def chip_adds(tag, pairs, qs, place):
    return [chip_add(f"{tag}_chip_add_{i}", p, q, place) for i, (p, q) in enumerate(zip(pairs, qs))]


def reduce_scatter_chips(tag, gs, place):
    pairs = pair_adds(tag, gs, run_plan(tag + "_pair_exchange", plan_pair_exchange(gs)), place)
    return run_plan(tag + "_pair_join", plan_pair_join(chip_adds(tag, pairs, run_plan(tag + "_chip_scatter", plan_chip_scatter(pairs)), place)))


BIG = [("w_out", (2, 512, 1024)), ("w_mem_kv", (2, 256, 1024)), ("s5_w_in", (1, 1024, 1024)), ("s5_w_glu", (1, 1536, 768)),
       ("mla_w_in", (1, 1024, 848)), ("mla_w_uq", (1, 512, 576)), ("mla_w_ukv", (1, 256, 768))]
SHARDED_SMALL = [("mla_q_lora_norm", (1, 128)), ("mla_kv_lora_norm", (1, 64))]
SMALL = [("ln_gain", (2, 1024)), ("mem_norm", (2, 1024)), ("xq_norm", (2, 128)), ("xk_norm", (2, 128)),
         ("s5_lambda_re", (1, 96, 64)), ("s5_lambda_im", (1, 96, 64)), ("s5_log_step", (1, 96)),
         ("s5_b_re", (1, 96, 64, 16)), ("s5_b_im", (1, 96, 64, 16)), ("s5_c_re", (1, 96, 16, 64)), ("s5_c_im", (1, 96, 16, 64)),
         ("s5_d", (1, 1536)), ("mla_q_nope_norm", (1, 128)), ("mla_k_nope_norm", (1, 128)), ("mla_q_rope_norm", (1, 64)),
         ("mla_k_rope_norm", (1, 64))]
WEIGHT_ORDER = ["ln_gain", "w_out", "mem_norm", "w_mem_kv", "xq_norm", "xk_norm", "s5_w_in", "s5_lambda_re", "s5_lambda_im",
                "s5_log_step", "s5_b_re", "s5_b_im", "s5_c_re", "s5_c_im", "s5_d", "s5_w_glu", "mla_w_in", "mla_q_lora_norm",
                "mla_kv_lora_norm", "mla_w_uq", "mla_w_ukv", "mla_q_nope_norm", "mla_k_nope_norm", "mla_q_rope_norm", "mla_k_rope_norm"]
MINOR_LAST = {"mla_w_in": (0, 2, 1), "mla_w_uq": (0, 2, 1), "s5_b_re": (0, 2, 3, 1), "s5_b_im": (0, 2, 3, 1),
              "s5_c_re": (0, 2, 3, 1), "s5_c_im": (0, 2, 3, 1)}
SMALL_FULL = SMALL + [(n, (1, 4 * s[1])) for n, s in SHARDED_SMALL]
N_SMALL = sum(math.prod(s) for _, s in SMALL_FULL)
SMALL_ROWS, SMALL_LANES = 128, 1024

PAIR_OUT, PAIR_MKV, PAIR_ROWS = 0, 512, 768


def stack_shards(w, dtype):
    pairs = [jnp.concatenate([w["w_out"][l], w["w_mem_kv"][l]], axis=0).astype(dtype) for l in range(2)]
    return ([w["s5_w_in"][0].astype(dtype)], [pairs[0], w["s5_w_glu"][0].astype(dtype)],
            [pairs[1], w["mla_w_ukv"][0].astype(dtype), w["mla_w_in"][0].astype(dtype), w["mla_w_uq"][0].astype(dtype)])


def pair_views(pair):
    return {"w_out": Sharded(pair, "row", PAIR_OUT, 512), "w_mem_kv": Sharded(pair, "row", PAIR_MKV, 256)}


def grad_views():
    pair = SDS((4, PAIR_ROWS, 1024), BF16)
    return {"w_out": Sharded(pair, "row", PAIR_OUT, 512), "w_mem_kv": Sharded(pair, "row", PAIR_MKV, 256),
            "s5_w_in": Sharded(SDS((4, 1024, 1024), BF16), "col", 0, 1024), "s5_w_glu": Sharded(SDS((4, 1536, 768), BF16), "col", 0, 1536),
            "mla_w_ukv": Sharded(SDS((4, 256, 768), BF16), "col", 0, 256)}


def cols_to_shards(full):
    return full.reshape(full.shape[0], 4, full.shape[1] // 4).transpose(1, 0, 2)


def shards_to_cols(arr):
    return arr.transpose(1, 0, 2).reshape(arr.shape[1], 4 * arr.shape[2])


def mla_in_permute(w):
    o1, o2, o3, o4 = QL, QL + KVL, QL + KVL + ROPE, QL + KVL + ROPE + XQW
    return jnp.concatenate([w[:, o4:], w[:, :o1], w[:, o3:o4], w[:, o1:o2], w[:, o2:o3],
                            jnp.zeros((w.shape[0], MLA_IN_P - MLA_IN), w.dtype)], axis=1)


def mla_in_unpermute(d):
    return jnp.concatenate([d[:, 2048:2560], d[:, 3072:3328], d[:, 3328:3392], d[:, 2560:3072], d[:, :2048]], axis=1)


def uq_permute(w):
    w3 = w.reshape(w.shape[0], MH, NOPE + ROPE)
    return jnp.concatenate([w3[:, :, :NOPE].reshape(w.shape[0], MH * NOPE), w3[:, :, NOPE:].reshape(w.shape[0], MH * ROPE)], axis=1)


def uq_unpermute(d):
    dn = d[:, :MH * NOPE].reshape(d.shape[0], MH, NOPE)
    dr = d[:, MH * NOPE:].reshape(d.shape[0], MH, ROPE)
    return jnp.concatenate([dn, dr], axis=2).reshape(d.shape[0], MH * (NOPE + ROPE))


def time_permute(a):
    return a.reshape(SEG, SEG_LEN, a.shape[-1]).transpose(1, 0, 2).reshape(L, a.shape[-1])


def time_unpermute(a):
    return a.reshape(SEG_LEN, SEG, a.shape[-1]).transpose(1, 0, 2).reshape(L, a.shape[-1])


def mem_branch_fwd(tag, mem, mem_norm, w_mem_kv, xk_norm):
    mn = row_fwd(tag + "_mem_rms", fn_rms, ML, ML, [(mem, D, 0)], [mem_norm], [(D, BF16)])[0]
    kv = matmul(tag + "_mem_kv", mn, w_mem_kv, "nn", F32)
    kn = row_fwd(tag + "_mem_knorm", fn_mem_k, ML, ML, [(kv, XQW, 0)], [xk_norm], [(XQW, F32)])[0]
    return mn, kv, kn


def mem_branch_bwd(tag, mem, mem_norm, w_mem_kv, xk_norm, mn, kv, dkn, dv, g_view, g_wide):
    dk, dxk = row_bwd(tag + "_mem_knorm_bwd", fn_mem_k, ML, ML, [(kv, XQW, 0)], [xk_norm], [(dkn, XQW, 0)], [True], [True])
    dkv = jnp.concatenate([dk, dv], axis=1)
    dmn = matmul(tag + "_mem_kv_dx", dkv, w_mem_kv, "nt", F32)
    g_wide = matmul(tag + "_mem_kv_dw", mn, dkv, "tn", out=g_view, into=g_wide)
    dmem_norm = row_bwd(tag + "_mem_rms_bwd", fn_rms, ML, ML, [(mem, D, 0)], [mem_norm], [(dmn, D, 0)], [False], [True])[0]
    return g_wide, dmem_norm, dxk


def mem_attn_fwd(tag, proj, cb, kn, kv, xq_norm):
    return row_fwd(tag + "_mem_attn", fn_mem_attn, L, ROW_TILE, [(proj, XQW, cb)], [kn, kv[:, XQW:], xq_norm], [(XQW, F32)])[0]


def mem_attn_bwd(tag, proj, cb, kn, kv, xq_norm, dmo, dproj):
    place = {"cols": proj.shape[1], "cb": cb, "into": dproj, "dtype": dproj.dtype}
    return row_bwd(tag + "_mem_attn_bwd", fn_mem_attn, L, ROW_TILE, [(proj, XQW, cb)], [kn, kv[:, XQW:], xq_norm], [(dmo, XQW, 0)],
                   [place], [True, True, True])


def device_step(x, mem, positions, target, small, env, hooks=None):
    hooks = hooks or {}

    def plans_for(name):
        return hooks[("plans", name)](env) if ("plans", name) in hooks else ()

    def around(when, name, last=None):
        if (when, name) in hooks:
            hooks[(when, name)](env, last)

    g = {}
    gw = grad_views()
    ln, mem_norm, xq_norm, xk_norm = small["ln_gain"], small["mem_norm"], small["xq_norm"], small["xk_norm"]

    lre, lim = small["s5_lambda_re"][0], small["s5_lambda_im"][0]
    ls = small["s5_log_step"].reshape(SG, 1)
    one = pl.BlockSpec((SG, SP), lambda i: (0, 0))
    col = pl.BlockSpec((SG, 1), lambda i: (0, 0))
    disc_ins = [(lre, one), (lim, one), (ls, col)]
    a_re, a_im, coef_re, coef_im = stage("s5_disc", fn_s5_disc, (1,), disc_ins, [(SDS((SG, SP), F32), one)] * 4)
    b_re, b_im = small["s5_b_re"].reshape(SN, SC), small["s5_b_im"].reshape(SN, SC)
    c_re, c_im = small["s5_c_re"].reshape(PW, SP), small["s5_c_im"].reshape(PW, SP)
    bmat_rows = [(b_re, SC, 0), (b_im, SC, 0), (coef_re.reshape(SN, 1), 1, 0), (coef_im.reshape(SN, 1), 1, 0)]
    wb_re, wb_im = row_fwd("s5_bmat", fn_s5_bmat, SN, BMAT_TILE, bmat_rows, [], [(128, F32)] * 2)
    cmat_rows = [(c_re, SP, 0), (c_im, SP, 0)]
    wc_re, wc_im = row_fwd("s5_cmat", fn_s5_cmat, PW, CMAT_TILE, cmat_rows, [], [(512, F32)] * 2)
    a_re_v, a_im_v = a_re.reshape(1, SN), a_im.reshape(1, SN)
    s5_d = small["s5_d"]

    xp = time_permute(x)
    h0 = row_fwd("l0_rms", fn_rms, L, ROW_TILE, [(xp, D, 0)], [ln[0:1]], [(D, BF16)])[0]
    around("before", "l0_in", wb_re)
    w_in0 = Sharded(env["in0"], "col", 0, 1024)
    proj0 = matmul("l0_in", h0, w_in0, "nn")
    s_re, s_im, g0 = s5_forward(proj0, wb_re, wb_im, wc_re, wc_im, a_re_v, a_im_v, s5_d, plans=plans_for("s5_forward"))
    around("before", "l0_glu", g0)
    w0 = dict(pair_views(env["pair0"]), s5_w_glu=Sharded(env["glu"], "col", 0, 1536))
    z0 = matmul("l0_glu", g0, w0["s5_w_glu"], "nn", plans=plans_for("l0_glu"))
    mn0, kv0, kn0 = mem_branch_fwd("l0", mem, mem_norm[0:1], w0["w_mem_kv"], xk_norm[0:1])
    mo0 = mem_attn_fwd("l0", proj0, 3, kn0, kv0, xq_norm[0:1])
    o0 = row_fwd("l0_merge", fn_merge_glu, L, ROW_TILE, [(z0, 2 * PW, 0), (mo0, XQW, 0), (proj0, BW, 1)], [], [(BW, BF16)])[0]
    around("before", "l0_out", o0)
    x1p = matmul("l0_out", o0, w0["w_out"], "nn", F32, add=xp, plans=plans_for("l0_out"))
    around("after", "l0_out", x1p)
    x1 = time_unpermute(x1p)

    w1 = dict(pair_views(env["pair1"]), mla_w_ukv=Sharded(env["ukv"], "col", 0, 256))
    w_in1, w_uq = env["w_in1"], env["w_uq"]
    h1 = row_fwd("l1_rms", fn_rms, L, ROW_TILE, [(x1, D, 0)], [ln[1:2]], [(D, BF16)])[0]
    proj1 = matmul("l1_in", h1, w_in1, "nn")
    qln, kvln = env["q_lora_norm"].reshape(1, QL), env["kv_lora_norm"].reshape(1, KVL)
    cqn = row_fwd("l1_q_lora_rms", fn_rms, L, ROW_TILE, [(proj1, QL, 4)], [qln], [(QL, BF16)])[0]
    ckvn = row_fwd("l1_kv_lora_rms", fn_rms, L, ROW_TILE, [(proj1, KVL, 12)], [kvln], [(KVL, BF16)])[0]
    q = matmul("l1_uq", cqn, w_uq, "nn")
    kv = matmul("l1_ukv", ckvn, w1["mla_w_ukv"], "nn")
    inv_freq = ROPE_THETA ** (-jnp.arange(ROPE // 2, dtype=F32) / (ROPE // 2))
    ang = positions.astype(F32)[:, None] * inv_freq
    cos2 = jnp.tile(jnp.cos(ang), (1, 4))
    sin_signed = jnp.tile(jnp.concatenate([-jnp.sin(ang), jnp.sin(ang)], axis=1), (1, 2))
    qnn, knn = small["mla_q_nope_norm"], small["mla_k_nope_norm"]
    qrn, krn = jnp.tile(small["mla_q_rope_norm"], (1, 2)), jnp.tile(small["mla_k_rope_norm"], (1, 2))
    tp = 256
    prep_ins = [(q, rspec(tp, MH * (NOPE + ROPE))), (kv, rspec(tp, MH * 256)), (proj1, rspec(tp, 128, 26)),
                (cos2, rspec(tp, 128)), (sin_signed, rspec(tp, 128))] + [(a, cspec((1, 128))) for a in (qnn, knn, qrn, krn)]
    hq_spec = pl.BlockSpec((MH, tp, 256), lambda i: (0, i, 0))
    hv_spec = pl.BlockSpec((MH, tp, 128), lambda i: (0, i, 0))
    qf, kf, vh = stage("l1_mla_prep", fn_mla_prep, (L // tp,), prep_ins,
                       [(SDS((MH, L, 256), BF16), hq_spec), (SDS((MH, L, 256), BF16), hq_spec), (SDS((MH, L, 128), BF16), hv_spec)])
    attn, attn_lse = causal_attn(qf, kf, vh)
    mn1, kv1, kn1 = mem_branch_fwd("l1", mem, mem_norm[1:2], w1["w_mem_kv"], xk_norm[1:2])
    mo1 = mem_attn_fwd("l1", proj1, 5, kn1, kv1, xq_norm[1:2])
    o1 = row_fwd("l1_merge", fn_merge, L, ROW_TILE, [(attn, PW, 0), (mo1, XQW, 0), (proj1, BW, 0)], [], [(BW, BF16)])[0]
    x2 = matmul("l1_out", o1, w1["w_out"], "nn", F32, add=x1)
    dx2, loss = loss_and_grad(x2, target)
    around("after", "loss", loss)

    do1 = matmul("l1_out_dx", dx2, w1["w_out"], "nt")
    g_pair1 = matmul("l1_out_dw", o1, dx2, "tn", out=gw["w_out"])
    dattn, dmo1, dproj1 = row_bwd("l1_merge_bwd", fn_merge, L, ROW_TILE, [(attn, PW, 0), (mo1, XQW, 0), (proj1, BW, 0)], [],
                                  [(do1, BW, 0)], [True, True, {"cols": MLA_IN_P, "cb": 0, "dtype": BF16}], [])
    dproj1, dkn1, dv1, dxqn1 = mem_attn_bwd("l1", proj1, 5, kn1, kv1, xq_norm[1:2], dmo1, dproj1)
    env["g_pair1"], dmem_norm1, dxk1 = mem_branch_bwd("l1", mem, mem_norm[1:2], w1["w_mem_kv"], xk_norm[1:2], mn1, kv1, dkn1, dv1,
                                                      gw["w_mem_kv"], g_pair1)
    dqf, dkf, dvh = causal_attn_bwd(qf, kf, vh, attn, attn_lse, dattn)
    prep_diffs = [("row", SDS((L, MH * (NOPE + ROPE)), BF16), rspec(tp, MH * (NOPE + ROPE))), ("row", SDS((L, MH * 256), BF16), rspec(tp, MH * 256)),
                  ("row", SDS((L, MLA_IN_P), BF16), rspec(tp, 128, 26), {"into": dproj1}), None, None] + [("acc", (0,))] * 4
    dq, dkv, dproj1, dqnn, dknn, dqrn, dkrn = stage_bwd("l1_mla_prep_bwd", fn_mla_prep, (L // tp,), prep_ins,
                                                        [(dqf, hq_spec), (dkf, hq_spec), (dvh, hv_spec)], prep_diffs)
    dcqn = matmul("l1_uq_dx", dq, w_uq, "nt")
    env["dw_uq"] = matmul("l1_uq_dw", cqn, dq, "tn")
    dckvn = matmul("l1_ukv_dx", dkv, w1["mla_w_ukv"], "nt")
    env["g_ukv"] = matmul("l1_ukv_dw", ckvn, dkv, "tn", out=gw["mla_w_ukv"])
    dproj1, dqln = row_bwd("l1_q_lora_rms_bwd", fn_rms, L, ROW_TILE, [(proj1, QL, 4)], [qln], [(dcqn, QL, 0)],
                           [{"cols": MLA_IN_P, "cb": 4, "into": dproj1, "dtype": BF16}], [True])
    dproj1, dkvln = row_bwd("l1_kv_lora_rms_bwd", fn_rms, L, ROW_TILE, [(proj1, KVL, 12)], [kvln], [(dckvn, KVL, 0)],
                            [{"cols": MLA_IN_P, "cb": 12, "into": dproj1, "dtype": BF16}], [True])
    env["dw_in1"] = matmul("l1_in_dw", h1, dproj1, "tn")
    dh1 = matmul("l1_in_dx", dproj1, w_in1, "nt", plans=plans_for("l1_in_dx"))
    around("after", "l1_in_dx", dh1)
    dx1, dln1 = row_bwd("l1_rms_bwd", fn_rms, L, ROW_TILE, [(x1, D, 0)], [ln[1:2]], [(dh1, D, 0)], [{"add": (dx2, D, 0)}], [True])
    dx1p = time_permute(dx1)

    do0 = matmul("l0_out_dx", dx1p, w0["w_out"], "nt", plans=plans_for("l0_out_dx"))
    g_pair0 = matmul("l0_out_dw", o0, dx1p, "tn", out=gw["w_out"])
    dz0, dmo0, dproj0 = row_bwd("l0_merge_bwd", fn_merge_glu, L, ROW_TILE, [(z0, 2 * PW, 0), (mo0, XQW, 0), (proj0, BW, 1)], [],
                                [(do0, BW, 0)], [{"dtype": BF16}, True, {"cols": 2 * BW, "cb": 1, "dtype": BF16}], [])
    dproj0, dkn0, dv0, dxqn0 = mem_attn_bwd("l0", proj0, 3, kn0, kv0, xq_norm[0:1], dmo0, dproj0)
    env["g_pair0"], dmem_norm0, dxk0 = mem_branch_bwd("l0", mem, mem_norm[0:1], w0["w_mem_kv"], xk_norm[0:1], mn0, kv0, dkn0, dv0,
                                                      gw["w_mem_kv"], g_pair0)
    env["g_glu"] = matmul("l0_glu_dw", g0, dz0, "tn", out=gw["s5_w_glu"], plans=plans_for("l0_glu_dw"))
    dg0 = matmul("l0_glu_dx", dz0, w0["s5_w_glu"], "nt", plans=plans_for("l0_glu_dx"))
    around("before", "s5_backward", dg0)
    dproj0, dd, dwc_re, dwc_im, dwb_re, dwb_im, da_re, da_im = s5_backward(dg0, proj0, s_re, s_im, wb_re, wb_im, wc_re, wc_im,
                                                                           a_re_v, a_im_v, s5_d, dproj0, plans=plans_for("s5_backward"))
    around("after", "s5_backward", dd)
    env["g_in0"] = matmul("l0_in_dw", h0, dproj0, "tn", out=gw["s5_w_in"], plans=plans_for("l0_in_dw"))
    dh0 = matmul("l0_in_dx", dproj0, w_in0, "nt", plans=plans_for("l0_in_dx"))
    around("after", "l0_in_dx", dh0)
    dxp, dln0 = row_bwd("l0_rms_bwd", fn_rms, L, ROW_TILE, [(xp, D, 0)], [ln[0:1]], [(dh0, D, 0)], [{"add": (dx1p, D, 0)}], [True])
    grad_x = time_unpermute(dxp)

    db_re, db_im, dcoef_re, dcoef_im = row_bwd("s5_bmat_bwd", fn_s5_bmat, SN, BMAT_TILE, bmat_rows, [], [(dwb_re, 128, 0), (dwb_im, 128, 0)],
                                               [True] * 4, [], plans=plans_for("s5_bmat_bwd"))
    dc_re, dc_im = row_bwd("s5_cmat_bwd", fn_s5_cmat, PW, CMAT_TILE, cmat_rows, [], [(dwc_re, 512, 0), (dwc_im, 512, 0)], [True] * 2, [],
                           plans=plans_for("s5_cmat_bwd"))
    disc_cts = [(da_re.reshape(SG, SP), one), (da_im.reshape(SG, SP), one), (dcoef_re.reshape(SG, SP), one), (dcoef_im.reshape(SG, SP), one)]
    dlre, dlim, dls = stage_bwd("s5_disc_bwd", fn_s5_disc, (1,), disc_ins, disc_cts, [("acc", (0,))] * 3)

    g["ln_gain"] = jnp.concatenate([dln0, dln1], axis=0)
    g["mem_norm"] = jnp.concatenate([dmem_norm0, dmem_norm1], axis=0)
    g["xq_norm"] = jnp.concatenate([dxqn0, dxqn1], axis=0)
    g["xk_norm"] = jnp.concatenate([dxk0, dxk1], axis=0)
    g["s5_lambda_re"], g["s5_lambda_im"], g["s5_log_step"] = dlre, dlim, dls
    g["s5_b_re"], g["s5_b_im"], g["s5_c_re"], g["s5_c_im"] = db_re, db_im, dc_re, dc_im
    g["s5_d"] = dd
    g["mla_q_lora_norm"], g["mla_kv_lora_norm"] = dqln, dkvln
    g["mla_q_nope_norm"], g["mla_k_nope_norm"] = dqnn, dknn
    g["mla_q_rope_norm"] = dqrn[:, :ROPE] + dqrn[:, ROPE:]
    g["mla_k_rope_norm"] = dkrn[:, :ROPE] + dkrn[:, ROPE:]
    return loss, grad_x, g


def kernel(x, mem, positions, ln_gain, w_out, mem_norm, w_mem_kv, xq_norm, xk_norm, s5_w_in, s5_lambda_re, s5_lambda_im, s5_log_step, s5_b_re, s5_b_im, s5_c_re, s5_c_im, s5_d, s5_w_glu, mla_w_in, mla_q_lora_norm, mla_kv_lora_norm, mla_w_uq, mla_w_ukv, mla_q_nope_norm, mla_k_nope_norm, mla_q_rope_norm, mla_k_rope_norm, loss_target, m_ln_gain, m_w_out, m_mem_norm, m_w_mem_kv, m_xq_norm, m_xk_norm, m_s5_w_in, m_s5_lambda_re, m_s5_lambda_im, m_s5_log_step, m_s5_b_re, m_s5_b_im, m_s5_c_re, m_s5_c_im, m_s5_d, m_s5_w_glu, m_mla_w_in, m_mla_q_lora_norm, m_mla_kv_lora_norm, m_mla_w_uq, m_mla_w_ukv, m_mla_q_nope_norm, m_mla_k_nope_norm, m_mla_q_rope_norm, m_mla_k_rope_norm, v_ln_gain, v_w_out, v_mem_norm, v_w_mem_kv, v_xq_norm, v_xk_norm, v_s5_w_in, v_s5_lambda_re, v_s5_lambda_im, v_s5_log_step, v_s5_b_re, v_s5_b_im, v_s5_c_re, v_s5_c_im, v_s5_d, v_s5_w_glu, v_mla_w_in, v_mla_q_lora_norm, v_mla_kv_lora_norm, v_mla_w_uq, v_mla_w_ukv, v_mla_q_nope_norm, v_mla_k_nope_norm, v_mla_q_rope_norm, v_mla_k_rope_norm):
    args = dict(locals())
    wts = {n: args[n] for n in WEIGHT_ORDER}
    mom = {n: args["m_" + n] for n in WEIGHT_ORDER}
    var = {n: args["v_" + n] for n in WEIGHT_ORDER}

    chip = 2 * lax.axis_index("x") + lax.axis_index("y")
    place = jnp.stack([chip, lax.axis_index("c")]).astype(jnp.int32)

    def own_slot(gathered, shards):
        return [lax.dynamic_update_slice(g, s[None], (chip, 0, 0)) for g, s in zip(gathered, shards)]

    groups = list(stack_shards(wts, BF16))
    groups[2].append(jnp.concatenate([mla_q_lora_norm, jnp.pad(mla_kv_lora_norm, ((0, 0), (0, 64))), jnp.zeros((14, 128), F32)], axis=0))
    over_ici = [plan_gather_ici(shards) for shards in groups]
    _SCHEDULE_BEHIND.clear()
    schedule_behind(split_start("gather_start", over_ici))
    env, hooks, passed_on = {}, {}, {}

    def arrived(k, after, pass_now):
        passing = plan_gather_pass(split_wait(f"gather_wait_{k}", over_ici[k], after))
        passed_on[k] = passing
        return own_slot(run_plan(f"gather_pass_{k}", passing), groups[k]) if pass_now else None

    def need_in0(env, last):
        env["in0"], = arrived(0, last, True)

    def need_layer0(env, last):
        env["pair0"], env["glu"] = arrived(1, last, True)

    def need_layer1(env, last):
        arrived(2, last, False)

    def layer1_weights(env, last):
        pair1, ukv, in1, uq, norms = own_slot(passed_on[2].results, groups[2])
        env.update(pair1=pair1, ukv=ukv, w_in1=mla_in_permute(shards_to_cols(in1)), w_uq=uq_permute(shards_to_cols(uq)),
                   q_lora_norm=norms[:, 0, :], kv_lora_norm=norms[:, 1, :64])

    hooks["before", "l0_in"], hooks["before", "l0_glu"], hooks["before", "l0_out"] = need_in0, need_layer0, need_layer1
    hooks["plans", "l0_out"], hooks["after", "l0_out"] = (lambda env: [passed_on[2]]), layer1_weights

    rs = {}

    def swap(k, gs):
        rs[k, "g"], rs[k, "swap"] = gs, plan_pair_exchange(gs)
        return rs[k, "swap"]

    def start_scatter(k):
        rs[k, "pairs"] = pair_adds(f"rs{k}", rs[k, "g"], rs[k, "swap"].results, place)
        rs[k, "scatter"] = plan_chip_scatter(rs[k, "pairs"])
        schedule_behind(split_start(f"rs{k}_scatter_start", [rs[k, "scatter"]]))

    def join(k, after):
        rs[k, "join"] = plan_pair_join(chip_adds(f"rs{k}", rs[k, "pairs"], split_wait(f"rs{k}_scatter_wait", rs[k, "scatter"], after), place))
        return rs[k, "join"]

    hooks["plans", "l1_in_dx"] = lambda env: [swap(1, [env["g_pair1"], env["g_ukv"], cols_to_shards(mla_in_unpermute(env["dw_in1"])).astype(BF16),
                                                        cols_to_shards(uq_unpermute(env["dw_uq"])).astype(BF16)])]
    hooks["after", "l1_in_dx"] = lambda env, last: start_scatter(1)
    hooks["plans", "l0_glu_dx"] = lambda env: [swap(0, [env["g_pair0"], env["g_glu"]])]

    def before_s5_backward(env, last):
        start_scatter(0)
        env["join1"] = join(1, last)

    hooks["before", "s5_backward"] = before_s5_backward
    hooks["plans", "s5_backward"] = lambda env: [env["join1"]]
    hooks["after", "s5_backward"] = lambda env, last: env.update(join0=join(0, last))
    hooks["plans", "l0_in_dx"] = lambda env: [swap(2, [env["g_in0"]]), env["join0"]]
    hooks["after", "l0_in_dx"] = lambda env, last: start_scatter(2)

    def total_loss(env, local):
        env["loss"] = lax.psum(local[0, 0], MESH_AXES)
        schedule_behind(env["loss"].reshape(1, 1))

    hooks["after", "loss"] = total_loss
    small = {n: wts[n] for n, _ in SMALL}
    loss, grad_x, g = device_step(x[0], mem[0], positions[0], loss_target[0], small, env, hooks)
    loss = env["loss"]
    r_in0, = run_plan("rs2_pair_join", join(2, g["s5_log_step"]))
    (r_pair1, r_ukv, r_in1, r_uq), (r_pair0, r_glu) = (rs[k, "join"].results for k in (1, 0))

    small_flat = jnp.concatenate([g[n].reshape(-1) for n, _ in SMALL_FULL])
    g_small = jnp.pad(small_flat, (0, 4 * SMALL_ROWS * SMALL_LANES - N_SMALL)).astype(BF16).reshape(4, SMALL_ROWS, SMALL_LANES)
    r_small = reduce_scatter_chips("rs3", [g_small], place)[0]
    small_all = own_slot(all_gather_chips("gather_small_grads", [r_small]), [r_small])[0].reshape(-1)[:N_SMALL]

    grads = {"w_out": jnp.stack([r_pair0[:PAIR_MKV], r_pair1[:PAIR_MKV]]), "w_mem_kv": jnp.stack([r_pair0[PAIR_MKV:], r_pair1[PAIR_MKV:]]),
             "s5_w_in": r_in0[None], "s5_w_glu": r_glu[None], "mla_w_ukv": r_ukv[None], "mla_w_in": r_in1[None], "mla_w_uq": r_uq[None]}
    off = 0
    for n, s in SMALL_FULL:
        grads[n] = small_all[off:off + math.prod(s)].reshape(s)
        off += math.prod(s)
    for n, s in SHARDED_SMALL:
        grads[n] = lax.dynamic_slice(grads[n], (0, chip * s[1]), s)

    delta, new_m, new_v = {}, {}, {}
    for n, s in BIG + [(n, s) for n, s in SMALL if len(s) == 4]:
        perm = MINOR_LAST.get(n, tuple(range(len(s))))
        turned = tuple(s[p] for p in perm)
        view = lambda a: jnp.transpose(a, perm).reshape(-1, turned[-1])
        res = adamw("adamw_" + n, view(wts[n]), view(grads[n]), view(mom[n]), view(var[n]))
        delta[n], new_m[n], new_v[n] = (jnp.transpose(r.reshape(turned), tuple(perm.index(i) for i in range(len(s)))) for r in res)
    small_names = [n for n, s in SMALL if len(s) < 4] + [n for n, _ in SHARDED_SMALL]
    n_own = sum(wts[n].size for n in small_names)
    rows_own = -(-n_own // (8 * 128)) * 8

    def pack_small(d):
        flat = jnp.concatenate([d[n].reshape(-1) for n in small_names])
        return jnp.pad(flat, (0, rows_own * 128 - n_own), constant_values=1.0).reshape(rows_own, 128)

    res = adamw("adamw_small", pack_small(wts), pack_small(grads), pack_small(mom), pack_small(var))
    off = 0
    for n in small_names:
        size = wts[n].size
        delta[n], new_m[n], new_v[n] = (r.reshape(-1)[off:off + size].reshape(wts[n].shape) for r in res)
        off += size

    return (loss, grad_x[None], *[grads[n] for n in WEIGHT_ORDER], *[delta[n] for n in WEIGHT_ORDER],
            *[new_m[n] for n in WEIGHT_ORDER], *[new_v[n] for n in WEIGHT_ORDER])
```

```python
import functools
import math

import jax
import jax.numpy as jnp
from jax import lax
from jax.experimental import pallas as pl
from jax.experimental.pallas import tpu as pltpu

F32, BF16 = jnp.float32, jnp.bfloat16
SDS = jax.ShapeDtypeStruct

D = 1024
L = 2048
ML = 256
BW = 2 * D
XQW = BW // 4
PW = BW - XQW
XH, XHD = 4, 128
SG, SC, SP = 96, 16, 64
SN = SG * SP
NOPE, ROPE, VD = 128, 64, 128
MH = 12
QL, KVL = 512, 256
EPS = 1e-6
ROPE_THETA = 10000.0
MLA_IN = QL + KVL + ROPE + XQW + BW
MLA_IN_P = 3456
ADAM_LR, ADAM_B1, ADAM_B2, ADAM_EPS, ADAM_WD, ADAM_STEP = 0.001, 0.9, 0.999, 1e-08, 0.01, 10

VMEM_LIMIT = 48 * 2**20
ROW_TILE = 512
BMAT_TILE, CMAT_TILE = 2048, 512
SEG = 8
SEG_LEN = L // SEG
MESH_AXES = ("x", "y", "c")


def _cparams():
    return pltpu.CompilerParams(vmem_limit_bytes=VMEM_LIMIT)


def _dg(a, b, ca, cb):
    return lax.dot_general(a.astype(BF16), b.astype(BF16), (((ca,), (cb,)), ((), ())), preferred_element_type=F32)


@jax.custom_vjp
def mm_nn(a, b):
    return _dg(a, b, 1, 0)


mm_nn.defvjp(lambda a, b: (_dg(a, b, 1, 0), (a, b)), lambda res, g: (_dg(g, res[1], 1, 1), _dg(res[0], g, 0, 0)))


@jax.custom_vjp
def mm_nt(a, b):
    return _dg(a, b, 1, 1)


mm_nt.defvjp(lambda a, b: (_dg(a, b, 1, 1), (a, b)), lambda res, g: (_dg(g, res[1], 1, 0), _dg(g, res[0], 0, 0)))


@functools.partial(jax.custom_vjp, nondiff_argnums=(1,))
def lane_roll(x, shift):
    return pltpu.roll(x, shift, 1)


lane_roll.defvjp(lambda x, shift: (pltpu.roll(x, shift, 1), None),
                 lambda shift, _, g: (pltpu.roll(g, (128 - shift) % 128, 1),))


def rms(x, g):
    return x * lax.rsqrt(jnp.mean(x * x, axis=-1, keepdims=True) + EPS) * g


@jax.custom_vjp
def softmax_rows(s):
    e = jnp.exp(s - jnp.max(s, axis=-1, keepdims=True))
    return e / jnp.sum(e, axis=-1, keepdims=True)


def _softmax_rows_fwd(s):
    p = softmax_rows(s)
    return p, p


def _softmax_rows_bwd(p, g):
    return (p * (g - jnp.sum(g * p, axis=-1, keepdims=True)),)


softmax_rows.defvjp(_softmax_rows_fwd, _softmax_rows_bwd)


def silu(x):
    return x * jax.nn.sigmoid(x)


ANY = pl.BlockSpec(memory_space=pl.ANY)
MESH_ID = pl.DeviceIdType.MESH


def _dma_sems(n):
    return [pltpu.SemaphoreType.DMA((n,)), pltpu.SemaphoreType.DMA((n,))]


class Plan:
    def __init__(self, operands, out_shape, aliases, n_sems, copies):
        self.operands, self.out_shape, self.aliases, self.n_sems, self.copies = list(operands), list(out_shape), aliases, n_sems, copies
        self.results = None


_SCHEDULE_BEHIND = []


def schedule_behind(token):
    _SCHEDULE_BEHIND.append(token)


def hosted_call(kern, *, name, grid, in_specs, out_specs, out_shape, operands, scratch_shapes=(), aliases=None, cparams=None, plans=(), deps=()):
    n_in, n_out, n_scr = len(in_specs), len(out_specs), len(scratch_shapes)
    p_in, p_out = [len(p.operands) for p in plans], [len(p.out_shape) for p in plans]
    deps = tuple(deps) + tuple(_SCHEDULE_BEHIND)
    _SCHEDULE_BEHIND.clear()
    all_aliases = dict(aliases or {})
    in_off, out_off = n_in, n_out
    for p, ni, no in zip(plans, p_in, p_out):
        all_aliases.update({in_off + i: out_off + o for i, o in p.aliases.items()})
        in_off, out_off = in_off + ni, out_off + no

    def body(*refs):
        pos, pins, pouts = n_in, [], []
        for ni in p_in:
            pins.append(refs[pos:pos + ni])
            pos += ni
        pos += len(deps)
        main_out = refs[pos:pos + n_out]
        pos += n_out
        for no in p_out:
            pouts.append(refs[pos:pos + no])
            pos += no
        main_scr = refs[pos:pos + n_scr]
        pos += n_scr
        if plans:
            ids = [pl.program_id(ax) for ax in range(len(grid))]
            first = functools.reduce(jnp.logical_and, [i == 0 for i in ids])
            last = functools.reduce(jnp.logical_and, [i == g - 1 for i, g in zip(ids, grid)])
            copies = [p.copies(pins[k], pouts[k], refs[pos + 2 * k], refs[pos + 2 * k + 1]) for k, p in enumerate(plans)]

            @pl.when(first)
            def _():
                for sends, _ in copies:
                    for cp in sends:
                        cp.start()

        kern(*refs[:n_in], *main_out, *main_scr)
        if plans:
            @pl.when(last)
            def _():
                for sends, recvs in copies:
                    for cp in recvs:
                        cp.wait_recv()
                    for cp in sends:
                        cp.wait_send()

    res = pl.pallas_call(body, grid=grid, in_specs=list(in_specs) + [ANY] * (sum(p_in) + len(deps)),
                         out_specs=list(out_specs) + [ANY] * sum(p_out), out_shape=list(out_shape) + [s for p in plans for s in p.out_shape],
                         scratch_shapes=list(scratch_shapes) + [s for p in plans for s in _dma_sems(p.n_sems)],
                         input_output_aliases=all_aliases, name=name, compiler_params=cparams or _cparams())(
        *operands, *[a for p in plans for a in p.operands], *deps)
    pos = n_out
    for p, no in zip(plans, p_out):
        p.results = list(res[pos:pos + no])
        pos += no
    return list(res[:n_out])


def _wide(v):
    return v.astype(F32) if v.dtype == BF16 else v


def stage(name, fn, grid, ins, outs):
    n_in = len(ins)

    def kern(*refs):
        res = fn(*[_wide(r[...]) for r in refs[:n_in]])
        for r, v in zip(refs[n_in:], res):
            r[...] = v.astype(r.dtype)

    return hosted_call(kern, name=name, grid=grid, in_specs=[s for _, s in ins], out_specs=[s for _, s in outs],
                       out_shape=[sd for sd, _ in outs], operands=[a for a, _ in ins])


def stage_bwd(name, fn, grid, ins, cts, diffs, plans=()):
    n_in, n_ct = len(ins), len(cts)
    didx = [i for i, d in enumerate(diffs) if d is not None]
    opts = {i: (diffs[i][3] if len(diffs[i]) > 3 else {}) for i in didx if diffs[i][0] == "row"}
    adds = [(i, opts[i]["add"]) for i in opts if "add" in opts[i]]
    intos = [(i, opts[i]["into"]) for i in opts if "into" in opts[i]]
    n_add, n_into = len(adds), len(intos)
    add_pos = {i: n_in + n_ct + k for k, (i, _) in enumerate(adds)}
    n_extra = n_in + n_ct + n_add + n_into

    def kern(*refs):
        vals = [_wide(r[...]) for r in refs[:n_in]]

        def f(*dv):
            full = list(vals)
            for i, v in zip(didx, dv):
                full[i] = v
            return fn(*full)

        _, vjp = jax.vjp(f, *[vals[i].astype(F32) for i in didx])
        gs = vjp(tuple(c[...].astype(F32) for c in refs[n_in:n_in + n_ct]))
        for o_ref, i, g in zip(refs[n_extra:], didx, gs):
            if diffs[i][0] == "row":
                if i in add_pos:
                    g = g + refs[add_pos[i]][...].astype(F32)
                o_ref[...] = g.astype(o_ref.dtype)
            else:
                first = functools.reduce(jnp.logical_and, [pl.program_id(ax) == 0 for ax in diffs[i][1]])

                @pl.when(first)
                def _():
                    o_ref[...] = g

                @pl.when(jnp.logical_not(first))
                def _():
                    o_ref[...] += g

    out_shape, out_specs = [], []
    for i in didx:
        if diffs[i][0] == "row":
            out_shape.append(diffs[i][1])
            out_specs.append(diffs[i][2])
        else:
            out_shape.append(SDS(ins[i][0].shape, F32))
            out_specs.append(ins[i][1])
    aliases = {n_in + n_ct + n_add + k: didx.index(i) for k, (i, _) in enumerate(intos)}
    in_specs = [s for _, s in ins] + [s for _, s in cts] + [s for _, (_, s) in adds] + [ANY] * n_into
    operands = [a for a, _ in ins] + [a for a, _ in cts] + [a for _, (a, _) in adds] + [a for _, a in intos]
    return hosted_call(kern, name=name, grid=grid, in_specs=in_specs, out_specs=out_specs, out_shape=out_shape, operands=operands,
                       aliases=aliases, plans=plans)


def rspec(tl, w, cb=0):
    return pl.BlockSpec((tl, w), lambda i: (i, cb))


def cspec(shape):
    return pl.BlockSpec(shape, lambda i: (0,) * len(shape))


def row_fwd(name, fn, rows, tl, row_ins, consts, outs):
    ins = [(a, rspec(tl, w, cb)) for a, w, cb in row_ins] + [(a, cspec(a.shape)) for a in consts]
    return stage(name, fn, (rows // tl,), ins, [(SDS((rows, w), dt), rspec(tl, w)) for w, dt in outs])


def row_bwd(name, fn, rows, tl, row_ins, consts, cts, row_diff, const_diff, plans=()):
    ins = [(a, rspec(tl, w, cb)) for a, w, cb in row_ins] + [(a, cspec(a.shape)) for a in consts]
    diffs = []
    for (a, w, cb), d in zip(row_ins, row_diff):
        if not d:
            diffs.append(None)
            continue
        d = d if isinstance(d, dict) else {}
        opts = {}
        if "add" in d:
            opts["add"] = (d["add"][0], rspec(tl, d["add"][1], d["add"][2]))
        if d.get("into") is not None:
            opts["into"] = d["into"]
        diffs.append(("row", SDS((rows, d.get("cols", w)), d.get("dtype", F32)), rspec(tl, w, d.get("cb", 0)), opts))
    diffs += [("acc", (0,)) if d else None for d in const_diff]
    return stage_bwd(name, fn, (rows // tl,), ins, [(a, rspec(tl, w, cb)) for a, w, cb in cts], diffs, plans=plans)


MATMUL_VMEM = 36 * 2**20
ADAMW_VMEM = 28 * 2**20


class Sharded:
    def __init__(self, arr, kind, roff, rows):
        self.arr, self.kind, self.roff, self.rows, self.n = arr, kind, roff, rows, arr.shape[2]
        self.shape = (rows, 4 * self.n) if kind == "col" else (4 * rows, self.n)

    def fits(self, t0, t1):
        return self.roff % t0 == 0 and self.rows % t0 == 0 and self.n % t1 == 0

    def spec(self, t0, t1, bidx):
        assert self.fits(t0, t1), (self.kind, self.roff, self.rows, self.n, t0, t1)
        r0 = self.roff // t0
        if self.kind == "col":
            per = self.n // t1
            return pl.BlockSpec((None, t0, t1), lambda *g: (bidx(*g)[1] // per, r0 + bidx(*g)[0], bidx(*g)[1] % per))
        per = self.rows // t0
        return pl.BlockSpec((None, t0, t1), lambda *g: (bidx(*g)[0] // per, r0 + bidx(*g)[0] % per, bidx(*g)[1]))


def matmul(name, a, b, mode, out_dtype=BF16, add=None, out=None, into=None, plans=()):
    if mode == "tn":
        k_dim, m = a.shape
    else:
        m, k_dim = a.shape
    n = b.shape[0] if mode == "nt" else b.shape[1]
    b_fit = b.fits if isinstance(b, Sharded) else (lambda t0, t1: True)
    o_fit = out.fits if out is not None else (lambda t0, t1: True)
    a_bytes, b_bytes = jnp.dtype(a.dtype).itemsize, jnp.dtype(b.arr.dtype if isinstance(b, Sharded) else b.dtype).itemsize
    o_bytes = jnp.dtype(out_dtype if out is None else out.arr.dtype).itemsize

    def vmem(tm, tn, tk):
        return 2 * (tm * tk * a_bytes + tk * tn * b_bytes + tm * tn * (o_bytes + (4 if add is not None else 0))) + 4 * tm * tn * (1 + (tk < k_dim))

    tiles = [(tm, tn, tk) for tm in (2048, 1024, 512, 256, 128) for tn in (1024, 768, 512, 384, 256, 128)
             for tk in sorted({k_dim, 1024, 768, 512, 384, 256, 128})
             if m % tm == 0 and n % tn == 0 and k_dim % tk == 0 and (b_fit(tn, tk) if mode == "nt" else b_fit(tk, tn)) and o_fit(tm, tn)
             and vmem(tm, tn, tk) <= MATMUL_VMEM]
    tm, tn, tk = max(tiles, key=lambda t: (t[2] == k_dim, t[0] * t[1] * t[2], t[0] * t[1]))
    nk = k_dim // tk
    a_spec = pl.BlockSpec((tk, tm), lambda i, j, k: (k, i)) if mode == "tn" else pl.BlockSpec((tm, tk), lambda i, j, k: (i, k))
    if isinstance(b, Sharded):
        b_spec = b.spec(tn, tk, lambda i, j, k: (j, k)) if mode == "nt" else b.spec(tk, tn, lambda i, j, k: (k, j))
        b = b.arr
    else:
        b_spec = pl.BlockSpec((tn, tk), lambda i, j, k: (j, k)) if mode == "nt" else pl.BlockSpec((tk, tn), lambda i, j, k: (k, j))
    o_spec = pl.BlockSpec((tm, tn), lambda i, j, k: (i, j))
    out_spec, out_shape = (o_spec, SDS((m, n), out_dtype)) if out is None else (out.spec(tm, tn, lambda i, j, k: (i, j)), out.arr)
    ca, cb = {"nn": (1, 0), "nt": (1, 1), "tn": (0, 0)}[mode]
    n_in = 2 + (add is not None)

    def finish(refs, o_ref, r):
        if add is not None:
            r = r + refs[2][...]
        o_ref[...] = r.astype(o_ref.dtype)

    def kern_whole(*refs):
        finish(refs, refs[-1], _dg(refs[0][...], refs[1][...], ca, cb))

    def kern_cut(*refs):
        o_ref, acc = refs[-2], refs[-1]
        k = pl.program_id(2)

        @pl.when(k == 0)
        def _():
            acc[...] = jnp.zeros_like(acc)

        acc[...] += _dg(refs[0][...], refs[1][...], ca, cb)

        @pl.when(k == nk - 1)
        def _():
            finish(refs, o_ref, acc[...])

    ins, specs = [a, b], [a_spec, b_spec]
    if add is not None:
        ins.append(add)
        specs.append(o_spec)
    if into is not None:
        ins.append(into)
        specs.append(ANY)
    return hosted_call(kern_whole if nk == 1 else kern_cut, name=name, grid=(m // tm, n // tn, nk), in_specs=specs, out_specs=[out_spec],
                       out_shape=[out_shape], operands=ins, scratch_shapes=[] if nk == 1 else [pltpu.VMEM((tm, tn), F32)],
                       aliases={} if into is None else {n_in: 0}, plans=plans)[0]


SCAN_UNROLL = 8


def _cmul(ar, ai, br, bi):
    return ar * br - ai * bi, ar * bi + ai * br


def _sub_shift(x, down):
    row = lax.broadcasted_iota(jnp.int32, x.shape, 0)
    if down:
        return jnp.where(row == 0, 0.0, pltpu.roll(x, 1, 0))
    return jnp.where(row == SEG - 1, 0.0, pltpu.roll(x, SEG - 1, 0))


def _pow_seg_len(ar, ai):
    for _ in range(int(math.log2(SEG_LEN))):
        ar, ai = _cmul(ar, ai, ar, ai)
    return ar, ai


def _scan_in_place(sr, si, a_re, a_im):
    lanes = sr.shape[1]
    ar = jnp.broadcast_to(a_re, (SEG, lanes))
    ai = jnp.broadcast_to(a_im, (SEG, lanes))
    zero = jnp.zeros((SEG, lanes), F32)

    def local(i, carry):
        rows = pl.ds(pl.multiple_of(i * SEG, SEG), SEG)
        mr, mi = _cmul(ar, ai, carry[0], carry[1])
        nr, ni = mr + sr[rows, :], mi + si[rows, :]
        sr[rows, :] = nr
        si[rows, :] = ni
        return nr, ni

    fr, fi = lax.fori_loop(0, SEG_LEN, local, (zero, zero), unroll=SCAN_UNROLL)
    pr, pi = _pow_seg_len(ar, ai)
    ir, ii = zero, zero
    for _ in range(SEG - 1):
        mr, mi = _cmul(pr, pi, ir, ii)
        ir, ii = _sub_shift(mr + fr, True), _sub_shift(mi + fi, True)

    def carry_in(i, pw):
        rows = pl.ds(pl.multiple_of(i * SEG, SEG), SEG)
        cr, ci = _cmul(pw[0], pw[1], ir, ii)
        sr[rows, :] += cr
        si[rows, :] += ci
        return _cmul(pw[0], pw[1], ar, ai)

    lax.fori_loop(0, SEG_LEN, carry_in, (ar, ai), unroll=SCAN_UNROLL)


S5_LANES = 8 * SP
S5_BLOCKS = SN // S5_LANES


def _s5_specs(n=1):
    u_spec = pl.BlockSpec((L, n * 8 * SC), lambda j: (0, j))
    s_spec = pl.BlockSpec((L, n * S5_LANES), lambda j: (0, j))
    wb_spec = pl.BlockSpec((n * S5_LANES, 8 * SC), lambda j: (j, 0))
    wc_spec = pl.BlockSpec((n * 8 * SC, S5_LANES), lambda j: (j, 0))
    a_spec = pl.BlockSpec((1, n * S5_LANES), lambda j: (0, j))
    d_spec = pl.BlockSpec((1, n * 8 * SC), lambda j: (0, j))
    return u_spec, s_spec, wb_spec, wc_spec, a_spec, d_spec


S5_FWD_BLOCKS = 2


def s5_forward(proj, wb_re, wb_im, wc_re, wc_im, a_re, a_im, d, plans=()):
    n = S5_FWD_BLOCKS
    cols, lanes = 8 * SC, S5_LANES

    def kern(u_ref, wbr, wbi, wcr, wci, ar, ai, d_ref, sr_out, si_out, g_ref, sr, si):
        u = _wide(u_ref[...])
        for b in range(n):
            sr[:, b * lanes:(b + 1) * lanes], si[:, b * lanes:(b + 1) * lanes] = fn_s5_bu(
                u[:, b * cols:(b + 1) * cols], wbr[b * lanes:(b + 1) * lanes, :], wbi[b * lanes:(b + 1) * lanes, :])
        _scan_in_place(sr, si, ar[...], ai[...])
        for b in range(n):
            g_ref[:, b * cols:(b + 1) * cols] = fn_s5_out(
                sr[:, b * lanes:(b + 1) * lanes], si[:, b * lanes:(b + 1) * lanes], u[:, b * cols:(b + 1) * cols],
                d_ref[:, b * cols:(b + 1) * cols], wcr[b * cols:(b + 1) * cols, :], wci[b * cols:(b + 1) * cols, :])[0].astype(g_ref.dtype)
        sr_out[...] = sr[...].astype(sr_out.dtype)
        si_out[...] = si[...].astype(si_out.dtype)

    u_spec, s_spec, wb_spec, wc_spec, a_spec, d_spec = _s5_specs(n)
    return hosted_call(kern, name="s5_forward", grid=(S5_BLOCKS // n,), in_specs=[u_spec, wb_spec, wb_spec, wc_spec, wc_spec, a_spec, a_spec, d_spec],
                       out_specs=[s_spec, s_spec, u_spec], out_shape=[SDS((L, SN), BF16)] * 2 + [SDS((L, PW), BF16)],
                       operands=[proj, wb_re, wb_im, wc_re, wc_im, a_re, a_im, d], scratch_shapes=[pltpu.VMEM((L, n * lanes), F32)] * 2,
                       plans=plans)


def _adjoint_scan_in_place(lr, li, sr, si, a_re, a_im):
    lanes = lr.shape[1]
    ar = jnp.broadcast_to(a_re, (SEG, lanes))
    ai = -jnp.broadcast_to(a_im, (SEG, lanes))
    zero = jnp.zeros((SEG, lanes), F32)

    def local(k, carry):
        i = SEG_LEN - 1 - k
        rows = pl.ds(pl.multiple_of(i * SEG, SEG), SEG)
        mr, mi = _cmul(ar, ai, carry[0], carry[1])
        nr, ni = mr + lr[rows, :], mi + li[rows, :]
        lr[rows, :] = nr
        li[rows, :] = ni
        return nr, ni

    fr, fi = lax.fori_loop(0, SEG_LEN, local, (zero, zero), unroll=SCAN_UNROLL)
    pr, pi = _pow_seg_len(ar, ai)
    ir, ii = zero, zero
    for _ in range(SEG - 1):
        mr, mi = _cmul(pr, pi, ir, ii)
        ir, ii = _sub_shift(mr + fr, False), _sub_shift(mi + fi, False)

    def fix(rows, pw):
        cr, ci = _cmul(pw[0], pw[1], ir, ii)
        tr, ti = lr[rows, :] + cr, li[rows, :] + ci
        lr[rows, :] = tr
        li[rows, :] = ti
        return tr, ti

    def grad_a(tr, ti, spr, spi, acc):
        return acc[0] + tr * spr + ti * spi, acc[1] + ti * spr - tr * spi

    def carry_in(k, c):
        i = SEG_LEN - 1 - k
        rows = pl.ds(pl.multiple_of(i * SEG, SEG), SEG)
        prev = pl.ds(pl.multiple_of((i - 1) * SEG, SEG), SEG)
        tr, ti = fix(rows, (c[0], c[1]))
        acc = grad_a(tr, ti, sr[prev, :], si[prev, :], (c[2], c[3]))
        nr, ni = _cmul(c[0], c[1], ar, ai)
        return nr, ni, acc[0], acc[1]

    pwr, pwi, accr, acci = lax.fori_loop(0, SEG_LEN - 1, carry_in, (ar, ai, zero, zero), unroll=5)
    tr, ti = fix(pl.ds(0, SEG), (pwr, pwi))
    last = pl.ds((SEG_LEN - 1) * SEG, SEG)
    accr, acci = grad_a(tr, ti, _sub_shift(sr[last, :], True), _sub_shift(si[last, :], True), (accr, acci))
    return jnp.sum(accr, axis=0, keepdims=True), jnp.sum(acci, axis=0, keepdims=True)


S5_BWD_VMEM = 58 * 2**20


def s5_backward(dg, proj, s_re, s_im, wb_re, wb_im, wc_re, wc_im, a_re, a_im, d, dproj, plans=()):
    def kern(dg_ref, u_ref, sr_in, si_in, wbr, wbi, wcr, wci, ar, ai, d_ref, _, du_ref, dd_ref, dwcr, dwci, dwbr, dwbi, dar, dai, lr, li, sr, si):
        u = _wide(u_ref[...])
        sr[...], si[...] = _wide(sr_in[...]), _wide(si_in[...])
        _, vjp_out = jax.vjp(fn_s5_out, sr[...], si[...], u, d_ref[...], wcr[...], wci[...])
        lr[...], li[...], du_out, dd_ref[...], dwcr[...], dwci[...] = vjp_out((_wide(dg_ref[...]),))
        dar[...], dai[...] = _adjoint_scan_in_place(lr, li, sr, si, ar[...], ai[...])
        _, vjp_in = jax.vjp(fn_s5_bu, u, wbr[...], wbi[...])
        du_in, dwbr[...], dwbi[...] = vjp_in((lr[...], li[...]))
        du_ref[...] = (du_out + du_in).astype(du_ref.dtype)

    u_spec, s_spec, wb_spec, wc_spec, a_spec, d_spec = _s5_specs()
    outs = [(SDS(dproj.shape, dproj.dtype), u_spec), (SDS(d.shape, F32), d_spec), (SDS(wc_re.shape, F32), wc_spec), (SDS(wc_im.shape, F32), wc_spec),
            (SDS(wb_re.shape, F32), wb_spec), (SDS(wb_im.shape, F32), wb_spec), (SDS(a_re.shape, F32), a_spec), (SDS(a_im.shape, F32), a_spec)]
    return hosted_call(kern, name="s5_backward", grid=(S5_BLOCKS,),
                       in_specs=[u_spec, u_spec, s_spec, s_spec, wb_spec, wb_spec, wc_spec, wc_spec, a_spec, a_spec, d_spec, ANY],
                       out_specs=[sp for _, sp in outs], out_shape=[sd for sd, _ in outs], aliases={11: 0},
                       operands=[dg, proj, s_re, s_im, wb_re, wb_im, wc_re, wc_im, a_re, a_im, d, dproj],
                       scratch_shapes=[pltpu.VMEM((L, S5_LANES), F32)] * 4,
                       cparams=pltpu.CompilerParams(vmem_limit_bytes=S5_BWD_VMEM), plans=plans)


def fn_rms(x, g):
    return (rms(x, g),)


def fn_s5_disc(lre, lim, ls):
    step = jnp.exp(ls)
    e = jnp.exp(lre * step)
    a_re, a_im = e * jnp.cos(lim * step), e * jnp.sin(lim * step)
    den = lre * lre + lim * lim
    nr, ni = a_re - 1.0, a_im
    return a_re, a_im, (nr * lre + ni * lim) / den, (ni * lre - nr * lim) / den


def _group_mask(rows, cols, row_div, col_div):
    r = lax.broadcasted_iota(jnp.int32, (rows, cols), 0) // row_div % 8
    c = lax.broadcasted_iota(jnp.int32, (rows, cols), 1) // col_div
    return r == c


def _spread(x, mask):
    w = x.shape[1]
    copy = (lax.broadcasted_iota(jnp.int32, (w, 8 * w), 1) % w == lax.broadcasted_iota(jnp.int32, (w, 8 * w), 0)).astype(F32)
    return jnp.where(mask, jnp.dot(x, copy, precision=lax.Precision.HIGHEST, preferred_element_type=F32), 0.0)


def fn_s5_bmat(b_re, b_im, coef_re, coef_im):
    mask = _group_mask(b_re.shape[0], 8 * SC, SP, SC)
    return _spread(coef_re * b_re - coef_im * b_im, mask), _spread(coef_re * b_im + coef_im * b_re, mask)


def fn_s5_cmat(c_re, c_im):
    mask = _group_mask(c_re.shape[0], 8 * SP, SC, SP)
    return _spread(c_re, mask), _spread(c_im, mask)


def fn_s5_bu(u, wb_re, wb_im):
    return mm_nt(u, wb_re), mm_nt(u, wb_im)


def fn_s5_out(sr, si, u, d, wc_re, wc_im):
    y = mm_nt(sr, wc_re) - mm_nt(si, wc_im) + d * u
    return (jax.nn.gelu(y),)


def fn_merge_glu(z, mo, gate):
    yg = z[:, :PW] * jax.nn.sigmoid(z[:, PW:])
    return (jnp.concatenate([yg, mo], axis=1) * silu(gate),)


def fn_merge(prim, mo, gate):
    return (jnp.concatenate([prim, mo], axis=1) * silu(gate),)


def fn_mem_k(kv, g):
    return (jnp.concatenate([rms(kv[:, h * XHD:(h + 1) * XHD], g) for h in range(XH)], axis=1),)


def fn_mem_attn(xq, kn, v, g):
    outs = []
    for h in range(XH):
        sl = slice(h * XHD, (h + 1) * XHD)
        p = softmax_rows(mm_nt(rms(xq[:, sl], g), kn[:, sl]) * (XHD ** -0.5))
        outs.append(mm_nn(p, v[:, sl]))
    return (jnp.concatenate(outs, axis=1),)


def _half_rms(x, g):
    lo = lax.broadcasted_iota(jnp.int32, x.shape, 1) < ROPE
    x2 = x * x
    s_lo = jnp.sum(jnp.where(lo, x2, 0.0), axis=1, keepdims=True)
    s_hi = jnp.sum(jnp.where(lo, 0.0, x2), axis=1, keepdims=True)
    return x * lax.rsqrt(jnp.where(lo, s_lo, s_hi) / ROPE + EPS) * g


def _rope(x, cos2, sin_signed):
    first = lax.broadcasted_iota(jnp.int32, x.shape, 1) % ROPE < ROPE // 2
    return x * cos2 + jnp.where(first, lane_roll(x, 128 - ROPE // 2), lane_roll(x, ROPE // 2)) * sin_signed


def fn_mla_prep(q, kv, kr, cos2, sin_signed, qnn, knn, qrn, krn):
    lo = lax.broadcasted_iota(jnp.int32, kr.shape, 1) < ROPE
    kr_pad = jnp.where(lo, _rope(_half_rms(kr, krn), cos2, sin_signed), 0.0)
    qf, kf, vs = [], [], []
    for m in range(MH // 2):
        pair = _rope(_half_rms(q[:, MH * NOPE + 128 * m:MH * NOPE + 128 * (m + 1)], qrn), cos2, sin_signed)
        for h, rope_h in ((2 * m, pair), (2 * m + 1, lane_roll(pair, ROPE))):
            qf.append(jnp.concatenate([rms(q[:, NOPE * h:NOPE * (h + 1)], qnn), jnp.where(lo, rope_h, 0.0)], axis=1))
    for h in range(MH):
        kf.append(jnp.concatenate([rms(kv[:, 256 * h:256 * h + NOPE], knn), kr_pad], axis=1))
        vs.append(kv[:, 256 * h + NOPE:256 * (h + 1)])
    return jnp.stack(qf), jnp.stack(kf), jnp.stack(vs)


ATT_TQ = 512


def _attn_scores(q, kf):
    tq = q.shape[0]
    scale = (NOPE + ROPE) ** -0.5
    own = _dg(q, kf[-tq:], 1, 1) * scale
    own = jnp.where(lax.broadcasted_iota(jnp.int32, own.shape, 1) <= lax.broadcasted_iota(jnp.int32, own.shape, 0), own, jnp.finfo(F32).min)
    return own if kf.shape[0] == tq else jnp.concatenate([_dg(q, kf[:-tq], 1, 1) * scale, own], axis=1)


ATT_FWD_HEADS, ATT_BWD_HEADS = 4, 2


def _attn_specs(heads):
    q_spec = pl.BlockSpec((heads, ATT_TQ, 256), lambda h, i: (h, i, 0))
    k_spec = pl.BlockSpec((heads, L, 256), lambda h, i: (h, 0, 0))
    v_spec = pl.BlockSpec((heads, L, 128), lambda h, i: (h, 0, 0))
    o_spec = pl.BlockSpec((ATT_TQ, heads * 128), lambda h, i: (i, h))
    lse_spec = pl.BlockSpec((heads, ATT_TQ, 1), lambda h, i: (h, i, 0))
    return q_spec, k_spec, v_spec, o_spec, lse_spec


def _attn_branches(heads, body):
    i = pl.program_id(1)
    for t in range(L // ATT_TQ):
        @pl.when(i == t)
        def _(t=t):
            for hh in range(heads):
                body(hh, slice(hh * 128, (hh + 1) * 128), (t + 1) * ATT_TQ)


def causal_attn(qf, kf, vh):
    def kern(q_ref, k_ref, v_ref, o_ref, lse_ref):
        def body(hh, lanes, keys):
            s = _attn_scores(q_ref[hh], k_ref[hh, :keys, :])
            m = jnp.max(s, axis=-1, keepdims=True)
            e = jnp.exp(s - m)
            total = jnp.sum(e, axis=-1, keepdims=True)
            o_ref[:, lanes] = (_dg(e, v_ref[hh, :keys, :], 1, 0) / total).astype(o_ref.dtype)
            lse_ref[hh] = m + jnp.log(total)

        _attn_branches(ATT_FWD_HEADS, body)

    q_spec, k_spec, v_spec, o_spec, lse_spec = _attn_specs(ATT_FWD_HEADS)
    return pl.pallas_call(kern, grid=(MH // ATT_FWD_HEADS, L // ATT_TQ), in_specs=[q_spec, k_spec, v_spec], out_specs=[o_spec, lse_spec],
                          out_shape=[SDS((L, MH * VD), F32), SDS((MH, L, 1), F32)], name="l1_attn", compiler_params=_cparams())(qf, kf, vh)


def causal_attn_bwd(qf, kf, vh, out, lse, dout):
    scale = (NOPE + ROPE) ** -0.5

    def kern(q_ref, k_ref, v_ref, o_ref, lse_ref, do_ref, dq_ref, dk_ref, dv_ref):
        @pl.when(pl.program_id(1) == 0)
        def _():
            dk_ref[...] = jnp.zeros_like(dk_ref)
            dv_ref[...] = jnp.zeros_like(dv_ref)

        def body(hh, lanes, keys):
            q, k, v, do = q_ref[hh], k_ref[hh, :keys, :], v_ref[hh, :keys, :], do_ref[:, lanes]
            p = jnp.exp(_attn_scores(q, k) - lse_ref[hh])
            delta = jnp.sum(do * _wide(o_ref[:, lanes]), axis=-1, keepdims=True)
            dv_ref[hh, :keys, :] += _dg(p, do, 0, 0)
            ds = p * (_dg(do, v, 1, 1) - delta) * scale
            dq_ref[hh] = _dg(ds, k, 1, 0)
            dk_ref[hh, :keys, :] += _dg(ds, q, 0, 0)

        _attn_branches(ATT_BWD_HEADS, body)

    q_spec, k_spec, v_spec, o_spec, lse_spec = _attn_specs(ATT_BWD_HEADS)
    return pl.pallas_call(kern, grid=(MH // ATT_BWD_HEADS, L // ATT_TQ), in_specs=[q_spec, k_spec, v_spec, o_spec, lse_spec, o_spec],
                          out_specs=[q_spec, k_spec, v_spec], out_shape=[SDS(qf.shape, F32), SDS(kf.shape, F32), SDS(vh.shape, F32)],
                          name="l1_attn_bwd", compiler_params=_cparams())(qf, kf, vh, out, lse, dout)


def loss_and_grad(y, target, tl=512):
    def kern(y_ref, t_ref, dy_ref, loss_ref):
        d = y_ref[...] - t_ref[...]
        dy_ref[...] = d / D

        @pl.when(pl.program_id(0) == 0)
        def _():
            loss_ref[...] = jnp.zeros_like(loss_ref)

        loss_ref[...] += 0.5 * jnp.sum(jnp.sum(d * d, axis=1, keepdims=True), axis=0, keepdims=True) / D

    return pl.pallas_call(kern, grid=(L // tl,), in_specs=[rspec(tl, D), rspec(tl, D)], out_specs=[rspec(tl, D), cspec((1, 1))],
                          out_shape=[SDS((L, D), F32), SDS((1, 1), F32)], name="loss", compiler_params=_cparams())(y, target)


def adamw(name, w, g, m, v):
    rows, cols = w.shape
    block_row_bytes = 7 * 2 * 4 * max(cols, 128)
    tr = _row_tile(rows, min(2048, ADAMW_VMEM // block_row_bytes // 8 * 8), 8)

    def kern(w_ref, g_ref, m_ref, v_ref, d_ref, nm_ref, nv_ref):
        gg = g_ref[...]
        nm = ADAM_B1 * m_ref[...] + (1.0 - ADAM_B1) * gg
        nv = ADAM_B2 * v_ref[...] + (1.0 - ADAM_B2) * jnp.square(gg)
        m_hat = nm / (1.0 - ADAM_B1 ** ADAM_STEP)
        v_hat = nv / (1.0 - ADAM_B2 ** ADAM_STEP)
        d_ref[...] = -ADAM_LR * (m_hat / (jnp.sqrt(v_hat) + ADAM_EPS) + ADAM_WD * w_ref[...])
        nm_ref[...] = nm
        nv_ref[...] = nv

    spec = rspec(tr, cols)
    return hosted_call(kern, name=name, grid=(rows // tr,), in_specs=[spec] * 4, out_specs=[spec] * 3,
                       out_shape=[SDS((rows, cols), F32)] * 3, operands=[w, g, m, v])


def _row_tile(rows, cap=512, unit=16):
    return max(t for t in range(unit, cap + 1, unit) if rows % t == 0)


def _place():
    x, y, c = lax.axis_index("x"), lax.axis_index("y"), lax.axis_index("c")
    return x, y, c, [(1 - x, y), (x, 1 - y), (1 - x, 1 - y)]


def _row_chunks(rows, n, dtype):
    unit = 32 // jnp.dtype(dtype).itemsize
    base, extra = divmod(rows // unit, n)
    out, start = [], 0
    for k in range(n):
        size = (base + (k < extra)) * unit
        if size:
            out.append((start, size))
            start += size
    assert start == rows, (rows, unit)
    return out


PIECE_BYTES = 1 << 20


def _pieces(shapes_dtypes, rows_of):
    out = []
    for b, (shape, dtype) in enumerate(shapes_dtypes):
        rows = rows_of(shape)
        n = max(1, min(4, rows * shape[-1] * jnp.dtype(dtype).itemsize // PIECE_BYTES))
        out += [(b, st, sz) for st, sz in _row_chunks(rows, n, dtype)]
    return out


def all_gather_chips(name, shards):
    nb = len(shards)
    pieces = _pieces([(s.shape, s.dtype) for s in shards], lambda shape: shape[0] // 2)
    n = len(pieces)

    def body(*refs):
        x_refs, out_refs, send_sems, recv_sems = refs[:nb], refs[nb:2 * nb], refs[2 * nb], refs[2 * nb + 1]
        x, y, c, chips = _place()
        sibling = (x, y, 1 - c)
        mine = 2 * x + y

        def copy(sem, chip, cc, k, to, from_input=False):
            b, st, sz = pieces[k]
            rows_k = pl.ds(cc * (x_refs[b].shape[0] // 2) + st, sz)
            dst = out_refs[b].at[chip, rows_k, :]
            return pltpu.make_async_remote_copy(src_ref=x_refs[b].at[rows_k, :] if from_input else dst, dst_ref=dst,
                                                send_sem=send_sems.at[sem], recv_sem=recv_sems.at[sem], device_id=to, device_id_type=MESH_ID)

        order = [(k, j, 2 * cx + cy, (cx, cy, c)) for k in range(n) for j, (cx, cy) in enumerate(chips)]
        first = [copy(j * n + k, mine, c, k, to, from_input=True) for k, j, _, to in order]
        for cp in first:
            cp.start()
        passed = []
        for k, j, chip, _ in order:
            copy(j * n + k, chip, c, k, sibling).wait_recv()
            passed.append(copy((3 + j) * n + k, chip, c, k, sibling))
            passed[-1].start()
        for k, j, chip, _ in order:
            copy((3 + j) * n + k, chip, 1 - c, k, sibling).wait_recv()
        for cp in first + passed:
            cp.wait_send()

    return pl.pallas_call(body, in_specs=[ANY] * nb, out_specs=[ANY] * nb, out_shape=[SDS((4,) + s.shape, s.dtype) for s in shards],
                          scratch_shapes=_dma_sems(6 * n), name=name)(*shards)


def plan_gather_ici(shards):
    pieces = _pieces([(s.shape, s.dtype) for s in shards], lambda shape: shape[0] // 2)
    n = len(pieces)

    def copies(x_refs, out_refs, send_sems, recv_sems):
        x, y, c, chips = _place()
        mine = 2 * x + y

        def copy(j, k, chip, to, from_input):
            b, st, sz = pieces[k]
            rows_k = pl.ds(c * (x_refs[b].shape[0] // 2) + st, sz)
            dst = out_refs[b].at[chip, rows_k, :]
            return pltpu.make_async_remote_copy(src_ref=x_refs[b].at[rows_k, :] if from_input else dst, dst_ref=dst, send_sem=send_sems.at[j * n + k],
                                                recv_sem=recv_sems.at[j * n + k], device_id=to, device_id_type=MESH_ID)

        order = [(k, j, 2 * cx + cy, (cx, cy, c)) for k in range(n) for j, (cx, cy) in enumerate(chips)]
        return [copy(j, k, mine, to, True) for k, j, _, to in order], [copy(j, k, chip, to, False) for k, j, chip, to in order]

    return Plan(shards, [SDS((4,) + s.shape, s.dtype) for s in shards], {}, 3 * n, copies)


def plan_gather_pass(gathered):
    pieces = _pieces([(g.shape[1:], g.dtype) for g in gathered], lambda shape: shape[0] // 2)
    n = len(pieces)

    def copies(_, out_refs, send_sems, recv_sems):
        x, y, c, chips = _place()

        def copy(j, k, chip, cc):
            b, st, sz = pieces[k]
            rows_k = out_refs[b].at[chip, pl.ds(cc * (out_refs[b].shape[1] // 2) + st, sz), :]
            return pltpu.make_async_remote_copy(src_ref=rows_k, dst_ref=rows_k, send_sem=send_sems.at[j * n + k], recv_sem=recv_sems.at[j * n + k],
                                                device_id=(x, y, 1 - c), device_id_type=MESH_ID)

        order = [(k, j, 2 * cx + cy) for k in range(n) for j, (cx, cy) in enumerate(chips)]
        return [copy(j, k, chip, c) for k, j, chip in order], [copy(j, k, chip, 1 - c) for k, j, chip in order]

    return Plan(gathered, [SDS(g.shape, g.dtype) for g in gathered], {i: i for i in range(len(gathered))}, 3 * n, copies)


def plan_pair_exchange(gs):
    pieces = _pieces([(g.shape, g.dtype) for g in gs], lambda shape: shape[1] // 2)

    def copies(g_refs, got_refs, send_sems, recv_sems):
        x, y, c, _ = _place()
        swaps = [pltpu.make_async_remote_copy(src_ref=g_refs[b].at[:, pl.ds((1 - c) * (g_refs[b].shape[1] // 2) + st, sz), :],
                                              dst_ref=got_refs[b].at[:, pl.ds(st, sz), :], send_sem=send_sems.at[k], recv_sem=recv_sems.at[k],
                                              device_id=(x, y, 1 - c), device_id_type=MESH_ID)
                 for k, (b, st, sz) in enumerate(pieces)]
        return swaps, swaps

    return Plan(gs, [SDS((g.shape[0], g.shape[1] // 2, g.shape[2]), g.dtype) for g in gs], {}, len(pieces), copies)


def plan_chip_scatter(ps):
    pieces = _pieces([(p.shape, p.dtype) for p in ps], lambda shape: shape[1])
    n = len(pieces)

    def copies(p_refs, q_refs, send_sems, recv_sems):
        x, y, c, chips = _place()
        mine = 2 * x + y

        def copy(j, k, src_slot, dst_slot, to):
            b, st, sz = pieces[k]
            return pltpu.make_async_remote_copy(src_ref=p_refs[b].at[src_slot, pl.ds(st, sz), :], dst_ref=q_refs[b].at[dst_slot, pl.ds(st, sz), :],
                                                send_sem=send_sems.at[j * n + k], recv_sem=recv_sems.at[j * n + k], device_id=to,
                                                device_id_type=MESH_ID)

        order = [(k, j, 2 * cx + cy, (cx, cy, c)) for k in range(n) for j, (cx, cy) in enumerate(chips)]
        return [copy(j, k, chip, mine, to) for k, j, chip, to in order], [copy(j, k, mine, chip, to) for k, j, chip, to in order]

    return Plan(ps, [SDS(p.shape, p.dtype) for p in ps], {}, 3 * n, copies)


def plan_pair_join(bufs):
    pieces = _pieces([(b.shape, b.dtype) for b in bufs], lambda shape: shape[0] // 2)

    def copies(_, out_refs, send_sems, recv_sems):
        x, y, c, _ = _place()

        def copy(k, cc):
            b, st, sz = pieces[k]
            rows_k = out_refs[b].at[pl.ds(cc * (out_refs[b].shape[0] // 2) + st, sz), :]
            return pltpu.make_async_remote_copy(src_ref=rows_k, dst_ref=rows_k, send_sem=send_sems.at[k], recv_sem=recv_sems.at[k],
                                                device_id=(x, y, 1 - c), device_id_type=MESH_ID)

        return [copy(k, c) for k in range(len(pieces))], [copy(k, 1 - c) for k in range(len(pieces))]

    return Plan(bufs, [SDS(b.shape, b.dtype) for b in bufs], {i: i for i in range(len(bufs))}, len(pieces), copies)


def run_plan(name, plan):
    hosted_call(lambda: None, name=name, grid=(1,), in_specs=[], out_specs=[], out_shape=[], operands=[], plans=[plan])
    return plan.results


HBM = pl.BlockSpec(memory_space=pltpu.HBM)
SEMS = pl.BlockSpec(memory_space=pltpu.SEMAPHORE)
SPLIT_PARAMS = dict(has_side_effects=pltpu.SideEffectType.DATAFLOW_SIDE_EFFECTING)


def _plan_buffers(plan):
    in_place = {o: i for i, o in plan.aliases.items()}
    bufs = [pltpu.with_memory_space_constraint(a, pltpu.HBM) for a in plan.operands]
    where = []
    for o, sd in enumerate(plan.out_shape):
        if o in in_place:
            where.append(in_place[o])
        else:
            where.append(len(bufs))
            bufs.append(pltpu.with_memory_space_constraint(lax.empty(sd.shape, sd.dtype), pltpu.HBM))
    return bufs, where


def split_start(name, plans):
    layout = [_plan_buffers(p) for p in plans]
    counts = [len(b) for b, _ in layout]
    n_buf = sum(counts)

    def body(*refs):
        sems, token = refs[n_buf:n_buf + 2 * len(plans)], refs[-1]
        pos = 0
        for k, (p, (_, where)) in enumerate(zip(plans, layout)):
            mine = refs[pos:pos + counts[k]]
            pos += counts[k]
            sends, _ = p.copies(mine[:len(p.operands)], [mine[w] for w in where], sems[2 * k], sems[2 * k + 1])
            for cp in sends:
                cp.start()
        token[...] = jnp.zeros_like(token)

    bufs = [b for bs, _ in layout for b in bs]
    res = pl.pallas_call(
        body, name=name, in_specs=[HBM] * n_buf,
        out_specs=[SEMS] * (2 * len(plans)) + [HBM] * n_buf + [pl.BlockSpec(memory_space=pltpu.VMEM)],
        out_shape=[pltpu.SemaphoreType.DMA((p.n_sems,)) for p in plans for _ in range(2)] + [pltpu.HBM(b.shape, b.dtype) for b in bufs]
        + [SDS((8, 128), F32)],
        input_output_aliases={i: 2 * len(plans) + i for i in range(n_buf)}, compiler_params=pltpu.CompilerParams(**SPLIT_PARAMS))(*bufs)
    pos = 2 * len(plans)
    for k, p in enumerate(plans):
        p.in_flight = (res[2 * k], res[2 * k + 1], list(res[pos:pos + counts[k]]), layout[k][1])
        pos += counts[k]
    return res[-1]


def split_wait(name, plan, after):
    send_sems, recv_sems, bufs, where = plan.in_flight
    n_buf = len(bufs)

    def body(*refs):
        mine = refs[:n_buf]
        sends, recvs = plan.copies(mine[:len(plan.operands)], [mine[w] for w in where], refs[n_buf], refs[n_buf + 1])
        for cp in recvs:
            cp.wait_recv()
        for cp in sends:
            cp.wait_send()

    res = pl.pallas_call(body, name=name, in_specs=[HBM] * n_buf + [SEMS, SEMS, ANY], out_specs=[HBM] * n_buf,
                         out_shape=[pltpu.HBM(b.shape, b.dtype) for b in bufs], input_output_aliases={i: i for i in range(n_buf)},
                         compiler_params=pltpu.CompilerParams(**SPLIT_PARAMS))(*bufs, send_sems, recv_sems, after)
    plan.results = [res[w] for w in where]
    return plan.results


def pair_add(name, g, got, place):
    slots, rows, cols = g.shape
    half = rows // 2
    tr = _row_tile(half)
    nb = half // tr

    def kern(_, g_ref, t_ref, o_ref):
        o_ref[...] = (g_ref[...].astype(F32) + t_ref[...].astype(F32)).astype(o_ref.dtype)

    blk = pl.BlockSpec((None, tr, cols), lambda s, i, p: (s, i, 0))
    grid_spec = pltpu.PrefetchScalarGridSpec(
        num_scalar_prefetch=1, grid=(slots, nb),
        in_specs=[pl.BlockSpec((None, tr, cols), lambda s, i, p: (s, p[1] * nb + i, 0)), blk], out_specs=blk)
    return pl.pallas_call(kern, grid_spec=grid_spec, out_shape=SDS((slots, half, cols), g.dtype), name=name,
                          compiler_params=_cparams())(place, g, got)


def chip_add(name, p, q, place):
    slots, half, cols = p.shape
    tr = _row_tile(half)
    nb = half // tr

    def kern(_, p_ref, q1, q2, q3, o_ref):
        o_ref[...] = p_ref[...].astype(F32) + q1[...].astype(F32) + q2[...].astype(F32) + q3[...].astype(F32)

    def slot(k):
        return pl.BlockSpec((None, tr, cols), lambda i, pr: ((pr[0] + k) % slots, i, 0))

    grid_spec = pltpu.PrefetchScalarGridSpec(
        num_scalar_prefetch=1, grid=(nb,), in_specs=[slot(0), slot(1), slot(2), slot(3)],
        out_specs=pl.BlockSpec((tr, cols), lambda i, pr: (pr[1] * nb + i, 0)))
    return pl.pallas_call(kern, grid_spec=grid_spec, out_shape=SDS((2 * half, cols), F32), name=name,
                          compiler_params=_cparams())(place, p, q, q, q)


def pair_adds(tag, gs, gots, place):
    return [pair_add(f"{tag}_pair_add_{i}", g, got, place) for i, (g, got) in enumerate(zip(gs, gots))]


def chip_adds(tag, pairs, qs, place):
    return [chip_add(f"{tag}_chip_add_{i}", p, q, place) for i, (p, q) in enumerate(zip(pairs, qs))]


def reduce_scatter_chips(tag, gs, place):
    pairs = pair_adds(tag, gs, run_plan(tag + "_pair_exchange", plan_pair_exchange(gs)), place)
    return run_plan(tag + "_pair_join", plan_pair_join(chip_adds(tag, pairs, run_plan(tag + "_chip_scatter", plan_chip_scatter(pairs)), place)))


BIG = [("w_out", (2, 512, 1024)), ("w_mem_kv", (2, 256, 1024)), ("s5_w_in", (1, 1024, 1024)), ("s5_w_glu", (1, 1536, 768)),
       ("mla_w_in", (1, 1024, 848)), ("mla_w_uq", (1, 512, 576)), ("mla_w_ukv", (1, 256, 768))]
SHARDED_SMALL = [("mla_q_lora_norm", (1, 128)), ("mla_kv_lora_norm", (1, 64))]
SMALL = [("ln_gain", (2, 1024)), ("mem_norm", (2, 1024)), ("xq_norm", (2, 128)), ("xk_norm", (2, 128)),
         ("s5_lambda_re", (1, 96, 64)), ("s5_lambda_im", (1, 96, 64)), ("s5_log_step", (1, 96)),
         ("s5_b_re", (1, 96, 64, 16)), ("s5_b_im", (1, 96, 64, 16)), ("s5_c_re", (1, 96, 16, 64)), ("s5_c_im", (1, 96, 16, 64)),
         ("s5_d", (1, 1536)), ("mla_q_nope_norm", (1, 128)), ("mla_k_nope_norm", (1, 128)), ("mla_q_rope_norm", (1, 64)),
         ("mla_k_rope_norm", (1, 64))]
WEIGHT_ORDER = ["ln_gain", "w_out", "mem_norm", "w_mem_kv", "xq_norm", "xk_norm", "s5_w_in", "s5_lambda_re", "s5_lambda_im",
                "s5_log_step", "s5_b_re", "s5_b_im", "s5_c_re", "s5_c_im", "s5_d", "s5_w_glu", "mla_w_in", "mla_q_lora_norm",
                "mla_kv_lora_norm", "mla_w_uq", "mla_w_ukv", "mla_q_nope_norm", "mla_k_nope_norm", "mla_q_rope_norm", "mla_k_rope_norm"]
MINOR_LAST = {"mla_w_in": (0, 2, 1), "mla_w_uq": (0, 2, 1), "s5_b_re": (0, 2, 3, 1), "s5_b_im": (0, 2, 3, 1),
              "s5_c_re": (0, 2, 3, 1), "s5_c_im": (0, 2, 3, 1)}
SMALL_FULL = SMALL + [(n, (1, 4 * s[1])) for n, s in SHARDED_SMALL]
N_SMALL = sum(math.prod(s) for _, s in SMALL_FULL)
SMALL_ROWS, SMALL_LANES = 128, 1024

PAIR_OUT, PAIR_MKV, PAIR_ROWS = 0, 512, 768


def stack_shards(w, dtype):
    pairs = [jnp.concatenate([w["w_out"][l], w["w_mem_kv"][l]], axis=0).astype(dtype) for l in range(2)]
    return ([w["s5_w_in"][0].astype(dtype)], [pairs[0], w["s5_w_glu"][0].astype(dtype)],
            [pairs[1], w["mla_w_ukv"][0].astype(dtype), jnp.pad(w["mla_w_in"][0].T.astype(dtype), ((0, IN1T_ROWS - MLA_IN // 4), (0, 0))),
             w["mla_w_uq"][0].T.astype(dtype)])


def pair_views(pair):
    return {"w_out": Sharded(pair, "row", PAIR_OUT, 512), "w_mem_kv": Sharded(pair, "row", PAIR_MKV, 256)}


def grad_views():
    pair = SDS((4, PAIR_ROWS, 1024), BF16)
    return {"w_out": Sharded(pair, "row", PAIR_OUT, 512), "w_mem_kv": Sharded(pair, "row", PAIR_MKV, 256),
            "s5_w_in": Sharded(SDS((4, 1024, 1024), BF16), "col", 0, 1024), "s5_w_glu": Sharded(SDS((4, 1536, 768), BF16), "col", 0, 1536),
            "mla_w_ukv": Sharded(SDS((4, 256, 768), BF16), "col", 0, 256)}


IN1T_ROWS = 864


def in1_rows_permute(wt):
    o1, o2, o3, o4 = QL, QL + KVL, QL + KVL + ROPE, QL + KVL + ROPE + XQW
    return jnp.concatenate([wt[o4:], wt[:o1], wt[o3:o4], wt[o1:o2], wt[o2:o3], jnp.zeros((MLA_IN_P - MLA_IN, wt.shape[1]), wt.dtype)], axis=0)


def in1_rows_unpermute(d):
    return jnp.concatenate([d[2048:2560], d[3072:3328], d[3328:3392], d[2560:3072], d[:2048]], axis=0)


def uq_rows_permute(wt):
    w3 = wt.reshape(MH, NOPE + ROPE, wt.shape[1])
    return jnp.concatenate([w3[:, :NOPE].reshape(MH * NOPE, wt.shape[1]), w3[:, NOPE:].reshape(MH * ROPE, wt.shape[1])], axis=0)


def uq_rows_unpermute(d):
    dn = d[:MH * NOPE].reshape(MH, NOPE, d.shape[1])
    dr = d[MH * NOPE:].reshape(MH, ROPE, d.shape[1])
    return jnp.concatenate([dn, dr], axis=1).reshape(MH * (NOPE + ROPE), d.shape[1])


def time_permute(a):
    return a.reshape(SEG, SEG_LEN, a.shape[-1]).transpose(1, 0, 2).reshape(L, a.shape[-1])


def time_unpermute(a):
    return a.reshape(SEG_LEN, SEG, a.shape[-1]).transpose(1, 0, 2).reshape(L, a.shape[-1])


def mem_branch_fwd(tag, mem, mem_norm, w_mem_kv, xk_norm):
    mn = row_fwd(tag + "_mem_rms", fn_rms, ML, ML, [(mem, D, 0)], [mem_norm], [(D, BF16)])[0]
    kv = matmul(tag + "_mem_kv", mn, w_mem_kv, "nn", F32)
    kn = row_fwd(tag + "_mem_knorm", fn_mem_k, ML, ML, [(kv, XQW, 0)], [xk_norm], [(XQW, F32)])[0]
    return mn, kv, kn


def mem_branch_bwd(tag, mem, mem_norm, w_mem_kv, xk_norm, mn, kv, dkn, dv, g_view, g_wide):
    dk, dxk = row_bwd(tag + "_mem_knorm_bwd", fn_mem_k, ML, ML, [(kv, XQW, 0)], [xk_norm], [(dkn, XQW, 0)], [True], [True])
    dkv = jnp.concatenate([dk, dv], axis=1)
    dmn = matmul(tag + "_mem_kv_dx", dkv, w_mem_kv, "nt", F32)
    g_wide = matmul(tag + "_mem_kv_dw", mn, dkv, "tn", out=g_view, into=g_wide)
    dmem_norm = row_bwd(tag + "_mem_rms_bwd", fn_rms, ML, ML, [(mem, D, 0)], [mem_norm], [(dmn, D, 0)], [False], [True])[0]
    return g_wide, dmem_norm, dxk


def mem_attn_fwd(tag, proj, cb, kn, kv, xq_norm):
    return row_fwd(tag + "_mem_attn", fn_mem_attn, L, ROW_TILE, [(proj, XQW, cb)], [kn, kv[:, XQW:], xq_norm], [(XQW, F32)])[0]


def mem_attn_bwd(tag, proj, cb, kn, kv, xq_norm, dmo, dproj):
    place = {"cols": proj.shape[1], "cb": cb, "into": dproj, "dtype": dproj.dtype}
    return row_bwd(tag + "_mem_attn_bwd", fn_mem_attn, L, ROW_TILE, [(proj, XQW, cb)], [kn, kv[:, XQW:], xq_norm], [(dmo, XQW, 0)],
                   [place], [True, True, True])


def device_step(x, mem, positions, target, small, env, hooks=None):
    hooks = hooks or {}

    def plans_for(name):
        return hooks[("plans", name)](env) if ("plans", name) in hooks else ()

    def around(when, name, last=None):
        if (when, name) in hooks:
            hooks[(when, name)](env, last)

    g = {}
    gw = grad_views()
    ln, mem_norm, xq_norm, xk_norm = small["ln_gain"], small["mem_norm"], small["xq_norm"], small["xk_norm"]

    lre, lim = small["s5_lambda_re"][0], small["s5_lambda_im"][0]
    ls = small["s5_log_step"].reshape(SG, 1)
    one = pl.BlockSpec((SG, SP), lambda i: (0, 0))
    col = pl.BlockSpec((SG, 1), lambda i: (0, 0))
    disc_ins = [(lre, one), (lim, one), (ls, col)]
    a_re, a_im, coef_re, coef_im = stage("s5_disc", fn_s5_disc, (1,), disc_ins, [(SDS((SG, SP), F32), one)] * 4)
    b_re, b_im = small["s5_b_re"].reshape(SN, SC), small["s5_b_im"].reshape(SN, SC)
    c_re, c_im = small["s5_c_re"].reshape(PW, SP), small["s5_c_im"].reshape(PW, SP)
    bmat_rows = [(b_re, SC, 0), (b_im, SC, 0), (coef_re.reshape(SN, 1), 1, 0), (coef_im.reshape(SN, 1), 1, 0)]
    wb_re, wb_im = row_fwd("s5_bmat", fn_s5_bmat, SN, BMAT_TILE, bmat_rows, [], [(128, F32)] * 2)
    cmat_rows = [(c_re, SP, 0), (c_im, SP, 0)]
    wc_re, wc_im = row_fwd("s5_cmat", fn_s5_cmat, PW, CMAT_TILE, cmat_rows, [], [(512, F32)] * 2)
    a_re_v, a_im_v = a_re.reshape(1, SN), a_im.reshape(1, SN)
    s5_d = small["s5_d"]

    xp = time_permute(x)
    h0 = row_fwd("l0_rms", fn_rms, L, ROW_TILE, [(xp, D, 0)], [ln[0:1]], [(D, BF16)])[0]
    around("before", "l0_in", wb_re)
    w_in0 = Sharded(env["in0"], "col", 0, 1024)
    proj0 = matmul("l0_in", h0, w_in0, "nn")
    s_re, s_im, g0 = s5_forward(proj0, wb_re, wb_im, wc_re, wc_im, a_re_v, a_im_v, s5_d, plans=plans_for("s5_forward"))
    around("before", "l0_glu", g0)
    w0 = dict(pair_views(env["pair0"]), s5_w_glu=Sharded(env["glu"], "col", 0, 1536))
    z0 = matmul("l0_glu", g0, w0["s5_w_glu"], "nn", plans=plans_for("l0_glu"))
    mn0, kv0, kn0 = mem_branch_fwd("l0", mem, mem_norm[0:1], w0["w_mem_kv"], xk_norm[0:1])
    mo0 = mem_attn_fwd("l0", proj0, 3, kn0, kv0, xq_norm[0:1])
    o0 = row_fwd("l0_merge", fn_merge_glu, L, ROW_TILE, [(z0, 2 * PW, 0), (mo0, XQW, 0), (proj0, BW, 1)], [], [(BW, BF16)])[0]
    around("before", "l0_out", o0)
    x1p = matmul("l0_out", o0, w0["w_out"], "nn", F32, add=xp, plans=plans_for("l0_out"))
    around("after", "l0_out", x1p)
    x1 = time_unpermute(x1p)

    w1 = dict(pair_views(env["pair1"]), mla_w_ukv=Sharded(env["ukv"], "col", 0, 256))
    w_in1, w_uq = env["w_in1"], env["w_uq"]
    h1 = row_fwd("l1_rms", fn_rms, L, ROW_TILE, [(x1, D, 0)], [ln[1:2]], [(D, BF16)])[0]
    proj1 = matmul("l1_in", h1, w_in1, "nt")
    qln, kvln = env["q_lora_norm"].reshape(1, QL), env["kv_lora_norm"].reshape(1, KVL)
    cqn = row_fwd("l1_q_lora_rms", fn_rms, L, ROW_TILE, [(proj1, QL, 4)], [qln], [(QL, BF16)])[0]
    ckvn = row_fwd("l1_kv_lora_rms", fn_rms, L, ROW_TILE, [(proj1, KVL, 12)], [kvln], [(KVL, BF16)])[0]
    q = matmul("l1_uq", cqn, w_uq, "nt")
    kv = matmul("l1_ukv", ckvn, w1["mla_w_ukv"], "nn")
    inv_freq = ROPE_THETA ** (-jnp.arange(ROPE // 2, dtype=F32) / (ROPE // 2))
    ang = positions.astype(F32)[:, None] * inv_freq
    cos2 = jnp.tile(jnp.cos(ang), (1, 4))
    sin_signed = jnp.tile(jnp.concatenate([-jnp.sin(ang), jnp.sin(ang)], axis=1), (1, 2))
    qnn, knn = small["mla_q_nope_norm"], small["mla_k_nope_norm"]
    qrn, krn = jnp.tile(small["mla_q_rope_norm"], (1, 2)), jnp.tile(small["mla_k_rope_norm"], (1, 2))
    tp = 256
    prep_ins = [(q, rspec(tp, MH * (NOPE + ROPE))), (kv, rspec(tp, MH * 256)), (proj1, rspec(tp, 128, 26)),
                (cos2, rspec(tp, 128)), (sin_signed, rspec(tp, 128))] + [(a, cspec((1, 128))) for a in (qnn, knn, qrn, krn)]
    hq_spec = pl.BlockSpec((MH, tp, 256), lambda i: (0, i, 0))
    hv_spec = pl.BlockSpec((MH, tp, 128), lambda i: (0, i, 0))
    qf, kf, vh = stage("l1_mla_prep", fn_mla_prep, (L // tp,), prep_ins,
                       [(SDS((MH, L, 256), BF16), hq_spec), (SDS((MH, L, 256), BF16), hq_spec), (SDS((MH, L, 128), BF16), hv_spec)])
    attn, attn_lse = causal_attn(qf, kf, vh)
    mn1, kv1, kn1 = mem_branch_fwd("l1", mem, mem_norm[1:2], w1["w_mem_kv"], xk_norm[1:2])
    mo1 = mem_attn_fwd("l1", proj1, 5, kn1, kv1, xq_norm[1:2])
    o1 = row_fwd("l1_merge", fn_merge, L, ROW_TILE, [(attn, PW, 0), (mo1, XQW, 0), (proj1, BW, 0)], [], [(BW, BF16)])[0]
    x2 = matmul("l1_out", o1, w1["w_out"], "nn", F32, add=x1)
    dx2, loss = loss_and_grad(x2, target)
    around("after", "loss", loss)

    do1 = matmul("l1_out_dx", dx2, w1["w_out"], "nt")
    g_pair1 = matmul("l1_out_dw", o1, dx2, "tn", out=gw["w_out"])
    dattn, dmo1, dproj1 = row_bwd("l1_merge_bwd", fn_merge, L, ROW_TILE, [(attn, PW, 0), (mo1, XQW, 0), (proj1, BW, 0)], [],
                                  [(do1, BW, 0)], [True, True, {"cols": MLA_IN_P, "cb": 0, "dtype": BF16}], [])
    dproj1, dkn1, dv1, dxqn1 = mem_attn_bwd("l1", proj1, 5, kn1, kv1, xq_norm[1:2], dmo1, dproj1)
    env["g_pair1"], dmem_norm1, dxk1 = mem_branch_bwd("l1", mem, mem_norm[1:2], w1["w_mem_kv"], xk_norm[1:2], mn1, kv1, dkn1, dv1,
                                                      gw["w_mem_kv"], g_pair1)
    dqf, dkf, dvh = causal_attn_bwd(qf, kf, vh, attn, attn_lse, dattn)
    prep_diffs = [("row", SDS((L, MH * (NOPE + ROPE)), BF16), rspec(tp, MH * (NOPE + ROPE))), ("row", SDS((L, MH * 256), BF16), rspec(tp, MH * 256)),
                  ("row", SDS((L, MLA_IN_P), BF16), rspec(tp, 128, 26), {"into": dproj1}), None, None] + [("acc", (0,))] * 4
    dq, dkv, dproj1, dqnn, dknn, dqrn, dkrn = stage_bwd("l1_mla_prep_bwd", fn_mla_prep, (L // tp,), prep_ins,
                                                        [(dqf, hq_spec), (dkf, hq_spec), (dvh, hv_spec)], prep_diffs)
    dcqn = matmul("l1_uq_dx", dq, w_uq, "nn")
    env["dw_uq"] = matmul("l1_uq_dw", dq, cqn, "tn")
    dckvn = matmul("l1_ukv_dx", dkv, w1["mla_w_ukv"], "nt")
    env["g_ukv"] = matmul("l1_ukv_dw", ckvn, dkv, "tn", out=gw["mla_w_ukv"])
    dproj1, dqln = row_bwd("l1_q_lora_rms_bwd", fn_rms, L, ROW_TILE, [(proj1, QL, 4)], [qln], [(dcqn, QL, 0)],
                           [{"cols": MLA_IN_P, "cb": 4, "into": dproj1, "dtype": BF16}], [True])
    dproj1, dkvln = row_bwd("l1_kv_lora_rms_bwd", fn_rms, L, ROW_TILE, [(proj1, KVL, 12)], [kvln], [(dckvn, KVL, 0)],
                            [{"cols": MLA_IN_P, "cb": 12, "into": dproj1, "dtype": BF16}], [True])
    env["dw_in1"] = matmul("l1_in_dw", dproj1, h1, "tn")
    dh1 = matmul("l1_in_dx", dproj1, w_in1, "nn", plans=plans_for("l1_in_dx"))
    around("after", "l1_in_dx", dh1)
    dx1, dln1 = row_bwd("l1_rms_bwd", fn_rms, L, ROW_TILE, [(x1, D, 0)], [ln[1:2]], [(dh1, D, 0)], [{"add": (dx2, D, 0)}], [True])
    dx1p = time_permute(dx1)

    do0 = matmul("l0_out_dx", dx1p, w0["w_out"], "nt", plans=plans_for("l0_out_dx"))
    g_pair0 = matmul("l0_out_dw", o0, dx1p, "tn", out=gw["w_out"])
    dz0, dmo0, dproj0 = row_bwd("l0_merge_bwd", fn_merge_glu, L, ROW_TILE, [(z0, 2 * PW, 0), (mo0, XQW, 0), (proj0, BW, 1)], [],
                                [(do0, BW, 0)], [{"dtype": BF16}, True, {"cols": 2 * BW, "cb": 1, "dtype": BF16}], [])
    dproj0, dkn0, dv0, dxqn0 = mem_attn_bwd("l0", proj0, 3, kn0, kv0, xq_norm[0:1], dmo0, dproj0)
    env["g_pair0"], dmem_norm0, dxk0 = mem_branch_bwd("l0", mem, mem_norm[0:1], w0["w_mem_kv"], xk_norm[0:1], mn0, kv0, dkn0, dv0,
                                                      gw["w_mem_kv"], g_pair0)
    env["g_glu"] = matmul("l0_glu_dw", g0, dz0, "tn", out=gw["s5_w_glu"], plans=plans_for("l0_glu_dw"))
    dg0 = matmul("l0_glu_dx", dz0, w0["s5_w_glu"], "nt", plans=plans_for("l0_glu_dx"))
    around("before", "s5_backward", dg0)
    dproj0, dd, dwc_re, dwc_im, dwb_re, dwb_im, da_re, da_im = s5_backward(dg0, proj0, s_re, s_im, wb_re, wb_im, wc_re, wc_im,
                                                                           a_re_v, a_im_v, s5_d, dproj0, plans=plans_for("s5_backward"))
    around("after", "s5_backward", dd)
    env["g_in0"] = matmul("l0_in_dw", h0, dproj0, "tn", out=gw["s5_w_in"], plans=plans_for("l0_in_dw"))
    dh0 = matmul("l0_in_dx", dproj0, w_in0, "nt", plans=plans_for("l0_in_dx"))
    around("after", "l0_in_dx", dh0)
    dxp, dln0 = row_bwd("l0_rms_bwd", fn_rms, L, ROW_TILE, [(xp, D, 0)], [ln[0:1]], [(dh0, D, 0)], [{"add": (dx1p, D, 0)}], [True])
    grad_x = time_unpermute(dxp)

    db_re, db_im, dcoef_re, dcoef_im = row_bwd("s5_bmat_bwd", fn_s5_bmat, SN, BMAT_TILE, bmat_rows, [], [(dwb_re, 128, 0), (dwb_im, 128, 0)],
                                               [True] * 4, [], plans=plans_for("s5_bmat_bwd"))
    dc_re, dc_im = row_bwd("s5_cmat_bwd", fn_s5_cmat, PW, CMAT_TILE, cmat_rows, [], [(dwc_re, 512, 0), (dwc_im, 512, 0)], [True] * 2, [],
                           plans=plans_for("s5_cmat_bwd"))
    disc_cts = [(da_re.reshape(SG, SP), one), (da_im.reshape(SG, SP), one), (dcoef_re.reshape(SG, SP), one), (dcoef_im.reshape(SG, SP), one)]
    dlre, dlim, dls = stage_bwd("s5_disc_bwd", fn_s5_disc, (1,), disc_ins, disc_cts, [("acc", (0,))] * 3)

    g["ln_gain"] = jnp.concatenate([dln0, dln1], axis=0)
    g["mem_norm"] = jnp.concatenate([dmem_norm0, dmem_norm1], axis=0)
    g["xq_norm"] = jnp.concatenate([dxqn0, dxqn1], axis=0)
    g["xk_norm"] = jnp.concatenate([dxk0, dxk1], axis=0)
    g["s5_lambda_re"], g["s5_lambda_im"], g["s5_log_step"] = dlre, dlim, dls
    g["s5_b_re"], g["s5_b_im"], g["s5_c_re"], g["s5_c_im"] = db_re, db_im, dc_re, dc_im
    g["s5_d"] = dd
    g["mla_q_lora_norm"], g["mla_kv_lora_norm"] = dqln, dkvln
    g["mla_q_nope_norm"], g["mla_k_nope_norm"] = dqnn, dknn
    g["mla_q_rope_norm"] = dqrn[:, :ROPE] + dqrn[:, ROPE:]
    g["mla_k_rope_norm"] = dkrn[:, :ROPE] + dkrn[:, ROPE:]
    return loss, grad_x, g


def kernel(x, mem, positions, ln_gain, w_out, mem_norm, w_mem_kv, xq_norm, xk_norm, s5_w_in, s5_lambda_re, s5_lambda_im, s5_log_step, s5_b_re, s5_b_im, s5_c_re, s5_c_im, s5_d, s5_w_glu, mla_w_in, mla_q_lora_norm, mla_kv_lora_norm, mla_w_uq, mla_w_ukv, mla_q_nope_norm, mla_k_nope_norm, mla_q_rope_norm, mla_k_rope_norm, loss_target, m_ln_gain, m_w_out, m_mem_norm, m_w_mem_kv, m_xq_norm, m_xk_norm, m_s5_w_in, m_s5_lambda_re, m_s5_lambda_im, m_s5_log_step, m_s5_b_re, m_s5_b_im, m_s5_c_re, m_s5_c_im, m_s5_d, m_s5_w_glu, m_mla_w_in, m_mla_q_lora_norm, m_mla_kv_lora_norm, m_mla_w_uq, m_mla_w_ukv, m_mla_q_nope_norm, m_mla_k_nope_norm, m_mla_q_rope_norm, m_mla_k_rope_norm, v_ln_gain, v_w_out, v_mem_norm, v_w_mem_kv, v_xq_norm, v_xk_norm, v_s5_w_in, v_s5_lambda_re, v_s5_lambda_im, v_s5_log_step, v_s5_b_re, v_s5_b_im, v_s5_c_re, v_s5_c_im, v_s5_d, v_s5_w_glu, v_mla_w_in, v_mla_q_lora_norm, v_mla_kv_lora_norm, v_mla_w_uq, v_mla_w_ukv, v_mla_q_nope_norm, v_mla_k_nope_norm, v_mla_q_rope_norm, v_mla_k_rope_norm):
    args = dict(locals())
    wts = {n: args[n] for n in WEIGHT_ORDER}
    mom = {n: args["m_" + n] for n in WEIGHT_ORDER}
    var = {n: args["v_" + n] for n in WEIGHT_ORDER}

    chip = 2 * lax.axis_index("x") + lax.axis_index("y")
    place = jnp.stack([chip, lax.axis_index("c")]).astype(jnp.int32)

    def own_slot(gathered, shards):
        return [lax.dynamic_update_slice(g, s[None], (chip, 0, 0)) for g, s in zip(gathered, shards)]

    groups = list(stack_shards(wts, BF16))
    groups[2].append(jnp.concatenate([mla_q_lora_norm, jnp.pad(mla_kv_lora_norm, ((0, 0), (0, 64))), jnp.zeros((14, 128), F32)], axis=0))
    over_ici = [plan_gather_ici(shards) for shards in groups]
    _SCHEDULE_BEHIND.clear()
    schedule_behind(split_start("gather_start", over_ici))
    env, hooks, passed_on = {}, {}, {}

    def arrived(k, after, pass_now):
        passing = plan_gather_pass(split_wait(f"gather_wait_{k}", over_ici[k], after))
        passed_on[k] = passing
        return own_slot(run_plan(f"gather_pass_{k}", passing), groups[k]) if pass_now else None

    def need_in0(env, last):
        env["in0"], = arrived(0, last, True)

    def need_layer0(env, last):
        env["pair0"], env["glu"] = arrived(1, last, True)

    def need_layer1(env, last):
        arrived(2, last, False)

    def layer1_weights(env, last):
        pair1, ukv, in1, uq, norms = own_slot(passed_on[2].results, groups[2])
        env.update(pair1=pair1, ukv=ukv, w_in1=in1_rows_permute(in1[:, :MLA_IN // 4].reshape(MLA_IN, D)), w_uq=uq_rows_permute(uq.reshape(-1, QL)),
                   q_lora_norm=norms[:, 0, :], kv_lora_norm=norms[:, 1, :64])

    hooks["before", "l0_in"], hooks["before", "l0_glu"], hooks["before", "l0_out"] = need_in0, need_layer0, need_layer1
    hooks["plans", "l0_out"], hooks["after", "l0_out"] = (lambda env: [passed_on[2]]), layer1_weights

    rs = {}

    def swap(k, gs):
        rs[k, "g"], rs[k, "swap"] = gs, plan_pair_exchange(gs)
        return rs[k, "swap"]

    def start_scatter(k):
        rs[k, "pairs"] = pair_adds(f"rs{k}", rs[k, "g"], rs[k, "swap"].results, place)
        rs[k, "scatter"] = plan_chip_scatter(rs[k, "pairs"])
        schedule_behind(split_start(f"rs{k}_scatter_start", [rs[k, "scatter"]]))

    def join(k, after):
        rs[k, "join"] = plan_pair_join(chip_adds(f"rs{k}", rs[k, "pairs"], split_wait(f"rs{k}_scatter_wait", rs[k, "scatter"], after), place))
        return rs[k, "join"]

    def layer1_gradients(env):
        g_in1 = jnp.pad(in1_rows_unpermute(env["dw_in1"]).reshape(4, MLA_IN // 4, D), ((0, 0), (0, IN1T_ROWS - MLA_IN // 4), (0, 0)))
        return [swap(1, [env["g_pair1"], env["g_ukv"], g_in1, uq_rows_unpermute(env["dw_uq"]).reshape(4, -1, QL)])]

    hooks["plans", "l1_in_dx"] = layer1_gradients
    hooks["after", "l1_in_dx"] = lambda env, last: start_scatter(1)
    hooks["plans", "l0_glu_dx"] = lambda env: [swap(0, [env["g_pair0"], env["g_glu"]])]

    def before_s5_backward(env, last):
        start_scatter(0)
        env["join1"] = join(1, last)

    hooks["before", "s5_backward"] = before_s5_backward
    hooks["plans", "s5_backward"] = lambda env: [env["join1"]]
    hooks["after", "s5_backward"] = lambda env, last: env.update(join0=join(0, last))
    hooks["plans", "l0_in_dx"] = lambda env: [swap(2, [env["g_in0"]]), env["join0"]]
    hooks["after", "l0_in_dx"] = lambda env, last: start_scatter(2)

    def total_loss(env, local):
        env["loss"] = lax.psum(local[0, 0], MESH_AXES)
        schedule_behind(env["loss"].reshape(1, 1))

    hooks["after", "loss"] = total_loss
    small = {n: wts[n] for n, _ in SMALL}
    loss, grad_x, g = device_step(x[0], mem[0], positions[0], loss_target[0], small, env, hooks)
    loss = env["loss"]
    r_in0, = run_plan("rs2_pair_join", join(2, g["s5_log_step"]))
    (r_pair1, r_ukv, r_in1, r_uq), (r_pair0, r_glu) = (rs[k, "join"].results for k in (1, 0))

    small_flat = jnp.concatenate([g[n].reshape(-1) for n, _ in SMALL_FULL])
    g_small = jnp.pad(small_flat, (0, 4 * SMALL_ROWS * SMALL_LANES - N_SMALL)).astype(BF16).reshape(4, SMALL_ROWS, SMALL_LANES)
    r_small = reduce_scatter_chips("rs3", [g_small], place)[0]
    small_all = own_slot(all_gather_chips("gather_small_grads", [r_small]), [r_small])[0].reshape(-1)[:N_SMALL]

    grads = {"w_out": jnp.stack([r_pair0[:PAIR_MKV], r_pair1[:PAIR_MKV]]), "w_mem_kv": jnp.stack([r_pair0[PAIR_MKV:], r_pair1[PAIR_MKV:]]),
             "s5_w_in": r_in0[None], "s5_w_glu": r_glu[None], "mla_w_ukv": r_ukv[None], "mla_w_in": r_in1[:MLA_IN // 4].T[None], "mla_w_uq": r_uq.T[None]}
    off = 0
    for n, s in SMALL_FULL:
        grads[n] = small_all[off:off + math.prod(s)].reshape(s)
        off += math.prod(s)
    for n, s in SHARDED_SMALL:
        grads[n] = lax.dynamic_slice(grads[n], (0, chip * s[1]), s)

    delta, new_m, new_v = {}, {}, {}
    for n, s in BIG + [(n, s) for n, s in SMALL if len(s) == 4]:
        perm = MINOR_LAST.get(n, tuple(range(len(s))))
        turned = tuple(s[p] for p in perm)
        view = lambda a: jnp.transpose(a, perm).reshape(-1, turned[-1])
        res = adamw("adamw_" + n, view(wts[n]), view(grads[n]), view(mom[n]), view(var[n]))
        delta[n], new_m[n], new_v[n] = (jnp.transpose(r.reshape(turned), tuple(perm.index(i) for i in range(len(s)))) for r in res)
    small_names = [n for n, s in SMALL if len(s) < 4] + [n for n, _ in SHARDED_SMALL]
    n_own = sum(wts[n].size for n in small_names)
    rows_own = -(-n_own // (8 * 128)) * 8

    def pack_small(d):
        flat = jnp.concatenate([d[n].reshape(-1) for n in small_names])
        return jnp.pad(flat, (0, rows_own * 128 - n_own), constant_values=1.0).reshape(rows_own, 128)

    res = adamw("adamw_small", pack_small(wts), pack_small(grads), pack_small(mom), pack_small(var))
    off = 0
    for n in small_names:
        size = wts[n].size
        delta[n], new_m[n], new_v[n] = (r.reshape(-1)[off:off + size].reshape(wts[n].shape) for r in res)
        off += size

    return (loss, grad_x[None], *[grads[n] for n in WEIGHT_ORDER], *[delta[n] for n in WEIGHT_ORDER],
            *[new_m[n] for n in WEIGHT_ORDER], *[new_v[n] for n in WEIGHT_ORDER])
```

```python
import functools
import math

import jax
import jax.numpy as jnp
from jax import lax
from jax.experimental import pallas as pl
from jax.experimental.pallas import tpu as pltpu

F32, BF16 = jnp.float32, jnp.bfloat16
SDS = jax.ShapeDtypeStruct

D = 1024
L = 2048
ML = 256
BW = 2 * D
XQW = BW // 4
PW = BW - XQW
XH, XHD = 4, 128
SG, SC, SP = 96, 16, 64
SN = SG * SP
NOPE, ROPE, VD = 128, 64, 128
MH = 12
QL, KVL = 512, 256
EPS = 1e-6
ROPE_THETA = 10000.0
MLA_IN = QL + KVL + ROPE + XQW + BW
MLA_IN_P = 3456
ADAM_LR, ADAM_B1, ADAM_B2, ADAM_EPS, ADAM_WD, ADAM_STEP = 0.001, 0.9, 0.999, 1e-08, 0.01, 10

VMEM_LIMIT = 48 * 2**20
ROW_TILE = 512
BMAT_TILE, CMAT_TILE = 2048, 512
SEG = 8
SEG_LEN = L // SEG
MESH_AXES = ("x", "y", "c")


def _cparams():
    return pltpu.CompilerParams(vmem_limit_bytes=VMEM_LIMIT)


def _dg(a, b, ca, cb):
    return lax.dot_general(a.astype(BF16), b.astype(BF16), (((ca,), (cb,)), ((), ())), preferred_element_type=F32)


@jax.custom_vjp
def mm_nn(a, b):
    return _dg(a, b, 1, 0)


mm_nn.defvjp(lambda a, b: (_dg(a, b, 1, 0), (a, b)), lambda res, g: (_dg(g, res[1], 1, 1), _dg(res[0], g, 0, 0)))


@jax.custom_vjp
def mm_nt(a, b):
    return _dg(a, b, 1, 1)


mm_nt.defvjp(lambda a, b: (_dg(a, b, 1, 1), (a, b)), lambda res, g: (_dg(g, res[1], 1, 0), _dg(g, res[0], 0, 0)))


@functools.partial(jax.custom_vjp, nondiff_argnums=(1,))
def lane_roll(x, shift):
    return pltpu.roll(x, shift, 1)


lane_roll.defvjp(lambda x, shift: (pltpu.roll(x, shift, 1), None),
                 lambda shift, _, g: (pltpu.roll(g, (128 - shift) % 128, 1),))


def rms(x, g):
    return x * lax.rsqrt(jnp.mean(x * x, axis=-1, keepdims=True) + EPS) * g


@jax.custom_vjp
def softmax_rows(s):
    e = jnp.exp(s - jnp.max(s, axis=-1, keepdims=True))
    return e / jnp.sum(e, axis=-1, keepdims=True)


def _softmax_rows_fwd(s):
    p = softmax_rows(s)
    return p, p


def _softmax_rows_bwd(p, g):
    return (p * (g - jnp.sum(g * p, axis=-1, keepdims=True)),)


softmax_rows.defvjp(_softmax_rows_fwd, _softmax_rows_bwd)


def silu(x):
    return x * jax.nn.sigmoid(x)


ANY = pl.BlockSpec(memory_space=pl.ANY)
MESH_ID = pl.DeviceIdType.MESH


def _dma_sems(n):
    return [pltpu.SemaphoreType.DMA((n,)), pltpu.SemaphoreType.DMA((n,))]


class Plan:
    def __init__(self, operands, out_shape, aliases, n_sems, copies):
        self.operands, self.out_shape, self.aliases, self.n_sems, self.copies = list(operands), list(out_shape), aliases, n_sems, copies
        self.results = None


_SCHEDULE_BEHIND = []


def schedule_behind(token):
    _SCHEDULE_BEHIND.append(token)


def hosted_call(kern, *, name, grid, in_specs, out_specs, out_shape, operands, scratch_shapes=(), aliases=None, cparams=None, plans=(), deps=()):
    n_in, n_out, n_scr = len(in_specs), len(out_specs), len(scratch_shapes)
    p_in, p_out = [len(p.operands) for p in plans], [len(p.out_shape) for p in plans]
    deps = tuple(deps) + tuple(_SCHEDULE_BEHIND)
    _SCHEDULE_BEHIND.clear()
    all_aliases = dict(aliases or {})
    in_off, out_off = n_in, n_out
    for p, ni, no in zip(plans, p_in, p_out):
        all_aliases.update({in_off + i: out_off + o for i, o in p.aliases.items()})
        in_off, out_off = in_off + ni, out_off + no

    def body(*refs):
        pos, pins, pouts = n_in, [], []
        for ni in p_in:
            pins.append(refs[pos:pos + ni])
            pos += ni
        pos += len(deps)
        main_out = refs[pos:pos + n_out]
        pos += n_out
        for no in p_out:
            pouts.append(refs[pos:pos + no])
            pos += no
        main_scr = refs[pos:pos + n_scr]
        pos += n_scr
        if plans:
            ids = [pl.program_id(ax) for ax in range(len(grid))]
            first = functools.reduce(jnp.logical_and, [i == 0 for i in ids])
            last = functools.reduce(jnp.logical_and, [i == g - 1 for i, g in zip(ids, grid)])
            copies = [p.copies(pins[k], pouts[k], refs[pos + 2 * k], refs[pos + 2 * k + 1]) for k, p in enumerate(plans)]

            @pl.when(first)
            def _():
                for sends, _ in copies:
                    for cp in sends:
                        cp.start()

        kern(*refs[:n_in], *main_out, *main_scr)
        if plans:
            @pl.when(last)
            def _():
                for sends, recvs in copies:
                    for cp in recvs:
                        cp.wait_recv()
                    for cp in sends:
                        cp.wait_send()

    res = pl.pallas_call(body, grid=grid, in_specs=list(in_specs) + [ANY] * (sum(p_in) + len(deps)),
                         out_specs=list(out_specs) + [ANY] * sum(p_out), out_shape=list(out_shape) + [s for p in plans for s in p.out_shape],
                         scratch_shapes=list(scratch_shapes) + [s for p in plans for s in _dma_sems(p.n_sems)],
                         input_output_aliases=all_aliases, name=name, compiler_params=cparams or _cparams())(
        *operands, *[a for p in plans for a in p.operands], *deps)
    pos = n_out
    for p, no in zip(plans, p_out):
        p.results = list(res[pos:pos + no])
        pos += no
    return list(res[:n_out])


def _wide(v):
    return v.astype(F32) if v.dtype == BF16 else v


def stage(name, fn, grid, ins, outs):
    n_in = len(ins)

    def kern(*refs):
        res = fn(*[_wide(r[...]) for r in refs[:n_in]])
        for r, v in zip(refs[n_in:], res):
            r[...] = v.astype(r.dtype)

    return hosted_call(kern, name=name, grid=grid, in_specs=[s for _, s in ins], out_specs=[s for _, s in outs],
                       out_shape=[sd for sd, _ in outs], operands=[a for a, _ in ins])


def stage_bwd(name, fn, grid, ins, cts, diffs, plans=()):
    n_in, n_ct = len(ins), len(cts)
    didx = [i for i, d in enumerate(diffs) if d is not None]
    opts = {i: (diffs[i][3] if len(diffs[i]) > 3 else {}) for i in didx if diffs[i][0] == "row"}
    adds = [(i, opts[i]["add"]) for i in opts if "add" in opts[i]]
    intos = [(i, opts[i]["into"]) for i in opts if "into" in opts[i]]
    n_add, n_into = len(adds), len(intos)
    add_pos = {i: n_in + n_ct + k for k, (i, _) in enumerate(adds)}
    n_extra = n_in + n_ct + n_add + n_into

    def kern(*refs):
        vals = [_wide(r[...]) for r in refs[:n_in]]

        def f(*dv):
            full = list(vals)
            for i, v in zip(didx, dv):
                full[i] = v
            return fn(*full)

        _, vjp = jax.vjp(f, *[vals[i].astype(F32) for i in didx])
        gs = vjp(tuple(c[...].astype(F32) for c in refs[n_in:n_in + n_ct]))
        for o_ref, i, g in zip(refs[n_extra:], didx, gs):
            if diffs[i][0] == "row":
                if i in add_pos:
                    g = g + refs[add_pos[i]][...].astype(F32)
                o_ref[...] = g.astype(o_ref.dtype)
            else:
                first = functools.reduce(jnp.logical_and, [pl.program_id(ax) == 0 for ax in diffs[i][1]])

                @pl.when(first)
                def _():
                    o_ref[...] = g

                @pl.when(jnp.logical_not(first))
                def _():
                    o_ref[...] += g

    out_shape, out_specs = [], []
    for i in didx:
        if diffs[i][0] == "row":
            out_shape.append(diffs[i][1])
            out_specs.append(diffs[i][2])
        else:
            out_shape.append(SDS(ins[i][0].shape, F32))
            out_specs.append(ins[i][1])
    aliases = {n_in + n_ct + n_add + k: didx.index(i) for k, (i, _) in enumerate(intos)}
    in_specs = [s for _, s in ins] + [s for _, s in cts] + [s for _, (_, s) in adds] + [ANY] * n_into
    operands = [a for a, _ in ins] + [a for a, _ in cts] + [a for _, (a, _) in adds] + [a for _, a in intos]
    return hosted_call(kern, name=name, grid=grid, in_specs=in_specs, out_specs=out_specs, out_shape=out_shape, operands=operands,
                       aliases=aliases, plans=plans)


def rspec(tl, w, cb=0):
    return pl.BlockSpec((tl, w), lambda i: (i, cb))


def cspec(shape):
    return pl.BlockSpec(shape, lambda i: (0,) * len(shape))


def row_fwd(name, fn, rows, tl, row_ins, consts, outs):
    ins = [(a, rspec(tl, w, cb)) for a, w, cb in row_ins] + [(a, cspec(a.shape)) for a in consts]
    return stage(name, fn, (rows // tl,), ins, [(SDS((rows, w), dt), rspec(tl, w)) for w, dt in outs])


def row_bwd(name, fn, rows, tl, row_ins, consts, cts, row_diff, const_diff, plans=()):
    ins = [(a, rspec(tl, w, cb)) for a, w, cb in row_ins] + [(a, cspec(a.shape)) for a in consts]
    diffs = []
    for (a, w, cb), d in zip(row_ins, row_diff):
        if not d:
            diffs.append(None)
            continue
        d = d if isinstance(d, dict) else {}
        opts = {}
        if "add" in d:
            opts["add"] = (d["add"][0], rspec(tl, d["add"][1], d["add"][2]))
        if d.get("into") is not None:
            opts["into"] = d["into"]
        diffs.append(("row", SDS((rows, d.get("cols", w)), d.get("dtype", F32)), rspec(tl, w, d.get("cb", 0)), opts))
    diffs += [("acc", (0,)) if d else None for d in const_diff]
    return stage_bwd(name, fn, (rows // tl,), ins, [(a, rspec(tl, w, cb)) for a, w, cb in cts], diffs, plans=plans)


MATMUL_VMEM = 36 * 2**20
ADAMW_VMEM = 28 * 2**20


class Sharded:
    def __init__(self, arr, kind, roff, rows):
        self.arr, self.kind, self.roff, self.rows, self.n = arr, kind, roff, rows, arr.shape[2]
        self.shape = (rows, 4 * self.n) if kind == "col" else (4 * rows, self.n)

    def fits(self, t0, t1):
        return self.roff % t0 == 0 and self.rows % t0 == 0 and self.n % t1 == 0

    def spec(self, t0, t1, bidx):
        assert self.fits(t0, t1), (self.kind, self.roff, self.rows, self.n, t0, t1)
        r0 = self.roff // t0
        if self.kind == "col":
            per = self.n // t1
            return pl.BlockSpec((None, t0, t1), lambda *g: (bidx(*g)[1] // per, r0 + bidx(*g)[0], bidx(*g)[1] % per))
        per = self.rows // t0
        return pl.BlockSpec((None, t0, t1), lambda *g: (bidx(*g)[0] // per, r0 + bidx(*g)[0] % per, bidx(*g)[1]))


def matmul(name, a, b, mode, out_dtype=BF16, add=None, out=None, into=None, plans=()):
    if mode == "tn":
        k_dim, m = a.shape
    else:
        m, k_dim = a.shape
    n = b.shape[0] if mode == "nt" else b.shape[1]
    b_fit = b.fits if isinstance(b, Sharded) else (lambda t0, t1: True)
    o_fit = out.fits if out is not None else (lambda t0, t1: True)
    a_bytes, b_bytes = jnp.dtype(a.dtype).itemsize, jnp.dtype(b.arr.dtype if isinstance(b, Sharded) else b.dtype).itemsize
    o_bytes = jnp.dtype(out_dtype if out is None else out.arr.dtype).itemsize

    def vmem(tm, tn, tk):
        return 2 * (tm * tk * a_bytes + tk * tn * b_bytes + tm * tn * (o_bytes + (4 if add is not None else 0))) + 4 * tm * tn * (1 + (tk < k_dim))

    tiles = [(tm, tn, tk) for tm in (2048, 1024, 512, 256, 128) for tn in (1024, 768, 512, 384, 256, 128)
             for tk in sorted({k_dim, 1024, 768, 512, 384, 256, 128})
             if m % tm == 0 and n % tn == 0 and k_dim % tk == 0 and (b_fit(tn, tk) if mode == "nt" else b_fit(tk, tn)) and o_fit(tm, tn)
             and vmem(tm, tn, tk) <= MATMUL_VMEM]
    tm, tn, tk = max(tiles, key=lambda t: (t[2] == k_dim, t[0] * t[1] * t[2], t[0] * t[1]))
    nk = k_dim // tk
    a_spec = pl.BlockSpec((tk, tm), lambda i, j, k: (k, i)) if mode == "tn" else pl.BlockSpec((tm, tk), lambda i, j, k: (i, k))
    if isinstance(b, Sharded):
        b_spec = b.spec(tn, tk, lambda i, j, k: (j, k)) if mode == "nt" else b.spec(tk, tn, lambda i, j, k: (k, j))
        b = b.arr
    else:
        b_spec = pl.BlockSpec((tn, tk), lambda i, j, k: (j, k)) if mode == "nt" else pl.BlockSpec((tk, tn), lambda i, j, k: (k, j))
    o_spec = pl.BlockSpec((tm, tn), lambda i, j, k: (i, j))
    out_spec, out_shape = (o_spec, SDS((m, n), out_dtype)) if out is None else (out.spec(tm, tn, lambda i, j, k: (i, j)), out.arr)
    ca, cb = {"nn": (1, 0), "nt": (1, 1), "tn": (0, 0)}[mode]
    n_in = 2 + (add is not None)

    def finish(refs, o_ref, r):
        if add is not None:
            r = r + refs[2][...]
        o_ref[...] = r.astype(o_ref.dtype)

    def kern_whole(*refs):
        finish(refs, refs[-1], _dg(refs[0][...], refs[1][...], ca, cb))

    def kern_cut(*refs):
        o_ref, acc = refs[-2], refs[-1]
        k = pl.program_id(2)

        @pl.when(k == 0)
        def _():
            acc[...] = jnp.zeros_like(acc)

        acc[...] += _dg(refs[0][...], refs[1][...], ca, cb)

        @pl.when(k == nk - 1)
        def _():
            finish(refs, o_ref, acc[...])

    ins, specs = [a, b], [a_spec, b_spec]
    if add is not None:
        ins.append(add)
        specs.append(o_spec)
    if into is not None:
        ins.append(into)
        specs.append(ANY)
    return hosted_call(kern_whole if nk == 1 else kern_cut, name=name, grid=(m // tm, n // tn, nk), in_specs=specs, out_specs=[out_spec],
                       out_shape=[out_shape], operands=ins, scratch_shapes=[] if nk == 1 else [pltpu.VMEM((tm, tn), F32)],
                       aliases={} if into is None else {n_in: 0}, plans=plans)[0]


SCAN_UNROLL = 8


def _cmul(ar, ai, br, bi):
    return ar * br - ai * bi, ar * bi + ai * br


def _sub_shift(x, down):
    row = lax.broadcasted_iota(jnp.int32, x.shape, 0)
    if down:
        return jnp.where(row == 0, 0.0, pltpu.roll(x, 1, 0))
    return jnp.where(row == SEG - 1, 0.0, pltpu.roll(x, SEG - 1, 0))


def _pow_seg_len(ar, ai):
    for _ in range(int(math.log2(SEG_LEN))):
        ar, ai = _cmul(ar, ai, ar, ai)
    return ar, ai


def _scan_in_place(sr, si, a_re, a_im):
    lanes = sr.shape[1]
    ar = jnp.broadcast_to(a_re, (SEG, lanes))
    ai = jnp.broadcast_to(a_im, (SEG, lanes))
    zero = jnp.zeros((SEG, lanes), F32)

    def local(i, carry):
        rows = pl.ds(pl.multiple_of(i * SEG, SEG), SEG)
        mr, mi = _cmul(ar, ai, carry[0], carry[1])
        nr, ni = mr + sr[rows, :], mi + si[rows, :]
        sr[rows, :] = nr
        si[rows, :] = ni
        return nr, ni

    fr, fi = lax.fori_loop(0, SEG_LEN, local, (zero, zero), unroll=SCAN_UNROLL)
    pr, pi = _pow_seg_len(ar, ai)
    ir, ii = zero, zero
    for _ in range(SEG - 1):
        mr, mi = _cmul(pr, pi, ir, ii)
        ir, ii = _sub_shift(mr + fr, True), _sub_shift(mi + fi, True)

    def carry_in(i, pw):
        rows = pl.ds(pl.multiple_of(i * SEG, SEG), SEG)
        cr, ci = _cmul(pw[0], pw[1], ir, ii)
        sr[rows, :] += cr
        si[rows, :] += ci
        return _cmul(pw[0], pw[1], ar, ai)

    lax.fori_loop(0, SEG_LEN, carry_in, (ar, ai), unroll=SCAN_UNROLL)


S5_LANES = 8 * SP
S5_BLOCKS = SN // S5_LANES


def _s5_specs(n=1):
    u_spec = pl.BlockSpec((L, n * 8 * SC), lambda j: (0, j))
    s_spec = pl.BlockSpec((L, n * S5_LANES), lambda j: (0, j))
    wb_spec = pl.BlockSpec((n * S5_LANES, 8 * SC), lambda j: (j, 0))
    wc_spec = pl.BlockSpec((n * 8 * SC, S5_LANES), lambda j: (j, 0))
    a_spec = pl.BlockSpec((1, n * S5_LANES), lambda j: (0, j))
    d_spec = pl.BlockSpec((1, n * 8 * SC), lambda j: (0, j))
    return u_spec, s_spec, wb_spec, wc_spec, a_spec, d_spec


S5_FWD_BLOCKS = 2


def s5_forward(proj, wb_re, wb_im, wc_re, wc_im, a_re, a_im, d, plans=()):
    n = S5_FWD_BLOCKS
    cols, lanes = 8 * SC, S5_LANES

    def kern(u_ref, wbr, wbi, wcr, wci, ar, ai, d_ref, sr_out, si_out, g_ref, sr, si):
        u = _wide(u_ref[...])
        for b in range(n):
            sr[:, b * lanes:(b + 1) * lanes], si[:, b * lanes:(b + 1) * lanes] = fn_s5_bu(
                u[:, b * cols:(b + 1) * cols], wbr[b * lanes:(b + 1) * lanes, :], wbi[b * lanes:(b + 1) * lanes, :])
        _scan_in_place(sr, si, ar[...], ai[...])
        for b in range(n):
            g_ref[:, b * cols:(b + 1) * cols] = fn_s5_out(
                sr[:, b * lanes:(b + 1) * lanes], si[:, b * lanes:(b + 1) * lanes], u[:, b * cols:(b + 1) * cols],
                d_ref[:, b * cols:(b + 1) * cols], wcr[b * cols:(b + 1) * cols, :], wci[b * cols:(b + 1) * cols, :])[0].astype(g_ref.dtype)
        sr_out[...] = sr[...].astype(sr_out.dtype)
        si_out[...] = si[...].astype(si_out.dtype)

    u_spec, s_spec, wb_spec, wc_spec, a_spec, d_spec = _s5_specs(n)
    return hosted_call(kern, name="s5_forward", grid=(S5_BLOCKS // n,), in_specs=[u_spec, wb_spec, wb_spec, wc_spec, wc_spec, a_spec, a_spec, d_spec],
                       out_specs=[s_spec, s_spec, u_spec], out_shape=[SDS((L, SN), BF16)] * 2 + [SDS((L, PW), BF16)],
                       operands=[proj, wb_re, wb_im, wc_re, wc_im, a_re, a_im, d], scratch_shapes=[pltpu.VMEM((L, n * lanes), F32)] * 2,
                       plans=plans)


def _adjoint_scan_in_place(lr, li, sr, si, a_re, a_im):
    lanes = lr.shape[1]
    ar = jnp.broadcast_to(a_re, (SEG, lanes))
    ai = -jnp.broadcast_to(a_im, (SEG, lanes))
    zero = jnp.zeros((SEG, lanes), F32)

    def local(k, carry):
        i = SEG_LEN - 1 - k
        rows = pl.ds(pl.multiple_of(i * SEG, SEG), SEG)
        mr, mi = _cmul(ar, ai, carry[0], carry[1])
        nr, ni = mr + lr[rows, :], mi + li[rows, :]
        lr[rows, :] = nr
        li[rows, :] = ni
        return nr, ni

    fr, fi = lax.fori_loop(0, SEG_LEN, local, (zero, zero), unroll=SCAN_UNROLL)
    pr, pi = _pow_seg_len(ar, ai)
    ir, ii = zero, zero
    for _ in range(SEG - 1):
        mr, mi = _cmul(pr, pi, ir, ii)
        ir, ii = _sub_shift(mr + fr, False), _sub_shift(mi + fi, False)

    def fix(rows, pw):
        cr, ci = _cmul(pw[0], pw[1], ir, ii)
        tr, ti = lr[rows, :] + cr, li[rows, :] + ci
        lr[rows, :] = tr
        li[rows, :] = ti
        return tr, ti

    def grad_a(tr, ti, spr, spi, acc):
        return acc[0] + tr * spr + ti * spi, acc[1] + ti * spr - tr * spi

    def carry_in(k, c):
        i = SEG_LEN - 1 - k
        rows = pl.ds(pl.multiple_of(i * SEG, SEG), SEG)
        prev = pl.ds(pl.multiple_of((i - 1) * SEG, SEG), SEG)
        tr, ti = fix(rows, (c[0], c[1]))
        acc = grad_a(tr, ti, sr[prev, :], si[prev, :], (c[2], c[3]))
        nr, ni = _cmul(c[0], c[1], ar, ai)
        return nr, ni, acc[0], acc[1]

    pwr, pwi, accr, acci = lax.fori_loop(0, SEG_LEN - 1, carry_in, (ar, ai, zero, zero), unroll=5)
    tr, ti = fix(pl.ds(0, SEG), (pwr, pwi))
    last = pl.ds((SEG_LEN - 1) * SEG, SEG)
    accr, acci = grad_a(tr, ti, _sub_shift(sr[last, :], True), _sub_shift(si[last, :], True), (accr, acci))
    return jnp.sum(accr, axis=0, keepdims=True), jnp.sum(acci, axis=0, keepdims=True)


S5_BWD_VMEM = 58 * 2**20


def s5_backward(dg, proj, s_re, s_im, wb_re, wb_im, wc_re, wc_im, a_re, a_im, d, dproj, plans=()):
    def kern(dg_ref, u_ref, sr_in, si_in, wbr, wbi, wcr, wci, ar, ai, d_ref, _, du_ref, dd_ref, dwcr, dwci, dwbr, dwbi, dar, dai, lr, li, sr, si):
        u, d = _wide(u_ref[...]), d_ref[...]
        y = _dg(sr_in[...], wcr[...], 1, 1) - _dg(si_in[...], wci[...], 1, 1) + d * u
        dy, = jax.vjp(jax.nn.gelu, y)[1](_wide(dg_ref[...]))
        lr[...] = _dg(dy, wcr[...], 1, 0)
        li[...] = -_dg(dy, wci[...], 1, 0)
        dwcr[...] = _dg(dy, sr_in[...], 0, 0)
        dwci[...] = -_dg(dy, si_in[...], 0, 0)
        dd_ref[...] = jnp.sum(dy * u, axis=0, keepdims=True)
        sr[...], si[...] = _wide(sr_in[...]), _wide(si_in[...])
        dar[...], dai[...] = _adjoint_scan_in_place(lr, li, sr, si, ar[...], ai[...])
        _, vjp_in = jax.vjp(fn_s5_bu, u, wbr[...], wbi[...])
        du_in, dwbr[...], dwbi[...] = vjp_in((lr[...], li[...]))
        du_ref[...] = (dy * d + du_in).astype(du_ref.dtype)

    u_spec, s_spec, wb_spec, wc_spec, a_spec, d_spec = _s5_specs()
    outs = [(SDS(dproj.shape, dproj.dtype), u_spec), (SDS(d.shape, F32), d_spec), (SDS(wc_re.shape, F32), wc_spec), (SDS(wc_im.shape, F32), wc_spec),
            (SDS(wb_re.shape, F32), wb_spec), (SDS(wb_im.shape, F32), wb_spec), (SDS(a_re.shape, F32), a_spec), (SDS(a_im.shape, F32), a_spec)]
    return hosted_call(kern, name="s5_backward", grid=(S5_BLOCKS,),
                       in_specs=[u_spec, u_spec, s_spec, s_spec, wb_spec, wb_spec, wc_spec, wc_spec, a_spec, a_spec, d_spec, ANY],
                       out_specs=[sp for _, sp in outs], out_shape=[sd for sd, _ in outs], aliases={11: 0},
                       operands=[dg, proj, s_re, s_im, wb_re, wb_im, wc_re, wc_im, a_re, a_im, d, dproj],
                       scratch_shapes=[pltpu.VMEM((L, S5_LANES), F32)] * 4,
                       cparams=pltpu.CompilerParams(vmem_limit_bytes=S5_BWD_VMEM), plans=plans)


def fn_rms(x, g):
    return (rms(x, g),)


def fn_s5_disc(lre, lim, ls):
    step = jnp.exp(ls)
    e = jnp.exp(lre * step)
    a_re, a_im = e * jnp.cos(lim * step), e * jnp.sin(lim * step)
    den = lre * lre + lim * lim
    nr, ni = a_re - 1.0, a_im
    return a_re, a_im, (nr * lre + ni * lim) / den, (ni * lre - nr * lim) / den


def _group_mask(rows, cols, row_div, col_div):
    r = lax.broadcasted_iota(jnp.int32, (rows, cols), 0) // row_div % 8
    c = lax.broadcasted_iota(jnp.int32, (rows, cols), 1) // col_div
    return r == c


def _spread(x, mask):
    w = x.shape[1]
    copy = (lax.broadcasted_iota(jnp.int32, (w, 8 * w), 1) % w == lax.broadcasted_iota(jnp.int32, (w, 8 * w), 0)).astype(F32)
    return jnp.where(mask, jnp.dot(x, copy, precision=lax.Precision.HIGHEST, preferred_element_type=F32), 0.0)


def fn_s5_bmat(b_re, b_im, coef_re, coef_im):
    mask = _group_mask(b_re.shape[0], 8 * SC, SP, SC)
    return _spread(coef_re * b_re - coef_im * b_im, mask), _spread(coef_re * b_im + coef_im * b_re, mask)


def fn_s5_cmat(c_re, c_im):
    mask = _group_mask(c_re.shape[0], 8 * SP, SC, SP)
    return _spread(c_re, mask), _spread(c_im, mask)


def fn_s5_bu(u, wb_re, wb_im):
    return mm_nt(u, wb_re), mm_nt(u, wb_im)


def fn_s5_out(sr, si, u, d, wc_re, wc_im):
    y = mm_nt(sr, wc_re) - mm_nt(si, wc_im) + d * u
    return (jax.nn.gelu(y),)


def fn_merge_glu(z, mo, gate):
    yg = z[:, :PW] * jax.nn.sigmoid(z[:, PW:])
    return (jnp.concatenate([yg, mo], axis=1) * silu(gate),)


def fn_merge(prim, mo, gate):
    return (jnp.concatenate([prim, mo], axis=1) * silu(gate),)


def fn_mem_k(kv, g):
    return (jnp.concatenate([rms(kv[:, h * XHD:(h + 1) * XHD], g) for h in range(XH)], axis=1),)


def fn_mem_attn(xq, kn, v, g):
    outs = []
    for h in range(XH):
        sl = slice(h * XHD, (h + 1) * XHD)
        p = softmax_rows(mm_nt(rms(xq[:, sl], g), kn[:, sl]) * (XHD ** -0.5))
        outs.append(mm_nn(p, v[:, sl]))
    return (jnp.concatenate(outs, axis=1),)


def _half_rms(x, g):
    lo = lax.broadcasted_iota(jnp.int32, x.shape, 1) < ROPE
    x2 = x * x
    s_lo = jnp.sum(jnp.where(lo, x2, 0.0), axis=1, keepdims=True)
    s_hi = jnp.sum(jnp.where(lo, 0.0, x2), axis=1, keepdims=True)
    return x * lax.rsqrt(jnp.where(lo, s_lo, s_hi) / ROPE + EPS) * g


def _rope(x, cos2, sin_signed):
    first = lax.broadcasted_iota(jnp.int32, x.shape, 1) % ROPE < ROPE // 2
    return x * cos2 + jnp.where(first, lane_roll(x, 128 - ROPE // 2), lane_roll(x, ROPE // 2)) * sin_signed


def fn_mla_prep(q, kv, kr, cos2, sin_signed, qnn, knn, qrn, krn):
    lo = lax.broadcasted_iota(jnp.int32, kr.shape, 1) < ROPE
    kr_pad = jnp.where(lo, _rope(_half_rms(kr, krn), cos2, sin_signed), 0.0)
    qf, kf, vs = [], [], []
    for m in range(MH // 2):
        pair = _rope(_half_rms(q[:, MH * NOPE + 128 * m:MH * NOPE + 128 * (m + 1)], qrn), cos2, sin_signed)
        for h, rope_h in ((2 * m, pair), (2 * m + 1, lane_roll(pair, ROPE))):
            qf.append(jnp.concatenate([rms(q[:, NOPE * h:NOPE * (h + 1)], qnn), jnp.where(lo, rope_h, 0.0)], axis=1))
    for h in range(MH):
        kf.append(jnp.concatenate([rms(kv[:, 256 * h:256 * h + NOPE], knn), kr_pad], axis=1))
        vs.append(kv[:, 256 * h + NOPE:256 * (h + 1)])
    return jnp.stack(qf), jnp.stack(kf), jnp.stack(vs)


ATT_TQ = 512


def _attn_scores(q, kf):
    tq = q.shape[0]
    scale = (NOPE + ROPE) ** -0.5
    own = _dg(q, kf[-tq:], 1, 1) * scale
    own = jnp.where(lax.broadcasted_iota(jnp.int32, own.shape, 1) <= lax.broadcasted_iota(jnp.int32, own.shape, 0), own, jnp.finfo(F32).min)
    return own if kf.shape[0] == tq else jnp.concatenate([_dg(q, kf[:-tq], 1, 1) * scale, own], axis=1)


ATT_FWD_HEADS, ATT_BWD_HEADS = 4, 2


def _attn_specs(heads):
    q_spec = pl.BlockSpec((heads, ATT_TQ, 256), lambda h, i: (h, i, 0))
    k_spec = pl.BlockSpec((heads, L, 256), lambda h, i: (h, 0, 0))
    v_spec = pl.BlockSpec((heads, L, 128), lambda h, i: (h, 0, 0))
    o_spec = pl.BlockSpec((ATT_TQ, heads * 128), lambda h, i: (i, h))
    lse_spec = pl.BlockSpec((heads, ATT_TQ, 1), lambda h, i: (h, i, 0))
    return q_spec, k_spec, v_spec, o_spec, lse_spec


def _attn_branches(heads, body):
    i = pl.program_id(1)
    for t in range(L // ATT_TQ):
        @pl.when(i == t)
        def _(t=t):
            for hh in range(heads):
                body(hh, slice(hh * 128, (hh + 1) * 128), (t + 1) * ATT_TQ)


def causal_attn(qf, kf, vh):
    def kern(q_ref, k_ref, v_ref, o_ref, lse_ref):
        def body(hh, lanes, keys):
            s = _attn_scores(q_ref[hh], k_ref[hh, :keys, :])
            m = jnp.max(s, axis=-1, keepdims=True)
            e = jnp.exp(s - m)
            total = jnp.sum(e, axis=-1, keepdims=True)
            o_ref[:, lanes] = (_dg(e, v_ref[hh, :keys, :], 1, 0) / total).astype(o_ref.dtype)
            lse_ref[hh] = m + jnp.log(total)

        _attn_branches(ATT_FWD_HEADS, body)

    q_spec, k_spec, v_spec, o_spec, lse_spec = _attn_specs(ATT_FWD_HEADS)
    return pl.pallas_call(kern, grid=(MH // ATT_FWD_HEADS, L // ATT_TQ), in_specs=[q_spec, k_spec, v_spec], out_specs=[o_spec, lse_spec],
                          out_shape=[SDS((L, MH * VD), F32), SDS((MH, L, 1), F32)], name="l1_attn", compiler_params=_cparams())(qf, kf, vh)


def causal_attn_bwd(qf, kf, vh, out, lse, dout):
    scale = (NOPE + ROPE) ** -0.5

    def kern(q_ref, k_ref, v_ref, o_ref, lse_ref, do_ref, dq_ref, dk_ref, dv_ref):
        @pl.when(pl.program_id(1) == 0)
        def _():
            dk_ref[...] = jnp.zeros_like(dk_ref)
            dv_ref[...] = jnp.zeros_like(dv_ref)

        def body(hh, lanes, keys):
            q, k, v, do = q_ref[hh], k_ref[hh, :keys, :], v_ref[hh, :keys, :], do_ref[:, lanes]
            p = jnp.exp(_attn_scores(q, k) - lse_ref[hh])
            delta = jnp.sum(do * _wide(o_ref[:, lanes]), axis=-1, keepdims=True)
            dv_ref[hh, :keys, :] += _dg(p, do, 0, 0)
            ds = p * (_dg(do, v, 1, 1) - delta) * scale
            dq_ref[hh] = _dg(ds, k, 1, 0)
            dk_ref[hh, :keys, :] += _dg(ds, q, 0, 0)

        _attn_branches(ATT_BWD_HEADS, body)

    q_spec, k_spec, v_spec, o_spec, lse_spec = _attn_specs(ATT_BWD_HEADS)
    return pl.pallas_call(kern, grid=(MH // ATT_BWD_HEADS, L // ATT_TQ), in_specs=[q_spec, k_spec, v_spec, o_spec, lse_spec, o_spec],
                          out_specs=[q_spec, k_spec, v_spec], out_shape=[SDS(qf.shape, F32), SDS(kf.shape, F32), SDS(vh.shape, F32)],
                          name="l1_attn_bwd", compiler_params=_cparams())(qf, kf, vh, out, lse, dout)


def loss_and_grad(y, target, tl=512):
    def kern(y_ref, t_ref, dy_ref, loss_ref):
        d = y_ref[...] - t_ref[...]
        dy_ref[...] = d / D

        @pl.when(pl.program_id(0) == 0)
        def _():
            loss_ref[...] = jnp.zeros_like(loss_ref)

        loss_ref[...] += 0.5 * jnp.sum(jnp.sum(d * d, axis=1, keepdims=True), axis=0, keepdims=True) / D

    return pl.pallas_call(kern, grid=(L // tl,), in_specs=[rspec(tl, D), rspec(tl, D)], out_specs=[rspec(tl, D), cspec((1, 1))],
                          out_shape=[SDS((L, D), F32), SDS((1, 1), F32)], name="loss", compiler_params=_cparams())(y, target)


def adamw(name, w, g, m, v):
    rows, cols = w.shape
    block_row_bytes = 7 * 2 * 4 * max(cols, 128)
    tr = _row_tile(rows, min(2048, ADAMW_VMEM // block_row_bytes // 8 * 8), 8)

    def kern(w_ref, g_ref, m_ref, v_ref, d_ref, nm_ref, nv_ref):
        gg = g_ref[...]
        nm = ADAM_B1 * m_ref[...] + (1.0 - ADAM_B1) * gg
        nv = ADAM_B2 * v_ref[...] + (1.0 - ADAM_B2) * jnp.square(gg)
        m_hat = nm / (1.0 - ADAM_B1 ** ADAM_STEP)
        v_hat = nv / (1.0 - ADAM_B2 ** ADAM_STEP)
        d_ref[...] = -ADAM_LR * (m_hat / (jnp.sqrt(v_hat) + ADAM_EPS) + ADAM_WD * w_ref[...])
        nm_ref[...] = nm
        nv_ref[...] = nv

    spec = rspec(tr, cols)
    return hosted_call(kern, name=name, grid=(rows // tr,), in_specs=[spec] * 4, out_specs=[spec] * 3,
                       out_shape=[SDS((rows, cols), F32)] * 3, operands=[w, g, m, v])


def _row_tile(rows, cap=512, unit=16):
    return max(t for t in range(unit, cap + 1, unit) if rows % t == 0)


def _place():
    x, y, c = lax.axis_index("x"), lax.axis_index("y"), lax.axis_index("c")
    return x, y, c, [(1 - x, y), (x, 1 - y), (1 - x, 1 - y)]


def _row_chunks(rows, n, dtype):
    unit = 32 // jnp.dtype(dtype).itemsize
    base, extra = divmod(rows // unit, n)
    out, start = [], 0
    for k in range(n):
        size = (base + (k < extra)) * unit
        if size:
            out.append((start, size))
            start += size
    assert start == rows, (rows, unit)
    return out


PIECE_BYTES = 1 << 20


def _pieces(shapes_dtypes, rows_of):
    out = []
    for b, (shape, dtype) in enumerate(shapes_dtypes):
        rows = rows_of(shape)
        n = max(1, min(4, rows * shape[-1] * jnp.dtype(dtype).itemsize // PIECE_BYTES))
        out += [(b, st, sz) for st, sz in _row_chunks(rows, n, dtype)]
    return out


def all_gather_chips(name, shards):
    nb = len(shards)
    pieces = _pieces([(s.shape, s.dtype) for s in shards], lambda shape: shape[0] // 2)
    n = len(pieces)

    def body(*refs):
        x_refs, out_refs, send_sems, recv_sems = refs[:nb], refs[nb:2 * nb], refs[2 * nb], refs[2 * nb + 1]
        x, y, c, chips = _place()
        sibling = (x, y, 1 - c)
        mine = 2 * x + y

        def copy(sem, chip, cc, k, to, from_input=False):
            b, st, sz = pieces[k]
            rows_k = pl.ds(cc * (x_refs[b].shape[0] // 2) + st, sz)
            dst = out_refs[b].at[chip, rows_k, :]
            return pltpu.make_async_remote_copy(src_ref=x_refs[b].at[rows_k, :] if from_input else dst, dst_ref=dst,
                                                send_sem=send_sems.at[sem], recv_sem=recv_sems.at[sem], device_id=to, device_id_type=MESH_ID)

        order = [(k, j, 2 * cx + cy, (cx, cy, c)) for k in range(n) for j, (cx, cy) in enumerate(chips)]
        first = [copy(j * n + k, mine, c, k, to, from_input=True) for k, j, _, to in order]
        for cp in first:
            cp.start()
        passed = []
        for k, j, chip, _ in order:
            copy(j * n + k, chip, c, k, sibling).wait_recv()
            passed.append(copy((3 + j) * n + k, chip, c, k, sibling))
            passed[-1].start()
        for k, j, chip, _ in order:
            copy((3 + j) * n + k, chip, 1 - c, k, sibling).wait_recv()
        for cp in first + passed:
            cp.wait_send()

    return pl.pallas_call(body, in_specs=[ANY] * nb, out_specs=[ANY] * nb, out_shape=[SDS((4,) + s.shape, s.dtype) for s in shards],
                          scratch_shapes=_dma_sems(6 * n), name=name)(*shards)


def plan_gather_ici(shards):
    pieces = _pieces([(s.shape, s.dtype) for s in shards], lambda shape: shape[0] // 2)
    n = len(pieces)

    def copies(x_refs, out_refs, send_sems, recv_sems):
        x, y, c, chips = _place()
        mine = 2 * x + y

        def copy(j, k, chip, to, from_input):
            b, st, sz = pieces[k]
            rows_k = pl.ds(c * (x_refs[b].shape[0] // 2) + st, sz)
            dst = out_refs[b].at[chip, rows_k, :]
            return pltpu.make_async_remote_copy(src_ref=x_refs[b].at[rows_k, :] if from_input else dst, dst_ref=dst, send_sem=send_sems.at[j * n + k],
                                                recv_sem=recv_sems.at[j * n + k], device_id=to, device_id_type=MESH_ID)

        order = [(k, j, 2 * cx + cy, (cx, cy, c)) for k in range(n) for j, (cx, cy) in enumerate(chips)]
        return [copy(j, k, mine, to, True) for k, j, _, to in order], [copy(j, k, chip, to, False) for k, j, chip, to in order]

    return Plan(shards, [SDS((4,) + s.shape, s.dtype) for s in shards], {}, 3 * n, copies)


def plan_gather_pass(gathered):
    pieces = _pieces([(g.shape[1:], g.dtype) for g in gathered], lambda shape: shape[0] // 2)
    n = len(pieces)

    def copies(_, out_refs, send_sems, recv_sems):
        x, y, c, chips = _place()

        def copy(j, k, chip, cc):
            b, st, sz = pieces[k]
            rows_k = out_refs[b].at[chip, pl.ds(cc * (out_refs[b].shape[1] // 2) + st, sz), :]
            return pltpu.make_async_remote_copy(src_ref=rows_k, dst_ref=rows_k, send_sem=send_sems.at[j * n + k], recv_sem=recv_sems.at[j * n + k],
                                                device_id=(x, y, 1 - c), device_id_type=MESH_ID)

        order = [(k, j, 2 * cx + cy) for k in range(n) for j, (cx, cy) in enumerate(chips)]
        return [copy(j, k, chip, c) for k, j, chip in order], [copy(j, k, chip, 1 - c) for k, j, chip in order]

    return Plan(gathered, [SDS(g.shape, g.dtype) for g in gathered], {i: i for i in range(len(gathered))}, 3 * n, copies)


def plan_pair_exchange(gs):
    pieces = _pieces([(g.shape, g.dtype) for g in gs], lambda shape: shape[1] // 2)

    def copies(g_refs, got_refs, send_sems, recv_sems):
        x, y, c, _ = _place()
        swaps = [pltpu.make_async_remote_copy(src_ref=g_refs[b].at[:, pl.ds((1 - c) * (g_refs[b].shape[1] // 2) + st, sz), :],
                                              dst_ref=got_refs[b].at[:, pl.ds(st, sz), :], send_sem=send_sems.at[k], recv_sem=recv_sems.at[k],
                                              device_id=(x, y, 1 - c), device_id_type=MESH_ID)
                 for k, (b, st, sz) in enumerate(pieces)]
        return swaps, swaps

    return Plan(gs, [SDS((g.shape[0], g.shape[1] // 2, g.shape[2]), g.dtype) for g in gs], {}, len(pieces), copies)


def plan_chip_scatter(ps):
    pieces = _pieces([(p.shape, p.dtype) for p in ps], lambda shape: shape[1])
    n = len(pieces)

    def copies(p_refs, q_refs, send_sems, recv_sems):
        x, y, c, chips = _place()
        mine = 2 * x + y

        def copy(j, k, src_slot, dst_slot, to):
            b, st, sz = pieces[k]
            return pltpu.make_async_remote_copy(src_ref=p_refs[b].at[src_slot, pl.ds(st, sz), :], dst_ref=q_refs[b].at[dst_slot, pl.ds(st, sz), :],
                                                send_sem=send_sems.at[j * n + k], recv_sem=recv_sems.at[j * n + k], device_id=to,
                                                device_id_type=MESH_ID)

        order = [(k, j, 2 * cx + cy, (cx, cy, c)) for k in range(n) for j, (cx, cy) in enumerate(chips)]
        return [copy(j, k, chip, mine, to) for k, j, chip, to in order], [copy(j, k, mine, chip, to) for k, j, chip, to in order]

    return Plan(ps, [SDS(p.shape, p.dtype) for p in ps], {}, 3 * n, copies)


def plan_pair_join(bufs):
    pieces = _pieces([(b.shape, b.dtype) for b in bufs], lambda shape: shape[0] // 2)

    def copies(_, out_refs, send_sems, recv_sems):
        x, y, c, _ = _place()

        def copy(k, cc):
            b, st, sz = pieces[k]
            rows_k = out_refs[b].at[pl.ds(cc * (out_refs[b].shape[0] // 2) + st, sz), :]
            return pltpu.make_async_remote_copy(src_ref=rows_k, dst_ref=rows_k, send_sem=send_sems.at[k], recv_sem=recv_sems.at[k],
                                                device_id=(x, y, 1 - c), device_id_type=MESH_ID)

        return [copy(k, c) for k in range(len(pieces))], [copy(k, 1 - c) for k in range(len(pieces))]

    return Plan(bufs, [SDS(b.shape, b.dtype) for b in bufs], {i: i for i in range(len(bufs))}, len(pieces), copies)


def run_plan(name, plan):
    hosted_call(lambda: None, name=name, grid=(1,), in_specs=[], out_specs=[], out_shape=[], operands=[], plans=[plan])
    return plan.results


HBM = pl.BlockSpec(memory_space=pltpu.HBM)
SEMS = pl.BlockSpec(memory_space=pltpu.SEMAPHORE)
SPLIT_PARAMS = dict(has_side_effects=pltpu.SideEffectType.DATAFLOW_SIDE_EFFECTING)


def _plan_buffers(plan):
    in_place = {o: i for i, o in plan.aliases.items()}
    bufs = [pltpu.with_memory_space_constraint(a, pltpu.HBM) for a in plan.operands]
    where = []
    for o, sd in enumerate(plan.out_shape):
        if o in in_place:
            where.append(in_place[o])
        else:
            where.append(len(bufs))
            bufs.append(pltpu.with_memory_space_constraint(lax.empty(sd.shape, sd.dtype), pltpu.HBM))
    return bufs, where


def split_start(name, plans):
    layout = [_plan_buffers(p) for p in plans]
    counts = [len(b) for b, _ in layout]
    n_buf = sum(counts)

    def body(*refs):
        sems, token = refs[n_buf:n_buf + 2 * len(plans)], refs[-1]
        pos = 0
        for k, (p, (_, where)) in enumerate(zip(plans, layout)):
            mine = refs[pos:pos + counts[k]]
            pos += counts[k]
            sends, _ = p.copies(mine[:len(p.operands)], [mine[w] for w in where], sems[2 * k], sems[2 * k + 1])
            for cp in sends:
                cp.start()
        token[...] = jnp.zeros_like(token)

    bufs = [b for bs, _ in layout for b in bs]
    res = pl.pallas_call(
        body, name=name, in_specs=[HBM] * n_buf,
        out_specs=[SEMS] * (2 * len(plans)) + [HBM] * n_buf + [pl.BlockSpec(memory_space=pltpu.VMEM)],
        out_shape=[pltpu.SemaphoreType.DMA((p.n_sems,)) for p in plans for _ in range(2)] + [pltpu.HBM(b.shape, b.dtype) for b in bufs]
        + [SDS((8, 128), F32)],
        input_output_aliases={i: 2 * len(plans) + i for i in range(n_buf)}, compiler_params=pltpu.CompilerParams(**SPLIT_PARAMS))(*bufs)
    pos = 2 * len(plans)
    for k, p in enumerate(plans):
        p.in_flight = (res[2 * k], res[2 * k + 1], list(res[pos:pos + counts[k]]), layout[k][1])
        pos += counts[k]
    return res[-1]


def split_wait(name, plan, after):
    send_sems, recv_sems, bufs, where = plan.in_flight
    n_buf = len(bufs)

    def body(*refs):
        mine = refs[:n_buf]
        sends, recvs = plan.copies(mine[:len(plan.operands)], [mine[w] for w in where], refs[n_buf], refs[n_buf + 1])
        for cp in recvs:
            cp.wait_recv()
        for cp in sends:
            cp.wait_send()

    res = pl.pallas_call(body, name=name, in_specs=[HBM] * n_buf + [SEMS, SEMS, ANY], out_specs=[HBM] * n_buf,
                         out_shape=[pltpu.HBM(b.shape, b.dtype) for b in bufs], input_output_aliases={i: i for i in range(n_buf)},
                         compiler_params=pltpu.CompilerParams(**SPLIT_PARAMS))(*bufs, send_sems, recv_sems, after)
    plan.results = [res[w] for w in where]
    return plan.results


def pair_add(name, g, got, place):
    slots, rows, cols = g.shape
    half = rows // 2
    tr = _row_tile(half)
    nb = half // tr

    def kern(_, g_ref, t_ref, o_ref):
        o_ref[...] = (g_ref[...].astype(F32) + t_ref[...].astype(F32)).astype(o_ref.dtype)

    blk = pl.BlockSpec((None, tr, cols), lambda s, i, p: (s, i, 0))
    grid_spec = pltpu.PrefetchScalarGridSpec(
        num_scalar_prefetch=1, grid=(slots, nb),
        in_specs=[pl.BlockSpec((None, tr, cols), lambda s, i, p: (s, p[1] * nb + i, 0)), blk], out_specs=blk)
    return pl.pallas_call(kern, grid_spec=grid_spec, out_shape=SDS((slots, half, cols), g.dtype), name=name,
                          compiler_params=_cparams())(place, g, got)


def chip_add(name, p, q, place):
    slots, half, cols = p.shape
    tr = _row_tile(half)
    nb = half // tr

    def kern(_, p_ref, q1, q2, q3, o_ref):
        o_ref[...] = p_ref[...].astype(F32) + q1[...].astype(F32) + q2[...].astype(F32) + q3[...].astype(F32)

    def slot(k):
        return pl.BlockSpec((None, tr, cols), lambda i, pr: ((pr[0] + k) % slots, i, 0))

    grid_spec = pltpu.PrefetchScalarGridSpec(
        num_scalar_prefetch=1, grid=(nb,), in_specs=[slot(0), slot(1), slot(2), slot(3)],
        out_specs=pl.BlockSpec((tr, cols), lambda i, pr: (pr[1] * nb + i, 0)))
    return pl.pallas_call(kern, grid_spec=grid_spec, out_shape=SDS((2 * half, cols), F32), name=name,
                          compiler_params=_cparams())(place, p, q, q, q)


def pair_adds(tag, gs, gots, place):
    return [pair_add(f"{tag}_pair_add_{i}", g, got, place) for i, (g, got) in enumerate(zip(gs, gots))]


def chip_adds(tag, pairs, qs, place):
    return [chip_add(f"{tag}_chip_add_{i}", p, q, place) for i, (p, q) in enumerate(zip(pairs, qs))]


def reduce_scatter_chips(tag, gs, place):
    pairs = pair_adds(tag, gs, run_plan(tag + "_pair_exchange", plan_pair_exchange(gs)), place)
    return run_plan(tag + "_pair_join", plan_pair_join(chip_adds(tag, pairs, run_plan(tag + "_chip_scatter", plan_chip_scatter(pairs)), place)))


BIG = [("w_out", (2, 512, 1024)), ("w_mem_kv", (2, 256, 1024)), ("s5_w_in", (1, 1024, 1024)), ("s5_w_glu", (1, 1536, 768)),
       ("mla_w_in", (1, 1024, 848)), ("mla_w_uq", (1, 512, 576)), ("mla_w_ukv", (1, 256, 768))]
SHARDED_SMALL = [("mla_q_lora_norm", (1, 128)), ("mla_kv_lora_norm", (1, 64))]
SMALL = [("ln_gain", (2, 1024)), ("mem_norm", (2, 1024)), ("xq_norm", (2, 128)), ("xk_norm", (2, 128)),
         ("s5_lambda_re", (1, 96, 64)), ("s5_lambda_im", (1, 96, 64)), ("s5_log_step", (1, 96)),
         ("s5_b_re", (1, 96, 64, 16)), ("s5_b_im", (1, 96, 64, 16)), ("s5_c_re", (1, 96, 16, 64)), ("s5_c_im", (1, 96, 16, 64)),
         ("s5_d", (1, 1536)), ("mla_q_nope_norm", (1, 128)), ("mla_k_nope_norm", (1, 128)), ("mla_q_rope_norm", (1, 64)),
         ("mla_k_rope_norm", (1, 64))]
WEIGHT_ORDER = ["ln_gain", "w_out", "mem_norm", "w_mem_kv", "xq_norm", "xk_norm", "s5_w_in", "s5_lambda_re", "s5_lambda_im",
                "s5_log_step", "s5_b_re", "s5_b_im", "s5_c_re", "s5_c_im", "s5_d", "s5_w_glu", "mla_w_in", "mla_q_lora_norm",
                "mla_kv_lora_norm", "mla_w_uq", "mla_w_ukv", "mla_q_nope_norm", "mla_k_nope_norm", "mla_q_rope_norm", "mla_k_rope_norm"]
MINOR_LAST = {"mla_w_in": (0, 2, 1), "mla_w_uq": (0, 2, 1), "s5_b_re": (0, 2, 3, 1), "s5_b_im": (0, 2, 3, 1),
              "s5_c_re": (0, 2, 3, 1), "s5_c_im": (0, 2, 3, 1)}
SMALL_FULL = SMALL + [(n, (1, 4 * s[1])) for n, s in SHARDED_SMALL]
N_SMALL = sum(math.prod(s) for _, s in SMALL_FULL)
SMALL_ROWS, SMALL_LANES = 128, 1024

PAIR_OUT, PAIR_MKV, PAIR_ROWS = 0, 512, 768


def stack_shards(w, dtype):
    pairs = [jnp.concatenate([w["w_out"][l], w["w_mem_kv"][l]], axis=0).astype(dtype) for l in range(2)]
    return ([w["s5_w_in"][0].astype(dtype)], [pairs[0], w["s5_w_glu"][0].astype(dtype)],
            [pairs[1], w["mla_w_ukv"][0].astype(dtype), jnp.pad(w["mla_w_in"][0].T.astype(dtype), ((0, IN1T_ROWS - MLA_IN // 4), (0, 0))),
             w["mla_w_uq"][0].T.astype(dtype)])


def pair_views(pair):
    return {"w_out": Sharded(pair, "row", PAIR_OUT, 512), "w_mem_kv": Sharded(pair, "row", PAIR_MKV, 256)}


def grad_views():
    pair = SDS((4, PAIR_ROWS, 1024), BF16)
    return {"w_out": Sharded(pair, "row", PAIR_OUT, 512), "w_mem_kv": Sharded(pair, "row", PAIR_MKV, 256),
            "s5_w_in": Sharded(SDS((4, 1024, 1024), BF16), "col", 0, 1024), "s5_w_glu": Sharded(SDS((4, 1536, 768), BF16), "col", 0, 1536),
            "mla_w_ukv": Sharded(SDS((4, 256, 768), BF16), "col", 0, 256)}


IN1T_ROWS = 864


def in1_rows_permute(wt):
    o1, o2, o3, o4 = QL, QL + KVL, QL + KVL + ROPE, QL + KVL + ROPE + XQW
    return jnp.concatenate([wt[o4:], wt[:o1], wt[o3:o4], wt[o1:o2], wt[o2:o3], jnp.zeros((MLA_IN_P - MLA_IN, wt.shape[1]), wt.dtype)], axis=0)


def in1_rows_unpermute(d):
    return jnp.concatenate([d[2048:2560], d[3072:3328], d[3328:3392], d[2560:3072], d[:2048]], axis=0)


def uq_rows_permute(wt):
    w3 = wt.reshape(MH, NOPE + ROPE, wt.shape[1])
    return jnp.concatenate([w3[:, :NOPE].reshape(MH * NOPE, wt.shape[1]), w3[:, NOPE:].reshape(MH * ROPE, wt.shape[1])], axis=0)


def uq_rows_unpermute(d):
    dn = d[:MH * NOPE].reshape(MH, NOPE, d.shape[1])
    dr = d[MH * NOPE:].reshape(MH, ROPE, d.shape[1])
    return jnp.concatenate([dn, dr], axis=1).reshape(MH * (NOPE + ROPE), d.shape[1])


def time_permute(a):
    return a.reshape(SEG, SEG_LEN, a.shape[-1]).transpose(1, 0, 2).reshape(L, a.shape[-1])


def time_unpermute(a):
    return a.reshape(SEG_LEN, SEG, a.shape[-1]).transpose(1, 0, 2).reshape(L, a.shape[-1])


def mem_branch_fwd(tag, mem, mem_norm, w_mem_kv, xk_norm):
    mn = row_fwd(tag + "_mem_rms", fn_rms, ML, ML, [(mem, D, 0)], [mem_norm], [(D, BF16)])[0]
    kv = matmul(tag + "_mem_kv", mn, w_mem_kv, "nn", F32)
    kn = row_fwd(tag + "_mem_knorm", fn_mem_k, ML, ML, [(kv, XQW, 0)], [xk_norm], [(XQW, F32)])[0]
    return mn, kv, kn


def mem_branch_bwd(tag, mem, mem_norm, w_mem_kv, xk_norm, mn, kv, dkn, dv, g_view, g_wide):
    dk, dxk = row_bwd(tag + "_mem_knorm_bwd", fn_mem_k, ML, ML, [(kv, XQW, 0)], [xk_norm], [(dkn, XQW, 0)], [True], [True])
    dkv = jnp.concatenate([dk, dv], axis=1)
    dmn = matmul(tag + "_mem_kv_dx", dkv, w_mem_kv, "nt", F32)
    g_wide = matmul(tag + "_mem_kv_dw", mn, dkv, "tn", out=g_view, into=g_wide)
    dmem_norm = row_bwd(tag + "_mem_rms_bwd", fn_rms, ML, ML, [(mem, D, 0)], [mem_norm], [(dmn, D, 0)], [False], [True])[0]
    return g_wide, dmem_norm, dxk


def mem_attn_fwd(tag, proj, cb, kn, kv, xq_norm):
    return row_fwd(tag + "_mem_attn", fn_mem_attn, L, ROW_TILE, [(proj, XQW, cb)], [kn, kv[:, XQW:], xq_norm], [(XQW, F32)])[0]


def mem_attn_bwd(tag, proj, cb, kn, kv, xq_norm, dmo, dproj):
    place = {"cols": proj.shape[1], "cb": cb, "into": dproj, "dtype": dproj.dtype}
    return row_bwd(tag + "_mem_attn_bwd", fn_mem_attn, L, ROW_TILE, [(proj, XQW, cb)], [kn, kv[:, XQW:], xq_norm], [(dmo, XQW, 0)],
                   [place], [True, True, True])


def device_step(x, mem, positions, target, small, env, hooks=None):
    hooks = hooks or {}

    def plans_for(name):
        return hooks[("plans", name)](env) if ("plans", name) in hooks else ()

    def around(when, name, last=None):
        if (when, name) in hooks:
            hooks[(when, name)](env, last)

    g = {}
    gw = grad_views()
    ln, mem_norm, xq_norm, xk_norm = small["ln_gain"], small["mem_norm"], small["xq_norm"], small["xk_norm"]

    lre, lim = small["s5_lambda_re"][0], small["s5_lambda_im"][0]
    ls = small["s5_log_step"].reshape(SG, 1)
    one = pl.BlockSpec((SG, SP), lambda i: (0, 0))
    col = pl.BlockSpec((SG, 1), lambda i: (0, 0))
    disc_ins = [(lre, one), (lim, one), (ls, col)]
    a_re, a_im, coef_re, coef_im = stage("s5_disc", fn_s5_disc, (1,), disc_ins, [(SDS((SG, SP), F32), one)] * 4)
    b_re, b_im = small["s5_b_re"].reshape(SN, SC), small["s5_b_im"].reshape(SN, SC)
    c_re, c_im = small["s5_c_re"].reshape(PW, SP), small["s5_c_im"].reshape(PW, SP)
    bmat_rows = [(b_re, SC, 0), (b_im, SC, 0), (coef_re.reshape(SN, 1), 1, 0), (coef_im.reshape(SN, 1), 1, 0)]
    wb_re, wb_im = row_fwd("s5_bmat", fn_s5_bmat, SN, BMAT_TILE, bmat_rows, [], [(128, F32)] * 2)
    cmat_rows = [(c_re, SP, 0), (c_im, SP, 0)]
    wc_re, wc_im = row_fwd("s5_cmat", fn_s5_cmat, PW, CMAT_TILE, cmat_rows, [], [(512, F32)] * 2)
    a_re_v, a_im_v = a_re.reshape(1, SN), a_im.reshape(1, SN)
    s5_d = small["s5_d"]

    xp = time_permute(x)
    h0 = row_fwd("l0_rms", fn_rms, L, ROW_TILE, [(xp, D, 0)], [ln[0:1]], [(D, BF16)])[0]
    around("before", "l0_in", wb_re)
    w_in0 = Sharded(env["in0"], "col", 0, 1024)
    proj0 = matmul("l0_in", h0, w_in0, "nn")
    s_re, s_im, g0 = s5_forward(proj0, wb_re, wb_im, wc_re, wc_im, a_re_v, a_im_v, s5_d, plans=plans_for("s5_forward"))
    around("before", "l0_glu", g0)
    w_glu = Sharded(env["glu"], "col", 0, 1536)
    z0 = matmul("l0_glu", g0, w_glu, "nn", plans=plans_for("l0_glu"))
    around("after", "l0_glu", z0)
    w0 = dict(pair_views(env["pair0"]), s5_w_glu=w_glu)
    mn0, kv0, kn0 = mem_branch_fwd("l0", mem, mem_norm[0:1], w0["w_mem_kv"], xk_norm[0:1])
    mo0 = mem_attn_fwd("l0", proj0, 3, kn0, kv0, xq_norm[0:1])
    o0 = row_fwd("l0_merge", fn_merge_glu, L, ROW_TILE, [(z0, 2 * PW, 0), (mo0, XQW, 0), (proj0, BW, 1)], [], [(BW, BF16)])[0]
    around("before", "l0_out", o0)
    x1p = matmul("l0_out", o0, w0["w_out"], "nn", F32, add=xp, plans=plans_for("l0_out"))
    around("after", "l0_out", x1p)
    x1 = time_unpermute(x1p)

    w1 = dict(pair_views(env["pair1"]), mla_w_ukv=Sharded(env["ukv"], "col", 0, 256))
    w_in1, w_uq = env["w_in1"], env["w_uq"]
    h1 = row_fwd("l1_rms", fn_rms, L, ROW_TILE, [(x1, D, 0)], [ln[1:2]], [(D, BF16)])[0]
    proj1 = matmul("l1_in", h1, w_in1, "nt")
    qln, kvln = env["q_lora_norm"].reshape(1, QL), env["kv_lora_norm"].reshape(1, KVL)
    cqn = row_fwd("l1_q_lora_rms", fn_rms, L, ROW_TILE, [(proj1, QL, 4)], [qln], [(QL, BF16)])[0]
    ckvn = row_fwd("l1_kv_lora_rms", fn_rms, L, ROW_TILE, [(proj1, KVL, 12)], [kvln], [(KVL, BF16)])[0]
    q = matmul("l1_uq", cqn, w_uq, "nt")
    kv = matmul("l1_ukv", ckvn, w1["mla_w_ukv"], "nn")
    inv_freq = ROPE_THETA ** (-jnp.arange(ROPE // 2, dtype=F32) / (ROPE // 2))
    ang = positions.astype(F32)[:, None] * inv_freq
    cos2 = jnp.tile(jnp.cos(ang), (1, 4))
    sin_signed = jnp.tile(jnp.concatenate([-jnp.sin(ang), jnp.sin(ang)], axis=1), (1, 2))
    qnn, knn = small["mla_q_nope_norm"], small["mla_k_nope_norm"]
    qrn, krn = jnp.tile(small["mla_q_rope_norm"], (1, 2)), jnp.tile(small["mla_k_rope_norm"], (1, 2))
    tp = 256
    prep_ins = [(q, rspec(tp, MH * (NOPE + ROPE))), (kv, rspec(tp, MH * 256)), (proj1, rspec(tp, 128, 26)),
                (cos2, rspec(tp, 128)), (sin_signed, rspec(tp, 128))] + [(a, cspec((1, 128))) for a in (qnn, knn, qrn, krn)]
    hq_spec = pl.BlockSpec((MH, tp, 256), lambda i: (0, i, 0))
    hv_spec = pl.BlockSpec((MH, tp, 128), lambda i: (0, i, 0))
    qf, kf, vh = stage("l1_mla_prep", fn_mla_prep, (L // tp,), prep_ins,
                       [(SDS((MH, L, 256), BF16), hq_spec), (SDS((MH, L, 256), BF16), hq_spec), (SDS((MH, L, 128), BF16), hv_spec)])
    attn, attn_lse = causal_attn(qf, kf, vh)
    mn1, kv1, kn1 = mem_branch_fwd("l1", mem, mem_norm[1:2], w1["w_mem_kv"], xk_norm[1:2])
    mo1 = mem_attn_fwd("l1", proj1, 5, kn1, kv1, xq_norm[1:2])
    o1 = row_fwd("l1_merge", fn_merge, L, ROW_TILE, [(attn, PW, 0), (mo1, XQW, 0), (proj1, BW, 0)], [], [(BW, BF16)])[0]
    x2 = matmul("l1_out", o1, w1["w_out"], "nn", F32, add=x1)
    dx2, loss = loss_and_grad(x2, target)
    around("after", "loss", loss)

    do1 = matmul("l1_out_dx", dx2, w1["w_out"], "nt")
    g_pair1 = matmul("l1_out_dw", o1, dx2, "tn", out=gw["w_out"])
    dattn, dmo1, dproj1 = row_bwd("l1_merge_bwd", fn_merge, L, ROW_TILE, [(attn, PW, 0), (mo1, XQW, 0), (proj1, BW, 0)], [],
                                  [(do1, BW, 0)], [True, True, {"cols": MLA_IN_P, "cb": 0, "dtype": BF16}], [])
    dproj1, dkn1, dv1, dxqn1 = mem_attn_bwd("l1", proj1, 5, kn1, kv1, xq_norm[1:2], dmo1, dproj1)
    env["g_pair1"], dmem_norm1, dxk1 = mem_branch_bwd("l1", mem, mem_norm[1:2], w1["w_mem_kv"], xk_norm[1:2], mn1, kv1, dkn1, dv1,
                                                      gw["w_mem_kv"], g_pair1)
    dqf, dkf, dvh = causal_attn_bwd(qf, kf, vh, attn, attn_lse, dattn)
    prep_diffs = [("row", SDS((L, MH * (NOPE + ROPE)), BF16), rspec(tp, MH * (NOPE + ROPE))), ("row", SDS((L, MH * 256), BF16), rspec(tp, MH * 256)),
                  ("row", SDS((L, MLA_IN_P), BF16), rspec(tp, 128, 26), {"into": dproj1}), None, None] + [("acc", (0,))] * 4
    dq, dkv, dproj1, dqnn, dknn, dqrn, dkrn = stage_bwd("l1_mla_prep_bwd", fn_mla_prep, (L // tp,), prep_ins,
                                                        [(dqf, hq_spec), (dkf, hq_spec), (dvh, hv_spec)], prep_diffs)
    dcqn = matmul("l1_uq_dx", dq, w_uq, "nn")
    env["dw_uq"] = matmul("l1_uq_dw", dq, cqn, "tn")
    dckvn = matmul("l1_ukv_dx", dkv, w1["mla_w_ukv"], "nt")
    env["g_ukv"] = matmul("l1_ukv_dw", ckvn, dkv, "tn", out=gw["mla_w_ukv"])
    dproj1, dqln = row_bwd("l1_q_lora_rms_bwd", fn_rms, L, ROW_TILE, [(proj1, QL, 4)], [qln], [(dcqn, QL, 0)],
                           [{"cols": MLA_IN_P, "cb": 4, "into": dproj1, "dtype": BF16}], [True])
    dproj1, dkvln = row_bwd("l1_kv_lora_rms_bwd", fn_rms, L, ROW_TILE, [(proj1, KVL, 12)], [kvln], [(dckvn, KVL, 0)],
                            [{"cols": MLA_IN_P, "cb": 12, "into": dproj1, "dtype": BF16}], [True])
    env["dw_in1"] = matmul("l1_in_dw", dproj1, h1, "tn")
    dh1 = matmul("l1_in_dx", dproj1, w_in1, "nn", plans=plans_for("l1_in_dx"))
    around("after", "l1_in_dx", dh1)
    dx1, dln1 = row_bwd("l1_rms_bwd", fn_rms, L, ROW_TILE, [(x1, D, 0)], [ln[1:2]], [(dh1, D, 0)], [{"add": (dx2, D, 0)}], [True])
    dx1p = time_permute(dx1)

    do0 = matmul("l0_out_dx", dx1p, w0["w_out"], "nt", plans=plans_for("l0_out_dx"))
    g_pair0 = matmul("l0_out_dw", o0, dx1p, "tn", out=gw["w_out"])
    dz0, dmo0, dproj0 = row_bwd("l0_merge_bwd", fn_merge_glu, L, ROW_TILE, [(z0, 2 * PW, 0), (mo0, XQW, 0), (proj0, BW, 1)], [],
                                [(do0, BW, 0)], [{"dtype": BF16}, True, {"cols": 2 * BW, "cb": 1, "dtype": BF16}], [])
    dproj0, dkn0, dv0, dxqn0 = mem_attn_bwd("l0", proj0, 3, kn0, kv0, xq_norm[0:1], dmo0, dproj0)
    env["g_pair0"], dmem_norm0, dxk0 = mem_branch_bwd("l0", mem, mem_norm[0:1], w0["w_mem_kv"], xk_norm[0:1], mn0, kv0, dkn0, dv0,
                                                      gw["w_mem_kv"], g_pair0)
    env["g_glu"] = matmul("l0_glu_dw", g0, dz0, "tn", out=gw["s5_w_glu"], plans=plans_for("l0_glu_dw"))
    dg0 = matmul("l0_glu_dx", dz0, w0["s5_w_glu"], "nt", plans=plans_for("l0_glu_dx"))
    around("before", "s5_backward", dg0)
    dproj0, dd, dwc_re, dwc_im, dwb_re, dwb_im, da_re, da_im = s5_backward(dg0, proj0, s_re, s_im, wb_re, wb_im, wc_re, wc_im,
                                                                           a_re_v, a_im_v, s5_d, dproj0, plans=plans_for("s5_backward"))
    around("after", "s5_backward", dd)
    env["g_in0"] = matmul("l0_in_dw", h0, dproj0, "tn", out=gw["s5_w_in"], plans=plans_for("l0_in_dw"))
    dh0 = matmul("l0_in_dx", dproj0, w_in0, "nt", plans=plans_for("l0_in_dx"))
    around("after", "l0_in_dx", dh0)
    dxp, dln0 = row_bwd("l0_rms_bwd", fn_rms, L, ROW_TILE, [(xp, D, 0)], [ln[0:1]], [(dh0, D, 0)], [{"add": (dx1p, D, 0)}], [True])
    grad_x = time_unpermute(dxp)

    db_re, db_im, dcoef_re, dcoef_im = row_bwd("s5_bmat_bwd", fn_s5_bmat, SN, BMAT_TILE, bmat_rows, [], [(dwb_re, 128, 0), (dwb_im, 128, 0)],
                                               [True] * 4, [], plans=plans_for("s5_bmat_bwd"))
    dc_re, dc_im = row_bwd("s5_cmat_bwd", fn_s5_cmat, PW, CMAT_TILE, cmat_rows, [], [(dwc_re, 512, 0), (dwc_im, 512, 0)], [True] * 2, [],
                           plans=plans_for("s5_cmat_bwd"))
    disc_cts = [(da_re.reshape(SG, SP), one), (da_im.reshape(SG, SP), one), (dcoef_re.reshape(SG, SP), one), (dcoef_im.reshape(SG, SP), one)]
    dlre, dlim, dls = stage_bwd("s5_disc_bwd", fn_s5_disc, (1,), disc_ins, disc_cts, [("acc", (0,))] * 3)

    g["ln_gain"] = jnp.concatenate([dln0, dln1], axis=0)
    g["mem_norm"] = jnp.concatenate([dmem_norm0, dmem_norm1], axis=0)
    g["xq_norm"] = jnp.concatenate([dxqn0, dxqn1], axis=0)
    g["xk_norm"] = jnp.concatenate([dxk0, dxk1], axis=0)
    g["s5_lambda_re"], g["s5_lambda_im"], g["s5_log_step"] = dlre, dlim, dls
    g["s5_b_re"], g["s5_b_im"], g["s5_c_re"], g["s5_c_im"] = db_re, db_im, dc_re, dc_im
    g["s5_d"] = dd
    g["mla_q_lora_norm"], g["mla_kv_lora_norm"] = dqln, dkvln
    g["mla_q_nope_norm"], g["mla_k_nope_norm"] = dqnn, dknn
    g["mla_q_rope_norm"] = dqrn[:, :ROPE] + dqrn[:, ROPE:]
    g["mla_k_rope_norm"] = dkrn[:, :ROPE] + dkrn[:, ROPE:]
    return loss, grad_x, g


def kernel(x, mem, positions, ln_gain, w_out, mem_norm, w_mem_kv, xq_norm, xk_norm, s5_w_in, s5_lambda_re, s5_lambda_im, s5_log_step, s5_b_re, s5_b_im, s5_c_re, s5_c_im, s5_d, s5_w_glu, mla_w_in, mla_q_lora_norm, mla_kv_lora_norm, mla_w_uq, mla_w_ukv, mla_q_nope_norm, mla_k_nope_norm, mla_q_rope_norm, mla_k_rope_norm, loss_target, m_ln_gain, m_w_out, m_mem_norm, m_w_mem_kv, m_xq_norm, m_xk_norm, m_s5_w_in, m_s5_lambda_re, m_s5_lambda_im, m_s5_log_step, m_s5_b_re, m_s5_b_im, m_s5_c_re, m_s5_c_im, m_s5_d, m_s5_w_glu, m_mla_w_in, m_mla_q_lora_norm, m_mla_kv_lora_norm, m_mla_w_uq, m_mla_w_ukv, m_mla_q_nope_norm, m_mla_k_nope_norm, m_mla_q_rope_norm, m_mla_k_rope_norm, v_ln_gain, v_w_out, v_mem_norm, v_w_mem_kv, v_xq_norm, v_xk_norm, v_s5_w_in, v_s5_lambda_re, v_s5_lambda_im, v_s5_log_step, v_s5_b_re, v_s5_b_im, v_s5_c_re, v_s5_c_im, v_s5_d, v_s5_w_glu, v_mla_w_in, v_mla_q_lora_norm, v_mla_kv_lora_norm, v_mla_w_uq, v_mla_w_ukv, v_mla_q_nope_norm, v_mla_k_nope_norm, v_mla_q_rope_norm, v_mla_k_rope_norm):
    args = dict(locals())
    wts = {n: args[n] for n in WEIGHT_ORDER}
    mom = {n: args["m_" + n] for n in WEIGHT_ORDER}
    var = {n: args["v_" + n] for n in WEIGHT_ORDER}

    chip = 2 * lax.axis_index("x") + lax.axis_index("y")
    place = jnp.stack([chip, lax.axis_index("c")]).astype(jnp.int32)

    def own_slot(gathered, shards):
        return [lax.dynamic_update_slice(g, s[None], (chip, 0, 0)) for g, s in zip(gathered, shards)]

    groups = list(stack_shards(wts, BF16))
    groups[2].append(jnp.concatenate([mla_q_lora_norm, jnp.pad(mla_kv_lora_norm, ((0, 0), (0, 64))), jnp.zeros((14, 128), F32)], axis=0))
    over_ici = [plan_gather_ici(shards) for shards in groups]
    _SCHEDULE_BEHIND.clear()
    schedule_behind(split_start("gather_start", over_ici))
    env, hooks, passed_on = {}, {}, {}

    def arrived(k, after, pass_now):
        passing = plan_gather_pass(split_wait(f"gather_wait_{k}", over_ici[k], after))
        passed_on[k] = passing
        return own_slot(run_plan(f"gather_pass_{k}", passing), groups[k]) if pass_now else None

    def need_in0(env, last):
        env["in0"], = arrived(0, last, True)

    def need_layer0(env, last):
        pair0, glu = split_wait("gather_wait_1", over_ici[1], last)
        passed_on[1] = plan_gather_pass([pair0])
        env["glu"], = own_slot(run_plan("gather_pass_glu", plan_gather_pass([glu])), groups[1][1:])

    def pair0_landed(env, last):
        env["pair0"], = own_slot(passed_on[1].results, groups[1][:1])

    def need_layer1(env, last):
        arrived(2, last, False)

    def layer1_weights(env, last):
        pair1, ukv, in1, uq, norms = own_slot(passed_on[2].results, groups[2])
        env.update(pair1=pair1, ukv=ukv, w_in1=in1_rows_permute(in1[:, :MLA_IN // 4].reshape(MLA_IN, D)), w_uq=uq_rows_permute(uq.reshape(-1, QL)),
                   q_lora_norm=norms[:, 0, :], kv_lora_norm=norms[:, 1, :64])

    hooks["before", "l0_in"], hooks["before", "l0_glu"], hooks["before", "l0_out"] = need_in0, need_layer0, need_layer1
    hooks["plans", "l0_glu"], hooks["after", "l0_glu"] = (lambda env: [passed_on[1]]), pair0_landed
    hooks["plans", "l0_out"], hooks["after", "l0_out"] = (lambda env: [passed_on[2]]), layer1_weights

    rs = {}

    def swap(k, gs):
        rs[k, "g"], rs[k, "swap"] = gs, plan_pair_exchange(gs)
        return rs[k, "swap"]

    def start_scatter(k):
        rs[k, "pairs"] = pair_adds(f"rs{k}", rs[k, "g"], rs[k, "swap"].results, place)
        rs[k, "scatter"] = plan_chip_scatter(rs[k, "pairs"])
        schedule_behind(split_start(f"rs{k}_scatter_start", [rs[k, "scatter"]]))

    def join(k, after):
        rs[k, "join"] = plan_pair_join(chip_adds(f"rs{k}", rs[k, "pairs"], split_wait(f"rs{k}_scatter_wait", rs[k, "scatter"], after), place))
        return rs[k, "join"]

    def layer1_gradients(env):
        g_in1 = jnp.pad(in1_rows_unpermute(env["dw_in1"]).reshape(4, MLA_IN // 4, D), ((0, 0), (0, IN1T_ROWS - MLA_IN // 4), (0, 0)))
        return [swap(1, [env["g_pair1"], env["g_ukv"], g_in1, uq_rows_unpermute(env["dw_uq"]).reshape(4, -1, QL)])]

    hooks["plans", "l1_in_dx"] = layer1_gradients
    hooks["after", "l1_in_dx"] = lambda env, last: start_scatter(1)
    hooks["plans", "l0_glu_dx"] = lambda env: [swap(0, [env["g_pair0"], env["g_glu"]])]

    def before_s5_backward(env, last):
        start_scatter(0)
        env["join1"] = join(1, last)

    hooks["before", "s5_backward"] = before_s5_backward
    hooks["plans", "s5_backward"] = lambda env: [env["join1"]]
    hooks["after", "s5_backward"] = lambda env, last: env.update(join0=join(0, last))
    hooks["plans", "l0_in_dx"] = lambda env: [swap(2, [env["g_in0"]]), env["join0"]]
    hooks["after", "l0_in_dx"] = lambda env, last: start_scatter(2)

    def total_loss(env, local):
        env["loss"] = lax.psum(local[0, 0], MESH_AXES)
        schedule_behind(env["loss"].reshape(1, 1))

    hooks["after", "loss"] = total_loss
    small = {n: wts[n] for n, _ in SMALL}
    loss, grad_x, g = device_step(x[0], mem[0], positions[0], loss_target[0], small, env, hooks)
    loss = env["loss"]
    r_in0, = run_plan("rs2_pair_join", join(2, g["s5_log_step"]))
    (r_pair1, r_ukv, r_in1, r_uq), (r_pair0, r_glu) = (rs[k, "join"].results for k in (1, 0))

    small_flat = jnp.concatenate([g[n].reshape(-1) for n, _ in SMALL_FULL])
    g_small = jnp.pad(small_flat, (0, 4 * SMALL_ROWS * SMALL_LANES - N_SMALL)).astype(BF16).reshape(4, SMALL_ROWS, SMALL_LANES)
    r_small = reduce_scatter_chips("rs3", [g_small], place)[0]
    small_all = own_slot(all_gather_chips("gather_small_grads", [r_small]), [r_small])[0].reshape(-1)[:N_SMALL]

    grads = {"w_out": jnp.stack([r_pair0[:PAIR_MKV], r_pair1[:PAIR_MKV]]), "w_mem_kv": jnp.stack([r_pair0[PAIR_MKV:], r_pair1[PAIR_MKV:]]),
             "s5_w_in": r_in0[None], "s5_w_glu": r_glu[None], "mla_w_ukv": r_ukv[None], "mla_w_in": r_in1[:MLA_IN // 4].T[None], "mla_w_uq": r_uq.T[None]}
    off = 0
    for n, s in SMALL_FULL:
        grads[n] = small_all[off:off + math.prod(s)].reshape(s)
        off += math.prod(s)
    for n, s in SHARDED_SMALL:
        grads[n] = lax.dynamic_slice(grads[n], (0, chip * s[1]), s)

    delta, new_m, new_v = {}, {}, {}
    for n, s in BIG + [(n, s) for n, s in SMALL if len(s) == 4]:
        perm = MINOR_LAST.get(n, tuple(range(len(s))))
        turned = tuple(s[p] for p in perm)
        view = lambda a: jnp.transpose(a, perm).reshape(-1, turned[-1])
        res = adamw("adamw_" + n, view(wts[n]), view(grads[n]), view(mom[n]), view(var[n]))
        delta[n], new_m[n], new_v[n] = (jnp.transpose(r.reshape(turned), tuple(perm.index(i) for i in range(len(s)))) for r in res)
    small_names = [n for n, s in SMALL if len(s) < 4] + [n for n, _ in SHARDED_SMALL]
    n_own = sum(wts[n].size for n in small_names)
    rows_own = -(-n_own // (8 * 128)) * 8

    def pack_small(d):
        flat = jnp.concatenate([d[n].reshape(-1) for n in small_names])
        return jnp.pad(flat, (0, rows_own * 128 - n_own), constant_values=1.0).reshape(rows_own, 128)

    res = adamw("adamw_small", pack_small(wts), pack_small(grads), pack_small(mom), pack_small(var))
    off = 0
    for n in small_names:
        size = wts[n].size
        delta[n], new_m[n], new_v[n] = (r.reshape(-1)[off:off + size].reshape(wts[n].shape) for r in res)
        off += size

    return (loss, grad_x[None], *[grads[n] for n in WEIGHT_ORDER], *[delta[n] for n in WEIGHT_ORDER],
            *[new_m[n] for n in WEIGHT_ORDER], *[new_v[n] for n in WEIGHT_ORDER])
```

```python
import functools
import math

import jax
import jax.numpy as jnp
from jax import lax
from jax.experimental import pallas as pl
from jax.experimental.pallas import tpu as pltpu

F32, BF16 = jnp.float32, jnp.bfloat16
SDS = jax.ShapeDtypeStruct

D = 1024
L = 2048
ML = 256
BW = 2 * D
XQW = BW // 4
PW = BW - XQW
XH, XHD = 4, 128
SG, SC, SP = 96, 16, 64
SN = SG * SP
NOPE, ROPE, VD = 128, 64, 128
MH = 12
QL, KVL = 512, 256
EPS = 1e-6
ROPE_THETA = 10000.0
MLA_IN = QL + KVL + ROPE + XQW + BW
MLA_IN_P = 3456
ADAM_LR, ADAM_B1, ADAM_B2, ADAM_EPS, ADAM_WD, ADAM_STEP = 0.001, 0.9, 0.999, 1e-08, 0.01, 10

VMEM_LIMIT = 48 * 2**20
ROW_TILE = 512
BMAT_TILE, CMAT_TILE = 2048, 512
SEG = 8
SEG_LEN = L // SEG
MESH_AXES = ("x", "y", "c")


def _cparams():
    return pltpu.CompilerParams(vmem_limit_bytes=VMEM_LIMIT)


def _dg(a, b, ca, cb):
    return lax.dot_general(a.astype(BF16), b.astype(BF16), (((ca,), (cb,)), ((), ())), preferred_element_type=F32)


@jax.custom_vjp
def mm_nn(a, b):
    return _dg(a, b, 1, 0)


mm_nn.defvjp(lambda a, b: (_dg(a, b, 1, 0), (a, b)), lambda res, g: (_dg(g, res[1], 1, 1), _dg(res[0], g, 0, 0)))


@jax.custom_vjp
def mm_nt(a, b):
    return _dg(a, b, 1, 1)


mm_nt.defvjp(lambda a, b: (_dg(a, b, 1, 1), (a, b)), lambda res, g: (_dg(g, res[1], 1, 0), _dg(g, res[0], 0, 0)))


@functools.partial(jax.custom_vjp, nondiff_argnums=(1,))
def lane_roll(x, shift):
    return pltpu.roll(x, shift, 1)


lane_roll.defvjp(lambda x, shift: (pltpu.roll(x, shift, 1), None),
                 lambda shift, _, g: (pltpu.roll(g, (128 - shift) % 128, 1),))


def rms(x, g):
    return x * lax.rsqrt(jnp.mean(x * x, axis=-1, keepdims=True) + EPS) * g


@jax.custom_vjp
def softmax_rows(s):
    e = jnp.exp(s - jnp.max(s, axis=-1, keepdims=True))
    return e / jnp.sum(e, axis=-1, keepdims=True)


def _softmax_rows_fwd(s):
    p = softmax_rows(s)
    return p, p


def _softmax_rows_bwd(p, g):
    return (p * (g - jnp.sum(g * p, axis=-1, keepdims=True)),)


softmax_rows.defvjp(_softmax_rows_fwd, _softmax_rows_bwd)


def silu(x):
    return x * jax.nn.sigmoid(x)


ANY = pl.BlockSpec(memory_space=pl.ANY)
MESH_ID = pl.DeviceIdType.MESH


def _dma_sems(n):
    return [pltpu.SemaphoreType.DMA((n,)), pltpu.SemaphoreType.DMA((n,))]


class Plan:
    def __init__(self, operands, out_shape, aliases, n_sems, copies):
        self.operands, self.out_shape, self.aliases, self.n_sems, self.copies = list(operands), list(out_shape), aliases, n_sems, copies
        self.results = None


_SCHEDULE_BEHIND = []


def schedule_behind(token):
    _SCHEDULE_BEHIND.append(token)


def hosted_call(kern, *, name, grid, in_specs, out_specs, out_shape, operands, scratch_shapes=(), aliases=None, cparams=None, plans=(), deps=()):
    n_in, n_out, n_scr = len(in_specs), len(out_specs), len(scratch_shapes)
    p_in, p_out = [len(p.operands) for p in plans], [len(p.out_shape) for p in plans]
    deps = tuple(deps) + tuple(_SCHEDULE_BEHIND)
    _SCHEDULE_BEHIND.clear()
    all_aliases = dict(aliases or {})
    in_off, out_off = n_in, n_out
    for p, ni, no in zip(plans, p_in, p_out):
        all_aliases.update({in_off + i: out_off + o for i, o in p.aliases.items()})
        in_off, out_off = in_off + ni, out_off + no

    def body(*refs):
        pos, pins, pouts = n_in, [], []
        for ni in p_in:
            pins.append(refs[pos:pos + ni])
            pos += ni
        pos += len(deps)
        main_out = refs[pos:pos + n_out]
        pos += n_out
        for no in p_out:
            pouts.append(refs[pos:pos + no])
            pos += no
        main_scr = refs[pos:pos + n_scr]
        pos += n_scr
        if plans:
            ids = [pl.program_id(ax) for ax in range(len(grid))]
            first = functools.reduce(jnp.logical_and, [i == 0 for i in ids])
            last = functools.reduce(jnp.logical_and, [i == g - 1 for i, g in zip(ids, grid)])
            copies = [p.copies(pins[k], pouts[k], refs[pos + 2 * k], refs[pos + 2 * k + 1]) for k, p in enumerate(plans)]

            @pl.when(first)
            def _():
                for sends, _ in copies:
                    for cp in sends:
                        cp.start()

        kern(*refs[:n_in], *main_out, *main_scr)
        if plans:
            @pl.when(last)
            def _():
                for sends, recvs in copies:
                    for cp in recvs:
                        cp.wait_recv()
                    for cp in sends:
                        cp.wait_send()

    res = pl.pallas_call(body, grid=grid, in_specs=list(in_specs) + [ANY] * (sum(p_in) + len(deps)),
                         out_specs=list(out_specs) + [ANY] * sum(p_out), out_shape=list(out_shape) + [s for p in plans for s in p.out_shape],
                         scratch_shapes=list(scratch_shapes) + [s for p in plans for s in _dma_sems(p.n_sems)],
                         input_output_aliases=all_aliases, name=name, compiler_params=cparams or _cparams())(
        *operands, *[a for p in plans for a in p.operands], *deps)
    pos = n_out
    for p, no in zip(plans, p_out):
        p.results = list(res[pos:pos + no])
        pos += no
    return list(res[:n_out])


def _wide(v):
    return v.astype(F32) if v.dtype == BF16 else v


def stage(name, fn, grid, ins, outs):
    n_in = len(ins)

    def kern(*refs):
        res = fn(*[_wide(r[...]) for r in refs[:n_in]])
        for r, v in zip(refs[n_in:], res):
            r[...] = v.astype(r.dtype)

    return hosted_call(kern, name=name, grid=grid, in_specs=[s for _, s in ins], out_specs=[s for _, s in outs],
                       out_shape=[sd for sd, _ in outs], operands=[a for a, _ in ins])


def stage_bwd(name, fn, grid, ins, cts, diffs, plans=()):
    n_in, n_ct = len(ins), len(cts)
    didx = [i for i, d in enumerate(diffs) if d is not None]
    opts = {i: (diffs[i][3] if len(diffs[i]) > 3 else {}) for i in didx if diffs[i][0] == "row"}
    adds = [(i, opts[i]["add"]) for i in opts if "add" in opts[i]]
    intos = [(i, opts[i]["into"]) for i in opts if "into" in opts[i]]
    n_add, n_into = len(adds), len(intos)
    add_pos = {i: n_in + n_ct + k for k, (i, _) in enumerate(adds)}
    n_extra = n_in + n_ct + n_add + n_into

    def kern(*refs):
        vals = [_wide(r[...]) for r in refs[:n_in]]

        def f(*dv):
            full = list(vals)
            for i, v in zip(didx, dv):
                full[i] = v
            return fn(*full)

        _, vjp = jax.vjp(f, *[vals[i].astype(F32) for i in didx])
        gs = vjp(tuple(c[...].astype(F32) for c in refs[n_in:n_in + n_ct]))
        for o_ref, i, g in zip(refs[n_extra:], didx, gs):
            if diffs[i][0] == "row":
                if i in add_pos:
                    g = g + refs[add_pos[i]][...].astype(F32)
                o_ref[...] = g.astype(o_ref.dtype)
            else:
                first = functools.reduce(jnp.logical_and, [pl.program_id(ax) == 0 for ax in diffs[i][1]])

                @pl.when(first)
                def _():
                    o_ref[...] = g

                @pl.when(jnp.logical_not(first))
                def _():
                    o_ref[...] += g

    out_shape, out_specs = [], []
    for i in didx:
        if diffs[i][0] == "row":
            out_shape.append(diffs[i][1])
            out_specs.append(diffs[i][2])
        else:
            out_shape.append(SDS(ins[i][0].shape, F32))
            out_specs.append(ins[i][1])
    aliases = {n_in + n_ct + n_add + k: didx.index(i) for k, (i, _) in enumerate(intos)}
    in_specs = [s for _, s in ins] + [s for _, s in cts] + [s for _, (_, s) in adds] + [ANY] * n_into
    operands = [a for a, _ in ins] + [a for a, _ in cts] + [a for _, (a, _) in adds] + [a for _, a in intos]
    return hosted_call(kern, name=name, grid=grid, in_specs=in_specs, out_specs=out_specs, out_shape=out_shape, operands=operands,
                       aliases=aliases, plans=plans)


def rspec(tl, w, cb=0):
    return pl.BlockSpec((tl, w), lambda i: (i, cb))


def cspec(shape):
    return pl.BlockSpec(shape, lambda i: (0,) * len(shape))


def row_fwd(name, fn, rows, tl, row_ins, consts, outs):
    ins = [(a, rspec(tl, w, cb)) for a, w, cb in row_ins] + [(a, cspec(a.shape)) for a in consts]
    return stage(name, fn, (rows // tl,), ins, [(SDS((rows, w), dt), rspec(tl, w)) for w, dt in outs])


def row_bwd(name, fn, rows, tl, row_ins, consts, cts, row_diff, const_diff, plans=()):
    ins = [(a, rspec(tl, w, cb)) for a, w, cb in row_ins] + [(a, cspec(a.shape)) for a in consts]
    diffs = []
    for (a, w, cb), d in zip(row_ins, row_diff):
        if not d:
            diffs.append(None)
            continue
        d = d if isinstance(d, dict) else {}
        opts = {}
        if "add" in d:
            opts["add"] = (d["add"][0], rspec(tl, d["add"][1], d["add"][2]))
        if d.get("into") is not None:
            opts["into"] = d["into"]
        diffs.append(("row", SDS((rows, d.get("cols", w)), d.get("dtype", F32)), rspec(tl, w, d.get("cb", 0)), opts))
    diffs += [("acc", (0,)) if d else None for d in const_diff]
    return stage_bwd(name, fn, (rows // tl,), ins, [(a, rspec(tl, w, cb)) for a, w, cb in cts], diffs, plans=plans)


MATMUL_VMEM = 36 * 2**20
ADAMW_VMEM = 28 * 2**20


class Sharded:
    def __init__(self, arr, kind, roff, rows):
        self.arr, self.kind, self.roff, self.rows, self.n = arr, kind, roff, rows, arr.shape[2]
        self.shape = (rows, 4 * self.n) if kind == "col" else (4 * rows, self.n)

    def fits(self, t0, t1):
        return self.roff % t0 == 0 and self.rows % t0 == 0 and self.n % t1 == 0

    def spec(self, t0, t1, bidx):
        assert self.fits(t0, t1), (self.kind, self.roff, self.rows, self.n, t0, t1)
        r0 = self.roff // t0
        if self.kind == "col":
            per = self.n // t1
            return pl.BlockSpec((None, t0, t1), lambda *g: (bidx(*g)[1] // per, r0 + bidx(*g)[0], bidx(*g)[1] % per))
        per = self.rows // t0
        return pl.BlockSpec((None, t0, t1), lambda *g: (bidx(*g)[0] // per, r0 + bidx(*g)[0] % per, bidx(*g)[1]))


def matmul(name, a, b, mode, out_dtype=BF16, add=None, out=None, into=None, plans=()):
    if mode == "tn":
        k_dim, m = a.shape
    else:
        m, k_dim = a.shape
    n = b.shape[0] if mode == "nt" else b.shape[1]
    b_fit = b.fits if isinstance(b, Sharded) else (lambda t0, t1: True)
    o_fit = out.fits if out is not None else (lambda t0, t1: True)
    a_bytes, b_bytes = jnp.dtype(a.dtype).itemsize, jnp.dtype(b.arr.dtype if isinstance(b, Sharded) else b.dtype).itemsize
    o_bytes = jnp.dtype(out_dtype if out is None else out.arr.dtype).itemsize

    def vmem(tm, tn, tk):
        return 2 * (tm * tk * a_bytes + tk * tn * b_bytes + tm * tn * (o_bytes + (4 if add is not None else 0))) + 4 * tm * tn * (1 + (tk < k_dim))

    tiles = [(tm, tn, tk) for tm in (2048, 1024, 512, 256, 128) for tn in (1024, 768, 512, 384, 256, 128)
             for tk in sorted({k_dim, 1024, 768, 512, 384, 256, 128})
             if m % tm == 0 and n % tn == 0 and k_dim % tk == 0 and (b_fit(tn, tk) if mode == "nt" else b_fit(tk, tn)) and o_fit(tm, tn)
             and vmem(tm, tn, tk) <= MATMUL_VMEM]
    tm, tn, tk = max(tiles, key=lambda t: (t[2] == k_dim, t[0] * t[1] * t[2], t[0] * t[1]))
    nk = k_dim // tk
    a_spec = pl.BlockSpec((tk, tm), lambda i, j, k: (k, i)) if mode == "tn" else pl.BlockSpec((tm, tk), lambda i, j, k: (i, k))
    if isinstance(b, Sharded):
        b_spec = b.spec(tn, tk, lambda i, j, k: (j, k)) if mode == "nt" else b.spec(tk, tn, lambda i, j, k: (k, j))
        b = b.arr
    else:
        b_spec = pl.BlockSpec((tn, tk), lambda i, j, k: (j, k)) if mode == "nt" else pl.BlockSpec((tk, tn), lambda i, j, k: (k, j))
    o_spec = pl.BlockSpec((tm, tn), lambda i, j, k: (i, j))
    out_spec, out_shape = (o_spec, SDS((m, n), out_dtype)) if out is None else (out.spec(tm, tn, lambda i, j, k: (i, j)), out.arr)
    ca, cb = {"nn": (1, 0), "nt": (1, 1), "tn": (0, 0)}[mode]
    n_in = 2 + (add is not None)

    def finish(refs, o_ref, r):
        if add is not None:
            r = r + refs[2][...]
        o_ref[...] = r.astype(o_ref.dtype)

    def kern_whole(*refs):
        finish(refs, refs[-1], _dg(refs[0][...], refs[1][...], ca, cb))

    def kern_cut(*refs):
        o_ref, acc = refs[-2], refs[-1]
        k = pl.program_id(2)

        @pl.when(k == 0)
        def _():
            acc[...] = jnp.zeros_like(acc)

        acc[...] += _dg(refs[0][...], refs[1][...], ca, cb)

        @pl.when(k == nk - 1)
        def _():
            finish(refs, o_ref, acc[...])

    ins, specs = [a, b], [a_spec, b_spec]
    if add is not None:
        ins.append(add)
        specs.append(o_spec)
    if into is not None:
        ins.append(into)
        specs.append(ANY)
    return hosted_call(kern_whole if nk == 1 else kern_cut, name=name, grid=(m // tm, n // tn, nk), in_specs=specs, out_specs=[out_spec],
                       out_shape=[out_shape], operands=ins, scratch_shapes=[] if nk == 1 else [pltpu.VMEM((tm, tn), F32)],
                       aliases={} if into is None else {n_in: 0}, plans=plans)[0]


SCAN_UNROLL = 8


def _cmul(ar, ai, br, bi):
    return ar * br - ai * bi, ar * bi + ai * br


def _sub_shift(x, down):
    row = lax.broadcasted_iota(jnp.int32, x.shape, 0)
    if down:
        return jnp.where(row == 0, 0.0, pltpu.roll(x, 1, 0))
    return jnp.where(row == SEG - 1, 0.0, pltpu.roll(x, SEG - 1, 0))


def _pow_seg_len(ar, ai):
    for _ in range(int(math.log2(SEG_LEN))):
        ar, ai = _cmul(ar, ai, ar, ai)
    return ar, ai


def _scan_in_place(sr, si, a_re, a_im):
    lanes = sr.shape[1]
    ar = jnp.broadcast_to(a_re, (SEG, lanes))
    ai = jnp.broadcast_to(a_im, (SEG, lanes))
    zero = jnp.zeros((SEG, lanes), F32)

    def local(i, carry):
        rows = pl.ds(pl.multiple_of(i * SEG, SEG), SEG)
        mr, mi = _cmul(ar, ai, carry[0], carry[1])
        nr, ni = mr + sr[rows, :], mi + si[rows, :]
        sr[rows, :] = nr
        si[rows, :] = ni
        return nr, ni

    fr, fi = lax.fori_loop(0, SEG_LEN, local, (zero, zero), unroll=SCAN_UNROLL)
    pr, pi = _pow_seg_len(ar, ai)
    ir, ii = zero, zero
    for _ in range(SEG - 1):
        mr, mi = _cmul(pr, pi, ir, ii)
        ir, ii = _sub_shift(mr + fr, True), _sub_shift(mi + fi, True)

    def carry_in(i, pw):
        rows = pl.ds(pl.multiple_of(i * SEG, SEG), SEG)
        cr, ci = _cmul(pw[0], pw[1], ir, ii)
        sr[rows, :] += cr
        si[rows, :] += ci
        return _cmul(pw[0], pw[1], ar, ai)

    lax.fori_loop(0, SEG_LEN, carry_in, (ar, ai), unroll=SCAN_UNROLL)


S5_LANES = 8 * SP
S5_BLOCKS = SN // S5_LANES


def _s5_specs(n=1):
    u_spec = pl.BlockSpec((L, n * 8 * SC), lambda j: (0, j))
    s_spec = pl.BlockSpec((L, n * S5_LANES), lambda j: (0, j))
    wb_spec = pl.BlockSpec((n * S5_LANES, 8 * SC), lambda j: (j, 0))
    wc_spec = pl.BlockSpec((n * 8 * SC, S5_LANES), lambda j: (j, 0))
    a_spec = pl.BlockSpec((1, n * S5_LANES), lambda j: (0, j))
    d_spec = pl.BlockSpec((1, n * 8 * SC), lambda j: (0, j))
    return u_spec, s_spec, wb_spec, wc_spec, a_spec, d_spec


S5_FWD_BLOCKS = 2


def s5_forward(proj, wb_re, wb_im, wc_re, wc_im, a_re, a_im, d, plans=()):
    n = S5_FWD_BLOCKS
    cols, lanes = 8 * SC, S5_LANES

    def kern(u_ref, wbr, wbi, wcr, wci, ar, ai, d_ref, sr_out, si_out, g_ref, sr, si):
        u = _wide(u_ref[...])
        for b in range(n):
            sr[:, b * lanes:(b + 1) * lanes], si[:, b * lanes:(b + 1) * lanes] = fn_s5_bu(
                u[:, b * cols:(b + 1) * cols], wbr[b * lanes:(b + 1) * lanes, :], wbi[b * lanes:(b + 1) * lanes, :])
        _scan_in_place(sr, si, ar[...], ai[...])
        for b in range(n):
            g_ref[:, b * cols:(b + 1) * cols] = fn_s5_out(
                sr[:, b * lanes:(b + 1) * lanes], si[:, b * lanes:(b + 1) * lanes], u[:, b * cols:(b + 1) * cols],
                d_ref[:, b * cols:(b + 1) * cols], wcr[b * cols:(b + 1) * cols, :], wci[b * cols:(b + 1) * cols, :])[0].astype(g_ref.dtype)
        sr_out[...] = sr[...].astype(sr_out.dtype)
        si_out[...] = si[...].astype(si_out.dtype)

    u_spec, s_spec, wb_spec, wc_spec, a_spec, d_spec = _s5_specs(n)
    return hosted_call(kern, name="s5_forward", grid=(S5_BLOCKS // n,), in_specs=[u_spec, wb_spec, wb_spec, wc_spec, wc_spec, a_spec, a_spec, d_spec],
                       out_specs=[s_spec, s_spec, u_spec], out_shape=[SDS((L, SN), BF16)] * 2 + [SDS((L, PW), BF16)],
                       operands=[proj, wb_re, wb_im, wc_re, wc_im, a_re, a_im, d], scratch_shapes=[pltpu.VMEM((L, n * lanes), F32)] * 2,
                       plans=plans)


def _adjoint_scan_in_place(lr, li, sr, si, a_re, a_im):
    lanes = lr.shape[1]
    ar = jnp.broadcast_to(a_re, (SEG, lanes))
    ai = -jnp.broadcast_to(a_im, (SEG, lanes))
    zero = jnp.zeros((SEG, lanes), F32)

    def local(k, carry):
        i = SEG_LEN - 1 - k
        rows = pl.ds(pl.multiple_of(i * SEG, SEG), SEG)
        mr, mi = _cmul(ar, ai, carry[0], carry[1])
        nr, ni = mr + lr[rows, :], mi + li[rows, :]
        lr[rows, :] = nr
        li[rows, :] = ni
        return nr, ni

    fr, fi = lax.fori_loop(0, SEG_LEN, local, (zero, zero), unroll=SCAN_UNROLL)
    pr, pi = _pow_seg_len(ar, ai)
    ir, ii = zero, zero
    for _ in range(SEG - 1):
        mr, mi = _cmul(pr, pi, ir, ii)
        ir, ii = _sub_shift(mr + fr, False), _sub_shift(mi + fi, False)

    def fix(rows, pw):
        cr, ci = _cmul(pw[0], pw[1], ir, ii)
        tr, ti = lr[rows, :] + cr, li[rows, :] + ci
        lr[rows, :] = tr
        li[rows, :] = ti
        return tr, ti

    def grad_a(tr, ti, spr, spi, acc):
        return acc[0] + tr * spr + ti * spi, acc[1] + ti * spr - tr * spi

    def carry_in(k, c):
        i = SEG_LEN - 1 - k
        rows = pl.ds(pl.multiple_of(i * SEG, SEG), SEG)
        prev = pl.ds(pl.multiple_of((i - 1) * SEG, SEG), SEG)
        tr, ti = fix(rows, (c[0], c[1]))
        acc = grad_a(tr, ti, sr[prev, :], si[prev, :], (c[2], c[3]))
        nr, ni = _cmul(c[0], c[1], ar, ai)
        return nr, ni, acc[0], acc[1]

    pwr, pwi, accr, acci = lax.fori_loop(0, SEG_LEN - 1, carry_in, (ar, ai, zero, zero), unroll=5)
    tr, ti = fix(pl.ds(0, SEG), (pwr, pwi))
    last = pl.ds((SEG_LEN - 1) * SEG, SEG)
    accr, acci = grad_a(tr, ti, _sub_shift(sr[last, :], True), _sub_shift(si[last, :], True), (accr, acci))
    return jnp.sum(accr, axis=0, keepdims=True), jnp.sum(acci, axis=0, keepdims=True)


S5_BWD_VMEM = 58 * 2**20


def s5_backward(dg, proj, s_re, s_im, wb_re, wb_im, wc_re, wc_im, a_re, a_im, d, dproj, plans=()):
    def kern(dg_ref, u_ref, sr_in, si_in, wbr, wbi, wcr, wci, ar, ai, d_ref, _, du_ref, dd_ref, dwcr, dwci, dwbr, dwbi, dar, dai, lr, li, sr, si):
        u = _wide(u_ref[...])
        sr[...], si[...] = _wide(sr_in[...]), _wide(si_in[...])
        _, vjp_out = jax.vjp(fn_s5_out, sr[...], si[...], u, d_ref[...], wcr[...], wci[...])
        lr[...], li[...], du_out, dd_ref[...], dwcr[...], dwci[...] = vjp_out((_wide(dg_ref[...]),))
        dar[...], dai[...] = _adjoint_scan_in_place(lr, li, sr, si, ar[...], ai[...])
        _, vjp_in = jax.vjp(fn_s5_bu, u, wbr[...], wbi[...])
        du_in, dwbr[...], dwbi[...] = vjp_in((lr[...], li[...]))
        du_ref[...] = (du_out + du_in).astype(du_ref.dtype)

    u_spec, s_spec, wb_spec, wc_spec, a_spec, d_spec = _s5_specs()
    outs = [(SDS(dproj.shape, dproj.dtype), u_spec), (SDS(d.shape, F32), d_spec), (SDS(wc_re.shape, F32), wc_spec), (SDS(wc_im.shape, F32), wc_spec),
            (SDS(wb_re.shape, F32), wb_spec), (SDS(wb_im.shape, F32), wb_spec), (SDS(a_re.shape, F32), a_spec), (SDS(a_im.shape, F32), a_spec)]
    return hosted_call(kern, name="s5_backward", grid=(S5_BLOCKS,),
                       in_specs=[u_spec, u_spec, s_spec, s_spec, wb_spec, wb_spec, wc_spec, wc_spec, a_spec, a_spec, d_spec, ANY],
                       out_specs=[sp for _, sp in outs], out_shape=[sd for sd, _ in outs], aliases={11: 0},
                       operands=[dg, proj, s_re, s_im, wb_re, wb_im, wc_re, wc_im, a_re, a_im, d, dproj],
                       scratch_shapes=[pltpu.VMEM((L, S5_LANES), F32)] * 4,
                       cparams=pltpu.CompilerParams(vmem_limit_bytes=S5_BWD_VMEM), plans=plans)


def fn_rms(x, g):
    return (rms(x, g),)


def fn_s5_disc(lre, lim, ls):
    step = jnp.exp(ls)
    e = jnp.exp(lre * step)
    a_re, a_im = e * jnp.cos(lim * step), e * jnp.sin(lim * step)
    den = lre * lre + lim * lim
    nr, ni = a_re - 1.0, a_im
    return a_re, a_im, (nr * lre + ni * lim) / den, (ni * lre - nr * lim) / den


def _group_mask(rows, cols, row_div, col_div):
    r = lax.broadcasted_iota(jnp.int32, (rows, cols), 0) // row_div % 8
    c = lax.broadcasted_iota(jnp.int32, (rows, cols), 1) // col_div
    return r == c


def _spread(x, mask):
    w = x.shape[1]
    copy = (lax.broadcasted_iota(jnp.int32, (w, 8 * w), 1) % w == lax.broadcasted_iota(jnp.int32, (w, 8 * w), 0)).astype(F32)
    return jnp.where(mask, jnp.dot(x, copy, precision=lax.Precision.HIGHEST, preferred_element_type=F32), 0.0)


def fn_s5_bmat(b_re, b_im, coef_re, coef_im):
    mask = _group_mask(b_re.shape[0], 8 * SC, SP, SC)
    return _spread(coef_re * b_re - coef_im * b_im, mask), _spread(coef_re * b_im + coef_im * b_re, mask)


def fn_s5_cmat(c_re, c_im):
    mask = _group_mask(c_re.shape[0], 8 * SP, SC, SP)
    return _spread(c_re, mask), _spread(c_im, mask)


def fn_s5_bu(u, wb_re, wb_im):
    return mm_nt(u, wb_re), mm_nt(u, wb_im)


def fn_s5_out(sr, si, u, d, wc_re, wc_im):
    y = mm_nt(sr, wc_re) - mm_nt(si, wc_im) + d * u
    return (jax.nn.gelu(y),)


def fn_merge_glu(z, mo, gate):
    yg = z[:, :PW] * jax.nn.sigmoid(z[:, PW:])
    return (jnp.concatenate([yg, mo], axis=1) * silu(gate),)


def fn_merge(prim, mo, gate):
    return (jnp.concatenate([prim, mo], axis=1) * silu(gate),)


def fn_mem_k(kv, g):
    return (jnp.concatenate([rms(kv[:, h * XHD:(h + 1) * XHD], g) for h in range(XH)], axis=1),)


def fn_mem_attn(xq, kn, v, g):
    outs = []
    for h in range(XH):
        sl = slice(h * XHD, (h + 1) * XHD)
        p = softmax_rows(mm_nt(rms(xq[:, sl], g), kn[:, sl]) * (XHD ** -0.5))
        outs.append(mm_nn(p, v[:, sl]))
    return (jnp.concatenate(outs, axis=1),)


def _half_rms(x, g):
    lo = lax.broadcasted_iota(jnp.int32, x.shape, 1) < ROPE
    x2 = x * x
    s_lo = jnp.sum(jnp.where(lo, x2, 0.0), axis=1, keepdims=True)
    s_hi = jnp.sum(jnp.where(lo, 0.0, x2), axis=1, keepdims=True)
    return x * lax.rsqrt(jnp.where(lo, s_lo, s_hi) / ROPE + EPS) * g


def _rope(x, cos2, sin_signed):
    first = lax.broadcasted_iota(jnp.int32, x.shape, 1) % ROPE < ROPE // 2
    return x * cos2 + jnp.where(first, lane_roll(x, 128 - ROPE // 2), lane_roll(x, ROPE // 2)) * sin_signed


def fn_mla_prep(q, kv, kr, cos2, sin_signed, qnn, knn, qrn, krn):
    lo = lax.broadcasted_iota(jnp.int32, kr.shape, 1) < ROPE
    kr_pad = jnp.where(lo, _rope(_half_rms(kr, krn), cos2, sin_signed), 0.0)
    qf, kf, vs = [], [], []
    for m in range(MH // 2):
        pair = _rope(_half_rms(q[:, MH * NOPE + 128 * m:MH * NOPE + 128 * (m + 1)], qrn), cos2, sin_signed)
        for h, rope_h in ((2 * m, pair), (2 * m + 1, lane_roll(pair, ROPE))):
            qf.append(jnp.concatenate([rms(q[:, NOPE * h:NOPE * (h + 1)], qnn), jnp.where(lo, rope_h, 0.0)], axis=1))
    for h in range(MH):
        kf.append(jnp.concatenate([rms(kv[:, 256 * h:256 * h + NOPE], knn), kr_pad], axis=1))
        vs.append(kv[:, 256 * h + NOPE:256 * (h + 1)])
    return jnp.stack(qf), jnp.stack(kf), jnp.stack(vs)


def _attn_scores(q, kf):
    tq = q.shape[0]
    scale = (NOPE + ROPE) ** -0.5
    own = _dg(q, kf[-tq:], 1, 1) * scale
    own = jnp.where(lax.broadcasted_iota(jnp.int32, own.shape, 1) <= lax.broadcasted_iota(jnp.int32, own.shape, 0), own, jnp.finfo(F32).min)
    return own if kf.shape[0] == tq else jnp.concatenate([_dg(q, kf[:-tq], 1, 1) * scale, own], axis=1)


ATT_FWD, ATT_BWD = (4, 512), (4, 256)


def _attn_specs(heads, tq):
    q_spec = pl.BlockSpec((heads, tq, 256), lambda h, i: (h, i, 0))
    k_spec = pl.BlockSpec((heads, L, 256), lambda h, i: (h, 0, 0))
    v_spec = pl.BlockSpec((heads, L, 128), lambda h, i: (h, 0, 0))
    o_spec = pl.BlockSpec((tq, heads * 128), lambda h, i: (i, h))
    lse_spec = pl.BlockSpec((heads, tq, 1), lambda h, i: (h, i, 0))
    return q_spec, k_spec, v_spec, o_spec, lse_spec


def _attn_branches(heads, tq, body):
    i = pl.program_id(1)
    for t in range(L // tq):
        @pl.when(i == t)
        def _(t=t):
            for hh in range(heads):
                body(hh, slice(hh * 128, (hh + 1) * 128), (t + 1) * tq)


def causal_attn(qf, kf, vh):
    def kern(q_ref, k_ref, v_ref, o_ref, lse_ref):
        def body(hh, lanes, keys):
            s = _attn_scores(q_ref[hh], k_ref[hh, :keys, :])
            m = jnp.max(s, axis=-1, keepdims=True)
            e = jnp.exp(s - m)
            total = jnp.sum(e, axis=-1, keepdims=True)
            o_ref[:, lanes] = (_dg(e, v_ref[hh, :keys, :], 1, 0) / total).astype(o_ref.dtype)
            lse_ref[hh] = m + jnp.log(total)

        _attn_branches(*ATT_FWD, body)

    q_spec, k_spec, v_spec, o_spec, lse_spec = _attn_specs(*ATT_FWD)
    return pl.pallas_call(kern, grid=(MH // ATT_FWD[0], L // ATT_FWD[1]), in_specs=[q_spec, k_spec, v_spec], out_specs=[o_spec, lse_spec],
                          out_shape=[SDS((L, MH * VD), F32), SDS((MH, L, 1), F32)], name="l1_attn", compiler_params=_cparams())(qf, kf, vh)


def causal_attn_bwd(qf, kf, vh, out, lse, dout):
    scale = (NOPE + ROPE) ** -0.5

    def kern(q_ref, k_ref, v_ref, o_ref, lse_ref, do_ref, dq_ref, dk_ref, dv_ref):
        @pl.when(pl.program_id(1) == 0)
        def _():
            dk_ref[...] = jnp.zeros_like(dk_ref)
            dv_ref[...] = jnp.zeros_like(dv_ref)

        def body(hh, lanes, keys):
            q, k, v, do = q_ref[hh], k_ref[hh, :keys, :], v_ref[hh, :keys, :], do_ref[:, lanes]
            p = jnp.exp(_attn_scores(q, k) - lse_ref[hh])
            delta = jnp.sum(do * _wide(o_ref[:, lanes]), axis=-1, keepdims=True)
            dv_ref[hh, :keys, :] += _dg(p, do, 0, 0)
            ds = p * (_dg(do, v, 1, 1) - delta) * scale
            dq_ref[hh] = _dg(ds, k, 1, 0)
            dk_ref[hh, :keys, :] += _dg(ds, q, 0, 0)

        _attn_branches(*ATT_BWD, body)

    q_spec, k_spec, v_spec, o_spec, lse_spec = _attn_specs(*ATT_BWD)
    return pl.pallas_call(kern, grid=(MH // ATT_BWD[0], L // ATT_BWD[1]), in_specs=[q_spec, k_spec, v_spec, o_spec, lse_spec, o_spec],
                          out_specs=[q_spec, k_spec, v_spec], out_shape=[SDS(qf.shape, F32), SDS(kf.shape, F32), SDS(vh.shape, F32)],
                          name="l1_attn_bwd", compiler_params=_cparams())(qf, kf, vh, out, lse, dout)


def loss_and_grad(y, target, tl=512):
    def kern(y_ref, t_ref, dy_ref, loss_ref):
        d = y_ref[...] - t_ref[...]
        dy_ref[...] = d / D

        @pl.when(pl.program_id(0) == 0)
        def _():
            loss_ref[...] = jnp.zeros_like(loss_ref)

        loss_ref[...] += 0.5 * jnp.sum(jnp.sum(d * d, axis=1, keepdims=True), axis=0, keepdims=True) / D

    return pl.pallas_call(kern, grid=(L // tl,), in_specs=[rspec(tl, D), rspec(tl, D)], out_specs=[rspec(tl, D), cspec((1, 1))],
                          out_shape=[SDS((L, D), F32), SDS((1, 1), F32)], name="loss", compiler_params=_cparams())(y, target)


def adamw(name, w, g, m, v):
    rows, cols = w.shape
    block_row_bytes = 7 * 2 * 4 * max(cols, 128)
    tr = _row_tile(rows, min(2048, ADAMW_VMEM // block_row_bytes // 8 * 8), 8)

    def kern(w_ref, g_ref, m_ref, v_ref, d_ref, nm_ref, nv_ref):
        gg = g_ref[...]
        nm = ADAM_B1 * m_ref[...] + (1.0 - ADAM_B1) * gg
        nv = ADAM_B2 * v_ref[...] + (1.0 - ADAM_B2) * jnp.square(gg)
        m_hat = nm / (1.0 - ADAM_B1 ** ADAM_STEP)
        v_hat = nv / (1.0 - ADAM_B2 ** ADAM_STEP)
        d_ref[...] = -ADAM_LR * (m_hat / (jnp.sqrt(v_hat) + ADAM_EPS) + ADAM_WD * w_ref[...])
        nm_ref[...] = nm
        nv_ref[...] = nv

    spec = rspec(tr, cols)
    return hosted_call(kern, name=name, grid=(rows // tr,), in_specs=[spec] * 4, out_specs=[spec] * 3,
                       out_shape=[SDS((rows, cols), F32)] * 3, operands=[w, g, m, v])


def _row_tile(rows, cap=512, unit=16):
    return max(t for t in range(unit, cap + 1, unit) if rows % t == 0)


def _place():
    x, y, c = lax.axis_index("x"), lax.axis_index("y"), lax.axis_index("c")
    return x, y, c, [(1 - x, y), (x, 1 - y), (1 - x, 1 - y)]


def _row_chunks(rows, n, dtype):
    unit = 32 // jnp.dtype(dtype).itemsize
    base, extra = divmod(rows // unit, n)
    out, start = [], 0
    for k in range(n):
        size = (base + (k < extra)) * unit
        if size:
            out.append((start, size))
            start += size
    assert start == rows, (rows, unit)
    return out


PIECE_BYTES = 1 << 20


def _pieces(shapes_dtypes, rows_of):
    out = []
    for b, (shape, dtype) in enumerate(shapes_dtypes):
        rows = rows_of(shape)
        n = max(1, min(4, rows * shape[-1] * jnp.dtype(dtype).itemsize // PIECE_BYTES))
        out += [(b, st, sz) for st, sz in _row_chunks(rows, n, dtype)]
    return out


def all_gather_chips(name, shards):
    nb = len(shards)
    pieces = _pieces([(s.shape, s.dtype) for s in shards], lambda shape: shape[0] // 2)
    n = len(pieces)

    def body(*refs):
        x_refs, out_refs, send_sems, recv_sems = refs[:nb], refs[nb:2 * nb], refs[2 * nb], refs[2 * nb + 1]
        x, y, c, chips = _place()
        sibling = (x, y, 1 - c)
        mine = 2 * x + y

        def copy(sem, chip, cc, k, to, from_input=False):
            b, st, sz = pieces[k]
            rows_k = pl.ds(cc * (x_refs[b].shape[0] // 2) + st, sz)
            dst = out_refs[b].at[chip, rows_k, :]
            return pltpu.make_async_remote_copy(src_ref=x_refs[b].at[rows_k, :] if from_input else dst, dst_ref=dst,
                                                send_sem=send_sems.at[sem], recv_sem=recv_sems.at[sem], device_id=to, device_id_type=MESH_ID)

        order = [(k, j, 2 * cx + cy, (cx, cy, c)) for k in range(n) for j, (cx, cy) in enumerate(chips)]
        first = [copy(j * n + k, mine, c, k, to, from_input=True) for k, j, _, to in order]
        for cp in first:
            cp.start()
        passed = []
        for k, j, chip, _ in order:
            copy(j * n + k, chip, c, k, sibling).wait_recv()
            passed.append(copy((3 + j) * n + k, chip, c, k, sibling))
            passed[-1].start()
        for k, j, chip, _ in order:
            copy((3 + j) * n + k, chip, 1 - c, k, sibling).wait_recv()
        for cp in first + passed:
            cp.wait_send()

    return pl.pallas_call(body, in_specs=[ANY] * nb, out_specs=[ANY] * nb, out_shape=[SDS((4,) + s.shape, s.dtype) for s in shards],
                          scratch_shapes=_dma_sems(6 * n), name=name)(*shards)


def plan_gather_ici(shards):
    pieces = _pieces([(s.shape, s.dtype) for s in shards], lambda shape: shape[0] // 2)
    n = len(pieces)

    def copies(x_refs, out_refs, send_sems, recv_sems):
        x, y, c, chips = _place()
        mine = 2 * x + y

        def copy(j, k, chip, to, from_input):
            b, st, sz = pieces[k]
            rows_k = pl.ds(c * (x_refs[b].shape[0] // 2) + st, sz)
            dst = out_refs[b].at[chip, rows_k, :]
            return pltpu.make_async_remote_copy(src_ref=x_refs[b].at[rows_k, :] if from_input else dst, dst_ref=dst, send_sem=send_sems.at[j * n + k],
                                                recv_sem=recv_sems.at[j * n + k], device_id=to, device_id_type=MESH_ID)

        order = [(k, j, 2 * cx + cy, (cx, cy, c)) for k in range(n) for j, (cx, cy) in enumerate(chips)]
        return [copy(j, k, mine, to, True) for k, j, _, to in order], [copy(j, k, chip, to, False) for k, j, chip, to in order]

    return Plan(shards, [SDS((4,) + s.shape, s.dtype) for s in shards], {}, 3 * n, copies)


def plan_gather_pass(gathered):
    pieces = _pieces([(g.shape[1:], g.dtype) for g in gathered], lambda shape: shape[0] // 2)
    n = len(pieces)

    def copies(_, out_refs, send_sems, recv_sems):
        x, y, c, chips = _place()

        def copy(j, k, chip, cc):
            b, st, sz = pieces[k]
            rows_k = out_refs[b].at[chip, pl.ds(cc * (out_refs[b].shape[1] // 2) + st, sz), :]
            return pltpu.make_async_remote_copy(src_ref=rows_k, dst_ref=rows_k, send_sem=send_sems.at[j * n + k], recv_sem=recv_sems.at[j * n + k],
                                                device_id=(x, y, 1 - c), device_id_type=MESH_ID)

        order = [(k, j, 2 * cx + cy) for k in range(n) for j, (cx, cy) in enumerate(chips)]
        return [copy(j, k, chip, c) for k, j, chip in order], [copy(j, k, chip, 1 - c) for k, j, chip in order]

    return Plan(gathered, [SDS(g.shape, g.dtype) for g in gathered], {i: i for i in range(len(gathered))}, 3 * n, copies)


def plan_pair_exchange(gs):
    pieces = _pieces([(g.shape, g.dtype) for g in gs], lambda shape: shape[1] // 2)

    def copies(g_refs, got_refs, send_sems, recv_sems):
        x, y, c, _ = _place()
        swaps = [pltpu.make_async_remote_copy(src_ref=g_refs[b].at[:, pl.ds((1 - c) * (g_refs[b].shape[1] // 2) + st, sz), :],
                                              dst_ref=got_refs[b].at[:, pl.ds(st, sz), :], send_sem=send_sems.at[k], recv_sem=recv_sems.at[k],
                                              device_id=(x, y, 1 - c), device_id_type=MESH_ID)
                 for k, (b, st, sz) in enumerate(pieces)]
        return swaps, swaps

    return Plan(gs, [SDS((g.shape[0], g.shape[1] // 2, g.shape[2]), g.dtype) for g in gs], {}, len(pieces), copies)


def plan_chip_scatter(ps):
    pieces = _pieces([(p.shape, p.dtype) for p in ps], lambda shape: shape[1])
    n = len(pieces)

    def copies(p_refs, q_refs, send_sems, recv_sems):
        x, y, c, chips = _place()
        mine = 2 * x + y

        def copy(j, k, src_slot, dst_slot, to):
            b, st, sz = pieces[k]
            return pltpu.make_async_remote_copy(src_ref=p_refs[b].at[src_slot, pl.ds(st, sz), :], dst_ref=q_refs[b].at[dst_slot, pl.ds(st, sz), :],
                                                send_sem=send_sems.at[j * n + k], recv_sem=recv_sems.at[j * n + k], device_id=to,
                                                device_id_type=MESH_ID)

        order = [(k, j, 2 * cx + cy, (cx, cy, c)) for k in range(n) for j, (cx, cy) in enumerate(chips)]
        return [copy(j, k, chip, mine, to) for k, j, chip, to in order], [copy(j, k, mine, chip, to) for k, j, chip, to in order]

    return Plan(ps, [SDS(p.shape, p.dtype) for p in ps], {}, 3 * n, copies)


def plan_pair_join(bufs):
    pieces = _pieces([(b.shape, b.dtype) for b in bufs], lambda shape: shape[0] // 2)

    def copies(_, out_refs, send_sems, recv_sems):
        x, y, c, _ = _place()

        def copy(k, cc):
            b, st, sz = pieces[k]
            rows_k = out_refs[b].at[pl.ds(cc * (out_refs[b].shape[0] // 2) + st, sz), :]
            return pltpu.make_async_remote_copy(src_ref=rows_k, dst_ref=rows_k, send_sem=send_sems.at[k], recv_sem=recv_sems.at[k],
                                                device_id=(x, y, 1 - c), device_id_type=MESH_ID)

        return [copy(k, c) for k in range(len(pieces))], [copy(k, 1 - c) for k in range(len(pieces))]

    return Plan(bufs, [SDS(b.shape, b.dtype) for b in bufs], {i: i for i in range(len(bufs))}, len(pieces), copies)


def run_plan(name, plan):
    hosted_call(lambda: None, name=name, grid=(1,), in_specs=[], out_specs=[], out_shape=[], operands=[], plans=[plan])
    return plan.results


HBM = pl.BlockSpec(memory_space=pltpu.HBM)
SEMS = pl.BlockSpec(memory_space=pltpu.SEMAPHORE)
SPLIT_PARAMS = dict(has_side_effects=pltpu.SideEffectType.DATAFLOW_SIDE_EFFECTING)


def _plan_buffers(plan):
    in_place = {o: i for i, o in plan.aliases.items()}
    bufs = [pltpu.with_memory_space_constraint(a, pltpu.HBM) for a in plan.operands]
    where = []
    for o, sd in enumerate(plan.out_shape):
        if o in in_place:
            where.append(in_place[o])
        else:
            where.append(len(bufs))
            bufs.append(pltpu.with_memory_space_constraint(lax.empty(sd.shape, sd.dtype), pltpu.HBM))
    return bufs, where


def split_start(name, plans):
    layout = [_plan_buffers(p) for p in plans]
    counts = [len(b) for b, _ in layout]
    n_buf = sum(counts)

    def body(*refs):
        sems, token = refs[n_buf:n_buf + 2 * len(plans)], refs[-1]
        pos = 0
        for k, (p, (_, where)) in enumerate(zip(plans, layout)):
            mine = refs[pos:pos + counts[k]]
            pos += counts[k]
            sends, _ = p.copies(mine[:len(p.operands)], [mine[w] for w in where], sems[2 * k], sems[2 * k + 1])
            for cp in sends:
                cp.start()
        token[...] = jnp.zeros_like(token)

    bufs = [b for bs, _ in layout for b in bs]
    res = pl.pallas_call(
        body, name=name, in_specs=[HBM] * n_buf,
        out_specs=[SEMS] * (2 * len(plans)) + [HBM] * n_buf + [pl.BlockSpec(memory_space=pltpu.VMEM)],
        out_shape=[pltpu.SemaphoreType.DMA((p.n_sems,)) for p in plans for _ in range(2)] + [pltpu.HBM(b.shape, b.dtype) for b in bufs]
        + [SDS((8, 128), F32)],
        input_output_aliases={i: 2 * len(plans) + i for i in range(n_buf)}, compiler_params=pltpu.CompilerParams(**SPLIT_PARAMS))(*bufs)
    pos = 2 * len(plans)
    for k, p in enumerate(plans):
        p.in_flight = (res[2 * k], res[2 * k + 1], list(res[pos:pos + counts[k]]), layout[k][1])
        pos += counts[k]
    return res[-1]


def split_wait(name, plan, after):
    send_sems, recv_sems, bufs, where = plan.in_flight
    n_buf = len(bufs)

    def body(*refs):
        mine = refs[:n_buf]
        sends, recvs = plan.copies(mine[:len(plan.operands)], [mine[w] for w in where], refs[n_buf], refs[n_buf + 1])
        for cp in recvs:
            cp.wait_recv()
        for cp in sends:
            cp.wait_send()

    res = pl.pallas_call(body, name=name, in_specs=[HBM] * n_buf + [SEMS, SEMS, ANY], out_specs=[HBM] * n_buf,
                         out_shape=[pltpu.HBM(b.shape, b.dtype) for b in bufs], input_output_aliases={i: i for i in range(n_buf)},
                         compiler_params=pltpu.CompilerParams(**SPLIT_PARAMS))(*bufs, send_sems, recv_sems, after)
    plan.results = [res[w] for w in where]
    return plan.results


def pair_add(name, g, got, place):
    slots, rows, cols = g.shape
    half = rows // 2
    tr = _row_tile(half)
    nb = half // tr

    def kern(_, g_ref, t_ref, o_ref):
        o_ref[...] = (g_ref[...].astype(F32) + t_ref[...].astype(F32)).astype(o_ref.dtype)

    blk = pl.BlockSpec((None, tr, cols), lambda s, i, p: (s, i, 0))
    grid_spec = pltpu.PrefetchScalarGridSpec(
        num_scalar_prefetch=1, grid=(slots, nb),
        in_specs=[pl.BlockSpec((None, tr, cols), lambda s, i, p: (s, p[1] * nb + i, 0)), blk], out_specs=blk)
    return pl.pallas_call(kern, grid_spec=grid_spec, out_shape=SDS((slots, half, cols), g.dtype), name=name,
                          compiler_params=_cparams())(place, g, got)


def chip_add(name, p, q, place):
    slots, half, cols = p.shape
    tr = _row_tile(half)
    nb = half // tr

    def kern(_, p_ref, q1, q2, q3, o_ref):
        o_ref[...] = p_ref[...].astype(F32) + q1[...].astype(F32) + q2[...].astype(F32) + q3[...].astype(F32)

    def slot(k):
        return pl.BlockSpec((None, tr, cols), lambda i, pr: ((pr[0] + k) % slots, i, 0))

    grid_spec = pltpu.PrefetchScalarGridSpec(
        num_scalar_prefetch=1, grid=(nb,), in_specs=[slot(0), slot(1), slot(2), slot(3)],
        out_specs=pl.BlockSpec((tr, cols), lambda i, pr: (pr[1] * nb + i, 0)))
    return pl.pallas_call(kern, grid_spec=grid_spec, out_shape=SDS((2 * half, cols), F32), name=name,
                          compiler_params=_cparams())(place, p, q, q, q)


def pair_adds(tag, gs, gots, place):
    return [pair_add(f"{tag}_pair_add_{i}", g, got, place) for i, (g, got) in enumerate(zip(gs, gots))]


def chip_adds(tag, pairs, qs, place):
    return [chip_add(f"{tag}_chip_add_{i}", p, q, place) for i, (p, q) in enumerate(zip(pairs, qs))]


def reduce_scatter_chips(tag, gs, place):
    pairs = pair_adds(tag, gs, run_plan(tag + "_pair_exchange", plan_pair_exchange(gs)), place)
    return run_plan(tag + "_pair_join", plan_pair_join(chip_adds(tag, pairs, run_plan(tag + "_chip_scatter", plan_chip_scatter(pairs)), place)))


BIG = [("w_out", (2, 512, 1024)), ("w_mem_kv", (2, 256, 1024)), ("s5_w_in", (1, 1024, 1024)), ("s5_w_glu", (1, 1536, 768)),
       ("mla_w_in", (1, 1024, 848)), ("mla_w_uq", (1, 512, 576)), ("mla_w_ukv", (1, 256, 768))]
SHARDED_SMALL = [("mla_q_lora_norm", (1, 128)), ("mla_kv_lora_norm", (1, 64))]
SMALL = [("ln_gain", (2, 1024)), ("mem_norm", (2, 1024)), ("xq_norm", (2, 128)), ("xk_norm", (2, 128)),
         ("s5_lambda_re", (1, 96, 64)), ("s5_lambda_im", (1, 96, 64)), ("s5_log_step", (1, 96)),
         ("s5_b_re", (1, 96, 64, 16)), ("s5_b_im", (1, 96, 64, 16)), ("s5_c_re", (1, 96, 16, 64)), ("s5_c_im", (1, 96, 16, 64)),
         ("s5_d", (1, 1536)), ("mla_q_nope_norm", (1, 128)), ("mla_k_nope_norm", (1, 128)), ("mla_q_rope_norm", (1, 64)),
         ("mla_k_rope_norm", (1, 64))]
WEIGHT_ORDER = ["ln_gain", "w_out", "mem_norm", "w_mem_kv", "xq_norm", "xk_norm", "s5_w_in", "s5_lambda_re", "s5_lambda_im",
                "s5_log_step", "s5_b_re", "s5_b_im", "s5_c_re", "s5_c_im", "s5_d", "s5_w_glu", "mla_w_in", "mla_q_lora_norm",
                "mla_kv_lora_norm", "mla_w_uq", "mla_w_ukv", "mla_q_nope_norm", "mla_k_nope_norm", "mla_q_rope_norm", "mla_k_rope_norm"]
MINOR_LAST = {"mla_w_in": (0, 2, 1), "mla_w_uq": (0, 2, 1), "s5_b_re": (0, 2, 3, 1), "s5_b_im": (0, 2, 3, 1),
              "s5_c_re": (0, 2, 3, 1), "s5_c_im": (0, 2, 3, 1)}
SMALL_FULL = SMALL + [(n, (1, 4 * s[1])) for n, s in SHARDED_SMALL]
N_SMALL = sum(math.prod(s) for _, s in SMALL_FULL)
SMALL_ROWS, SMALL_LANES = 128, 1024

PAIR_OUT, PAIR_MKV, PAIR_ROWS = 0, 512, 768


def stack_shards(w, dtype):
    pairs = [jnp.concatenate([w["w_out"][l], w["w_mem_kv"][l]], axis=0).astype(dtype) for l in range(2)]
    return ([w["s5_w_in"][0].astype(dtype)], [pairs[0], w["s5_w_glu"][0].astype(dtype)],
            [pairs[1], w["mla_w_ukv"][0].astype(dtype), jnp.pad(w["mla_w_in"][0].T.astype(dtype), ((0, IN1T_ROWS - MLA_IN // 4), (0, 0))),
             w["mla_w_uq"][0].T.astype(dtype)])


def pair_views(pair):
    return {"w_out": Sharded(pair, "row", PAIR_OUT, 512), "w_mem_kv": Sharded(pair, "row", PAIR_MKV, 256)}


def grad_views():
    pair = SDS((4, PAIR_ROWS, 1024), BF16)
    return {"w_out": Sharded(pair, "row", PAIR_OUT, 512), "w_mem_kv": Sharded(pair, "row", PAIR_MKV, 256),
            "s5_w_in": Sharded(SDS((4, 1024, 1024), BF16), "col", 0, 1024), "s5_w_glu": Sharded(SDS((4, 1536, 768), BF16), "col", 0, 1536),
            "mla_w_ukv": Sharded(SDS((4, 256, 768), BF16), "col", 0, 256)}


IN1T_ROWS = 864


def in1_rows_permute(wt):
    o1, o2, o3, o4 = QL, QL + KVL, QL + KVL + ROPE, QL + KVL + ROPE + XQW
    return jnp.concatenate([wt[o4:], wt[:o1], wt[o3:o4], wt[o1:o2], wt[o2:o3], jnp.zeros((MLA_IN_P - MLA_IN, wt.shape[1]), wt.dtype)], axis=0)


def in1_rows_unpermute(d):
    return jnp.concatenate([d[2048:2560], d[3072:3328], d[3328:3392], d[2560:3072], d[:2048]], axis=0)


def uq_rows_permute(wt):
    w3 = wt.reshape(MH, NOPE + ROPE, wt.shape[1])
    return jnp.concatenate([w3[:, :NOPE].reshape(MH * NOPE, wt.shape[1]), w3[:, NOPE:].reshape(MH * ROPE, wt.shape[1])], axis=0)


def uq_rows_unpermute(d):
    dn = d[:MH * NOPE].reshape(MH, NOPE, d.shape[1])
    dr = d[MH * NOPE:].reshape(MH, ROPE, d.shape[1])
    return jnp.concatenate([dn, dr], axis=1).reshape(MH * (NOPE + ROPE), d.shape[1])


def time_permute(a):
    return a.reshape(SEG, SEG_LEN, a.shape[-1]).transpose(1, 0, 2).reshape(L, a.shape[-1])


def time_unpermute(a):
    return a.reshape(SEG_LEN, SEG, a.shape[-1]).transpose(1, 0, 2).reshape(L, a.shape[-1])


def mem_branch_fwd(tag, mem, mem_norm, w_mem_kv, xk_norm):
    mn = row_fwd(tag + "_mem_rms", fn_rms, ML, ML, [(mem, D, 0)], [mem_norm], [(D, BF16)])[0]
    kv = matmul(tag + "_mem_kv", mn, w_mem_kv, "nn", F32)
    kn = row_fwd(tag + "_mem_knorm", fn_mem_k, ML, ML, [(kv, XQW, 0)], [xk_norm], [(XQW, F32)])[0]
    return mn, kv, kn


def mem_branch_bwd(tag, mem, mem_norm, w_mem_kv, xk_norm, mn, kv, dkn, dv, g_view, g_wide):
    dk, dxk = row_bwd(tag + "_mem_knorm_bwd", fn_mem_k, ML, ML, [(kv, XQW, 0)], [xk_norm], [(dkn, XQW, 0)], [True], [True])
    dkv = jnp.concatenate([dk, dv], axis=1)
    dmn = matmul(tag + "_mem_kv_dx", dkv, w_mem_kv, "nt", F32)
    g_wide = matmul(tag + "_mem_kv_dw", mn, dkv, "tn", out=g_view, into=g_wide)
    dmem_norm = row_bwd(tag + "_mem_rms_bwd", fn_rms, ML, ML, [(mem, D, 0)], [mem_norm], [(dmn, D, 0)], [False], [True])[0]
    return g_wide, dmem_norm, dxk


def mem_attn_fwd(tag, proj, cb, kn, kv, xq_norm):
    return row_fwd(tag + "_mem_attn", fn_mem_attn, L, ROW_TILE, [(proj, XQW, cb)], [kn, kv[:, XQW:], xq_norm], [(XQW, F32)])[0]


def mem_attn_bwd(tag, proj, cb, kn, kv, xq_norm, dmo, dproj):
    place = {"cols": proj.shape[1], "cb": cb, "into": dproj, "dtype": dproj.dtype}
    return row_bwd(tag + "_mem_attn_bwd", fn_mem_attn, L, ROW_TILE, [(proj, XQW, cb)], [kn, kv[:, XQW:], xq_norm], [(dmo, XQW, 0)],
                   [place], [True, True, True])


def device_step(x, mem, positions, target, small, env, hooks=None):
    hooks = hooks or {}

    def plans_for(name):
        return hooks[("plans", name)](env) if ("plans", name) in hooks else ()

    def around(when, name, last=None):
        if (when, name) in hooks:
            hooks[(when, name)](env, last)

    g = {}
    gw = grad_views()
    ln, mem_norm, xq_norm, xk_norm = small["ln_gain"], small["mem_norm"], small["xq_norm"], small["xk_norm"]

    lre, lim = small["s5_lambda_re"][0], small["s5_lambda_im"][0]
    ls = small["s5_log_step"].reshape(SG, 1)
    one = pl.BlockSpec((SG, SP), lambda i: (0, 0))
    col = pl.BlockSpec((SG, 1), lambda i: (0, 0))
    disc_ins = [(lre, one), (lim, one), (ls, col)]
    a_re, a_im, coef_re, coef_im = stage("s5_disc", fn_s5_disc, (1,), disc_ins, [(SDS((SG, SP), F32), one)] * 4)
    b_re, b_im = small["s5_b_re"].reshape(SN, SC), small["s5_b_im"].reshape(SN, SC)
    c_re, c_im = small["s5_c_re"].reshape(PW, SP), small["s5_c_im"].reshape(PW, SP)
    bmat_rows = [(b_re, SC, 0), (b_im, SC, 0), (coef_re.reshape(SN, 1), 1, 0), (coef_im.reshape(SN, 1), 1, 0)]
    wb_re, wb_im = row_fwd("s5_bmat", fn_s5_bmat, SN, BMAT_TILE, bmat_rows, [], [(128, F32)] * 2)
    cmat_rows = [(c_re, SP, 0), (c_im, SP, 0)]
    wc_re, wc_im = row_fwd("s5_cmat", fn_s5_cmat, PW, CMAT_TILE, cmat_rows, [], [(512, F32)] * 2)
    a_re_v, a_im_v = a_re.reshape(1, SN), a_im.reshape(1, SN)
    s5_d = small["s5_d"]

    xp = time_permute(x)
    h0 = row_fwd("l0_rms", fn_rms, L, ROW_TILE, [(xp, D, 0)], [ln[0:1]], [(D, BF16)])[0]
    around("before", "l0_in", wb_re)
    w_in0 = Sharded(env["in0"], "col", 0, 1024)
    proj0 = matmul("l0_in", h0, w_in0, "nn")
    s_re, s_im, g0 = s5_forward(proj0, wb_re, wb_im, wc_re, wc_im, a_re_v, a_im_v, s5_d, plans=plans_for("s5_forward"))
    around("before", "l0_glu", g0)
    w0 = dict(pair_views(env["pair0"]), s5_w_glu=Sharded(env["glu"], "col", 0, 1536))
    z0 = matmul("l0_glu", g0, w0["s5_w_glu"], "nn", plans=plans_for("l0_glu"))
    mn0, kv0, kn0 = mem_branch_fwd("l0", mem, mem_norm[0:1], w0["w_mem_kv"], xk_norm[0:1])
    mo0 = mem_attn_fwd("l0", proj0, 3, kn0, kv0, xq_norm[0:1])
    o0 = row_fwd("l0_merge", fn_merge_glu, L, ROW_TILE, [(z0, 2 * PW, 0), (mo0, XQW, 0), (proj0, BW, 1)], [], [(BW, BF16)])[0]
    around("before", "l0_out", o0)
    x1p = matmul("l0_out", o0, w0["w_out"], "nn", F32, add=xp, plans=plans_for("l0_out"))
    around("after", "l0_out", x1p)
    x1 = time_unpermute(x1p)

    w1 = dict(pair_views(env["pair1"]), mla_w_ukv=Sharded(env["ukv"], "col", 0, 256))
    w_in1, w_uq = env["w_in1"], env["w_uq"]
    h1 = row_fwd("l1_rms", fn_rms, L, ROW_TILE, [(x1, D, 0)], [ln[1:2]], [(D, BF16)])[0]
    proj1 = matmul("l1_in", h1, w_in1, "nt")
    qln, kvln = env["q_lora_norm"].reshape(1, QL), env["kv_lora_norm"].reshape(1, KVL)
    cqn = row_fwd("l1_q_lora_rms", fn_rms, L, ROW_TILE, [(proj1, QL, 4)], [qln], [(QL, BF16)])[0]
    ckvn = row_fwd("l1_kv_lora_rms", fn_rms, L, ROW_TILE, [(proj1, KVL, 12)], [kvln], [(KVL, BF16)])[0]
    q = matmul("l1_uq", cqn, w_uq, "nt")
    kv = matmul("l1_ukv", ckvn, w1["mla_w_ukv"], "nn")
    inv_freq = ROPE_THETA ** (-jnp.arange(ROPE // 2, dtype=F32) / (ROPE // 2))
    ang = positions.astype(F32)[:, None] * inv_freq
    cos2 = jnp.tile(jnp.cos(ang), (1, 4))
    sin_signed = jnp.tile(jnp.concatenate([-jnp.sin(ang), jnp.sin(ang)], axis=1), (1, 2))
    qnn, knn = small["mla_q_nope_norm"], small["mla_k_nope_norm"]
    qrn, krn = jnp.tile(small["mla_q_rope_norm"], (1, 2)), jnp.tile(small["mla_k_rope_norm"], (1, 2))
    tp = 256
    prep_ins = [(q, rspec(tp, MH * (NOPE + ROPE))), (kv, rspec(tp, MH * 256)), (proj1, rspec(tp, 128, 26)),
                (cos2, rspec(tp, 128)), (sin_signed, rspec(tp, 128))] + [(a, cspec((1, 128))) for a in (qnn, knn, qrn, krn)]
    hq_spec = pl.BlockSpec((MH, tp, 256), lambda i: (0, i, 0))
    hv_spec = pl.BlockSpec((MH, tp, 128), lambda i: (0, i, 0))
    qf, kf, vh = stage("l1_mla_prep", fn_mla_prep, (L // tp,), prep_ins,
                       [(SDS((MH, L, 256), BF16), hq_spec), (SDS((MH, L, 256), BF16), hq_spec), (SDS((MH, L, 128), BF16), hv_spec)])
    attn, attn_lse = causal_attn(qf, kf, vh)
    mn1, kv1, kn1 = mem_branch_fwd("l1", mem, mem_norm[1:2], w1["w_mem_kv"], xk_norm[1:2])
    mo1 = mem_attn_fwd("l1", proj1, 5, kn1, kv1, xq_norm[1:2])
    o1 = row_fwd("l1_merge", fn_merge, L, ROW_TILE, [(attn, PW, 0), (mo1, XQW, 0), (proj1, BW, 0)], [], [(BW, BF16)])[0]
    x2 = matmul("l1_out", o1, w1["w_out"], "nn", F32, add=x1)
    dx2, loss = loss_and_grad(x2, target)
    around("after", "loss", loss)

    do1 = matmul("l1_out_dx", dx2, w1["w_out"], "nt")
    g_pair1 = matmul("l1_out_dw", o1, dx2, "tn", out=gw["w_out"])
    dattn, dmo1, dproj1 = row_bwd("l1_merge_bwd", fn_merge, L, ROW_TILE, [(attn, PW, 0), (mo1, XQW, 0), (proj1, BW, 0)], [],
                                  [(do1, BW, 0)], [True, True, {"cols": MLA_IN_P, "cb": 0, "dtype": BF16}], [])
    dproj1, dkn1, dv1, dxqn1 = mem_attn_bwd("l1", proj1, 5, kn1, kv1, xq_norm[1:2], dmo1, dproj1)
    env["g_pair1"], dmem_norm1, dxk1 = mem_branch_bwd("l1", mem, mem_norm[1:2], w1["w_mem_kv"], xk_norm[1:2], mn1, kv1, dkn1, dv1,
                                                      gw["w_mem_kv"], g_pair1)
    dqf, dkf, dvh = causal_attn_bwd(qf, kf, vh, attn, attn_lse, dattn)
    prep_diffs = [("row", SDS((L, MH * (NOPE + ROPE)), BF16), rspec(tp, MH * (NOPE + ROPE))), ("row", SDS((L, MH * 256), BF16), rspec(tp, MH * 256)),
                  ("row", SDS((L, MLA_IN_P), BF16), rspec(tp, 128, 26), {"into": dproj1}), None, None] + [("acc", (0,))] * 4
    dq, dkv, dproj1, dqnn, dknn, dqrn, dkrn = stage_bwd("l1_mla_prep_bwd", fn_mla_prep, (L // tp,), prep_ins,
                                                        [(dqf, hq_spec), (dkf, hq_spec), (dvh, hv_spec)], prep_diffs)
    dcqn = matmul("l1_uq_dx", dq, w_uq, "nn")
    env["dw_uq"] = matmul("l1_uq_dw", dq, cqn, "tn")
    dckvn = matmul("l1_ukv_dx", dkv, w1["mla_w_ukv"], "nt")
    env["g_ukv"] = matmul("l1_ukv_dw", ckvn, dkv, "tn", out=gw["mla_w_ukv"])
    dproj1, dqln = row_bwd("l1_q_lora_rms_bwd", fn_rms, L, ROW_TILE, [(proj1, QL, 4)], [qln], [(dcqn, QL, 0)],
                           [{"cols": MLA_IN_P, "cb": 4, "into": dproj1, "dtype": BF16}], [True])
    dproj1, dkvln = row_bwd("l1_kv_lora_rms_bwd", fn_rms, L, ROW_TILE, [(proj1, KVL, 12)], [kvln], [(dckvn, KVL, 0)],
                            [{"cols": MLA_IN_P, "cb": 12, "into": dproj1, "dtype": BF16}], [True])
    env["dw_in1"] = matmul("l1_in_dw", dproj1, h1, "tn")
    dh1 = matmul("l1_in_dx", dproj1, w_in1, "nn", plans=plans_for("l1_in_dx"))
    around("after", "l1_in_dx", dh1)
    dx1, dln1 = row_bwd("l1_rms_bwd", fn_rms, L, ROW_TILE, [(x1, D, 0)], [ln[1:2]], [(dh1, D, 0)], [{"add": (dx2, D, 0)}], [True])
    dx1p = time_permute(dx1)

    do0 = matmul("l0_out_dx", dx1p, w0["w_out"], "nt", plans=plans_for("l0_out_dx"))
    g_pair0 = matmul("l0_out_dw", o0, dx1p, "tn", out=gw["w_out"])
    dz0, dmo0, dproj0 = row_bwd("l0_merge_bwd", fn_merge_glu, L, ROW_TILE, [(z0, 2 * PW, 0), (mo0, XQW, 0), (proj0, BW, 1)], [],
                                [(do0, BW, 0)], [{"dtype": BF16}, True, {"cols": 2 * BW, "cb": 1, "dtype": BF16}], [])
    dproj0, dkn0, dv0, dxqn0 = mem_attn_bwd("l0", proj0, 3, kn0, kv0, xq_norm[0:1], dmo0, dproj0)
    env["g_pair0"], dmem_norm0, dxk0 = mem_branch_bwd("l0", mem, mem_norm[0:1], w0["w_mem_kv"], xk_norm[0:1], mn0, kv0, dkn0, dv0,
                                                      gw["w_mem_kv"], g_pair0)
    env["g_glu"] = matmul("l0_glu_dw", g0, dz0, "tn", out=gw["s5_w_glu"], plans=plans_for("l0_glu_dw"))
    dg0 = matmul("l0_glu_dx", dz0, w0["s5_w_glu"], "nt", plans=plans_for("l0_glu_dx"))
    around("before", "s5_backward", dg0)
    dproj0, dd, dwc_re, dwc_im, dwb_re, dwb_im, da_re, da_im = s5_backward(dg0, proj0, s_re, s_im, wb_re, wb_im, wc_re, wc_im,
                                                                           a_re_v, a_im_v, s5_d, dproj0, plans=plans_for("s5_backward"))
    around("after", "s5_backward", dd)
    env["g_in0"] = matmul("l0_in_dw", h0, dproj0, "tn", out=gw["s5_w_in"], plans=plans_for("l0_in_dw"))
    dh0 = matmul("l0_in_dx", dproj0, w_in0, "nt", plans=plans_for("l0_in_dx"))
    around("after", "l0_in_dx", dh0)
    dxp, dln0 = row_bwd("l0_rms_bwd", fn_rms, L, ROW_TILE, [(xp, D, 0)], [ln[0:1]], [(dh0, D, 0)], [{"add": (dx1p, D, 0)}], [True])
    grad_x = time_unpermute(dxp)

    db_re, db_im, dcoef_re, dcoef_im = row_bwd("s5_bmat_bwd", fn_s5_bmat, SN, BMAT_TILE, bmat_rows, [], [(dwb_re, 128, 0), (dwb_im, 128, 0)],
                                               [True] * 4, [], plans=plans_for("s5_bmat_bwd"))
    dc_re, dc_im = row_bwd("s5_cmat_bwd", fn_s5_cmat, PW, CMAT_TILE, cmat_rows, [], [(dwc_re, 512, 0), (dwc_im, 512, 0)], [True] * 2, [],
                           plans=plans_for("s5_cmat_bwd"))
    disc_cts = [(da_re.reshape(SG, SP), one), (da_im.reshape(SG, SP), one), (dcoef_re.reshape(SG, SP), one), (dcoef_im.reshape(SG, SP), one)]
    dlre, dlim, dls = stage_bwd("s5_disc_bwd", fn_s5_disc, (1,), disc_ins, disc_cts, [("acc", (0,))] * 3)

    g["ln_gain"] = jnp.concatenate([dln0, dln1], axis=0)
    g["mem_norm"] = jnp.concatenate([dmem_norm0, dmem_norm1], axis=0)
    g["xq_norm"] = jnp.concatenate([dxqn0, dxqn1], axis=0)
    g["xk_norm"] = jnp.concatenate([dxk0, dxk1], axis=0)
    g["s5_lambda_re"], g["s5_lambda_im"], g["s5_log_step"] = dlre, dlim, dls
    g["s5_b_re"], g["s5_b_im"], g["s5_c_re"], g["s5_c_im"] = db_re, db_im, dc_re, dc_im
    g["s5_d"] = dd
    g["mla_q_lora_norm"], g["mla_kv_lora_norm"] = dqln, dkvln
    g["mla_q_nope_norm"], g["mla_k_nope_norm"] = dqnn, dknn
    g["mla_q_rope_norm"] = dqrn[:, :ROPE] + dqrn[:, ROPE:]
    g["mla_k_rope_norm"] = dkrn[:, :ROPE] + dkrn[:, ROPE:]
    return loss, grad_x, g


def kernel(x, mem, positions, ln_gain, w_out, mem_norm, w_mem_kv, xq_norm, xk_norm, s5_w_in, s5_lambda_re, s5_lambda_im, s5_log_step, s5_b_re, s5_b_im, s5_c_re, s5_c_im, s5_d, s5_w_glu, mla_w_in, mla_q_lora_norm, mla_kv_lora_norm, mla_w_uq, mla_w_ukv, mla_q_nope_norm, mla_k_nope_norm, mla_q_rope_norm, mla_k_rope_norm, loss_target, m_ln_gain, m_w_out, m_mem_norm, m_w_mem_kv, m_xq_norm, m_xk_norm, m_s5_w_in, m_s5_lambda_re, m_s5_lambda_im, m_s5_log_step, m_s5_b_re, m_s5_b_im, m_s5_c_re, m_s5_c_im, m_s5_d, m_s5_w_glu, m_mla_w_in, m_mla_q_lora_norm, m_mla_kv_lora_norm, m_mla_w_uq, m_mla_w_ukv, m_mla_q_nope_norm, m_mla_k_nope_norm, m_mla_q_rope_norm, m_mla_k_rope_norm, v_ln_gain, v_w_out, v_mem_norm, v_w_mem_kv, v_xq_norm, v_xk_norm, v_s5_w_in, v_s5_lambda_re, v_s5_lambda_im, v_s5_log_step, v_s5_b_re, v_s5_b_im, v_s5_c_re, v_s5_c_im, v_s5_d, v_s5_w_glu, v_mla_w_in, v_mla_q_lora_norm, v_mla_kv_lora_norm, v_mla_w_uq, v_mla_w_ukv, v_mla_q_nope_norm, v_mla_k_nope_norm, v_mla_q_rope_norm, v_mla_k_rope_norm):
    args = dict(locals())
    wts = {n: args[n] for n in WEIGHT_ORDER}
    mom = {n: args["m_" + n] for n in WEIGHT_ORDER}
    var = {n: args["v_" + n] for n in WEIGHT_ORDER}

    chip = 2 * lax.axis_index("x") + lax.axis_index("y")
    place = jnp.stack([chip, lax.axis_index("c")]).astype(jnp.int32)

    def own_slot(gathered, shards):
        return [lax.dynamic_update_slice(g, s[None], (chip, 0, 0)) for g, s in zip(gathered, shards)]

    groups = list(stack_shards(wts, BF16))
    groups[2].append(jnp.concatenate([mla_q_lora_norm, jnp.pad(mla_kv_lora_norm, ((0, 0), (0, 64))), jnp.zeros((14, 128), F32)], axis=0))
    over_ici = [plan_gather_ici(shards) for shards in groups]
    _SCHEDULE_BEHIND.clear()
    schedule_behind(split_start("gather_start", over_ici))
    env, hooks, passed_on = {}, {}, {}

    def arrived(k, after, pass_now):
        passing = plan_gather_pass(split_wait(f"gather_wait_{k}", over_ici[k], after))
        passed_on[k] = passing
        return own_slot(run_plan(f"gather_pass_{k}", passing), groups[k]) if pass_now else None

    def need_in0(env, last):
        env["in0"], = arrived(0, last, True)

    def need_layer0(env, last):
        env["pair0"], env["glu"] = arrived(1, last, True)

    def need_layer1(env, last):
        arrived(2, last, False)

    def layer1_weights(env, last):
        pair1, ukv, in1, uq, norms = own_slot(passed_on[2].results, groups[2])
        env.update(pair1=pair1, ukv=ukv, w_in1=in1_rows_permute(in1[:, :MLA_IN // 4].reshape(MLA_IN, D)), w_uq=uq_rows_permute(uq.reshape(-1, QL)),
                   q_lora_norm=norms[:, 0, :], kv_lora_norm=norms[:, 1, :64])

    hooks["before", "l0_in"], hooks["before", "l0_glu"], hooks["before", "l0_out"] = need_in0, need_layer0, need_layer1
    hooks["plans", "l0_out"], hooks["after", "l0_out"] = (lambda env: [passed_on[2]]), layer1_weights

    rs = {}

    def swap(k, gs):
        rs[k, "g"], rs[k, "swap"] = gs, plan_pair_exchange(gs)
        return rs[k, "swap"]

    def start_scatter(k):
        rs[k, "pairs"] = pair_adds(f"rs{k}", rs[k, "g"], rs[k, "swap"].results, place)
        rs[k, "scatter"] = plan_chip_scatter(rs[k, "pairs"])
        schedule_behind(split_start(f"rs{k}_scatter_start", [rs[k, "scatter"]]))

    def join(k, after):
        rs[k, "join"] = plan_pair_join(chip_adds(f"rs{k}", rs[k, "pairs"], split_wait(f"rs{k}_scatter_wait", rs[k, "scatter"], after), place))
        return rs[k, "join"]

    def layer1_gradients(env):
        g_in1 = jnp.pad(in1_rows_unpermute(env["dw_in1"]).reshape(4, MLA_IN // 4, D), ((0, 0), (0, IN1T_ROWS - MLA_IN // 4), (0, 0)))
        return [swap(1, [env["g_pair1"], env["g_ukv"], g_in1, uq_rows_unpermute(env["dw_uq"]).reshape(4, -1, QL)])]

    hooks["plans", "l1_in_dx"] = layer1_gradients
    hooks["after", "l1_in_dx"] = lambda env, last: start_scatter(1)
    hooks["plans", "l0_glu_dx"] = lambda env: [swap(0, [env["g_pair0"], env["g_glu"]])]

    def before_s5_backward(env, last):
        start_scatter(0)
        env["join1"] = join(1, last)

    hooks["before", "s5_backward"] = before_s5_backward
    hooks["plans", "s5_backward"] = lambda env: [env["join1"]]
    hooks["after", "s5_backward"] = lambda env, last: env.update(join0=join(0, last))
    hooks["plans", "l0_in_dx"] = lambda env: [swap(2, [env["g_in0"]]), env["join0"]]
    hooks["after", "l0_in_dx"] = lambda env, last: start_scatter(2)

    def total_loss(env, local):
        env["loss"] = lax.psum(local[0, 0], MESH_AXES)
        schedule_behind(env["loss"].reshape(1, 1))

    hooks["after", "loss"] = total_loss
    small = {n: wts[n] for n, _ in SMALL}
    loss, grad_x, g = device_step(x[0], mem[0], positions[0], loss_target[0], small, env, hooks)
    loss = env["loss"]
    r_in0, = run_plan("rs2_pair_join", join(2, g["s5_log_step"]))
    (r_pair1, r_ukv, r_in1, r_uq), (r_pair0, r_glu) = (rs[k, "join"].results for k in (1, 0))

    small_flat = jnp.concatenate([g[n].reshape(-1) for n, _ in SMALL_FULL])
    g_small = jnp.pad(small_flat, (0, 4 * SMALL_ROWS * SMALL_LANES - N_SMALL)).astype(BF16).reshape(4, SMALL_ROWS, SMALL_LANES)
    r_small = reduce_scatter_chips("rs3", [g_small], place)[0]
    small_all = own_slot(all_gather_chips("gather_small_grads", [r_small]), [r_small])[0].reshape(-1)[:N_SMALL]

    grads = {"w_out": jnp.stack([r_pair0[:PAIR_MKV], r_pair1[:PAIR_MKV]]), "w_mem_kv": jnp.stack([r_pair0[PAIR_MKV:], r_pair1[PAIR_MKV:]]),
             "s5_w_in": r_in0[None], "s5_w_glu": r_glu[None], "mla_w_ukv": r_ukv[None], "mla_w_in": r_in1[:MLA_IN // 4].T[None], "mla_w_uq": r_uq.T[None]}
    off = 0
    for n, s in SMALL_FULL:
        grads[n] = small_all[off:off + math.prod(s)].reshape(s)
        off += math.prod(s)
    for n, s in SHARDED_SMALL:
        grads[n] = lax.dynamic_slice(grads[n], (0, chip * s[1]), s)

    delta, new_m, new_v = {}, {}, {}
    for n, s in BIG + [(n, s) for n, s in SMALL if len(s) == 4]:
        perm = MINOR_LAST.get(n, tuple(range(len(s))))
        turned = tuple(s[p] for p in perm)
        view = lambda a: jnp.transpose(a, perm).reshape(-1, turned[-1])
        res = adamw("adamw_" + n, view(wts[n]), view(grads[n]), view(mom[n]), view(var[n]))
        delta[n], new_m[n], new_v[n] = (jnp.transpose(r.reshape(turned), tuple(perm.index(i) for i in range(len(s)))) for r in res)
    small_names = [n for n, s in SMALL if len(s) < 4] + [n for n, _ in SHARDED_SMALL]
    n_own = sum(wts[n].size for n in small_names)
    rows_own = -(-n_own // (8 * 128)) * 8

    def pack_small(d):
        flat = jnp.concatenate([d[n].reshape(-1) for n in small_names])
        return jnp.pad(flat, (0, rows_own * 128 - n_own), constant_values=1.0).reshape(rows_own, 128)

    res = adamw("adamw_small", pack_small(wts), pack_small(grads), pack_small(mom), pack_small(var))
    off = 0
    for n in small_names:
        size = wts[n].size
        delta[n], new_m[n], new_v[n] = (r.reshape(-1)[off:off + size].reshape(wts[n].shape) for r in res)
        off += size

    return (loss, grad_x[None], *[grads[n] for n in WEIGHT_ORDER], *[delta[n] for n in WEIGHT_ORDER],
            *[new_m[n] for n in WEIGHT_ORDER], *[new_v[n] for n in WEIGHT_ORDER])
```

```python
import functools
import math

import jax
import jax.numpy as jnp
from jax import lax
from jax.experimental import pallas as pl
from jax.experimental.pallas import tpu as pltpu

F32, BF16 = jnp.float32, jnp.bfloat16
SDS = jax.ShapeDtypeStruct

D = 1024
L = 2048
ML = 256
BW = 2 * D
XQW = BW // 4
PW = BW - XQW
XH, XHD = 4, 128
SG, SC, SP = 96, 16, 64
SN = SG * SP
NOPE, ROPE, VD = 128, 64, 128
MH = 12
QL, KVL = 512, 256
EPS = 1e-6
ROPE_THETA = 10000.0
MLA_IN = QL + KVL + ROPE + XQW + BW
MLA_IN_P = 3456
ADAM_LR, ADAM_B1, ADAM_B2, ADAM_EPS, ADAM_WD, ADAM_STEP = 0.001, 0.9, 0.999, 1e-08, 0.01, 10

VMEM_LIMIT = 48 * 2**20
ROW_TILE = 512
BMAT_TILE, CMAT_TILE = 2048, 512
SEG = 8
SEG_LEN = L // SEG
MESH_AXES = ("x", "y", "c")


def _cparams():
    return pltpu.CompilerParams(vmem_limit_bytes=VMEM_LIMIT)


def _dg(a, b, ca, cb):
    return lax.dot_general(a.astype(BF16), b.astype(BF16), (((ca,), (cb,)), ((), ())), preferred_element_type=F32)


@jax.custom_vjp
def mm_nn(a, b):
    return _dg(a, b, 1, 0)


mm_nn.defvjp(lambda a, b: (_dg(a, b, 1, 0), (a, b)), lambda res, g: (_dg(g, res[1], 1, 1), _dg(res[0], g, 0, 0)))


@jax.custom_vjp
def mm_nt(a, b):
    return _dg(a, b, 1, 1)


mm_nt.defvjp(lambda a, b: (_dg(a, b, 1, 1), (a, b)), lambda res, g: (_dg(g, res[1], 1, 0), _dg(g, res[0], 0, 0)))


@functools.partial(jax.custom_vjp, nondiff_argnums=(1,))
def lane_roll(x, shift):
    return pltpu.roll(x, shift, 1)


lane_roll.defvjp(lambda x, shift: (pltpu.roll(x, shift, 1), None),
                 lambda shift, _, g: (pltpu.roll(g, (128 - shift) % 128, 1),))


def rms(x, g):
    return x * lax.rsqrt(jnp.mean(x * x, axis=-1, keepdims=True) + EPS) * g


@jax.custom_vjp
def softmax_rows(s):
    e = jnp.exp(s - jnp.max(s, axis=-1, keepdims=True))
    return e / jnp.sum(e, axis=-1, keepdims=True)


def _softmax_rows_fwd(s):
    p = softmax_rows(s)
    return p, p


def _softmax_rows_bwd(p, g):
    return (p * (g - jnp.sum(g * p, axis=-1, keepdims=True)),)


softmax_rows.defvjp(_softmax_rows_fwd, _softmax_rows_bwd)


def silu(x):
    return x * jax.nn.sigmoid(x)


ANY = pl.BlockSpec(memory_space=pl.ANY)
MESH_ID = pl.DeviceIdType.MESH


def _dma_sems(n):
    return [pltpu.SemaphoreType.DMA((n,)), pltpu.SemaphoreType.DMA((n,))]


class Plan:
    def __init__(self, operands, out_shape, aliases, n_sems, copies):
        self.operands, self.out_shape, self.aliases, self.n_sems, self.copies = list(operands), list(out_shape), aliases, n_sems, copies
        self.results = None


_SCHEDULE_BEHIND = []


def schedule_behind(token):
    _SCHEDULE_BEHIND.append(token)


def hosted_call(kern, *, name, grid, in_specs, out_specs, out_shape, operands, scratch_shapes=(), aliases=None, cparams=None, plans=(), deps=()):
    n_in, n_out, n_scr = len(in_specs), len(out_specs), len(scratch_shapes)
    p_in, p_out = [len(p.operands) for p in plans], [len(p.out_shape) for p in plans]
    deps = tuple(deps) + tuple(_SCHEDULE_BEHIND)
    _SCHEDULE_BEHIND.clear()
    all_aliases = dict(aliases or {})
    in_off, out_off = n_in, n_out
    for p, ni, no in zip(plans, p_in, p_out):
        all_aliases.update({in_off + i: out_off + o for i, o in p.aliases.items()})
        in_off, out_off = in_off + ni, out_off + no

    def body(*refs):
        pos, pins, pouts = n_in, [], []
        for ni in p_in:
            pins.append(refs[pos:pos + ni])
            pos += ni
        pos += len(deps)
        main_out = refs[pos:pos + n_out]
        pos += n_out
        for no in p_out:
            pouts.append(refs[pos:pos + no])
            pos += no
        main_scr = refs[pos:pos + n_scr]
        pos += n_scr
        if plans:
            ids = [pl.program_id(ax) for ax in range(len(grid))]
            first = functools.reduce(jnp.logical_and, [i == 0 for i in ids])
            last = functools.reduce(jnp.logical_and, [i == g - 1 for i, g in zip(ids, grid)])
            copies = [p.copies(pins[k], pouts[k], refs[pos + 2 * k], refs[pos + 2 * k + 1]) for k, p in enumerate(plans)]

            @pl.when(first)
            def _():
                for sends, _ in copies:
                    for cp in sends:
                        cp.start()

        kern(*refs[:n_in], *main_out, *main_scr)
        if plans:
            @pl.when(last)
            def _():
                for sends, recvs in copies:
                    for cp in recvs:
                        cp.wait_recv()
                    for cp in sends:
                        cp.wait_send()

    res = pl.pallas_call(body, grid=grid, in_specs=list(in_specs) + [ANY] * (sum(p_in) + len(deps)),
                         out_specs=list(out_specs) + [ANY] * sum(p_out), out_shape=list(out_shape) + [s for p in plans for s in p.out_shape],
                         scratch_shapes=list(scratch_shapes) + [s for p in plans for s in _dma_sems(p.n_sems)],
                         input_output_aliases=all_aliases, name=name, compiler_params=cparams or _cparams())(
        *operands, *[a for p in plans for a in p.operands], *deps)
    pos = n_out
    for p, no in zip(plans, p_out):
        p.results = list(res[pos:pos + no])
        pos += no
    return list(res[:n_out])


def _wide(v):
    return v.astype(F32) if v.dtype == BF16 else v


def stage(name, fn, grid, ins, outs):
    n_in = len(ins)

    def kern(*refs):
        res = fn(*[_wide(r[...]) for r in refs[:n_in]])
        for r, v in zip(refs[n_in:], res):
            r[...] = v.astype(r.dtype)

    return hosted_call(kern, name=name, grid=grid, in_specs=[s for _, s in ins], out_specs=[s for _, s in outs],
                       out_shape=[sd for sd, _ in outs], operands=[a for a, _ in ins])


def stage_bwd(name, fn, grid, ins, cts, diffs, plans=()):
    n_in, n_ct = len(ins), len(cts)
    didx = [i for i, d in enumerate(diffs) if d is not None]
    opts = {i: (diffs[i][3] if len(diffs[i]) > 3 else {}) for i in didx if diffs[i][0] == "row"}
    adds = [(i, opts[i]["add"]) for i in opts if "add" in opts[i]]
    intos = [(i, opts[i]["into"]) for i in opts if "into" in opts[i]]
    n_add, n_into = len(adds), len(intos)
    add_pos = {i: n_in + n_ct + k for k, (i, _) in enumerate(adds)}
    n_extra = n_in + n_ct + n_add + n_into

    def kern(*refs):
        vals = [_wide(r[...]) for r in refs[:n_in]]

        def f(*dv):
            full = list(vals)
            for i, v in zip(didx, dv):
                full[i] = v
            return fn(*full)

        _, vjp = jax.vjp(f, *[vals[i].astype(F32) for i in didx])
        gs = vjp(tuple(c[...].astype(F32) for c in refs[n_in:n_in + n_ct]))
        for o_ref, i, g in zip(refs[n_extra:], didx, gs):
            if diffs[i][0] == "row":
                if i in add_pos:
                    g = g + refs[add_pos[i]][...].astype(F32)
                o_ref[...] = g.astype(o_ref.dtype)
            else:
                first = functools.reduce(jnp.logical_and, [pl.program_id(ax) == 0 for ax in diffs[i][1]])

                @pl.when(first)
                def _():
                    o_ref[...] = g

                @pl.when(jnp.logical_not(first))
                def _():
                    o_ref[...] += g

    out_shape, out_specs = [], []
    for i in didx:
        if diffs[i][0] == "row":
            out_shape.append(diffs[i][1])
            out_specs.append(diffs[i][2])
        else:
            out_shape.append(SDS(ins[i][0].shape, F32))
            out_specs.append(ins[i][1])
    aliases = {n_in + n_ct + n_add + k: didx.index(i) for k, (i, _) in enumerate(intos)}
    in_specs = [s for _, s in ins] + [s for _, s in cts] + [s for _, (_, s) in adds] + [ANY] * n_into
    operands = [a for a, _ in ins] + [a for a, _ in cts] + [a for _, (a, _) in adds] + [a for _, a in intos]
    return hosted_call(kern, name=name, grid=grid, in_specs=in_specs, out_specs=out_specs, out_shape=out_shape, operands=operands,
                       aliases=aliases, plans=plans)


def rspec(tl, w, cb=0):
    return pl.BlockSpec((tl, w), lambda i: (i, cb))


def cspec(shape):
    return pl.BlockSpec(shape, lambda i: (0,) * len(shape))


def row_fwd(name, fn, rows, tl, row_ins, consts, outs):
    ins = [(a, rspec(tl, w, cb)) for a, w, cb in row_ins] + [(a, cspec(a.shape)) for a in consts]
    return stage(name, fn, (rows // tl,), ins, [(SDS((rows, w), dt), rspec(tl, w)) for w, dt in outs])


def row_bwd(name, fn, rows, tl, row_ins, consts, cts, row_diff, const_diff, plans=()):
    ins = [(a, rspec(tl, w, cb)) for a, w, cb in row_ins] + [(a, cspec(a.shape)) for a in consts]
    diffs = []
    for (a, w, cb), d in zip(row_ins, row_diff):
        if not d:
            diffs.append(None)
            continue
        d = d if isinstance(d, dict) else {}
        opts = {}
        if "add" in d:
            opts["add"] = (d["add"][0], rspec(tl, d["add"][1], d["add"][2]))
        if d.get("into") is not None:
            opts["into"] = d["into"]
        diffs.append(("row", SDS((rows, d.get("cols", w)), d.get("dtype", F32)), rspec(tl, w, d.get("cb", 0)), opts))
    diffs += [("acc", (0,)) if d else None for d in const_diff]
    return stage_bwd(name, fn, (rows // tl,), ins, [(a, rspec(tl, w, cb)) for a, w, cb in cts], diffs, plans=plans)


MATMUL_VMEM = 36 * 2**20
ADAMW_VMEM = 28 * 2**20


class Sharded:
    def __init__(self, arr, kind, roff, rows):
        self.arr, self.kind, self.roff, self.rows, self.n = arr, kind, roff, rows, arr.shape[2]
        self.shape = (rows, 4 * self.n) if kind == "col" else (4 * rows, self.n)

    def fits(self, t0, t1):
        return self.roff % t0 == 0 and self.rows % t0 == 0 and self.n % t1 == 0

    def spec(self, t0, t1, bidx):
        assert self.fits(t0, t1), (self.kind, self.roff, self.rows, self.n, t0, t1)
        r0 = self.roff // t0
        if self.kind == "col":
            per = self.n // t1
            return pl.BlockSpec((None, t0, t1), lambda *g: (bidx(*g)[1] // per, r0 + bidx(*g)[0], bidx(*g)[1] % per))
        per = self.rows // t0
        return pl.BlockSpec((None, t0, t1), lambda *g: (bidx(*g)[0] // per, r0 + bidx(*g)[0] % per, bidx(*g)[1]))


def _matmul_over_shards(name, a, b, mode, out_dtype, add, plans):
    m, k_dim = a.shape
    per = k_dim // 4
    n = b.n if mode == "nn" else b.rows
    assert k_dim % 4 == 0 and (b.rows == per and b.roff % per == 0 if mode == "nn" else b.n == per), (name, a.shape, mode)
    a_bytes, o_bytes = jnp.dtype(a.dtype).itemsize, jnp.dtype(out_dtype).itemsize

    def vmem(tm, tn):
        return 2 * (tm * k_dim * a_bytes + k_dim * tn * 2 + tm * tn * (o_bytes + (4 if add is not None else 0))) + 4 * tm * tn

    tiles = [(tm, tn) for tm in (2048, 1024, 512, 256) for tn in (1024, 512, 256, 128)
             if m % tm == 0 and n % tn == 0 and (mode == "nn" or b.roff % tn == 0) and vmem(tm, tn) <= MATMUL_VMEM]
    tm, tn = max(tiles, key=lambda t: (t[0] * t[1], t[0]))
    if mode == "nn":
        b_specs = [pl.BlockSpec((None, per, tn), lambda i, j, s=s: (s, b.roff // per, j)) for s in range(4)]
    else:
        b_specs = [pl.BlockSpec((None, tn, per), lambda i, j, s=s: (s, b.roff // tn + j, 0)) for s in range(4)]
    o_spec = pl.BlockSpec((tm, tn), lambda i, j: (i, j))
    cb = 0 if mode == "nn" else 1

    def kern(*refs):
        r = _dg(refs[0][:, 0:per], refs[1][...], 1, cb)
        for s in range(1, 4):
            r = r + _dg(refs[0][:, s * per:(s + 1) * per], refs[1 + s][...], 1, cb)
        if add is not None:
            r = r + refs[5][...]
        refs[-1][...] = r.astype(refs[-1].dtype)

    ins, specs = [a] + [b.arr] * 4, [pl.BlockSpec((tm, k_dim), lambda i, j: (i, 0))] + b_specs
    if add is not None:
        ins.append(add)
        specs.append(o_spec)
    return hosted_call(kern, name=name, grid=(m // tm, n // tn), in_specs=specs, out_specs=[o_spec], out_shape=[SDS((m, n), out_dtype)],
                       operands=ins, plans=plans)[0]


def matmul(name, a, b, mode, out_dtype=BF16, add=None, out=None, into=None, plans=()):
    if mode == "tn":
        k_dim, m = a.shape
    else:
        m, k_dim = a.shape
    n = b.shape[0] if mode == "nt" else b.shape[1]
    if isinstance(b, Sharded) and out is None and (mode, b.kind) in (("nn", "row"), ("nt", "col")):
        return _matmul_over_shards(name, a, b, mode, out_dtype, add, plans)
    b_fit = b.fits if isinstance(b, Sharded) else (lambda t0, t1: True)
    o_fit = out.fits if out is not None else (lambda t0, t1: True)
    a_bytes, b_bytes = jnp.dtype(a.dtype).itemsize, jnp.dtype(b.arr.dtype if isinstance(b, Sharded) else b.dtype).itemsize
    o_bytes = jnp.dtype(out_dtype if out is None else out.arr.dtype).itemsize

    def vmem(tm, tn, tk):
        return 2 * (tm * tk * a_bytes + tk * tn * b_bytes + tm * tn * (o_bytes + (4 if add is not None else 0))) + 4 * tm * tn * (1 + (tk < k_dim))

    tiles = [(tm, tn, tk) for tm in (2048, 1024, 512, 256, 128) for tn in (1024, 768, 512, 384, 256, 128)
             for tk in sorted({k_dim, 1024, 768, 512, 384, 256, 128})
             if m % tm == 0 and n % tn == 0 and k_dim % tk == 0 and (b_fit(tn, tk) if mode == "nt" else b_fit(tk, tn)) and o_fit(tm, tn)
             and vmem(tm, tn, tk) <= MATMUL_VMEM]
    tm, tn, tk = max(tiles, key=lambda t: (t[2] == k_dim, t[0] * t[1] * t[2], t[0] * t[1]))
    nk = k_dim // tk
    a_spec = pl.BlockSpec((tk, tm), lambda i, j, k: (k, i)) if mode == "tn" else pl.BlockSpec((tm, tk), lambda i, j, k: (i, k))
    if isinstance(b, Sharded):
        b_spec = b.spec(tn, tk, lambda i, j, k: (j, k)) if mode == "nt" else b.spec(tk, tn, lambda i, j, k: (k, j))
        b = b.arr
    else:
        b_spec = pl.BlockSpec((tn, tk), lambda i, j, k: (j, k)) if mode == "nt" else pl.BlockSpec((tk, tn), lambda i, j, k: (k, j))
    o_spec = pl.BlockSpec((tm, tn), lambda i, j, k: (i, j))
    out_spec, out_shape = (o_spec, SDS((m, n), out_dtype)) if out is None else (out.spec(tm, tn, lambda i, j, k: (i, j)), out.arr)
    ca, cb = {"nn": (1, 0), "nt": (1, 1), "tn": (0, 0)}[mode]
    n_in = 2 + (add is not None)

    def finish(refs, o_ref, r):
        if add is not None:
            r = r + refs[2][...]
        o_ref[...] = r.astype(o_ref.dtype)

    def kern_whole(*refs):
        finish(refs, refs[-1], _dg(refs[0][...], refs[1][...], ca, cb))

    def kern_cut(*refs):
        o_ref, acc = refs[-2], refs[-1]
        k = pl.program_id(2)

        @pl.when(k == 0)
        def _():
            acc[...] = jnp.zeros_like(acc)

        acc[...] += _dg(refs[0][...], refs[1][...], ca, cb)

        @pl.when(k == nk - 1)
        def _():
            finish(refs, o_ref, acc[...])

    ins, specs = [a, b], [a_spec, b_spec]
    if add is not None:
        ins.append(add)
        specs.append(o_spec)
    if into is not None:
        ins.append(into)
        specs.append(ANY)
    return hosted_call(kern_whole if nk == 1 else kern_cut, name=name, grid=(m // tm, n // tn, nk), in_specs=specs, out_specs=[out_spec],
                       out_shape=[out_shape], operands=ins, scratch_shapes=[] if nk == 1 else [pltpu.VMEM((tm, tn), F32)],
                       aliases={} if into is None else {n_in: 0}, plans=plans)[0]


SCAN_UNROLL = 8


def _cmul(ar, ai, br, bi):
    return ar * br - ai * bi, ar * bi + ai * br


def _sub_shift(x, down):
    row = lax.broadcasted_iota(jnp.int32, x.shape, 0)
    if down:
        return jnp.where(row == 0, 0.0, pltpu.roll(x, 1, 0))
    return jnp.where(row == SEG - 1, 0.0, pltpu.roll(x, SEG - 1, 0))


def _pow_seg_len(ar, ai):
    for _ in range(int(math.log2(SEG_LEN))):
        ar, ai = _cmul(ar, ai, ar, ai)
    return ar, ai


def _scan_in_place(sr, si, a_re, a_im):
    lanes = sr.shape[1]
    ar = jnp.broadcast_to(a_re, (SEG, lanes))
    ai = jnp.broadcast_to(a_im, (SEG, lanes))
    zero = jnp.zeros((SEG, lanes), F32)

    def local(i, carry):
        rows = pl.ds(pl.multiple_of(i * SEG, SEG), SEG)
        mr, mi = _cmul(ar, ai, carry[0], carry[1])
        nr, ni = mr + sr[rows, :], mi + si[rows, :]
        sr[rows, :] = nr
        si[rows, :] = ni
        return nr, ni

    fr, fi = lax.fori_loop(0, SEG_LEN, local, (zero, zero), unroll=SCAN_UNROLL)
    pr, pi = _pow_seg_len(ar, ai)
    ir, ii = zero, zero
    for _ in range(SEG - 1):
        mr, mi = _cmul(pr, pi, ir, ii)
        ir, ii = _sub_shift(mr + fr, True), _sub_shift(mi + fi, True)

    def carry_in(i, pw):
        rows = pl.ds(pl.multiple_of(i * SEG, SEG), SEG)
        cr, ci = _cmul(pw[0], pw[1], ir, ii)
        sr[rows, :] += cr
        si[rows, :] += ci
        return _cmul(pw[0], pw[1], ar, ai)

    lax.fori_loop(0, SEG_LEN, carry_in, (ar, ai), unroll=SCAN_UNROLL)


S5_LANES = 8 * SP
S5_BLOCKS = SN // S5_LANES


def _s5_specs(n=1):
    u_spec = pl.BlockSpec((L, n * 8 * SC), lambda j: (0, j))
    s_spec = pl.BlockSpec((L, n * S5_LANES), lambda j: (0, j))
    wb_spec = pl.BlockSpec((n * S5_LANES, 8 * SC), lambda j: (j, 0))
    wc_spec = pl.BlockSpec((n * 8 * SC, S5_LANES), lambda j: (j, 0))
    a_spec = pl.BlockSpec((1, n * S5_LANES), lambda j: (0, j))
    d_spec = pl.BlockSpec((1, n * 8 * SC), lambda j: (0, j))
    return u_spec, s_spec, wb_spec, wc_spec, a_spec, d_spec


S5_FWD_BLOCKS = 2


def s5_forward(proj, wb_re, wb_im, wc_re, wc_im, a_re, a_im, d, plans=()):
    n = S5_FWD_BLOCKS
    cols, lanes = 8 * SC, S5_LANES

    def kern(u_ref, wbr, wbi, wcr, wci, ar, ai, d_ref, sr_out, si_out, g_ref, sr, si):
        u = _wide(u_ref[...])
        for b in range(n):
            sr[:, b * lanes:(b + 1) * lanes], si[:, b * lanes:(b + 1) * lanes] = fn_s5_bu(
                u[:, b * cols:(b + 1) * cols], wbr[b * lanes:(b + 1) * lanes, :], wbi[b * lanes:(b + 1) * lanes, :])
        _scan_in_place(sr, si, ar[...], ai[...])
        for b in range(n):
            g_ref[:, b * cols:(b + 1) * cols] = fn_s5_out(
                sr[:, b * lanes:(b + 1) * lanes], si[:, b * lanes:(b + 1) * lanes], u[:, b * cols:(b + 1) * cols],
                d_ref[:, b * cols:(b + 1) * cols], wcr[b * cols:(b + 1) * cols, :], wci[b * cols:(b + 1) * cols, :])[0].astype(g_ref.dtype)
        sr_out[...] = sr[...].astype(sr_out.dtype)
        si_out[...] = si[...].astype(si_out.dtype)

    u_spec, s_spec, wb_spec, wc_spec, a_spec, d_spec = _s5_specs(n)
    return hosted_call(kern, name="s5_forward", grid=(S5_BLOCKS // n,), in_specs=[u_spec, wb_spec, wb_spec, wc_spec, wc_spec, a_spec, a_spec, d_spec],
                       out_specs=[s_spec, s_spec, u_spec], out_shape=[SDS((L, SN), BF16)] * 2 + [SDS((L, PW), BF16)],
                       operands=[proj, wb_re, wb_im, wc_re, wc_im, a_re, a_im, d], scratch_shapes=[pltpu.VMEM((L, n * lanes), F32)] * 2,
                       plans=plans)


def _adjoint_scan_in_place(lr, li, sr, si, a_re, a_im):
    lanes = lr.shape[1]
    ar = jnp.broadcast_to(a_re, (SEG, lanes))
    ai = -jnp.broadcast_to(a_im, (SEG, lanes))
    zero = jnp.zeros((SEG, lanes), F32)

    def local(k, carry):
        i = SEG_LEN - 1 - k
        rows = pl.ds(pl.multiple_of(i * SEG, SEG), SEG)
        mr, mi = _cmul(ar, ai, carry[0], carry[1])
        nr, ni = mr + lr[rows, :], mi + li[rows, :]
        lr[rows, :] = nr
        li[rows, :] = ni
        return nr, ni

    fr, fi = lax.fori_loop(0, SEG_LEN, local, (zero, zero), unroll=SCAN_UNROLL)
    pr, pi = _pow_seg_len(ar, ai)
    ir, ii = zero, zero
    for _ in range(SEG - 1):
        mr, mi = _cmul(pr, pi, ir, ii)
        ir, ii = _sub_shift(mr + fr, False), _sub_shift(mi + fi, False)

    def fix(rows, pw):
        cr, ci = _cmul(pw[0], pw[1], ir, ii)
        tr, ti = lr[rows, :] + cr, li[rows, :] + ci
        lr[rows, :] = tr
        li[rows, :] = ti
        return tr, ti

    def grad_a(tr, ti, spr, spi, acc):
        return acc[0] + tr * spr + ti * spi, acc[1] + ti * spr - tr * spi

    def carry_in(k, c):
        i = SEG_LEN - 1 - k
        rows = pl.ds(pl.multiple_of(i * SEG, SEG), SEG)
        prev = pl.ds(pl.multiple_of((i - 1) * SEG, SEG), SEG)
        tr, ti = fix(rows, (c[0], c[1]))
        acc = grad_a(tr, ti, sr[prev, :], si[prev, :], (c[2], c[3]))
        nr, ni = _cmul(c[0], c[1], ar, ai)
        return nr, ni, acc[0], acc[1]

    pwr, pwi, accr, acci = lax.fori_loop(0, SEG_LEN - 1, carry_in, (ar, ai, zero, zero), unroll=5)
    tr, ti = fix(pl.ds(0, SEG), (pwr, pwi))
    last = pl.ds((SEG_LEN - 1) * SEG, SEG)
    accr, acci = grad_a(tr, ti, _sub_shift(sr[last, :], True), _sub_shift(si[last, :], True), (accr, acci))
    return jnp.sum(accr, axis=0, keepdims=True), jnp.sum(acci, axis=0, keepdims=True)


S5_BWD_VMEM = 58 * 2**20


def s5_backward(dg, proj, s_re, s_im, wb_re, wb_im, wc_re, wc_im, a_re, a_im, d, dproj, plans=()):
    def kern(dg_ref, u_ref, sr_in, si_in, wbr, wbi, wcr, wci, ar, ai, d_ref, _, du_ref, dd_ref, dwcr, dwci, dwbr, dwbi, dar, dai, lr, li, sr, si):
        u = _wide(u_ref[...])
        sr[...], si[...] = _wide(sr_in[...]), _wide(si_in[...])
        _, vjp_out = jax.vjp(fn_s5_out, sr[...], si[...], u, d_ref[...], wcr[...], wci[...])
        lr[...], li[...], du_out, dd_ref[...], dwcr[...], dwci[...] = vjp_out((_wide(dg_ref[...]),))
        dar[...], dai[...] = _adjoint_scan_in_place(lr, li, sr, si, ar[...], ai[...])
        _, vjp_in = jax.vjp(fn_s5_bu, u, wbr[...], wbi[...])
        du_in, dwbr[...], dwbi[...] = vjp_in((lr[...], li[...]))
        du_ref[...] = (du_out + du_in).astype(du_ref.dtype)

    u_spec, s_spec, wb_spec, wc_spec, a_spec, d_spec = _s5_specs()
    outs = [(SDS(dproj.shape, dproj.dtype), u_spec), (SDS(d.shape, F32), d_spec), (SDS(wc_re.shape, F32), wc_spec), (SDS(wc_im.shape, F32), wc_spec),
            (SDS(wb_re.shape, F32), wb_spec), (SDS(wb_im.shape, F32), wb_spec), (SDS(a_re.shape, F32), a_spec), (SDS(a_im.shape, F32), a_spec)]
    return hosted_call(kern, name="s5_backward", grid=(S5_BLOCKS,),
                       in_specs=[u_spec, u_spec, s_spec, s_spec, wb_spec, wb_spec, wc_spec, wc_spec, a_spec, a_spec, d_spec, ANY],
                       out_specs=[sp for _, sp in outs], out_shape=[sd for sd, _ in outs], aliases={11: 0},
                       operands=[dg, proj, s_re, s_im, wb_re, wb_im, wc_re, wc_im, a_re, a_im, d, dproj],
                       scratch_shapes=[pltpu.VMEM((L, S5_LANES), F32)] * 4,
                       cparams=pltpu.CompilerParams(vmem_limit_bytes=S5_BWD_VMEM), plans=plans)


def fn_rms(x, g):
    return (rms(x, g),)


def fn_s5_disc(lre, lim, ls):
    step = jnp.exp(ls)
    e = jnp.exp(lre * step)
    a_re, a_im = e * jnp.cos(lim * step), e * jnp.sin(lim * step)
    den = lre * lre + lim * lim
    nr, ni = a_re - 1.0, a_im
    return a_re, a_im, (nr * lre + ni * lim) / den, (ni * lre - nr * lim) / den


def _group_mask(rows, cols, row_div, col_div):
    r = lax.broadcasted_iota(jnp.int32, (rows, cols), 0) // row_div % 8
    c = lax.broadcasted_iota(jnp.int32, (rows, cols), 1) // col_div
    return r == c


def _spread(x, mask):
    w = x.shape[1]
    copy = (lax.broadcasted_iota(jnp.int32, (w, 8 * w), 1) % w == lax.broadcasted_iota(jnp.int32, (w, 8 * w), 0)).astype(F32)
    return jnp.where(mask, jnp.dot(x, copy, precision=lax.Precision.HIGHEST, preferred_element_type=F32), 0.0)


def fn_s5_bmat(b_re, b_im, coef_re, coef_im):
    mask = _group_mask(b_re.shape[0], 8 * SC, SP, SC)
    return _spread(coef_re * b_re - coef_im * b_im, mask), _spread(coef_re * b_im + coef_im * b_re, mask)


def fn_s5_cmat(c_re, c_im):
    mask = _group_mask(c_re.shape[0], 8 * SP, SC, SP)
    return _spread(c_re, mask), _spread(c_im, mask)


def fn_s5_bu(u, wb_re, wb_im):
    return mm_nt(u, wb_re), mm_nt(u, wb_im)


def fn_s5_out(sr, si, u, d, wc_re, wc_im):
    y = mm_nt(sr, wc_re) - mm_nt(si, wc_im) + d * u
    return (jax.nn.gelu(y),)


def fn_merge_glu(z, mo, gate):
    yg = z[:, :PW] * jax.nn.sigmoid(z[:, PW:])
    return (jnp.concatenate([yg, mo], axis=1) * silu(gate),)


def fn_merge(prim, mo, gate):
    return (jnp.concatenate([prim, mo], axis=1) * silu(gate),)


def fn_mem_k(kv, g):
    return (jnp.concatenate([rms(kv[:, h * XHD:(h + 1) * XHD], g) for h in range(XH)], axis=1),)


def fn_mem_attn(xq, kn, v, g):
    outs = []
    for h in range(XH):
        sl = slice(h * XHD, (h + 1) * XHD)
        p = softmax_rows(mm_nt(rms(xq[:, sl], g), kn[:, sl]) * (XHD ** -0.5))
        outs.append(mm_nn(p, v[:, sl]))
    return (jnp.concatenate(outs, axis=1),)


def _half_rms(x, g):
    lo = lax.broadcasted_iota(jnp.int32, x.shape, 1) < ROPE
    x2 = x * x
    s_lo = jnp.sum(jnp.where(lo, x2, 0.0), axis=1, keepdims=True)
    s_hi = jnp.sum(jnp.where(lo, 0.0, x2), axis=1, keepdims=True)
    return x * lax.rsqrt(jnp.where(lo, s_lo, s_hi) / ROPE + EPS) * g


def _rope(x, cos2, sin_signed):
    first = lax.broadcasted_iota(jnp.int32, x.shape, 1) % ROPE < ROPE // 2
    return x * cos2 + jnp.where(first, lane_roll(x, 128 - ROPE // 2), lane_roll(x, ROPE // 2)) * sin_signed


def fn_mla_prep(q, kv, kr, cos2, sin_signed, qnn, knn, qrn, krn):
    lo = lax.broadcasted_iota(jnp.int32, kr.shape, 1) < ROPE
    kr_pad = jnp.where(lo, _rope(_half_rms(kr, krn), cos2, sin_signed), 0.0)
    qf, kf, vs = [], [], []
    for m in range(MH // 2):
        pair = _rope(_half_rms(q[:, MH * NOPE + 128 * m:MH * NOPE + 128 * (m + 1)], qrn), cos2, sin_signed)
        for h, rope_h in ((2 * m, pair), (2 * m + 1, lane_roll(pair, ROPE))):
            qf.append(jnp.concatenate([rms(q[:, NOPE * h:NOPE * (h + 1)], qnn), jnp.where(lo, rope_h, 0.0)], axis=1))
    for h in range(MH):
        kf.append(jnp.concatenate([rms(kv[:, 256 * h:256 * h + NOPE], knn), kr_pad], axis=1))
        vs.append(kv[:, 256 * h + NOPE:256 * (h + 1)])
    return jnp.stack(qf), jnp.stack(kf), jnp.stack(vs)


ATT_TQ = 512


def _attn_scores(q, kf):
    tq = q.shape[0]
    scale = (NOPE + ROPE) ** -0.5
    own = _dg(q, kf[-tq:], 1, 1) * scale
    own = jnp.where(lax.broadcasted_iota(jnp.int32, own.shape, 1) <= lax.broadcasted_iota(jnp.int32, own.shape, 0), own, jnp.finfo(F32).min)
    return own if kf.shape[0] == tq else jnp.concatenate([_dg(q, kf[:-tq], 1, 1) * scale, own], axis=1)


ATT_FWD_HEADS, ATT_BWD_HEADS = 4, 2


def _attn_specs(heads):
    q_spec = pl.BlockSpec((heads, ATT_TQ, 256), lambda h, i: (h, i, 0))
    k_spec = pl.BlockSpec((heads, L, 256), lambda h, i: (h, 0, 0))
    v_spec = pl.BlockSpec((heads, L, 128), lambda h, i: (h, 0, 0))
    o_spec = pl.BlockSpec((ATT_TQ, heads * 128), lambda h, i: (i, h))
    lse_spec = pl.BlockSpec((heads, ATT_TQ, 1), lambda h, i: (h, i, 0))
    return q_spec, k_spec, v_spec, o_spec, lse_spec


def _attn_branches(heads, body):
    i = pl.program_id(1)
    for t in range(L // ATT_TQ):
        @pl.when(i == t)
        def _(t=t):
            for hh in range(heads):
                body(hh, slice(hh * 128, (hh + 1) * 128), (t + 1) * ATT_TQ)


def causal_attn(qf, kf, vh):
    def kern(q_ref, k_ref, v_ref, o_ref, lse_ref):
        def body(hh, lanes, keys):
            s = _attn_scores(q_ref[hh], k_ref[hh, :keys, :])
            m = jnp.max(s, axis=-1, keepdims=True)
            e = jnp.exp(s - m)
            total = jnp.sum(e, axis=-1, keepdims=True)
            o_ref[:, lanes] = (_dg(e, v_ref[hh, :keys, :], 1, 0) / total).astype(o_ref.dtype)
            lse_ref[hh] = m + jnp.log(total)

        _attn_branches(ATT_FWD_HEADS, body)

    q_spec, k_spec, v_spec, o_spec, lse_spec = _attn_specs(ATT_FWD_HEADS)
    return pl.pallas_call(kern, grid=(MH // ATT_FWD_HEADS, L // ATT_TQ), in_specs=[q_spec, k_spec, v_spec], out_specs=[o_spec, lse_spec],
                          out_shape=[SDS((L, MH * VD), F32), SDS((MH, L, 1), F32)], name="l1_attn", compiler_params=_cparams())(qf, kf, vh)


def causal_attn_bwd(qf, kf, vh, out, lse, dout):
    scale = (NOPE + ROPE) ** -0.5

    def kern(q_ref, k_ref, v_ref, o_ref, lse_ref, do_ref, dq_ref, dk_ref, dv_ref):
        @pl.when(pl.program_id(1) == 0)
        def _():
            dk_ref[...] = jnp.zeros_like(dk_ref)
            dv_ref[...] = jnp.zeros_like(dv_ref)

        def body(hh, lanes, keys):
            q, k, v, do = q_ref[hh], k_ref[hh, :keys, :], v_ref[hh, :keys, :], do_ref[:, lanes]
            p = jnp.exp(_attn_scores(q, k) - lse_ref[hh])
            delta = jnp.sum(do * _wide(o_ref[:, lanes]), axis=-1, keepdims=True)
            dv_ref[hh, :keys, :] += _dg(p, do, 0, 0)
            ds = p * (_dg(do, v, 1, 1) - delta) * scale
            dq_ref[hh] = _dg(ds, k, 1, 0)
            dk_ref[hh, :keys, :] += _dg(ds, q, 0, 0)

        _attn_branches(ATT_BWD_HEADS, body)

    q_spec, k_spec, v_spec, o_spec, lse_spec = _attn_specs(ATT_BWD_HEADS)
    return pl.pallas_call(kern, grid=(MH // ATT_BWD_HEADS, L // ATT_TQ), in_specs=[q_spec, k_spec, v_spec, o_spec, lse_spec, o_spec],
                          out_specs=[q_spec, k_spec, v_spec], out_shape=[SDS(qf.shape, F32), SDS(kf.shape, F32), SDS(vh.shape, F32)],
                          name="l1_attn_bwd", compiler_params=_cparams())(qf, kf, vh, out, lse, dout)


def loss_and_grad(y, target, tl=512):
    def kern(y_ref, t_ref, dy_ref, loss_ref):
        d = y_ref[...] - t_ref[...]
        dy_ref[...] = d / D

        @pl.when(pl.program_id(0) == 0)
        def _():
            loss_ref[...] = jnp.zeros_like(loss_ref)

        loss_ref[...] += 0.5 * jnp.sum(jnp.sum(d * d, axis=1, keepdims=True), axis=0, keepdims=True) / D

    return pl.pallas_call(kern, grid=(L // tl,), in_specs=[rspec(tl, D), rspec(tl, D)], out_specs=[rspec(tl, D), cspec((1, 1))],
                          out_shape=[SDS((L, D), F32), SDS((1, 1), F32)], name="loss", compiler_params=_cparams())(y, target)


def adamw(name, w, g, m, v):
    rows, cols = w.shape
    block_row_bytes = 7 * 2 * 4 * max(cols, 128)
    tr = _row_tile(rows, min(2048, ADAMW_VMEM // block_row_bytes // 8 * 8), 8)

    def kern(w_ref, g_ref, m_ref, v_ref, d_ref, nm_ref, nv_ref):
        gg = g_ref[...]
        nm = ADAM_B1 * m_ref[...] + (1.0 - ADAM_B1) * gg
        nv = ADAM_B2 * v_ref[...] + (1.0 - ADAM_B2) * jnp.square(gg)
        m_hat = nm / (1.0 - ADAM_B1 ** ADAM_STEP)
        v_hat = nv / (1.0 - ADAM_B2 ** ADAM_STEP)
        d_ref[...] = -ADAM_LR * (m_hat / (jnp.sqrt(v_hat) + ADAM_EPS) + ADAM_WD * w_ref[...])
        nm_ref[...] = nm
        nv_ref[...] = nv

    spec = rspec(tr, cols)
    return hosted_call(kern, name=name, grid=(rows // tr,), in_specs=[spec] * 4, out_specs=[spec] * 3,
                       out_shape=[SDS((rows, cols), F32)] * 3, operands=[w, g, m, v])


def _row_tile(rows, cap=512, unit=16):
    return max(t for t in range(unit, cap + 1, unit) if rows % t == 0)


def _place():
    x, y, c = lax.axis_index("x"), lax.axis_index("y"), lax.axis_index("c")
    return x, y, c, [(1 - x, y), (x, 1 - y), (1 - x, 1 - y)]


def _row_chunks(rows, n, dtype):
    unit = 32 // jnp.dtype(dtype).itemsize
    base, extra = divmod(rows // unit, n)
    out, start = [], 0
    for k in range(n):
        size = (base + (k < extra)) * unit
        if size:
            out.append((start, size))
            start += size
    assert start == rows, (rows, unit)
    return out


PIECE_BYTES = 1 << 20


def _pieces(shapes_dtypes, rows_of):
    out = []
    for b, (shape, dtype) in enumerate(shapes_dtypes):
        rows = rows_of(shape)
        n = max(1, min(4, rows * shape[-1] * jnp.dtype(dtype).itemsize // PIECE_BYTES))
        out += [(b, st, sz) for st, sz in _row_chunks(rows, n, dtype)]
    return out


def all_gather_chips(name, shards):
    nb = len(shards)
    pieces = _pieces([(s.shape, s.dtype) for s in shards], lambda shape: shape[0] // 2)
    n = len(pieces)

    def body(*refs):
        x_refs, out_refs, send_sems, recv_sems = refs[:nb], refs[nb:2 * nb], refs[2 * nb], refs[2 * nb + 1]
        x, y, c, chips = _place()
        sibling = (x, y, 1 - c)
        mine = 2 * x + y

        def copy(sem, chip, cc, k, to, from_input=False):
            b, st, sz = pieces[k]
            rows_k = pl.ds(cc * (x_refs[b].shape[0] // 2) + st, sz)
            dst = out_refs[b].at[chip, rows_k, :]
            return pltpu.make_async_remote_copy(src_ref=x_refs[b].at[rows_k, :] if from_input else dst, dst_ref=dst,
                                                send_sem=send_sems.at[sem], recv_sem=recv_sems.at[sem], device_id=to, device_id_type=MESH_ID)

        order = [(k, j, 2 * cx + cy, (cx, cy, c)) for k in range(n) for j, (cx, cy) in enumerate(chips)]
        first = [copy(j * n + k, mine, c, k, to, from_input=True) for k, j, _, to in order]
        for cp in first:
            cp.start()
        passed = []
        for k, j, chip, _ in order:
            copy(j * n + k, chip, c, k, sibling).wait_recv()
            passed.append(copy((3 + j) * n + k, chip, c, k, sibling))
            passed[-1].start()
        for k, j, chip, _ in order:
            copy((3 + j) * n + k, chip, 1 - c, k, sibling).wait_recv()
        for cp in first + passed:
            cp.wait_send()

    return pl.pallas_call(body, in_specs=[ANY] * nb, out_specs=[ANY] * nb, out_shape=[SDS((4,) + s.shape, s.dtype) for s in shards],
                          scratch_shapes=_dma_sems(6 * n), name=name)(*shards)


def plan_gather_ici(shards):
    pieces = _pieces([(s.shape, s.dtype) for s in shards], lambda shape: shape[0] // 2)
    n = len(pieces)

    def copies(x_refs, out_refs, send_sems, recv_sems):
        x, y, c, chips = _place()
        mine = 2 * x + y

        def copy(j, k, chip, to, from_input):
            b, st, sz = pieces[k]
            rows_k = pl.ds(c * (x_refs[b].shape[0] // 2) + st, sz)
            dst = out_refs[b].at[chip, rows_k, :]
            return pltpu.make_async_remote_copy(src_ref=x_refs[b].at[rows_k, :] if from_input else dst, dst_ref=dst, send_sem=send_sems.at[j * n + k],
                                                recv_sem=recv_sems.at[j * n + k], device_id=to, device_id_type=MESH_ID)

        order = [(k, j, 2 * cx + cy, (cx, cy, c)) for k in range(n) for j, (cx, cy) in enumerate(chips)]
        return [copy(j, k, mine, to, True) for k, j, _, to in order], [copy(j, k, chip, to, False) for k, j, chip, to in order]

    return Plan(shards, [SDS((4,) + s.shape, s.dtype) for s in shards], {}, 3 * n, copies)


def plan_gather_pass(gathered):
    pieces = _pieces([(g.shape[1:], g.dtype) for g in gathered], lambda shape: shape[0] // 2)
    n = len(pieces)

    def copies(_, out_refs, send_sems, recv_sems):
        x, y, c, chips = _place()

        def copy(j, k, chip, cc):
            b, st, sz = pieces[k]
            rows_k = out_refs[b].at[chip, pl.ds(cc * (out_refs[b].shape[1] // 2) + st, sz), :]
            return pltpu.make_async_remote_copy(src_ref=rows_k, dst_ref=rows_k, send_sem=send_sems.at[j * n + k], recv_sem=recv_sems.at[j * n + k],
                                                device_id=(x, y, 1 - c), device_id_type=MESH_ID)

        order = [(k, j, 2 * cx + cy) for k in range(n) for j, (cx, cy) in enumerate(chips)]
        return [copy(j, k, chip, c) for k, j, chip in order], [copy(j, k, chip, 1 - c) for k, j, chip in order]

    return Plan(gathered, [SDS(g.shape, g.dtype) for g in gathered], {i: i for i in range(len(gathered))}, 3 * n, copies)


def plan_pair_exchange(gs):
    pieces = _pieces([(g.shape, g.dtype) for g in gs], lambda shape: shape[1] // 2)

    def copies(g_refs, got_refs, send_sems, recv_sems):
        x, y, c, _ = _place()
        swaps = [pltpu.make_async_remote_copy(src_ref=g_refs[b].at[:, pl.ds((1 - c) * (g_refs[b].shape[1] // 2) + st, sz), :],
                                              dst_ref=got_refs[b].at[:, pl.ds(st, sz), :], send_sem=send_sems.at[k], recv_sem=recv_sems.at[k],
                                              device_id=(x, y, 1 - c), device_id_type=MESH_ID)
                 for k, (b, st, sz) in enumerate(pieces)]
        return swaps, swaps

    return Plan(gs, [SDS((g.shape[0], g.shape[1] // 2, g.shape[2]), g.dtype) for g in gs], {}, len(pieces), copies)


def plan_chip_scatter(ps):
    pieces = _pieces([(p.shape, p.dtype) for p in ps], lambda shape: shape[1])
    n = len(pieces)

    def copies(p_refs, q_refs, send_sems, recv_sems):
        x, y, c, chips = _place()
        mine = 2 * x + y

        def copy(j, k, src_slot, dst_slot, to):
            b, st, sz = pieces[k]
            return pltpu.make_async_remote_copy(src_ref=p_refs[b].at[src_slot, pl.ds(st, sz), :], dst_ref=q_refs[b].at[dst_slot, pl.ds(st, sz), :],
                                                send_sem=send_sems.at[j * n + k], recv_sem=recv_sems.at[j * n + k], device_id=to,
                                                device_id_type=MESH_ID)

        order = [(k, j, 2 * cx + cy, (cx, cy, c)) for k in range(n) for j, (cx, cy) in enumerate(chips)]
        return [copy(j, k, chip, mine, to) for k, j, chip, to in order], [copy(j, k, mine, chip, to) for k, j, chip, to in order]

    return Plan(ps, [SDS(p.shape, p.dtype) for p in ps], {}, 3 * n, copies)


def plan_pair_join(bufs):
    pieces = _pieces([(b.shape, b.dtype) for b in bufs], lambda shape: shape[0] // 2)

    def copies(_, out_refs, send_sems, recv_sems):
        x, y, c, _ = _place()

        def copy(k, cc):
            b, st, sz = pieces[k]
            rows_k = out_refs[b].at[pl.ds(cc * (out_refs[b].shape[0] // 2) + st, sz), :]
            return pltpu.make_async_remote_copy(src_ref=rows_k, dst_ref=rows_k, send_sem=send_sems.at[k], recv_sem=recv_sems.at[k],
                                                device_id=(x, y, 1 - c), device_id_type=MESH_ID)

        return [copy(k, c) for k in range(len(pieces))], [copy(k, 1 - c) for k in range(len(pieces))]

    return Plan(bufs, [SDS(b.shape, b.dtype) for b in bufs], {i: i for i in range(len(bufs))}, len(pieces), copies)


def run_plan(name, plan):
    hosted_call(lambda: None, name=name, grid=(1,), in_specs=[], out_specs=[], out_shape=[], operands=[], plans=[plan])
    return plan.results


HBM = pl.BlockSpec(memory_space=pltpu.HBM)
SEMS = pl.BlockSpec(memory_space=pltpu.SEMAPHORE)
SPLIT_PARAMS = dict(has_side_effects=pltpu.SideEffectType.DATAFLOW_SIDE_EFFECTING)


def _plan_buffers(plan):
    in_place = {o: i for i, o in plan.aliases.items()}
    bufs = [pltpu.with_memory_space_constraint(a, pltpu.HBM) for a in plan.operands]
    where = []
    for o, sd in enumerate(plan.out_shape):
        if o in in_place:
            where.append(in_place[o])
        else:
            where.append(len(bufs))
            bufs.append(pltpu.with_memory_space_constraint(lax.empty(sd.shape, sd.dtype), pltpu.HBM))
    return bufs, where


def split_start(name, plans):
    layout = [_plan_buffers(p) for p in plans]
    counts = [len(b) for b, _ in layout]
    n_buf = sum(counts)

    def body(*refs):
        sems, token = refs[n_buf:n_buf + 2 * len(plans)], refs[-1]
        pos = 0
        for k, (p, (_, where)) in enumerate(zip(plans, layout)):
            mine = refs[pos:pos + counts[k]]
            pos += counts[k]
            sends, _ = p.copies(mine[:len(p.operands)], [mine[w] for w in where], sems[2 * k], sems[2 * k + 1])
            for cp in sends:
                cp.start()
        token[...] = jnp.zeros_like(token)

    bufs = [b for bs, _ in layout for b in bs]
    res = pl.pallas_call(
        body, name=name, in_specs=[HBM] * n_buf,
        out_specs=[SEMS] * (2 * len(plans)) + [HBM] * n_buf + [pl.BlockSpec(memory_space=pltpu.VMEM)],
        out_shape=[pltpu.SemaphoreType.DMA((p.n_sems,)) for p in plans for _ in range(2)] + [pltpu.HBM(b.shape, b.dtype) for b in bufs]
        + [SDS((8, 128), F32)],
        input_output_aliases={i: 2 * len(plans) + i for i in range(n_buf)}, compiler_params=pltpu.CompilerParams(**SPLIT_PARAMS))(*bufs)
    pos = 2 * len(plans)
    for k, p in enumerate(plans):
        p.in_flight = (res[2 * k], res[2 * k + 1], list(res[pos:pos + counts[k]]), layout[k][1])
        pos += counts[k]
    return res[-1]


def split_wait(name, plan, after):
    send_sems, recv_sems, bufs, where = plan.in_flight
    n_buf = len(bufs)

    def body(*refs):
        mine = refs[:n_buf]
        sends, recvs = plan.copies(mine[:len(plan.operands)], [mine[w] for w in where], refs[n_buf], refs[n_buf + 1])
        for cp in recvs:
            cp.wait_recv()
        for cp in sends:
            cp.wait_send()

    res = pl.pallas_call(body, name=name, in_specs=[HBM] * n_buf + [SEMS, SEMS, ANY], out_specs=[HBM] * n_buf,
                         out_shape=[pltpu.HBM(b.shape, b.dtype) for b in bufs], input_output_aliases={i: i for i in range(n_buf)},
                         compiler_params=pltpu.CompilerParams(**SPLIT_PARAMS))(*bufs, send_sems, recv_sems, after)
    plan.results = [res[w] for w in where]
    return plan.results


def pair_add(name, g, got, place):
    slots, rows, cols = g.shape
    half = rows // 2
    tr = _row_tile(half)
    nb = half // tr

    def kern(_, g_ref, t_ref, o_ref):
        o_ref[...] = (g_ref[...].astype(F32) + t_ref[...].astype(F32)).astype(o_ref.dtype)

    blk = pl.BlockSpec((None, tr, cols), lambda s, i, p: (s, i, 0))
    grid_spec = pltpu.PrefetchScalarGridSpec(
        num_scalar_prefetch=1, grid=(slots, nb),
        in_specs=[pl.BlockSpec((None, tr, cols), lambda s, i, p: (s, p[1] * nb + i, 0)), blk], out_specs=blk)
    return pl.pallas_call(kern, grid_spec=grid_spec, out_shape=SDS((slots, half, cols), g.dtype), name=name,
                          compiler_params=_cparams())(place, g, got)


def chip_add(name, p, q, place):
    slots, half, cols = p.shape
    tr = _row_tile(half)
    nb = half // tr

    def kern(_, p_ref, q1, q2, q3, o_ref):
        o_ref[...] = p_ref[...].astype(F32) + q1[...].astype(F32) + q2[...].astype(F32) + q3[...].astype(F32)

    def slot(k):
        return pl.BlockSpec((None, tr, cols), lambda i, pr: ((pr[0] + k) % slots, i, 0))

    grid_spec = pltpu.PrefetchScalarGridSpec(
        num_scalar_prefetch=1, grid=(nb,), in_specs=[slot(0), slot(1), slot(2), slot(3)],
        out_specs=pl.BlockSpec((tr, cols), lambda i, pr: (pr[1] * nb + i, 0)))
    return pl.pallas_call(kern, grid_spec=grid_spec, out_shape=SDS((2 * half, cols), F32), name=name,
                          compiler_params=_cparams())(place, p, q, q, q)


def pair_adds(tag, gs, gots, place):
    return [pair_add(f"{tag}_pair_add_{i}", g, got, place) for i, (g, got) in enumerate(zip(gs, gots))]


def chip_adds(tag, pairs, qs, place):
    return [chip_add(f"{tag}_chip_add_{i}", p, q, place) for i, (p, q) in enumerate(zip(pairs, qs))]


def reduce_scatter_chips(tag, gs, place):
    pairs = pair_adds(tag, gs, run_plan(tag + "_pair_exchange", plan_pair_exchange(gs)), place)
    return run_plan(tag + "_pair_join", plan_pair_join(chip_adds(tag, pairs, run_plan(tag + "_chip_scatter", plan_chip_scatter(pairs)), place)))


BIG = [("w_out", (2, 512, 1024)), ("w_mem_kv", (2, 256, 1024)), ("s5_w_in", (1, 1024, 1024)), ("s5_w_glu", (1, 1536, 768)),
       ("mla_w_in", (1, 1024, 848)), ("mla_w_uq", (1, 512, 576)), ("mla_w_ukv", (1, 256, 768))]
SHARDED_SMALL = [("mla_q_lora_norm", (1, 128)), ("mla_kv_lora_norm", (1, 64))]
SMALL = [("ln_gain", (2, 1024)), ("mem_norm", (2, 1024)), ("xq_norm", (2, 128)), ("xk_norm", (2, 128)),
         ("s5_lambda_re", (1, 96, 64)), ("s5_lambda_im", (1, 96, 64)), ("s5_log_step", (1, 96)),
         ("s5_b_re", (1, 96, 64, 16)), ("s5_b_im", (1, 96, 64, 16)), ("s5_c_re", (1, 96, 16, 64)), ("s5_c_im", (1, 96, 16, 64)),
         ("s5_d", (1, 1536)), ("mla_q_nope_norm", (1, 128)), ("mla_k_nope_norm", (1, 128)), ("mla_q_rope_norm", (1, 64)),
         ("mla_k_rope_norm", (1, 64))]
WEIGHT_ORDER = ["ln_gain", "w_out", "mem_norm", "w_mem_kv", "xq_norm", "xk_norm", "s5_w_in", "s5_lambda_re", "s5_lambda_im",
                "s5_log_step", "s5_b_re", "s5_b_im", "s5_c_re", "s5_c_im", "s5_d", "s5_w_glu", "mla_w_in", "mla_q_lora_norm",
                "mla_kv_lora_norm", "mla_w_uq", "mla_w_ukv", "mla_q_nope_norm", "mla_k_nope_norm", "mla_q_rope_norm", "mla_k_rope_norm"]
MINOR_LAST = {"mla_w_in": (0, 2, 1), "mla_w_uq": (0, 2, 1), "s5_b_re": (0, 2, 3, 1), "s5_b_im": (0, 2, 3, 1),
              "s5_c_re": (0, 2, 3, 1), "s5_c_im": (0, 2, 3, 1)}
SMALL_FULL = SMALL + [(n, (1, 4 * s[1])) for n, s in SHARDED_SMALL]
N_SMALL = sum(math.prod(s) for _, s in SMALL_FULL)
SMALL_ROWS, SMALL_LANES = 128, 1024

PAIR_OUT, PAIR_MKV, PAIR_ROWS = 0, 512, 768


def stack_shards(w, dtype):
    pairs = [jnp.concatenate([w["w_out"][l], w["w_mem_kv"][l]], axis=0).astype(dtype) for l in range(2)]
    return ([w["s5_w_in"][0].astype(dtype)], [pairs[0], w["s5_w_glu"][0].astype(dtype)],
            [pairs[1], w["mla_w_ukv"][0].astype(dtype), jnp.pad(w["mla_w_in"][0].T.astype(dtype), ((0, IN1T_ROWS - MLA_IN // 4), (0, 0))),
             w["mla_w_uq"][0].T.astype(dtype)])


def pair_views(pair):
    return {"w_out": Sharded(pair, "row", PAIR_OUT, 512), "w_mem_kv": Sharded(pair, "row", PAIR_MKV, 256)}


def grad_views():
    pair = SDS((4, PAIR_ROWS, 1024), BF16)
    return {"w_out": Sharded(pair, "row", PAIR_OUT, 512), "w_mem_kv": Sharded(pair, "row", PAIR_MKV, 256),
            "s5_w_in": Sharded(SDS((4, 1024, 1024), BF16), "col", 0, 1024), "s5_w_glu": Sharded(SDS((4, 1536, 768), BF16), "col", 0, 1536),
            "mla_w_ukv": Sharded(SDS((4, 256, 768), BF16), "col", 0, 256)}


IN1T_ROWS = 864


def in1_rows_permute(wt):
    o1, o2, o3, o4 = QL, QL + KVL, QL + KVL + ROPE, QL + KVL + ROPE + XQW
    return jnp.concatenate([wt[o4:], wt[:o1], wt[o3:o4], wt[o1:o2], wt[o2:o3], jnp.zeros((MLA_IN_P - MLA_IN, wt.shape[1]), wt.dtype)], axis=0)


def in1_rows_unpermute(d):
    return jnp.concatenate([d[2048:2560], d[3072:3328], d[3328:3392], d[2560:3072], d[:2048]], axis=0)


def uq_rows_permute(wt):
    w3 = wt.reshape(MH, NOPE + ROPE, wt.shape[1])
    return jnp.concatenate([w3[:, :NOPE].reshape(MH * NOPE, wt.shape[1]), w3[:, NOPE:].reshape(MH * ROPE, wt.shape[1])], axis=0)


def uq_rows_unpermute(d):
    dn = d[:MH * NOPE].reshape(MH, NOPE, d.shape[1])
    dr = d[MH * NOPE:].reshape(MH, ROPE, d.shape[1])
    return jnp.concatenate([dn, dr], axis=1).reshape(MH * (NOPE + ROPE), d.shape[1])


def time_permute(a):
    return a.reshape(SEG, SEG_LEN, a.shape[-1]).transpose(1, 0, 2).reshape(L, a.shape[-1])


def time_unpermute(a):
    return a.reshape(SEG_LEN, SEG, a.shape[-1]).transpose(1, 0, 2).reshape(L, a.shape[-1])


def mem_branch_fwd(tag, mem, mem_norm, w_mem_kv, xk_norm):
    mn = row_fwd(tag + "_mem_rms", fn_rms, ML, ML, [(mem, D, 0)], [mem_norm], [(D, BF16)])[0]
    kv = matmul(tag + "_mem_kv", mn, w_mem_kv, "nn", F32)
    kn = row_fwd(tag + "_mem_knorm", fn_mem_k, ML, ML, [(kv, XQW, 0)], [xk_norm], [(XQW, F32)])[0]
    return mn, kv, kn


def mem_branch_bwd(tag, mem, mem_norm, w_mem_kv, xk_norm, mn, kv, dkn, dv, g_view, g_wide):
    dk, dxk = row_bwd(tag + "_mem_knorm_bwd", fn_mem_k, ML, ML, [(kv, XQW, 0)], [xk_norm], [(dkn, XQW, 0)], [True], [True])
    dkv = jnp.concatenate([dk, dv], axis=1)
    dmn = matmul(tag + "_mem_kv_dx", dkv, w_mem_kv, "nt", F32)
    g_wide = matmul(tag + "_mem_kv_dw", mn, dkv, "tn", out=g_view, into=g_wide)
    dmem_norm = row_bwd(tag + "_mem_rms_bwd", fn_rms, ML, ML, [(mem, D, 0)], [mem_norm], [(dmn, D, 0)], [False], [True])[0]
    return g_wide, dmem_norm, dxk


def mem_attn_fwd(tag, proj, cb, kn, kv, xq_norm):
    return row_fwd(tag + "_mem_attn", fn_mem_attn, L, ROW_TILE, [(proj, XQW, cb)], [kn, kv[:, XQW:], xq_norm], [(XQW, F32)])[0]


def mem_attn_bwd(tag, proj, cb, kn, kv, xq_norm, dmo, dproj):
    place = {"cols": proj.shape[1], "cb": cb, "into": dproj, "dtype": dproj.dtype}
    return row_bwd(tag + "_mem_attn_bwd", fn_mem_attn, L, ROW_TILE, [(proj, XQW, cb)], [kn, kv[:, XQW:], xq_norm], [(dmo, XQW, 0)],
                   [place], [True, True, True])


def device_step(x, mem, positions, target, small, env, hooks=None):
    hooks = hooks or {}

    def plans_for(name):
        return hooks[("plans", name)](env) if ("plans", name) in hooks else ()

    def around(when, name, last=None):
        if (when, name) in hooks:
            hooks[(when, name)](env, last)

    g = {}
    gw = grad_views()
    ln, mem_norm, xq_norm, xk_norm = small["ln_gain"], small["mem_norm"], small["xq_norm"], small["xk_norm"]

    lre, lim = small["s5_lambda_re"][0], small["s5_lambda_im"][0]
    ls = small["s5_log_step"].reshape(SG, 1)
    one = pl.BlockSpec((SG, SP), lambda i: (0, 0))
    col = pl.BlockSpec((SG, 1), lambda i: (0, 0))
    disc_ins = [(lre, one), (lim, one), (ls, col)]
    a_re, a_im, coef_re, coef_im = stage("s5_disc", fn_s5_disc, (1,), disc_ins, [(SDS((SG, SP), F32), one)] * 4)
    b_re, b_im = small["s5_b_re"].reshape(SN, SC), small["s5_b_im"].reshape(SN, SC)
    c_re, c_im = small["s5_c_re"].reshape(PW, SP), small["s5_c_im"].reshape(PW, SP)
    bmat_rows = [(b_re, SC, 0), (b_im, SC, 0), (coef_re.reshape(SN, 1), 1, 0), (coef_im.reshape(SN, 1), 1, 0)]
    wb_re, wb_im = row_fwd("s5_bmat", fn_s5_bmat, SN, BMAT_TILE, bmat_rows, [], [(128, F32)] * 2)
    cmat_rows = [(c_re, SP, 0), (c_im, SP, 0)]
    wc_re, wc_im = row_fwd("s5_cmat", fn_s5_cmat, PW, CMAT_TILE, cmat_rows, [], [(512, F32)] * 2)
    a_re_v, a_im_v = a_re.reshape(1, SN), a_im.reshape(1, SN)
    s5_d = small["s5_d"]

    xp = time_permute(x)
    h0 = row_fwd("l0_rms", fn_rms, L, ROW_TILE, [(xp, D, 0)], [ln[0:1]], [(D, BF16)])[0]
    around("before", "l0_in", wb_re)
    w_in0 = Sharded(env["in0"], "col", 0, 1024)
    proj0 = matmul("l0_in", h0, w_in0, "nn")
    s_re, s_im, g0 = s5_forward(proj0, wb_re, wb_im, wc_re, wc_im, a_re_v, a_im_v, s5_d, plans=plans_for("s5_forward"))
    around("before", "l0_glu", g0)
    w0 = dict(pair_views(env["pair0"]), s5_w_glu=Sharded(env["glu"], "col", 0, 1536))
    z0 = matmul("l0_glu", g0, w0["s5_w_glu"], "nn", plans=plans_for("l0_glu"))
    mn0, kv0, kn0 = mem_branch_fwd("l0", mem, mem_norm[0:1], w0["w_mem_kv"], xk_norm[0:1])
    mo0 = mem_attn_fwd("l0", proj0, 3, kn0, kv0, xq_norm[0:1])
    o0 = row_fwd("l0_merge", fn_merge_glu, L, ROW_TILE, [(z0, 2 * PW, 0), (mo0, XQW, 0), (proj0, BW, 1)], [], [(BW, BF16)])[0]
    around("before", "l0_out", o0)
    x1p = matmul("l0_out", o0, w0["w_out"], "nn", F32, add=xp, plans=plans_for("l0_out"))
    around("after", "l0_out", x1p)
    x1 = time_unpermute(x1p)

    w1 = dict(pair_views(env["pair1"]), mla_w_ukv=Sharded(env["ukv"], "col", 0, 256))
    w_in1, w_uq = env["w_in1"], env["w_uq"]
    h1 = row_fwd("l1_rms", fn_rms, L, ROW_TILE, [(x1, D, 0)], [ln[1:2]], [(D, BF16)])[0]
    proj1 = matmul("l1_in", h1, w_in1, "nt")
    qln, kvln = env["q_lora_norm"].reshape(1, QL), env["kv_lora_norm"].reshape(1, KVL)
    cqn = row_fwd("l1_q_lora_rms", fn_rms, L, ROW_TILE, [(proj1, QL, 4)], [qln], [(QL, BF16)])[0]
    ckvn = row_fwd("l1_kv_lora_rms", fn_rms, L, ROW_TILE, [(proj1, KVL, 12)], [kvln], [(KVL, BF16)])[0]
    q = matmul("l1_uq", cqn, w_uq, "nt")
    kv = matmul("l1_ukv", ckvn, w1["mla_w_ukv"], "nn")
    inv_freq = ROPE_THETA ** (-jnp.arange(ROPE // 2, dtype=F32) / (ROPE // 2))
    ang = positions.astype(F32)[:, None] * inv_freq
    cos2 = jnp.tile(jnp.cos(ang), (1, 4))
    sin_signed = jnp.tile(jnp.concatenate([-jnp.sin(ang), jnp.sin(ang)], axis=1), (1, 2))
    qnn, knn = small["mla_q_nope_norm"], small["mla_k_nope_norm"]
    qrn, krn = jnp.tile(small["mla_q_rope_norm"], (1, 2)), jnp.tile(small["mla_k_rope_norm"], (1, 2))
    tp = 256
    prep_ins = [(q, rspec(tp, MH * (NOPE + ROPE))), (kv, rspec(tp, MH * 256)), (proj1, rspec(tp, 128, 26)),
                (cos2, rspec(tp, 128)), (sin_signed, rspec(tp, 128))] + [(a, cspec((1, 128))) for a in (qnn, knn, qrn, krn)]
    hq_spec = pl.BlockSpec((MH, tp, 256), lambda i: (0, i, 0))
    hv_spec = pl.BlockSpec((MH, tp, 128), lambda i: (0, i, 0))
    qf, kf, vh = stage("l1_mla_prep", fn_mla_prep, (L // tp,), prep_ins,
                       [(SDS((MH, L, 256), BF16), hq_spec), (SDS((MH, L, 256), BF16), hq_spec), (SDS((MH, L, 128), BF16), hv_spec)])
    attn, attn_lse = causal_attn(qf, kf, vh)
    mn1, kv1, kn1 = mem_branch_fwd("l1", mem, mem_norm[1:2], w1["w_mem_kv"], xk_norm[1:2])
    mo1 = mem_attn_fwd("l1", proj1, 5, kn1, kv1, xq_norm[1:2])
    o1 = row_fwd("l1_merge", fn_merge, L, ROW_TILE, [(attn, PW, 0), (mo1, XQW, 0), (proj1, BW, 0)], [], [(BW, BF16)])[0]
    x2 = matmul("l1_out", o1, w1["w_out"], "nn", F32, add=x1)
    dx2, loss = loss_and_grad(x2, target)
    around("after", "loss", loss)

    do1 = matmul("l1_out_dx", dx2, w1["w_out"], "nt")
    g_pair1 = matmul("l1_out_dw", o1, dx2, "tn", out=gw["w_out"])
    dattn, dmo1, dproj1 = row_bwd("l1_merge_bwd", fn_merge, L, ROW_TILE, [(attn, PW, 0), (mo1, XQW, 0), (proj1, BW, 0)], [],
                                  [(do1, BW, 0)], [True, True, {"cols": MLA_IN_P, "cb": 0, "dtype": BF16}], [])
    dproj1, dkn1, dv1, dxqn1 = mem_attn_bwd("l1", proj1, 5, kn1, kv1, xq_norm[1:2], dmo1, dproj1)
    env["g_pair1"], dmem_norm1, dxk1 = mem_branch_bwd("l1", mem, mem_norm[1:2], w1["w_mem_kv"], xk_norm[1:2], mn1, kv1, dkn1, dv1,
                                                      gw["w_mem_kv"], g_pair1)
    dqf, dkf, dvh = causal_attn_bwd(qf, kf, vh, attn, attn_lse, dattn)
    prep_diffs = [("row", SDS((L, MH * (NOPE + ROPE)), BF16), rspec(tp, MH * (NOPE + ROPE))), ("row", SDS((L, MH * 256), BF16), rspec(tp, MH * 256)),
                  ("row", SDS((L, MLA_IN_P), BF16), rspec(tp, 128, 26), {"into": dproj1}), None, None] + [("acc", (0,))] * 4
    dq, dkv, dproj1, dqnn, dknn, dqrn, dkrn = stage_bwd("l1_mla_prep_bwd", fn_mla_prep, (L // tp,), prep_ins,
                                                        [(dqf, hq_spec), (dkf, hq_spec), (dvh, hv_spec)], prep_diffs)
    dcqn = matmul("l1_uq_dx", dq, w_uq, "nn")
    env["dw_uq"] = matmul("l1_uq_dw", dq, cqn, "tn")
    dckvn = matmul("l1_ukv_dx", dkv, w1["mla_w_ukv"], "nt")
    env["g_ukv"] = matmul("l1_ukv_dw", ckvn, dkv, "tn", out=gw["mla_w_ukv"])
    dproj1, dqln = row_bwd("l1_q_lora_rms_bwd", fn_rms, L, ROW_TILE, [(proj1, QL, 4)], [qln], [(dcqn, QL, 0)],
                           [{"cols": MLA_IN_P, "cb": 4, "into": dproj1, "dtype": BF16}], [True])
    dproj1, dkvln = row_bwd("l1_kv_lora_rms_bwd", fn_rms, L, ROW_TILE, [(proj1, KVL, 12)], [kvln], [(dckvn, KVL, 0)],
                            [{"cols": MLA_IN_P, "cb": 12, "into": dproj1, "dtype": BF16}], [True])
    env["dw_in1"] = matmul("l1_in_dw", dproj1, h1, "tn")
    dh1 = matmul("l1_in_dx", dproj1, w_in1, "nn", plans=plans_for("l1_in_dx"))
    around("after", "l1_in_dx", dh1)
    dx1, dln1 = row_bwd("l1_rms_bwd", fn_rms, L, ROW_TILE, [(x1, D, 0)], [ln[1:2]], [(dh1, D, 0)], [{"add": (dx2, D, 0)}], [True])
    dx1p = time_permute(dx1)

    do0 = matmul("l0_out_dx", dx1p, w0["w_out"], "nt", plans=plans_for("l0_out_dx"))
    g_pair0 = matmul("l0_out_dw", o0, dx1p, "tn", out=gw["w_out"])
    dz0, dmo0, dproj0 = row_bwd("l0_merge_bwd", fn_merge_glu, L, ROW_TILE, [(z0, 2 * PW, 0), (mo0, XQW, 0), (proj0, BW, 1)], [],
                                [(do0, BW, 0)], [{"dtype": BF16}, True, {"cols": 2 * BW, "cb": 1, "dtype": BF16}], [])
    dproj0, dkn0, dv0, dxqn0 = mem_attn_bwd("l0", proj0, 3, kn0, kv0, xq_norm[0:1], dmo0, dproj0)
    env["g_pair0"], dmem_norm0, dxk0 = mem_branch_bwd("l0", mem, mem_norm[0:1], w0["w_mem_kv"], xk_norm[0:1], mn0, kv0, dkn0, dv0,
                                                      gw["w_mem_kv"], g_pair0)
    env["g_glu"] = matmul("l0_glu_dw", g0, dz0, "tn", out=gw["s5_w_glu"], plans=plans_for("l0_glu_dw"))
    dg0 = matmul("l0_glu_dx", dz0, w0["s5_w_glu"], "nt", plans=plans_for("l0_glu_dx"))
    around("before", "s5_backward", dg0)
    dproj0, dd, dwc_re, dwc_im, dwb_re, dwb_im, da_re, da_im = s5_backward(dg0, proj0, s_re, s_im, wb_re, wb_im, wc_re, wc_im,
                                                                           a_re_v, a_im_v, s5_d, dproj0, plans=plans_for("s5_backward"))
    around("after", "s5_backward", dd)
    env["g_in0"] = matmul("l0_in_dw", h0, dproj0, "tn", out=gw["s5_w_in"], plans=plans_for("l0_in_dw"))
    dh0 = matmul("l0_in_dx", dproj0, w_in0, "nt", plans=plans_for("l0_in_dx"))
    around("after", "l0_in_dx", dh0)
    dxp, dln0 = row_bwd("l0_rms_bwd", fn_rms, L, ROW_TILE, [(xp, D, 0)], [ln[0:1]], [(dh0, D, 0)], [{"add": (dx1p, D, 0)}], [True])
    grad_x = time_unpermute(dxp)

    db_re, db_im, dcoef_re, dcoef_im = row_bwd("s5_bmat_bwd", fn_s5_bmat, SN, BMAT_TILE, bmat_rows, [], [(dwb_re, 128, 0), (dwb_im, 128, 0)],
                                               [True] * 4, [], plans=plans_for("s5_bmat_bwd"))
    dc_re, dc_im = row_bwd("s5_cmat_bwd", fn_s5_cmat, PW, CMAT_TILE, cmat_rows, [], [(dwc_re, 512, 0), (dwc_im, 512, 0)], [True] * 2, [],
                           plans=plans_for("s5_cmat_bwd"))
    disc_cts = [(da_re.reshape(SG, SP), one), (da_im.reshape(SG, SP), one), (dcoef_re.reshape(SG, SP), one), (dcoef_im.reshape(SG, SP), one)]
    dlre, dlim, dls = stage_bwd("s5_disc_bwd", fn_s5_disc, (1,), disc_ins, disc_cts, [("acc", (0,))] * 3)

    g["ln_gain"] = jnp.concatenate([dln0, dln1], axis=0)
    g["mem_norm"] = jnp.concatenate([dmem_norm0, dmem_norm1], axis=0)
    g["xq_norm"] = jnp.concatenate([dxqn0, dxqn1], axis=0)
    g["xk_norm"] = jnp.concatenate([dxk0, dxk1], axis=0)
    g["s5_lambda_re"], g["s5_lambda_im"], g["s5_log_step"] = dlre, dlim, dls
    g["s5_b_re"], g["s5_b_im"], g["s5_c_re"], g["s5_c_im"] = db_re, db_im, dc_re, dc_im
    g["s5_d"] = dd
    g["mla_q_lora_norm"], g["mla_kv_lora_norm"] = dqln, dkvln
    g["mla_q_nope_norm"], g["mla_k_nope_norm"] = dqnn, dknn
    g["mla_q_rope_norm"] = dqrn[:, :ROPE] + dqrn[:, ROPE:]
    g["mla_k_rope_norm"] = dkrn[:, :ROPE] + dkrn[:, ROPE:]
    return loss, grad_x, g


def kernel(x, mem, positions, ln_gain, w_out, mem_norm, w_mem_kv, xq_norm, xk_norm, s5_w_in, s5_lambda_re, s5_lambda_im, s5_log_step, s5_b_re, s5_b_im, s5_c_re, s5_c_im, s5_d, s5_w_glu, mla_w_in, mla_q_lora_norm, mla_kv_lora_norm, mla_w_uq, mla_w_ukv, mla_q_nope_norm, mla_k_nope_norm, mla_q_rope_norm, mla_k_rope_norm, loss_target, m_ln_gain, m_w_out, m_mem_norm, m_w_mem_kv, m_xq_norm, m_xk_norm, m_s5_w_in, m_s5_lambda_re, m_s5_lambda_im, m_s5_log_step, m_s5_b_re, m_s5_b_im, m_s5_c_re, m_s5_c_im, m_s5_d, m_s5_w_glu, m_mla_w_in, m_mla_q_lora_norm, m_mla_kv_lora_norm, m_mla_w_uq, m_mla_w_ukv, m_mla_q_nope_norm, m_mla_k_nope_norm, m_mla_q_rope_norm, m_mla_k_rope_norm, v_ln_gain, v_w_out, v_mem_norm, v_w_mem_kv, v_xq_norm, v_xk_norm, v_s5_w_in, v_s5_lambda_re, v_s5_lambda_im, v_s5_log_step, v_s5_b_re, v_s5_b_im, v_s5_c_re, v_s5_c_im, v_s5_d, v_s5_w_glu, v_mla_w_in, v_mla_q_lora_norm, v_mla_kv_lora_norm, v_mla_w_uq, v_mla_w_ukv, v_mla_q_nope_norm, v_mla_k_nope_norm, v_mla_q_rope_norm, v_mla_k_rope_norm):
    args = dict(locals())
    wts = {n: args[n] for n in WEIGHT_ORDER}
    mom = {n: args["m_" + n] for n in WEIGHT_ORDER}
    var = {n: args["v_" + n] for n in WEIGHT_ORDER}

    chip = 2 * lax.axis_index("x") + lax.axis_index("y")
    place = jnp.stack([chip, lax.axis_index("c")]).astype(jnp.int32)

    def own_slot(gathered, shards):
        return [lax.dynamic_update_slice(g, s[None], (chip, 0, 0)) for g, s in zip(gathered, shards)]

    groups = list(stack_shards(wts, BF16))
    groups[2].append(jnp.concatenate([mla_q_lora_norm, jnp.pad(mla_kv_lora_norm, ((0, 0), (0, 64))), jnp.zeros((14, 128), F32)], axis=0))
    over_ici = [plan_gather_ici(shards) for shards in groups]
    _SCHEDULE_BEHIND.clear()
    schedule_behind(split_start("gather_start", over_ici))
    env, hooks, passed_on = {}, {}, {}

    def arrived(k, after, pass_now):
        passing = plan_gather_pass(split_wait(f"gather_wait_{k}", over_ici[k], after))
        passed_on[k] = passing
        return own_slot(run_plan(f"gather_pass_{k}", passing), groups[k]) if pass_now else None

    def need_in0(env, last):
        env["in0"], = arrived(0, last, True)

    def need_layer0(env, last):
        env["pair0"], env["glu"] = arrived(1, last, True)

    def need_layer1(env, last):
        arrived(2, last, False)

    def layer1_weights(env, last):
        pair1, ukv, in1, uq, norms = own_slot(passed_on[2].results, groups[2])
        env.update(pair1=pair1, ukv=ukv, w_in1=in1_rows_permute(in1[:, :MLA_IN // 4].reshape(MLA_IN, D)), w_uq=uq_rows_permute(uq.reshape(-1, QL)),
                   q_lora_norm=norms[:, 0, :], kv_lora_norm=norms[:, 1, :64])

    hooks["before", "l0_in"], hooks["before", "l0_glu"], hooks["before", "l0_out"] = need_in0, need_layer0, need_layer1
    hooks["plans", "l0_out"], hooks["after", "l0_out"] = (lambda env: [passed_on[2]]), layer1_weights

    rs = {}

    def swap(k, gs):
        rs[k, "g"], rs[k, "swap"] = gs, plan_pair_exchange(gs)
        return rs[k, "swap"]

    def start_scatter(k):
        rs[k, "pairs"] = pair_adds(f"rs{k}", rs[k, "g"], rs[k, "swap"].results, place)
        rs[k, "scatter"] = plan_chip_scatter(rs[k, "pairs"])
        schedule_behind(split_start(f"rs{k}_scatter_start", [rs[k, "scatter"]]))

    def join(k, after):
        rs[k, "join"] = plan_pair_join(chip_adds(f"rs{k}", rs[k, "pairs"], split_wait(f"rs{k}_scatter_wait", rs[k, "scatter"], after), place))
        return rs[k, "join"]

    def layer1_gradients(env):
        g_in1 = jnp.pad(in1_rows_unpermute(env["dw_in1"]).reshape(4, MLA_IN // 4, D), ((0, 0), (0, IN1T_ROWS - MLA_IN // 4), (0, 0)))
        return [swap(1, [env["g_pair1"], env["g_ukv"], g_in1, uq_rows_unpermute(env["dw_uq"]).reshape(4, -1, QL)])]

    hooks["plans", "l1_in_dx"] = layer1_gradients
    hooks["after", "l1_in_dx"] = lambda env, last: start_scatter(1)
    hooks["plans", "l0_glu_dx"] = lambda env: [swap(0, [env["g_pair0"], env["g_glu"]])]

    def before_s5_backward(env, last):
        start_scatter(0)
        env["join1"] = join(1, last)

    hooks["before", "s5_backward"] = before_s5_backward
    hooks["plans", "s5_backward"] = lambda env: [env["join1"]]
    hooks["after", "s5_backward"] = lambda env, last: env.update(join0=join(0, last))
    hooks["plans", "l0_in_dx"] = lambda env: [swap(2, [env["g_in0"]]), env["join0"]]
    hooks["after", "l0_in_dx"] = lambda env, last: start_scatter(2)

    def total_loss(env, local):
        env["loss"] = lax.psum(local[0, 0], MESH_AXES)
        schedule_behind(env["loss"].reshape(1, 1))

    hooks["after", "loss"] = total_loss
    small = {n: wts[n] for n, _ in SMALL}
    loss, grad_x, g = device_step(x[0], mem[0], positions[0], loss_target[0], small, env, hooks)
    loss = env["loss"]
    r_in0, = run_plan("rs2_pair_join", join(2, g["s5_log_step"]))
    (r_pair1, r_ukv, r_in1, r_uq), (r_pair0, r_glu) = (rs[k, "join"].results for k in (1, 0))

    small_flat = jnp.concatenate([g[n].reshape(-1) for n, _ in SMALL_FULL])
    g_small = jnp.pad(small_flat, (0, 4 * SMALL_ROWS * SMALL_LANES - N_SMALL)).astype(BF16).reshape(4, SMALL_ROWS, SMALL_LANES)
    r_small = reduce_scatter_chips("rs3", [g_small], place)[0]
    small_all = own_slot(all_gather_chips("gather_small_grads", [r_small]), [r_small])[0].reshape(-1)[:N_SMALL]

    grads = {"w_out": jnp.stack([r_pair0[:PAIR_MKV], r_pair1[:PAIR_MKV]]), "w_mem_kv": jnp.stack([r_pair0[PAIR_MKV:], r_pair1[PAIR_MKV:]]),
             "s5_w_in": r_in0[None], "s5_w_glu": r_glu[None], "mla_w_ukv": r_ukv[None], "mla_w_in": r_in1[:MLA_IN // 4].T[None], "mla_w_uq": r_uq.T[None]}
    off = 0
    for n, s in SMALL_FULL:
        grads[n] = small_all[off:off + math.prod(s)].reshape(s)
        off += math.prod(s)
    for n, s in SHARDED_SMALL:
        grads[n] = lax.dynamic_slice(grads[n], (0, chip * s[1]), s)

    delta, new_m, new_v = {}, {}, {}
    for n, s in BIG + [(n, s) for n, s in SMALL if len(s) == 4]:
        perm = MINOR_LAST.get(n, tuple(range(len(s))))
        turned = tuple(s[p] for p in perm)
        view = lambda a: jnp.transpose(a, perm).reshape(-1, turned[-1])
        res = adamw("adamw_" + n, view(wts[n]), view(grads[n]), view(mom[n]), view(var[n]))
        delta[n], new_m[n], new_v[n] = (jnp.transpose(r.reshape(turned), tuple(perm.index(i) for i in range(len(s)))) for r in res)
    small_names = [n for n, s in SMALL if len(s) < 4] + [n for n, _ in SHARDED_SMALL]
    n_own = sum(wts[n].size for n in small_names)
    rows_own = -(-n_own // (8 * 128)) * 8

    def pack_small(d):
        flat = jnp.concatenate([d[n].reshape(-1) for n in small_names])
        return jnp.pad(flat, (0, rows_own * 128 - n_own), constant_values=1.0).reshape(rows_own, 128)

    res = adamw("adamw_small", pack_small(wts), pack_small(grads), pack_small(mom), pack_small(var))
    off = 0
    for n in small_names:
        size = wts[n].size
        delta[n], new_m[n], new_v[n] = (r.reshape(-1)[off:off + size].reshape(wts[n].shape) for r in res)
        off += size

    return (loss, grad_x[None], *[grads[n] for n in WEIGHT_ORDER], *[delta[n] for n in WEIGHT_ORDER],
            *[new_m[n] for n in WEIGHT_ORDER], *[new_v[n] for n in WEIGHT_ORDER])
```

```python
import functools
import math

import jax
import jax.numpy as jnp
from jax import lax
from jax.experimental import pallas as pl
from jax.experimental.pallas import tpu as pltpu

F32, BF16 = jnp.float32, jnp.bfloat16
SDS = jax.ShapeDtypeStruct

D = 1024
L = 2048
ML = 256
BW = 2 * D
XQW = BW // 4
PW = BW - XQW
XH, XHD = 4, 128
SG, SC, SP = 96, 16, 64
SN = SG * SP
NOPE, ROPE, VD = 128, 64, 128
MH = 12
QL, KVL = 512, 256
EPS = 1e-6
ROPE_THETA = 10000.0
MLA_IN = QL + KVL + ROPE + XQW + BW
MLA_IN_P = 3456
ADAM_LR, ADAM_B1, ADAM_B2, ADAM_EPS, ADAM_WD, ADAM_STEP = 0.001, 0.9, 0.999, 1e-08, 0.01, 10

VMEM_LIMIT = 48 * 2**20
ROW_TILE = 512
BMAT_TILE, CMAT_TILE = 2048, 512
SEG = 8
SEG_LEN = L // SEG
MESH_AXES = ("x", "y", "c")


def _cparams():
    return pltpu.CompilerParams(vmem_limit_bytes=VMEM_LIMIT)


def _dg(a, b, ca, cb):
    return lax.dot_general(a.astype(BF16), b.astype(BF16), (((ca,), (cb,)), ((), ())), preferred_element_type=F32)


@jax.custom_vjp
def mm_nn(a, b):
    return _dg(a, b, 1, 0)


mm_nn.defvjp(lambda a, b: (_dg(a, b, 1, 0), (a, b)), lambda res, g: (_dg(g, res[1], 1, 1), _dg(res[0], g, 0, 0)))


@jax.custom_vjp
def mm_nt(a, b):
    return _dg(a, b, 1, 1)


mm_nt.defvjp(lambda a, b: (_dg(a, b, 1, 1), (a, b)), lambda res, g: (_dg(g, res[1], 1, 0), _dg(g, res[0], 0, 0)))


@functools.partial(jax.custom_vjp, nondiff_argnums=(1,))
def lane_roll(x, shift):
    return pltpu.roll(x, shift, 1)


lane_roll.defvjp(lambda x, shift: (pltpu.roll(x, shift, 1), None),
                 lambda shift, _, g: (pltpu.roll(g, (128 - shift) % 128, 1),))


def rms(x, g):
    return x * lax.rsqrt(jnp.mean(x * x, axis=-1, keepdims=True) + EPS) * g


@jax.custom_vjp
def softmax_rows(s):
    e = jnp.exp(s - jnp.max(s, axis=-1, keepdims=True))
    return e / jnp.sum(e, axis=-1, keepdims=True)


def _softmax_rows_fwd(s):
    p = softmax_rows(s)
    return p, p


def _softmax_rows_bwd(p, g):
    return (p * (g - jnp.sum(g * p, axis=-1, keepdims=True)),)


softmax_rows.defvjp(_softmax_rows_fwd, _softmax_rows_bwd)


def silu(x):
    return x * jax.nn.sigmoid(x)


ANY = pl.BlockSpec(memory_space=pl.ANY)
MESH_ID = pl.DeviceIdType.MESH


def _dma_sems(n):
    return [pltpu.SemaphoreType.DMA((n,)), pltpu.SemaphoreType.DMA((n,))]


class Plan:
    def __init__(self, operands, out_shape, aliases, n_sems, copies):
        self.operands, self.out_shape, self.aliases, self.n_sems, self.copies = list(operands), list(out_shape), aliases, n_sems, copies
        self.results = None


_SCHEDULE_BEHIND = []


def schedule_behind(token):
    _SCHEDULE_BEHIND.append(token)


def hosted_call(kern, *, name, grid, in_specs, out_specs, out_shape, operands, scratch_shapes=(), aliases=None, cparams=None, plans=(), deps=()):
    n_in, n_out, n_scr = len(in_specs), len(out_specs), len(scratch_shapes)
    p_in, p_out = [len(p.operands) for p in plans], [len(p.out_shape) for p in plans]
    deps = tuple(deps) + tuple(_SCHEDULE_BEHIND)
    _SCHEDULE_BEHIND.clear()
    all_aliases = dict(aliases or {})
    in_off, out_off = n_in, n_out
    for p, ni, no in zip(plans, p_in, p_out):
        all_aliases.update({in_off + i: out_off + o for i, o in p.aliases.items()})
        in_off, out_off = in_off + ni, out_off + no

    def body(*refs):
        pos, pins, pouts = n_in, [], []
        for ni in p_in:
            pins.append(refs[pos:pos + ni])
            pos += ni
        pos += len(deps)
        main_out = refs[pos:pos + n_out]
        pos += n_out
        for no in p_out:
            pouts.append(refs[pos:pos + no])
            pos += no
        main_scr = refs[pos:pos + n_scr]
        pos += n_scr
        if plans:
            ids = [pl.program_id(ax) for ax in range(len(grid))]
            first = functools.reduce(jnp.logical_and, [i == 0 for i in ids])
            last = functools.reduce(jnp.logical_and, [i == g - 1 for i, g in zip(ids, grid)])
            copies = [p.copies(pins[k], pouts[k], refs[pos + 2 * k], refs[pos + 2 * k + 1]) for k, p in enumerate(plans)]

            @pl.when(first)
            def _():
                for sends, _ in copies:
                    for cp in sends:
                        cp.start()

        kern(*refs[:n_in], *main_out, *main_scr)
        if plans:
            @pl.when(last)
            def _():
                for sends, recvs in copies:
                    for cp in recvs:
                        cp.wait_recv()
                    for cp in sends:
                        cp.wait_send()

    res = pl.pallas_call(body, grid=grid, in_specs=list(in_specs) + [ANY] * (sum(p_in) + len(deps)),
                         out_specs=list(out_specs) + [ANY] * sum(p_out), out_shape=list(out_shape) + [s for p in plans for s in p.out_shape],
                         scratch_shapes=list(scratch_shapes) + [s for p in plans for s in _dma_sems(p.n_sems)],
                         input_output_aliases=all_aliases, name=name, compiler_params=cparams or _cparams())(
        *operands, *[a for p in plans for a in p.operands], *deps)
    pos = n_out
    for p, no in zip(plans, p_out):
        p.results = list(res[pos:pos + no])
        pos += no
    return list(res[:n_out])


def _wide(v):
    return v.astype(F32) if v.dtype == BF16 else v


def stage(name, fn, grid, ins, outs):
    n_in = len(ins)

    def kern(*refs):
        res = fn(*[_wide(r[...]) for r in refs[:n_in]])
        for r, v in zip(refs[n_in:], res):
            r[...] = v.astype(r.dtype)

    return hosted_call(kern, name=name, grid=grid, in_specs=[s for _, s in ins], out_specs=[s for _, s in outs],
                       out_shape=[sd for sd, _ in outs], operands=[a for a, _ in ins])


def stage_bwd(name, fn, grid, ins, cts, diffs, plans=()):
    n_in, n_ct = len(ins), len(cts)
    didx = [i for i, d in enumerate(diffs) if d is not None]
    opts = {i: (diffs[i][3] if len(diffs[i]) > 3 else {}) for i in didx if diffs[i][0] == "row"}
    adds = [(i, opts[i]["add"]) for i in opts if "add" in opts[i]]
    intos = [(i, opts[i]["into"]) for i in opts if "into" in opts[i]]
    n_add, n_into = len(adds), len(intos)
    add_pos = {i: n_in + n_ct + k for k, (i, _) in enumerate(adds)}
    n_extra = n_in + n_ct + n_add + n_into

    def kern(*refs):
        vals = [_wide(r[...]) for r in refs[:n_in]]

        def f(*dv):
            full = list(vals)
            for i, v in zip(didx, dv):
                full[i] = v
            return fn(*full)

        _, vjp = jax.vjp(f, *[vals[i].astype(F32) for i in didx])
        gs = vjp(tuple(c[...].astype(F32) for c in refs[n_in:n_in + n_ct]))
        for o_ref, i, g in zip(refs[n_extra:], didx, gs):
            if diffs[i][0] == "row":
                if i in add_pos:
                    g = g + refs[add_pos[i]][...].astype(F32)
                o_ref[...] = g.astype(o_ref.dtype)
            else:
                first = functools.reduce(jnp.logical_and, [pl.program_id(ax) == 0 for ax in diffs[i][1]])

                @pl.when(first)
                def _():
                    o_ref[...] = g

                @pl.when(jnp.logical_not(first))
                def _():
                    o_ref[...] += g

    out_shape, out_specs = [], []
    for i in didx:
        if diffs[i][0] == "row":
            out_shape.append(diffs[i][1])
            out_specs.append(diffs[i][2])
        else:
            out_shape.append(SDS(ins[i][0].shape, F32))
            out_specs.append(ins[i][1])
    aliases = {n_in + n_ct + n_add + k: didx.index(i) for k, (i, _) in enumerate(intos)}
    in_specs = [s for _, s in ins] + [s for _, s in cts] + [s for _, (_, s) in adds] + [ANY] * n_into
    operands = [a for a, _ in ins] + [a for a, _ in cts] + [a for _, (a, _) in adds] + [a for _, a in intos]
    return hosted_call(kern, name=name, grid=grid, in_specs=in_specs, out_specs=out_specs, out_shape=out_shape, operands=operands,
                       aliases=aliases, plans=plans)


def rspec(tl, w, cb=0):
    return pl.BlockSpec((tl, w), lambda i: (i, cb))


def cspec(shape):
    return pl.BlockSpec(shape, lambda i: (0,) * len(shape))


def row_fwd(name, fn, rows, tl, row_ins, consts, outs):
    ins = [(a, rspec(tl, w, cb)) for a, w, cb in row_ins] + [(a, cspec(a.shape)) for a in consts]
    return stage(name, fn, (rows // tl,), ins, [(SDS((rows, w), dt), rspec(tl, w)) for w, dt in outs])


def row_bwd(name, fn, rows, tl, row_ins, consts, cts, row_diff, const_diff, plans=()):
    ins = [(a, rspec(tl, w, cb)) for a, w, cb in row_ins] + [(a, cspec(a.shape)) for a in consts]
    diffs = []
    for (a, w, cb), d in zip(row_ins, row_diff):
        if not d:
            diffs.append(None)
            continue
        d = d if isinstance(d, dict) else {}
        opts = {}
        if "add" in d:
            opts["add"] = (d["add"][0], rspec(tl, d["add"][1], d["add"][2]))
        if d.get("into") is not None:
            opts["into"] = d["into"]
        diffs.append(("row", SDS((rows, d.get("cols", w)), d.get("dtype", F32)), rspec(tl, w, d.get("cb", 0)), opts))
    diffs += [("acc", (0,)) if d else None for d in const_diff]
    return stage_bwd(name, fn, (rows // tl,), ins, [(a, rspec(tl, w, cb)) for a, w, cb in cts], diffs, plans=plans)


MATMUL_VMEM = 36 * 2**20
ADAMW_VMEM = 28 * 2**20


class Sharded:
    def __init__(self, arr, kind, roff, rows):
        self.arr, self.kind, self.roff, self.rows, self.n = arr, kind, roff, rows, arr.shape[2]
        self.shape = (rows, 4 * self.n) if kind == "col" else (4 * rows, self.n)

    def fits(self, t0, t1):
        return self.roff % t0 == 0 and self.rows % t0 == 0 and self.n % t1 == 0

    def spec(self, t0, t1, bidx):
        assert self.fits(t0, t1), (self.kind, self.roff, self.rows, self.n, t0, t1)
        r0 = self.roff // t0
        if self.kind == "col":
            per = self.n // t1
            return pl.BlockSpec((None, t0, t1), lambda *g: (bidx(*g)[1] // per, r0 + bidx(*g)[0], bidx(*g)[1] % per))
        per = self.rows // t0
        return pl.BlockSpec((None, t0, t1), lambda *g: (bidx(*g)[0] // per, r0 + bidx(*g)[0] % per, bidx(*g)[1]))


def _matmul_over_shards(name, a, b, mode, out_dtype, add, plans):
    m, k_dim = a.shape
    per = k_dim // 4
    n = b.n if mode == "nn" else b.rows
    assert k_dim % 4 == 0 and (b.rows == per and b.roff % per == 0 if mode == "nn" else b.n == per), (name, a.shape, mode)
    a_bytes, o_bytes = jnp.dtype(a.dtype).itemsize, jnp.dtype(out_dtype).itemsize

    def vmem(tm, tn):
        return 2 * (tm * k_dim * a_bytes + k_dim * tn * 2 + tm * tn * (o_bytes + (4 if add is not None else 0))) + 4 * tm * tn

    tiles = [(tm, tn) for tm in (2048, 1024, 512, 256) for tn in (1024, 512, 256, 128)
             if m % tm == 0 and n % tn == 0 and (mode == "nn" or b.roff % tn == 0) and vmem(tm, tn) <= MATMUL_VMEM]
    tm, tn = max(tiles, key=lambda t: (t[0] * t[1], t[0]))
    if mode == "nn":
        b_specs = [pl.BlockSpec((None, per, tn), lambda i, j, s=s: (s, b.roff // per, j)) for s in range(4)]
    else:
        b_specs = [pl.BlockSpec((None, tn, per), lambda i, j, s=s: (s, b.roff // tn + j, 0)) for s in range(4)]
    o_spec = pl.BlockSpec((tm, tn), lambda i, j: (i, j))
    cb = 0 if mode == "nn" else 1

    def kern(*refs):
        r = _dg(refs[0][:, 0:per], refs[1][...], 1, cb)
        for s in range(1, 4):
            r = r + _dg(refs[0][:, s * per:(s + 1) * per], refs[1 + s][...], 1, cb)
        if add is not None:
            r = r + refs[5][...]
        refs[-1][...] = r.astype(refs[-1].dtype)

    ins, specs = [a] + [b.arr] * 4, [pl.BlockSpec((tm, k_dim), lambda i, j: (i, 0))] + b_specs
    if add is not None:
        ins.append(add)
        specs.append(o_spec)
    return hosted_call(kern, name=name, grid=(m // tm, n // tn), in_specs=specs, out_specs=[o_spec], out_shape=[SDS((m, n), out_dtype)],
                       operands=ins, plans=plans)[0]


def matmul(name, a, b, mode, out_dtype=BF16, add=None, out=None, into=None, plans=()):
    if mode == "tn":
        k_dim, m = a.shape
    else:
        m, k_dim = a.shape
    n = b.shape[0] if mode == "nt" else b.shape[1]
    if isinstance(b, Sharded) and out is None and (mode, b.kind) in (("nn", "row"), ("nt", "col")):
        return _matmul_over_shards(name, a, b, mode, out_dtype, add, plans)
    b_fit = b.fits if isinstance(b, Sharded) else (lambda t0, t1: True)
    o_fit = out.fits if out is not None else (lambda t0, t1: True)
    a_bytes, b_bytes = jnp.dtype(a.dtype).itemsize, jnp.dtype(b.arr.dtype if isinstance(b, Sharded) else b.dtype).itemsize
    o_bytes = jnp.dtype(out_dtype if out is None else out.arr.dtype).itemsize

    def vmem(tm, tn):
        return 2 * (tm * k_dim * a_bytes + k_dim * tn * b_bytes + tm * tn * (o_bytes + (4 if add is not None else 0))) + 4 * tm * tn

    tiles = [(tm, tn) for tm in (2048, 1024, 512, 256, 128) for tn in (1024, 768, 512, 384, 256, 128)
             if m % tm == 0 and n % tn == 0 and (b_fit(tn, k_dim) if mode == "nt" else b_fit(k_dim, tn)) and o_fit(tm, tn)
             and vmem(tm, tn) <= MATMUL_VMEM]
    tm, tn = max(tiles, key=lambda t: t[0] * t[1])
    a_spec = pl.BlockSpec((k_dim, tm), lambda i, j: (0, i)) if mode == "tn" else pl.BlockSpec((tm, k_dim), lambda i, j: (i, 0))
    if isinstance(b, Sharded):
        b_spec = b.spec(tn, k_dim, lambda i, j: (j, 0)) if mode == "nt" else b.spec(k_dim, tn, lambda i, j: (0, j))
        b = b.arr
    else:
        b_spec = pl.BlockSpec((tn, k_dim), lambda i, j: (j, 0)) if mode == "nt" else pl.BlockSpec((k_dim, tn), lambda i, j: (0, j))
    o_spec = pl.BlockSpec((tm, tn), lambda i, j: (i, j))
    out_spec, out_shape = (o_spec, SDS((m, n), out_dtype)) if out is None else (out.spec(tm, tn, lambda i, j: (i, j)), out.arr)
    ca, cb = {"nn": (1, 0), "nt": (1, 1), "tn": (0, 0)}[mode]
    n_in = 2 + (add is not None)

    def kern(*refs):
        r = _dg(refs[0][...], refs[1][...], ca, cb)
        if add is not None:
            r = r + refs[2][...]
        refs[-1][...] = r.astype(refs[-1].dtype)

    ins, specs = [a, b], [a_spec, b_spec]
    if add is not None:
        ins.append(add)
        specs.append(o_spec)
    if into is not None:
        ins.append(into)
        specs.append(ANY)
    return hosted_call(kern, name=name, grid=(m // tm, n // tn), in_specs=specs, out_specs=[out_spec], out_shape=[out_shape],
                       operands=ins, aliases={} if into is None else {n_in: 0}, plans=plans)[0]


SCAN_UNROLL = 8


def _cmul(ar, ai, br, bi):
    return ar * br - ai * bi, ar * bi + ai * br


def _sub_shift(x, down):
    row = lax.broadcasted_iota(jnp.int32, x.shape, 0)
    if down:
        return jnp.where(row == 0, 0.0, pltpu.roll(x, 1, 0))
    return jnp.where(row == SEG - 1, 0.0, pltpu.roll(x, SEG - 1, 0))


def _pow_seg_len(ar, ai):
    for _ in range(int(math.log2(SEG_LEN))):
        ar, ai = _cmul(ar, ai, ar, ai)
    return ar, ai


def _scan_in_place(sr, si, a_re, a_im):
    lanes = sr.shape[1]
    ar = jnp.broadcast_to(a_re, (SEG, lanes))
    ai = jnp.broadcast_to(a_im, (SEG, lanes))
    zero = jnp.zeros((SEG, lanes), F32)

    def local(i, carry):
        rows = pl.ds(pl.multiple_of(i * SEG, SEG), SEG)
        mr, mi = _cmul(ar, ai, carry[0], carry[1])
        nr, ni = mr + sr[rows, :], mi + si[rows, :]
        sr[rows, :] = nr
        si[rows, :] = ni
        return nr, ni

    fr, fi = lax.fori_loop(0, SEG_LEN, local, (zero, zero), unroll=SCAN_UNROLL)
    pr, pi = _pow_seg_len(ar, ai)
    ir, ii = zero, zero
    for _ in range(SEG - 1):
        mr, mi = _cmul(pr, pi, ir, ii)
        ir, ii = _sub_shift(mr + fr, True), _sub_shift(mi + fi, True)

    def carry_in(i, pw):
        rows = pl.ds(pl.multiple_of(i * SEG, SEG), SEG)
        cr, ci = _cmul(pw[0], pw[1], ir, ii)
        sr[rows, :] += cr
        si[rows, :] += ci
        return _cmul(pw[0], pw[1], ar, ai)

    lax.fori_loop(0, SEG_LEN, carry_in, (ar, ai), unroll=SCAN_UNROLL)


S5_LANES = 8 * SP
S5_BLOCKS = SN // S5_LANES


def _s5_specs(n=1):
    u_spec = pl.BlockSpec((L, n * 8 * SC), lambda j: (0, j))
    s_spec = pl.BlockSpec((L, n * S5_LANES), lambda j: (0, j))
    wb_spec = pl.BlockSpec((n * S5_LANES, 8 * SC), lambda j: (j, 0))
    wc_spec = pl.BlockSpec((n * 8 * SC, S5_LANES), lambda j: (j, 0))
    a_spec = pl.BlockSpec((1, n * S5_LANES), lambda j: (0, j))
    d_spec = pl.BlockSpec((1, n * 8 * SC), lambda j: (0, j))
    return u_spec, s_spec, wb_spec, wc_spec, a_spec, d_spec


S5_FWD_BLOCKS = 2


def s5_forward(proj, wb_re, wb_im, wc_re, wc_im, a_re, a_im, d, plans=()):
    n = S5_FWD_BLOCKS
    cols, lanes = 8 * SC, S5_LANES

    def kern(u_ref, wbr, wbi, wcr, wci, ar, ai, d_ref, sr_out, si_out, g_ref, sr, si):
        u = _wide(u_ref[...])
        for b in range(n):
            sr[:, b * lanes:(b + 1) * lanes], si[:, b * lanes:(b + 1) * lanes] = fn_s5_bu(
                u[:, b * cols:(b + 1) * cols], wbr[b * lanes:(b + 1) * lanes, :], wbi[b * lanes:(b + 1) * lanes, :])
        _scan_in_place(sr, si, ar[...], ai[...])
        for b in range(n):
            g_ref[:, b * cols:(b + 1) * cols] = fn_s5_out(
                sr[:, b * lanes:(b + 1) * lanes], si[:, b * lanes:(b + 1) * lanes], u[:, b * cols:(b + 1) * cols],
                d_ref[:, b * cols:(b + 1) * cols], wcr[b * cols:(b + 1) * cols, :], wci[b * cols:(b + 1) * cols, :])[0].astype(g_ref.dtype)
        sr_out[...] = sr[...].astype(sr_out.dtype)
        si_out[...] = si[...].astype(si_out.dtype)

    u_spec, s_spec, wb_spec, wc_spec, a_spec, d_spec = _s5_specs(n)
    return hosted_call(kern, name="s5_forward", grid=(S5_BLOCKS // n,), in_specs=[u_spec, wb_spec, wb_spec, wc_spec, wc_spec, a_spec, a_spec, d_spec],
                       out_specs=[s_spec, s_spec, u_spec], out_shape=[SDS((L, SN), BF16)] * 2 + [SDS((L, PW), BF16)],
                       operands=[proj, wb_re, wb_im, wc_re, wc_im, a_re, a_im, d], scratch_shapes=[pltpu.VMEM((L, n * lanes), F32)] * 2,
                       plans=plans)


def _adjoint_scan_in_place(lr, li, sr, si, a_re, a_im):
    lanes = lr.shape[1]
    ar = jnp.broadcast_to(a_re, (SEG, lanes))
    ai = -jnp.broadcast_to(a_im, (SEG, lanes))
    zero = jnp.zeros((SEG, lanes), F32)

    def local(k, carry):
        i = SEG_LEN - 1 - k
        rows = pl.ds(pl.multiple_of(i * SEG, SEG), SEG)
        mr, mi = _cmul(ar, ai, carry[0], carry[1])
        nr, ni = mr + lr[rows, :], mi + li[rows, :]
        lr[rows, :] = nr
        li[rows, :] = ni
        return nr, ni

    fr, fi = lax.fori_loop(0, SEG_LEN, local, (zero, zero), unroll=SCAN_UNROLL)
    pr, pi = _pow_seg_len(ar, ai)
    ir, ii = zero, zero
    for _ in range(SEG - 1):
        mr, mi = _cmul(pr, pi, ir, ii)
        ir, ii = _sub_shift(mr + fr, False), _sub_shift(mi + fi, False)

    def fix(rows, pw):
        cr, ci = _cmul(pw[0], pw[1], ir, ii)
        tr, ti = lr[rows, :] + cr, li[rows, :] + ci
        lr[rows, :] = tr
        li[rows, :] = ti
        return tr, ti

    def grad_a(tr, ti, spr, spi, acc):
        return acc[0] + tr * spr + ti * spi, acc[1] + ti * spr - tr * spi

    def carry_in(k, c):
        i = SEG_LEN - 1 - k
        rows = pl.ds(pl.multiple_of(i * SEG, SEG), SEG)
        prev = pl.ds(pl.multiple_of((i - 1) * SEG, SEG), SEG)
        tr, ti = fix(rows, (c[0], c[1]))
        acc = grad_a(tr, ti, sr[prev, :], si[prev, :], (c[2], c[3]))
        nr, ni = _cmul(c[0], c[1], ar, ai)
        return nr, ni, acc[0], acc[1]

    pwr, pwi, accr, acci = lax.fori_loop(0, SEG_LEN - 1, carry_in, (ar, ai, zero, zero), unroll=5)
    tr, ti = fix(pl.ds(0, SEG), (pwr, pwi))
    last = pl.ds((SEG_LEN - 1) * SEG, SEG)
    accr, acci = grad_a(tr, ti, _sub_shift(sr[last, :], True), _sub_shift(si[last, :], True), (accr, acci))
    return jnp.sum(accr, axis=0, keepdims=True), jnp.sum(acci, axis=0, keepdims=True)


S5_BWD_VMEM = 58 * 2**20


def s5_backward(dg, proj, s_re, s_im, wb_re, wb_im, wc_re, wc_im, a_re, a_im, d, dproj, plans=()):
    def kern(dg_ref, u_ref, sr_in, si_in, wbr, wbi, wcr, wci, ar, ai, d_ref, _, du_ref, dd_ref, dwcr, dwci, dwbr, dwbi, dar, dai, lr, li, sr, si):
        u = _wide(u_ref[...])
        sr[...], si[...] = _wide(sr_in[...]), _wide(si_in[...])
        _, vjp_out = jax.vjp(fn_s5_out, sr[...], si[...], u, d_ref[...], wcr[...], wci[...])
        lr[...], li[...], du_out, dd_ref[...], dwcr[...], dwci[...] = vjp_out((_wide(dg_ref[...]),))
        dar[...], dai[...] = _adjoint_scan_in_place(lr, li, sr, si, ar[...], ai[...])
        _, vjp_in = jax.vjp(fn_s5_bu, u, wbr[...], wbi[...])
        du_in, dwbr[...], dwbi[...] = vjp_in((lr[...], li[...]))
        du_ref[...] = (du_out + du_in).astype(du_ref.dtype)

    u_spec, s_spec, wb_spec, wc_spec, a_spec, d_spec = _s5_specs()
    outs = [(SDS(dproj.shape, dproj.dtype), u_spec), (SDS(d.shape, F32), d_spec), (SDS(wc_re.shape, F32), wc_spec), (SDS(wc_im.shape, F32), wc_spec),
            (SDS(wb_re.shape, F32), wb_spec), (SDS(wb_im.shape, F32), wb_spec), (SDS(a_re.shape, F32), a_spec), (SDS(a_im.shape, F32), a_spec)]
    return hosted_call(kern, name="s5_backward", grid=(S5_BLOCKS,),
                       in_specs=[u_spec, u_spec, s_spec, s_spec, wb_spec, wb_spec, wc_spec, wc_spec, a_spec, a_spec, d_spec, ANY],
                       out_specs=[sp for _, sp in outs], out_shape=[sd for sd, _ in outs], aliases={11: 0},
                       operands=[dg, proj, s_re, s_im, wb_re, wb_im, wc_re, wc_im, a_re, a_im, d, dproj],
                       scratch_shapes=[pltpu.VMEM((L, S5_LANES), F32)] * 4,
                       cparams=pltpu.CompilerParams(vmem_limit_bytes=S5_BWD_VMEM), plans=plans)


def fn_rms(x, g):
    return (rms(x, g),)


def fn_s5_disc(lre, lim, ls):
    step = jnp.exp(ls)
    e = jnp.exp(lre * step)
    a_re, a_im = e * jnp.cos(lim * step), e * jnp.sin(lim * step)
    den = lre * lre + lim * lim
    nr, ni = a_re - 1.0, a_im
    return a_re, a_im, (nr * lre + ni * lim) / den, (ni * lre - nr * lim) / den


def _group_mask(rows, cols, row_div, col_div):
    r = lax.broadcasted_iota(jnp.int32, (rows, cols), 0) // row_div % 8
    c = lax.broadcasted_iota(jnp.int32, (rows, cols), 1) // col_div
    return r == c


def _spread(x, mask):
    w = x.shape[1]
    copy = (lax.broadcasted_iota(jnp.int32, (w, 8 * w), 1) % w == lax.broadcasted_iota(jnp.int32, (w, 8 * w), 0)).astype(F32)
    return jnp.where(mask, jnp.dot(x, copy, precision=lax.Precision.HIGHEST, preferred_element_type=F32), 0.0)


def fn_s5_bmat(b_re, b_im, coef_re, coef_im):
    mask = _group_mask(b_re.shape[0], 8 * SC, SP, SC)
    return _spread(coef_re * b_re - coef_im * b_im, mask), _spread(coef_re * b_im + coef_im * b_re, mask)


def fn_s5_cmat(c_re, c_im):
    mask = _group_mask(c_re.shape[0], 8 * SP, SC, SP)
    return _spread(c_re, mask), _spread(c_im, mask)


def fn_s5_bu(u, wb_re, wb_im):
    return mm_nt(u, wb_re), mm_nt(u, wb_im)


def fn_s5_out(sr, si, u, d, wc_re, wc_im):
    y = mm_nt(sr, wc_re) - mm_nt(si, wc_im) + d * u
    return (jax.nn.gelu(y),)


def fn_merge_glu(z, mo, gate):
    yg = z[:, :PW] * jax.nn.sigmoid(z[:, PW:])
    return (jnp.concatenate([yg, mo], axis=1) * silu(gate),)


def fn_merge(prim, mo, gate):
    return (jnp.concatenate([prim, mo], axis=1) * silu(gate),)


def fn_mem_k(kv, g):
    return (jnp.concatenate([rms(kv[:, h * XHD:(h + 1) * XHD], g) for h in range(XH)], axis=1),)


def fn_mem_attn(xq, kn, v, g):
    outs = []
    for h in range(XH):
        sl = slice(h * XHD, (h + 1) * XHD)
        p = softmax_rows(mm_nt(rms(xq[:, sl], g), kn[:, sl]) * (XHD ** -0.5))
        outs.append(mm_nn(p, v[:, sl]))
    return (jnp.concatenate(outs, axis=1),)


def _half_rms(x, g):
    lo = lax.broadcasted_iota(jnp.int32, x.shape, 1) < ROPE
    x2 = x * x
    s_lo = jnp.sum(jnp.where(lo, x2, 0.0), axis=1, keepdims=True)
    s_hi = jnp.sum(jnp.where(lo, 0.0, x2), axis=1, keepdims=True)
    return x * lax.rsqrt(jnp.where(lo, s_lo, s_hi) / ROPE + EPS) * g


def _rope(x, cos2, sin_signed):
    first = lax.broadcasted_iota(jnp.int32, x.shape, 1) % ROPE < ROPE // 2
    return x * cos2 + jnp.where(first, lane_roll(x, 128 - ROPE // 2), lane_roll(x, ROPE // 2)) * sin_signed


def fn_mla_prep(q, kv, kr, cos2, sin_signed, qnn, knn, qrn, krn):
    lo = lax.broadcasted_iota(jnp.int32, kr.shape, 1) < ROPE
    kr_pad = jnp.where(lo, _rope(_half_rms(kr, krn), cos2, sin_signed), 0.0)
    qf, kf, vs = [], [], []
    for m in range(MH // 2):
        pair = _rope(_half_rms(q[:, MH * NOPE + 128 * m:MH * NOPE + 128 * (m + 1)], qrn), cos2, sin_signed)
        for h, rope_h in ((2 * m, pair), (2 * m + 1, lane_roll(pair, ROPE))):
            qf.append(jnp.concatenate([rms(q[:, NOPE * h:NOPE * (h + 1)], qnn), jnp.where(lo, rope_h, 0.0)], axis=1))
    for h in range(MH):
        kf.append(jnp.concatenate([rms(kv[:, 256 * h:256 * h + NOPE], knn), kr_pad], axis=1))
        vs.append(kv[:, 256 * h + NOPE:256 * (h + 1)])
    return jnp.stack(qf), jnp.stack(kf), jnp.stack(vs)


ATT_TQ = 512


def _attn_scores(q, kf):
    tq = q.shape[0]
    scale = (NOPE + ROPE) ** -0.5
    own = _dg(q, kf[-tq:], 1, 1) * scale
    own = jnp.where(lax.broadcasted_iota(jnp.int32, own.shape, 1) <= lax.broadcasted_iota(jnp.int32, own.shape, 0), own, jnp.finfo(F32).min)
    return own if kf.shape[0] == tq else jnp.concatenate([_dg(q, kf[:-tq], 1, 1) * scale, own], axis=1)


ATT_FWD_HEADS, ATT_BWD_HEADS = 4, 2


def _attn_specs(heads):
    q_spec = pl.BlockSpec((heads, ATT_TQ, 256), lambda h, i: (h, i, 0))
    k_spec = pl.BlockSpec((heads, L, 256), lambda h, i: (h, 0, 0))
    v_spec = pl.BlockSpec((heads, L, 128), lambda h, i: (h, 0, 0))
    o_spec = pl.BlockSpec((ATT_TQ, heads * 128), lambda h, i: (i, h))
    lse_spec = pl.BlockSpec((heads, ATT_TQ, 1), lambda h, i: (h, i, 0))
    return q_spec, k_spec, v_spec, o_spec, lse_spec


def _attn_branches(heads, body):
    i = pl.program_id(1)
    for t in range(L // ATT_TQ):
        @pl.when(i == t)
        def _(t=t):
            for hh in range(heads):
                body(hh, slice(hh * 128, (hh + 1) * 128), (t + 1) * ATT_TQ)


def causal_attn(qf, kf, vh):
    def kern(q_ref, k_ref, v_ref, o_ref, lse_ref):
        def body(hh, lanes, keys):
            s = _attn_scores(q_ref[hh], k_ref[hh, :keys, :])
            m = jnp.max(s, axis=-1, keepdims=True)
            e = jnp.exp(s - m)
            total = jnp.sum(e, axis=-1, keepdims=True)
            o_ref[:, lanes] = (_dg(e, v_ref[hh, :keys, :], 1, 0) / total).astype(o_ref.dtype)
            lse_ref[hh] = m + jnp.log(total)

        _attn_branches(ATT_FWD_HEADS, body)

    q_spec, k_spec, v_spec, o_spec, lse_spec = _attn_specs(ATT_FWD_HEADS)
    return pl.pallas_call(kern, grid=(MH // ATT_FWD_HEADS, L // ATT_TQ), in_specs=[q_spec, k_spec, v_spec], out_specs=[o_spec, lse_spec],
                          out_shape=[SDS((L, MH * VD), F32), SDS((MH, L, 1), F32)], name="l1_attn", compiler_params=_cparams())(qf, kf, vh)


def causal_attn_bwd(qf, kf, vh, out, lse, dout):
    scale = (NOPE + ROPE) ** -0.5

    def kern(q_ref, k_ref, v_ref, o_ref, lse_ref, do_ref, dq_ref, dk_ref, dv_ref):
        @pl.when(pl.program_id(1) == 0)
        def _():
            dk_ref[...] = jnp.zeros_like(dk_ref)
            dv_ref[...] = jnp.zeros_like(dv_ref)

        def body(hh, lanes, keys):
            q, k, v, do = q_ref[hh], k_ref[hh, :keys, :], v_ref[hh, :keys, :], do_ref[:, lanes]
            p = jnp.exp(_attn_scores(q, k) - lse_ref[hh])
            delta = jnp.sum(do * _wide(o_ref[:, lanes]), axis=-1, keepdims=True)
            dv_ref[hh, :keys, :] += _dg(p, do, 0, 0)
            ds = p * (_dg(do, v, 1, 1) - delta) * scale
            dq_ref[hh] = _dg(ds, k, 1, 0)
            dk_ref[hh, :keys, :] += _dg(ds, q, 0, 0)

        _attn_branches(ATT_BWD_HEADS, body)

    q_spec, k_spec, v_spec, o_spec, lse_spec = _attn_specs(ATT_BWD_HEADS)
    return pl.pallas_call(kern, grid=(MH // ATT_BWD_HEADS, L // ATT_TQ), in_specs=[q_spec, k_spec, v_spec, o_spec, lse_spec, o_spec],
                          out_specs=[q_spec, k_spec, v_spec], out_shape=[SDS(qf.shape, F32), SDS(kf.shape, F32), SDS(vh.shape, F32)],
                          name="l1_attn_bwd", compiler_params=_cparams())(qf, kf, vh, out, lse, dout)


def loss_and_grad(y, target, tl=512):
    def kern(y_ref, t_ref, dy_ref, loss_ref):
        d = y_ref[...] - t_ref[...]
        dy_ref[...] = d / D

        @pl.when(pl.program_id(0) == 0)
        def _():
            loss_ref[...] = jnp.zeros_like(loss_ref)

        loss_ref[...] += 0.5 * jnp.sum(jnp.sum(d * d, axis=1, keepdims=True), axis=0, keepdims=True) / D

    return pl.pallas_call(kern, grid=(L // tl,), in_specs=[rspec(tl, D), rspec(tl, D)], out_specs=[rspec(tl, D), cspec((1, 1))],
                          out_shape=[SDS((L, D), F32), SDS((1, 1), F32)], name="loss", compiler_params=_cparams())(y, target)


def adamw(name, w, g, m, v):
    rows, cols = w.shape
    block_row_bytes = 7 * 2 * 4 * max(cols, 128)
    tr = _row_tile(rows, min(2048, ADAMW_VMEM // block_row_bytes // 8 * 8), 8)

    def kern(w_ref, g_ref, m_ref, v_ref, d_ref, nm_ref, nv_ref):
        gg = g_ref[...]
        nm = ADAM_B1 * m_ref[...] + (1.0 - ADAM_B1) * gg
        nv = ADAM_B2 * v_ref[...] + (1.0 - ADAM_B2) * jnp.square(gg)
        m_hat = nm / (1.0 - ADAM_B1 ** ADAM_STEP)
        v_hat = nv / (1.0 - ADAM_B2 ** ADAM_STEP)
        d_ref[...] = -ADAM_LR * (m_hat / (jnp.sqrt(v_hat) + ADAM_EPS) + ADAM_WD * w_ref[...])
        nm_ref[...] = nm
        nv_ref[...] = nv

    spec = rspec(tr, cols)
    return hosted_call(kern, name=name, grid=(rows // tr,), in_specs=[spec] * 4, out_specs=[spec] * 3,
                       out_shape=[SDS((rows, cols), F32)] * 3, operands=[w, g, m, v])


def _row_tile(rows, cap=512, unit=16):
    return max(t for t in range(unit, cap + 1, unit) if rows % t == 0)


def _place():
    x, y, c = lax.axis_index("x"), lax.axis_index("y"), lax.axis_index("c")
    return x, y, c, [(1 - x, y), (x, 1 - y), (1 - x, 1 - y)]


def _row_chunks(rows, n, dtype):
    unit = 32 // jnp.dtype(dtype).itemsize
    base, extra = divmod(rows // unit, n)
    out, start = [], 0
    for k in range(n):
        size = (base + (k < extra)) * unit
        if size:
            out.append((start, size))
            start += size
    assert start == rows, (rows, unit)
    return out


PIECE_BYTES = 1 << 20


def _pieces(shapes_dtypes, rows_of):
    out = []
    for b, (shape, dtype) in enumerate(shapes_dtypes):
        rows = rows_of(shape)
        n = max(1, min(4, rows * shape[-1] * jnp.dtype(dtype).itemsize // PIECE_BYTES))
        out += [(b, st, sz) for st, sz in _row_chunks(rows, n, dtype)]
    return out


def all_gather_chips(name, shards):
    nb = len(shards)
    pieces = _pieces([(s.shape, s.dtype) for s in shards], lambda shape: shape[0] // 2)
    n = len(pieces)

    def body(*refs):
        x_refs, out_refs, send_sems, recv_sems = refs[:nb], refs[nb:2 * nb], refs[2 * nb], refs[2 * nb + 1]
        x, y, c, chips = _place()
        sibling = (x, y, 1 - c)
        mine = 2 * x + y

        def copy(sem, chip, cc, k, to, from_input=False):
            b, st, sz = pieces[k]
            rows_k = pl.ds(cc * (x_refs[b].shape[0] // 2) + st, sz)
            dst = out_refs[b].at[chip, rows_k, :]
            return pltpu.make_async_remote_copy(src_ref=x_refs[b].at[rows_k, :] if from_input else dst, dst_ref=dst,
                                                send_sem=send_sems.at[sem], recv_sem=recv_sems.at[sem], device_id=to, device_id_type=MESH_ID)

        order = [(k, j, 2 * cx + cy, (cx, cy, c)) for k in range(n) for j, (cx, cy) in enumerate(chips)]
        first = [copy(j * n + k, mine, c, k, to, from_input=True) for k, j, _, to in order]
        for cp in first:
            cp.start()
        passed = []
        for k, j, chip, _ in order:
            copy(j * n + k, chip, c, k, sibling).wait_recv()
            passed.append(copy((3 + j) * n + k, chip, c, k, sibling))
            passed[-1].start()
        for k, j, chip, _ in order:
            copy((3 + j) * n + k, chip, 1 - c, k, sibling).wait_recv()
        for cp in first + passed:
            cp.wait_send()

    return pl.pallas_call(body, in_specs=[ANY] * nb, out_specs=[ANY] * nb, out_shape=[SDS((4,) + s.shape, s.dtype) for s in shards],
                          scratch_shapes=_dma_sems(6 * n), name=name)(*shards)


def plan_gather_ici(shards):
    pieces = _pieces([(s.shape, s.dtype) for s in shards], lambda shape: shape[0] // 2)
    n = len(pieces)

    def copies(x_refs, out_refs, send_sems, recv_sems):
        x, y, c, chips = _place()
        mine = 2 * x + y

        def copy(j, k, chip, to, from_input):
            b, st, sz = pieces[k]
            rows_k = pl.ds(c * (x_refs[b].shape[0] // 2) + st, sz)
            dst = out_refs[b].at[chip, rows_k, :]
            return pltpu.make_async_remote_copy(src_ref=x_refs[b].at[rows_k, :] if from_input else dst, dst_ref=dst, send_sem=send_sems.at[j * n + k],
                                                recv_sem=recv_sems.at[j * n + k], device_id=to, device_id_type=MESH_ID)

        order = [(k, j, 2 * cx + cy, (cx, cy, c)) for k in range(n) for j, (cx, cy) in enumerate(chips)]
        return [copy(j, k, mine, to, True) for k, j, _, to in order], [copy(j, k, chip, to, False) for k, j, chip, to in order]

    return Plan(shards, [SDS((4,) + s.shape, s.dtype) for s in shards], {}, 3 * n, copies)


def plan_gather_pass(gathered):
    pieces = _pieces([(g.shape[1:], g.dtype) for g in gathered], lambda shape: shape[0] // 2)
    n = len(pieces)

    def copies(_, out_refs, send_sems, recv_sems):
        x, y, c, chips = _place()

        def copy(j, k, chip, cc):
            b, st, sz = pieces[k]
            rows_k = out_refs[b].at[chip, pl.ds(cc * (out_refs[b].shape[1] // 2) + st, sz), :]
            return pltpu.make_async_remote_copy(src_ref=rows_k, dst_ref=rows_k, send_sem=send_sems.at[j * n + k], recv_sem=recv_sems.at[j * n + k],
                                                device_id=(x, y, 1 - c), device_id_type=MESH_ID)

        order = [(k, j, 2 * cx + cy) for k in range(n) for j, (cx, cy) in enumerate(chips)]
        return [copy(j, k, chip, c) for k, j, chip in order], [copy(j, k, chip, 1 - c) for k, j, chip in order]

    return Plan(gathered, [SDS(g.shape, g.dtype) for g in gathered], {i: i for i in range(len(gathered))}, 3 * n, copies)


def plan_pair_exchange(gs):
    pieces = _pieces([(g.shape, g.dtype) for g in gs], lambda shape: shape[1] // 2)

    def copies(g_refs, got_refs, send_sems, recv_sems):
        x, y, c, _ = _place()
        swaps = [pltpu.make_async_remote_copy(src_ref=g_refs[b].at[:, pl.ds((1 - c) * (g_refs[b].shape[1] // 2) + st, sz), :],
                                              dst_ref=got_refs[b].at[:, pl.ds(st, sz), :], send_sem=send_sems.at[k], recv_sem=recv_sems.at[k],
                                              device_id=(x, y, 1 - c), device_id_type=MESH_ID)
                 for k, (b, st, sz) in enumerate(pieces)]
        return swaps, swaps

    return Plan(gs, [SDS((g.shape[0], g.shape[1] // 2, g.shape[2]), g.dtype) for g in gs], {}, len(pieces), copies)


def plan_chip_scatter(ps):
    pieces = _pieces([(p.shape, p.dtype) for p in ps], lambda shape: shape[1])
    n = len(pieces)

    def copies(p_refs, q_refs, send_sems, recv_sems):
        x, y, c, chips = _place()
        mine = 2 * x + y

        def copy(j, k, src_slot, dst_slot, to):
            b, st, sz = pieces[k]
            return pltpu.make_async_remote_copy(src_ref=p_refs[b].at[src_slot, pl.ds(st, sz), :], dst_ref=q_refs[b].at[dst_slot, pl.ds(st, sz), :],
                                                send_sem=send_sems.at[j * n + k], recv_sem=recv_sems.at[j * n + k], device_id=to,
                                                device_id_type=MESH_ID)

        order = [(k, j, 2 * cx + cy, (cx, cy, c)) for k in range(n) for j, (cx, cy) in enumerate(chips)]
        return [copy(j, k, chip, mine, to) for k, j, chip, to in order], [copy(j, k, mine, chip, to) for k, j, chip, to in order]

    return Plan(ps, [SDS(p.shape, p.dtype) for p in ps], {}, 3 * n, copies)


def plan_pair_join(bufs):
    pieces = _pieces([(b.shape, b.dtype) for b in bufs], lambda shape: shape[0] // 2)

    def copies(_, out_refs, send_sems, recv_sems):
        x, y, c, _ = _place()

        def copy(k, cc):
            b, st, sz = pieces[k]
            rows_k = out_refs[b].at[pl.ds(cc * (out_refs[b].shape[0] // 2) + st, sz), :]
            return pltpu.make_async_remote_copy(src_ref=rows_k, dst_ref=rows_k, send_sem=send_sems.at[k], recv_sem=recv_sems.at[k],
                                                device_id=(x, y, 1 - c), device_id_type=MESH_ID)

        return [copy(k, c) for k in range(len(pieces))], [copy(k, 1 - c) for k in range(len(pieces))]

    return Plan(bufs, [SDS(b.shape, b.dtype) for b in bufs], {i: i for i in range(len(bufs))}, len(pieces), copies)


def run_plan(name, plan):
    hosted_call(lambda: None, name=name, grid=(1,), in_specs=[], out_specs=[], out_shape=[], operands=[], plans=[plan])
    return plan.results


HBM = pl.BlockSpec(memory_space=pltpu.HBM)
SEMS = pl.BlockSpec(memory_space=pltpu.SEMAPHORE)
SPLIT_PARAMS = dict(has_side_effects=pltpu.SideEffectType.DATAFLOW_SIDE_EFFECTING)


def _plan_buffers(plan):
    in_place = {o: i for i, o in plan.aliases.items()}
    bufs = [pltpu.with_memory_space_constraint(a, pltpu.HBM) for a in plan.operands]
    where = []
    for o, sd in enumerate(plan.out_shape):
        if o in in_place:
            where.append(in_place[o])
        else:
            where.append(len(bufs))
            bufs.append(pltpu.with_memory_space_constraint(lax.empty(sd.shape, sd.dtype), pltpu.HBM))
    return bufs, where


def split_start(name, plans):
    layout = [_plan_buffers(p) for p in plans]
    counts = [len(b) for b, _ in layout]
    n_buf = sum(counts)

    def body(*refs):
        sems, token = refs[n_buf:n_buf + 2 * len(plans)], refs[-1]
        pos = 0
        for k, (p, (_, where)) in enumerate(zip(plans, layout)):
            mine = refs[pos:pos + counts[k]]
            pos += counts[k]
            sends, _ = p.copies(mine[:len(p.operands)], [mine[w] for w in where], sems[2 * k], sems[2 * k + 1])
            for cp in sends:
                cp.start()
        token[...] = jnp.zeros_like(token)

    bufs = [b for bs, _ in layout for b in bs]
    res = pl.pallas_call(
        body, name=name, in_specs=[HBM] * n_buf,
        out_specs=[SEMS] * (2 * len(plans)) + [HBM] * n_buf + [pl.BlockSpec(memory_space=pltpu.VMEM)],
        out_shape=[pltpu.SemaphoreType.DMA((p.n_sems,)) for p in plans for _ in range(2)] + [pltpu.HBM(b.shape, b.dtype) for b in bufs]
        + [SDS((8, 128), F32)],
        input_output_aliases={i: 2 * len(plans) + i for i in range(n_buf)}, compiler_params=pltpu.CompilerParams(**SPLIT_PARAMS))(*bufs)
    pos = 2 * len(plans)
    for k, p in enumerate(plans):
        p.in_flight = (res[2 * k], res[2 * k + 1], list(res[pos:pos + counts[k]]), layout[k][1])
        pos += counts[k]
    return res[-1]


def split_wait(name, plan, after):
    send_sems, recv_sems, bufs, where = plan.in_flight
    n_buf = len(bufs)

    def body(*refs):
        mine = refs[:n_buf]
        sends, recvs = plan.copies(mine[:len(plan.operands)], [mine[w] for w in where], refs[n_buf], refs[n_buf + 1])
        for cp in recvs:
            cp.wait_recv()
        for cp in sends:
            cp.wait_send()

    res = pl.pallas_call(body, name=name, in_specs=[HBM] * n_buf + [SEMS, SEMS, ANY], out_specs=[HBM] * n_buf,
                         out_shape=[pltpu.HBM(b.shape, b.dtype) for b in bufs], input_output_aliases={i: i for i in range(n_buf)},
                         compiler_params=pltpu.CompilerParams(**SPLIT_PARAMS))(*bufs, send_sems, recv_sems, after)
    plan.results = [res[w] for w in where]
    return plan.results


def pair_add(name, g, got, place):
    slots, rows, cols = g.shape
    half = rows // 2
    tr = _row_tile(half)
    nb = half // tr

    def kern(_, g_ref, t_ref, o_ref):
        o_ref[...] = (g_ref[...].astype(F32) + t_ref[...].astype(F32)).astype(o_ref.dtype)

    blk = pl.BlockSpec((None, tr, cols), lambda s, i, p: (s, i, 0))
    grid_spec = pltpu.PrefetchScalarGridSpec(
        num_scalar_prefetch=1, grid=(slots, nb),
        in_specs=[pl.BlockSpec((None, tr, cols), lambda s, i, p: (s, p[1] * nb + i, 0)), blk], out_specs=blk)
    return pl.pallas_call(kern, grid_spec=grid_spec, out_shape=SDS((slots, half, cols), g.dtype), name=name,
                          compiler_params=_cparams())(place, g, got)


def chip_add(name, p, q, place):
    slots, half, cols = p.shape
    tr = _row_tile(half)
    nb = half // tr

    def kern(_, p_ref, q1, q2, q3, o_ref):
        o_ref[...] = p_ref[...].astype(F32) + q1[...].astype(F32) + q2[...].astype(F32) + q3[...].astype(F32)

    def slot(k):
        return pl.BlockSpec((None, tr, cols), lambda i, pr: ((pr[0] + k) % slots, i, 0))

    grid_spec = pltpu.PrefetchScalarGridSpec(
        num_scalar_prefetch=1, grid=(nb,), in_specs=[slot(0), slot(1), slot(2), slot(3)],
        out_specs=pl.BlockSpec((tr, cols), lambda i, pr: (pr[1] * nb + i, 0)))
    return pl.pallas_call(kern, grid_spec=grid_spec, out_shape=SDS((2 * half, cols), F32), name=name,
                          compiler_params=_cparams())(place, p, q, q, q)


def pair_adds(tag, gs, gots, place):
    return [pair_add(f"{tag}_pair_add_{i}", g, got, place) for i, (g, got) in enumerate(zip(gs, gots))]


def chip_adds(tag, pairs, qs, place):
    return [chip_add(f"{tag}_chip_add_{i}", p, q, place) for i, (p, q) in enumerate(zip(pairs, qs))]


def reduce_scatter_chips(tag, gs, place):
    pairs = pair_adds(tag, gs, run_plan(tag + "_pair_exchange", plan_pair_exchange(gs)), place)
    return run_plan(tag + "_pair_join", plan_pair_join(chip_adds(tag, pairs, run_plan(tag + "_chip_scatter", plan_chip_scatter(pairs)), place)))


BIG = [("w_out", (2, 512, 1024)), ("w_mem_kv", (2, 256, 1024)), ("s5_w_in", (1, 1024, 1024)), ("s5_w_glu", (1, 1536, 768)),
       ("mla_w_in", (1, 1024, 848)), ("mla_w_uq", (1, 512, 576)), ("mla_w_ukv", (1, 256, 768))]
SHARDED_SMALL = [("mla_q_lora_norm", (1, 128)), ("mla_kv_lora_norm", (1, 64))]
SMALL = [("ln_gain", (2, 1024)), ("mem_norm", (2, 1024)), ("xq_norm", (2, 128)), ("xk_norm", (2, 128)),
         ("s5_lambda_re", (1, 96, 64)), ("s5_lambda_im", (1, 96, 64)), ("s5_log_step", (1, 96)),
         ("s5_b_re", (1, 96, 64, 16)), ("s5_b_im", (1, 96, 64, 16)), ("s5_c_re", (1, 96, 16, 64)), ("s5_c_im", (1, 96, 16, 64)),
         ("s5_d", (1, 1536)), ("mla_q_nope_norm", (1, 128)), ("mla_k_nope_norm", (1, 128)), ("mla_q_rope_norm", (1, 64)),
         ("mla_k_rope_norm", (1, 64))]
WEIGHT_ORDER = ["ln_gain", "w_out", "mem_norm", "w_mem_kv", "xq_norm", "xk_norm", "s5_w_in", "s5_lambda_re", "s5_lambda_im",
                "s5_log_step", "s5_b_re", "s5_b_im", "s5_c_re", "s5_c_im", "s5_d", "s5_w_glu", "mla_w_in", "mla_q_lora_norm",
                "mla_kv_lora_norm", "mla_w_uq", "mla_w_ukv", "mla_q_nope_norm", "mla_k_nope_norm", "mla_q_rope_norm", "mla_k_rope_norm"]
MINOR_LAST = {"mla_w_in": (0, 2, 1), "mla_w_uq": (0, 2, 1), "s5_b_re": (0, 2, 3, 1), "s5_b_im": (0, 2, 3, 1),
              "s5_c_re": (0, 2, 3, 1), "s5_c_im": (0, 2, 3, 1)}
SMALL_FULL = SMALL + [(n, (1, 4 * s[1])) for n, s in SHARDED_SMALL]
N_SMALL = sum(math.prod(s) for _, s in SMALL_FULL)
SMALL_ROWS, SMALL_LANES = 128, 1024

PAIR_OUT, PAIR_MKV, PAIR_ROWS = 0, 512, 768


def stack_shards(w, dtype):
    pairs = [jnp.concatenate([w["w_out"][l], w["w_mem_kv"][l]], axis=0).astype(dtype) for l in range(2)]
    return ([w["s5_w_in"][0].astype(dtype)], [pairs[0], w["s5_w_glu"][0].astype(dtype)],
            [pairs[1], w["mla_w_ukv"][0].astype(dtype), jnp.pad(w["mla_w_in"][0].T.astype(dtype), ((0, IN1T_ROWS - MLA_IN // 4), (0, 0))),
             w["mla_w_uq"][0].T.astype(dtype)])


def pair_views(pair):
    return {"w_out": Sharded(pair, "row", PAIR_OUT, 512), "w_mem_kv": Sharded(pair, "row", PAIR_MKV, 256)}


def grad_views():
    pair = SDS((4, PAIR_ROWS, 1024), BF16)
    return {"w_out": Sharded(pair, "row", PAIR_OUT, 512), "w_mem_kv": Sharded(pair, "row", PAIR_MKV, 256),
            "s5_w_in": Sharded(SDS((4, 1024, 1024), BF16), "col", 0, 1024), "s5_w_glu": Sharded(SDS((4, 1536, 768), BF16), "col", 0, 1536),
            "mla_w_ukv": Sharded(SDS((4, 256, 768), BF16), "col", 0, 256)}


IN1T_ROWS = 864


def in1_rows_permute(wt):
    o1, o2, o3, o4 = QL, QL + KVL, QL + KVL + ROPE, QL + KVL + ROPE + XQW
    return jnp.concatenate([wt[o4:], wt[:o1], wt[o3:o4], wt[o1:o2], wt[o2:o3], jnp.zeros((MLA_IN_P - MLA_IN, wt.shape[1]), wt.dtype)], axis=0)


def in1_rows_unpermute(d):
    return jnp.concatenate([d[2048:2560], d[3072:3328], d[3328:3392], d[2560:3072], d[:2048]], axis=0)


def uq_rows_permute(wt):
    w3 = wt.reshape(MH, NOPE + ROPE, wt.shape[1])
    return jnp.concatenate([w3[:, :NOPE].reshape(MH * NOPE, wt.shape[1]), w3[:, NOPE:].reshape(MH * ROPE, wt.shape[1])], axis=0)


def uq_rows_unpermute(d):
    dn = d[:MH * NOPE].reshape(MH, NOPE, d.shape[1])
    dr = d[MH * NOPE:].reshape(MH, ROPE, d.shape[1])
    return jnp.concatenate([dn, dr], axis=1).reshape(MH * (NOPE + ROPE), d.shape[1])


def time_permute(a):
    return a.reshape(SEG, SEG_LEN, a.shape[-1]).transpose(1, 0, 2).reshape(L, a.shape[-1])


def time_unpermute(a):
    return a.reshape(SEG_LEN, SEG, a.shape[-1]).transpose(1, 0, 2).reshape(L, a.shape[-1])


def mem_branch_fwd(tag, mem, mem_norm, w_mem_kv, xk_norm):
    mn = row_fwd(tag + "_mem_rms", fn_rms, ML, ML, [(mem, D, 0)], [mem_norm], [(D, BF16)])[0]
    kv = matmul(tag + "_mem_kv", mn, w_mem_kv, "nn", F32)
    kn = row_fwd(tag + "_mem_knorm", fn_mem_k, ML, ML, [(kv, XQW, 0)], [xk_norm], [(XQW, F32)])[0]
    return mn, kv, kn


def mem_branch_bwd(tag, mem, mem_norm, w_mem_kv, xk_norm, mn, kv, dkn, dv, g_view, g_wide):
    dk, dxk = row_bwd(tag + "_mem_knorm_bwd", fn_mem_k, ML, ML, [(kv, XQW, 0)], [xk_norm], [(dkn, XQW, 0)], [True], [True])
    dkv = jnp.concatenate([dk, dv], axis=1)
    dmn = matmul(tag + "_mem_kv_dx", dkv, w_mem_kv, "nt", F32)
    g_wide = matmul(tag + "_mem_kv_dw", mn, dkv, "tn", out=g_view, into=g_wide)
    dmem_norm = row_bwd(tag + "_mem_rms_bwd", fn_rms, ML, ML, [(mem, D, 0)], [mem_norm], [(dmn, D, 0)], [False], [True])[0]
    return g_wide, dmem_norm, dxk


def mem_attn_fwd(tag, proj, cb, kn, kv, xq_norm):
    return row_fwd(tag + "_mem_attn", fn_mem_attn, L, ROW_TILE, [(proj, XQW, cb)], [kn, kv[:, XQW:], xq_norm], [(XQW, F32)])[0]


def mem_attn_bwd(tag, proj, cb, kn, kv, xq_norm, dmo, dproj):
    place = {"cols": proj.shape[1], "cb": cb, "into": dproj, "dtype": dproj.dtype}
    return row_bwd(tag + "_mem_attn_bwd", fn_mem_attn, L, ROW_TILE, [(proj, XQW, cb)], [kn, kv[:, XQW:], xq_norm], [(dmo, XQW, 0)],
                   [place], [True, True, True])


def device_step(x, mem, positions, target, small, env, hooks=None):
    hooks = hooks or {}

    def plans_for(name):
        return hooks[("plans", name)](env) if ("plans", name) in hooks else ()

    def around(when, name, last=None):
        if (when, name) in hooks:
            hooks[(when, name)](env, last)

    g = {}
    gw = grad_views()
    ln, mem_norm, xq_norm, xk_norm = small["ln_gain"], small["mem_norm"], small["xq_norm"], small["xk_norm"]

    lre, lim = small["s5_lambda_re"][0], small["s5_lambda_im"][0]
    ls = small["s5_log_step"].reshape(SG, 1)
    one = pl.BlockSpec((SG, SP), lambda i: (0, 0))
    col = pl.BlockSpec((SG, 1), lambda i: (0, 0))
    disc_ins = [(lre, one), (lim, one), (ls, col)]
    a_re, a_im, coef_re, coef_im = stage("s5_disc", fn_s5_disc, (1,), disc_ins, [(SDS((SG, SP), F32), one)] * 4)
    b_re, b_im = small["s5_b_re"].reshape(SN, SC), small["s5_b_im"].reshape(SN, SC)
    c_re, c_im = small["s5_c_re"].reshape(PW, SP), small["s5_c_im"].reshape(PW, SP)
    bmat_rows = [(b_re, SC, 0), (b_im, SC, 0), (coef_re.reshape(SN, 1), 1, 0), (coef_im.reshape(SN, 1), 1, 0)]
    wb_re, wb_im = row_fwd("s5_bmat", fn_s5_bmat, SN, BMAT_TILE, bmat_rows, [], [(128, F32)] * 2)
    cmat_rows = [(c_re, SP, 0), (c_im, SP, 0)]
    wc_re, wc_im = row_fwd("s5_cmat", fn_s5_cmat, PW, CMAT_TILE, cmat_rows, [], [(512, F32)] * 2)
    a_re_v, a_im_v = a_re.reshape(1, SN), a_im.reshape(1, SN)
    s5_d = small["s5_d"]

    xp = time_permute(x)
    h0 = row_fwd("l0_rms", fn_rms, L, ROW_TILE, [(xp, D, 0)], [ln[0:1]], [(D, BF16)])[0]
    around("before", "l0_in", wb_re)
    w_in0 = Sharded(env["in0"], "col", 0, 1024)
    proj0 = matmul("l0_in", h0, w_in0, "nn")
    s_re, s_im, g0 = s5_forward(proj0, wb_re, wb_im, wc_re, wc_im, a_re_v, a_im_v, s5_d, plans=plans_for("s5_forward"))
    around("before", "l0_glu", g0)
    w0 = dict(pair_views(env["pair0"]), s5_w_glu=Sharded(env["glu"], "col", 0, 1536))
    z0 = matmul("l0_glu", g0, w0["s5_w_glu"], "nn", plans=plans_for("l0_glu"))
    mn0, kv0, kn0 = mem_branch_fwd("l0", mem, mem_norm[0:1], w0["w_mem_kv"], xk_norm[0:1])
    mo0 = mem_attn_fwd("l0", proj0, 3, kn0, kv0, xq_norm[0:1])
    o0 = row_fwd("l0_merge", fn_merge_glu, L, ROW_TILE, [(z0, 2 * PW, 0), (mo0, XQW, 0), (proj0, BW, 1)], [], [(BW, BF16)])[0]
    around("before", "l0_out", o0)
    x1p = matmul("l0_out", o0, w0["w_out"], "nn", F32, add=xp, plans=plans_for("l0_out"))
    around("after", "l0_out", x1p)
    x1 = time_unpermute(x1p)

    w1 = dict(pair_views(env["pair1"]), mla_w_ukv=Sharded(env["ukv"], "col", 0, 256))
    w_in1, w_uq = env["w_in1"], env["w_uq"]
    h1 = row_fwd("l1_rms", fn_rms, L, ROW_TILE, [(x1, D, 0)], [ln[1:2]], [(D, BF16)])[0]
    proj1 = matmul("l1_in", h1, w_in1, "nt")
    qln, kvln = env["q_lora_norm"].reshape(1, QL), env["kv_lora_norm"].reshape(1, KVL)
    cqn = row_fwd("l1_q_lora_rms", fn_rms, L, ROW_TILE, [(proj1, QL, 4)], [qln], [(QL, BF16)])[0]
    ckvn = row_fwd("l1_kv_lora_rms", fn_rms, L, ROW_TILE, [(proj1, KVL, 12)], [kvln], [(KVL, BF16)])[0]
    q = matmul("l1_uq", cqn, w_uq, "nt")
    kv = matmul("l1_ukv", ckvn, w1["mla_w_ukv"], "nn")
    inv_freq = ROPE_THETA ** (-jnp.arange(ROPE // 2, dtype=F32) / (ROPE // 2))
    ang = positions.astype(F32)[:, None] * inv_freq
    cos2 = jnp.tile(jnp.cos(ang), (1, 4))
    sin_signed = jnp.tile(jnp.concatenate([-jnp.sin(ang), jnp.sin(ang)], axis=1), (1, 2))
    qnn, knn = small["mla_q_nope_norm"], small["mla_k_nope_norm"]
    qrn, krn = jnp.tile(small["mla_q_rope_norm"], (1, 2)), jnp.tile(small["mla_k_rope_norm"], (1, 2))
    tp = 256
    prep_ins = [(q, rspec(tp, MH * (NOPE + ROPE))), (kv, rspec(tp, MH * 256)), (proj1, rspec(tp, 128, 26)),
                (cos2, rspec(tp, 128)), (sin_signed, rspec(tp, 128))] + [(a, cspec((1, 128))) for a in (qnn, knn, qrn, krn)]
    hq_spec = pl.BlockSpec((MH, tp, 256), lambda i: (0, i, 0))
    hv_spec = pl.BlockSpec((MH, tp, 128), lambda i: (0, i, 0))
    qf, kf, vh = stage("l1_mla_prep", fn_mla_prep, (L // tp,), prep_ins,
                       [(SDS((MH, L, 256), BF16), hq_spec), (SDS((MH, L, 256), BF16), hq_spec), (SDS((MH, L, 128), BF16), hv_spec)])
    attn, attn_lse = causal_attn(qf, kf, vh)
    mn1, kv1, kn1 = mem_branch_fwd("l1", mem, mem_norm[1:2], w1["w_mem_kv"], xk_norm[1:2])
    mo1 = mem_attn_fwd("l1", proj1, 5, kn1, kv1, xq_norm[1:2])
    o1 = row_fwd("l1_merge", fn_merge, L, ROW_TILE, [(attn, PW, 0), (mo1, XQW, 0), (proj1, BW, 0)], [], [(BW, BF16)])[0]
    x2 = matmul("l1_out", o1, w1["w_out"], "nn", F32, add=x1)
    dx2, loss = loss_and_grad(x2, target)
    around("after", "loss", loss)

    do1 = matmul("l1_out_dx", dx2, w1["w_out"], "nt")
    g_pair1 = matmul("l1_out_dw", o1, dx2, "tn", out=gw["w_out"])
    dattn, dmo1, dproj1 = row_bwd("l1_merge_bwd", fn_merge, L, ROW_TILE, [(attn, PW, 0), (mo1, XQW, 0), (proj1, BW, 0)], [],
                                  [(do1, BW, 0)], [True, True, {"cols": MLA_IN_P, "cb": 0, "dtype": BF16}], [])
    dproj1, dkn1, dv1, dxqn1 = mem_attn_bwd("l1", proj1, 5, kn1, kv1, xq_norm[1:2], dmo1, dproj1)
    env["g_pair1"], dmem_norm1, dxk1 = mem_branch_bwd("l1", mem, mem_norm[1:2], w1["w_mem_kv"], xk_norm[1:2], mn1, kv1, dkn1, dv1,
                                                      gw["w_mem_kv"], g_pair1)
    dqf, dkf, dvh = causal_attn_bwd(qf, kf, vh, attn, attn_lse, dattn)
    prep_diffs = [("row", SDS((L, MH * (NOPE + ROPE)), BF16), rspec(tp, MH * (NOPE + ROPE))), ("row", SDS((L, MH * 256), BF16), rspec(tp, MH * 256)),
                  ("row", SDS((L, MLA_IN_P), BF16), rspec(tp, 128, 26), {"into": dproj1}), None, None] + [("acc", (0,))] * 4
    dq, dkv, dproj1, dqnn, dknn, dqrn, dkrn = stage_bwd("l1_mla_prep_bwd", fn_mla_prep, (L // tp,), prep_ins,
                                                        [(dqf, hq_spec), (dkf, hq_spec), (dvh, hv_spec)], prep_diffs)
    dcqn = matmul("l1_uq_dx", dq, w_uq, "nn")
    env["dw_uq"] = matmul("l1_uq_dw", dq, cqn, "tn")
    dckvn = matmul("l1_ukv_dx", dkv, w1["mla_w_ukv"], "nt")
    env["g_ukv"] = matmul("l1_ukv_dw", ckvn, dkv, "tn", out=gw["mla_w_ukv"])
    dproj1, dqln = row_bwd("l1_q_lora_rms_bwd", fn_rms, L, ROW_TILE, [(proj1, QL, 4)], [qln], [(dcqn, QL, 0)],
                           [{"cols": MLA_IN_P, "cb": 4, "into": dproj1, "dtype": BF16}], [True])
    dproj1, dkvln = row_bwd("l1_kv_lora_rms_bwd", fn_rms, L, ROW_TILE, [(proj1, KVL, 12)], [kvln], [(dckvn, KVL, 0)],
                            [{"cols": MLA_IN_P, "cb": 12, "into": dproj1, "dtype": BF16}], [True])
    env["dw_in1"] = matmul("l1_in_dw", dproj1, h1, "tn")
    dh1 = matmul("l1_in_dx", dproj1, w_in1, "nn", plans=plans_for("l1_in_dx"))
    around("after", "l1_in_dx", dh1)
    dx1, dln1 = row_bwd("l1_rms_bwd", fn_rms, L, ROW_TILE, [(x1, D, 0)], [ln[1:2]], [(dh1, D, 0)], [{"add": (dx2, D, 0)}], [True])
    dx1p = time_permute(dx1)

    do0 = matmul("l0_out_dx", dx1p, w0["w_out"], "nt", plans=plans_for("l0_out_dx"))
    g_pair0 = matmul("l0_out_dw", o0, dx1p, "tn", out=gw["w_out"])
    dz0, dmo0, dproj0 = row_bwd("l0_merge_bwd", fn_merge_glu, L, ROW_TILE, [(z0, 2 * PW, 0), (mo0, XQW, 0), (proj0, BW, 1)], [],
                                [(do0, BW, 0)], [{"dtype": BF16}, True, {"cols": 2 * BW, "cb": 1, "dtype": BF16}], [])
    dproj0, dkn0, dv0, dxqn0 = mem_attn_bwd("l0", proj0, 3, kn0, kv0, xq_norm[0:1], dmo0, dproj0)
    env["g_pair0"], dmem_norm0, dxk0 = mem_branch_bwd("l0", mem, mem_norm[0:1], w0["w_mem_kv"], xk_norm[0:1], mn0, kv0, dkn0, dv0,
                                                      gw["w_mem_kv"], g_pair0)
    env["g_glu"] = matmul("l0_glu_dw", g0, dz0, "tn", out=gw["s5_w_glu"], plans=plans_for("l0_glu_dw"))
    dg0 = matmul("l0_glu_dx", dz0, w0["s5_w_glu"], "nt", plans=plans_for("l0_glu_dx"))
    around("before", "s5_backward", dg0)
    dproj0, dd, dwc_re, dwc_im, dwb_re, dwb_im, da_re, da_im = s5_backward(dg0, proj0, s_re, s_im, wb_re, wb_im, wc_re, wc_im,
                                                                           a_re_v, a_im_v, s5_d, dproj0, plans=plans_for("s5_backward"))
    around("after", "s5_backward", dd)
    env["g_in0"] = matmul("l0_in_dw", h0, dproj0, "tn", out=gw["s5_w_in"], plans=plans_for("l0_in_dw"))
    dh0 = matmul("l0_in_dx", dproj0, w_in0, "nt", plans=plans_for("l0_in_dx"))
    around("after", "l0_in_dx", dh0)
    dxp, dln0 = row_bwd("l0_rms_bwd", fn_rms, L, ROW_TILE, [(xp, D, 0)], [ln[0:1]], [(dh0, D, 0)], [{"add": (dx1p, D, 0)}], [True])
    grad_x = time_unpermute(dxp)

    db_re, db_im, dcoef_re, dcoef_im = row_bwd("s5_bmat_bwd", fn_s5_bmat, SN, BMAT_TILE, bmat_rows, [], [(dwb_re, 128, 0), (dwb_im, 128, 0)],
                                               [True] * 4, [], plans=plans_for("s5_bmat_bwd"))
    dc_re, dc_im = row_bwd("s5_cmat_bwd", fn_s5_cmat, PW, CMAT_TILE, cmat_rows, [], [(dwc_re, 512, 0), (dwc_im, 512, 0)], [True] * 2, [],
                           plans=plans_for("s5_cmat_bwd"))
    disc_cts = [(da_re.reshape(SG, SP), one), (da_im.reshape(SG, SP), one), (dcoef_re.reshape(SG, SP), one), (dcoef_im.reshape(SG, SP), one)]
    dlre, dlim, dls = stage_bwd("s5_disc_bwd", fn_s5_disc, (1,), disc_ins, disc_cts, [("acc", (0,))] * 3)

    g["ln_gain"] = jnp.concatenate([dln0, dln1], axis=0)
    g["mem_norm"] = jnp.concatenate([dmem_norm0, dmem_norm1], axis=0)
    g["xq_norm"] = jnp.concatenate([dxqn0, dxqn1], axis=0)
    g["xk_norm"] = jnp.concatenate([dxk0, dxk1], axis=0)
    g["s5_lambda_re"], g["s5_lambda_im"], g["s5_log_step"] = dlre, dlim, dls
    g["s5_b_re"], g["s5_b_im"], g["s5_c_re"], g["s5_c_im"] = db_re, db_im, dc_re, dc_im
    g["s5_d"] = dd
    g["mla_q_lora_norm"], g["mla_kv_lora_norm"] = dqln, dkvln
    g["mla_q_nope_norm"], g["mla_k_nope_norm"] = dqnn, dknn
    g["mla_q_rope_norm"] = dqrn[:, :ROPE] + dqrn[:, ROPE:]
    g["mla_k_rope_norm"] = dkrn[:, :ROPE] + dkrn[:, ROPE:]
    return loss, grad_x, g


def kernel(x, mem, positions, ln_gain, w_out, mem_norm, w_mem_kv, xq_norm, xk_norm, s5_w_in, s5_lambda_re, s5_lambda_im, s5_log_step, s5_b_re, s5_b_im, s5_c_re, s5_c_im, s5_d, s5_w_glu, mla_w_in, mla_q_lora_norm, mla_kv_lora_norm, mla_w_uq, mla_w_ukv, mla_q_nope_norm, mla_k_nope_norm, mla_q_rope_norm, mla_k_rope_norm, loss_target, m_ln_gain, m_w_out, m_mem_norm, m_w_mem_kv, m_xq_norm, m_xk_norm, m_s5_w_in, m_s5_lambda_re, m_s5_lambda_im, m_s5_log_step, m_s5_b_re, m_s5_b_im, m_s5_c_re, m_s5_c_im, m_s5_d, m_s5_w_glu, m_mla_w_in, m_mla_q_lora_norm, m_mla_kv_lora_norm, m_mla_w_uq, m_mla_w_ukv, m_mla_q_nope_norm, m_mla_k_nope_norm, m_mla_q_rope_norm, m_mla_k_rope_norm, v_ln_gain, v_w_out, v_mem_norm, v_w_mem_kv, v_xq_norm, v_xk_norm, v_s5_w_in, v_s5_lambda_re, v_s5_lambda_im, v_s5_log_step, v_s5_b_re, v_s5_b_im, v_s5_c_re, v_s5_c_im, v_s5_d, v_s5_w_glu, v_mla_w_in, v_mla_q_lora_norm, v_mla_kv_lora_norm, v_mla_w_uq, v_mla_w_ukv, v_mla_q_nope_norm, v_mla_k_nope_norm, v_mla_q_rope_norm, v_mla_k_rope_norm):
    args = dict(locals())
    wts = {n: args[n] for n in WEIGHT_ORDER}
    mom = {n: args["m_" + n] for n in WEIGHT_ORDER}
    var = {n: args["v_" + n] for n in WEIGHT_ORDER}

    chip = 2 * lax.axis_index("x") + lax.axis_index("y")
    place = jnp.stack([chip, lax.axis_index("c")]).astype(jnp.int32)

    def own_slot(gathered, shards):
        return [lax.dynamic_update_slice(g, s[None], (chip, 0, 0)) for g, s in zip(gathered, shards)]

    groups = list(stack_shards(wts, BF16))
    groups[2].append(jnp.concatenate([mla_q_lora_norm, jnp.pad(mla_kv_lora_norm, ((0, 0), (0, 64))), jnp.zeros((14, 128), F32)], axis=0))
    over_ici = [plan_gather_ici(shards) for shards in groups]
    _SCHEDULE_BEHIND.clear()
    schedule_behind(split_start("gather_start", over_ici))
    env, hooks, passed_on = {}, {}, {}

    def arrived(k, after, pass_now):
        passing = plan_gather_pass(split_wait(f"gather_wait_{k}", over_ici[k], after))
        passed_on[k] = passing
        return own_slot(run_plan(f"gather_pass_{k}", passing), groups[k]) if pass_now else None

    def need_in0(env, last):
        env["in0"], = arrived(0, last, True)

    def need_layer0(env, last):
        env["pair0"], env["glu"] = arrived(1, last, True)

    def need_layer1(env, last):
        arrived(2, last, False)

    def layer1_weights(env, last):
        pair1, ukv, in1, uq, norms = own_slot(passed_on[2].results, groups[2])
        env.update(pair1=pair1, ukv=ukv, w_in1=in1_rows_permute(in1[:, :MLA_IN // 4].reshape(MLA_IN, D)), w_uq=uq_rows_permute(uq.reshape(-1, QL)),
                   q_lora_norm=norms[:, 0, :], kv_lora_norm=norms[:, 1, :64])

    hooks["before", "l0_in"], hooks["before", "l0_glu"], hooks["before", "l0_out"] = need_in0, need_layer0, need_layer1
    hooks["plans", "l0_out"], hooks["after", "l0_out"] = (lambda env: [passed_on[2]]), layer1_weights

    rs = {}

    def swap(k, gs):
        rs[k, "g"], rs[k, "swap"] = gs, plan_pair_exchange(gs)
        return rs[k, "swap"]

    def start_scatter(k):
        rs[k, "pairs"] = pair_adds(f"rs{k}", rs[k, "g"], rs[k, "swap"].results, place)
        rs[k, "scatter"] = plan_chip_scatter(rs[k, "pairs"])
        schedule_behind(split_start(f"rs{k}_scatter_start", [rs[k, "scatter"]]))

    def join(k, after):
        rs[k, "join"] = plan_pair_join(chip_adds(f"rs{k}", rs[k, "pairs"], split_wait(f"rs{k}_scatter_wait", rs[k, "scatter"], after), place))
        return rs[k, "join"]

    def layer1_gradients(env):
        g_in1 = jnp.pad(in1_rows_unpermute(env["dw_in1"]).reshape(4, MLA_IN // 4, D), ((0, 0), (0, IN1T_ROWS - MLA_IN // 4), (0, 0)))
        return [swap(1, [env["g_pair1"], env["g_ukv"], g_in1, uq_rows_unpermute(env["dw_uq"]).reshape(4, -1, QL)])]

    hooks["plans", "l1_in_dx"] = layer1_gradients
    hooks["after", "l1_in_dx"] = lambda env, last: start_scatter(1)
    hooks["plans", "l0_glu_dx"] = lambda env: [swap(0, [env["g_pair0"], env["g_glu"]])]

    def before_s5_backward(env, last):
        start_scatter(0)
        env["join1"] = join(1, last)

    hooks["before", "s5_backward"] = before_s5_backward
    hooks["plans", "s5_backward"] = lambda env: [env["join1"]]
    hooks["after", "s5_backward"] = lambda env, last: env.update(join0=join(0, last))
    hooks["plans", "l0_in_dx"] = lambda env: [swap(2, [env["g_in0"]]), env["join0"]]
    hooks["after", "l0_in_dx"] = lambda env, last: start_scatter(2)

    def total_loss(env, local):
        env["loss"] = lax.psum(local[0, 0], MESH_AXES)
        schedule_behind(env["loss"].reshape(1, 1))

    hooks["after", "loss"] = total_loss
    small = {n: wts[n] for n, _ in SMALL}
    loss, grad_x, g = device_step(x[0], mem[0], positions[0], loss_target[0], small, env, hooks)
    loss = env["loss"]
    r_in0, = run_plan("rs2_pair_join", join(2, g["s5_log_step"]))
    (r_pair1, r_ukv, r_in1, r_uq), (r_pair0, r_glu) = (rs[k, "join"].results for k in (1, 0))

    small_flat = jnp.concatenate([g[n].reshape(-1) for n, _ in SMALL_FULL])
    g_small = jnp.pad(small_flat, (0, 4 * SMALL_ROWS * SMALL_LANES - N_SMALL)).astype(BF16).reshape(4, SMALL_ROWS, SMALL_LANES)
    r_small = reduce_scatter_chips("rs3", [g_small], place)[0]
    small_all = own_slot(all_gather_chips("gather_small_grads", [r_small]), [r_small])[0].reshape(-1)[:N_SMALL]

    grads = {"w_out": jnp.stack([r_pair0[:PAIR_MKV], r_pair1[:PAIR_MKV]]), "w_mem_kv": jnp.stack([r_pair0[PAIR_MKV:], r_pair1[PAIR_MKV:]]),
             "s5_w_in": r_in0[None], "s5_w_glu": r_glu[None], "mla_w_ukv": r_ukv[None], "mla_w_in": r_in1[:MLA_IN // 4].T[None], "mla_w_uq": r_uq.T[None]}
    off = 0
    for n, s in SMALL_FULL:
        grads[n] = small_all[off:off + math.prod(s)].reshape(s)
        off += math.prod(s)
    for n, s in SHARDED_SMALL:
        grads[n] = lax.dynamic_slice(grads[n], (0, chip * s[1]), s)

    delta, new_m, new_v = {}, {}, {}
    for n, s in BIG + [(n, s) for n, s in SMALL if len(s) == 4]:
        perm = MINOR_LAST.get(n, tuple(range(len(s))))
        turned = tuple(s[p] for p in perm)
        view = lambda a: jnp.transpose(a, perm).reshape(-1, turned[-1])
        res = adamw("adamw_" + n, view(wts[n]), view(grads[n]), view(mom[n]), view(var[n]))
        delta[n], new_m[n], new_v[n] = (jnp.transpose(r.reshape(turned), tuple(perm.index(i) for i in range(len(s)))) for r in res)
    small_names = [n for n, s in SMALL if len(s) < 4] + [n for n, _ in SHARDED_SMALL]
    n_own = sum(wts[n].size for n in small_names)
    rows_own = -(-n_own // (8 * 128)) * 8

    def pack_small(d):
        flat = jnp.concatenate([d[n].reshape(-1) for n in small_names])
        return jnp.pad(flat, (0, rows_own * 128 - n_own), constant_values=1.0).reshape(rows_own, 128)

    res = adamw("adamw_small", pack_small(wts), pack_small(grads), pack_small(mom), pack_small(var))
    off = 0
    for n in small_names:
        size = wts[n].size
        delta[n], new_m[n], new_v[n] = (r.reshape(-1)[off:off + size].reshape(wts[n].shape) for r in res)
        off += size

    return (loss, grad_x[None], *[grads[n] for n in WEIGHT_ORDER], *[delta[n] for n in WEIGHT_ORDER],
            *[new_m[n] for n in WEIGHT_ORDER], *[new_v[n] for n in WEIGHT_ORDER])
```

```python
import functools
import math

import jax
import jax.numpy as jnp
from jax import lax
from jax.experimental import pallas as pl
from jax.experimental.pallas import tpu as pltpu

F32, BF16 = jnp.float32, jnp.bfloat16
SDS = jax.ShapeDtypeStruct

D = 1024
L = 2048
ML = 256
BW = 2 * D
XQW = BW // 4
PW = BW - XQW
XH, XHD = 4, 128
SG, SC, SP = 96, 16, 64
SN = SG * SP
NOPE, ROPE, VD = 128, 64, 128
MH = 12
QL, KVL = 512, 256
EPS = 1e-6
ROPE_THETA = 10000.0
MLA_IN = QL + KVL + ROPE + XQW + BW
MLA_IN_P = 3456
ADAM_LR, ADAM_B1, ADAM_B2, ADAM_EPS, ADAM_WD, ADAM_STEP = 0.001, 0.9, 0.999, 1e-08, 0.01, 10

VMEM_LIMIT = 48 * 2**20
ROW_TILE = 512
BMAT_TILE, CMAT_TILE = 2048, 512
SEG = 8
SEG_LEN = L // SEG
MESH_AXES = ("x", "y", "c")


def _cparams():
    return pltpu.CompilerParams(vmem_limit_bytes=VMEM_LIMIT)


def _dg(a, b, ca, cb):
    return lax.dot_general(a.astype(BF16), b.astype(BF16), (((ca,), (cb,)), ((), ())), preferred_element_type=F32)


@jax.custom_vjp
def mm_nn(a, b):
    return _dg(a, b, 1, 0)


mm_nn.defvjp(lambda a, b: (_dg(a, b, 1, 0), (a, b)), lambda res, g: (_dg(g, res[1], 1, 1), _dg(res[0], g, 0, 0)))


@jax.custom_vjp
def mm_nt(a, b):
    return _dg(a, b, 1, 1)


mm_nt.defvjp(lambda a, b: (_dg(a, b, 1, 1), (a, b)), lambda res, g: (_dg(g, res[1], 1, 0), _dg(g, res[0], 0, 0)))


@functools.partial(jax.custom_vjp, nondiff_argnums=(1,))
def lane_roll(x, shift):
    return pltpu.roll(x, shift, 1)


lane_roll.defvjp(lambda x, shift: (pltpu.roll(x, shift, 1), None),
                 lambda shift, _, g: (pltpu.roll(g, (128 - shift) % 128, 1),))


def rms(x, g):
    return x * lax.rsqrt(jnp.mean(x * x, axis=-1, keepdims=True) + EPS) * g


@jax.custom_vjp
def softmax_rows(s):
    e = jnp.exp(s - jnp.max(s, axis=-1, keepdims=True))
    return e / jnp.sum(e, axis=-1, keepdims=True)


def _softmax_rows_fwd(s):
    p = softmax_rows(s)
    return p, p


def _softmax_rows_bwd(p, g):
    return (p * (g - jnp.sum(g * p, axis=-1, keepdims=True)),)


softmax_rows.defvjp(_softmax_rows_fwd, _softmax_rows_bwd)


def silu(x):
    return x * jax.nn.sigmoid(x)


ANY = pl.BlockSpec(memory_space=pl.ANY)
MESH_ID = pl.DeviceIdType.MESH


def _dma_sems(n):
    return [pltpu.SemaphoreType.DMA((n,)), pltpu.SemaphoreType.DMA((n,))]


class Plan:
    def __init__(self, operands, out_shape, aliases, n_sems, copies):
        self.operands, self.out_shape, self.aliases, self.n_sems, self.copies = list(operands), list(out_shape), aliases, n_sems, copies
        self.results = None


_SCHEDULE_BEHIND = []


def schedule_behind(token):
    _SCHEDULE_BEHIND.append(token)


def hosted_call(kern, *, name, grid, in_specs, out_specs, out_shape, operands, scratch_shapes=(), aliases=None, cparams=None, plans=(), deps=()):
    n_in, n_out, n_scr = len(in_specs), len(out_specs), len(scratch_shapes)
    p_in, p_out = [len(p.operands) for p in plans], [len(p.out_shape) for p in plans]
    deps = tuple(deps) + tuple(_SCHEDULE_BEHIND)
    _SCHEDULE_BEHIND.clear()
    all_aliases = dict(aliases or {})
    in_off, out_off = n_in, n_out
    for p, ni, no in zip(plans, p_in, p_out):
        all_aliases.update({in_off + i: out_off + o for i, o in p.aliases.items()})
        in_off, out_off = in_off + ni, out_off + no

    def body(*refs):
        pos, pins, pouts = n_in, [], []
        for ni in p_in:
            pins.append(refs[pos:pos + ni])
            pos += ni
        pos += len(deps)
        main_out = refs[pos:pos + n_out]
        pos += n_out
        for no in p_out:
            pouts.append(refs[pos:pos + no])
            pos += no
        main_scr = refs[pos:pos + n_scr]
        pos += n_scr
        if plans:
            ids = [pl.program_id(ax) for ax in range(len(grid))]
            first = functools.reduce(jnp.logical_and, [i == 0 for i in ids])
            last = functools.reduce(jnp.logical_and, [i == g - 1 for i, g in zip(ids, grid)])
            copies = [p.copies(pins[k], pouts[k], refs[pos + 2 * k], refs[pos + 2 * k + 1]) for k, p in enumerate(plans)]

            @pl.when(first)
            def _():
                for sends, _ in copies:
                    for cp in sends:
                        cp.start()

        kern(*refs[:n_in], *main_out, *main_scr)
        if plans:
            @pl.when(last)
            def _():
                for sends, recvs in copies:
                    for cp in recvs:
                        cp.wait_recv()
                    for cp in sends:
                        cp.wait_send()

    res = pl.pallas_call(body, grid=grid, in_specs=list(in_specs) + [ANY] * (sum(p_in) + len(deps)),
                         out_specs=list(out_specs) + [ANY] * sum(p_out), out_shape=list(out_shape) + [s for p in plans for s in p.out_shape],
                         scratch_shapes=list(scratch_shapes) + [s for p in plans for s in _dma_sems(p.n_sems)],
                         input_output_aliases=all_aliases, name=name, compiler_params=cparams or _cparams())(
        *operands, *[a for p in plans for a in p.operands], *deps)
    pos = n_out
    for p, no in zip(plans, p_out):
        p.results = list(res[pos:pos + no])
        pos += no
    return list(res[:n_out])


def _wide(v):
    return v.astype(F32) if v.dtype == BF16 else v


def stage(name, fn, grid, ins, outs):
    n_in = len(ins)

    def kern(*refs):
        res = fn(*[_wide(r[...]) for r in refs[:n_in]])
        for r, v in zip(refs[n_in:], res):
            r[...] = v.astype(r.dtype)

    return hosted_call(kern, name=name, grid=grid, in_specs=[s for _, s in ins], out_specs=[s for _, s in outs],
                       out_shape=[sd for sd, _ in outs], operands=[a for a, _ in ins])


def stage_bwd(name, fn, grid, ins, cts, diffs, plans=()):
    n_in, n_ct = len(ins), len(cts)
    didx = [i for i, d in enumerate(diffs) if d is not None]
    opts = {i: (diffs[i][3] if len(diffs[i]) > 3 else {}) for i in didx if diffs[i][0] == "row"}
    adds = [(i, opts[i]["add"]) for i in opts if "add" in opts[i]]
    intos = [(i, opts[i]["into"]) for i in opts if "into" in opts[i]]
    n_add, n_into = len(adds), len(intos)
    add_pos = {i: n_in + n_ct + k for k, (i, _) in enumerate(adds)}
    n_extra = n_in + n_ct + n_add + n_into

    def kern(*refs):
        vals = [_wide(r[...]) for r in refs[:n_in]]

        def f(*dv):
            full = list(vals)
            for i, v in zip(didx, dv):
                full[i] = v
            return fn(*full)

        _, vjp = jax.vjp(f, *[vals[i].astype(F32) for i in didx])
        gs = vjp(tuple(c[...].astype(F32) for c in refs[n_in:n_in + n_ct]))
        for o_ref, i, g in zip(refs[n_extra:], didx, gs):
            if diffs[i][0] == "row":
                if i in add_pos:
                    g = g + refs[add_pos[i]][...].astype(F32)
                o_ref[...] = g.astype(o_ref.dtype)
            else:
                first = functools.reduce(jnp.logical_and, [pl.program_id(ax) == 0 for ax in diffs[i][1]])

                @pl.when(first)
                def _():
                    o_ref[...] = g

                @pl.when(jnp.logical_not(first))
                def _():
                    o_ref[...] += g

    out_shape, out_specs = [], []
    for i in didx:
        if diffs[i][0] == "row":
            out_shape.append(diffs[i][1])
            out_specs.append(diffs[i][2])
        else:
            out_shape.append(SDS(ins[i][0].shape, F32))
            out_specs.append(ins[i][1])
    aliases = {n_in + n_ct + n_add + k: didx.index(i) for k, (i, _) in enumerate(intos)}
    in_specs = [s for _, s in ins] + [s for _, s in cts] + [s for _, (_, s) in adds] + [ANY] * n_into
    operands = [a for a, _ in ins] + [a for a, _ in cts] + [a for _, (a, _) in adds] + [a for _, a in intos]
    return hosted_call(kern, name=name, grid=grid, in_specs=in_specs, out_specs=out_specs, out_shape=out_shape, operands=operands,
                       aliases=aliases, plans=plans)


def rspec(tl, w, cb=0):
    return pl.BlockSpec((tl, w), lambda i: (i, cb))


def cspec(shape):
    return pl.BlockSpec(shape, lambda i: (0,) * len(shape))


def row_fwd(name, fn, rows, tl, row_ins, consts, outs):
    ins = [(a, rspec(tl, w, cb)) for a, w, cb in row_ins] + [(a, cspec(a.shape)) for a in consts]
    return stage(name, fn, (rows // tl,), ins, [(SDS((rows, w), dt), rspec(tl, w)) for w, dt in outs])


def row_bwd(name, fn, rows, tl, row_ins, consts, cts, row_diff, const_diff, plans=()):
    ins = [(a, rspec(tl, w, cb)) for a, w, cb in row_ins] + [(a, cspec(a.shape)) for a in consts]
    diffs = []
    for (a, w, cb), d in zip(row_ins, row_diff):
        if not d:
            diffs.append(None)
            continue
        d = d if isinstance(d, dict) else {}
        opts = {}
        if "add" in d:
            opts["add"] = (d["add"][0], rspec(tl, d["add"][1], d["add"][2]))
        if d.get("into") is not None:
            opts["into"] = d["into"]
        diffs.append(("row", SDS((rows, d.get("cols", w)), d.get("dtype", F32)), rspec(tl, w, d.get("cb", 0)), opts))
    diffs += [("acc", (0,)) if d else None for d in const_diff]
    return stage_bwd(name, fn, (rows // tl,), ins, [(a, rspec(tl, w, cb)) for a, w, cb in cts], diffs, plans=plans)


MATMUL_VMEM = 36 * 2**20
ADAMW_VMEM = 28 * 2**20


class Sharded:
    def __init__(self, arr, kind, roff, rows):
        self.arr, self.kind, self.roff, self.rows, self.n = arr, kind, roff, rows, arr.shape[2]
        self.shape = (rows, 4 * self.n) if kind == "col" else (4 * rows, self.n)

    def fits(self, t0, t1):
        return self.roff % t0 == 0 and self.rows % t0 == 0 and self.n % t1 == 0

    def spec(self, t0, t1, bidx):
        assert self.fits(t0, t1), (self.kind, self.roff, self.rows, self.n, t0, t1)
        r0 = self.roff // t0
        if self.kind == "col":
            per = self.n // t1
            return pl.BlockSpec((None, t0, t1), lambda *g: (bidx(*g)[1] // per, r0 + bidx(*g)[0], bidx(*g)[1] % per))
        per = self.rows // t0
        return pl.BlockSpec((None, t0, t1), lambda *g: (bidx(*g)[0] // per, r0 + bidx(*g)[0] % per, bidx(*g)[1]))


def _matmul_over_shards(name, a, b, mode, out_dtype, add, plans):
    m, k_dim = a.shape
    per = k_dim // 4
    n = b.n if mode == "nn" else b.rows
    assert k_dim % 4 == 0 and (b.rows == per and b.roff % per == 0 if mode == "nn" else b.n == per), (name, a.shape, mode)
    a_bytes, o_bytes = jnp.dtype(a.dtype).itemsize, jnp.dtype(out_dtype).itemsize

    def vmem(tm, tn):
        return 2 * (tm * k_dim * a_bytes + k_dim * tn * 2 + tm * tn * (o_bytes + (4 if add is not None else 0))) + 4 * tm * tn

    tiles = [(tm, tn) for tm in (2048, 1024, 512, 256) for tn in (1024, 512, 256, 128)
             if m % tm == 0 and n % tn == 0 and (mode == "nn" or b.roff % tn == 0) and vmem(tm, tn) <= MATMUL_VMEM]
    tm, tn = max(tiles, key=lambda t: (t[0] * t[1], t[0]))
    if mode == "nn":
        b_specs = [pl.BlockSpec((None, per, tn), lambda i, j, s=s: (s, b.roff // per, j)) for s in range(4)]
    else:
        b_specs = [pl.BlockSpec((None, tn, per), lambda i, j, s=s: (s, b.roff // tn + j, 0)) for s in range(4)]
    o_spec = pl.BlockSpec((tm, tn), lambda i, j: (i, j))
    cb = 0 if mode == "nn" else 1

    def kern(*refs):
        r = _dg(refs[0][:, 0:per], refs[1][...], 1, cb)
        for s in range(1, 4):
            r = r + _dg(refs[0][:, s * per:(s + 1) * per], refs[1 + s][...], 1, cb)
        if add is not None:
            r = r + refs[5][...]
        refs[-1][...] = r.astype(refs[-1].dtype)

    ins, specs = [a] + [b.arr] * 4, [pl.BlockSpec((tm, k_dim), lambda i, j: (i, 0))] + b_specs
    if add is not None:
        ins.append(add)
        specs.append(o_spec)
    return hosted_call(kern, name=name, grid=(m // tm, n // tn), in_specs=specs, out_specs=[o_spec], out_shape=[SDS((m, n), out_dtype)],
                       operands=ins, plans=plans)[0]


def matmul(name, a, b, mode, out_dtype=BF16, add=None, out=None, into=None, plans=()):
    if mode == "tn":
        k_dim, m = a.shape
    else:
        m, k_dim = a.shape
    n = b.shape[0] if mode == "nt" else b.shape[1]
    if isinstance(b, Sharded) and out is None and (mode, b.kind) in (("nn", "row"), ("nt", "col")):
        return _matmul_over_shards(name, a, b, mode, out_dtype, add, plans)
    b_fit = b.fits if isinstance(b, Sharded) else (lambda t0, t1: True)
    o_fit = out.fits if out is not None else (lambda t0, t1: True)
    a_bytes, b_bytes = jnp.dtype(a.dtype).itemsize, jnp.dtype(b.arr.dtype if isinstance(b, Sharded) else b.dtype).itemsize
    o_bytes = jnp.dtype(out_dtype if out is None else out.arr.dtype).itemsize

    def vmem(tm, tn):
        return 2 * (tm * k_dim * a_bytes + k_dim * tn * b_bytes + tm * tn * (o_bytes + (4 if add is not None else 0))) + 4 * tm * tn

    tiles = [(tm, tn) for tm in (2048, 1024, 512, 256, 128) for tn in (1024, 768, 512, 384, 256, 128)
             if m % tm == 0 and n % tn == 0 and (b_fit(tn, k_dim) if mode == "nt" else b_fit(k_dim, tn)) and o_fit(tm, tn)
             and vmem(tm, tn) <= MATMUL_VMEM]
    tm, tn = max(tiles, key=lambda t: t[0] * t[1])
    a_spec = pl.BlockSpec((k_dim, tm), lambda i, j: (0, i)) if mode == "tn" else pl.BlockSpec((tm, k_dim), lambda i, j: (i, 0))
    if isinstance(b, Sharded):
        b_spec = b.spec(tn, k_dim, lambda i, j: (j, 0)) if mode == "nt" else b.spec(k_dim, tn, lambda i, j: (0, j))
        b = b.arr
    else:
        b_spec = pl.BlockSpec((tn, k_dim), lambda i, j: (j, 0)) if mode == "nt" else pl.BlockSpec((k_dim, tn), lambda i, j: (0, j))
    o_spec = pl.BlockSpec((tm, tn), lambda i, j: (i, j))
    out_spec, out_shape = (o_spec, SDS((m, n), out_dtype)) if out is None else (out.spec(tm, tn, lambda i, j: (i, j)), out.arr)
    ca, cb = {"nn": (1, 0), "nt": (1, 1), "tn": (0, 0)}[mode]
    n_in = 2 + (add is not None)

    def kern(*refs):
        r = _dg(refs[0][...], refs[1][...], ca, cb)
        if add is not None:
            r = r + refs[2][...]
        refs[-1][...] = r.astype(refs[-1].dtype)

    ins, specs = [a, b], [a_spec, b_spec]
    if add is not None:
        ins.append(add)
        specs.append(o_spec)
    if into is not None:
        ins.append(into)
        specs.append(ANY)
    return hosted_call(kern, name=name, grid=(m // tm, n // tn), in_specs=specs, out_specs=[out_spec], out_shape=[out_shape],
                       operands=ins, aliases={} if into is None else {n_in: 0}, plans=plans)[0]


SCAN_UNROLL = 8


def _cmul(ar, ai, br, bi):
    return ar * br - ai * bi, ar * bi + ai * br


def _sub_shift(x, down):
    row = lax.broadcasted_iota(jnp.int32, x.shape, 0)
    if down:
        return jnp.where(row == 0, 0.0, pltpu.roll(x, 1, 0))
    return jnp.where(row == SEG - 1, 0.0, pltpu.roll(x, SEG - 1, 0))


def _pow_seg_len(ar, ai):
    for _ in range(int(math.log2(SEG_LEN))):
        ar, ai = _cmul(ar, ai, ar, ai)
    return ar, ai


def _scan_in_place(sr, si, a_re, a_im):
    lanes = sr.shape[1]
    ar = jnp.broadcast_to(a_re, (SEG, lanes))
    ai = jnp.broadcast_to(a_im, (SEG, lanes))
    zero = jnp.zeros((SEG, lanes), F32)

    def local(i, carry):
        rows = pl.ds(pl.multiple_of(i * SEG, SEG), SEG)
        mr, mi = _cmul(ar, ai, carry[0], carry[1])
        nr, ni = mr + sr[rows, :], mi + si[rows, :]
        sr[rows, :] = nr
        si[rows, :] = ni
        return nr, ni

    fr, fi = lax.fori_loop(0, SEG_LEN, local, (zero, zero), unroll=SCAN_UNROLL)
    pr, pi = _pow_seg_len(ar, ai)
    ir, ii = zero, zero
    for _ in range(SEG - 1):
        mr, mi = _cmul(pr, pi, ir, ii)
        ir, ii = _sub_shift(mr + fr, True), _sub_shift(mi + fi, True)

    def carry_in(i, pw):
        rows = pl.ds(pl.multiple_of(i * SEG, SEG), SEG)
        cr, ci = _cmul(pw[0], pw[1], ir, ii)
        sr[rows, :] += cr
        si[rows, :] += ci
        return _cmul(pw[0], pw[1], ar, ai)

    lax.fori_loop(0, SEG_LEN, carry_in, (ar, ai), unroll=SCAN_UNROLL)


S5_LANES = 8 * SP
S5_BLOCKS = SN // S5_LANES


def _s5_specs(n=1):
    u_spec = pl.BlockSpec((L, n * 8 * SC), lambda j: (0, j))
    s_spec = pl.BlockSpec((L, n * S5_LANES), lambda j: (0, j))
    wb_spec = pl.BlockSpec((n * S5_LANES, 8 * SC), lambda j: (j, 0))
    wc_spec = pl.BlockSpec((n * 8 * SC, S5_LANES), lambda j: (j, 0))
    a_spec = pl.BlockSpec((1, n * S5_LANES), lambda j: (0, j))
    d_spec = pl.BlockSpec((1, n * 8 * SC), lambda j: (0, j))
    return u_spec, s_spec, wb_spec, wc_spec, a_spec, d_spec


S5_FWD_BLOCKS = 2


def s5_forward(proj, wb_re, wb_im, wc_re, wc_im, a_re, a_im, d, plans=()):
    n = S5_FWD_BLOCKS
    cols, lanes = 8 * SC, S5_LANES

    def kern(u_ref, wbr, wbi, wcr, wci, ar, ai, d_ref, sr_out, si_out, g_ref, sr, si):
        u = _wide(u_ref[...])
        for b in range(n):
            sr[:, b * lanes:(b + 1) * lanes], si[:, b * lanes:(b + 1) * lanes] = fn_s5_bu(
                u[:, b * cols:(b + 1) * cols], wbr[b * lanes:(b + 1) * lanes, :], wbi[b * lanes:(b + 1) * lanes, :])
        _scan_in_place(sr, si, ar[...], ai[...])
        for b in range(n):
            g_ref[:, b * cols:(b + 1) * cols] = fn_s5_out(
                sr[:, b * lanes:(b + 1) * lanes], si[:, b * lanes:(b + 1) * lanes], u[:, b * cols:(b + 1) * cols],
                d_ref[:, b * cols:(b + 1) * cols], wcr[b * cols:(b + 1) * cols, :], wci[b * cols:(b + 1) * cols, :])[0].astype(g_ref.dtype)
        sr_out[...] = sr[...].astype(sr_out.dtype)
        si_out[...] = si[...].astype(si_out.dtype)

    u_spec, s_spec, wb_spec, wc_spec, a_spec, d_spec = _s5_specs(n)
    return hosted_call(kern, name="s5_forward", grid=(S5_BLOCKS // n,), in_specs=[u_spec, wb_spec, wb_spec, wc_spec, wc_spec, a_spec, a_spec, d_spec],
                       out_specs=[s_spec, s_spec, u_spec], out_shape=[SDS((L, SN), BF16)] * 2 + [SDS((L, PW), BF16)],
                       operands=[proj, wb_re, wb_im, wc_re, wc_im, a_re, a_im, d], scratch_shapes=[pltpu.VMEM((L, n * lanes), F32)] * 2,
                       plans=plans)


def _adjoint_scan_in_place(lr, li, sr, si, a_re, a_im):
    lanes = lr.shape[1]
    ar = jnp.broadcast_to(a_re, (SEG, lanes))
    ai = -jnp.broadcast_to(a_im, (SEG, lanes))
    zero = jnp.zeros((SEG, lanes), F32)

    def local(k, carry):
        i = SEG_LEN - 1 - k
        rows = pl.ds(pl.multiple_of(i * SEG, SEG), SEG)
        mr, mi = _cmul(ar, ai, carry[0], carry[1])
        nr, ni = mr + lr[rows, :], mi + li[rows, :]
        lr[rows, :] = nr
        li[rows, :] = ni
        return nr, ni

    fr, fi = lax.fori_loop(0, SEG_LEN, local, (zero, zero), unroll=SCAN_UNROLL)
    pr, pi = _pow_seg_len(ar, ai)
    ir, ii = zero, zero
    for _ in range(SEG - 1):
        mr, mi = _cmul(pr, pi, ir, ii)
        ir, ii = _sub_shift(mr + fr, False), _sub_shift(mi + fi, False)

    def fix(rows, pw):
        cr, ci = _cmul(pw[0], pw[1], ir, ii)
        tr, ti = lr[rows, :] + cr, li[rows, :] + ci
        lr[rows, :] = tr
        li[rows, :] = ti
        return tr, ti

    def grad_a(tr, ti, spr, spi, acc):
        return acc[0] + tr * spr + ti * spi, acc[1] + ti * spr - tr * spi

    def carry_in(k, c):
        i = SEG_LEN - 1 - k
        rows = pl.ds(pl.multiple_of(i * SEG, SEG), SEG)
        prev = pl.ds(pl.multiple_of((i - 1) * SEG, SEG), SEG)
        tr, ti = fix(rows, (c[0], c[1]))
        acc = grad_a(tr, ti, sr[prev, :], si[prev, :], (c[2], c[3]))
        nr, ni = _cmul(c[0], c[1], ar, ai)
        return nr, ni, acc[0], acc[1]

    pwr, pwi, accr, acci = lax.fori_loop(0, SEG_LEN - 1, carry_in, (ar, ai, zero, zero), unroll=5)
    tr, ti = fix(pl.ds(0, SEG), (pwr, pwi))
    last = pl.ds((SEG_LEN - 1) * SEG, SEG)
    accr, acci = grad_a(tr, ti, _sub_shift(sr[last, :], True), _sub_shift(si[last, :], True), (accr, acci))
    return jnp.sum(accr, axis=0, keepdims=True), jnp.sum(acci, axis=0, keepdims=True)


S5_BWD_VMEM = 58 * 2**20


def s5_backward(dg, proj, s_re, s_im, wb_re, wb_im, wc_re, wc_im, a_re, a_im, d, dproj, plans=()):
    def kern(dg_ref, u_ref, sr_in, si_in, wbr, wbi, wcr, wci, ar, ai, d_ref, _, du_ref, dd_ref, dwcr, dwci, dwbr, dwbi, dar, dai, lr, li, sr, si):
        u = _wide(u_ref[...])
        sr[...], si[...] = _wide(sr_in[...]), _wide(si_in[...])
        _, vjp_out = jax.vjp(fn_s5_out, sr[...], si[...], u, d_ref[...], wcr[...], wci[...])
        lr[...], li[...], du_out, dd_ref[...], dwcr[...], dwci[...] = vjp_out((_wide(dg_ref[...]),))
        dar[...], dai[...] = _adjoint_scan_in_place(lr, li, sr, si, ar[...], ai[...])
        _, vjp_in = jax.vjp(fn_s5_bu, u, wbr[...], wbi[...])
        du_in, dwbr[...], dwbi[...] = vjp_in((lr[...], li[...]))
        du_ref[...] = (du_out + du_in).astype(du_ref.dtype)

    u_spec, s_spec, wb_spec, wc_spec, a_spec, d_spec = _s5_specs()
    outs = [(SDS(dproj.shape, dproj.dtype), u_spec), (SDS(d.shape, F32), d_spec), (SDS(wc_re.shape, F32), wc_spec), (SDS(wc_im.shape, F32), wc_spec),
            (SDS(wb_re.shape, F32), wb_spec), (SDS(wb_im.shape, F32), wb_spec), (SDS(a_re.shape, F32), a_spec), (SDS(a_im.shape, F32), a_spec)]
    return hosted_call(kern, name="s5_backward", grid=(S5_BLOCKS,),
                       in_specs=[u_spec, u_spec, s_spec, s_spec, wb_spec, wb_spec, wc_spec, wc_spec, a_spec, a_spec, d_spec, ANY],
                       out_specs=[sp for _, sp in outs], out_shape=[sd for sd, _ in outs], aliases={11: 0},
                       operands=[dg, proj, s_re, s_im, wb_re, wb_im, wc_re, wc_im, a_re, a_im, d, dproj],
                       scratch_shapes=[pltpu.VMEM((L, S5_LANES), F32)] * 4,
                       cparams=pltpu.CompilerParams(vmem_limit_bytes=S5_BWD_VMEM), plans=plans)


def fn_rms(x, g):
    return (rms(x, g),)


def fn_s5_disc(lre, lim, ls):
    step = jnp.exp(ls)
    e = jnp.exp(lre * step)
    a_re, a_im = e * jnp.cos(lim * step), e * jnp.sin(lim * step)
    den = lre * lre + lim * lim
    nr, ni = a_re - 1.0, a_im
    return a_re, a_im, (nr * lre + ni * lim) / den, (ni * lre - nr * lim) / den


def _group_mask(rows, cols, row_div, col_div):
    r = lax.broadcasted_iota(jnp.int32, (rows, cols), 0) // row_div % 8
    c = lax.broadcasted_iota(jnp.int32, (rows, cols), 1) // col_div
    return r == c


def _spread(x, mask):
    w = x.shape[1]
    copy = (lax.broadcasted_iota(jnp.int32, (w, 8 * w), 1) % w == lax.broadcasted_iota(jnp.int32, (w, 8 * w), 0)).astype(F32)
    return jnp.where(mask, jnp.dot(x, copy, precision=lax.Precision.HIGHEST, preferred_element_type=F32), 0.0)


def fn_s5_bmat(b_re, b_im, coef_re, coef_im):
    mask = _group_mask(b_re.shape[0], 8 * SC, SP, SC)
    return _spread(coef_re * b_re - coef_im * b_im, mask), _spread(coef_re * b_im + coef_im * b_re, mask)


def fn_s5_cmat(c_re, c_im):
    mask = _group_mask(c_re.shape[0], 8 * SP, SC, SP)
    return _spread(c_re, mask), _spread(c_im, mask)


def fn_s5_bu(u, wb_re, wb_im):
    return mm_nt(u, wb_re), mm_nt(u, wb_im)


def fn_s5_out(sr, si, u, d, wc_re, wc_im):
    y = mm_nt(sr, wc_re) - mm_nt(si, wc_im) + d * u
    return (jax.nn.gelu(y),)


def fn_merge_glu(z, mo, gate):
    yg = z[:, :PW] * jax.nn.sigmoid(z[:, PW:])
    return (jnp.concatenate([yg, mo], axis=1) * silu(gate),)


def fn_merge(prim, mo, gate):
    return (jnp.concatenate([prim, mo], axis=1) * silu(gate),)


def fn_mem_k(kv, g):
    return (jnp.concatenate([rms(kv[:, h * XHD:(h + 1) * XHD], g) for h in range(XH)], axis=1),)


def fn_mem_attn(xq, kn, v, g):
    outs = []
    for h in range(XH):
        sl = slice(h * XHD, (h + 1) * XHD)
        p = softmax_rows(mm_nt(rms(xq[:, sl], g), kn[:, sl]) * (XHD ** -0.5))
        outs.append(mm_nn(p, v[:, sl]))
    return (jnp.concatenate(outs, axis=1),)


def _half_rms(x, g):
    lo = lax.broadcasted_iota(jnp.int32, x.shape, 1) < ROPE
    x2 = x * x
    s_lo = jnp.sum(jnp.where(lo, x2, 0.0), axis=1, keepdims=True)
    s_hi = jnp.sum(jnp.where(lo, 0.0, x2), axis=1, keepdims=True)
    return x * lax.rsqrt(jnp.where(lo, s_lo, s_hi) / ROPE + EPS) * g


def _rope(x, cos2, sin_signed):
    first = lax.broadcasted_iota(jnp.int32, x.shape, 1) % ROPE < ROPE // 2
    return x * cos2 + jnp.where(first, lane_roll(x, 128 - ROPE // 2), lane_roll(x, ROPE // 2)) * sin_signed


def fn_mla_prep(q, kv, kr, cos2, sin_signed, qnn, knn, qrn, krn):
    lo = lax.broadcasted_iota(jnp.int32, kr.shape, 1) < ROPE
    kr_pad = jnp.where(lo, _rope(_half_rms(kr, krn), cos2, sin_signed), 0.0)
    qf, kf, vs = [], [], []
    for m in range(MH // 2):
        pair = _rope(_half_rms(q[:, MH * NOPE + 128 * m:MH * NOPE + 128 * (m + 1)], qrn), cos2, sin_signed)
        for h, rope_h in ((2 * m, pair), (2 * m + 1, lane_roll(pair, ROPE))):
            qf.append(jnp.concatenate([rms(q[:, NOPE * h:NOPE * (h + 1)], qnn), jnp.where(lo, rope_h, 0.0)], axis=1))
    for h in range(MH):
        kf.append(jnp.concatenate([rms(kv[:, 256 * h:256 * h + NOPE], knn), kr_pad], axis=1))
        vs.append(kv[:, 256 * h + NOPE:256 * (h + 1)])
    return jnp.stack(qf), jnp.stack(kf), jnp.stack(vs)


ATT_TQ = 512


def _attn_scores(q, kf):
    tq = q.shape[0]
    scale = (NOPE + ROPE) ** -0.5
    own = _dg(q, kf[-tq:], 1, 1) * scale
    own = jnp.where(lax.broadcasted_iota(jnp.int32, own.shape, 1) <= lax.broadcasted_iota(jnp.int32, own.shape, 0), own, jnp.finfo(F32).min)
    return own if kf.shape[0] == tq else jnp.concatenate([_dg(q, kf[:-tq], 1, 1) * scale, own], axis=1)


ATT_FWD_HEADS, ATT_BWD_HEADS = 4, 2


def _attn_specs(heads):
    q_spec = pl.BlockSpec((heads, ATT_TQ, 256), lambda h, i: (h, i, 0))
    k_spec = pl.BlockSpec((heads, L, 256), lambda h, i: (h, 0, 0))
    v_spec = pl.BlockSpec((heads, L, 128), lambda h, i: (h, 0, 0))
    o_spec = pl.BlockSpec((ATT_TQ, heads * 128), lambda h, i: (i, h))
    lse_spec = pl.BlockSpec((heads, ATT_TQ, 1), lambda h, i: (h, i, 0))
    return q_spec, k_spec, v_spec, o_spec, lse_spec


def _attn_branches(heads, body):
    i = pl.program_id(1)
    for t in range(L // ATT_TQ):
        @pl.when(i == t)
        def _(t=t):
            for hh in range(heads):
                body(hh, slice(hh * 128, (hh + 1) * 128), (t + 1) * ATT_TQ)


def causal_attn(qf, kf, vh):
    def kern(q_ref, k_ref, v_ref, o_ref, lse_ref):
        def body(hh, lanes, keys):
            s = _attn_scores(q_ref[hh], k_ref[hh, :keys, :])
            m = jnp.max(s, axis=-1, keepdims=True)
            e = jnp.exp(s - m)
            total = jnp.sum(e, axis=-1, keepdims=True)
            o_ref[:, lanes] = (_dg(e, v_ref[hh, :keys, :], 1, 0) / total).astype(o_ref.dtype)
            lse_ref[hh] = m + jnp.log(total)

        _attn_branches(ATT_FWD_HEADS, body)

    q_spec, k_spec, v_spec, o_spec, lse_spec = _attn_specs(ATT_FWD_HEADS)
    return pl.pallas_call(kern, grid=(MH // ATT_FWD_HEADS, L // ATT_TQ), in_specs=[q_spec, k_spec, v_spec], out_specs=[o_spec, lse_spec],
                          out_shape=[SDS((L, MH * VD), F32), SDS((MH, L, 1), F32)], name="l1_attn", compiler_params=_cparams())(qf, kf, vh)


def causal_attn_bwd(qf, kf, vh, out, lse, dout):
    scale = (NOPE + ROPE) ** -0.5

    def kern(q_ref, k_ref, v_ref, o_ref, lse_ref, do_ref, dq_ref, dk_ref, dv_ref):
        @pl.when(pl.program_id(1) == 0)
        def _():
            dk_ref[...] = jnp.zeros_like(dk_ref)
            dv_ref[...] = jnp.zeros_like(dv_ref)

        def body(hh, lanes, keys):
            q, k, v, do = q_ref[hh], k_ref[hh, :keys, :], v_ref[hh, :keys, :], do_ref[:, lanes]
            p = jnp.exp(_attn_scores(q, k) - lse_ref[hh])
            delta = jnp.sum(do * _wide(o_ref[:, lanes]), axis=-1, keepdims=True)
            dv_ref[hh, :keys, :] += _dg(p, do, 0, 0)
            ds = p * (_dg(do, v, 1, 1) - delta) * scale
            dq_ref[hh] = _dg(ds, k, 1, 0)
            dk_ref[hh, :keys, :] += _dg(ds, q, 0, 0)

        _attn_branches(ATT_BWD_HEADS, body)

    q_spec, k_spec, v_spec, o_spec, lse_spec = _attn_specs(ATT_BWD_HEADS)
    return pl.pallas_call(kern, grid=(MH // ATT_BWD_HEADS, L // ATT_TQ), in_specs=[q_spec, k_spec, v_spec, o_spec, lse_spec, o_spec],
                          out_specs=[q_spec, k_spec, v_spec], out_shape=[SDS(qf.shape, F32), SDS(kf.shape, F32), SDS(vh.shape, F32)],
                          name="l1_attn_bwd", compiler_params=_cparams())(qf, kf, vh, out, lse, dout)


def loss_and_grad(y, target, tl=512):
    def kern(y_ref, t_ref, dy_ref, loss_ref):
        d = y_ref[...] - t_ref[...]
        dy_ref[...] = d / D

        @pl.when(pl.program_id(0) == 0)
        def _():
            loss_ref[...] = jnp.zeros_like(loss_ref)

        loss_ref[...] += 0.5 * jnp.sum(jnp.sum(d * d, axis=1, keepdims=True), axis=0, keepdims=True) / D

    return pl.pallas_call(kern, grid=(L // tl,), in_specs=[rspec(tl, D), rspec(tl, D)], out_specs=[rspec(tl, D), cspec((1, 1))],
                          out_shape=[SDS((L, D), F32), SDS((1, 1), F32)], name="loss", compiler_params=_cparams())(y, target)


def adamw(name, w, g, m, v):
    rows, cols = w.shape
    block_row_bytes = 7 * 2 * 4 * max(cols, 128)
    tr = _row_tile(rows, min(2048, ADAMW_VMEM // block_row_bytes // 8 * 8), 8)

    def kern(w_ref, g_ref, m_ref, v_ref, d_ref, nm_ref, nv_ref):
        gg = g_ref[...]
        nm = ADAM_B1 * m_ref[...] + (1.0 - ADAM_B1) * gg
        nv = ADAM_B2 * v_ref[...] + (1.0 - ADAM_B2) * jnp.square(gg)
        m_hat = nm / (1.0 - ADAM_B1 ** ADAM_STEP)
        v_hat = nv / (1.0 - ADAM_B2 ** ADAM_STEP)
        d_ref[...] = -ADAM_LR * (m_hat / (jnp.sqrt(v_hat) + ADAM_EPS) + ADAM_WD * w_ref[...])
        nm_ref[...] = nm
        nv_ref[...] = nv

    spec = rspec(tr, cols)
    return hosted_call(kern, name=name, grid=(rows // tr,), in_specs=[spec] * 4, out_specs=[spec] * 3,
                       out_shape=[SDS((rows, cols), F32)] * 3, operands=[w, g, m, v])


def _row_tile(rows, cap=512, unit=16):
    return max(t for t in range(unit, cap + 1, unit) if rows % t == 0)


def _place():
    x, y, c = lax.axis_index("x"), lax.axis_index("y"), lax.axis_index("c")
    return x, y, c, [(1 - x, y), (x, 1 - y), (1 - x, 1 - y)]


def _row_chunks(rows, n, dtype):
    unit = 32 // jnp.dtype(dtype).itemsize
    base, extra = divmod(rows // unit, n)
    out, start = [], 0
    for k in range(n):
        size = (base + (k < extra)) * unit
        if size:
            out.append((start, size))
            start += size
    assert start == rows, (rows, unit)
    return out


PIECE_BYTES = 1 << 20


def _pieces(shapes_dtypes, rows_of):
    out = []
    for b, (shape, dtype) in enumerate(shapes_dtypes):
        rows = rows_of(shape)
        n = max(1, min(4, rows * shape[-1] * jnp.dtype(dtype).itemsize // PIECE_BYTES))
        out += [(b, st, sz) for st, sz in _row_chunks(rows, n, dtype)]
    return out


def all_gather_chips(name, shards):
    nb = len(shards)
    pieces = _pieces([(s.shape, s.dtype) for s in shards], lambda shape: shape[0] // 2)
    n = len(pieces)

    def body(*refs):
        x_refs, out_refs, send_sems, recv_sems = refs[:nb], refs[nb:2 * nb], refs[2 * nb], refs[2 * nb + 1]
        x, y, c, chips = _place()
        sibling = (x, y, 1 - c)
        mine = 2 * x + y

        def copy(sem, chip, cc, k, to, from_input=False):
            b, st, sz = pieces[k]
            rows_k = pl.ds(cc * (x_refs[b].shape[0] // 2) + st, sz)
            dst = out_refs[b].at[chip, rows_k, :]
            return pltpu.make_async_remote_copy(src_ref=x_refs[b].at[rows_k, :] if from_input else dst, dst_ref=dst,
                                                send_sem=send_sems.at[sem], recv_sem=recv_sems.at[sem], device_id=to, device_id_type=MESH_ID)

        order = [(k, j, 2 * cx + cy, (cx, cy, c)) for k in range(n) for j, (cx, cy) in enumerate(chips)]
        first = [copy(j * n + k, mine, c, k, to, from_input=True) for k, j, _, to in order]
        for cp in first:
            cp.start()
        passed = []
        for k, j, chip, _ in order:
            copy(j * n + k, chip, c, k, sibling).wait_recv()
            passed.append(copy((3 + j) * n + k, chip, c, k, sibling))
            passed[-1].start()
        for k, j, chip, _ in order:
            copy((3 + j) * n + k, chip, 1 - c, k, sibling).wait_recv()
        for cp in first + passed:
            cp.wait_send()

    return pl.pallas_call(body, in_specs=[ANY] * nb, out_specs=[ANY] * nb, out_shape=[SDS((4,) + s.shape, s.dtype) for s in shards],
                          scratch_shapes=_dma_sems(6 * n), name=name)(*shards)


def plan_gather_ici(shards):
    pieces = _pieces([(s.shape, s.dtype) for s in shards], lambda shape: shape[0] // 2)
    n = len(pieces)

    def copies(x_refs, out_refs, send_sems, recv_sems):
        x, y, c, chips = _place()
        mine = 2 * x + y

        def copy(j, k, chip, to, from_input):
            b, st, sz = pieces[k]
            rows_k = pl.ds(c * (x_refs[b].shape[0] // 2) + st, sz)
            dst = out_refs[b].at[chip, rows_k, :]
            return pltpu.make_async_remote_copy(src_ref=x_refs[b].at[rows_k, :] if from_input else dst, dst_ref=dst, send_sem=send_sems.at[j * n + k],
                                                recv_sem=recv_sems.at[j * n + k], device_id=to, device_id_type=MESH_ID)

        order = [(k, j, 2 * cx + cy, (cx, cy, c)) for k in range(n) for j, (cx, cy) in enumerate(chips)]
        return [copy(j, k, mine, to, True) for k, j, _, to in order], [copy(j, k, chip, to, False) for k, j, chip, to in order]

    return Plan(shards, [SDS((4,) + s.shape, s.dtype) for s in shards], {}, 3 * n, copies)


def plan_gather_pass(gathered):
    pieces = _pieces([(g.shape[1:], g.dtype) for g in gathered], lambda shape: shape[0] // 2)
    n = len(pieces)

    def copies(_, out_refs, send_sems, recv_sems):
        x, y, c, chips = _place()

        def copy(j, k, chip, cc):
            b, st, sz = pieces[k]
            rows_k = out_refs[b].at[chip, pl.ds(cc * (out_refs[b].shape[1] // 2) + st, sz), :]
            return pltpu.make_async_remote_copy(src_ref=rows_k, dst_ref=rows_k, send_sem=send_sems.at[j * n + k], recv_sem=recv_sems.at[j * n + k],
                                                device_id=(x, y, 1 - c), device_id_type=MESH_ID)

        order = [(k, j, 2 * cx + cy) for k in range(n) for j, (cx, cy) in enumerate(chips)]
        return [copy(j, k, chip, c) for k, j, chip in order], [copy(j, k, chip, 1 - c) for k, j, chip in order]

    return Plan(gathered, [SDS(g.shape, g.dtype) for g in gathered], {i: i for i in range(len(gathered))}, 3 * n, copies)


def plan_pair_exchange(gs):
    pieces = _pieces([(g.shape, g.dtype) for g in gs], lambda shape: shape[1] // 2)

    def copies(g_refs, got_refs, send_sems, recv_sems):
        x, y, c, _ = _place()
        swaps = [pltpu.make_async_remote_copy(src_ref=g_refs[b].at[:, pl.ds((1 - c) * (g_refs[b].shape[1] // 2) + st, sz), :],
                                              dst_ref=got_refs[b].at[:, pl.ds(st, sz), :], send_sem=send_sems.at[k], recv_sem=recv_sems.at[k],
                                              device_id=(x, y, 1 - c), device_id_type=MESH_ID)
                 for k, (b, st, sz) in enumerate(pieces)]
        return swaps, swaps

    return Plan(gs, [SDS((g.shape[0], g.shape[1] // 2, g.shape[2]), g.dtype) for g in gs], {}, len(pieces), copies)


def plan_chip_scatter(ps):
    pieces = _pieces([(p.shape, p.dtype) for p in ps], lambda shape: shape[1])
    n = len(pieces)

    def copies(p_refs, q_refs, send_sems, recv_sems):
        x, y, c, chips = _place()
        mine = 2 * x + y

        def copy(j, k, src_slot, dst_slot, to):
            b, st, sz = pieces[k]
            return pltpu.make_async_remote_copy(src_ref=p_refs[b].at[src_slot, pl.ds(st, sz), :], dst_ref=q_refs[b].at[dst_slot, pl.ds(st, sz), :],
                                                send_sem=send_sems.at[j * n + k], recv_sem=recv_sems.at[j * n + k], device_id=to,
                                                device_id_type=MESH_ID)

        order = [(k, j, 2 * cx + cy, (cx, cy, c)) for k in range(n) for j, (cx, cy) in enumerate(chips)]
        return [copy(j, k, chip, mine, to) for k, j, chip, to in order], [copy(j, k, mine, chip, to) for k, j, chip, to in order]

    return Plan(ps, [SDS(p.shape, p.dtype) for p in ps], {}, 3 * n, copies)


def plan_pair_join(bufs):
    pieces = _pieces([(b.shape, b.dtype) for b in bufs], lambda shape: shape[0] // 2)

    def copies(_, out_refs, send_sems, recv_sems):
        x, y, c, _ = _place()

        def copy(k, cc):
            b, st, sz = pieces[k]
            rows_k = out_refs[b].at[pl.ds(cc * (out_refs[b].shape[0] // 2) + st, sz), :]
            return pltpu.make_async_remote_copy(src_ref=rows_k, dst_ref=rows_k, send_sem=send_sems.at[k], recv_sem=recv_sems.at[k],
                                                device_id=(x, y, 1 - c), device_id_type=MESH_ID)

        return [copy(k, c) for k in range(len(pieces))], [copy(k, 1 - c) for k in range(len(pieces))]

    return Plan(bufs, [SDS(b.shape, b.dtype) for b in bufs], {i: i for i in range(len(bufs))}, len(pieces), copies)


def run_plan(name, plan):
    hosted_call(lambda: None, name=name, grid=(1,), in_specs=[], out_specs=[], out_shape=[], operands=[], plans=[plan])
    return plan.results


HBM = pl.BlockSpec(memory_space=pltpu.HBM)
SEMS = pl.BlockSpec(memory_space=pltpu.SEMAPHORE)
SPLIT_PARAMS = dict(has_side_effects=pltpu.SideEffectType.DATAFLOW_SIDE_EFFECTING)


def _plan_buffers(plan):
    in_place = {o: i for i, o in plan.aliases.items()}
    bufs = [pltpu.with_memory_space_constraint(a, pltpu.HBM) for a in plan.operands]
    where = []
    for o, sd in enumerate(plan.out_shape):
        if o in in_place:
            where.append(in_place[o])
        else:
            where.append(len(bufs))
            bufs.append(pltpu.with_memory_space_constraint(lax.empty(sd.shape, sd.dtype), pltpu.HBM))
    return bufs, where


def split_start(name, plans):
    layout = [_plan_buffers(p) for p in plans]
    counts = [len(b) for b, _ in layout]
    n_buf = sum(counts)

    def body(*refs):
        sems, token = refs[n_buf:n_buf + 2 * len(plans)], refs[-1]
        pos = 0
        for k, (p, (_, where)) in enumerate(zip(plans, layout)):
            mine = refs[pos:pos + counts[k]]
            pos += counts[k]
            sends, _ = p.copies(mine[:len(p.operands)], [mine[w] for w in where], sems[2 * k], sems[2 * k + 1])
            for cp in sends:
                cp.start()
        token[...] = jnp.zeros_like(token)

    bufs = [b for bs, _ in layout for b in bs]
    res = pl.pallas_call(
        body, name=name, in_specs=[HBM] * n_buf,
        out_specs=[SEMS] * (2 * len(plans)) + [HBM] * n_buf + [pl.BlockSpec(memory_space=pltpu.VMEM)],
        out_shape=[pltpu.SemaphoreType.DMA((p.n_sems,)) for p in plans for _ in range(2)] + [pltpu.HBM(b.shape, b.dtype) for b in bufs]
        + [SDS((8, 128), F32)],
        input_output_aliases={i: 2 * len(plans) + i for i in range(n_buf)}, compiler_params=pltpu.CompilerParams(**SPLIT_PARAMS))(*bufs)
    pos = 2 * len(plans)
    for k, p in enumerate(plans):
        p.in_flight = (res[2 * k], res[2 * k + 1], list(res[pos:pos + counts[k]]), layout[k][1])
        pos += counts[k]
    return res[-1]


def split_wait(name, plan, after):
    send_sems, recv_sems, bufs, where = plan.in_flight
    n_buf = len(bufs)

    def body(*refs):
        mine = refs[:n_buf]
        sends, recvs = plan.copies(mine[:len(plan.operands)], [mine[w] for w in where], refs[n_buf], refs[n_buf + 1])
        for cp in recvs:
            cp.wait_recv()
        for cp in sends:
            cp.wait_send()

    res = pl.pallas_call(body, name=name, in_specs=[HBM] * n_buf + [SEMS, SEMS, ANY], out_specs=[HBM] * n_buf,
                         out_shape=[pltpu.HBM(b.shape, b.dtype) for b in bufs], input_output_aliases={i: i for i in range(n_buf)},
                         compiler_params=pltpu.CompilerParams(**SPLIT_PARAMS))(*bufs, send_sems, recv_sems, after)
    plan.results = [res[w] for w in where]
    return plan.results


def pair_add(name, g, got, place):
    slots, rows, cols = g.shape
    half = rows // 2
    tr = _row_tile(half)
    nb = half // tr

    def kern(_, g_ref, t_ref, o_ref):
        o_ref[...] = (g_ref[...].astype(F32) + t_ref[...].astype(F32)).astype(o_ref.dtype)

    blk = pl.BlockSpec((None, tr, cols), lambda s, i, p: (s, i, 0))
    grid_spec = pltpu.PrefetchScalarGridSpec(
        num_scalar_prefetch=1, grid=(slots, nb),
        in_specs=[pl.BlockSpec((None, tr, cols), lambda s, i, p: (s, p[1] * nb + i, 0)), blk], out_specs=blk)
    return pl.pallas_call(kern, grid_spec=grid_spec, out_shape=SDS((slots, half, cols), g.dtype), name=name,
                          compiler_params=_cparams())(place, g, got)


def chip_add(name, p, q, place):
    slots, half, cols = p.shape
    tr = _row_tile(half)
    nb = half // tr

    def kern(_, p_ref, q1, q2, q3, o_ref):
        o_ref[...] = p_ref[...].astype(F32) + q1[...].astype(F32) + q2[...].astype(F32) + q3[...].astype(F32)

    def slot(k):
        return pl.BlockSpec((None, tr, cols), lambda i, pr: ((pr[0] + k) % slots, i, 0))

    grid_spec = pltpu.PrefetchScalarGridSpec(
        num_scalar_prefetch=1, grid=(nb,), in_specs=[slot(0), slot(1), slot(2), slot(3)],
        out_specs=pl.BlockSpec((tr, cols), lambda i, pr: (pr[1] * nb + i, 0)))
    return pl.pallas_call(kern, grid_spec=grid_spec, out_shape=SDS((2 * half, cols), F32), name=name,
                          compiler_params=_cparams())(place, p, q, q, q)


def pair_adds(tag, gs, gots, place):
    return [pair_add(f"{tag}_pair_add_{i}", g, got, place) for i, (g, got) in enumerate(zip(gs, gots))]


def chip_adds(tag, pairs, qs, place):
    return [chip_add(f"{tag}_chip_add_{i}", p, q, place) for i, (p, q) in enumerate(zip(pairs, qs))]


def reduce_scatter_chips(tag, gs, place):
    pairs = pair_adds(tag, gs, run_plan(tag + "_pair_exchange", plan_pair_exchange(gs)), place)
    return run_plan(tag + "_pair_join", plan_pair_join(chip_adds(tag, pairs, run_plan(tag + "_chip_scatter", plan_chip_scatter(pairs)), place)))


def all_reduce_in_vmem(name, g):
    slots, rows, cols = g.shape
    half = rows // 2

    def kern(g_ref, out_ref, got, pair, q, send_sems, recv_sems):
        x, y, c, chips = _place()
        mine = 2 * x + y
        other_core = (x, y, 1 - c)
        my_rows, its_rows = pl.ds(c * half, half), pl.ds((1 - c) * half, half)

        def exchange(copies):
            for cp in copies:
                cp.start()
            for cp in copies:
                cp.wait()

        def copy(k, src, dst, to):
            return pltpu.make_async_remote_copy(src_ref=src, dst_ref=dst, send_sem=send_sems.at[k], recv_sem=recv_sems.at[k], device_id=to,
                                                device_id_type=MESH_ID)

        exchange([copy(0, g_ref.at[:, its_rows, :], got, other_core)])
        pair[...] = (g_ref[:, my_rows, :].astype(F32) + got[...].astype(F32)).astype(pair.dtype)
        exchange([copy(1 + j, pair.at[2 * cx + cy], q.at[mine], (cx, cy, c)) for j, (cx, cy) in enumerate(chips)])
        total = pair[mine].astype(F32)
        for k in range(1, slots):
            total = total + q[(mine + k) % slots].astype(F32)
        out_ref[mine, my_rows, :] = total
        reduced = out_ref.at[mine, my_rows, :]
        exchange([copy(4 + j, reduced, reduced, (cx, cy, c)) for j, (cx, cy) in enumerate(chips)] + [copy(7, reduced, reduced, other_core)])
        exchange([copy(8 + j, out_ref.at[2 * cx + cy, my_rows, :], out_ref.at[2 * cx + cy, my_rows, :], other_core)
                  for j, (cx, cy) in enumerate(chips)])

    vmem = pl.BlockSpec(memory_space=pltpu.VMEM)
    return pl.pallas_call(kern, in_specs=[vmem], out_specs=vmem, out_shape=SDS(g.shape, F32), name=name, compiler_params=_cparams(),
                          scratch_shapes=[pltpu.VMEM((slots, half, cols), g.dtype)] * 3 + [pltpu.SemaphoreType.DMA((11,))] * 2)(g)


BIG = [("w_out", (2, 512, 1024)), ("w_mem_kv", (2, 256, 1024)), ("s5_w_in", (1, 1024, 1024)), ("s5_w_glu", (1, 1536, 768)),
       ("mla_w_in", (1, 1024, 848)), ("mla_w_uq", (1, 512, 576)), ("mla_w_ukv", (1, 256, 768))]
SHARDED_SMALL = [("mla_q_lora_norm", (1, 128)), ("mla_kv_lora_norm", (1, 64))]
SMALL = [("ln_gain", (2, 1024)), ("mem_norm", (2, 1024)), ("xq_norm", (2, 128)), ("xk_norm", (2, 128)),
         ("s5_lambda_re", (1, 96, 64)), ("s5_lambda_im", (1, 96, 64)), ("s5_log_step", (1, 96)),
         ("s5_b_re", (1, 96, 64, 16)), ("s5_b_im", (1, 96, 64, 16)), ("s5_c_re", (1, 96, 16, 64)), ("s5_c_im", (1, 96, 16, 64)),
         ("s5_d", (1, 1536)), ("mla_q_nope_norm", (1, 128)), ("mla_k_nope_norm", (1, 128)), ("mla_q_rope_norm", (1, 64)),
         ("mla_k_rope_norm", (1, 64))]
WEIGHT_ORDER = ["ln_gain", "w_out", "mem_norm", "w_mem_kv", "xq_norm", "xk_norm", "s5_w_in", "s5_lambda_re", "s5_lambda_im",
                "s5_log_step", "s5_b_re", "s5_b_im", "s5_c_re", "s5_c_im", "s5_d", "s5_w_glu", "mla_w_in", "mla_q_lora_norm",
                "mla_kv_lora_norm", "mla_w_uq", "mla_w_ukv", "mla_q_nope_norm", "mla_k_nope_norm", "mla_q_rope_norm", "mla_k_rope_norm"]
MINOR_LAST = {"mla_w_in": (0, 2, 1), "mla_w_uq": (0, 2, 1), "s5_b_re": (0, 2, 3, 1), "s5_b_im": (0, 2, 3, 1),
              "s5_c_re": (0, 2, 3, 1), "s5_c_im": (0, 2, 3, 1)}
SMALL_FULL = SMALL + [(n, (1, 4 * s[1])) for n, s in SHARDED_SMALL]
N_SMALL = sum(math.prod(s) for _, s in SMALL_FULL)
SMALL_ROWS, SMALL_LANES = 128, 1024

PAIR_OUT, PAIR_MKV, PAIR_ROWS = 0, 512, 768


def stack_shards(w, dtype):
    pairs = [jnp.concatenate([w["w_out"][l], w["w_mem_kv"][l]], axis=0).astype(dtype) for l in range(2)]
    return ([w["s5_w_in"][0].astype(dtype)], [pairs[0], w["s5_w_glu"][0].astype(dtype)],
            [pairs[1], w["mla_w_ukv"][0].astype(dtype), jnp.pad(w["mla_w_in"][0].T.astype(dtype), ((0, IN1T_ROWS - MLA_IN // 4), (0, 0))),
             w["mla_w_uq"][0].T.astype(dtype)])


def pair_views(pair):
    return {"w_out": Sharded(pair, "row", PAIR_OUT, 512), "w_mem_kv": Sharded(pair, "row", PAIR_MKV, 256)}


def grad_views():
    pair = SDS((4, PAIR_ROWS, 1024), BF16)
    return {"w_out": Sharded(pair, "row", PAIR_OUT, 512), "w_mem_kv": Sharded(pair, "row", PAIR_MKV, 256),
            "s5_w_in": Sharded(SDS((4, 1024, 1024), BF16), "col", 0, 1024), "s5_w_glu": Sharded(SDS((4, 1536, 768), BF16), "col", 0, 1536),
            "mla_w_ukv": Sharded(SDS((4, 256, 768), BF16), "col", 0, 256)}


IN1T_ROWS = 864


def in1_rows_permute(wt):
    o1, o2, o3, o4 = QL, QL + KVL, QL + KVL + ROPE, QL + KVL + ROPE + XQW
    return jnp.concatenate([wt[o4:], wt[:o1], wt[o3:o4], wt[o1:o2], wt[o2:o3], jnp.zeros((MLA_IN_P - MLA_IN, wt.shape[1]), wt.dtype)], axis=0)


def in1_rows_unpermute(d):
    return jnp.concatenate([d[2048:2560], d[3072:3328], d[3328:3392], d[2560:3072], d[:2048]], axis=0)


def uq_rows_permute(wt):
    w3 = wt.reshape(MH, NOPE + ROPE, wt.shape[1])
    return jnp.concatenate([w3[:, :NOPE].reshape(MH * NOPE, wt.shape[1]), w3[:, NOPE:].reshape(MH * ROPE, wt.shape[1])], axis=0)


def uq_rows_unpermute(d):
    dn = d[:MH * NOPE].reshape(MH, NOPE, d.shape[1])
    dr = d[MH * NOPE:].reshape(MH, ROPE, d.shape[1])
    return jnp.concatenate([dn, dr], axis=1).reshape(MH * (NOPE + ROPE), d.shape[1])


def time_permute(a):
    return a.reshape(SEG, SEG_LEN, a.shape[-1]).transpose(1, 0, 2).reshape(L, a.shape[-1])


def time_unpermute(a):
    return a.reshape(SEG_LEN, SEG, a.shape[-1]).transpose(1, 0, 2).reshape(L, a.shape[-1])


def mem_branch_fwd(tag, mem, mem_norm, w_mem_kv, xk_norm):
    mn = row_fwd(tag + "_mem_rms", fn_rms, ML, ML, [(mem, D, 0)], [mem_norm], [(D, BF16)])[0]
    kv = matmul(tag + "_mem_kv", mn, w_mem_kv, "nn", F32)
    kn = row_fwd(tag + "_mem_knorm", fn_mem_k, ML, ML, [(kv, XQW, 0)], [xk_norm], [(XQW, F32)])[0]
    return mn, kv, kn


def mem_branch_bwd(tag, mem, mem_norm, w_mem_kv, xk_norm, mn, kv, dkn, dv, g_view, g_wide):
    dk, dxk = row_bwd(tag + "_mem_knorm_bwd", fn_mem_k, ML, ML, [(kv, XQW, 0)], [xk_norm], [(dkn, XQW, 0)], [True], [True])
    dkv = jnp.concatenate([dk, dv], axis=1)
    dmn = matmul(tag + "_mem_kv_dx", dkv, w_mem_kv, "nt", F32)
    g_wide = matmul(tag + "_mem_kv_dw", mn, dkv, "tn", out=g_view, into=g_wide)
    dmem_norm = row_bwd(tag + "_mem_rms_bwd", fn_rms, ML, ML, [(mem, D, 0)], [mem_norm], [(dmn, D, 0)], [False], [True])[0]
    return g_wide, dmem_norm, dxk


def mem_attn_fwd(tag, proj, cb, kn, kv, xq_norm):
    return row_fwd(tag + "_mem_attn", fn_mem_attn, L, ROW_TILE, [(proj, XQW, cb)], [kn, kv[:, XQW:], xq_norm], [(XQW, F32)])[0]


def mem_attn_bwd(tag, proj, cb, kn, kv, xq_norm, dmo, dproj):
    place = {"cols": proj.shape[1], "cb": cb, "into": dproj, "dtype": dproj.dtype}
    return row_bwd(tag + "_mem_attn_bwd", fn_mem_attn, L, ROW_TILE, [(proj, XQW, cb)], [kn, kv[:, XQW:], xq_norm], [(dmo, XQW, 0)],
                   [place], [True, True, True])


def device_step(x, mem, positions, target, small, env, hooks=None):
    hooks = hooks or {}

    def plans_for(name):
        return hooks[("plans", name)](env) if ("plans", name) in hooks else ()

    def around(when, name, last=None):
        if (when, name) in hooks:
            hooks[(when, name)](env, last)

    g = {}
    gw = grad_views()
    ln, mem_norm, xq_norm, xk_norm = small["ln_gain"], small["mem_norm"], small["xq_norm"], small["xk_norm"]

    lre, lim = small["s5_lambda_re"][0], small["s5_lambda_im"][0]
    ls = small["s5_log_step"].reshape(SG, 1)
    one = pl.BlockSpec((SG, SP), lambda i: (0, 0))
    col = pl.BlockSpec((SG, 1), lambda i: (0, 0))
    disc_ins = [(lre, one), (lim, one), (ls, col)]
    a_re, a_im, coef_re, coef_im = stage("s5_disc", fn_s5_disc, (1,), disc_ins, [(SDS((SG, SP), F32), one)] * 4)
    b_re, b_im = small["s5_b_re"].reshape(SN, SC), small["s5_b_im"].reshape(SN, SC)
    c_re, c_im = small["s5_c_re"].reshape(PW, SP), small["s5_c_im"].reshape(PW, SP)
    bmat_rows = [(b_re, SC, 0), (b_im, SC, 0), (coef_re.reshape(SN, 1), 1, 0), (coef_im.reshape(SN, 1), 1, 0)]
    wb_re, wb_im = row_fwd("s5_bmat", fn_s5_bmat, SN, BMAT_TILE, bmat_rows, [], [(128, F32)] * 2)
    cmat_rows = [(c_re, SP, 0), (c_im, SP, 0)]
    wc_re, wc_im = row_fwd("s5_cmat", fn_s5_cmat, PW, CMAT_TILE, cmat_rows, [], [(512, F32)] * 2)
    a_re_v, a_im_v = a_re.reshape(1, SN), a_im.reshape(1, SN)
    s5_d = small["s5_d"]

    xp = time_permute(x)
    h0 = row_fwd("l0_rms", fn_rms, L, ROW_TILE, [(xp, D, 0)], [ln[0:1]], [(D, BF16)])[0]
    around("before", "l0_in", wb_re)
    w_in0 = Sharded(env["in0"], "col", 0, 1024)
    proj0 = matmul("l0_in", h0, w_in0, "nn")
    s_re, s_im, g0 = s5_forward(proj0, wb_re, wb_im, wc_re, wc_im, a_re_v, a_im_v, s5_d, plans=plans_for("s5_forward"))
    around("before", "l0_glu", g0)
    w0 = dict(pair_views(env["pair0"]), s5_w_glu=Sharded(env["glu"], "col", 0, 1536))
    z0 = matmul("l0_glu", g0, w0["s5_w_glu"], "nn", plans=plans_for("l0_glu"))
    mn0, kv0, kn0 = mem_branch_fwd("l0", mem, mem_norm[0:1], w0["w_mem_kv"], xk_norm[0:1])
    mo0 = mem_attn_fwd("l0", proj0, 3, kn0, kv0, xq_norm[0:1])
    o0 = row_fwd("l0_merge", fn_merge_glu, L, ROW_TILE, [(z0, 2 * PW, 0), (mo0, XQW, 0), (proj0, BW, 1)], [], [(BW, BF16)])[0]
    around("before", "l0_out", o0)
    x1p = matmul("l0_out", o0, w0["w_out"], "nn", F32, add=xp, plans=plans_for("l0_out"))
    around("after", "l0_out", x1p)
    x1 = time_unpermute(x1p)

    w1 = dict(pair_views(env["pair1"]), mla_w_ukv=Sharded(env["ukv"], "col", 0, 256))
    w_in1, w_uq = env["w_in1"], env["w_uq"]
    h1 = row_fwd("l1_rms", fn_rms, L, ROW_TILE, [(x1, D, 0)], [ln[1:2]], [(D, BF16)])[0]
    proj1 = matmul("l1_in", h1, w_in1, "nt")
    qln, kvln = env["q_lora_norm"].reshape(1, QL), env["kv_lora_norm"].reshape(1, KVL)
    cqn = row_fwd("l1_q_lora_rms", fn_rms, L, ROW_TILE, [(proj1, QL, 4)], [qln], [(QL, BF16)])[0]
    ckvn = row_fwd("l1_kv_lora_rms", fn_rms, L, ROW_TILE, [(proj1, KVL, 12)], [kvln], [(KVL, BF16)])[0]
    q = matmul("l1_uq", cqn, w_uq, "nt")
    kv = matmul("l1_ukv", ckvn, w1["mla_w_ukv"], "nn")
    inv_freq = ROPE_THETA ** (-jnp.arange(ROPE // 2, dtype=F32) / (ROPE // 2))
    ang = positions.astype(F32)[:, None] * inv_freq
    cos2 = jnp.tile(jnp.cos(ang), (1, 4))
    sin_signed = jnp.tile(jnp.concatenate([-jnp.sin(ang), jnp.sin(ang)], axis=1), (1, 2))
    qnn, knn = small["mla_q_nope_norm"], small["mla_k_nope_norm"]
    qrn, krn = jnp.tile(small["mla_q_rope_norm"], (1, 2)), jnp.tile(small["mla_k_rope_norm"], (1, 2))
    tp = 256
    prep_ins = [(q, rspec(tp, MH * (NOPE + ROPE))), (kv, rspec(tp, MH * 256)), (proj1, rspec(tp, 128, 26)),
                (cos2, rspec(tp, 128)), (sin_signed, rspec(tp, 128))] + [(a, cspec((1, 128))) for a in (qnn, knn, qrn, krn)]
    hq_spec = pl.BlockSpec((MH, tp, 256), lambda i: (0, i, 0))
    hv_spec = pl.BlockSpec((MH, tp, 128), lambda i: (0, i, 0))
    qf, kf, vh = stage("l1_mla_prep", fn_mla_prep, (L // tp,), prep_ins,
                       [(SDS((MH, L, 256), BF16), hq_spec), (SDS((MH, L, 256), BF16), hq_spec), (SDS((MH, L, 128), BF16), hv_spec)])
    attn, attn_lse = causal_attn(qf, kf, vh)
    mn1, kv1, kn1 = mem_branch_fwd("l1", mem, mem_norm[1:2], w1["w_mem_kv"], xk_norm[1:2])
    mo1 = mem_attn_fwd("l1", proj1, 5, kn1, kv1, xq_norm[1:2])
    o1 = row_fwd("l1_merge", fn_merge, L, ROW_TILE, [(attn, PW, 0), (mo1, XQW, 0), (proj1, BW, 0)], [], [(BW, BF16)])[0]
    x2 = matmul("l1_out", o1, w1["w_out"], "nn", F32, add=x1)
    dx2, loss = loss_and_grad(x2, target)
    around("after", "loss", loss)

    do1 = matmul("l1_out_dx", dx2, w1["w_out"], "nt")
    g_pair1 = matmul("l1_out_dw", o1, dx2, "tn", out=gw["w_out"])
    dattn, dmo1, dproj1 = row_bwd("l1_merge_bwd", fn_merge, L, ROW_TILE, [(attn, PW, 0), (mo1, XQW, 0), (proj1, BW, 0)], [],
                                  [(do1, BW, 0)], [True, True, {"cols": MLA_IN_P, "cb": 0, "dtype": BF16}], [])
    dproj1, dkn1, dv1, dxqn1 = mem_attn_bwd("l1", proj1, 5, kn1, kv1, xq_norm[1:2], dmo1, dproj1)
    env["g_pair1"], dmem_norm1, dxk1 = mem_branch_bwd("l1", mem, mem_norm[1:2], w1["w_mem_kv"], xk_norm[1:2], mn1, kv1, dkn1, dv1,
                                                      gw["w_mem_kv"], g_pair1)
    dqf, dkf, dvh = causal_attn_bwd(qf, kf, vh, attn, attn_lse, dattn)
    prep_diffs = [("row", SDS((L, MH * (NOPE + ROPE)), BF16), rspec(tp, MH * (NOPE + ROPE))), ("row", SDS((L, MH * 256), BF16), rspec(tp, MH * 256)),
                  ("row", SDS((L, MLA_IN_P), BF16), rspec(tp, 128, 26), {"into": dproj1}), None, None] + [("acc", (0,))] * 4
    dq, dkv, dproj1, dqnn, dknn, dqrn, dkrn = stage_bwd("l1_mla_prep_bwd", fn_mla_prep, (L // tp,), prep_ins,
                                                        [(dqf, hq_spec), (dkf, hq_spec), (dvh, hv_spec)], prep_diffs)
    dcqn = matmul("l1_uq_dx", dq, w_uq, "nn")
    env["dw_uq"] = matmul("l1_uq_dw", dq, cqn, "tn")
    dckvn = matmul("l1_ukv_dx", dkv, w1["mla_w_ukv"], "nt")
    env["g_ukv"] = matmul("l1_ukv_dw", ckvn, dkv, "tn", out=gw["mla_w_ukv"])
    dproj1, dqln = row_bwd("l1_q_lora_rms_bwd", fn_rms, L, ROW_TILE, [(proj1, QL, 4)], [qln], [(dcqn, QL, 0)],
                           [{"cols": MLA_IN_P, "cb": 4, "into": dproj1, "dtype": BF16}], [True])
    dproj1, dkvln = row_bwd("l1_kv_lora_rms_bwd", fn_rms, L, ROW_TILE, [(proj1, KVL, 12)], [kvln], [(dckvn, KVL, 0)],
                            [{"cols": MLA_IN_P, "cb": 12, "into": dproj1, "dtype": BF16}], [True])
    env["dw_in1"] = matmul("l1_in_dw", dproj1, h1, "tn")
    dh1 = matmul("l1_in_dx", dproj1, w_in1, "nn", plans=plans_for("l1_in_dx"))
    around("after", "l1_in_dx", dh1)
    dx1, dln1 = row_bwd("l1_rms_bwd", fn_rms, L, ROW_TILE, [(x1, D, 0)], [ln[1:2]], [(dh1, D, 0)], [{"add": (dx2, D, 0)}], [True])
    dx1p = time_permute(dx1)

    do0 = matmul("l0_out_dx", dx1p, w0["w_out"], "nt", plans=plans_for("l0_out_dx"))
    g_pair0 = matmul("l0_out_dw", o0, dx1p, "tn", out=gw["w_out"])
    dz0, dmo0, dproj0 = row_bwd("l0_merge_bwd", fn_merge_glu, L, ROW_TILE, [(z0, 2 * PW, 0), (mo0, XQW, 0), (proj0, BW, 1)], [],
                                [(do0, BW, 0)], [{"dtype": BF16}, True, {"cols": 2 * BW, "cb": 1, "dtype": BF16}], [])
    dproj0, dkn0, dv0, dxqn0 = mem_attn_bwd("l0", proj0, 3, kn0, kv0, xq_norm[0:1], dmo0, dproj0)
    env["g_pair0"], dmem_norm0, dxk0 = mem_branch_bwd("l0", mem, mem_norm[0:1], w0["w_mem_kv"], xk_norm[0:1], mn0, kv0, dkn0, dv0,
                                                      gw["w_mem_kv"], g_pair0)
    env["g_glu"] = matmul("l0_glu_dw", g0, dz0, "tn", out=gw["s5_w_glu"], plans=plans_for("l0_glu_dw"))
    dg0 = matmul("l0_glu_dx", dz0, w0["s5_w_glu"], "nt", plans=plans_for("l0_glu_dx"))
    around("before", "s5_backward", dg0)
    dproj0, dd, dwc_re, dwc_im, dwb_re, dwb_im, da_re, da_im = s5_backward(dg0, proj0, s_re, s_im, wb_re, wb_im, wc_re, wc_im,
                                                                           a_re_v, a_im_v, s5_d, dproj0, plans=plans_for("s5_backward"))
    around("after", "s5_backward", dd)
    env["g_in0"] = matmul("l0_in_dw", h0, dproj0, "tn", out=gw["s5_w_in"], plans=plans_for("l0_in_dw"))
    dh0 = matmul("l0_in_dx", dproj0, w_in0, "nt", plans=plans_for("l0_in_dx"))
    around("after", "l0_in_dx", dh0)
    dxp, dln0 = row_bwd("l0_rms_bwd", fn_rms, L, ROW_TILE, [(xp, D, 0)], [ln[0:1]], [(dh0, D, 0)], [{"add": (dx1p, D, 0)}], [True])
    grad_x = time_unpermute(dxp)

    db_re, db_im, dcoef_re, dcoef_im = row_bwd("s5_bmat_bwd", fn_s5_bmat, SN, BMAT_TILE, bmat_rows, [], [(dwb_re, 128, 0), (dwb_im, 128, 0)],
                                               [True] * 4, [], plans=plans_for("s5_bmat_bwd"))
    dc_re, dc_im = row_bwd("s5_cmat_bwd", fn_s5_cmat, PW, CMAT_TILE, cmat_rows, [], [(dwc_re, 512, 0), (dwc_im, 512, 0)], [True] * 2, [],
                           plans=plans_for("s5_cmat_bwd"))
    disc_cts = [(da_re.reshape(SG, SP), one), (da_im.reshape(SG, SP), one), (dcoef_re.reshape(SG, SP), one), (dcoef_im.reshape(SG, SP), one)]
    dlre, dlim, dls = stage_bwd("s5_disc_bwd", fn_s5_disc, (1,), disc_ins, disc_cts, [("acc", (0,))] * 3)

    g["ln_gain"] = jnp.concatenate([dln0, dln1], axis=0)
    g["mem_norm"] = jnp.concatenate([dmem_norm0, dmem_norm1], axis=0)
    g["xq_norm"] = jnp.concatenate([dxqn0, dxqn1], axis=0)
    g["xk_norm"] = jnp.concatenate([dxk0, dxk1], axis=0)
    g["s5_lambda_re"], g["s5_lambda_im"], g["s5_log_step"] = dlre, dlim, dls
    g["s5_b_re"], g["s5_b_im"], g["s5_c_re"], g["s5_c_im"] = db_re, db_im, dc_re, dc_im
    g["s5_d"] = dd
    g["mla_q_lora_norm"], g["mla_kv_lora_norm"] = dqln, dkvln
    g["mla_q_nope_norm"], g["mla_k_nope_norm"] = dqnn, dknn
    g["mla_q_rope_norm"] = dqrn[:, :ROPE] + dqrn[:, ROPE:]
    g["mla_k_rope_norm"] = dkrn[:, :ROPE] + dkrn[:, ROPE:]
    return loss, grad_x, g


def kernel(x, mem, positions, ln_gain, w_out, mem_norm, w_mem_kv, xq_norm, xk_norm, s5_w_in, s5_lambda_re, s5_lambda_im, s5_log_step, s5_b_re, s5_b_im, s5_c_re, s5_c_im, s5_d, s5_w_glu, mla_w_in, mla_q_lora_norm, mla_kv_lora_norm, mla_w_uq, mla_w_ukv, mla_q_nope_norm, mla_k_nope_norm, mla_q_rope_norm, mla_k_rope_norm, loss_target, m_ln_gain, m_w_out, m_mem_norm, m_w_mem_kv, m_xq_norm, m_xk_norm, m_s5_w_in, m_s5_lambda_re, m_s5_lambda_im, m_s5_log_step, m_s5_b_re, m_s5_b_im, m_s5_c_re, m_s5_c_im, m_s5_d, m_s5_w_glu, m_mla_w_in, m_mla_q_lora_norm, m_mla_kv_lora_norm, m_mla_w_uq, m_mla_w_ukv, m_mla_q_nope_norm, m_mla_k_nope_norm, m_mla_q_rope_norm, m_mla_k_rope_norm, v_ln_gain, v_w_out, v_mem_norm, v_w_mem_kv, v_xq_norm, v_xk_norm, v_s5_w_in, v_s5_lambda_re, v_s5_lambda_im, v_s5_log_step, v_s5_b_re, v_s5_b_im, v_s5_c_re, v_s5_c_im, v_s5_d, v_s5_w_glu, v_mla_w_in, v_mla_q_lora_norm, v_mla_kv_lora_norm, v_mla_w_uq, v_mla_w_ukv, v_mla_q_nope_norm, v_mla_k_nope_norm, v_mla_q_rope_norm, v_mla_k_rope_norm):
    args = dict(locals())
    wts = {n: args[n] for n in WEIGHT_ORDER}
    mom = {n: args["m_" + n] for n in WEIGHT_ORDER}
    var = {n: args["v_" + n] for n in WEIGHT_ORDER}

    chip = 2 * lax.axis_index("x") + lax.axis_index("y")
    place = jnp.stack([chip, lax.axis_index("c")]).astype(jnp.int32)

    def own_slot(gathered, shards):
        return [lax.dynamic_update_slice(g, s[None], (chip, 0, 0)) for g, s in zip(gathered, shards)]

    groups = list(stack_shards(wts, BF16))
    groups[2].append(jnp.concatenate([mla_q_lora_norm, jnp.pad(mla_kv_lora_norm, ((0, 0), (0, 64))), jnp.zeros((14, 128), F32)], axis=0))
    over_ici = [plan_gather_ici(shards) for shards in groups]
    _SCHEDULE_BEHIND.clear()
    schedule_behind(split_start("gather_start", over_ici))
    env, hooks, passed_on = {}, {}, {}

    def arrived(k, after, pass_now):
        passing = plan_gather_pass(split_wait(f"gather_wait_{k}", over_ici[k], after))
        passed_on[k] = passing
        return own_slot(run_plan(f"gather_pass_{k}", passing), groups[k]) if pass_now else None

    def need_in0(env, last):
        env["in0"], = arrived(0, last, True)

    def need_layer0(env, last):
        env["pair0"], env["glu"] = arrived(1, last, True)

    def need_layer1(env, last):
        arrived(2, last, False)

    def layer1_weights(env, last):
        pair1, ukv, in1, uq, norms = own_slot(passed_on[2].results, groups[2])
        env.update(pair1=pair1, ukv=ukv, w_in1=in1_rows_permute(in1[:, :MLA_IN // 4].reshape(MLA_IN, D)), w_uq=uq_rows_permute(uq.reshape(-1, QL)),
                   q_lora_norm=norms[:, 0, :], kv_lora_norm=norms[:, 1, :64])

    hooks["before", "l0_in"], hooks["before", "l0_glu"], hooks["before", "l0_out"] = need_in0, need_layer0, need_layer1
    hooks["plans", "l0_out"], hooks["after", "l0_out"] = (lambda env: [passed_on[2]]), layer1_weights

    rs = {}

    def swap(k, gs):
        rs[k, "g"], rs[k, "swap"] = gs, plan_pair_exchange(gs)
        return rs[k, "swap"]

    def start_scatter(k):
        rs[k, "pairs"] = pair_adds(f"rs{k}", rs[k, "g"], rs[k, "swap"].results, place)
        rs[k, "scatter"] = plan_chip_scatter(rs[k, "pairs"])
        schedule_behind(split_start(f"rs{k}_scatter_start", [rs[k, "scatter"]]))

    def join(k, after):
        rs[k, "join"] = plan_pair_join(chip_adds(f"rs{k}", rs[k, "pairs"], split_wait(f"rs{k}_scatter_wait", rs[k, "scatter"], after), place))
        return rs[k, "join"]

    def layer1_gradients(env):
        g_in1 = jnp.pad(in1_rows_unpermute(env["dw_in1"]).reshape(4, MLA_IN // 4, D), ((0, 0), (0, IN1T_ROWS - MLA_IN // 4), (0, 0)))
        return [swap(1, [env["g_pair1"], env["g_ukv"], g_in1, uq_rows_unpermute(env["dw_uq"]).reshape(4, -1, QL)])]

    hooks["plans", "l1_in_dx"] = layer1_gradients
    hooks["after", "l1_in_dx"] = lambda env, last: start_scatter(1)
    hooks["plans", "l0_glu_dx"] = lambda env: [swap(0, [env["g_pair0"], env["g_glu"]])]

    def before_s5_backward(env, last):
        start_scatter(0)
        env["join1"] = join(1, last)

    hooks["before", "s5_backward"] = before_s5_backward
    hooks["plans", "s5_backward"] = lambda env: [env["join1"]]
    hooks["after", "s5_backward"] = lambda env, last: env.update(join0=join(0, last))
    hooks["plans", "l0_in_dx"] = lambda env: [swap(2, [env["g_in0"]]), env["join0"]]
    hooks["after", "l0_in_dx"] = lambda env, last: start_scatter(2)

    def total_loss(env, local):
        env["loss"] = lax.psum(local[0, 0], MESH_AXES)
        schedule_behind(env["loss"].reshape(1, 1))

    hooks["after", "loss"] = total_loss
    small = {n: wts[n] for n, _ in SMALL}
    loss, grad_x, g = device_step(x[0], mem[0], positions[0], loss_target[0], small, env, hooks)
    loss = env["loss"]
    r_in0, = run_plan("rs2_pair_join", join(2, g["s5_log_step"]))
    (r_pair1, r_ukv, r_in1, r_uq), (r_pair0, r_glu) = (rs[k, "join"].results for k in (1, 0))

    small_flat = jnp.concatenate([g[n].reshape(-1) for n, _ in SMALL_FULL])
    g_small = jnp.pad(small_flat, (0, 4 * SMALL_ROWS * SMALL_LANES - N_SMALL)).astype(BF16).reshape(4, SMALL_ROWS, SMALL_LANES)
    small_all = all_reduce_in_vmem("small_grads_all_reduce", g_small).reshape(-1)[:N_SMALL]

    grads = {"w_out": jnp.stack([r_pair0[:PAIR_MKV], r_pair1[:PAIR_MKV]]), "w_mem_kv": jnp.stack([r_pair0[PAIR_MKV:], r_pair1[PAIR_MKV:]]),
             "s5_w_in": r_in0[None], "s5_w_glu": r_glu[None], "mla_w_ukv": r_ukv[None], "mla_w_in": r_in1[:MLA_IN // 4].T[None], "mla_w_uq": r_uq.T[None]}
    off = 0
    for n, s in SMALL_FULL:
        grads[n] = small_all[off:off + math.prod(s)].reshape(s)
        off += math.prod(s)
    for n, s in SHARDED_SMALL:
        grads[n] = lax.dynamic_slice(grads[n], (0, chip * s[1]), s)

    delta, new_m, new_v = {}, {}, {}
    for n, s in BIG + [(n, s) for n, s in SMALL if len(s) == 4]:
        perm = MINOR_LAST.get(n, tuple(range(len(s))))
        turned = tuple(s[p] for p in perm)
        view = lambda a: jnp.transpose(a, perm).reshape(-1, turned[-1])
        res = adamw("adamw_" + n, view(wts[n]), view(grads[n]), view(mom[n]), view(var[n]))
        delta[n], new_m[n], new_v[n] = (jnp.transpose(r.reshape(turned), tuple(perm.index(i) for i in range(len(s)))) for r in res)
    small_names = [n for n, s in SMALL if len(s) < 4] + [n for n, _ in SHARDED_SMALL]
    n_own = sum(wts[n].size for n in small_names)
    rows_own = -(-n_own // (8 * 128)) * 8

    def pack_small(d):
        flat = jnp.concatenate([d[n].reshape(-1) for n in small_names])
        return jnp.pad(flat, (0, rows_own * 128 - n_own), constant_values=1.0).reshape(rows_own, 128)

    res = adamw("adamw_small", pack_small(wts), pack_small(grads), pack_small(mom), pack_small(var))
    off = 0
    for n in small_names:
        size = wts[n].size
        delta[n], new_m[n], new_v[n] = (r.reshape(-1)[off:off + size].reshape(wts[n].shape) for r in res)
        off += size

    return (loss, grad_x[None], *[grads[n] for n in WEIGHT_ORDER], *[delta[n] for n in WEIGHT_ORDER],
            *[new_m[n] for n in WEIGHT_ORDER], *[new_v[n] for n in WEIGHT_ORDER])
```

```python
import functools
import math

import jax
import jax.numpy as jnp
from jax import lax
from jax.experimental import pallas as pl
from jax.experimental.pallas import tpu as pltpu

F32, BF16 = jnp.float32, jnp.bfloat16
SDS = jax.ShapeDtypeStruct

D = 1024
L = 2048
ML = 256
BW = 2 * D
XQW = BW // 4
PW = BW - XQW
XH, XHD = 4, 128
SG, SC, SP = 96, 16, 64
SN = SG * SP
NOPE, ROPE, VD = 128, 64, 128
MH = 12
QL, KVL = 512, 256
EPS = 1e-6
ROPE_THETA = 10000.0
MLA_IN = QL + KVL + ROPE + XQW + BW
MLA_IN_P = 3456
ADAM_LR, ADAM_B1, ADAM_B2, ADAM_EPS, ADAM_WD, ADAM_STEP = 0.001, 0.9, 0.999, 1e-08, 0.01, 10

VMEM_LIMIT = 48 * 2**20
ROW_TILE = 512
BMAT_TILE, CMAT_TILE = 2048, 512
SEG = 8
SEG_LEN = L // SEG
MESH_AXES = ("x", "y", "c")


def _cparams():
    return pltpu.CompilerParams(vmem_limit_bytes=VMEM_LIMIT)


def _dg(a, b, ca, cb):
    return lax.dot_general(a.astype(BF16), b.astype(BF16), (((ca,), (cb,)), ((), ())), preferred_element_type=F32)


@jax.custom_vjp
def mm_nn(a, b):
    return _dg(a, b, 1, 0)


mm_nn.defvjp(lambda a, b: (_dg(a, b, 1, 0), (a, b)), lambda res, g: (_dg(g, res[1], 1, 1), _dg(res[0], g, 0, 0)))


@jax.custom_vjp
def mm_nt(a, b):
    return _dg(a, b, 1, 1)


mm_nt.defvjp(lambda a, b: (_dg(a, b, 1, 1), (a, b)), lambda res, g: (_dg(g, res[1], 1, 0), _dg(g, res[0], 0, 0)))


@functools.partial(jax.custom_vjp, nondiff_argnums=(1,))
def lane_roll(x, shift):
    return pltpu.roll(x, shift, 1)


lane_roll.defvjp(lambda x, shift: (pltpu.roll(x, shift, 1), None),
                 lambda shift, _, g: (pltpu.roll(g, (128 - shift) % 128, 1),))


def rms(x, g):
    return x * lax.rsqrt(jnp.mean(x * x, axis=-1, keepdims=True) + EPS) * g


@jax.custom_vjp
def softmax_rows(s):
    e = jnp.exp(s - jnp.max(s, axis=-1, keepdims=True))
    return e / jnp.sum(e, axis=-1, keepdims=True)


def _softmax_rows_fwd(s):
    p = softmax_rows(s)
    return p, p


def _softmax_rows_bwd(p, g):
    return (p * (g - jnp.sum(g * p, axis=-1, keepdims=True)),)


softmax_rows.defvjp(_softmax_rows_fwd, _softmax_rows_bwd)


def silu(x):
    return x * jax.nn.sigmoid(x)


ANY = pl.BlockSpec(memory_space=pl.ANY)
MESH_ID = pl.DeviceIdType.MESH


def _dma_sems(n):
    return [pltpu.SemaphoreType.DMA((n,)), pltpu.SemaphoreType.DMA((n,))]


class Plan:
    def __init__(self, operands, out_shape, aliases, n_sems, copies):
        self.operands, self.out_shape, self.aliases, self.n_sems, self.copies = list(operands), list(out_shape), aliases, n_sems, copies
        self.results = None


_SCHEDULE_BEHIND = []


def schedule_behind(token):
    _SCHEDULE_BEHIND.append(token)


def hosted_call(kern, *, name, grid, in_specs, out_specs, out_shape, operands, scratch_shapes=(), aliases=None, cparams=None, plans=(), deps=()):
    n_in, n_out, n_scr = len(in_specs), len(out_specs), len(scratch_shapes)
    p_in, p_out = [len(p.operands) for p in plans], [len(p.out_shape) for p in plans]
    deps = tuple(deps) + tuple(_SCHEDULE_BEHIND)
    _SCHEDULE_BEHIND.clear()
    all_aliases = dict(aliases or {})
    in_off, out_off = n_in, n_out
    for p, ni, no in zip(plans, p_in, p_out):
        all_aliases.update({in_off + i: out_off + o for i, o in p.aliases.items()})
        in_off, out_off = in_off + ni, out_off + no

    def body(*refs):
        pos, pins, pouts = n_in, [], []
        for ni in p_in:
            pins.append(refs[pos:pos + ni])
            pos += ni
        pos += len(deps)
        main_out = refs[pos:pos + n_out]
        pos += n_out
        for no in p_out:
            pouts.append(refs[pos:pos + no])
            pos += no
        main_scr = refs[pos:pos + n_scr]
        pos += n_scr
        if plans:
            ids = [pl.program_id(ax) for ax in range(len(grid))]
            first = functools.reduce(jnp.logical_and, [i == 0 for i in ids])
            last = functools.reduce(jnp.logical_and, [i == g - 1 for i, g in zip(ids, grid)])
            copies = [p.copies(pins[k], pouts[k], refs[pos + 2 * k], refs[pos + 2 * k + 1]) for k, p in enumerate(plans)]

            @pl.when(first)
            def _():
                for sends, _ in copies:
                    for cp in sends:
                        cp.start()

        kern(*refs[:n_in], *main_out, *main_scr)
        if plans:
            @pl.when(last)
            def _():
                for sends, recvs in copies:
                    for cp in recvs:
                        cp.wait_recv()
                    for cp in sends:
                        cp.wait_send()

    res = pl.pallas_call(body, grid=grid, in_specs=list(in_specs) + [ANY] * (sum(p_in) + len(deps)),
                         out_specs=list(out_specs) + [ANY] * sum(p_out), out_shape=list(out_shape) + [s for p in plans for s in p.out_shape],
                         scratch_shapes=list(scratch_shapes) + [s for p in plans for s in _dma_sems(p.n_sems)],
                         input_output_aliases=all_aliases, name=name, compiler_params=cparams or _cparams())(
        *operands, *[a for p in plans for a in p.operands], *deps)
    pos = n_out
    for p, no in zip(plans, p_out):
        p.results = list(res[pos:pos + no])
        pos += no
    return list(res[:n_out])


def _wide(v):
    return v.astype(F32) if v.dtype == BF16 else v


def stage(name, fn, grid, ins, outs):
    n_in = len(ins)

    def kern(*refs):
        res = fn(*[_wide(r[...]) for r in refs[:n_in]])
        for r, v in zip(refs[n_in:], res):
            r[...] = v.astype(r.dtype)

    return hosted_call(kern, name=name, grid=grid, in_specs=[s for _, s in ins], out_specs=[s for _, s in outs],
                       out_shape=[sd for sd, _ in outs], operands=[a for a, _ in ins])


def stage_bwd(name, fn, grid, ins, cts, diffs, plans=()):
    n_in, n_ct = len(ins), len(cts)
    didx = [i for i, d in enumerate(diffs) if d is not None]
    opts = {i: (diffs[i][3] if len(diffs[i]) > 3 else {}) for i in didx if diffs[i][0] == "row"}
    adds = [(i, opts[i]["add"]) for i in opts if "add" in opts[i]]
    intos = [(i, opts[i]["into"]) for i in opts if "into" in opts[i]]
    n_add, n_into = len(adds), len(intos)
    add_pos = {i: n_in + n_ct + k for k, (i, _) in enumerate(adds)}
    n_extra = n_in + n_ct + n_add + n_into

    def kern(*refs):
        vals = [_wide(r[...]) for r in refs[:n_in]]

        def f(*dv):
            full = list(vals)
            for i, v in zip(didx, dv):
                full[i] = v
            return fn(*full)

        _, vjp = jax.vjp(f, *[vals[i].astype(F32) for i in didx])
        gs = vjp(tuple(c[...].astype(F32) for c in refs[n_in:n_in + n_ct]))
        for o_ref, i, g in zip(refs[n_extra:], didx, gs):
            if diffs[i][0] == "row":
                if i in add_pos:
                    g = g + refs[add_pos[i]][...].astype(F32)
                o_ref[...] = g.astype(o_ref.dtype)
            else:
                first = functools.reduce(jnp.logical_and, [pl.program_id(ax) == 0 for ax in diffs[i][1]])

                @pl.when(first)
                def _():
                    o_ref[...] = g

                @pl.when(jnp.logical_not(first))
                def _():
                    o_ref[...] += g

    out_shape, out_specs = [], []
    for i in didx:
        if diffs[i][0] == "row":
            out_shape.append(diffs[i][1])
            out_specs.append(diffs[i][2])
        else:
            out_shape.append(SDS(ins[i][0].shape, F32))
            out_specs.append(ins[i][1])
    aliases = {n_in + n_ct + n_add + k: didx.index(i) for k, (i, _) in enumerate(intos)}
    in_specs = [s for _, s in ins] + [s for _, s in cts] + [s for _, (_, s) in adds] + [ANY] * n_into
    operands = [a for a, _ in ins] + [a for a, _ in cts] + [a for _, (a, _) in adds] + [a for _, a in intos]
    return hosted_call(kern, name=name, grid=grid, in_specs=in_specs, out_specs=out_specs, out_shape=out_shape, operands=operands,
                       aliases=aliases, plans=plans)


def rspec(tl, w, cb=0):
    return pl.BlockSpec((tl, w), lambda i: (i, cb))


def cspec(shape):
    return pl.BlockSpec(shape, lambda i: (0,) * len(shape))


def row_fwd(name, fn, rows, tl, row_ins, consts, outs):
    ins = [(a, rspec(tl, w, cb)) for a, w, cb in row_ins] + [(a, cspec(a.shape)) for a in consts]
    return stage(name, fn, (rows // tl,), ins, [(SDS((rows, w), dt), rspec(tl, w)) for w, dt in outs])


def row_bwd(name, fn, rows, tl, row_ins, consts, cts, row_diff, const_diff, plans=()):
    ins = [(a, rspec(tl, w, cb)) for a, w, cb in row_ins] + [(a, cspec(a.shape)) for a in consts]
    diffs = []
    for (a, w, cb), d in zip(row_ins, row_diff):
        if not d:
            diffs.append(None)
            continue
        d = d if isinstance(d, dict) else {}
        opts = {}
        if "add" in d:
            opts["add"] = (d["add"][0], rspec(tl, d["add"][1], d["add"][2]))
        if d.get("into") is not None:
            opts["into"] = d["into"]
        diffs.append(("row", SDS((rows, d.get("cols", w)), d.get("dtype", F32)), rspec(tl, w, d.get("cb", 0)), opts))
    diffs += [("acc", (0,)) if d else None for d in const_diff]
    return stage_bwd(name, fn, (rows // tl,), ins, [(a, rspec(tl, w, cb)) for a, w, cb in cts], diffs, plans=plans)


MATMUL_VMEM = 36 * 2**20
ADAMW_VMEM = 28 * 2**20


class Sharded:
    def __init__(self, arr, kind, roff, rows):
        self.arr, self.kind, self.roff, self.rows, self.n = arr, kind, roff, rows, arr.shape[2]
        self.shape = (rows, 4 * self.n) if kind == "col" else (4 * rows, self.n)

    def fits(self, t0, t1):
        return self.roff % t0 == 0 and self.rows % t0 == 0 and self.n % t1 == 0

    def spec(self, t0, t1, bidx):
        assert self.fits(t0, t1), (self.kind, self.roff, self.rows, self.n, t0, t1)
        r0 = self.roff // t0
        if self.kind == "col":
            per = self.n // t1
            return pl.BlockSpec((None, t0, t1), lambda *g: (bidx(*g)[1] // per, r0 + bidx(*g)[0], bidx(*g)[1] % per))
        per = self.rows // t0
        return pl.BlockSpec((None, t0, t1), lambda *g: (bidx(*g)[0] // per, r0 + bidx(*g)[0] % per, bidx(*g)[1]))


def _matmul_over_shards(name, a, b, mode, out_dtype, add, plans):
    m, k_dim = a.shape
    per = k_dim // 4
    n = b.n if mode == "nn" else b.rows
    assert k_dim % 4 == 0 and (b.rows == per and b.roff % per == 0 if mode == "nn" else b.n == per), (name, a.shape, mode)
    a_bytes, o_bytes = jnp.dtype(a.dtype).itemsize, jnp.dtype(out_dtype).itemsize

    def vmem(tm, tn):
        return 2 * (tm * k_dim * a_bytes + k_dim * tn * 2 + tm * tn * (o_bytes + (4 if add is not None else 0))) + 4 * tm * tn

    tiles = [(tm, tn) for tm in (2048, 1024, 512, 256) for tn in (1024, 512, 256, 128)
             if m % tm == 0 and n % tn == 0 and (mode == "nn" or b.roff % tn == 0) and vmem(tm, tn) <= MATMUL_VMEM]
    tm, tn = max(tiles, key=lambda t: (t[0] * t[1], t[0]))
    if mode == "nn":
        b_specs = [pl.BlockSpec((None, per, tn), lambda i, j, s=s: (s, b.roff // per, j)) for s in range(4)]
    else:
        b_specs = [pl.BlockSpec((None, tn, per), lambda i, j, s=s: (s, b.roff // tn + j, 0)) for s in range(4)]
    o_spec = pl.BlockSpec((tm, tn), lambda i, j: (i, j))
    cb = 0 if mode == "nn" else 1

    def kern(*refs):
        r = _dg(refs[0][:, 0:per], refs[1][...], 1, cb)
        for s in range(1, 4):
            r = r + _dg(refs[0][:, s * per:(s + 1) * per], refs[1 + s][...], 1, cb)
        if add is not None:
            r = r + refs[5][...]
        refs[-1][...] = r.astype(refs[-1].dtype)

    ins, specs = [a] + [b.arr] * 4, [pl.BlockSpec((tm, k_dim), lambda i, j: (i, 0))] + b_specs
    if add is not None:
        ins.append(add)
        specs.append(o_spec)
    return hosted_call(kern, name=name, grid=(m // tm, n // tn), in_specs=specs, out_specs=[o_spec], out_shape=[SDS((m, n), out_dtype)],
                       operands=ins, plans=plans)[0]


def matmul(name, a, b, mode, out_dtype=BF16, add=None, out=None, into=None, plans=()):
    if mode == "tn":
        k_dim, m = a.shape
    else:
        m, k_dim = a.shape
    n = b.shape[0] if mode == "nt" else b.shape[1]
    if isinstance(b, Sharded) and out is None and (mode, b.kind) in (("nn", "row"), ("nt", "col")):
        return _matmul_over_shards(name, a, b, mode, out_dtype, add, plans)
    b_fit = b.fits if isinstance(b, Sharded) else (lambda t0, t1: True)
    o_fit = out.fits if out is not None else (lambda t0, t1: True)
    a_bytes, b_bytes = jnp.dtype(a.dtype).itemsize, jnp.dtype(b.arr.dtype if isinstance(b, Sharded) else b.dtype).itemsize
    o_bytes = jnp.dtype(out_dtype if out is None else out.arr.dtype).itemsize

    def vmem(tm, tn):
        return 2 * (tm * k_dim * a_bytes + k_dim * tn * b_bytes + tm * tn * (o_bytes + (4 if add is not None else 0))) + 4 * tm * tn

    tiles = [(tm, tn) for tm in (2048, 1024, 512, 256, 128) for tn in (1024, 768, 512, 384, 256, 128)
             if m % tm == 0 and n % tn == 0 and (b_fit(tn, k_dim) if mode == "nt" else b_fit(k_dim, tn)) and o_fit(tm, tn)
             and vmem(tm, tn) <= MATMUL_VMEM]
    tm, tn = max(tiles, key=lambda t: t[0] * t[1])
    a_spec = pl.BlockSpec((k_dim, tm), lambda i, j: (0, i)) if mode == "tn" else pl.BlockSpec((tm, k_dim), lambda i, j: (i, 0))
    if isinstance(b, Sharded):
        b_spec = b.spec(tn, k_dim, lambda i, j: (j, 0)) if mode == "nt" else b.spec(k_dim, tn, lambda i, j: (0, j))
        b = b.arr
    else:
        b_spec = pl.BlockSpec((tn, k_dim), lambda i, j: (j, 0)) if mode == "nt" else pl.BlockSpec((k_dim, tn), lambda i, j: (0, j))
    o_spec = pl.BlockSpec((tm, tn), lambda i, j: (i, j))
    out_spec, out_shape = (o_spec, SDS((m, n), out_dtype)) if out is None else (out.spec(tm, tn, lambda i, j: (i, j)), out.arr)
    ca, cb = {"nn": (1, 0), "nt": (1, 1), "tn": (0, 0)}[mode]
    n_in = 2 + (add is not None)

    def kern(*refs):
        r = _dg(refs[0][...], refs[1][...], ca, cb)
        if add is not None:
            r = r + refs[2][...]
        refs[-1][...] = r.astype(refs[-1].dtype)

    ins, specs = [a, b], [a_spec, b_spec]
    if add is not None:
        ins.append(add)
        specs.append(o_spec)
    if into is not None:
        ins.append(into)
        specs.append(ANY)
    return hosted_call(kern, name=name, grid=(m // tm, n // tn), in_specs=specs, out_specs=[out_spec], out_shape=[out_shape],
                       operands=ins, aliases={} if into is None else {n_in: 0}, plans=plans)[0]


SCAN_UNROLL = 8


def _cmul(ar, ai, br, bi):
    return ar * br - ai * bi, ar * bi + ai * br


def _sub_shift(x, down):
    row = lax.broadcasted_iota(jnp.int32, x.shape, 0)
    if down:
        return jnp.where(row == 0, 0.0, pltpu.roll(x, 1, 0))
    return jnp.where(row == SEG - 1, 0.0, pltpu.roll(x, SEG - 1, 0))


def _pow_seg_len(ar, ai):
    for _ in range(int(math.log2(SEG_LEN))):
        ar, ai = _cmul(ar, ai, ar, ai)
    return ar, ai


def _scan_in_place(sr, si, a_re, a_im):
    lanes = sr.shape[1]
    ar = jnp.broadcast_to(a_re, (SEG, lanes))
    ai = jnp.broadcast_to(a_im, (SEG, lanes))
    zero = jnp.zeros((SEG, lanes), F32)

    def local(i, carry):
        rows = pl.ds(pl.multiple_of(i * SEG, SEG), SEG)
        mr, mi = _cmul(ar, ai, carry[0], carry[1])
        nr, ni = mr + sr[rows, :], mi + si[rows, :]
        sr[rows, :] = nr
        si[rows, :] = ni
        return nr, ni

    fr, fi = lax.fori_loop(0, SEG_LEN, local, (zero, zero), unroll=SCAN_UNROLL)
    pr, pi = _pow_seg_len(ar, ai)
    ir, ii = zero, zero
    for _ in range(SEG - 1):
        mr, mi = _cmul(pr, pi, ir, ii)
        ir, ii = _sub_shift(mr + fr, True), _sub_shift(mi + fi, True)

    def carry_in(i, pw):
        rows = pl.ds(pl.multiple_of(i * SEG, SEG), SEG)
        cr, ci = _cmul(pw[0], pw[1], ir, ii)
        sr[rows, :] += cr
        si[rows, :] += ci
        return _cmul(pw[0], pw[1], ar, ai)

    lax.fori_loop(0, SEG_LEN, carry_in, (ar, ai), unroll=SCAN_UNROLL)


S5_LANES = 8 * SP
S5_BLOCKS = SN // S5_LANES


def _s5_specs(n=1):
    u_spec = pl.BlockSpec((L, n * 8 * SC), lambda j: (0, j))
    s_spec = pl.BlockSpec((L, n * S5_LANES), lambda j: (0, j))
    wb_spec = pl.BlockSpec((n * S5_LANES, 8 * SC), lambda j: (j, 0))
    wc_spec = pl.BlockSpec((n * 8 * SC, S5_LANES), lambda j: (j, 0))
    a_spec = pl.BlockSpec((1, n * S5_LANES), lambda j: (0, j))
    d_spec = pl.BlockSpec((1, n * 8 * SC), lambda j: (0, j))
    return u_spec, s_spec, wb_spec, wc_spec, a_spec, d_spec


S5_FWD_BLOCKS = 2


def s5_forward(proj, wb_re, wb_im, wc_re, wc_im, a_re, a_im, d, plans=()):
    n = S5_FWD_BLOCKS
    cols, lanes = 8 * SC, S5_LANES

    def kern(u_ref, wbr, wbi, wcr, wci, ar, ai, d_ref, sr_out, si_out, g_ref, sr, si):
        u = _wide(u_ref[...])
        for b in range(n):
            sr[:, b * lanes:(b + 1) * lanes], si[:, b * lanes:(b + 1) * lanes] = fn_s5_bu(
                u[:, b * cols:(b + 1) * cols], wbr[b * lanes:(b + 1) * lanes, :], wbi[b * lanes:(b + 1) * lanes, :])
        _scan_in_place(sr, si, ar[...], ai[...])
        for b in range(n):
            g_ref[:, b * cols:(b + 1) * cols] = fn_s5_out(
                sr[:, b * lanes:(b + 1) * lanes], si[:, b * lanes:(b + 1) * lanes], u[:, b * cols:(b + 1) * cols],
                d_ref[:, b * cols:(b + 1) * cols], wcr[b * cols:(b + 1) * cols, :], wci[b * cols:(b + 1) * cols, :])[0].astype(g_ref.dtype)
        sr_out[...] = sr[...].astype(sr_out.dtype)
        si_out[...] = si[...].astype(si_out.dtype)

    u_spec, s_spec, wb_spec, wc_spec, a_spec, d_spec = _s5_specs(n)
    return hosted_call(kern, name="s5_forward", grid=(S5_BLOCKS // n,), in_specs=[u_spec, wb_spec, wb_spec, wc_spec, wc_spec, a_spec, a_spec, d_spec],
                       out_specs=[s_spec, s_spec, u_spec], out_shape=[SDS((L, SN), BF16)] * 2 + [SDS((L, PW), BF16)],
                       operands=[proj, wb_re, wb_im, wc_re, wc_im, a_re, a_im, d], scratch_shapes=[pltpu.VMEM((L, n * lanes), F32)] * 2,
                       plans=plans)


def _adjoint_scan_in_place(lr, li, sr, si, a_re, a_im):
    lanes = lr.shape[1]
    ar = jnp.broadcast_to(a_re, (SEG, lanes))
    ai = -jnp.broadcast_to(a_im, (SEG, lanes))
    zero = jnp.zeros((SEG, lanes), F32)

    def local(k, carry):
        i = SEG_LEN - 1 - k
        rows = pl.ds(pl.multiple_of(i * SEG, SEG), SEG)
        mr, mi = _cmul(ar, ai, carry[0], carry[1])
        nr, ni = mr + lr[rows, :], mi + li[rows, :]
        lr[rows, :] = nr
        li[rows, :] = ni
        return nr, ni

    fr, fi = lax.fori_loop(0, SEG_LEN, local, (zero, zero), unroll=SCAN_UNROLL)
    pr, pi = _pow_seg_len(ar, ai)
    ir, ii = zero, zero
    for _ in range(SEG - 1):
        mr, mi = _cmul(pr, pi, ir, ii)
        ir, ii = _sub_shift(mr + fr, False), _sub_shift(mi + fi, False)

    def fix(rows, pw):
        cr, ci = _cmul(pw[0], pw[1], ir, ii)
        tr, ti = lr[rows, :] + cr, li[rows, :] + ci
        lr[rows, :] = tr
        li[rows, :] = ti
        return tr, ti

    def grad_a(tr, ti, spr, spi, acc):
        return acc[0] + tr * spr + ti * spi, acc[1] + ti * spr - tr * spi

    def carry_in(k, c):
        i = SEG_LEN - 1 - k
        rows = pl.ds(pl.multiple_of(i * SEG, SEG), SEG)
        prev = pl.ds(pl.multiple_of((i - 1) * SEG, SEG), SEG)
        tr, ti = fix(rows, (c[0], c[1]))
        acc = grad_a(tr, ti, sr[prev, :], si[prev, :], (c[2], c[3]))
        nr, ni = _cmul(c[0], c[1], ar, ai)
        return nr, ni, acc[0], acc[1]

    pwr, pwi, accr, acci = lax.fori_loop(0, SEG_LEN - 1, carry_in, (ar, ai, zero, zero), unroll=5)
    tr, ti = fix(pl.ds(0, SEG), (pwr, pwi))
    last = pl.ds((SEG_LEN - 1) * SEG, SEG)
    accr, acci = grad_a(tr, ti, _sub_shift(sr[last, :], True), _sub_shift(si[last, :], True), (accr, acci))
    return jnp.sum(accr, axis=0, keepdims=True), jnp.sum(acci, axis=0, keepdims=True)


S5_BWD_VMEM = 58 * 2**20


def s5_backward(dg, proj, s_re, s_im, wb_re, wb_im, wc_re, wc_im, a_re, a_im, d, dproj, plans=()):
    def kern(dg_ref, u_ref, sr_in, si_in, wbr, wbi, wcr, wci, ar, ai, d_ref, _, du_ref, dd_ref, dwcr, dwci, dwbr, dwbi, dar, dai, lr, li, sr, si):
        u = _wide(u_ref[...])
        sr[...], si[...] = _wide(sr_in[...]), _wide(si_in[...])
        _, vjp_out = jax.vjp(fn_s5_out, sr[...], si[...], u, d_ref[...], wcr[...], wci[...])
        lr[...], li[...], du_out, dd_ref[...], dwcr[...], dwci[...] = vjp_out((_wide(dg_ref[...]),))
        dar[...], dai[...] = _adjoint_scan_in_place(lr, li, sr, si, ar[...], ai[...])
        _, vjp_in = jax.vjp(fn_s5_bu, u, wbr[...], wbi[...])
        du_in, dwbr[...], dwbi[...] = vjp_in((lr[...], li[...]))
        du_ref[...] = (du_out + du_in).astype(du_ref.dtype)

    u_spec, s_spec, wb_spec, wc_spec, a_spec, d_spec = _s5_specs()
    outs = [(SDS(dproj.shape, dproj.dtype), u_spec), (SDS(d.shape, F32), d_spec), (SDS(wc_re.shape, F32), wc_spec), (SDS(wc_im.shape, F32), wc_spec),
            (SDS(wb_re.shape, F32), wb_spec), (SDS(wb_im.shape, F32), wb_spec), (SDS(a_re.shape, F32), a_spec), (SDS(a_im.shape, F32), a_spec)]
    return hosted_call(kern, name="s5_backward", grid=(S5_BLOCKS,),
                       in_specs=[u_spec, u_spec, s_spec, s_spec, wb_spec, wb_spec, wc_spec, wc_spec, a_spec, a_spec, d_spec, ANY],
                       out_specs=[sp for _, sp in outs], out_shape=[sd for sd, _ in outs], aliases={11: 0},
                       operands=[dg, proj, s_re, s_im, wb_re, wb_im, wc_re, wc_im, a_re, a_im, d, dproj],
                       scratch_shapes=[pltpu.VMEM((L, S5_LANES), F32)] * 4,
                       cparams=pltpu.CompilerParams(vmem_limit_bytes=S5_BWD_VMEM), plans=plans)


def fn_rms(x, g):
    return (rms(x, g),)


def fn_s5_disc(lre, lim, ls):
    step = jnp.exp(ls)
    e = jnp.exp(lre * step)
    a_re, a_im = e * jnp.cos(lim * step), e * jnp.sin(lim * step)
    den = lre * lre + lim * lim
    nr, ni = a_re - 1.0, a_im
    return a_re, a_im, (nr * lre + ni * lim) / den, (ni * lre - nr * lim) / den


def _group_mask(rows, cols, row_div, col_div):
    r = lax.broadcasted_iota(jnp.int32, (rows, cols), 0) // row_div % 8
    c = lax.broadcasted_iota(jnp.int32, (rows, cols), 1) // col_div
    return r == c


def _spread(x, mask):
    w = x.shape[1]
    copy = (lax.broadcasted_iota(jnp.int32, (w, 8 * w), 1) % w == lax.broadcasted_iota(jnp.int32, (w, 8 * w), 0)).astype(F32)
    return jnp.where(mask, jnp.dot(x, copy, precision=lax.Precision.HIGHEST, preferred_element_type=F32), 0.0)


def fn_s5_bmat(b_re, b_im, coef_re, coef_im):
    mask = _group_mask(b_re.shape[0], 8 * SC, SP, SC)
    return _spread(coef_re * b_re - coef_im * b_im, mask), _spread(coef_re * b_im + coef_im * b_re, mask)


def fn_s5_cmat(c_re, c_im):
    mask = _group_mask(c_re.shape[0], 8 * SP, SC, SP)
    return _spread(c_re, mask), _spread(c_im, mask)


def fn_s5_bu(u, wb_re, wb_im):
    return mm_nt(u, wb_re), mm_nt(u, wb_im)


def fn_s5_out(sr, si, u, d, wc_re, wc_im):
    y = mm_nt(sr, wc_re) - mm_nt(si, wc_im) + d * u
    return (jax.nn.gelu(y),)


def fn_merge_glu(z, mo, gate):
    yg = z[:, :PW] * jax.nn.sigmoid(z[:, PW:])
    return (jnp.concatenate([yg, mo], axis=1) * silu(gate),)


def fn_merge(prim, mo, gate):
    return (jnp.concatenate([prim, mo], axis=1) * silu(gate),)


def fn_mem_k(kv, g):
    return (jnp.concatenate([rms(kv[:, h * XHD:(h + 1) * XHD], g) for h in range(XH)], axis=1),)


def fn_mem_attn(xq, kn, v, g):
    outs = []
    for h in range(XH):
        sl = slice(h * XHD, (h + 1) * XHD)
        p = softmax_rows(mm_nt(rms(xq[:, sl], g), kn[:, sl]) * (XHD ** -0.5))
        outs.append(mm_nn(p, v[:, sl]))
    return (jnp.concatenate(outs, axis=1),)


def _half_rms(x, g):
    lo = lax.broadcasted_iota(jnp.int32, x.shape, 1) < ROPE
    x2 = x * x
    s_lo = jnp.sum(jnp.where(lo, x2, 0.0), axis=1, keepdims=True)
    s_hi = jnp.sum(jnp.where(lo, 0.0, x2), axis=1, keepdims=True)
    return x * lax.rsqrt(jnp.where(lo, s_lo, s_hi) / ROPE + EPS) * g


def _rope(x, cos2, sin_signed):
    first = lax.broadcasted_iota(jnp.int32, x.shape, 1) % ROPE < ROPE // 2
    return x * cos2 + jnp.where(first, lane_roll(x, 128 - ROPE // 2), lane_roll(x, ROPE // 2)) * sin_signed


def fn_mla_prep(q, kv, kr, cos2, sin_signed, qnn, knn, qrn, krn):
    lo = lax.broadcasted_iota(jnp.int32, kr.shape, 1) < ROPE
    kr_pad = jnp.where(lo, _rope(_half_rms(kr, krn), cos2, sin_signed), 0.0)
    qf, kf, vs = [], [], []
    for m in range(MH // 2):
        pair = _rope(_half_rms(q[:, MH * NOPE + 128 * m:MH * NOPE + 128 * (m + 1)], qrn), cos2, sin_signed)
        for h, rope_h in ((2 * m, pair), (2 * m + 1, lane_roll(pair, ROPE))):
            qf.append(jnp.concatenate([rms(q[:, NOPE * h:NOPE * (h + 1)], qnn), jnp.where(lo, rope_h, 0.0)], axis=1))
    for h in range(MH):
        kf.append(jnp.concatenate([rms(kv[:, 256 * h:256 * h + NOPE], knn), kr_pad], axis=1))
        vs.append(kv[:, 256 * h + NOPE:256 * (h + 1)])
    return jnp.stack(qf), jnp.stack(kf), jnp.stack(vs)


ATT_TQ = 512


def _attn_scores(q, kf):
    tq = q.shape[0]
    scale = (NOPE + ROPE) ** -0.5
    own = _dg(q, kf[-tq:], 1, 1) * scale
    own = jnp.where(lax.broadcasted_iota(jnp.int32, own.shape, 1) <= lax.broadcasted_iota(jnp.int32, own.shape, 0), own, jnp.finfo(F32).min)
    return own if kf.shape[0] == tq else jnp.concatenate([_dg(q, kf[:-tq], 1, 1) * scale, own], axis=1)


ATT_FWD_HEADS, ATT_BWD_HEADS = 4, 2


def _attn_specs(heads):
    q_spec = pl.BlockSpec((heads, ATT_TQ, 256), lambda h, i: (h, i, 0))
    k_spec = pl.BlockSpec((heads, L, 256), lambda h, i: (h, 0, 0))
    v_spec = pl.BlockSpec((heads, L, 128), lambda h, i: (h, 0, 0))
    o_spec = pl.BlockSpec((ATT_TQ, heads * 128), lambda h, i: (i, h))
    lse_spec = pl.BlockSpec((heads, ATT_TQ, 1), lambda h, i: (h, i, 0))
    return q_spec, k_spec, v_spec, o_spec, lse_spec


def _attn_branches(heads, body):
    i = pl.program_id(1)
    for t in range(L // ATT_TQ):
        @pl.when(i == t)
        def _(t=t):
            for hh in range(heads):
                body(hh, slice(hh * 128, (hh + 1) * 128), (t + 1) * ATT_TQ)


def causal_attn(qf, kf, vh):
    def kern(q_ref, k_ref, v_ref, o_ref, lse_ref):
        def body(hh, lanes, keys):
            s = _attn_scores(q_ref[hh], k_ref[hh, :keys, :])
            m = jnp.max(s, axis=-1, keepdims=True)
            e = jnp.exp(s - m)
            total = jnp.sum(e, axis=-1, keepdims=True)
            o_ref[:, lanes] = (_dg(e, v_ref[hh, :keys, :], 1, 0) / total).astype(o_ref.dtype)
            lse_ref[hh] = m + jnp.log(total)

        _attn_branches(ATT_FWD_HEADS, body)

    q_spec, k_spec, v_spec, o_spec, lse_spec = _attn_specs(ATT_FWD_HEADS)
    return pl.pallas_call(kern, grid=(MH // ATT_FWD_HEADS, L // ATT_TQ), in_specs=[q_spec, k_spec, v_spec], out_specs=[o_spec, lse_spec],
                          out_shape=[SDS((L, MH * VD), F32), SDS((MH, L, 1), F32)], name="l1_attn", compiler_params=_cparams())(qf, kf, vh)


def causal_attn_bwd(qf, kf, vh, out, lse, dout):
    scale = (NOPE + ROPE) ** -0.5

    def kern(q_ref, k_ref, v_ref, o_ref, lse_ref, do_ref, dq_ref, dk_ref, dv_ref):
        @pl.when(pl.program_id(1) == 0)
        def _():
            dk_ref[...] = jnp.zeros_like(dk_ref)
            dv_ref[...] = jnp.zeros_like(dv_ref)

        def body(hh, lanes, keys):
            q, k, v, do = q_ref[hh], k_ref[hh, :keys, :], v_ref[hh, :keys, :], do_ref[:, lanes]
            p = jnp.exp(_attn_scores(q, k) - lse_ref[hh])
            delta = jnp.sum(do * _wide(o_ref[:, lanes]), axis=-1, keepdims=True)
            dv_ref[hh, :keys, :] += _dg(p, do, 0, 0)
            ds = p * (_dg(do, v, 1, 1) - delta) * scale
            dq_ref[hh] = _dg(ds, k, 1, 0)
            dk_ref[hh, :keys, :] += _dg(ds, q, 0, 0)

        _attn_branches(ATT_BWD_HEADS, body)

    q_spec, k_spec, v_spec, o_spec, lse_spec = _attn_specs(ATT_BWD_HEADS)
    return pl.pallas_call(kern, grid=(MH // ATT_BWD_HEADS, L // ATT_TQ), in_specs=[q_spec, k_spec, v_spec, o_spec, lse_spec, o_spec],
                          out_specs=[q_spec, k_spec, v_spec], out_shape=[SDS(qf.shape, F32), SDS(kf.shape, F32), SDS(vh.shape, F32)],
                          name="l1_attn_bwd", compiler_params=_cparams())(qf, kf, vh, out, lse, dout)


def loss_and_grad(y, target, tl=512):
    def kern(y_ref, t_ref, dy_ref, loss_ref):
        d = y_ref[...] - t_ref[...]
        dy_ref[...] = d / D

        @pl.when(pl.program_id(0) == 0)
        def _():
            loss_ref[...] = jnp.zeros_like(loss_ref)

        loss_ref[...] += 0.5 * jnp.sum(jnp.sum(d * d, axis=1, keepdims=True), axis=0, keepdims=True) / D

    return pl.pallas_call(kern, grid=(L // tl,), in_specs=[rspec(tl, D), rspec(tl, D)], out_specs=[rspec(tl, D), cspec((1, 1))],
                          out_shape=[SDS((L, D), F32), SDS((1, 1), F32)], name="loss", compiler_params=_cparams())(y, target)


def adamw(name, w, g, m, v):
    rows, cols = w.shape
    block_row_bytes = 7 * 2 * 4 * max(cols, 128)
    tr = _row_tile(rows, min(2048, ADAMW_VMEM // block_row_bytes // 8 * 8), 8)

    def kern(w_ref, g_ref, m_ref, v_ref, d_ref, nm_ref, nv_ref):
        gg = g_ref[...]
        nm = ADAM_B1 * m_ref[...] + (1.0 - ADAM_B1) * gg
        nv = ADAM_B2 * v_ref[...] + (1.0 - ADAM_B2) * jnp.square(gg)
        m_hat = nm / (1.0 - ADAM_B1 ** ADAM_STEP)
        v_hat = nv / (1.0 - ADAM_B2 ** ADAM_STEP)
        d_ref[...] = -ADAM_LR * (m_hat / (jnp.sqrt(v_hat) + ADAM_EPS) + ADAM_WD * w_ref[...])
        nm_ref[...] = nm
        nv_ref[...] = nv

    spec = rspec(tr, cols)
    return hosted_call(kern, name=name, grid=(rows // tr,), in_specs=[spec] * 4, out_specs=[spec] * 3,
                       out_shape=[SDS((rows, cols), F32)] * 3, operands=[w, g, m, v])


def _row_tile(rows, cap=512, unit=16):
    return max(t for t in range(unit, cap + 1, unit) if rows % t == 0)


def _place():
    x, y, c = lax.axis_index("x"), lax.axis_index("y"), lax.axis_index("c")
    return x, y, c, [(1 - x, y), (x, 1 - y), (1 - x, 1 - y)]


def _row_chunks(rows, n, dtype):
    unit = 32 // jnp.dtype(dtype).itemsize
    base, extra = divmod(rows // unit, n)
    out, start = [], 0
    for k in range(n):
        size = (base + (k < extra)) * unit
        if size:
            out.append((start, size))
            start += size
    assert start == rows, (rows, unit)
    return out


PIECE_BYTES = 1 << 20


def _pieces(shapes_dtypes, rows_of):
    out = []
    for b, (shape, dtype) in enumerate(shapes_dtypes):
        rows = rows_of(shape)
        n = max(1, min(4, rows * shape[-1] * jnp.dtype(dtype).itemsize // PIECE_BYTES))
        out += [(b, st, sz) for st, sz in _row_chunks(rows, n, dtype)]
    return out


def plan_gather_ici(shards):
    pieces = _pieces([(s.shape, s.dtype) for s in shards], lambda shape: shape[0] // 2)
    n = len(pieces)

    def copies(x_refs, out_refs, send_sems, recv_sems):
        x, y, c, chips = _place()
        mine = 2 * x + y

        def copy(j, k, chip, to, from_input):
            b, st, sz = pieces[k]
            rows_k = pl.ds(c * (x_refs[b].shape[0] // 2) + st, sz)
            dst = out_refs[b].at[chip, rows_k, :]
            return pltpu.make_async_remote_copy(src_ref=x_refs[b].at[rows_k, :] if from_input else dst, dst_ref=dst, send_sem=send_sems.at[j * n + k],
                                                recv_sem=recv_sems.at[j * n + k], device_id=to, device_id_type=MESH_ID)

        order = [(k, j, 2 * cx + cy, (cx, cy, c)) for k in range(n) for j, (cx, cy) in enumerate(chips)]
        return [copy(j, k, mine, to, True) for k, j, _, to in order], [copy(j, k, chip, to, False) for k, j, chip, to in order]

    return Plan(shards, [SDS((4,) + s.shape, s.dtype) for s in shards], {}, 3 * n, copies)


def plan_gather_pass(gathered):
    pieces = _pieces([(g.shape[1:], g.dtype) for g in gathered], lambda shape: shape[0] // 2)
    n = len(pieces)

    def copies(_, out_refs, send_sems, recv_sems):
        x, y, c, chips = _place()

        def copy(j, k, chip, cc):
            b, st, sz = pieces[k]
            rows_k = out_refs[b].at[chip, pl.ds(cc * (out_refs[b].shape[1] // 2) + st, sz), :]
            return pltpu.make_async_remote_copy(src_ref=rows_k, dst_ref=rows_k, send_sem=send_sems.at[j * n + k], recv_sem=recv_sems.at[j * n + k],
                                                device_id=(x, y, 1 - c), device_id_type=MESH_ID)

        order = [(k, j, 2 * cx + cy) for k in range(n) for j, (cx, cy) in enumerate(chips)]
        return [copy(j, k, chip, c) for k, j, chip in order], [copy(j, k, chip, 1 - c) for k, j, chip in order]

    return Plan(gathered, [SDS(g.shape, g.dtype) for g in gathered], {i: i for i in range(len(gathered))}, 3 * n, copies)


def plan_pair_exchange(gs):
    pieces = _pieces([(g.shape, g.dtype) for g in gs], lambda shape: shape[1] // 2)

    def copies(g_refs, got_refs, send_sems, recv_sems):
        x, y, c, _ = _place()
        swaps = [pltpu.make_async_remote_copy(src_ref=g_refs[b].at[:, pl.ds((1 - c) * (g_refs[b].shape[1] // 2) + st, sz), :],
                                              dst_ref=got_refs[b].at[:, pl.ds(st, sz), :], send_sem=send_sems.at[k], recv_sem=recv_sems.at[k],
                                              device_id=(x, y, 1 - c), device_id_type=MESH_ID)
                 for k, (b, st, sz) in enumerate(pieces)]
        return swaps, swaps

    return Plan(gs, [SDS((g.shape[0], g.shape[1] // 2, g.shape[2]), g.dtype) for g in gs], {}, len(pieces), copies)


def plan_chip_scatter(ps):
    pieces = _pieces([(p.shape, p.dtype) for p in ps], lambda shape: shape[1])
    n = len(pieces)

    def copies(p_refs, q_refs, send_sems, recv_sems):
        x, y, c, chips = _place()
        mine = 2 * x + y

        def copy(j, k, src_slot, dst_slot, to):
            b, st, sz = pieces[k]
            return pltpu.make_async_remote_copy(src_ref=p_refs[b].at[src_slot, pl.ds(st, sz), :], dst_ref=q_refs[b].at[dst_slot, pl.ds(st, sz), :],
                                                send_sem=send_sems.at[j * n + k], recv_sem=recv_sems.at[j * n + k], device_id=to,
                                                device_id_type=MESH_ID)

        order = [(k, j, 2 * cx + cy, (cx, cy, c)) for k in range(n) for j, (cx, cy) in enumerate(chips)]
        return [copy(j, k, chip, mine, to) for k, j, chip, to in order], [copy(j, k, mine, chip, to) for k, j, chip, to in order]

    return Plan(ps, [SDS(p.shape, p.dtype) for p in ps], {}, 3 * n, copies)


def plan_pair_join(bufs):
    pieces = _pieces([(b.shape, b.dtype) for b in bufs], lambda shape: shape[0] // 2)

    def copies(_, out_refs, send_sems, recv_sems):
        x, y, c, _ = _place()

        def copy(k, cc):
            b, st, sz = pieces[k]
            rows_k = out_refs[b].at[pl.ds(cc * (out_refs[b].shape[0] // 2) + st, sz), :]
            return pltpu.make_async_remote_copy(src_ref=rows_k, dst_ref=rows_k, send_sem=send_sems.at[k], recv_sem=recv_sems.at[k],
                                                device_id=(x, y, 1 - c), device_id_type=MESH_ID)

        return [copy(k, c) for k in range(len(pieces))], [copy(k, 1 - c) for k in range(len(pieces))]

    return Plan(bufs, [SDS(b.shape, b.dtype) for b in bufs], {i: i for i in range(len(bufs))}, len(pieces), copies)


def run_plan(name, plan):
    hosted_call(lambda: None, name=name, grid=(1,), in_specs=[], out_specs=[], out_shape=[], operands=[], plans=[plan])
    return plan.results


HBM = pl.BlockSpec(memory_space=pltpu.HBM)
SEMS = pl.BlockSpec(memory_space=pltpu.SEMAPHORE)
SPLIT_PARAMS = dict(has_side_effects=pltpu.SideEffectType.DATAFLOW_SIDE_EFFECTING)


def _plan_buffers(plan):
    in_place = {o: i for i, o in plan.aliases.items()}
    bufs = [pltpu.with_memory_space_constraint(a, pltpu.HBM) for a in plan.operands]
    where = []
    for o, sd in enumerate(plan.out_shape):
        if o in in_place:
            where.append(in_place[o])
        else:
            where.append(len(bufs))
            bufs.append(pltpu.with_memory_space_constraint(lax.empty(sd.shape, sd.dtype), pltpu.HBM))
    return bufs, where


def split_start(name, plans):
    layout = [_plan_buffers(p) for p in plans]
    counts = [len(b) for b, _ in layout]
    n_buf = sum(counts)

    def body(*refs):
        sems, token = refs[n_buf:n_buf + 2 * len(plans)], refs[-1]
        pos = 0
        for k, (p, (_, where)) in enumerate(zip(plans, layout)):
            mine = refs[pos:pos + counts[k]]
            pos += counts[k]
            sends, _ = p.copies(mine[:len(p.operands)], [mine[w] for w in where], sems[2 * k], sems[2 * k + 1])
            for cp in sends:
                cp.start()
        token[...] = jnp.zeros_like(token)

    bufs = [b for bs, _ in layout for b in bs]
    res = pl.pallas_call(
        body, name=name, in_specs=[HBM] * n_buf,
        out_specs=[SEMS] * (2 * len(plans)) + [HBM] * n_buf + [pl.BlockSpec(memory_space=pltpu.VMEM)],
        out_shape=[pltpu.SemaphoreType.DMA((p.n_sems,)) for p in plans for _ in range(2)] + [pltpu.HBM(b.shape, b.dtype) for b in bufs]
        + [SDS((8, 128), F32)],
        input_output_aliases={i: 2 * len(plans) + i for i in range(n_buf)}, compiler_params=pltpu.CompilerParams(**SPLIT_PARAMS))(*bufs)
    pos = 2 * len(plans)
    for k, p in enumerate(plans):
        p.in_flight = (res[2 * k], res[2 * k + 1], list(res[pos:pos + counts[k]]), layout[k][1])
        pos += counts[k]
    return res[-1]


def split_wait(name, plan, after):
    send_sems, recv_sems, bufs, where = plan.in_flight
    n_buf = len(bufs)

    def body(*refs):
        mine = refs[:n_buf]
        sends, recvs = plan.copies(mine[:len(plan.operands)], [mine[w] for w in where], refs[n_buf], refs[n_buf + 1])
        for cp in recvs:
            cp.wait_recv()
        for cp in sends:
            cp.wait_send()

    res = pl.pallas_call(body, name=name, in_specs=[HBM] * n_buf + [SEMS, SEMS, ANY], out_specs=[HBM] * n_buf,
                         out_shape=[pltpu.HBM(b.shape, b.dtype) for b in bufs], input_output_aliases={i: i for i in range(n_buf)},
                         compiler_params=pltpu.CompilerParams(**SPLIT_PARAMS))(*bufs, send_sems, recv_sems, after)
    plan.results = [res[w] for w in where]
    return plan.results


def pair_add(name, g, got, place):
    slots, rows, cols = g.shape
    half = rows // 2
    tr = _row_tile(half)
    nb = half // tr

    def kern(_, g_ref, t_ref, o_ref):
        o_ref[...] = (g_ref[...].astype(F32) + t_ref[...].astype(F32)).astype(o_ref.dtype)

    blk = pl.BlockSpec((None, tr, cols), lambda s, i, p: (s, i, 0))
    grid_spec = pltpu.PrefetchScalarGridSpec(
        num_scalar_prefetch=1, grid=(slots, nb),
        in_specs=[pl.BlockSpec((None, tr, cols), lambda s, i, p: (s, p[1] * nb + i, 0)), blk], out_specs=blk)
    return pl.pallas_call(kern, grid_spec=grid_spec, out_shape=SDS((slots, half, cols), g.dtype), name=name,
                          compiler_params=_cparams())(place, g, got)


def chip_add(name, p, q, place):
    slots, half, cols = p.shape
    tr = _row_tile(half)
    nb = half // tr

    def kern(_, p_ref, q1, q2, q3, o_ref):
        o_ref[...] = p_ref[...].astype(F32) + q1[...].astype(F32) + q2[...].astype(F32) + q3[...].astype(F32)

    def slot(k):
        return pl.BlockSpec((None, tr, cols), lambda i, pr: ((pr[0] + k) % slots, i, 0))

    grid_spec = pltpu.PrefetchScalarGridSpec(
        num_scalar_prefetch=1, grid=(nb,), in_specs=[slot(0), slot(1), slot(2), slot(3)],
        out_specs=pl.BlockSpec((tr, cols), lambda i, pr: (pr[1] * nb + i, 0)))
    return pl.pallas_call(kern, grid_spec=grid_spec, out_shape=SDS((2 * half, cols), F32), name=name,
                          compiler_params=_cparams())(place, p, q, q, q)


def pair_adds(tag, gs, gots, place):
    return [pair_add(f"{tag}_pair_add_{i}", g, got, place) for i, (g, got) in enumerate(zip(gs, gots))]


def chip_adds(tag, pairs, qs, place):
    return [chip_add(f"{tag}_chip_add_{i}", p, q, place) for i, (p, q) in enumerate(zip(pairs, qs))]


def all_reduce_in_vmem(name, g):
    slots, rows, cols = g.shape
    half = rows // 2

    def kern(g_ref, out_ref, got, pair, q, send_sems, recv_sems):
        x, y, c, chips = _place()
        mine = 2 * x + y
        other_core = (x, y, 1 - c)
        my_rows, its_rows = pl.ds(c * half, half), pl.ds((1 - c) * half, half)

        def exchange(copies):
            for cp in copies:
                cp.start()
            for cp in copies:
                cp.wait()

        def copy(k, src, dst, to):
            return pltpu.make_async_remote_copy(src_ref=src, dst_ref=dst, send_sem=send_sems.at[k], recv_sem=recv_sems.at[k], device_id=to,
                                                device_id_type=MESH_ID)

        exchange([copy(0, g_ref.at[:, its_rows, :], got, other_core)])
        pair[...] = (g_ref[:, my_rows, :].astype(F32) + got[...].astype(F32)).astype(pair.dtype)
        exchange([copy(1 + j, pair.at[2 * cx + cy], q.at[mine], (cx, cy, c)) for j, (cx, cy) in enumerate(chips)])
        total = pair[mine].astype(F32)
        for k in range(1, slots):
            total = total + q[(mine + k) % slots].astype(F32)
        out_ref[mine, my_rows, :] = total
        reduced = out_ref.at[mine, my_rows, :]
        first = [copy(4 + j, reduced, reduced, (cx, cy, c)) for j, (cx, cy) in enumerate(chips)] + [copy(7, reduced, reduced, other_core)]
        for cp in first:
            cp.start()
        passed = []
        for j, (cx, cy) in enumerate(chips):
            first[j].wait_recv()
            landed = out_ref.at[2 * cx + cy, my_rows, :]
            passed.append(copy(8 + j, landed, landed, other_core))
            passed[-1].start()
        for cp in [first[3]] + passed:
            cp.wait_recv()
        for cp in first + passed:
            cp.wait_send()

    vmem = pl.BlockSpec(memory_space=pltpu.VMEM)
    return pl.pallas_call(kern, in_specs=[vmem], out_specs=vmem, out_shape=SDS(g.shape, F32), name=name, compiler_params=_cparams(),
                          scratch_shapes=[pltpu.VMEM((slots, half, cols), g.dtype)] * 3 + [pltpu.SemaphoreType.DMA((11,))] * 2)(g)


BIG = [("w_out", (2, 512, 1024)), ("w_mem_kv", (2, 256, 1024)), ("s5_w_in", (1, 1024, 1024)), ("s5_w_glu", (1, 1536, 768)),
       ("mla_w_in", (1, 1024, 848)), ("mla_w_uq", (1, 512, 576)), ("mla_w_ukv", (1, 256, 768))]
SHARDED_SMALL = [("mla_q_lora_norm", (1, 128)), ("mla_kv_lora_norm", (1, 64))]
SMALL = [("ln_gain", (2, 1024)), ("mem_norm", (2, 1024)), ("xq_norm", (2, 128)), ("xk_norm", (2, 128)),
         ("s5_lambda_re", (1, 96, 64)), ("s5_lambda_im", (1, 96, 64)), ("s5_log_step", (1, 96)),
         ("s5_b_re", (1, 96, 64, 16)), ("s5_b_im", (1, 96, 64, 16)), ("s5_c_re", (1, 96, 16, 64)), ("s5_c_im", (1, 96, 16, 64)),
         ("s5_d", (1, 1536)), ("mla_q_nope_norm", (1, 128)), ("mla_k_nope_norm", (1, 128)), ("mla_q_rope_norm", (1, 64)),
         ("mla_k_rope_norm", (1, 64))]
WEIGHT_ORDER = ["ln_gain", "w_out", "mem_norm", "w_mem_kv", "xq_norm", "xk_norm", "s5_w_in", "s5_lambda_re", "s5_lambda_im",
                "s5_log_step", "s5_b_re", "s5_b_im", "s5_c_re", "s5_c_im", "s5_d", "s5_w_glu", "mla_w_in", "mla_q_lora_norm",
                "mla_kv_lora_norm", "mla_w_uq", "mla_w_ukv", "mla_q_nope_norm", "mla_k_nope_norm", "mla_q_rope_norm", "mla_k_rope_norm"]
MINOR_LAST = {"mla_w_in": (0, 2, 1), "mla_w_uq": (0, 2, 1), "s5_b_re": (0, 2, 3, 1), "s5_b_im": (0, 2, 3, 1),
              "s5_c_re": (0, 2, 3, 1), "s5_c_im": (0, 2, 3, 1)}
SMALL_FULL = SMALL + [(n, (1, 4 * s[1])) for n, s in SHARDED_SMALL]
N_SMALL = sum(math.prod(s) for _, s in SMALL_FULL)
SMALL_ROWS, SMALL_LANES = 128, 1024

PAIR_OUT, PAIR_MKV, PAIR_ROWS = 0, 512, 768


def stack_shards(w, dtype):
    pairs = [jnp.concatenate([w["w_out"][l], w["w_mem_kv"][l]], axis=0).astype(dtype) for l in range(2)]
    return ([w["s5_w_in"][0].astype(dtype)], [pairs[0], w["s5_w_glu"][0].astype(dtype)],
            [pairs[1], w["mla_w_ukv"][0].astype(dtype), jnp.pad(w["mla_w_in"][0].T.astype(dtype), ((0, IN1T_ROWS - MLA_IN // 4), (0, 0))),
             w["mla_w_uq"][0].T.astype(dtype)])


def pair_views(pair):
    return {"w_out": Sharded(pair, "row", PAIR_OUT, 512), "w_mem_kv": Sharded(pair, "row", PAIR_MKV, 256)}


def grad_views():
    pair = SDS((4, PAIR_ROWS, 1024), BF16)
    return {"w_out": Sharded(pair, "row", PAIR_OUT, 512), "w_mem_kv": Sharded(pair, "row", PAIR_MKV, 256),
            "s5_w_in": Sharded(SDS((4, 1024, 1024), BF16), "col", 0, 1024), "s5_w_glu": Sharded(SDS((4, 1536, 768), BF16), "col", 0, 1536),
            "mla_w_ukv": Sharded(SDS((4, 256, 768), BF16), "col", 0, 256)}


IN1T_ROWS = 864


def in1_rows_permute(wt):
    o1, o2, o3, o4 = QL, QL + KVL, QL + KVL + ROPE, QL + KVL + ROPE + XQW
    return jnp.concatenate([wt[o4:], wt[:o1], wt[o3:o4], wt[o1:o2], wt[o2:o3], jnp.zeros((MLA_IN_P - MLA_IN, wt.shape[1]), wt.dtype)], axis=0)


def in1_rows_unpermute(d):
    return jnp.concatenate([d[2048:2560], d[3072:3328], d[3328:3392], d[2560:3072], d[:2048]], axis=0)


def uq_rows_permute(wt):
    w3 = wt.reshape(MH, NOPE + ROPE, wt.shape[1])
    return jnp.concatenate([w3[:, :NOPE].reshape(MH * NOPE, wt.shape[1]), w3[:, NOPE:].reshape(MH * ROPE, wt.shape[1])], axis=0)


def uq_rows_unpermute(d):
    dn = d[:MH * NOPE].reshape(MH, NOPE, d.shape[1])
    dr = d[MH * NOPE:].reshape(MH, ROPE, d.shape[1])
    return jnp.concatenate([dn, dr], axis=1).reshape(MH * (NOPE + ROPE), d.shape[1])


def time_permute(a):
    return a.reshape(SEG, SEG_LEN, a.shape[-1]).transpose(1, 0, 2).reshape(L, a.shape[-1])


def time_unpermute(a):
    return a.reshape(SEG_LEN, SEG, a.shape[-1]).transpose(1, 0, 2).reshape(L, a.shape[-1])


def mem_branch_fwd(tag, mem, mem_norm, w_mem_kv, xk_norm):
    mn = row_fwd(tag + "_mem_rms", fn_rms, ML, ML, [(mem, D, 0)], [mem_norm], [(D, BF16)])[0]
    kv = matmul(tag + "_mem_kv", mn, w_mem_kv, "nn", F32)
    kn = row_fwd(tag + "_mem_knorm", fn_mem_k, ML, ML, [(kv, XQW, 0)], [xk_norm], [(XQW, F32)])[0]
    return mn, kv, kn


def mem_branch_bwd(tag, mem, mem_norm, w_mem_kv, xk_norm, mn, kv, dkn, dv, g_view, g_wide):
    dk, dxk = row_bwd(tag + "_mem_knorm_bwd", fn_mem_k, ML, ML, [(kv, XQW, 0)], [xk_norm], [(dkn, XQW, 0)], [True], [True])
    dkv = jnp.concatenate([dk, dv], axis=1)
    dmn = matmul(tag + "_mem_kv_dx", dkv, w_mem_kv, "nt", F32)
    g_wide = matmul(tag + "_mem_kv_dw", mn, dkv, "tn", out=g_view, into=g_wide)
    dmem_norm = row_bwd(tag + "_mem_rms_bwd", fn_rms, ML, ML, [(mem, D, 0)], [mem_norm], [(dmn, D, 0)], [False], [True])[0]
    return g_wide, dmem_norm, dxk


def mem_attn_fwd(tag, proj, cb, kn, kv, xq_norm):
    return row_fwd(tag + "_mem_attn", fn_mem_attn, L, ROW_TILE, [(proj, XQW, cb)], [kn, kv[:, XQW:], xq_norm], [(XQW, F32)])[0]


def mem_attn_bwd(tag, proj, cb, kn, kv, xq_norm, dmo, dproj):
    place = {"cols": proj.shape[1], "cb": cb, "into": dproj, "dtype": dproj.dtype}
    return row_bwd(tag + "_mem_attn_bwd", fn_mem_attn, L, ROW_TILE, [(proj, XQW, cb)], [kn, kv[:, XQW:], xq_norm], [(dmo, XQW, 0)],
                   [place], [True, True, True])


def device_step(x, mem, positions, target, small, env, hooks=None):
    hooks = hooks or {}

    def plans_for(name):
        return hooks[("plans", name)](env) if ("plans", name) in hooks else ()

    def around(when, name, last=None):
        if (when, name) in hooks:
            hooks[(when, name)](env, last)

    g = {}
    gw = grad_views()
    ln, mem_norm, xq_norm, xk_norm = small["ln_gain"], small["mem_norm"], small["xq_norm"], small["xk_norm"]

    lre, lim = small["s5_lambda_re"][0], small["s5_lambda_im"][0]
    ls = small["s5_log_step"].reshape(SG, 1)
    one = pl.BlockSpec((SG, SP), lambda i: (0, 0))
    col = pl.BlockSpec((SG, 1), lambda i: (0, 0))
    disc_ins = [(lre, one), (lim, one), (ls, col)]
    a_re, a_im, coef_re, coef_im = stage("s5_disc", fn_s5_disc, (1,), disc_ins, [(SDS((SG, SP), F32), one)] * 4)
    b_re, b_im = small["s5_b_re"].reshape(SN, SC), small["s5_b_im"].reshape(SN, SC)
    c_re, c_im = small["s5_c_re"].reshape(PW, SP), small["s5_c_im"].reshape(PW, SP)
    bmat_rows = [(b_re, SC, 0), (b_im, SC, 0), (coef_re.reshape(SN, 1), 1, 0), (coef_im.reshape(SN, 1), 1, 0)]
    wb_re, wb_im = row_fwd("s5_bmat", fn_s5_bmat, SN, BMAT_TILE, bmat_rows, [], [(128, F32)] * 2)
    cmat_rows = [(c_re, SP, 0), (c_im, SP, 0)]
    wc_re, wc_im = row_fwd("s5_cmat", fn_s5_cmat, PW, CMAT_TILE, cmat_rows, [], [(512, F32)] * 2)
    a_re_v, a_im_v = a_re.reshape(1, SN), a_im.reshape(1, SN)
    s5_d = small["s5_d"]

    xp = time_permute(x)
    h0 = row_fwd("l0_rms", fn_rms, L, ROW_TILE, [(xp, D, 0)], [ln[0:1]], [(D, BF16)])[0]
    around("before", "l0_in", wb_re)
    w_in0 = Sharded(env["in0"], "col", 0, 1024)
    proj0 = matmul("l0_in", h0, w_in0, "nn")
    s_re, s_im, g0 = s5_forward(proj0, wb_re, wb_im, wc_re, wc_im, a_re_v, a_im_v, s5_d, plans=plans_for("s5_forward"))
    around("before", "l0_glu", g0)
    w0 = dict(pair_views(env["pair0"]), s5_w_glu=Sharded(env["glu"], "col", 0, 1536))
    z0 = matmul("l0_glu", g0, w0["s5_w_glu"], "nn", plans=plans_for("l0_glu"))
    mn0, kv0, kn0 = mem_branch_fwd("l0", mem, mem_norm[0:1], w0["w_mem_kv"], xk_norm[0:1])
    mo0 = mem_attn_fwd("l0", proj0, 3, kn0, kv0, xq_norm[0:1])
    o0 = row_fwd("l0_merge", fn_merge_glu, L, ROW_TILE, [(z0, 2 * PW, 0), (mo0, XQW, 0), (proj0, BW, 1)], [], [(BW, BF16)])[0]
    around("before", "l0_out", o0)
    x1p = matmul("l0_out", o0, w0["w_out"], "nn", F32, add=xp, plans=plans_for("l0_out"))
    around("after", "l0_out", x1p)
    x1 = time_unpermute(x1p)

    w1 = dict(pair_views(env["pair1"]), mla_w_ukv=Sharded(env["ukv"], "col", 0, 256))
    w_in1, w_uq = env["w_in1"], env["w_uq"]
    h1 = row_fwd("l1_rms", fn_rms, L, ROW_TILE, [(x1, D, 0)], [ln[1:2]], [(D, BF16)])[0]
    proj1 = matmul("l1_in", h1, w_in1, "nt")
    qln, kvln = env["q_lora_norm"].reshape(1, QL), env["kv_lora_norm"].reshape(1, KVL)
    cqn = row_fwd("l1_q_lora_rms", fn_rms, L, ROW_TILE, [(proj1, QL, 4)], [qln], [(QL, BF16)])[0]
    ckvn = row_fwd("l1_kv_lora_rms", fn_rms, L, ROW_TILE, [(proj1, KVL, 12)], [kvln], [(KVL, BF16)])[0]
    q = matmul("l1_uq", cqn, w_uq, "nt")
    kv = matmul("l1_ukv", ckvn, w1["mla_w_ukv"], "nn")
    inv_freq = ROPE_THETA ** (-jnp.arange(ROPE // 2, dtype=F32) / (ROPE // 2))
    ang = positions.astype(F32)[:, None] * inv_freq
    cos2 = jnp.tile(jnp.cos(ang), (1, 4))
    sin_signed = jnp.tile(jnp.concatenate([-jnp.sin(ang), jnp.sin(ang)], axis=1), (1, 2))
    qnn, knn = small["mla_q_nope_norm"], small["mla_k_nope_norm"]
    qrn, krn = jnp.tile(small["mla_q_rope_norm"], (1, 2)), jnp.tile(small["mla_k_rope_norm"], (1, 2))
    tp = 256
    prep_ins = [(q, rspec(tp, MH * (NOPE + ROPE))), (kv, rspec(tp, MH * 256)), (proj1, rspec(tp, 128, 26)),
                (cos2, rspec(tp, 128)), (sin_signed, rspec(tp, 128))] + [(a, cspec((1, 128))) for a in (qnn, knn, qrn, krn)]
    hq_spec = pl.BlockSpec((MH, tp, 256), lambda i: (0, i, 0))
    hv_spec = pl.BlockSpec((MH, tp, 128), lambda i: (0, i, 0))
    qf, kf, vh = stage("l1_mla_prep", fn_mla_prep, (L // tp,), prep_ins,
                       [(SDS((MH, L, 256), BF16), hq_spec), (SDS((MH, L, 256), BF16), hq_spec), (SDS((MH, L, 128), BF16), hv_spec)])
    attn, attn_lse = causal_attn(qf, kf, vh)
    mn1, kv1, kn1 = mem_branch_fwd("l1", mem, mem_norm[1:2], w1["w_mem_kv"], xk_norm[1:2])
    mo1 = mem_attn_fwd("l1", proj1, 5, kn1, kv1, xq_norm[1:2])
    o1 = row_fwd("l1_merge", fn_merge, L, ROW_TILE, [(attn, PW, 0), (mo1, XQW, 0), (proj1, BW, 0)], [], [(BW, BF16)])[0]
    x2 = matmul("l1_out", o1, w1["w_out"], "nn", F32, add=x1)
    dx2, loss = loss_and_grad(x2, target)
    around("after", "loss", loss)

    do1 = matmul("l1_out_dx", dx2, w1["w_out"], "nt")
    g_pair1 = matmul("l1_out_dw", o1, dx2, "tn", out=gw["w_out"])
    dattn, dmo1, dproj1 = row_bwd("l1_merge_bwd", fn_merge, L, ROW_TILE, [(attn, PW, 0), (mo1, XQW, 0), (proj1, BW, 0)], [],
                                  [(do1, BW, 0)], [True, True, {"cols": MLA_IN_P, "cb": 0, "dtype": BF16}], [])
    dproj1, dkn1, dv1, dxqn1 = mem_attn_bwd("l1", proj1, 5, kn1, kv1, xq_norm[1:2], dmo1, dproj1)
    env["g_pair1"], dmem_norm1, dxk1 = mem_branch_bwd("l1", mem, mem_norm[1:2], w1["w_mem_kv"], xk_norm[1:2], mn1, kv1, dkn1, dv1,
                                                      gw["w_mem_kv"], g_pair1)
    dqf, dkf, dvh = causal_attn_bwd(qf, kf, vh, attn, attn_lse, dattn)
    prep_diffs = [("row", SDS((L, MH * (NOPE + ROPE)), BF16), rspec(tp, MH * (NOPE + ROPE))), ("row", SDS((L, MH * 256), BF16), rspec(tp, MH * 256)),
                  ("row", SDS((L, MLA_IN_P), BF16), rspec(tp, 128, 26), {"into": dproj1}), None, None] + [("acc", (0,))] * 4
    dq, dkv, dproj1, dqnn, dknn, dqrn, dkrn = stage_bwd("l1_mla_prep_bwd", fn_mla_prep, (L // tp,), prep_ins,
                                                        [(dqf, hq_spec), (dkf, hq_spec), (dvh, hv_spec)], prep_diffs)
    dcqn = matmul("l1_uq_dx", dq, w_uq, "nn")
    env["dw_uq"] = matmul("l1_uq_dw", dq, cqn, "tn")
    dckvn = matmul("l1_ukv_dx", dkv, w1["mla_w_ukv"], "nt")
    env["g_ukv"] = matmul("l1_ukv_dw", ckvn, dkv, "tn", out=gw["mla_w_ukv"])
    dproj1, dqln = row_bwd("l1_q_lora_rms_bwd", fn_rms, L, ROW_TILE, [(proj1, QL, 4)], [qln], [(dcqn, QL, 0)],
                           [{"cols": MLA_IN_P, "cb": 4, "into": dproj1, "dtype": BF16}], [True])
    dproj1, dkvln = row_bwd("l1_kv_lora_rms_bwd", fn_rms, L, ROW_TILE, [(proj1, KVL, 12)], [kvln], [(dckvn, KVL, 0)],
                            [{"cols": MLA_IN_P, "cb": 12, "into": dproj1, "dtype": BF16}], [True])
    env["dw_in1"] = matmul("l1_in_dw", dproj1, h1, "tn")
    dh1 = matmul("l1_in_dx", dproj1, w_in1, "nn", plans=plans_for("l1_in_dx"))
    around("after", "l1_in_dx", dh1)
    dx1, dln1 = row_bwd("l1_rms_bwd", fn_rms, L, ROW_TILE, [(x1, D, 0)], [ln[1:2]], [(dh1, D, 0)], [{"add": (dx2, D, 0)}], [True])
    dx1p = time_permute(dx1)

    do0 = matmul("l0_out_dx", dx1p, w0["w_out"], "nt", plans=plans_for("l0_out_dx"))
    g_pair0 = matmul("l0_out_dw", o0, dx1p, "tn", out=gw["w_out"])
    dz0, dmo0, dproj0 = row_bwd("l0_merge_bwd", fn_merge_glu, L, ROW_TILE, [(z0, 2 * PW, 0), (mo0, XQW, 0), (proj0, BW, 1)], [],
                                [(do0, BW, 0)], [{"dtype": BF16}, True, {"cols": 2 * BW, "cb": 1, "dtype": BF16}], [])
    dproj0, dkn0, dv0, dxqn0 = mem_attn_bwd("l0", proj0, 3, kn0, kv0, xq_norm[0:1], dmo0, dproj0)
    env["g_pair0"], dmem_norm0, dxk0 = mem_branch_bwd("l0", mem, mem_norm[0:1], w0["w_mem_kv"], xk_norm[0:1], mn0, kv0, dkn0, dv0,
                                                      gw["w_mem_kv"], g_pair0)
    env["g_glu"] = matmul("l0_glu_dw", g0, dz0, "tn", out=gw["s5_w_glu"], plans=plans_for("l0_glu_dw"))
    dg0 = matmul("l0_glu_dx", dz0, w0["s5_w_glu"], "nt", plans=plans_for("l0_glu_dx"))
    around("before", "s5_backward", dg0)
    dproj0, dd, dwc_re, dwc_im, dwb_re, dwb_im, da_re, da_im = s5_backward(dg0, proj0, s_re, s_im, wb_re, wb_im, wc_re, wc_im,
                                                                           a_re_v, a_im_v, s5_d, dproj0, plans=plans_for("s5_backward"))
    around("after", "s5_backward", dd)
    env["g_in0"] = matmul("l0_in_dw", h0, dproj0, "tn", out=gw["s5_w_in"], plans=plans_for("l0_in_dw"))
    dh0 = matmul("l0_in_dx", dproj0, w_in0, "nt", plans=plans_for("l0_in_dx"))
    around("after", "l0_in_dx", dh0)
    dxp, dln0 = row_bwd("l0_rms_bwd", fn_rms, L, ROW_TILE, [(xp, D, 0)], [ln[0:1]], [(dh0, D, 0)], [{"add": (dx1p, D, 0)}], [True])
    grad_x = time_unpermute(dxp)

    db_re, db_im, dcoef_re, dcoef_im = row_bwd("s5_bmat_bwd", fn_s5_bmat, SN, BMAT_TILE, bmat_rows, [], [(dwb_re, 128, 0), (dwb_im, 128, 0)],
                                               [True] * 4, [], plans=plans_for("s5_bmat_bwd"))
    dc_re, dc_im = row_bwd("s5_cmat_bwd", fn_s5_cmat, PW, CMAT_TILE, cmat_rows, [], [(dwc_re, 512, 0), (dwc_im, 512, 0)], [True] * 2, [],
                           plans=plans_for("s5_cmat_bwd"))
    disc_cts = [(da_re.reshape(SG, SP), one), (da_im.reshape(SG, SP), one), (dcoef_re.reshape(SG, SP), one), (dcoef_im.reshape(SG, SP), one)]
    dlre, dlim, dls = stage_bwd("s5_disc_bwd", fn_s5_disc, (1,), disc_ins, disc_cts, [("acc", (0,))] * 3)

    g["ln_gain"] = jnp.concatenate([dln0, dln1], axis=0)
    g["mem_norm"] = jnp.concatenate([dmem_norm0, dmem_norm1], axis=0)
    g["xq_norm"] = jnp.concatenate([dxqn0, dxqn1], axis=0)
    g["xk_norm"] = jnp.concatenate([dxk0, dxk1], axis=0)
    g["s5_lambda_re"], g["s5_lambda_im"], g["s5_log_step"] = dlre, dlim, dls
    g["s5_b_re"], g["s5_b_im"], g["s5_c_re"], g["s5_c_im"] = db_re, db_im, dc_re, dc_im
    g["s5_d"] = dd
    g["mla_q_lora_norm"], g["mla_kv_lora_norm"] = dqln, dkvln
    g["mla_q_nope_norm"], g["mla_k_nope_norm"] = dqnn, dknn
    g["mla_q_rope_norm"] = dqrn[:, :ROPE] + dqrn[:, ROPE:]
    g["mla_k_rope_norm"] = dkrn[:, :ROPE] + dkrn[:, ROPE:]
    return loss, grad_x, g


def kernel(x, mem, positions, ln_gain, w_out, mem_norm, w_mem_kv, xq_norm, xk_norm, s5_w_in, s5_lambda_re, s5_lambda_im, s5_log_step, s5_b_re, s5_b_im, s5_c_re, s5_c_im, s5_d, s5_w_glu, mla_w_in, mla_q_lora_norm, mla_kv_lora_norm, mla_w_uq, mla_w_ukv, mla_q_nope_norm, mla_k_nope_norm, mla_q_rope_norm, mla_k_rope_norm, loss_target, m_ln_gain, m_w_out, m_mem_norm, m_w_mem_kv, m_xq_norm, m_xk_norm, m_s5_w_in, m_s5_lambda_re, m_s5_lambda_im, m_s5_log_step, m_s5_b_re, m_s5_b_im, m_s5_c_re, m_s5_c_im, m_s5_d, m_s5_w_glu, m_mla_w_in, m_mla_q_lora_norm, m_mla_kv_lora_norm, m_mla_w_uq, m_mla_w_ukv, m_mla_q_nope_norm, m_mla_k_nope_norm, m_mla_q_rope_norm, m_mla_k_rope_norm, v_ln_gain, v_w_out, v_mem_norm, v_w_mem_kv, v_xq_norm, v_xk_norm, v_s5_w_in, v_s5_lambda_re, v_s5_lambda_im, v_s5_log_step, v_s5_b_re, v_s5_b_im, v_s5_c_re, v_s5_c_im, v_s5_d, v_s5_w_glu, v_mla_w_in, v_mla_q_lora_norm, v_mla_kv_lora_norm, v_mla_w_uq, v_mla_w_ukv, v_mla_q_nope_norm, v_mla_k_nope_norm, v_mla_q_rope_norm, v_mla_k_rope_norm):
    args = dict(locals())
    wts = {n: args[n] for n in WEIGHT_ORDER}
    mom = {n: args["m_" + n] for n in WEIGHT_ORDER}
    var = {n: args["v_" + n] for n in WEIGHT_ORDER}

    chip = 2 * lax.axis_index("x") + lax.axis_index("y")
    place = jnp.stack([chip, lax.axis_index("c")]).astype(jnp.int32)

    def own_slot(gathered, shards):
        return [lax.dynamic_update_slice(g, s[None], (chip, 0, 0)) for g, s in zip(gathered, shards)]

    groups = list(stack_shards(wts, BF16))
    groups[2].append(jnp.concatenate([mla_q_lora_norm, jnp.pad(mla_kv_lora_norm, ((0, 0), (0, 64))), jnp.zeros((14, 128), F32)], axis=0))
    over_ici = [plan_gather_ici(shards) for shards in groups]
    _SCHEDULE_BEHIND.clear()
    schedule_behind(split_start("gather_start", over_ici))
    env, hooks, passed_on = {}, {}, {}

    def arrived(k, after, pass_now):
        passing = plan_gather_pass(split_wait(f"gather_wait_{k}", over_ici[k], after))
        passed_on[k] = passing
        return own_slot(run_plan(f"gather_pass_{k}", passing), groups[k]) if pass_now else None

    def need_in0(env, last):
        env["in0"], = arrived(0, last, True)

    def need_layer0(env, last):
        env["pair0"], env["glu"] = arrived(1, last, True)

    def need_layer1(env, last):
        arrived(2, last, False)

    def layer1_weights(env, last):
        pair1, ukv, in1, uq, norms = own_slot(passed_on[2].results, groups[2])
        env.update(pair1=pair1, ukv=ukv, w_in1=in1_rows_permute(in1[:, :MLA_IN // 4].reshape(MLA_IN, D)), w_uq=uq_rows_permute(uq.reshape(-1, QL)),
                   q_lora_norm=norms[:, 0, :], kv_lora_norm=norms[:, 1, :64])

    hooks["before", "l0_in"], hooks["before", "l0_glu"], hooks["before", "l0_out"] = need_in0, need_layer0, need_layer1
    hooks["plans", "l0_out"], hooks["after", "l0_out"] = (lambda env: [passed_on[2]]), layer1_weights

    rs = {}

    def swap(k, gs):
        rs[k, "g"], rs[k, "swap"] = gs, plan_pair_exchange(gs)
        return rs[k, "swap"]

    def start_scatter(k):
        rs[k, "pairs"] = pair_adds(f"rs{k}", rs[k, "g"], rs[k, "swap"].results, place)
        rs[k, "scatter"] = plan_chip_scatter(rs[k, "pairs"])
        schedule_behind(split_start(f"rs{k}_scatter_start", [rs[k, "scatter"]]))

    def join(k, after):
        rs[k, "join"] = plan_pair_join(chip_adds(f"rs{k}", rs[k, "pairs"], split_wait(f"rs{k}_scatter_wait", rs[k, "scatter"], after), place))
        return rs[k, "join"]

    def layer1_gradients(env):
        g_in1 = jnp.pad(in1_rows_unpermute(env["dw_in1"]).reshape(4, MLA_IN // 4, D), ((0, 0), (0, IN1T_ROWS - MLA_IN // 4), (0, 0)))
        return [swap(1, [env["g_pair1"], env["g_ukv"], g_in1, uq_rows_unpermute(env["dw_uq"]).reshape(4, -1, QL)])]

    hooks["plans", "l1_in_dx"] = layer1_gradients
    hooks["after", "l1_in_dx"] = lambda env, last: start_scatter(1)
    hooks["plans", "l0_glu_dx"] = lambda env: [swap(0, [env["g_pair0"], env["g_glu"]])]

    def before_s5_backward(env, last):
        start_scatter(0)
        env["join1"] = join(1, last)

    hooks["before", "s5_backward"] = before_s5_backward
    hooks["plans", "s5_backward"] = lambda env: [env["join1"]]
    hooks["after", "s5_backward"] = lambda env, last: env.update(join0=join(0, last))
    hooks["plans", "l0_in_dx"] = lambda env: [swap(2, [env["g_in0"]]), env["join0"]]
    hooks["after", "l0_in_dx"] = lambda env, last: start_scatter(2)

    def total_loss(env, local):
        env["loss"] = lax.psum(local[0, 0], MESH_AXES)
        schedule_behind(env["loss"].reshape(1, 1))

    hooks["after", "loss"] = total_loss
    small = {n: wts[n] for n, _ in SMALL}
    loss, grad_x, g = device_step(x[0], mem[0], positions[0], loss_target[0], small, env, hooks)
    loss = env["loss"]
    r_in0, = run_plan("rs2_pair_join", join(2, g["s5_log_step"]))
    (r_pair1, r_ukv, r_in1, r_uq), (r_pair0, r_glu) = (rs[k, "join"].results for k in (1, 0))

    small_flat = jnp.concatenate([g[n].reshape(-1) for n, _ in SMALL_FULL])
    g_small = jnp.pad(small_flat, (0, 4 * SMALL_ROWS * SMALL_LANES - N_SMALL)).astype(BF16).reshape(4, SMALL_ROWS, SMALL_LANES)
    small_all = all_reduce_in_vmem("small_grads_all_reduce", g_small).reshape(-1)[:N_SMALL]

    grads = {"w_out": jnp.stack([r_pair0[:PAIR_MKV], r_pair1[:PAIR_MKV]]), "w_mem_kv": jnp.stack([r_pair0[PAIR_MKV:], r_pair1[PAIR_MKV:]]),
             "s5_w_in": r_in0[None], "s5_w_glu": r_glu[None], "mla_w_ukv": r_ukv[None], "mla_w_in": r_in1[:MLA_IN // 4].T[None], "mla_w_uq": r_uq.T[None]}
    off = 0
    for n, s in SMALL_FULL:
        grads[n] = small_all[off:off + math.prod(s)].reshape(s)
        off += math.prod(s)
    for n, s in SHARDED_SMALL:
        grads[n] = lax.dynamic_slice(grads[n], (0, chip * s[1]), s)

    delta, new_m, new_v = {}, {}, {}
    for n, s in BIG + [(n, s) for n, s in SMALL if len(s) == 4]:
        perm = MINOR_LAST.get(n, tuple(range(len(s))))
        turned = tuple(s[p] for p in perm)
        view = lambda a: jnp.transpose(a, perm).reshape(-1, turned[-1])
        res = adamw("adamw_" + n, view(wts[n]), view(grads[n]), view(mom[n]), view(var[n]))
        delta[n], new_m[n], new_v[n] = (jnp.transpose(r.reshape(turned), tuple(perm.index(i) for i in range(len(s)))) for r in res)
    small_names = [n for n, s in SMALL if len(s) < 4] + [n for n, _ in SHARDED_SMALL]
    n_own = sum(wts[n].size for n in small_names)
    rows_own = -(-n_own // (8 * 128)) * 8

    def pack_small(d):
        flat = jnp.concatenate([d[n].reshape(-1) for n in small_names])
        return jnp.pad(flat, (0, rows_own * 128 - n_own), constant_values=1.0).reshape(rows_own, 128)

    res = adamw("adamw_small", pack_small(wts), pack_small(grads), pack_small(mom), pack_small(var))
    off = 0
    for n in small_names:
        size = wts[n].size
        delta[n], new_m[n], new_v[n] = (r.reshape(-1)[off:off + size].reshape(wts[n].shape) for r in res)
        off += size

    return (loss, grad_x[None], *[grads[n] for n in WEIGHT_ORDER], *[delta[n] for n in WEIGHT_ORDER],
            *[new_m[n] for n in WEIGHT_ORDER], *[new_v[n] for n in WEIGHT_ORDER])
```

```python
import functools
import math

import jax
import jax.numpy as jnp
from jax import lax
from jax.experimental import pallas as pl
from jax.experimental.pallas import tpu as pltpu

F32, BF16 = jnp.float32, jnp.bfloat16
SDS = jax.ShapeDtypeStruct

D = 1024
L = 2048
ML = 256
BW = 2 * D
XQW = BW // 4
PW = BW - XQW
XH, XHD = 4, 128
SG, SC, SP = 96, 16, 64
SN = SG * SP
NOPE, ROPE, VD = 128, 64, 128
MH = 12
QL, KVL = 512, 256
EPS = 1e-6
ROPE_THETA = 10000.0
MLA_IN = QL + KVL + ROPE + XQW + BW
MLA_IN_P = 3456
ADAM_LR, ADAM_B1, ADAM_B2, ADAM_EPS, ADAM_WD, ADAM_STEP = 0.001, 0.9, 0.999, 1e-08, 0.01, 10

VMEM_LIMIT = 48 * 2**20
ROW_TILE = 512
BMAT_TILE, CMAT_TILE = 2048, 512
SEG = 8
SEG_LEN = L // SEG
MESH_AXES = ("x", "y", "c")


def _cparams():
    return pltpu.CompilerParams(vmem_limit_bytes=VMEM_LIMIT)


def _dg(a, b, ca, cb):
    return lax.dot_general(a.astype(BF16), b.astype(BF16), (((ca,), (cb,)), ((), ())), preferred_element_type=F32)


@jax.custom_vjp
def mm_nn(a, b):
    return _dg(a, b, 1, 0)


mm_nn.defvjp(lambda a, b: (_dg(a, b, 1, 0), (a, b)), lambda res, g: (_dg(g, res[1], 1, 1), _dg(res[0], g, 0, 0)))


@jax.custom_vjp
def mm_nt(a, b):
    return _dg(a, b, 1, 1)


mm_nt.defvjp(lambda a, b: (_dg(a, b, 1, 1), (a, b)), lambda res, g: (_dg(g, res[1], 1, 0), _dg(g, res[0], 0, 0)))


@functools.partial(jax.custom_vjp, nondiff_argnums=(1,))
def lane_roll(x, shift):
    return pltpu.roll(x, shift, 1)


lane_roll.defvjp(lambda x, shift: (pltpu.roll(x, shift, 1), None),
                 lambda shift, _, g: (pltpu.roll(g, (128 - shift) % 128, 1),))


def rms(x, g):
    return x * lax.rsqrt(jnp.mean(x * x, axis=-1, keepdims=True) + EPS) * g


@jax.custom_vjp
def softmax_rows(s):
    e = jnp.exp(s - jnp.max(s, axis=-1, keepdims=True))
    return e / jnp.sum(e, axis=-1, keepdims=True)


def _softmax_rows_fwd(s):
    p = softmax_rows(s)
    return p, p


def _softmax_rows_bwd(p, g):
    return (p * (g - jnp.sum(g * p, axis=-1, keepdims=True)),)


softmax_rows.defvjp(_softmax_rows_fwd, _softmax_rows_bwd)


def silu(x):
    return x * jax.nn.sigmoid(x)


ANY = pl.BlockSpec(memory_space=pl.ANY)
MESH_ID = pl.DeviceIdType.MESH


def _dma_sems(n):
    return [pltpu.SemaphoreType.DMA((n,)), pltpu.SemaphoreType.DMA((n,))]


class Plan:
    def __init__(self, operands, out_shape, aliases, n_sems, copies):
        self.operands, self.out_shape, self.aliases, self.n_sems, self.copies = list(operands), list(out_shape), aliases, n_sems, copies
        self.results = None


_SCHEDULE_BEHIND = []


def schedule_behind(token):
    _SCHEDULE_BEHIND.append(token)


def hosted_call(kern, *, name, grid, in_specs, out_specs, out_shape, operands, scratch_shapes=(), aliases=None, cparams=None, plans=(), deps=()):
    n_in, n_out, n_scr = len(in_specs), len(out_specs), len(scratch_shapes)
    p_in, p_out = [len(p.operands) for p in plans], [len(p.out_shape) for p in plans]
    deps = tuple(deps) + tuple(_SCHEDULE_BEHIND)
    _SCHEDULE_BEHIND.clear()
    all_aliases = dict(aliases or {})
    in_off, out_off = n_in, n_out
    for p, ni, no in zip(plans, p_in, p_out):
        all_aliases.update({in_off + i: out_off + o for i, o in p.aliases.items()})
        in_off, out_off = in_off + ni, out_off + no

    def body(*refs):
        pos, pins, pouts = n_in, [], []
        for ni in p_in:
            pins.append(refs[pos:pos + ni])
            pos += ni
        pos += len(deps)
        main_out = refs[pos:pos + n_out]
        pos += n_out
        for no in p_out:
            pouts.append(refs[pos:pos + no])
            pos += no
        main_scr = refs[pos:pos + n_scr]
        pos += n_scr
        if plans:
            ids = [pl.program_id(ax) for ax in range(len(grid))]
            first = functools.reduce(jnp.logical_and, [i == 0 for i in ids])
            last = functools.reduce(jnp.logical_and, [i == g - 1 for i, g in zip(ids, grid)])
            copies = [p.copies(pins[k], pouts[k], refs[pos + 2 * k], refs[pos + 2 * k + 1]) for k, p in enumerate(plans)]

            @pl.when(first)
            def _():
                for sends, _ in copies:
                    for cp in sends:
                        cp.start()

        kern(*refs[:n_in], *main_out, *main_scr)
        if plans:
            @pl.when(last)
            def _():
                for sends, recvs in copies:
                    for cp in recvs:
                        cp.wait_recv()
                    for cp in sends:
                        cp.wait_send()

    res = pl.pallas_call(body, grid=grid, in_specs=list(in_specs) + [ANY] * (sum(p_in) + len(deps)),
                         out_specs=list(out_specs) + [ANY] * sum(p_out), out_shape=list(out_shape) + [s for p in plans for s in p.out_shape],
                         scratch_shapes=list(scratch_shapes) + [s for p in plans for s in _dma_sems(p.n_sems)],
                         input_output_aliases=all_aliases, name=name, compiler_params=cparams or _cparams())(
        *operands, *[a for p in plans for a in p.operands], *deps)
    pos = n_out
    for p, no in zip(plans, p_out):
        p.results = list(res[pos:pos + no])
        pos += no
    return list(res[:n_out])


def _wide(v):
    return v.astype(F32) if v.dtype == BF16 else v


def stage(name, fn, grid, ins, outs):
    n_in = len(ins)

    def kern(*refs):
        res = fn(*[_wide(r[...]) for r in refs[:n_in]])
        for r, v in zip(refs[n_in:], res):
            r[...] = v.astype(r.dtype)

    return hosted_call(kern, name=name, grid=grid, in_specs=[s for _, s in ins], out_specs=[s for _, s in outs],
                       out_shape=[sd for sd, _ in outs], operands=[a for a, _ in ins])


def stage_bwd(name, fn, grid, ins, cts, diffs, plans=()):
    n_in, n_ct = len(ins), len(cts)
    didx = [i for i, d in enumerate(diffs) if d is not None]
    opts = {i: (diffs[i][3] if len(diffs[i]) > 3 else {}) for i in didx if diffs[i][0] == "row"}
    adds = [(i, opts[i]["add"]) for i in opts if "add" in opts[i]]
    intos = [(i, opts[i]["into"]) for i in opts if "into" in opts[i]]
    n_add, n_into = len(adds), len(intos)
    add_pos = {i: n_in + n_ct + k for k, (i, _) in enumerate(adds)}
    n_extra = n_in + n_ct + n_add + n_into

    def kern(*refs):
        vals = [_wide(r[...]) for r in refs[:n_in]]

        def f(*dv):
            full = list(vals)
            for i, v in zip(didx, dv):
                full[i] = v
            return fn(*full)

        _, vjp = jax.vjp(f, *[vals[i].astype(F32) for i in didx])
        gs = vjp(tuple(c[...].astype(F32) for c in refs[n_in:n_in + n_ct]))
        for o_ref, i, g in zip(refs[n_extra:], didx, gs):
            if diffs[i][0] == "row":
                if i in add_pos:
                    g = g + refs[add_pos[i]][...].astype(F32)
                o_ref[...] = g.astype(o_ref.dtype)
            else:
                first = functools.reduce(jnp.logical_and, [pl.program_id(ax) == 0 for ax in diffs[i][1]])

                @pl.when(first)
                def _():
                    o_ref[...] = g

                @pl.when(jnp.logical_not(first))
                def _():
                    o_ref[...] += g

    out_shape, out_specs = [], []
    for i in didx:
        if diffs[i][0] == "row":
            out_shape.append(diffs[i][1])
            out_specs.append(diffs[i][2])
        else:
            out_shape.append(SDS(ins[i][0].shape, F32))
            out_specs.append(ins[i][1])
    aliases = {n_in + n_ct + n_add + k: didx.index(i) for k, (i, _) in enumerate(intos)}
    in_specs = [s for _, s in ins] + [s for _, s in cts] + [s for _, (_, s) in adds] + [ANY] * n_into
    operands = [a for a, _ in ins] + [a for a, _ in cts] + [a for _, (a, _) in adds] + [a for _, a in intos]
    return hosted_call(kern, name=name, grid=grid, in_specs=in_specs, out_specs=out_specs, out_shape=out_shape, operands=operands,
                       aliases=aliases, plans=plans)


def rspec(tl, w, cb=0):
    return pl.BlockSpec((tl, w), lambda i: (i, cb))


def cspec(shape):
    return pl.BlockSpec(shape, lambda i: (0,) * len(shape))


def row_fwd(name, fn, rows, tl, row_ins, consts, outs):
    ins = [(a, rspec(tl, w, cb)) for a, w, cb in row_ins] + [(a, cspec(a.shape)) for a in consts]
    return stage(name, fn, (rows // tl,), ins, [(SDS((rows, w), dt), rspec(tl, w)) for w, dt in outs])


def row_bwd(name, fn, rows, tl, row_ins, consts, cts, row_diff, const_diff, plans=()):
    ins = [(a, rspec(tl, w, cb)) for a, w, cb in row_ins] + [(a, cspec(a.shape)) for a in consts]
    diffs = []
    for (a, w, cb), d in zip(row_ins, row_diff):
        if not d:
            diffs.append(None)
            continue
        d = d if isinstance(d, dict) else {}
        opts = {}
        if "add" in d:
            opts["add"] = (d["add"][0], rspec(tl, d["add"][1], d["add"][2]))
        if d.get("into") is not None:
            opts["into"] = d["into"]
        diffs.append(("row", SDS((rows, d.get("cols", w)), d.get("dtype", F32)), rspec(tl, w, d.get("cb", 0)), opts))
    diffs += [("acc", (0,)) if d else None for d in const_diff]
    return stage_bwd(name, fn, (rows // tl,), ins, [(a, rspec(tl, w, cb)) for a, w, cb in cts], diffs, plans=plans)


MATMUL_VMEM = 36 * 2**20
ADAMW_VMEM = 28 * 2**20


class Sharded:
    def __init__(self, arr, kind, roff, rows):
        self.arr, self.kind, self.roff, self.rows, self.n = arr, kind, roff, rows, arr.shape[2]
        self.shape = (rows, 4 * self.n) if kind == "col" else (4 * rows, self.n)

    def fits(self, t0, t1):
        return self.roff % t0 == 0 and self.rows % t0 == 0 and self.n % t1 == 0

    def spec(self, t0, t1, bidx):
        assert self.fits(t0, t1), (self.kind, self.roff, self.rows, self.n, t0, t1)
        r0 = self.roff // t0
        if self.kind == "col":
            per = self.n // t1
            return pl.BlockSpec((None, t0, t1), lambda *g: (bidx(*g)[1] // per, r0 + bidx(*g)[0], bidx(*g)[1] % per))
        per = self.rows // t0
        return pl.BlockSpec((None, t0, t1), lambda *g: (bidx(*g)[0] // per, r0 + bidx(*g)[0] % per, bidx(*g)[1]))


def _matmul_over_shards(name, a, b, mode, out_dtype, add, plans):
    m, k_dim = a.shape
    per = k_dim // 4
    n = b.n if mode == "nn" else b.rows
    assert k_dim % 4 == 0 and (b.rows == per and b.roff % per == 0 if mode == "nn" else b.n == per), (name, a.shape, mode)
    a_bytes, o_bytes = jnp.dtype(a.dtype).itemsize, jnp.dtype(out_dtype).itemsize

    def vmem(tm, tn):
        return 2 * (tm * k_dim * a_bytes + k_dim * tn * 2 + tm * tn * (o_bytes + (4 if add is not None else 0))) + 4 * tm * tn

    tiles = [(tm, tn) for tm in (2048, 1024, 512, 256) for tn in (1024, 512, 256, 128)
             if m % tm == 0 and n % tn == 0 and (mode == "nn" or b.roff % tn == 0) and vmem(tm, tn) <= MATMUL_VMEM]
    tm, tn = max(tiles, key=lambda t: (t[0] * t[1], t[0]))
    if mode == "nn":
        b_specs = [pl.BlockSpec((None, per, tn), lambda i, j, s=s: (s, b.roff // per, j)) for s in range(4)]
    else:
        b_specs = [pl.BlockSpec((None, tn, per), lambda i, j, s=s: (s, b.roff // tn + j, 0)) for s in range(4)]
    o_spec = pl.BlockSpec((tm, tn), lambda i, j: (i, j))
    cb = 0 if mode == "nn" else 1

    def kern(*refs):
        r = _dg(refs[0][:, 0:per], refs[1][...], 1, cb)
        for s in range(1, 4):
            r = r + _dg(refs[0][:, s * per:(s + 1) * per], refs[1 + s][...], 1, cb)
        if add is not None:
            r = r + refs[5][...]
        refs[-1][...] = r.astype(refs[-1].dtype)

    ins, specs = [a] + [b.arr] * 4, [pl.BlockSpec((tm, k_dim), lambda i, j: (i, 0))] + b_specs
    if add is not None:
        ins.append(add)
        specs.append(o_spec)
    return hosted_call(kern, name=name, grid=(m // tm, n // tn), in_specs=specs, out_specs=[o_spec], out_shape=[SDS((m, n), out_dtype)],
                       operands=ins, plans=plans)[0]


def matmul(name, a, b, mode, out_dtype=BF16, add=None, out=None, into=None, plans=()):
    if mode == "tn":
        k_dim, m = a.shape
    else:
        m, k_dim = a.shape
    n = b.shape[0] if mode == "nt" else b.shape[1]
    if isinstance(b, Sharded) and out is None and (mode, b.kind) in (("nn", "row"), ("nt", "col")):
        return _matmul_over_shards(name, a, b, mode, out_dtype, add, plans)
    b_fit = b.fits if isinstance(b, Sharded) else (lambda t0, t1: True)
    o_fit = out.fits if out is not None else (lambda t0, t1: True)
    a_bytes, b_bytes = jnp.dtype(a.dtype).itemsize, jnp.dtype(b.arr.dtype if isinstance(b, Sharded) else b.dtype).itemsize
    o_bytes = jnp.dtype(out_dtype if out is None else out.arr.dtype).itemsize

    def vmem(tm, tn):
        return 2 * (tm * k_dim * a_bytes + k_dim * tn * b_bytes + tm * tn * (o_bytes + (4 if add is not None else 0))) + 4 * tm * tn

    tiles = [(tm, tn) for tm in (2048, 1024, 512, 256, 128) for tn in (1024, 768, 512, 384, 256, 128)
             if m % tm == 0 and n % tn == 0 and (b_fit(tn, k_dim) if mode == "nt" else b_fit(k_dim, tn)) and o_fit(tm, tn)
             and vmem(tm, tn) <= MATMUL_VMEM]
    tm, tn = max(tiles, key=lambda t: t[0] * t[1])
    a_spec = pl.BlockSpec((k_dim, tm), lambda i, j: (0, i)) if mode == "tn" else pl.BlockSpec((tm, k_dim), lambda i, j: (i, 0))
    if isinstance(b, Sharded):
        b_spec = b.spec(tn, k_dim, lambda i, j: (j, 0)) if mode == "nt" else b.spec(k_dim, tn, lambda i, j: (0, j))
        b = b.arr
    else:
        b_spec = pl.BlockSpec((tn, k_dim), lambda i, j: (j, 0)) if mode == "nt" else pl.BlockSpec((k_dim, tn), lambda i, j: (0, j))
    o_spec = pl.BlockSpec((tm, tn), lambda i, j: (i, j))
    out_spec, out_shape = (o_spec, SDS((m, n), out_dtype)) if out is None else (out.spec(tm, tn, lambda i, j: (i, j)), out.arr)
    ca, cb = {"nn": (1, 0), "nt": (1, 1), "tn": (0, 0)}[mode]
    n_in = 2 + (add is not None)

    def kern(*refs):
        r = _dg(refs[0][...], refs[1][...], ca, cb)
        if add is not None:
            r = r + refs[2][...]
        refs[-1][...] = r.astype(refs[-1].dtype)

    ins, specs = [a, b], [a_spec, b_spec]
    if add is not None:
        ins.append(add)
        specs.append(o_spec)
    if into is not None:
        ins.append(into)
        specs.append(ANY)
    return hosted_call(kern, name=name, grid=(m // tm, n // tn), in_specs=specs, out_specs=[out_spec], out_shape=[out_shape],
                       operands=ins, aliases={} if into is None else {n_in: 0}, plans=plans)[0]


SCAN_UNROLL = 8


def _cmul(ar, ai, br, bi):
    return ar * br - ai * bi, ar * bi + ai * br


def _sub_shift(x, down):
    row = lax.broadcasted_iota(jnp.int32, x.shape, 0)
    if down:
        return jnp.where(row == 0, 0.0, pltpu.roll(x, 1, 0))
    return jnp.where(row == SEG - 1, 0.0, pltpu.roll(x, SEG - 1, 0))


def _pow_seg_len(ar, ai):
    for _ in range(int(math.log2(SEG_LEN))):
        ar, ai = _cmul(ar, ai, ar, ai)
    return ar, ai


def _scan_in_place(sr, si, a_re, a_im):
    lanes = sr.shape[1]
    ar = jnp.broadcast_to(a_re, (SEG, lanes))
    ai = jnp.broadcast_to(a_im, (SEG, lanes))
    zero = jnp.zeros((SEG, lanes), F32)

    def local(i, carry):
        rows = pl.ds(pl.multiple_of(i * SEG, SEG), SEG)
        mr, mi = _cmul(ar, ai, carry[0], carry[1])
        nr, ni = mr + sr[rows, :], mi + si[rows, :]
        sr[rows, :] = nr
        si[rows, :] = ni
        return nr, ni

    fr, fi = lax.fori_loop(0, SEG_LEN, local, (zero, zero), unroll=SCAN_UNROLL)
    pr, pi = _pow_seg_len(ar, ai)
    ir, ii = zero, zero
    for _ in range(SEG - 1):
        mr, mi = _cmul(pr, pi, ir, ii)
        ir, ii = _sub_shift(mr + fr, True), _sub_shift(mi + fi, True)

    def carry_in(i, pw):
        rows = pl.ds(pl.multiple_of(i * SEG, SEG), SEG)
        cr, ci = _cmul(pw[0], pw[1], ir, ii)
        sr[rows, :] += cr
        si[rows, :] += ci
        return _cmul(pw[0], pw[1], ar, ai)

    lax.fori_loop(0, SEG_LEN, carry_in, (ar, ai), unroll=SCAN_UNROLL)


S5_LANES = 8 * SP
S5_BLOCKS = SN // S5_LANES


def _s5_specs(n=1):
    u_spec = pl.BlockSpec((L, n * 8 * SC), lambda j: (0, j))
    s_spec = pl.BlockSpec((L, n * S5_LANES), lambda j: (0, j))
    wb_spec = pl.BlockSpec((n * S5_LANES, 8 * SC), lambda j: (j, 0))
    wc_spec = pl.BlockSpec((n * 8 * SC, S5_LANES), lambda j: (j, 0))
    a_spec = pl.BlockSpec((1, n * S5_LANES), lambda j: (0, j))
    d_spec = pl.BlockSpec((1, n * 8 * SC), lambda j: (0, j))
    return u_spec, s_spec, wb_spec, wc_spec, a_spec, d_spec


S5_FWD_BLOCKS = 2


def s5_forward(proj, wb_re, wb_im, wc_re, wc_im, a_re, a_im, d, plans=()):
    n = S5_FWD_BLOCKS
    cols, lanes = 8 * SC, S5_LANES

    def kern(u_ref, wbr, wbi, wcr, wci, ar, ai, d_ref, sr_out, si_out, g_ref, sr, si):
        u = _wide(u_ref[...])
        for b in range(n):
            sr[:, b * lanes:(b + 1) * lanes], si[:, b * lanes:(b + 1) * lanes] = fn_s5_bu(
                u[:, b * cols:(b + 1) * cols], wbr[b * lanes:(b + 1) * lanes, :], wbi[b * lanes:(b + 1) * lanes, :])
        _scan_in_place(sr, si, ar[...], ai[...])
        for b in range(n):
            g_ref[:, b * cols:(b + 1) * cols] = fn_s5_out(
                sr[:, b * lanes:(b + 1) * lanes], si[:, b * lanes:(b + 1) * lanes], u[:, b * cols:(b + 1) * cols],
                d_ref[:, b * cols:(b + 1) * cols], wcr[b * cols:(b + 1) * cols, :], wci[b * cols:(b + 1) * cols, :])[0].astype(g_ref.dtype)
        sr_out[...] = sr[...].astype(sr_out.dtype)
        si_out[...] = si[...].astype(si_out.dtype)

    u_spec, s_spec, wb_spec, wc_spec, a_spec, d_spec = _s5_specs(n)
    return hosted_call(kern, name="s5_forward", grid=(S5_BLOCKS // n,), in_specs=[u_spec, wb_spec, wb_spec, wc_spec, wc_spec, a_spec, a_spec, d_spec],
                       out_specs=[s_spec, s_spec, u_spec], out_shape=[SDS((L, SN), BF16)] * 2 + [SDS((L, PW), BF16)],
                       operands=[proj, wb_re, wb_im, wc_re, wc_im, a_re, a_im, d], scratch_shapes=[pltpu.VMEM((L, n * lanes), F32)] * 2,
                       plans=plans)


def _adjoint_scan_in_place(lr, li, sr, si, a_re, a_im):
    lanes = lr.shape[1]
    ar = jnp.broadcast_to(a_re, (SEG, lanes))
    ai = -jnp.broadcast_to(a_im, (SEG, lanes))
    zero = jnp.zeros((SEG, lanes), F32)

    def local(k, carry):
        i = SEG_LEN - 1 - k
        rows = pl.ds(pl.multiple_of(i * SEG, SEG), SEG)
        mr, mi = _cmul(ar, ai, carry[0], carry[1])
        nr, ni = mr + lr[rows, :], mi + li[rows, :]
        lr[rows, :] = nr
        li[rows, :] = ni
        return nr, ni

    fr, fi = lax.fori_loop(0, SEG_LEN, local, (zero, zero), unroll=SCAN_UNROLL)
    pr, pi = _pow_seg_len(ar, ai)
    ir, ii = zero, zero
    for _ in range(SEG - 1):
        mr, mi = _cmul(pr, pi, ir, ii)
        ir, ii = _sub_shift(mr + fr, False), _sub_shift(mi + fi, False)

    def fix(rows, pw):
        cr, ci = _cmul(pw[0], pw[1], ir, ii)
        tr, ti = lr[rows, :] + cr, li[rows, :] + ci
        lr[rows, :] = tr
        li[rows, :] = ti
        return tr, ti

    def grad_a(tr, ti, spr, spi, acc):
        return acc[0] + tr * spr + ti * spi, acc[1] + ti * spr - tr * spi

    def carry_in(k, c):
        i = SEG_LEN - 1 - k
        rows = pl.ds(pl.multiple_of(i * SEG, SEG), SEG)
        prev = pl.ds(pl.multiple_of((i - 1) * SEG, SEG), SEG)
        tr, ti = fix(rows, (c[0], c[1]))
        acc = grad_a(tr, ti, sr[prev, :], si[prev, :], (c[2], c[3]))
        nr, ni = _cmul(c[0], c[1], ar, ai)
        return nr, ni, acc[0], acc[1]

    pwr, pwi, accr, acci = lax.fori_loop(0, SEG_LEN - 1, carry_in, (ar, ai, zero, zero), unroll=5)
    tr, ti = fix(pl.ds(0, SEG), (pwr, pwi))
    last = pl.ds((SEG_LEN - 1) * SEG, SEG)
    accr, acci = grad_a(tr, ti, _sub_shift(sr[last, :], True), _sub_shift(si[last, :], True), (accr, acci))
    return jnp.sum(accr, axis=0, keepdims=True), jnp.sum(acci, axis=0, keepdims=True)


S5_BWD_VMEM = 58 * 2**20


def s5_backward(dg, proj, s_re, s_im, wb_re, wb_im, wc_re, wc_im, a_re, a_im, d, dproj, plans=()):
    def kern(dg_ref, u_ref, sr_in, si_in, wbr, wbi, wcr, wci, ar, ai, d_ref, _, du_ref, dd_ref, dwcr, dwci, dwbr, dwbi, dar, dai, lr, li, sr, si):
        u = _wide(u_ref[...])
        sr[...], si[...] = _wide(sr_in[...]), _wide(si_in[...])
        _, vjp_out = jax.vjp(fn_s5_out, sr[...], si[...], u, d_ref[...], wcr[...], wci[...])
        lr[...], li[...], du_out, dd_ref[...], dwcr[...], dwci[...] = vjp_out((_wide(dg_ref[...]),))
        dar[...], dai[...] = _adjoint_scan_in_place(lr, li, sr, si, ar[...], ai[...])
        _, vjp_in = jax.vjp(fn_s5_bu, u, wbr[...], wbi[...])
        du_in, dwbr[...], dwbi[...] = vjp_in((lr[...], li[...]))
        du_ref[...] = (du_out + du_in).astype(du_ref.dtype)

    u_spec, s_spec, wb_spec, wc_spec, a_spec, d_spec = _s5_specs()
    outs = [(SDS(dproj.shape, dproj.dtype), u_spec), (SDS(d.shape, F32), d_spec), (SDS(wc_re.shape, F32), wc_spec), (SDS(wc_im.shape, F32), wc_spec),
            (SDS(wb_re.shape, F32), wb_spec), (SDS(wb_im.shape, F32), wb_spec), (SDS(a_re.shape, F32), a_spec), (SDS(a_im.shape, F32), a_spec)]
    return hosted_call(kern, name="s5_backward", grid=(S5_BLOCKS,),
                       in_specs=[u_spec, u_spec, s_spec, s_spec, wb_spec, wb_spec, wc_spec, wc_spec, a_spec, a_spec, d_spec, ANY],
                       out_specs=[sp for _, sp in outs], out_shape=[sd for sd, _ in outs], aliases={11: 0},
                       operands=[dg, proj, s_re, s_im, wb_re, wb_im, wc_re, wc_im, a_re, a_im, d, dproj],
                       scratch_shapes=[pltpu.VMEM((L, S5_LANES), F32)] * 4,
                       cparams=pltpu.CompilerParams(vmem_limit_bytes=S5_BWD_VMEM), plans=plans)


def fn_rms(x, g):
    return (rms(x, g),)


def fn_s5_disc(lre, lim, ls):
    step = jnp.exp(ls)
    e = jnp.exp(lre * step)
    a_re, a_im = e * jnp.cos(lim * step), e * jnp.sin(lim * step)
    den = lre * lre + lim * lim
    nr, ni = a_re - 1.0, a_im
    return a_re, a_im, (nr * lre + ni * lim) / den, (ni * lre - nr * lim) / den


def _group_mask(rows, cols, row_div, col_div):
    r = lax.broadcasted_iota(jnp.int32, (rows, cols), 0) // row_div % 8
    c = lax.broadcasted_iota(jnp.int32, (rows, cols), 1) // col_div
    return r == c


def _spread(x, mask):
    w = x.shape[1]
    copy = (lax.broadcasted_iota(jnp.int32, (w, 8 * w), 1) % w == lax.broadcasted_iota(jnp.int32, (w, 8 * w), 0)).astype(F32)
    return jnp.where(mask, jnp.dot(x, copy, precision=lax.Precision.HIGHEST, preferred_element_type=F32), 0.0)


def fn_s5_bmat(b_re, b_im, coef_re, coef_im):
    mask = _group_mask(b_re.shape[0], 8 * SC, SP, SC)
    return _spread(coef_re * b_re - coef_im * b_im, mask), _spread(coef_re * b_im + coef_im * b_re, mask)


def fn_s5_cmat(c_re, c_im):
    mask = _group_mask(c_re.shape[0], 8 * SP, SC, SP)
    return _spread(c_re, mask), _spread(c_im, mask)


def fn_s5_bu(u, wb_re, wb_im):
    return mm_nt(u, wb_re), mm_nt(u, wb_im)


def fn_s5_out(sr, si, u, d, wc_re, wc_im):
    y = mm_nt(sr, wc_re) - mm_nt(si, wc_im) + d * u
    return (jax.nn.gelu(y),)


def fn_merge_glu(z, mo, gate):
    yg = z[:, :PW] * jax.nn.sigmoid(z[:, PW:])
    return (jnp.concatenate([yg, mo], axis=1) * silu(gate),)


def fn_merge(prim, mo, gate):
    return (jnp.concatenate([prim, mo], axis=1) * silu(gate),)


def fn_mem_k(kv, g):
    return (jnp.concatenate([rms(kv[:, h * XHD:(h + 1) * XHD], g) for h in range(XH)], axis=1),)


def fn_mem_attn(xq, kn, v, g):
    outs = []
    for h in range(XH):
        sl = slice(h * XHD, (h + 1) * XHD)
        p = softmax_rows(mm_nt(rms(xq[:, sl], g), kn[:, sl]) * (XHD ** -0.5))
        outs.append(mm_nn(p, v[:, sl]))
    return (jnp.concatenate(outs, axis=1),)


def _half_rms(x, g):
    lo = lax.broadcasted_iota(jnp.int32, x.shape, 1) < ROPE
    x2 = x * x
    s_lo = jnp.sum(jnp.where(lo, x2, 0.0), axis=1, keepdims=True)
    s_hi = jnp.sum(jnp.where(lo, 0.0, x2), axis=1, keepdims=True)
    return x * lax.rsqrt(jnp.where(lo, s_lo, s_hi) / ROPE + EPS) * g


def _rope(x, cos2, sin_signed):
    first = lax.broadcasted_iota(jnp.int32, x.shape, 1) % ROPE < ROPE // 2
    return x * cos2 + jnp.where(first, lane_roll(x, 128 - ROPE // 2), lane_roll(x, ROPE // 2)) * sin_signed


def fn_mla_prep(q, kv, kr, cos2, sin_signed, qnn, knn, qrn, krn):
    lo = lax.broadcasted_iota(jnp.int32, kr.shape, 1) < ROPE
    kr_pad = jnp.where(lo, _rope(_half_rms(kr, krn), cos2, sin_signed), 0.0)
    qf, kf, vs = [], [], []
    for m in range(MH // 2):
        pair = _rope(_half_rms(q[:, MH * NOPE + 128 * m:MH * NOPE + 128 * (m + 1)], qrn), cos2, sin_signed)
        for h, rope_h in ((2 * m, pair), (2 * m + 1, lane_roll(pair, ROPE))):
            qf.append(jnp.concatenate([rms(q[:, NOPE * h:NOPE * (h + 1)], qnn), jnp.where(lo, rope_h, 0.0)], axis=1))
    for h in range(MH):
        kf.append(jnp.concatenate([rms(kv[:, 256 * h:256 * h + NOPE], knn), kr_pad], axis=1))
        vs.append(kv[:, 256 * h + NOPE:256 * (h + 1)])
    return jnp.stack(qf), jnp.stack(kf), jnp.stack(vs)


ATT_TQ = 512


def _attn_scores(q, kf):
    tq = q.shape[0]
    scale = (NOPE + ROPE) ** -0.5
    own = _dg(q, kf[-tq:], 1, 1) * scale
    own = jnp.where(lax.broadcasted_iota(jnp.int32, own.shape, 1) <= lax.broadcasted_iota(jnp.int32, own.shape, 0), own, jnp.finfo(F32).min)
    return own if kf.shape[0] == tq else jnp.concatenate([_dg(q, kf[:-tq], 1, 1) * scale, own], axis=1)


ATT_FWD_HEADS, ATT_BWD_HEADS = 4, 2


def _attn_specs(heads):
    q_spec = pl.BlockSpec((heads, ATT_TQ, 256), lambda h, i: (h, i, 0))
    k_spec = pl.BlockSpec((heads, L, 256), lambda h, i: (h, 0, 0))
    v_spec = pl.BlockSpec((heads, L, 128), lambda h, i: (h, 0, 0))
    o_spec = pl.BlockSpec((ATT_TQ, heads * 128), lambda h, i: (i, h))
    lse_spec = pl.BlockSpec((heads, ATT_TQ, 1), lambda h, i: (h, i, 0))
    return q_spec, k_spec, v_spec, o_spec, lse_spec


def _attn_branches(heads, body):
    i = pl.program_id(1)
    for t in range(L // ATT_TQ):
        @pl.when(i == t)
        def _(t=t):
            for hh in range(heads):
                body(hh, slice(hh * 128, (hh + 1) * 128), (t + 1) * ATT_TQ)


def causal_attn(qf, kf, vh):
    def kern(q_ref, k_ref, v_ref, o_ref, lse_ref):
        def body(hh, lanes, keys):
            s = _attn_scores(q_ref[hh], k_ref[hh, :keys, :])
            m = jnp.max(s, axis=-1, keepdims=True)
            e = jnp.exp(s - m)
            total = jnp.sum(e, axis=-1, keepdims=True)
            o_ref[:, lanes] = (_dg(e, v_ref[hh, :keys, :], 1, 0) / total).astype(o_ref.dtype)
            lse_ref[hh] = m + jnp.log(total)

        _attn_branches(ATT_FWD_HEADS, body)

    q_spec, k_spec, v_spec, o_spec, lse_spec = _attn_specs(ATT_FWD_HEADS)
    return pl.pallas_call(kern, grid=(MH // ATT_FWD_HEADS, L // ATT_TQ), in_specs=[q_spec, k_spec, v_spec], out_specs=[o_spec, lse_spec],
                          out_shape=[SDS((L, MH * VD), F32), SDS((MH, L, 1), F32)], name="l1_attn", compiler_params=_cparams())(qf, kf, vh)


def causal_attn_bwd(qf, kf, vh, out, lse, dout):
    scale = (NOPE + ROPE) ** -0.5

    def kern(q_ref, k_ref, v_ref, o_ref, lse_ref, do_ref, dq_ref, dk_ref, dv_ref):
        @pl.when(pl.program_id(1) == 0)
        def _():
            dk_ref[...] = jnp.zeros_like(dk_ref)
            dv_ref[...] = jnp.zeros_like(dv_ref)

        def body(hh, lanes, keys):
            q, k, v, do = q_ref[hh], k_ref[hh, :keys, :], v_ref[hh, :keys, :], do_ref[:, lanes]
            p = jnp.exp(_attn_scores(q, k) - lse_ref[hh])
            delta = jnp.sum(do * _wide(o_ref[:, lanes]), axis=-1, keepdims=True)
            dv_ref[hh, :keys, :] += _dg(p, do, 0, 0)
            ds = p * (_dg(do, v, 1, 1) - delta) * scale
            dq_ref[hh] = _dg(ds, k, 1, 0)
            dk_ref[hh, :keys, :] += _dg(ds, q, 0, 0)

        _attn_branches(ATT_BWD_HEADS, body)

    q_spec, k_spec, v_spec, o_spec, lse_spec = _attn_specs(ATT_BWD_HEADS)
    return pl.pallas_call(kern, grid=(MH // ATT_BWD_HEADS, L // ATT_TQ), in_specs=[q_spec, k_spec, v_spec, o_spec, lse_spec, o_spec],
                          out_specs=[q_spec, k_spec, v_spec], out_shape=[SDS(qf.shape, F32), SDS(kf.shape, F32), SDS(vh.shape, F32)],
                          name="l1_attn_bwd", compiler_params=_cparams())(qf, kf, vh, out, lse, dout)


def loss_and_grad(y, target, tl=512):
    def kern(y_ref, t_ref, dy_ref, loss_ref):
        d = y_ref[...] - t_ref[...]
        dy_ref[...] = d / D

        @pl.when(pl.program_id(0) == 0)
        def _():
            loss_ref[...] = jnp.zeros_like(loss_ref)

        loss_ref[...] += 0.5 * jnp.sum(jnp.sum(d * d, axis=1, keepdims=True), axis=0, keepdims=True) / D

    return pl.pallas_call(kern, grid=(L // tl,), in_specs=[rspec(tl, D), rspec(tl, D)], out_specs=[rspec(tl, D), cspec((1, 1))],
                          out_shape=[SDS((L, D), F32), SDS((1, 1), F32)], name="loss", compiler_params=_cparams())(y, target)


def adamw(name, w, g, m, v):
    rows, cols = w.shape
    block_row_bytes = 7 * 2 * 4 * max(cols, 128)
    tr = _row_tile(rows, min(2048, ADAMW_VMEM // block_row_bytes // 8 * 8), 8)

    def kern(w_ref, g_ref, m_ref, v_ref, d_ref, nm_ref, nv_ref):
        gg = g_ref[...]
        nm = ADAM_B1 * m_ref[...] + (1.0 - ADAM_B1) * gg
        nv = ADAM_B2 * v_ref[...] + (1.0 - ADAM_B2) * jnp.square(gg)
        m_hat = nm / (1.0 - ADAM_B1 ** ADAM_STEP)
        v_hat = nv / (1.0 - ADAM_B2 ** ADAM_STEP)
        d_ref[...] = -ADAM_LR * (m_hat / (jnp.sqrt(v_hat) + ADAM_EPS) + ADAM_WD * w_ref[...])
        nm_ref[...] = nm
        nv_ref[...] = nv

    spec = rspec(tr, cols)
    return hosted_call(kern, name=name, grid=(rows // tr,), in_specs=[spec] * 4, out_specs=[spec] * 3,
                       out_shape=[SDS((rows, cols), F32)] * 3, operands=[w, g, m, v])


def _row_tile(rows, cap=512, unit=16):
    return max(t for t in range(unit, cap + 1, unit) if rows % t == 0)


def _place():
    x, y, c = lax.axis_index("x"), lax.axis_index("y"), lax.axis_index("c")
    return x, y, c, [(1 - x, y), (x, 1 - y), (1 - x, 1 - y)]


def _row_chunks(rows, n, dtype):
    unit = 32 // jnp.dtype(dtype).itemsize
    base, extra = divmod(rows // unit, n)
    out, start = [], 0
    for k in range(n):
        size = (base + (k < extra)) * unit
        if size:
            out.append((start, size))
            start += size
    assert start == rows, (rows, unit)
    return out


PIECE_BYTES = 1 << 20


def _pieces(shapes_dtypes, rows_of):
    out = []
    for b, (shape, dtype) in enumerate(shapes_dtypes):
        rows = rows_of(shape)
        n = max(1, min(4, rows * shape[-1] * jnp.dtype(dtype).itemsize // PIECE_BYTES))
        out += [(b, st, sz) for st, sz in _row_chunks(rows, n, dtype)]
    return out


def plan_gather_ici(shards):
    pieces = _pieces([(s.shape, s.dtype) for s in shards], lambda shape: shape[0] // 2)
    n = len(pieces)

    def copies(x_refs, out_refs, send_sems, recv_sems):
        x, y, c, chips = _place()
        mine = 2 * x + y

        def copy(j, k, chip, to, from_input):
            b, st, sz = pieces[k]
            rows_k = pl.ds(c * (x_refs[b].shape[0] // 2) + st, sz)
            dst = out_refs[b].at[chip, rows_k, :]
            return pltpu.make_async_remote_copy(src_ref=x_refs[b].at[rows_k, :] if from_input else dst, dst_ref=dst, send_sem=send_sems.at[j * n + k],
                                                recv_sem=recv_sems.at[j * n + k], device_id=to, device_id_type=MESH_ID)

        order = [(k, j, 2 * cx + cy, (cx, cy, c)) for k in range(n) for j, (cx, cy) in enumerate(chips)]
        return [copy(j, k, mine, to, True) for k, j, _, to in order], [copy(j, k, chip, to, False) for k, j, chip, to in order]

    return Plan(shards, [SDS((4,) + s.shape, s.dtype) for s in shards], {}, 3 * n, copies)


def plan_gather_pass(gathered):
    pieces = _pieces([(g.shape[1:], g.dtype) for g in gathered], lambda shape: shape[0] // 2)
    n = len(pieces)

    def copies(_, out_refs, send_sems, recv_sems):
        x, y, c, chips = _place()

        def copy(j, k, chip, cc):
            b, st, sz = pieces[k]
            rows_k = out_refs[b].at[chip, pl.ds(cc * (out_refs[b].shape[1] // 2) + st, sz), :]
            return pltpu.make_async_remote_copy(src_ref=rows_k, dst_ref=rows_k, send_sem=send_sems.at[j * n + k], recv_sem=recv_sems.at[j * n + k],
                                                device_id=(x, y, 1 - c), device_id_type=MESH_ID)

        order = [(k, j, 2 * cx + cy) for k in range(n) for j, (cx, cy) in enumerate(chips)]
        return [copy(j, k, chip, c) for k, j, chip in order], [copy(j, k, chip, 1 - c) for k, j, chip in order]

    return Plan(gathered, [SDS(g.shape, g.dtype) for g in gathered], {i: i for i in range(len(gathered))}, 3 * n, copies)


def plan_pair_exchange(gs):
    pieces = _pieces([(g.shape, g.dtype) for g in gs], lambda shape: shape[1] // 2)

    def copies(g_refs, got_refs, send_sems, recv_sems):
        x, y, c, _ = _place()
        swaps = [pltpu.make_async_remote_copy(src_ref=g_refs[b].at[:, pl.ds((1 - c) * (g_refs[b].shape[1] // 2) + st, sz), :],
                                              dst_ref=got_refs[b].at[:, pl.ds(st, sz), :], send_sem=send_sems.at[k], recv_sem=recv_sems.at[k],
                                              device_id=(x, y, 1 - c), device_id_type=MESH_ID)
                 for k, (b, st, sz) in enumerate(pieces)]
        return swaps, swaps

    return Plan(gs, [SDS((g.shape[0], g.shape[1] // 2, g.shape[2]), g.dtype) for g in gs], {}, len(pieces), copies)


def plan_chip_scatter(ps):
    pieces = _pieces([(p.shape, p.dtype) for p in ps], lambda shape: shape[1])
    n = len(pieces)

    def copies(p_refs, q_refs, send_sems, recv_sems):
        x, y, c, chips = _place()
        mine = 2 * x + y

        def copy(j, k, src_slot, dst_slot, to):
            b, st, sz = pieces[k]
            return pltpu.make_async_remote_copy(src_ref=p_refs[b].at[src_slot, pl.ds(st, sz), :], dst_ref=q_refs[b].at[dst_slot, pl.ds(st, sz), :],
                                                send_sem=send_sems.at[j * n + k], recv_sem=recv_sems.at[j * n + k], device_id=to,
                                                device_id_type=MESH_ID)

        order = [(k, j, 2 * cx + cy, (cx, cy, c)) for k in range(n) for j, (cx, cy) in enumerate(chips)]
        return [copy(j, k, chip, mine, to) for k, j, chip, to in order], [copy(j, k, mine, chip, to) for k, j, chip, to in order]

    return Plan(ps, [SDS(p.shape, p.dtype) for p in ps], {}, 3 * n, copies)


def plan_pair_join(bufs):
    pieces = _pieces([(b.shape, b.dtype) for b in bufs], lambda shape: shape[0] // 2)

    def copies(_, out_refs, send_sems, recv_sems):
        x, y, c, _ = _place()

        def copy(k, cc):
            b, st, sz = pieces[k]
            rows_k = out_refs[b].at[pl.ds(cc * (out_refs[b].shape[0] // 2) + st, sz), :]
            return pltpu.make_async_remote_copy(src_ref=rows_k, dst_ref=rows_k, send_sem=send_sems.at[k], recv_sem=recv_sems.at[k],
                                                device_id=(x, y, 1 - c), device_id_type=MESH_ID)

        return [copy(k, c) for k in range(len(pieces))], [copy(k, 1 - c) for k in range(len(pieces))]

    return Plan(bufs, [SDS(b.shape, b.dtype) for b in bufs], {i: i for i in range(len(bufs))}, len(pieces), copies)


def run_plan(name, plan):
    hosted_call(lambda: None, name=name, grid=(1,), in_specs=[], out_specs=[], out_shape=[], operands=[], plans=[plan])
    return plan.results


HBM = pl.BlockSpec(memory_space=pltpu.HBM)
SEMS = pl.BlockSpec(memory_space=pltpu.SEMAPHORE)
SPLIT_PARAMS = dict(has_side_effects=pltpu.SideEffectType.DATAFLOW_SIDE_EFFECTING)


def _plan_buffers(plan):
    in_place = {o: i for i, o in plan.aliases.items()}
    bufs = [pltpu.with_memory_space_constraint(a, pltpu.HBM) for a in plan.operands]
    where = []
    for o, sd in enumerate(plan.out_shape):
        if o in in_place:
            where.append(in_place[o])
        else:
            where.append(len(bufs))
            bufs.append(pltpu.with_memory_space_constraint(lax.empty(sd.shape, sd.dtype), pltpu.HBM))
    return bufs, where


def split_start(name, plans):
    layout = [_plan_buffers(p) for p in plans]
    counts = [len(b) for b, _ in layout]
    n_buf = sum(counts)

    def body(*refs):
        sems, token = refs[n_buf:n_buf + 2 * len(plans)], refs[-1]
        pos = 0
        for k, (p, (_, where)) in enumerate(zip(plans, layout)):
            mine = refs[pos:pos + counts[k]]
            pos += counts[k]
            sends, _ = p.copies(mine[:len(p.operands)], [mine[w] for w in where], sems[2 * k], sems[2 * k + 1])
            for cp in sends:
                cp.start()
        token[...] = jnp.zeros_like(token)

    bufs = [b for bs, _ in layout for b in bs]
    res = pl.pallas_call(
        body, name=name, in_specs=[HBM] * n_buf,
        out_specs=[SEMS] * (2 * len(plans)) + [HBM] * n_buf + [pl.BlockSpec(memory_space=pltpu.VMEM)],
        out_shape=[pltpu.SemaphoreType.DMA((p.n_sems,)) for p in plans for _ in range(2)] + [pltpu.HBM(b.shape, b.dtype) for b in bufs]
        + [SDS((8, 128), F32)],
        input_output_aliases={i: 2 * len(plans) + i for i in range(n_buf)}, compiler_params=pltpu.CompilerParams(**SPLIT_PARAMS))(*bufs)
    pos = 2 * len(plans)
    for k, p in enumerate(plans):
        p.in_flight = (res[2 * k], res[2 * k + 1], list(res[pos:pos + counts[k]]), layout[k][1])
        pos += counts[k]
    return res[-1]


def split_wait(name, plan, after):
    send_sems, recv_sems, bufs, where = plan.in_flight
    n_buf = len(bufs)

    def body(*refs):
        mine = refs[:n_buf]
        sends, recvs = plan.copies(mine[:len(plan.operands)], [mine[w] for w in where], refs[n_buf], refs[n_buf + 1])
        for cp in recvs:
            cp.wait_recv()
        for cp in sends:
            cp.wait_send()

    res = pl.pallas_call(body, name=name, in_specs=[HBM] * n_buf + [SEMS, SEMS, ANY], out_specs=[HBM] * n_buf,
                         out_shape=[pltpu.HBM(b.shape, b.dtype) for b in bufs], input_output_aliases={i: i for i in range(n_buf)},
                         compiler_params=pltpu.CompilerParams(**SPLIT_PARAMS))(*bufs, send_sems, recv_sems, after)
    plan.results = [res[w] for w in where]
    return plan.results


def pair_add(name, g, got, place):
    slots, rows, cols = g.shape
    half = rows // 2
    tr = _row_tile(half)
    nb = half // tr

    def kern(_, g_ref, t_ref, o_ref):
        o_ref[...] = (g_ref[...].astype(F32) + t_ref[...].astype(F32)).astype(o_ref.dtype)

    blk = pl.BlockSpec((None, tr, cols), lambda s, i, p: (s, i, 0))
    grid_spec = pltpu.PrefetchScalarGridSpec(
        num_scalar_prefetch=1, grid=(slots, nb),
        in_specs=[pl.BlockSpec((None, tr, cols), lambda s, i, p: (s, p[1] * nb + i, 0)), blk], out_specs=blk)
    return pl.pallas_call(kern, grid_spec=grid_spec, out_shape=SDS((slots, half, cols), g.dtype), name=name,
                          compiler_params=_cparams())(place, g, got)


def chip_add(name, p, q, place):
    slots, half, cols = p.shape
    tr = _row_tile(half)
    nb = half // tr

    def kern(_, p_ref, q1, q2, q3, o_ref):
        o_ref[...] = p_ref[...].astype(F32) + q1[...].astype(F32) + q2[...].astype(F32) + q3[...].astype(F32)

    def slot(k):
        return pl.BlockSpec((None, tr, cols), lambda i, pr: ((pr[0] + k) % slots, i, 0))

    grid_spec = pltpu.PrefetchScalarGridSpec(
        num_scalar_prefetch=1, grid=(nb,), in_specs=[slot(0), slot(1), slot(2), slot(3)],
        out_specs=pl.BlockSpec((tr, cols), lambda i, pr: (pr[1] * nb + i, 0)))
    return pl.pallas_call(kern, grid_spec=grid_spec, out_shape=SDS((2 * half, cols), F32), name=name,
                          compiler_params=_cparams())(place, p, q, q, q)


def pair_adds(tag, gs, gots, place):
    return [pair_add(f"{tag}_pair_add_{i}", g, got, place) for i, (g, got) in enumerate(zip(gs, gots))]


def chip_adds(tag, pairs, qs, place):
    return [chip_add(f"{tag}_chip_add_{i}", p, q, place) for i, (p, q) in enumerate(zip(pairs, qs))]


def all_reduce_in_vmem(name, g):
    slots, rows, cols = g.shape
    half = rows // 2

    def kern(g_ref, out_ref, got, pair, q, send_sems, recv_sems):
        x, y, c, chips = _place()
        mine = 2 * x + y
        other_core = (x, y, 1 - c)
        my_rows, its_rows = pl.ds(c * half, half), pl.ds((1 - c) * half, half)

        def exchange(copies):
            for cp in copies:
                cp.start()
            for cp in copies:
                cp.wait()

        def copy(k, src, dst, to):
            return pltpu.make_async_remote_copy(src_ref=src, dst_ref=dst, send_sem=send_sems.at[k], recv_sem=recv_sems.at[k], device_id=to,
                                                device_id_type=MESH_ID)

        exchange([copy(0, g_ref.at[:, its_rows, :], got, other_core)])
        pair[...] = (g_ref[:, my_rows, :].astype(F32) + got[...].astype(F32)).astype(pair.dtype)
        exchange([copy(1 + j, pair.at[2 * cx + cy], q.at[mine], (cx, cy, c)) for j, (cx, cy) in enumerate(chips)])
        total = pair[mine].astype(F32)
        for k in range(1, slots):
            total = total + q[(mine + k) % slots].astype(F32)
        out_ref[mine, my_rows, :] = total
        reduced = out_ref.at[mine, my_rows, :]
        first = [copy(4 + j, reduced, reduced, (cx, cy, c)) for j, (cx, cy) in enumerate(chips)] + [copy(7, reduced, reduced, other_core)]
        for cp in first:
            cp.start()
        passed = []
        for j, (cx, cy) in enumerate(chips):
            first[j].wait_recv()
            landed = out_ref.at[2 * cx + cy, my_rows, :]
            passed.append(copy(8 + j, landed, landed, other_core))
            passed[-1].start()
        for cp in [first[3]] + passed:
            cp.wait_recv()
        for cp in first + passed:
            cp.wait_send()

    vmem = pl.BlockSpec(memory_space=pltpu.VMEM)
    return pl.pallas_call(kern, in_specs=[vmem], out_specs=vmem, out_shape=SDS(g.shape, F32), name=name, compiler_params=_cparams(),
                          scratch_shapes=[pltpu.VMEM((slots, half, cols), g.dtype)] * 3 + [pltpu.SemaphoreType.DMA((11,))] * 2)(g)


BIG = [("w_out", (2, 512, 1024)), ("w_mem_kv", (2, 256, 1024)), ("s5_w_in", (1, 1024, 1024)), ("s5_w_glu", (1, 1536, 768)),
       ("mla_w_in", (1, 1024, 848)), ("mla_w_uq", (1, 512, 576)), ("mla_w_ukv", (1, 256, 768))]
SHARDED_SMALL = [("mla_q_lora_norm", (1, 128)), ("mla_kv_lora_norm", (1, 64))]
SMALL = [("ln_gain", (2, 1024)), ("mem_norm", (2, 1024)), ("xq_norm", (2, 128)), ("xk_norm", (2, 128)),
         ("s5_lambda_re", (1, 96, 64)), ("s5_lambda_im", (1, 96, 64)), ("s5_log_step", (1, 96)),
         ("s5_b_re", (1, 96, 64, 16)), ("s5_b_im", (1, 96, 64, 16)), ("s5_c_re", (1, 96, 16, 64)), ("s5_c_im", (1, 96, 16, 64)),
         ("s5_d", (1, 1536)), ("mla_q_nope_norm", (1, 128)), ("mla_k_nope_norm", (1, 128)), ("mla_q_rope_norm", (1, 64)),
         ("mla_k_rope_norm", (1, 64))]
WEIGHT_ORDER = ["ln_gain", "w_out", "mem_norm", "w_mem_kv", "xq_norm", "xk_norm", "s5_w_in", "s5_lambda_re", "s5_lambda_im",
                "s5_log_step", "s5_b_re", "s5_b_im", "s5_c_re", "s5_c_im", "s5_d", "s5_w_glu", "mla_w_in", "mla_q_lora_norm",
                "mla_kv_lora_norm", "mla_w_uq", "mla_w_ukv", "mla_q_nope_norm", "mla_k_nope_norm", "mla_q_rope_norm", "mla_k_rope_norm"]
MINOR_LAST = {"mla_w_in": (0, 2, 1), "mla_w_uq": (0, 2, 1), "s5_b_re": (0, 2, 3, 1), "s5_b_im": (0, 2, 3, 1),
              "s5_c_re": (0, 2, 3, 1), "s5_c_im": (0, 2, 3, 1)}
SMALL_FULL = SMALL + [(n, (1, 4 * s[1])) for n, s in SHARDED_SMALL]
N_SMALL = sum(math.prod(s) for _, s in SMALL_FULL)
SMALL_ROWS, SMALL_LANES = 128, 896

PAIR_OUT, PAIR_MKV, PAIR_ROWS = 0, 512, 768


def stack_shards(w, dtype):
    pairs = [jnp.concatenate([w["w_out"][l], w["w_mem_kv"][l]], axis=0).astype(dtype) for l in range(2)]
    return ([w["s5_w_in"][0].astype(dtype)], [pairs[0], w["s5_w_glu"][0].astype(dtype)],
            [pairs[1], w["mla_w_ukv"][0].astype(dtype), jnp.pad(w["mla_w_in"][0].T.astype(dtype), ((0, IN1T_ROWS - MLA_IN // 4), (0, 0))),
             w["mla_w_uq"][0].T.astype(dtype)])


def pair_views(pair):
    return {"w_out": Sharded(pair, "row", PAIR_OUT, 512), "w_mem_kv": Sharded(pair, "row", PAIR_MKV, 256)}


def grad_views():
    pair = SDS((4, PAIR_ROWS, 1024), BF16)
    return {"w_out": Sharded(pair, "row", PAIR_OUT, 512), "w_mem_kv": Sharded(pair, "row", PAIR_MKV, 256),
            "s5_w_in": Sharded(SDS((4, 1024, 1024), BF16), "col", 0, 1024), "s5_w_glu": Sharded(SDS((4, 1536, 768), BF16), "col", 0, 1536),
            "mla_w_ukv": Sharded(SDS((4, 256, 768), BF16), "col", 0, 256)}


IN1T_ROWS = 864


def in1_rows_permute(wt):
    o1, o2, o3, o4 = QL, QL + KVL, QL + KVL + ROPE, QL + KVL + ROPE + XQW
    return jnp.concatenate([wt[o4:], wt[:o1], wt[o3:o4], wt[o1:o2], wt[o2:o3], jnp.zeros((MLA_IN_P - MLA_IN, wt.shape[1]), wt.dtype)], axis=0)


def in1_rows_unpermute(d):
    return jnp.concatenate([d[2048:2560], d[3072:3328], d[3328:3392], d[2560:3072], d[:2048]], axis=0)


def uq_rows_permute(wt):
    w3 = wt.reshape(MH, NOPE + ROPE, wt.shape[1])
    return jnp.concatenate([w3[:, :NOPE].reshape(MH * NOPE, wt.shape[1]), w3[:, NOPE:].reshape(MH * ROPE, wt.shape[1])], axis=0)


def uq_rows_unpermute(d):
    dn = d[:MH * NOPE].reshape(MH, NOPE, d.shape[1])
    dr = d[MH * NOPE:].reshape(MH, ROPE, d.shape[1])
    return jnp.concatenate([dn, dr], axis=1).reshape(MH * (NOPE + ROPE), d.shape[1])


def time_permute(a):
    return a.reshape(SEG, SEG_LEN, a.shape[-1]).transpose(1, 0, 2).reshape(L, a.shape[-1])


def time_unpermute(a):
    return a.reshape(SEG_LEN, SEG, a.shape[-1]).transpose(1, 0, 2).reshape(L, a.shape[-1])


def mem_branch_fwd(tag, mem, mem_norm, w_mem_kv, xk_norm):
    mn = row_fwd(tag + "_mem_rms", fn_rms, ML, ML, [(mem, D, 0)], [mem_norm], [(D, BF16)])[0]
    kv = matmul(tag + "_mem_kv", mn, w_mem_kv, "nn", F32)
    kn = row_fwd(tag + "_mem_knorm", fn_mem_k, ML, ML, [(kv, XQW, 0)], [xk_norm], [(XQW, F32)])[0]
    return mn, kv, kn


def mem_branch_bwd(tag, mem, mem_norm, w_mem_kv, xk_norm, mn, kv, dkn, dv, g_view, g_wide):
    dk, dxk = row_bwd(tag + "_mem_knorm_bwd", fn_mem_k, ML, ML, [(kv, XQW, 0)], [xk_norm], [(dkn, XQW, 0)], [True], [True])
    dkv = jnp.concatenate([dk, dv], axis=1)
    dmn = matmul(tag + "_mem_kv_dx", dkv, w_mem_kv, "nt", F32)
    g_wide = matmul(tag + "_mem_kv_dw", mn, dkv, "tn", out=g_view, into=g_wide)
    dmem_norm = row_bwd(tag + "_mem_rms_bwd", fn_rms, ML, ML, [(mem, D, 0)], [mem_norm], [(dmn, D, 0)], [False], [True])[0]
    return g_wide, dmem_norm, dxk


def mem_attn_fwd(tag, proj, cb, kn, kv, xq_norm):
    return row_fwd(tag + "_mem_attn", fn_mem_attn, L, ROW_TILE, [(proj, XQW, cb)], [kn, kv[:, XQW:], xq_norm], [(XQW, F32)])[0]


def mem_attn_bwd(tag, proj, cb, kn, kv, xq_norm, dmo, dproj):
    place = {"cols": proj.shape[1], "cb": cb, "into": dproj, "dtype": dproj.dtype}
    return row_bwd(tag + "_mem_attn_bwd", fn_mem_attn, L, ROW_TILE, [(proj, XQW, cb)], [kn, kv[:, XQW:], xq_norm], [(dmo, XQW, 0)],
                   [place], [True, True, True])


def device_step(x, mem, positions, target, small, env, hooks=None):
    hooks = hooks or {}

    def plans_for(name):
        return hooks[("plans", name)](env) if ("plans", name) in hooks else ()

    def around(when, name, last=None):
        if (when, name) in hooks:
            hooks[(when, name)](env, last)

    g = {}
    gw = grad_views()
    ln, mem_norm, xq_norm, xk_norm = small["ln_gain"], small["mem_norm"], small["xq_norm"], small["xk_norm"]

    lre, lim = small["s5_lambda_re"][0], small["s5_lambda_im"][0]
    ls = small["s5_log_step"].reshape(SG, 1)
    one = pl.BlockSpec((SG, SP), lambda i: (0, 0))
    col = pl.BlockSpec((SG, 1), lambda i: (0, 0))
    disc_ins = [(lre, one), (lim, one), (ls, col)]
    a_re, a_im, coef_re, coef_im = stage("s5_disc", fn_s5_disc, (1,), disc_ins, [(SDS((SG, SP), F32), one)] * 4)
    b_re, b_im = small["s5_b_re"].reshape(SN, SC), small["s5_b_im"].reshape(SN, SC)
    c_re, c_im = small["s5_c_re"].reshape(PW, SP), small["s5_c_im"].reshape(PW, SP)
    bmat_rows = [(b_re, SC, 0), (b_im, SC, 0), (coef_re.reshape(SN, 1), 1, 0), (coef_im.reshape(SN, 1), 1, 0)]
    wb_re, wb_im = row_fwd("s5_bmat", fn_s5_bmat, SN, BMAT_TILE, bmat_rows, [], [(128, F32)] * 2)
    cmat_rows = [(c_re, SP, 0), (c_im, SP, 0)]
    wc_re, wc_im = row_fwd("s5_cmat", fn_s5_cmat, PW, CMAT_TILE, cmat_rows, [], [(512, F32)] * 2)
    a_re_v, a_im_v = a_re.reshape(1, SN), a_im.reshape(1, SN)
    s5_d = small["s5_d"]

    xp = time_permute(x)
    h0 = row_fwd("l0_rms", fn_rms, L, ROW_TILE, [(xp, D, 0)], [ln[0:1]], [(D, BF16)])[0]
    around("before", "l0_in", wb_re)
    w_in0 = Sharded(env["in0"], "col", 0, 1024)
    proj0 = matmul("l0_in", h0, w_in0, "nn")
    s_re, s_im, g0 = s5_forward(proj0, wb_re, wb_im, wc_re, wc_im, a_re_v, a_im_v, s5_d, plans=plans_for("s5_forward"))
    around("before", "l0_glu", g0)
    w0 = dict(pair_views(env["pair0"]), s5_w_glu=Sharded(env["glu"], "col", 0, 1536))
    z0 = matmul("l0_glu", g0, w0["s5_w_glu"], "nn", plans=plans_for("l0_glu"))
    mn0, kv0, kn0 = mem_branch_fwd("l0", mem, mem_norm[0:1], w0["w_mem_kv"], xk_norm[0:1])
    mo0 = mem_attn_fwd("l0", proj0, 3, kn0, kv0, xq_norm[0:1])
    o0 = row_fwd("l0_merge", fn_merge_glu, L, ROW_TILE, [(z0, 2 * PW, 0), (mo0, XQW, 0), (proj0, BW, 1)], [], [(BW, BF16)])[0]
    around("before", "l0_out", o0)
    x1p = matmul("l0_out", o0, w0["w_out"], "nn", F32, add=xp, plans=plans_for("l0_out"))
    around("after", "l0_out", x1p)
    x1 = time_unpermute(x1p)

    w1 = dict(pair_views(env["pair1"]), mla_w_ukv=Sharded(env["ukv"], "col", 0, 256))
    w_in1, w_uq = env["w_in1"], env["w_uq"]
    h1 = row_fwd("l1_rms", fn_rms, L, ROW_TILE, [(x1, D, 0)], [ln[1:2]], [(D, BF16)])[0]
    proj1 = matmul("l1_in", h1, w_in1, "nt")
    qln, kvln = env["q_lora_norm"].reshape(1, QL), env["kv_lora_norm"].reshape(1, KVL)
    cqn = row_fwd("l1_q_lora_rms", fn_rms, L, ROW_TILE, [(proj1, QL, 4)], [qln], [(QL, BF16)])[0]
    ckvn = row_fwd("l1_kv_lora_rms", fn_rms, L, ROW_TILE, [(proj1, KVL, 12)], [kvln], [(KVL, BF16)])[0]
    q = matmul("l1_uq", cqn, w_uq, "nt")
    kv = matmul("l1_ukv", ckvn, w1["mla_w_ukv"], "nn")
    inv_freq = ROPE_THETA ** (-jnp.arange(ROPE // 2, dtype=F32) / (ROPE // 2))
    ang = positions.astype(F32)[:, None] * inv_freq
    cos2 = jnp.tile(jnp.cos(ang), (1, 4))
    sin_signed = jnp.tile(jnp.concatenate([-jnp.sin(ang), jnp.sin(ang)], axis=1), (1, 2))
    qnn, knn = small["mla_q_nope_norm"], small["mla_k_nope_norm"]
    qrn, krn = jnp.tile(small["mla_q_rope_norm"], (1, 2)), jnp.tile(small["mla_k_rope_norm"], (1, 2))
    tp = 256
    prep_ins = [(q, rspec(tp, MH * (NOPE + ROPE))), (kv, rspec(tp, MH * 256)), (proj1, rspec(tp, 128, 26)),
                (cos2, rspec(tp, 128)), (sin_signed, rspec(tp, 128))] + [(a, cspec((1, 128))) for a in (qnn, knn, qrn, krn)]
    hq_spec = pl.BlockSpec((MH, tp, 256), lambda i: (0, i, 0))
    hv_spec = pl.BlockSpec((MH, tp, 128), lambda i: (0, i, 0))
    qf, kf, vh = stage("l1_mla_prep", fn_mla_prep, (L // tp,), prep_ins,
                       [(SDS((MH, L, 256), BF16), hq_spec), (SDS((MH, L, 256), BF16), hq_spec), (SDS((MH, L, 128), BF16), hv_spec)])
    attn, attn_lse = causal_attn(qf, kf, vh)
    mn1, kv1, kn1 = mem_branch_fwd("l1", mem, mem_norm[1:2], w1["w_mem_kv"], xk_norm[1:2])
    mo1 = mem_attn_fwd("l1", proj1, 5, kn1, kv1, xq_norm[1:2])
    o1 = row_fwd("l1_merge", fn_merge, L, ROW_TILE, [(attn, PW, 0), (mo1, XQW, 0), (proj1, BW, 0)], [], [(BW, BF16)])[0]
    x2 = matmul("l1_out", o1, w1["w_out"], "nn", F32, add=x1)
    dx2, loss = loss_and_grad(x2, target)
    around("after", "loss", loss)

    do1 = matmul("l1_out_dx", dx2, w1["w_out"], "nt")
    g_pair1 = matmul("l1_out_dw", o1, dx2, "tn", out=gw["w_out"])
    dattn, dmo1, dproj1 = row_bwd("l1_merge_bwd", fn_merge, L, ROW_TILE, [(attn, PW, 0), (mo1, XQW, 0), (proj1, BW, 0)], [],
                                  [(do1, BW, 0)], [True, True, {"cols": MLA_IN_P, "cb": 0, "dtype": BF16}], [])
    dproj1, dkn1, dv1, dxqn1 = mem_attn_bwd("l1", proj1, 5, kn1, kv1, xq_norm[1:2], dmo1, dproj1)
    env["g_pair1"], dmem_norm1, dxk1 = mem_branch_bwd("l1", mem, mem_norm[1:2], w1["w_mem_kv"], xk_norm[1:2], mn1, kv1, dkn1, dv1,
                                                      gw["w_mem_kv"], g_pair1)
    dqf, dkf, dvh = causal_attn_bwd(qf, kf, vh, attn, attn_lse, dattn)
    prep_diffs = [("row", SDS((L, MH * (NOPE + ROPE)), BF16), rspec(tp, MH * (NOPE + ROPE))), ("row", SDS((L, MH * 256), BF16), rspec(tp, MH * 256)),
                  ("row", SDS((L, MLA_IN_P), BF16), rspec(tp, 128, 26), {"into": dproj1}), None, None] + [("acc", (0,))] * 4
    dq, dkv, dproj1, dqnn, dknn, dqrn, dkrn = stage_bwd("l1_mla_prep_bwd", fn_mla_prep, (L // tp,), prep_ins,
                                                        [(dqf, hq_spec), (dkf, hq_spec), (dvh, hv_spec)], prep_diffs)
    dcqn = matmul("l1_uq_dx", dq, w_uq, "nn")
    env["dw_uq"] = matmul("l1_uq_dw", dq, cqn, "tn")
    dckvn = matmul("l1_ukv_dx", dkv, w1["mla_w_ukv"], "nt")
    env["g_ukv"] = matmul("l1_ukv_dw", ckvn, dkv, "tn", out=gw["mla_w_ukv"])
    dproj1, dqln = row_bwd("l1_q_lora_rms_bwd", fn_rms, L, ROW_TILE, [(proj1, QL, 4)], [qln], [(dcqn, QL, 0)],
                           [{"cols": MLA_IN_P, "cb": 4, "into": dproj1, "dtype": BF16}], [True])
    dproj1, dkvln = row_bwd("l1_kv_lora_rms_bwd", fn_rms, L, ROW_TILE, [(proj1, KVL, 12)], [kvln], [(dckvn, KVL, 0)],
                            [{"cols": MLA_IN_P, "cb": 12, "into": dproj1, "dtype": BF16}], [True])
    env["dw_in1"] = matmul("l1_in_dw", dproj1, h1, "tn")
    dh1 = matmul("l1_in_dx", dproj1, w_in1, "nn", plans=plans_for("l1_in_dx"))
    around("after", "l1_in_dx", dh1)
    dx1, dln1 = row_bwd("l1_rms_bwd", fn_rms, L, ROW_TILE, [(x1, D, 0)], [ln[1:2]], [(dh1, D, 0)], [{"add": (dx2, D, 0)}], [True])
    dx1p = time_permute(dx1)

    do0 = matmul("l0_out_dx", dx1p, w0["w_out"], "nt", plans=plans_for("l0_out_dx"))
    g_pair0 = matmul("l0_out_dw", o0, dx1p, "tn", out=gw["w_out"])
    dz0, dmo0, dproj0 = row_bwd("l0_merge_bwd", fn_merge_glu, L, ROW_TILE, [(z0, 2 * PW, 0), (mo0, XQW, 0), (proj0, BW, 1)], [],
                                [(do0, BW, 0)], [{"dtype": BF16}, True, {"cols": 2 * BW, "cb": 1, "dtype": BF16}], [])
    dproj0, dkn0, dv0, dxqn0 = mem_attn_bwd("l0", proj0, 3, kn0, kv0, xq_norm[0:1], dmo0, dproj0)
    env["g_pair0"], dmem_norm0, dxk0 = mem_branch_bwd("l0", mem, mem_norm[0:1], w0["w_mem_kv"], xk_norm[0:1], mn0, kv0, dkn0, dv0,
                                                      gw["w_mem_kv"], g_pair0)
    env["g_glu"] = matmul("l0_glu_dw", g0, dz0, "tn", out=gw["s5_w_glu"], plans=plans_for("l0_glu_dw"))
    dg0 = matmul("l0_glu_dx", dz0, w0["s5_w_glu"], "nt", plans=plans_for("l0_glu_dx"))
    around("before", "s5_backward", dg0)
    dproj0, dd, dwc_re, dwc_im, dwb_re, dwb_im, da_re, da_im = s5_backward(dg0, proj0, s_re, s_im, wb_re, wb_im, wc_re, wc_im,
                                                                           a_re_v, a_im_v, s5_d, dproj0, plans=plans_for("s5_backward"))
    around("after", "s5_backward", dd)
    env["g_in0"] = matmul("l0_in_dw", h0, dproj0, "tn", out=gw["s5_w_in"], plans=plans_for("l0_in_dw"))
    dh0 = matmul("l0_in_dx", dproj0, w_in0, "nt", plans=plans_for("l0_in_dx"))
    around("after", "l0_in_dx", dh0)
    dxp, dln0 = row_bwd("l0_rms_bwd", fn_rms, L, ROW_TILE, [(xp, D, 0)], [ln[0:1]], [(dh0, D, 0)], [{"add": (dx1p, D, 0)}], [True])
    grad_x = time_unpermute(dxp)

    db_re, db_im, dcoef_re, dcoef_im = row_bwd("s5_bmat_bwd", fn_s5_bmat, SN, BMAT_TILE, bmat_rows, [], [(dwb_re, 128, 0), (dwb_im, 128, 0)],
                                               [True] * 4, [], plans=plans_for("s5_bmat_bwd"))
    dc_re, dc_im = row_bwd("s5_cmat_bwd", fn_s5_cmat, PW, CMAT_TILE, cmat_rows, [], [(dwc_re, 512, 0), (dwc_im, 512, 0)], [True] * 2, [],
                           plans=plans_for("s5_cmat_bwd"))
    disc_cts = [(da_re.reshape(SG, SP), one), (da_im.reshape(SG, SP), one), (dcoef_re.reshape(SG, SP), one), (dcoef_im.reshape(SG, SP), one)]
    dlre, dlim, dls = stage_bwd("s5_disc_bwd", fn_s5_disc, (1,), disc_ins, disc_cts, [("acc", (0,))] * 3)

    g["ln_gain"] = jnp.concatenate([dln0, dln1], axis=0)
    g["mem_norm"] = jnp.concatenate([dmem_norm0, dmem_norm1], axis=0)
    g["xq_norm"] = jnp.concatenate([dxqn0, dxqn1], axis=0)
    g["xk_norm"] = jnp.concatenate([dxk0, dxk1], axis=0)
    g["s5_lambda_re"], g["s5_lambda_im"], g["s5_log_step"] = dlre, dlim, dls
    g["s5_b_re"], g["s5_b_im"], g["s5_c_re"], g["s5_c_im"] = db_re, db_im, dc_re, dc_im
    g["s5_d"] = dd
    g["mla_q_lora_norm"], g["mla_kv_lora_norm"] = dqln, dkvln
    g["mla_q_nope_norm"], g["mla_k_nope_norm"] = dqnn, dknn
    g["mla_q_rope_norm"] = dqrn[:, :ROPE] + dqrn[:, ROPE:]
    g["mla_k_rope_norm"] = dkrn[:, :ROPE] + dkrn[:, ROPE:]
    return loss, grad_x, g


def kernel(x, mem, positions, ln_gain, w_out, mem_norm, w_mem_kv, xq_norm, xk_norm, s5_w_in, s5_lambda_re, s5_lambda_im, s5_log_step, s5_b_re, s5_b_im, s5_c_re, s5_c_im, s5_d, s5_w_glu, mla_w_in, mla_q_lora_norm, mla_kv_lora_norm, mla_w_uq, mla_w_ukv, mla_q_nope_norm, mla_k_nope_norm, mla_q_rope_norm, mla_k_rope_norm, loss_target, m_ln_gain, m_w_out, m_mem_norm, m_w_mem_kv, m_xq_norm, m_xk_norm, m_s5_w_in, m_s5_lambda_re, m_s5_lambda_im, m_s5_log_step, m_s5_b_re, m_s5_b_im, m_s5_c_re, m_s5_c_im, m_s5_d, m_s5_w_glu, m_mla_w_in, m_mla_q_lora_norm, m_mla_kv_lora_norm, m_mla_w_uq, m_mla_w_ukv, m_mla_q_nope_norm, m_mla_k_nope_norm, m_mla_q_rope_norm, m_mla_k_rope_norm, v_ln_gain, v_w_out, v_mem_norm, v_w_mem_kv, v_xq_norm, v_xk_norm, v_s5_w_in, v_s5_lambda_re, v_s5_lambda_im, v_s5_log_step, v_s5_b_re, v_s5_b_im, v_s5_c_re, v_s5_c_im, v_s5_d, v_s5_w_glu, v_mla_w_in, v_mla_q_lora_norm, v_mla_kv_lora_norm, v_mla_w_uq, v_mla_w_ukv, v_mla_q_nope_norm, v_mla_k_nope_norm, v_mla_q_rope_norm, v_mla_k_rope_norm):
    args = dict(locals())
    wts = {n: args[n] for n in WEIGHT_ORDER}
    mom = {n: args["m_" + n] for n in WEIGHT_ORDER}
    var = {n: args["v_" + n] for n in WEIGHT_ORDER}

    chip = 2 * lax.axis_index("x") + lax.axis_index("y")
    place = jnp.stack([chip, lax.axis_index("c")]).astype(jnp.int32)

    def own_slot(gathered, shards):
        return [lax.dynamic_update_slice(g, s[None], (chip, 0, 0)) for g, s in zip(gathered, shards)]

    groups = list(stack_shards(wts, BF16))
    groups[2].append(jnp.concatenate([mla_q_lora_norm, jnp.pad(mla_kv_lora_norm, ((0, 0), (0, 64))), jnp.zeros((14, 128), F32)], axis=0))
    over_ici = [plan_gather_ici(shards) for shards in groups]
    _SCHEDULE_BEHIND.clear()
    schedule_behind(split_start("gather_start", over_ici))
    env, hooks, passed_on = {}, {}, {}

    def arrived(k, after, pass_now):
        passing = plan_gather_pass(split_wait(f"gather_wait_{k}", over_ici[k], after))
        passed_on[k] = passing
        return own_slot(run_plan(f"gather_pass_{k}", passing), groups[k]) if pass_now else None

    def need_in0(env, last):
        env["in0"], = arrived(0, last, True)

    def need_layer0(env, last):
        env["pair0"], env["glu"] = arrived(1, last, True)

    def need_layer1(env, last):
        arrived(2, last, False)

    def layer1_weights(env, last):
        pair1, ukv, in1, uq, norms = own_slot(passed_on[2].results, groups[2])
        env.update(pair1=pair1, ukv=ukv, w_in1=in1_rows_permute(in1[:, :MLA_IN // 4].reshape(MLA_IN, D)), w_uq=uq_rows_permute(uq.reshape(-1, QL)),
                   q_lora_norm=norms[:, 0, :], kv_lora_norm=norms[:, 1, :64])

    hooks["before", "l0_in"], hooks["before", "l0_glu"], hooks["before", "l0_out"] = need_in0, need_layer0, need_layer1
    hooks["plans", "l0_out"], hooks["after", "l0_out"] = (lambda env: [passed_on[2]]), layer1_weights

    rs = {}

    def swap(k, gs):
        rs[k, "g"], rs[k, "swap"] = gs, plan_pair_exchange(gs)
        return rs[k, "swap"]

    def start_scatter(k):
        rs[k, "pairs"] = pair_adds(f"rs{k}", rs[k, "g"], rs[k, "swap"].results, place)
        rs[k, "scatter"] = plan_chip_scatter(rs[k, "pairs"])
        schedule_behind(split_start(f"rs{k}_scatter_start", [rs[k, "scatter"]]))

    def join(k, after):
        rs[k, "join"] = plan_pair_join(chip_adds(f"rs{k}", rs[k, "pairs"], split_wait(f"rs{k}_scatter_wait", rs[k, "scatter"], after), place))
        return rs[k, "join"]

    def layer1_gradients(env):
        g_in1 = jnp.pad(in1_rows_unpermute(env["dw_in1"]).reshape(4, MLA_IN // 4, D), ((0, 0), (0, IN1T_ROWS - MLA_IN // 4), (0, 0)))
        return [swap(1, [env["g_pair1"], env["g_ukv"], g_in1, uq_rows_unpermute(env["dw_uq"]).reshape(4, -1, QL)])]

    hooks["plans", "l1_in_dx"] = layer1_gradients
    hooks["after", "l1_in_dx"] = lambda env, last: start_scatter(1)
    hooks["plans", "l0_glu_dx"] = lambda env: [swap(0, [env["g_pair0"], env["g_glu"]])]

    def before_s5_backward(env, last):
        start_scatter(0)
        env["join1"] = join(1, last)

    hooks["before", "s5_backward"] = before_s5_backward
    hooks["plans", "s5_backward"] = lambda env: [env["join1"]]
    hooks["after", "s5_backward"] = lambda env, last: env.update(join0=join(0, last))
    hooks["plans", "l0_in_dx"] = lambda env: [swap(2, [env["g_in0"]]), env["join0"]]
    hooks["after", "l0_in_dx"] = lambda env, last: start_scatter(2)

    def total_loss(env, local):
        env["loss"] = lax.psum(local[0, 0], MESH_AXES)
        schedule_behind(env["loss"].reshape(1, 1))

    hooks["after", "loss"] = total_loss
    small = {n: wts[n] for n, _ in SMALL}
    loss, grad_x, g = device_step(x[0], mem[0], positions[0], loss_target[0], small, env, hooks)
    loss = env["loss"]
    r_in0, = run_plan("rs2_pair_join", join(2, g["s5_log_step"]))
    (r_pair1, r_ukv, r_in1, r_uq), (r_pair0, r_glu) = (rs[k, "join"].results for k in (1, 0))

    small_flat = jnp.concatenate([g[n].reshape(-1) for n, _ in SMALL_FULL])
    g_small = jnp.pad(small_flat, (0, 4 * SMALL_ROWS * SMALL_LANES - N_SMALL)).astype(BF16).reshape(4, SMALL_ROWS, SMALL_LANES)
    small_all = all_reduce_in_vmem("small_grads_all_reduce", g_small).reshape(-1)[:N_SMALL]

    grads = {"w_out": jnp.stack([r_pair0[:PAIR_MKV], r_pair1[:PAIR_MKV]]), "w_mem_kv": jnp.stack([r_pair0[PAIR_MKV:], r_pair1[PAIR_MKV:]]),
             "s5_w_in": r_in0[None], "s5_w_glu": r_glu[None], "mla_w_ukv": r_ukv[None], "mla_w_in": r_in1[:MLA_IN // 4].T[None], "mla_w_uq": r_uq.T[None]}
    off = 0
    for n, s in SMALL_FULL:
        grads[n] = small_all[off:off + math.prod(s)].reshape(s)
        off += math.prod(s)
    for n, s in SHARDED_SMALL:
        grads[n] = lax.dynamic_slice(grads[n], (0, chip * s[1]), s)

    delta, new_m, new_v = {}, {}, {}
    for n, s in BIG + [(n, s) for n, s in SMALL if len(s) == 4]:
        perm = MINOR_LAST.get(n, tuple(range(len(s))))
        turned = tuple(s[p] for p in perm)
        view = lambda a: jnp.transpose(a, perm).reshape(-1, turned[-1])
        res = adamw("adamw_" + n, view(wts[n]), view(grads[n]), view(mom[n]), view(var[n]))
        delta[n], new_m[n], new_v[n] = (jnp.transpose(r.reshape(turned), tuple(perm.index(i) for i in range(len(s)))) for r in res)
    small_names = [n for n, s in SMALL if len(s) < 4] + [n for n, _ in SHARDED_SMALL]
    n_own = sum(wts[n].size for n in small_names)
    rows_own = -(-n_own // (8 * 128)) * 8

    def pack_small(d):
        flat = jnp.concatenate([d[n].reshape(-1) for n in small_names])
        return jnp.pad(flat, (0, rows_own * 128 - n_own), constant_values=1.0).reshape(rows_own, 128)

    res = adamw("adamw_small", pack_small(wts), pack_small(grads), pack_small(mom), pack_small(var))
    off = 0
    for n in small_names:
        size = wts[n].size
        delta[n], new_m[n], new_v[n] = (r.reshape(-1)[off:off + size].reshape(wts[n].shape) for r in res)
        off += size

    return (loss, grad_x[None], *[grads[n] for n in WEIGHT_ORDER], *[delta[n] for n in WEIGHT_ORDER],
            *[new_m[n] for n in WEIGHT_ORDER], *[new_v[n] for n in WEIGHT_ORDER])
```
